```python
import math
import jax, jax.numpy as jnp
from jax import lax
import numpy as np

D_MODEL = 1024
BATCH = 16
SEQ = 2048
DEPTH = 1

HEAD_DIM = 64
BLOCK = 128
A_Q_HEADS = 16
A_KV_HEADS = 2
A_GROUP = A_Q_HEADS // A_KV_HEADS
A_WINDOW = 128
B_PATTERNS = ((128, 1), (512, 4), (2048, 16))
B_HEADS_PER_GROUP = 8
B_HEADS = B_HEADS_PER_GROUP * len(B_PATTERNS)
A_Q_W = A_Q_HEADS * HEAD_DIM
A_KV_W = A_KV_HEADS * HEAD_DIM
B_W = B_HEADS * HEAD_DIM
B_OUT_W = B_HEADS_PER_GROUP * HEAD_DIM
IN_W = A_Q_W + 2 * A_KV_W + 3 * B_W + 2 * D_MODEL
D_FF = -(-8 * D_MODEL // (3 * 256)) * 256
ROPE_THETA = 10000.0
LN_EPS = 1e-5
DEEPNORM_ALPHA = (2 * DEPTH) ** 0.25
DEEPNORM_BETA = (8 * DEPTH) ** -0.25
NEG_INF = -1e30

kernel_name = "hybrid_swa_sink_dilated_gated_deepnorm_adaln"


def layer_norm(x, g, b):
    xf = x.astype(jnp.float32)
    mu = jnp.mean(xf, axis=-1, keepdims=True)
    xc = xf - mu
    var = jnp.mean(xc * xc, axis=-1, keepdims=True)
    y = xc * lax.rsqrt(var + LN_EPS)
    return (y * g.astype(jnp.float32) + b.astype(jnp.float32)).astype(x.dtype)


def rope(x, positions):
    half = HEAD_DIM // 2
    inv = ROPE_THETA ** (-jnp.arange(half, dtype=jnp.float32) / half)
    ang = positions.astype(jnp.float32)[..., None] * inv
    cos = jnp.cos(ang)[:, :, None, :]
    sin = jnp.sin(ang)[:, :, None, :]
    xf = x.astype(jnp.float32)
    x1, x2 = xf[..., :half], xf[..., half:]
    out = jnp.concatenate([x1 * cos - x2 * sin, x2 * cos + x1 * sin], axis=-1)
    return out.astype(x.dtype)


def banded_window_attention(q, k, v, n_back, sink=None):
    N, T, Hkv, G, dh = q.shape
    nblk = -(-T // BLOCK)
    pad = nblk * BLOCK - T
    qp = jnp.pad(q, ((0, 0), (0, pad), (0, 0), (0, 0), (0, 0)))
    kp = jnp.pad(k, ((0, 0), (BLOCK, pad), (0, 0), (0, 0)))
    vp = jnp.pad(v, ((0, 0), (BLOCK, pad), (0, 0), (0, 0)))
    qb = qp.reshape(N, nblk, BLOCK, Hkv, G, dh)
    kb = kp.reshape(N, nblk + 1, BLOCK, Hkv, dh)
    vb = vp.reshape(N, nblk + 1, BLOCK, Hkv, dh)
    kw = jnp.concatenate([kb[:, :-1], kb[:, 1:]], axis=2)
    vw = jnp.concatenate([vb[:, :-1], vb[:, 1:]], axis=2)
    s = jnp.einsum('nbqhgd,nbkhd->nbhgqk', qb, kw,
                   preferred_element_type=jnp.float32) * (dh ** -0.5)
    qi = jnp.arange(BLOCK)[:, None]
    ki = jnp.arange(2 * BLOCK)[None, :]
    dist = qi + BLOCK - ki
    kpos = jnp.arange(nblk)[:, None, None] * BLOCK + ki[None] - BLOCK
    mask = (dist >= 0) & (dist <= n_back) & (kpos >= 0)
    s = jnp.where(mask[None, :, None, None], s, NEG_INF)
    m = jnp.max(s, axis=-1)
    if sink is not None:
        sk = sink.astype(jnp.float32).reshape(Hkv, G)[None, None, :, :, None]
        m = jnp.maximum(m, sk)
    p = jnp.exp(s - m[..., None])
    denom = jnp.sum(p, axis=-1)
    if sink is not None:
        denom = denom + jnp.exp(sk - m)
    o = jnp.einsum('nbhgqk,nbkhd->nbqhgd', p, vw.astype(jnp.float32))
    denom_t = jnp.moveaxis(denom, -1, 2)
    o = o / denom_t[..., None]
    lse = jnp.moveaxis(m, -1, 2) + jnp.log(denom_t)
    o = o.reshape(N, nblk * BLOCK, Hkv, G, dh)[:, :T]
    lse = lse.reshape(N, nblk * BLOCK, Hkv, G)[:, :T]
    return o.astype(q.dtype), lse


def dilated_attention(q, k, v, window, dilation):
    Bn, T, H, dh = q.shape
    r = dilation
    tsub = -(-T // r)
    pad = tsub * r - T

    def to_strided(t):
        t = jnp.pad(t, ((0, 0), (0, pad), (0, 0), (0, 0)))
        return t.reshape(Bn, tsub, r, H, dh).transpose(0, 2, 1, 3, 4).reshape(Bn * r, tsub, H, dh)

    o, lse = banded_window_attention(to_strided(q)[:, :, :, None], to_strided(k), to_strided(v),
                                     window // r)
    o = o[:, :, :, 0].reshape(Bn, r, tsub, H, dh).transpose(0, 2, 1, 3, 4).reshape(Bn, tsub * r, H, dh)[:, :T]
    lse = lse[:, :, :, 0].reshape(Bn, r, tsub, H).transpose(0, 2, 1, 3).reshape(Bn, tsub * r, H)[:, :T]
    return o, lse


def mixer(u, positions, w_in, sinks, w_branch_a, w_branch_b, w_o):
    Bn, T, _ = u.shape
    proj = jnp.einsum('btd,de->bte', u, w_in)
    sizes = [A_Q_W, A_KV_W, A_KV_W, B_W, B_W, B_W, D_MODEL]
    offs, acc = [], 0
    for s_ in sizes:
        acc += s_
        offs.append(acc)
    qa, ka, va, qb, kb, vb, ga, gb = jnp.split(proj, offs, axis=-1)

    qa = rope(qa.reshape(Bn, T, A_Q_HEADS, HEAD_DIM), positions).reshape(Bn, T, A_KV_HEADS, A_GROUP, HEAD_DIM)
    ka = rope(ka.reshape(Bn, T, A_KV_HEADS, HEAD_DIM), positions)
    va = va.reshape(Bn, T, A_KV_HEADS, HEAD_DIM)
    oa, _ = banded_window_attention(qa, ka, va, A_WINDOW - 1, sinks)
    ya = jnp.einsum('bte,ed->btd', oa.reshape(Bn, T, A_Q_W), w_branch_a)

    qb = rope(qb.reshape(Bn, T, B_HEADS, HEAD_DIM), positions)
    kb = rope(kb.reshape(Bn, T, B_HEADS, HEAD_DIM), positions)
    vb = vb.reshape(Bn, T, B_HEADS, HEAD_DIM)
    outs, lses = [], []
    for g, (window, dil) in enumerate(B_PATTERNS):
        sl = slice(g * B_HEADS_PER_GROUP, (g + 1) * B_HEADS_PER_GROUP)
        o_g, l_g = dilated_attention(qb[:, :, sl], kb[:, :, sl], vb[:, :, sl], window, dil)
        outs.append(o_g)
        lses.append(l_g)
    o_all = jnp.stack(outs).astype(jnp.float32)
    wts = jax.nn.softmax(jnp.stack(lses), axis=0)
    ob = jnp.sum(wts[..., None] * o_all, axis=0).astype(u.dtype)
    yb = jnp.einsum('bte,ed->btd', ob.reshape(Bn, T, B_OUT_W), w_branch_b)

    merged = jax.nn.sigmoid(ga) * ya + jax.nn.sigmoid(gb) * yb
    return jnp.einsum('btd,de->bte', merged, w_o)


def swiglu(u, w_gate_up, w_down):
    h = jnp.einsum('btd,df->btf', u, w_gate_up)
    hg, hu = jnp.split(h, 2, axis=-1)
    return jnp.einsum('btf,fd->btd', jax.nn.silu(hg) * hu, w_down)


def _fwd_setup_inputs(seed: int = 0) -> dict:
    key = jax.random.key(seed)
    ks = jax.random.split(key, 20)
    f32 = jnp.float32
    nrm = lambda k, shape, scale: jax.random.normal(k, shape, f32) * scale
    x = jax.random.normal(ks[0], (BATCH, SEQ, D_MODEL), f32)
    c = jax.random.normal(ks[1], (BATCH, D_MODEL), f32)
    offset = jax.random.randint(ks[2], (BATCH, 1), 0, 1024, dtype=jnp.int32)
    positions = offset + jnp.arange(SEQ, dtype=jnp.int32)[None, :]
    return {
        "x": x,
        "c": c,
        "positions": positions,
        "w_ada": nrm(ks[3], (DEPTH, D_MODEL, 6 * D_MODEL), 0.1 * D_MODEL ** -0.5),
        "b_ada": nrm(ks[4], (DEPTH, 6 * D_MODEL), 0.01),
        "w_in": nrm(ks[5], (DEPTH, D_MODEL, IN_W), D_MODEL ** -0.5),
        "sinks": nrm(ks[6], (DEPTH, A_Q_HEADS), 1.0),
        "w_branch_a": nrm(ks[7], (DEPTH, A_Q_W, D_MODEL), A_Q_W ** -0.5),
        "w_branch_b": nrm(ks[8], (DEPTH, B_OUT_W, D_MODEL), B_OUT_W ** -0.5),
        "w_o": nrm(ks[9], (DEPTH, D_MODEL, D_MODEL), DEEPNORM_BETA * D_MODEL ** -0.5),
        "ln1_g": 1.0 + nrm(ks[10], (DEPTH, D_MODEL), 0.02),
        "ln1_b": nrm(ks[11], (DEPTH, D_MODEL), 0.02),
        "w_gate_up": nrm(ks[12], (DEPTH, D_MODEL, 2 * D_FF), D_MODEL ** -0.5),
        "w_down": nrm(ks[13], (DEPTH, D_FF, D_MODEL), DEEPNORM_BETA * D_FF ** -0.5),
        "ln2_g": 1.0 + nrm(ks[14], (DEPTH, D_MODEL), 0.02),
        "ln2_b": nrm(ks[15], (DEPTH, D_MODEL), 0.02),
    }


def _fwd_reference(x, c, positions, w_ada, b_ada, w_in, sinks, w_branch_a, w_branch_b, w_o,
              ln1_g, ln1_b, w_gate_up, w_down, ln2_g, ln2_b):
    c_act = jax.nn.silu(c)
    for l in range(DEPTH):
        mod = (jnp.einsum('bd,de->be', c_act, w_ada[l]) + b_ada[l])[:, None, :]
        shift_m, scale_m, gate_m, shift_f, scale_f, gate_f = jnp.split(mod, 6, axis=-1)
        u = x * (1.0 + scale_m) + shift_m
        y = mixer(u, positions, w_in[l], sinks[l], w_branch_a[l], w_branch_b[l], w_o[l])
        x = layer_norm(DEEPNORM_ALPHA * x + (1.0 + gate_m) * y, ln1_g[l], ln1_b[l])
        u = x * (1.0 + scale_f) + shift_f
        y = swiglu(u, w_gate_up[l], w_down[l])
        x = layer_norm(DEEPNORM_ALPHA * x + (1.0 + gate_f) * y, ln2_g[l], ln2_b[l])
    return x


import jax as _jax
import jax.numpy as _jnp

TWIN_FORMAT = 'train_step'
FWD_PARAMS = ['x', 'c', 'positions', 'w_ada', 'b_ada', 'w_in', 'sinks', 'w_branch_a', 'w_branch_b', 'w_o', 'ln1_g', 'ln1_b', 'w_gate_up', 'w_down', 'ln2_g', 'ln2_b']
TWIN_WEIGHTS = ['w_ada', 'b_ada', 'w_in', 'sinks', 'w_branch_a', 'w_branch_b', 'w_o', 'ln1_g', 'ln1_b', 'w_gate_up', 'w_down', 'ln2_g', 'ln2_b']
TWIN_DIFF_INPUT = 'x'
TWIN_INPUTS = ['x', 'c', 'positions', 'w_ada', 'b_ada', 'w_in', 'sinks', 'w_branch_a', 'w_branch_b', 'w_o', 'ln1_g', 'ln1_b', 'w_gate_up', 'w_down', 'ln2_g', 'ln2_b', 'loss_target', 'm_w_ada', 'm_b_ada', 'm_w_in', 'm_sinks', 'm_w_branch_a', 'm_w_branch_b', 'm_w_o', 'm_ln1_g', 'm_ln1_b', 'm_w_gate_up', 'm_w_down', 'm_ln2_g', 'm_ln2_b', 'v_w_ada', 'v_b_ada', 'v_w_in', 'v_sinks', 'v_w_branch_a', 'v_w_branch_b', 'v_w_o', 'v_ln1_g', 'v_ln1_b', 'v_w_gate_up', 'v_w_down', 'v_ln2_g', 'v_ln2_b']
TWIN_OUTPUTS = ['loss', 'grad_x', 'grad_w_ada', 'grad_b_ada', 'grad_w_in', 'grad_sinks', 'grad_w_branch_a', 'grad_w_branch_b', 'grad_w_o', 'grad_ln1_g', 'grad_ln1_b', 'grad_w_gate_up', 'grad_w_down', 'grad_ln2_g', 'grad_ln2_b', 'delta_w_ada', 'delta_b_ada', 'delta_w_in', 'delta_sinks', 'delta_w_branch_a', 'delta_w_branch_b', 'delta_w_o', 'delta_ln1_g', 'delta_ln1_b', 'delta_w_gate_up', 'delta_w_down', 'delta_ln2_g', 'delta_ln2_b', 'new_m_w_ada', 'new_m_b_ada', 'new_m_w_in', 'new_m_sinks', 'new_m_w_branch_a', 'new_m_w_branch_b', 'new_m_w_o', 'new_m_ln1_g', 'new_m_ln1_b', 'new_m_w_gate_up', 'new_m_w_down', 'new_m_ln2_g', 'new_m_ln2_b', 'new_v_w_ada', 'new_v_b_ada', 'new_v_w_in', 'new_v_sinks', 'new_v_w_branch_a', 'new_v_w_branch_b', 'new_v_w_o', 'new_v_ln1_g', 'new_v_ln1_b', 'new_v_w_gate_up', 'new_v_w_down', 'new_v_ln2_g', 'new_v_ln2_b']
TWIN_LEAF_KINDS = {'loss': 'loss', 'grad_x': 'grad_x', 'grad_w_ada': 'grad_w', 'grad_b_ada': 'grad_w', 'grad_w_in': 'grad_w', 'grad_sinks': 'grad_w', 'grad_w_branch_a': 'grad_w', 'grad_w_branch_b': 'grad_w', 'grad_w_o': 'grad_w', 'grad_ln1_g': 'grad_w', 'grad_ln1_b': 'grad_w', 'grad_w_gate_up': 'grad_w', 'grad_w_down': 'grad_w', 'grad_ln2_g': 'grad_w', 'grad_ln2_b': 'grad_w', 'delta_w_ada': 'delta_w', 'delta_b_ada': 'delta_w', 'delta_w_in': 'delta_w', 'delta_sinks': 'delta_w', 'delta_w_branch_a': 'delta_w', 'delta_w_branch_b': 'delta_w', 'delta_w_o': 'delta_w', 'delta_ln1_g': 'delta_w', 'delta_ln1_b': 'delta_w', 'delta_w_gate_up': 'delta_w', 'delta_w_down': 'delta_w', 'delta_ln2_g': 'delta_w', 'delta_ln2_b': 'delta_w', 'new_m_w_ada': 'new_m', 'new_m_b_ada': 'new_m', 'new_m_w_in': 'new_m', 'new_m_sinks': 'new_m', 'new_m_w_branch_a': 'new_m', 'new_m_w_branch_b': 'new_m', 'new_m_w_o': 'new_m', 'new_m_ln1_g': 'new_m', 'new_m_ln1_b': 'new_m', 'new_m_w_gate_up': 'new_m', 'new_m_w_down': 'new_m', 'new_m_ln2_g': 'new_m', 'new_m_ln2_b': 'new_m', 'new_v_w_ada': 'new_v', 'new_v_b_ada': 'new_v', 'new_v_w_in': 'new_v', 'new_v_sinks': 'new_v', 'new_v_w_branch_a': 'new_v', 'new_v_w_branch_b': 'new_v', 'new_v_w_o': 'new_v', 'new_v_ln1_g': 'new_v', 'new_v_ln1_b': 'new_v', 'new_v_w_gate_up': 'new_v', 'new_v_w_down': 'new_v', 'new_v_ln2_g': 'new_v', 'new_v_ln2_b': 'new_v'}


def _forward(args):
    return _fwd_reference(*[args[k] for k in FWD_PARAMS])


def _output_shape():
    out = _jax.eval_shape(lambda: _forward(_fwd_setup_inputs(0)))
    return out.shape, out.dtype

N_MICROBATCH = 1
ADAM_LR = 0.001
ADAM_B1 = 0.9
ADAM_B2 = 0.999
ADAM_EPS = 1e-08
ADAM_WD = 0.01
ADAM_STEP = 10
PER_EXAMPLE_BATCH_AXIS = {'x': 0, 'c': 0, 'positions': 0, 'loss_target': 0}
SHARED_INPUTS = []
_WEIGHT_DTYPES = {'w_ada': _jnp.float32, 'b_ada': _jnp.float32, 'w_in': _jnp.float32, 'sinks': _jnp.float32, 'w_branch_a': _jnp.float32, 'w_branch_b': _jnp.float32, 'w_o': _jnp.float32, 'ln1_g': _jnp.float32, 'ln1_b': _jnp.float32, 'w_gate_up': _jnp.float32, 'w_down': _jnp.float32, 'ln2_g': _jnp.float32, 'ln2_b': _jnp.float32}
MOMENT_SCALE = {'w_ada': 3.637290e-02, 'b_ada': 6.751681e-02, 'w_in': 8.395710e-03, 'sinks': 8.696896e-03, 'w_branch_a': 1.055092e-02, 'w_branch_b': 9.882756e-03, 'w_o': 2.431025e-02, 'ln1_g': 9.336940e-01, 'ln1_b': 4.405296e-01, 'w_gate_up': 3.148123e-02, 'w_down': 8.626503e-02, 'ln2_g': 3.199661e+01, 'ln2_b': 6.962133e-01}


def _to_microbatches(a, axis):
    t = _jnp.moveaxis(a, axis, 0)
    t = t.reshape((N_MICROBATCH, t.shape[0] // N_MICROBATCH) + t.shape[1:])
    return _jnp.moveaxis(t, 1, axis + 1)


def setup_inputs(seed: int = 0) -> dict:
    inp = _fwd_setup_inputs(seed)
    key = _jax.random.fold_in(_jax.random.key(seed), 7919)
    shape, _ = _output_shape()
    out = dict(inp)
    out["loss_target"] = _jax.random.normal(_jax.random.fold_in(key, 0), shape, _jnp.float32)
    for i, name in enumerate(TWIN_WEIGHTS):
        w = inp[name].astype(_jnp.float32)
        if MOMENT_SCALE is None:
            s = _jnp.sqrt(_jnp.mean(_jnp.square(w)) + 1e-30)
        else:
            s = MOMENT_SCALE[name]
        km, kv = _jax.random.split(_jax.random.fold_in(key, i + 1))
        out[name] = w
        out["m_" + name] = s * _jax.random.normal(km, w.shape, _jnp.float32)
        out["v_" + name] = (s * s) * _jax.random.uniform(kv, w.shape, _jnp.float32, 0.5, 1.5)
    if N_MICROBATCH > 1:
        for name, axis in PER_EXAMPLE_BATCH_AXIS.items():
            out[name] = _to_microbatches(out[name], axis)
    return {'x': out['x'], 'c': out['c'], 'positions': out['positions'], 'w_ada': out['w_ada'], 'b_ada': out['b_ada'], 'w_in': out['w_in'], 'sinks': out['sinks'], 'w_branch_a': out['w_branch_a'], 'w_branch_b': out['w_branch_b'], 'w_o': out['w_o'], 'ln1_g': out['ln1_g'], 'ln1_b': out['ln1_b'], 'w_gate_up': out['w_gate_up'], 'w_down': out['w_down'], 'ln2_g': out['ln2_g'], 'ln2_b': out['ln2_b'], 'loss_target': out['loss_target'], 'm_w_ada': out['m_w_ada'], 'm_b_ada': out['m_b_ada'], 'm_w_in': out['m_w_in'], 'm_sinks': out['m_sinks'], 'm_w_branch_a': out['m_w_branch_a'], 'm_w_branch_b': out['m_w_branch_b'], 'm_w_o': out['m_w_o'], 'm_ln1_g': out['m_ln1_g'], 'm_ln1_b': out['m_ln1_b'], 'm_w_gate_up': out['m_w_gate_up'], 'm_w_down': out['m_w_down'], 'm_ln2_g': out['m_ln2_g'], 'm_ln2_b': out['m_ln2_b'], 'v_w_ada': out['v_w_ada'], 'v_b_ada': out['v_b_ada'], 'v_w_in': out['v_w_in'], 'v_sinks': out['v_sinks'], 'v_w_branch_a': out['v_w_branch_a'], 'v_w_branch_b': out['v_w_branch_b'], 'v_w_o': out['v_w_o'], 'v_ln1_g': out['v_ln1_g'], 'v_ln1_b': out['v_ln1_b'], 'v_w_gate_up': out['v_w_gate_up'], 'v_w_down': out['v_w_down'], 'v_ln2_g': out['v_ln2_g'], 'v_ln2_b': out['v_ln2_b']}


def _loss(weights, diff, rest, loss_target):
    with _jax.named_scope("forward"):
        args = {**rest, TWIN_DIFF_INPUT: diff, **{k: w.astype(_WEIGHT_DTYPES[k]) for k, w in weights.items()}}
        y = _forward(args)
    with _jax.named_scope("loss_head"):
        err = _jnp.square(y.astype(_jnp.float32) - loss_target)
        return 0.5 * _jnp.sum(_jnp.mean(err, axis=-1)) if err.ndim else 0.5 * err


def _adamw(w, g, m, v):
    m = ADAM_B1 * m + (1.0 - ADAM_B1) * g
    v = ADAM_B2 * v + (1.0 - ADAM_B2) * _jnp.square(g)
    m_hat = m / (1.0 - ADAM_B1 ** ADAM_STEP)
    v_hat = v / (1.0 - ADAM_B2 ** ADAM_STEP)
    delta = -ADAM_LR * (m_hat / (_jnp.sqrt(v_hat) + ADAM_EPS) + ADAM_WD * w)
    return delta, m, v


def reference(x, c, positions, w_ada, b_ada, w_in, sinks, w_branch_a, w_branch_b, w_o, ln1_g, ln1_b, w_gate_up, w_down, ln2_g, ln2_b, loss_target, m_w_ada, m_b_ada, m_w_in, m_sinks, m_w_branch_a, m_w_branch_b, m_w_o, m_ln1_g, m_ln1_b, m_w_gate_up, m_w_down, m_ln2_g, m_ln2_b, v_w_ada, v_b_ada, v_w_in, v_sinks, v_w_branch_a, v_w_branch_b, v_w_o, v_ln1_g, v_ln1_b, v_w_gate_up, v_w_down, v_ln2_g, v_ln2_b):
    given = dict(x=x, c=c, positions=positions, w_ada=w_ada, b_ada=b_ada, w_in=w_in, sinks=sinks, w_branch_a=w_branch_a, w_branch_b=w_branch_b, w_o=w_o, ln1_g=ln1_g, ln1_b=ln1_b, w_gate_up=w_gate_up, w_down=w_down, ln2_g=ln2_g, ln2_b=ln2_b, loss_target=loss_target, m_w_ada=m_w_ada, m_b_ada=m_b_ada, m_w_in=m_w_in, m_sinks=m_sinks, m_w_branch_a=m_w_branch_a, m_w_branch_b=m_w_branch_b, m_w_o=m_w_o, m_ln1_g=m_ln1_g, m_ln1_b=m_ln1_b, m_w_gate_up=m_w_gate_up, m_w_down=m_w_down, m_ln2_g=m_ln2_g, m_ln2_b=m_ln2_b, v_w_ada=v_w_ada, v_b_ada=v_b_ada, v_w_in=v_w_in, v_sinks=v_sinks, v_w_branch_a=v_w_branch_a, v_w_branch_b=v_w_branch_b, v_w_o=v_w_o, v_ln1_g=v_ln1_g, v_ln1_b=v_ln1_b, v_w_gate_up=v_w_gate_up, v_w_down=v_w_down, v_ln2_g=v_ln2_g, v_ln2_b=v_ln2_b)
    weights = {n: given[n] for n in TWIN_WEIGHTS}
    shared = {n: given[n] for n in SHARED_INPUTS}
    per_example = {n: given[n] for n in ['x', 'c', 'positions']}
    grad_fn = _jax.value_and_grad(_loss, argnums=(0, 1))

    def one_microbatch(ex, loss_target):
        ex = dict(ex)
        diff = ex.pop(TWIN_DIFF_INPUT)
        return grad_fn(weights, diff, {**shared, **ex}, loss_target)

    if N_MICROBATCH == 1:
        loss, (grad_w, grad_x) = one_microbatch(per_example, given["loss_target"])
    else:
        def body(carry, xs):
            loss_sum, grad_sum = carry
            l_k, (gw_k, gx_k) = one_microbatch(xs[0], xs[1])
            with _jax.named_scope("update"):
                return (loss_sum + l_k, _jax.tree.map(_jnp.add, grad_sum, gw_k)), gx_k

        init = (_jnp.zeros((), _jnp.float32), _jax.tree.map(_jnp.zeros_like, weights))
        (loss, grad_w), grad_x = _jax.lax.scan(body, init, (per_example, given["loss_target"]))
    with _jax.named_scope("update"):
        delta_w, new_m, new_v = {}, {}, {}
        for n in TWIN_WEIGHTS:
            delta_w[n], new_m[n], new_v[n] = _adamw(weights[n], grad_w[n], given["m_" + n], given["v_" + n])
    return (loss, grad_x, *[grad_w[n] for n in TWIN_WEIGHTS], *[delta_w[n] for n in TWIN_WEIGHTS],
            *[new_m[n] for n in TWIN_WEIGHTS], *[new_v[n] for n in TWIN_WEIGHTS])
```

```python
import functools

import jax
import jax.numpy as jnp
from jax import lax
from jax.experimental import pallas as pl
from jax.experimental.pallas import tpu as pltpu

F32 = jnp.float32
BF16 = jnp.bfloat16

D_MODEL = 1024
HEAD_DIM = 64
A_Q_HEADS = 16
A_KV_HEADS = 2
A_WINDOW = 128
B_PATTERNS = ((128, 1), (512, 4), (2048, 16))
B_HEADS_PER_GROUP = 8
D_FF = 2816
QBLOCK = 128
ROPE_THETA = 10000.0
LN_EPS = 1e-5
DEEPNORM_ALPHA = 2.0 ** 0.25
NEG_INF = -1e30
ADAM_LR, ADAM_B1, ADAM_B2, ADAM_EPS, ADAM_WD, ADAM_STEP = 0.001, 0.9, 0.999, 1e-08, 0.01, 10

N_DEV = 8
MESH_AXES = ("x", "y", "c")
LANES = 128
VMEM_LIMIT_BYTES = 56 * 1024 * 1024
MESH = pl.DeviceIdType.MESH

OFF_QA, OFF_KVA, OFF_QKVB, OFF_GAB = 0, 1024, 1280, 5888
PACK_ROWS = (("w_in", 992), ("w_branch_a", 128), ("w_branch_b", 64), ("w_o", 128), ("w_gate_up", 704), ("w_down", 352))
PACK_TOTAL = sum(r for _, r in PACK_ROWS)


def _params(*sem):
    return pltpu.CompilerParams(dimension_semantics=sem, vmem_limit_bytes=VMEM_LIMIT_BYTES)


def _sigmoid(x):
    return 1.0 / (1.0 + jnp.exp(-x))


_DIMS = {"nn": (((1,), (0,)), ((), ())), "nt": (((1,), (1,)), ((), ())), "tn": (((0,), (0,)), ((), ()))}


def _matmul(a, b, *, mode, out_dtype, tm, tn, tk, name, n=None, b_off=0):
    if mode == "nn":
        (m, k), nn_ = a.shape, b.shape[1]
    elif mode == "nt":
        (m, k), nn_ = a.shape, (b.shape[0] if n is None else n)
    else:
        (k, m), nn_ = a.shape, b.shape[1]
    assert m % tm == 0 and nn_ % tn == 0 and k % tk == 0 and b_off % tn == 0, (name, m, nn_, k)
    nk = k // tk
    joff = b_off // tn
    if mode == "nn":
        a_spec = pl.BlockSpec((tm, tk), lambda i, j, kk: (i, kk))
        b_spec = pl.BlockSpec((tk, tn), lambda i, j, kk: (kk, j))
    elif mode == "nt":
        a_spec = pl.BlockSpec((tm, tk), lambda i, j, kk: (i, kk))
        b_spec = pl.BlockSpec((tn, tk), lambda i, j, kk: (j + joff, kk))
    else:
        a_spec = pl.BlockSpec((tk, tm), lambda i, j, kk: (kk, i))
        b_spec = pl.BlockSpec((tk, tn), lambda i, j, kk: (kk, j))
    dims = _DIMS[mode]

    def body(a_ref, b_ref, o_ref, acc_ref):
        kk = pl.program_id(2)
        part = lax.dot_general(a_ref[...].astype(BF16), b_ref[...].astype(BF16), dims, preferred_element_type=F32)
        if nk == 1:
            o_ref[...] = part.astype(o_ref.dtype)
        else:
            @pl.when(kk == 0)
            def _():
                acc_ref[...] = part

            @pl.when(kk > 0)
            def _():
                acc_ref[...] += part

            @pl.when(kk == nk - 1)
            def _():
                o_ref[...] = acc_ref[...].astype(o_ref.dtype)

    return pl.pallas_call(
        body,
        name=name,
        grid=(m // tm, nn_ // tn, nk),
        in_specs=[a_spec, b_spec],
        out_specs=pl.BlockSpec((tm, tn), lambda i, j, kk: (i, j)),
        out_shape=jax.ShapeDtypeStruct((m, nn_), out_dtype),
        scratch_shapes=[pltpu.VMEM((tm, tn) if nk > 1 else (8, LANES), F32)],
        compiler_params=_params("parallel", "parallel", "arbitrary"),
    )(a, b)


ROW_TILE = 256


def _rows(width, col=0):
    return pl.BlockSpec((1, ROW_TILE, width), lambda b, t: (b, t, col))


def _per_batch(nrows, width):
    return pl.BlockSpec((1, nrows, width), lambda b, t: (b, 0, 0))


def _whole(shape):
    return pl.BlockSpec(shape, lambda b, t: (0,) * len(shape))


def _row_call(body, name, bsz, seq, in_specs, out_specs, out_shape, accumulates=False):
    return pl.pallas_call(
        body,
        name=name,
        grid=(bsz, seq // ROW_TILE),
        in_specs=in_specs,
        out_specs=out_specs,
        out_shape=out_shape,
        compiler_params=_params("parallel", "arbitrary" if accumulates else "parallel"),
    )


def _acc_rows(acc_ref, first, rows):
    @pl.when(first)
    def _():
        acc_ref[...] = jnp.zeros_like(acc_ref)

    for r, val in enumerate(rows):
        acc_ref[0, r:r + 1, :] += val


def _colsum(v):
    return jnp.sum(v, axis=0, keepdims=True)


def _ln_stats(z):
    mu = jnp.mean(z, axis=-1, keepdims=True)
    zc = z - mu
    var = jnp.mean(zc * zc, axis=-1, keepdims=True)
    rstd = lax.rsqrt(var + LN_EPS)
    return zc * rstd, rstd


def _ln_bwd(dxhat, xhat, rstd):
    m1 = jnp.mean(dxhat, axis=-1, keepdims=True)
    m2 = jnp.mean(dxhat * xhat, axis=-1, keepdims=True)
    return rstd * (dxhat - m1 - xhat * m2)


def _modulate_in(x, mod):
    bsz, seq, d = x.shape

    def body(x_ref, mod_ref, u_ref):
        u_ref[0] = (x_ref[0] * (1.0 + mod_ref[0, 1:2, :]) + mod_ref[0, 0:1, :]).astype(BF16)

    return _row_call(body, "modulate_in", bsz, seq, [_rows(d), _per_batch(8, d)], _rows(d),
                     jax.ShapeDtypeStruct((bsz, seq, d), BF16))(x, mod)


def _gate_merge(gab, ya, yb):
    bsz, seq, d = ya.shape

    def body(ga_ref, gb_ref, ya_ref, yb_ref, o_ref):
        o_ref[0] = (_sigmoid(ga_ref[0]) * ya_ref[0] + _sigmoid(gb_ref[0]) * yb_ref[0]).astype(BF16)

    return _row_call(body, "gate_merge", bsz, seq, [_rows(d, 0), _rows(d, 1), _rows(d), _rows(d)], _rows(d),
                     jax.ShapeDtypeStruct((bsz, seq, d), BF16))(gab, gab, ya, yb)


def _ln1_fwd(x, y1, mod, g, b):
    bsz, seq, d = x.shape

    def body(x_ref, y_ref, mod_ref, g_ref, b_ref, h_ref, u_ref):
        z = DEEPNORM_ALPHA * x_ref[0] + (1.0 + mod_ref[0, 2:3, :]) * y_ref[0]
        xhat, _ = _ln_stats(z)
        h = xhat * g_ref[...] + b_ref[...]
        h_ref[0] = h
        u_ref[0] = (h * (1.0 + mod_ref[0, 4:5, :]) + mod_ref[0, 3:4, :]).astype(BF16)

    return _row_call(body, "ln1_fwd", bsz, seq,
                     [_rows(d), _rows(d), _per_batch(8, d), _whole((1, d)), _whole((1, d))],
                     [_rows(d), _rows(d)],
                     [jax.ShapeDtypeStruct((bsz, seq, d), F32), jax.ShapeDtypeStruct((bsz, seq, d), BF16)])(x, y1, mod, g, b)


def _silu_mul(h):
    bsz, seq, _ = h.shape

    def body(hg_ref, hu_ref, a_ref):
        hg = hg_ref[0]
        a_ref[0] = (hg * _sigmoid(hg) * hu_ref[0]).astype(BF16)

    return _row_call(body, "silu_mul", bsz, seq, [_rows(D_FF, 0), _rows(D_FF, 1)], _rows(D_FF),
                     jax.ShapeDtypeStruct((bsz, seq, D_FF), BF16))(h, h)


def _ln2_loss_bwd(h1, y2, mod, g, b, target):
    bsz, seq, d = h1.shape

    def body(h_ref, y_ref, mod_ref, g_ref, b_ref, t_ref, dy_ref, dh_ref, acc_ref):
        y = y_ref[0]
        gate = 1.0 + mod_ref[0, 5:6, :]
        z = DEEPNORM_ALPHA * h_ref[0] + gate * y
        xhat, rstd = _ln_stats(z)
        diff = xhat * g_ref[...] + b_ref[...] - t_ref[0]
        loss = 0.5 * jnp.sum(jnp.sum(diff * diff, axis=-1, keepdims=True) / d, axis=0, keepdims=True)
        dout = diff / d
        dz = _ln_bwd(dout * g_ref[...], xhat, rstd)
        dy_ref[0] = (gate * dz).astype(BF16)
        dh_ref[0] = DEEPNORM_ALPHA * dz
        _acc_rows(acc_ref, pl.program_id(1) == 0,
                  [_colsum(dout * xhat), _colsum(dout), _colsum(dz * y), jnp.broadcast_to(loss, (1, d))])

    return _row_call(body, "ln2_loss_bwd", bsz, seq,
                     [_rows(d), _rows(d), _per_batch(8, d), _whole((1, d)), _whole((1, d)), _rows(d)],
                     [_rows(d), _rows(d), _per_batch(8, d)],
                     [jax.ShapeDtypeStruct((bsz, seq, d), BF16), jax.ShapeDtypeStruct((bsz, seq, d), F32),
                      jax.ShapeDtypeStruct((bsz, 8, d), F32)], accumulates=True)(h1, y2, mod, g, b, target)


def _silu_mul_bwd(da, h):
    bsz, seq, _ = h.shape

    def body(da_ref, hg_ref, hu_ref, dh_ref):
        hg, da_ = hg_ref[0], da_ref[0]
        sg = _sigmoid(hg)
        dh_ref[0, :, :D_FF] = (da_ * hu_ref[0] * (sg * (1.0 + hg * (1.0 - sg)))).astype(BF16)
        dh_ref[0, :, D_FF:] = (da_ * (hg * sg)).astype(BF16)

    return _row_call(body, "silu_mul_bwd", bsz, seq, [_rows(D_FF), _rows(D_FF, 0), _rows(D_FF, 1)], _rows(2 * D_FF),
                     jax.ShapeDtypeStruct((bsz, seq, 2 * D_FF), BF16))(da, h, h)


def _ln1_bwd(du2, dh1a, x, y1, mod, g, b):
    bsz, seq, d = x.shape

    def body(du_ref, dh_ref, x_ref, y_ref, mod_ref, g_ref, b_ref, dy_ref, dx_ref, acc_ref):
        y, du = y_ref[0], du_ref[0]
        gate = 1.0 + mod_ref[0, 2:3, :]
        z = DEEPNORM_ALPHA * x_ref[0] + gate * y
        xhat, rstd = _ln_stats(z)
        h1 = xhat * g_ref[...] + b_ref[...]
        dh1 = dh_ref[0] + du * (1.0 + mod_ref[0, 4:5, :])
        dz = _ln_bwd(dh1 * g_ref[...], xhat, rstd)
        dy_ref[0] = (gate * dz).astype(BF16)
        dx_ref[0] = DEEPNORM_ALPHA * dz
        _acc_rows(acc_ref, pl.program_id(1) == 0,
                  [_colsum(dh1 * xhat), _colsum(dh1), _colsum(dz * y), _colsum(du * h1), _colsum(du)])

    return _row_call(body, "ln1_bwd", bsz, seq,
                     [_rows(d), _rows(d), _rows(d), _rows(d), _per_batch(8, d), _whole((1, d)), _whole((1, d))],
                     [_rows(d), _rows(d), _per_batch(8, d)],
                     [jax.ShapeDtypeStruct((bsz, seq, d), BF16), jax.ShapeDtypeStruct((bsz, seq, d), F32),
                      jax.ShapeDtypeStruct((bsz, 8, d), F32)], accumulates=True)(du2, dh1a, x, y1, mod, g, b)


def _gate_merge_bwd(dm, gab, ya, yb):
    bsz, seq, d = ya.shape

    def body(dm_ref, ga_ref, gb_ref, ya_ref, yb_ref, dya_ref, dyb_ref, dg_ref):
        dm_ = dm_ref[0]
        sa, sb = _sigmoid(ga_ref[0]), _sigmoid(gb_ref[0])
        dya_ref[0] = (dm_ * sa).astype(BF16)
        dyb_ref[0] = (dm_ * sb).astype(BF16)
        dg_ref[0, :, :d] = (dm_ * ya_ref[0] * sa * (1.0 - sa)).astype(BF16)
        dg_ref[0, :, d:] = (dm_ * yb_ref[0] * sb * (1.0 - sb)).astype(BF16)

    return _row_call(body, "gate_merge_bwd", bsz, seq,
                     [_rows(d), _rows(d, 0), _rows(d, 1), _rows(d), _rows(d)],
                     [_rows(d), _rows(d), _rows(2 * d)],
                     [jax.ShapeDtypeStruct((bsz, seq, d), BF16), jax.ShapeDtypeStruct((bsz, seq, d), BF16),
                      jax.ShapeDtypeStruct((bsz, seq, 2 * d), BF16)])(dm, gab, gab, ya, yb)


def _grad_x(dxa, du1, x, mod):
    bsz, seq, d = x.shape

    def body(dxa_ref, du_ref, x_ref, mod_ref, gx_ref, acc_ref):
        du = du_ref[0]
        gx_ref[0] = dxa_ref[0] + du * (1.0 + mod_ref[0, 1:2, :])
        _acc_rows(acc_ref, pl.program_id(1) == 0, [_colsum(du * x_ref[0]), _colsum(du)])

    return _row_call(body, "grad_x", bsz, seq, [_rows(d), _rows(d), _rows(d), _per_batch(8, d)],
                     [_rows(d), _per_batch(8, d)],
                     [jax.ShapeDtypeStruct((bsz, seq, d), F32), jax.ShapeDtypeStruct((bsz, 8, d), F32)],
                     accumulates=True)(dxa, du1, x, mod)


def _segsum64(v):
    rows, width = v.shape
    ri = lax.broadcasted_iota(jnp.int32, (LANES, LANES), 0) // HEAD_DIM
    ci = lax.broadcasted_iota(jnp.int32, (LANES, LANES), 1) // HEAD_DIM
    ones = jnp.where(ri == ci, 1.0, 0.0).astype(BF16)
    out = []
    for c in range(width // LANES):
        part = v[:, c * LANES:(c + 1) * LANES]
        hi = part.astype(BF16)
        lo = (part - hi.astype(F32)).astype(BF16)
        out.append(jnp.dot(hi, ones, preferred_element_type=F32) + jnp.dot(lo, ones, preferred_element_type=F32))
    return jnp.concatenate(out, axis=1) if len(out) > 1 else out[0]


def _merge_b(os_, ls_):
    bsz, seq, w = os_[0].shape

    def body(o0, o1, o2, l0, l1, l2, ob_ref):
        ls = [l0[0], l1[0], l2[0]]
        mx = jnp.maximum(jnp.maximum(ls[0], ls[1]), ls[2])
        es = [jnp.exp(l - mx) for l in ls]
        den = es[0] + es[1] + es[2]
        ob_ref[0] = ((es[0] / den) * o0[0] + (es[1] / den) * o1[0] + (es[2] / den) * o2[0]).astype(BF16)

    return _row_call(body, "merge_b", bsz, seq, [_rows(w)] * 6, _rows(w),
                     jax.ShapeDtypeStruct((bsz, seq, w), BF16))(*os_, *ls_)


def _merge_b_bwd(dob, os_, ls_):
    bsz, seq, w = os_[0].shape

    def body(dob_ref, o0, o1, o2, l0, l1, l2, do0, do1, do2, dd0, dd1, dd2):
        dob_ = dob_ref[0]
        ls = [l0[0], l1[0], l2[0]]
        mx = jnp.maximum(jnp.maximum(ls[0], ls[1]), ls[2])
        es = [jnp.exp(l - mx) for l in ls]
        den = es[0] + es[1] + es[2]
        ws = [e / den for e in es]
        dws = [_segsum64(dob_ * o[0]) for o in (o0, o1, o2)]
        mean = ws[0] * dws[0] + ws[1] * dws[1] + ws[2] * dws[2]
        for wg, dwg, do_ref, dd_ref in zip(ws, dws, (do0, do1, do2), (dd0, dd1, dd2)):
            do_ref[0] = wg * dob_
            dd_ref[0] = -wg * mean

    shp = jax.ShapeDtypeStruct((bsz, seq, w), F32)
    return _row_call(body, "merge_b_bwd", bsz, seq, [_rows(w)] * 7, [_rows(w)] * 6, [shp] * 6)(dob, *os_, *ls_)


def _delta_a(doa, oa, lse_a, sinks_exp):
    bsz, seq, w = oa.shape

    def body(do_ref, o_ref, l_ref, s_ref, dd_ref, acc_ref):
        dd = -_segsum64(do_ref[0] * o_ref[0])
        dd_ref[0] = dd
        _acc_rows(acc_ref, pl.program_id(1) == 0, [_colsum(dd * jnp.exp(s_ref[...] - l_ref[0]))])

    return _row_call(body, "delta_a", bsz, seq, [_rows(w), _rows(w), _rows(w), _whole((1, w))],
                     [_rows(w), _per_batch(8, w)],
                     [jax.ShapeDtypeStruct((bsz, seq, w), F32), jax.ShapeDtypeStruct((bsz, 8, w), F32)],
                     accumulates=True)(doa, oa, lse_a, sinks_exp)


def _swap_halves(v):
    lane = lax.broadcasted_iota(jnp.int32, v.shape, 1)
    return jnp.where((lane % HEAD_DIM) < HEAD_DIM // 2, pltpu.roll(v, LANES - HEAD_DIM // 2, 1),
                     pltpu.roll(v, HEAD_DIM // 2, 1))


def _rope(v, cos, sin, sign=1.0):
    out = []
    for c in range(v.shape[1] // LANES):
        part = v[:, c * LANES:(c + 1) * LANES]
        out.append(part * cos + sign * (_swap_halves(part) * sin))
    return jnp.concatenate(out, axis=1) if len(out) > 1 else out[0]


def _stack_heads(v, heads):
    return jnp.concatenate([v[:, h * HEAD_DIM:(h + 1) * HEAD_DIM] for h in heads], axis=0) if len(heads) > 1 else \
        v[:, heads[0] * HEAD_DIM:(heads[0] + 1) * HEAD_DIM]


def _stack_cols(v, heads):
    return jnp.concatenate([v[:, h * HEAD_DIM:h * HEAD_DIM + 1] for h in heads], axis=0) if len(heads) > 1 else \
        v[:, heads[0] * HEAD_DIM:heads[0] * HEAD_DIM + 1]


def _for_each_class(r, fn):
    if r == 1:
        fn(pl.ds(0, QBLOCK))
    else:
        def step(rho, carry):
            fn(pl.ds(rho, QBLOCK, stride=r))
            return carry

        lax.fori_loop(0, r, step, 0)


def _attn_fwd(q_arr, k_arr, v_arr, cos, sin, *, name, hq, hkv, q_col, k_col, v_col, nchunk, r, n_back, sink_rows=None):
    bsz, seq, _ = q_arr.shape
    rr = QBLOCK * r
    nblk = seq // rr
    grp = hq // hkv
    qw, kw = hq * HEAD_DIM, hkv * HEAD_DIM
    has_prev = nblk > 1
    has_sink = sink_rows is not None

    def body(*refs):
        refs = list(refs)
        q_ref, kc_ref, vc_ref, cc_ref, sc_ref = refs[:5]
        pos = 5
        if has_prev:
            kp_ref, vp_ref, cp_ref, sp_ref = refs[pos:pos + 4]
            pos += 4
        if has_sink:
            sink_ref = refs[pos]
            pos += 1
        o_ref, lse_ref = refs[pos:pos + 2]
        blk = pl.program_id(2)

        def one_class(rows):
            cq, sq = cc_ref[0, rows, :], sc_ref[0, rows, :]
            q = _rope(q_ref[0, rows, :], cq, sq) * (HEAD_DIM ** -0.5)
            k = _rope(kc_ref[0, rows, :], cq, sq)
            v = vc_ref[0, rows, :]
            if has_prev:
                k = jnp.concatenate([_rope(kp_ref[0, rows, :], cp_ref[0, rows, :], sp_ref[0, rows, :]), k], axis=0)
                v = jnp.concatenate([vp_ref[0, rows, :], v], axis=0)
            nk = k.shape[0]
            qi = lax.broadcasted_iota(jnp.int32, (grp * QBLOCK, nk), 0) % QBLOCK
            ki = lax.broadcasted_iota(jnp.int32, (grp * QBLOCK, nk), 1)
            if has_prev:
                dist = qi + QBLOCK - ki
                valid = (dist >= 0) & (dist <= n_back) & ((ki >= QBLOCK) | (blk > 0))
            else:
                dist = qi - ki
                valid = (dist >= 0) & (dist <= n_back)
            outs, lses = [], []
            for hk in range(hkv):
                heads = [hk * grp + g for g in range(grp)]
                kh = k[:, hk * HEAD_DIM:(hk + 1) * HEAD_DIM].astype(BF16)
                vh = v[:, hk * HEAD_DIM:(hk + 1) * HEAD_DIM].astype(BF16)
                qs = _stack_heads(q, heads).astype(BF16)
                s = lax.dot_general(qs, kh, _DIMS["nt"], preferred_element_type=F32)
                s = jnp.where(valid, s, NEG_INF)
                m = jnp.max(s, axis=1, keepdims=True)
                if has_sink:
                    sk = sink_ref[hk]
                    m = jnp.maximum(m, sk)
                p = jnp.exp(s - m)
                den = jnp.sum(p, axis=1, keepdims=True)
                if has_sink:
                    den = den + jnp.exp(sk - m)
                o = jnp.dot(p.astype(BF16), vh, preferred_element_type=F32) / den
                lse = m + jnp.log(den)
                for g in range(grp):
                    outs.append(o[g * QBLOCK:(g + 1) * QBLOCK])
                    lses.append(jnp.broadcast_to(lse[g * QBLOCK:(g + 1) * QBLOCK], (QBLOCK, HEAD_DIM)))
            o_ref[0, rows, :] = jnp.concatenate(outs, axis=1)
            lse_ref[0, rows, :] = jnp.concatenate(lses, axis=1)

        _for_each_class(r, one_class)

    def cur(width, col0):
        return pl.BlockSpec((1, rr, width), lambda b, c, i: (b, i, col0 + c))

    def prev(width, col0):
        return pl.BlockSpec((1, rr, width), lambda b, c, i: (b, jnp.maximum(i - 1, 0), col0 + c))

    def table(shift):
        return pl.BlockSpec((1, rr, LANES), lambda b, c, i: (b, jnp.maximum(i - shift, 0), 0))

    in_specs = [cur(qw, q_col), cur(kw, k_col), cur(kw, v_col), table(0), table(0)]
    args = [q_arr, k_arr, v_arr, cos, sin]
    if has_prev:
        in_specs += [prev(kw, k_col), prev(kw, v_col), table(1), table(1)]
        args += [k_arr, v_arr, cos, sin]
    if has_sink:
        in_specs.append(pl.BlockSpec(sink_rows.shape, lambda b, c, i: (0, 0, 0)))
        args.append(sink_rows)
    out_w = nchunk * qw
    return pl.pallas_call(
        body,
        name=name,
        grid=(bsz, nchunk, nblk),
        in_specs=in_specs,
        out_specs=[pl.BlockSpec((1, rr, qw), lambda b, c, i: (b, i, c))] * 2,
        out_shape=[jax.ShapeDtypeStruct((bsz, seq, out_w), F32)] * 2,
        compiler_params=_params("parallel", "parallel", "parallel"),
    )(*args)


def _attn_bwd(q_arr, k_arr, v_arr, cos, sin, do, lse, dd, *, name, hq, hkv, q_col, k_col, v_col, nchunk, r, n_back):
    bsz, seq, _ = q_arr.shape
    rr = QBLOCK * r
    nblk = seq // rr
    grp = hq // hkv
    qw, kw = hq * HEAD_DIM, hkv * HEAD_DIM
    has_next = nblk > 1
    scale = HEAD_DIM ** -0.5

    def body(*refs):
        refs = list(refs)
        k_ref, v_ref = refs[:2]
        cur_refs = refs[2:8]
        pos = 8
        if has_next:
            nxt_refs = refs[pos:pos + 6]
            pos += 6
        dq_ref, dk_ref, dv_ref = refs[pos:pos + 3]
        carry_ref = refs[pos + 3]
        blk = pl.program_id(2)
        if has_next:
            @pl.when(blk == 0)
            def _():
                carry_ref[...] = jnp.zeros_like(carry_ref)

        def one_class(rows):
            qi = lax.broadcasted_iota(jnp.int32, (grp * QBLOCK, QBLOCK), 0) % QBLOCK
            ki = lax.broadcasted_iota(jnp.int32, (grp * QBLOCK, QBLOCK), 1)
            tiles = []
            for which, trefs in (("cur", cur_refs),) + ((("next", nxt_refs),) if has_next else ()):
                q_ref, do_ref, l_ref, dd_ref, c_ref, s_ref = trefs
                cq, sq = c_ref[0, rows, :], s_ref[0, rows, :]
                if which == "cur":
                    valid = qi >= ki
                    ck, sk_ = cq, sq
                else:
                    valid = (qi + QBLOCK - ki <= n_back) & (blk + 1 < nblk)
                q = _rope(q_ref[0, rows, :], cq, sq) * scale
                tiles.append((q, do_ref[0, rows, :], l_ref[0, rows, :], dd_ref[0, rows, :], valid))
            k = _rope(k_ref[0, rows, :], ck, sk_)
            v = v_ref[0, rows, :]
            dq_parts = [[] for _ in tiles]
            dks, dvs = [], []
            for hk in range(hkv):
                heads = [hk * grp + g for g in range(grp)]
                kh = k[:, hk * HEAD_DIM:(hk + 1) * HEAD_DIM].astype(BF16)
                vh = v[:, hk * HEAD_DIM:(hk + 1) * HEAD_DIM].astype(BF16)
                dk_acc = jnp.zeros((QBLOCK, HEAD_DIM), F32)
                dv_acc = jnp.zeros((QBLOCK, HEAD_DIM), F32)
                for t, (q, do_, l_, dd_, valid) in enumerate(tiles):
                    qs = _stack_heads(q, heads).astype(BF16)
                    dos = _stack_heads(do_, heads).astype(BF16)
                    s = lax.dot_general(qs, kh, _DIMS["nt"], preferred_element_type=F32)
                    p = jnp.exp(jnp.where(valid, s, NEG_INF) - _stack_cols(l_, heads))
                    dp = lax.dot_general(dos, vh, _DIMS["nt"], preferred_element_type=F32)
                    ds = (p * (dp + _stack_cols(dd_, heads))).astype(BF16)
                    dv_acc += lax.dot_general(p.astype(BF16), dos, _DIMS["tn"], preferred_element_type=F32)
                    dk_acc += lax.dot_general(ds, qs, _DIMS["tn"], preferred_element_type=F32)
                    dqs = jnp.dot(ds, kh, preferred_element_type=F32) * scale
                    dq_parts[t] += [dqs[g * QBLOCK:(g + 1) * QBLOCK] for g in range(grp)]
                dks.append(dk_acc)
                dvs.append(dv_acc)
            cat = lambda parts: jnp.concatenate(parts, axis=1) if len(parts) > 1 else parts[0]
            dk_ref[0, rows, :] = _rope(cat(dks), ck, sk_, sign=-1.0)
            dv_ref[0, rows, :] = cat(dvs)
            dq = cat(dq_parts[0])
            if has_next:
                dq = dq + carry_ref[rows, :]
                carry_ref[rows, :] = cat(dq_parts[1])
            dq_ref[0, rows, :] = _rope(dq, ck, sk_, sign=-1.0)

        _for_each_class(r, one_class)

    def at(width, col0, shift):
        return pl.BlockSpec((1, rr, width), lambda b, c, i: (b, jnp.minimum(i + shift, nblk - 1), col0 + c))

    def table(shift):
        return pl.BlockSpec((1, rr, LANES), lambda b, c, i: (b, jnp.minimum(i + shift, nblk - 1), 0))

    in_specs = [at(kw, k_col, 0), at(kw, v_col, 0)]
    args = [k_arr, v_arr]
    for shift in (0, 1) if has_next else (0,):
        in_specs += [at(qw, q_col, shift), at(qw, 0, shift), at(qw, 0, shift), at(qw, 0, shift), table(shift), table(shift)]
        args += [q_arr, do, lse, dd, cos, sin]
    return pl.pallas_call(
        body,
        name=name,
        grid=(bsz, nchunk, nblk),
        in_specs=in_specs,
        out_specs=[pl.BlockSpec((1, rr, qw), lambda b, c, i: (b, i, c)),
                   pl.BlockSpec((1, rr, kw), lambda b, c, i: (b, i, c)),
                   pl.BlockSpec((1, rr, kw), lambda b, c, i: (b, i, c))],
        out_shape=[jax.ShapeDtypeStruct((bsz, seq, nchunk * qw), F32),
                   jax.ShapeDtypeStruct((bsz, seq, nchunk * kw), F32),
                   jax.ShapeDtypeStruct((bsz, seq, nchunk * kw), F32)],
        scratch_shapes=[pltpu.VMEM((rr, qw) if has_next else (8, LANES), F32)],
        compiler_params=_params("parallel", "parallel", "arbitrary"),
    )(*args)


B_CHUNKS = {1: (4, 2), 4: (2, 4), 16: (2, 4)}


def _rope_tables(positions):
    half = HEAD_DIM // 2
    inv = ROPE_THETA ** (-jnp.arange(half, dtype=F32) / half)
    ang = positions.astype(F32)[..., None] * inv
    cos, sin = jnp.cos(ang), jnp.sin(ang)
    return jnp.concatenate([cos] * 4, axis=-1), jnp.concatenate([-sin, sin, -sin, sin], axis=-1)


def _layer_step(x, mod, positions, sinks, ln1_g, ln1_b, ln2_g, ln2_b, target, wint, wba, wbbt, wo, wgut, wd):
    bsz, seq, d = x.shape
    ntok = bsz * seq
    flat = lambda v: v.reshape(ntok, v.shape[-1])
    unflat = lambda v: v.reshape(bsz, seq, v.shape[-1])
    cos, sin = _rope_tables(positions)
    mm = functools.partial(_matmul, tm=1024, tk=1024)

    u1 = _modulate_in(x, mod)
    u1f = flat(u1)
    qa = unflat(mm(u1f, wint, mode="nt", out_dtype=F32, tn=512, name="proj_qa", n=1024, b_off=OFF_QA))
    kva = unflat(mm(u1f, wint, mode="nt", out_dtype=F32, tn=256, name="proj_kva", n=256, b_off=OFF_KVA))
    qkvb = unflat(mm(u1f, wint, mode="nt", out_dtype=F32, tn=256, name="proj_qkvb", n=4608, b_off=OFF_QKVB))
    gab = unflat(mm(u1f, wint, mode="nt", out_dtype=F32, tn=256, name="proj_gab", n=2048, b_off=OFF_GAB))

    grp_a = A_Q_HEADS // A_KV_HEADS
    sink_rows = jnp.repeat(sinks.reshape(A_KV_HEADS, grp_a), QBLOCK, axis=1).reshape(A_KV_HEADS, grp_a * QBLOCK, 1)
    a_kw = dict(hq=A_Q_HEADS, hkv=A_KV_HEADS, q_col=0, k_col=0, v_col=1, nchunk=1, r=1, n_back=A_WINDOW - 1)
    oa, lse_a = _attn_fwd(qa, kva, kva, cos, sin, name="attn_a_fwd", sink_rows=sink_rows, **a_kw)
    ya = unflat(mm(flat(oa), wba, mode="nn", out_dtype=F32, tn=512, name="branch_a"))

    b_kws, os_, ls_ = [], [], []
    for g, (window, r) in enumerate(B_PATTERNS):
        hc, nch = B_CHUNKS[r]
        per = B_HEADS_PER_GROUP // hc
        nsec = len(B_PATTERNS) * per
        kw_ = dict(hq=hc, hkv=hc, q_col=g * per, k_col=nsec + g * per, v_col=2 * nsec + g * per, nchunk=nch, r=r,
                   n_back=window // r)
        b_kws.append(kw_)
        o_g, l_g = _attn_fwd(qkvb, qkvb, qkvb, cos, sin, name=f"attn_b{g}_fwd", **kw_)
        os_.append(o_g)
        ls_.append(l_g)
    ob = _merge_b(os_, ls_)
    yb = unflat(mm(flat(ob), wbbt, mode="nt", out_dtype=F32, tk=512, tn=512, name="branch_b"))
    merged = _gate_merge(gab, ya, yb)
    y1 = unflat(mm(flat(merged), wo, mode="nn", out_dtype=F32, tn=512, name="w_o"))
    h1, u2 = _ln1_fwd(x, y1, mod, ln1_g, ln1_b)
    h = unflat(mm(flat(u2), wgut, mode="nt", out_dtype=F32, tn=512, name="gate_up"))
    a = _silu_mul(h)
    y2 = unflat(mm(flat(a), wd, mode="nn", out_dtype=F32, tk=D_FF, tn=512, name="down"))

    dy2, dh1a, acc2 = _ln2_loss_bwd(h1, y2, mod, ln2_g, ln2_b, target)
    dy2f = flat(dy2)
    da = unflat(mm(dy2f, wd, mode="nt", out_dtype=F32, tn=256, name="down_dgrad"))
    g_wd = _matmul(flat(a), dy2f, mode="tn", out_dtype=BF16, tm=256, tn=1024, tk=ntok, name="down_wgrad")
    dh = _silu_mul_bwd(da, h)
    dhf = flat(dh)
    du2 = unflat(mm(dhf, wgut, mode="nn", out_dtype=F32, tk=D_FF, tn=512, name="gate_up_dgrad"))
    g_wgut = _matmul(dhf, flat(u2), mode="tn", out_dtype=BF16, tm=256, tn=1024, tk=ntok, name="gate_up_wgrad")
    dy1, dxa, acc1 = _ln1_bwd(du2, dh1a, x, y1, mod, ln1_g, ln1_b)
    dy1f = flat(dy1)
    dmerged = unflat(mm(dy1f, wo, mode="nt", out_dtype=F32, tn=512, name="w_o_dgrad"))
    g_wo = _matmul(flat(merged), dy1f, mode="tn", out_dtype=BF16, tm=256, tn=1024, tk=ntok, name="w_o_wgrad")
    dya, dyb, dgab = _gate_merge_bwd(dmerged, gab, ya, yb)
    dyaf, dybf = flat(dya), flat(dyb)
    doa = unflat(mm(dyaf, wba, mode="nt", out_dtype=F32, tn=512, name="branch_a_dgrad"))
    g_wba = _matmul(flat(oa), dyaf, mode="tn", out_dtype=BF16, tm=256, tn=1024, tk=ntok, name="branch_a_wgrad")
    dob = unflat(mm(dybf, wbbt, mode="nn", out_dtype=F32, tn=512, name="branch_b_dgrad"))
    g_wbbt = _matmul(dybf, flat(ob), mode="tn", out_dtype=BF16, tm=256, tn=512, tk=ntok, name="branch_b_wgrad")

    sinks_exp = jnp.repeat(sinks.reshape(1, A_Q_HEADS), HEAD_DIM, axis=1)
    dd_a, acc_s = _delta_a(doa, oa, lse_a, sinks_exp)
    dqa, dka, dva = _attn_bwd(qa, kva, kva, cos, sin, doa, lse_a, dd_a, name="attn_a_bwd", **a_kw)
    merged_bwd = _merge_b_bwd(dob, os_, ls_)
    dqs, dks, dvs = [], [], []
    for g in range(len(B_PATTERNS)):
        dq_g, dk_g, dv_g = _attn_bwd(qkvb, qkvb, qkvb, cos, sin, merged_bwd[g], ls_[g], merged_bwd[3 + g],
                                     name=f"attn_b{g}_bwd", **b_kws[g])
        dqs.append(dq_g)
        dks.append(dk_g)
        dvs.append(dv_g)
    dproj = jnp.concatenate([t.astype(BF16) for t in [dqa, dka, dva] + dqs + dks + dvs] + [dgab], axis=-1)
    dprojf = flat(dproj)
    du1 = unflat(_matmul(dprojf, wint, mode="nn", out_dtype=F32, tm=1024, tn=512, tk=wint.shape[0] // 2, name="w_in_dgrad"))
    g_wint = _matmul(dprojf, u1f, mode="tn", out_dtype=BF16, tm=256, tn=1024, tk=ntok, name="w_in_wgrad")
    grad_x, acc0 = _grad_x(dxa, du1, x, mod)

    loss_part = jnp.sum(acc2[:, 3, 0])
    dmod = jnp.stack([acc0[:, 1], acc0[:, 0], acc1[:, 2], acc1[:, 4], acc1[:, 3], acc2[:, 2]], axis=1)
    small = jnp.stack([acc1[:, 0].sum(0), acc1[:, 1].sum(0), acc2[:, 0].sum(0), acc2[:, 1].sum(0), acc_s[:, 0].sum(0)])
    grads = dict(w_in=g_wint, w_branch_a=g_wba, w_branch_b=g_wbbt, w_o=g_wo, w_gate_up=g_wgut, w_down=g_wd)
    return loss_part, grad_x, dmod, small, grads


def _my_place():
    return lax.axis_index("x"), lax.axis_index("y"), lax.axis_index("c")


def _flip(place, k):
    px, py, pc = place
    return (1 - px if k & 4 else px, 1 - py if k & 2 else py, 1 - pc if k & 1 else pc)


def _index(place):
    return 4 * place[0] + 2 * place[1] + place[2]


def _gather_small(v, name):
    rows, cols = v.shape

    def body(v_ref, out_ref, send_sems, recv_sems):
        me = _my_place()
        out_ref[_index(me)] = v_ref[...]
        copies = []
        for k in range(1, N_DEV):
            copies.append(pltpu.make_async_remote_copy(
                src_ref=v_ref, dst_ref=out_ref.at[_index(me)], send_sem=send_sems.at[k - 1], recv_sem=recv_sems.at[k - 1],
                device_id=_flip(me, k), device_id_type=MESH))
        for cp in copies:
            cp.start()
        for k in range(1, N_DEV):
            pltpu.make_async_remote_copy(
                src_ref=v_ref, dst_ref=out_ref.at[_index(_flip(me, k))], send_sem=send_sems.at[k - 1],
                recv_sem=recv_sems.at[k - 1], device_id=_flip(me, k), device_id_type=MESH).wait_recv()
        for cp in copies:
            cp.wait_send()

    return pl.pallas_call(
        body,
        name=name,
        out_shape=jax.ShapeDtypeStruct((N_DEV, rows, cols), v.dtype),
        in_specs=[pl.BlockSpec(memory_space=pltpu.VMEM)],
        out_specs=pl.BlockSpec(memory_space=pltpu.VMEM),
        scratch_shapes=[pltpu.SemaphoreType.DMA((N_DEV - 1,)), pltpu.SemaphoreType.DMA((N_DEV - 1,))],
        compiler_params=pltpu.CompilerParams(vmem_limit_bytes=VMEM_LIMIT_BYTES),
    )(v)


def _gather_weights(packed):
    rows, cols = packed.shape

    def body(v_ref, out_ref, send_sems, recv_sems, local_sem):
        me = _my_place()
        sibling = _flip(me, 1)
        chips = [2, 4, 6]

        def slot(place):
            return out_ref.at[_index(place)]

        def copy(sem, block, to, src=None):
            return pltpu.make_async_remote_copy(
                src_ref=slot(block) if src is None else src, dst_ref=slot(block), send_sem=send_sems.at[sem],
                recv_sem=recv_sems.at[sem], device_id=to, device_id_type=MESH)

        mine = pltpu.make_async_copy(v_ref, slot(me), local_sem)
        mine.start()
        first = [copy(0, me, sibling, src=v_ref)] + [copy(1 + j, me, _flip(me, k), src=v_ref) for j, k in enumerate(chips)]
        for cp in first:
            cp.start()
        passed = [copy(4 + j, _flip(me, k), sibling) for j, k in enumerate(chips)]
        for j, k in enumerate(chips):
            copy(1 + j, _flip(me, k), me).wait_recv()
            passed[j].start()
        copy(0, sibling, me).wait_recv()
        for j, k in enumerate(chips):
            copy(4 + j, _flip(sibling, k), me).wait_recv()
        for cp in first + passed:
            cp.wait_send()
        mine.wait()

    return pl.pallas_call(
        body,
        name="gather_weights",
        out_shape=jax.ShapeDtypeStruct((N_DEV, rows, cols), packed.dtype),
        in_specs=[pl.BlockSpec(memory_space=pltpu.HBM)],
        out_specs=pl.BlockSpec(memory_space=pltpu.HBM),
        scratch_shapes=[pltpu.SemaphoreType.DMA((7,)), pltpu.SemaphoreType.DMA((7,)), pltpu.SemaphoreType.DMA],
    )(packed)


def _exchange_sibling(parts):
    _, nchip, rows, cols = parts.shape

    def body(p_ref, out_ref, send_sem, recv_sem):
        me = _my_place()
        cp = pltpu.make_async_remote_copy(src_ref=p_ref.at[1 - me[2]], dst_ref=out_ref, send_sem=send_sem, recv_sem=recv_sem,
                                          device_id=_flip(me, 1), device_id_type=MESH)
        cp.start()
        cp.wait()

    return pl.pallas_call(
        body,
        name="grad_exchange_sibling",
        out_shape=jax.ShapeDtypeStruct((nchip, rows, cols), parts.dtype),
        in_specs=[pl.BlockSpec(memory_space=pltpu.HBM)],
        out_specs=pl.BlockSpec(memory_space=pltpu.HBM),
        scratch_shapes=[pltpu.SemaphoreType.DMA, pltpu.SemaphoreType.DMA],
    )(parts)


def _exchange_chips(parts):
    _, rows, cols = parts.shape

    def body(p_ref, out_ref, send_sems, recv_sems):
        me = _my_place()
        copies = []
        for j, k in enumerate((2, 4, 6)):
            to = _flip(me, k)
            copies.append(pltpu.make_async_remote_copy(
                src_ref=p_ref.at[2 * to[0] + to[1]], dst_ref=out_ref.at[j], send_sem=send_sems.at[j], recv_sem=recv_sems.at[j],
                device_id=to, device_id_type=MESH))
        for cp in copies:
            cp.start()
        for cp in copies:
            cp.wait()

    return pl.pallas_call(
        body,
        name="grad_exchange_chips",
        out_shape=jax.ShapeDtypeStruct((3, rows, cols), parts.dtype),
        in_specs=[pl.BlockSpec(memory_space=pltpu.HBM)],
        out_specs=pl.BlockSpec(memory_space=pltpu.HBM),
        scratch_shapes=[pltpu.SemaphoreType.DMA((3,)), pltpu.SemaphoreType.DMA((3,))],
    )(parts)


SUM_TILE = 592


def _sum_pairs(mine, theirs):
    nchip, rows, cols = mine.shape
    spec = pl.BlockSpec((1, SUM_TILE, cols), lambda q, t: (q, t, 0))

    def body(a_ref, b_ref, o_ref):
        o_ref[...] = (a_ref[...].astype(F32) + b_ref[...].astype(F32)).astype(BF16)

    return pl.pallas_call(body, name="grad_sum_sibling", grid=(nchip, rows // SUM_TILE), in_specs=[spec, spec], out_specs=spec,
                          out_shape=jax.ShapeDtypeStruct(mine.shape, BF16), compiler_params=_params("parallel", "parallel"))(mine, theirs)


def _sum_final(own, got):
    rows, cols = own.shape

    def body(a_ref, g_ref, o_ref):
        o_ref[...] = ((a_ref[...].astype(F32) + g_ref[0].astype(F32)) + g_ref[1].astype(F32)) + g_ref[2].astype(F32)

    return pl.pallas_call(
        body, name="grad_sum_chips", grid=(rows // SUM_TILE,),
        in_specs=[pl.BlockSpec((SUM_TILE, cols), lambda t: (t, 0)), pl.BlockSpec((3, SUM_TILE, cols), lambda t: (0, t, 0))],
        out_specs=pl.BlockSpec((SUM_TILE, cols), lambda t: (t, 0)),
        out_shape=jax.ShapeDtypeStruct((rows, cols), F32), compiler_params=_params("parallel"))(own, got)


def _ada_fwd(c_all, w, b):
    nb, _ = c_all.shape
    ncol = w.shape[1]

    def body(c_ref, w_ref, b_ref, o_ref):
        c = c_ref[...]
        act = (c * _sigmoid(c)).astype(BF16)
        o_ref[...] = jnp.dot(act, w_ref[...].astype(BF16), preferred_element_type=F32) + b_ref[...]

    return pl.pallas_call(body, name="ada_fwd", out_shape=jax.ShapeDtypeStruct((nb, ncol), F32),
                          compiler_params=pltpu.CompilerParams(vmem_limit_bytes=VMEM_LIMIT_BYTES))(c_all, w, b)


def _ada_wgrad(c_all_t, dmod_cols):
    d, nb = c_all_t.shape
    ncol = dmod_cols.shape[1]

    def body(ct_ref, dm_ref, o_ref):
        ct = ct_ref[...]
        act = (ct * _sigmoid(ct)).astype(BF16).astype(F32)
        dm = dm_ref[...].astype(BF16).astype(F32)
        acc = act[:, 0:1] * dm[0:1, :]
        for i in range(1, nb):
            acc = acc + act[:, i:i + 1] * dm[i:i + 1, :]
        o_ref[...] = acc

    return pl.pallas_call(body, name="ada_wgrad", out_shape=jax.ShapeDtypeStruct((d, ncol), F32),
                          compiler_params=pltpu.CompilerParams(vmem_limit_bytes=VMEM_LIMIT_BYTES))(c_all_t, dmod_cols)


SMALL_ROWS = 24


def _reduce_small(gathered):
    def body(g_ref, o_ref):
        acc = g_ref[0]
        for dev in range(1, N_DEV):
            acc = acc + g_ref[dev]
        o_ref[...] = acc

    return pl.pallas_call(body, name="reduce_small", out_shape=jax.ShapeDtypeStruct(gathered.shape[1:], F32))(gathered)


def _adamw(w, g, m, v, name):
    rows, cols = w.shape
    tile = rows
    for cand in (256, 128, 64, 32, 16, 8):
        if rows % cand == 0 and rows > cand:
            tile = cand
            break
    spec = pl.BlockSpec((tile, cols), lambda t: (t, 0))
    bc1 = 1.0 - ADAM_B1 ** ADAM_STEP
    bc2 = 1.0 - ADAM_B2 ** ADAM_STEP

    def body(w_ref, g_ref, m_ref, v_ref, d_ref, nm_ref, nv_ref):
        g_ = g_ref[...]
        nm = ADAM_B1 * m_ref[...] + (1.0 - ADAM_B1) * g_
        nv = ADAM_B2 * v_ref[...] + (1.0 - ADAM_B2) * (g_ * g_)
        d_ref[...] = -ADAM_LR * ((nm / bc1) / (jnp.sqrt(nv / bc2) + ADAM_EPS) + ADAM_WD * w_ref[...])
        nm_ref[...] = nm
        nv_ref[...] = nv

    shp = jax.ShapeDtypeStruct((rows, cols), F32)
    return pl.pallas_call(body, name=name, grid=(rows // tile,), in_specs=[spec] * 4, out_specs=[spec] * 3, out_shape=[shp] * 3,
                          compiler_params=_params("parallel"))(w, g, m, v)


_WEIGHTS = ("w_ada", "b_ada", "w_in", "sinks", "w_branch_a", "w_branch_b", "w_o", "ln1_g", "ln1_b", "w_gate_up", "w_down",
            "ln2_g", "ln2_b")
_TRANSPOSED = ("w_in", "w_branch_b", "w_gate_up")


def _pack_shard(name, w):
    w = w.astype(BF16)
    if name in _TRANSPOSED:
        w = w.T
    return w.reshape(-1, D_MODEL)


def _unpack_full(name, slab):
    if name == "w_branch_b":
        return slab.reshape(N_DEV * 128, 512)
    return slab.reshape(-1, D_MODEL)


def kernel(x, c, positions, w_ada, b_ada, w_in, sinks, w_branch_a, w_branch_b, w_o, ln1_g, ln1_b, w_gate_up, w_down, ln2_g, ln2_b, loss_target, m_w_ada, m_b_ada, m_w_in, m_sinks, m_w_branch_a, m_w_branch_b, m_w_o, m_ln1_g, m_ln1_b, m_w_gate_up, m_w_down, m_ln2_g, m_ln2_b, v_w_ada, v_b_ada, v_w_in, v_sinks, v_w_branch_a, v_w_branch_b, v_w_o, v_ln1_g, v_ln1_b, v_w_gate_up, v_w_down, v_ln2_g, v_ln2_b):
    weights = dict(w_ada=w_ada, b_ada=b_ada, w_in=w_in, sinks=sinks, w_branch_a=w_branch_a, w_branch_b=w_branch_b, w_o=w_o,
                   ln1_g=ln1_g, ln1_b=ln1_b, w_gate_up=w_gate_up, w_down=w_down, ln2_g=ln2_g, ln2_b=ln2_b)
    m_in = dict(w_ada=m_w_ada, b_ada=m_b_ada, w_in=m_w_in, sinks=m_sinks, w_branch_a=m_w_branch_a, w_branch_b=m_w_branch_b,
                w_o=m_w_o, ln1_g=m_ln1_g, ln1_b=m_ln1_b, w_gate_up=m_w_gate_up, w_down=m_w_down, ln2_g=m_ln2_g, ln2_b=m_ln2_b)
    v_in = dict(w_ada=v_w_ada, b_ada=v_b_ada, w_in=v_w_in, sinks=v_sinks, w_branch_a=v_w_branch_a, w_branch_b=v_w_branch_b,
                w_o=v_w_o, ln1_g=v_ln1_g, ln1_b=v_ln1_b, w_gate_up=v_w_gate_up, w_down=v_w_down, ln2_g=v_ln2_g, ln2_b=v_ln2_b)
    bsz = x.shape[0]
    me = _index(_my_place())
    ada_cols = w_ada.shape[2]

    c_all = _gather_small(jnp.pad(c, ((0, 8 - bsz), (0, 0))), "gather_c")[:, :bsz].reshape(N_DEV * bsz, D_MODEL)
    b_cols = lax.dynamic_slice_in_dim(b_ada, me * ada_cols, ada_cols, axis=1)
    mod_cols = _ada_fwd(c_all, w_ada[0], b_cols)
    mod_all = _gather_small(mod_cols, "gather_mod").transpose(1, 0, 2).reshape(N_DEV * bsz, 6, D_MODEL)
    mod = jnp.pad(lax.dynamic_slice_in_dim(mod_all, me * bsz, bsz, axis=0), ((0, 0), (0, 2), (0, 0)))

    packed = jnp.concatenate([_pack_shard(n, weights[n][0]) for n, _ in PACK_ROWS], axis=0)
    gathered = _gather_weights(packed)
    full, off = {}, 0
    for n, r in PACK_ROWS:
        full[n] = _unpack_full(n, gathered[:, off:off + r])
        off += r

    loss_part, grad_x, dmod, small, grads = _layer_step(
        x, mod, positions, sinks[0], ln1_g, ln1_b, ln2_g, ln2_b, loss_target, full["w_in"], full["w_branch_a"],
        full["w_branch_b"], full["w_o"], full["w_gate_up"], full["w_down"])
    loss = lax.psum(loss_part, MESH_AXES)

    slabs = jnp.concatenate([grads[n].reshape(N_DEV, r, D_MODEL) for n, r in PACK_ROWS], axis=1)
    parts = slabs.reshape(4, 2, PACK_TOTAL, D_MODEL).transpose(1, 0, 2, 3)
    my_c = lax.axis_index("c")
    my_chip = 2 * lax.axis_index("x") + lax.axis_index("y")
    from_sibling = _exchange_sibling(parts)
    chip_sum = _sum_pairs(lax.dynamic_index_in_dim(parts, my_c, 0, keepdims=False), from_sibling)
    from_chips = _exchange_chips(chip_sum)
    g_packed = _sum_final(lax.dynamic_index_in_dim(chip_sum, my_chip, 0, keepdims=False), from_chips)
    g_w, off = {}, 0
    for n, r in PACK_ROWS:
        part = g_packed[off:off + r]
        off += r
        if n == "w_branch_b":
            part = part.reshape(128, 512)
        g_w[n] = part.T if n in _TRANSPOSED else part

    rows = jnp.concatenate([dmod.reshape(bsz * 6, D_MODEL), small, jnp.zeros((SMALL_ROWS - bsz * 6 - 5, D_MODEL), F32)], axis=0)
    small_all = _gather_small(rows, "gather_small")
    sums = _reduce_small(small_all)
    dmod_all = small_all[:, :bsz * 6].reshape(N_DEV * bsz, 6 * D_MODEL)
    g_w["b_ada"] = functools.reduce(jnp.add, [sums[6 * i:6 * i + 6] for i in range(bsz)]).reshape(1, 6 * D_MODEL)
    g_w["ln1_g"], g_w["ln1_b"], g_w["ln2_g"], g_w["ln2_b"] = (sums[12 + i][None] for i in range(4))
    g_w["sinks"] = sums[16][::HEAD_DIM][None]
    dmod_cols = lax.dynamic_slice_in_dim(dmod_all, me * ada_cols, ada_cols, axis=1)
    g_w["w_ada"] = _ada_wgrad(c_all.T, dmod_cols)

    out_g, out_d, out_m, out_v = [], [], [], []
    for n in _WEIGHTS:
        w2, m2, v2 = (t[n][0] if t[n].ndim == 3 else t[n] for t in (weights, m_in, v_in))
        shape = weights[n].shape
        dlt, nm, nv = _adamw(w2, g_w[n], m2, v2, "adamw_" + n)
        out_g.append(g_w[n].reshape(shape))
        out_d.append(dlt.reshape(shape))
        out_m.append(nm.reshape(shape))
        out_v.append(nv.reshape(shape))
    return (loss, grad_x, *out_g, *out_d, *out_m, *out_v)
```

```python
import functools

import jax
import jax.numpy as jnp
from jax import lax
from jax.experimental import pallas as pl
from jax.experimental.pallas import tpu as pltpu

F32 = jnp.float32
BF16 = jnp.bfloat16

D_MODEL = 1024
HEAD_DIM = 64
A_Q_HEADS = 16
A_KV_HEADS = 2
A_WINDOW = 128
B_PATTERNS = ((128, 1), (512, 4), (2048, 16))
B_HEADS_PER_GROUP = 8
D_FF = 2816
QBLOCK = 128
ROPE_THETA = 10000.0
LN_EPS = 1e-5
DEEPNORM_ALPHA = 2.0 ** 0.25
NEG_INF = -1e30
ADAM_LR, ADAM_B1, ADAM_B2, ADAM_EPS, ADAM_WD, ADAM_STEP = 0.001, 0.9, 0.999, 1e-08, 0.01, 10

N_DEV = 8
MESH_AXES = ("x", "y", "c")
LANES = 128
VMEM_LIMIT_BYTES = 56 * 1024 * 1024
MESH = pl.DeviceIdType.MESH

OFF_QA, OFF_KVA, OFF_QKVB, OFF_GAB = 0, 1024, 1280, 5888
PACK_ROWS = (("w_in", 992), ("w_branch_a", 128), ("w_branch_b", 64), ("w_o", 128), ("w_gate_up", 704), ("w_down", 352))
PACK_TOTAL = sum(r for _, r in PACK_ROWS)


def _params(*sem):
    return pltpu.CompilerParams(dimension_semantics=sem, vmem_limit_bytes=VMEM_LIMIT_BYTES)


def _sigmoid(x):
    return 1.0 / (1.0 + jnp.exp(-x))


_DIMS = {"nn": (((1,), (0,)), ((), ())), "nt": (((1,), (1,)), ((), ())), "tn": (((0,), (0,)), ((), ()))}


def _matmul(a, b, *, mode, out_dtype, tm, tn, tk, name, n=None, b_off=0):
    if mode == "nn":
        (m, k), nn_ = a.shape, b.shape[1]
    elif mode == "nt":
        (m, k), nn_ = a.shape, (b.shape[0] if n is None else n)
    else:
        (k, m), nn_ = a.shape, b.shape[1]
    assert m % tm == 0 and nn_ % tn == 0 and k % tk == 0 and b_off % tn == 0, (name, m, nn_, k)
    nk = k // tk
    joff = b_off // tn
    if mode == "nn":
        a_spec = pl.BlockSpec((tm, tk), lambda i, j, kk: (i, kk))
        b_spec = pl.BlockSpec((tk, tn), lambda i, j, kk: (kk, j))
    elif mode == "nt":
        a_spec = pl.BlockSpec((tm, tk), lambda i, j, kk: (i, kk))
        b_spec = pl.BlockSpec((tn, tk), lambda i, j, kk: (j + joff, kk))
    else:
        a_spec = pl.BlockSpec((tk, tm), lambda i, j, kk: (kk, i))
        b_spec = pl.BlockSpec((tk, tn), lambda i, j, kk: (kk, j))
    dims = _DIMS[mode]

    def body(a_ref, b_ref, o_ref, acc_ref):
        kk = pl.program_id(2)
        part = lax.dot_general(a_ref[...].astype(BF16), b_ref[...].astype(BF16), dims, preferred_element_type=F32)
        if nk == 1:
            o_ref[...] = part.astype(o_ref.dtype)
        else:
            @pl.when(kk == 0)
            def _():
                acc_ref[...] = part

            @pl.when(kk > 0)
            def _():
                acc_ref[...] += part

            @pl.when(kk == nk - 1)
            def _():
                o_ref[...] = acc_ref[...].astype(o_ref.dtype)

    return pl.pallas_call(
        body,
        name=name,
        grid=(m // tm, nn_ // tn, nk),
        in_specs=[a_spec, b_spec],
        out_specs=pl.BlockSpec((tm, tn), lambda i, j, kk: (i, j)),
        out_shape=jax.ShapeDtypeStruct((m, nn_), out_dtype),
        scratch_shapes=[pltpu.VMEM((tm, tn) if nk > 1 else (8, LANES), F32)],
        compiler_params=_params("parallel", "parallel", "arbitrary"),
    )(a, b)


def _proj_rope(a, bt, cos, sin, *, n, b_off, rope_cols, tm, tn, name):
    m, k = a.shape
    assert m % tm == 0 and n % tn == 0 and b_off % tn == 0 and rope_cols % tn == 0, name
    joff = b_off // tn
    nrope = rope_cols // tn

    def body(a_ref, b_ref, c_ref, s_ref, o_ref):
        acc = lax.dot_general(a_ref[...], b_ref[...], _DIMS["nt"], preferred_element_type=F32)
        j = pl.program_id(1)

        @pl.when(j < nrope)
        def _():
            o_ref[...] = _rope(acc, c_ref[...], s_ref[...])

        @pl.when(j >= nrope)
        def _():
            o_ref[...] = acc

    table = pl.BlockSpec((tm, LANES), lambda i, j: (i, 0))
    return pl.pallas_call(
        body,
        name=name,
        grid=(m // tm, n // tn),
        in_specs=[pl.BlockSpec((tm, k), lambda i, j: (i, 0)), pl.BlockSpec((tn, k), lambda i, j: (j + joff, 0)), table, table],
        out_specs=pl.BlockSpec((tm, tn), lambda i, j: (i, j)),
        out_shape=jax.ShapeDtypeStruct((m, n), F32),
        compiler_params=_params("parallel", "parallel"),
    )(a, bt, cos, sin)


ROW_TILE = 256


def _rows(width, col=0):
    return pl.BlockSpec((1, ROW_TILE, width), lambda b, t: (b, t, col))


def _per_batch(nrows, width):
    return pl.BlockSpec((1, nrows, width), lambda b, t: (b, 0, 0))


def _whole(shape):
    return pl.BlockSpec(shape, lambda b, t: (0,) * len(shape))


def _row_call(body, name, bsz, seq, in_specs, out_specs, out_shape, accumulates=False):
    return pl.pallas_call(
        body,
        name=name,
        grid=(bsz, seq // ROW_TILE),
        in_specs=in_specs,
        out_specs=out_specs,
        out_shape=out_shape,
        compiler_params=_params("parallel", "arbitrary" if accumulates else "parallel"),
    )


def _acc_rows(acc_ref, first, rows):
    @pl.when(first)
    def _():
        acc_ref[...] = jnp.zeros_like(acc_ref)

    for r, val in enumerate(rows):
        acc_ref[0, r:r + 1, :] += val


def _colsum(v):
    return jnp.sum(v, axis=0, keepdims=True)


def _ln_stats(z):
    mu = jnp.mean(z, axis=-1, keepdims=True)
    zc = z - mu
    var = jnp.mean(zc * zc, axis=-1, keepdims=True)
    rstd = lax.rsqrt(var + LN_EPS)
    return zc * rstd, rstd


def _ln_bwd(dxhat, xhat, rstd):
    m1 = jnp.mean(dxhat, axis=-1, keepdims=True)
    m2 = jnp.mean(dxhat * xhat, axis=-1, keepdims=True)
    return rstd * (dxhat - m1 - xhat * m2)


def _modulate_in(x, mod):
    bsz, seq, d = x.shape

    def body(x_ref, mod_ref, u_ref):
        u_ref[0] = (x_ref[0] * (1.0 + mod_ref[0, 1:2, :]) + mod_ref[0, 0:1, :]).astype(BF16)

    return _row_call(body, "modulate_in", bsz, seq, [_rows(d), _per_batch(8, d)], _rows(d),
                     jax.ShapeDtypeStruct((bsz, seq, d), BF16))(x, mod)


def _gate_merge(gab, ya, yb):
    bsz, seq, d = ya.shape

    def body(ga_ref, gb_ref, ya_ref, yb_ref, o_ref):
        o_ref[0] = (_sigmoid(ga_ref[0]) * ya_ref[0] + _sigmoid(gb_ref[0]) * yb_ref[0]).astype(BF16)

    return _row_call(body, "gate_merge", bsz, seq, [_rows(d, 0), _rows(d, 1), _rows(d), _rows(d)], _rows(d),
                     jax.ShapeDtypeStruct((bsz, seq, d), BF16))(gab, gab, ya, yb)


def _ln1_fwd(x, y1, mod, g, b):
    bsz, seq, d = x.shape

    def body(x_ref, y_ref, mod_ref, g_ref, b_ref, h_ref, u_ref):
        z = DEEPNORM_ALPHA * x_ref[0] + (1.0 + mod_ref[0, 2:3, :]) * y_ref[0]
        xhat, _ = _ln_stats(z)
        h = xhat * g_ref[...] + b_ref[...]
        h_ref[0] = h
        u_ref[0] = (h * (1.0 + mod_ref[0, 4:5, :]) + mod_ref[0, 3:4, :]).astype(BF16)

    return _row_call(body, "ln1_fwd", bsz, seq,
                     [_rows(d), _rows(d), _per_batch(8, d), _whole((1, d)), _whole((1, d))],
                     [_rows(d), _rows(d)],
                     [jax.ShapeDtypeStruct((bsz, seq, d), F32), jax.ShapeDtypeStruct((bsz, seq, d), BF16)])(x, y1, mod, g, b)


def _silu_mul(h):
    bsz, seq, _ = h.shape

    def body(hg_ref, hu_ref, a_ref):
        hg = hg_ref[0]
        a_ref[0] = (hg * _sigmoid(hg) * hu_ref[0]).astype(BF16)

    return _row_call(body, "silu_mul", bsz, seq, [_rows(D_FF, 0), _rows(D_FF, 1)], _rows(D_FF),
                     jax.ShapeDtypeStruct((bsz, seq, D_FF), BF16))(h, h)


def _ln2_loss_bwd(h1, y2, mod, g, b, target):
    bsz, seq, d = h1.shape

    def body(h_ref, y_ref, mod_ref, g_ref, b_ref, t_ref, dy_ref, dh_ref, acc_ref):
        y = y_ref[0]
        gate = 1.0 + mod_ref[0, 5:6, :]
        z = DEEPNORM_ALPHA * h_ref[0] + gate * y
        xhat, rstd = _ln_stats(z)
        diff = xhat * g_ref[...] + b_ref[...] - t_ref[0]
        loss = 0.5 * jnp.sum(jnp.sum(diff * diff, axis=-1, keepdims=True) / d, axis=0, keepdims=True)
        dout = diff / d
        dz = _ln_bwd(dout * g_ref[...], xhat, rstd)
        dy_ref[0] = (gate * dz).astype(BF16)
        dh_ref[0] = DEEPNORM_ALPHA * dz
        _acc_rows(acc_ref, pl.program_id(1) == 0,
                  [_colsum(dout * xhat), _colsum(dout), _colsum(dz * y), jnp.broadcast_to(loss, (1, d))])

    return _row_call(body, "ln2_loss_bwd", bsz, seq,
                     [_rows(d), _rows(d), _per_batch(8, d), _whole((1, d)), _whole((1, d)), _rows(d)],
                     [_rows(d), _rows(d), _per_batch(8, d)],
                     [jax.ShapeDtypeStruct((bsz, seq, d), BF16), jax.ShapeDtypeStruct((bsz, seq, d), F32),
                      jax.ShapeDtypeStruct((bsz, 8, d), F32)], accumulates=True)(h1, y2, mod, g, b, target)


def _silu_mul_bwd(da, h):
    bsz, seq, _ = h.shape

    def body(da_ref, hg_ref, hu_ref, dh_ref):
        hg, da_ = hg_ref[0], da_ref[0]
        sg = _sigmoid(hg)
        dh_ref[0, :, :D_FF] = (da_ * hu_ref[0] * (sg * (1.0 + hg * (1.0 - sg)))).astype(BF16)
        dh_ref[0, :, D_FF:] = (da_ * (hg * sg)).astype(BF16)

    return _row_call(body, "silu_mul_bwd", bsz, seq, [_rows(D_FF), _rows(D_FF, 0), _rows(D_FF, 1)], _rows(2 * D_FF),
                     jax.ShapeDtypeStruct((bsz, seq, 2 * D_FF), BF16))(da, h, h)


def _ln1_bwd(du2, dh1a, x, y1, mod, g, b):
    bsz, seq, d = x.shape

    def body(du_ref, dh_ref, x_ref, y_ref, mod_ref, g_ref, b_ref, dy_ref, dx_ref, acc_ref):
        y, du = y_ref[0], du_ref[0]
        gate = 1.0 + mod_ref[0, 2:3, :]
        z = DEEPNORM_ALPHA * x_ref[0] + gate * y
        xhat, rstd = _ln_stats(z)
        h1 = xhat * g_ref[...] + b_ref[...]
        dh1 = dh_ref[0] + du * (1.0 + mod_ref[0, 4:5, :])
        dz = _ln_bwd(dh1 * g_ref[...], xhat, rstd)
        dy_ref[0] = (gate * dz).astype(BF16)
        dx_ref[0] = DEEPNORM_ALPHA * dz
        _acc_rows(acc_ref, pl.program_id(1) == 0,
                  [_colsum(dh1 * xhat), _colsum(dh1), _colsum(dz * y), _colsum(du * h1), _colsum(du)])

    return _row_call(body, "ln1_bwd", bsz, seq,
                     [_rows(d), _rows(d), _rows(d), _rows(d), _per_batch(8, d), _whole((1, d)), _whole((1, d))],
                     [_rows(d), _rows(d), _per_batch(8, d)],
                     [jax.ShapeDtypeStruct((bsz, seq, d), BF16), jax.ShapeDtypeStruct((bsz, seq, d), F32),
                      jax.ShapeDtypeStruct((bsz, 8, d), F32)], accumulates=True)(du2, dh1a, x, y1, mod, g, b)


def _gate_merge_bwd(dm, gab, ya, yb):
    bsz, seq, d = ya.shape

    def body(dm_ref, ga_ref, gb_ref, ya_ref, yb_ref, dya_ref, dyb_ref, dg_ref):
        dm_ = dm_ref[0]
        sa, sb = _sigmoid(ga_ref[0]), _sigmoid(gb_ref[0])
        dya_ref[0] = (dm_ * sa).astype(BF16)
        dyb_ref[0] = (dm_ * sb).astype(BF16)
        dg_ref[0, :, :d] = (dm_ * ya_ref[0] * sa * (1.0 - sa)).astype(BF16)
        dg_ref[0, :, d:] = (dm_ * yb_ref[0] * sb * (1.0 - sb)).astype(BF16)

    return _row_call(body, "gate_merge_bwd", bsz, seq,
                     [_rows(d), _rows(d, 0), _rows(d, 1), _rows(d), _rows(d)],
                     [_rows(d), _rows(d), _rows(2 * d)],
                     [jax.ShapeDtypeStruct((bsz, seq, d), BF16), jax.ShapeDtypeStruct((bsz, seq, d), BF16),
                      jax.ShapeDtypeStruct((bsz, seq, 2 * d), BF16)])(dm, gab, gab, ya, yb)


def _grad_x(dxa, du1, x, mod):
    bsz, seq, d = x.shape

    def body(dxa_ref, du_ref, x_ref, mod_ref, gx_ref, acc_ref):
        du = du_ref[0]
        gx_ref[0] = dxa_ref[0] + du * (1.0 + mod_ref[0, 1:2, :])
        _acc_rows(acc_ref, pl.program_id(1) == 0, [_colsum(du * x_ref[0]), _colsum(du)])

    return _row_call(body, "grad_x", bsz, seq, [_rows(d), _rows(d), _rows(d), _per_batch(8, d)],
                     [_rows(d), _per_batch(8, d)],
                     [jax.ShapeDtypeStruct((bsz, seq, d), F32), jax.ShapeDtypeStruct((bsz, 8, d), F32)],
                     accumulates=True)(dxa, du1, x, mod)


def _segsum64(v):
    rows, width = v.shape
    ri = lax.broadcasted_iota(jnp.int32, (LANES, LANES), 0) // HEAD_DIM
    ci = lax.broadcasted_iota(jnp.int32, (LANES, LANES), 1) // HEAD_DIM
    ones = jnp.where(ri == ci, 1.0, 0.0).astype(BF16)
    out = []
    for c in range(width // LANES):
        part = v[:, c * LANES:(c + 1) * LANES]
        hi = part.astype(BF16)
        lo = (part - hi.astype(F32)).astype(BF16)
        out.append(jnp.dot(hi, ones, preferred_element_type=F32) + jnp.dot(lo, ones, preferred_element_type=F32))
    return jnp.concatenate(out, axis=1) if len(out) > 1 else out[0]


def _merge_b(os_, ls_):
    bsz, seq, w = os_[0].shape

    def body(o0, o1, o2, l0, l1, l2, ob_ref):
        ls = [l0[0], l1[0], l2[0]]
        mx = jnp.maximum(jnp.maximum(ls[0], ls[1]), ls[2])
        es = [jnp.exp(l - mx) for l in ls]
        den = es[0] + es[1] + es[2]
        ob_ref[0] = ((es[0] / den) * o0[0] + (es[1] / den) * o1[0] + (es[2] / den) * o2[0]).astype(BF16)

    return _row_call(body, "merge_b", bsz, seq, [_rows(w)] * 6, _rows(w),
                     jax.ShapeDtypeStruct((bsz, seq, w), BF16))(*os_, *ls_)


def _merge_b_bwd(dob, os_, ls_):
    bsz, seq, w = os_[0].shape

    def body(dob_ref, o0, o1, o2, l0, l1, l2, do0, do1, do2, dd0, dd1, dd2):
        dob_ = dob_ref[0]
        ls = [l0[0], l1[0], l2[0]]
        mx = jnp.maximum(jnp.maximum(ls[0], ls[1]), ls[2])
        es = [jnp.exp(l - mx) for l in ls]
        den = es[0] + es[1] + es[2]
        ws = [e / den for e in es]
        dws = [_segsum64(dob_ * o[0]) for o in (o0, o1, o2)]
        mean = ws[0] * dws[0] + ws[1] * dws[1] + ws[2] * dws[2]
        for wg, dwg, do_ref, dd_ref in zip(ws, dws, (do0, do1, do2), (dd0, dd1, dd2)):
            do_ref[0] = wg * dob_
            dd_ref[0] = -wg * mean

    shp = jax.ShapeDtypeStruct((bsz, seq, w), F32)
    return _row_call(body, "merge_b_bwd", bsz, seq, [_rows(w)] * 7, [_rows(w)] * 6, [shp] * 6)(dob, *os_, *ls_)


def _delta_a(doa, oa, lse_a, sinks_exp):
    bsz, seq, w = oa.shape

    def body(do_ref, o_ref, l_ref, s_ref, dd_ref, acc_ref):
        dd = -_segsum64(do_ref[0] * o_ref[0])
        dd_ref[0] = dd
        _acc_rows(acc_ref, pl.program_id(1) == 0, [_colsum(dd * jnp.exp(s_ref[...] - l_ref[0]))])

    return _row_call(body, "delta_a", bsz, seq, [_rows(w), _rows(w), _rows(w), _whole((1, w))],
                     [_rows(w), _per_batch(8, w)],
                     [jax.ShapeDtypeStruct((bsz, seq, w), F32), jax.ShapeDtypeStruct((bsz, 8, w), F32)],
                     accumulates=True)(doa, oa, lse_a, sinks_exp)


def _swap_halves(v):
    lane = lax.broadcasted_iota(jnp.int32, v.shape, 1)
    return jnp.where((lane % HEAD_DIM) < HEAD_DIM // 2, pltpu.roll(v, LANES - HEAD_DIM // 2, 1),
                     pltpu.roll(v, HEAD_DIM // 2, 1))


def _rope(v, cos, sin, sign=1.0):
    out = []
    for c in range(v.shape[1] // LANES):
        part = v[:, c * LANES:(c + 1) * LANES]
        out.append(part * cos + sign * (_swap_halves(part) * sin))
    return jnp.concatenate(out, axis=1) if len(out) > 1 else out[0]


def _stack_heads(v, heads):
    return jnp.concatenate([v[:, h * HEAD_DIM:(h + 1) * HEAD_DIM] for h in heads], axis=0) if len(heads) > 1 else \
        v[:, heads[0] * HEAD_DIM:(heads[0] + 1) * HEAD_DIM]


def _stack_cols(v, heads):
    return jnp.concatenate([v[:, h * HEAD_DIM:h * HEAD_DIM + 1] for h in heads], axis=0) if len(heads) > 1 else \
        v[:, heads[0] * HEAD_DIM:heads[0] * HEAD_DIM + 1]


def _for_each_class(r, fn):
    if r == 1:
        fn(pl.ds(0, QBLOCK))
    else:
        def step(rho, carry):
            fn(pl.ds(rho, QBLOCK, stride=r))
            return carry

        lax.fori_loop(0, r, step, 0, unroll=4)


def _attn_fwd(q_arr, k_arr, v_arr, cos, sin, *, name, hq, hkv, q_col, k_col, v_col, nchunk, r, n_back, sink_rows=None):
    bsz, seq, _ = q_arr.shape
    rr = QBLOCK * r
    nblk = seq // rr
    grp = hq // hkv
    qw, kw = hq * HEAD_DIM, hkv * HEAD_DIM
    has_prev = nblk > 1
    has_sink = sink_rows is not None

    def body(*refs):
        refs = list(refs)
        q_ref, kc_ref, vc_ref, cc_ref, sc_ref = refs[:5]
        pos = 5
        if has_prev:
            kp_ref, vp_ref, cp_ref, sp_ref = refs[pos:pos + 4]
            pos += 4
        if has_sink:
            sink_ref = refs[pos]
            pos += 1
        o_ref, lse_ref = refs[pos:pos + 2]
        blk = pl.program_id(2)

        def one_class(rows):
            cq, sq = cc_ref[0, rows, :], sc_ref[0, rows, :]
            q = _rope(q_ref[0, rows, :], cq, sq) * (HEAD_DIM ** -0.5)
            k = _rope(kc_ref[0, rows, :], cq, sq)
            v = vc_ref[0, rows, :]
            if has_prev:
                k = jnp.concatenate([_rope(kp_ref[0, rows, :], cp_ref[0, rows, :], sp_ref[0, rows, :]), k], axis=0)
                v = jnp.concatenate([vp_ref[0, rows, :], v], axis=0)
            nk = k.shape[0]
            qi = lax.broadcasted_iota(jnp.int32, (grp * QBLOCK, nk), 0) % QBLOCK
            ki = lax.broadcasted_iota(jnp.int32, (grp * QBLOCK, nk), 1)
            if has_prev:
                dist = qi + QBLOCK - ki
                valid = (dist >= 0) & (dist <= n_back) & ((ki >= QBLOCK) | (blk > 0))
            else:
                dist = qi - ki
                valid = (dist >= 0) & (dist <= n_back)
            outs, lses = [], []
            for hk in range(hkv):
                heads = [hk * grp + g for g in range(grp)]
                kh = k[:, hk * HEAD_DIM:(hk + 1) * HEAD_DIM].astype(BF16)
                vh = v[:, hk * HEAD_DIM:(hk + 1) * HEAD_DIM].astype(BF16)
                qs = _stack_heads(q, heads).astype(BF16)
                s = lax.dot_general(qs, kh, _DIMS["nt"], preferred_element_type=F32)
                s = jnp.where(valid, s, NEG_INF)
                m = jnp.max(s, axis=1, keepdims=True)
                if has_sink:
                    sk = sink_ref[hk]
                    m = jnp.maximum(m, sk)
                p = jnp.exp(s - m)
                den = jnp.sum(p, axis=1, keepdims=True)
                if has_sink:
                    den = den + jnp.exp(sk - m)
                o = jnp.dot(p.astype(BF16), vh, preferred_element_type=F32) / den
                lse = m + jnp.log(den)
                for g in range(grp):
                    outs.append(o[g * QBLOCK:(g + 1) * QBLOCK])
                    lses.append(jnp.broadcast_to(lse[g * QBLOCK:(g + 1) * QBLOCK], (QBLOCK, HEAD_DIM)))
            o_ref[0, rows, :] = jnp.concatenate(outs, axis=1)
            lse_ref[0, rows, :] = jnp.concatenate(lses, axis=1)

        _for_each_class(r, one_class)

    def cur(width, col0):
        return pl.BlockSpec((1, rr, width), lambda b, c, i: (b, i, col0 + c))

    def prev(width, col0):
        return pl.BlockSpec((1, rr, width), lambda b, c, i: (b, jnp.maximum(i - 1, 0), col0 + c))

    def table(shift):
        return pl.BlockSpec((1, rr, LANES), lambda b, c, i: (b, jnp.maximum(i - shift, 0), 0))

    in_specs = [cur(qw, q_col), cur(kw, k_col), cur(kw, v_col), table(0), table(0)]
    args = [q_arr, k_arr, v_arr, cos, sin]
    if has_prev:
        in_specs += [prev(kw, k_col), prev(kw, v_col), table(1), table(1)]
        args += [k_arr, v_arr, cos, sin]
    if has_sink:
        in_specs.append(pl.BlockSpec(sink_rows.shape, lambda b, c, i: (0, 0, 0)))
        args.append(sink_rows)
    out_w = nchunk * qw
    return pl.pallas_call(
        body,
        name=name,
        grid=(bsz, nchunk, nblk),
        in_specs=in_specs,
        out_specs=[pl.BlockSpec((1, rr, qw), lambda b, c, i: (b, i, c))] * 2,
        out_shape=[jax.ShapeDtypeStruct((bsz, seq, out_w), F32)] * 2,
        compiler_params=_params("parallel", "parallel", "parallel"),
    )(*args)


def _attn_bwd(q_arr, k_arr, v_arr, cos, sin, do, lse, dd, *, name, hq, hkv, q_col, k_col, v_col, nchunk, r, n_back):
    bsz, seq, _ = q_arr.shape
    rr = QBLOCK * r
    nblk = seq // rr
    grp = hq // hkv
    qw, kw = hq * HEAD_DIM, hkv * HEAD_DIM
    has_next = nblk > 1
    scale = HEAD_DIM ** -0.5

    def body(*refs):
        refs = list(refs)
        k_ref, v_ref = refs[:2]
        cur_refs = refs[2:8]
        pos = 8
        if has_next:
            nxt_refs = refs[pos:pos + 6]
            pos += 6
        dq_ref, dk_ref, dv_ref = refs[pos:pos + 3]
        carry_ref = refs[pos + 3]
        blk = pl.program_id(2)
        if has_next:
            @pl.when(blk == 0)
            def _():
                carry_ref[...] = jnp.zeros_like(carry_ref)

        def one_class(rows):
            qi = lax.broadcasted_iota(jnp.int32, (grp * QBLOCK, QBLOCK), 0) % QBLOCK
            ki = lax.broadcasted_iota(jnp.int32, (grp * QBLOCK, QBLOCK), 1)
            tiles = []
            for which, trefs in (("cur", cur_refs),) + ((("next", nxt_refs),) if has_next else ()):
                q_ref, do_ref, l_ref, dd_ref, c_ref, s_ref = trefs
                cq, sq = c_ref[0, rows, :], s_ref[0, rows, :]
                if which == "cur":
                    valid = qi >= ki
                    ck, sk_ = cq, sq
                else:
                    valid = (qi + QBLOCK - ki <= n_back) & (blk + 1 < nblk)
                q = _rope(q_ref[0, rows, :], cq, sq) * scale
                tiles.append((q, do_ref[0, rows, :], l_ref[0, rows, :], dd_ref[0, rows, :], valid))
            k = _rope(k_ref[0, rows, :], ck, sk_)
            v = v_ref[0, rows, :]
            dq_parts = [[] for _ in tiles]
            dks, dvs = [], []
            for hk in range(hkv):
                heads = [hk * grp + g for g in range(grp)]
                kh = k[:, hk * HEAD_DIM:(hk + 1) * HEAD_DIM].astype(BF16)
                vh = v[:, hk * HEAD_DIM:(hk + 1) * HEAD_DIM].astype(BF16)
                dk_acc = jnp.zeros((QBLOCK, HEAD_DIM), F32)
                dv_acc = jnp.zeros((QBLOCK, HEAD_DIM), F32)
                for t, (q, do_, l_, dd_, valid) in enumerate(tiles):
                    qs = _stack_heads(q, heads).astype(BF16)
                    dos = _stack_heads(do_, heads).astype(BF16)
                    s = lax.dot_general(qs, kh, _DIMS["nt"], preferred_element_type=F32)
                    p = jnp.exp(jnp.where(valid, s, NEG_INF) - _stack_cols(l_, heads))
                    dp = lax.dot_general(dos, vh, _DIMS["nt"], preferred_element_type=F32)
                    ds = (p * (dp + _stack_cols(dd_, heads))).astype(BF16)
                    dv_acc += lax.dot_general(p.astype(BF16), dos, _DIMS["tn"], preferred_element_type=F32)
                    dk_acc += lax.dot_general(ds, qs, _DIMS["tn"], preferred_element_type=F32)
                    dqs = jnp.dot(ds, kh, preferred_element_type=F32) * scale
                    dq_parts[t] += [dqs[g * QBLOCK:(g + 1) * QBLOCK] for g in range(grp)]
                dks.append(dk_acc)
                dvs.append(dv_acc)
            cat = lambda parts: jnp.concatenate(parts, axis=1) if len(parts) > 1 else parts[0]
            dk_ref[0, rows, :] = _rope(cat(dks), ck, sk_, sign=-1.0)
            dv_ref[0, rows, :] = cat(dvs)
            dq = cat(dq_parts[0])
            if has_next:
                dq = dq + carry_ref[rows, :]
                carry_ref[rows, :] = cat(dq_parts[1])
            dq_ref[0, rows, :] = _rope(dq, ck, sk_, sign=-1.0)

        _for_each_class(r, one_class)

    def at(width, col0, shift):
        return pl.BlockSpec((1, rr, width), lambda b, c, i: (b, jnp.minimum(i + shift, nblk - 1), col0 + c))

    def table(shift):
        return pl.BlockSpec((1, rr, LANES), lambda b, c, i: (b, jnp.minimum(i + shift, nblk - 1), 0))

    in_specs = [at(kw, k_col, 0), at(kw, v_col, 0)]
    args = [k_arr, v_arr]
    for shift in (0, 1) if has_next else (0,):
        in_specs += [at(qw, q_col, shift), at(qw, 0, shift), at(qw, 0, shift), at(qw, 0, shift), table(shift), table(shift)]
        args += [q_arr, do, lse, dd, cos, sin]
    return pl.pallas_call(
        body,
        name=name,
        grid=(bsz, nchunk, nblk),
        in_specs=in_specs,
        out_specs=[pl.BlockSpec((1, rr, qw), lambda b, c, i: (b, i, c)),
                   pl.BlockSpec((1, rr, kw), lambda b, c, i: (b, i, c)),
                   pl.BlockSpec((1, rr, kw), lambda b, c, i: (b, i, c))],
        out_shape=[jax.ShapeDtypeStruct((bsz, seq, nchunk * qw), F32),
                   jax.ShapeDtypeStruct((bsz, seq, nchunk * kw), F32),
                   jax.ShapeDtypeStruct((bsz, seq, nchunk * kw), F32)],
        scratch_shapes=[pltpu.VMEM((rr, qw) if has_next else (8, LANES), F32)],
        compiler_params=_params("parallel", "parallel", "arbitrary"),
    )(*args)


def _half_mask(shape, half):
    lane = lax.broadcasted_iota(jnp.int32, shape, len(shape) - 1) % LANES
    return (lane < HEAD_DIM) if half == 0 else (lane >= HEAD_DIM)


def _dup_half(v, half):
    return jnp.where(_half_mask(v.shape, half), v, pltpu.roll(v, HEAD_DIM, 1))


def _fold_halves(v):
    return v + pltpu.roll(v, HEAD_DIM, 1)


def _pick_halves(lo_rows, hi_rows):
    return jnp.where(_half_mask(lo_rows.shape, 0), lo_rows, hi_rows)


def _stack_masked(v, pairs):
    parts = []
    for c in pairs:
        pair = v[:, c * LANES:(c + 1) * LANES]
        parts += [jnp.where(_half_mask(pair.shape, half), pair, 0.0) for half in (0, 1)]
    return jnp.concatenate(parts, axis=0)


def _stack_pair_cols(v, pairs):
    return jnp.concatenate([v[:, c * LANES + half * HEAD_DIM:c * LANES + half * HEAD_DIM + 1] for c in pairs for half in (0, 1)],
                           axis=0)


def _band_mask(nrows, nk, blk, n_back, has_prev):
    qi = lax.broadcasted_iota(jnp.int32, (nrows, nk), 0) % QBLOCK
    ki = lax.broadcasted_iota(jnp.int32, (nrows, nk), 1)
    if has_prev:
        dist = qi + QBLOCK - ki
        return (dist >= 0) & (dist <= n_back) & ((ki >= QBLOCK) | (blk > 0))
    dist = qi - ki
    return (dist >= 0) & (dist <= n_back)


def _attn2_fwd(q_arr, k_arr, v_arr, *, name, npair, gqa, q_col, k_col, v_col, nchunk, r, n_back, sink_rows=None):
    bsz, seq, _ = q_arr.shape
    rr = QBLOCK * r
    nblk = seq // rr
    qw = npair * LANES
    kw = LANES if gqa else qw
    has_prev = nblk > 1
    has_sink = sink_rows is not None
    scale = HEAD_DIM ** -0.5

    def body(*refs):
        refs = list(refs)
        q_ref, kc_ref, vc_ref = refs[:3]
        pos = 3
        if has_prev:
            kp_ref, vp_ref = refs[pos:pos + 2]
            pos += 2
        if has_sink:
            sink_ref = refs[pos]
            pos += 1
        o_ref, lse_ref = refs[pos:pos + 2]
        blk = pl.program_id(2)

        def softmax_pv(s, vmat, sk):
            m = jnp.max(s, axis=1, keepdims=True)
            if sk is not None:
                m = jnp.maximum(m, sk)
            p = jnp.exp(s - m)
            den = jnp.sum(p, axis=1, keepdims=True)
            if sk is not None:
                den = den + jnp.exp(sk - m)
            return jnp.dot(p.astype(BF16), vmat, preferred_element_type=F32) / den, m + jnp.log(den)

        def one_class(rows):
            q = q_ref[0, rows, :] * scale
            k, v = kc_ref[0, rows, :], vc_ref[0, rows, :]
            if has_prev:
                k = jnp.concatenate([kp_ref[0, rows, :], k], axis=0)
                v = jnp.concatenate([vp_ref[0, rows, :], v], axis=0)
            nk = k.shape[0]
            outs, lses = [None] * npair, [None] * npair
            valid = _band_mask(QBLOCK, nk, blk, n_back, has_prev)
            per = npair // 2
            if gqa:
                kdup = [_dup_half(k, hk).astype(BF16) for hk in range(2)]
                vdup = [_dup_half(v, hk) for hk in range(2)]
            for c in range(npair):
                sl = slice(c * LANES, (c + 1) * LANES)
                qc = q[:, sl]
                kc, vc = (kdup[c // per], vdup[c // per]) if gqa else (k[:, sl].astype(BF16), v[:, sl])
                for half in (0, 1):
                    qm = jnp.where(_half_mask(qc.shape, half), qc, 0.0).astype(BF16)
                    vm = jnp.where(_half_mask(vc.shape, half), vc, 0.0).astype(BF16)
                    s = lax.dot_general(qm, kc, _DIMS["nt"], preferred_element_type=F32)
                    sk = sink_ref[2 * c + half] if has_sink else None
                    o, lse = softmax_pv(jnp.where(valid, s, NEG_INF), vm, sk)
                    lse = jnp.broadcast_to(lse, o.shape)
                    outs[c] = o if half == 0 else outs[c] + o
                    lses[c] = lse if half == 0 else _pick_halves(lses[c], lse)
            o_ref[0, rows, :] = jnp.concatenate(outs, axis=1) if npair > 1 else outs[0]
            lse_ref[0, rows, :] = jnp.concatenate(lses, axis=1) if npair > 1 else lses[0]

        _for_each_class(r, one_class)

    def cur(width, col0):
        return pl.BlockSpec((1, rr, width), lambda b, c, i: (b, i, col0 + c))

    def prev(width, col0):
        return pl.BlockSpec((1, rr, width), lambda b, c, i: (b, jnp.maximum(i - 1, 0), col0 + c))

    in_specs = [cur(qw, q_col), cur(kw, k_col), cur(kw, v_col)]
    args = [q_arr, k_arr, v_arr]
    if has_prev:
        in_specs += [prev(kw, k_col), prev(kw, v_col)]
        args += [k_arr, v_arr]
    if has_sink:
        in_specs.append(pl.BlockSpec(memory_space=pltpu.SMEM))
        args.append(sink_rows)
    return pl.pallas_call(
        body,
        name=name,
        grid=(bsz, nchunk, nblk),
        in_specs=in_specs,
        out_specs=[pl.BlockSpec((1, rr, qw), lambda b, c, i: (b, i, c))] * 2,
        out_shape=[jax.ShapeDtypeStruct((bsz, seq, nchunk * qw), F32)] * 2,
        compiler_params=_params("parallel", "parallel", "parallel"),
    )(*args)


def _attn2_bwd(q_arr, k_arr, v_arr, cos, sin, do, lse, dd, *, name, npair, gqa, q_col, k_col, v_col, nchunk, r, n_back):
    bsz, seq, _ = q_arr.shape
    rr = QBLOCK * r
    nblk = seq // rr
    qw = npair * LANES
    kw = LANES if gqa else qw
    has_next = nblk > 1
    scale = HEAD_DIM ** -0.5

    def body(*refs):
        refs = list(refs)
        k_ref, v_ref, c_ref, s_ref = refs[:4]
        tile_refs = [refs[4:8]]
        pos = 8
        if has_next:
            tile_refs.append(refs[pos:pos + 4])
            pos += 4
        dq_ref, dk_ref, dv_ref = refs[pos:pos + 3]
        carry_ref = refs[pos + 3]
        blk = pl.program_id(2)
        if has_next:
            @pl.when(blk == 0)
            def _():
                carry_ref[...] = jnp.zeros_like(carry_ref)

        def tile_grads(qs, dos, lcol, dcol, valid, kmat, vmat):
            s = lax.dot_general(qs, kmat, _DIMS["nt"], preferred_element_type=F32)
            p = jnp.exp(jnp.where(valid, s, NEG_INF) - lcol)
            dp = lax.dot_general(dos, vmat, _DIMS["nt"], preferred_element_type=F32)
            ds = (p * (dp + dcol)).astype(BF16)
            dv = lax.dot_general(p.astype(BF16), dos, _DIMS["tn"], preferred_element_type=F32)
            dk = lax.dot_general(ds, qs, _DIMS["tn"], preferred_element_type=F32)
            return dv, dk, ds

        def one_class(rows):
            nrows = (npair if gqa else 1) * QBLOCK
            qi = lax.broadcasted_iota(jnp.int32, (nrows, QBLOCK), 0) % QBLOCK
            ki = lax.broadcasted_iota(jnp.int32, (nrows, QBLOCK), 1)
            valids = [qi >= ki, (qi + QBLOCK - ki <= n_back) & (blk + 1 < nblk)]
            tiles = [(q_ref[0, rows, :] * scale, do_ref[0, rows, :], l_ref[0, rows, :], d_ref[0, rows, :])
                     for q_ref, do_ref, l_ref, d_ref in tile_refs]
            k, v = k_ref[0, rows, :], v_ref[0, rows, :]
            dq = [[None] * npair for _ in tiles]
            if gqa:
                per = npair // 2
                dk_out = dv_out = None
                for hk in range(2):
                    pairs = list(range(hk * per, (hk + 1) * per))
                    kd, vd = _dup_half(k, hk).astype(BF16), _dup_half(v, hk).astype(BF16)
                    dk_acc = dv_acc = None
                    for t, (q, do_, l_, d_) in enumerate(tiles):
                        dv_t, dk_t, ds = tile_grads(_stack_masked(q, pairs).astype(BF16), _stack_masked(do_, pairs).astype(BF16),
                                                    _stack_pair_cols(l_, pairs), _stack_pair_cols(d_, pairs), valids[t], kd, vd)
                        dv_acc = dv_t if t == 0 else dv_acc + dv_t
                        dk_acc = dk_t if t == 0 else dk_acc + dk_t
                        dqs = jnp.dot(ds, kd, preferred_element_type=F32) * scale
                        for i, c in enumerate(pairs):
                            dq[t][c] = _pick_halves(dqs[2 * i * QBLOCK:(2 * i + 1) * QBLOCK],
                                                    dqs[(2 * i + 1) * QBLOCK:(2 * i + 2) * QBLOCK])
                    dk_h, dv_h = _fold_halves(dk_acc), _fold_halves(dv_acc)
                    dk_out = dk_h if hk == 0 else _pick_halves(dk_out, dk_h)
                    dv_out = dv_h if hk == 0 else _pick_halves(dv_out, dv_h)
            else:
                dks, dvs = [], []
                for c in range(npair):
                    sl = slice(c * LANES, (c + 1) * LANES)
                    kc, vc = k[:, sl], v[:, sl].astype(BF16)
                    kcb = kc.astype(BF16)
                    dk_acc = dv_acc = None
                    for t, (q, do_, l_, d_) in enumerate(tiles):
                        qc, doc = q[:, sl], do_[:, sl]
                        for half in (0, 1):
                            hm = _half_mask(qc.shape, half)
                            col = c * LANES + half * HEAD_DIM
                            dv_t, dk_t, ds = tile_grads(jnp.where(hm, qc, 0.0).astype(BF16), jnp.where(hm, doc, 0.0).astype(BF16),
                                                        l_[:, col:col + 1], d_[:, col:col + 1], valids[t], kcb, vc)
                            dv_acc = dv_t if dv_acc is None else dv_acc + dv_t
                            dk_acc = dk_t if dk_acc is None else dk_acc + dk_t
                            dq_h = jnp.dot(ds, jnp.where(hm, kc, 0.0).astype(BF16), preferred_element_type=F32)
                            dq[t][c] = dq_h if half == 0 else dq[t][c] + dq_h
                        dq[t][c] = dq[t][c] * scale
                    dks.append(dk_acc)
                    dvs.append(dv_acc)
                dk_out = jnp.concatenate(dks, axis=1) if npair > 1 else dks[0]
                dv_out = jnp.concatenate(dvs, axis=1) if npair > 1 else dvs[0]
            cat = lambda parts: jnp.concatenate(parts, axis=1) if len(parts) > 1 else parts[0]
            ck, sk_ = c_ref[0, rows, :], s_ref[0, rows, :]
            dk_ref[0, rows, :] = _rope(dk_out, ck, sk_, sign=-1.0)
            dv_ref[0, rows, :] = dv_out
            dq_cur = cat(dq[0])
            if has_next:
                dq_cur = dq_cur + carry_ref[rows, :]
                carry_ref[rows, :] = cat(dq[1])
            dq_ref[0, rows, :] = _rope(dq_cur, ck, sk_, sign=-1.0)

        _for_each_class(r, one_class)

    def at(width, col0, shift):
        return pl.BlockSpec((1, rr, width), lambda b, c, i: (b, jnp.minimum(i + shift, nblk - 1), col0 + c))

    in_specs = [at(kw, k_col, 0), at(kw, v_col, 0), pl.BlockSpec((1, rr, LANES), lambda b, c, i: (b, i, 0)),
                pl.BlockSpec((1, rr, LANES), lambda b, c, i: (b, i, 0))]
    args = [k_arr, v_arr, cos, sin]
    for shift in (0, 1) if has_next else (0,):
        in_specs += [at(qw, q_col, shift), at(qw, 0, shift), at(qw, 0, shift), at(qw, 0, shift)]
        args += [q_arr, do, lse, dd]
    return pl.pallas_call(
        body,
        name=name,
        grid=(bsz, nchunk, nblk),
        in_specs=in_specs,
        out_specs=[pl.BlockSpec((1, rr, qw), lambda b, c, i: (b, i, c)),
                   pl.BlockSpec((1, rr, kw), lambda b, c, i: (b, i, c)),
                   pl.BlockSpec((1, rr, kw), lambda b, c, i: (b, i, c))],
        out_shape=[jax.ShapeDtypeStruct((bsz, seq, nchunk * qw), F32),
                   jax.ShapeDtypeStruct((bsz, seq, nchunk * kw), F32),
                   jax.ShapeDtypeStruct((bsz, seq, nchunk * kw), F32)],
        scratch_shapes=[pltpu.VMEM((rr, qw) if has_next else (8, LANES), F32)],
        compiler_params=_params("parallel", "parallel", "arbitrary"),
    )(*args)


B_CHUNKS = {1: (4, 1), 4: (1, 4), 16: (1, 4)}


def _rope_tables(positions):
    half = HEAD_DIM // 2
    inv = ROPE_THETA ** (-jnp.arange(half, dtype=F32) / half)
    ang = positions.astype(F32)[..., None] * inv
    cos, sin = jnp.cos(ang), jnp.sin(ang)
    return jnp.concatenate([cos] * 4, axis=-1), jnp.concatenate([-sin, sin, -sin, sin], axis=-1)


def _layer_step(x, mod, positions, sinks, ln1_g, ln1_b, ln2_g, ln2_b, target, wint, wba, wbbt, wo, wgut, wd):
    bsz, seq, d = x.shape
    ntok = bsz * seq
    flat = lambda v: v.reshape(ntok, v.shape[-1])
    unflat = lambda v: v.reshape(bsz, seq, v.shape[-1])
    cos, sin = _rope_tables(positions)
    mm = functools.partial(_matmul, tm=1024, tk=1024)

    u1 = _modulate_in(x, mod)
    u1f = flat(u1)
    cosf, sinf = flat(cos), flat(sin)
    proj = functools.partial(_proj_rope, u1f, wint, cosf, sinf, tm=1024)
    qa = unflat(proj(n=1024, b_off=OFF_QA, rope_cols=1024, tn=512, name="proj_qa"))
    kva = unflat(proj(n=256, b_off=OFF_KVA, rope_cols=128, tn=128, name="proj_kva"))
    qkvb = unflat(proj(n=4608, b_off=OFF_QKVB, rope_cols=3072, tn=256, name="proj_qkvb"))
    gab = unflat(proj(n=2048, b_off=OFF_GAB, rope_cols=0, tn=256, name="proj_gab"))

    sink_rows = sinks.reshape(A_Q_HEADS)
    a_kw = dict(npair=A_Q_HEADS // 2, gqa=True, q_col=0, k_col=0, v_col=1, nchunk=1, r=1, n_back=A_WINDOW - 1)
    oa, lse_a = _attn2_fwd(qa, kva, kva, name="attn_a_fwd", sink_rows=sink_rows, **a_kw)
    ya = unflat(mm(flat(oa), wba, mode="nn", out_dtype=F32, tn=512, name="branch_a"))

    b_kws, os_, ls_ = [], [], []
    for g, (window, r) in enumerate(B_PATTERNS):
        npair, nch = B_CHUNKS[r]
        per = B_HEADS_PER_GROUP // (2 * npair)
        nsec = len(B_PATTERNS) * per
        kw_ = dict(npair=npair, gqa=False, q_col=g * per, k_col=nsec + g * per, v_col=2 * nsec + g * per, nchunk=nch, r=r,
                   n_back=window // r)
        b_kws.append(kw_)
        o_g, l_g = _attn2_fwd(qkvb, qkvb, qkvb, name=f"attn_b{g}_fwd", **kw_)
        os_.append(o_g)
        ls_.append(l_g)
    ob = _merge_b(os_, ls_)
    yb = unflat(mm(flat(ob), wbbt, mode="nt", out_dtype=F32, tk=512, tn=512, name="branch_b"))
    merged = _gate_merge(gab, ya, yb)
    y1 = unflat(mm(flat(merged), wo, mode="nn", out_dtype=F32, tn=512, name="w_o"))
    h1, u2 = _ln1_fwd(x, y1, mod, ln1_g, ln1_b)
    h = unflat(mm(flat(u2), wgut, mode="nt", out_dtype=F32, tn=512, name="gate_up"))
    a = _silu_mul(h)
    y2 = unflat(mm(flat(a), wd, mode="nn", out_dtype=F32, tk=D_FF, tn=512, name="down"))

    dy2, dh1a, acc2 = _ln2_loss_bwd(h1, y2, mod, ln2_g, ln2_b, target)
    dy2f = flat(dy2)
    da = unflat(mm(dy2f, wd, mode="nt", out_dtype=F32, tn=256, name="down_dgrad"))
    g_wd = _matmul(flat(a), dy2f, mode="tn", out_dtype=BF16, tm=256, tn=1024, tk=ntok, name="down_wgrad")
    dh = _silu_mul_bwd(da, h)
    dhf = flat(dh)
    du2 = unflat(mm(dhf, wgut, mode="nn", out_dtype=F32, tk=D_FF, tn=512, name="gate_up_dgrad"))
    g_wgut = _matmul(dhf, flat(u2), mode="tn", out_dtype=BF16, tm=256, tn=1024, tk=ntok, name="gate_up_wgrad")
    dy1, dxa, acc1 = _ln1_bwd(du2, dh1a, x, y1, mod, ln1_g, ln1_b)
    dy1f = flat(dy1)
    dmerged = unflat(mm(dy1f, wo, mode="nt", out_dtype=F32, tn=512, name="w_o_dgrad"))
    g_wo = _matmul(flat(merged), dy1f, mode="tn", out_dtype=BF16, tm=256, tn=1024, tk=ntok, name="w_o_wgrad")
    dya, dyb, dgab = _gate_merge_bwd(dmerged, gab, ya, yb)
    dyaf, dybf = flat(dya), flat(dyb)
    doa = unflat(mm(dyaf, wba, mode="nt", out_dtype=F32, tn=512, name="branch_a_dgrad"))
    g_wba = _matmul(flat(oa), dyaf, mode="tn", out_dtype=BF16, tm=256, tn=1024, tk=ntok, name="branch_a_wgrad")
    dob = unflat(mm(dybf, wbbt, mode="nn", out_dtype=F32, tn=512, name="branch_b_dgrad"))
    g_wbbt = _matmul(dybf, flat(ob), mode="tn", out_dtype=BF16, tm=256, tn=512, tk=ntok, name="branch_b_wgrad")

    sinks_exp = jnp.repeat(sinks.reshape(1, A_Q_HEADS), HEAD_DIM, axis=1)
    dd_a, acc_s = _delta_a(doa, oa, lse_a, sinks_exp)
    dqa, dka, dva = _attn2_bwd(qa, kva, kva, cos, sin, doa, lse_a, dd_a, name="attn_a_bwd", **a_kw)
    merged_bwd = _merge_b_bwd(dob, os_, ls_)
    dqs, dks, dvs = [], [], []
    for g in range(len(B_PATTERNS)):
        dq_g, dk_g, dv_g = _attn2_bwd(qkvb, qkvb, qkvb, cos, sin, merged_bwd[g], ls_[g], merged_bwd[3 + g],
                                      name=f"attn_b{g}_bwd", **b_kws[g])
        dqs.append(dq_g)
        dks.append(dk_g)
        dvs.append(dv_g)
    dproj = jnp.concatenate([t.astype(BF16) for t in [dqa, dka, dva] + dqs + dks + dvs] + [dgab], axis=-1)
    dprojf = flat(dproj)
    du1 = unflat(_matmul(dprojf, wint, mode="nn", out_dtype=F32, tm=1024, tn=512, tk=wint.shape[0] // 2, name="w_in_dgrad"))
    g_wint = _matmul(dprojf, u1f, mode="tn", out_dtype=BF16, tm=256, tn=1024, tk=ntok, name="w_in_wgrad")
    grad_x, acc0 = _grad_x(dxa, du1, x, mod)

    loss_part = jnp.sum(acc2[:, 3, 0])
    dmod = jnp.stack([acc0[:, 1], acc0[:, 0], acc1[:, 2], acc1[:, 4], acc1[:, 3], acc2[:, 2]], axis=1)
    small = jnp.stack([acc1[:, 0].sum(0), acc1[:, 1].sum(0), acc2[:, 0].sum(0), acc2[:, 1].sum(0), acc_s[:, 0].sum(0)])
    grads = dict(w_in=g_wint, w_branch_a=g_wba, w_branch_b=g_wbbt, w_o=g_wo, w_gate_up=g_wgut, w_down=g_wd)
    return loss_part, grad_x, dmod, small, grads


def _my_place():
    return lax.axis_index("x"), lax.axis_index("y"), lax.axis_index("c")


def _flip(place, k):
    px, py, pc = place
    return (1 - px if k & 4 else px, 1 - py if k & 2 else py, 1 - pc if k & 1 else pc)


def _index(place):
    return 4 * place[0] + 2 * place[1] + place[2]


def _gather_small(v, name):
    rows, cols = v.shape

    def body(v_ref, out_ref, send_sems, recv_sems):
        me = _my_place()
        out_ref[_index(me)] = v_ref[...]
        copies = []
        for k in range(1, N_DEV):
            copies.append(pltpu.make_async_remote_copy(
                src_ref=v_ref, dst_ref=out_ref.at[_index(me)], send_sem=send_sems.at[k - 1], recv_sem=recv_sems.at[k - 1],
                device_id=_flip(me, k), device_id_type=MESH))
        for cp in copies:
            cp.start()
        for k in range(1, N_DEV):
            pltpu.make_async_remote_copy(
                src_ref=v_ref, dst_ref=out_ref.at[_index(_flip(me, k))], send_sem=send_sems.at[k - 1],
                recv_sem=recv_sems.at[k - 1], device_id=_flip(me, k), device_id_type=MESH).wait_recv()
        for cp in copies:
            cp.wait_send()

    return pl.pallas_call(
        body,
        name=name,
        out_shape=jax.ShapeDtypeStruct((N_DEV, rows, cols), v.dtype),
        in_specs=[pl.BlockSpec(memory_space=pltpu.VMEM)],
        out_specs=pl.BlockSpec(memory_space=pltpu.VMEM),
        scratch_shapes=[pltpu.SemaphoreType.DMA((N_DEV - 1,)), pltpu.SemaphoreType.DMA((N_DEV - 1,))],
        compiler_params=pltpu.CompilerParams(vmem_limit_bytes=VMEM_LIMIT_BYTES),
    )(v)


def _gather_weights(packed):
    rows, cols = packed.shape

    def body(v_ref, out_ref, send_sems, recv_sems, local_sem):
        me = _my_place()
        sibling = _flip(me, 1)
        chips = [2, 4, 6]

        def slot(place):
            return out_ref.at[_index(place)]

        def copy(sem, block, to, src=None):
            return pltpu.make_async_remote_copy(
                src_ref=slot(block) if src is None else src, dst_ref=slot(block), send_sem=send_sems.at[sem],
                recv_sem=recv_sems.at[sem], device_id=to, device_id_type=MESH)

        mine = pltpu.make_async_copy(v_ref, slot(me), local_sem)
        mine.start()
        first = [copy(0, me, sibling, src=v_ref)] + [copy(1 + j, me, _flip(me, k), src=v_ref) for j, k in enumerate(chips)]
        for cp in first:
            cp.start()
        passed = [copy(4 + j, _flip(me, k), sibling) for j, k in enumerate(chips)]
        for j, k in enumerate(chips):
            copy(1 + j, _flip(me, k), me).wait_recv()
            passed[j].start()
        copy(0, sibling, me).wait_recv()
        for j, k in enumerate(chips):
            copy(4 + j, _flip(sibling, k), me).wait_recv()
        for cp in first + passed:
            cp.wait_send()
        mine.wait()

    return pl.pallas_call(
        body,
        name="gather_weights",
        out_shape=jax.ShapeDtypeStruct((N_DEV, rows, cols), packed.dtype),
        in_specs=[pl.BlockSpec(memory_space=pltpu.HBM)],
        out_specs=pl.BlockSpec(memory_space=pltpu.HBM),
        scratch_shapes=[pltpu.SemaphoreType.DMA((7,)), pltpu.SemaphoreType.DMA((7,)), pltpu.SemaphoreType.DMA],
    )(packed)


def _exchange_sibling(parts):
    _, nchip, rows, cols = parts.shape

    def body(p_ref, out_ref, send_sem, recv_sem):
        me = _my_place()
        cp = pltpu.make_async_remote_copy(src_ref=p_ref.at[1 - me[2]], dst_ref=out_ref, send_sem=send_sem, recv_sem=recv_sem,
                                          device_id=_flip(me, 1), device_id_type=MESH)
        cp.start()
        cp.wait()

    return pl.pallas_call(
        body,
        name="grad_exchange_sibling",
        out_shape=jax.ShapeDtypeStruct((nchip, rows, cols), parts.dtype),
        in_specs=[pl.BlockSpec(memory_space=pltpu.HBM)],
        out_specs=pl.BlockSpec(memory_space=pltpu.HBM),
        scratch_shapes=[pltpu.SemaphoreType.DMA, pltpu.SemaphoreType.DMA],
    )(parts)


def _exchange_chips(parts):
    _, rows, cols = parts.shape

    def body(p_ref, out_ref, send_sems, recv_sems):
        me = _my_place()
        copies = []
        for j, k in enumerate((2, 4, 6)):
            to = _flip(me, k)
            copies.append(pltpu.make_async_remote_copy(
                src_ref=p_ref.at[2 * to[0] + to[1]], dst_ref=out_ref.at[j], send_sem=send_sems.at[j], recv_sem=recv_sems.at[j],
                device_id=to, device_id_type=MESH))
        for cp in copies:
            cp.start()
        for cp in copies:
            cp.wait()

    return pl.pallas_call(
        body,
        name="grad_exchange_chips",
        out_shape=jax.ShapeDtypeStruct((3, rows, cols), parts.dtype),
        in_specs=[pl.BlockSpec(memory_space=pltpu.HBM)],
        out_specs=pl.BlockSpec(memory_space=pltpu.HBM),
        scratch_shapes=[pltpu.SemaphoreType.DMA((3,)), pltpu.SemaphoreType.DMA((3,))],
    )(parts)


SUM_TILE = 592


def _sum_pairs(mine, theirs):
    nchip, rows, cols = mine.shape
    spec = pl.BlockSpec((1, SUM_TILE, cols), lambda q, t: (q, t, 0))

    def body(a_ref, b_ref, o_ref):
        o_ref[...] = (a_ref[...].astype(F32) + b_ref[...].astype(F32)).astype(BF16)

    return pl.pallas_call(body, name="grad_sum_sibling", grid=(nchip, rows // SUM_TILE), in_specs=[spec, spec], out_specs=spec,
                          out_shape=jax.ShapeDtypeStruct(mine.shape, BF16), compiler_params=_params("parallel", "parallel"))(mine, theirs)


def _sum_final(own, got):
    rows, cols = own.shape

    def body(a_ref, g_ref, o_ref):
        o_ref[...] = ((a_ref[...].astype(F32) + g_ref[0].astype(F32)) + g_ref[1].astype(F32)) + g_ref[2].astype(F32)

    return pl.pallas_call(
        body, name="grad_sum_chips", grid=(rows // SUM_TILE,),
        in_specs=[pl.BlockSpec((SUM_TILE, cols), lambda t: (t, 0)), pl.BlockSpec((3, SUM_TILE, cols), lambda t: (0, t, 0))],
        out_specs=pl.BlockSpec((SUM_TILE, cols), lambda t: (t, 0)),
        out_shape=jax.ShapeDtypeStruct((rows, cols), F32), compiler_params=_params("parallel"))(own, got)


def _ada_fwd(c_all, w, b):
    nb, _ = c_all.shape
    ncol = w.shape[1]

    def body(c_ref, w_ref, b_ref, o_ref):
        c = c_ref[...]
        act = (c * _sigmoid(c)).astype(BF16)
        o_ref[...] = jnp.dot(act, w_ref[...].astype(BF16), preferred_element_type=F32) + b_ref[...]

    return pl.pallas_call(body, name="ada_fwd", out_shape=jax.ShapeDtypeStruct((nb, ncol), F32),
                          compiler_params=pltpu.CompilerParams(vmem_limit_bytes=VMEM_LIMIT_BYTES))(c_all, w, b)


def _ada_wgrad(c_all_t, dmod_cols):
    d, nb = c_all_t.shape
    ncol = dmod_cols.shape[1]

    def body(ct_ref, dm_ref, o_ref):
        ct = ct_ref[...]
        act = (ct * _sigmoid(ct)).astype(BF16).astype(F32)
        dm = dm_ref[...].astype(BF16).astype(F32)
        acc = act[:, 0:1] * dm[0:1, :]
        for i in range(1, nb):
            acc = acc + act[:, i:i + 1] * dm[i:i + 1, :]
        o_ref[...] = acc

    return pl.pallas_call(body, name="ada_wgrad", out_shape=jax.ShapeDtypeStruct((d, ncol), F32),
                          compiler_params=pltpu.CompilerParams(vmem_limit_bytes=VMEM_LIMIT_BYTES))(c_all_t, dmod_cols)


SMALL_ROWS = 24


def _reduce_small(gathered):
    def body(g_ref, o_ref):
        acc = g_ref[0]
        for dev in range(1, N_DEV):
            acc = acc + g_ref[dev]
        o_ref[...] = acc

    return pl.pallas_call(body, name="reduce_small", out_shape=jax.ShapeDtypeStruct(gathered.shape[1:], F32))(gathered)


def _adamw(w, g, m, v, name):
    rows, cols = w.shape
    tile = rows
    for cand in (256, 128, 64, 32, 16, 8):
        if rows % cand == 0 and rows > cand:
            tile = cand
            break
    spec = pl.BlockSpec((tile, cols), lambda t: (t, 0))
    bc1 = 1.0 - ADAM_B1 ** ADAM_STEP
    bc2 = 1.0 - ADAM_B2 ** ADAM_STEP

    def body(w_ref, g_ref, m_ref, v_ref, d_ref, nm_ref, nv_ref):
        g_ = g_ref[...]
        nm = ADAM_B1 * m_ref[...] + (1.0 - ADAM_B1) * g_
        nv = ADAM_B2 * v_ref[...] + (1.0 - ADAM_B2) * (g_ * g_)
        d_ref[...] = -ADAM_LR * ((nm / bc1) / (jnp.sqrt(nv / bc2) + ADAM_EPS) + ADAM_WD * w_ref[...])
        nm_ref[...] = nm
        nv_ref[...] = nv

    shp = jax.ShapeDtypeStruct((rows, cols), F32)
    return pl.pallas_call(body, name=name, grid=(rows // tile,), in_specs=[spec] * 4, out_specs=[spec] * 3, out_shape=[shp] * 3,
                          compiler_params=_params("parallel"))(w, g, m, v)


_WEIGHTS = ("w_ada", "b_ada", "w_in", "sinks", "w_branch_a", "w_branch_b", "w_o", "ln1_g", "ln1_b", "w_gate_up", "w_down",
            "ln2_g", "ln2_b")
_TRANSPOSED = ("w_in", "w_branch_b", "w_gate_up")


def _pack_shard(name, w):
    w = w.astype(BF16)
    if name in _TRANSPOSED:
        w = w.T
    return w.reshape(-1, D_MODEL)


def _unpack_full(name, slab):
    if name == "w_branch_b":
        return slab.reshape(N_DEV * 128, 512)
    return slab.reshape(-1, D_MODEL)


def kernel(x, c, positions, w_ada, b_ada, w_in, sinks, w_branch_a, w_branch_b, w_o, ln1_g, ln1_b, w_gate_up, w_down, ln2_g, ln2_b, loss_target, m_w_ada, m_b_ada, m_w_in, m_sinks, m_w_branch_a, m_w_branch_b, m_w_o, m_ln1_g, m_ln1_b, m_w_gate_up, m_w_down, m_ln2_g, m_ln2_b, v_w_ada, v_b_ada, v_w_in, v_sinks, v_w_branch_a, v_w_branch_b, v_w_o, v_ln1_g, v_ln1_b, v_w_gate_up, v_w_down, v_ln2_g, v_ln2_b):
    weights = dict(w_ada=w_ada, b_ada=b_ada, w_in=w_in, sinks=sinks, w_branch_a=w_branch_a, w_branch_b=w_branch_b, w_o=w_o,
                   ln1_g=ln1_g, ln1_b=ln1_b, w_gate_up=w_gate_up, w_down=w_down, ln2_g=ln2_g, ln2_b=ln2_b)
    m_in = dict(w_ada=m_w_ada, b_ada=m_b_ada, w_in=m_w_in, sinks=m_sinks, w_branch_a=m_w_branch_a, w_branch_b=m_w_branch_b,
                w_o=m_w_o, ln1_g=m_ln1_g, ln1_b=m_ln1_b, w_gate_up=m_w_gate_up, w_down=m_w_down, ln2_g=m_ln2_g, ln2_b=m_ln2_b)
    v_in = dict(w_ada=v_w_ada, b_ada=v_b_ada, w_in=v_w_in, sinks=v_sinks, w_branch_a=v_w_branch_a, w_branch_b=v_w_branch_b,
                w_o=v_w_o, ln1_g=v_ln1_g, ln1_b=v_ln1_b, w_gate_up=v_w_gate_up, w_down=v_w_down, ln2_g=v_ln2_g, ln2_b=v_ln2_b)
    bsz = x.shape[0]
    me = _index(_my_place())
    ada_cols = w_ada.shape[2]

    c_all = _gather_small(jnp.pad(c, ((0, 8 - bsz), (0, 0))), "gather_c")[:, :bsz].reshape(N_DEV * bsz, D_MODEL)
    b_cols = lax.dynamic_slice_in_dim(b_ada, me * ada_cols, ada_cols, axis=1)
    mod_cols = _ada_fwd(c_all, w_ada[0], b_cols)
    mod_all = _gather_small(mod_cols, "gather_mod").transpose(1, 0, 2).reshape(N_DEV * bsz, 6, D_MODEL)
    mod = jnp.pad(lax.dynamic_slice_in_dim(mod_all, me * bsz, bsz, axis=0), ((0, 0), (0, 2), (0, 0)))

    packed = jnp.concatenate([_pack_shard(n, weights[n][0]) for n, _ in PACK_ROWS], axis=0)
    gathered = _gather_weights(packed)
    full, off = {}, 0
    for n, r in PACK_ROWS:
        full[n] = _unpack_full(n, gathered[:, off:off + r])
        off += r

    loss_part, grad_x, dmod, small, grads = _layer_step(
        x, mod, positions, sinks[0], ln1_g, ln1_b, ln2_g, ln2_b, loss_target, full["w_in"], full["w_branch_a"],
        full["w_branch_b"], full["w_o"], full["w_gate_up"], full["w_down"])
    loss = lax.psum(loss_part, MESH_AXES)

    slabs = jnp.concatenate([grads[n].reshape(N_DEV, r, D_MODEL) for n, r in PACK_ROWS], axis=1)
    parts = slabs.reshape(4, 2, PACK_TOTAL, D_MODEL).transpose(1, 0, 2, 3)
    my_c = lax.axis_index("c")
    my_chip = 2 * lax.axis_index("x") + lax.axis_index("y")
    from_sibling = _exchange_sibling(parts)
    chip_sum = _sum_pairs(lax.dynamic_index_in_dim(parts, my_c, 0, keepdims=False), from_sibling)
    from_chips = _exchange_chips(chip_sum)
    g_packed = _sum_final(lax.dynamic_index_in_dim(chip_sum, my_chip, 0, keepdims=False), from_chips)
    g_w, off = {}, 0
    for n, r in PACK_ROWS:
        part = g_packed[off:off + r]
        off += r
        if n == "w_branch_b":
            part = part.reshape(128, 512)
        g_w[n] = part.T if n in _TRANSPOSED else part

    rows = jnp.concatenate([dmod.reshape(bsz * 6, D_MODEL), small, jnp.zeros((SMALL_ROWS - bsz * 6 - 5, D_MODEL), F32)], axis=0)
    small_all = _gather_small(rows, "gather_small")
    sums = _reduce_small(small_all)
    dmod_all = small_all[:, :bsz * 6].reshape(N_DEV * bsz, 6 * D_MODEL)
    g_w["b_ada"] = functools.reduce(jnp.add, [sums[6 * i:6 * i + 6] for i in range(bsz)]).reshape(1, 6 * D_MODEL)
    g_w["ln1_g"], g_w["ln1_b"], g_w["ln2_g"], g_w["ln2_b"] = (sums[12 + i][None] for i in range(4))
    g_w["sinks"] = sums[16][::HEAD_DIM][None]
    dmod_cols = lax.dynamic_slice_in_dim(dmod_all, me * ada_cols, ada_cols, axis=1)
    g_w["w_ada"] = _ada_wgrad(c_all.T, dmod_cols)

    out_g, out_d, out_m, out_v = [], [], [], []
    for n in _WEIGHTS:
        w2, m2, v2 = (t[n][0] if t[n].ndim == 3 else t[n] for t in (weights, m_in, v_in))
        shape = weights[n].shape
        dlt, nm, nv = _adamw(w2, g_w[n], m2, v2, "adamw_" + n)
        out_g.append(g_w[n].reshape(shape))
        out_d.append(dlt.reshape(shape))
        out_m.append(nm.reshape(shape))
        out_v.append(nv.reshape(shape))
    return (loss, grad_x, *out_g, *out_d, *out_m, *out_v)
```

```python
import functools

import jax
import jax.numpy as jnp
from jax import lax
from jax.experimental import pallas as pl
from jax.experimental.pallas import tpu as pltpu

F32 = jnp.float32
BF16 = jnp.bfloat16

D_MODEL = 1024
HEAD_DIM = 64
A_Q_HEADS = 16
A_KV_HEADS = 2
A_WINDOW = 128
B_PATTERNS = ((128, 1), (512, 4), (2048, 16))
B_HEADS_PER_GROUP = 8
D_FF = 2816
QBLOCK = 128
ROPE_THETA = 10000.0
LN_EPS = 1e-5
DEEPNORM_ALPHA = 2.0 ** 0.25
NEG_INF = -1e30
ADAM_LR, ADAM_B1, ADAM_B2, ADAM_EPS, ADAM_WD, ADAM_STEP = 0.001, 0.9, 0.999, 1e-08, 0.01, 10

N_DEV = 8
MESH_AXES = ("x", "y", "c")
LANES = 128
VMEM_LIMIT_BYTES = 56 * 1024 * 1024
MESH = pl.DeviceIdType.MESH

OFF_QA, OFF_KVA, OFF_QKVB, OFF_GAB = 0, 1024, 1280, 5888
PACK_ROWS = (("w_in", 992), ("w_branch_a", 128), ("w_branch_b", 64), ("w_o", 128), ("w_gate_up", 704), ("w_down", 352))
PACK_TOTAL = sum(r for _, r in PACK_ROWS)


def _params(*sem):
    return pltpu.CompilerParams(dimension_semantics=sem, vmem_limit_bytes=VMEM_LIMIT_BYTES)


def _sigmoid(x):
    return 1.0 / (1.0 + jnp.exp(-x))


_DIMS = {"nn": (((1,), (0,)), ((), ())), "nt": (((1,), (1,)), ((), ())), "tn": (((0,), (0,)), ((), ()))}


def _matmul(a, b, *, mode, out_dtype, tm, tn, tk, name, n=None, b_off=0):
    if mode == "nn":
        (m, k), nn_ = a.shape, b.shape[1]
    elif mode == "nt":
        (m, k), nn_ = a.shape, (b.shape[0] if n is None else n)
    else:
        (k, m), nn_ = a.shape, b.shape[1]
    assert m % tm == 0 and nn_ % tn == 0 and k % tk == 0 and b_off % tn == 0, (name, m, nn_, k)
    nk = k // tk
    joff = b_off // tn
    if mode == "nn":
        a_spec = pl.BlockSpec((tm, tk), lambda i, j, kk: (i, kk))
        b_spec = pl.BlockSpec((tk, tn), lambda i, j, kk: (kk, j))
    elif mode == "nt":
        a_spec = pl.BlockSpec((tm, tk), lambda i, j, kk: (i, kk))
        b_spec = pl.BlockSpec((tn, tk), lambda i, j, kk: (j + joff, kk))
    else:
        a_spec = pl.BlockSpec((tk, tm), lambda i, j, kk: (kk, i))
        b_spec = pl.BlockSpec((tk, tn), lambda i, j, kk: (kk, j))
    dims = _DIMS[mode]

    def body(a_ref, b_ref, o_ref, acc_ref):
        kk = pl.program_id(2)
        part = lax.dot_general(a_ref[...].astype(BF16), b_ref[...].astype(BF16), dims, preferred_element_type=F32)
        if nk == 1:
            o_ref[...] = part.astype(o_ref.dtype)
        else:
            @pl.when(kk == 0)
            def _():
                acc_ref[...] = part

            @pl.when(kk > 0)
            def _():
                acc_ref[...] += part

            @pl.when(kk == nk - 1)
            def _():
                o_ref[...] = acc_ref[...].astype(o_ref.dtype)

    return pl.pallas_call(
        body,
        name=name,
        grid=(m // tm, nn_ // tn, nk),
        in_specs=[a_spec, b_spec],
        out_specs=pl.BlockSpec((tm, tn), lambda i, j, kk: (i, j)),
        out_shape=jax.ShapeDtypeStruct((m, nn_), out_dtype),
        scratch_shapes=[pltpu.VMEM((tm, tn) if nk > 1 else (8, LANES), F32)],
        compiler_params=_params("parallel", "parallel", "arbitrary"),
    )(a, b)


def _proj_rope(a, bt, cos, sin, *, n, b_off, rope_cols, tm, tn, name):
    m, k = a.shape
    assert m % tm == 0 and n % tn == 0 and b_off % tn == 0 and rope_cols % tn == 0, name
    joff = b_off // tn
    nrope = rope_cols // tn

    def body(a_ref, b_ref, c_ref, s_ref, o_ref):
        acc = lax.dot_general(a_ref[...], b_ref[...], _DIMS["nt"], preferred_element_type=F32)
        j = pl.program_id(1)

        @pl.when(j < nrope)
        def _():
            o_ref[...] = _rope(acc, c_ref[...], s_ref[...])

        @pl.when(j >= nrope)
        def _():
            o_ref[...] = acc

    table = pl.BlockSpec((tm, LANES), lambda i, j: (i, 0))
    return pl.pallas_call(
        body,
        name=name,
        grid=(m // tm, n // tn),
        in_specs=[pl.BlockSpec((tm, k), lambda i, j: (i, 0)), pl.BlockSpec((tn, k), lambda i, j: (j + joff, 0)), table, table],
        out_specs=pl.BlockSpec((tm, tn), lambda i, j: (i, j)),
        out_shape=jax.ShapeDtypeStruct((m, n), F32),
        compiler_params=_params("parallel", "parallel"),
    )(a, bt, cos, sin)


ROW_TILE = 256


def _rows(width, col=0):
    return pl.BlockSpec((1, ROW_TILE, width), lambda b, t: (b, t, col))


def _per_batch(nrows, width):
    return pl.BlockSpec((1, nrows, width), lambda b, t: (b, 0, 0))


def _whole(shape):
    return pl.BlockSpec(shape, lambda b, t: (0,) * len(shape))


def _row_call(body, name, bsz, seq, in_specs, out_specs, out_shape, accumulates=False):
    return pl.pallas_call(
        body,
        name=name,
        grid=(bsz, seq // ROW_TILE),
        in_specs=in_specs,
        out_specs=out_specs,
        out_shape=out_shape,
        compiler_params=_params("parallel", "arbitrary" if accumulates else "parallel"),
    )


def _acc_rows(acc_ref, first, rows):
    @pl.when(first)
    def _():
        acc_ref[...] = jnp.zeros_like(acc_ref)

    for r, val in enumerate(rows):
        acc_ref[0, r:r + 1, :] += val


def _colsum(v):
    return jnp.sum(v, axis=0, keepdims=True)


def _ln_stats(z):
    mu = jnp.mean(z, axis=-1, keepdims=True)
    zc = z - mu
    var = jnp.mean(zc * zc, axis=-1, keepdims=True)
    rstd = lax.rsqrt(var + LN_EPS)
    return zc * rstd, rstd


def _ln_bwd(dxhat, xhat, rstd):
    m1 = jnp.mean(dxhat, axis=-1, keepdims=True)
    m2 = jnp.mean(dxhat * xhat, axis=-1, keepdims=True)
    return rstd * (dxhat - m1 - xhat * m2)


def _modulate_in(x, mod):
    bsz, seq, d = x.shape

    def body(x_ref, mod_ref, u_ref):
        u_ref[0] = (x_ref[0] * (1.0 + mod_ref[0, 1:2, :]) + mod_ref[0, 0:1, :]).astype(BF16)

    return _row_call(body, "modulate_in", bsz, seq, [_rows(d), _per_batch(8, d)], _rows(d),
                     jax.ShapeDtypeStruct((bsz, seq, d), BF16))(x, mod)


def _gate_merge(gab, ya, yb):
    bsz, seq, d = ya.shape

    def body(ga_ref, gb_ref, ya_ref, yb_ref, o_ref):
        o_ref[0] = (_sigmoid(ga_ref[0]) * ya_ref[0] + _sigmoid(gb_ref[0]) * yb_ref[0]).astype(BF16)

    return _row_call(body, "gate_merge", bsz, seq, [_rows(d, 0), _rows(d, 1), _rows(d), _rows(d)], _rows(d),
                     jax.ShapeDtypeStruct((bsz, seq, d), BF16))(gab, gab, ya, yb)


def _ln1_fwd(x, y1, mod, g, b):
    bsz, seq, d = x.shape

    def body(x_ref, y_ref, mod_ref, g_ref, b_ref, h_ref, u_ref):
        z = DEEPNORM_ALPHA * x_ref[0] + (1.0 + mod_ref[0, 2:3, :]) * y_ref[0]
        xhat, _ = _ln_stats(z)
        h = xhat * g_ref[...] + b_ref[...]
        h_ref[0] = h
        u_ref[0] = (h * (1.0 + mod_ref[0, 4:5, :]) + mod_ref[0, 3:4, :]).astype(BF16)

    return _row_call(body, "ln1_fwd", bsz, seq,
                     [_rows(d), _rows(d), _per_batch(8, d), _whole((1, d)), _whole((1, d))],
                     [_rows(d), _rows(d)],
                     [jax.ShapeDtypeStruct((bsz, seq, d), F32), jax.ShapeDtypeStruct((bsz, seq, d), BF16)])(x, y1, mod, g, b)


def _silu_mul(h):
    bsz, seq, _ = h.shape

    def body(hg_ref, hu_ref, a_ref):
        hg = hg_ref[0]
        a_ref[0] = (hg * _sigmoid(hg) * hu_ref[0]).astype(BF16)

    return _row_call(body, "silu_mul", bsz, seq, [_rows(D_FF, 0), _rows(D_FF, 1)], _rows(D_FF),
                     jax.ShapeDtypeStruct((bsz, seq, D_FF), BF16))(h, h)


def _ln2_loss_bwd(h1, y2, mod, g, b, target):
    bsz, seq, d = h1.shape

    def body(h_ref, y_ref, mod_ref, g_ref, b_ref, t_ref, dy_ref, dh_ref, acc_ref):
        y = y_ref[0]
        gate = 1.0 + mod_ref[0, 5:6, :]
        z = DEEPNORM_ALPHA * h_ref[0] + gate * y
        xhat, rstd = _ln_stats(z)
        diff = xhat * g_ref[...] + b_ref[...] - t_ref[0]
        loss = 0.5 * jnp.sum(jnp.sum(diff * diff, axis=-1, keepdims=True) / d, axis=0, keepdims=True)
        dout = diff / d
        dz = _ln_bwd(dout * g_ref[...], xhat, rstd)
        dy_ref[0] = (gate * dz).astype(BF16)
        dh_ref[0] = DEEPNORM_ALPHA * dz
        _acc_rows(acc_ref, pl.program_id(1) == 0,
                  [_colsum(dout * xhat), _colsum(dout), _colsum(dz * y), jnp.broadcast_to(loss, (1, d))])

    return _row_call(body, "ln2_loss_bwd", bsz, seq,
                     [_rows(d), _rows(d), _per_batch(8, d), _whole((1, d)), _whole((1, d)), _rows(d)],
                     [_rows(d), _rows(d), _per_batch(8, d)],
                     [jax.ShapeDtypeStruct((bsz, seq, d), BF16), jax.ShapeDtypeStruct((bsz, seq, d), F32),
                      jax.ShapeDtypeStruct((bsz, 8, d), F32)], accumulates=True)(h1, y2, mod, g, b, target)


def _silu_mul_bwd(da, h):
    bsz, seq, _ = h.shape

    def body(da_ref, hg_ref, hu_ref, dh_ref):
        hg, da_ = hg_ref[0], da_ref[0]
        sg = _sigmoid(hg)
        dh_ref[0, :, :D_FF] = (da_ * hu_ref[0] * (sg * (1.0 + hg * (1.0 - sg)))).astype(BF16)
        dh_ref[0, :, D_FF:] = (da_ * (hg * sg)).astype(BF16)

    return _row_call(body, "silu_mul_bwd", bsz, seq, [_rows(D_FF), _rows(D_FF, 0), _rows(D_FF, 1)], _rows(2 * D_FF),
                     jax.ShapeDtypeStruct((bsz, seq, 2 * D_FF), BF16))(da, h, h)


def _ln1_bwd(du2, dh1a, x, y1, mod, g, b):
    bsz, seq, d = x.shape

    def body(du_ref, dh_ref, x_ref, y_ref, mod_ref, g_ref, b_ref, dy_ref, dx_ref, acc_ref):
        y, du = y_ref[0], du_ref[0]
        gate = 1.0 + mod_ref[0, 2:3, :]
        z = DEEPNORM_ALPHA * x_ref[0] + gate * y
        xhat, rstd = _ln_stats(z)
        h1 = xhat * g_ref[...] + b_ref[...]
        dh1 = dh_ref[0] + du * (1.0 + mod_ref[0, 4:5, :])
        dz = _ln_bwd(dh1 * g_ref[...], xhat, rstd)
        dy_ref[0] = (gate * dz).astype(BF16)
        dx_ref[0] = DEEPNORM_ALPHA * dz
        _acc_rows(acc_ref, pl.program_id(1) == 0,
                  [_colsum(dh1 * xhat), _colsum(dh1), _colsum(dz * y), _colsum(du * h1), _colsum(du)])

    return _row_call(body, "ln1_bwd", bsz, seq,
                     [_rows(d), _rows(d), _rows(d), _rows(d), _per_batch(8, d), _whole((1, d)), _whole((1, d))],
                     [_rows(d), _rows(d), _per_batch(8, d)],
                     [jax.ShapeDtypeStruct((bsz, seq, d), BF16), jax.ShapeDtypeStruct((bsz, seq, d), F32),
                      jax.ShapeDtypeStruct((bsz, 8, d), F32)], accumulates=True)(du2, dh1a, x, y1, mod, g, b)


def _gate_merge_bwd(dm, gab, ya, yb):
    bsz, seq, d = ya.shape

    def body(dm_ref, ga_ref, gb_ref, ya_ref, yb_ref, dya_ref, dyb_ref, dg_ref):
        dm_ = dm_ref[0]
        sa, sb = _sigmoid(ga_ref[0]), _sigmoid(gb_ref[0])
        dya_ref[0] = (dm_ * sa).astype(BF16)
        dyb_ref[0] = (dm_ * sb).astype(BF16)
        dg_ref[0, :, :d] = (dm_ * ya_ref[0] * sa * (1.0 - sa)).astype(BF16)
        dg_ref[0, :, d:] = (dm_ * yb_ref[0] * sb * (1.0 - sb)).astype(BF16)

    return _row_call(body, "gate_merge_bwd", bsz, seq,
                     [_rows(d), _rows(d, 0), _rows(d, 1), _rows(d), _rows(d)],
                     [_rows(d), _rows(d), _rows(2 * d)],
                     [jax.ShapeDtypeStruct((bsz, seq, d), BF16), jax.ShapeDtypeStruct((bsz, seq, d), BF16),
                      jax.ShapeDtypeStruct((bsz, seq, 2 * d), BF16)])(dm, gab, gab, ya, yb)


def _grad_x(dxa, du1, x, mod):
    bsz, seq, d = x.shape

    def body(dxa_ref, du_ref, x_ref, mod_ref, gx_ref, acc_ref):
        du = du_ref[0]
        gx_ref[0] = dxa_ref[0] + du * (1.0 + mod_ref[0, 1:2, :])
        _acc_rows(acc_ref, pl.program_id(1) == 0, [_colsum(du * x_ref[0]), _colsum(du)])

    return _row_call(body, "grad_x", bsz, seq, [_rows(d), _rows(d), _rows(d), _per_batch(8, d)],
                     [_rows(d), _per_batch(8, d)],
                     [jax.ShapeDtypeStruct((bsz, seq, d), F32), jax.ShapeDtypeStruct((bsz, 8, d), F32)],
                     accumulates=True)(dxa, du1, x, mod)


def _segsum64(v):
    rows, width = v.shape
    ri = lax.broadcasted_iota(jnp.int32, (LANES, LANES), 0) // HEAD_DIM
    ci = lax.broadcasted_iota(jnp.int32, (LANES, LANES), 1) // HEAD_DIM
    ones = jnp.where(ri == ci, 1.0, 0.0).astype(BF16)
    out = []
    for c in range(width // LANES):
        part = v[:, c * LANES:(c + 1) * LANES]
        hi = part.astype(BF16)
        lo = (part - hi.astype(F32)).astype(BF16)
        out.append(jnp.dot(hi, ones, preferred_element_type=F32) + jnp.dot(lo, ones, preferred_element_type=F32))
    return jnp.concatenate(out, axis=1) if len(out) > 1 else out[0]


def _merge_b(os_, ls_):
    bsz, seq, w = os_[0].shape

    def body(o0, o1, o2, l0, l1, l2, ob_ref):
        ls = [l0[0], l1[0], l2[0]]
        mx = jnp.maximum(jnp.maximum(ls[0], ls[1]), ls[2])
        es = [jnp.exp(l - mx) for l in ls]
        den = es[0] + es[1] + es[2]
        ob_ref[0] = ((es[0] / den) * o0[0] + (es[1] / den) * o1[0] + (es[2] / den) * o2[0]).astype(BF16)

    return _row_call(body, "merge_b", bsz, seq, [_rows(w)] * 6, _rows(w),
                     jax.ShapeDtypeStruct((bsz, seq, w), BF16))(*os_, *ls_)


def _merge_b_bwd(dob, os_, ls_):
    bsz, seq, w = os_[0].shape

    def body(dob_ref, o0, o1, o2, l0, l1, l2, do0, do1, do2, dd0, dd1, dd2):
        dob_ = dob_ref[0]
        ls = [l0[0], l1[0], l2[0]]
        mx = jnp.maximum(jnp.maximum(ls[0], ls[1]), ls[2])
        es = [jnp.exp(l - mx) for l in ls]
        den = es[0] + es[1] + es[2]
        ws = [e / den for e in es]
        dws = [_segsum64(dob_ * o[0]) for o in (o0, o1, o2)]
        mean = ws[0] * dws[0] + ws[1] * dws[1] + ws[2] * dws[2]
        for wg, dwg, do_ref, dd_ref in zip(ws, dws, (do0, do1, do2), (dd0, dd1, dd2)):
            do_ref[0] = wg * dob_
            dd_ref[0] = -wg * mean

    shp = jax.ShapeDtypeStruct((bsz, seq, w), F32)
    return _row_call(body, "merge_b_bwd", bsz, seq, [_rows(w)] * 7, [_rows(w)] * 6, [shp] * 6)(dob, *os_, *ls_)


def _delta_a(doa, oa, lse_a, sinks_exp):
    bsz, seq, w = oa.shape

    def body(do_ref, o_ref, l_ref, s_ref, dd_ref, acc_ref):
        dd = -_segsum64(do_ref[0] * o_ref[0])
        dd_ref[0] = dd
        _acc_rows(acc_ref, pl.program_id(1) == 0, [_colsum(dd * jnp.exp(s_ref[...] - l_ref[0]))])

    return _row_call(body, "delta_a", bsz, seq, [_rows(w), _rows(w), _rows(w), _whole((1, w))],
                     [_rows(w), _per_batch(8, w)],
                     [jax.ShapeDtypeStruct((bsz, seq, w), F32), jax.ShapeDtypeStruct((bsz, 8, w), F32)],
                     accumulates=True)(doa, oa, lse_a, sinks_exp)


def _swap_halves(v):
    lane = lax.broadcasted_iota(jnp.int32, v.shape, 1)
    return jnp.where((lane % HEAD_DIM) < HEAD_DIM // 2, pltpu.roll(v, LANES - HEAD_DIM // 2, 1),
                     pltpu.roll(v, HEAD_DIM // 2, 1))


def _rope(v, cos, sin, sign=1.0):
    out = []
    for c in range(v.shape[1] // LANES):
        part = v[:, c * LANES:(c + 1) * LANES]
        out.append(part * cos + sign * (_swap_halves(part) * sin))
    return jnp.concatenate(out, axis=1) if len(out) > 1 else out[0]


def _stack_heads(v, heads):
    return jnp.concatenate([v[:, h * HEAD_DIM:(h + 1) * HEAD_DIM] for h in heads], axis=0) if len(heads) > 1 else \
        v[:, heads[0] * HEAD_DIM:(heads[0] + 1) * HEAD_DIM]


def _stack_cols(v, heads):
    return jnp.concatenate([v[:, h * HEAD_DIM:h * HEAD_DIM + 1] for h in heads], axis=0) if len(heads) > 1 else \
        v[:, heads[0] * HEAD_DIM:heads[0] * HEAD_DIM + 1]


def _for_each_class(r, fn):
    if r == 1:
        fn(pl.ds(0, QBLOCK))
    else:
        for rho in range(r):
            fn(pl.ds(rho, QBLOCK, stride=r))


def _attn_fwd(q_arr, k_arr, v_arr, cos, sin, *, name, hq, hkv, q_col, k_col, v_col, nchunk, r, n_back, sink_rows=None):
    bsz, seq, _ = q_arr.shape
    rr = QBLOCK * r
    nblk = seq // rr
    grp = hq // hkv
    qw, kw = hq * HEAD_DIM, hkv * HEAD_DIM
    has_prev = nblk > 1
    has_sink = sink_rows is not None

    def body(*refs):
        refs = list(refs)
        q_ref, kc_ref, vc_ref, cc_ref, sc_ref = refs[:5]
        pos = 5
        if has_prev:
            kp_ref, vp_ref, cp_ref, sp_ref = refs[pos:pos + 4]
            pos += 4
        if has_sink:
            sink_ref = refs[pos]
            pos += 1
        o_ref, lse_ref = refs[pos:pos + 2]
        blk = pl.program_id(2)

        def one_class(rows):
            cq, sq = cc_ref[0, rows, :], sc_ref[0, rows, :]
            q = _rope(q_ref[0, rows, :], cq, sq) * (HEAD_DIM ** -0.5)
            k = _rope(kc_ref[0, rows, :], cq, sq)
            v = vc_ref[0, rows, :]
            if has_prev:
                k = jnp.concatenate([_rope(kp_ref[0, rows, :], cp_ref[0, rows, :], sp_ref[0, rows, :]), k], axis=0)
                v = jnp.concatenate([vp_ref[0, rows, :], v], axis=0)
            nk = k.shape[0]
            qi = lax.broadcasted_iota(jnp.int32, (grp * QBLOCK, nk), 0) % QBLOCK
            ki = lax.broadcasted_iota(jnp.int32, (grp * QBLOCK, nk), 1)
            if has_prev:
                dist = qi + QBLOCK - ki
                valid = (dist >= 0) & (dist <= n_back) & ((ki >= QBLOCK) | (blk > 0))
            else:
                dist = qi - ki
                valid = (dist >= 0) & (dist <= n_back)
            outs, lses = [], []
            for hk in range(hkv):
                heads = [hk * grp + g for g in range(grp)]
                kh = k[:, hk * HEAD_DIM:(hk + 1) * HEAD_DIM].astype(BF16)
                vh = v[:, hk * HEAD_DIM:(hk + 1) * HEAD_DIM].astype(BF16)
                qs = _stack_heads(q, heads).astype(BF16)
                s = lax.dot_general(qs, kh, _DIMS["nt"], preferred_element_type=F32)
                s = jnp.where(valid, s, NEG_INF)
                m = jnp.max(s, axis=1, keepdims=True)
                if has_sink:
                    sk = sink_ref[hk]
                    m = jnp.maximum(m, sk)
                p = jnp.exp(s - m)
                den = jnp.sum(p, axis=1, keepdims=True)
                if has_sink:
                    den = den + jnp.exp(sk - m)
                o = jnp.dot(p.astype(BF16), vh, preferred_element_type=F32) / den
                lse = m + jnp.log(den)
                for g in range(grp):
                    outs.append(o[g * QBLOCK:(g + 1) * QBLOCK])
                    lses.append(jnp.broadcast_to(lse[g * QBLOCK:(g + 1) * QBLOCK], (QBLOCK, HEAD_DIM)))
            o_ref[0, rows, :] = jnp.concatenate(outs, axis=1)
            lse_ref[0, rows, :] = jnp.concatenate(lses, axis=1)

        _for_each_class(r, one_class)

    def cur(width, col0):
        return pl.BlockSpec((1, rr, width), lambda b, c, i: (b, i, col0 + c))

    def prev(width, col0):
        return pl.BlockSpec((1, rr, width), lambda b, c, i: (b, jnp.maximum(i - 1, 0), col0 + c))

    def table(shift):
        return pl.BlockSpec((1, rr, LANES), lambda b, c, i: (b, jnp.maximum(i - shift, 0), 0))

    in_specs = [cur(qw, q_col), cur(kw, k_col), cur(kw, v_col), table(0), table(0)]
    args = [q_arr, k_arr, v_arr, cos, sin]
    if has_prev:
        in_specs += [prev(kw, k_col), prev(kw, v_col), table(1), table(1)]
        args += [k_arr, v_arr, cos, sin]
    if has_sink:
        in_specs.append(pl.BlockSpec(sink_rows.shape, lambda b, c, i: (0, 0, 0)))
        args.append(sink_rows)
    out_w = nchunk * qw
    return pl.pallas_call(
        body,
        name=name,
        grid=(bsz, nchunk, nblk),
        in_specs=in_specs,
        out_specs=[pl.BlockSpec((1, rr, qw), lambda b, c, i: (b, i, c))] * 2,
        out_shape=[jax.ShapeDtypeStruct((bsz, seq, out_w), F32)] * 2,
        compiler_params=_params("parallel", "parallel", "parallel"),
    )(*args)


def _attn_bwd(q_arr, k_arr, v_arr, cos, sin, do, lse, dd, *, name, hq, hkv, q_col, k_col, v_col, nchunk, r, n_back):
    bsz, seq, _ = q_arr.shape
    rr = QBLOCK * r
    nblk = seq // rr
    grp = hq // hkv
    qw, kw = hq * HEAD_DIM, hkv * HEAD_DIM
    has_next = nblk > 1
    scale = HEAD_DIM ** -0.5

    def body(*refs):
        refs = list(refs)
        k_ref, v_ref = refs[:2]
        cur_refs = refs[2:8]
        pos = 8
        if has_next:
            nxt_refs = refs[pos:pos + 6]
            pos += 6
        dq_ref, dk_ref, dv_ref = refs[pos:pos + 3]
        carry_ref = refs[pos + 3]
        blk = pl.program_id(2)
        if has_next:
            @pl.when(blk == 0)
            def _():
                carry_ref[...] = jnp.zeros_like(carry_ref)

        def one_class(rows):
            qi = lax.broadcasted_iota(jnp.int32, (grp * QBLOCK, QBLOCK), 0) % QBLOCK
            ki = lax.broadcasted_iota(jnp.int32, (grp * QBLOCK, QBLOCK), 1)
            tiles = []
            for which, trefs in (("cur", cur_refs),) + ((("next", nxt_refs),) if has_next else ()):
                q_ref, do_ref, l_ref, dd_ref, c_ref, s_ref = trefs
                cq, sq = c_ref[0, rows, :], s_ref[0, rows, :]
                if which == "cur":
                    valid = qi >= ki
                    ck, sk_ = cq, sq
                else:
                    valid = (qi + QBLOCK - ki <= n_back) & (blk + 1 < nblk)
                q = _rope(q_ref[0, rows, :], cq, sq) * scale
                tiles.append((q, do_ref[0, rows, :], l_ref[0, rows, :], dd_ref[0, rows, :], valid))
            k = _rope(k_ref[0, rows, :], ck, sk_)
            v = v_ref[0, rows, :]
            dq_parts = [[] for _ in tiles]
            dks, dvs = [], []
            for hk in range(hkv):
                heads = [hk * grp + g for g in range(grp)]
                kh = k[:, hk * HEAD_DIM:(hk + 1) * HEAD_DIM].astype(BF16)
                vh = v[:, hk * HEAD_DIM:(hk + 1) * HEAD_DIM].astype(BF16)
                dk_acc = jnp.zeros((QBLOCK, HEAD_DIM), F32)
                dv_acc = jnp.zeros((QBLOCK, HEAD_DIM), F32)
                for t, (q, do_, l_, dd_, valid) in enumerate(tiles):
                    qs = _stack_heads(q, heads).astype(BF16)
                    dos = _stack_heads(do_, heads).astype(BF16)
                    s = lax.dot_general(qs, kh, _DIMS["nt"], preferred_element_type=F32)
                    p = jnp.exp(jnp.where(valid, s, NEG_INF) - _stack_cols(l_, heads))
                    dp = lax.dot_general(dos, vh, _DIMS["nt"], preferred_element_type=F32)
                    ds = (p * (dp + _stack_cols(dd_, heads))).astype(BF16)
                    dv_acc += lax.dot_general(p.astype(BF16), dos, _DIMS["tn"], preferred_element_type=F32)
                    dk_acc += lax.dot_general(ds, qs, _DIMS["tn"], preferred_element_type=F32)
                    dqs = jnp.dot(ds, kh, preferred_element_type=F32) * scale
                    dq_parts[t] += [dqs[g * QBLOCK:(g + 1) * QBLOCK] for g in range(grp)]
                dks.append(dk_acc)
                dvs.append(dv_acc)
            cat = lambda parts: jnp.concatenate(parts, axis=1) if len(parts) > 1 else parts[0]
            dk_ref[0, rows, :] = _rope(cat(dks), ck, sk_, sign=-1.0)
            dv_ref[0, rows, :] = cat(dvs)
            dq = cat(dq_parts[0])
            if has_next:
                dq = dq + carry_ref[rows, :]
                carry_ref[rows, :] = cat(dq_parts[1])
            dq_ref[0, rows, :] = _rope(dq, ck, sk_, sign=-1.0)

        _for_each_class(r, one_class)

    def at(width, col0, shift):
        return pl.BlockSpec((1, rr, width), lambda b, c, i: (b, jnp.minimum(i + shift, nblk - 1), col0 + c))

    def table(shift):
        return pl.BlockSpec((1, rr, LANES), lambda b, c, i: (b, jnp.minimum(i + shift, nblk - 1), 0))

    in_specs = [at(kw, k_col, 0), at(kw, v_col, 0)]
    args = [k_arr, v_arr]
    for shift in (0, 1) if has_next else (0,):
        in_specs += [at(qw, q_col, shift), at(qw, 0, shift), at(qw, 0, shift), at(qw, 0, shift), table(shift), table(shift)]
        args += [q_arr, do, lse, dd, cos, sin]
    return pl.pallas_call(
        body,
        name=name,
        grid=(bsz, nchunk, nblk),
        in_specs=in_specs,
        out_specs=[pl.BlockSpec((1, rr, qw), lambda b, c, i: (b, i, c)),
                   pl.BlockSpec((1, rr, kw), lambda b, c, i: (b, i, c)),
                   pl.BlockSpec((1, rr, kw), lambda b, c, i: (b, i, c))],
        out_shape=[jax.ShapeDtypeStruct((bsz, seq, nchunk * qw), F32),
                   jax.ShapeDtypeStruct((bsz, seq, nchunk * kw), F32),
                   jax.ShapeDtypeStruct((bsz, seq, nchunk * kw), F32)],
        scratch_shapes=[pltpu.VMEM((rr, qw) if has_next else (8, LANES), F32)],
        compiler_params=_params("parallel", "parallel", "arbitrary"),
    )(*args)


def _half_mask(shape, half):
    lane = lax.broadcasted_iota(jnp.int32, shape, len(shape) - 1) % LANES
    return (lane < HEAD_DIM) if half == 0 else (lane >= HEAD_DIM)


def _dup_half(v, half):
    return jnp.where(_half_mask(v.shape, half), v, pltpu.roll(v, HEAD_DIM, 1))


def _fold_halves(v):
    return v + pltpu.roll(v, HEAD_DIM, 1)


def _pick_halves(lo_rows, hi_rows):
    return jnp.where(_half_mask(lo_rows.shape, 0), lo_rows, hi_rows)


def _stack_masked(v, pairs):
    parts = []
    for c in pairs:
        pair = v[:, c * LANES:(c + 1) * LANES]
        parts += [jnp.where(_half_mask(pair.shape, half), pair, 0.0) for half in (0, 1)]
    return jnp.concatenate(parts, axis=0)


def _stack_pair_cols(v, pairs):
    return jnp.concatenate([v[:, c * LANES + half * HEAD_DIM:c * LANES + half * HEAD_DIM + 1] for c in pairs for half in (0, 1)],
                           axis=0)


ATTN_UNITS = 16


def _class_rows(r):
    return [pl.ds(0, QBLOCK)] if r == 1 else [pl.ds(rho, QBLOCK, stride=r) for rho in range(r)]


def _band_mask(nrows, nk, blk, n_back, has_prev):
    qi = lax.broadcasted_iota(jnp.int32, (nrows, nk), 0) % QBLOCK
    ki = lax.broadcasted_iota(jnp.int32, (nrows, nk), 1)
    if has_prev:
        dist = qi + QBLOCK - ki
        return (dist >= 0) & (dist <= n_back) & ((ki >= QBLOCK) | (blk > 0))
    dist = qi - ki
    return (dist >= 0) & (dist <= n_back)


def _attn2_fwd(q_arr, k_arr, v_arr, *, name, npair, gqa, q_col, k_col, v_col, nchunk, r, n_back, sink_rows=None):
    bsz, seq, _ = q_arr.shape
    rr = QBLOCK * r
    nblk = seq // rr
    qw = npair * LANES
    kw = LANES if gqa else qw
    has_prev = nblk > 1
    has_sink = sink_rows is not None
    scale = HEAD_DIM ** -0.5

    def body(*refs):
        refs = list(refs)
        q_ref, kc_ref, vc_ref = refs[:3]
        pos = 3
        if has_prev:
            kp_ref, vp_ref = refs[pos:pos + 2]
            pos += 2
        if has_sink:
            sink_ref = refs[pos]
            pos += 1
        o_ref, lse_ref = refs[pos:pos + 2]
        blk = pl.program_id(2)

        nk = (2 if has_prev else 1) * QBLOCK
        valid = _band_mask(QBLOCK, nk, blk, n_back, has_prev)
        per = npair // 2
        classes = _class_rows(r)
        step = max(1, ATTN_UNITS // (2 * npair))
        for first in range(0, len(classes), step):
            batch = classes[first:first + step]
            units = []
            for ci, rows in enumerate(batch):
                q = q_ref[0, rows, :] * scale
                k, v = kc_ref[0, rows, :], vc_ref[0, rows, :]
                if has_prev:
                    k = jnp.concatenate([kp_ref[0, rows, :], k], axis=0)
                    v = jnp.concatenate([vp_ref[0, rows, :], v], axis=0)
                if gqa:
                    kdup = [_dup_half(k, hk).astype(BF16) for hk in range(2)]
                    vdup = [_dup_half(v, hk) for hk in range(2)]
                for c in range(npair):
                    sl = slice(c * LANES, (c + 1) * LANES)
                    qc = q[:, sl]
                    kc, vc = (kdup[c // per], vdup[c // per]) if gqa else (k[:, sl].astype(BF16), v[:, sl])
                    for half in (0, 1):
                        qm = jnp.where(_half_mask(qc.shape, half), qc, 0.0).astype(BF16)
                        vm = jnp.where(_half_mask(vc.shape, half), vc, 0.0).astype(BF16)
                        s = lax.dot_general(qm, kc, _DIMS["nt"], preferred_element_type=F32)
                        units.append(dict(ci=ci, c=c, half=half, s=s, vm=vm, sk=sink_ref[2 * c + half] if has_sink else None))
            for u in units:
                s = jnp.where(valid, u["s"], NEG_INF)
                m = jnp.max(s, axis=1, keepdims=True)
                if has_sink:
                    m = jnp.maximum(m, u["sk"])
                p = jnp.exp(s - m)
                den = jnp.sum(p, axis=1, keepdims=True)
                if has_sink:
                    den = den + jnp.exp(u["sk"] - m)
                u.update(p=p.astype(BF16), den=den, lse=m + jnp.log(den))
            for u in units:
                u["o"] = jnp.dot(u["p"], u["vm"], preferred_element_type=F32) / u["den"]
            for ci, rows in enumerate(batch):
                outs, lses = [None] * npair, [None] * npair
                for u in units:
                    if u["ci"] != ci:
                        continue
                    c, o = u["c"], u["o"]
                    lse = jnp.broadcast_to(u["lse"], o.shape)
                    outs[c] = o if u["half"] == 0 else outs[c] + o
                    lses[c] = lse if u["half"] == 0 else _pick_halves(lses[c], lse)
                o_ref[0, rows, :] = jnp.concatenate(outs, axis=1) if npair > 1 else outs[0]
                lse_ref[0, rows, :] = jnp.concatenate(lses, axis=1) if npair > 1 else lses[0]

    def cur(width, col0):
        return pl.BlockSpec((1, rr, width), lambda b, c, i: (b, i, col0 + c))

    def prev(width, col0):
        return pl.BlockSpec((1, rr, width), lambda b, c, i: (b, jnp.maximum(i - 1, 0), col0 + c))

    in_specs = [cur(qw, q_col), cur(kw, k_col), cur(kw, v_col)]
    args = [q_arr, k_arr, v_arr]
    if has_prev:
        in_specs += [prev(kw, k_col), prev(kw, v_col)]
        args += [k_arr, v_arr]
    if has_sink:
        in_specs.append(pl.BlockSpec(memory_space=pltpu.SMEM))
        args.append(sink_rows)
    return pl.pallas_call(
        body,
        name=name,
        grid=(bsz, nchunk, nblk),
        in_specs=in_specs,
        out_specs=[pl.BlockSpec((1, rr, qw), lambda b, c, i: (b, i, c))] * 2,
        out_shape=[jax.ShapeDtypeStruct((bsz, seq, nchunk * qw), F32)] * 2,
        compiler_params=_params("parallel", "parallel", "parallel"),
    )(*args)


def _attn2_bwd(q_arr, k_arr, v_arr, cos, sin, do, lse, dd, *, name, npair, gqa, q_col, k_col, v_col, nchunk, r, n_back):
    bsz, seq, _ = q_arr.shape
    rr = QBLOCK * r
    nblk = seq // rr
    qw = npair * LANES
    kw = LANES if gqa else qw
    has_next = nblk > 1
    scale = HEAD_DIM ** -0.5

    def body(*refs):
        refs = list(refs)
        k_ref, v_ref, c_ref, s_ref = refs[:4]
        tile_refs = [refs[4:8]]
        pos = 8
        if has_next:
            tile_refs.append(refs[pos:pos + 4])
            pos += 4
        dq_ref, dk_ref, dv_ref = refs[pos:pos + 3]
        carry_ref = refs[pos + 3]
        blk = pl.program_id(2)
        if has_next:
            @pl.when(blk == 0)
            def _():
                carry_ref[...] = jnp.zeros_like(carry_ref)

        nrows = (npair if gqa else 1) * QBLOCK
        qi = lax.broadcasted_iota(jnp.int32, (nrows, QBLOCK), 0) % QBLOCK
        ki = lax.broadcasted_iota(jnp.int32, (nrows, QBLOCK), 1)
        valids = [qi >= ki, (qi + QBLOCK - ki <= n_back) & (blk + 1 < nblk)]
        per = npair // 2
        ntile = len(tile_refs)
        cat = lambda parts: jnp.concatenate(parts, axis=1) if len(parts) > 1 else parts[0]
        classes = _class_rows(r)
        step = max(1, ATTN_UNITS // (ntile * (2 if gqa else 2 * npair)))
        for first in range(0, len(classes), step):
            batch = classes[first:first + step]
            units = []
            for ci, rows in enumerate(batch):
                tiles = [(q_ref[0, rows, :] * scale, do_ref[0, rows, :], l_ref[0, rows, :], d_ref[0, rows, :])
                         for q_ref, do_ref, l_ref, d_ref in tile_refs]
                k, v = k_ref[0, rows, :], v_ref[0, rows, :]
                if gqa:
                    for hk in range(2):
                        pairs = list(range(hk * per, (hk + 1) * per))
                        kd, vd = _dup_half(k, hk).astype(BF16), _dup_half(v, hk).astype(BF16)
                        for t, (q, do_, l_, d_) in enumerate(tiles):
                            units.append(dict(ci=ci, t=t, hk=hk, pairs=pairs, qs=_stack_masked(q, pairs).astype(BF16),
                                              dos=_stack_masked(do_, pairs).astype(BF16), lcol=_stack_pair_cols(l_, pairs),
                                              dcol=_stack_pair_cols(d_, pairs), kmat=kd, vmat=vd, kdq=kd))
                else:
                    for c in range(npair):
                        sl = slice(c * LANES, (c + 1) * LANES)
                        kc, vcb = k[:, sl], v[:, sl].astype(BF16)
                        kcb = kc.astype(BF16)
                        for t, (q, do_, l_, d_) in enumerate(tiles):
                            for half in (0, 1):
                                hm = _half_mask(kc.shape, half)
                                col = c * LANES + half * HEAD_DIM
                                units.append(dict(ci=ci, t=t, c=c, half=half, qs=jnp.where(hm, q[:, sl], 0.0).astype(BF16),
                                                  dos=jnp.where(hm, do_[:, sl], 0.0).astype(BF16), lcol=l_[:, col:col + 1],
                                                  dcol=d_[:, col:col + 1], kmat=kcb, vmat=vcb,
                                                  kdq=jnp.where(hm, kc, 0.0).astype(BF16)))
            for u in units:
                u["s"] = lax.dot_general(u["qs"], u["kmat"], _DIMS["nt"], preferred_element_type=F32)
                u["dp"] = lax.dot_general(u["dos"], u["vmat"], _DIMS["nt"], preferred_element_type=F32)
            for u in units:
                p = jnp.exp(jnp.where(valids[u["t"]], u["s"], NEG_INF) - u["lcol"])
                u["ds"] = (p * (u["dp"] + u["dcol"])).astype(BF16)
                u["p"] = p.astype(BF16)
            for u in units:
                u["dv"] = lax.dot_general(u["p"], u["dos"], _DIMS["tn"], preferred_element_type=F32)
                u["dk"] = lax.dot_general(u["ds"], u["qs"], _DIMS["tn"], preferred_element_type=F32)
                u["dq"] = jnp.dot(u["ds"], u["kdq"], preferred_element_type=F32) * scale
            for ci, rows in enumerate(batch):
                mine = [u for u in units if u["ci"] == ci]
                dq = [[None] * npair for _ in range(ntile)]
                if gqa:
                    dk_out = dv_out = None
                    for hk in range(2):
                        us = [u for u in mine if u["hk"] == hk]
                        for u in us:
                            for i, c in enumerate(u["pairs"]):
                                dq[u["t"]][c] = _pick_halves(u["dq"][2 * i * QBLOCK:(2 * i + 1) * QBLOCK],
                                                             u["dq"][(2 * i + 1) * QBLOCK:(2 * i + 2) * QBLOCK])
                        dk_h = _fold_halves(functools.reduce(jnp.add, [u["dk"] for u in us]))
                        dv_h = _fold_halves(functools.reduce(jnp.add, [u["dv"] for u in us]))
                        dk_out = dk_h if hk == 0 else _pick_halves(dk_out, dk_h)
                        dv_out = dv_h if hk == 0 else _pick_halves(dv_out, dv_h)
                else:
                    dks, dvs = [], []
                    for c in range(npair):
                        us = [u for u in mine if u["c"] == c]
                        dks.append(functools.reduce(jnp.add, [u["dk"] for u in us]))
                        dvs.append(functools.reduce(jnp.add, [u["dv"] for u in us]))
                        for t in range(ntile):
                            dq[t][c] = functools.reduce(jnp.add, [u["dq"] for u in us if u["t"] == t])
                    dk_out, dv_out = cat(dks), cat(dvs)
                ck, sk_ = c_ref[0, rows, :], s_ref[0, rows, :]
                dk_ref[0, rows, :] = _rope(dk_out, ck, sk_, sign=-1.0)
                dv_ref[0, rows, :] = dv_out
                dq_cur = cat(dq[0])
                if has_next:
                    dq_cur = dq_cur + carry_ref[rows, :]
                    carry_ref[rows, :] = cat(dq[1])
                dq_ref[0, rows, :] = _rope(dq_cur, ck, sk_, sign=-1.0)

    def at(width, col0, shift):
        return pl.BlockSpec((1, rr, width), lambda b, c, i: (b, jnp.minimum(i + shift, nblk - 1), col0 + c))

    in_specs = [at(kw, k_col, 0), at(kw, v_col, 0), pl.BlockSpec((1, rr, LANES), lambda b, c, i: (b, i, 0)),
                pl.BlockSpec((1, rr, LANES), lambda b, c, i: (b, i, 0))]
    args = [k_arr, v_arr, cos, sin]
    for shift in (0, 1) if has_next else (0,):
        in_specs += [at(qw, q_col, shift), at(qw, 0, shift), at(qw, 0, shift), at(qw, 0, shift)]
        args += [q_arr, do, lse, dd]
    return pl.pallas_call(
        body,
        name=name,
        grid=(bsz, nchunk, nblk),
        in_specs=in_specs,
        out_specs=[pl.BlockSpec((1, rr, qw), lambda b, c, i: (b, i, c)),
                   pl.BlockSpec((1, rr, kw), lambda b, c, i: (b, i, c)),
                   pl.BlockSpec((1, rr, kw), lambda b, c, i: (b, i, c))],
        out_shape=[jax.ShapeDtypeStruct((bsz, seq, nchunk * qw), F32),
                   jax.ShapeDtypeStruct((bsz, seq, nchunk * kw), F32),
                   jax.ShapeDtypeStruct((bsz, seq, nchunk * kw), F32)],
        scratch_shapes=[pltpu.VMEM((rr, qw) if has_next else (8, LANES), F32)],
        compiler_params=_params("parallel", "parallel", "arbitrary"),
    )(*args)


B_CHUNKS = {1: (4, 1), 4: (1, 4), 16: (1, 4)}


def _rope_tables(positions):
    half = HEAD_DIM // 2
    inv = ROPE_THETA ** (-jnp.arange(half, dtype=F32) / half)
    ang = positions.astype(F32)[..., None] * inv
    cos, sin = jnp.cos(ang), jnp.sin(ang)
    return jnp.concatenate([cos] * 4, axis=-1), jnp.concatenate([-sin, sin, -sin, sin], axis=-1)


def _layer_step(x, mod, positions, sinks, ln1_g, ln1_b, ln2_g, ln2_b, target, wint, wba, wbbt, wo, wgut, wd):
    bsz, seq, d = x.shape
    ntok = bsz * seq
    flat = lambda v: v.reshape(ntok, v.shape[-1])
    unflat = lambda v: v.reshape(bsz, seq, v.shape[-1])
    cos, sin = _rope_tables(positions)
    mm = functools.partial(_matmul, tm=1024, tk=1024)

    u1 = _modulate_in(x, mod)
    u1f = flat(u1)
    cosf, sinf = flat(cos), flat(sin)
    proj = functools.partial(_proj_rope, u1f, wint, cosf, sinf, tm=1024)
    qa = unflat(proj(n=1024, b_off=OFF_QA, rope_cols=1024, tn=512, name="proj_qa"))
    kva = unflat(proj(n=256, b_off=OFF_KVA, rope_cols=128, tn=128, name="proj_kva"))
    qkvb = unflat(proj(n=4608, b_off=OFF_QKVB, rope_cols=3072, tn=256, name="proj_qkvb"))
    gab = unflat(proj(n=2048, b_off=OFF_GAB, rope_cols=0, tn=256, name="proj_gab"))

    sink_rows = sinks.reshape(A_Q_HEADS)
    a_kw = dict(npair=A_Q_HEADS // 2, gqa=True, q_col=0, k_col=0, v_col=1, nchunk=1, r=1, n_back=A_WINDOW - 1)
    oa, lse_a = _attn2_fwd(qa, kva, kva, name="attn_a_fwd", sink_rows=sink_rows, **a_kw)
    ya = unflat(mm(flat(oa), wba, mode="nn", out_dtype=F32, tn=512, name="branch_a"))

    b_kws, os_, ls_ = [], [], []
    for g, (window, r) in enumerate(B_PATTERNS):
        npair, nch = B_CHUNKS[r]
        per = B_HEADS_PER_GROUP // (2 * npair)
        nsec = len(B_PATTERNS) * per
        kw_ = dict(npair=npair, gqa=False, q_col=g * per, k_col=nsec + g * per, v_col=2 * nsec + g * per, nchunk=nch, r=r,
                   n_back=window // r)
        b_kws.append(kw_)
        o_g, l_g = _attn2_fwd(qkvb, qkvb, qkvb, name=f"attn_b{g}_fwd", **kw_)
        os_.append(o_g)
        ls_.append(l_g)
    ob = _merge_b(os_, ls_)
    yb = unflat(mm(flat(ob), wbbt, mode="nt", out_dtype=F32, tk=512, tn=512, name="branch_b"))
    merged = _gate_merge(gab, ya, yb)
    y1 = unflat(mm(flat(merged), wo, mode="nn", out_dtype=F32, tn=512, name="w_o"))
    h1, u2 = _ln1_fwd(x, y1, mod, ln1_g, ln1_b)
    h = unflat(mm(flat(u2), wgut, mode="nt", out_dtype=F32, tn=512, name="gate_up"))
    a = _silu_mul(h)
    y2 = unflat(mm(flat(a), wd, mode="nn", out_dtype=F32, tk=D_FF, tn=512, name="down"))

    dy2, dh1a, acc2 = _ln2_loss_bwd(h1, y2, mod, ln2_g, ln2_b, target)
    dy2f = flat(dy2)
    da = unflat(mm(dy2f, wd, mode="nt", out_dtype=F32, tn=256, name="down_dgrad"))
    g_wd = _matmul(flat(a), dy2f, mode="tn", out_dtype=BF16, tm=256, tn=1024, tk=ntok, name="down_wgrad")
    dh = _silu_mul_bwd(da, h)
    dhf = flat(dh)
    du2 = unflat(mm(dhf, wgut, mode="nn", out_dtype=F32, tk=D_FF, tn=512, name="gate_up_dgrad"))
    g_wgut = _matmul(dhf, flat(u2), mode="tn", out_dtype=BF16, tm=256, tn=1024, tk=ntok, name="gate_up_wgrad")
    dy1, dxa, acc1 = _ln1_bwd(du2, dh1a, x, y1, mod, ln1_g, ln1_b)
    dy1f = flat(dy1)
    dmerged = unflat(mm(dy1f, wo, mode="nt", out_dtype=F32, tn=512, name="w_o_dgrad"))
    g_wo = _matmul(flat(merged), dy1f, mode="tn", out_dtype=BF16, tm=256, tn=1024, tk=ntok, name="w_o_wgrad")
    dya, dyb, dgab = _gate_merge_bwd(dmerged, gab, ya, yb)
    dyaf, dybf = flat(dya), flat(dyb)
    doa = unflat(mm(dyaf, wba, mode="nt", out_dtype=F32, tn=512, name="branch_a_dgrad"))
    g_wba = _matmul(flat(oa), dyaf, mode="tn", out_dtype=BF16, tm=256, tn=1024, tk=ntok, name="branch_a_wgrad")
    dob = unflat(mm(dybf, wbbt, mode="nn", out_dtype=F32, tn=512, name="branch_b_dgrad"))
    g_wbbt = _matmul(dybf, flat(ob), mode="tn", out_dtype=BF16, tm=256, tn=512, tk=ntok, name="branch_b_wgrad")

    sinks_exp = jnp.repeat(sinks.reshape(1, A_Q_HEADS), HEAD_DIM, axis=1)
    dd_a, acc_s = _delta_a(doa, oa, lse_a, sinks_exp)
    dqa, dka, dva = _attn2_bwd(qa, kva, kva, cos, sin, doa, lse_a, dd_a, name="attn_a_bwd", **a_kw)
    merged_bwd = _merge_b_bwd(dob, os_, ls_)
    dqs, dks, dvs = [], [], []
    for g in range(len(B_PATTERNS)):
        dq_g, dk_g, dv_g = _attn2_bwd(qkvb, qkvb, qkvb, cos, sin, merged_bwd[g], ls_[g], merged_bwd[3 + g],
                                      name=f"attn_b{g}_bwd", **b_kws[g])
        dqs.append(dq_g)
        dks.append(dk_g)
        dvs.append(dv_g)
    dproj = jnp.concatenate([t.astype(BF16) for t in [dqa, dka, dva] + dqs + dks + dvs] + [dgab], axis=-1)
    dprojf = flat(dproj)
    du1 = unflat(_matmul(dprojf, wint, mode="nn", out_dtype=F32, tm=1024, tn=512, tk=wint.shape[0] // 2, name="w_in_dgrad"))
    g_wint = _matmul(dprojf, u1f, mode="tn", out_dtype=BF16, tm=256, tn=1024, tk=ntok, name="w_in_wgrad")
    grad_x, acc0 = _grad_x(dxa, du1, x, mod)

    loss_part = jnp.sum(acc2[:, 3, 0])
    dmod = jnp.stack([acc0[:, 1], acc0[:, 0], acc1[:, 2], acc1[:, 4], acc1[:, 3], acc2[:, 2]], axis=1)
    small = jnp.stack([acc1[:, 0].sum(0), acc1[:, 1].sum(0), acc2[:, 0].sum(0), acc2[:, 1].sum(0), acc_s[:, 0].sum(0)])
    grads = dict(w_in=g_wint, w_branch_a=g_wba, w_branch_b=g_wbbt, w_o=g_wo, w_gate_up=g_wgut, w_down=g_wd)
    return loss_part, grad_x, dmod, small, grads


def _my_place():
    return lax.axis_index("x"), lax.axis_index("y"), lax.axis_index("c")


def _flip(place, k):
    px, py, pc = place
    return (1 - px if k & 4 else px, 1 - py if k & 2 else py, 1 - pc if k & 1 else pc)


def _index(place):
    return 4 * place[0] + 2 * place[1] + place[2]


def _gather_small(v, name):
    rows, cols = v.shape

    def body(v_ref, out_ref, send_sems, recv_sems):
        me = _my_place()
        out_ref[_index(me)] = v_ref[...]
        copies = []
        for k in range(1, N_DEV):
            copies.append(pltpu.make_async_remote_copy(
                src_ref=v_ref, dst_ref=out_ref.at[_index(me)], send_sem=send_sems.at[k - 1], recv_sem=recv_sems.at[k - 1],
                device_id=_flip(me, k), device_id_type=MESH))
        for cp in copies:
            cp.start()
        for k in range(1, N_DEV):
            pltpu.make_async_remote_copy(
                src_ref=v_ref, dst_ref=out_ref.at[_index(_flip(me, k))], send_sem=send_sems.at[k - 1],
                recv_sem=recv_sems.at[k - 1], device_id=_flip(me, k), device_id_type=MESH).wait_recv()
        for cp in copies:
            cp.wait_send()

    return pl.pallas_call(
        body,
        name=name,
        out_shape=jax.ShapeDtypeStruct((N_DEV, rows, cols), v.dtype),
        in_specs=[pl.BlockSpec(memory_space=pltpu.VMEM)],
        out_specs=pl.BlockSpec(memory_space=pltpu.VMEM),
        scratch_shapes=[pltpu.SemaphoreType.DMA((N_DEV - 1,)), pltpu.SemaphoreType.DMA((N_DEV - 1,))],
        compiler_params=pltpu.CompilerParams(vmem_limit_bytes=VMEM_LIMIT_BYTES),
    )(v)


def _gather_weights(packed):
    rows, cols = packed.shape

    def body(v_ref, out_ref, send_sems, recv_sems, local_sem):
        me = _my_place()
        sibling = _flip(me, 1)
        chips = [2, 4, 6]

        def slot(place):
            return out_ref.at[_index(place)]

        def copy(sem, block, to, src=None):
            return pltpu.make_async_remote_copy(
                src_ref=slot(block) if src is None else src, dst_ref=slot(block), send_sem=send_sems.at[sem],
                recv_sem=recv_sems.at[sem], device_id=to, device_id_type=MESH)

        mine = pltpu.make_async_copy(v_ref, slot(me), local_sem)
        mine.start()
        first = [copy(0, me, sibling, src=v_ref)] + [copy(1 + j, me, _flip(me, k), src=v_ref) for j, k in enumerate(chips)]
        for cp in first:
            cp.start()
        passed = [copy(4 + j, _flip(me, k), sibling) for j, k in enumerate(chips)]
        for j, k in enumerate(chips):
            copy(1 + j, _flip(me, k), me).wait_recv()
            passed[j].start()
        copy(0, sibling, me).wait_recv()
        for j, k in enumerate(chips):
            copy(4 + j, _flip(sibling, k), me).wait_recv()
        for cp in first + passed:
            cp.wait_send()
        mine.wait()

    return pl.pallas_call(
        body,
        name="gather_weights",
        out_shape=jax.ShapeDtypeStruct((N_DEV, rows, cols), packed.dtype),
        in_specs=[pl.BlockSpec(memory_space=pltpu.HBM)],
        out_specs=pl.BlockSpec(memory_space=pltpu.HBM),
        scratch_shapes=[pltpu.SemaphoreType.DMA((7,)), pltpu.SemaphoreType.DMA((7,)), pltpu.SemaphoreType.DMA],
    )(packed)


def _exchange_sibling(parts):
    _, nchip, rows, cols = parts.shape

    def body(p_ref, out_ref, send_sem, recv_sem):
        me = _my_place()
        cp = pltpu.make_async_remote_copy(src_ref=p_ref.at[1 - me[2]], dst_ref=out_ref, send_sem=send_sem, recv_sem=recv_sem,
                                          device_id=_flip(me, 1), device_id_type=MESH)
        cp.start()
        cp.wait()

    return pl.pallas_call(
        body,
        name="grad_exchange_sibling",
        out_shape=jax.ShapeDtypeStruct((nchip, rows, cols), parts.dtype),
        in_specs=[pl.BlockSpec(memory_space=pltpu.HBM)],
        out_specs=pl.BlockSpec(memory_space=pltpu.HBM),
        scratch_shapes=[pltpu.SemaphoreType.DMA, pltpu.SemaphoreType.DMA],
    )(parts)


def _exchange_chips(parts):
    _, rows, cols = parts.shape

    def body(p_ref, out_ref, send_sems, recv_sems):
        me = _my_place()
        copies = []
        for j, k in enumerate((2, 4, 6)):
            to = _flip(me, k)
            copies.append(pltpu.make_async_remote_copy(
                src_ref=p_ref.at[2 * to[0] + to[1]], dst_ref=out_ref.at[j], send_sem=send_sems.at[j], recv_sem=recv_sems.at[j],
                device_id=to, device_id_type=MESH))
        for cp in copies:
            cp.start()
        for cp in copies:
            cp.wait()

    return pl.pallas_call(
        body,
        name="grad_exchange_chips",
        out_shape=jax.ShapeDtypeStruct((3, rows, cols), parts.dtype),
        in_specs=[pl.BlockSpec(memory_space=pltpu.HBM)],
        out_specs=pl.BlockSpec(memory_space=pltpu.HBM),
        scratch_shapes=[pltpu.SemaphoreType.DMA((3,)), pltpu.SemaphoreType.DMA((3,))],
    )(parts)


SUM_TILE = 592


def _sum_pairs(mine, theirs):
    nchip, rows, cols = mine.shape
    spec = pl.BlockSpec((1, SUM_TILE, cols), lambda q, t: (q, t, 0))

    def body(a_ref, b_ref, o_ref):
        o_ref[...] = (a_ref[...].astype(F32) + b_ref[...].astype(F32)).astype(BF16)

    return pl.pallas_call(body, name="grad_sum_sibling", grid=(nchip, rows // SUM_TILE), in_specs=[spec, spec], out_specs=spec,
                          out_shape=jax.ShapeDtypeStruct(mine.shape, BF16), compiler_params=_params("parallel", "parallel"))(mine, theirs)


def _sum_final(own, got):
    rows, cols = own.shape

    def body(a_ref, g_ref, o_ref):
        o_ref[...] = ((a_ref[...].astype(F32) + g_ref[0].astype(F32)) + g_ref[1].astype(F32)) + g_ref[2].astype(F32)

    return pl.pallas_call(
        body, name="grad_sum_chips", grid=(rows // SUM_TILE,),
        in_specs=[pl.BlockSpec((SUM_TILE, cols), lambda t: (t, 0)), pl.BlockSpec((3, SUM_TILE, cols), lambda t: (0, t, 0))],
        out_specs=pl.BlockSpec((SUM_TILE, cols), lambda t: (t, 0)),
        out_shape=jax.ShapeDtypeStruct((rows, cols), F32), compiler_params=_params("parallel"))(own, got)


def _ada_fwd(c_all, w, b):
    nb, _ = c_all.shape
    ncol = w.shape[1]

    def body(c_ref, w_ref, b_ref, o_ref):
        c = c_ref[...]
        act = (c * _sigmoid(c)).astype(BF16)
        o_ref[...] = jnp.dot(act, w_ref[...].astype(BF16), preferred_element_type=F32) + b_ref[...]

    return pl.pallas_call(body, name="ada_fwd", out_shape=jax.ShapeDtypeStruct((nb, ncol), F32),
                          compiler_params=pltpu.CompilerParams(vmem_limit_bytes=VMEM_LIMIT_BYTES))(c_all, w, b)


def _ada_wgrad(c_all_t, dmod_cols):
    d, nb = c_all_t.shape
    ncol = dmod_cols.shape[1]

    def body(ct_ref, dm_ref, o_ref):
        ct = ct_ref[...]
        act = (ct * _sigmoid(ct)).astype(BF16).astype(F32)
        dm = dm_ref[...].astype(BF16).astype(F32)
        acc = act[:, 0:1] * dm[0:1, :]
        for i in range(1, nb):
            acc = acc + act[:, i:i + 1] * dm[i:i + 1, :]
        o_ref[...] = acc

    return pl.pallas_call(body, name="ada_wgrad", out_shape=jax.ShapeDtypeStruct((d, ncol), F32),
                          compiler_params=pltpu.CompilerParams(vmem_limit_bytes=VMEM_LIMIT_BYTES))(c_all_t, dmod_cols)


SMALL_ROWS = 24


def _reduce_small(gathered):
    def body(g_ref, o_ref):
        acc = g_ref[0]
        for dev in range(1, N_DEV):
            acc = acc + g_ref[dev]
        o_ref[...] = acc

    return pl.pallas_call(body, name="reduce_small", out_shape=jax.ShapeDtypeStruct(gathered.shape[1:], F32))(gathered)


def _adamw(w, g, m, v, name):
    rows, cols = w.shape
    tile = rows
    for cand in (256, 128, 64, 32, 16, 8):
        if rows % cand == 0 and rows > cand:
            tile = cand
            break
    spec = pl.BlockSpec((tile, cols), lambda t: (t, 0))
    bc1 = 1.0 - ADAM_B1 ** ADAM_STEP
    bc2 = 1.0 - ADAM_B2 ** ADAM_STEP

    def body(w_ref, g_ref, m_ref, v_ref, d_ref, nm_ref, nv_ref):
        g_ = g_ref[...]
        nm = ADAM_B1 * m_ref[...] + (1.0 - ADAM_B1) * g_
        nv = ADAM_B2 * v_ref[...] + (1.0 - ADAM_B2) * (g_ * g_)
        d_ref[...] = -ADAM_LR * ((nm / bc1) / (jnp.sqrt(nv / bc2) + ADAM_EPS) + ADAM_WD * w_ref[...])
        nm_ref[...] = nm
        nv_ref[...] = nv

    shp = jax.ShapeDtypeStruct((rows, cols), F32)
    return pl.pallas_call(body, name=name, grid=(rows // tile,), in_specs=[spec] * 4, out_specs=[spec] * 3, out_shape=[shp] * 3,
                          compiler_params=_params("parallel"))(w, g, m, v)


_WEIGHTS = ("w_ada", "b_ada", "w_in", "sinks", "w_branch_a", "w_branch_b", "w_o", "ln1_g", "ln1_b", "w_gate_up", "w_down",
            "ln2_g", "ln2_b")
_TRANSPOSED = ("w_in", "w_branch_b", "w_gate_up")


def _pack_shard(name, w):
    w = w.astype(BF16)
    if name in _TRANSPOSED:
        w = w.T
    return w.reshape(-1, D_MODEL)


def _unpack_full(name, slab):
    if name == "w_branch_b":
        return slab.reshape(N_DEV * 128, 512)
    return slab.reshape(-1, D_MODEL)


def kernel(x, c, positions, w_ada, b_ada, w_in, sinks, w_branch_a, w_branch_b, w_o, ln1_g, ln1_b, w_gate_up, w_down, ln2_g, ln2_b, loss_target, m_w_ada, m_b_ada, m_w_in, m_sinks, m_w_branch_a, m_w_branch_b, m_w_o, m_ln1_g, m_ln1_b, m_w_gate_up, m_w_down, m_ln2_g, m_ln2_b, v_w_ada, v_b_ada, v_w_in, v_sinks, v_w_branch_a, v_w_branch_b, v_w_o, v_ln1_g, v_ln1_b, v_w_gate_up, v_w_down, v_ln2_g, v_ln2_b):
    weights = dict(w_ada=w_ada, b_ada=b_ada, w_in=w_in, sinks=sinks, w_branch_a=w_branch_a, w_branch_b=w_branch_b, w_o=w_o,
                   ln1_g=ln1_g, ln1_b=ln1_b, w_gate_up=w_gate_up, w_down=w_down, ln2_g=ln2_g, ln2_b=ln2_b)
    m_in = dict(w_ada=m_w_ada, b_ada=m_b_ada, w_in=m_w_in, sinks=m_sinks, w_branch_a=m_w_branch_a, w_branch_b=m_w_branch_b,
                w_o=m_w_o, ln1_g=m_ln1_g, ln1_b=m_ln1_b, w_gate_up=m_w_gate_up, w_down=m_w_down, ln2_g=m_ln2_g, ln2_b=m_ln2_b)
    v_in = dict(w_ada=v_w_ada, b_ada=v_b_ada, w_in=v_w_in, sinks=v_sinks, w_branch_a=v_w_branch_a, w_branch_b=v_w_branch_b,
                w_o=v_w_o, ln1_g=v_ln1_g, ln1_b=v_ln1_b, w_gate_up=v_w_gate_up, w_down=v_w_down, ln2_g=v_ln2_g, ln2_b=v_ln2_b)
    bsz = x.shape[0]
    me = _index(_my_place())
    ada_cols = w_ada.shape[2]

    c_all = _gather_small(jnp.pad(c, ((0, 8 - bsz), (0, 0))), "gather_c")[:, :bsz].reshape(N_DEV * bsz, D_MODEL)
    b_cols = lax.dynamic_slice_in_dim(b_ada, me * ada_cols, ada_cols, axis=1)
    mod_cols = _ada_fwd(c_all, w_ada[0], b_cols)
    mod_all = _gather_small(mod_cols, "gather_mod").transpose(1, 0, 2).reshape(N_DEV * bsz, 6, D_MODEL)
    mod = jnp.pad(lax.dynamic_slice_in_dim(mod_all, me * bsz, bsz, axis=0), ((0, 0), (0, 2), (0, 0)))

    packed = jnp.concatenate([_pack_shard(n, weights[n][0]) for n, _ in PACK_ROWS], axis=0)
    gathered = _gather_weights(packed)
    full, off = {}, 0
    for n, r in PACK_ROWS:
        full[n] = _unpack_full(n, gathered[:, off:off + r])
        off += r

    loss_part, grad_x, dmod, small, grads = _layer_step(
        x, mod, positions, sinks[0], ln1_g, ln1_b, ln2_g, ln2_b, loss_target, full["w_in"], full["w_branch_a"],
        full["w_branch_b"], full["w_o"], full["w_gate_up"], full["w_down"])
    loss = lax.psum(loss_part, MESH_AXES)

    slabs = jnp.concatenate([grads[n].reshape(N_DEV, r, D_MODEL) for n, r in PACK_ROWS], axis=1)
    parts = slabs.reshape(4, 2, PACK_TOTAL, D_MODEL).transpose(1, 0, 2, 3)
    my_c = lax.axis_index("c")
    my_chip = 2 * lax.axis_index("x") + lax.axis_index("y")
    from_sibling = _exchange_sibling(parts)
    chip_sum = _sum_pairs(lax.dynamic_index_in_dim(parts, my_c, 0, keepdims=False), from_sibling)
    from_chips = _exchange_chips(chip_sum)
    g_packed = _sum_final(lax.dynamic_index_in_dim(chip_sum, my_chip, 0, keepdims=False), from_chips)
    g_w, off = {}, 0
    for n, r in PACK_ROWS:
        part = g_packed[off:off + r]
        off += r
        if n == "w_branch_b":
            part = part.reshape(128, 512)
        g_w[n] = part.T if n in _TRANSPOSED else part

    rows = jnp.concatenate([dmod.reshape(bsz * 6, D_MODEL), small, jnp.zeros((SMALL_ROWS - bsz * 6 - 5, D_MODEL), F32)], axis=0)
    small_all = _gather_small(rows, "gather_small")
    sums = _reduce_small(small_all)
    dmod_all = small_all[:, :bsz * 6].reshape(N_DEV * bsz, 6 * D_MODEL)
    g_w["b_ada"] = functools.reduce(jnp.add, [sums[6 * i:6 * i + 6] for i in range(bsz)]).reshape(1, 6 * D_MODEL)
    g_w["ln1_g"], g_w["ln1_b"], g_w["ln2_g"], g_w["ln2_b"] = (sums[12 + i][None] for i in range(4))
    g_w["sinks"] = sums[16][::HEAD_DIM][None]
    dmod_cols = lax.dynamic_slice_in_dim(dmod_all, me * ada_cols, ada_cols, axis=1)
    g_w["w_ada"] = _ada_wgrad(c_all.T, dmod_cols)

    out_g, out_d, out_m, out_v = [], [], [], []
    for n in _WEIGHTS:
        w2, m2, v2 = (t[n][0] if t[n].ndim == 3 else t[n] for t in (weights, m_in, v_in))
        shape = weights[n].shape
        dlt, nm, nv = _adamw(w2, g_w[n], m2, v2, "adamw_" + n)
        out_g.append(g_w[n].reshape(shape))
        out_d.append(dlt.reshape(shape))
        out_m.append(nm.reshape(shape))
        out_v.append(nv.reshape(shape))
    return (loss, grad_x, *out_g, *out_d, *out_m, *out_v)
```

```python
import functools

import jax
import jax.numpy as jnp
from jax import lax
from jax.experimental import pallas as pl
from jax.experimental.pallas import tpu as pltpu

F32 = jnp.float32
BF16 = jnp.bfloat16

D_MODEL = 1024
HEAD_DIM = 64
A_Q_HEADS = 16
A_WINDOW = 128
B_PATTERNS = ((128, 1), (512, 4), (2048, 16))
B_HEADS_PER_GROUP = 8
D_FF = 2816
QBLOCK = 128
ROPE_THETA = 10000.0
LN_EPS = 1e-5
DEEPNORM_ALPHA = 2.0 ** 0.25
NEG_INF = -1e30
ADAM_LR, ADAM_B1, ADAM_B2, ADAM_EPS, ADAM_WD, ADAM_STEP = 0.001, 0.9, 0.999, 1e-08, 0.01, 10

N_DEV = 8
MESH_AXES = ("x", "y", "c")
LANES = 128
VMEM_LIMIT_BYTES = 56 * 1024 * 1024
MESH = pl.DeviceIdType.MESH

OFF_QA, OFF_KVA, OFF_QKVB, OFF_GAB = 0, 1024, 1280, 5888
GROUP_IN = (("w_in", 992),)
GROUP_REST = (("w_branch_a", 128), ("w_branch_b", 64), ("w_o", 128), ("w_gate_up", 704), ("w_down", 352))


def _params(*sem):
    return pltpu.CompilerParams(dimension_semantics=sem, vmem_limit_bytes=VMEM_LIMIT_BYTES)


def _sigmoid(x):
    return 1.0 / (1.0 + jnp.exp(-x))


_DIMS = {"nn": (((1,), (0,)), ((), ())), "nt": (((1,), (1,)), ((), ())), "tn": (((0,), (0,)), ((), ()))}


def _matmul(a, b, *, mode, out_dtype, tm, tn, tk, name, n=None, b_off=0, token=None):
    if mode == "nn":
        (m, k), nn_ = a.shape, b.shape[1]
    elif mode == "nt":
        (m, k), nn_ = a.shape, (b.shape[0] if n is None else n)
    else:
        (k, m), nn_ = a.shape, b.shape[1]
    assert m % tm == 0 and nn_ % tn == 0 and k % tk == 0 and b_off % tn == 0, (name, m, nn_, k)
    nk = k // tk
    joff = b_off // tn
    if mode == "nn":
        a_spec = pl.BlockSpec((tm, tk), lambda i, j, kk: (i, kk))
        b_spec = pl.BlockSpec((tk, tn), lambda i, j, kk: (kk, j))
    elif mode == "nt":
        a_spec = pl.BlockSpec((tm, tk), lambda i, j, kk: (i, kk))
        b_spec = pl.BlockSpec((tn, tk), lambda i, j, kk: (j + joff, kk))
    else:
        a_spec = pl.BlockSpec((tk, tm), lambda i, j, kk: (kk, i))
        b_spec = pl.BlockSpec((tk, tn), lambda i, j, kk: (kk, j))
    dims = _DIMS[mode]
    has_token = token is not None

    def body(*refs):
        a_ref, b_ref = refs[:2]
        o_ref, acc_ref = refs[-2:]
        kk = pl.program_id(2)
        part = lax.dot_general(a_ref[...].astype(BF16), b_ref[...].astype(BF16), dims, preferred_element_type=F32)
        if nk == 1:
            o_ref[...] = part.astype(o_ref.dtype)
        else:
            @pl.when(kk == 0)
            def _():
                acc_ref[...] = part

            @pl.when(kk > 0)
            def _():
                acc_ref[...] += part

            @pl.when(kk == nk - 1)
            def _():
                o_ref[...] = acc_ref[...].astype(o_ref.dtype)

    in_specs, args = [a_spec, b_spec], [a, b]
    if has_token:
        in_specs.append(pl.BlockSpec(token.shape, lambda i, j, kk: (0, 0)))
        args.append(token)
    return pl.pallas_call(
        body,
        name=name,
        grid=(m // tm, nn_ // tn, nk),
        in_specs=in_specs,
        out_specs=pl.BlockSpec((tm, tn), lambda i, j, kk: (i, j)),
        out_shape=jax.ShapeDtypeStruct((m, nn_), out_dtype),
        scratch_shapes=[pltpu.VMEM((tm, tn) if nk > 1 else (8, LANES), F32)],
        compiler_params=_params("parallel", "parallel", "arbitrary"),
    )(*args)


def _proj_rope(a, bt, cos, sin, *, n, b_off, rope_cols, tm, tn, name):
    m, k = a.shape
    assert m % tm == 0 and n % tn == 0 and b_off % tn == 0 and rope_cols % tn == 0, name
    joff = b_off // tn
    nrope = rope_cols // tn

    def body(a_ref, b_ref, c_ref, s_ref, o_ref):
        acc = lax.dot_general(a_ref[...], b_ref[...], _DIMS["nt"], preferred_element_type=F32)
        j = pl.program_id(1)

        @pl.when(j < nrope)
        def _():
            o_ref[...] = _rope(acc, c_ref[...], s_ref[...])

        @pl.when(j >= nrope)
        def _():
            o_ref[...] = acc

    table = pl.BlockSpec((tm, LANES), lambda i, j: (i, 0))
    return pl.pallas_call(
        body,
        name=name,
        grid=(m // tm, n // tn),
        in_specs=[pl.BlockSpec((tm, k), lambda i, j: (i, 0)), pl.BlockSpec((tn, k), lambda i, j: (j + joff, 0)), table, table],
        out_specs=pl.BlockSpec((tm, tn), lambda i, j: (i, j)),
        out_shape=jax.ShapeDtypeStruct((m, n), F32),
        compiler_params=_params("parallel", "parallel"),
    )(a, bt, cos, sin)


ROW_TILE = 256


def _rows(width, col=0):
    return pl.BlockSpec((1, ROW_TILE, width), lambda b, t: (b, t, col))


def _per_batch(nrows, width):
    return pl.BlockSpec((1, nrows, width), lambda b, t: (b, 0, 0))


def _whole(shape):
    return pl.BlockSpec(shape, lambda b, t: (0,) * len(shape))


def _row_call(body, name, bsz, seq, in_specs, out_specs, out_shape, accumulates=False):
    return pl.pallas_call(
        body,
        name=name,
        grid=(bsz, seq // ROW_TILE),
        in_specs=in_specs,
        out_specs=out_specs,
        out_shape=out_shape,
        compiler_params=_params("parallel", "arbitrary" if accumulates else "parallel"),
    )


def _acc_rows(acc_ref, first, rows):
    @pl.when(first)
    def _():
        acc_ref[...] = jnp.zeros_like(acc_ref)

    for r, val in enumerate(rows):
        acc_ref[0, r:r + 1, :] += val


def _colsum(v):
    return jnp.sum(v, axis=0, keepdims=True)


def _ln_stats(z):
    mu = jnp.mean(z, axis=-1, keepdims=True)
    zc = z - mu
    var = jnp.mean(zc * zc, axis=-1, keepdims=True)
    rstd = lax.rsqrt(var + LN_EPS)
    return zc * rstd, rstd


def _ln_bwd(dxhat, xhat, rstd):
    m1 = jnp.mean(dxhat, axis=-1, keepdims=True)
    m2 = jnp.mean(dxhat * xhat, axis=-1, keepdims=True)
    return rstd * (dxhat - m1 - xhat * m2)


def _modulate_in(x, mod):
    bsz, seq, d = x.shape

    def body(x_ref, mod_ref, u_ref):
        u_ref[0] = (x_ref[0] * (1.0 + mod_ref[0, 1:2, :]) + mod_ref[0, 0:1, :]).astype(BF16)

    return _row_call(body, "modulate_in", bsz, seq, [_rows(d), _per_batch(8, d)], _rows(d),
                     jax.ShapeDtypeStruct((bsz, seq, d), BF16))(x, mod)


def _gate_merge(gab, ya, yb):
    bsz, seq, d = ya.shape

    def body(ga_ref, gb_ref, ya_ref, yb_ref, o_ref):
        o_ref[0] = (_sigmoid(ga_ref[0]) * ya_ref[0] + _sigmoid(gb_ref[0]) * yb_ref[0]).astype(BF16)

    return _row_call(body, "gate_merge", bsz, seq, [_rows(d, 0), _rows(d, 1), _rows(d), _rows(d)], _rows(d),
                     jax.ShapeDtypeStruct((bsz, seq, d), BF16))(gab, gab, ya, yb)


def _ln1_fwd(x, y1, mod, g, b):
    bsz, seq, d = x.shape

    def body(x_ref, y_ref, mod_ref, g_ref, b_ref, h_ref, u_ref):
        z = DEEPNORM_ALPHA * x_ref[0] + (1.0 + mod_ref[0, 2:3, :]) * y_ref[0]
        xhat, _ = _ln_stats(z)
        h = xhat * g_ref[...] + b_ref[...]
        h_ref[0] = h
        u_ref[0] = (h * (1.0 + mod_ref[0, 4:5, :]) + mod_ref[0, 3:4, :]).astype(BF16)

    return _row_call(body, "ln1_fwd", bsz, seq,
                     [_rows(d), _rows(d), _per_batch(8, d), _whole((1, d)), _whole((1, d))],
                     [_rows(d), _rows(d)],
                     [jax.ShapeDtypeStruct((bsz, seq, d), F32), jax.ShapeDtypeStruct((bsz, seq, d), BF16)])(x, y1, mod, g, b)


def _silu_mul(h):
    bsz, seq, _ = h.shape

    def body(hg_ref, hu_ref, a_ref):
        hg = hg_ref[0]
        a_ref[0] = (hg * _sigmoid(hg) * hu_ref[0]).astype(BF16)

    return _row_call(body, "silu_mul", bsz, seq, [_rows(D_FF, 0), _rows(D_FF, 1)], _rows(D_FF),
                     jax.ShapeDtypeStruct((bsz, seq, D_FF), BF16))(h, h)


def _ln2_loss_bwd(h1, y2, mod, g, b, target):
    bsz, seq, d = h1.shape

    def body(h_ref, y_ref, mod_ref, g_ref, b_ref, t_ref, dy_ref, dh_ref, acc_ref):
        y = y_ref[0]
        gate = 1.0 + mod_ref[0, 5:6, :]
        z = DEEPNORM_ALPHA * h_ref[0] + gate * y
        xhat, rstd = _ln_stats(z)
        diff = xhat * g_ref[...] + b_ref[...] - t_ref[0]
        loss = 0.5 * jnp.sum(jnp.sum(diff * diff, axis=-1, keepdims=True) / d, axis=0, keepdims=True)
        dout = diff / d
        dz = _ln_bwd(dout * g_ref[...], xhat, rstd)
        dy_ref[0] = (gate * dz).astype(BF16)
        dh_ref[0] = DEEPNORM_ALPHA * dz
        _acc_rows(acc_ref, pl.program_id(1) == 0,
                  [_colsum(dout * xhat), _colsum(dout), _colsum(dz * y), jnp.broadcast_to(loss, (1, d))])

    return _row_call(body, "ln2_loss_bwd", bsz, seq,
                     [_rows(d), _rows(d), _per_batch(8, d), _whole((1, d)), _whole((1, d)), _rows(d)],
                     [_rows(d), _rows(d), _per_batch(8, d)],
                     [jax.ShapeDtypeStruct((bsz, seq, d), BF16), jax.ShapeDtypeStruct((bsz, seq, d), F32),
                      jax.ShapeDtypeStruct((bsz, 8, d), F32)], accumulates=True)(h1, y2, mod, g, b, target)


def _silu_mul_bwd(da, h):
    bsz, seq, _ = h.shape

    def body(da_ref, hg_ref, hu_ref, dh_ref):
        hg, da_ = hg_ref[0], da_ref[0]
        sg = _sigmoid(hg)
        dh_ref[0, :, :D_FF] = (da_ * hu_ref[0] * (sg * (1.0 + hg * (1.0 - sg)))).astype(BF16)
        dh_ref[0, :, D_FF:] = (da_ * (hg * sg)).astype(BF16)

    return _row_call(body, "silu_mul_bwd", bsz, seq, [_rows(D_FF), _rows(D_FF, 0), _rows(D_FF, 1)], _rows(2 * D_FF),
                     jax.ShapeDtypeStruct((bsz, seq, 2 * D_FF), BF16))(da, h, h)


def _ln1_bwd(du2, dh1a, x, y1, mod, g, b):
    bsz, seq, d = x.shape

    def body(du_ref, dh_ref, x_ref, y_ref, mod_ref, g_ref, b_ref, dy_ref, dx_ref, acc_ref):
        y, du = y_ref[0], du_ref[0]
        gate = 1.0 + mod_ref[0, 2:3, :]
        z = DEEPNORM_ALPHA * x_ref[0] + gate * y
        xhat, rstd = _ln_stats(z)
        h1 = xhat * g_ref[...] + b_ref[...]
        dh1 = dh_ref[0] + du * (1.0 + mod_ref[0, 4:5, :])
        dz = _ln_bwd(dh1 * g_ref[...], xhat, rstd)
        dy_ref[0] = (gate * dz).astype(BF16)
        dx_ref[0] = DEEPNORM_ALPHA * dz
        _acc_rows(acc_ref, pl.program_id(1) == 0,
                  [_colsum(dh1 * xhat), _colsum(dh1), _colsum(dz * y), _colsum(du * h1), _colsum(du)])

    return _row_call(body, "ln1_bwd", bsz, seq,
                     [_rows(d), _rows(d), _rows(d), _rows(d), _per_batch(8, d), _whole((1, d)), _whole((1, d))],
                     [_rows(d), _rows(d), _per_batch(8, d)],
                     [jax.ShapeDtypeStruct((bsz, seq, d), BF16), jax.ShapeDtypeStruct((bsz, seq, d), F32),
                      jax.ShapeDtypeStruct((bsz, 8, d), F32)], accumulates=True)(du2, dh1a, x, y1, mod, g, b)


def _gate_merge_bwd(dm, gab, ya, yb):
    bsz, seq, d = ya.shape

    def body(dm_ref, ga_ref, gb_ref, ya_ref, yb_ref, dya_ref, dyb_ref, dg_ref):
        dm_ = dm_ref[0]
        sa, sb = _sigmoid(ga_ref[0]), _sigmoid(gb_ref[0])
        dya_ref[0] = (dm_ * sa).astype(BF16)
        dyb_ref[0] = (dm_ * sb).astype(BF16)
        dg_ref[0, :, :d] = (dm_ * ya_ref[0] * sa * (1.0 - sa)).astype(BF16)
        dg_ref[0, :, d:] = (dm_ * yb_ref[0] * sb * (1.0 - sb)).astype(BF16)

    return _row_call(body, "gate_merge_bwd", bsz, seq,
                     [_rows(d), _rows(d, 0), _rows(d, 1), _rows(d), _rows(d)],
                     [_rows(d), _rows(d), _rows(2 * d)],
                     [jax.ShapeDtypeStruct((bsz, seq, d), BF16), jax.ShapeDtypeStruct((bsz, seq, d), BF16),
                      jax.ShapeDtypeStruct((bsz, seq, 2 * d), BF16)])(dm, gab, gab, ya, yb)


def _grad_x(dxa, du1, x, mod):
    bsz, seq, d = x.shape

    def body(dxa_ref, du_ref, x_ref, mod_ref, gx_ref, acc_ref):
        du = du_ref[0]
        gx_ref[0] = dxa_ref[0] + du * (1.0 + mod_ref[0, 1:2, :])
        _acc_rows(acc_ref, pl.program_id(1) == 0, [_colsum(du * x_ref[0]), _colsum(du)])

    return _row_call(body, "grad_x", bsz, seq, [_rows(d), _rows(d), _rows(d), _per_batch(8, d)],
                     [_rows(d), _per_batch(8, d)],
                     [jax.ShapeDtypeStruct((bsz, seq, d), F32), jax.ShapeDtypeStruct((bsz, 8, d), F32)],
                     accumulates=True)(dxa, du1, x, mod)


def _segsum64(v):
    rows, width = v.shape
    ri = lax.broadcasted_iota(jnp.int32, (LANES, LANES), 0) // HEAD_DIM
    ci = lax.broadcasted_iota(jnp.int32, (LANES, LANES), 1) // HEAD_DIM
    ones = jnp.where(ri == ci, 1.0, 0.0).astype(BF16)
    out = []
    for c in range(width // LANES):
        part = v[:, c * LANES:(c + 1) * LANES]
        hi = part.astype(BF16)
        lo = (part - hi.astype(F32)).astype(BF16)
        out.append(jnp.dot(hi, ones, preferred_element_type=F32) + jnp.dot(lo, ones, preferred_element_type=F32))
    return jnp.concatenate(out, axis=1) if len(out) > 1 else out[0]


def _merge_b(os_, ls_):
    bsz, seq, w = os_[0].shape

    def body(o0, o1, o2, l0, l1, l2, ob_ref):
        ls = [l0[0], l1[0], l2[0]]
        mx = jnp.maximum(jnp.maximum(ls[0], ls[1]), ls[2])
        es = [jnp.exp(l - mx) for l in ls]
        den = es[0] + es[1] + es[2]
        ob_ref[0] = ((es[0] / den) * o0[0] + (es[1] / den) * o1[0] + (es[2] / den) * o2[0]).astype(BF16)

    return _row_call(body, "merge_b", bsz, seq, [_rows(w)] * 6, _rows(w),
                     jax.ShapeDtypeStruct((bsz, seq, w), BF16))(*os_, *ls_)


def _merge_b_bwd(dob, os_, ls_):
    bsz, seq, w = os_[0].shape

    def body(dob_ref, o0, o1, o2, l0, l1, l2, do0, do1, do2, dd0, dd1, dd2):
        dob_ = dob_ref[0]
        ls = [l0[0], l1[0], l2[0]]
        mx = jnp.maximum(jnp.maximum(ls[0], ls[1]), ls[2])
        es = [jnp.exp(l - mx) for l in ls]
        den = es[0] + es[1] + es[2]
        ws = [e / den for e in es]
        dws = [_segsum64(dob_ * o[0]) for o in (o0, o1, o2)]
        mean = ws[0] * dws[0] + ws[1] * dws[1] + ws[2] * dws[2]
        for wg, dwg, do_ref, dd_ref in zip(ws, dws, (do0, do1, do2), (dd0, dd1, dd2)):
            do_ref[0] = wg * dob_
            dd_ref[0] = -wg * mean

    shp = jax.ShapeDtypeStruct((bsz, seq, w), F32)
    return _row_call(body, "merge_b_bwd", bsz, seq, [_rows(w)] * 7, [_rows(w)] * 6, [shp] * 6)(dob, *os_, *ls_)


def _delta_a(doa, oa, lse_a, sinks_exp):
    bsz, seq, w = oa.shape

    def body(do_ref, o_ref, l_ref, s_ref, dd_ref, acc_ref):
        dd = -_segsum64(do_ref[0] * o_ref[0])
        dd_ref[0] = dd
        _acc_rows(acc_ref, pl.program_id(1) == 0, [_colsum(dd * jnp.exp(s_ref[...] - l_ref[0]))])

    return _row_call(body, "delta_a", bsz, seq, [_rows(w), _rows(w), _rows(w), _whole((1, w))],
                     [_rows(w), _per_batch(8, w)],
                     [jax.ShapeDtypeStruct((bsz, seq, w), F32), jax.ShapeDtypeStruct((bsz, 8, w), F32)],
                     accumulates=True)(doa, oa, lse_a, sinks_exp)


def _swap_halves(v):
    lane = lax.broadcasted_iota(jnp.int32, v.shape, 1)
    return jnp.where((lane % HEAD_DIM) < HEAD_DIM // 2, pltpu.roll(v, LANES - HEAD_DIM // 2, 1),
                     pltpu.roll(v, HEAD_DIM // 2, 1))


def _rope(v, cos, sin, sign=1.0):
    out = []
    for c in range(v.shape[1] // LANES):
        part = v[:, c * LANES:(c + 1) * LANES]
        out.append(part * cos + sign * (_swap_halves(part) * sin))
    return jnp.concatenate(out, axis=1) if len(out) > 1 else out[0]


def _half_mask(shape, half):
    lane = lax.broadcasted_iota(jnp.int32, shape, len(shape) - 1) % LANES
    return (lane < HEAD_DIM) if half == 0 else (lane >= HEAD_DIM)


def _dup_half(v, half):
    return jnp.where(_half_mask(v.shape, half), v, pltpu.roll(v, HEAD_DIM, 1))


def _fold_halves(v):
    return v + pltpu.roll(v, HEAD_DIM, 1)


def _pick_halves(lo_rows, hi_rows):
    return jnp.where(_half_mask(lo_rows.shape, 0), lo_rows, hi_rows)


def _stack_masked(v, pairs):
    parts = []
    for c in pairs:
        pair = v[:, c * LANES:(c + 1) * LANES]
        parts += [jnp.where(_half_mask(pair.shape, half), pair, 0.0) for half in (0, 1)]
    return jnp.concatenate(parts, axis=0)


def _stack_pair_cols(v, pairs):
    return jnp.concatenate([v[:, c * LANES + half * HEAD_DIM:c * LANES + half * HEAD_DIM + 1] for c in pairs for half in (0, 1)],
                           axis=0)


ATTN_UNITS = 16


def _class_rows(r):
    return [pl.ds(0, QBLOCK)] if r == 1 else [pl.ds(rho, QBLOCK, stride=r) for rho in range(r)]


def _band_mask(nrows, nk, blk, n_back, has_prev):
    qi = lax.broadcasted_iota(jnp.int32, (nrows, nk), 0) % QBLOCK
    ki = lax.broadcasted_iota(jnp.int32, (nrows, nk), 1)
    if has_prev:
        dist = qi + QBLOCK - ki
        return (dist >= 0) & (dist <= n_back) & ((ki >= QBLOCK) | (blk > 0))
    dist = qi - ki
    return (dist >= 0) & (dist <= n_back)


def _attn_fwd(q_arr, k_arr, v_arr, *, name, npair, gqa, q_col, k_col, v_col, nchunk, r, n_back, sinks=None):
    bsz, seq, _ = q_arr.shape
    rr = QBLOCK * r
    nblk = seq // rr
    qw = npair * LANES
    kw = LANES if gqa else qw
    has_prev = nblk > 1
    has_sink = sinks is not None
    scale = HEAD_DIM ** -0.5

    def body(*refs):
        refs = list(refs)
        q_ref, kc_ref, vc_ref = refs[:3]
        pos = 3
        if has_prev:
            kp_ref, vp_ref = refs[pos:pos + 2]
            pos += 2
        if has_sink:
            sink_ref = refs[pos]
            pos += 1
        o_ref, lse_ref = refs[pos:pos + 2]
        blk = pl.program_id(2)
        nk = (2 if has_prev else 1) * QBLOCK
        valid = _band_mask(QBLOCK, nk, blk, n_back, has_prev)
        per = npair // 2
        classes = _class_rows(r)
        step = max(1, ATTN_UNITS // (2 * npair))
        for first in range(0, len(classes), step):
            batch = classes[first:first + step]
            units = []
            for ci, rows in enumerate(batch):
                q = q_ref[0, rows, :] * scale
                k, v = kc_ref[0, rows, :], vc_ref[0, rows, :]
                if has_prev:
                    k = jnp.concatenate([kp_ref[0, rows, :], k], axis=0)
                    v = jnp.concatenate([vp_ref[0, rows, :], v], axis=0)
                if gqa:
                    kdup = [_dup_half(k, hk).astype(BF16) for hk in range(2)]
                    vdup = [_dup_half(v, hk) for hk in range(2)]
                for c in range(npair):
                    sl = slice(c * LANES, (c + 1) * LANES)
                    qc = q[:, sl]
                    kc, vc = (kdup[c // per], vdup[c // per]) if gqa else (k[:, sl].astype(BF16), v[:, sl])
                    for half in (0, 1):
                        qm = jnp.where(_half_mask(qc.shape, half), qc, 0.0).astype(BF16)
                        vm = jnp.where(_half_mask(vc.shape, half), vc, 0.0).astype(BF16)
                        s = lax.dot_general(qm, kc, _DIMS["nt"], preferred_element_type=F32)
                        units.append(dict(ci=ci, c=c, half=half, s=s, vm=vm, sk=sink_ref[2 * c + half] if has_sink else None))
            for u in units:
                s = jnp.where(valid, u["s"], NEG_INF)
                m = jnp.max(s, axis=1, keepdims=True)
                if has_sink:
                    m = jnp.maximum(m, u["sk"])
                p = jnp.exp(s - m)
                den = jnp.sum(p, axis=1, keepdims=True)
                if has_sink:
                    den = den + jnp.exp(u["sk"] - m)
                u.update(p=p.astype(BF16), den=den, lse=m + jnp.log(den))
            for u in units:
                u["o"] = jnp.dot(u["p"], u["vm"], preferred_element_type=F32) / u["den"]
            for ci, rows in enumerate(batch):
                outs, lses = [None] * npair, [None] * npair
                for u in units:
                    if u["ci"] != ci:
                        continue
                    c, o = u["c"], u["o"]
                    lse = jnp.broadcast_to(u["lse"], o.shape)
                    outs[c] = o if u["half"] == 0 else outs[c] + o
                    lses[c] = lse if u["half"] == 0 else _pick_halves(lses[c], lse)
                o_ref[0, rows, :] = jnp.concatenate(outs, axis=1) if npair > 1 else outs[0]
                lse_ref[0, rows, :] = jnp.concatenate(lses, axis=1) if npair > 1 else lses[0]

    def cur(width, col0):
        return pl.BlockSpec((1, rr, width), lambda b, c, i: (b, i, col0 + c))

    def prev(width, col0):
        return pl.BlockSpec((1, rr, width), lambda b, c, i: (b, jnp.maximum(i - 1, 0), col0 + c))

    in_specs = [cur(qw, q_col), cur(kw, k_col), cur(kw, v_col)]
    args = [q_arr, k_arr, v_arr]
    if has_prev:
        in_specs += [prev(kw, k_col), prev(kw, v_col)]
        args += [k_arr, v_arr]
    if has_sink:
        in_specs.append(pl.BlockSpec(memory_space=pltpu.SMEM))
        args.append(sinks)
    return pl.pallas_call(
        body,
        name=name,
        grid=(bsz, nchunk, nblk),
        in_specs=in_specs,
        out_specs=[pl.BlockSpec((1, rr, qw), lambda b, c, i: (b, i, c))] * 2,
        out_shape=[jax.ShapeDtypeStruct((bsz, seq, nchunk * qw), F32)] * 2,
        compiler_params=_params("parallel", "parallel", "parallel"),
    )(*args)


def _attn_bwd(q_arr, k_arr, v_arr, cos, sin, do, lse, dd, *, name, npair, gqa, q_col, k_col, v_col, nchunk, r, n_back,
              token=None):
    bsz, seq, _ = q_arr.shape
    rr = QBLOCK * r
    nblk = seq // rr
    qw = npair * LANES
    kw = LANES if gqa else qw
    has_next = nblk > 1
    has_token = token is not None
    scale = HEAD_DIM ** -0.5

    def body(*refs):
        refs = list(refs)
        k_ref, v_ref, c_ref, s_ref = refs[:4]
        tile_refs = [refs[4:8]]
        pos = 8
        if has_next:
            tile_refs.append(refs[pos:pos + 4])
            pos += 4
        if has_token:
            pos += 1
        dq_ref, dk_ref, dv_ref = refs[pos:pos + 3]
        carry_ref = refs[pos + 3]
        blk = pl.program_id(2)
        if has_next:
            @pl.when(blk == 0)
            def _():
                carry_ref[...] = jnp.zeros_like(carry_ref)

        nrows = (npair if gqa else 1) * QBLOCK
        qi = lax.broadcasted_iota(jnp.int32, (nrows, QBLOCK), 0) % QBLOCK
        ki = lax.broadcasted_iota(jnp.int32, (nrows, QBLOCK), 1)
        valids = [qi >= ki, (qi + QBLOCK - ki <= n_back) & (blk + 1 < nblk)]
        per = npair // 2
        ntile = len(tile_refs)
        cat = lambda parts: jnp.concatenate(parts, axis=1) if len(parts) > 1 else parts[0]
        classes = _class_rows(r)
        step = max(1, ATTN_UNITS // (ntile * (2 if gqa else 2 * npair)))
        for first in range(0, len(classes), step):
            batch = classes[first:first + step]
            units = []
            for ci, rows in enumerate(batch):
                tiles = [(q_ref[0, rows, :] * scale, do_ref[0, rows, :], l_ref[0, rows, :], d_ref[0, rows, :])
                         for q_ref, do_ref, l_ref, d_ref in tile_refs]
                k, v = k_ref[0, rows, :], v_ref[0, rows, :]
                if gqa:
                    for hk in range(2):
                        pairs = list(range(hk * per, (hk + 1) * per))
                        kd, vd = _dup_half(k, hk).astype(BF16), _dup_half(v, hk).astype(BF16)
                        for t, (q, do_, l_, d_) in enumerate(tiles):
                            units.append(dict(ci=ci, t=t, hk=hk, pairs=pairs, qs=_stack_masked(q, pairs).astype(BF16),
                                              dos=_stack_masked(do_, pairs).astype(BF16), lcol=_stack_pair_cols(l_, pairs),
                                              dcol=_stack_pair_cols(d_, pairs), kmat=kd, vmat=vd, kdq=kd))
                else:
                    for c in range(npair):
                        sl = slice(c * LANES, (c + 1) * LANES)
                        kc, vcb = k[:, sl], v[:, sl].astype(BF16)
                        kcb = kc.astype(BF16)
                        for t, (q, do_, l_, d_) in enumerate(tiles):
                            for half in (0, 1):
                                hm = _half_mask(kc.shape, half)
                                col = c * LANES + half * HEAD_DIM
                                units.append(dict(ci=ci, t=t, c=c, half=half, qs=jnp.where(hm, q[:, sl], 0.0).astype(BF16),
                                                  dos=jnp.where(hm, do_[:, sl], 0.0).astype(BF16), lcol=l_[:, col:col + 1],
                                                  dcol=d_[:, col:col + 1], kmat=kcb, vmat=vcb,
                                                  kdq=jnp.where(hm, kc, 0.0).astype(BF16)))
            for u in units:
                u["s"] = lax.dot_general(u["qs"], u["kmat"], _DIMS["nt"], preferred_element_type=F32)
                u["dp"] = lax.dot_general(u["dos"], u["vmat"], _DIMS["nt"], preferred_element_type=F32)
            for u in units:
                p = jnp.exp(jnp.where(valids[u["t"]], u["s"], NEG_INF) - u["lcol"])
                u["ds"] = (p * (u["dp"] + u["dcol"])).astype(BF16)
                u["p"] = p.astype(BF16)
            for u in units:
                u["dv"] = lax.dot_general(u["p"], u["dos"], _DIMS["tn"], preferred_element_type=F32)
                u["dk"] = lax.dot_general(u["ds"], u["qs"], _DIMS["tn"], preferred_element_type=F32)
                u["dq"] = jnp.dot(u["ds"], u["kdq"], preferred_element_type=F32) * scale
            for ci, rows in enumerate(batch):
                mine = [u for u in units if u["ci"] == ci]
                dq = [[None] * npair for _ in range(ntile)]
                if gqa:
                    dk_out = dv_out = None
                    for hk in range(2):
                        us = [u for u in mine if u["hk"] == hk]
                        for u in us:
                            for i, c in enumerate(u["pairs"]):
                                dq[u["t"]][c] = _pick_halves(u["dq"][2 * i * QBLOCK:(2 * i + 1) * QBLOCK],
                                                             u["dq"][(2 * i + 1) * QBLOCK:(2 * i + 2) * QBLOCK])
                        dk_h = _fold_halves(functools.reduce(jnp.add, [u["dk"] for u in us]))
                        dv_h = _fold_halves(functools.reduce(jnp.add, [u["dv"] for u in us]))
                        dk_out = dk_h if hk == 0 else _pick_halves(dk_out, dk_h)
                        dv_out = dv_h if hk == 0 else _pick_halves(dv_out, dv_h)
                else:
                    dks, dvs = [], []
                    for c in range(npair):
                        us = [u for u in mine if u["c"] == c]
                        dks.append(functools.reduce(jnp.add, [u["dk"] for u in us]))
                        dvs.append(functools.reduce(jnp.add, [u["dv"] for u in us]))
                        for t in range(ntile):
                            dq[t][c] = functools.reduce(jnp.add, [u["dq"] for u in us if u["t"] == t])
                    dk_out, dv_out = cat(dks), cat(dvs)
                ck, sk_ = c_ref[0, rows, :], s_ref[0, rows, :]
                dk_ref[0, rows, :] = _rope(dk_out, ck, sk_, sign=-1.0)
                dv_ref[0, rows, :] = dv_out
                dq_cur = cat(dq[0])
                if has_next:
                    dq_cur = dq_cur + carry_ref[rows, :]
                    carry_ref[rows, :] = cat(dq[1])
                dq_ref[0, rows, :] = _rope(dq_cur, ck, sk_, sign=-1.0)

    def at(width, col0, shift):
        return pl.BlockSpec((1, rr, width), lambda b, c, i: (b, jnp.minimum(i + shift, nblk - 1), col0 + c))

    in_specs = [at(kw, k_col, 0), at(kw, v_col, 0), pl.BlockSpec((1, rr, LANES), lambda b, c, i: (b, i, 0)),
                pl.BlockSpec((1, rr, LANES), lambda b, c, i: (b, i, 0))]
    args = [k_arr, v_arr, cos, sin]
    for shift in (0, 1) if has_next else (0,):
        in_specs += [at(qw, q_col, shift), at(qw, 0, shift), at(qw, 0, shift), at(qw, 0, shift)]
        args += [q_arr, do, lse, dd]
    if has_token:
        in_specs.append(pl.BlockSpec(token.shape, lambda b, c, i: (0, 0)))
        args.append(token)
    return pl.pallas_call(
        body,
        name=name,
        grid=(bsz, nchunk, nblk),
        in_specs=in_specs,
        out_specs=[pl.BlockSpec((1, rr, qw), lambda b, c, i: (b, i, c)),
                   pl.BlockSpec((1, rr, kw), lambda b, c, i: (b, i, c)),
                   pl.BlockSpec((1, rr, kw), lambda b, c, i: (b, i, c))],
        out_shape=[jax.ShapeDtypeStruct((bsz, seq, nchunk * qw), F32),
                   jax.ShapeDtypeStruct((bsz, seq, nchunk * kw), F32),
                   jax.ShapeDtypeStruct((bsz, seq, nchunk * kw), F32)],
        scratch_shapes=[pltpu.VMEM((rr, qw) if has_next else (8, LANES), F32)],
        compiler_params=_params("parallel", "parallel", "arbitrary"),
    )(*args)


B_CHUNKS = {1: (4, 1), 4: (1, 4), 16: (1, 4)}


def _rope_tables(positions):
    half = HEAD_DIM // 2
    inv = ROPE_THETA ** (-jnp.arange(half, dtype=F32) / half)
    ang = positions.astype(F32)[..., None] * inv
    cos, sin = jnp.cos(ang), jnp.sin(ang)
    return jnp.concatenate([cos] * 4, axis=-1), jnp.concatenate([-sin, sin, -sin, sin], axis=-1)


def _layer_step(x, mod, positions, sinks, ln1_g, ln1_b, ln2_g, ln2_b, target, get_w_in, get_rest, hook):
    bsz, seq, d = x.shape
    ntok = bsz * seq
    flat = lambda v: v.reshape(ntok, v.shape[-1])
    unflat = lambda v: v.reshape(bsz, seq, v.shape[-1])
    cos, sin = _rope_tables(positions)
    mm = functools.partial(_matmul, tm=1024, tk=1024)
    scalar = lambda tok: 0.0 if tok is None else tok[0, 0]

    u1 = _modulate_in(x, mod)
    u1f = flat(u1)
    wint = get_w_in(u1)
    cosf, sinf = flat(cos), flat(sin)
    proj = functools.partial(_proj_rope, u1f, wint, cosf, sinf, tm=1024)
    qa = unflat(proj(n=1024, b_off=OFF_QA, rope_cols=1024, tn=512, name="proj_qa"))
    kva = unflat(proj(n=256, b_off=OFF_KVA, rope_cols=128, tn=128, name="proj_kva"))
    qkvb = unflat(proj(n=4608, b_off=OFF_QKVB, rope_cols=3072, tn=256, name="proj_qkvb"))
    gab = unflat(proj(n=2048, b_off=OFF_GAB, rope_cols=0, tn=256, name="proj_gab"))

    sink_vec = sinks.reshape(A_Q_HEADS) + scalar(hook("projected", gab))
    a_kw = dict(npair=A_Q_HEADS // 2, gqa=True, q_col=0, k_col=0, v_col=1, nchunk=1, r=1, n_back=A_WINDOW - 1)
    oa, lse_a = _attn_fwd(qa, kva, kva, name="attn_a_fwd", sinks=sink_vec, **a_kw)
    rest = get_rest(oa)
    wba, wbbt, wo, wgut, wd = (rest[n] for n in ("w_branch_a", "w_branch_b", "w_o", "w_gate_up", "w_down"))
    ya = unflat(mm(flat(oa), wba, mode="nn", out_dtype=F32, tn=512, name="branch_a"))

    b_kws, os_, ls_ = [], [], []
    for g, (window, r) in enumerate(B_PATTERNS):
        npair, nch = B_CHUNKS[r]
        per = B_HEADS_PER_GROUP // (2 * npair)
        nsec = len(B_PATTERNS) * per
        kw_ = dict(npair=npair, gqa=False, q_col=g * per, k_col=nsec + g * per, v_col=2 * nsec + g * per, nchunk=nch, r=r,
                   n_back=window // r)
        b_kws.append(kw_)
        o_g, l_g = _attn_fwd(qkvb, qkvb, qkvb, name=f"attn_b{g}_fwd", **kw_)
        os_.append(o_g)
        ls_.append(l_g)
    ob = _merge_b(os_, ls_)
    yb = unflat(mm(flat(ob), wbbt, mode="nt", out_dtype=F32, tk=512, tn=512, name="branch_b"))
    merged = _gate_merge(gab, ya, yb)
    y1 = unflat(mm(flat(merged), wo, mode="nn", out_dtype=F32, tn=512, name="w_o"))
    h1, u2 = _ln1_fwd(x, y1, mod, ln1_g, ln1_b)
    h = unflat(mm(flat(u2), wgut, mode="nt", out_dtype=F32, tn=512, name="gate_up"))
    a = _silu_mul(h)
    y2 = unflat(mm(flat(a), wd, mode="nn", out_dtype=F32, tk=D_FF, tn=512, name="down"))

    dy2, dh1a, acc2 = _ln2_loss_bwd(h1, y2, mod, ln2_g, ln2_b, target)
    dy2f = flat(dy2)
    da = unflat(mm(dy2f, wd, mode="nt", out_dtype=F32, tn=256, name="down_dgrad"))
    g_wd = _matmul(flat(a), dy2f, mode="tn", out_dtype=BF16, tm=256, tn=1024, tk=ntok, name="down_wgrad")
    dh = _silu_mul_bwd(da, h)
    dhf = flat(dh)
    du2 = unflat(mm(dhf, wgut, mode="nn", out_dtype=F32, tk=D_FF, tn=512, name="gate_up_dgrad"))
    g_wgut = _matmul(dhf, flat(u2), mode="tn", out_dtype=BF16, tm=256, tn=1024, tk=ntok, name="gate_up_wgrad")
    dy1, dxa, acc1 = _ln1_bwd(du2, dh1a, x, y1, mod, ln1_g, ln1_b)
    dy1f = flat(dy1)
    dmerged = unflat(mm(dy1f, wo, mode="nt", out_dtype=F32, tn=512, name="w_o_dgrad"))
    g_wo = _matmul(flat(merged), dy1f, mode="tn", out_dtype=BF16, tm=256, tn=1024, tk=ntok, name="w_o_wgrad")
    dya, dyb, dgab = _gate_merge_bwd(dmerged, gab, ya, yb)
    dyaf, dybf = flat(dya), flat(dyb)
    doa = unflat(mm(dyaf, wba, mode="nt", out_dtype=F32, tn=512, name="branch_a_dgrad"))
    g_wba = _matmul(flat(oa), dyaf, mode="tn", out_dtype=BF16, tm=256, tn=1024, tk=ntok, name="branch_a_wgrad")
    dob = unflat(mm(dybf, wbbt, mode="nn", out_dtype=F32, tn=512, name="branch_b_dgrad"))
    g_wbbt = _matmul(dybf, flat(ob), mode="tn", out_dtype=BF16, tm=256, tn=512, tk=ntok, name="branch_b_wgrad")
    tok = hook("grads_rest", dict(w_branch_a=g_wba, w_branch_b=g_wbbt, w_o=g_wo, w_gate_up=g_wgut, w_down=g_wd))

    sinks_exp = jnp.repeat(sinks.reshape(1, A_Q_HEADS), HEAD_DIM, axis=1) + scalar(tok)
    dd_a, acc_s = _delta_a(doa, oa, lse_a, sinks_exp)
    tok = hook("delta_done", dd_a)
    dqa, dka, dva = _attn_bwd(qa, kva, kva, cos, sin, doa, lse_a, dd_a, name="attn_a_bwd", token=tok, **a_kw)
    merged_bwd = _merge_b_bwd(dob, os_, ls_)
    dqs, dks, dvs = [], [], []
    for g in range(len(B_PATTERNS)):
        dq_g, dk_g, dv_g = _attn_bwd(qkvb, qkvb, qkvb, cos, sin, merged_bwd[g], ls_[g], merged_bwd[3 + g],
                                     name=f"attn_b{g}_bwd", **b_kws[g])
        dqs.append(dq_g)
        dks.append(dk_g)
        dvs.append(dv_g)
    dproj = jnp.concatenate([t.astype(BF16) for t in [dqa, dka, dva] + dqs + dks + dvs] + [dgab], axis=-1)
    dprojf = flat(dproj)
    g_wint = _matmul(dprojf, u1f, mode="tn", out_dtype=BF16, tm=256, tn=1024, tk=ntok, name="w_in_wgrad")
    tok = hook("grads_w_in", dict(w_in=g_wint))
    du1 = unflat(_matmul(dprojf, wint, mode="nn", out_dtype=F32, tm=1024, tn=512, tk=wint.shape[0] // 2, name="w_in_dgrad",
                         token=tok))
    tok = hook("dgrad_done", du1)
    grad_x, acc0 = _grad_x(dxa, du1, x, mod + scalar(tok))

    loss_part = jnp.sum(acc2[:, 3, 0])
    dmod = jnp.stack([acc0[:, 1], acc0[:, 0], acc1[:, 2], acc1[:, 4], acc1[:, 3], acc2[:, 2]], axis=1)
    small = jnp.stack([acc1[:, 0].sum(0), acc1[:, 1].sum(0), acc2[:, 0].sum(0), acc2[:, 1].sum(0), acc_s[:, 0].sum(0)])
    return loss_part, grad_x, dmod, small


CHIP_FLIPS = (2, 4, 6)


def _my_place():
    return lax.axis_index("x"), lax.axis_index("y"), lax.axis_index("c")


def _flip(place, k):
    px, py, pc = place
    return (1 - px if k & 4 else px, 1 - py if k & 2 else py, 1 - pc if k & 1 else pc)


def _index(place):
    return 4 * place[0] + 2 * place[1] + place[2]


def _gather_small(v, name):
    rows, cols = v.shape

    def body(v_ref, out_ref, send_sems, recv_sems):
        me = _my_place()
        out_ref[_index(me)] = v_ref[...]
        copies = []
        for k in range(1, N_DEV):
            copies.append(pltpu.make_async_remote_copy(
                src_ref=v_ref, dst_ref=out_ref.at[_index(me)], send_sem=send_sems.at[k - 1], recv_sem=recv_sems.at[k - 1],
                device_id=_flip(me, k), device_id_type=MESH))
        for cp in copies:
            cp.start()
        for k in range(1, N_DEV):
            pltpu.make_async_remote_copy(
                src_ref=v_ref, dst_ref=out_ref.at[_index(_flip(me, k))], send_sem=send_sems.at[k - 1],
                recv_sem=recv_sems.at[k - 1], device_id=_flip(me, k), device_id_type=MESH).wait_recv()
        for cp in copies:
            cp.wait_send()

    return pl.pallas_call(
        body,
        name=name,
        out_shape=jax.ShapeDtypeStruct((N_DEV, rows, cols), v.dtype),
        in_specs=[pl.BlockSpec(memory_space=pltpu.VMEM)],
        out_specs=pl.BlockSpec(memory_space=pltpu.VMEM),
        scratch_shapes=[pltpu.SemaphoreType.DMA((N_DEV - 1,)), pltpu.SemaphoreType.DMA((N_DEV - 1,))],
        compiler_params=pltpu.CompilerParams(vmem_limit_bytes=VMEM_LIMIT_BYTES),
    )(v)


_HBM = pl.BlockSpec(memory_space=pltpu.HBM)
_SEM = pl.BlockSpec(memory_space=pltpu.SEMAPHORE)
_EFFECT = pltpu.SideEffectType.DATAFLOW_SIDE_EFFECTING


def _remote(src, dst, send_sems, recv_sems, j, to):
    return pltpu.make_async_remote_copy(src_ref=src, dst_ref=dst, send_sem=send_sems.at[j], recv_sem=recv_sems.at[j],
                                        device_id=to, device_id_type=MESH)


def _copies_start(name, src, land_shape, make_copies, nsem):
    def body(src_ref, land_ref, send_sems, recv_sems, src_thru, land_thru, token):
        for cp in make_copies(src_ref, land_ref, send_sems, recv_sems):
            cp.start()
        token[...] = jnp.zeros_like(token)

    sems = pltpu.SemaphoreType.DMA((nsem,))
    return pl.pallas_call(
        body, name=name,
        out_shape=(sems, sems, pltpu.HBM(src.shape, src.dtype), pltpu.HBM(land_shape, src.dtype),
                   jax.ShapeDtypeStruct((8, LANES), F32)),
        in_specs=(_HBM, _HBM), out_specs=(_SEM, _SEM, _HBM, _HBM, pl.BlockSpec(memory_space=pltpu.VMEM)),
        input_output_aliases={0: 2, 1: 3},
        compiler_params=pltpu.CompilerParams(has_side_effects=_EFFECT),
    )(pltpu.with_memory_space_constraint(src, pltpu.HBM),
      pltpu.with_memory_space_constraint(lax.empty(land_shape, src.dtype), pltpu.HBM))


def _copies_wait(name, started, make_copies, after):
    send_sems, recv_sems, src_thru, land_thru, _ = started

    def body(src_ref, land_ref, send_sems, recv_sems, after_ref, src_dead, got_ref):
        for cp in make_copies(src_ref, land_ref, send_sems, recv_sems):
            cp.wait_send()
            cp.wait_recv()

    return pl.pallas_call(
        body, name=name,
        out_shape=(pltpu.HBM(src_thru.shape, src_thru.dtype), pltpu.HBM(land_thru.shape, land_thru.dtype)),
        in_specs=(_HBM, _HBM, _SEM, _SEM, pl.BlockSpec(memory_space=pl.ANY)), out_specs=(_HBM, _HBM),
        input_output_aliases={0: 0, 1: 1},
        compiler_params=pltpu.CompilerParams(has_side_effects=_EFFECT),
    )(src_thru, land_thru, send_sems, recv_sems, after)


def _gather1_copies(src_ref, land_ref, send_sems, recv_sems):
    me = _my_place()
    return [_remote(src_ref, land_ref.at[_index(me)], send_sems, recv_sems, j, _flip(me, k)) for j, k in enumerate((1,) + CHIP_FLIPS)]


def _gather2_copies(src_ref, land_ref, send_sems, recv_sems):
    me = _my_place()
    return [_remote(src_ref.at[_index(_flip(me, k))], land_ref.at[j], send_sems, recv_sems, j, _flip(me, 1))
            for j, k in enumerate(CHIP_FLIPS)]


def _to_sibling_copies(src_ref, land_ref, send_sems, recv_sems):
    me = _my_place()
    return [_remote(src_ref.at[1 - me[2]], land_ref, send_sems, recv_sems, 0, _flip(me, 1))]


def _to_chips_copies(src_ref, land_ref, send_sems, recv_sems):
    me = _my_place()
    copies = []
    for j, k in enumerate(CHIP_FLIPS):
        to = _flip(me, k)
        copies.append(_remote(src_ref.at[2 * to[0] + to[1]], land_ref.at[j], send_sems, recv_sems, j, to))
    return copies


class _Gather:
    def __init__(self, name, packed):
        self.name, self.packed = name, packed
        self.rows = packed.shape[0]
        self.first = _copies_start(name + "_start", packed, (N_DEV, self.rows, D_MODEL), _gather1_copies, 4)
        self.token = self.first[4]

    def pass_on(self, after):
        _, land = _copies_wait(self.name + "_wait", self.first, _gather1_copies, after)
        self.second = _copies_start(self.name + "_pass_start", land, (3, self.rows, D_MODEL), _gather2_copies, 3)
        return self.second[4]

    def finish(self, after):
        full, passed = _copies_wait(self.name + "_pass_wait", self.second, _gather2_copies, after)
        me = _my_place()
        full = lax.dynamic_update_slice(full, self.packed[None], (_index(me), 0, 0))
        for j, k in enumerate(CHIP_FLIPS):
            full = lax.dynamic_update_slice(full, passed[j][None], (_index(_flip(me, k | 1)), 0, 0))
        return full


SUM_SPLIT = 2


def _sum_pairs(mine, theirs):
    nchip, rows, cols = mine.shape
    tile = rows // SUM_SPLIT
    spec = pl.BlockSpec((1, tile, cols), lambda q, t: (q, t, 0))

    def body(a_ref, b_ref, o_ref):
        o_ref[...] = (a_ref[...].astype(F32) + b_ref[...].astype(F32)).astype(BF16)

    return pl.pallas_call(body, name="grad_sum_sibling", grid=(nchip, SUM_SPLIT), in_specs=[spec, spec], out_specs=spec,
                          out_shape=jax.ShapeDtypeStruct(mine.shape, BF16), compiler_params=_params("parallel", "parallel"))(mine, theirs)


def _sum_final(own, got):
    rows, cols = own.shape
    tile = rows // SUM_SPLIT

    def body(a_ref, g_ref, o_ref):
        o_ref[...] = ((a_ref[...].astype(F32) + g_ref[0].astype(F32)) + g_ref[1].astype(F32)) + g_ref[2].astype(F32)

    return pl.pallas_call(
        body, name="grad_sum_chips", grid=(SUM_SPLIT,),
        in_specs=[pl.BlockSpec((tile, cols), lambda t: (t, 0)), pl.BlockSpec((3, tile, cols), lambda t: (0, t, 0))],
        out_specs=pl.BlockSpec((tile, cols), lambda t: (t, 0)),
        out_shape=jax.ShapeDtypeStruct((rows, cols), F32), compiler_params=_params("parallel"))(own, got)


class _ReduceScatter:
    def __init__(self, name, slabs):
        self.name = name
        rows = slabs.shape[1]
        self.rows = rows
        self.parts = slabs.reshape(4, 2, rows, D_MODEL).transpose(1, 0, 2, 3)
        self.first = _copies_start(name + "_sibling_start", self.parts, (4, rows, D_MODEL), _to_sibling_copies, 1)
        self.token = self.first[4]

    def between_chips(self, after):
        parts, theirs = _copies_wait(self.name + "_sibling_wait", self.first, _to_sibling_copies, after)
        mine = lax.dynamic_index_in_dim(parts, lax.axis_index("c"), 0, keepdims=False)
        chip_sum = _sum_pairs(mine, theirs)
        self.second = _copies_start(self.name + "_chips_start", chip_sum, (3, self.rows, D_MODEL), _to_chips_copies, 3)
        return self.second[4]

    def finish(self, after):
        chip_sum, got = _copies_wait(self.name + "_chips_wait", self.second, _to_chips_copies, after)
        my_chip = 2 * lax.axis_index("x") + lax.axis_index("y")
        return _sum_final(lax.dynamic_index_in_dim(chip_sum, my_chip, 0, keepdims=False), got)


def _ada_fwd(c_all, w, b):
    nb, _ = c_all.shape
    ncol = w.shape[1]

    def body(c_ref, w_ref, b_ref, o_ref):
        c = c_ref[...]
        act = (c * _sigmoid(c)).astype(BF16)
        o_ref[...] = jnp.dot(act, w_ref[...].astype(BF16), preferred_element_type=F32) + b_ref[...]

    return pl.pallas_call(body, name="ada_fwd", out_shape=jax.ShapeDtypeStruct((nb, ncol), F32),
                          compiler_params=pltpu.CompilerParams(vmem_limit_bytes=VMEM_LIMIT_BYTES))(c_all, w, b)


def _ada_wgrad(c_all_t, dmod_cols):
    d, nb = c_all_t.shape
    ncol = dmod_cols.shape[1]

    def body(ct_ref, dm_ref, o_ref):
        ct = ct_ref[...]
        act = (ct * _sigmoid(ct)).astype(BF16).astype(F32)
        dm = dm_ref[...].astype(BF16).astype(F32)
        acc = act[:, 0:1] * dm[0:1, :]
        for i in range(1, nb):
            acc = acc + act[:, i:i + 1] * dm[i:i + 1, :]
        o_ref[...] = acc

    return pl.pallas_call(body, name="ada_wgrad", out_shape=jax.ShapeDtypeStruct((d, ncol), F32),
                          compiler_params=pltpu.CompilerParams(vmem_limit_bytes=VMEM_LIMIT_BYTES))(c_all_t, dmod_cols)


SMALL_ROWS = 24


def _reduce_small(gathered):
    def body(g_ref, o_ref):
        acc = g_ref[0]
        for dev in range(1, N_DEV):
            acc = acc + g_ref[dev]
        o_ref[...] = acc

    return pl.pallas_call(body, name="reduce_small", out_shape=jax.ShapeDtypeStruct(gathered.shape[1:], F32))(gathered)


def _adamw(w, g, m, v, name):
    rows, cols = w.shape
    tile = rows
    for cand in (256, 128, 64, 32, 16, 8):
        if rows % cand == 0 and rows > cand:
            tile = cand
            break
    spec = pl.BlockSpec((tile, cols), lambda t: (t, 0))
    bc1 = 1.0 - ADAM_B1 ** ADAM_STEP
    bc2 = 1.0 - ADAM_B2 ** ADAM_STEP

    def body(w_ref, g_ref, m_ref, v_ref, d_ref, nm_ref, nv_ref):
        g_ = g_ref[...]
        nm = ADAM_B1 * m_ref[...] + (1.0 - ADAM_B1) * g_
        nv = ADAM_B2 * v_ref[...] + (1.0 - ADAM_B2) * (g_ * g_)
        d_ref[...] = -ADAM_LR * ((nm / bc1) / (jnp.sqrt(nv / bc2) + ADAM_EPS) + ADAM_WD * w_ref[...])
        nm_ref[...] = nm
        nv_ref[...] = nv

    shp = jax.ShapeDtypeStruct((rows, cols), F32)
    return pl.pallas_call(body, name=name, grid=(rows // tile,), in_specs=[spec] * 4, out_specs=[spec] * 3, out_shape=[shp] * 3,
                          compiler_params=_params("parallel"))(w, g, m, v)


_WEIGHTS = ("w_ada", "b_ada", "w_in", "sinks", "w_branch_a", "w_branch_b", "w_o", "ln1_g", "ln1_b", "w_gate_up", "w_down",
            "ln2_g", "ln2_b")
_TRANSPOSED = ("w_in", "w_branch_b", "w_gate_up")


def _pack_shard(name, w):
    w = w.astype(BF16)
    if name in _TRANSPOSED:
        w = w.T
    return w.reshape(-1, D_MODEL)


def _unpack_full(name, slab):
    if name == "w_branch_b":
        return slab.reshape(N_DEV * 128, 512)
    return slab.reshape(-1, D_MODEL)


def _unpack_group(group, gathered):
    full, off = {}, 0
    for n, r in group:
        full[n] = _unpack_full(n, gathered[:, off:off + r])
        off += r
    return full


def _unpack_grads(group, g_packed):
    g_w, off = {}, 0
    for n, r in group:
        part = g_packed[off:off + r]
        off += r
        if n == "w_branch_b":
            part = part.reshape(128, 512)
        g_w[n] = part.T if n in _TRANSPOSED else part
    return g_w


def kernel(x, c, positions, w_ada, b_ada, w_in, sinks, w_branch_a, w_branch_b, w_o, ln1_g, ln1_b, w_gate_up, w_down, ln2_g, ln2_b, loss_target, m_w_ada, m_b_ada, m_w_in, m_sinks, m_w_branch_a, m_w_branch_b, m_w_o, m_ln1_g, m_ln1_b, m_w_gate_up, m_w_down, m_ln2_g, m_ln2_b, v_w_ada, v_b_ada, v_w_in, v_sinks, v_w_branch_a, v_w_branch_b, v_w_o, v_ln1_g, v_ln1_b, v_w_gate_up, v_w_down, v_ln2_g, v_ln2_b):
    weights = dict(w_ada=w_ada, b_ada=b_ada, w_in=w_in, sinks=sinks, w_branch_a=w_branch_a, w_branch_b=w_branch_b, w_o=w_o,
                   ln1_g=ln1_g, ln1_b=ln1_b, w_gate_up=w_gate_up, w_down=w_down, ln2_g=ln2_g, ln2_b=ln2_b)
    m_in = dict(w_ada=m_w_ada, b_ada=m_b_ada, w_in=m_w_in, sinks=m_sinks, w_branch_a=m_w_branch_a, w_branch_b=m_w_branch_b,
                w_o=m_w_o, ln1_g=m_ln1_g, ln1_b=m_ln1_b, w_gate_up=m_w_gate_up, w_down=m_w_down, ln2_g=m_ln2_g, ln2_b=m_ln2_b)
    v_in = dict(w_ada=v_w_ada, b_ada=v_b_ada, w_in=v_w_in, sinks=v_sinks, w_branch_a=v_w_branch_a, w_branch_b=v_w_branch_b,
                w_o=v_w_o, ln1_g=v_ln1_g, ln1_b=v_ln1_b, w_gate_up=v_w_gate_up, w_down=v_w_down, ln2_g=v_ln2_g, ln2_b=v_ln2_b)
    bsz = x.shape[0]
    me = _index(_my_place())
    ada_cols = w_ada.shape[2]
    outs = {}

    def adamw(n, g):
        w2, m2, v2 = (t[n][0] if t[n].ndim == 3 else t[n] for t in (weights, m_in, v_in))
        shape = weights[n].shape
        dlt, nm, nv = _adamw(w2, g, m2, v2, "adamw_" + n)
        outs[n] = tuple(t.reshape(shape) for t in (g, dlt, nm, nv))
        return nv

    gather_in = _Gather("gather_w_in", jnp.concatenate([_pack_shard(n, weights[n][0]) for n, _ in GROUP_IN], axis=0))
    gather_rest = _Gather("gather_rest", jnp.concatenate([_pack_shard(n, weights[n][0]) for n, _ in GROUP_REST], axis=0))

    c_pad = jnp.pad(c, ((0, 8 - bsz), (0, 0))) + (gather_in.token[0, 0] + gather_rest.token[0, 0])
    c_all = _gather_small(c_pad, "gather_c")[:, :bsz].reshape(N_DEV * bsz, D_MODEL)
    b_cols = lax.dynamic_slice_in_dim(b_ada, me * ada_cols, ada_cols, axis=1)
    mod_cols = _ada_fwd(c_all, w_ada[0], b_cols)
    mod_all = _gather_small(mod_cols, "gather_mod").transpose(1, 0, 2).reshape(N_DEV * bsz, 6, D_MODEL)
    mod = jnp.pad(lax.dynamic_slice_in_dim(mod_all, me * bsz, bsz, axis=0), ((0, 0), (0, 2), (0, 0)))
    mod = mod + gather_in.pass_on(mod)[0, 0]

    scatters = {}

    def get_w_in(after):
        return _unpack_group(GROUP_IN, gather_in.finish(after))["w_in"]

    def get_rest(after):
        return _unpack_group(GROUP_REST, gather_rest.finish(after))

    def pack_grads(group, grads):
        return jnp.concatenate([grads[n].reshape(N_DEV, r, D_MODEL) for n, r in group], axis=1)

    def hook(point, value):
        if point == "projected":
            return gather_rest.pass_on(value)
        if point == "grads_rest":
            scatters["rest"] = _ReduceScatter("scatter_rest", pack_grads(GROUP_REST, value))
            return scatters["rest"].token
        if point == "delta_done":
            return scatters["rest"].between_chips(value)
        if point == "grads_w_in":
            scatters["in"] = _ReduceScatter("scatter_w_in", pack_grads(GROUP_IN, value))
            return scatters["in"].token
        if point == "dgrad_done":
            tok = scatters["in"].between_chips(value)
            for n, g in _unpack_grads(GROUP_REST, scatters["rest"].finish(tok)).items():
                adamw(n, g)
            return tok
        raise ValueError(point)

    loss_part, grad_x, dmod, small = _layer_step(x, mod, positions, sinks[0], ln1_g, ln1_b, ln2_g, ln2_b, loss_target,
                                                 get_w_in, get_rest, hook)
    loss = lax.psum(loss_part, MESH_AXES)

    rows = jnp.concatenate([dmod.reshape(bsz * 6, D_MODEL), small, jnp.zeros((SMALL_ROWS - bsz * 6 - 5, D_MODEL), F32)], axis=0)
    small_all = _gather_small(rows, "gather_small")
    sums = _reduce_small(small_all)
    dmod_all = small_all[:, :bsz * 6].reshape(N_DEV * bsz, 6 * D_MODEL)
    adamw("b_ada", functools.reduce(jnp.add, [sums[6 * i:6 * i + 6] for i in range(bsz)]).reshape(1, 6 * D_MODEL))
    for i, n in enumerate(("ln1_g", "ln1_b", "ln2_g", "ln2_b")):
        adamw(n, sums[12 + i][None])
    adamw("sinks", sums[16][::HEAD_DIM][None])
    dmod_cols = lax.dynamic_slice_in_dim(dmod_all, me * ada_cols, ada_cols, axis=1)
    last = adamw("w_ada", _ada_wgrad(c_all.T, dmod_cols))
    for n, g in _unpack_grads(GROUP_IN, scatters["in"].finish(last)).items():
        adamw(n, g)

    return (loss, grad_x, *[outs[n][0] for n in _WEIGHTS], *[outs[n][1] for n in _WEIGHTS], *[outs[n][2] for n in _WEIGHTS],
            *[outs[n][3] for n in _WEIGHTS])
```

```python
import functools

import jax
import jax.numpy as jnp
from jax import lax
from jax.experimental import pallas as pl
from jax.experimental.pallas import tpu as pltpu

F32 = jnp.float32
BF16 = jnp.bfloat16

D_MODEL = 1024
HEAD_DIM = 64
A_Q_HEADS = 16
A_WINDOW = 128
B_PATTERNS = ((128, 1), (512, 4), (2048, 16))
B_HEADS_PER_GROUP = 8
D_FF = 2816
QBLOCK = 128
ROPE_THETA = 10000.0
LN_EPS = 1e-5
DEEPNORM_ALPHA = 2.0 ** 0.25
NEG_INF = -1e30
ADAM_LR, ADAM_B1, ADAM_B2, ADAM_EPS, ADAM_WD, ADAM_STEP = 0.001, 0.9, 0.999, 1e-08, 0.01, 10

N_DEV = 8
MESH_AXES = ("x", "y", "c")
LANES = 128
VMEM_LIMIT_BYTES = 56 * 1024 * 1024
MESH = pl.DeviceIdType.MESH

OFF_QA, OFF_KVA, OFF_QKVB, OFF_GAB = 0, 1024, 1280, 5888
GROUP_IN = (("w_in", 992),)
GROUP_REST = (("w_branch_a", 128), ("w_branch_b", 64), ("w_o", 128), ("w_gate_up", 704), ("w_down", 352))


def _params(*sem):
    return pltpu.CompilerParams(dimension_semantics=sem, vmem_limit_bytes=VMEM_LIMIT_BYTES)


def _sigmoid(x):
    return 1.0 / (1.0 + jnp.exp(-x))


_DIMS = {"nn": (((1,), (0,)), ((), ())), "nt": (((1,), (1,)), ((), ())), "tn": (((0,), (0,)), ((), ()))}


def _matmul(a, b, *, mode, out_dtype, tm, tn, tk, name, n=None, b_off=0, token=None):
    if mode == "nn":
        (m, k), nn_ = a.shape, b.shape[1]
    elif mode == "nt":
        (m, k), nn_ = a.shape, (b.shape[0] if n is None else n)
    else:
        (k, m), nn_ = a.shape, b.shape[1]
    assert m % tm == 0 and nn_ % tn == 0 and k % tk == 0 and b_off % tn == 0, (name, m, nn_, k)
    nk = k // tk
    joff = b_off // tn
    if mode == "nn":
        a_spec = pl.BlockSpec((tm, tk), lambda i, j, kk: (i, kk))
        b_spec = pl.BlockSpec((tk, tn), lambda i, j, kk: (kk, j))
    elif mode == "nt":
        a_spec = pl.BlockSpec((tm, tk), lambda i, j, kk: (i, kk))
        b_spec = pl.BlockSpec((tn, tk), lambda i, j, kk: (j + joff, kk))
    else:
        a_spec = pl.BlockSpec((tk, tm), lambda i, j, kk: (kk, i))
        b_spec = pl.BlockSpec((tk, tn), lambda i, j, kk: (kk, j))
    dims = _DIMS[mode]
    has_token = token is not None

    def body(*refs):
        a_ref, b_ref = refs[:2]
        o_ref, acc_ref = refs[-2:]
        kk = pl.program_id(2)
        part = lax.dot_general(a_ref[...].astype(BF16), b_ref[...].astype(BF16), dims, preferred_element_type=F32)
        if nk == 1:
            o_ref[...] = part.astype(o_ref.dtype)
        else:
            @pl.when(kk == 0)
            def _():
                acc_ref[...] = part

            @pl.when(kk > 0)
            def _():
                acc_ref[...] += part

            @pl.when(kk == nk - 1)
            def _():
                o_ref[...] = acc_ref[...].astype(o_ref.dtype)

    in_specs, args = [a_spec, b_spec], [a, b]
    if has_token:
        in_specs.append(pl.BlockSpec(token.shape, lambda i, j, kk: (0, 0)))
        args.append(token)
    return pl.pallas_call(
        body,
        name=name,
        grid=(m // tm, nn_ // tn, nk),
        in_specs=in_specs,
        out_specs=pl.BlockSpec((tm, tn), lambda i, j, kk: (i, j)),
        out_shape=jax.ShapeDtypeStruct((m, nn_), out_dtype),
        scratch_shapes=[pltpu.VMEM((tm, tn) if nk > 1 else (8, LANES), F32)],
        compiler_params=_params("parallel", "parallel", "arbitrary"),
    )(*args)


def _proj_rope(a, bt, cos, sin, *, n, b_off, rope_cols, tm, tn, name, out_dtype=F32):
    m, k = a.shape
    assert m % tm == 0 and n % tn == 0 and b_off % tn == 0 and rope_cols % tn == 0, name
    joff = b_off // tn
    nrope = rope_cols // tn

    def body(a_ref, b_ref, c_ref, s_ref, o_ref):
        acc = lax.dot_general(a_ref[...], b_ref[...], _DIMS["nt"], preferred_element_type=F32)
        j = pl.program_id(1)

        @pl.when(j < nrope)
        def _():
            o_ref[...] = _rope(acc, c_ref[...], s_ref[...]).astype(o_ref.dtype)

        @pl.when(j >= nrope)
        def _():
            o_ref[...] = acc.astype(o_ref.dtype)

    table = pl.BlockSpec((tm, LANES), lambda i, j: (i, 0))
    return pl.pallas_call(
        body,
        name=name,
        grid=(m // tm, n // tn),
        in_specs=[pl.BlockSpec((tm, k), lambda i, j: (i, 0)), pl.BlockSpec((tn, k), lambda i, j: (j + joff, 0)), table, table],
        out_specs=pl.BlockSpec((tm, tn), lambda i, j: (i, j)),
        out_shape=jax.ShapeDtypeStruct((m, n), out_dtype),
        compiler_params=_params("parallel", "parallel"),
    )(a, bt, cos, sin)


ROW_TILE = 256


def _rows(width, col=0):
    return pl.BlockSpec((1, ROW_TILE, width), lambda b, t: (b, t, col))


def _per_batch(nrows, width):
    return pl.BlockSpec((1, nrows, width), lambda b, t: (b, 0, 0))


def _whole(shape):
    return pl.BlockSpec(shape, lambda b, t: (0,) * len(shape))


def _row_call(body, name, bsz, seq, in_specs, out_specs, out_shape, accumulates=False):
    return pl.pallas_call(
        body,
        name=name,
        grid=(bsz, seq // ROW_TILE),
        in_specs=in_specs,
        out_specs=out_specs,
        out_shape=out_shape,
        compiler_params=_params("parallel", "arbitrary" if accumulates else "parallel"),
    )


def _acc_rows(acc_ref, first, rows):
    @pl.when(first)
    def _():
        acc_ref[...] = jnp.zeros_like(acc_ref)

    for r, val in enumerate(rows):
        acc_ref[0, r:r + 1, :] += val


def _colsum(v):
    return jnp.sum(v, axis=0, keepdims=True)


def _ln_stats(z):
    mu = jnp.mean(z, axis=-1, keepdims=True)
    zc = z - mu
    var = jnp.mean(zc * zc, axis=-1, keepdims=True)
    rstd = lax.rsqrt(var + LN_EPS)
    return zc * rstd, rstd


def _ln_bwd(dxhat, xhat, rstd):
    m1 = jnp.mean(dxhat, axis=-1, keepdims=True)
    m2 = jnp.mean(dxhat * xhat, axis=-1, keepdims=True)
    return rstd * (dxhat - m1 - xhat * m2)


def _modulate_in(x, mod):
    bsz, seq, d = x.shape

    def body(x_ref, mod_ref, u_ref):
        u_ref[0] = (x_ref[0] * (1.0 + mod_ref[0, 1:2, :]) + mod_ref[0, 0:1, :]).astype(BF16)

    return _row_call(body, "modulate_in", bsz, seq, [_rows(d), _per_batch(8, d)], _rows(d),
                     jax.ShapeDtypeStruct((bsz, seq, d), BF16))(x, mod)


def _gate_merge(gab, ya, yb):
    bsz, seq, d = ya.shape

    def body(ga_ref, gb_ref, ya_ref, yb_ref, o_ref):
        ga, gb, ya_, yb_ = (r[0].astype(F32) for r in (ga_ref, gb_ref, ya_ref, yb_ref))
        o_ref[0] = (_sigmoid(ga) * ya_ + _sigmoid(gb) * yb_).astype(BF16)

    return _row_call(body, "gate_merge", bsz, seq, [_rows(d, 0), _rows(d, 1), _rows(d), _rows(d)], _rows(d),
                     jax.ShapeDtypeStruct((bsz, seq, d), BF16))(gab, gab, ya, yb)


def _ln1_fwd(x, y1, mod, g, b):
    bsz, seq, d = x.shape

    def body(x_ref, y_ref, mod_ref, g_ref, b_ref, h_ref, u_ref):
        z = DEEPNORM_ALPHA * x_ref[0] + (1.0 + mod_ref[0, 2:3, :]) * y_ref[0]
        xhat, _ = _ln_stats(z)
        h = xhat * g_ref[...] + b_ref[...]
        h_ref[0] = h
        u_ref[0] = (h * (1.0 + mod_ref[0, 4:5, :]) + mod_ref[0, 3:4, :]).astype(BF16)

    return _row_call(body, "ln1_fwd", bsz, seq,
                     [_rows(d), _rows(d), _per_batch(8, d), _whole((1, d)), _whole((1, d))],
                     [_rows(d), _rows(d)],
                     [jax.ShapeDtypeStruct((bsz, seq, d), F32), jax.ShapeDtypeStruct((bsz, seq, d), BF16)])(x, y1, mod, g, b)


def _silu_mul(h):
    bsz, seq, _ = h.shape

    def body(hg_ref, hu_ref, a_ref):
        hg = hg_ref[0].astype(F32)
        a_ref[0] = (hg * _sigmoid(hg) * hu_ref[0].astype(F32)).astype(BF16)

    return _row_call(body, "silu_mul", bsz, seq, [_rows(D_FF, 0), _rows(D_FF, 1)], _rows(D_FF),
                     jax.ShapeDtypeStruct((bsz, seq, D_FF), BF16))(h, h)


def _ln2_loss_bwd(h1, y2, mod, g, b, target):
    bsz, seq, d = h1.shape

    def body(h_ref, y_ref, mod_ref, g_ref, b_ref, t_ref, dy_ref, dh_ref, acc_ref):
        y = y_ref[0]
        gate = 1.0 + mod_ref[0, 5:6, :]
        z = DEEPNORM_ALPHA * h_ref[0] + gate * y
        xhat, rstd = _ln_stats(z)
        diff = xhat * g_ref[...] + b_ref[...] - t_ref[0]
        loss = 0.5 * jnp.sum(jnp.sum(diff * diff, axis=-1, keepdims=True) / d, axis=0, keepdims=True)
        dout = diff / d
        dz = _ln_bwd(dout * g_ref[...], xhat, rstd)
        dy_ref[0] = (gate * dz).astype(BF16)
        dh_ref[0] = DEEPNORM_ALPHA * dz
        _acc_rows(acc_ref, pl.program_id(1) == 0,
                  [_colsum(dout * xhat), _colsum(dout), _colsum(dz * y), jnp.broadcast_to(loss, (1, d))])

    return _row_call(body, "ln2_loss_bwd", bsz, seq,
                     [_rows(d), _rows(d), _per_batch(8, d), _whole((1, d)), _whole((1, d)), _rows(d)],
                     [_rows(d), _rows(d), _per_batch(8, d)],
                     [jax.ShapeDtypeStruct((bsz, seq, d), BF16), jax.ShapeDtypeStruct((bsz, seq, d), F32),
                      jax.ShapeDtypeStruct((bsz, 8, d), F32)], accumulates=True)(h1, y2, mod, g, b, target)


def _silu_mul_bwd(da, h):
    bsz, seq, _ = h.shape

    def body(da_ref, hg_ref, hu_ref, dh_ref):
        hg, da_ = hg_ref[0].astype(F32), da_ref[0].astype(F32)
        sg = _sigmoid(hg)
        dh_ref[0, :, :D_FF] = (da_ * hu_ref[0].astype(F32) * (sg * (1.0 + hg * (1.0 - sg)))).astype(BF16)
        dh_ref[0, :, D_FF:] = (da_ * (hg * sg)).astype(BF16)

    return _row_call(body, "silu_mul_bwd", bsz, seq, [_rows(D_FF), _rows(D_FF, 0), _rows(D_FF, 1)], _rows(2 * D_FF),
                     jax.ShapeDtypeStruct((bsz, seq, 2 * D_FF), BF16))(da, h, h)


def _ln1_bwd(du2, dh1a, x, y1, mod, g, b):
    bsz, seq, d = x.shape

    def body(du_ref, dh_ref, x_ref, y_ref, mod_ref, g_ref, b_ref, dy_ref, dx_ref, acc_ref):
        y, du = y_ref[0], du_ref[0]
        gate = 1.0 + mod_ref[0, 2:3, :]
        z = DEEPNORM_ALPHA * x_ref[0] + gate * y
        xhat, rstd = _ln_stats(z)
        h1 = xhat * g_ref[...] + b_ref[...]
        dh1 = dh_ref[0] + du * (1.0 + mod_ref[0, 4:5, :])
        dz = _ln_bwd(dh1 * g_ref[...], xhat, rstd)
        dy_ref[0] = (gate * dz).astype(BF16)
        dx_ref[0] = DEEPNORM_ALPHA * dz
        _acc_rows(acc_ref, pl.program_id(1) == 0,
                  [_colsum(dh1 * xhat), _colsum(dh1), _colsum(dz * y), _colsum(du * h1), _colsum(du)])

    return _row_call(body, "ln1_bwd", bsz, seq,
                     [_rows(d), _rows(d), _rows(d), _rows(d), _per_batch(8, d), _whole((1, d)), _whole((1, d))],
                     [_rows(d), _rows(d), _per_batch(8, d)],
                     [jax.ShapeDtypeStruct((bsz, seq, d), BF16), jax.ShapeDtypeStruct((bsz, seq, d), F32),
                      jax.ShapeDtypeStruct((bsz, 8, d), F32)], accumulates=True)(du2, dh1a, x, y1, mod, g, b)


def _gate_merge_bwd(dm, gab, ya, yb):
    bsz, seq, d = ya.shape

    def body(dm_ref, ga_ref, gb_ref, ya_ref, yb_ref, dya_ref, dyb_ref, dg_ref):
        dm_ = dm_ref[0].astype(F32)
        sa, sb = _sigmoid(ga_ref[0].astype(F32)), _sigmoid(gb_ref[0].astype(F32))
        dya_ref[0] = (dm_ * sa).astype(BF16)
        dyb_ref[0] = (dm_ * sb).astype(BF16)
        dg_ref[0, :, :d] = (dm_ * ya_ref[0].astype(F32) * sa * (1.0 - sa)).astype(BF16)
        dg_ref[0, :, d:] = (dm_ * yb_ref[0].astype(F32) * sb * (1.0 - sb)).astype(BF16)

    return _row_call(body, "gate_merge_bwd", bsz, seq,
                     [_rows(d), _rows(d, 0), _rows(d, 1), _rows(d), _rows(d)],
                     [_rows(d), _rows(d), _rows(2 * d)],
                     [jax.ShapeDtypeStruct((bsz, seq, d), BF16), jax.ShapeDtypeStruct((bsz, seq, d), BF16),
                      jax.ShapeDtypeStruct((bsz, seq, 2 * d), BF16)])(dm, gab, gab, ya, yb)


def _grad_x(dxa, du1, x, mod):
    bsz, seq, d = x.shape

    def body(dxa_ref, du_ref, x_ref, mod_ref, gx_ref, acc_ref):
        du = du_ref[0]
        gx_ref[0] = dxa_ref[0] + du * (1.0 + mod_ref[0, 1:2, :])
        _acc_rows(acc_ref, pl.program_id(1) == 0, [_colsum(du * x_ref[0]), _colsum(du)])

    return _row_call(body, "grad_x", bsz, seq, [_rows(d), _rows(d), _rows(d), _per_batch(8, d)],
                     [_rows(d), _per_batch(8, d)],
                     [jax.ShapeDtypeStruct((bsz, seq, d), F32), jax.ShapeDtypeStruct((bsz, 8, d), F32)],
                     accumulates=True)(dxa, du1, x, mod)


def _segsum64(v):
    rows, width = v.shape
    ri = lax.broadcasted_iota(jnp.int32, (LANES, LANES), 0) // HEAD_DIM
    ci = lax.broadcasted_iota(jnp.int32, (LANES, LANES), 1) // HEAD_DIM
    ones = jnp.where(ri == ci, 1.0, 0.0).astype(BF16)
    out = []
    for c in range(width // LANES):
        part = v[:, c * LANES:(c + 1) * LANES]
        hi = part.astype(BF16)
        lo = (part - hi.astype(F32)).astype(BF16)
        out.append(jnp.dot(hi, ones, preferred_element_type=F32) + jnp.dot(lo, ones, preferred_element_type=F32))
    return jnp.concatenate(out, axis=1) if len(out) > 1 else out[0]


def _merge_b(os_, ls_):
    bsz, seq, w = os_[0].shape

    def body(o0, o1, o2, l0, l1, l2, ob_ref):
        ls = [l0[0], l1[0], l2[0]]
        mx = jnp.maximum(jnp.maximum(ls[0], ls[1]), ls[2])
        es = [jnp.exp(l - mx) for l in ls]
        den = es[0] + es[1] + es[2]
        ob_ref[0] = ((es[0] / den) * o0[0] + (es[1] / den) * o1[0] + (es[2] / den) * o2[0]).astype(BF16)

    return _row_call(body, "merge_b", bsz, seq, [_rows(w)] * 6, _rows(w),
                     jax.ShapeDtypeStruct((bsz, seq, w), BF16))(*os_, *ls_)


def _merge_b_bwd(dob, os_, ls_):
    bsz, seq, w = os_[0].shape

    def body(dob_ref, o0, o1, o2, l0, l1, l2, do0, do1, do2, dd0, dd1, dd2):
        dob_ = dob_ref[0].astype(F32)
        ls = [l0[0], l1[0], l2[0]]
        mx = jnp.maximum(jnp.maximum(ls[0], ls[1]), ls[2])
        es = [jnp.exp(l - mx) for l in ls]
        den = es[0] + es[1] + es[2]
        ws = [e / den for e in es]
        dws = [_segsum64(dob_ * o[0]) for o in (o0, o1, o2)]
        mean = ws[0] * dws[0] + ws[1] * dws[1] + ws[2] * dws[2]
        for wg, dwg, do_ref, dd_ref in zip(ws, dws, (do0, do1, do2), (dd0, dd1, dd2)):
            do_ref[0] = wg * dob_
            dd_ref[0] = -wg * mean

    shp = jax.ShapeDtypeStruct((bsz, seq, w), F32)
    return _row_call(body, "merge_b_bwd", bsz, seq, [_rows(w)] * 7, [_rows(w)] * 6, [shp] * 6)(dob, *os_, *ls_)


def _delta_a(doa, oa, lse_a, sinks_exp):
    bsz, seq, w = oa.shape

    def body(do_ref, o_ref, l_ref, s_ref, dd_ref, acc_ref):
        dd = -_segsum64(do_ref[0].astype(F32) * o_ref[0])
        dd_ref[0] = dd
        _acc_rows(acc_ref, pl.program_id(1) == 0, [_colsum(dd * jnp.exp(s_ref[...] - l_ref[0]))])

    return _row_call(body, "delta_a", bsz, seq, [_rows(w), _rows(w), _rows(w), _whole((1, w))],
                     [_rows(w), _per_batch(8, w)],
                     [jax.ShapeDtypeStruct((bsz, seq, w), F32), jax.ShapeDtypeStruct((bsz, 8, w), F32)],
                     accumulates=True)(doa, oa, lse_a, sinks_exp)


def _swap_halves(v):
    lane = lax.broadcasted_iota(jnp.int32, v.shape, 1)
    return jnp.where((lane % HEAD_DIM) < HEAD_DIM // 2, pltpu.roll(v, LANES - HEAD_DIM // 2, 1),
                     pltpu.roll(v, HEAD_DIM // 2, 1))


def _rope(v, cos, sin, sign=1.0):
    out = []
    for c in range(v.shape[1] // LANES):
        part = v[:, c * LANES:(c + 1) * LANES]
        out.append(part * cos + sign * (_swap_halves(part) * sin))
    return jnp.concatenate(out, axis=1) if len(out) > 1 else out[0]


def _half_mask(shape, half):
    lane = lax.broadcasted_iota(jnp.int32, shape, len(shape) - 1) % LANES
    return (lane < HEAD_DIM) if half == 0 else (lane >= HEAD_DIM)


def _dup_half(v, half):
    return jnp.where(_half_mask(v.shape, half), v, pltpu.roll(v, HEAD_DIM, 1))


def _fold_halves(v):
    return v + pltpu.roll(v, HEAD_DIM, 1)


def _pick_halves(lo_rows, hi_rows):
    return jnp.where(_half_mask(lo_rows.shape, 0), lo_rows, hi_rows)


def _stack_masked(v, pairs):
    parts = []
    for c in pairs:
        pair = v[:, c * LANES:(c + 1) * LANES]
        parts += [jnp.where(_half_mask(pair.shape, half), pair, 0.0) for half in (0, 1)]
    return jnp.concatenate(parts, axis=0)


def _stack_pair_cols(v, pairs):
    return jnp.concatenate([v[:, c * LANES + half * HEAD_DIM:c * LANES + half * HEAD_DIM + 1] for c in pairs for half in (0, 1)],
                           axis=0)


ATTN_UNITS = 16


def _class_rows(r):
    return [pl.ds(0, QBLOCK)] if r == 1 else [pl.ds(rho, QBLOCK, stride=r) for rho in range(r)]


def _band_mask(nrows, nk, blk, n_back, has_prev):
    qi = lax.broadcasted_iota(jnp.int32, (nrows, nk), 0) % QBLOCK
    ki = lax.broadcasted_iota(jnp.int32, (nrows, nk), 1)
    if has_prev:
        dist = qi + QBLOCK - ki
        return (dist >= 0) & (dist <= n_back) & ((ki >= QBLOCK) | (blk > 0))
    dist = qi - ki
    return (dist >= 0) & (dist <= n_back)


def _attn_fwd(q_arr, k_arr, v_arr, *, name, npair, gqa, q_col, k_col, v_col, nchunk, r, n_back, sinks=None):
    bsz, seq, _ = q_arr.shape
    rr = QBLOCK * r
    nblk = seq // rr
    qw = npair * LANES
    kw = LANES if gqa else qw
    has_prev = nblk > 1
    has_sink = sinks is not None
    scale = HEAD_DIM ** -0.5

    def body(*refs):
        refs = list(refs)
        q_ref, kc_ref, vc_ref = refs[:3]
        pos = 3
        if has_prev:
            kp_ref, vp_ref = refs[pos:pos + 2]
            pos += 2
        if has_sink:
            sink_ref = refs[pos]
            pos += 1
        o_ref, lse_ref = refs[pos:pos + 2]
        blk = pl.program_id(2)
        nk = (2 if has_prev else 1) * QBLOCK
        valid = _band_mask(QBLOCK, nk, blk, n_back, has_prev)
        per = npair // 2
        classes = _class_rows(r)
        step = max(1, ATTN_UNITS // (2 * npair))
        for first in range(0, len(classes), step):
            batch = classes[first:first + step]
            units = []
            for ci, rows in enumerate(batch):
                q = q_ref[0, rows, :] * scale
                k, v = kc_ref[0, rows, :], vc_ref[0, rows, :]
                if has_prev:
                    k = jnp.concatenate([kp_ref[0, rows, :], k], axis=0)
                    v = jnp.concatenate([vp_ref[0, rows, :], v], axis=0)
                if gqa:
                    kdup = [_dup_half(k, hk).astype(BF16) for hk in range(2)]
                    vdup = [_dup_half(v, hk) for hk in range(2)]
                for c in range(npair):
                    sl = slice(c * LANES, (c + 1) * LANES)
                    qc = q[:, sl]
                    kc, vc = (kdup[c // per], vdup[c // per]) if gqa else (k[:, sl].astype(BF16), v[:, sl])
                    for half in (0, 1):
                        qm = jnp.where(_half_mask(qc.shape, half), qc, 0.0).astype(BF16)
                        vm = jnp.where(_half_mask(vc.shape, half), vc, 0.0).astype(BF16)
                        s = lax.dot_general(qm, kc, _DIMS["nt"], preferred_element_type=F32)
                        units.append(dict(ci=ci, c=c, half=half, s=s, vm=vm, sk=sink_ref[2 * c + half] if has_sink else None))
            for u in units:
                s = jnp.where(valid, u["s"], NEG_INF)
                m = jnp.max(s, axis=1, keepdims=True)
                if has_sink:
                    m = jnp.maximum(m, u["sk"])
                p = jnp.exp(s - m)
                den = jnp.sum(p, axis=1, keepdims=True)
                if has_sink:
                    den = den + jnp.exp(u["sk"] - m)
                u.update(p=p.astype(BF16), den=den, lse=m + jnp.log(den))
            for u in units:
                u["o"] = jnp.dot(u["p"], u["vm"], preferred_element_type=F32) / u["den"]
            for ci, rows in enumerate(batch):
                outs, lses = [None] * npair, [None] * npair
                for u in units:
                    if u["ci"] != ci:
                        continue
                    c, o = u["c"], u["o"]
                    lse = jnp.broadcast_to(u["lse"], o.shape)
                    outs[c] = o if u["half"] == 0 else outs[c] + o
                    lses[c] = lse if u["half"] == 0 else _pick_halves(lses[c], lse)
                o_ref[0, rows, :] = jnp.concatenate(outs, axis=1) if npair > 1 else outs[0]
                lse_ref[0, rows, :] = jnp.concatenate(lses, axis=1) if npair > 1 else lses[0]

    def cur(width, col0):
        return pl.BlockSpec((1, rr, width), lambda b, c, i: (b, i, col0 + c))

    def prev(width, col0):
        return pl.BlockSpec((1, rr, width), lambda b, c, i: (b, jnp.maximum(i - 1, 0), col0 + c))

    in_specs = [cur(qw, q_col), cur(kw, k_col), cur(kw, v_col)]
    args = [q_arr, k_arr, v_arr]
    if has_prev:
        in_specs += [prev(kw, k_col), prev(kw, v_col)]
        args += [k_arr, v_arr]
    if has_sink:
        in_specs.append(pl.BlockSpec(memory_space=pltpu.SMEM))
        args.append(sinks)
    return pl.pallas_call(
        body,
        name=name,
        grid=(bsz, nchunk, nblk),
        in_specs=in_specs,
        out_specs=[pl.BlockSpec((1, rr, qw), lambda b, c, i: (b, i, c))] * 2,
        out_shape=[jax.ShapeDtypeStruct((bsz, seq, nchunk * qw), F32)] * 2,
        compiler_params=_params("parallel", "parallel", "parallel"),
    )(*args)


def _attn_bwd(q_arr, k_arr, v_arr, cos, sin, do, lse, dd, *, name, npair, gqa, q_col, k_col, v_col, nchunk, r, n_back,
              token=None):
    bsz, seq, _ = q_arr.shape
    rr = QBLOCK * r
    nblk = seq // rr
    qw = npair * LANES
    kw = LANES if gqa else qw
    has_next = nblk > 1
    has_token = token is not None
    scale = HEAD_DIM ** -0.5

    def body(*refs):
        refs = list(refs)
        k_ref, v_ref, c_ref, s_ref = refs[:4]
        tile_refs = [refs[4:8]]
        pos = 8
        if has_next:
            tile_refs.append(refs[pos:pos + 4])
            pos += 4
        if has_token:
            pos += 1
        dq_ref, dk_ref, dv_ref = refs[pos:pos + 3]
        carry_ref = refs[pos + 3]
        blk = pl.program_id(2)
        if has_next:
            @pl.when(blk == 0)
            def _():
                carry_ref[...] = jnp.zeros_like(carry_ref)

        nrows = (npair if gqa else 1) * QBLOCK
        qi = lax.broadcasted_iota(jnp.int32, (nrows, QBLOCK), 0) % QBLOCK
        ki = lax.broadcasted_iota(jnp.int32, (nrows, QBLOCK), 1)
        valids = [qi >= ki, (qi + QBLOCK - ki <= n_back) & (blk + 1 < nblk)]
        per = npair // 2
        ntile = len(tile_refs)
        cat = lambda parts: jnp.concatenate(parts, axis=1) if len(parts) > 1 else parts[0]
        classes = _class_rows(r)
        step = max(1, ATTN_UNITS // (ntile * (2 if gqa else 2 * npair)))
        for first in range(0, len(classes), step):
            batch = classes[first:first + step]
            units = []
            for ci, rows in enumerate(batch):
                tiles = [(q_ref[0, rows, :] * scale, do_ref[0, rows, :], l_ref[0, rows, :], d_ref[0, rows, :])
                         for q_ref, do_ref, l_ref, d_ref in tile_refs]
                k, v = k_ref[0, rows, :], v_ref[0, rows, :]
                if gqa:
                    for hk in range(2):
                        pairs = list(range(hk * per, (hk + 1) * per))
                        kd, vd = _dup_half(k, hk).astype(BF16), _dup_half(v, hk).astype(BF16)
                        for t, (q, do_, l_, d_) in enumerate(tiles):
                            units.append(dict(ci=ci, t=t, hk=hk, pairs=pairs, qs=_stack_masked(q, pairs).astype(BF16),
                                              dos=_stack_masked(do_, pairs).astype(BF16), lcol=_stack_pair_cols(l_, pairs),
                                              dcol=_stack_pair_cols(d_, pairs), kmat=kd, vmat=vd, kdq=kd))
                else:
                    for c in range(npair):
                        sl = slice(c * LANES, (c + 1) * LANES)
                        kc, vcb = k[:, sl], v[:, sl].astype(BF16)
                        kcb = kc.astype(BF16)
                        for t, (q, do_, l_, d_) in enumerate(tiles):
                            for half in (0, 1):
                                hm = _half_mask(kc.shape, half)
                                col = c * LANES + half * HEAD_DIM
                                units.append(dict(ci=ci, t=t, c=c, half=half, qs=jnp.where(hm, q[:, sl], 0.0).astype(BF16),
                                                  dos=jnp.where(hm, do_[:, sl], 0.0).astype(BF16), lcol=l_[:, col:col + 1],
                                                  dcol=d_[:, col:col + 1], kmat=kcb, vmat=vcb,
                                                  kdq=jnp.where(hm, kc, 0.0).astype(BF16)))
            for u in units:
                u["s"] = lax.dot_general(u["qs"], u["kmat"], _DIMS["nt"], preferred_element_type=F32)
                u["dp"] = lax.dot_general(u["dos"], u["vmat"], _DIMS["nt"], preferred_element_type=F32)
            for u in units:
                p = jnp.exp(jnp.where(valids[u["t"]], u["s"], NEG_INF) - u["lcol"])
                u["ds"] = (p * (u["dp"] + u["dcol"])).astype(BF16)
                u["p"] = p.astype(BF16)
            for u in units:
                u["dv"] = lax.dot_general(u["p"], u["dos"], _DIMS["tn"], preferred_element_type=F32)
                u["dk"] = lax.dot_general(u["ds"], u["qs"], _DIMS["tn"], preferred_element_type=F32)
                u["dq"] = jnp.dot(u["ds"], u["kdq"], preferred_element_type=F32) * scale
            for ci, rows in enumerate(batch):
                mine = [u for u in units if u["ci"] == ci]
                dq = [[None] * npair for _ in range(ntile)]
                if gqa:
                    dk_out = dv_out = None
                    for hk in range(2):
                        us = [u for u in mine if u["hk"] == hk]
                        for u in us:
                            for i, c in enumerate(u["pairs"]):
                                dq[u["t"]][c] = _pick_halves(u["dq"][2 * i * QBLOCK:(2 * i + 1) * QBLOCK],
                                                             u["dq"][(2 * i + 1) * QBLOCK:(2 * i + 2) * QBLOCK])
                        dk_h = _fold_halves(functools.reduce(jnp.add, [u["dk"] for u in us]))
                        dv_h = _fold_halves(functools.reduce(jnp.add, [u["dv"] for u in us]))
                        dk_out = dk_h if hk == 0 else _pick_halves(dk_out, dk_h)
                        dv_out = dv_h if hk == 0 else _pick_halves(dv_out, dv_h)
                else:
                    dks, dvs = [], []
                    for c in range(npair):
                        us = [u for u in mine if u["c"] == c]
                        dks.append(functools.reduce(jnp.add, [u["dk"] for u in us]))
                        dvs.append(functools.reduce(jnp.add, [u["dv"] for u in us]))
                        for t in range(ntile):
                            dq[t][c] = functools.reduce(jnp.add, [u["dq"] for u in us if u["t"] == t])
                    dk_out, dv_out = cat(dks), cat(dvs)
                ck, sk_ = c_ref[0, rows, :], s_ref[0, rows, :]
                dk_ref[0, rows, :] = _rope(dk_out, ck, sk_, sign=-1.0)
                dv_ref[0, rows, :] = dv_out
                dq_cur = cat(dq[0])
                if has_next:
                    dq_cur = dq_cur + carry_ref[rows, :]
                    carry_ref[rows, :] = cat(dq[1])
                dq_ref[0, rows, :] = _rope(dq_cur, ck, sk_, sign=-1.0)

    def at(width, col0, shift):
        return pl.BlockSpec((1, rr, width), lambda b, c, i: (b, jnp.minimum(i + shift, nblk - 1), col0 + c))

    in_specs = [at(kw, k_col, 0), at(kw, v_col, 0), pl.BlockSpec((1, rr, LANES), lambda b, c, i: (b, i, 0)),
                pl.BlockSpec((1, rr, LANES), lambda b, c, i: (b, i, 0))]
    args = [k_arr, v_arr, cos, sin]
    for shift in (0, 1) if has_next else (0,):
        in_specs += [at(qw, q_col, shift), at(qw, 0, shift), at(qw, 0, shift), at(qw, 0, shift)]
        args += [q_arr, do, lse, dd]
    if has_token:
        in_specs.append(pl.BlockSpec(token.shape, lambda b, c, i: (0, 0)))
        args.append(token)
    return pl.pallas_call(
        body,
        name=name,
        grid=(bsz, nchunk, nblk),
        in_specs=in_specs,
        out_specs=[pl.BlockSpec((1, rr, qw), lambda b, c, i: (b, i, c)),
                   pl.BlockSpec((1, rr, kw), lambda b, c, i: (b, i, c)),
                   pl.BlockSpec((1, rr, kw), lambda b, c, i: (b, i, c))],
        out_shape=[jax.ShapeDtypeStruct((bsz, seq, nchunk * qw), F32),
                   jax.ShapeDtypeStruct((bsz, seq, nchunk * kw), F32),
                   jax.ShapeDtypeStruct((bsz, seq, nchunk * kw), F32)],
        scratch_shapes=[pltpu.VMEM((rr, qw) if has_next else (8, LANES), F32)],
        compiler_params=_params("parallel", "parallel", "arbitrary"),
    )(*args)


B_CHUNKS = {1: (4, 1), 4: (1, 4), 16: (1, 4)}


def _rope_tables(positions):
    half = HEAD_DIM // 2
    inv = ROPE_THETA ** (-jnp.arange(half, dtype=F32) / half)
    ang = positions.astype(F32)[..., None] * inv
    cos, sin = jnp.cos(ang), jnp.sin(ang)
    return jnp.concatenate([cos] * 4, axis=-1), jnp.concatenate([-sin, sin, -sin, sin], axis=-1)


def _layer_step(x, mod, positions, sinks, ln1_g, ln1_b, ln2_g, ln2_b, target, get_w_in, get_rest, hook):
    bsz, seq, d = x.shape
    ntok = bsz * seq
    flat = lambda v: v.reshape(ntok, v.shape[-1])
    unflat = lambda v: v.reshape(bsz, seq, v.shape[-1])
    cos, sin = _rope_tables(positions)
    mm = functools.partial(_matmul, tm=1024, tk=1024)
    scalar = lambda tok: 0.0 if tok is None else tok[0, 0]

    u1 = _modulate_in(x, mod)
    u1f = flat(u1)
    wint = get_w_in(u1)
    cosf, sinf = flat(cos), flat(sin)
    proj = functools.partial(_proj_rope, u1f, wint, cosf, sinf, tm=1024)
    qa = unflat(proj(n=1024, b_off=OFF_QA, rope_cols=1024, tn=512, name="proj_qa"))
    kva = unflat(proj(n=256, b_off=OFF_KVA, rope_cols=128, tn=128, name="proj_kva"))
    qkvb = unflat(proj(n=4608, b_off=OFF_QKVB, rope_cols=3072, tn=256, name="proj_qkvb"))
    gab = unflat(proj(n=2048, b_off=OFF_GAB, rope_cols=0, tn=256, name="proj_gab", out_dtype=BF16))

    sink_vec = sinks.reshape(A_Q_HEADS) + scalar(hook("projected", gab))
    a_kw = dict(npair=A_Q_HEADS // 2, gqa=True, q_col=0, k_col=0, v_col=1, nchunk=1, r=1, n_back=A_WINDOW - 1)
    oa, lse_a = _attn_fwd(qa, kva, kva, name="attn_a_fwd", sinks=sink_vec, **a_kw)
    rest = get_rest(oa)
    wba, wbbt, wo, wgut, wd = (rest[n] for n in ("w_branch_a", "w_branch_b", "w_o", "w_gate_up", "w_down"))
    ya = unflat(mm(flat(oa), wba, mode="nn", out_dtype=BF16, tn=512, name="branch_a"))

    b_kws, os_, ls_ = [], [], []
    for g, (window, r) in enumerate(B_PATTERNS):
        npair, nch = B_CHUNKS[r]
        per = B_HEADS_PER_GROUP // (2 * npair)
        nsec = len(B_PATTERNS) * per
        kw_ = dict(npair=npair, gqa=False, q_col=g * per, k_col=nsec + g * per, v_col=2 * nsec + g * per, nchunk=nch, r=r,
                   n_back=window // r)
        b_kws.append(kw_)
        o_g, l_g = _attn_fwd(qkvb, qkvb, qkvb, name=f"attn_b{g}_fwd", **kw_)
        os_.append(o_g)
        ls_.append(l_g)
    ob = _merge_b(os_, ls_)
    yb = unflat(mm(flat(ob), wbbt, mode="nt", out_dtype=BF16, tk=512, tn=512, name="branch_b"))
    merged = _gate_merge(gab, ya, yb)
    y1 = unflat(mm(flat(merged), wo, mode="nn", out_dtype=F32, tn=512, name="w_o"))
    h1, u2 = _ln1_fwd(x, y1, mod, ln1_g, ln1_b)
    h = unflat(mm(flat(u2), wgut, mode="nt", out_dtype=BF16, tn=512, name="gate_up"))
    a = _silu_mul(h)
    y2 = unflat(mm(flat(a), wd, mode="nn", out_dtype=F32, tk=D_FF, tn=512, name="down"))

    dy2, dh1a, acc2 = _ln2_loss_bwd(h1, y2, mod, ln2_g, ln2_b, target)
    dy2f = flat(dy2)
    da = unflat(mm(dy2f, wd, mode="nt", out_dtype=BF16, tn=256, name="down_dgrad"))
    g_wd = _matmul(flat(a), dy2f, mode="tn", out_dtype=BF16, tm=256, tn=1024, tk=ntok, name="down_wgrad")
    dh = _silu_mul_bwd(da, h)
    dhf = flat(dh)
    du2 = unflat(mm(dhf, wgut, mode="nn", out_dtype=F32, tk=D_FF, tn=512, name="gate_up_dgrad"))
    g_wgut = _matmul(dhf, flat(u2), mode="tn", out_dtype=BF16, tm=256, tn=1024, tk=ntok, name="gate_up_wgrad")
    dy1, dxa, acc1 = _ln1_bwd(du2, dh1a, x, y1, mod, ln1_g, ln1_b)
    dy1f = flat(dy1)
    dmerged = unflat(mm(dy1f, wo, mode="nt", out_dtype=BF16, tn=512, name="w_o_dgrad"))
    g_wo = _matmul(flat(merged), dy1f, mode="tn", out_dtype=BF16, tm=256, tn=1024, tk=ntok, name="w_o_wgrad")
    dya, dyb, dgab = _gate_merge_bwd(dmerged, gab, ya, yb)
    dyaf, dybf = flat(dya), flat(dyb)
    doa = unflat(mm(dyaf, wba, mode="nt", out_dtype=F32, tn=512, name="branch_a_dgrad"))
    g_wba = _matmul(flat(oa), dyaf, mode="tn", out_dtype=BF16, tm=256, tn=1024, tk=ntok, name="branch_a_wgrad")
    dob = unflat(mm(dybf, wbbt, mode="nn", out_dtype=BF16, tn=512, name="branch_b_dgrad"))
    g_wbbt = _matmul(dybf, flat(ob), mode="tn", out_dtype=BF16, tm=256, tn=512, tk=ntok, name="branch_b_wgrad")
    tok = hook("grads_rest", dict(w_branch_a=g_wba, w_branch_b=g_wbbt, w_o=g_wo, w_gate_up=g_wgut, w_down=g_wd))

    sinks_exp = jnp.repeat(sinks.reshape(1, A_Q_HEADS), HEAD_DIM, axis=1) + scalar(tok)
    dd_a, acc_s = _delta_a(doa, oa, lse_a, sinks_exp)
    tok = hook("delta_done", dd_a)
    dqa, dka, dva = _attn_bwd(qa, kva, kva, cos, sin, doa, lse_a, dd_a, name="attn_a_bwd", token=tok, **a_kw)
    merged_bwd = _merge_b_bwd(dob, os_, ls_)
    dqs, dks, dvs = [], [], []
    for g in range(len(B_PATTERNS)):
        dq_g, dk_g, dv_g = _attn_bwd(qkvb, qkvb, qkvb, cos, sin, merged_bwd[g], ls_[g], merged_bwd[3 + g],
                                     name=f"attn_b{g}_bwd", **b_kws[g])
        dqs.append(dq_g)
        dks.append(dk_g)
        dvs.append(dv_g)
    dproj = jnp.concatenate([t.astype(BF16) for t in [dqa, dka, dva] + dqs + dks + dvs] + [dgab], axis=-1)
    dprojf = flat(dproj)
    g_wint = _matmul(dprojf, u1f, mode="tn", out_dtype=BF16, tm=256, tn=1024, tk=ntok, name="w_in_wgrad")
    tok = hook("grads_w_in", dict(w_in=g_wint))
    du1 = unflat(_matmul(dprojf, wint, mode="nn", out_dtype=F32, tm=1024, tn=512, tk=wint.shape[0] // 2, name="w_in_dgrad",
                         token=tok))
    tok = hook("dgrad_done", du1)
    grad_x, acc0 = _grad_x(dxa, du1, x, mod + scalar(tok))

    loss_part = jnp.sum(acc2[:, 3, 0])
    dmod = jnp.stack([acc0[:, 1], acc0[:, 0], acc1[:, 2], acc1[:, 4], acc1[:, 3], acc2[:, 2]], axis=1)
    small = jnp.stack([acc1[:, 0].sum(0), acc1[:, 1].sum(0), acc2[:, 0].sum(0), acc2[:, 1].sum(0), acc_s[:, 0].sum(0)])
    return loss_part, grad_x, dmod, small


CHIP_FLIPS = (2, 4, 6)


def _my_place():
    return lax.axis_index("x"), lax.axis_index("y"), lax.axis_index("c")


def _flip(place, k):
    px, py, pc = place
    return (1 - px if k & 4 else px, 1 - py if k & 2 else py, 1 - pc if k & 1 else pc)


def _index(place):
    return 4 * place[0] + 2 * place[1] + place[2]


def _gather_small(v, name):
    rows, cols = v.shape

    def body(v_ref, out_ref, send_sems, recv_sems):
        me = _my_place()
        out_ref[_index(me)] = v_ref[...]
        copies = []
        for k in range(1, N_DEV):
            copies.append(pltpu.make_async_remote_copy(
                src_ref=v_ref, dst_ref=out_ref.at[_index(me)], send_sem=send_sems.at[k - 1], recv_sem=recv_sems.at[k - 1],
                device_id=_flip(me, k), device_id_type=MESH))
        for cp in copies:
            cp.start()
        for k in range(1, N_DEV):
            pltpu.make_async_remote_copy(
                src_ref=v_ref, dst_ref=out_ref.at[_index(_flip(me, k))], send_sem=send_sems.at[k - 1],
                recv_sem=recv_sems.at[k - 1], device_id=_flip(me, k), device_id_type=MESH).wait_recv()
        for cp in copies:
            cp.wait_send()

    return pl.pallas_call(
        body,
        name=name,
        out_shape=jax.ShapeDtypeStruct((N_DEV, rows, cols), v.dtype),
        in_specs=[pl.BlockSpec(memory_space=pltpu.VMEM)],
        out_specs=pl.BlockSpec(memory_space=pltpu.VMEM),
        scratch_shapes=[pltpu.SemaphoreType.DMA((N_DEV - 1,)), pltpu.SemaphoreType.DMA((N_DEV - 1,))],
        compiler_params=pltpu.CompilerParams(vmem_limit_bytes=VMEM_LIMIT_BYTES),
    )(v)


_HBM = pl.BlockSpec(memory_space=pltpu.HBM)
_SEM = pl.BlockSpec(memory_space=pltpu.SEMAPHORE)
_EFFECT = pltpu.SideEffectType.DATAFLOW_SIDE_EFFECTING


def _remote(src, dst, send_sems, recv_sems, j, to):
    return pltpu.make_async_remote_copy(src_ref=src, dst_ref=dst, send_sem=send_sems.at[j], recv_sem=recv_sems.at[j],
                                        device_id=to, device_id_type=MESH)


def _copies_start(name, src, land_shape, make_copies, nsem):
    def body(src_ref, land_ref, send_sems, recv_sems, src_thru, land_thru, token):
        for cp in make_copies(src_ref, land_ref, send_sems, recv_sems):
            cp.start()
        token[...] = jnp.zeros_like(token)

    sems = pltpu.SemaphoreType.DMA((nsem,))
    return pl.pallas_call(
        body, name=name,
        out_shape=(sems, sems, pltpu.HBM(src.shape, src.dtype), pltpu.HBM(land_shape, src.dtype),
                   jax.ShapeDtypeStruct((8, LANES), F32)),
        in_specs=(_HBM, _HBM), out_specs=(_SEM, _SEM, _HBM, _HBM, pl.BlockSpec(memory_space=pltpu.VMEM)),
        input_output_aliases={0: 2, 1: 3},
        compiler_params=pltpu.CompilerParams(has_side_effects=_EFFECT),
    )(pltpu.with_memory_space_constraint(src, pltpu.HBM),
      pltpu.with_memory_space_constraint(lax.empty(land_shape, src.dtype), pltpu.HBM))


def _copies_wait(name, started, make_copies, after):
    send_sems, recv_sems, src_thru, land_thru, _ = started

    def body(src_ref, land_ref, send_sems, recv_sems, after_ref, src_dead, got_ref):
        for cp in make_copies(src_ref, land_ref, send_sems, recv_sems):
            cp.wait_send()
            cp.wait_recv()

    return pl.pallas_call(
        body, name=name,
        out_shape=(pltpu.HBM(src_thru.shape, src_thru.dtype), pltpu.HBM(land_thru.shape, land_thru.dtype)),
        in_specs=(_HBM, _HBM, _SEM, _SEM, pl.BlockSpec(memory_space=pl.ANY)), out_specs=(_HBM, _HBM),
        input_output_aliases={0: 0, 1: 1},
        compiler_params=pltpu.CompilerParams(has_side_effects=_EFFECT),
    )(src_thru, land_thru, send_sems, recv_sems, after)


def _gather1_copies(src_ref, land_ref, send_sems, recv_sems):
    me = _my_place()
    return [_remote(src_ref, land_ref.at[_index(me)], send_sems, recv_sems, j, _flip(me, k)) for j, k in enumerate((1,) + CHIP_FLIPS)]


def _gather2_copies(src_ref, land_ref, send_sems, recv_sems):
    me = _my_place()
    return [_remote(src_ref.at[_index(_flip(me, k))], land_ref.at[j], send_sems, recv_sems, j, _flip(me, 1))
            for j, k in enumerate(CHIP_FLIPS)]


def _to_sibling_copies(src_ref, land_ref, send_sems, recv_sems):
    me = _my_place()
    return [_remote(src_ref.at[1 - me[2]], land_ref, send_sems, recv_sems, 0, _flip(me, 1))]


def _to_chips_copies(src_ref, land_ref, send_sems, recv_sems):
    me = _my_place()
    copies = []
    for j, k in enumerate(CHIP_FLIPS):
        to = _flip(me, k)
        copies.append(_remote(src_ref.at[2 * to[0] + to[1]], land_ref.at[j], send_sems, recv_sems, j, to))
    return copies


class _Gather:
    def __init__(self, name, packed):
        self.name, self.packed = name, packed
        self.rows = packed.shape[0]
        self.first = _copies_start(name + "_start", packed, (N_DEV, self.rows, D_MODEL), _gather1_copies, 4)
        self.token = self.first[4]

    def pass_on(self, after):
        _, land = _copies_wait(self.name + "_wait", self.first, _gather1_copies, after)
        self.second = _copies_start(self.name + "_pass_start", land, (3, self.rows, D_MODEL), _gather2_copies, 3)
        return self.second[4]

    def finish(self, after):
        full, passed = _copies_wait(self.name + "_pass_wait", self.second, _gather2_copies, after)
        me = _my_place()
        full = lax.dynamic_update_slice(full, self.packed[None], (_index(me), 0, 0))
        for j, k in enumerate(CHIP_FLIPS):
            full = lax.dynamic_update_slice(full, passed[j][None], (_index(_flip(me, k | 1)), 0, 0))
        return full


SUM_SPLIT = 2


def _sum_pairs(mine, theirs):
    nchip, rows, cols = mine.shape
    tile = rows // SUM_SPLIT
    spec = pl.BlockSpec((1, tile, cols), lambda q, t: (q, t, 0))

    def body(a_ref, b_ref, o_ref):
        o_ref[...] = (a_ref[...].astype(F32) + b_ref[...].astype(F32)).astype(BF16)

    return pl.pallas_call(body, name="grad_sum_sibling", grid=(nchip, SUM_SPLIT), in_specs=[spec, spec], out_specs=spec,
                          out_shape=jax.ShapeDtypeStruct(mine.shape, BF16), compiler_params=_params("parallel", "parallel"))(mine, theirs)


def _sum_final(own, got):
    rows, cols = own.shape
    tile = rows // SUM_SPLIT

    def body(a_ref, g_ref, o_ref):
        o_ref[...] = ((a_ref[...].astype(F32) + g_ref[0].astype(F32)) + g_ref[1].astype(F32)) + g_ref[2].astype(F32)

    return pl.pallas_call(
        body, name="grad_sum_chips", grid=(SUM_SPLIT,),
        in_specs=[pl.BlockSpec((tile, cols), lambda t: (t, 0)), pl.BlockSpec((3, tile, cols), lambda t: (0, t, 0))],
        out_specs=pl.BlockSpec((tile, cols), lambda t: (t, 0)),
        out_shape=jax.ShapeDtypeStruct((rows, cols), F32), compiler_params=_params("parallel"))(own, got)


class _ReduceScatter:
    def __init__(self, name, slabs):
        self.name = name
        rows = slabs.shape[1]
        self.rows = rows
        self.parts = slabs.reshape(4, 2, rows, D_MODEL).transpose(1, 0, 2, 3)
        self.first = _copies_start(name + "_sibling_start", self.parts, (4, rows, D_MODEL), _to_sibling_copies, 1)
        self.token = self.first[4]

    def between_chips(self, after):
        parts, theirs = _copies_wait(self.name + "_sibling_wait", self.first, _to_sibling_copies, after)
        mine = lax.dynamic_index_in_dim(parts, lax.axis_index("c"), 0, keepdims=False)
        chip_sum = _sum_pairs(mine, theirs)
        self.second = _copies_start(self.name + "_chips_start", chip_sum, (3, self.rows, D_MODEL), _to_chips_copies, 3)
        return self.second[4]

    def finish(self, after):
        chip_sum, got = _copies_wait(self.name + "_chips_wait", self.second, _to_chips_copies, after)
        my_chip = 2 * lax.axis_index("x") + lax.axis_index("y")
        return _sum_final(lax.dynamic_index_in_dim(chip_sum, my_chip, 0, keepdims=False), got)


def _ada_fwd(c_all, w, b):
    nb, _ = c_all.shape
    ncol = w.shape[1]

    def body(c_ref, w_ref, b_ref, o_ref):
        c = c_ref[...]
        act = (c * _sigmoid(c)).astype(BF16)
        o_ref[...] = jnp.dot(act, w_ref[...].astype(BF16), preferred_element_type=F32) + b_ref[...]

    return pl.pallas_call(body, name="ada_fwd", out_shape=jax.ShapeDtypeStruct((nb, ncol), F32),
                          compiler_params=pltpu.CompilerParams(vmem_limit_bytes=VMEM_LIMIT_BYTES))(c_all, w, b)


def _ada_wgrad(c_all_t, dmod_cols):
    d, nb = c_all_t.shape
    ncol = dmod_cols.shape[1]

    def body(ct_ref, dm_ref, o_ref):
        ct = ct_ref[...]
        act = (ct * _sigmoid(ct)).astype(BF16).astype(F32)
        dm = dm_ref[...].astype(BF16).astype(F32)
        acc = act[:, 0:1] * dm[0:1, :]
        for i in range(1, nb):
            acc = acc + act[:, i:i + 1] * dm[i:i + 1, :]
        o_ref[...] = acc

    return pl.pallas_call(body, name="ada_wgrad", out_shape=jax.ShapeDtypeStruct((d, ncol), F32),
                          compiler_params=pltpu.CompilerParams(vmem_limit_bytes=VMEM_LIMIT_BYTES))(c_all_t, dmod_cols)


SMALL_ROWS = 24


def _reduce_small(gathered):
    def body(g_ref, o_ref):
        acc = g_ref[0]
        for dev in range(1, N_DEV):
            acc = acc + g_ref[dev]
        o_ref[...] = acc

    return pl.pallas_call(body, name="reduce_small", out_shape=jax.ShapeDtypeStruct(gathered.shape[1:], F32))(gathered)


def _adamw(w, g, m, v, name):
    rows, cols = w.shape
    tile = rows
    for cand in (256, 128, 64, 32, 16, 8):
        if rows % cand == 0 and rows > cand:
            tile = cand
            break
    spec = pl.BlockSpec((tile, cols), lambda t: (t, 0))
    bc1 = 1.0 - ADAM_B1 ** ADAM_STEP
    bc2 = 1.0 - ADAM_B2 ** ADAM_STEP

    def body(w_ref, g_ref, m_ref, v_ref, d_ref, nm_ref, nv_ref):
        g_ = g_ref[...]
        nm = ADAM_B1 * m_ref[...] + (1.0 - ADAM_B1) * g_
        nv = ADAM_B2 * v_ref[...] + (1.0 - ADAM_B2) * (g_ * g_)
        d_ref[...] = -ADAM_LR * ((nm / bc1) / (jnp.sqrt(nv / bc2) + ADAM_EPS) + ADAM_WD * w_ref[...])
        nm_ref[...] = nm
        nv_ref[...] = nv

    shp = jax.ShapeDtypeStruct((rows, cols), F32)
    return pl.pallas_call(body, name=name, grid=(rows // tile,), in_specs=[spec] * 4, out_specs=[spec] * 3, out_shape=[shp] * 3,
                          compiler_params=_params("parallel"))(w, g, m, v)


_WEIGHTS = ("w_ada", "b_ada", "w_in", "sinks", "w_branch_a", "w_branch_b", "w_o", "ln1_g", "ln1_b", "w_gate_up", "w_down",
            "ln2_g", "ln2_b")
_TRANSPOSED = ("w_in", "w_branch_b", "w_gate_up")


def _pack_shard(name, w):
    w = w.astype(BF16)
    if name in _TRANSPOSED:
        w = w.T
    return w.reshape(-1, D_MODEL)


def _unpack_full(name, slab):
    if name == "w_branch_b":
        return slab.reshape(N_DEV * 128, 512)
    return slab.reshape(-1, D_MODEL)


def _unpack_group(group, gathered):
    full, off = {}, 0
    for n, r in group:
        full[n] = _unpack_full(n, gathered[:, off:off + r])
        off += r
    return full


def _unpack_grads(group, g_packed):
    g_w, off = {}, 0
    for n, r in group:
        part = g_packed[off:off + r]
        off += r
        if n == "w_branch_b":
            part = part.reshape(128, 512)
        g_w[n] = part.T if n in _TRANSPOSED else part
    return g_w


def kernel(x, c, positions, w_ada, b_ada, w_in, sinks, w_branch_a, w_branch_b, w_o, ln1_g, ln1_b, w_gate_up, w_down, ln2_g, ln2_b, loss_target, m_w_ada, m_b_ada, m_w_in, m_sinks, m_w_branch_a, m_w_branch_b, m_w_o, m_ln1_g, m_ln1_b, m_w_gate_up, m_w_down, m_ln2_g, m_ln2_b, v_w_ada, v_b_ada, v_w_in, v_sinks, v_w_branch_a, v_w_branch_b, v_w_o, v_ln1_g, v_ln1_b, v_w_gate_up, v_w_down, v_ln2_g, v_ln2_b):
    weights = dict(w_ada=w_ada, b_ada=b_ada, w_in=w_in, sinks=sinks, w_branch_a=w_branch_a, w_branch_b=w_branch_b, w_o=w_o,
                   ln1_g=ln1_g, ln1_b=ln1_b, w_gate_up=w_gate_up, w_down=w_down, ln2_g=ln2_g, ln2_b=ln2_b)
    m_in = dict(w_ada=m_w_ada, b_ada=m_b_ada, w_in=m_w_in, sinks=m_sinks, w_branch_a=m_w_branch_a, w_branch_b=m_w_branch_b,
                w_o=m_w_o, ln1_g=m_ln1_g, ln1_b=m_ln1_b, w_gate_up=m_w_gate_up, w_down=m_w_down, ln2_g=m_ln2_g, ln2_b=m_ln2_b)
    v_in = dict(w_ada=v_w_ada, b_ada=v_b_ada, w_in=v_w_in, sinks=v_sinks, w_branch_a=v_w_branch_a, w_branch_b=v_w_branch_b,
                w_o=v_w_o, ln1_g=v_ln1_g, ln1_b=v_ln1_b, w_gate_up=v_w_gate_up, w_down=v_w_down, ln2_g=v_ln2_g, ln2_b=v_ln2_b)
    bsz = x.shape[0]
    me = _index(_my_place())
    ada_cols = w_ada.shape[2]
    outs = {}

    def adamw(n, g):
        w2, m2, v2 = (t[n][0] if t[n].ndim == 3 else t[n] for t in (weights, m_in, v_in))
        shape = weights[n].shape
        dlt, nm, nv = _adamw(w2, g, m2, v2, "adamw_" + n)
        outs[n] = tuple(t.reshape(shape) for t in (g, dlt, nm, nv))
        return nv

    packed_in = jnp.concatenate([_pack_shard(n, weights[n][0]) for n, _ in GROUP_IN], axis=0)
    packed_rest = jnp.concatenate([_pack_shard(n, weights[n][0]) for n, _ in GROUP_REST], axis=0)
    c_all = _gather_small(jnp.pad(c, ((0, 8 - bsz), (0, 0))), "gather_c")[:, :bsz].reshape(N_DEV * bsz, D_MODEL)
    gather_in = _Gather("gather_w_in", lax.optimization_barrier((packed_in, c_all))[0])
    b_cols = lax.dynamic_slice_in_dim(b_ada, me * ada_cols, ada_cols, axis=1)
    mod_cols = _ada_fwd(c_all, w_ada[0], b_cols + gather_in.token[0, 0])
    mod_all = _gather_small(mod_cols, "gather_mod").transpose(1, 0, 2).reshape(N_DEV * bsz, 6, D_MODEL)
    gather_rest = _Gather("gather_rest", lax.optimization_barrier((packed_rest, mod_all))[0])
    mod = jnp.pad(lax.dynamic_slice_in_dim(mod_all, me * bsz, bsz, axis=0), ((0, 0), (0, 2), (0, 0)))
    mod = mod + gather_rest.token[0, 0]
    mod = mod + gather_in.pass_on(mod)[0, 0]

    scatters = {}

    def get_w_in(after):
        return _unpack_group(GROUP_IN, gather_in.finish(after))["w_in"]

    def get_rest(after):
        return _unpack_group(GROUP_REST, gather_rest.finish(after))

    def pack_grads(group, grads):
        return jnp.concatenate([grads[n].reshape(N_DEV, r, D_MODEL) for n, r in group], axis=1)

    def hook(point, value):
        if point == "projected":
            return gather_rest.pass_on(value)
        if point == "grads_rest":
            scatters["rest"] = _ReduceScatter("scatter_rest", pack_grads(GROUP_REST, value))
            return scatters["rest"].token
        if point == "delta_done":
            return scatters["rest"].between_chips(value)
        if point == "grads_w_in":
            scatters["in"] = _ReduceScatter("scatter_w_in", pack_grads(GROUP_IN, value))
            return scatters["in"].token
        if point == "dgrad_done":
            tok = scatters["in"].between_chips(value)
            for n, g in _unpack_grads(GROUP_REST, scatters["rest"].finish(tok)).items():
                adamw(n, g)
            return tok
        raise ValueError(point)

    loss_part, grad_x, dmod, small = _layer_step(x, mod, positions, sinks[0], ln1_g, ln1_b, ln2_g, ln2_b, loss_target,
                                                 get_w_in, get_rest, hook)
    loss = lax.psum(loss_part, MESH_AXES)

    rows = jnp.concatenate([dmod.reshape(bsz * 6, D_MODEL), small, jnp.zeros((SMALL_ROWS - bsz * 6 - 5, D_MODEL), F32)], axis=0)
    small_all = _gather_small(rows, "gather_small")
    sums = _reduce_small(small_all)
    dmod_all = small_all[:, :bsz * 6].reshape(N_DEV * bsz, 6 * D_MODEL)
    adamw("b_ada", functools.reduce(jnp.add, [sums[6 * i:6 * i + 6] for i in range(bsz)]).reshape(1, 6 * D_MODEL))
    for i, n in enumerate(("ln1_g", "ln1_b", "ln2_g", "ln2_b")):
        adamw(n, sums[12 + i][None])
    adamw("sinks", sums[16][::HEAD_DIM][None])
    dmod_cols = lax.dynamic_slice_in_dim(dmod_all, me * ada_cols, ada_cols, axis=1)
    last = adamw("w_ada", _ada_wgrad(c_all.T, dmod_cols))
    for n, g in _unpack_grads(GROUP_IN, scatters["in"].finish(last)).items():
        adamw(n, g)

    return (loss, grad_x, *[outs[n][0] for n in _WEIGHTS], *[outs[n][1] for n in _WEIGHTS], *[outs[n][2] for n in _WEIGHTS],
            *[outs[n][3] for n in _WEIGHTS])
```

```python
import functools

import jax
import jax.numpy as jnp
from jax import lax
from jax.experimental import pallas as pl
from jax.experimental.pallas import tpu as pltpu

F32 = jnp.float32
BF16 = jnp.bfloat16

D_MODEL = 1024
HEAD_DIM = 64
A_Q_HEADS = 16
A_WINDOW = 128
B_PATTERNS = ((128, 1), (512, 4), (2048, 16))
B_HEADS_PER_GROUP = 8
D_FF = 2816
QBLOCK = 128
ROPE_THETA = 10000.0
LN_EPS = 1e-5
DEEPNORM_ALPHA = 2.0 ** 0.25
NEG_INF = -1e30
ADAM_LR, ADAM_B1, ADAM_B2, ADAM_EPS, ADAM_WD, ADAM_STEP = 0.001, 0.9, 0.999, 1e-08, 0.01, 10

N_DEV = 8
MESH_AXES = ("x", "y", "c")
LANES = 128
VMEM_LIMIT_BYTES = 56 * 1024 * 1024
MESH = pl.DeviceIdType.MESH

OFF_QA, OFF_KVA, OFF_QKVB, OFF_GAB = 0, 1024, 1280, 5888
GROUP_IN = (("w_in", 992),)
GROUP_REST = (("w_branch_a", 128), ("w_branch_b", 64), ("w_o", 128), ("w_gate_up", 704), ("w_down", 352))


def _params(*sem):
    return pltpu.CompilerParams(dimension_semantics=sem, vmem_limit_bytes=VMEM_LIMIT_BYTES)


def _sigmoid(x):
    return 1.0 / (1.0 + jnp.exp(-x))


_DIMS = {"nn": (((1,), (0,)), ((), ())), "nt": (((1,), (1,)), ((), ())), "tn": (((0,), (0,)), ((), ()))}


def _matmul(a, b, *, mode, out_dtype, tm, tn, tk, name, n=None, b_off=0, token=None):
    if mode == "nn":
        (m, k), nn_ = a.shape, b.shape[1]
    elif mode == "nt":
        (m, k), nn_ = a.shape, (b.shape[0] if n is None else n)
    else:
        (k, m), nn_ = a.shape, b.shape[1]
    assert m % tm == 0 and nn_ % tn == 0 and k % tk == 0 and b_off % tn == 0, (name, m, nn_, k)
    nk = k // tk
    joff = b_off // tn
    if mode == "nn":
        a_spec = pl.BlockSpec((tm, tk), lambda i, j, kk: (i, kk))
        b_spec = pl.BlockSpec((tk, tn), lambda i, j, kk: (kk, j))
    elif mode == "nt":
        a_spec = pl.BlockSpec((tm, tk), lambda i, j, kk: (i, kk))
        b_spec = pl.BlockSpec((tn, tk), lambda i, j, kk: (j + joff, kk))
    else:
        a_spec = pl.BlockSpec((tk, tm), lambda i, j, kk: (kk, i))
        b_spec = pl.BlockSpec((tk, tn), lambda i, j, kk: (kk, j))
    dims = _DIMS[mode]
    has_token = token is not None

    def body(*refs):
        a_ref, b_ref = refs[:2]
        o_ref, acc_ref = refs[-2:]
        kk = pl.program_id(2)
        part = lax.dot_general(a_ref[...].astype(BF16), b_ref[...].astype(BF16), dims, preferred_element_type=F32)
        if nk == 1:
            o_ref[...] = part.astype(o_ref.dtype)
        else:
            @pl.when(kk == 0)
            def _():
                acc_ref[...] = part

            @pl.when(kk > 0)
            def _():
                acc_ref[...] += part

            @pl.when(kk == nk - 1)
            def _():
                o_ref[...] = acc_ref[...].astype(o_ref.dtype)

    in_specs, args = [a_spec, b_spec], [a, b]
    if has_token:
        in_specs.append(pl.BlockSpec(token.shape, lambda i, j, kk: (0, 0)))
        args.append(token)
    return pl.pallas_call(
        body,
        name=name,
        grid=(m // tm, nn_ // tn, nk),
        in_specs=in_specs,
        out_specs=pl.BlockSpec((tm, tn), lambda i, j, kk: (i, j)),
        out_shape=jax.ShapeDtypeStruct((m, nn_), out_dtype),
        scratch_shapes=[pltpu.VMEM((tm, tn) if nk > 1 else (8, LANES), F32)],
        compiler_params=_params("parallel", "parallel", "arbitrary"),
    )(*args)


def _proj_rope(a, bt, cos, sin, *, n, b_off, rope_cols, tm, tn, name, out_dtype=F32):
    m, k = a.shape
    assert m % tm == 0 and n % tn == 0 and b_off % tn == 0 and rope_cols % tn == 0, name
    joff = b_off // tn
    nrope = rope_cols // tn

    def body(a_ref, b_ref, c_ref, s_ref, o_ref):
        acc = lax.dot_general(a_ref[...], b_ref[...], _DIMS["nt"], preferred_element_type=F32)
        j = pl.program_id(1)

        @pl.when(j < nrope)
        def _():
            o_ref[...] = _rope(acc, c_ref[...], s_ref[...]).astype(o_ref.dtype)

        @pl.when(j >= nrope)
        def _():
            o_ref[...] = acc.astype(o_ref.dtype)

    table = pl.BlockSpec((tm, LANES), lambda i, j: (i, 0))
    return pl.pallas_call(
        body,
        name=name,
        grid=(m // tm, n // tn),
        in_specs=[pl.BlockSpec((tm, k), lambda i, j: (i, 0)), pl.BlockSpec((tn, k), lambda i, j: (j + joff, 0)), table, table],
        out_specs=pl.BlockSpec((tm, tn), lambda i, j: (i, j)),
        out_shape=jax.ShapeDtypeStruct((m, n), out_dtype),
        compiler_params=_params("parallel", "parallel"),
    )(a, bt, cos, sin)


ROW_TILE = 256


def _rows(width, col=0):
    return pl.BlockSpec((1, ROW_TILE, width), lambda b, t: (b, t, col))


def _per_batch(nrows, width):
    return pl.BlockSpec((1, nrows, width), lambda b, t: (b, 0, 0))


def _whole(shape):
    return pl.BlockSpec(shape, lambda b, t: (0,) * len(shape))


def _row_call(body, name, bsz, seq, in_specs, out_specs, out_shape, accumulates=False):
    return pl.pallas_call(
        body,
        name=name,
        grid=(bsz, seq // ROW_TILE),
        in_specs=in_specs,
        out_specs=out_specs,
        out_shape=out_shape,
        compiler_params=_params("parallel", "arbitrary" if accumulates else "parallel"),
    )


def _acc_rows(acc_ref, first, rows):
    @pl.when(first)
    def _():
        acc_ref[...] = jnp.zeros_like(acc_ref)

    for r, val in enumerate(rows):
        acc_ref[0, r:r + 1, :] += val


def _colsum(v):
    return jnp.sum(v, axis=0, keepdims=True)


def _ln_stats(z):
    mu = jnp.mean(z, axis=-1, keepdims=True)
    zc = z - mu
    var = jnp.mean(zc * zc, axis=-1, keepdims=True)
    rstd = lax.rsqrt(var + LN_EPS)
    return zc * rstd, rstd


def _ln_bwd(dxhat, xhat, rstd):
    m1 = jnp.mean(dxhat, axis=-1, keepdims=True)
    m2 = jnp.mean(dxhat * xhat, axis=-1, keepdims=True)
    return rstd * (dxhat - m1 - xhat * m2)


def _modulate_in(x, mod):
    bsz, seq, d = x.shape

    def body(x_ref, mod_ref, u_ref):
        u_ref[0] = (x_ref[0] * (1.0 + mod_ref[0, 1:2, :]) + mod_ref[0, 0:1, :]).astype(BF16)

    return _row_call(body, "modulate_in", bsz, seq, [_rows(d), _per_batch(8, d)], _rows(d),
                     jax.ShapeDtypeStruct((bsz, seq, d), BF16))(x, mod)


def _gate_merge(gab, ya, yb):
    bsz, seq, d = ya.shape

    def body(ga_ref, gb_ref, ya_ref, yb_ref, o_ref):
        ga, gb, ya_, yb_ = (r[0].astype(F32) for r in (ga_ref, gb_ref, ya_ref, yb_ref))
        o_ref[0] = (_sigmoid(ga) * ya_ + _sigmoid(gb) * yb_).astype(BF16)

    return _row_call(body, "gate_merge", bsz, seq, [_rows(d, 0), _rows(d, 1), _rows(d), _rows(d)], _rows(d),
                     jax.ShapeDtypeStruct((bsz, seq, d), BF16))(gab, gab, ya, yb)


def _ln1_fwd(x, y1, mod, g, b):
    bsz, seq, d = x.shape

    def body(x_ref, y_ref, mod_ref, g_ref, b_ref, h_ref, u_ref):
        z = DEEPNORM_ALPHA * x_ref[0] + (1.0 + mod_ref[0, 2:3, :]) * y_ref[0]
        xhat, _ = _ln_stats(z)
        h = xhat * g_ref[...] + b_ref[...]
        h_ref[0] = h
        u_ref[0] = (h * (1.0 + mod_ref[0, 4:5, :]) + mod_ref[0, 3:4, :]).astype(BF16)

    return _row_call(body, "ln1_fwd", bsz, seq,
                     [_rows(d), _rows(d), _per_batch(8, d), _whole((1, d)), _whole((1, d))],
                     [_rows(d), _rows(d)],
                     [jax.ShapeDtypeStruct((bsz, seq, d), F32), jax.ShapeDtypeStruct((bsz, seq, d), BF16)])(x, y1, mod, g, b)


def _silu_mul(h):
    bsz, seq, _ = h.shape

    def body(hg_ref, hu_ref, a_ref):
        hg = hg_ref[0].astype(F32)
        a_ref[0] = (hg * _sigmoid(hg) * hu_ref[0].astype(F32)).astype(BF16)

    return _row_call(body, "silu_mul", bsz, seq, [_rows(D_FF, 0), _rows(D_FF, 1)], _rows(D_FF),
                     jax.ShapeDtypeStruct((bsz, seq, D_FF), BF16))(h, h)


def _ln2_loss_bwd(h1, y2, mod, g, b, target):
    bsz, seq, d = h1.shape

    def body(h_ref, y_ref, mod_ref, g_ref, b_ref, t_ref, dy_ref, dh_ref, acc_ref):
        y = y_ref[0]
        gate = 1.0 + mod_ref[0, 5:6, :]
        z = DEEPNORM_ALPHA * h_ref[0] + gate * y
        xhat, rstd = _ln_stats(z)
        diff = xhat * g_ref[...] + b_ref[...] - t_ref[0]
        loss = 0.5 * jnp.sum(jnp.sum(diff * diff, axis=-1, keepdims=True) / d, axis=0, keepdims=True)
        dout = diff / d
        dz = _ln_bwd(dout * g_ref[...], xhat, rstd)
        dy_ref[0] = (gate * dz).astype(BF16)
        dh_ref[0] = DEEPNORM_ALPHA * dz
        _acc_rows(acc_ref, pl.program_id(1) == 0,
                  [_colsum(dout * xhat), _colsum(dout), _colsum(dz * y), jnp.broadcast_to(loss, (1, d))])

    return _row_call(body, "ln2_loss_bwd", bsz, seq,
                     [_rows(d), _rows(d), _per_batch(8, d), _whole((1, d)), _whole((1, d)), _rows(d)],
                     [_rows(d), _rows(d), _per_batch(8, d)],
                     [jax.ShapeDtypeStruct((bsz, seq, d), BF16), jax.ShapeDtypeStruct((bsz, seq, d), F32),
                      jax.ShapeDtypeStruct((bsz, 8, d), F32)], accumulates=True)(h1, y2, mod, g, b, target)


def _silu_mul_bwd(da, h):
    bsz, seq, _ = h.shape

    def body(da_ref, hg_ref, hu_ref, dh_ref):
        hg, da_ = hg_ref[0].astype(F32), da_ref[0].astype(F32)
        sg = _sigmoid(hg)
        dh_ref[0, :, :D_FF] = (da_ * hu_ref[0].astype(F32) * (sg * (1.0 + hg * (1.0 - sg)))).astype(BF16)
        dh_ref[0, :, D_FF:] = (da_ * (hg * sg)).astype(BF16)

    return _row_call(body, "silu_mul_bwd", bsz, seq, [_rows(D_FF), _rows(D_FF, 0), _rows(D_FF, 1)], _rows(2 * D_FF),
                     jax.ShapeDtypeStruct((bsz, seq, 2 * D_FF), BF16))(da, h, h)


def _ln1_bwd(du2, dh1a, x, y1, mod, g, b):
    bsz, seq, d = x.shape

    def body(du_ref, dh_ref, x_ref, y_ref, mod_ref, g_ref, b_ref, dy_ref, dx_ref, acc_ref):
        y, du = y_ref[0], du_ref[0]
        gate = 1.0 + mod_ref[0, 2:3, :]
        z = DEEPNORM_ALPHA * x_ref[0] + gate * y
        xhat, rstd = _ln_stats(z)
        h1 = xhat * g_ref[...] + b_ref[...]
        dh1 = dh_ref[0] + du * (1.0 + mod_ref[0, 4:5, :])
        dz = _ln_bwd(dh1 * g_ref[...], xhat, rstd)
        dy_ref[0] = (gate * dz).astype(BF16)
        dx_ref[0] = DEEPNORM_ALPHA * dz
        _acc_rows(acc_ref, pl.program_id(1) == 0,
                  [_colsum(dh1 * xhat), _colsum(dh1), _colsum(dz * y), _colsum(du * h1), _colsum(du)])

    return _row_call(body, "ln1_bwd", bsz, seq,
                     [_rows(d), _rows(d), _rows(d), _rows(d), _per_batch(8, d), _whole((1, d)), _whole((1, d))],
                     [_rows(d), _rows(d), _per_batch(8, d)],
                     [jax.ShapeDtypeStruct((bsz, seq, d), BF16), jax.ShapeDtypeStruct((bsz, seq, d), F32),
                      jax.ShapeDtypeStruct((bsz, 8, d), F32)], accumulates=True)(du2, dh1a, x, y1, mod, g, b)


def _gate_merge_bwd(dm, gab, ya, yb):
    bsz, seq, d = ya.shape

    def body(dm_ref, ga_ref, gb_ref, ya_ref, yb_ref, dya_ref, dyb_ref, dg_ref):
        dm_ = dm_ref[0].astype(F32)
        sa, sb = _sigmoid(ga_ref[0].astype(F32)), _sigmoid(gb_ref[0].astype(F32))
        dya_ref[0] = (dm_ * sa).astype(BF16)
        dyb_ref[0] = (dm_ * sb).astype(BF16)
        dg_ref[0, :, :d] = (dm_ * ya_ref[0].astype(F32) * sa * (1.0 - sa)).astype(BF16)
        dg_ref[0, :, d:] = (dm_ * yb_ref[0].astype(F32) * sb * (1.0 - sb)).astype(BF16)

    return _row_call(body, "gate_merge_bwd", bsz, seq,
                     [_rows(d), _rows(d, 0), _rows(d, 1), _rows(d), _rows(d)],
                     [_rows(d), _rows(d), _rows(2 * d)],
                     [jax.ShapeDtypeStruct((bsz, seq, d), BF16), jax.ShapeDtypeStruct((bsz, seq, d), BF16),
                      jax.ShapeDtypeStruct((bsz, seq, 2 * d), BF16)])(dm, gab, gab, ya, yb)


def _grad_x(dxa, du1, x, mod):
    bsz, seq, d = x.shape

    def body(dxa_ref, du_ref, x_ref, mod_ref, gx_ref, acc_ref):
        du = du_ref[0]
        gx_ref[0] = dxa_ref[0] + du * (1.0 + mod_ref[0, 1:2, :])
        _acc_rows(acc_ref, pl.program_id(1) == 0, [_colsum(du * x_ref[0]), _colsum(du)])

    return _row_call(body, "grad_x", bsz, seq, [_rows(d), _rows(d), _rows(d), _per_batch(8, d)],
                     [_rows(d), _per_batch(8, d)],
                     [jax.ShapeDtypeStruct((bsz, seq, d), F32), jax.ShapeDtypeStruct((bsz, 8, d), F32)],
                     accumulates=True)(dxa, du1, x, mod)


def _segsum64(v):
    rows, width = v.shape
    ri = lax.broadcasted_iota(jnp.int32, (LANES, LANES), 0) // HEAD_DIM
    ci = lax.broadcasted_iota(jnp.int32, (LANES, LANES), 1) // HEAD_DIM
    ones = jnp.where(ri == ci, 1.0, 0.0).astype(BF16)
    out = []
    for c in range(width // LANES):
        part = v[:, c * LANES:(c + 1) * LANES]
        hi = part.astype(BF16)
        lo = (part - hi.astype(F32)).astype(BF16)
        out.append(jnp.dot(hi, ones, preferred_element_type=F32) + jnp.dot(lo, ones, preferred_element_type=F32))
    return jnp.concatenate(out, axis=1) if len(out) > 1 else out[0]


def _merge_b(os_, ls_):
    bsz, seq, w = os_[0].shape

    def body(o0, o1, o2, l0, l1, l2, ob_ref):
        ls = [l0[0], l1[0], l2[0]]
        mx = jnp.maximum(jnp.maximum(ls[0], ls[1]), ls[2])
        es = [jnp.exp(l - mx) for l in ls]
        den = es[0] + es[1] + es[2]
        ob_ref[0] = ((es[0] / den) * o0[0] + (es[1] / den) * o1[0] + (es[2] / den) * o2[0]).astype(BF16)

    return _row_call(body, "merge_b", bsz, seq, [_rows(w)] * 6, _rows(w),
                     jax.ShapeDtypeStruct((bsz, seq, w), BF16))(*os_, *ls_)


def _merge_b_bwd(dob, os_, ls_):
    bsz, seq, w = os_[0].shape

    def body(dob_ref, o0, o1, o2, l0, l1, l2, do0, do1, do2, dd0, dd1, dd2):
        dob_ = dob_ref[0].astype(F32)
        ls = [l0[0], l1[0], l2[0]]
        mx = jnp.maximum(jnp.maximum(ls[0], ls[1]), ls[2])
        es = [jnp.exp(l - mx) for l in ls]
        den = es[0] + es[1] + es[2]
        ws = [e / den for e in es]
        dws = [_segsum64(dob_ * o[0]) for o in (o0, o1, o2)]
        mean = ws[0] * dws[0] + ws[1] * dws[1] + ws[2] * dws[2]
        for wg, dwg, do_ref, dd_ref in zip(ws, dws, (do0, do1, do2), (dd0, dd1, dd2)):
            do_ref[0] = wg * dob_
            dd_ref[0] = -wg * mean

    shp = jax.ShapeDtypeStruct((bsz, seq, w), F32)
    return _row_call(body, "merge_b_bwd", bsz, seq, [_rows(w)] * 7, [_rows(w)] * 6, [shp] * 6)(dob, *os_, *ls_)


def _delta_a(doa, oa, lse_a, sinks_exp):
    bsz, seq, w = oa.shape

    def body(do_ref, o_ref, l_ref, s_ref, dd_ref, acc_ref):
        dd = -_segsum64(do_ref[0].astype(F32) * o_ref[0])
        dd_ref[0] = dd
        _acc_rows(acc_ref, pl.program_id(1) == 0, [_colsum(dd * jnp.exp(s_ref[...] - l_ref[0]))])

    return _row_call(body, "delta_a", bsz, seq, [_rows(w), _rows(w), _rows(w), _whole((1, w))],
                     [_rows(w), _per_batch(8, w)],
                     [jax.ShapeDtypeStruct((bsz, seq, w), F32), jax.ShapeDtypeStruct((bsz, 8, w), F32)],
                     accumulates=True)(doa, oa, lse_a, sinks_exp)


def _swap_halves(v):
    src = lax.broadcasted_iota(jnp.int32, (LANES, LANES), 0)
    dst = lax.broadcasted_iota(jnp.int32, (LANES, LANES), 1)
    partner = jnp.where((dst % HEAD_DIM) < HEAD_DIM // 2, dst + HEAD_DIM // 2, dst - HEAD_DIM // 2)
    perm = jnp.where(src == partner, 1.0, 0.0).astype(BF16)
    hi = v.astype(BF16)
    lo = (v - hi.astype(F32)).astype(BF16)
    return jnp.dot(hi, perm, preferred_element_type=F32) + jnp.dot(lo, perm, preferred_element_type=F32)


def _rope(v, cos, sin, sign=1.0):
    out = []
    for c in range(v.shape[1] // LANES):
        part = v[:, c * LANES:(c + 1) * LANES]
        out.append(part * cos + sign * (_swap_halves(part) * sin))
    return jnp.concatenate(out, axis=1) if len(out) > 1 else out[0]


def _half_mask(shape, half):
    lane = lax.broadcasted_iota(jnp.int32, shape, len(shape) - 1) % LANES
    return (lane < HEAD_DIM) if half == 0 else (lane >= HEAD_DIM)


def _dup_half(v, half):
    return jnp.where(_half_mask(v.shape, half), v, pltpu.roll(v, HEAD_DIM, 1))


def _fold_halves(v):
    return v + pltpu.roll(v, HEAD_DIM, 1)


def _pick_halves(lo_rows, hi_rows):
    return jnp.where(_half_mask(lo_rows.shape, 0), lo_rows, hi_rows)


def _stack_masked(v, pairs):
    parts = []
    for c in pairs:
        pair = v[:, c * LANES:(c + 1) * LANES]
        parts += [jnp.where(_half_mask(pair.shape, half), pair, 0.0) for half in (0, 1)]
    return jnp.concatenate(parts, axis=0)


def _stack_pair_cols(v, pairs):
    return jnp.concatenate([v[:, c * LANES + half * HEAD_DIM:c * LANES + half * HEAD_DIM + 1] for c in pairs for half in (0, 1)],
                           axis=0)


ATTN_UNITS = 16


def _class_rows(r):
    return [pl.ds(0, QBLOCK)] if r == 1 else [pl.ds(rho, QBLOCK, stride=r) for rho in range(r)]


def _band_mask(nrows, nk, blk, n_back, has_prev):
    qi = lax.broadcasted_iota(jnp.int32, (nrows, nk), 0) % QBLOCK
    ki = lax.broadcasted_iota(jnp.int32, (nrows, nk), 1)
    if has_prev:
        dist = qi + QBLOCK - ki
        return (dist >= 0) & (dist <= n_back) & ((ki >= QBLOCK) | (blk > 0))
    dist = qi - ki
    return (dist >= 0) & (dist <= n_back)


def _attn_fwd(q_arr, k_arr, v_arr, *, name, npair, gqa, q_col, k_col, v_col, nchunk, r, n_back, sinks=None):
    bsz, seq, _ = q_arr.shape
    rr = QBLOCK * r
    nblk = seq // rr
    qw = npair * LANES
    kw = LANES if gqa else qw
    has_prev = nblk > 1
    has_sink = sinks is not None
    scale = HEAD_DIM ** -0.5

    def body(*refs):
        refs = list(refs)
        q_ref, kc_ref, vc_ref = refs[:3]
        pos = 3
        if has_prev:
            kp_ref, vp_ref = refs[pos:pos + 2]
            pos += 2
        if has_sink:
            sink_ref = refs[pos]
            pos += 1
        o_ref, lse_ref = refs[pos:pos + 2]
        blk = pl.program_id(2)
        nk = (2 if has_prev else 1) * QBLOCK
        valid = _band_mask(QBLOCK, nk, blk, n_back, has_prev)
        per = npair // 2
        classes = _class_rows(r)
        step = max(1, ATTN_UNITS // (2 * npair))
        for first in range(0, len(classes), step):
            batch = classes[first:first + step]
            units = []
            for ci, rows in enumerate(batch):
                q = q_ref[0, rows, :] * scale
                k, v = kc_ref[0, rows, :], vc_ref[0, rows, :]
                if has_prev:
                    k = jnp.concatenate([kp_ref[0, rows, :], k], axis=0)
                    v = jnp.concatenate([vp_ref[0, rows, :], v], axis=0)
                if gqa:
                    kdup = [_dup_half(k, hk).astype(BF16) for hk in range(2)]
                    vdup = [_dup_half(v, hk) for hk in range(2)]
                for c in range(npair):
                    sl = slice(c * LANES, (c + 1) * LANES)
                    qc = q[:, sl]
                    kc, vc = (kdup[c // per], vdup[c // per]) if gqa else (k[:, sl].astype(BF16), v[:, sl])
                    for half in (0, 1):
                        qm = jnp.where(_half_mask(qc.shape, half), qc, 0.0).astype(BF16)
                        vm = jnp.where(_half_mask(vc.shape, half), vc, 0.0).astype(BF16)
                        s = lax.dot_general(qm, kc, _DIMS["nt"], preferred_element_type=F32)
                        units.append(dict(ci=ci, c=c, half=half, s=s, vm=vm, sk=sink_ref[2 * c + half] if has_sink else None))
            for u in units:
                s = jnp.where(valid, u["s"], NEG_INF)
                m = jnp.max(s, axis=1, keepdims=True)
                if has_sink:
                    m = jnp.maximum(m, u["sk"])
                p = jnp.exp(s - m)
                den = jnp.sum(p, axis=1, keepdims=True)
                if has_sink:
                    den = den + jnp.exp(u["sk"] - m)
                u.update(p=p.astype(BF16), den=den, lse=m + jnp.log(den))
            for u in units:
                u["o"] = jnp.dot(u["p"], u["vm"], preferred_element_type=F32) / u["den"]
            for ci, rows in enumerate(batch):
                outs, lses = [None] * npair, [None] * npair
                for u in units:
                    if u["ci"] != ci:
                        continue
                    c, o = u["c"], u["o"]
                    lse = jnp.broadcast_to(u["lse"], o.shape)
                    outs[c] = o if u["half"] == 0 else outs[c] + o
                    lses[c] = lse if u["half"] == 0 else _pick_halves(lses[c], lse)
                o_ref[0, rows, :] = jnp.concatenate(outs, axis=1) if npair > 1 else outs[0]
                lse_ref[0, rows, :] = jnp.concatenate(lses, axis=1) if npair > 1 else lses[0]

    def cur(width, col0):
        return pl.BlockSpec((1, rr, width), lambda b, c, i: (b, i, col0 + c))

    def prev(width, col0):
        return pl.BlockSpec((1, rr, width), lambda b, c, i: (b, jnp.maximum(i - 1, 0), col0 + c))

    in_specs = [cur(qw, q_col), cur(kw, k_col), cur(kw, v_col)]
    args = [q_arr, k_arr, v_arr]
    if has_prev:
        in_specs += [prev(kw, k_col), prev(kw, v_col)]
        args += [k_arr, v_arr]
    if has_sink:
        in_specs.append(pl.BlockSpec(memory_space=pltpu.SMEM))
        args.append(sinks)
    return pl.pallas_call(
        body,
        name=name,
        grid=(bsz, nchunk, nblk),
        in_specs=in_specs,
        out_specs=[pl.BlockSpec((1, rr, qw), lambda b, c, i: (b, i, c))] * 2,
        out_shape=[jax.ShapeDtypeStruct((bsz, seq, nchunk * qw), F32)] * 2,
        compiler_params=_params("parallel", "parallel", "parallel"),
    )(*args)


def _attn_bwd(q_arr, k_arr, v_arr, cos, sin, do, lse, dd, *, name, npair, gqa, q_col, k_col, v_col, nchunk, r, n_back,
              token=None):
    bsz, seq, _ = q_arr.shape
    rr = QBLOCK * r
    nblk = seq // rr
    qw = npair * LANES
    kw = LANES if gqa else qw
    has_next = nblk > 1
    has_token = token is not None
    scale = HEAD_DIM ** -0.5

    def body(*refs):
        refs = list(refs)
        k_ref, v_ref, c_ref, s_ref = refs[:4]
        tile_refs = [refs[4:8]]
        pos = 8
        if has_next:
            tile_refs.append(refs[pos:pos + 4])
            pos += 4
        if has_token:
            pos += 1
        dq_ref, dk_ref, dv_ref = refs[pos:pos + 3]
        carry_ref = refs[pos + 3]
        blk = pl.program_id(2)
        if has_next:
            @pl.when(blk == 0)
            def _():
                carry_ref[...] = jnp.zeros_like(carry_ref)

        nrows = (npair if gqa else 1) * QBLOCK
        qi = lax.broadcasted_iota(jnp.int32, (nrows, QBLOCK), 0) % QBLOCK
        ki = lax.broadcasted_iota(jnp.int32, (nrows, QBLOCK), 1)
        valids = [qi >= ki, (qi + QBLOCK - ki <= n_back) & (blk + 1 < nblk)]
        per = npair // 2
        ntile = len(tile_refs)
        cat = lambda parts: jnp.concatenate(parts, axis=1) if len(parts) > 1 else parts[0]
        classes = _class_rows(r)
        step = max(1, ATTN_UNITS // (ntile * (2 if gqa else 2 * npair)))
        for first in range(0, len(classes), step):
            batch = classes[first:first + step]
            units = []
            for ci, rows in enumerate(batch):
                tiles = [(q_ref[0, rows, :] * scale, do_ref[0, rows, :], l_ref[0, rows, :], d_ref[0, rows, :])
                         for q_ref, do_ref, l_ref, d_ref in tile_refs]
                k, v = k_ref[0, rows, :], v_ref[0, rows, :]
                if gqa:
                    for hk in range(2):
                        pairs = list(range(hk * per, (hk + 1) * per))
                        kd, vd = _dup_half(k, hk).astype(BF16), _dup_half(v, hk).astype(BF16)
                        for t, (q, do_, l_, d_) in enumerate(tiles):
                            units.append(dict(ci=ci, t=t, hk=hk, pairs=pairs, qs=_stack_masked(q, pairs).astype(BF16),
                                              dos=_stack_masked(do_, pairs).astype(BF16), lcol=_stack_pair_cols(l_, pairs),
                                              dcol=_stack_pair_cols(d_, pairs), kmat=kd, vmat=vd, kdq=kd))
                else:
                    for c in range(npair):
                        sl = slice(c * LANES, (c + 1) * LANES)
                        kc, vcb = k[:, sl], v[:, sl].astype(BF16)
                        kcb = kc.astype(BF16)
                        for t, (q, do_, l_, d_) in enumerate(tiles):
                            for half in (0, 1):
                                hm = _half_mask(kc.shape, half)
                                col = c * LANES + half * HEAD_DIM
                                units.append(dict(ci=ci, t=t, c=c, half=half, qs=jnp.where(hm, q[:, sl], 0.0).astype(BF16),
                                                  dos=jnp.where(hm, do_[:, sl], 0.0).astype(BF16), lcol=l_[:, col:col + 1],
                                                  dcol=d_[:, col:col + 1], kmat=kcb, vmat=vcb,
                                                  kdq=jnp.where(hm, kc, 0.0).astype(BF16)))
            for u in units:
                u["s"] = lax.dot_general(u["qs"], u["kmat"], _DIMS["nt"], preferred_element_type=F32)
                u["dp"] = lax.dot_general(u["dos"], u["vmat"], _DIMS["nt"], preferred_element_type=F32)
            for u in units:
                p = jnp.exp(jnp.where(valids[u["t"]], u["s"], NEG_INF) - u["lcol"])
                u["ds"] = (p * (u["dp"] + u["dcol"])).astype(BF16)
                u["p"] = p.astype(BF16)
            for u in units:
                u["dv"] = lax.dot_general(u["p"], u["dos"], _DIMS["tn"], preferred_element_type=F32)
                u["dk"] = lax.dot_general(u["ds"], u["qs"], _DIMS["tn"], preferred_element_type=F32)
                u["dq"] = jnp.dot(u["ds"], u["kdq"], preferred_element_type=F32) * scale
            for ci, rows in enumerate(batch):
                mine = [u for u in units if u["ci"] == ci]
                dq = [[None] * npair for _ in range(ntile)]
                if gqa:
                    dk_out = dv_out = None
                    for hk in range(2):
                        us = [u for u in mine if u["hk"] == hk]
                        for u in us:
                            for i, c in enumerate(u["pairs"]):
                                dq[u["t"]][c] = _pick_halves(u["dq"][2 * i * QBLOCK:(2 * i + 1) * QBLOCK],
                                                             u["dq"][(2 * i + 1) * QBLOCK:(2 * i + 2) * QBLOCK])
                        dk_h = _fold_halves(functools.reduce(jnp.add, [u["dk"] for u in us]))
                        dv_h = _fold_halves(functools.reduce(jnp.add, [u["dv"] for u in us]))
                        dk_out = dk_h if hk == 0 else _pick_halves(dk_out, dk_h)
                        dv_out = dv_h if hk == 0 else _pick_halves(dv_out, dv_h)
                else:
                    dks, dvs = [], []
                    for c in range(npair):
                        us = [u for u in mine if u["c"] == c]
                        dks.append(functools.reduce(jnp.add, [u["dk"] for u in us]))
                        dvs.append(functools.reduce(jnp.add, [u["dv"] for u in us]))
                        for t in range(ntile):
                            dq[t][c] = functools.reduce(jnp.add, [u["dq"] for u in us if u["t"] == t])
                    dk_out, dv_out = cat(dks), cat(dvs)
                ck, sk_ = c_ref[0, rows, :], s_ref[0, rows, :]
                dk_ref[0, rows, :] = _rope(dk_out, ck, sk_, sign=-1.0)
                dv_ref[0, rows, :] = dv_out
                dq_cur = cat(dq[0])
                if has_next:
                    dq_cur = dq_cur + carry_ref[rows, :]
                    carry_ref[rows, :] = cat(dq[1])
                dq_ref[0, rows, :] = _rope(dq_cur, ck, sk_, sign=-1.0)

    def at(width, col0, shift):
        return pl.BlockSpec((1, rr, width), lambda b, c, i: (b, jnp.minimum(i + shift, nblk - 1), col0 + c))

    in_specs = [at(kw, k_col, 0), at(kw, v_col, 0), pl.BlockSpec((1, rr, LANES), lambda b, c, i: (b, i, 0)),
                pl.BlockSpec((1, rr, LANES), lambda b, c, i: (b, i, 0))]
    args = [k_arr, v_arr, cos, sin]
    for shift in (0, 1) if has_next else (0,):
        in_specs += [at(qw, q_col, shift), at(qw, 0, shift), at(qw, 0, shift), at(qw, 0, shift)]
        args += [q_arr, do, lse, dd]
    if has_token:
        in_specs.append(pl.BlockSpec(token.shape, lambda b, c, i: (0, 0)))
        args.append(token)
    return pl.pallas_call(
        body,
        name=name,
        grid=(bsz, nchunk, nblk),
        in_specs=in_specs,
        out_specs=[pl.BlockSpec((1, rr, qw), lambda b, c, i: (b, i, c)),
                   pl.BlockSpec((1, rr, kw), lambda b, c, i: (b, i, c)),
                   pl.BlockSpec((1, rr, kw), lambda b, c, i: (b, i, c))],
        out_shape=[jax.ShapeDtypeStruct((bsz, seq, nchunk * qw), F32),
                   jax.ShapeDtypeStruct((bsz, seq, nchunk * kw), F32),
                   jax.ShapeDtypeStruct((bsz, seq, nchunk * kw), F32)],
        scratch_shapes=[pltpu.VMEM((rr, qw) if has_next else (8, LANES), F32)],
        compiler_params=_params("parallel", "parallel", "arbitrary"),
    )(*args)


B_CHUNKS = {1: (4, 1), 4: (1, 4), 16: (1, 4)}


def _rope_tables(positions):
    half = HEAD_DIM // 2
    inv = ROPE_THETA ** (-jnp.arange(half, dtype=F32) / half)
    ang = positions.astype(F32)[..., None] * inv
    cos, sin = jnp.cos(ang), jnp.sin(ang)
    return jnp.concatenate([cos] * 4, axis=-1), jnp.concatenate([-sin, sin, -sin, sin], axis=-1)


def _layer_step(x, mod, positions, sinks, ln1_g, ln1_b, ln2_g, ln2_b, target, get_w_in, get_rest, hook):
    bsz, seq, d = x.shape
    ntok = bsz * seq
    flat = lambda v: v.reshape(ntok, v.shape[-1])
    unflat = lambda v: v.reshape(bsz, seq, v.shape[-1])
    cos, sin = _rope_tables(positions)
    mm = functools.partial(_matmul, tm=1024, tk=1024)
    scalar = lambda tok: 0.0 if tok is None else tok[0, 0]

    u1 = _modulate_in(x, mod)
    u1f = flat(u1)
    wint = get_w_in(u1)
    cosf, sinf = flat(cos), flat(sin)
    proj = functools.partial(_proj_rope, u1f, wint, cosf, sinf, tm=2048)
    qa = unflat(proj(n=1024, b_off=OFF_QA, rope_cols=1024, tn=512, name="proj_qa"))
    kva = unflat(proj(n=256, b_off=OFF_KVA, rope_cols=128, tn=128, name="proj_kva"))
    qkvb = unflat(proj(n=4608, b_off=OFF_QKVB, rope_cols=3072, tn=256, name="proj_qkvb"))
    gab = unflat(proj(n=2048, b_off=OFF_GAB, rope_cols=0, tn=256, name="proj_gab", out_dtype=BF16))

    sink_vec = sinks.reshape(A_Q_HEADS) + scalar(hook("projected", gab))
    a_kw = dict(npair=A_Q_HEADS // 2, gqa=True, q_col=0, k_col=0, v_col=1, nchunk=1, r=1, n_back=A_WINDOW - 1)
    oa, lse_a = _attn_fwd(qa, kva, kva, name="attn_a_fwd", sinks=sink_vec, **a_kw)
    rest = get_rest(oa)
    wba, wbbt, wo, wgut, wd = (rest[n] for n in ("w_branch_a", "w_branch_b", "w_o", "w_gate_up", "w_down"))
    ya = unflat(mm(flat(oa), wba, mode="nn", out_dtype=BF16, tn=512, name="branch_a"))

    b_kws, os_, ls_ = [], [], []
    for g, (window, r) in enumerate(B_PATTERNS):
        npair, nch = B_CHUNKS[r]
        per = B_HEADS_PER_GROUP // (2 * npair)
        nsec = len(B_PATTERNS) * per
        kw_ = dict(npair=npair, gqa=False, q_col=g * per, k_col=nsec + g * per, v_col=2 * nsec + g * per, nchunk=nch, r=r,
                   n_back=window // r)
        b_kws.append(kw_)
        o_g, l_g = _attn_fwd(qkvb, qkvb, qkvb, name=f"attn_b{g}_fwd", **kw_)
        os_.append(o_g)
        ls_.append(l_g)
    ob = _merge_b(os_, ls_)
    yb = unflat(mm(flat(ob), wbbt, mode="nt", out_dtype=BF16, tk=512, tn=512, name="branch_b"))
    merged = _gate_merge(gab, ya, yb)
    y1 = unflat(mm(flat(merged), wo, mode="nn", out_dtype=F32, tn=512, name="w_o"))
    h1, u2 = _ln1_fwd(x, y1, mod, ln1_g, ln1_b)
    h = unflat(mm(flat(u2), wgut, mode="nt", out_dtype=BF16, tn=D_FF // 2, name="gate_up"))
    a = _silu_mul(h)
    y2 = unflat(mm(flat(a), wd, mode="nn", out_dtype=F32, tk=D_FF, tn=512, name="down"))

    dy2, dh1a, acc2 = _ln2_loss_bwd(h1, y2, mod, ln2_g, ln2_b, target)
    dy2f = flat(dy2)
    da = unflat(mm(dy2f, wd, mode="nt", out_dtype=BF16, tn=D_FF // 2, name="down_dgrad"))
    g_wd = _matmul(flat(a), dy2f, mode="tn", out_dtype=BF16, tm=256, tn=1024, tk=ntok, name="down_wgrad")
    dh = _silu_mul_bwd(da, h)
    dhf = flat(dh)
    du2 = unflat(mm(dhf, wgut, mode="nn", out_dtype=F32, tk=D_FF, tn=512, name="gate_up_dgrad"))
    g_wgut = _matmul(dhf, flat(u2), mode="tn", out_dtype=BF16, tm=256, tn=1024, tk=ntok, name="gate_up_wgrad")
    dy1, dxa, acc1 = _ln1_bwd(du2, dh1a, x, y1, mod, ln1_g, ln1_b)
    dy1f = flat(dy1)
    dmerged = unflat(mm(dy1f, wo, mode="nt", out_dtype=BF16, tn=512, name="w_o_dgrad"))
    g_wo = _matmul(flat(merged), dy1f, mode="tn", out_dtype=BF16, tm=256, tn=1024, tk=ntok, name="w_o_wgrad")
    dya, dyb, dgab = _gate_merge_bwd(dmerged, gab, ya, yb)
    dyaf, dybf = flat(dya), flat(dyb)
    doa = unflat(mm(dyaf, wba, mode="nt", out_dtype=F32, tn=512, name="branch_a_dgrad"))
    g_wba = _matmul(flat(oa), dyaf, mode="tn", out_dtype=BF16, tm=256, tn=1024, tk=ntok, name="branch_a_wgrad")
    dob = unflat(mm(dybf, wbbt, mode="nn", out_dtype=BF16, tn=512, name="branch_b_dgrad"))
    g_wbbt = _matmul(dybf, flat(ob), mode="tn", out_dtype=BF16, tm=256, tn=512, tk=ntok, name="branch_b_wgrad")
    tok = hook("grads_rest", dict(w_branch_a=g_wba, w_branch_b=g_wbbt, w_o=g_wo, w_gate_up=g_wgut, w_down=g_wd))

    sinks_exp = jnp.repeat(sinks.reshape(1, A_Q_HEADS), HEAD_DIM, axis=1) + scalar(tok)
    dd_a, acc_s = _delta_a(doa, oa, lse_a, sinks_exp)
    tok = hook("delta_done", dd_a)
    dqa, dka, dva = _attn_bwd(qa, kva, kva, cos, sin, doa, lse_a, dd_a, name="attn_a_bwd", token=tok, **a_kw)
    merged_bwd = _merge_b_bwd(dob, os_, ls_)
    dqs, dks, dvs = [], [], []
    for g in range(len(B_PATTERNS)):
        dq_g, dk_g, dv_g = _attn_bwd(qkvb, qkvb, qkvb, cos, sin, merged_bwd[g], ls_[g], merged_bwd[3 + g],
                                     name=f"attn_b{g}_bwd", **b_kws[g])
        dqs.append(dq_g)
        dks.append(dk_g)
        dvs.append(dv_g)
    dproj = jnp.concatenate([t.astype(BF16) for t in [dqa, dka, dva] + dqs + dks + dvs] + [dgab], axis=-1)
    dprojf = flat(dproj)
    g_wint = _matmul(dprojf, u1f, mode="tn", out_dtype=BF16, tm=256, tn=1024, tk=ntok, name="w_in_wgrad")
    tok = hook("grads_w_in", dict(w_in=g_wint))
    du1 = unflat(_matmul(dprojf, wint, mode="nn", out_dtype=F32, tm=1024, tn=512, tk=wint.shape[0] // 2, name="w_in_dgrad",
                         token=tok))
    tok = hook("dgrad_done", du1)
    grad_x, acc0 = _grad_x(dxa, du1, x, mod + scalar(tok))

    loss_part = jnp.sum(acc2[:, 3, 0])
    dmod = jnp.stack([acc0[:, 1], acc0[:, 0], acc1[:, 2], acc1[:, 4], acc1[:, 3], acc2[:, 2]], axis=1)
    small = jnp.stack([acc1[:, 0].sum(0), acc1[:, 1].sum(0), acc2[:, 0].sum(0), acc2[:, 1].sum(0), acc_s[:, 0].sum(0)])
    return loss_part, grad_x, dmod, small


CHIP_FLIPS = (2, 4, 6)


def _my_place():
    return lax.axis_index("x"), lax.axis_index("y"), lax.axis_index("c")


def _flip(place, k):
    px, py, pc = place
    return (1 - px if k & 4 else px, 1 - py if k & 2 else py, 1 - pc if k & 1 else pc)


def _index(place):
    return 4 * place[0] + 2 * place[1] + place[2]


def _gather_small(v, name):
    rows, cols = v.shape

    def body(v_ref, out_ref, send_sems, recv_sems):
        me = _my_place()
        out_ref[_index(me)] = v_ref[...]
        copies = []
        for k in range(1, N_DEV):
            copies.append(pltpu.make_async_remote_copy(
                src_ref=v_ref, dst_ref=out_ref.at[_index(me)], send_sem=send_sems.at[k - 1], recv_sem=recv_sems.at[k - 1],
                device_id=_flip(me, k), device_id_type=MESH))
        for cp in copies:
            cp.start()
        for k in range(1, N_DEV):
            pltpu.make_async_remote_copy(
                src_ref=v_ref, dst_ref=out_ref.at[_index(_flip(me, k))], send_sem=send_sems.at[k - 1],
                recv_sem=recv_sems.at[k - 1], device_id=_flip(me, k), device_id_type=MESH).wait_recv()
        for cp in copies:
            cp.wait_send()

    return pl.pallas_call(
        body,
        name=name,
        out_shape=jax.ShapeDtypeStruct((N_DEV, rows, cols), v.dtype),
        in_specs=[pl.BlockSpec(memory_space=pltpu.VMEM)],
        out_specs=pl.BlockSpec(memory_space=pltpu.VMEM),
        scratch_shapes=[pltpu.SemaphoreType.DMA((N_DEV - 1,)), pltpu.SemaphoreType.DMA((N_DEV - 1,))],
        compiler_params=pltpu.CompilerParams(vmem_limit_bytes=VMEM_LIMIT_BYTES),
    )(v)


_HBM = pl.BlockSpec(memory_space=pltpu.HBM)
_SEM = pl.BlockSpec(memory_space=pltpu.SEMAPHORE)
_EFFECT = pltpu.SideEffectType.DATAFLOW_SIDE_EFFECTING


def _remote(src, dst, send_sems, recv_sems, j, to):
    return pltpu.make_async_remote_copy(src_ref=src, dst_ref=dst, send_sem=send_sems.at[j], recv_sem=recv_sems.at[j],
                                        device_id=to, device_id_type=MESH)


def _copies_start(name, src, land_shape, make_copies, nsem):
    def body(src_ref, land_ref, send_sems, recv_sems, src_thru, land_thru, token):
        for cp in make_copies(src_ref, land_ref, send_sems, recv_sems):
            cp.start()
        token[...] = jnp.zeros_like(token)

    sems = pltpu.SemaphoreType.DMA((nsem,))
    return pl.pallas_call(
        body, name=name,
        out_shape=(sems, sems, pltpu.HBM(src.shape, src.dtype), pltpu.HBM(land_shape, src.dtype),
                   jax.ShapeDtypeStruct((8, LANES), F32)),
        in_specs=(_HBM, _HBM), out_specs=(_SEM, _SEM, _HBM, _HBM, pl.BlockSpec(memory_space=pltpu.VMEM)),
        input_output_aliases={0: 2, 1: 3},
        compiler_params=pltpu.CompilerParams(has_side_effects=_EFFECT),
    )(pltpu.with_memory_space_constraint(src, pltpu.HBM),
      pltpu.with_memory_space_constraint(lax.empty(land_shape, src.dtype), pltpu.HBM))


def _copies_wait(name, started, make_copies, after):
    send_sems, recv_sems, src_thru, land_thru, _ = started

    def body(src_ref, land_ref, send_sems, recv_sems, after_ref, src_dead, got_ref):
        for cp in make_copies(src_ref, land_ref, send_sems, recv_sems):
            cp.wait_send()
            cp.wait_recv()

    return pl.pallas_call(
        body, name=name,
        out_shape=(pltpu.HBM(src_thru.shape, src_thru.dtype), pltpu.HBM(land_thru.shape, land_thru.dtype)),
        in_specs=(_HBM, _HBM, _SEM, _SEM, pl.BlockSpec(memory_space=pl.ANY)), out_specs=(_HBM, _HBM),
        input_output_aliases={0: 0, 1: 1},
        compiler_params=pltpu.CompilerParams(has_side_effects=_EFFECT),
    )(src_thru, land_thru, send_sems, recv_sems, after)


def _gather1_copies(src_ref, land_ref, send_sems, recv_sems):
    me = _my_place()
    return [_remote(src_ref, land_ref.at[_index(me)], send_sems, recv_sems, j, _flip(me, k)) for j, k in enumerate((1,) + CHIP_FLIPS)]


def _gather2_copies(src_ref, land_ref, send_sems, recv_sems):
    me = _my_place()
    return [_remote(src_ref.at[_index(_flip(me, k))], land_ref.at[j], send_sems, recv_sems, j, _flip(me, 1))
            for j, k in enumerate(CHIP_FLIPS)]


def _to_sibling_copies(src_ref, land_ref, send_sems, recv_sems):
    me = _my_place()
    return [_remote(src_ref.at[1 - me[2]], land_ref, send_sems, recv_sems, 0, _flip(me, 1))]


def _to_chips_copies(src_ref, land_ref, send_sems, recv_sems):
    me = _my_place()
    copies = []
    for j, k in enumerate(CHIP_FLIPS):
        to = _flip(me, k)
        copies.append(_remote(src_ref.at[2 * to[0] + to[1]], land_ref.at[j], send_sems, recv_sems, j, to))
    return copies


class _Gather:
    def __init__(self, name, packed):
        self.name, self.packed = name, packed
        self.rows = packed.shape[0]
        self.first = _copies_start(name + "_start", packed, (N_DEV, self.rows, D_MODEL), _gather1_copies, 4)
        self.token = self.first[4]

    def pass_on(self, after):
        _, land = _copies_wait(self.name + "_wait", self.first, _gather1_copies, after)
        self.second = _copies_start(self.name + "_pass_start", land, (3, self.rows, D_MODEL), _gather2_copies, 3)
        return self.second[4]

    def finish(self, after):
        full, passed = _copies_wait(self.name + "_pass_wait", self.second, _gather2_copies, after)
        me = _my_place()
        full = lax.dynamic_update_slice(full, self.packed[None], (_index(me), 0, 0))
        for j, k in enumerate(CHIP_FLIPS):
            full = lax.dynamic_update_slice(full, passed[j][None], (_index(_flip(me, k | 1)), 0, 0))
        return full


SUM_SPLIT = 2


def _sum_pairs(mine, theirs):
    nchip, rows, cols = mine.shape
    tile = rows // SUM_SPLIT
    spec = pl.BlockSpec((1, tile, cols), lambda q, t: (q, t, 0))

    def body(a_ref, b_ref, o_ref):
        o_ref[...] = (a_ref[...].astype(F32) + b_ref[...].astype(F32)).astype(BF16)

    return pl.pallas_call(body, name="grad_sum_sibling", grid=(nchip, SUM_SPLIT), in_specs=[spec, spec], out_specs=spec,
                          out_shape=jax.ShapeDtypeStruct(mine.shape, BF16), compiler_params=_params("parallel", "parallel"))(mine, theirs)


def _sum_final(own, got):
    rows, cols = own.shape
    tile = rows // SUM_SPLIT

    def body(a_ref, g_ref, o_ref):
        o_ref[...] = ((a_ref[...].astype(F32) + g_ref[0].astype(F32)) + g_ref[1].astype(F32)) + g_ref[2].astype(F32)

    return pl.pallas_call(
        body, name="grad_sum_chips", grid=(SUM_SPLIT,),
        in_specs=[pl.BlockSpec((tile, cols), lambda t: (t, 0)), pl.BlockSpec((3, tile, cols), lambda t: (0, t, 0))],
        out_specs=pl.BlockSpec((tile, cols), lambda t: (t, 0)),
        out_shape=jax.ShapeDtypeStruct((rows, cols), F32), compiler_params=_params("parallel"))(own, got)


class _ReduceScatter:
    def __init__(self, name, slabs):
        self.name = name
        rows = slabs.shape[1]
        self.rows = rows
        self.parts = slabs.reshape(4, 2, rows, D_MODEL).transpose(1, 0, 2, 3)
        self.first = _copies_start(name + "_sibling_start", self.parts, (4, rows, D_MODEL), _to_sibling_copies, 1)
        self.token = self.first[4]

    def between_chips(self, after):
        parts, theirs = _copies_wait(self.name + "_sibling_wait", self.first, _to_sibling_copies, after)
        mine = lax.dynamic_index_in_dim(parts, lax.axis_index("c"), 0, keepdims=False)
        chip_sum = _sum_pairs(mine, theirs)
        self.second = _copies_start(self.name + "_chips_start", chip_sum, (3, self.rows, D_MODEL), _to_chips_copies, 3)
        return self.second[4]

    def finish(self, after):
        chip_sum, got = _copies_wait(self.name + "_chips_wait", self.second, _to_chips_copies, after)
        my_chip = 2 * lax.axis_index("x") + lax.axis_index("y")
        return _sum_final(lax.dynamic_index_in_dim(chip_sum, my_chip, 0, keepdims=False), got)


def _ada_fwd(c_all, w, b):
    nb, _ = c_all.shape
    ncol = w.shape[1]

    def body(c_ref, w_ref, b_ref, o_ref):
        c = c_ref[...]
        act = (c * _sigmoid(c)).astype(BF16)
        o_ref[...] = jnp.dot(act, w_ref[...].astype(BF16), preferred_element_type=F32) + b_ref[...]

    return pl.pallas_call(body, name="ada_fwd", out_shape=jax.ShapeDtypeStruct((nb, ncol), F32),
                          compiler_params=pltpu.CompilerParams(vmem_limit_bytes=VMEM_LIMIT_BYTES))(c_all, w, b)


def _ada_wgrad(c_all_t, dmod_cols):
    d, nb = c_all_t.shape
    ncol = dmod_cols.shape[1]

    def body(ct_ref, dm_ref, o_ref):
        ct = ct_ref[...]
        act = (ct * _sigmoid(ct)).astype(BF16).astype(F32)
        dm = dm_ref[...].astype(BF16).astype(F32)
        acc = act[:, 0:1] * dm[0:1, :]
        for i in range(1, nb):
            acc = acc + act[:, i:i + 1] * dm[i:i + 1, :]
        o_ref[...] = acc

    return pl.pallas_call(body, name="ada_wgrad", out_shape=jax.ShapeDtypeStruct((d, ncol), F32),
                          compiler_params=pltpu.CompilerParams(vmem_limit_bytes=VMEM_LIMIT_BYTES))(c_all_t, dmod_cols)


SMALL_ROWS = 24


def _reduce_small(gathered):
    def body(g_ref, o_ref):
        acc = g_ref[0]
        for dev in range(1, N_DEV):
            acc = acc + g_ref[dev]
        o_ref[...] = acc

    return pl.pallas_call(body, name="reduce_small", out_shape=jax.ShapeDtypeStruct(gathered.shape[1:], F32))(gathered)


def _adamw(w, g, m, v, name):
    rows, cols = w.shape
    tile = rows
    for cand in (256, 128, 64, 32, 16, 8):
        if rows % cand == 0 and rows > cand:
            tile = cand
            break
    spec = pl.BlockSpec((tile, cols), lambda t: (t, 0))
    bc1 = 1.0 - ADAM_B1 ** ADAM_STEP
    bc2 = 1.0 - ADAM_B2 ** ADAM_STEP

    def body(w_ref, g_ref, m_ref, v_ref, d_ref, nm_ref, nv_ref):
        g_ = g_ref[...]
        nm = ADAM_B1 * m_ref[...] + (1.0 - ADAM_B1) * g_
        nv = ADAM_B2 * v_ref[...] + (1.0 - ADAM_B2) * (g_ * g_)
        d_ref[...] = -ADAM_LR * ((nm / bc1) / (jnp.sqrt(nv / bc2) + ADAM_EPS) + ADAM_WD * w_ref[...])
        nm_ref[...] = nm
        nv_ref[...] = nv

    shp = jax.ShapeDtypeStruct((rows, cols), F32)
    return pl.pallas_call(body, name=name, grid=(rows // tile,), in_specs=[spec] * 4, out_specs=[spec] * 3, out_shape=[shp] * 3,
                          compiler_params=_params("parallel"))(w, g, m, v)


_WEIGHTS = ("w_ada", "b_ada", "w_in", "sinks", "w_branch_a", "w_branch_b", "w_o", "ln1_g", "ln1_b", "w_gate_up", "w_down",
            "ln2_g", "ln2_b")
_TRANSPOSED = ("w_in", "w_branch_b", "w_gate_up")


def _pack_shard(name, w):
    w = w.astype(BF16)
    if name in _TRANSPOSED:
        w = w.T
    return w.reshape(-1, D_MODEL)


def _unpack_full(name, slab):
    if name == "w_branch_b":
        return slab.reshape(N_DEV * 128, 512)
    return slab.reshape(-1, D_MODEL)


def _unpack_group(group, gathered):
    full, off = {}, 0
    for n, r in group:
        full[n] = _unpack_full(n, gathered[:, off:off + r])
        off += r
    return full


def _unpack_grads(group, g_packed):
    g_w, off = {}, 0
    for n, r in group:
        part = g_packed[off:off + r]
        off += r
        if n == "w_branch_b":
            part = part.reshape(128, 512)
        g_w[n] = part.T if n in _TRANSPOSED else part
    return g_w


def kernel(x, c, positions, w_ada, b_ada, w_in, sinks, w_branch_a, w_branch_b, w_o, ln1_g, ln1_b, w_gate_up, w_down, ln2_g, ln2_b, loss_target, m_w_ada, m_b_ada, m_w_in, m_sinks, m_w_branch_a, m_w_branch_b, m_w_o, m_ln1_g, m_ln1_b, m_w_gate_up, m_w_down, m_ln2_g, m_ln2_b, v_w_ada, v_b_ada, v_w_in, v_sinks, v_w_branch_a, v_w_branch_b, v_w_o, v_ln1_g, v_ln1_b, v_w_gate_up, v_w_down, v_ln2_g, v_ln2_b):
    weights = dict(w_ada=w_ada, b_ada=b_ada, w_in=w_in, sinks=sinks, w_branch_a=w_branch_a, w_branch_b=w_branch_b, w_o=w_o,
                   ln1_g=ln1_g, ln1_b=ln1_b, w_gate_up=w_gate_up, w_down=w_down, ln2_g=ln2_g, ln2_b=ln2_b)
    m_in = dict(w_ada=m_w_ada, b_ada=m_b_ada, w_in=m_w_in, sinks=m_sinks, w_branch_a=m_w_branch_a, w_branch_b=m_w_branch_b,
                w_o=m_w_o, ln1_g=m_ln1_g, ln1_b=m_ln1_b, w_gate_up=m_w_gate_up, w_down=m_w_down, ln2_g=m_ln2_g, ln2_b=m_ln2_b)
    v_in = dict(w_ada=v_w_ada, b_ada=v_b_ada, w_in=v_w_in, sinks=v_sinks, w_branch_a=v_w_branch_a, w_branch_b=v_w_branch_b,
                w_o=v_w_o, ln1_g=v_ln1_g, ln1_b=v_ln1_b, w_gate_up=v_w_gate_up, w_down=v_w_down, ln2_g=v_ln2_g, ln2_b=v_ln2_b)
    bsz = x.shape[0]
    me = _index(_my_place())
    ada_cols = w_ada.shape[2]
    outs = {}

    def adamw(n, g):
        w2, m2, v2 = (t[n][0] if t[n].ndim == 3 else t[n] for t in (weights, m_in, v_in))
        shape = weights[n].shape
        dlt, nm, nv = _adamw(w2, g, m2, v2, "adamw_" + n)
        outs[n] = tuple(t.reshape(shape) for t in (g, dlt, nm, nv))
        return nv

    packed_in = jnp.concatenate([_pack_shard(n, weights[n][0]) for n, _ in GROUP_IN], axis=0)
    packed_rest = jnp.concatenate([_pack_shard(n, weights[n][0]) for n, _ in GROUP_REST], axis=0)
    c_all = _gather_small(jnp.pad(c, ((0, 8 - bsz), (0, 0))), "gather_c")[:, :bsz].reshape(N_DEV * bsz, D_MODEL)
    gather_in = _Gather("gather_w_in", lax.optimization_barrier((packed_in, c_all))[0])
    b_cols = lax.dynamic_slice_in_dim(b_ada, me * ada_cols, ada_cols, axis=1)
    mod_cols = _ada_fwd(c_all, w_ada[0], b_cols + gather_in.token[0, 0])
    mod_all = _gather_small(mod_cols, "gather_mod").transpose(1, 0, 2).reshape(N_DEV * bsz, 6, D_MODEL)
    gather_rest = _Gather("gather_rest", lax.optimization_barrier((packed_rest, mod_all))[0])
    mod = jnp.pad(lax.dynamic_slice_in_dim(mod_all, me * bsz, bsz, axis=0), ((0, 0), (0, 2), (0, 0)))
    mod = mod + gather_rest.token[0, 0]
    mod = mod + gather_in.pass_on(mod)[0, 0]

    scatters = {}

    def get_w_in(after):
        return _unpack_group(GROUP_IN, gather_in.finish(after))["w_in"]

    def get_rest(after):
        return _unpack_group(GROUP_REST, gather_rest.finish(after))

    def pack_grads(group, grads):
        return jnp.concatenate([grads[n].reshape(N_DEV, r, D_MODEL) for n, r in group], axis=1)

    def hook(point, value):
        if point == "projected":
            return gather_rest.pass_on(value)
        if point == "grads_rest":
            scatters["rest"] = _ReduceScatter("scatter_rest", pack_grads(GROUP_REST, value))
            return scatters["rest"].token
        if point == "delta_done":
            return scatters["rest"].between_chips(value)
        if point == "grads_w_in":
            scatters["in"] = _ReduceScatter("scatter_w_in", pack_grads(GROUP_IN, value))
            return scatters["in"].token
        if point == "dgrad_done":
            tok = scatters["in"].between_chips(value)
            for n, g in _unpack_grads(GROUP_REST, scatters["rest"].finish(tok)).items():
                adamw(n, g)
            return tok
        raise ValueError(point)

    loss_part, grad_x, dmod, small = _layer_step(x, mod, positions, sinks[0], ln1_g, ln1_b, ln2_g, ln2_b, loss_target,
                                                 get_w_in, get_rest, hook)
    loss = lax.psum(loss_part, MESH_AXES)

    rows = jnp.concatenate([dmod.reshape(bsz * 6, D_MODEL), small, jnp.zeros((SMALL_ROWS - bsz * 6 - 5, D_MODEL), F32)], axis=0)
    small_all = _gather_small(rows, "gather_small")
    sums = _reduce_small(small_all)
    dmod_all = small_all[:, :bsz * 6].reshape(N_DEV * bsz, 6 * D_MODEL)
    adamw("b_ada", functools.reduce(jnp.add, [sums[6 * i:6 * i + 6] for i in range(bsz)]).reshape(1, 6 * D_MODEL))
    for i, n in enumerate(("ln1_g", "ln1_b", "ln2_g", "ln2_b")):
        adamw(n, sums[12 + i][None])
    adamw("sinks", sums[16][::HEAD_DIM][None])
    dmod_cols = lax.dynamic_slice_in_dim(dmod_all, me * ada_cols, ada_cols, axis=1)
    last = adamw("w_ada", _ada_wgrad(c_all.T, dmod_cols))
    for n, g in _unpack_grads(GROUP_IN, scatters["in"].finish(last)).items():
        adamw(n, g)

    return (loss, grad_x, *[outs[n][0] for n in _WEIGHTS], *[outs[n][1] for n in _WEIGHTS], *[outs[n][2] for n in _WEIGHTS],
            *[outs[n][3] for n in _WEIGHTS])
```

```python
import functools

import jax
import jax.numpy as jnp
from jax import lax
from jax.experimental import pallas as pl
from jax.experimental.pallas import tpu as pltpu

F32 = jnp.float32
BF16 = jnp.bfloat16

D_MODEL = 1024
HEAD_DIM = 64
A_Q_HEADS = 16
A_WINDOW = 128
B_PATTERNS = ((128, 1), (512, 4), (2048, 16))
B_HEADS_PER_GROUP = 8
D_FF = 2816
QBLOCK = 128
ROPE_THETA = 10000.0
LN_EPS = 1e-5
DEEPNORM_ALPHA = 2.0 ** 0.25
NEG_INF = -1e30
ADAM_LR, ADAM_B1, ADAM_B2, ADAM_EPS, ADAM_WD, ADAM_STEP = 0.001, 0.9, 0.999, 1e-08, 0.01, 10

N_DEV = 8
MESH_AXES = ("x", "y", "c")
LANES = 128
VMEM_LIMIT_BYTES = 56 * 1024 * 1024
MESH = pl.DeviceIdType.MESH

OFF_QA, OFF_KVA, OFF_QKVB, OFF_GAB = 0, 1024, 1280, 5888
GROUP_IN = (("w_in", 992),)
GROUP_REST = (("w_branch_a", 128), ("w_branch_b", 64), ("w_o", 128), ("w_gate_up", 704), ("w_down", 352))


def _params(*sem):
    return pltpu.CompilerParams(dimension_semantics=sem, vmem_limit_bytes=VMEM_LIMIT_BYTES)


def _sigmoid(x):
    return 1.0 / (1.0 + jnp.exp(-x))


_DIMS = {"nn": (((1,), (0,)), ((), ())), "nt": (((1,), (1,)), ((), ())), "tn": (((0,), (0,)), ((), ()))}


def _matmul(a, b, *, mode, out_dtype, tm, tn, tk, name, n=None, b_off=0, token=None):
    if mode == "nn":
        (m, k), nn_ = a.shape, b.shape[1]
    elif mode == "nt":
        (m, k), nn_ = a.shape, (b.shape[0] if n is None else n)
    else:
        (k, m), nn_ = a.shape, b.shape[1]
    assert m % tm == 0 and nn_ % tn == 0 and k % tk == 0 and b_off % tn == 0, (name, m, nn_, k)
    nk = k // tk
    joff = b_off // tn
    if mode == "nn":
        a_spec = pl.BlockSpec((tm, tk), lambda i, j, kk: (i, kk))
        b_spec = pl.BlockSpec((tk, tn), lambda i, j, kk: (kk, j))
    elif mode == "nt":
        a_spec = pl.BlockSpec((tm, tk), lambda i, j, kk: (i, kk))
        b_spec = pl.BlockSpec((tn, tk), lambda i, j, kk: (j + joff, kk))
    else:
        a_spec = pl.BlockSpec((tk, tm), lambda i, j, kk: (kk, i))
        b_spec = pl.BlockSpec((tk, tn), lambda i, j, kk: (kk, j))
    dims = _DIMS[mode]
    has_token = token is not None

    def body(*refs):
        a_ref, b_ref = refs[:2]
        o_ref, acc_ref = refs[-2:]
        kk = pl.program_id(2)
        part = lax.dot_general(a_ref[...].astype(BF16), b_ref[...].astype(BF16), dims, preferred_element_type=F32)
        if nk == 1:
            o_ref[...] = part.astype(o_ref.dtype)
        else:
            @pl.when(kk == 0)
            def _():
                acc_ref[...] = part

            @pl.when(kk > 0)
            def _():
                acc_ref[...] += part

            @pl.when(kk == nk - 1)
            def _():
                o_ref[...] = acc_ref[...].astype(o_ref.dtype)

    in_specs, args = [a_spec, b_spec], [a, b]
    if has_token:
        in_specs.append(pl.BlockSpec(token.shape, lambda i, j, kk: (0, 0)))
        args.append(token)
    return pl.pallas_call(
        body,
        name=name,
        grid=(m // tm, nn_ // tn, nk),
        in_specs=in_specs,
        out_specs=pl.BlockSpec((tm, tn), lambda i, j, kk: (i, j)),
        out_shape=jax.ShapeDtypeStruct((m, nn_), out_dtype),
        scratch_shapes=[pltpu.VMEM((tm, tn) if nk > 1 else (8, LANES), F32)],
        compiler_params=_params("parallel", "parallel", "arbitrary"),
    )(*args)


def _proj_rope(a, bt, cos, sin, *, n, b_off, rope_cols, tm, tn, name, out_dtype=F32):
    m, k = a.shape
    assert m % tm == 0 and n % tn == 0 and b_off % tn == 0 and rope_cols % tn == 0, name
    joff = b_off // tn
    nrope = rope_cols // tn

    def body(a_ref, b_ref, c_ref, s_ref, o_ref):
        acc = lax.dot_general(a_ref[...], b_ref[...], _DIMS["nt"], preferred_element_type=F32)
        j = pl.program_id(1)

        @pl.when(j < nrope)
        def _():
            o_ref[...] = _rope(acc, c_ref[...], s_ref[...]).astype(o_ref.dtype)

        @pl.when(j >= nrope)
        def _():
            o_ref[...] = acc.astype(o_ref.dtype)

    table = pl.BlockSpec((tm, LANES), lambda i, j: (i, 0))
    return pl.pallas_call(
        body,
        name=name,
        grid=(m // tm, n // tn),
        in_specs=[pl.BlockSpec((tm, k), lambda i, j: (i, 0)), pl.BlockSpec((tn, k), lambda i, j: (j + joff, 0)), table, table],
        out_specs=pl.BlockSpec((tm, tn), lambda i, j: (i, j)),
        out_shape=jax.ShapeDtypeStruct((m, n), out_dtype),
        compiler_params=_params("parallel", "parallel"),
    )(a, bt, cos, sin)


ROW_TILE = 256


def _rows(width, col=0):
    return pl.BlockSpec((1, ROW_TILE, width), lambda b, t: (b, t, col))


def _per_batch(nrows, width):
    return pl.BlockSpec((1, nrows, width), lambda b, t: (b, 0, 0))


def _whole(shape):
    return pl.BlockSpec(shape, lambda b, t: (0,) * len(shape))


def _row_call(body, name, bsz, seq, in_specs, out_specs, out_shape, accumulates=False):
    return pl.pallas_call(
        body,
        name=name,
        grid=(bsz, seq // ROW_TILE),
        in_specs=in_specs,
        out_specs=out_specs,
        out_shape=out_shape,
        compiler_params=_params("parallel", "arbitrary" if accumulates else "parallel"),
    )


def _acc_rows(acc_ref, first, rows):
    @pl.when(first)
    def _():
        acc_ref[...] = jnp.zeros_like(acc_ref)

    for r, val in enumerate(rows):
        acc_ref[0, r:r + 1, :] += val


def _colsum(v):
    return jnp.sum(v, axis=0, keepdims=True)


def _ln_stats(z):
    mu = jnp.mean(z, axis=-1, keepdims=True)
    zc = z - mu
    var = jnp.mean(zc * zc, axis=-1, keepdims=True)
    rstd = lax.rsqrt(var + LN_EPS)
    return zc * rstd, rstd


def _ln_bwd(dxhat, xhat, rstd):
    m1 = jnp.mean(dxhat, axis=-1, keepdims=True)
    m2 = jnp.mean(dxhat * xhat, axis=-1, keepdims=True)
    return rstd * (dxhat - m1 - xhat * m2)


def _modulate_in(x, mod):
    bsz, seq, d = x.shape

    def body(x_ref, mod_ref, u_ref):
        u_ref[0] = (x_ref[0] * (1.0 + mod_ref[0, 1:2, :]) + mod_ref[0, 0:1, :]).astype(BF16)

    return _row_call(body, "modulate_in", bsz, seq, [_rows(d), _per_batch(8, d)], _rows(d),
                     jax.ShapeDtypeStruct((bsz, seq, d), BF16))(x, mod)


def _gate_merge(gab, ya, yb):
    bsz, seq, d = ya.shape

    def body(ga_ref, gb_ref, ya_ref, yb_ref, o_ref):
        ga, gb, ya_, yb_ = (r[0].astype(F32) for r in (ga_ref, gb_ref, ya_ref, yb_ref))
        o_ref[0] = (_sigmoid(ga) * ya_ + _sigmoid(gb) * yb_).astype(BF16)

    return _row_call(body, "gate_merge", bsz, seq, [_rows(d, 0), _rows(d, 1), _rows(d), _rows(d)], _rows(d),
                     jax.ShapeDtypeStruct((bsz, seq, d), BF16))(gab, gab, ya, yb)


def _ln1_fwd(x, y1, mod, g, b):
    bsz, seq, d = x.shape

    def body(x_ref, y_ref, mod_ref, g_ref, b_ref, h_ref, u_ref):
        z = DEEPNORM_ALPHA * x_ref[0] + (1.0 + mod_ref[0, 2:3, :]) * y_ref[0]
        xhat, _ = _ln_stats(z)
        h = xhat * g_ref[...] + b_ref[...]
        h_ref[0] = h
        u_ref[0] = (h * (1.0 + mod_ref[0, 4:5, :]) + mod_ref[0, 3:4, :]).astype(BF16)

    return _row_call(body, "ln1_fwd", bsz, seq,
                     [_rows(d), _rows(d), _per_batch(8, d), _whole((1, d)), _whole((1, d))],
                     [_rows(d), _rows(d)],
                     [jax.ShapeDtypeStruct((bsz, seq, d), F32), jax.ShapeDtypeStruct((bsz, seq, d), BF16)])(x, y1, mod, g, b)


def _silu_mul(h):
    bsz, seq, _ = h.shape

    def body(hg_ref, hu_ref, a_ref):
        hg = hg_ref[0].astype(F32)
        a_ref[0] = (hg * _sigmoid(hg) * hu_ref[0].astype(F32)).astype(BF16)

    return _row_call(body, "silu_mul", bsz, seq, [_rows(D_FF, 0), _rows(D_FF, 1)], _rows(D_FF),
                     jax.ShapeDtypeStruct((bsz, seq, D_FF), BF16))(h, h)


def _ln2_loss_bwd(h1, y2, mod, g, b, target):
    bsz, seq, d = h1.shape

    def body(h_ref, y_ref, mod_ref, g_ref, b_ref, t_ref, dy_ref, dh_ref, acc_ref):
        y = y_ref[0]
        gate = 1.0 + mod_ref[0, 5:6, :]
        z = DEEPNORM_ALPHA * h_ref[0] + gate * y
        xhat, rstd = _ln_stats(z)
        diff = xhat * g_ref[...] + b_ref[...] - t_ref[0]
        loss = 0.5 * jnp.sum(jnp.sum(diff * diff, axis=-1, keepdims=True) / d, axis=0, keepdims=True)
        dout = diff / d
        dz = _ln_bwd(dout * g_ref[...], xhat, rstd)
        dy_ref[0] = (gate * dz).astype(BF16)
        dh_ref[0] = DEEPNORM_ALPHA * dz
        _acc_rows(acc_ref, pl.program_id(1) == 0,
                  [_colsum(dout * xhat), _colsum(dout), _colsum(dz * y), jnp.broadcast_to(loss, (1, d))])

    return _row_call(body, "ln2_loss_bwd", bsz, seq,
                     [_rows(d), _rows(d), _per_batch(8, d), _whole((1, d)), _whole((1, d)), _rows(d)],
                     [_rows(d), _rows(d), _per_batch(8, d)],
                     [jax.ShapeDtypeStruct((bsz, seq, d), BF16), jax.ShapeDtypeStruct((bsz, seq, d), F32),
                      jax.ShapeDtypeStruct((bsz, 8, d), F32)], accumulates=True)(h1, y2, mod, g, b, target)


def _silu_mul_bwd(da, h):
    bsz, seq, _ = h.shape

    def body(da_ref, hg_ref, hu_ref, dh_ref):
        hg, da_ = hg_ref[0].astype(F32), da_ref[0].astype(F32)
        sg = _sigmoid(hg)
        dh_ref[0, :, :D_FF] = (da_ * hu_ref[0].astype(F32) * (sg * (1.0 + hg * (1.0 - sg)))).astype(BF16)
        dh_ref[0, :, D_FF:] = (da_ * (hg * sg)).astype(BF16)

    return _row_call(body, "silu_mul_bwd", bsz, seq, [_rows(D_FF), _rows(D_FF, 0), _rows(D_FF, 1)], _rows(2 * D_FF),
                     jax.ShapeDtypeStruct((bsz, seq, 2 * D_FF), BF16))(da, h, h)


def _ln1_bwd(du2, dh1a, x, y1, mod, g, b):
    bsz, seq, d = x.shape

    def body(du_ref, dh_ref, x_ref, y_ref, mod_ref, g_ref, b_ref, dy_ref, dx_ref, acc_ref):
        y, du = y_ref[0], du_ref[0]
        gate = 1.0 + mod_ref[0, 2:3, :]
        z = DEEPNORM_ALPHA * x_ref[0] + gate * y
        xhat, rstd = _ln_stats(z)
        h1 = xhat * g_ref[...] + b_ref[...]
        dh1 = dh_ref[0] + du * (1.0 + mod_ref[0, 4:5, :])
        dz = _ln_bwd(dh1 * g_ref[...], xhat, rstd)
        dy_ref[0] = (gate * dz).astype(BF16)
        dx_ref[0] = DEEPNORM_ALPHA * dz
        _acc_rows(acc_ref, pl.program_id(1) == 0,
                  [_colsum(dh1 * xhat), _colsum(dh1), _colsum(dz * y), _colsum(du * h1), _colsum(du)])

    return _row_call(body, "ln1_bwd", bsz, seq,
                     [_rows(d), _rows(d), _rows(d), _rows(d), _per_batch(8, d), _whole((1, d)), _whole((1, d))],
                     [_rows(d), _rows(d), _per_batch(8, d)],
                     [jax.ShapeDtypeStruct((bsz, seq, d), BF16), jax.ShapeDtypeStruct((bsz, seq, d), F32),
                      jax.ShapeDtypeStruct((bsz, 8, d), F32)], accumulates=True)(du2, dh1a, x, y1, mod, g, b)


def _gate_merge_bwd(dm, gab, ya, yb):
    bsz, seq, d = ya.shape

    def body(dm_ref, ga_ref, gb_ref, ya_ref, yb_ref, dya_ref, dyb_ref, dg_ref):
        dm_ = dm_ref[0].astype(F32)
        sa, sb = _sigmoid(ga_ref[0].astype(F32)), _sigmoid(gb_ref[0].astype(F32))
        dya_ref[0] = (dm_ * sa).astype(BF16)
        dyb_ref[0] = (dm_ * sb).astype(BF16)
        dg_ref[0, :, :d] = (dm_ * ya_ref[0].astype(F32) * sa * (1.0 - sa)).astype(BF16)
        dg_ref[0, :, d:] = (dm_ * yb_ref[0].astype(F32) * sb * (1.0 - sb)).astype(BF16)

    return _row_call(body, "gate_merge_bwd", bsz, seq,
                     [_rows(d), _rows(d, 0), _rows(d, 1), _rows(d), _rows(d)],
                     [_rows(d), _rows(d), _rows(2 * d)],
                     [jax.ShapeDtypeStruct((bsz, seq, d), BF16), jax.ShapeDtypeStruct((bsz, seq, d), BF16),
                      jax.ShapeDtypeStruct((bsz, seq, 2 * d), BF16)])(dm, gab, gab, ya, yb)


def _grad_x(dxa, du1, x, mod):
    bsz, seq, d = x.shape

    def body(dxa_ref, du_ref, x_ref, mod_ref, gx_ref, acc_ref):
        du = du_ref[0]
        gx_ref[0] = dxa_ref[0] + du * (1.0 + mod_ref[0, 1:2, :])
        _acc_rows(acc_ref, pl.program_id(1) == 0, [_colsum(du * x_ref[0]), _colsum(du)])

    return _row_call(body, "grad_x", bsz, seq, [_rows(d), _rows(d), _rows(d), _per_batch(8, d)],
                     [_rows(d), _per_batch(8, d)],
                     [jax.ShapeDtypeStruct((bsz, seq, d), F32), jax.ShapeDtypeStruct((bsz, 8, d), F32)],
                     accumulates=True)(dxa, du1, x, mod)


def _segsum64(v):
    rows, width = v.shape
    ri = lax.broadcasted_iota(jnp.int32, (LANES, LANES), 0) // HEAD_DIM
    ci = lax.broadcasted_iota(jnp.int32, (LANES, LANES), 1) // HEAD_DIM
    ones = jnp.where(ri == ci, 1.0, 0.0).astype(BF16)
    out = []
    for c in range(width // LANES):
        part = v[:, c * LANES:(c + 1) * LANES]
        hi = part.astype(BF16)
        lo = (part - hi.astype(F32)).astype(BF16)
        out.append(jnp.dot(hi, ones, preferred_element_type=F32) + jnp.dot(lo, ones, preferred_element_type=F32))
    return jnp.concatenate(out, axis=1) if len(out) > 1 else out[0]


def _per_head(nheads):
    return pl.BlockSpec((1, nheads, ROW_TILE, LANES), lambda b, t: (b, 0, t, 0))


def _head_rowsum(v, half):
    v = jnp.where(_half_mask(v.shape, half), v, 0.0)
    ones = jnp.ones((LANES, LANES), BF16)
    hi = v.astype(BF16)
    lo = (v - hi.astype(F32)).astype(BF16)
    return jnp.dot(hi, ones, preferred_element_type=F32) + jnp.dot(lo, ones, preferred_element_type=F32)


def _group_weights(l_refs, head):
    ls = [l[0, head] for l in l_refs]
    mx = functools.reduce(jnp.maximum, ls)
    es = [jnp.exp(l - mx) for l in ls]
    den = functools.reduce(jnp.add, es)
    return [e / den for e in es]


def _merge_b(os_, ls_):
    bsz, seq, w = os_[0].shape
    nheads = w // HEAD_DIM

    def body(o0, o1, o2, l0, l1, l2, ob_ref):
        for c in range(w // LANES):
            sl = slice(c * LANES, (c + 1) * LANES)
            w_lo, w_hi = _group_weights((l0, l1, l2), 2 * c), _group_weights((l0, l1, l2), 2 * c + 1)
            acc = 0.0
            for g, o in enumerate((o0, o1, o2)):
                acc = acc + _pick_halves(w_lo[g], w_hi[g]) * o[0, :, sl]
            ob_ref[0, :, sl] = acc.astype(BF16)

    return _row_call(body, "merge_b", bsz, seq, [_rows(w)] * 3 + [_per_head(nheads)] * 3, _rows(w),
                     jax.ShapeDtypeStruct((bsz, seq, w), BF16))(*os_, *ls_)


def _merge_b_bwd(dob, os_, ls_):
    bsz, seq, w = os_[0].shape
    nheads = w // HEAD_DIM

    def body(dob_ref, o0, o1, o2, l0, l1, l2, do0, do1, do2, dd0, dd1, dd2):
        for c in range(w // LANES):
            sl = slice(c * LANES, (c + 1) * LANES)
            dob_ = dob_ref[0, :, sl].astype(F32)
            ws = []
            for half in (0, 1):
                head = 2 * c + half
                wh = _group_weights((l0, l1, l2), head)
                dws = [_head_rowsum(dob_ * o[0, :, sl], half) for o in (o0, o1, o2)]
                mean = wh[0] * dws[0] + wh[1] * dws[1] + wh[2] * dws[2]
                for g, dd_ref in enumerate((dd0, dd1, dd2)):
                    dd_ref[0, head] = -wh[g] * mean
                ws.append(wh)
            for g, do_ref in enumerate((do0, do1, do2)):
                do_ref[0, :, sl] = _pick_halves(ws[0][g], ws[1][g]) * dob_

    wide = jax.ShapeDtypeStruct((bsz, nheads, seq, LANES), F32)
    return _row_call(body, "merge_b_bwd", bsz, seq, [_rows(w)] * 4 + [_per_head(nheads)] * 3,
                     [_rows(w)] * 3 + [_per_head(nheads)] * 3,
                     [jax.ShapeDtypeStruct((bsz, seq, w), F32)] * 3 + [wide] * 3)(dob, *os_, *ls_)


def _delta_a(doa, oa, lse_a, sinks):
    bsz, seq, w = oa.shape
    nheads = w // HEAD_DIM

    def body(do_ref, o_ref, l_ref, s_ref, dd_ref, acc_ref):
        sums = []
        for head in range(nheads):
            c, half = divmod(head, 2)
            sl = slice(c * LANES, (c + 1) * LANES)
            dd = -_head_rowsum(do_ref[0, :, sl].astype(F32) * o_ref[0, :, sl], half)
            dd_ref[0, head] = dd
            sums.append(_colsum(dd * jnp.exp(s_ref[head] - l_ref[0, head])))
        _acc_rows(acc_ref, pl.program_id(1) == 0, sums)

    return _row_call(body, "delta_a", bsz, seq,
                     [_rows(w), _rows(w), _per_head(nheads), pl.BlockSpec(memory_space=pltpu.SMEM)],
                     [_per_head(nheads), _per_batch(nheads, LANES)],
                     [jax.ShapeDtypeStruct((bsz, nheads, seq, LANES), F32), jax.ShapeDtypeStruct((bsz, nheads, LANES), F32)],
                     accumulates=True)(doa, oa, lse_a, sinks)


def _swap_halves(v):
    src = lax.broadcasted_iota(jnp.int32, (LANES, LANES), 0)
    dst = lax.broadcasted_iota(jnp.int32, (LANES, LANES), 1)
    partner = jnp.where((dst % HEAD_DIM) < HEAD_DIM // 2, dst + HEAD_DIM // 2, dst - HEAD_DIM // 2)
    perm = jnp.where(src == partner, 1.0, 0.0).astype(BF16)
    hi = v.astype(BF16)
    lo = (v - hi.astype(F32)).astype(BF16)
    return jnp.dot(hi, perm, preferred_element_type=F32) + jnp.dot(lo, perm, preferred_element_type=F32)


def _swap_halves_roll(v):
    lane = lax.broadcasted_iota(jnp.int32, v.shape, 1)
    return jnp.where((lane % HEAD_DIM) < HEAD_DIM // 2, pltpu.roll(v, LANES - HEAD_DIM // 2, 1),
                     pltpu.roll(v, HEAD_DIM // 2, 1))


def _rope(v, cos, sin, sign=1.0, mxu=True):
    swap = _swap_halves if mxu else _swap_halves_roll
    out = []
    for c in range(v.shape[1] // LANES):
        part = v[:, c * LANES:(c + 1) * LANES]
        out.append(part * cos + sign * (swap(part) * sin))
    return jnp.concatenate(out, axis=1) if len(out) > 1 else out[0]


def _half_mask(shape, half):
    lane = lax.broadcasted_iota(jnp.int32, shape, len(shape) - 1) % LANES
    return (lane < HEAD_DIM) if half == 0 else (lane >= HEAD_DIM)


def _dup_half(v, half):
    return jnp.where(_half_mask(v.shape, half), v, pltpu.roll(v, HEAD_DIM, 1))


def _fold_halves(v):
    return v + pltpu.roll(v, HEAD_DIM, 1)


def _pick_halves(lo_rows, hi_rows):
    return jnp.where(_half_mask(lo_rows.shape, 0), lo_rows, hi_rows)


def _stack_masked(v, pairs):
    parts = []
    for c in pairs:
        pair = v[:, c * LANES:(c + 1) * LANES]
        parts += [jnp.where(_half_mask(pair.shape, half), pair, 0.0) for half in (0, 1)]
    return jnp.concatenate(parts, axis=0)


def _stack_pair_cols(v, pairs):
    return jnp.concatenate([v[:, c * LANES + half * HEAD_DIM:c * LANES + half * HEAD_DIM + 1] for c in pairs for half in (0, 1)],
                           axis=0)


ATTN_UNITS = 16


def _class_rows(r):
    return [pl.ds(0, QBLOCK)] if r == 1 else [pl.ds(rho, QBLOCK, stride=r) for rho in range(r)]


def _band_mask(nrows, nk, blk, n_back, has_prev):
    qi = lax.broadcasted_iota(jnp.int32, (nrows, nk), 0) % QBLOCK
    ki = lax.broadcasted_iota(jnp.int32, (nrows, nk), 1)
    if has_prev:
        dist = qi + QBLOCK - ki
        return (dist >= 0) & (dist <= n_back) & ((ki >= QBLOCK) | (blk > 0))
    dist = qi - ki
    return (dist >= 0) & (dist <= n_back)


def _attn_fwd(q_arr, k_arr, v_arr, *, name, npair, gqa, q_col, k_col, v_col, nchunk, r, n_back, sinks=None):
    bsz, seq, _ = q_arr.shape
    rr = QBLOCK * r
    nblk = seq // rr
    qw = npair * LANES
    kw = LANES if gqa else qw
    has_prev = nblk > 1
    has_sink = sinks is not None
    scale = HEAD_DIM ** -0.5

    def body(*refs):
        refs = list(refs)
        q_ref, kc_ref, vc_ref = refs[:3]
        pos = 3
        if has_prev:
            kp_ref, vp_ref = refs[pos:pos + 2]
            pos += 2
        if has_sink:
            sink_ref = refs[pos]
            pos += 1
        o_ref, lse_ref = refs[pos:pos + 2]
        blk = pl.program_id(2)
        nk = (2 if has_prev else 1) * QBLOCK
        valid = _band_mask(QBLOCK, nk, blk, n_back, has_prev)
        per = npair // 2
        classes = _class_rows(r)
        step = max(1, ATTN_UNITS // (2 * npair))
        for first in range(0, len(classes), step):
            batch = classes[first:first + step]
            units = []
            for ci, rows in enumerate(batch):
                q = q_ref[0, rows, :] * scale
                k, v = kc_ref[0, rows, :], vc_ref[0, rows, :]
                if has_prev:
                    k = jnp.concatenate([kp_ref[0, rows, :], k], axis=0)
                    v = jnp.concatenate([vp_ref[0, rows, :], v], axis=0)
                if gqa:
                    kdup = [_dup_half(k, hk).astype(BF16) for hk in range(2)]
                    vdup = [_dup_half(v, hk) for hk in range(2)]
                for c in range(npair):
                    sl = slice(c * LANES, (c + 1) * LANES)
                    qc = q[:, sl]
                    kc, vc = (kdup[c // per], vdup[c // per]) if gqa else (k[:, sl].astype(BF16), v[:, sl])
                    for half in (0, 1):
                        qm = jnp.where(_half_mask(qc.shape, half), qc, 0.0).astype(BF16)
                        vm = jnp.where(_half_mask(vc.shape, half), vc, 0.0).astype(BF16)
                        s = lax.dot_general(qm, kc, _DIMS["nt"], preferred_element_type=F32)
                        units.append(dict(ci=ci, c=c, half=half, s=s, vm=vm, sk=sink_ref[2 * c + half] if has_sink else None))
            for u in units:
                s = jnp.where(valid, u["s"], NEG_INF)
                m = jnp.max(s, axis=1, keepdims=True)
                if has_sink:
                    m = jnp.maximum(m, u["sk"])
                p = jnp.exp(s - m)
                den = jnp.sum(p, axis=1, keepdims=True)
                if has_sink:
                    den = den + jnp.exp(u["sk"] - m)
                u.update(p=p.astype(BF16), den=den, lse=m + jnp.log(den))
            for u in units:
                u["o"] = jnp.dot(u["p"], u["vm"], preferred_element_type=F32) / u["den"]
            for ci, rows in enumerate(batch):
                outs = [None] * npair
                for u in units:
                    if u["ci"] != ci:
                        continue
                    c, o = u["c"], u["o"]
                    outs[c] = o if u["half"] == 0 else outs[c] + o
                    lse_ref[0, 2 * c + u["half"], rows, :] = jnp.broadcast_to(u["lse"], (QBLOCK, LANES))
                o_ref[0, rows, :] = jnp.concatenate(outs, axis=1) if npair > 1 else outs[0]

    def cur(width, col0):
        return pl.BlockSpec((1, rr, width), lambda b, c, i: (b, i, col0 + c))

    def prev(width, col0):
        return pl.BlockSpec((1, rr, width), lambda b, c, i: (b, jnp.maximum(i - 1, 0), col0 + c))

    in_specs = [cur(qw, q_col), cur(kw, k_col), cur(kw, v_col)]
    args = [q_arr, k_arr, v_arr]
    if has_prev:
        in_specs += [prev(kw, k_col), prev(kw, v_col)]
        args += [k_arr, v_arr]
    if has_sink:
        in_specs.append(pl.BlockSpec(memory_space=pltpu.SMEM))
        args.append(sinks)
    return pl.pallas_call(
        body,
        name=name,
        grid=(bsz, nchunk, nblk),
        in_specs=in_specs,
        out_specs=[pl.BlockSpec((1, rr, qw), lambda b, c, i: (b, i, c)),
                   pl.BlockSpec((1, 2 * npair, rr, LANES), lambda b, c, i: (b, c, i, 0))],
        out_shape=[jax.ShapeDtypeStruct((bsz, seq, nchunk * qw), F32),
                   jax.ShapeDtypeStruct((bsz, nchunk * 2 * npair, seq, LANES), F32)],
        compiler_params=_params("parallel", "parallel", "parallel"),
    )(*args)


def _attn_bwd(q_arr, k_arr, v_arr, cos, sin, do, lse, dd, *, name, npair, gqa, q_col, k_col, v_col, nchunk, r, n_back,
              token=None):
    bsz, seq, _ = q_arr.shape
    rr = QBLOCK * r
    nblk = seq // rr
    qw = npair * LANES
    kw = LANES if gqa else qw
    has_next = nblk > 1
    has_token = token is not None
    scale = HEAD_DIM ** -0.5

    def body(*refs):
        refs = list(refs)
        k_ref, v_ref, c_ref, s_ref = refs[:4]
        tile_refs = [refs[4:8]]
        pos = 8
        if has_next:
            tile_refs.append(refs[pos:pos + 4])
            pos += 4
        if has_token:
            pos += 1
        dq_ref, dk_ref, dv_ref = refs[pos:pos + 3]
        carry_ref = refs[pos + 3]
        blk = pl.program_id(2)
        if has_next:
            @pl.when(blk == 0)
            def _():
                carry_ref[...] = jnp.zeros_like(carry_ref)

        nrows = (npair if gqa else 1) * QBLOCK
        qi = lax.broadcasted_iota(jnp.int32, (nrows, QBLOCK), 0) % QBLOCK
        ki = lax.broadcasted_iota(jnp.int32, (nrows, QBLOCK), 1)
        valids = [qi >= ki, (qi + QBLOCK - ki <= n_back) & (blk + 1 < nblk)]
        per = npair // 2
        ntile = len(tile_refs)
        cat = lambda parts: jnp.concatenate(parts, axis=1) if len(parts) > 1 else parts[0]
        classes = _class_rows(r)
        step = max(1, ATTN_UNITS // (ntile * (2 if gqa else 2 * npair)))
        for first in range(0, len(classes), step):
            batch = classes[first:first + step]
            units = []
            for ci, rows in enumerate(batch):
                tiles = [(q_ref[0, rows, :] * scale, do_ref[0, rows, :], l_ref, d_ref) for q_ref, do_ref, l_ref, d_ref in tile_refs]
                k, v = k_ref[0, rows, :], v_ref[0, rows, :]
                if gqa:
                    for hk in range(2):
                        pairs = list(range(hk * per, (hk + 1) * per))
                        heads = [2 * c + half for c in pairs for half in (0, 1)]
                        kd, vd = _dup_half(k, hk).astype(BF16), _dup_half(v, hk).astype(BF16)
                        for t, (q, do_, l_ref, d_ref) in enumerate(tiles):
                            units.append(dict(ci=ci, t=t, hk=hk, pairs=pairs, qs=_stack_masked(q, pairs).astype(BF16),
                                              dos=_stack_masked(do_, pairs).astype(BF16),
                                              lcol=jnp.concatenate([l_ref[0, h, rows, :] for h in heads], axis=0),
                                              dcol=jnp.concatenate([d_ref[0, h, rows, :] for h in heads], axis=0),
                                              kmat=kd, vmat=vd, kdq=kd))
                else:
                    for c in range(npair):
                        sl = slice(c * LANES, (c + 1) * LANES)
                        kc, vcb = k[:, sl], v[:, sl].astype(BF16)
                        kcb = kc.astype(BF16)
                        for t, (q, do_, l_ref, d_ref) in enumerate(tiles):
                            for half in (0, 1):
                                hm = _half_mask(kc.shape, half)
                                units.append(dict(ci=ci, t=t, c=c, half=half, qs=jnp.where(hm, q[:, sl], 0.0).astype(BF16),
                                                  dos=jnp.where(hm, do_[:, sl], 0.0).astype(BF16),
                                                  lcol=l_ref[0, 2 * c + half, rows, :], dcol=d_ref[0, 2 * c + half, rows, :],
                                                  kmat=kcb, vmat=vcb, kdq=jnp.where(hm, kc, 0.0).astype(BF16)))
            for u in units:
                u["s"] = lax.dot_general(u["qs"], u["kmat"], _DIMS["nt"], preferred_element_type=F32)
                u["dp"] = lax.dot_general(u["dos"], u["vmat"], _DIMS["nt"], preferred_element_type=F32)
            for u in units:
                p = jnp.exp(jnp.where(valids[u["t"]], u["s"], NEG_INF) - u["lcol"])
                u["ds"] = (p * (u["dp"] + u["dcol"])).astype(BF16)
                u["p"] = p.astype(BF16)
            for u in units:
                u["dv"] = lax.dot_general(u["p"], u["dos"], _DIMS["tn"], preferred_element_type=F32)
                u["dk"] = lax.dot_general(u["ds"], u["qs"], _DIMS["tn"], preferred_element_type=F32)
                u["dq"] = jnp.dot(u["ds"], u["kdq"], preferred_element_type=F32) * scale
            for ci, rows in enumerate(batch):
                mine = [u for u in units if u["ci"] == ci]
                dq = [[None] * npair for _ in range(ntile)]
                if gqa:
                    dk_out = dv_out = None
                    for hk in range(2):
                        us = [u for u in mine if u["hk"] == hk]
                        for u in us:
                            for i, c in enumerate(u["pairs"]):
                                dq[u["t"]][c] = _pick_halves(u["dq"][2 * i * QBLOCK:(2 * i + 1) * QBLOCK],
                                                             u["dq"][(2 * i + 1) * QBLOCK:(2 * i + 2) * QBLOCK])
                        dk_h = _fold_halves(functools.reduce(jnp.add, [u["dk"] for u in us]))
                        dv_h = _fold_halves(functools.reduce(jnp.add, [u["dv"] for u in us]))
                        dk_out = dk_h if hk == 0 else _pick_halves(dk_out, dk_h)
                        dv_out = dv_h if hk == 0 else _pick_halves(dv_out, dv_h)
                else:
                    dks, dvs = [], []
                    for c in range(npair):
                        us = [u for u in mine if u["c"] == c]
                        dks.append(functools.reduce(jnp.add, [u["dk"] for u in us]))
                        dvs.append(functools.reduce(jnp.add, [u["dv"] for u in us]))
                        for t in range(ntile):
                            dq[t][c] = functools.reduce(jnp.add, [u["dq"] for u in us if u["t"] == t])
                    dk_out, dv_out = cat(dks), cat(dvs)
                ck, sk_ = c_ref[0, rows, :], s_ref[0, rows, :]
                dk_ref[0, rows, :] = _rope(dk_out, ck, sk_, sign=-1.0, mxu=gqa)
                dv_ref[0, rows, :] = dv_out
                dq_cur = cat(dq[0])
                if has_next:
                    dq_cur = dq_cur + carry_ref[rows, :]
                    carry_ref[rows, :] = cat(dq[1])
                dq_ref[0, rows, :] = _rope(dq_cur, ck, sk_, sign=-1.0, mxu=gqa)

    def at(width, col0, shift):
        return pl.BlockSpec((1, rr, width), lambda b, c, i: (b, jnp.minimum(i + shift, nblk - 1), col0 + c))

    in_specs = [at(kw, k_col, 0), at(kw, v_col, 0), pl.BlockSpec((1, rr, LANES), lambda b, c, i: (b, i, 0)),
                pl.BlockSpec((1, rr, LANES), lambda b, c, i: (b, i, 0))]
    args = [k_arr, v_arr, cos, sin]
    def per_head(shift):
        return pl.BlockSpec((1, 2 * npair, rr, LANES), lambda b, c, i: (b, c, jnp.minimum(i + shift, nblk - 1), 0))

    for shift in (0, 1) if has_next else (0,):
        in_specs += [at(qw, q_col, shift), at(qw, 0, shift), per_head(shift), per_head(shift)]
        args += [q_arr, do, lse, dd]
    if has_token:
        in_specs.append(pl.BlockSpec(token.shape, lambda b, c, i: (0, 0)))
        args.append(token)
    return pl.pallas_call(
        body,
        name=name,
        grid=(bsz, nchunk, nblk),
        in_specs=in_specs,
        out_specs=[pl.BlockSpec((1, rr, qw), lambda b, c, i: (b, i, c)),
                   pl.BlockSpec((1, rr, kw), lambda b, c, i: (b, i, c)),
                   pl.BlockSpec((1, rr, kw), lambda b, c, i: (b, i, c))],
        out_shape=[jax.ShapeDtypeStruct((bsz, seq, nchunk * qw), F32),
                   jax.ShapeDtypeStruct((bsz, seq, nchunk * kw), F32),
                   jax.ShapeDtypeStruct((bsz, seq, nchunk * kw), F32)],
        scratch_shapes=[pltpu.VMEM((rr, qw) if has_next else (8, LANES), F32)],
        compiler_params=_params("parallel", "parallel", "arbitrary"),
    )(*args)


B_CHUNKS = {1: (4, 1), 4: (1, 4), 16: (1, 4)}


def _rope_tables(positions):
    half = HEAD_DIM // 2
    inv = ROPE_THETA ** (-jnp.arange(half, dtype=F32) / half)
    ang = positions.astype(F32)[..., None] * inv
    cos, sin = jnp.cos(ang), jnp.sin(ang)
    return jnp.concatenate([cos] * 4, axis=-1), jnp.concatenate([-sin, sin, -sin, sin], axis=-1)


def _layer_step(x, mod, positions, sinks, ln1_g, ln1_b, ln2_g, ln2_b, target, get_w_in, get_rest, hook):
    bsz, seq, d = x.shape
    ntok = bsz * seq
    flat = lambda v: v.reshape(ntok, v.shape[-1])
    unflat = lambda v: v.reshape(bsz, seq, v.shape[-1])
    cos, sin = _rope_tables(positions)
    mm = functools.partial(_matmul, tm=1024, tk=1024)
    scalar = lambda tok: 0.0 if tok is None else tok[0, 0]

    u1 = _modulate_in(x, mod)
    u1f = flat(u1)
    wint = get_w_in(u1)
    cosf, sinf = flat(cos), flat(sin)
    proj = functools.partial(_proj_rope, u1f, wint, cosf, sinf, tm=2048)
    qa = unflat(proj(n=1024, b_off=OFF_QA, rope_cols=1024, tn=512, name="proj_qa"))
    kva = unflat(proj(n=256, b_off=OFF_KVA, rope_cols=128, tn=128, name="proj_kva"))
    qkvb = unflat(proj(n=4608, b_off=OFF_QKVB, rope_cols=3072, tn=256, name="proj_qkvb"))
    gab = unflat(proj(n=2048, b_off=OFF_GAB, rope_cols=0, tn=256, name="proj_gab", out_dtype=BF16))

    sink_vec = sinks.reshape(A_Q_HEADS) + scalar(hook("projected", gab))
    a_kw = dict(npair=A_Q_HEADS // 2, gqa=True, q_col=0, k_col=0, v_col=1, nchunk=1, r=1, n_back=A_WINDOW - 1)
    oa, lse_a = _attn_fwd(qa, kva, kva, name="attn_a_fwd", sinks=sink_vec, **a_kw)
    rest = get_rest(oa)
    wba, wbbt, wo, wgut, wd = (rest[n] for n in ("w_branch_a", "w_branch_b", "w_o", "w_gate_up", "w_down"))
    ya = unflat(mm(flat(oa), wba, mode="nn", out_dtype=BF16, tn=512, name="branch_a"))

    b_kws, os_, ls_ = [], [], []
    for g, (window, r) in enumerate(B_PATTERNS):
        npair, nch = B_CHUNKS[r]
        per = B_HEADS_PER_GROUP // (2 * npair)
        nsec = len(B_PATTERNS) * per
        kw_ = dict(npair=npair, gqa=False, q_col=g * per, k_col=nsec + g * per, v_col=2 * nsec + g * per, nchunk=nch, r=r,
                   n_back=window // r)
        b_kws.append(kw_)
        o_g, l_g = _attn_fwd(qkvb, qkvb, qkvb, name=f"attn_b{g}_fwd", **kw_)
        os_.append(o_g)
        ls_.append(l_g)
    ob = _merge_b(os_, ls_)
    yb = unflat(mm(flat(ob), wbbt, mode="nt", out_dtype=BF16, tk=512, tn=512, name="branch_b"))
    merged = _gate_merge(gab, ya, yb)
    y1 = unflat(mm(flat(merged), wo, mode="nn", out_dtype=F32, tn=512, name="w_o"))
    h1, u2 = _ln1_fwd(x, y1, mod, ln1_g, ln1_b)
    h = unflat(mm(flat(u2), wgut, mode="nt", out_dtype=BF16, tn=D_FF // 2, name="gate_up"))
    a = _silu_mul(h)
    y2 = unflat(mm(flat(a), wd, mode="nn", out_dtype=F32, tk=D_FF, tn=512, name="down"))

    dy2, dh1a, acc2 = _ln2_loss_bwd(h1, y2, mod, ln2_g, ln2_b, target)
    dy2f = flat(dy2)
    da = unflat(mm(dy2f, wd, mode="nt", out_dtype=BF16, tn=D_FF // 2, name="down_dgrad"))
    g_wd = _matmul(flat(a), dy2f, mode="tn", out_dtype=BF16, tm=256, tn=1024, tk=ntok, name="down_wgrad")
    dh = _silu_mul_bwd(da, h)
    dhf = flat(dh)
    du2 = unflat(mm(dhf, wgut, mode="nn", out_dtype=F32, tk=D_FF, tn=512, name="gate_up_dgrad"))
    g_wgut = _matmul(dhf, flat(u2), mode="tn", out_dtype=BF16, tm=256, tn=1024, tk=ntok, name="gate_up_wgrad")
    dy1, dxa, acc1 = _ln1_bwd(du2, dh1a, x, y1, mod, ln1_g, ln1_b)
    dy1f = flat(dy1)
    dmerged = unflat(mm(dy1f, wo, mode="nt", out_dtype=BF16, tn=512, name="w_o_dgrad"))
    g_wo = _matmul(flat(merged), dy1f, mode="tn", out_dtype=BF16, tm=256, tn=1024, tk=ntok, name="w_o_wgrad")
    dya, dyb, dgab = _gate_merge_bwd(dmerged, gab, ya, yb)
    dyaf, dybf = flat(dya), flat(dyb)
    doa = unflat(mm(dyaf, wba, mode="nt", out_dtype=F32, tn=512, name="branch_a_dgrad"))
    g_wba = _matmul(flat(oa), dyaf, mode="tn", out_dtype=BF16, tm=256, tn=1024, tk=ntok, name="branch_a_wgrad")
    dob = unflat(mm(dybf, wbbt, mode="nn", out_dtype=BF16, tn=512, name="branch_b_dgrad"))
    g_wbbt = _matmul(dybf, flat(ob), mode="tn", out_dtype=BF16, tm=256, tn=512, tk=ntok, name="branch_b_wgrad")
    tok = hook("grads_rest", dict(w_branch_a=g_wba, w_branch_b=g_wbbt, w_o=g_wo, w_gate_up=g_wgut, w_down=g_wd))

    dd_a, acc_s = _delta_a(doa, oa, lse_a, sinks.reshape(A_Q_HEADS) + scalar(tok))
    tok = hook("delta_done", dd_a)
    dqa, dka, dva = _attn_bwd(qa, kva, kva, cos, sin, doa, lse_a, dd_a, name="attn_a_bwd", token=tok, **a_kw)
    merged_bwd = _merge_b_bwd(dob, os_, ls_)
    dqs, dks, dvs = [], [], []
    for g in range(len(B_PATTERNS)):
        dq_g, dk_g, dv_g = _attn_bwd(qkvb, qkvb, qkvb, cos, sin, merged_bwd[g], ls_[g], merged_bwd[3 + g],
                                     name=f"attn_b{g}_bwd", **b_kws[g])
        dqs.append(dq_g)
        dks.append(dk_g)
        dvs.append(dv_g)
    dproj = jnp.concatenate([t.astype(BF16) for t in [dqa, dka, dva] + dqs + dks + dvs] + [dgab], axis=-1)
    dprojf = flat(dproj)
    g_wint = _matmul(dprojf, u1f, mode="tn", out_dtype=BF16, tm=256, tn=1024, tk=ntok, name="w_in_wgrad")
    tok = hook("grads_w_in", dict(w_in=g_wint))
    du1 = unflat(_matmul(dprojf, wint, mode="nn", out_dtype=F32, tm=1024, tn=512, tk=wint.shape[0] // 2, name="w_in_dgrad",
                         token=tok))
    tok = hook("dgrad_done", du1)
    grad_x, acc0 = _grad_x(dxa, du1, x, mod + scalar(tok))

    loss_part = jnp.sum(acc2[:, 3, 0])
    dmod = jnp.stack([acc0[:, 1], acc0[:, 0], acc1[:, 2], acc1[:, 4], acc1[:, 3], acc2[:, 2]], axis=1)
    sink_row = jnp.zeros((d,), F32).at[::HEAD_DIM].set(acc_s[:, :, 0].sum(0))
    small = jnp.stack([acc1[:, 0].sum(0), acc1[:, 1].sum(0), acc2[:, 0].sum(0), acc2[:, 1].sum(0), sink_row])
    return loss_part, grad_x, dmod, small


CHIP_FLIPS = (2, 4, 6)


def _my_place():
    return lax.axis_index("x"), lax.axis_index("y"), lax.axis_index("c")


def _flip(place, k):
    px, py, pc = place
    return (1 - px if k & 4 else px, 1 - py if k & 2 else py, 1 - pc if k & 1 else pc)


def _index(place):
    return 4 * place[0] + 2 * place[1] + place[2]


def _gather_small(v, name):
    rows, cols = v.shape

    def body(v_ref, out_ref, send_sems, recv_sems):
        me = _my_place()
        out_ref[_index(me)] = v_ref[...]
        copies = []
        for k in range(1, N_DEV):
            copies.append(pltpu.make_async_remote_copy(
                src_ref=v_ref, dst_ref=out_ref.at[_index(me)], send_sem=send_sems.at[k - 1], recv_sem=recv_sems.at[k - 1],
                device_id=_flip(me, k), device_id_type=MESH))
        for cp in copies:
            cp.start()
        for k in range(1, N_DEV):
            pltpu.make_async_remote_copy(
                src_ref=v_ref, dst_ref=out_ref.at[_index(_flip(me, k))], send_sem=send_sems.at[k - 1],
                recv_sem=recv_sems.at[k - 1], device_id=_flip(me, k), device_id_type=MESH).wait_recv()
        for cp in copies:
            cp.wait_send()

    return pl.pallas_call(
        body,
        name=name,
        out_shape=jax.ShapeDtypeStruct((N_DEV, rows, cols), v.dtype),
        in_specs=[pl.BlockSpec(memory_space=pltpu.VMEM)],
        out_specs=pl.BlockSpec(memory_space=pltpu.VMEM),
        scratch_shapes=[pltpu.SemaphoreType.DMA((N_DEV - 1,)), pltpu.SemaphoreType.DMA((N_DEV - 1,))],
        compiler_params=pltpu.CompilerParams(vmem_limit_bytes=VMEM_LIMIT_BYTES),
    )(v)


_HBM = pl.BlockSpec(memory_space=pltpu.HBM)
_SEM = pl.BlockSpec(memory_space=pltpu.SEMAPHORE)
_EFFECT = pltpu.SideEffectType.DATAFLOW_SIDE_EFFECTING


def _remote(src, dst, send_sems, recv_sems, j, to):
    return pltpu.make_async_remote_copy(src_ref=src, dst_ref=dst, send_sem=send_sems.at[j], recv_sem=recv_sems.at[j],
                                        device_id=to, device_id_type=MESH)


def _copies_start(name, src, land_shape, make_copies, nsem):
    def body(src_ref, land_ref, send_sems, recv_sems, src_thru, land_thru, token):
        for cp in make_copies(src_ref, land_ref, send_sems, recv_sems):
            cp.start()
        token[...] = jnp.zeros_like(token)

    sems = pltpu.SemaphoreType.DMA((nsem,))
    return pl.pallas_call(
        body, name=name,
        out_shape=(sems, sems, pltpu.HBM(src.shape, src.dtype), pltpu.HBM(land_shape, src.dtype),
                   jax.ShapeDtypeStruct((8, LANES), F32)),
        in_specs=(_HBM, _HBM), out_specs=(_SEM, _SEM, _HBM, _HBM, pl.BlockSpec(memory_space=pltpu.VMEM)),
        input_output_aliases={0: 2, 1: 3},
        compiler_params=pltpu.CompilerParams(has_side_effects=_EFFECT),
    )(pltpu.with_memory_space_constraint(src, pltpu.HBM),
      pltpu.with_memory_space_constraint(lax.empty(land_shape, src.dtype), pltpu.HBM))


def _copies_wait(name, started, make_copies, after):
    send_sems, recv_sems, src_thru, land_thru, _ = started

    def body(src_ref, land_ref, send_sems, recv_sems, after_ref, src_dead, got_ref):
        for cp in make_copies(src_ref, land_ref, send_sems, recv_sems):
            cp.wait_send()
            cp.wait_recv()

    return pl.pallas_call(
        body, name=name,
        out_shape=(pltpu.HBM(src_thru.shape, src_thru.dtype), pltpu.HBM(land_thru.shape, land_thru.dtype)),
        in_specs=(_HBM, _HBM, _SEM, _SEM, pl.BlockSpec(memory_space=pl.ANY)), out_specs=(_HBM, _HBM),
        input_output_aliases={0: 0, 1: 1},
        compiler_params=pltpu.CompilerParams(has_side_effects=_EFFECT),
    )(src_thru, land_thru, send_sems, recv_sems, after)


def _gather1_copies(src_ref, land_ref, send_sems, recv_sems):
    me = _my_place()
    return [_remote(src_ref, land_ref.at[_index(me)], send_sems, recv_sems, j, _flip(me, k)) for j, k in enumerate((1,) + CHIP_FLIPS)]


def _gather2_copies(src_ref, land_ref, send_sems, recv_sems):
    me = _my_place()
    return [_remote(src_ref.at[_index(_flip(me, k))], land_ref.at[j], send_sems, recv_sems, j, _flip(me, 1))
            for j, k in enumerate(CHIP_FLIPS)]


def _to_sibling_copies(src_ref, land_ref, send_sems, recv_sems):
    me = _my_place()
    return [_remote(src_ref.at[1 - me[2]], land_ref, send_sems, recv_sems, 0, _flip(me, 1))]


def _to_chips_copies(src_ref, land_ref, send_sems, recv_sems):
    me = _my_place()
    copies = []
    for j, k in enumerate(CHIP_FLIPS):
        to = _flip(me, k)
        copies.append(_remote(src_ref.at[2 * to[0] + to[1]], land_ref.at[j], send_sems, recv_sems, j, to))
    return copies


class _Gather:
    def __init__(self, name, packed):
        self.name, self.packed = name, packed
        self.rows = packed.shape[0]
        self.first = _copies_start(name + "_start", packed, (N_DEV, self.rows, D_MODEL), _gather1_copies, 4)
        self.token = self.first[4]

    def pass_on(self, after):
        _, land = _copies_wait(self.name + "_wait", self.first, _gather1_copies, after)
        self.second = _copies_start(self.name + "_pass_start", land, (3, self.rows, D_MODEL), _gather2_copies, 3)
        return self.second[4]

    def finish(self, after):
        full, passed = _copies_wait(self.name + "_pass_wait", self.second, _gather2_copies, after)
        me = _my_place()
        full = lax.dynamic_update_slice(full, self.packed[None], (_index(me), 0, 0))
        for j, k in enumerate(CHIP_FLIPS):
            full = lax.dynamic_update_slice(full, passed[j][None], (_index(_flip(me, k | 1)), 0, 0))
        return full


SUM_SPLIT = 2


def _sum_pairs(mine, theirs):
    nchip, rows, cols = mine.shape
    tile = rows // SUM_SPLIT
    spec = pl.BlockSpec((1, tile, cols), lambda q, t: (q, t, 0))

    def body(a_ref, b_ref, o_ref):
        o_ref[...] = (a_ref[...].astype(F32) + b_ref[...].astype(F32)).astype(BF16)

    return pl.pallas_call(body, name="grad_sum_sibling", grid=(nchip, SUM_SPLIT), in_specs=[spec, spec], out_specs=spec,
                          out_shape=jax.ShapeDtypeStruct(mine.shape, BF16), compiler_params=_params("parallel", "parallel"))(mine, theirs)


def _sum_final(own, got):
    rows, cols = own.shape
    tile = rows // SUM_SPLIT

    def body(a_ref, g_ref, o_ref):
        o_ref[...] = ((a_ref[...].astype(F32) + g_ref[0].astype(F32)) + g_ref[1].astype(F32)) + g_ref[2].astype(F32)

    return pl.pallas_call(
        body, name="grad_sum_chips", grid=(SUM_SPLIT,),
        in_specs=[pl.BlockSpec((tile, cols), lambda t: (t, 0)), pl.BlockSpec((3, tile, cols), lambda t: (0, t, 0))],
        out_specs=pl.BlockSpec((tile, cols), lambda t: (t, 0)),
        out_shape=jax.ShapeDtypeStruct((rows, cols), F32), compiler_params=_params("parallel"))(own, got)


class _ReduceScatter:
    def __init__(self, name, slabs):
        self.name = name
        rows = slabs.shape[1]
        self.rows = rows
        self.parts = slabs.reshape(4, 2, rows, D_MODEL).transpose(1, 0, 2, 3)
        self.first = _copies_start(name + "_sibling_start", self.parts, (4, rows, D_MODEL), _to_sibling_copies, 1)
        self.token = self.first[4]

    def between_chips(self, after):
        parts, theirs = _copies_wait(self.name + "_sibling_wait", self.first, _to_sibling_copies, after)
        mine = lax.dynamic_index_in_dim(parts, lax.axis_index("c"), 0, keepdims=False)
        chip_sum = _sum_pairs(mine, theirs)
        self.second = _copies_start(self.name + "_chips_start", chip_sum, (3, self.rows, D_MODEL), _to_chips_copies, 3)
        return self.second[4]

    def finish(self, after):
        chip_sum, got = _copies_wait(self.name + "_chips_wait", self.second, _to_chips_copies, after)
        my_chip = 2 * lax.axis_index("x") + lax.axis_index("y")
        return _sum_final(lax.dynamic_index_in_dim(chip_sum, my_chip, 0, keepdims=False), got)


def _ada_fwd(c_all, w, b):
    nb, _ = c_all.shape
    ncol = w.shape[1]

    def body(c_ref, w_ref, b_ref, o_ref):
        c = c_ref[...]
        act = (c * _sigmoid(c)).astype(BF16)
        o_ref[...] = jnp.dot(act, w_ref[...].astype(BF16), preferred_element_type=F32) + b_ref[...]

    return pl.pallas_call(body, name="ada_fwd", out_shape=jax.ShapeDtypeStruct((nb, ncol), F32),
                          compiler_params=pltpu.CompilerParams(vmem_limit_bytes=VMEM_LIMIT_BYTES))(c_all, w, b)


def _ada_wgrad(c_all_t, dmod_cols):
    d, nb = c_all_t.shape
    ncol = dmod_cols.shape[1]

    def body(ct_ref, dm_ref, o_ref):
        ct = ct_ref[...]
        act = (ct * _sigmoid(ct)).astype(BF16).astype(F32)
        dm = dm_ref[...].astype(BF16).astype(F32)
        acc = act[:, 0:1] * dm[0:1, :]
        for i in range(1, nb):
            acc = acc + act[:, i:i + 1] * dm[i:i + 1, :]
        o_ref[...] = acc

    return pl.pallas_call(body, name="ada_wgrad", out_shape=jax.ShapeDtypeStruct((d, ncol), F32),
                          compiler_params=pltpu.CompilerParams(vmem_limit_bytes=VMEM_LIMIT_BYTES))(c_all_t, dmod_cols)


SMALL_ROWS = 24


def _reduce_small(gathered):
    def body(g_ref, o_ref):
        acc = g_ref[0]
        for dev in range(1, N_DEV):
            acc = acc + g_ref[dev]
        o_ref[...] = acc

    return pl.pallas_call(body, name="reduce_small", out_shape=jax.ShapeDtypeStruct(gathered.shape[1:], F32))(gathered)


def _adamw(w, g, m, v, name):
    rows, cols = w.shape
    tile = rows
    for cand in (256, 128, 64, 32, 16, 8):
        if rows % cand == 0 and rows > cand:
            tile = cand
            break
    spec = pl.BlockSpec((tile, cols), lambda t: (t, 0))
    bc1 = 1.0 - ADAM_B1 ** ADAM_STEP
    bc2 = 1.0 - ADAM_B2 ** ADAM_STEP

    def body(w_ref, g_ref, m_ref, v_ref, d_ref, nm_ref, nv_ref):
        g_ = g_ref[...]
        nm = ADAM_B1 * m_ref[...] + (1.0 - ADAM_B1) * g_
        nv = ADAM_B2 * v_ref[...] + (1.0 - ADAM_B2) * (g_ * g_)
        d_ref[...] = -ADAM_LR * ((nm / bc1) / (jnp.sqrt(nv / bc2) + ADAM_EPS) + ADAM_WD * w_ref[...])
        nm_ref[...] = nm
        nv_ref[...] = nv

    shp = jax.ShapeDtypeStruct((rows, cols), F32)
    return pl.pallas_call(body, name=name, grid=(rows // tile,), in_specs=[spec] * 4, out_specs=[spec] * 3, out_shape=[shp] * 3,
                          compiler_params=_params("parallel"))(w, g, m, v)


_WEIGHTS = ("w_ada", "b_ada", "w_in", "sinks", "w_branch_a", "w_branch_b", "w_o", "ln1_g", "ln1_b", "w_gate_up", "w_down",
            "ln2_g", "ln2_b")
_TRANSPOSED = ("w_in", "w_branch_b", "w_gate_up")


def _pack_shard(name, w):
    w = w.astype(BF16)
    if name in _TRANSPOSED:
        w = w.T
    return w.reshape(-1, D_MODEL)


def _unpack_full(name, slab):
    if name == "w_branch_b":
        return slab.reshape(N_DEV * 128, 512)
    return slab.reshape(-1, D_MODEL)


def _unpack_group(group, gathered):
    full, off = {}, 0
    for n, r in group:
        full[n] = _unpack_full(n, gathered[:, off:off + r])
        off += r
    return full


def _unpack_grads(group, g_packed):
    g_w, off = {}, 0
    for n, r in group:
        part = g_packed[off:off + r]
        off += r
        if n == "w_branch_b":
            part = part.reshape(128, 512)
        g_w[n] = part.T if n in _TRANSPOSED else part
    return g_w


def kernel(x, c, positions, w_ada, b_ada, w_in, sinks, w_branch_a, w_branch_b, w_o, ln1_g, ln1_b, w_gate_up, w_down, ln2_g, ln2_b, loss_target, m_w_ada, m_b_ada, m_w_in, m_sinks, m_w_branch_a, m_w_branch_b, m_w_o, m_ln1_g, m_ln1_b, m_w_gate_up, m_w_down, m_ln2_g, m_ln2_b, v_w_ada, v_b_ada, v_w_in, v_sinks, v_w_branch_a, v_w_branch_b, v_w_o, v_ln1_g, v_ln1_b, v_w_gate_up, v_w_down, v_ln2_g, v_ln2_b):
    weights = dict(w_ada=w_ada, b_ada=b_ada, w_in=w_in, sinks=sinks, w_branch_a=w_branch_a, w_branch_b=w_branch_b, w_o=w_o,
                   ln1_g=ln1_g, ln1_b=ln1_b, w_gate_up=w_gate_up, w_down=w_down, ln2_g=ln2_g, ln2_b=ln2_b)
    m_in = dict(w_ada=m_w_ada, b_ada=m_b_ada, w_in=m_w_in, sinks=m_sinks, w_branch_a=m_w_branch_a, w_branch_b=m_w_branch_b,
                w_o=m_w_o, ln1_g=m_ln1_g, ln1_b=m_ln1_b, w_gate_up=m_w_gate_up, w_down=m_w_down, ln2_g=m_ln2_g, ln2_b=m_ln2_b)
    v_in = dict(w_ada=v_w_ada, b_ada=v_b_ada, w_in=v_w_in, sinks=v_sinks, w_branch_a=v_w_branch_a, w_branch_b=v_w_branch_b,
                w_o=v_w_o, ln1_g=v_ln1_g, ln1_b=v_ln1_b, w_gate_up=v_w_gate_up, w_down=v_w_down, ln2_g=v_ln2_g, ln2_b=v_ln2_b)
    bsz = x.shape[0]
    me = _index(_my_place())
    ada_cols = w_ada.shape[2]
    outs = {}

    def adamw(n, g):
        w2, m2, v2 = (t[n][0] if t[n].ndim == 3 else t[n] for t in (weights, m_in, v_in))
        shape = weights[n].shape
        dlt, nm, nv = _adamw(w2, g, m2, v2, "adamw_" + n)
        outs[n] = tuple(t.reshape(shape) for t in (g, dlt, nm, nv))
        return nv

    packed_in = jnp.concatenate([_pack_shard(n, weights[n][0]) for n, _ in GROUP_IN], axis=0)
    packed_rest = jnp.concatenate([_pack_shard(n, weights[n][0]) for n, _ in GROUP_REST], axis=0)
    c_all = _gather_small(jnp.pad(c, ((0, 8 - bsz), (0, 0))), "gather_c")[:, :bsz].reshape(N_DEV * bsz, D_MODEL)
    gather_in = _Gather("gather_w_in", lax.optimization_barrier((packed_in, c_all))[0])
    b_cols = lax.dynamic_slice_in_dim(b_ada, me * ada_cols, ada_cols, axis=1)
    mod_cols = _ada_fwd(c_all, w_ada[0], b_cols + gather_in.token[0, 0])
    mod_all = _gather_small(mod_cols, "gather_mod").transpose(1, 0, 2).reshape(N_DEV * bsz, 6, D_MODEL)
    gather_rest = _Gather("gather_rest", lax.optimization_barrier((packed_rest, mod_all))[0])
    mod = jnp.pad(lax.dynamic_slice_in_dim(mod_all, me * bsz, bsz, axis=0), ((0, 0), (0, 2), (0, 0)))
    mod = mod + gather_rest.token[0, 0]
    mod = mod + gather_in.pass_on(mod)[0, 0]

    scatters = {}

    def get_w_in(after):
        return _unpack_group(GROUP_IN, gather_in.finish(after))["w_in"]

    def get_rest(after):
        return _unpack_group(GROUP_REST, gather_rest.finish(after))

    def pack_grads(group, grads):
        return jnp.concatenate([grads[n].reshape(N_DEV, r, D_MODEL) for n, r in group], axis=1)

    def hook(point, value):
        if point == "projected":
            return gather_rest.pass_on(value)
        if point == "grads_rest":
            scatters["rest"] = _ReduceScatter("scatter_rest", pack_grads(GROUP_REST, value))
            return scatters["rest"].token
        if point == "delta_done":
            return scatters["rest"].between_chips(value)
        if point == "grads_w_in":
            scatters["in"] = _ReduceScatter("scatter_w_in", pack_grads(GROUP_IN, value))
            return scatters["in"].token
        if point == "dgrad_done":
            tok = scatters["in"].between_chips(value)
            for n, g in _unpack_grads(GROUP_REST, scatters["rest"].finish(tok)).items():
                adamw(n, g)
            return tok
        raise ValueError(point)

    loss_part, grad_x, dmod, small = _layer_step(x, mod, positions, sinks[0], ln1_g, ln1_b, ln2_g, ln2_b, loss_target,
                                                 get_w_in, get_rest, hook)
    loss = lax.psum(loss_part, MESH_AXES)

    rows = jnp.concatenate([dmod.reshape(bsz * 6, D_MODEL), small, jnp.zeros((SMALL_ROWS - bsz * 6 - 5, D_MODEL), F32)], axis=0)
    small_all = _gather_small(rows, "gather_small")
    sums = _reduce_small(small_all)
    dmod_all = small_all[:, :bsz * 6].reshape(N_DEV * bsz, 6 * D_MODEL)
    adamw("b_ada", functools.reduce(jnp.add, [sums[6 * i:6 * i + 6] for i in range(bsz)]).reshape(1, 6 * D_MODEL))
    for i, n in enumerate(("ln1_g", "ln1_b", "ln2_g", "ln2_b")):
        adamw(n, sums[12 + i][None])
    adamw("sinks", sums[16][::HEAD_DIM][None])
    dmod_cols = lax.dynamic_slice_in_dim(dmod_all, me * ada_cols, ada_cols, axis=1)
    last = adamw("w_ada", _ada_wgrad(c_all.T, dmod_cols))
    for n, g in _unpack_grads(GROUP_IN, scatters["in"].finish(last)).items():
        adamw(n, g)

    return (loss, grad_x, *[outs[n][0] for n in _WEIGHTS], *[outs[n][1] for n in _WEIGHTS], *[outs[n][2] for n in _WEIGHTS],
            *[outs[n][3] for n in _WEIGHTS])
```

```python
import functools

import jax
import jax.numpy as jnp
from jax import lax
from jax.experimental import pallas as pl
from jax.experimental.pallas import tpu as pltpu

F32 = jnp.float32
BF16 = jnp.bfloat16

D_MODEL = 1024
HEAD_DIM = 64
A_Q_HEADS = 16
A_WINDOW = 128
B_PATTERNS = ((128, 1), (512, 4), (2048, 16))
B_HEADS_PER_GROUP = 8
D_FF = 2816
QBLOCK = 128
ROPE_THETA = 10000.0
LN_EPS = 1e-5
DEEPNORM_ALPHA = 2.0 ** 0.25
NEG_INF = -1e30
ADAM_LR, ADAM_B1, ADAM_B2, ADAM_EPS, ADAM_WD, ADAM_STEP = 0.001, 0.9, 0.999, 1e-08, 0.01, 10

N_DEV = 8
MESH_AXES = ("x", "y", "c")
LANES = 128
VMEM_LIMIT_BYTES = 56 * 1024 * 1024
MESH = pl.DeviceIdType.MESH

OFF_QA, OFF_KVA, OFF_QKVB, OFF_GAB = 0, 1024, 1280, 5888
GROUP_IN = (("w_in", 992),)
GROUP_REST = (("w_branch_a", 128), ("w_branch_b", 64), ("w_o", 128), ("w_gate_up", 704), ("w_down", 352))


def _params(*sem):
    return pltpu.CompilerParams(dimension_semantics=sem, vmem_limit_bytes=VMEM_LIMIT_BYTES)


def _sigmoid(x):
    return 1.0 / (1.0 + jnp.exp(-x))


_DIMS = {"nn": (((1,), (0,)), ((), ())), "nt": (((1,), (1,)), ((), ())), "tn": (((0,), (0,)), ((), ()))}


def _matmul(a, b, *, mode, tm, tn, tk, name, out_dtype=None, n=None, b_off=0, token=None, ins=(), outs=None, epilogue=None):
    if mode == "nn":
        (m, k), nn_ = a.shape, b.shape[1]
    elif mode == "nt":
        (m, k), nn_ = a.shape, (b.shape[0] if n is None else n)
    else:
        (k, m), nn_ = a.shape, b.shape[1]
    assert m % tm == 0 and nn_ % tn == 0 and k % tk == 0 and b_off % tn == 0, (name, m, nn_, k)
    nk = k // tk
    joff = b_off // tn
    if mode == "nn":
        a_spec = pl.BlockSpec((tm, tk), lambda i, j, kk: (i, kk))
        b_spec = pl.BlockSpec((tk, tn), lambda i, j, kk: (kk, j))
    elif mode == "nt":
        a_spec = pl.BlockSpec((tm, tk), lambda i, j, kk: (i, kk))
        b_spec = pl.BlockSpec((tn, tk), lambda i, j, kk: (j + joff, kk))
    else:
        a_spec = pl.BlockSpec((tk, tm), lambda i, j, kk: (kk, i))
        b_spec = pl.BlockSpec((tk, tn), lambda i, j, kk: (kk, j))
    dims = _DIMS[mode]
    has_token = token is not None
    plain = epilogue is None
    if plain:
        outs = [(jax.ShapeDtypeStruct((m, nn_), out_dtype), (tm, tn), lambda i, j: (i, j))]

        def epilogue(acc, i, j, in_refs, out_refs):
            out_refs[0][...] = acc.astype(out_refs[0].dtype)

    nin = len(ins)

    def body(*refs):
        a_ref, b_ref = refs[:2]
        in_refs = refs[2:2 + nin]
        out_refs = refs[2 + nin + has_token:-1]
        acc_ref = refs[-1]
        kk = pl.program_id(2)
        part = lax.dot_general(a_ref[...].astype(BF16), b_ref[...].astype(BF16), dims, preferred_element_type=F32)

        def finish(acc):
            epilogue(acc, pl.program_id(0), pl.program_id(1), in_refs, out_refs)

        if nk == 1:
            finish(part)
        else:
            @pl.when(kk == 0)
            def _():
                acc_ref[...] = part

            @pl.when(kk > 0)
            def _():
                acc_ref[...] += part

            @pl.when(kk == nk - 1)
            def _():
                finish(acc_ref[...])

    def spec(block, index):
        return pl.BlockSpec(block, lambda i, j, kk: index(i, j))

    in_specs, args = [a_spec, b_spec], [a, b]
    for arr, block, index in ins:
        in_specs.append(spec(block, index))
        args.append(arr)
    if has_token:
        in_specs.append(pl.BlockSpec(token.shape, lambda i, j, kk: (0, 0)))
        args.append(token)
    res = pl.pallas_call(
        body,
        name=name,
        grid=(m // tm, nn_ // tn, nk),
        in_specs=in_specs,
        out_specs=[spec(block, index) for _, block, index in outs],
        out_shape=[shape for shape, _, _ in outs],
        scratch_shapes=[pltpu.VMEM((tm, tn) if nk > 1 else (8, LANES), F32)],
        compiler_params=_params("arbitrary", "arbitrary", "arbitrary"),
    )(*args)
    return res[0] if plain else res


def _proj_rope(a, bt, cos, sin, *, n, b_off, rope_cols, tm, tn, name, out_dtype=F32):
    m, k = a.shape
    assert m % tm == 0 and n % tn == 0 and b_off % tn == 0 and rope_cols % tn == 0, name
    joff = b_off // tn
    nrope = rope_cols // tn

    def body(a_ref, b_ref, c_ref, s_ref, o_ref):
        acc = lax.dot_general(a_ref[...], b_ref[...], _DIMS["nt"], preferred_element_type=F32)
        j = pl.program_id(1)

        @pl.when(j < nrope)
        def _():
            o_ref[...] = _rope(acc, c_ref[...], s_ref[...]).astype(o_ref.dtype)

        @pl.when(j >= nrope)
        def _():
            o_ref[...] = acc.astype(o_ref.dtype)

    table = pl.BlockSpec((tm, LANES), lambda i, j: (i, 0))
    return pl.pallas_call(
        body,
        name=name,
        grid=(m // tm, n // tn),
        in_specs=[pl.BlockSpec((tm, k), lambda i, j: (i, 0)), pl.BlockSpec((tn, k), lambda i, j: (j + joff, 0)), table, table],
        out_specs=pl.BlockSpec((tm, tn), lambda i, j: (i, j)),
        out_shape=jax.ShapeDtypeStruct((m, n), out_dtype),
        compiler_params=_params("parallel", "parallel"),
    )(a, bt, cos, sin)


ROW_TILE = 256


def _rows(width, col=0):
    return pl.BlockSpec((1, ROW_TILE, width), lambda b, t: (b, t, col))


def _per_batch(nrows, width):
    return pl.BlockSpec((1, nrows, width), lambda b, t: (b, 0, 0))


def _whole(shape):
    return pl.BlockSpec(shape, lambda b, t: (0,) * len(shape))


def _row_call(body, name, bsz, seq, in_specs, out_specs, out_shape, accumulates=False):
    return pl.pallas_call(
        body,
        name=name,
        grid=(bsz, seq // ROW_TILE),
        in_specs=in_specs,
        out_specs=out_specs,
        out_shape=out_shape,
        compiler_params=_params("parallel", "arbitrary" if accumulates else "parallel"),
    )


def _acc_rows(acc_ref, first, rows):
    @pl.when(first)
    def _():
        acc_ref[...] = jnp.zeros_like(acc_ref)

    for r, val in enumerate(rows):
        acc_ref[0, r:r + 1, :] += val


def _colsum(v):
    return jnp.sum(v, axis=0, keepdims=True)


def _ln_stats(z):
    mu = jnp.mean(z, axis=-1, keepdims=True)
    zc = z - mu
    var = jnp.mean(zc * zc, axis=-1, keepdims=True)
    rstd = lax.rsqrt(var + LN_EPS)
    return zc * rstd, rstd


def _ln_bwd(dxhat, xhat, rstd):
    m1 = jnp.mean(dxhat, axis=-1, keepdims=True)
    m2 = jnp.mean(dxhat * xhat, axis=-1, keepdims=True)
    return rstd * (dxhat - m1 - xhat * m2)


def _modulate_in(x, mod):
    bsz, seq, d = x.shape

    def body(x_ref, mod_ref, u_ref):
        u_ref[0] = (x_ref[0] * (1.0 + mod_ref[0, 1:2, :]) + mod_ref[0, 0:1, :]).astype(BF16)

    return _row_call(body, "modulate_in", bsz, seq, [_rows(d), _per_batch(8, d)], _rows(d),
                     jax.ShapeDtypeStruct((bsz, seq, d), BF16))(x, mod)


def _gate_merge(gab, ya, yb):
    bsz, seq, d = ya.shape

    def body(ga_ref, gb_ref, ya_ref, yb_ref, o_ref):
        ga, gb, ya_, yb_ = (r[0].astype(F32) for r in (ga_ref, gb_ref, ya_ref, yb_ref))
        o_ref[0] = (_sigmoid(ga) * ya_ + _sigmoid(gb) * yb_).astype(BF16)

    return _row_call(body, "gate_merge", bsz, seq, [_rows(d, 0), _rows(d, 1), _rows(d), _rows(d)], _rows(d),
                     jax.ShapeDtypeStruct((bsz, seq, d), BF16))(gab, gab, ya, yb)


EP_TILE = 512


def _ep_specs(seq, d):
    tiles = seq // EP_TILE
    return ((EP_TILE, d), lambda i, j: (i, 0)), ((1, 8, d), lambda i, j: (i // tiles, 0, 0)), ((1, d), lambda i, j: (0, 0))


def _wo_ln1(merged, wo, x, mod, g, b, seq):
    ntok, d = x.shape
    row, per_b, whole = _ep_specs(seq, d)

    def epilogue(y, i, j, ins, outs):
        x_ref, mod_ref, g_ref, b_ref = ins
        y_ref, h_ref, u_ref = outs
        z = DEEPNORM_ALPHA * x_ref[...] + (1.0 + mod_ref[0, 2:3, :]) * y
        xhat, _ = _ln_stats(z)
        h = xhat * g_ref[...] + b_ref[...]
        y_ref[...] = y
        h_ref[...] = h
        u_ref[...] = (h * (1.0 + mod_ref[0, 4:5, :]) + mod_ref[0, 3:4, :]).astype(BF16)

    f32, bf16 = jax.ShapeDtypeStruct((ntok, d), F32), jax.ShapeDtypeStruct((ntok, d), BF16)
    return _matmul(merged, wo, mode="nn", tm=EP_TILE, tn=d, tk=d, name="w_o_ln1",
                   ins=[(x,) + row, (mod,) + per_b, (g,) + whole, (b,) + whole],
                   outs=[(f32,) + row, (f32,) + row, (bf16,) + row], epilogue=epilogue)


def _silu_mul(h):
    bsz, seq, _ = h.shape

    def body(hg_ref, hu_ref, a_ref):
        hg = hg_ref[0].astype(F32)
        a_ref[0] = (hg * _sigmoid(hg) * hu_ref[0].astype(F32)).astype(BF16)

    return _row_call(body, "silu_mul", bsz, seq, [_rows(D_FF, 0), _rows(D_FF, 1)], _rows(D_FF),
                     jax.ShapeDtypeStruct((bsz, seq, D_FF), BF16))(h, h)


def _down_ln2_loss_bwd(a, wd, h1, mod, g, b, target, seq):
    ntok, d = h1.shape
    row, per_b, whole = _ep_specs(seq, d)
    tiles = seq // EP_TILE

    def epilogue(y, i, j, ins, outs):
        h_ref, mod_ref, g_ref, b_ref, t_ref = ins
        dy_ref, dh_ref, acc_ref = outs
        gate = 1.0 + mod_ref[0, 5:6, :]
        z = DEEPNORM_ALPHA * h_ref[...] + gate * y
        xhat, rstd = _ln_stats(z)
        diff = xhat * g_ref[...] + b_ref[...] - t_ref[...]
        loss = 0.5 * jnp.sum(jnp.sum(diff * diff, axis=-1, keepdims=True) / d, axis=0, keepdims=True)
        dout = diff / d
        dz = _ln_bwd(dout * g_ref[...], xhat, rstd)
        dy_ref[...] = (gate * dz).astype(BF16)
        dh_ref[...] = DEEPNORM_ALPHA * dz
        _acc_rows(acc_ref, i % tiles == 0,
                  [_colsum(dout * xhat), _colsum(dout), _colsum(dz * y), jnp.broadcast_to(loss, (1, d))])

    return _matmul(a, wd, mode="nn", tm=EP_TILE, tn=d, tk=a.shape[1], name="down_ln2_loss_bwd",
                   ins=[(h1,) + row, (mod,) + per_b, (g,) + whole, (b,) + whole, (target,) + row],
                   outs=[(jax.ShapeDtypeStruct((ntok, d), BF16),) + row, (jax.ShapeDtypeStruct((ntok, d), F32),) + row,
                         (jax.ShapeDtypeStruct((ntok // seq, 8, d), F32),) + per_b], epilogue=epilogue)


def _silu_mul_bwd(da, h):
    bsz, seq, _ = h.shape

    def body(da_ref, hg_ref, hu_ref, dh_ref):
        hg, da_ = hg_ref[0].astype(F32), da_ref[0].astype(F32)
        sg = _sigmoid(hg)
        dh_ref[0, :, :D_FF] = (da_ * hu_ref[0].astype(F32) * (sg * (1.0 + hg * (1.0 - sg)))).astype(BF16)
        dh_ref[0, :, D_FF:] = (da_ * (hg * sg)).astype(BF16)

    return _row_call(body, "silu_mul_bwd", bsz, seq, [_rows(D_FF), _rows(D_FF, 0), _rows(D_FF, 1)], _rows(2 * D_FF),
                     jax.ShapeDtypeStruct((bsz, seq, 2 * D_FF), BF16))(da, h, h)


def _gate_up_dgrad_ln1_bwd(dh, wgut, dh1a, x, y1, mod, g, b, seq):
    ntok, d = x.shape
    row, per_b, whole = _ep_specs(seq, d)
    tiles = seq // EP_TILE

    def epilogue(du, i, j, ins, outs):
        dh_ref, x_ref, y_ref, mod_ref, g_ref, b_ref = ins
        dy_ref, dx_ref, acc_ref = outs
        y = y_ref[...]
        gate = 1.0 + mod_ref[0, 2:3, :]
        z = DEEPNORM_ALPHA * x_ref[...] + gate * y
        xhat, rstd = _ln_stats(z)
        h1 = xhat * g_ref[...] + b_ref[...]
        dh1 = dh_ref[...] + du * (1.0 + mod_ref[0, 4:5, :])
        dz = _ln_bwd(dh1 * g_ref[...], xhat, rstd)
        dy_ref[...] = (gate * dz).astype(BF16)
        dx_ref[...] = DEEPNORM_ALPHA * dz
        _acc_rows(acc_ref, i % tiles == 0,
                  [_colsum(dh1 * xhat), _colsum(dh1), _colsum(dz * y), _colsum(du * h1), _colsum(du)])

    return _matmul(dh, wgut, mode="nn", tm=EP_TILE, tn=d, tk=D_FF, name="gate_up_dgrad_ln1_bwd",
                   ins=[(dh1a,) + row, (x,) + row, (y1,) + row, (mod,) + per_b, (g,) + whole, (b,) + whole],
                   outs=[(jax.ShapeDtypeStruct((ntok, d), BF16),) + row, (jax.ShapeDtypeStruct((ntok, d), F32),) + row,
                         (jax.ShapeDtypeStruct((ntok // seq, 8, d), F32),) + per_b], epilogue=epilogue)


def _wo_dgrad_gate_bwd(dy1, wo, gab, ya, yb):
    ntok, d = ya.shape
    tm, tn = 1024, 512
    tile = ((tm, tn), lambda i, j: (i, j))
    tile_b = ((tm, tn), lambda i, j: (i, j + d // tn))

    def epilogue(dm_, i, j, ins, outs):
        ga_ref, gb_ref, ya_ref, yb_ref = ins
        dya_ref, dyb_ref, dga_ref, dgb_ref = outs
        sa, sb = _sigmoid(ga_ref[...].astype(F32)), _sigmoid(gb_ref[...].astype(F32))
        dya_ref[...] = (dm_ * sa).astype(BF16)
        dyb_ref[...] = (dm_ * sb).astype(BF16)
        dga_ref[...] = (dm_ * ya_ref[...].astype(F32) * sa * (1.0 - sa)).astype(BF16)
        dgb_ref[...] = (dm_ * yb_ref[...].astype(F32) * sb * (1.0 - sb)).astype(BF16)

    shp = jax.ShapeDtypeStruct((ntok, d), BF16)
    return _matmul(dy1, wo, mode="nt", tm=tm, tn=tn, tk=d, name="w_o_dgrad_gate_bwd",
                   ins=[(gab,) + tile, (gab,) + tile_b, (ya,) + tile, (yb,) + tile],
                   outs=[(shp,) + tile] * 4, epilogue=epilogue)


def _grad_x(dxa, du1, x, mod):
    bsz, seq, d = x.shape

    def body(dxa_ref, du_ref, x_ref, mod_ref, gx_ref, acc_ref):
        du = du_ref[0]
        gx_ref[0] = dxa_ref[0] + du * (1.0 + mod_ref[0, 1:2, :])
        _acc_rows(acc_ref, pl.program_id(1) == 0, [_colsum(du * x_ref[0]), _colsum(du)])

    return _row_call(body, "grad_x", bsz, seq, [_rows(d), _rows(d), _rows(d), _per_batch(8, d)],
                     [_rows(d), _per_batch(8, d)],
                     [jax.ShapeDtypeStruct((bsz, seq, d), F32), jax.ShapeDtypeStruct((bsz, 8, d), F32)],
                     accumulates=True)(dxa, du1, x, mod)


def _segsum64(v):
    rows, width = v.shape
    ri = lax.broadcasted_iota(jnp.int32, (LANES, LANES), 0) // HEAD_DIM
    ci = lax.broadcasted_iota(jnp.int32, (LANES, LANES), 1) // HEAD_DIM
    ones = jnp.where(ri == ci, 1.0, 0.0).astype(BF16)
    out = []
    for c in range(width // LANES):
        part = v[:, c * LANES:(c + 1) * LANES]
        hi = part.astype(BF16)
        lo = (part - hi.astype(F32)).astype(BF16)
        out.append(jnp.dot(hi, ones, preferred_element_type=F32) + jnp.dot(lo, ones, preferred_element_type=F32))
    return jnp.concatenate(out, axis=1) if len(out) > 1 else out[0]


def _merge_b(os_, ls_):
    bsz, seq, w = os_[0].shape

    def body(o0, o1, o2, l0, l1, l2, ob_ref):
        ls = [l0[0], l1[0], l2[0]]
        mx = jnp.maximum(jnp.maximum(ls[0], ls[1]), ls[2])
        es = [jnp.exp(l - mx) for l in ls]
        den = es[0] + es[1] + es[2]
        ob_ref[0] = ((es[0] / den) * o0[0] + (es[1] / den) * o1[0] + (es[2] / den) * o2[0]).astype(BF16)

    return _row_call(body, "merge_b", bsz, seq, [_rows(w)] * 6, _rows(w),
                     jax.ShapeDtypeStruct((bsz, seq, w), BF16))(*os_, *ls_)


def _branch_b_dgrad_merge_bwd(dyb, wbbt, os_, ls_):
    ntok, w = os_[0].shape
    row = ((EP_TILE, w), lambda i, j: (i, 0))

    def epilogue(dob_, i, j, ins, outs):
        os_r, ls_r = ins[:3], ins[3:]
        do_r, dd_r = outs[:3], outs[3:]
        ls = [l[...] for l in ls_r]
        mx = jnp.maximum(jnp.maximum(ls[0], ls[1]), ls[2])
        es = [jnp.exp(l - mx) for l in ls]
        den = es[0] + es[1] + es[2]
        ws = [e / den for e in es]
        dws = [_segsum64(dob_ * o[...]) for o in os_r]
        mean = ws[0] * dws[0] + ws[1] * dws[1] + ws[2] * dws[2]
        for wg, do_ref, dd_ref in zip(ws, do_r, dd_r):
            do_ref[...] = wg * dob_
            dd_ref[...] = -wg * mean

    shp = jax.ShapeDtypeStruct((ntok, w), F32)
    return _matmul(dyb, wbbt, mode="nn", tm=EP_TILE, tn=w, tk=dyb.shape[1], name="branch_b_dgrad_merge_bwd",
                   ins=[(v,) + row for v in list(os_) + list(ls_)], outs=[(shp,) + row] * 6, epilogue=epilogue)


def _branch_a_dgrad_delta(dya, wba, oa, lse_a, sinks_exp, seq):
    ntok, w = oa.shape
    row, per_b, whole = _ep_specs(seq, w)
    tiles = seq // EP_TILE

    def epilogue(do_, i, j, ins, outs):
        o_ref, l_ref, s_ref = ins
        do_ref, dd_ref, acc_ref = outs
        dd = -_segsum64(do_ * o_ref[...])
        do_ref[...] = do_
        dd_ref[...] = dd
        _acc_rows(acc_ref, i % tiles == 0, [_colsum(dd * jnp.exp(s_ref[...] - l_ref[...]))])

    shp = jax.ShapeDtypeStruct((ntok, w), F32)
    return _matmul(dya, wba, mode="nt", tm=EP_TILE, tn=w, tk=dya.shape[1], name="branch_a_dgrad_delta",
                   ins=[(oa,) + row, (lse_a,) + row, (sinks_exp,) + whole],
                   outs=[(shp,) + row, (shp,) + row, (jax.ShapeDtypeStruct((ntok // seq, 8, w), F32),) + per_b],
                   epilogue=epilogue)


def _swap_halves(v):
    src = lax.broadcasted_iota(jnp.int32, (LANES, LANES), 0)
    dst = lax.broadcasted_iota(jnp.int32, (LANES, LANES), 1)
    partner = jnp.where((dst % HEAD_DIM) < HEAD_DIM // 2, dst + HEAD_DIM // 2, dst - HEAD_DIM // 2)
    perm = jnp.where(src == partner, 1.0, 0.0).astype(BF16)
    hi = v.astype(BF16)
    lo = (v - hi.astype(F32)).astype(BF16)
    return jnp.dot(hi, perm, preferred_element_type=F32) + jnp.dot(lo, perm, preferred_element_type=F32)


def _swap_halves_roll(v):
    lane = lax.broadcasted_iota(jnp.int32, v.shape, 1)
    return jnp.where((lane % HEAD_DIM) < HEAD_DIM // 2, pltpu.roll(v, LANES - HEAD_DIM // 2, 1),
                     pltpu.roll(v, HEAD_DIM // 2, 1))


def _rope(v, cos, sin, sign=1.0, mxu=True):
    swap = _swap_halves if mxu else _swap_halves_roll
    out = []
    for c in range(v.shape[1] // LANES):
        part = v[:, c * LANES:(c + 1) * LANES]
        out.append(part * cos + sign * (swap(part) * sin))
    return jnp.concatenate(out, axis=1) if len(out) > 1 else out[0]


def _half_mask(shape, half):
    lane = lax.broadcasted_iota(jnp.int32, shape, len(shape) - 1) % LANES
    return (lane < HEAD_DIM) if half == 0 else (lane >= HEAD_DIM)


def _dup_half(v, half):
    return jnp.where(_half_mask(v.shape, half), v, pltpu.roll(v, HEAD_DIM, 1))


def _fold_halves(v):
    return v + pltpu.roll(v, HEAD_DIM, 1)


def _pick_halves(lo_rows, hi_rows):
    return jnp.where(_half_mask(lo_rows.shape, 0), lo_rows, hi_rows)


def _stack_masked(v, pairs):
    parts = []
    for c in pairs:
        pair = v[:, c * LANES:(c + 1) * LANES]
        parts += [jnp.where(_half_mask(pair.shape, half), pair, 0.0) for half in (0, 1)]
    return jnp.concatenate(parts, axis=0)


def _stack_pair_cols(v, pairs):
    return jnp.concatenate([v[:, c * LANES + half * HEAD_DIM:c * LANES + half * HEAD_DIM + 1] for c in pairs for half in (0, 1)],
                           axis=0)


ATTN_UNITS = 16


def _class_rows(r):
    return [pl.ds(0, QBLOCK)] if r == 1 else [pl.ds(rho, QBLOCK, stride=r) for rho in range(r)]


def _band_mask(nrows, nk, blk, n_back, has_prev):
    qi = lax.broadcasted_iota(jnp.int32, (nrows, nk), 0) % QBLOCK
    ki = lax.broadcasted_iota(jnp.int32, (nrows, nk), 1)
    if has_prev:
        dist = qi + QBLOCK - ki
        return (dist >= 0) & (dist <= n_back) & ((ki >= QBLOCK) | (blk > 0))
    dist = qi - ki
    return (dist >= 0) & (dist <= n_back)


def _attn_fwd(q_arr, k_arr, v_arr, *, name, npair, gqa, q_col, k_col, v_col, nchunk, r, n_back, sinks=None):
    bsz, seq, _ = q_arr.shape
    rr = QBLOCK * r
    nblk = seq // rr
    qw = npair * LANES
    kw = LANES if gqa else qw
    has_prev = nblk > 1
    has_sink = sinks is not None
    scale = HEAD_DIM ** -0.5

    def body(*refs):
        refs = list(refs)
        q_ref, kc_ref, vc_ref = refs[:3]
        pos = 3
        if has_prev:
            kp_ref, vp_ref = refs[pos:pos + 2]
            pos += 2
        if has_sink:
            sink_ref = refs[pos]
            pos += 1
        o_ref, lse_ref = refs[pos:pos + 2]
        blk = pl.program_id(2)
        nk = (2 if has_prev else 1) * QBLOCK
        valid = _band_mask(QBLOCK, nk, blk, n_back, has_prev)
        per = npair // 2
        classes = _class_rows(r)
        step = max(1, ATTN_UNITS // (2 * npair))
        for first in range(0, len(classes), step):
            batch = classes[first:first + step]
            units = []
            for ci, rows in enumerate(batch):
                q = q_ref[0, rows, :] * scale
                k, v = kc_ref[0, rows, :], vc_ref[0, rows, :]
                if has_prev:
                    k = jnp.concatenate([kp_ref[0, rows, :], k], axis=0)
                    v = jnp.concatenate([vp_ref[0, rows, :], v], axis=0)
                if gqa:
                    kdup = [_dup_half(k, hk).astype(BF16) for hk in range(2)]
                    vdup = [_dup_half(v, hk) for hk in range(2)]
                for c in range(npair):
                    sl = slice(c * LANES, (c + 1) * LANES)
                    qc = q[:, sl]
                    kc, vc = (kdup[c // per], vdup[c // per]) if gqa else (k[:, sl].astype(BF16), v[:, sl])
                    for half in (0, 1):
                        qm = jnp.where(_half_mask(qc.shape, half), qc, 0.0).astype(BF16)
                        vm = jnp.where(_half_mask(vc.shape, half), vc, 0.0).astype(BF16)
                        s = lax.dot_general(qm, kc, _DIMS["nt"], preferred_element_type=F32)
                        units.append(dict(ci=ci, c=c, half=half, s=s, vm=vm, sk=sink_ref[2 * c + half] if has_sink else None))
            for u in units:
                s = jnp.where(valid, u["s"], NEG_INF)
                m = jnp.max(s, axis=1, keepdims=True)
                if has_sink:
                    m = jnp.maximum(m, u["sk"])
                p = jnp.exp(s - m)
                den = jnp.sum(p, axis=1, keepdims=True)
                if has_sink:
                    den = den + jnp.exp(u["sk"] - m)
                u.update(p=p.astype(BF16), den=den, lse=m + jnp.log(den))
            for u in units:
                u["o"] = jnp.dot(u["p"], u["vm"], preferred_element_type=F32) / u["den"]
            for ci, rows in enumerate(batch):
                outs, lses = [None] * npair, [None] * npair
                for u in units:
                    if u["ci"] != ci:
                        continue
                    c, o = u["c"], u["o"]
                    lse = jnp.broadcast_to(u["lse"], o.shape)
                    outs[c] = o if u["half"] == 0 else outs[c] + o
                    lses[c] = lse if u["half"] == 0 else _pick_halves(lses[c], lse)
                o_ref[0, rows, :] = jnp.concatenate(outs, axis=1) if npair > 1 else outs[0]
                lse_ref[0, rows, :] = jnp.concatenate(lses, axis=1) if npair > 1 else lses[0]

    def cur(width, col0):
        return pl.BlockSpec((1, rr, width), lambda b, c, i: (b, i, col0 + c))

    def prev(width, col0):
        return pl.BlockSpec((1, rr, width), lambda b, c, i: (b, jnp.maximum(i - 1, 0), col0 + c))

    in_specs = [cur(qw, q_col), cur(kw, k_col), cur(kw, v_col)]
    args = [q_arr, k_arr, v_arr]
    if has_prev:
        in_specs += [prev(kw, k_col), prev(kw, v_col)]
        args += [k_arr, v_arr]
    if has_sink:
        in_specs.append(pl.BlockSpec(memory_space=pltpu.SMEM))
        args.append(sinks)
    return pl.pallas_call(
        body,
        name=name,
        grid=(bsz, nchunk, nblk),
        in_specs=in_specs,
        out_specs=[pl.BlockSpec((1, rr, qw), lambda b, c, i: (b, i, c))] * 2,
        out_shape=[jax.ShapeDtypeStruct((bsz, seq, nchunk * qw), F32)] * 2,
        compiler_params=_params("parallel", "parallel", "parallel"),
    )(*args)


def _attn_bwd(q_arr, k_arr, v_arr, cos, sin, do, lse, dd, *, name, npair, gqa, q_col, k_col, v_col, nchunk, r, n_back,
              token=None):
    bsz, seq, _ = q_arr.shape
    rr = QBLOCK * r
    nblk = seq // rr
    qw = npair * LANES
    kw = LANES if gqa else qw
    has_next = nblk > 1
    has_token = token is not None
    scale = HEAD_DIM ** -0.5

    def body(*refs):
        refs = list(refs)
        k_ref, v_ref, c_ref, s_ref = refs[:4]
        tile_refs = [refs[4:8]]
        pos = 8
        if has_next:
            tile_refs.append(refs[pos:pos + 4])
            pos += 4
        if has_token:
            pos += 1
        dq_ref, dk_ref, dv_ref = refs[pos:pos + 3]
        carry_ref = refs[pos + 3]
        blk = pl.program_id(2)
        if has_next:
            @pl.when(blk == 0)
            def _():
                carry_ref[...] = jnp.zeros_like(carry_ref)

        nrows = (npair if gqa else 1) * QBLOCK
        qi = lax.broadcasted_iota(jnp.int32, (nrows, QBLOCK), 0) % QBLOCK
        ki = lax.broadcasted_iota(jnp.int32, (nrows, QBLOCK), 1)
        valids = [qi >= ki, (qi + QBLOCK - ki <= n_back) & (blk + 1 < nblk)]
        per = npair // 2
        ntile = len(tile_refs)
        cat = lambda parts: jnp.concatenate(parts, axis=1) if len(parts) > 1 else parts[0]
        classes = _class_rows(r)
        step = max(1, ATTN_UNITS // (ntile * (2 if gqa else 2 * npair)))
        for first in range(0, len(classes), step):
            batch = classes[first:first + step]
            units = []
            for ci, rows in enumerate(batch):
                tiles = [(q_ref[0, rows, :] * scale, do_ref[0, rows, :], l_ref[0, rows, :], d_ref[0, rows, :])
                         for q_ref, do_ref, l_ref, d_ref in tile_refs]
                k, v = k_ref[0, rows, :], v_ref[0, rows, :]
                if gqa:
                    for hk in range(2):
                        pairs = list(range(hk * per, (hk + 1) * per))
                        kd, vd = _dup_half(k, hk).astype(BF16), _dup_half(v, hk).astype(BF16)
                        for t, (q, do_, l_, d_) in enumerate(tiles):
                            units.append(dict(ci=ci, t=t, hk=hk, pairs=pairs, qs=_stack_masked(q, pairs).astype(BF16),
                                              dos=_stack_masked(do_, pairs).astype(BF16), lcol=_stack_pair_cols(l_, pairs),
                                              dcol=_stack_pair_cols(d_, pairs), kmat=kd, vmat=vd, kdq=kd))
                else:
                    for c in range(npair):
                        sl = slice(c * LANES, (c + 1) * LANES)
                        kc, vcb = k[:, sl], v[:, sl].astype(BF16)
                        kcb = kc.astype(BF16)
                        for t, (q, do_, l_, d_) in enumerate(tiles):
                            for half in (0, 1):
                                hm = _half_mask(kc.shape, half)
                                col = c * LANES + half * HEAD_DIM
                                units.append(dict(ci=ci, t=t, c=c, half=half, qs=jnp.where(hm, q[:, sl], 0.0).astype(BF16),
                                                  dos=jnp.where(hm, do_[:, sl], 0.0).astype(BF16), lcol=l_[:, col:col + 1],
                                                  dcol=d_[:, col:col + 1], kmat=kcb, vmat=vcb,
                                                  kdq=jnp.where(hm, kc, 0.0).astype(BF16)))
            for u in units:
                u["s"] = lax.dot_general(u["qs"], u["kmat"], _DIMS["nt"], preferred_element_type=F32)
                u["dp"] = lax.dot_general(u["dos"], u["vmat"], _DIMS["nt"], preferred_element_type=F32)
            for u in units:
                p = jnp.exp(jnp.where(valids[u["t"]], u["s"], NEG_INF) - u["lcol"])
                u["ds"] = (p * (u["dp"] + u["dcol"])).astype(BF16)
                u["p"] = p.astype(BF16)
            for u in units:
                u["dv"] = lax.dot_general(u["p"], u["dos"], _DIMS["tn"], preferred_element_type=F32)
                u["dk"] = lax.dot_general(u["ds"], u["qs"], _DIMS["tn"], preferred_element_type=F32)
                u["dq"] = jnp.dot(u["ds"], u["kdq"], preferred_element_type=F32) * scale
            for ci, rows in enumerate(batch):
                mine = [u for u in units if u["ci"] == ci]
                dq = [[None] * npair for _ in range(ntile)]
                if gqa:
                    dk_out = dv_out = None
                    for hk in range(2):
                        us = [u for u in mine if u["hk"] == hk]
                        for u in us:
                            for i, c in enumerate(u["pairs"]):
                                dq[u["t"]][c] = _pick_halves(u["dq"][2 * i * QBLOCK:(2 * i + 1) * QBLOCK],
                                                             u["dq"][(2 * i + 1) * QBLOCK:(2 * i + 2) * QBLOCK])
                        dk_h = _fold_halves(functools.reduce(jnp.add, [u["dk"] for u in us]))
                        dv_h = _fold_halves(functools.reduce(jnp.add, [u["dv"] for u in us]))
                        dk_out = dk_h if hk == 0 else _pick_halves(dk_out, dk_h)
                        dv_out = dv_h if hk == 0 else _pick_halves(dv_out, dv_h)
                else:
                    dks, dvs = [], []
                    for c in range(npair):
                        us = [u for u in mine if u["c"] == c]
                        dks.append(functools.reduce(jnp.add, [u["dk"] for u in us]))
                        dvs.append(functools.reduce(jnp.add, [u["dv"] for u in us]))
                        for t in range(ntile):
                            dq[t][c] = functools.reduce(jnp.add, [u["dq"] for u in us if u["t"] == t])
                    dk_out, dv_out = cat(dks), cat(dvs)
                ck, sk_ = c_ref[0, rows, :], s_ref[0, rows, :]
                dk_ref[0, rows, :] = _rope(dk_out, ck, sk_, sign=-1.0, mxu=gqa)
                dv_ref[0, rows, :] = dv_out
                dq_cur = cat(dq[0])
                if has_next:
                    dq_cur = dq_cur + carry_ref[rows, :]
                    carry_ref[rows, :] = cat(dq[1])
                dq_ref[0, rows, :] = _rope(dq_cur, ck, sk_, sign=-1.0, mxu=gqa)

    def at(width, col0, shift):
        return pl.BlockSpec((1, rr, width), lambda b, c, i: (b, jnp.minimum(i + shift, nblk - 1), col0 + c))

    in_specs = [at(kw, k_col, 0), at(kw, v_col, 0), pl.BlockSpec((1, rr, LANES), lambda b, c, i: (b, i, 0)),
                pl.BlockSpec((1, rr, LANES), lambda b, c, i: (b, i, 0))]
    args = [k_arr, v_arr, cos, sin]
    for shift in (0, 1) if has_next else (0,):
        in_specs += [at(qw, q_col, shift), at(qw, 0, shift), at(qw, 0, shift), at(qw, 0, shift)]
        args += [q_arr, do, lse, dd]
    if has_token:
        in_specs.append(pl.BlockSpec(token.shape, lambda b, c, i: (0, 0)))
        args.append(token)
    return pl.pallas_call(
        body,
        name=name,
        grid=(bsz, nchunk, nblk),
        in_specs=in_specs,
        out_specs=[pl.BlockSpec((1, rr, qw), lambda b, c, i: (b, i, c)),
                   pl.BlockSpec((1, rr, kw), lambda b, c, i: (b, i, c)),
                   pl.BlockSpec((1, rr, kw), lambda b, c, i: (b, i, c))],
        out_shape=[jax.ShapeDtypeStruct((bsz, seq, nchunk * qw), F32),
                   jax.ShapeDtypeStruct((bsz, seq, nchunk * kw), F32),
                   jax.ShapeDtypeStruct((bsz, seq, nchunk * kw), F32)],
        scratch_shapes=[pltpu.VMEM((rr, qw) if has_next else (8, LANES), F32)],
        compiler_params=_params("parallel", "parallel", "arbitrary"),
    )(*args)


B_CHUNKS = {1: (4, 1), 4: (1, 4), 16: (1, 4)}


def _rope_tables(positions):
    half = HEAD_DIM // 2
    inv = ROPE_THETA ** (-jnp.arange(half, dtype=F32) / half)
    ang = positions.astype(F32)[..., None] * inv
    cos, sin = jnp.cos(ang), jnp.sin(ang)
    return jnp.concatenate([cos] * 4, axis=-1), jnp.concatenate([-sin, sin, -sin, sin], axis=-1)


def _layer_step(x, mod, positions, sinks, ln1_g, ln1_b, ln2_g, ln2_b, target, get_w_in, get_rest, hook):
    bsz, seq, d = x.shape
    ntok = bsz * seq
    flat = lambda v: v.reshape(ntok, v.shape[-1])
    unflat = lambda v: v.reshape(bsz, seq, v.shape[-1])
    cos, sin = _rope_tables(positions)
    mm = functools.partial(_matmul, tm=1024, tk=1024)
    scalar = lambda tok: 0.0 if tok is None else tok[0, 0]

    u1 = _modulate_in(x, mod)
    u1f = flat(u1)
    wint = get_w_in(u1)
    cosf, sinf = flat(cos), flat(sin)
    proj = functools.partial(_proj_rope, u1f, wint, cosf, sinf, tm=2048)
    qa = unflat(proj(n=1024, b_off=OFF_QA, rope_cols=1024, tn=512, name="proj_qa"))
    kva = unflat(proj(n=256, b_off=OFF_KVA, rope_cols=128, tn=128, name="proj_kva"))
    qkvb = unflat(proj(n=4608, b_off=OFF_QKVB, rope_cols=3072, tn=256, name="proj_qkvb"))
    gab = unflat(proj(n=2048, b_off=OFF_GAB, rope_cols=0, tn=256, name="proj_gab", out_dtype=BF16))

    sink_vec = sinks.reshape(A_Q_HEADS) + scalar(hook("projected", gab))
    a_kw = dict(npair=A_Q_HEADS // 2, gqa=True, q_col=0, k_col=0, v_col=1, nchunk=1, r=1, n_back=A_WINDOW - 1)
    oa, lse_a = _attn_fwd(qa, kva, kva, name="attn_a_fwd", sinks=sink_vec, **a_kw)
    rest = get_rest(oa)
    wba, wbbt, wo, wgut, wd = (rest[n] for n in ("w_branch_a", "w_branch_b", "w_o", "w_gate_up", "w_down"))
    ya = unflat(mm(flat(oa), wba, mode="nn", out_dtype=BF16, tn=512, name="branch_a"))

    b_kws, os_, ls_ = [], [], []
    for g, (window, r) in enumerate(B_PATTERNS):
        npair, nch = B_CHUNKS[r]
        per = B_HEADS_PER_GROUP // (2 * npair)
        nsec = len(B_PATTERNS) * per
        kw_ = dict(npair=npair, gqa=False, q_col=g * per, k_col=nsec + g * per, v_col=2 * nsec + g * per, nchunk=nch, r=r,
                   n_back=window // r)
        b_kws.append(kw_)
        o_g, l_g = _attn_fwd(qkvb, qkvb, qkvb, name=f"attn_b{g}_fwd", **kw_)
        os_.append(o_g)
        ls_.append(l_g)
    ob = _merge_b(os_, ls_)
    yb = unflat(mm(flat(ob), wbbt, mode="nt", out_dtype=BF16, tk=512, tn=512, name="branch_b"))
    merged = _gate_merge(gab, ya, yb)
    xf = flat(x)
    y1f, h1f, u2f = _wo_ln1(flat(merged), wo, xf, mod, ln1_g, ln1_b, seq)
    h = unflat(mm(u2f, wgut, mode="nt", out_dtype=BF16, tn=D_FF // 2, name="gate_up"))
    a = _silu_mul(h)

    dy2f, dh1af, acc2 = _down_ln2_loss_bwd(flat(a), wd, h1f, mod, ln2_g, ln2_b, flat(target), seq)
    da = unflat(mm(dy2f, wd, mode="nt", out_dtype=BF16, tn=D_FF // 2, name="down_dgrad"))
    g_wd = _matmul(flat(a), dy2f, mode="tn", out_dtype=BF16, tm=256, tn=1024, tk=ntok, name="down_wgrad")
    dh = _silu_mul_bwd(da, h)
    dhf = flat(dh)
    g_wgut = _matmul(dhf, u2f, mode="tn", out_dtype=BF16, tm=256, tn=1024, tk=ntok, name="gate_up_wgrad")
    dy1f, dxaf, acc1 = _gate_up_dgrad_ln1_bwd(dhf, wgut, dh1af, xf, y1f, mod, ln1_g, ln1_b, seq)
    g_wo = _matmul(flat(merged), dy1f, mode="tn", out_dtype=BF16, tm=256, tn=1024, tk=ntok, name="w_o_wgrad")
    dyaf, dybf, dgaf, dgbf = _wo_dgrad_gate_bwd(dy1f, wo, flat(gab), flat(ya), flat(yb))
    g_wba = _matmul(flat(oa), dyaf, mode="tn", out_dtype=BF16, tm=256, tn=1024, tk=ntok, name="branch_a_wgrad")
    g_wbbt = _matmul(dybf, flat(ob), mode="tn", out_dtype=BF16, tm=256, tn=512, tk=ntok, name="branch_b_wgrad")
    tok = hook("grads_rest", dict(w_branch_a=g_wba, w_branch_b=g_wbbt, w_o=g_wo, w_gate_up=g_wgut, w_down=g_wd))

    sinks_exp = jnp.repeat(sinks.reshape(1, A_Q_HEADS), HEAD_DIM, axis=1) + scalar(tok)
    doa, dd_a, acc_s = _branch_a_dgrad_delta(dyaf, wba, flat(oa), flat(lse_a), sinks_exp, seq)
    doa, dd_a = unflat(doa), unflat(dd_a)
    tok = hook("delta_done", dd_a)
    dqa, dka, dva = _attn_bwd(qa, kva, kva, cos, sin, doa, lse_a, dd_a, name="attn_a_bwd", token=tok, **a_kw)
    merged_bwd = [unflat(t) for t in _branch_b_dgrad_merge_bwd(dybf, wbbt, [flat(t) for t in os_], [flat(t) for t in ls_])]
    dqs, dks, dvs = [], [], []
    for g in range(len(B_PATTERNS)):
        dq_g, dk_g, dv_g = _attn_bwd(qkvb, qkvb, qkvb, cos, sin, merged_bwd[g], ls_[g], merged_bwd[3 + g],
                                     name=f"attn_b{g}_bwd", **b_kws[g])
        dqs.append(dq_g)
        dks.append(dk_g)
        dvs.append(dv_g)
    dproj = jnp.concatenate([t.astype(BF16) for t in [dqa, dka, dva] + dqs + dks + dvs] + [unflat(dgaf), unflat(dgbf)], axis=-1)
    dprojf = flat(dproj)
    g_wint = _matmul(dprojf, u1f, mode="tn", out_dtype=BF16, tm=256, tn=1024, tk=ntok, name="w_in_wgrad")
    tok = hook("grads_w_in", dict(w_in=g_wint))
    du1 = unflat(_matmul(dprojf, wint, mode="nn", out_dtype=F32, tm=1024, tn=512, tk=wint.shape[0] // 2, name="w_in_dgrad",
                         token=tok))
    tok = hook("dgrad_done", du1)
    grad_x, acc0 = _grad_x(unflat(dxaf), du1, x, mod + scalar(tok))

    loss_part = jnp.sum(acc2[:, 3, 0])
    dmod = jnp.stack([acc0[:, 1], acc0[:, 0], acc1[:, 2], acc1[:, 4], acc1[:, 3], acc2[:, 2]], axis=1)
    small = jnp.stack([acc1[:, 0].sum(0), acc1[:, 1].sum(0), acc2[:, 0].sum(0), acc2[:, 1].sum(0), acc_s[:, 0].sum(0)])
    return loss_part, grad_x, dmod, small


CHIP_FLIPS = (2, 4, 6)


def _my_place():
    return lax.axis_index("x"), lax.axis_index("y"), lax.axis_index("c")


def _flip(place, k):
    px, py, pc = place
    return (1 - px if k & 4 else px, 1 - py if k & 2 else py, 1 - pc if k & 1 else pc)


def _index(place):
    return 4 * place[0] + 2 * place[1] + place[2]


def _gather_small(v, name):
    rows, cols = v.shape

    def body(v_ref, out_ref, send_sems, recv_sems):
        me = _my_place()
        out_ref[_index(me)] = v_ref[...]
        copies = []
        for k in range(1, N_DEV):
            copies.append(pltpu.make_async_remote_copy(
                src_ref=v_ref, dst_ref=out_ref.at[_index(me)], send_sem=send_sems.at[k - 1], recv_sem=recv_sems.at[k - 1],
                device_id=_flip(me, k), device_id_type=MESH))
        for cp in copies:
            cp.start()
        for k in range(1, N_DEV):
            pltpu.make_async_remote_copy(
                src_ref=v_ref, dst_ref=out_ref.at[_index(_flip(me, k))], send_sem=send_sems.at[k - 1],
                recv_sem=recv_sems.at[k - 1], device_id=_flip(me, k), device_id_type=MESH).wait_recv()
        for cp in copies:
            cp.wait_send()

    return pl.pallas_call(
        body,
        name=name,
        out_shape=jax.ShapeDtypeStruct((N_DEV, rows, cols), v.dtype),
        in_specs=[pl.BlockSpec(memory_space=pltpu.VMEM)],
        out_specs=pl.BlockSpec(memory_space=pltpu.VMEM),
        scratch_shapes=[pltpu.SemaphoreType.DMA((N_DEV - 1,)), pltpu.SemaphoreType.DMA((N_DEV - 1,))],
        compiler_params=pltpu.CompilerParams(vmem_limit_bytes=VMEM_LIMIT_BYTES),
    )(v)


_HBM = pl.BlockSpec(memory_space=pltpu.HBM)
_SEM = pl.BlockSpec(memory_space=pltpu.SEMAPHORE)
_EFFECT = pltpu.SideEffectType.DATAFLOW_SIDE_EFFECTING


def _remote(src, dst, send_sems, recv_sems, j, to):
    return pltpu.make_async_remote_copy(src_ref=src, dst_ref=dst, send_sem=send_sems.at[j], recv_sem=recv_sems.at[j],
                                        device_id=to, device_id_type=MESH)


def _copies_start(name, src, land_shape, make_copies, nsem):
    def body(src_ref, land_ref, send_sems, recv_sems, src_thru, land_thru, token):
        for cp in make_copies(src_ref, land_ref, send_sems, recv_sems):
            cp.start()
        token[...] = jnp.zeros_like(token)

    sems = pltpu.SemaphoreType.DMA((nsem,))
    return pl.pallas_call(
        body, name=name,
        out_shape=(sems, sems, pltpu.HBM(src.shape, src.dtype), pltpu.HBM(land_shape, src.dtype),
                   jax.ShapeDtypeStruct((8, LANES), F32)),
        in_specs=(_HBM, _HBM), out_specs=(_SEM, _SEM, _HBM, _HBM, pl.BlockSpec(memory_space=pltpu.VMEM)),
        input_output_aliases={0: 2, 1: 3},
        compiler_params=pltpu.CompilerParams(has_side_effects=_EFFECT),
    )(pltpu.with_memory_space_constraint(src, pltpu.HBM),
      pltpu.with_memory_space_constraint(lax.empty(land_shape, src.dtype), pltpu.HBM))


def _copies_wait(name, started, make_copies, after):
    send_sems, recv_sems, src_thru, land_thru, _ = started

    def body(src_ref, land_ref, send_sems, recv_sems, after_ref, src_dead, got_ref):
        for cp in make_copies(src_ref, land_ref, send_sems, recv_sems):
            cp.wait_send()
            cp.wait_recv()

    return pl.pallas_call(
        body, name=name,
        out_shape=(pltpu.HBM(src_thru.shape, src_thru.dtype), pltpu.HBM(land_thru.shape, land_thru.dtype)),
        in_specs=(_HBM, _HBM, _SEM, _SEM, pl.BlockSpec(memory_space=pl.ANY)), out_specs=(_HBM, _HBM),
        input_output_aliases={0: 0, 1: 1},
        compiler_params=pltpu.CompilerParams(has_side_effects=_EFFECT),
    )(src_thru, land_thru, send_sems, recv_sems, after)


def _gather1_copies(src_ref, land_ref, send_sems, recv_sems):
    me = _my_place()
    return [_remote(src_ref, land_ref.at[_index(me)], send_sems, recv_sems, j, _flip(me, k)) for j, k in enumerate((1,) + CHIP_FLIPS)]


def _gather2_copies(src_ref, land_ref, send_sems, recv_sems):
    me = _my_place()
    return [_remote(src_ref.at[_index(_flip(me, k))], land_ref.at[j], send_sems, recv_sems, j, _flip(me, 1))
            for j, k in enumerate(CHIP_FLIPS)]


def _to_sibling_copies(src_ref, land_ref, send_sems, recv_sems):
    me = _my_place()
    return [_remote(src_ref.at[1 - me[2]], land_ref, send_sems, recv_sems, 0, _flip(me, 1))]


def _to_chips_copies(src_ref, land_ref, send_sems, recv_sems):
    me = _my_place()
    copies = []
    for j, k in enumerate(CHIP_FLIPS):
        to = _flip(me, k)
        copies.append(_remote(src_ref.at[2 * to[0] + to[1]], land_ref.at[j], send_sems, recv_sems, j, to))
    return copies


class _Gather:
    def __init__(self, name, packed):
        self.name, self.packed = name, packed
        self.rows = packed.shape[0]
        self.first = _copies_start(name + "_start", packed, (N_DEV, self.rows, D_MODEL), _gather1_copies, 4)
        self.token = self.first[4]

    def pass_on(self, after):
        _, land = _copies_wait(self.name + "_wait", self.first, _gather1_copies, after)
        self.second = _copies_start(self.name + "_pass_start", land, (3, self.rows, D_MODEL), _gather2_copies, 3)
        return self.second[4]

    def finish(self, after):
        full, passed = _copies_wait(self.name + "_pass_wait", self.second, _gather2_copies, after)
        me = _my_place()
        full = lax.dynamic_update_slice(full, self.packed[None], (_index(me), 0, 0))
        for j, k in enumerate(CHIP_FLIPS):
            full = lax.dynamic_update_slice(full, passed[j][None], (_index(_flip(me, k | 1)), 0, 0))
        return full


SUM_SPLIT = 2


def _sum_pairs(mine, theirs):
    nchip, rows, cols = mine.shape
    tile = rows // SUM_SPLIT
    spec = pl.BlockSpec((1, tile, cols), lambda q, t: (q, t, 0))

    def body(a_ref, b_ref, o_ref):
        o_ref[...] = (a_ref[...].astype(F32) + b_ref[...].astype(F32)).astype(BF16)

    return pl.pallas_call(body, name="grad_sum_sibling", grid=(nchip, SUM_SPLIT), in_specs=[spec, spec], out_specs=spec,
                          out_shape=jax.ShapeDtypeStruct(mine.shape, BF16), compiler_params=_params("parallel", "parallel"))(mine, theirs)


def _sum_final(own, got):
    rows, cols = own.shape
    tile = rows // SUM_SPLIT

    def body(a_ref, g_ref, o_ref):
        o_ref[...] = ((a_ref[...].astype(F32) + g_ref[0].astype(F32)) + g_ref[1].astype(F32)) + g_ref[2].astype(F32)

    return pl.pallas_call(
        body, name="grad_sum_chips", grid=(SUM_SPLIT,),
        in_specs=[pl.BlockSpec((tile, cols), lambda t: (t, 0)), pl.BlockSpec((3, tile, cols), lambda t: (0, t, 0))],
        out_specs=pl.BlockSpec((tile, cols), lambda t: (t, 0)),
        out_shape=jax.ShapeDtypeStruct((rows, cols), F32), compiler_params=_params("parallel"))(own, got)


class _ReduceScatter:
    def __init__(self, name, slabs):
        self.name = name
        rows = slabs.shape[1]
        self.rows = rows
        self.parts = slabs.reshape(4, 2, rows, D_MODEL).transpose(1, 0, 2, 3)
        self.first = _copies_start(name + "_sibling_start", self.parts, (4, rows, D_MODEL), _to_sibling_copies, 1)
        self.token = self.first[4]

    def between_chips(self, after):
        parts, theirs = _copies_wait(self.name + "_sibling_wait", self.first, _to_sibling_copies, after)
        mine = lax.dynamic_index_in_dim(parts, lax.axis_index("c"), 0, keepdims=False)
        chip_sum = _sum_pairs(mine, theirs)
        self.second = _copies_start(self.name + "_chips_start", chip_sum, (3, self.rows, D_MODEL), _to_chips_copies, 3)
        return self.second[4]

    def finish(self, after):
        chip_sum, got = _copies_wait(self.name + "_chips_wait", self.second, _to_chips_copies, after)
        my_chip = 2 * lax.axis_index("x") + lax.axis_index("y")
        return _sum_final(lax.dynamic_index_in_dim(chip_sum, my_chip, 0, keepdims=False), got)


def _ada_fwd(c_all, w, b):
    nb, _ = c_all.shape
    ncol = w.shape[1]

    def body(c_ref, w_ref, b_ref, o_ref):
        c = c_ref[...]
        act = (c * _sigmoid(c)).astype(BF16)
        o_ref[...] = jnp.dot(act, w_ref[...].astype(BF16), preferred_element_type=F32) + b_ref[...]

    return pl.pallas_call(body, name="ada_fwd", out_shape=jax.ShapeDtypeStruct((nb, ncol), F32),
                          compiler_params=pltpu.CompilerParams(vmem_limit_bytes=VMEM_LIMIT_BYTES))(c_all, w, b)


def _ada_wgrad(c_all_t, dmod_cols):
    d, nb = c_all_t.shape
    ncol = dmod_cols.shape[1]

    def body(ct_ref, dm_ref, o_ref):
        ct = ct_ref[...]
        act = (ct * _sigmoid(ct)).astype(BF16).astype(F32)
        dm = dm_ref[...].astype(BF16).astype(F32)
        acc = act[:, 0:1] * dm[0:1, :]
        for i in range(1, nb):
            acc = acc + act[:, i:i + 1] * dm[i:i + 1, :]
        o_ref[...] = acc

    return pl.pallas_call(body, name="ada_wgrad", out_shape=jax.ShapeDtypeStruct((d, ncol), F32),
                          compiler_params=pltpu.CompilerParams(vmem_limit_bytes=VMEM_LIMIT_BYTES))(c_all_t, dmod_cols)


SMALL_ROWS = 24


def _reduce_small(gathered):
    def body(g_ref, o_ref):
        acc = g_ref[0]
        for dev in range(1, N_DEV):
            acc = acc + g_ref[dev]
        o_ref[...] = acc

    return pl.pallas_call(body, name="reduce_small", out_shape=jax.ShapeDtypeStruct(gathered.shape[1:], F32))(gathered)


def _adamw(w, g, m, v, name):
    rows, cols = w.shape
    tile = rows
    for cand in (256, 128, 64, 32, 16, 8):
        if rows % cand == 0 and rows > cand:
            tile = cand
            break
    spec = pl.BlockSpec((tile, cols), lambda t: (t, 0))
    bc1 = 1.0 - ADAM_B1 ** ADAM_STEP
    bc2 = 1.0 - ADAM_B2 ** ADAM_STEP

    def body(w_ref, g_ref, m_ref, v_ref, d_ref, nm_ref, nv_ref):
        g_ = g_ref[...]
        nm = ADAM_B1 * m_ref[...] + (1.0 - ADAM_B1) * g_
        nv = ADAM_B2 * v_ref[...] + (1.0 - ADAM_B2) * (g_ * g_)
        d_ref[...] = -ADAM_LR * ((nm / bc1) / (jnp.sqrt(nv / bc2) + ADAM_EPS) + ADAM_WD * w_ref[...])
        nm_ref[...] = nm
        nv_ref[...] = nv

    shp = jax.ShapeDtypeStruct((rows, cols), F32)
    return pl.pallas_call(body, name=name, grid=(rows // tile,), in_specs=[spec] * 4, out_specs=[spec] * 3, out_shape=[shp] * 3,
                          compiler_params=_params("parallel"))(w, g, m, v)


_WEIGHTS = ("w_ada", "b_ada", "w_in", "sinks", "w_branch_a", "w_branch_b", "w_o", "ln1_g", "ln1_b", "w_gate_up", "w_down",
            "ln2_g", "ln2_b")
_TRANSPOSED = ("w_in", "w_branch_b", "w_gate_up")


def _pack_shard(name, w):
    w = w.astype(BF16)
    if name in _TRANSPOSED:
        w = w.T
    return w.reshape(-1, D_MODEL)


def _unpack_full(name, slab):
    if name == "w_branch_b":
        return slab.reshape(N_DEV * 128, 512)
    return slab.reshape(-1, D_MODEL)


def _unpack_group(group, gathered):
    full, off = {}, 0
    for n, r in group:
        full[n] = _unpack_full(n, gathered[:, off:off + r])
        off += r
    return full


def _unpack_grads(group, g_packed):
    g_w, off = {}, 0
    for n, r in group:
        part = g_packed[off:off + r]
        off += r
        if n == "w_branch_b":
            part = part.reshape(128, 512)
        g_w[n] = part.T if n in _TRANSPOSED else part
    return g_w


def kernel(x, c, positions, w_ada, b_ada, w_in, sinks, w_branch_a, w_branch_b, w_o, ln1_g, ln1_b, w_gate_up, w_down, ln2_g, ln2_b, loss_target, m_w_ada, m_b_ada, m_w_in, m_sinks, m_w_branch_a, m_w_branch_b, m_w_o, m_ln1_g, m_ln1_b, m_w_gate_up, m_w_down, m_ln2_g, m_ln2_b, v_w_ada, v_b_ada, v_w_in, v_sinks, v_w_branch_a, v_w_branch_b, v_w_o, v_ln1_g, v_ln1_b, v_w_gate_up, v_w_down, v_ln2_g, v_ln2_b):
    weights = dict(w_ada=w_ada, b_ada=b_ada, w_in=w_in, sinks=sinks, w_branch_a=w_branch_a, w_branch_b=w_branch_b, w_o=w_o,
                   ln1_g=ln1_g, ln1_b=ln1_b, w_gate_up=w_gate_up, w_down=w_down, ln2_g=ln2_g, ln2_b=ln2_b)
    m_in = dict(w_ada=m_w_ada, b_ada=m_b_ada, w_in=m_w_in, sinks=m_sinks, w_branch_a=m_w_branch_a, w_branch_b=m_w_branch_b,
                w_o=m_w_o, ln1_g=m_ln1_g, ln1_b=m_ln1_b, w_gate_up=m_w_gate_up, w_down=m_w_down, ln2_g=m_ln2_g, ln2_b=m_ln2_b)
    v_in = dict(w_ada=v_w_ada, b_ada=v_b_ada, w_in=v_w_in, sinks=v_sinks, w_branch_a=v_w_branch_a, w_branch_b=v_w_branch_b,
                w_o=v_w_o, ln1_g=v_ln1_g, ln1_b=v_ln1_b, w_gate_up=v_w_gate_up, w_down=v_w_down, ln2_g=v_ln2_g, ln2_b=v_ln2_b)
    bsz = x.shape[0]
    me = _index(_my_place())
    ada_cols = w_ada.shape[2]
    outs = {}

    def adamw(n, g):
        w2, m2, v2 = (t[n][0] if t[n].ndim == 3 else t[n] for t in (weights, m_in, v_in))
        shape = weights[n].shape
        dlt, nm, nv = _adamw(w2, g, m2, v2, "adamw_" + n)
        outs[n] = tuple(t.reshape(shape) for t in (g, dlt, nm, nv))
        return nv

    packed_in = jnp.concatenate([_pack_shard(n, weights[n][0]) for n, _ in GROUP_IN], axis=0)
    packed_rest = jnp.concatenate([_pack_shard(n, weights[n][0]) for n, _ in GROUP_REST], axis=0)
    c_all = _gather_small(jnp.pad(c, ((0, 8 - bsz), (0, 0))), "gather_c")[:, :bsz].reshape(N_DEV * bsz, D_MODEL)
    gather_in = _Gather("gather_w_in", lax.optimization_barrier((packed_in, c_all))[0])
    b_cols = lax.dynamic_slice_in_dim(b_ada, me * ada_cols, ada_cols, axis=1)
    mod_cols = _ada_fwd(c_all, w_ada[0], b_cols + gather_in.token[0, 0])
    mod_all = _gather_small(mod_cols, "gather_mod").transpose(1, 0, 2).reshape(N_DEV * bsz, 6, D_MODEL)
    gather_rest = _Gather("gather_rest", lax.optimization_barrier((packed_rest, mod_all))[0])
    mod = jnp.pad(lax.dynamic_slice_in_dim(mod_all, me * bsz, bsz, axis=0), ((0, 0), (0, 2), (0, 0)))
    mod = mod + gather_rest.token[0, 0]
    mod = mod + gather_in.pass_on(mod)[0, 0]

    scatters = {}

    def get_w_in(after):
        return _unpack_group(GROUP_IN, gather_in.finish(after))["w_in"]

    def get_rest(after):
        return _unpack_group(GROUP_REST, gather_rest.finish(after))

    def pack_grads(group, grads):
        return jnp.concatenate([grads[n].reshape(N_DEV, r, D_MODEL) for n, r in group], axis=1)

    def hook(point, value):
        if point == "projected":
            return gather_rest.pass_on(value)
        if point == "grads_rest":
            scatters["rest"] = _ReduceScatter("scatter_rest", pack_grads(GROUP_REST, value))
            return scatters["rest"].token
        if point == "delta_done":
            return scatters["rest"].between_chips(value)
        if point == "grads_w_in":
            scatters["in"] = _ReduceScatter("scatter_w_in", pack_grads(GROUP_IN, value))
            return scatters["in"].token
        if point == "dgrad_done":
            tok = scatters["in"].between_chips(value)
            for n, g in _unpack_grads(GROUP_REST, scatters["rest"].finish(tok)).items():
                adamw(n, g)
            return tok
        raise ValueError(point)

    loss_part, grad_x, dmod, small = _layer_step(x, mod, positions, sinks[0], ln1_g, ln1_b, ln2_g, ln2_b, loss_target,
                                                 get_w_in, get_rest, hook)
    loss = lax.psum(loss_part, MESH_AXES)

    rows = jnp.concatenate([dmod.reshape(bsz * 6, D_MODEL), small, jnp.zeros((SMALL_ROWS - bsz * 6 - 5, D_MODEL), F32)], axis=0)
    small_all = _gather_small(rows, "gather_small")
    sums = _reduce_small(small_all)
    dmod_all = small_all[:, :bsz * 6].reshape(N_DEV * bsz, 6 * D_MODEL)
    adamw("b_ada", functools.reduce(jnp.add, [sums[6 * i:6 * i + 6] for i in range(bsz)]).reshape(1, 6 * D_MODEL))
    for i, n in enumerate(("ln1_g", "ln1_b", "ln2_g", "ln2_b")):
        adamw(n, sums[12 + i][None])
    adamw("sinks", sums[16][::HEAD_DIM][None])
    dmod_cols = lax.dynamic_slice_in_dim(dmod_all, me * ada_cols, ada_cols, axis=1)
    last = adamw("w_ada", _ada_wgrad(c_all.T, dmod_cols))
    for n, g in _unpack_grads(GROUP_IN, scatters["in"].finish(last)).items():
        adamw(n, g)

    return (loss, grad_x, *[outs[n][0] for n in _WEIGHTS], *[outs[n][1] for n in _WEIGHTS], *[outs[n][2] for n in _WEIGHTS],
            *[outs[n][3] for n in _WEIGHTS])
```

```python
import functools

import jax
import jax.numpy as jnp
from jax import lax
from jax.experimental import pallas as pl
from jax.experimental.pallas import tpu as pltpu

F32 = jnp.float32
BF16 = jnp.bfloat16

D_MODEL = 1024
HEAD_DIM = 64
A_Q_HEADS = 16
A_WINDOW = 128
B_PATTERNS = ((128, 1), (512, 4), (2048, 16))
B_HEADS_PER_GROUP = 8
D_FF = 2816
QBLOCK = 128
ROPE_THETA = 10000.0
LN_EPS = 1e-5
DEEPNORM_ALPHA = 2.0 ** 0.25
NEG_INF = -1e30
ADAM_LR, ADAM_B1, ADAM_B2, ADAM_EPS, ADAM_WD, ADAM_STEP = 0.001, 0.9, 0.999, 1e-08, 0.01, 10

N_DEV = 8
MESH_AXES = ("x", "y", "c")
LANES = 128
VMEM_LIMIT_BYTES = 56 * 1024 * 1024
MESH = pl.DeviceIdType.MESH

OFF_QA, OFF_KVA, OFF_QKVB, OFF_GAB = 0, 1024, 1280, 5888
GROUP_IN = (("w_in", 992),)
GROUP_REST = (("w_branch_a", 128), ("w_branch_b", 64), ("w_o", 128), ("w_gate_up", 704), ("w_down", 352))


def _params(*sem):
    return pltpu.CompilerParams(dimension_semantics=sem, vmem_limit_bytes=VMEM_LIMIT_BYTES)


def _sigmoid(x):
    return 1.0 / (1.0 + jnp.exp(-x))


_DIMS = {"nn": (((1,), (0,)), ((), ())), "nt": (((1,), (1,)), ((), ())), "tn": (((0,), (0,)), ((), ()))}


def _matmul(a, b, *, mode, tm, tn, tk, name, out_dtype=None, n=None, b_off=0, token=None, ins=(), outs=None, epilogue=None):
    if mode == "nn":
        (m, k), nn_ = a.shape, b.shape[1]
    elif mode == "nt":
        (m, k), nn_ = a.shape, (b.shape[0] if n is None else n)
    else:
        (k, m), nn_ = a.shape, b.shape[1]
    assert m % tm == 0 and nn_ % tn == 0 and k % tk == 0 and b_off % tn == 0, (name, m, nn_, k)
    nk = k // tk
    joff = b_off // tn
    if mode == "nn":
        a_spec = pl.BlockSpec((tm, tk), lambda i, j, kk: (i, kk))
        b_spec = pl.BlockSpec((tk, tn), lambda i, j, kk: (kk, j))
    elif mode == "nt":
        a_spec = pl.BlockSpec((tm, tk), lambda i, j, kk: (i, kk))
        b_spec = pl.BlockSpec((tn, tk), lambda i, j, kk: (j + joff, kk))
    else:
        a_spec = pl.BlockSpec((tk, tm), lambda i, j, kk: (kk, i))
        b_spec = pl.BlockSpec((tk, tn), lambda i, j, kk: (kk, j))
    dims = _DIMS[mode]
    has_token = token is not None
    plain = epilogue is None
    if plain:
        outs = [(jax.ShapeDtypeStruct((m, nn_), out_dtype), (tm, tn), lambda i, j: (i, j))]

        def epilogue(acc, i, j, in_refs, out_refs):
            out_refs[0][...] = acc.astype(out_refs[0].dtype)

    nin = len(ins)

    def body(*refs):
        a_ref, b_ref = refs[:2]
        in_refs = refs[2:2 + nin]
        out_refs = refs[2 + nin + has_token:-1]
        acc_ref = refs[-1]
        kk = pl.program_id(2)
        part = lax.dot_general(a_ref[...].astype(BF16), b_ref[...].astype(BF16), dims, preferred_element_type=F32)

        def finish(acc):
            epilogue(acc, pl.program_id(0), pl.program_id(1), in_refs, out_refs)

        if nk == 1:
            finish(part)
        else:
            @pl.when(kk == 0)
            def _():
                acc_ref[...] = part

            @pl.when(kk > 0)
            def _():
                acc_ref[...] += part

            @pl.when(kk == nk - 1)
            def _():
                finish(acc_ref[...])

    def spec(block, index):
        return pl.BlockSpec(block, lambda i, j, kk: index(i, j))

    in_specs, args = [a_spec, b_spec], [a, b]
    for arr, block, index in ins:
        in_specs.append(spec(block, index))
        args.append(arr)
    if has_token:
        in_specs.append(pl.BlockSpec(token.shape, lambda i, j, kk: (0, 0)))
        args.append(token)
    res = pl.pallas_call(
        body,
        name=name,
        grid=(m // tm, nn_ // tn, nk),
        in_specs=in_specs,
        out_specs=[spec(block, index) for _, block, index in outs],
        out_shape=[shape for shape, _, _ in outs],
        scratch_shapes=[pltpu.VMEM((tm, tn) if nk > 1 else (8, LANES), F32)],
        compiler_params=_params("arbitrary", "arbitrary", "arbitrary"),
    )(*args)
    return res[0] if plain else res


def _proj_rope(a, bt, cos, sin, *, n, b_off, rope_cols, tm, tn, name, out_dtype=F32):
    m, k = a.shape
    assert m % tm == 0 and n % tn == 0 and b_off % tn == 0 and rope_cols % tn == 0, name
    joff = b_off // tn
    nrope = rope_cols // tn

    def body(a_ref, b_ref, c_ref, s_ref, o_ref):
        acc = lax.dot_general(a_ref[...], b_ref[...], _DIMS["nt"], preferred_element_type=F32)
        j = pl.program_id(1)

        @pl.when(j < nrope)
        def _():
            o_ref[...] = _rope(acc, c_ref[...], s_ref[...]).astype(o_ref.dtype)

        @pl.when(j >= nrope)
        def _():
            o_ref[...] = acc.astype(o_ref.dtype)

    table = pl.BlockSpec((tm, LANES), lambda i, j: (i, 0))
    return pl.pallas_call(
        body,
        name=name,
        grid=(m // tm, n // tn),
        in_specs=[pl.BlockSpec((tm, k), lambda i, j: (i, 0)), pl.BlockSpec((tn, k), lambda i, j: (j + joff, 0)), table, table],
        out_specs=pl.BlockSpec((tm, tn), lambda i, j: (i, j)),
        out_shape=jax.ShapeDtypeStruct((m, n), out_dtype),
        compiler_params=_params("parallel", "parallel"),
    )(a, bt, cos, sin)


ROW_TILE = 256


def _rows(width, col=0):
    return pl.BlockSpec((1, ROW_TILE, width), lambda b, t: (b, t, col))


def _per_batch(nrows, width):
    return pl.BlockSpec((1, nrows, width), lambda b, t: (b, 0, 0))


def _whole(shape):
    return pl.BlockSpec(shape, lambda b, t: (0,) * len(shape))


def _row_call(body, name, bsz, seq, in_specs, out_specs, out_shape, accumulates=False):
    return pl.pallas_call(
        body,
        name=name,
        grid=(bsz, seq // ROW_TILE),
        in_specs=in_specs,
        out_specs=out_specs,
        out_shape=out_shape,
        compiler_params=_params("parallel", "arbitrary" if accumulates else "parallel"),
    )


def _acc_rows(acc_ref, first, rows):
    @pl.when(first)
    def _():
        acc_ref[...] = jnp.zeros_like(acc_ref)

    for r, val in enumerate(rows):
        acc_ref[0, r:r + 1, :] += val


def _colsum(v):
    return jnp.sum(v, axis=0, keepdims=True)


def _ln_stats(z):
    mu = jnp.mean(z, axis=-1, keepdims=True)
    zc = z - mu
    var = jnp.mean(zc * zc, axis=-1, keepdims=True)
    rstd = lax.rsqrt(var + LN_EPS)
    return zc * rstd, rstd


def _ln_bwd(dxhat, xhat, rstd):
    m1 = jnp.mean(dxhat, axis=-1, keepdims=True)
    m2 = jnp.mean(dxhat * xhat, axis=-1, keepdims=True)
    return rstd * (dxhat - m1 - xhat * m2)


def _modulate_in(x, mod):
    bsz, seq, d = x.shape

    def body(x_ref, mod_ref, u_ref):
        u_ref[0] = (x_ref[0] * (1.0 + mod_ref[0, 1:2, :]) + mod_ref[0, 0:1, :]).astype(BF16)

    return _row_call(body, "modulate_in", bsz, seq, [_rows(d), _per_batch(8, d)], _rows(d),
                     jax.ShapeDtypeStruct((bsz, seq, d), BF16))(x, mod)


def _gate_merge(gab, ya, yb):
    bsz, seq, d = ya.shape

    def body(ga_ref, gb_ref, ya_ref, yb_ref, o_ref):
        ga, gb, ya_, yb_ = (r[0].astype(F32) for r in (ga_ref, gb_ref, ya_ref, yb_ref))
        o_ref[0] = (_sigmoid(ga) * ya_ + _sigmoid(gb) * yb_).astype(BF16)

    return _row_call(body, "gate_merge", bsz, seq, [_rows(d, 0), _rows(d, 1), _rows(d), _rows(d)], _rows(d),
                     jax.ShapeDtypeStruct((bsz, seq, d), BF16))(gab, gab, ya, yb)


EP_TILE = 512


def _ep_specs(seq, d):
    tiles = seq // EP_TILE
    return ((EP_TILE, d), lambda i, j: (i, 0)), ((1, 8, d), lambda i, j: (i // tiles, 0, 0)), ((1, d), lambda i, j: (0, 0))


def _wo_ln1(merged, wo, x, mod, g, b, seq):
    ntok, d = x.shape
    row, per_b, whole = _ep_specs(seq, d)

    def epilogue(y, i, j, ins, outs):
        x_ref, mod_ref, g_ref, b_ref = ins
        y_ref, h_ref, u_ref = outs
        z = DEEPNORM_ALPHA * x_ref[...] + (1.0 + mod_ref[0, 2:3, :]) * y
        xhat, _ = _ln_stats(z)
        h = xhat * g_ref[...] + b_ref[...]
        y_ref[...] = y
        h_ref[...] = h
        u_ref[...] = (h * (1.0 + mod_ref[0, 4:5, :]) + mod_ref[0, 3:4, :]).astype(BF16)

    f32, bf16 = jax.ShapeDtypeStruct((ntok, d), F32), jax.ShapeDtypeStruct((ntok, d), BF16)
    return _matmul(merged, wo, mode="nn", tm=EP_TILE, tn=d, tk=d, name="w_o_ln1",
                   ins=[(x,) + row, (mod,) + per_b, (g,) + whole, (b,) + whole],
                   outs=[(f32,) + row, (f32,) + row, (bf16,) + row], epilogue=epilogue)


def _silu_mul(h):
    bsz, seq, _ = h.shape

    def body(hg_ref, hu_ref, a_ref):
        hg = hg_ref[0].astype(F32)
        a_ref[0] = (hg * _sigmoid(hg) * hu_ref[0].astype(F32)).astype(BF16)

    return _row_call(body, "silu_mul", bsz, seq, [_rows(D_FF, 0), _rows(D_FF, 1)], _rows(D_FF),
                     jax.ShapeDtypeStruct((bsz, seq, D_FF), BF16))(h, h)


def _down_ln2_loss_bwd(a, wd, h1, mod, g, b, target, seq):
    ntok, d = h1.shape
    row, per_b, whole = _ep_specs(seq, d)
    tiles = seq // EP_TILE

    def epilogue(y, i, j, ins, outs):
        h_ref, mod_ref, g_ref, b_ref, t_ref = ins
        dy_ref, dh_ref, acc_ref = outs
        gate = 1.0 + mod_ref[0, 5:6, :]
        z = DEEPNORM_ALPHA * h_ref[...] + gate * y
        xhat, rstd = _ln_stats(z)
        diff = xhat * g_ref[...] + b_ref[...] - t_ref[...]
        loss = 0.5 * jnp.sum(jnp.sum(diff * diff, axis=-1, keepdims=True) / d, axis=0, keepdims=True)
        dout = diff / d
        dz = _ln_bwd(dout * g_ref[...], xhat, rstd)
        dy_ref[...] = (gate * dz).astype(BF16)
        dh_ref[...] = DEEPNORM_ALPHA * dz
        _acc_rows(acc_ref, i % tiles == 0,
                  [_colsum(dout * xhat), _colsum(dout), _colsum(dz * y), jnp.broadcast_to(loss, (1, d))])

    return _matmul(a, wd, mode="nn", tm=EP_TILE, tn=d, tk=a.shape[1], name="down_ln2_loss_bwd",
                   ins=[(h1,) + row, (mod,) + per_b, (g,) + whole, (b,) + whole, (target,) + row],
                   outs=[(jax.ShapeDtypeStruct((ntok, d), BF16),) + row, (jax.ShapeDtypeStruct((ntok, d), F32),) + row,
                         (jax.ShapeDtypeStruct((ntok // seq, 8, d), F32),) + per_b], epilogue=epilogue)


def _silu_mul_bwd(da, h):
    bsz, seq, _ = h.shape

    def body(da_ref, hg_ref, hu_ref, dh_ref):
        hg, da_ = hg_ref[0].astype(F32), da_ref[0].astype(F32)
        sg = _sigmoid(hg)
        dh_ref[0, :, :D_FF] = (da_ * hu_ref[0].astype(F32) * (sg * (1.0 + hg * (1.0 - sg)))).astype(BF16)
        dh_ref[0, :, D_FF:] = (da_ * (hg * sg)).astype(BF16)

    return _row_call(body, "silu_mul_bwd", bsz, seq, [_rows(D_FF), _rows(D_FF, 0), _rows(D_FF, 1)], _rows(2 * D_FF),
                     jax.ShapeDtypeStruct((bsz, seq, 2 * D_FF), BF16))(da, h, h)


def _gate_up_dgrad_ln1_bwd(dh, wgut, dh1a, x, y1, mod, g, b, seq):
    ntok, d = x.shape
    row, per_b, whole = _ep_specs(seq, d)
    tiles = seq // EP_TILE

    def epilogue(du, i, j, ins, outs):
        dh_ref, x_ref, y_ref, mod_ref, g_ref, b_ref = ins
        dy_ref, dx_ref, acc_ref = outs
        y = y_ref[...]
        gate = 1.0 + mod_ref[0, 2:3, :]
        z = DEEPNORM_ALPHA * x_ref[...] + gate * y
        xhat, rstd = _ln_stats(z)
        h1 = xhat * g_ref[...] + b_ref[...]
        dh1 = dh_ref[...] + du * (1.0 + mod_ref[0, 4:5, :])
        dz = _ln_bwd(dh1 * g_ref[...], xhat, rstd)
        dy_ref[...] = (gate * dz).astype(BF16)
        dx_ref[...] = DEEPNORM_ALPHA * dz
        _acc_rows(acc_ref, i % tiles == 0,
                  [_colsum(dh1 * xhat), _colsum(dh1), _colsum(dz * y), _colsum(du * h1), _colsum(du)])

    return _matmul(dh, wgut, mode="nn", tm=EP_TILE, tn=d, tk=D_FF, name="gate_up_dgrad_ln1_bwd",
                   ins=[(dh1a,) + row, (x,) + row, (y1,) + row, (mod,) + per_b, (g,) + whole, (b,) + whole],
                   outs=[(jax.ShapeDtypeStruct((ntok, d), BF16),) + row, (jax.ShapeDtypeStruct((ntok, d), F32),) + row,
                         (jax.ShapeDtypeStruct((ntok // seq, 8, d), F32),) + per_b], epilogue=epilogue)


def _wo_dgrad_gate_bwd(dy1, wo, gab, ya, yb):
    ntok, d = ya.shape
    tm, tn = 1024, 512
    tile = ((tm, tn), lambda i, j: (i, j))
    tile_b = ((tm, tn), lambda i, j: (i, j + d // tn))

    def epilogue(dm_, i, j, ins, outs):
        ga_ref, gb_ref, ya_ref, yb_ref = ins
        dya_ref, dyb_ref, dga_ref, dgb_ref = outs
        sa, sb = _sigmoid(ga_ref[...].astype(F32)), _sigmoid(gb_ref[...].astype(F32))
        dya_ref[...] = (dm_ * sa).astype(BF16)
        dyb_ref[...] = (dm_ * sb).astype(BF16)
        dga_ref[...] = (dm_ * ya_ref[...].astype(F32) * sa * (1.0 - sa)).astype(BF16)
        dgb_ref[...] = (dm_ * yb_ref[...].astype(F32) * sb * (1.0 - sb)).astype(BF16)

    shp = jax.ShapeDtypeStruct((ntok, d), BF16)
    return _matmul(dy1, wo, mode="nt", tm=tm, tn=tn, tk=d, name="w_o_dgrad_gate_bwd",
                   ins=[(gab,) + tile, (gab,) + tile_b, (ya,) + tile, (yb,) + tile],
                   outs=[(shp,) + tile] * 4, epilogue=epilogue)


def _grad_x(dxa, du1, x, mod):
    bsz, seq, d = x.shape

    def body(dxa_ref, du_ref, x_ref, mod_ref, gx_ref, acc_ref):
        du = du_ref[0]
        gx_ref[0] = dxa_ref[0] + du * (1.0 + mod_ref[0, 1:2, :])
        _acc_rows(acc_ref, pl.program_id(1) == 0, [_colsum(du * x_ref[0]), _colsum(du)])

    return _row_call(body, "grad_x", bsz, seq, [_rows(d), _rows(d), _rows(d), _per_batch(8, d)],
                     [_rows(d), _per_batch(8, d)],
                     [jax.ShapeDtypeStruct((bsz, seq, d), F32), jax.ShapeDtypeStruct((bsz, 8, d), F32)],
                     accumulates=True)(dxa, du1, x, mod)


def _segsum64(v):
    rows, width = v.shape
    ri = lax.broadcasted_iota(jnp.int32, (LANES, LANES), 0) // HEAD_DIM
    ci = lax.broadcasted_iota(jnp.int32, (LANES, LANES), 1) // HEAD_DIM
    ones = jnp.where(ri == ci, 1.0, 0.0).astype(BF16)
    out = []
    for c in range(width // LANES):
        part = v[:, c * LANES:(c + 1) * LANES]
        hi = part.astype(BF16)
        lo = (part - hi.astype(F32)).astype(BF16)
        out.append(jnp.dot(hi, ones, preferred_element_type=F32) + jnp.dot(lo, ones, preferred_element_type=F32))
    return jnp.concatenate(out, axis=1) if len(out) > 1 else out[0]


def _merge_b(os_, ls_):
    bsz, seq, w = os_[0].shape

    def body(o0, o1, o2, l0, l1, l2, ob_ref):
        ls = [l0[0], l1[0], l2[0]]
        mx = jnp.maximum(jnp.maximum(ls[0], ls[1]), ls[2])
        es = [jnp.exp(l - mx) for l in ls]
        den = es[0] + es[1] + es[2]
        ob_ref[0] = ((es[0] / den) * o0[0] + (es[1] / den) * o1[0] + (es[2] / den) * o2[0]).astype(BF16)

    return _row_call(body, "merge_b", bsz, seq, [_rows(w)] * 6, _rows(w),
                     jax.ShapeDtypeStruct((bsz, seq, w), BF16))(*os_, *ls_)


def _branch_b_dgrad_merge_bwd(dyb, wbbt, os_, ls_):
    ntok, w = os_[0].shape
    row = ((EP_TILE, w), lambda i, j: (i, 0))

    def epilogue(dob_, i, j, ins, outs):
        os_r, ls_r = ins[:3], ins[3:]
        do_r, dd_r = outs[:3], outs[3:]
        ls = [l[...] for l in ls_r]
        mx = jnp.maximum(jnp.maximum(ls[0], ls[1]), ls[2])
        es = [jnp.exp(l - mx) for l in ls]
        den = es[0] + es[1] + es[2]
        ws = [e / den for e in es]
        dws = [_segsum64(dob_ * o[...]) for o in os_r]
        mean = ws[0] * dws[0] + ws[1] * dws[1] + ws[2] * dws[2]
        for wg, do_ref, dd_ref in zip(ws, do_r, dd_r):
            do_ref[...] = wg * dob_
            dd_ref[...] = -wg * mean

    shp = jax.ShapeDtypeStruct((ntok, w), F32)
    return _matmul(dyb, wbbt, mode="nn", tm=EP_TILE, tn=w, tk=dyb.shape[1], name="branch_b_dgrad_merge_bwd",
                   ins=[(v,) + row for v in list(os_) + list(ls_)], outs=[(shp,) + row] * 6, epilogue=epilogue)


def _branch_a_dgrad_delta(dya, wba, oa, lse_a, sinks_exp, seq):
    ntok, w = oa.shape
    row, per_b, whole = _ep_specs(seq, w)
    tiles = seq // EP_TILE

    def epilogue(do_, i, j, ins, outs):
        o_ref, l_ref, s_ref = ins
        do_ref, dd_ref, acc_ref = outs
        dd = -_segsum64(do_ * o_ref[...])
        do_ref[...] = do_
        dd_ref[...] = dd
        _acc_rows(acc_ref, i % tiles == 0, [_colsum(dd * jnp.exp(s_ref[...] - l_ref[...]))])

    shp = jax.ShapeDtypeStruct((ntok, w), F32)
    return _matmul(dya, wba, mode="nt", tm=EP_TILE, tn=w, tk=dya.shape[1], name="branch_a_dgrad_delta",
                   ins=[(oa,) + row, (lse_a,) + row, (sinks_exp,) + whole],
                   outs=[(shp,) + row, (shp,) + row, (jax.ShapeDtypeStruct((ntok // seq, 8, w), F32),) + per_b],
                   epilogue=epilogue)


def _swap_halves(v):
    src = lax.broadcasted_iota(jnp.int32, (LANES, LANES), 0)
    dst = lax.broadcasted_iota(jnp.int32, (LANES, LANES), 1)
    partner = jnp.where((dst % HEAD_DIM) < HEAD_DIM // 2, dst + HEAD_DIM // 2, dst - HEAD_DIM // 2)
    perm = jnp.where(src == partner, 1.0, 0.0).astype(BF16)
    hi = v.astype(BF16)
    lo = (v - hi.astype(F32)).astype(BF16)
    return jnp.dot(hi, perm, preferred_element_type=F32) + jnp.dot(lo, perm, preferred_element_type=F32)


def _swap_halves_roll(v):
    lane = lax.broadcasted_iota(jnp.int32, v.shape, 1)
    return jnp.where((lane % HEAD_DIM) < HEAD_DIM // 2, pltpu.roll(v, LANES - HEAD_DIM // 2, 1),
                     pltpu.roll(v, HEAD_DIM // 2, 1))


def _rope(v, cos, sin, sign=1.0, mxu=True):
    swap = _swap_halves if mxu else _swap_halves_roll
    out = []
    for c in range(v.shape[1] // LANES):
        part = v[:, c * LANES:(c + 1) * LANES]
        out.append(part * cos + sign * (swap(part) * sin))
    return jnp.concatenate(out, axis=1) if len(out) > 1 else out[0]


def _half_mask(shape, half):
    lane = lax.broadcasted_iota(jnp.int32, shape, len(shape) - 1) % LANES
    return (lane < HEAD_DIM) if half == 0 else (lane >= HEAD_DIM)


def _dup_half(v, half):
    return jnp.where(_half_mask(v.shape, half), v, pltpu.roll(v, HEAD_DIM, 1))


def _fold_halves(v):
    return v + pltpu.roll(v, HEAD_DIM, 1)


def _pick_halves(lo_rows, hi_rows):
    return jnp.where(_half_mask(lo_rows.shape, 0), lo_rows, hi_rows)


def _stack_masked(v, pairs):
    parts = []
    for c in pairs:
        pair = v[:, c * LANES:(c + 1) * LANES]
        parts += [jnp.where(_half_mask(pair.shape, half), pair, 0.0) for half in (0, 1)]
    return jnp.concatenate(parts, axis=0)


def _stack_pair_cols(v, pairs):
    return jnp.concatenate([v[:, c * LANES + half * HEAD_DIM:c * LANES + half * HEAD_DIM + 1] for c in pairs for half in (0, 1)],
                           axis=0)


ATTN_UNITS = 16


def _class_rows(r):
    return [pl.ds(0, QBLOCK)] if r == 1 else [pl.ds(rho, QBLOCK, stride=r) for rho in range(r)]


def _band_mask(nrows, nk, blk, n_back, has_prev):
    qi = lax.broadcasted_iota(jnp.int32, (nrows, nk), 0) % QBLOCK
    ki = lax.broadcasted_iota(jnp.int32, (nrows, nk), 1)
    if has_prev:
        dist = qi + QBLOCK - ki
        return (dist >= 0) & (dist <= n_back) & ((ki >= QBLOCK) | (blk > 0))
    dist = qi - ki
    return (dist >= 0) & (dist <= n_back)


def _attn_fwd(q_arr, k_arr, v_arr, *, name, npair, gqa, q_col, k_col, v_col, nchunk, r, n_back, sinks=None):
    bsz, seq, _ = q_arr.shape
    rr = QBLOCK * r
    nblk = seq // rr
    qw = npair * LANES
    kw = LANES if gqa else qw
    has_prev = nblk > 1
    has_sink = sinks is not None
    scale = HEAD_DIM ** -0.5

    def body(*refs):
        refs = list(refs)
        q_ref, kc_ref, vc_ref = refs[:3]
        pos = 3
        if has_prev:
            kp_ref, vp_ref = refs[pos:pos + 2]
            pos += 2
        if has_sink:
            sink_ref = refs[pos]
            pos += 1
        o_ref, lse_ref = refs[pos:pos + 2]
        blk = pl.program_id(2)
        nk = (2 if has_prev else 1) * QBLOCK
        valid = _band_mask(QBLOCK, nk, blk, n_back, has_prev)
        per = npair // 2
        classes = _class_rows(r)
        step = max(1, ATTN_UNITS // (2 * npair))
        for first in range(0, len(classes), step):
            batch = classes[first:first + step]
            units = []
            for ci, rows in enumerate(batch):
                q = q_ref[0, rows, :] * scale
                k, v = kc_ref[0, rows, :], vc_ref[0, rows, :]
                if has_prev:
                    k = jnp.concatenate([kp_ref[0, rows, :], k], axis=0)
                    v = jnp.concatenate([vp_ref[0, rows, :], v], axis=0)
                if gqa:
                    kdup = [_dup_half(k, hk).astype(BF16) for hk in range(2)]
                    vdup = [_dup_half(v, hk) for hk in range(2)]
                for c in range(npair):
                    sl = slice(c * LANES, (c + 1) * LANES)
                    qc = q[:, sl]
                    kc, vc = (kdup[c // per], vdup[c // per]) if gqa else (k[:, sl].astype(BF16), v[:, sl])
                    for half in (0, 1):
                        qm = jnp.where(_half_mask(qc.shape, half), qc, 0.0).astype(BF16)
                        vm = jnp.where(_half_mask(vc.shape, half), vc, 0.0).astype(BF16)
                        s = lax.dot_general(qm, kc, _DIMS["nt"], preferred_element_type=F32)
                        units.append(dict(ci=ci, c=c, half=half, s=s, vm=vm, sk=sink_ref[2 * c + half] if has_sink else None))
            for u in units:
                s = jnp.where(valid, u["s"], NEG_INF)
                m = jnp.max(s, axis=1, keepdims=True)
                if has_sink:
                    m = jnp.maximum(m, u["sk"])
                p = jnp.exp(s - m)
                den = jnp.sum(p, axis=1, keepdims=True)
                if has_sink:
                    den = den + jnp.exp(u["sk"] - m)
                u.update(p=p.astype(BF16), den=den, lse=m + jnp.log(den))
            for u in units:
                u["o"] = jnp.dot(u["p"], u["vm"], preferred_element_type=F32) / u["den"]
            for ci, rows in enumerate(batch):
                outs, lses = [None] * npair, [None] * npair
                for u in units:
                    if u["ci"] != ci:
                        continue
                    c, o = u["c"], u["o"]
                    lse = jnp.broadcast_to(u["lse"], o.shape)
                    outs[c] = o if u["half"] == 0 else outs[c] + o
                    lses[c] = lse if u["half"] == 0 else _pick_halves(lses[c], lse)
                o_ref[0, rows, :] = jnp.concatenate(outs, axis=1) if npair > 1 else outs[0]
                lse_ref[0, rows, :] = jnp.concatenate(lses, axis=1) if npair > 1 else lses[0]

    def cur(width, col0):
        return pl.BlockSpec((1, rr, width), lambda b, c, i: (b, i, col0 + c))

    def prev(width, col0):
        return pl.BlockSpec((1, rr, width), lambda b, c, i: (b, jnp.maximum(i - 1, 0), col0 + c))

    in_specs = [cur(qw, q_col), cur(kw, k_col), cur(kw, v_col)]
    args = [q_arr, k_arr, v_arr]
    if has_prev:
        in_specs += [prev(kw, k_col), prev(kw, v_col)]
        args += [k_arr, v_arr]
    if has_sink:
        in_specs.append(pl.BlockSpec(memory_space=pltpu.SMEM))
        args.append(sinks)
    return pl.pallas_call(
        body,
        name=name,
        grid=(bsz, nchunk, nblk),
        in_specs=in_specs,
        out_specs=[pl.BlockSpec((1, rr, qw), lambda b, c, i: (b, i, c))] * 2,
        out_shape=[jax.ShapeDtypeStruct((bsz, seq, nchunk * qw), F32)] * 2,
        compiler_params=_params("parallel", "parallel", "parallel"),
    )(*args)


def _attn_bwd(q_arr, k_arr, v_arr, cos, sin, do, lse, dd, *, name, npair, gqa, q_col, k_col, v_col, nchunk, r, n_back,
              token=None):
    bsz, seq, _ = q_arr.shape
    rr = QBLOCK * r
    nblk = seq // rr
    qw = npair * LANES
    kw = LANES if gqa else qw
    has_next = nblk > 1
    has_token = token is not None
    scale = HEAD_DIM ** -0.5

    def body(*refs):
        refs = list(refs)
        k_ref, v_ref, c_ref, s_ref = refs[:4]
        tile_refs = [refs[4:8]]
        pos = 8
        if has_next:
            tile_refs.append(refs[pos:pos + 4])
            pos += 4
        if has_token:
            pos += 1
        dq_ref, dk_ref, dv_ref = refs[pos:pos + 3]
        carry_ref = refs[pos + 3]
        blk = pl.program_id(2)
        if has_next:
            @pl.when(blk == 0)
            def _():
                carry_ref[...] = jnp.zeros_like(carry_ref)

        nrows = (npair if gqa else 1) * QBLOCK
        qi = lax.broadcasted_iota(jnp.int32, (nrows, QBLOCK), 0) % QBLOCK
        ki = lax.broadcasted_iota(jnp.int32, (nrows, QBLOCK), 1)
        valids = [qi >= ki, (qi + QBLOCK - ki <= n_back) & (blk + 1 < nblk)]
        per = npair // 2
        ntile = len(tile_refs)
        cat = lambda parts: jnp.concatenate(parts, axis=1) if len(parts) > 1 else parts[0]
        classes = _class_rows(r)
        step = max(1, ATTN_UNITS // (ntile * (2 if gqa else 2 * npair)))
        for first in range(0, len(classes), step):
            batch = classes[first:first + step]
            units = []
            for ci, rows in enumerate(batch):
                tiles = [(q_ref[0, rows, :] * scale, do_ref[0, rows, :], l_ref[0, rows, :], d_ref[0, rows, :])
                         for q_ref, do_ref, l_ref, d_ref in tile_refs]
                k, v = k_ref[0, rows, :], v_ref[0, rows, :]
                if gqa:
                    for hk in range(2):
                        pairs = list(range(hk * per, (hk + 1) * per))
                        kd, vd = _dup_half(k, hk).astype(BF16), _dup_half(v, hk).astype(BF16)
                        for t, (q, do_, l_, d_) in enumerate(tiles):
                            units.append(dict(ci=ci, t=t, hk=hk, pairs=pairs, qs=_stack_masked(q, pairs).astype(BF16),
                                              dos=_stack_masked(do_, pairs).astype(BF16), lcol=_stack_pair_cols(l_, pairs),
                                              dcol=_stack_pair_cols(d_, pairs), kmat=kd, vmat=vd, kdq=kd))
                else:
                    for c in range(npair):
                        sl = slice(c * LANES, (c + 1) * LANES)
                        kc, vcb = k[:, sl], v[:, sl].astype(BF16)
                        kcb = kc.astype(BF16)
                        for t, (q, do_, l_, d_) in enumerate(tiles):
                            for half in (0, 1):
                                hm = _half_mask(kc.shape, half)
                                col = c * LANES + half * HEAD_DIM
                                units.append(dict(ci=ci, t=t, c=c, half=half, qs=jnp.where(hm, q[:, sl], 0.0).astype(BF16),
                                                  dos=jnp.where(hm, do_[:, sl], 0.0).astype(BF16), lcol=l_[:, col:col + 1],
                                                  dcol=d_[:, col:col + 1], kmat=kcb, vmat=vcb,
                                                  kdq=jnp.where(hm, kc, 0.0).astype(BF16)))
            for u in units:
                u["s"] = lax.dot_general(u["qs"], u["kmat"], _DIMS["nt"], preferred_element_type=F32)
                u["dp"] = lax.dot_general(u["dos"], u["vmat"], _DIMS["nt"], preferred_element_type=F32)
            for u in units:
                p = jnp.exp(jnp.where(valids[u["t"]], u["s"], NEG_INF) - u["lcol"])
                u["ds"] = (p * (u["dp"] + u["dcol"])).astype(BF16)
                u["p"] = p.astype(BF16)
            for u in units:
                u["dv"] = lax.dot_general(u["p"], u["dos"], _DIMS["tn"], preferred_element_type=F32)
                u["dk"] = lax.dot_general(u["ds"], u["qs"], _DIMS["tn"], preferred_element_type=F32)
                u["dq"] = jnp.dot(u["ds"], u["kdq"], preferred_element_type=F32) * scale
            for ci, rows in enumerate(batch):
                mine = [u for u in units if u["ci"] == ci]
                dq = [[None] * npair for _ in range(ntile)]
                if gqa:
                    dk_out = dv_out = None
                    for hk in range(2):
                        us = [u for u in mine if u["hk"] == hk]
                        for u in us:
                            for i, c in enumerate(u["pairs"]):
                                dq[u["t"]][c] = _pick_halves(u["dq"][2 * i * QBLOCK:(2 * i + 1) * QBLOCK],
                                                             u["dq"][(2 * i + 1) * QBLOCK:(2 * i + 2) * QBLOCK])
                        dk_h = _fold_halves(functools.reduce(jnp.add, [u["dk"] for u in us]))
                        dv_h = _fold_halves(functools.reduce(jnp.add, [u["dv"] for u in us]))
                        dk_out = dk_h if hk == 0 else _pick_halves(dk_out, dk_h)
                        dv_out = dv_h if hk == 0 else _pick_halves(dv_out, dv_h)
                else:
                    dks, dvs = [], []
                    for c in range(npair):
                        us = [u for u in mine if u["c"] == c]
                        dks.append(functools.reduce(jnp.add, [u["dk"] for u in us]))
                        dvs.append(functools.reduce(jnp.add, [u["dv"] for u in us]))
                        for t in range(ntile):
                            dq[t][c] = functools.reduce(jnp.add, [u["dq"] for u in us if u["t"] == t])
                    dk_out, dv_out = cat(dks), cat(dvs)
                ck, sk_ = c_ref[0, rows, :], s_ref[0, rows, :]
                dk_ref[0, rows, :] = _rope(dk_out, ck, sk_, sign=-1.0, mxu=gqa)
                dv_ref[0, rows, :] = dv_out
                dq_cur = cat(dq[0])
                if has_next:
                    dq_cur = dq_cur + carry_ref[rows, :]
                    carry_ref[rows, :] = cat(dq[1])
                dq_ref[0, rows, :] = _rope(dq_cur, ck, sk_, sign=-1.0, mxu=gqa)

    def at(width, col0, shift):
        return pl.BlockSpec((1, rr, width), lambda b, c, i: (b, jnp.minimum(i + shift, nblk - 1), col0 + c))

    in_specs = [at(kw, k_col, 0), at(kw, v_col, 0), pl.BlockSpec((1, rr, LANES), lambda b, c, i: (b, i, 0)),
                pl.BlockSpec((1, rr, LANES), lambda b, c, i: (b, i, 0))]
    args = [k_arr, v_arr, cos, sin]
    for shift in (0, 1) if has_next else (0,):
        in_specs += [at(qw, q_col, shift), at(qw, 0, shift), at(qw, 0, shift), at(qw, 0, shift)]
        args += [q_arr, do, lse, dd]
    if has_token:
        in_specs.append(pl.BlockSpec(token.shape, lambda b, c, i: (0, 0)))
        args.append(token)
    return pl.pallas_call(
        body,
        name=name,
        grid=(bsz, nchunk, nblk),
        in_specs=in_specs,
        out_specs=[pl.BlockSpec((1, rr, qw), lambda b, c, i: (b, i, c)),
                   pl.BlockSpec((1, rr, kw), lambda b, c, i: (b, i, c)),
                   pl.BlockSpec((1, rr, kw), lambda b, c, i: (b, i, c))],
        out_shape=[jax.ShapeDtypeStruct((bsz, seq, nchunk * qw), F32),
                   jax.ShapeDtypeStruct((bsz, seq, nchunk * kw), F32),
                   jax.ShapeDtypeStruct((bsz, seq, nchunk * kw), F32)],
        scratch_shapes=[pltpu.VMEM((rr, qw) if has_next else (8, LANES), F32)],
        compiler_params=_params("parallel", "parallel", "arbitrary"),
    )(*args)


B_CHUNKS = {1: (4, 1), 4: (1, 4), 16: (1, 4)}


def _rope_tables(positions):
    half = HEAD_DIM // 2
    inv = ROPE_THETA ** (-jnp.arange(half, dtype=F32) / half)
    ang = positions.astype(F32)[..., None] * inv
    cos, sin = jnp.cos(ang), jnp.sin(ang)
    return jnp.concatenate([cos] * 4, axis=-1), jnp.concatenate([-sin, sin, -sin, sin], axis=-1)


def _layer_step(x, mod, positions, sinks, ln1_g, ln1_b, ln2_g, ln2_b, target, get_w_in, get_rest, hook):
    bsz, seq, d = x.shape
    ntok = bsz * seq
    flat = lambda v: v.reshape(ntok, v.shape[-1])
    unflat = lambda v: v.reshape(bsz, seq, v.shape[-1])
    cos, sin = _rope_tables(positions)
    mm = functools.partial(_matmul, tm=1024, tk=1024)
    scalar = lambda tok: 0.0 if tok is None else tok[0, 0]

    u1 = _modulate_in(x, mod)
    u1f = flat(u1)
    wint = get_w_in(u1)
    cosf, sinf = flat(cos), flat(sin)
    proj = functools.partial(_proj_rope, u1f, wint, cosf, sinf, tm=2048)
    qa = unflat(proj(n=1024, b_off=OFF_QA, rope_cols=1024, tn=512, name="proj_qa"))
    kva = unflat(proj(n=256, b_off=OFF_KVA, rope_cols=128, tn=128, name="proj_kva"))
    qkvb = unflat(proj(n=4608, b_off=OFF_QKVB, rope_cols=3072, tn=256, name="proj_qkvb"))
    gab = unflat(proj(n=2048, b_off=OFF_GAB, rope_cols=0, tn=256, name="proj_gab", out_dtype=BF16))

    sink_vec = sinks.reshape(A_Q_HEADS) + scalar(hook("projected", gab))
    a_kw = dict(npair=A_Q_HEADS // 2, gqa=True, q_col=0, k_col=0, v_col=1, nchunk=1, r=1, n_back=A_WINDOW - 1)
    oa, lse_a = _attn_fwd(qa, kva, kva, name="attn_a_fwd", sinks=sink_vec, **a_kw)
    rest = get_rest(oa)
    wba, wbbt, wo, wgut, wd = (rest[n] for n in ("w_branch_a", "w_branch_b", "w_o", "w_gate_up", "w_down"))
    ya = unflat(mm(flat(oa), wba, mode="nn", out_dtype=BF16, tn=512, name="branch_a"))

    b_kws, os_, ls_ = [], [], []
    for g, (window, r) in enumerate(B_PATTERNS):
        npair, nch = B_CHUNKS[r]
        per = B_HEADS_PER_GROUP // (2 * npair)
        nsec = len(B_PATTERNS) * per
        kw_ = dict(npair=npair, gqa=False, q_col=g * per, k_col=nsec + g * per, v_col=2 * nsec + g * per, nchunk=nch, r=r,
                   n_back=window // r)
        b_kws.append(kw_)
        o_g, l_g = _attn_fwd(qkvb, qkvb, qkvb, name=f"attn_b{g}_fwd", **kw_)
        os_.append(o_g)
        ls_.append(l_g)
    ob = _merge_b(os_, ls_)
    yb = unflat(mm(flat(ob), wbbt, mode="nt", out_dtype=BF16, tk=512, tn=512, name="branch_b"))
    merged = _gate_merge(gab, ya, yb)
    xf = flat(x)
    y1f, h1f, u2f = _wo_ln1(flat(merged), wo, xf, mod, ln1_g, ln1_b, seq)
    h = unflat(mm(u2f, wgut, mode="nt", out_dtype=BF16, tn=D_FF // 2, name="gate_up"))
    a = _silu_mul(h)

    dy2f, dh1af, acc2 = _down_ln2_loss_bwd(flat(a), wd, h1f, mod, ln2_g, ln2_b, flat(target), seq)
    da = unflat(mm(dy2f, wd, mode="nt", out_dtype=BF16, tn=D_FF // 2, name="down_dgrad"))
    g_wd = _matmul(flat(a), dy2f, mode="tn", out_dtype=BF16, tm=256, tn=1024, tk=ntok, name="down_wgrad")
    dh = _silu_mul_bwd(da, h)
    dhf = flat(dh)
    g_wgut = _matmul(dhf, u2f, mode="tn", out_dtype=BF16, tm=256, tn=1024, tk=ntok, name="gate_up_wgrad")
    dy1f, dxaf, acc1 = _gate_up_dgrad_ln1_bwd(dhf, wgut, dh1af, xf, y1f, mod, ln1_g, ln1_b, seq)
    g_wo = _matmul(flat(merged), dy1f, mode="tn", out_dtype=BF16, tm=256, tn=1024, tk=ntok, name="w_o_wgrad")
    dyaf, dybf, dgaf, dgbf = _wo_dgrad_gate_bwd(dy1f, wo, flat(gab), flat(ya), flat(yb))
    g_wba = _matmul(flat(oa), dyaf, mode="tn", out_dtype=BF16, tm=256, tn=1024, tk=ntok, name="branch_a_wgrad")
    g_wbbt = _matmul(dybf, flat(ob), mode="tn", out_dtype=BF16, tm=256, tn=512, tk=ntok, name="branch_b_wgrad")
    tok = hook("grads_rest", dict(w_branch_a=g_wba, w_branch_b=g_wbbt, w_o=g_wo, w_gate_up=g_wgut, w_down=g_wd))

    sinks_exp = jnp.repeat(sinks.reshape(1, A_Q_HEADS), HEAD_DIM, axis=1) + scalar(tok)
    doa, dd_a, acc_s = _branch_a_dgrad_delta(dyaf, wba, flat(oa), flat(lse_a), sinks_exp, seq)
    doa, dd_a = unflat(doa), unflat(dd_a)
    tok = hook("delta_done", dd_a)
    dqa, dka, dva = _attn_bwd(qa, kva, kva, cos, sin, doa, lse_a, dd_a, name="attn_a_bwd", token=tok, **a_kw)
    merged_bwd = [unflat(t) for t in _branch_b_dgrad_merge_bwd(dybf, wbbt, [flat(t) for t in os_], [flat(t) for t in ls_])]
    dqs, dks, dvs = [], [], []
    for g in range(len(B_PATTERNS)):
        dq_g, dk_g, dv_g = _attn_bwd(qkvb, qkvb, qkvb, cos, sin, merged_bwd[g], ls_[g], merged_bwd[3 + g],
                                     name=f"attn_b{g}_bwd", **b_kws[g])
        dqs.append(dq_g)
        dks.append(dk_g)
        dvs.append(dv_g)
    dproj = jnp.concatenate([t.astype(BF16) for t in [dqa, dka, dva] + dqs + dks + dvs] + [unflat(dgaf), unflat(dgbf)], axis=-1)
    dprojf = flat(dproj)
    g_wint = _matmul(dprojf, u1f, mode="tn", out_dtype=BF16, tm=256, tn=1024, tk=ntok, name="w_in_wgrad")
    tok = hook("grads_w_in", dict(w_in=g_wint))
    du1 = unflat(_matmul(dprojf, wint, mode="nn", out_dtype=F32, tm=1024, tn=512, tk=wint.shape[0] // 2, name="w_in_dgrad",
                         token=tok))
    tok = hook("dgrad_done", du1)
    grad_x, acc0 = _grad_x(unflat(dxaf), du1, x, mod + scalar(tok))

    loss_part = jnp.sum(acc2[:, 3, 0])
    dmod = jnp.stack([acc0[:, 1], acc0[:, 0], acc1[:, 2], acc1[:, 4], acc1[:, 3], acc2[:, 2]], axis=1)
    small = jnp.stack([acc1[:, 0].sum(0), acc1[:, 1].sum(0), acc2[:, 0].sum(0), acc2[:, 1].sum(0), acc_s[:, 0].sum(0)])
    return loss_part, grad_x, dmod, small


CHIP_FLIPS = (2, 4, 6)


def _my_place():
    return lax.axis_index("x"), lax.axis_index("y"), lax.axis_index("c")


def _flip(place, k):
    px, py, pc = place
    return (1 - px if k & 4 else px, 1 - py if k & 2 else py, 1 - pc if k & 1 else pc)


def _index(place):
    return 4 * place[0] + 2 * place[1] + place[2]


def _gather_small(v, name):
    rows, cols = v.shape

    def body(v_ref, out_ref, send_sems, recv_sems):
        me = _my_place()
        out_ref[_index(me)] = v_ref[...]
        copies = []
        for k in range(1, N_DEV):
            copies.append(pltpu.make_async_remote_copy(
                src_ref=v_ref, dst_ref=out_ref.at[_index(me)], send_sem=send_sems.at[k - 1], recv_sem=recv_sems.at[k - 1],
                device_id=_flip(me, k), device_id_type=MESH))
        for cp in copies:
            cp.start()
        for k in range(1, N_DEV):
            pltpu.make_async_remote_copy(
                src_ref=v_ref, dst_ref=out_ref.at[_index(_flip(me, k))], send_sem=send_sems.at[k - 1],
                recv_sem=recv_sems.at[k - 1], device_id=_flip(me, k), device_id_type=MESH).wait_recv()
        for cp in copies:
            cp.wait_send()

    return pl.pallas_call(
        body,
        name=name,
        out_shape=jax.ShapeDtypeStruct((N_DEV, rows, cols), v.dtype),
        in_specs=[pl.BlockSpec(memory_space=pltpu.VMEM)],
        out_specs=pl.BlockSpec(memory_space=pltpu.VMEM),
        scratch_shapes=[pltpu.SemaphoreType.DMA((N_DEV - 1,)), pltpu.SemaphoreType.DMA((N_DEV - 1,))],
        compiler_params=pltpu.CompilerParams(vmem_limit_bytes=VMEM_LIMIT_BYTES),
    )(v)


_HBM = pl.BlockSpec(memory_space=pltpu.HBM)
_SEM = pl.BlockSpec(memory_space=pltpu.SEMAPHORE)
_EFFECT = pltpu.SideEffectType.DATAFLOW_SIDE_EFFECTING


def _remote(src, dst, send_sems, recv_sems, j, to):
    return pltpu.make_async_remote_copy(src_ref=src, dst_ref=dst, send_sem=send_sems.at[j], recv_sem=recv_sems.at[j],
                                        device_id=to, device_id_type=MESH)


def _copies_start(name, bufs, make_copies, nsem):
    nbuf = len(bufs)

    def body(*refs):
        for cp in make_copies(refs[:nbuf], refs[nbuf], refs[nbuf + 1]):
            cp.start()
        refs[-1][...] = jnp.zeros_like(refs[-1])

    sems = pltpu.SemaphoreType.DMA((nsem,))
    res = pl.pallas_call(
        body, name=name,
        out_shape=(sems, sems, *[pltpu.HBM(v.shape, v.dtype) for v in bufs], jax.ShapeDtypeStruct((8, LANES), F32)),
        in_specs=(_HBM,) * nbuf, out_specs=(_SEM, _SEM) + (_HBM,) * nbuf + (pl.BlockSpec(memory_space=pltpu.VMEM),),
        input_output_aliases={i: 2 + i for i in range(nbuf)},
        compiler_params=pltpu.CompilerParams(has_side_effects=_EFFECT),
    )(*[pltpu.with_memory_space_constraint(v, pltpu.HBM) for v in bufs])
    return res[0], res[1], list(res[2:2 + nbuf]), res[-1]


def _copies_wait(name, started, make_copies, after):
    send_sems, recv_sems, bufs, _ = started
    nbuf = len(bufs)

    def body(*refs):
        for cp in make_copies(refs[:nbuf], refs[nbuf], refs[nbuf + 1]):
            cp.wait_send()
            cp.wait_recv()

    return list(pl.pallas_call(
        body, name=name,
        out_shape=tuple(pltpu.HBM(v.shape, v.dtype) for v in bufs),
        in_specs=(_HBM,) * nbuf + (_SEM, _SEM, pl.BlockSpec(memory_space=pl.ANY)), out_specs=(_HBM,) * nbuf,
        input_output_aliases={i: i for i in range(nbuf)},
        compiler_params=pltpu.CompilerParams(has_side_effects=_EFFECT),
    )(*bufs, send_sems, recv_sems, after))


def _to_sibling_copies(refs, send_sems, recv_sems):
    src_ref, land_ref = refs
    me = _my_place()
    return [_remote(src_ref.at[q, 1 - me[2]], land_ref.at[q], send_sems, recv_sems, q, _flip(me, 1)) for q in range(4)]


def _to_chips_copies(refs, send_sems, recv_sems):
    src_ref, land_ref = refs
    me = _my_place()
    copies = []
    for j, k in enumerate(CHIP_FLIPS):
        to = _flip(me, k)
        copies.append(_remote(src_ref.at[2 * to[0] + to[1]], land_ref.at[j], send_sems, recv_sems, j, to))
    return copies


class _Gather:
    def __init__(self, name, blocks):
        self.name, self.n = name, len(blocks)
        at_me = (_index(_my_place()), 0, 0)
        lands = [lax.dynamic_update_slice(lax.empty((N_DEV,) + v.shape, v.dtype), v[None], at_me) for v in blocks]
        self.first = _copies_start(name + "_start", list(blocks) + lands, self._first_copies, 4 * self.n)
        self.token = self.first[3]

    def _first_copies(self, refs, send_sems, recv_sems):
        me = _my_place()
        return [_remote(refs[w], refs[self.n + w].at[_index(me)], send_sems, recv_sems, 4 * w + j, _flip(me, k))
                for w in range(self.n) for j, k in enumerate((1,) + CHIP_FLIPS)]

    def _pass_copies(self, refs, send_sems, recv_sems):
        me = _my_place()
        copies = []
        for w, land in enumerate(refs):
            for j, k in enumerate(CHIP_FLIPS):
                slot = land.at[_index(_flip(me, k))]
                copies.append(_remote(slot, slot, send_sems, recv_sems, 3 * w + j, _flip(me, 1)))
        return copies

    def pass_on(self, after):
        lands = _copies_wait(self.name + "_wait", self.first, self._first_copies, after)[self.n:]
        self.second = _copies_start(self.name + "_pass_start", lands, self._pass_copies, 3 * self.n)
        return self.second[3]

    def finish(self, after):
        return _copies_wait(self.name + "_pass_wait", self.second, self._pass_copies, after)


SUM_SPLIT = 2


def _sum_pairs(parts, theirs):
    nchip, _, rows, cols = parts.shape
    tile = rows // SUM_SPLIT

    def body(c_ref, a_ref, b_ref, o_ref):
        o_ref[...] = (a_ref[0].astype(F32) + b_ref[...].astype(F32)).astype(BF16)

    spec = pl.BlockSpec((1, tile, cols), lambda q, t, c_ref: (q, t, 0))
    grid_spec = pltpu.PrefetchScalarGridSpec(
        num_scalar_prefetch=1, grid=(nchip, SUM_SPLIT),
        in_specs=[pl.BlockSpec((1, 1, tile, cols), lambda q, t, c_ref: (q, c_ref[0], t, 0)), spec], out_specs=spec)
    return pl.pallas_call(body, name="grad_sum_sibling", grid_spec=grid_spec,
                          out_shape=jax.ShapeDtypeStruct((nchip, rows, cols), BF16),
                          compiler_params=_params("parallel", "parallel"))(lax.axis_index("c").reshape(1), parts, theirs)


def _sum_final(chip_sum, got):
    _, rows, cols = chip_sum.shape
    tile = rows // SUM_SPLIT

    def body(q_ref, a_ref, g_ref, o_ref):
        o_ref[...] = ((a_ref[0].astype(F32) + g_ref[0].astype(F32)) + g_ref[1].astype(F32)) + g_ref[2].astype(F32)

    grid_spec = pltpu.PrefetchScalarGridSpec(
        num_scalar_prefetch=1, grid=(SUM_SPLIT,),
        in_specs=[pl.BlockSpec((1, tile, cols), lambda t, q_ref: (q_ref[0], t, 0)),
                  pl.BlockSpec((3, tile, cols), lambda t, q_ref: (0, t, 0))],
        out_specs=pl.BlockSpec((tile, cols), lambda t, q_ref: (t, 0)))
    my_chip = (2 * lax.axis_index("x") + lax.axis_index("y")).reshape(1)
    return pl.pallas_call(body, name="grad_sum_chips", grid_spec=grid_spec, out_shape=jax.ShapeDtypeStruct((rows, cols), F32),
                          compiler_params=_params("parallel"))(my_chip, chip_sum, got)


class _ReduceScatter:
    def __init__(self, name, slabs):
        self.name, self.rows = name, slabs.shape[1]
        parts = slabs.reshape(4, 2, self.rows, D_MODEL)
        self.first = _copies_start(name + "_sibling_start", [parts, lax.empty((4, self.rows, D_MODEL), slabs.dtype)],
                                   _to_sibling_copies, 4)
        self.token = self.first[3]

    def between_chips(self, after):
        parts, theirs = _copies_wait(self.name + "_sibling_wait", self.first, _to_sibling_copies, after)
        chip_sum = _sum_pairs(parts, theirs)
        self.second = _copies_start(self.name + "_chips_start", [chip_sum, lax.empty((3, self.rows, D_MODEL), chip_sum.dtype)],
                                    _to_chips_copies, 3)
        return self.second[3]

    def finish(self, after):
        chip_sum, got = _copies_wait(self.name + "_chips_wait", self.second, _to_chips_copies, after)
        return _sum_final(chip_sum, got)


def _ada_fwd(c_all, w, b):
    nb, _ = c_all.shape
    ncol = w.shape[1]

    def body(c_ref, w_ref, b_ref, o_ref):
        c = c_ref[...]
        act = (c * _sigmoid(c)).astype(BF16)
        o_ref[...] = jnp.dot(act, w_ref[...].astype(BF16), preferred_element_type=F32) + b_ref[...]

    return pl.pallas_call(body, name="ada_fwd", out_shape=jax.ShapeDtypeStruct((nb, ncol), F32),
                          compiler_params=pltpu.CompilerParams(vmem_limit_bytes=VMEM_LIMIT_BYTES))(c_all, w, b)


def _ada_wgrad(c_all_t, dmod_cols):
    d, nb = c_all_t.shape
    ncol = dmod_cols.shape[1]

    def body(ct_ref, dm_ref, o_ref):
        ct = ct_ref[...]
        act = (ct * _sigmoid(ct)).astype(BF16).astype(F32)
        dm = dm_ref[...].astype(BF16).astype(F32)
        acc = act[:, 0:1] * dm[0:1, :]
        for i in range(1, nb):
            acc = acc + act[:, i:i + 1] * dm[i:i + 1, :]
        o_ref[...] = acc

    return pl.pallas_call(body, name="ada_wgrad", out_shape=jax.ShapeDtypeStruct((d, ncol), F32),
                          compiler_params=pltpu.CompilerParams(vmem_limit_bytes=VMEM_LIMIT_BYTES))(c_all_t, dmod_cols)


SMALL_ROWS = 24


def _reduce_small(gathered):
    def body(g_ref, o_ref):
        acc = g_ref[0]
        for dev in range(1, N_DEV):
            acc = acc + g_ref[dev]
        o_ref[...] = acc

    return pl.pallas_call(body, name="reduce_small", out_shape=jax.ShapeDtypeStruct(gathered.shape[1:], F32))(gathered)


def _adamw(w, g, m, v, name):
    rows, cols = w.shape
    tile = rows
    for cand in (256, 128, 64, 32, 16, 8):
        if rows % cand == 0 and rows > cand:
            tile = cand
            break
    spec = pl.BlockSpec((tile, cols), lambda t: (t, 0))
    bc1 = 1.0 - ADAM_B1 ** ADAM_STEP
    bc2 = 1.0 - ADAM_B2 ** ADAM_STEP

    def body(w_ref, g_ref, m_ref, v_ref, d_ref, nm_ref, nv_ref):
        g_ = g_ref[...]
        nm = ADAM_B1 * m_ref[...] + (1.0 - ADAM_B1) * g_
        nv = ADAM_B2 * v_ref[...] + (1.0 - ADAM_B2) * (g_ * g_)
        d_ref[...] = -ADAM_LR * ((nm / bc1) / (jnp.sqrt(nv / bc2) + ADAM_EPS) + ADAM_WD * w_ref[...])
        nm_ref[...] = nm
        nv_ref[...] = nv

    shp = jax.ShapeDtypeStruct((rows, cols), F32)
    return pl.pallas_call(body, name=name, grid=(rows // tile,), in_specs=[spec] * 4, out_specs=[spec] * 3, out_shape=[shp] * 3,
                          compiler_params=_params("parallel"))(w, g, m, v)


_WEIGHTS = ("w_ada", "b_ada", "w_in", "sinks", "w_branch_a", "w_branch_b", "w_o", "ln1_g", "ln1_b", "w_gate_up", "w_down",
            "ln2_g", "ln2_b")
_TRANSPOSED = ("w_in", "w_branch_b", "w_gate_up")


def _pack_shard(name, w):
    w = w.astype(BF16)
    if name in _TRANSPOSED:
        w = w.T
    return w.reshape(-1, D_MODEL)


def _unpack_full(name, slab):
    if name == "w_branch_b":
        return slab.reshape(N_DEV * 128, 512)
    return slab.reshape(-1, D_MODEL)


def _unpack_group(group, gathered):
    return {n: _unpack_full(n, slab) for (n, _), slab in zip(group, gathered)}


def _unpack_grads(group, g_packed):
    g_w, off = {}, 0
    for n, r in group:
        part = g_packed[off:off + r]
        off += r
        g_w[n] = part.reshape(128, 512) if n == "w_branch_b" else part
    return g_w


def kernel(x, c, positions, w_ada, b_ada, w_in, sinks, w_branch_a, w_branch_b, w_o, ln1_g, ln1_b, w_gate_up, w_down, ln2_g, ln2_b, loss_target, m_w_ada, m_b_ada, m_w_in, m_sinks, m_w_branch_a, m_w_branch_b, m_w_o, m_ln1_g, m_ln1_b, m_w_gate_up, m_w_down, m_ln2_g, m_ln2_b, v_w_ada, v_b_ada, v_w_in, v_sinks, v_w_branch_a, v_w_branch_b, v_w_o, v_ln1_g, v_ln1_b, v_w_gate_up, v_w_down, v_ln2_g, v_ln2_b):
    weights = dict(w_ada=w_ada, b_ada=b_ada, w_in=w_in, sinks=sinks, w_branch_a=w_branch_a, w_branch_b=w_branch_b, w_o=w_o,
                   ln1_g=ln1_g, ln1_b=ln1_b, w_gate_up=w_gate_up, w_down=w_down, ln2_g=ln2_g, ln2_b=ln2_b)
    m_in = dict(w_ada=m_w_ada, b_ada=m_b_ada, w_in=m_w_in, sinks=m_sinks, w_branch_a=m_w_branch_a, w_branch_b=m_w_branch_b,
                w_o=m_w_o, ln1_g=m_ln1_g, ln1_b=m_ln1_b, w_gate_up=m_w_gate_up, w_down=m_w_down, ln2_g=m_ln2_g, ln2_b=m_ln2_b)
    v_in = dict(w_ada=v_w_ada, b_ada=v_b_ada, w_in=v_w_in, sinks=v_sinks, w_branch_a=v_w_branch_a, w_branch_b=v_w_branch_b,
                w_o=v_w_o, ln1_g=v_ln1_g, ln1_b=v_ln1_b, w_gate_up=v_w_gate_up, w_down=v_w_down, ln2_g=v_ln2_g, ln2_b=v_ln2_b)
    bsz = x.shape[0]
    me = _index(_my_place())
    ada_cols = w_ada.shape[2]
    outs = {}

    def adamw(n, g):
        w2, m2, v2 = (t[n][0] if t[n].ndim == 3 else t[n] for t in (weights, m_in, v_in))
        shape = weights[n].shape
        if n in _TRANSPOSED:
            dlt, nm, nv = _adamw(w2.T, g, m2.T, v2.T, "adamw_" + n)
            outs[n] = tuple(t.T.reshape(shape) for t in (g, dlt, nm, nv))
        else:
            dlt, nm, nv = _adamw(w2, g, m2, v2, "adamw_" + n)
            outs[n] = tuple(t.reshape(shape) for t in (g, dlt, nm, nv))
        return nv

    packed_in = [_pack_shard(n, weights[n][0]) for n, _ in GROUP_IN]
    packed_rest = [_pack_shard(n, weights[n][0]) for n, _ in GROUP_REST]
    c_all = _gather_small(jnp.pad(c, ((0, 8 - bsz), (0, 0))), "gather_c")[:, :bsz].reshape(N_DEV * bsz, D_MODEL)
    gather_in = _Gather("gather_w_in", lax.optimization_barrier((packed_in, c_all))[0])
    b_cols = lax.dynamic_slice_in_dim(b_ada, me * ada_cols, ada_cols, axis=1)
    mod_cols = _ada_fwd(c_all, w_ada[0], b_cols + gather_in.token[0, 0])
    mod_all = _gather_small(mod_cols, "gather_mod").transpose(1, 0, 2).reshape(N_DEV * bsz, 6, D_MODEL)
    gather_rest = _Gather("gather_rest", lax.optimization_barrier((packed_rest, mod_all))[0])
    mod = jnp.pad(lax.dynamic_slice_in_dim(mod_all, me * bsz, bsz, axis=0), ((0, 0), (0, 2), (0, 0)))
    mod = mod + gather_rest.token[0, 0]
    mod = mod + gather_in.pass_on(mod)[0, 0]

    scatters = {}

    def get_w_in(after):
        return _unpack_group(GROUP_IN, gather_in.finish(after))["w_in"]

    def get_rest(after):
        return _unpack_group(GROUP_REST, gather_rest.finish(after))

    def pack_grads(group, grads):
        return jnp.concatenate([grads[n].reshape(N_DEV, r, D_MODEL) for n, r in group], axis=1)

    def hook(point, value):
        if point == "projected":
            return gather_rest.pass_on(value)
        if point == "grads_rest":
            scatters["rest"] = _ReduceScatter("scatter_rest", pack_grads(GROUP_REST, value))
            return scatters["rest"].token
        if point == "delta_done":
            return scatters["rest"].between_chips(value)
        if point == "grads_w_in":
            scatters["in"] = _ReduceScatter("scatter_w_in", pack_grads(GROUP_IN, value))
            return scatters["in"].token
        if point == "dgrad_done":
            tok = scatters["in"].between_chips(value)
            for n, g in _unpack_grads(GROUP_REST, scatters["rest"].finish(tok)).items():
                adamw(n, g)
            return tok
        raise ValueError(point)

    loss_part, grad_x, dmod, small = _layer_step(x, mod, positions, sinks[0], ln1_g, ln1_b, ln2_g, ln2_b, loss_target,
                                                 get_w_in, get_rest, hook)

    rows = jnp.concatenate([dmod.reshape(bsz * 6, D_MODEL), small, jnp.full((1, D_MODEL), loss_part, F32),
                            jnp.zeros((SMALL_ROWS - bsz * 6 - 6, D_MODEL), F32)], axis=0)
    small_all = _gather_small(rows, "gather_small")
    sums = _reduce_small(small_all)
    loss = sums[bsz * 6 + 5, 0]
    dmod_all = small_all[:, :bsz * 6].reshape(N_DEV * bsz, 6 * D_MODEL)
    adamw("b_ada", functools.reduce(jnp.add, [sums[6 * i:6 * i + 6] for i in range(bsz)]).reshape(1, 6 * D_MODEL))
    for i, n in enumerate(("ln1_g", "ln1_b", "ln2_g", "ln2_b")):
        adamw(n, sums[12 + i][None])
    adamw("sinks", sums[16][::HEAD_DIM][None])
    dmod_cols = lax.dynamic_slice_in_dim(dmod_all, me * ada_cols, ada_cols, axis=1)
    last = adamw("w_ada", _ada_wgrad(c_all.T, dmod_cols))
    for n, g in _unpack_grads(GROUP_IN, scatters["in"].finish(last)).items():
        adamw(n, g)

    return (loss, grad_x, *[outs[n][0] for n in _WEIGHTS], *[outs[n][1] for n in _WEIGHTS], *[outs[n][2] for n in _WEIGHTS],
            *[outs[n][3] for n in _WEIGHTS])
```

```python
import functools

import jax
import jax.numpy as jnp
from jax import lax
from jax.experimental import pallas as pl
from jax.experimental.pallas import tpu as pltpu

F32 = jnp.float32
BF16 = jnp.bfloat16

D_MODEL = 1024
HEAD_DIM = 64
A_Q_HEADS = 16
A_WINDOW = 128
B_PATTERNS = ((128, 1), (512, 4), (2048, 16))
B_HEADS_PER_GROUP = 8
D_FF = 2816
QBLOCK = 128
ROPE_THETA = 10000.0
LN_EPS = 1e-5
DEEPNORM_ALPHA = 2.0 ** 0.25
NEG_INF = -1e30
ADAM_LR, ADAM_B1, ADAM_B2, ADAM_EPS, ADAM_WD, ADAM_STEP = 0.001, 0.9, 0.999, 1e-08, 0.01, 10

N_DEV = 8
MESH_AXES = ("x", "y", "c")
LANES = 128
VMEM_LIMIT_BYTES = 56 * 1024 * 1024
MESH = pl.DeviceIdType.MESH

OFF_QA, OFF_KVA, OFF_QKVB, OFF_GAB = 0, 1024, 1280, 5888
GROUP_IN = (("w_in", 992),)
GROUP_REST = (("w_branch_a", 128), ("w_branch_b", 64), ("w_o", 128), ("w_gate_up", 704), ("w_down", 352))


def _params(*sem):
    return pltpu.CompilerParams(dimension_semantics=sem, vmem_limit_bytes=VMEM_LIMIT_BYTES)


def _sigmoid(x):
    return 1.0 / (1.0 + jnp.exp(-x))


_DIMS = {"nn": (((1,), (0,)), ((), ())), "nt": (((1,), (1,)), ((), ())), "tn": (((0,), (0,)), ((), ()))}


def _matmul(a, b, *, mode, tm, tn, tk, name, out_dtype=None, n=None, b_off=0, token=None, ins=(), outs=None, epilogue=None):
    if mode == "nn":
        (m, k), nn_ = a.shape, b.shape[1]
    elif mode == "nt":
        (m, k), nn_ = a.shape, (b.shape[0] if n is None else n)
    else:
        (k, m), nn_ = a.shape, b.shape[1]
    assert m % tm == 0 and nn_ % tn == 0 and k % tk == 0 and b_off % tn == 0, (name, m, nn_, k)
    nk = k // tk
    joff = b_off // tn
    if mode == "nn":
        a_spec = pl.BlockSpec((tm, tk), lambda i, j, kk: (i, kk))
        b_spec = pl.BlockSpec((tk, tn), lambda i, j, kk: (kk, j))
    elif mode == "nt":
        a_spec = pl.BlockSpec((tm, tk), lambda i, j, kk: (i, kk))
        b_spec = pl.BlockSpec((tn, tk), lambda i, j, kk: (j + joff, kk))
    else:
        a_spec = pl.BlockSpec((tk, tm), lambda i, j, kk: (kk, i))
        b_spec = pl.BlockSpec((tk, tn), lambda i, j, kk: (kk, j))
    dims = _DIMS[mode]
    has_token = token is not None
    plain = epilogue is None
    if plain:
        outs = [(jax.ShapeDtypeStruct((m, nn_), out_dtype), (tm, tn), lambda i, j: (i, j))]

        def epilogue(acc, i, j, in_refs, out_refs):
            out_refs[0][...] = acc.astype(out_refs[0].dtype)

    nin = len(ins)

    def body(*refs):
        a_ref, b_ref = refs[:2]
        in_refs = refs[2:2 + nin]
        out_refs = refs[2 + nin + has_token:-1]
        acc_ref = refs[-1]
        kk = pl.program_id(2)
        part = lax.dot_general(a_ref[...].astype(BF16), b_ref[...].astype(BF16), dims, preferred_element_type=F32)

        def finish(acc):
            epilogue(acc, pl.program_id(0), pl.program_id(1), in_refs, out_refs)

        if nk == 1:
            finish(part)
        else:
            @pl.when(kk == 0)
            def _():
                acc_ref[...] = part

            @pl.when(kk > 0)
            def _():
                acc_ref[...] += part

            @pl.when(kk == nk - 1)
            def _():
                finish(acc_ref[...])

    def spec(block, index):
        return pl.BlockSpec(block, lambda i, j, kk: index(i, j))

    in_specs, args = [a_spec, b_spec], [a, b]
    for arr, block, index in ins:
        in_specs.append(spec(block, index))
        args.append(arr)
    if has_token:
        in_specs.append(pl.BlockSpec(token.shape, lambda i, j, kk: (0, 0)))
        args.append(token)
    res = pl.pallas_call(
        body,
        name=name,
        grid=(m // tm, nn_ // tn, nk),
        in_specs=in_specs,
        out_specs=[spec(block, index) for _, block, index in outs],
        out_shape=[shape for shape, _, _ in outs],
        scratch_shapes=[pltpu.VMEM((tm, tn) if nk > 1 else (8, LANES), F32)],
        compiler_params=_params("arbitrary", "arbitrary", "arbitrary"),
    )(*args)
    return res[0] if plain else res


def _proj_rope(a, bt, cos, sin, *, n, b_off, rope_cols, tm, tn, name, out_dtype=F32):
    m, k = a.shape
    assert m % tm == 0 and n % tn == 0 and b_off % tn == 0 and rope_cols % tn == 0, name
    joff = b_off // tn
    nrope = rope_cols // tn

    def body(a_ref, b_ref, c_ref, s_ref, o_ref):
        acc = lax.dot_general(a_ref[...], b_ref[...], _DIMS["nt"], preferred_element_type=F32)
        j = pl.program_id(1)

        @pl.when(j < nrope)
        def _():
            o_ref[...] = _rope(acc, c_ref[...], s_ref[...]).astype(o_ref.dtype)

        @pl.when(j >= nrope)
        def _():
            o_ref[...] = acc.astype(o_ref.dtype)

    table = pl.BlockSpec((tm, LANES), lambda i, j: (i, 0))
    return pl.pallas_call(
        body,
        name=name,
        grid=(m // tm, n // tn),
        in_specs=[pl.BlockSpec((tm, k), lambda i, j: (i, 0)), pl.BlockSpec((tn, k), lambda i, j: (j + joff, 0)), table, table],
        out_specs=pl.BlockSpec((tm, tn), lambda i, j: (i, j)),
        out_shape=jax.ShapeDtypeStruct((m, n), out_dtype),
        compiler_params=_params("parallel", "parallel"),
    )(a, bt, cos, sin)


ROW_TILE = 256


def _rows(width, col=0):
    return pl.BlockSpec((1, ROW_TILE, width), lambda b, t: (b, t, col))


def _per_batch(nrows, width):
    return pl.BlockSpec((1, nrows, width), lambda b, t: (b, 0, 0))


def _whole(shape):
    return pl.BlockSpec(shape, lambda b, t: (0,) * len(shape))


def _row_call(body, name, bsz, seq, in_specs, out_specs, out_shape, accumulates=False):
    return pl.pallas_call(
        body,
        name=name,
        grid=(bsz, seq // ROW_TILE),
        in_specs=in_specs,
        out_specs=out_specs,
        out_shape=out_shape,
        compiler_params=_params("parallel", "arbitrary" if accumulates else "parallel"),
    )


def _acc_rows(acc_ref, first, rows):
    @pl.when(first)
    def _():
        acc_ref[...] = jnp.zeros_like(acc_ref)

    for r, val in enumerate(rows):
        acc_ref[0, r:r + 1, :] += val


def _colsum(v):
    return jnp.sum(v, axis=0, keepdims=True)


def _ln_stats(z):
    mu = jnp.mean(z, axis=-1, keepdims=True)
    zc = z - mu
    var = jnp.mean(zc * zc, axis=-1, keepdims=True)
    rstd = lax.rsqrt(var + LN_EPS)
    return zc * rstd, rstd


def _ln_bwd(dxhat, xhat, rstd):
    m1 = jnp.mean(dxhat, axis=-1, keepdims=True)
    m2 = jnp.mean(dxhat * xhat, axis=-1, keepdims=True)
    return rstd * (dxhat - m1 - xhat * m2)


def _modulate_in(x, mod):
    bsz, seq, d = x.shape

    def body(x_ref, mod_ref, u_ref):
        u_ref[0] = (x_ref[0] * (1.0 + mod_ref[0, 1:2, :]) + mod_ref[0, 0:1, :]).astype(BF16)

    return _row_call(body, "modulate_in", bsz, seq, [_rows(d), _per_batch(8, d)], _rows(d),
                     jax.ShapeDtypeStruct((bsz, seq, d), BF16))(x, mod)


def _gate_merge(gab, ya, yb):
    bsz, seq, d = ya.shape

    def body(ga_ref, gb_ref, ya_ref, yb_ref, o_ref):
        ga, gb, ya_, yb_ = (r[0].astype(F32) for r in (ga_ref, gb_ref, ya_ref, yb_ref))
        o_ref[0] = (_sigmoid(ga) * ya_ + _sigmoid(gb) * yb_).astype(BF16)

    return _row_call(body, "gate_merge", bsz, seq, [_rows(d, 0), _rows(d, 1), _rows(d), _rows(d)], _rows(d),
                     jax.ShapeDtypeStruct((bsz, seq, d), BF16))(gab, gab, ya, yb)


EP_TILE = 512


def _ep_specs(seq, d):
    tiles = seq // EP_TILE
    return ((EP_TILE, d), lambda i, j: (i, 0)), ((1, 8, d), lambda i, j: (i // tiles, 0, 0)), ((1, d), lambda i, j: (0, 0))


def _wo_ln1(merged, wo, x, mod, g, b, seq):
    ntok, d = x.shape
    row, per_b, whole = _ep_specs(seq, d)

    def epilogue(y, i, j, ins, outs):
        x_ref, mod_ref, g_ref, b_ref = ins
        y_ref, h_ref, u_ref = outs
        z = DEEPNORM_ALPHA * x_ref[...] + (1.0 + mod_ref[0, 2:3, :]) * y
        xhat, _ = _ln_stats(z)
        h = xhat * g_ref[...] + b_ref[...]
        y_ref[...] = y
        h_ref[...] = h
        u_ref[...] = (h * (1.0 + mod_ref[0, 4:5, :]) + mod_ref[0, 3:4, :]).astype(BF16)

    f32, bf16 = jax.ShapeDtypeStruct((ntok, d), F32), jax.ShapeDtypeStruct((ntok, d), BF16)
    return _matmul(merged, wo, mode="nn", tm=EP_TILE, tn=d, tk=d, name="w_o_ln1",
                   ins=[(x,) + row, (mod,) + per_b, (g,) + whole, (b,) + whole],
                   outs=[(f32,) + row, (f32,) + row, (bf16,) + row], epilogue=epilogue)


def _silu_mul(h):
    bsz, seq, _ = h.shape

    def body(hg_ref, hu_ref, a_ref):
        hg = hg_ref[0].astype(F32)
        a_ref[0] = (hg * _sigmoid(hg) * hu_ref[0].astype(F32)).astype(BF16)

    return _row_call(body, "silu_mul", bsz, seq, [_rows(D_FF, 0), _rows(D_FF, 1)], _rows(D_FF),
                     jax.ShapeDtypeStruct((bsz, seq, D_FF), BF16))(h, h)


def _down_ln2_loss_bwd(a, wd, h1, mod, g, b, target, seq):
    ntok, d = h1.shape
    row, per_b, whole = _ep_specs(seq, d)
    tiles = seq // EP_TILE

    def epilogue(y, i, j, ins, outs):
        h_ref, mod_ref, g_ref, b_ref, t_ref = ins
        dy_ref, dh_ref, acc_ref = outs
        gate = 1.0 + mod_ref[0, 5:6, :]
        z = DEEPNORM_ALPHA * h_ref[...] + gate * y
        xhat, rstd = _ln_stats(z)
        diff = xhat * g_ref[...] + b_ref[...] - t_ref[...]
        loss = 0.5 * jnp.sum(jnp.sum(diff * diff, axis=-1, keepdims=True) / d, axis=0, keepdims=True)
        dout = diff / d
        dz = _ln_bwd(dout * g_ref[...], xhat, rstd)
        dy_ref[...] = (gate * dz).astype(BF16)
        dh_ref[...] = DEEPNORM_ALPHA * dz
        _acc_rows(acc_ref, i % tiles == 0,
                  [_colsum(dout * xhat), _colsum(dout), _colsum(dz * y), jnp.broadcast_to(loss, (1, d))])

    return _matmul(a, wd, mode="nn", tm=EP_TILE, tn=d, tk=a.shape[1], name="down_ln2_loss_bwd",
                   ins=[(h1,) + row, (mod,) + per_b, (g,) + whole, (b,) + whole, (target,) + row],
                   outs=[(jax.ShapeDtypeStruct((ntok, d), BF16),) + row, (jax.ShapeDtypeStruct((ntok, d), F32),) + row,
                         (jax.ShapeDtypeStruct((ntok // seq, 8, d), F32),) + per_b], epilogue=epilogue)


def _silu_mul_bwd(da, h):
    bsz, seq, _ = h.shape

    def body(da_ref, hg_ref, hu_ref, dh_ref):
        hg, da_ = hg_ref[0].astype(F32), da_ref[0].astype(F32)
        sg = _sigmoid(hg)
        dh_ref[0, :, :D_FF] = (da_ * hu_ref[0].astype(F32) * (sg * (1.0 + hg * (1.0 - sg)))).astype(BF16)
        dh_ref[0, :, D_FF:] = (da_ * (hg * sg)).astype(BF16)

    return _row_call(body, "silu_mul_bwd", bsz, seq, [_rows(D_FF), _rows(D_FF, 0), _rows(D_FF, 1)], _rows(2 * D_FF),
                     jax.ShapeDtypeStruct((bsz, seq, 2 * D_FF), BF16))(da, h, h)


def _gate_up_dgrad_ln1_bwd(dh, wgut, dh1a, x, y1, mod, g, b, seq):
    ntok, d = x.shape
    row, per_b, whole = _ep_specs(seq, d)
    tiles = seq // EP_TILE

    def epilogue(du, i, j, ins, outs):
        dh_ref, x_ref, y_ref, mod_ref, g_ref, b_ref = ins
        dy_ref, dx_ref, acc_ref = outs
        y = y_ref[...]
        gate = 1.0 + mod_ref[0, 2:3, :]
        z = DEEPNORM_ALPHA * x_ref[...] + gate * y
        xhat, rstd = _ln_stats(z)
        h1 = xhat * g_ref[...] + b_ref[...]
        dh1 = dh_ref[...] + du * (1.0 + mod_ref[0, 4:5, :])
        dz = _ln_bwd(dh1 * g_ref[...], xhat, rstd)
        dy_ref[...] = (gate * dz).astype(BF16)
        dx_ref[...] = DEEPNORM_ALPHA * dz
        _acc_rows(acc_ref, i % tiles == 0,
                  [_colsum(dh1 * xhat), _colsum(dh1), _colsum(dz * y), _colsum(du * h1), _colsum(du)])

    return _matmul(dh, wgut, mode="nn", tm=EP_TILE, tn=d, tk=D_FF, name="gate_up_dgrad_ln1_bwd",
                   ins=[(dh1a,) + row, (x,) + row, (y1,) + row, (mod,) + per_b, (g,) + whole, (b,) + whole],
                   outs=[(jax.ShapeDtypeStruct((ntok, d), BF16),) + row, (jax.ShapeDtypeStruct((ntok, d), F32),) + row,
                         (jax.ShapeDtypeStruct((ntok // seq, 8, d), F32),) + per_b], epilogue=epilogue)


def _wo_dgrad_gate_bwd(dy1, wo, gab, ya, yb):
    ntok, d = ya.shape
    tm, tn = 1024, 512
    tile = ((tm, tn), lambda i, j: (i, j))
    tile_b = ((tm, tn), lambda i, j: (i, j + d // tn))

    def epilogue(dm_, i, j, ins, outs):
        ga_ref, gb_ref, ya_ref, yb_ref = ins
        dya_ref, dyb_ref, dga_ref, dgb_ref = outs
        sa, sb = _sigmoid(ga_ref[...].astype(F32)), _sigmoid(gb_ref[...].astype(F32))
        dya_ref[...] = (dm_ * sa).astype(BF16)
        dyb_ref[...] = (dm_ * sb).astype(BF16)
        dga_ref[...] = (dm_ * ya_ref[...].astype(F32) * sa * (1.0 - sa)).astype(BF16)
        dgb_ref[...] = (dm_ * yb_ref[...].astype(F32) * sb * (1.0 - sb)).astype(BF16)

    shp = jax.ShapeDtypeStruct((ntok, d), BF16)
    return _matmul(dy1, wo, mode="nt", tm=tm, tn=tn, tk=d, name="w_o_dgrad_gate_bwd",
                   ins=[(gab,) + tile, (gab,) + tile_b, (ya,) + tile, (yb,) + tile],
                   outs=[(shp,) + tile] * 4, epilogue=epilogue)


def _w_in_dgrad_grad_x(dproj, wint, dxa, x, mod, seq, token):
    ntok, d = x.shape
    row, per_b, _ = _ep_specs(seq, d)
    tiles = seq // EP_TILE

    def epilogue(du, i, j, ins, outs):
        dxa_ref, x_ref, mod_ref = ins
        gx_ref, acc_ref = outs
        gx_ref[...] = dxa_ref[...] + du * (1.0 + mod_ref[0, 1:2, :])
        _acc_rows(acc_ref, i % tiles == 0, [_colsum(du * x_ref[...]), _colsum(du)])

    return _matmul(dproj, wint, mode="nn", tm=EP_TILE, tn=d, tk=wint.shape[0] // 2, name="w_in_dgrad_grad_x", token=token,
                   ins=[(dxa,) + row, (x,) + row, (mod,) + per_b],
                   outs=[(jax.ShapeDtypeStruct((ntok, d), F32),) + row, (jax.ShapeDtypeStruct((ntok // seq, 8, d), F32),) + per_b],
                   epilogue=epilogue)


def _branch_b_gate_merge(ob, wbbt, gab, ya):
    ntok, d = ya.shape
    tm, tn = 1024, 512
    tile = ((tm, tn), lambda i, j: (i, j))
    tile_b = ((tm, tn), lambda i, j: (i, j + d // tn))

    def epilogue(yb, i, j, ins, outs):
        ga_ref, gb_ref, ya_ref = ins
        yb_ref, merged_ref = outs
        yb_ref[...] = yb.astype(BF16)
        merged_ref[...] = (_sigmoid(ga_ref[...].astype(F32)) * ya_ref[...].astype(F32)
                           + _sigmoid(gb_ref[...].astype(F32)) * yb).astype(BF16)

    shp = jax.ShapeDtypeStruct((ntok, d), BF16)
    return _matmul(ob, wbbt, mode="nt", tm=tm, tn=tn, tk=ob.shape[1], name="branch_b_gate_merge",
                   ins=[(gab,) + tile, (gab,) + tile_b, (ya,) + tile], outs=[(shp,) + tile] * 2, epilogue=epilogue)


def _segsum64(v):
    rows, width = v.shape
    ri = lax.broadcasted_iota(jnp.int32, (LANES, LANES), 0) // HEAD_DIM
    ci = lax.broadcasted_iota(jnp.int32, (LANES, LANES), 1) // HEAD_DIM
    ones = jnp.where(ri == ci, 1.0, 0.0).astype(BF16)
    out = []
    for c in range(width // LANES):
        part = v[:, c * LANES:(c + 1) * LANES]
        hi = part.astype(BF16)
        lo = (part - hi.astype(F32)).astype(BF16)
        out.append(jnp.dot(hi, ones, preferred_element_type=F32) + jnp.dot(lo, ones, preferred_element_type=F32))
    return jnp.concatenate(out, axis=1) if len(out) > 1 else out[0]


def _merge_b(os_, ls_):
    bsz, seq, w = os_[0].shape

    def body(o0, o1, o2, l0, l1, l2, ob_ref):
        ls = [l0[0], l1[0], l2[0]]
        mx = jnp.maximum(jnp.maximum(ls[0], ls[1]), ls[2])
        es = [jnp.exp(l - mx) for l in ls]
        den = es[0] + es[1] + es[2]
        ob_ref[0] = ((es[0] / den) * o0[0] + (es[1] / den) * o1[0] + (es[2] / den) * o2[0]).astype(BF16)

    return _row_call(body, "merge_b", bsz, seq, [_rows(w)] * 6, _rows(w),
                     jax.ShapeDtypeStruct((bsz, seq, w), BF16))(*os_, *ls_)


def _branch_b_dgrad_merge_bwd(dyb, wbbt, os_, ls_):
    ntok, w = os_[0].shape
    row = ((EP_TILE, w), lambda i, j: (i, 0))

    def epilogue(dob_, i, j, ins, outs):
        os_r, ls_r = ins[:3], ins[3:]
        do_r, dd_r = outs[:3], outs[3:]
        ls = [l[...] for l in ls_r]
        mx = jnp.maximum(jnp.maximum(ls[0], ls[1]), ls[2])
        es = [jnp.exp(l - mx) for l in ls]
        den = es[0] + es[1] + es[2]
        ws = [e / den for e in es]
        dws = [_segsum64(dob_ * o[...]) for o in os_r]
        mean = ws[0] * dws[0] + ws[1] * dws[1] + ws[2] * dws[2]
        for wg, do_ref, dd_ref in zip(ws, do_r, dd_r):
            do_ref[...] = wg * dob_
            dd_ref[...] = -wg * mean

    shp = jax.ShapeDtypeStruct((ntok, w), F32)
    return _matmul(dyb, wbbt, mode="nn", tm=EP_TILE, tn=w, tk=dyb.shape[1], name="branch_b_dgrad_merge_bwd",
                   ins=[(v,) + row for v in list(os_) + list(ls_)], outs=[(shp,) + row] * 6, epilogue=epilogue)


def _branch_a_dgrad_delta(dya, wba, oa, lse_a, sinks_exp, seq):
    ntok, w = oa.shape
    row, per_b, whole = _ep_specs(seq, w)
    tiles = seq // EP_TILE

    def epilogue(do_, i, j, ins, outs):
        o_ref, l_ref, s_ref = ins
        do_ref, dd_ref, acc_ref = outs
        dd = -_segsum64(do_ * o_ref[...])
        do_ref[...] = do_
        dd_ref[...] = dd
        _acc_rows(acc_ref, i % tiles == 0, [_colsum(dd * jnp.exp(s_ref[...] - l_ref[...]))])

    shp = jax.ShapeDtypeStruct((ntok, w), F32)
    return _matmul(dya, wba, mode="nt", tm=EP_TILE, tn=w, tk=dya.shape[1], name="branch_a_dgrad_delta",
                   ins=[(oa,) + row, (lse_a,) + row, (sinks_exp,) + whole],
                   outs=[(shp,) + row, (shp,) + row, (jax.ShapeDtypeStruct((ntok // seq, 8, w), F32),) + per_b],
                   epilogue=epilogue)


def _swap_halves(v):
    src = lax.broadcasted_iota(jnp.int32, (LANES, LANES), 0)
    dst = lax.broadcasted_iota(jnp.int32, (LANES, LANES), 1)
    partner = jnp.where((dst % HEAD_DIM) < HEAD_DIM // 2, dst + HEAD_DIM // 2, dst - HEAD_DIM // 2)
    perm = jnp.where(src == partner, 1.0, 0.0).astype(BF16)
    hi = v.astype(BF16)
    lo = (v - hi.astype(F32)).astype(BF16)
    return jnp.dot(hi, perm, preferred_element_type=F32) + jnp.dot(lo, perm, preferred_element_type=F32)


def _swap_halves_roll(v):
    lane = lax.broadcasted_iota(jnp.int32, v.shape, 1)
    return jnp.where((lane % HEAD_DIM) < HEAD_DIM // 2, pltpu.roll(v, LANES - HEAD_DIM // 2, 1),
                     pltpu.roll(v, HEAD_DIM // 2, 1))


def _rope(v, cos, sin, sign=1.0, mxu=True):
    swap = _swap_halves if mxu else _swap_halves_roll
    out = []
    for c in range(v.shape[1] // LANES):
        part = v[:, c * LANES:(c + 1) * LANES]
        out.append(part * cos + sign * (swap(part) * sin))
    return jnp.concatenate(out, axis=1) if len(out) > 1 else out[0]


def _half_mask(shape, half):
    lane = lax.broadcasted_iota(jnp.int32, shape, len(shape) - 1) % LANES
    return (lane < HEAD_DIM) if half == 0 else (lane >= HEAD_DIM)


def _dup_half(v, half):
    return jnp.where(_half_mask(v.shape, half), v, pltpu.roll(v, HEAD_DIM, 1))


def _fold_halves(v):
    return v + pltpu.roll(v, HEAD_DIM, 1)


def _pick_halves(lo_rows, hi_rows):
    return jnp.where(_half_mask(lo_rows.shape, 0), lo_rows, hi_rows)


def _stack_masked(v, pairs):
    parts = []
    for c in pairs:
        pair = v[:, c * LANES:(c + 1) * LANES]
        parts += [jnp.where(_half_mask(pair.shape, half), pair, 0.0) for half in (0, 1)]
    return jnp.concatenate(parts, axis=0)


def _stack_pair_cols(v, pairs):
    return jnp.concatenate([v[:, c * LANES + half * HEAD_DIM:c * LANES + half * HEAD_DIM + 1] for c in pairs for half in (0, 1)],
                           axis=0)


ATTN_UNITS = 16


def _class_rows(r):
    return [pl.ds(0, QBLOCK)] if r == 1 else [pl.ds(rho, QBLOCK, stride=r) for rho in range(r)]


def _band_mask(nrows, nk, blk, n_back, has_prev):
    qi = lax.broadcasted_iota(jnp.int32, (nrows, nk), 0) % QBLOCK
    ki = lax.broadcasted_iota(jnp.int32, (nrows, nk), 1)
    if has_prev:
        dist = qi + QBLOCK - ki
        return (dist >= 0) & (dist <= n_back) & ((ki >= QBLOCK) | (blk > 0))
    dist = qi - ki
    return (dist >= 0) & (dist <= n_back)


def _attn_fwd(q_arr, k_arr, v_arr, *, name, npair, gqa, q_col, k_col, v_col, nchunk, r, n_back, sinks=None):
    bsz, seq, _ = q_arr.shape
    rr = QBLOCK * r
    nblk = seq // rr
    qw = npair * LANES
    kw = LANES if gqa else qw
    has_prev = nblk > 1
    has_sink = sinks is not None
    scale = HEAD_DIM ** -0.5

    def body(*refs):
        refs = list(refs)
        q_ref, kc_ref, vc_ref = refs[:3]
        pos = 3
        if has_prev:
            kp_ref, vp_ref = refs[pos:pos + 2]
            pos += 2
        if has_sink:
            sink_ref = refs[pos]
            pos += 1
        o_ref, lse_ref = refs[pos:pos + 2]
        blk = pl.program_id(2)
        nk = (2 if has_prev else 1) * QBLOCK
        valid = _band_mask(QBLOCK, nk, blk, n_back, has_prev)
        per = npair // 2
        classes = _class_rows(r)
        step = max(1, ATTN_UNITS // (2 * npair))
        for first in range(0, len(classes), step):
            batch = classes[first:first + step]
            units = []
            for ci, rows in enumerate(batch):
                q = q_ref[0, rows, :] * scale
                k, v = kc_ref[0, rows, :], vc_ref[0, rows, :]
                if has_prev:
                    k = jnp.concatenate([kp_ref[0, rows, :], k], axis=0)
                    v = jnp.concatenate([vp_ref[0, rows, :], v], axis=0)
                if gqa:
                    kdup = [_dup_half(k, hk).astype(BF16) for hk in range(2)]
                    vdup = [_dup_half(v, hk) for hk in range(2)]
                for c in range(npair):
                    sl = slice(c * LANES, (c + 1) * LANES)
                    qc = q[:, sl]
                    kc, vc = (kdup[c // per], vdup[c // per]) if gqa else (k[:, sl].astype(BF16), v[:, sl])
                    for half in (0, 1):
                        qm = jnp.where(_half_mask(qc.shape, half), qc, 0.0).astype(BF16)
                        vm = jnp.where(_half_mask(vc.shape, half), vc, 0.0).astype(BF16)
                        s = lax.dot_general(qm, kc, _DIMS["nt"], preferred_element_type=F32)
                        units.append(dict(ci=ci, c=c, half=half, s=s, vm=vm, sk=sink_ref[2 * c + half] if has_sink else None))
            for u in units:
                s = jnp.where(valid, u["s"], NEG_INF)
                m = jnp.max(s, axis=1, keepdims=True)
                if has_sink:
                    m = jnp.maximum(m, u["sk"])
                p = jnp.exp(s - m)
                den = jnp.sum(p, axis=1, keepdims=True)
                if has_sink:
                    den = den + jnp.exp(u["sk"] - m)
                u.update(p=p.astype(BF16), den=den, lse=m + jnp.log(den))
            for u in units:
                u["o"] = jnp.dot(u["p"], u["vm"], preferred_element_type=F32) / u["den"]
            for ci, rows in enumerate(batch):
                outs, lses = [None] * npair, [None] * npair
                for u in units:
                    if u["ci"] != ci:
                        continue
                    c, o = u["c"], u["o"]
                    lse = jnp.broadcast_to(u["lse"], o.shape)
                    outs[c] = o if u["half"] == 0 else outs[c] + o
                    lses[c] = lse if u["half"] == 0 else _pick_halves(lses[c], lse)
                o_ref[0, rows, :] = jnp.concatenate(outs, axis=1) if npair > 1 else outs[0]
                lse_ref[0, rows, :] = jnp.concatenate(lses, axis=1) if npair > 1 else lses[0]

    def cur(width, col0):
        return pl.BlockSpec((1, rr, width), lambda b, c, i: (b, i, col0 + c))

    def prev(width, col0):
        return pl.BlockSpec((1, rr, width), lambda b, c, i: (b, jnp.maximum(i - 1, 0), col0 + c))

    in_specs = [cur(qw, q_col), cur(kw, k_col), cur(kw, v_col)]
    args = [q_arr, k_arr, v_arr]
    if has_prev:
        in_specs += [prev(kw, k_col), prev(kw, v_col)]
        args += [k_arr, v_arr]
    if has_sink:
        in_specs.append(pl.BlockSpec(memory_space=pltpu.SMEM))
        args.append(sinks)
    return pl.pallas_call(
        body,
        name=name,
        grid=(bsz, nchunk, nblk),
        in_specs=in_specs,
        out_specs=[pl.BlockSpec((1, rr, qw), lambda b, c, i: (b, i, c))] * 2,
        out_shape=[jax.ShapeDtypeStruct((bsz, seq, nchunk * qw), F32)] * 2,
        compiler_params=_params("parallel", "parallel", "parallel"),
    )(*args)


def _attn_bwd(q_arr, k_arr, v_arr, cos, sin, do, lse, dd, *, name, npair, gqa, q_col, k_col, v_col, nchunk, r, n_back,
              token=None):
    bsz, seq, _ = q_arr.shape
    rr = QBLOCK * r
    nblk = seq // rr
    qw = npair * LANES
    kw = LANES if gqa else qw
    has_next = nblk > 1
    has_token = token is not None
    scale = HEAD_DIM ** -0.5

    def body(*refs):
        refs = list(refs)
        k_ref, v_ref, c_ref, s_ref = refs[:4]
        tile_refs = [refs[4:8]]
        pos = 8
        if has_next:
            tile_refs.append(refs[pos:pos + 4])
            pos += 4
        if has_token:
            pos += 1
        dq_ref, dk_ref, dv_ref = refs[pos:pos + 3]
        carry_ref = refs[pos + 3]
        blk = pl.program_id(2)
        if has_next:
            @pl.when(blk == 0)
            def _():
                carry_ref[...] = jnp.zeros_like(carry_ref)

        nrows = (npair if gqa else 1) * QBLOCK
        qi = lax.broadcasted_iota(jnp.int32, (nrows, QBLOCK), 0) % QBLOCK
        ki = lax.broadcasted_iota(jnp.int32, (nrows, QBLOCK), 1)
        valids = [qi >= ki, (qi + QBLOCK - ki <= n_back) & (blk + 1 < nblk)]
        per = npair // 2
        ntile = len(tile_refs)
        cat = lambda parts: jnp.concatenate(parts, axis=1) if len(parts) > 1 else parts[0]
        classes = _class_rows(r)
        step = max(1, ATTN_UNITS // (ntile * (2 if gqa else 2 * npair)))
        for first in range(0, len(classes), step):
            batch = classes[first:first + step]
            units = []
            for ci, rows in enumerate(batch):
                tiles = [(q_ref[0, rows, :] * scale, do_ref[0, rows, :], l_ref[0, rows, :], d_ref[0, rows, :])
                         for q_ref, do_ref, l_ref, d_ref in tile_refs]
                k, v = k_ref[0, rows, :], v_ref[0, rows, :]
                if gqa:
                    for hk in range(2):
                        pairs = list(range(hk * per, (hk + 1) * per))
                        kd, vd = _dup_half(k, hk).astype(BF16), _dup_half(v, hk).astype(BF16)
                        for t, (q, do_, l_, d_) in enumerate(tiles):
                            units.append(dict(ci=ci, t=t, hk=hk, pairs=pairs, qs=_stack_masked(q, pairs).astype(BF16),
                                              dos=_stack_masked(do_, pairs).astype(BF16), lcol=_stack_pair_cols(l_, pairs),
                                              dcol=_stack_pair_cols(d_, pairs), kmat=kd, vmat=vd, kdq=kd))
                else:
                    for c in range(npair):
                        sl = slice(c * LANES, (c + 1) * LANES)
                        kc, vcb = k[:, sl], v[:, sl].astype(BF16)
                        kcb = kc.astype(BF16)
                        for t, (q, do_, l_, d_) in enumerate(tiles):
                            for half in (0, 1):
                                hm = _half_mask(kc.shape, half)
                                col = c * LANES + half * HEAD_DIM
                                units.append(dict(ci=ci, t=t, c=c, half=half, qs=jnp.where(hm, q[:, sl], 0.0).astype(BF16),
                                                  dos=jnp.where(hm, do_[:, sl], 0.0).astype(BF16), lcol=l_[:, col:col + 1],
                                                  dcol=d_[:, col:col + 1], kmat=kcb, vmat=vcb,
                                                  kdq=jnp.where(hm, kc, 0.0).astype(BF16)))
            for u in units:
                u["s"] = lax.dot_general(u["qs"], u["kmat"], _DIMS["nt"], preferred_element_type=F32)
                u["dp"] = lax.dot_general(u["dos"], u["vmat"], _DIMS["nt"], preferred_element_type=F32)
            for u in units:
                p = jnp.exp(jnp.where(valids[u["t"]], u["s"], NEG_INF) - u["lcol"])
                u["ds"] = (p * (u["dp"] + u["dcol"])).astype(BF16)
                u["p"] = p.astype(BF16)
            for u in units:
                u["dv"] = lax.dot_general(u["p"], u["dos"], _DIMS["tn"], preferred_element_type=F32)
                u["dk"] = lax.dot_general(u["ds"], u["qs"], _DIMS["tn"], preferred_element_type=F32)
                u["dq"] = jnp.dot(u["ds"], u["kdq"], preferred_element_type=F32) * scale
            for ci, rows in enumerate(batch):
                mine = [u for u in units if u["ci"] == ci]
                dq = [[None] * npair for _ in range(ntile)]
                if gqa:
                    dk_out = dv_out = None
                    for hk in range(2):
                        us = [u for u in mine if u["hk"] == hk]
                        for u in us:
                            for i, c in enumerate(u["pairs"]):
                                dq[u["t"]][c] = _pick_halves(u["dq"][2 * i * QBLOCK:(2 * i + 1) * QBLOCK],
                                                             u["dq"][(2 * i + 1) * QBLOCK:(2 * i + 2) * QBLOCK])
                        dk_h = _fold_halves(functools.reduce(jnp.add, [u["dk"] for u in us]))
                        dv_h = _fold_halves(functools.reduce(jnp.add, [u["dv"] for u in us]))
                        dk_out = dk_h if hk == 0 else _pick_halves(dk_out, dk_h)
                        dv_out = dv_h if hk == 0 else _pick_halves(dv_out, dv_h)
                else:
                    dks, dvs = [], []
                    for c in range(npair):
                        us = [u for u in mine if u["c"] == c]
                        dks.append(functools.reduce(jnp.add, [u["dk"] for u in us]))
                        dvs.append(functools.reduce(jnp.add, [u["dv"] for u in us]))
                        for t in range(ntile):
                            dq[t][c] = functools.reduce(jnp.add, [u["dq"] for u in us if u["t"] == t])
                    dk_out, dv_out = cat(dks), cat(dvs)
                ck, sk_ = c_ref[0, rows, :], s_ref[0, rows, :]
                dk_ref[0, rows, :] = _rope(dk_out, ck, sk_, sign=-1.0, mxu=gqa)
                dv_ref[0, rows, :] = dv_out
                dq_cur = cat(dq[0])
                if has_next:
                    dq_cur = dq_cur + carry_ref[rows, :]
                    carry_ref[rows, :] = cat(dq[1])
                dq_ref[0, rows, :] = _rope(dq_cur, ck, sk_, sign=-1.0, mxu=gqa)

    def at(width, col0, shift):
        return pl.BlockSpec((1, rr, width), lambda b, c, i: (b, jnp.minimum(i + shift, nblk - 1), col0 + c))

    in_specs = [at(kw, k_col, 0), at(kw, v_col, 0), pl.BlockSpec((1, rr, LANES), lambda b, c, i: (b, i, 0)),
                pl.BlockSpec((1, rr, LANES), lambda b, c, i: (b, i, 0))]
    args = [k_arr, v_arr, cos, sin]
    for shift in (0, 1) if has_next else (0,):
        in_specs += [at(qw, q_col, shift), at(qw, 0, shift), at(qw, 0, shift), at(qw, 0, shift)]
        args += [q_arr, do, lse, dd]
    if has_token:
        in_specs.append(pl.BlockSpec(token.shape, lambda b, c, i: (0, 0)))
        args.append(token)
    return pl.pallas_call(
        body,
        name=name,
        grid=(bsz, nchunk, nblk),
        in_specs=in_specs,
        out_specs=[pl.BlockSpec((1, rr, qw), lambda b, c, i: (b, i, c)),
                   pl.BlockSpec((1, rr, kw), lambda b, c, i: (b, i, c)),
                   pl.BlockSpec((1, rr, kw), lambda b, c, i: (b, i, c))],
        out_shape=[jax.ShapeDtypeStruct((bsz, seq, nchunk * qw), F32),
                   jax.ShapeDtypeStruct((bsz, seq, nchunk * kw), F32),
                   jax.ShapeDtypeStruct((bsz, seq, nchunk * kw), F32)],
        scratch_shapes=[pltpu.VMEM((rr, qw) if has_next else (8, LANES), F32)],
        compiler_params=_params("parallel", "parallel", "arbitrary"),
    )(*args)


B_CHUNKS = {1: (4, 1), 4: (1, 4), 16: (1, 4)}


def _rope_tables(positions):
    half = HEAD_DIM // 2
    inv = ROPE_THETA ** (-jnp.arange(half, dtype=F32) / half)
    ang = positions.astype(F32)[..., None] * inv
    cos, sin = jnp.cos(ang), jnp.sin(ang)
    return jnp.concatenate([cos] * 4, axis=-1), jnp.concatenate([-sin, sin, -sin, sin], axis=-1)


def _layer_step(x, mod, positions, sinks, ln1_g, ln1_b, ln2_g, ln2_b, target, get_w_in, get_rest, hook):
    bsz, seq, d = x.shape
    ntok = bsz * seq
    flat = lambda v: v.reshape(ntok, v.shape[-1])
    unflat = lambda v: v.reshape(bsz, seq, v.shape[-1])
    cos, sin = _rope_tables(positions)
    mm = functools.partial(_matmul, tm=1024, tk=1024)
    scalar = lambda tok: 0.0 if tok is None else tok[0, 0]

    u1 = _modulate_in(x, mod)
    u1f = flat(u1)
    wint = get_w_in(u1)
    cosf, sinf = flat(cos), flat(sin)
    proj = functools.partial(_proj_rope, u1f, wint, cosf, sinf, tm=2048)
    qa = unflat(proj(n=1024, b_off=OFF_QA, rope_cols=1024, tn=512, name="proj_qa"))
    kva = unflat(proj(n=256, b_off=OFF_KVA, rope_cols=128, tn=128, name="proj_kva"))
    qkvb = unflat(proj(n=4608, b_off=OFF_QKVB, rope_cols=3072, tn=256, name="proj_qkvb"))
    gab = unflat(proj(n=2048, b_off=OFF_GAB, rope_cols=0, tn=256, name="proj_gab", out_dtype=BF16))

    sink_vec = sinks.reshape(A_Q_HEADS) + scalar(hook("projected", gab))
    a_kw = dict(npair=A_Q_HEADS // 2, gqa=True, q_col=0, k_col=0, v_col=1, nchunk=1, r=1, n_back=A_WINDOW - 1)
    oa, lse_a = _attn_fwd(qa, kva, kva, name="attn_a_fwd", sinks=sink_vec, **a_kw)
    rest = get_rest(oa)
    wba, wbbt, wo, wgut, wd = (rest[n] for n in ("w_branch_a", "w_branch_b", "w_o", "w_gate_up", "w_down"))
    ya = unflat(mm(flat(oa), wba, mode="nn", out_dtype=BF16, tn=512, name="branch_a"))

    b_kws, os_, ls_ = [], [], []
    for g, (window, r) in enumerate(B_PATTERNS):
        npair, nch = B_CHUNKS[r]
        per = B_HEADS_PER_GROUP // (2 * npair)
        nsec = len(B_PATTERNS) * per
        kw_ = dict(npair=npair, gqa=False, q_col=g * per, k_col=nsec + g * per, v_col=2 * nsec + g * per, nchunk=nch, r=r,
                   n_back=window // r)
        b_kws.append(kw_)
        o_g, l_g = _attn_fwd(qkvb, qkvb, qkvb, name=f"attn_b{g}_fwd", **kw_)
        os_.append(o_g)
        ls_.append(l_g)
    ob = _merge_b(os_, ls_)
    ybf, mergedf = _branch_b_gate_merge(flat(ob), wbbt, flat(gab), flat(ya))
    xf = flat(x)
    y1f, h1f, u2f = _wo_ln1(mergedf, wo, xf, mod, ln1_g, ln1_b, seq)
    h = unflat(mm(u2f, wgut, mode="nt", out_dtype=BF16, tn=D_FF // 2, name="gate_up"))
    a = _silu_mul(h)

    dy2f, dh1af, acc2 = _down_ln2_loss_bwd(flat(a), wd, h1f, mod, ln2_g, ln2_b, flat(target), seq)
    da = unflat(mm(dy2f, wd, mode="nt", out_dtype=BF16, tn=D_FF // 2, name="down_dgrad"))
    g_wd = _matmul(flat(a), dy2f, mode="tn", out_dtype=BF16, tm=256, tn=1024, tk=ntok, name="down_wgrad")
    dh = _silu_mul_bwd(da, h)
    dhf = flat(dh)
    g_wgut = _matmul(dhf, u2f, mode="tn", out_dtype=BF16, tm=256, tn=1024, tk=ntok, name="gate_up_wgrad")
    dy1f, dxaf, acc1 = _gate_up_dgrad_ln1_bwd(dhf, wgut, dh1af, xf, y1f, mod, ln1_g, ln1_b, seq)
    g_wo = _matmul(mergedf, dy1f, mode="tn", out_dtype=BF16, tm=256, tn=1024, tk=ntok, name="w_o_wgrad")
    dyaf, dybf, dgaf, dgbf = _wo_dgrad_gate_bwd(dy1f, wo, flat(gab), flat(ya), ybf)
    g_wba = _matmul(flat(oa), dyaf, mode="tn", out_dtype=BF16, tm=256, tn=1024, tk=ntok, name="branch_a_wgrad")
    g_wbbt = _matmul(dybf, flat(ob), mode="tn", out_dtype=BF16, tm=256, tn=512, tk=ntok, name="branch_b_wgrad")
    tok = hook("grads_rest", dict(w_branch_a=g_wba, w_branch_b=g_wbbt, w_o=g_wo, w_gate_up=g_wgut, w_down=g_wd))

    sinks_exp = jnp.repeat(sinks.reshape(1, A_Q_HEADS), HEAD_DIM, axis=1) + scalar(tok)
    doa, dd_a, acc_s = _branch_a_dgrad_delta(dyaf, wba, flat(oa), flat(lse_a), sinks_exp, seq)
    doa, dd_a = unflat(doa), unflat(dd_a)
    tok = hook("delta_done", dd_a)
    dqa, dka, dva = _attn_bwd(qa, kva, kva, cos, sin, doa, lse_a, dd_a, name="attn_a_bwd", token=tok, **a_kw)
    merged_bwd = [unflat(t) for t in _branch_b_dgrad_merge_bwd(dybf, wbbt, [flat(t) for t in os_], [flat(t) for t in ls_])]
    dqs, dks, dvs = [], [], []
    for g in range(len(B_PATTERNS)):
        dq_g, dk_g, dv_g = _attn_bwd(qkvb, qkvb, qkvb, cos, sin, merged_bwd[g], ls_[g], merged_bwd[3 + g],
                                     name=f"attn_b{g}_bwd", **b_kws[g])
        dqs.append(dq_g)
        dks.append(dk_g)
        dvs.append(dv_g)
    dproj = jnp.concatenate([t.astype(BF16) for t in [dqa, dka, dva] + dqs + dks + dvs] + [unflat(dgaf), unflat(dgbf)], axis=-1)
    dprojf = flat(dproj)
    g_wint = _matmul(dprojf, u1f, mode="tn", out_dtype=BF16, tm=256, tn=1024, tk=ntok, name="w_in_wgrad")
    tok = hook("grads_w_in", dict(w_in=g_wint))
    grad_x, acc0 = _w_in_dgrad_grad_x(dprojf, wint, dxaf, xf, mod, seq, tok)
    grad_x = unflat(grad_x)
    tok = hook("dgrad_done", grad_x)

    loss_part = jnp.sum(acc2[:, 3, 0])
    dmod = jnp.stack([acc0[:, 1], acc0[:, 0], acc1[:, 2], acc1[:, 4], acc1[:, 3], acc2[:, 2]], axis=1)
    small = jnp.stack([acc1[:, 0].sum(0), acc1[:, 1].sum(0), acc2[:, 0].sum(0), acc2[:, 1].sum(0), acc_s[:, 0].sum(0)])
    small = small + scalar(tok)
    return loss_part, grad_x, dmod, small


CHIP_FLIPS = (2, 4, 6)


def _my_place():
    return lax.axis_index("x"), lax.axis_index("y"), lax.axis_index("c")


def _flip(place, k):
    px, py, pc = place
    return (1 - px if k & 4 else px, 1 - py if k & 2 else py, 1 - pc if k & 1 else pc)


def _index(place):
    return 4 * place[0] + 2 * place[1] + place[2]


def _gather_small(v, name):
    rows, cols = v.shape

    def body(v_ref, out_ref, send_sems, recv_sems):
        me = _my_place()
        out_ref[_index(me)] = v_ref[...]
        copies = []
        for k in range(1, N_DEV):
            copies.append(pltpu.make_async_remote_copy(
                src_ref=v_ref, dst_ref=out_ref.at[_index(me)], send_sem=send_sems.at[k - 1], recv_sem=recv_sems.at[k - 1],
                device_id=_flip(me, k), device_id_type=MESH))
        for cp in copies:
            cp.start()
        for k in range(1, N_DEV):
            pltpu.make_async_remote_copy(
                src_ref=v_ref, dst_ref=out_ref.at[_index(_flip(me, k))], send_sem=send_sems.at[k - 1],
                recv_sem=recv_sems.at[k - 1], device_id=_flip(me, k), device_id_type=MESH).wait_recv()
        for cp in copies:
            cp.wait_send()

    return pl.pallas_call(
        body,
        name=name,
        out_shape=jax.ShapeDtypeStruct((N_DEV, rows, cols), v.dtype),
        in_specs=[pl.BlockSpec(memory_space=pltpu.VMEM)],
        out_specs=pl.BlockSpec(memory_space=pltpu.VMEM),
        scratch_shapes=[pltpu.SemaphoreType.DMA((N_DEV - 1,)), pltpu.SemaphoreType.DMA((N_DEV - 1,))],
        compiler_params=pltpu.CompilerParams(vmem_limit_bytes=VMEM_LIMIT_BYTES),
    )(v)


_HBM = pl.BlockSpec(memory_space=pltpu.HBM)
_SEM = pl.BlockSpec(memory_space=pltpu.SEMAPHORE)
_EFFECT = pltpu.SideEffectType.DATAFLOW_SIDE_EFFECTING


def _remote(src, dst, send_sems, recv_sems, j, to):
    return pltpu.make_async_remote_copy(src_ref=src, dst_ref=dst, send_sem=send_sems.at[j], recv_sem=recv_sems.at[j],
                                        device_id=to, device_id_type=MESH)


def _copies_start(name, bufs, make_copies, nsem):
    nbuf = len(bufs)

    def body(*refs):
        for cp in make_copies(refs[:nbuf], refs[nbuf], refs[nbuf + 1]):
            cp.start()
        refs[-1][...] = jnp.zeros_like(refs[-1])

    sems = pltpu.SemaphoreType.DMA((nsem,))
    res = pl.pallas_call(
        body, name=name,
        out_shape=(sems, sems, *[pltpu.HBM(v.shape, v.dtype) for v in bufs], jax.ShapeDtypeStruct((8, LANES), F32)),
        in_specs=(_HBM,) * nbuf, out_specs=(_SEM, _SEM) + (_HBM,) * nbuf + (pl.BlockSpec(memory_space=pltpu.VMEM),),
        input_output_aliases={i: 2 + i for i in range(nbuf)},
        compiler_params=pltpu.CompilerParams(has_side_effects=_EFFECT),
    )(*[pltpu.with_memory_space_constraint(v, pltpu.HBM) for v in bufs])
    return res[0], res[1], list(res[2:2 + nbuf]), res[-1]


def _copies_wait(name, started, make_copies, after):
    send_sems, recv_sems, bufs, _ = started
    nbuf = len(bufs)

    def body(*refs):
        for cp in make_copies(refs[:nbuf], refs[nbuf], refs[nbuf + 1]):
            cp.wait_send()
            cp.wait_recv()

    return list(pl.pallas_call(
        body, name=name,
        out_shape=tuple(pltpu.HBM(v.shape, v.dtype) for v in bufs),
        in_specs=(_HBM,) * nbuf + (_SEM, _SEM, pl.BlockSpec(memory_space=pl.ANY)), out_specs=(_HBM,) * nbuf,
        input_output_aliases={i: i for i in range(nbuf)},
        compiler_params=pltpu.CompilerParams(has_side_effects=_EFFECT),
    )(*bufs, send_sems, recv_sems, after))


def _to_sibling_copies(refs, send_sems, recv_sems):
    src_ref, land_ref = refs
    me = _my_place()
    return [_remote(src_ref.at[q, 1 - me[2]], land_ref.at[q], send_sems, recv_sems, q, _flip(me, 1)) for q in range(4)]


def _to_chips_copies(refs, send_sems, recv_sems):
    src_ref, land_ref = refs
    me = _my_place()
    copies = []
    for j, k in enumerate(CHIP_FLIPS):
        to = _flip(me, k)
        copies.append(_remote(src_ref.at[2 * to[0] + to[1]], land_ref.at[j], send_sems, recv_sems, j, to))
    return copies


def _place_own(blocks, name):
    n = len(blocks)

    def body(*refs):
        me = _index(_my_place())
        copies = [pltpu.make_async_copy(refs[w], refs[n + w].at[me], refs[2 * n].at[w]) for w in range(n)]
        for cp in copies:
            cp.start()
        for cp in copies:
            cp.wait()

    return pl.pallas_call(body, name=name, out_shape=[jax.ShapeDtypeStruct((N_DEV,) + v.shape, v.dtype) for v in blocks],
                          in_specs=[_HBM] * n, out_specs=[_HBM] * n, scratch_shapes=[pltpu.SemaphoreType.DMA((n,))])(*blocks)


class _Gather:
    def __init__(self, name, blocks):
        self.name, self.n = name, len(blocks)
        lands = _place_own(blocks, name + "_place")
        self.first = _copies_start(name + "_start", list(blocks) + list(lands), self._first_copies, 4 * self.n)
        self.token = self.first[3]

    def _first_copies(self, refs, send_sems, recv_sems):
        me = _my_place()
        return [_remote(refs[w], refs[self.n + w].at[_index(me)], send_sems, recv_sems, 4 * w + j, _flip(me, k))
                for w in range(self.n) for j, k in enumerate((1,) + CHIP_FLIPS)]

    def _pass_copies(self, refs, send_sems, recv_sems):
        me = _my_place()
        copies = []
        for w, land in enumerate(refs):
            for j, k in enumerate(CHIP_FLIPS):
                slot = land.at[_index(_flip(me, k))]
                copies.append(_remote(slot, slot, send_sems, recv_sems, 3 * w + j, _flip(me, 1)))
        return copies

    def pass_on(self, after):
        lands = _copies_wait(self.name + "_wait", self.first, self._first_copies, after)[self.n:]
        self.second = _copies_start(self.name + "_pass_start", lands, self._pass_copies, 3 * self.n)
        return self.second[3]

    def finish(self, after):
        return _copies_wait(self.name + "_pass_wait", self.second, self._pass_copies, after)


SUM_SPLIT = 2


def _sum_pairs(parts, theirs):
    nchip, _, rows, cols = parts.shape
    tile = rows // SUM_SPLIT

    def body(c_ref, a_ref, b_ref, o_ref):
        o_ref[...] = (a_ref[0].astype(F32) + b_ref[...].astype(F32)).astype(BF16)

    spec = pl.BlockSpec((1, tile, cols), lambda q, t, c_ref: (q, t, 0))
    grid_spec = pltpu.PrefetchScalarGridSpec(
        num_scalar_prefetch=1, grid=(nchip, SUM_SPLIT),
        in_specs=[pl.BlockSpec((1, 1, tile, cols), lambda q, t, c_ref: (q, c_ref[0], t, 0)), spec], out_specs=spec)
    return pl.pallas_call(body, name="grad_sum_sibling", grid_spec=grid_spec,
                          out_shape=jax.ShapeDtypeStruct((nchip, rows, cols), BF16),
                          compiler_params=_params("parallel", "parallel"))(lax.axis_index("c").reshape(1), parts, theirs)


def _sum_final(chip_sum, got):
    _, rows, cols = chip_sum.shape
    tile = rows // SUM_SPLIT

    def body(q_ref, a_ref, g_ref, o_ref):
        o_ref[...] = ((a_ref[0].astype(F32) + g_ref[0].astype(F32)) + g_ref[1].astype(F32)) + g_ref[2].astype(F32)

    grid_spec = pltpu.PrefetchScalarGridSpec(
        num_scalar_prefetch=1, grid=(SUM_SPLIT,),
        in_specs=[pl.BlockSpec((1, tile, cols), lambda t, q_ref: (q_ref[0], t, 0)),
                  pl.BlockSpec((3, tile, cols), lambda t, q_ref: (0, t, 0))],
        out_specs=pl.BlockSpec((tile, cols), lambda t, q_ref: (t, 0)))
    my_chip = (2 * lax.axis_index("x") + lax.axis_index("y")).reshape(1)
    return pl.pallas_call(body, name="grad_sum_chips", grid_spec=grid_spec, out_shape=jax.ShapeDtypeStruct((rows, cols), F32),
                          compiler_params=_params("parallel"))(my_chip, chip_sum, got)


class _ReduceScatter:
    def __init__(self, name, slabs):
        self.name, self.rows = name, slabs.shape[1]
        parts = slabs.reshape(4, 2, self.rows, D_MODEL)
        self.first = _copies_start(name + "_sibling_start", [parts, lax.empty((4, self.rows, D_MODEL), slabs.dtype)],
                                   _to_sibling_copies, 4)
        self.token = self.first[3]

    def between_chips(self, after):
        parts, theirs = _copies_wait(self.name + "_sibling_wait", self.first, _to_sibling_copies, after)
        chip_sum = _sum_pairs(parts, theirs)
        self.second = _copies_start(self.name + "_chips_start", [chip_sum, lax.empty((3, self.rows, D_MODEL), chip_sum.dtype)],
                                    _to_chips_copies, 3)
        return self.second[3]

    def finish(self, after):
        chip_sum, got = _copies_wait(self.name + "_chips_wait", self.second, _to_chips_copies, after)
        return _sum_final(chip_sum, got)


def _ada_fwd(c_all, w, b):
    nb, _ = c_all.shape
    ncol = w.shape[1]

    def body(c_ref, w_ref, b_ref, o_ref):
        c = c_ref[...]
        act = (c * _sigmoid(c)).astype(BF16)
        o_ref[...] = jnp.dot(act, w_ref[...].astype(BF16), preferred_element_type=F32) + b_ref[...]

    return pl.pallas_call(body, name="ada_fwd", out_shape=jax.ShapeDtypeStruct((nb, ncol), F32),
                          compiler_params=pltpu.CompilerParams(vmem_limit_bytes=VMEM_LIMIT_BYTES))(c_all, w, b)


def _ada_wgrad(c_all_t, dmod_cols):
    d, nb = c_all_t.shape
    ncol = dmod_cols.shape[1]

    def body(ct_ref, dm_ref, o_ref):
        ct = ct_ref[...]
        act = (ct * _sigmoid(ct)).astype(BF16).astype(F32)
        dm = dm_ref[...].astype(BF16).astype(F32)
        acc = act[:, 0:1] * dm[0:1, :]
        for i in range(1, nb):
            acc = acc + act[:, i:i + 1] * dm[i:i + 1, :]
        o_ref[...] = acc

    return pl.pallas_call(body, name="ada_wgrad", out_shape=jax.ShapeDtypeStruct((d, ncol), F32),
                          compiler_params=pltpu.CompilerParams(vmem_limit_bytes=VMEM_LIMIT_BYTES))(c_all_t, dmod_cols)


SMALL_ROWS = 24


def _reduce_small(gathered):
    def body(g_ref, o_ref):
        acc = g_ref[0]
        for dev in range(1, N_DEV):
            acc = acc + g_ref[dev]
        o_ref[...] = acc

    return pl.pallas_call(body, name="reduce_small", out_shape=jax.ShapeDtypeStruct(gathered.shape[1:], F32))(gathered)


def _adamw(w, g, m, v, name):
    rows, cols = w.shape
    tile = rows
    for cand in (256, 128, 64, 32, 16, 8):
        if rows % cand == 0 and rows > cand:
            tile = cand
            break
    spec = pl.BlockSpec((tile, cols), lambda t: (t, 0))
    bc1 = 1.0 - ADAM_B1 ** ADAM_STEP
    bc2 = 1.0 - ADAM_B2 ** ADAM_STEP

    def body(w_ref, g_ref, m_ref, v_ref, d_ref, nm_ref, nv_ref):
        g_ = g_ref[...]
        nm = ADAM_B1 * m_ref[...] + (1.0 - ADAM_B1) * g_
        nv = ADAM_B2 * v_ref[...] + (1.0 - ADAM_B2) * (g_ * g_)
        d_ref[...] = -ADAM_LR * ((nm / bc1) / (jnp.sqrt(nv / bc2) + ADAM_EPS) + ADAM_WD * w_ref[...])
        nm_ref[...] = nm
        nv_ref[...] = nv

    shp = jax.ShapeDtypeStruct((rows, cols), F32)
    return pl.pallas_call(body, name=name, grid=(rows // tile,), in_specs=[spec] * 4, out_specs=[spec] * 3, out_shape=[shp] * 3,
                          compiler_params=_params("parallel"))(w, g, m, v)


_WEIGHTS = ("w_ada", "b_ada", "w_in", "sinks", "w_branch_a", "w_branch_b", "w_o", "ln1_g", "ln1_b", "w_gate_up", "w_down",
            "ln2_g", "ln2_b")
_TRANSPOSED = ("w_in", "w_branch_b", "w_gate_up")


def _pack_shard(name, w):
    w = w.astype(BF16)
    if name in _TRANSPOSED:
        w = w.T
    return w.reshape(-1, D_MODEL)


def _unpack_full(name, slab):
    if name == "w_branch_b":
        return slab.reshape(N_DEV * 128, 512)
    return slab.reshape(-1, D_MODEL)


def _unpack_group(group, gathered):
    return {n: _unpack_full(n, slab) for (n, _), slab in zip(group, gathered)}


def _unpack_grads(group, g_packed):
    g_w, off = {}, 0
    for n, r in group:
        part = g_packed[off:off + r]
        off += r
        g_w[n] = part.reshape(128, 512) if n == "w_branch_b" else part
    return g_w


def kernel(x, c, positions, w_ada, b_ada, w_in, sinks, w_branch_a, w_branch_b, w_o, ln1_g, ln1_b, w_gate_up, w_down, ln2_g, ln2_b, loss_target, m_w_ada, m_b_ada, m_w_in, m_sinks, m_w_branch_a, m_w_branch_b, m_w_o, m_ln1_g, m_ln1_b, m_w_gate_up, m_w_down, m_ln2_g, m_ln2_b, v_w_ada, v_b_ada, v_w_in, v_sinks, v_w_branch_a, v_w_branch_b, v_w_o, v_ln1_g, v_ln1_b, v_w_gate_up, v_w_down, v_ln2_g, v_ln2_b):
    weights = dict(w_ada=w_ada, b_ada=b_ada, w_in=w_in, sinks=sinks, w_branch_a=w_branch_a, w_branch_b=w_branch_b, w_o=w_o,
                   ln1_g=ln1_g, ln1_b=ln1_b, w_gate_up=w_gate_up, w_down=w_down, ln2_g=ln2_g, ln2_b=ln2_b)
    m_in = dict(w_ada=m_w_ada, b_ada=m_b_ada, w_in=m_w_in, sinks=m_sinks, w_branch_a=m_w_branch_a, w_branch_b=m_w_branch_b,
                w_o=m_w_o, ln1_g=m_ln1_g, ln1_b=m_ln1_b, w_gate_up=m_w_gate_up, w_down=m_w_down, ln2_g=m_ln2_g, ln2_b=m_ln2_b)
    v_in = dict(w_ada=v_w_ada, b_ada=v_b_ada, w_in=v_w_in, sinks=v_sinks, w_branch_a=v_w_branch_a, w_branch_b=v_w_branch_b,
                w_o=v_w_o, ln1_g=v_ln1_g, ln1_b=v_ln1_b, w_gate_up=v_w_gate_up, w_down=v_w_down, ln2_g=v_ln2_g, ln2_b=v_ln2_b)
    bsz = x.shape[0]
    me = _index(_my_place())
    ada_cols = w_ada.shape[2]
    outs = {}

    def adamw(n, g):
        w2, m2, v2 = (t[n][0] if t[n].ndim == 3 else t[n] for t in (weights, m_in, v_in))
        shape = weights[n].shape
        if n in _TRANSPOSED:
            dlt, nm, nv = _adamw(w2.T, g, m2.T, v2.T, "adamw_" + n)
            outs[n] = tuple(t.T.reshape(shape) for t in (g, dlt, nm, nv))
        else:
            dlt, nm, nv = _adamw(w2, g, m2, v2, "adamw_" + n)
            outs[n] = tuple(t.reshape(shape) for t in (g, dlt, nm, nv))
        return nv

    packed_in = [_pack_shard(n, weights[n][0]) for n, _ in GROUP_IN]
    packed_rest = [_pack_shard(n, weights[n][0]) for n, _ in GROUP_REST]
    c_all = _gather_small(jnp.pad(c, ((0, 8 - bsz), (0, 0))), "gather_c")[:, :bsz].reshape(N_DEV * bsz, D_MODEL)
    gather_in = _Gather("gather_w_in", lax.optimization_barrier((packed_in, c_all))[0])
    b_cols = lax.dynamic_slice_in_dim(b_ada, me * ada_cols, ada_cols, axis=1)
    mod_cols = _ada_fwd(c_all, w_ada[0], b_cols + gather_in.token[0, 0])
    mod_all = _gather_small(mod_cols, "gather_mod").transpose(1, 0, 2).reshape(N_DEV * bsz, 6, D_MODEL)
    gather_rest = _Gather("gather_rest", lax.optimization_barrier((packed_rest, mod_all))[0])
    mod = jnp.pad(lax.dynamic_slice_in_dim(mod_all, me * bsz, bsz, axis=0), ((0, 0), (0, 2), (0, 0)))
    mod = mod + gather_rest.token[0, 0]
    mod = mod + gather_in.pass_on(mod)[0, 0]

    scatters = {}

    def get_w_in(after):
        return _unpack_group(GROUP_IN, gather_in.finish(after))["w_in"]

    def get_rest(after):
        return _unpack_group(GROUP_REST, gather_rest.finish(after))

    def pack_grads(group, grads):
        return jnp.concatenate([grads[n].reshape(N_DEV, r, D_MODEL) for n, r in group], axis=1)

    def hook(point, value):
        if point == "projected":
            return gather_rest.pass_on(value)
        if point == "grads_rest":
            scatters["rest"] = _ReduceScatter("scatter_rest", pack_grads(GROUP_REST, value))
            return scatters["rest"].token
        if point == "delta_done":
            return scatters["rest"].between_chips(value)
        if point == "grads_w_in":
            scatters["in"] = _ReduceScatter("scatter_w_in", pack_grads(GROUP_IN, value))
            return scatters["in"].token
        if point == "dgrad_done":
            tok = scatters["in"].between_chips(value)
            for n, g in _unpack_grads(GROUP_REST, scatters["rest"].finish(tok)).items():
                adamw(n, g)
            return tok
        raise ValueError(point)

    loss_part, grad_x, dmod, small = _layer_step(x, mod, positions, sinks[0], ln1_g, ln1_b, ln2_g, ln2_b, loss_target,
                                                 get_w_in, get_rest, hook)

    rows = jnp.concatenate([dmod.reshape(bsz * 6, D_MODEL), small, jnp.full((1, D_MODEL), loss_part, F32),
                            jnp.zeros((SMALL_ROWS - bsz * 6 - 6, D_MODEL), F32)], axis=0)
    small_all = _gather_small(rows, "gather_small")
    sums = _reduce_small(small_all)
    loss = sums[bsz * 6 + 5, 0]
    dmod_all = small_all[:, :bsz * 6].reshape(N_DEV * bsz, 6 * D_MODEL)
    adamw("b_ada", functools.reduce(jnp.add, [sums[6 * i:6 * i + 6] for i in range(bsz)]).reshape(1, 6 * D_MODEL))
    for i, n in enumerate(("ln1_g", "ln1_b", "ln2_g", "ln2_b")):
        adamw(n, sums[12 + i][None])
    adamw("sinks", sums[16][::HEAD_DIM][None])
    dmod_cols = lax.dynamic_slice_in_dim(dmod_all, me * ada_cols, ada_cols, axis=1)
    last = adamw("w_ada", _ada_wgrad(c_all.T, dmod_cols))
    for n, g in _unpack_grads(GROUP_IN, scatters["in"].finish(last)).items():
        adamw(n, g)

    return (loss, grad_x, *[outs[n][0] for n in _WEIGHTS], *[outs[n][1] for n in _WEIGHTS], *[outs[n][2] for n in _WEIGHTS],
            *[outs[n][3] for n in _WEIGHTS])
```

```python
import functools

import jax
import jax.numpy as jnp
from jax import lax
from jax.experimental import pallas as pl
from jax.experimental.pallas import tpu as pltpu

F32 = jnp.float32
BF16 = jnp.bfloat16

D_MODEL = 1024
HEAD_DIM = 64
A_Q_HEADS = 16
A_WINDOW = 128
B_PATTERNS = ((128, 1), (512, 4), (2048, 16))
B_HEADS_PER_GROUP = 8
D_FF = 2816
QBLOCK = 128
ROPE_THETA = 10000.0
LN_EPS = 1e-5
DEEPNORM_ALPHA = 2.0 ** 0.25
NEG_INF = -1e30
ADAM_LR, ADAM_B1, ADAM_B2, ADAM_EPS, ADAM_WD, ADAM_STEP = 0.001, 0.9, 0.999, 1e-08, 0.01, 10

N_DEV = 8
MESH_AXES = ("x", "y", "c")
LANES = 128
VMEM_LIMIT_BYTES = 56 * 1024 * 1024
MESH = pl.DeviceIdType.MESH

OFF_QA, OFF_KVA, OFF_QKVB, OFF_GAB = 0, 1024, 1280, 5888
GROUP_IN = (("w_in", 992),)
GROUP_REST = (("w_branch_a", 128), ("w_branch_b", 64), ("w_o", 128), ("w_gate_up", 704), ("w_down", 352))


def _params(*sem):
    return pltpu.CompilerParams(dimension_semantics=sem, vmem_limit_bytes=VMEM_LIMIT_BYTES)


def _sigmoid(x):
    return 1.0 / (1.0 + jnp.exp(-x))


_DIMS = {"nn": (((1,), (0,)), ((), ())), "nt": (((1,), (1,)), ((), ())), "tn": (((0,), (0,)), ((), ()))}


def _matmul(a, b, *, mode, tm, tn, tk, name, out_dtype=None, n=None, b_off=0, token=None, ins=(), outs=None, epilogue=None):
    if mode == "nn":
        (m, k), nn_ = a.shape, b.shape[1]
    elif mode == "nt":
        (m, k), nn_ = a.shape, (b.shape[0] if n is None else n)
    else:
        (k, m), nn_ = a.shape, b.shape[1]
    assert m % tm == 0 and nn_ % tn == 0 and k % tk == 0 and b_off % tn == 0, (name, m, nn_, k)
    nk = k // tk
    joff = b_off // tn
    if mode == "nn":
        a_spec = pl.BlockSpec((tm, tk), lambda i, j, kk: (i, kk))
        b_spec = pl.BlockSpec((tk, tn), lambda i, j, kk: (kk, j))
    elif mode == "nt":
        a_spec = pl.BlockSpec((tm, tk), lambda i, j, kk: (i, kk))
        b_spec = pl.BlockSpec((tn, tk), lambda i, j, kk: (j + joff, kk))
    else:
        a_spec = pl.BlockSpec((tk, tm), lambda i, j, kk: (kk, i))
        b_spec = pl.BlockSpec((tk, tn), lambda i, j, kk: (kk, j))
    dims = _DIMS[mode]
    has_token = token is not None
    plain = epilogue is None
    if plain:
        outs = [(jax.ShapeDtypeStruct((m, nn_), out_dtype), (tm, tn), lambda i, j: (i, j))]

        def epilogue(acc, i, j, in_refs, out_refs):
            out_refs[0][...] = acc.astype(out_refs[0].dtype)

    nin = len(ins)

    def body(*refs):
        a_ref, b_ref = refs[:2]
        in_refs = refs[2:2 + nin]
        out_refs = refs[2 + nin + has_token:-1]
        acc_ref = refs[-1]
        kk = pl.program_id(2)
        part = lax.dot_general(a_ref[...].astype(BF16), b_ref[...].astype(BF16), dims, preferred_element_type=F32)

        def finish(acc):
            epilogue(acc, pl.program_id(0), pl.program_id(1), in_refs, out_refs)

        if nk == 1:
            finish(part)
        else:
            @pl.when(kk == 0)
            def _():
                acc_ref[...] = part

            @pl.when(kk > 0)
            def _():
                acc_ref[...] += part

            @pl.when(kk == nk - 1)
            def _():
                finish(acc_ref[...])

    def spec(block, index):
        return pl.BlockSpec(block, lambda i, j, kk: index(i, j))

    in_specs, args = [a_spec, b_spec], [a, b]
    for arr, block, index in ins:
        in_specs.append(spec(block, index))
        args.append(arr)
    if has_token:
        in_specs.append(pl.BlockSpec(token.shape, lambda i, j, kk: (0, 0)))
        args.append(token)
    res = pl.pallas_call(
        body,
        name=name,
        grid=(m // tm, nn_ // tn, nk),
        in_specs=in_specs,
        out_specs=[spec(block, index) for _, block, index in outs],
        out_shape=[shape for shape, _, _ in outs],
        scratch_shapes=[pltpu.VMEM((tm, tn) if nk > 1 else (8, LANES), F32)],
        compiler_params=_params("arbitrary", "arbitrary", "arbitrary"),
    )(*args)
    return res[0] if plain else res


def _proj_rope(a, bt, cos, sin, *, n, b_off, rope_cols, tm, tn, name, out_dtype=F32):
    m, k = a.shape
    assert m % tm == 0 and n % tn == 0 and b_off % tn == 0 and rope_cols % tn == 0, name
    joff = b_off // tn
    nrope = rope_cols // tn

    def body(a_ref, b_ref, c_ref, s_ref, o_ref):
        acc = lax.dot_general(a_ref[...], b_ref[...], _DIMS["nt"], preferred_element_type=F32)
        j = pl.program_id(1)

        @pl.when(j < nrope)
        def _():
            o_ref[...] = _rope(acc, c_ref[...], s_ref[...]).astype(o_ref.dtype)

        @pl.when(j >= nrope)
        def _():
            o_ref[...] = acc.astype(o_ref.dtype)

    table = pl.BlockSpec((tm, LANES), lambda i, j: (i, 0))
    return pl.pallas_call(
        body,
        name=name,
        grid=(m // tm, n // tn),
        in_specs=[pl.BlockSpec((tm, k), lambda i, j: (i, 0)), pl.BlockSpec((tn, k), lambda i, j: (j + joff, 0)), table, table],
        out_specs=pl.BlockSpec((tm, tn), lambda i, j: (i, j)),
        out_shape=jax.ShapeDtypeStruct((m, n), out_dtype),
        compiler_params=_params("parallel", "parallel"),
    )(a, bt, cos, sin)


ROW_TILE = 256


def _rows(width, col=0):
    return pl.BlockSpec((1, ROW_TILE, width), lambda b, t: (b, t, col))


def _per_batch(nrows, width):
    return pl.BlockSpec((1, nrows, width), lambda b, t: (b, 0, 0))


def _whole(shape):
    return pl.BlockSpec(shape, lambda b, t: (0,) * len(shape))


def _row_call(body, name, bsz, seq, in_specs, out_specs, out_shape, accumulates=False):
    return pl.pallas_call(
        body,
        name=name,
        grid=(bsz, seq // ROW_TILE),
        in_specs=in_specs,
        out_specs=out_specs,
        out_shape=out_shape,
        compiler_params=_params("parallel", "arbitrary" if accumulates else "parallel"),
    )


def _acc_rows(acc_ref, first, rows):
    @pl.when(first)
    def _():
        acc_ref[...] = jnp.zeros_like(acc_ref)

    for r, val in enumerate(rows):
        acc_ref[0, r:r + 1, :] += val


def _colsum(v):
    return jnp.sum(v, axis=0, keepdims=True)


def _ln_stats(z):
    mu = jnp.mean(z, axis=-1, keepdims=True)
    zc = z - mu
    var = jnp.mean(zc * zc, axis=-1, keepdims=True)
    rstd = lax.rsqrt(var + LN_EPS)
    return zc * rstd, rstd


def _ln_bwd(dxhat, xhat, rstd):
    m1 = jnp.mean(dxhat, axis=-1, keepdims=True)
    m2 = jnp.mean(dxhat * xhat, axis=-1, keepdims=True)
    return rstd * (dxhat - m1 - xhat * m2)


def _modulate_in(x, mod):
    bsz, seq, d = x.shape

    def body(x_ref, mod_ref, u_ref):
        u_ref[0] = (x_ref[0] * (1.0 + mod_ref[0, 1:2, :]) + mod_ref[0, 0:1, :]).astype(BF16)

    return _row_call(body, "modulate_in", bsz, seq, [_rows(d), _per_batch(8, d)], _rows(d),
                     jax.ShapeDtypeStruct((bsz, seq, d), BF16))(x, mod)


def _gate_merge(gab, ya, yb):
    bsz, seq, d = ya.shape

    def body(ga_ref, gb_ref, ya_ref, yb_ref, o_ref):
        ga, gb, ya_, yb_ = (r[0].astype(F32) for r in (ga_ref, gb_ref, ya_ref, yb_ref))
        o_ref[0] = (_sigmoid(ga) * ya_ + _sigmoid(gb) * yb_).astype(BF16)

    return _row_call(body, "gate_merge", bsz, seq, [_rows(d, 0), _rows(d, 1), _rows(d), _rows(d)], _rows(d),
                     jax.ShapeDtypeStruct((bsz, seq, d), BF16))(gab, gab, ya, yb)


EP_TILE = 512


def _ep_specs(seq, d):
    tiles = seq // EP_TILE
    return ((EP_TILE, d), lambda i, j: (i, 0)), ((1, 8, d), lambda i, j: (i // tiles, 0, 0)), ((1, d), lambda i, j: (0, 0))


def _wo_ln1(merged, wo, x, mod, g, b, seq):
    ntok, d = x.shape
    row, per_b, whole = _ep_specs(seq, d)

    def epilogue(y, i, j, ins, outs):
        x_ref, mod_ref, g_ref, b_ref = ins
        y_ref, h_ref, u_ref = outs
        z = DEEPNORM_ALPHA * x_ref[...] + (1.0 + mod_ref[0, 2:3, :]) * y
        xhat, _ = _ln_stats(z)
        h = xhat * g_ref[...] + b_ref[...]
        y_ref[...] = y
        h_ref[...] = h
        u_ref[...] = (h * (1.0 + mod_ref[0, 4:5, :]) + mod_ref[0, 3:4, :]).astype(BF16)

    f32, bf16 = jax.ShapeDtypeStruct((ntok, d), F32), jax.ShapeDtypeStruct((ntok, d), BF16)
    return _matmul(merged, wo, mode="nn", tm=EP_TILE, tn=d, tk=d, name="w_o_ln1",
                   ins=[(x,) + row, (mod,) + per_b, (g,) + whole, (b,) + whole],
                   outs=[(f32,) + row, (f32,) + row, (bf16,) + row], epilogue=epilogue)


def _silu_mul(h):
    bsz, seq, _ = h.shape

    def body(hg_ref, hu_ref, a_ref):
        hg = hg_ref[0].astype(F32)
        a_ref[0] = (hg * _sigmoid(hg) * hu_ref[0].astype(F32)).astype(BF16)

    return _row_call(body, "silu_mul", bsz, seq, [_rows(D_FF, 0), _rows(D_FF, 1)], _rows(D_FF),
                     jax.ShapeDtypeStruct((bsz, seq, D_FF), BF16))(h, h)


def _down_ln2_loss_bwd(a, wd, h1, mod, g, b, target, seq):
    ntok, d = h1.shape
    row, per_b, whole = _ep_specs(seq, d)
    tiles = seq // EP_TILE

    def epilogue(y, i, j, ins, outs):
        h_ref, mod_ref, g_ref, b_ref, t_ref = ins
        dy_ref, dh_ref, acc_ref = outs
        gate = 1.0 + mod_ref[0, 5:6, :]
        z = DEEPNORM_ALPHA * h_ref[...] + gate * y
        xhat, rstd = _ln_stats(z)
        diff = xhat * g_ref[...] + b_ref[...] - t_ref[...]
        loss = 0.5 * jnp.sum(jnp.sum(diff * diff, axis=-1, keepdims=True) / d, axis=0, keepdims=True)
        dout = diff / d
        dz = _ln_bwd(dout * g_ref[...], xhat, rstd)
        dy_ref[...] = (gate * dz).astype(BF16)
        dh_ref[...] = DEEPNORM_ALPHA * dz
        _acc_rows(acc_ref, i % tiles == 0,
                  [_colsum(dout * xhat), _colsum(dout), _colsum(dz * y), jnp.broadcast_to(loss, (1, d))])

    return _matmul(a, wd, mode="nn", tm=EP_TILE, tn=d, tk=a.shape[1], name="down_ln2_loss_bwd",
                   ins=[(h1,) + row, (mod,) + per_b, (g,) + whole, (b,) + whole, (target,) + row],
                   outs=[(jax.ShapeDtypeStruct((ntok, d), BF16),) + row, (jax.ShapeDtypeStruct((ntok, d), F32),) + row,
                         (jax.ShapeDtypeStruct((ntok // seq, 8, d), F32),) + per_b], epilogue=epilogue)


def _silu_mul_bwd(da, h):
    bsz, seq, _ = h.shape

    def body(da_ref, hg_ref, hu_ref, dh_ref):
        hg, da_ = hg_ref[0].astype(F32), da_ref[0].astype(F32)
        sg = _sigmoid(hg)
        dh_ref[0, :, :D_FF] = (da_ * hu_ref[0].astype(F32) * (sg * (1.0 + hg * (1.0 - sg)))).astype(BF16)
        dh_ref[0, :, D_FF:] = (da_ * (hg * sg)).astype(BF16)

    return _row_call(body, "silu_mul_bwd", bsz, seq, [_rows(D_FF), _rows(D_FF, 0), _rows(D_FF, 1)], _rows(2 * D_FF),
                     jax.ShapeDtypeStruct((bsz, seq, 2 * D_FF), BF16))(da, h, h)


def _gate_up_dgrad_ln1_bwd(dh, wgut, dh1a, x, y1, mod, g, b, seq):
    ntok, d = x.shape
    row, per_b, whole = _ep_specs(seq, d)
    tiles = seq // EP_TILE

    def epilogue(du, i, j, ins, outs):
        dh_ref, x_ref, y_ref, mod_ref, g_ref, b_ref = ins
        dy_ref, dx_ref, acc_ref = outs
        y = y_ref[...]
        gate = 1.0 + mod_ref[0, 2:3, :]
        z = DEEPNORM_ALPHA * x_ref[...] + gate * y
        xhat, rstd = _ln_stats(z)
        h1 = xhat * g_ref[...] + b_ref[...]
        dh1 = dh_ref[...] + du * (1.0 + mod_ref[0, 4:5, :])
        dz = _ln_bwd(dh1 * g_ref[...], xhat, rstd)
        dy_ref[...] = (gate * dz).astype(BF16)
        dx_ref[...] = DEEPNORM_ALPHA * dz
        _acc_rows(acc_ref, i % tiles == 0,
                  [_colsum(dh1 * xhat), _colsum(dh1), _colsum(dz * y), _colsum(du * h1), _colsum(du)])

    return _matmul(dh, wgut, mode="nn", tm=EP_TILE, tn=d, tk=D_FF, name="gate_up_dgrad_ln1_bwd",
                   ins=[(dh1a,) + row, (x,) + row, (y1,) + row, (mod,) + per_b, (g,) + whole, (b,) + whole],
                   outs=[(jax.ShapeDtypeStruct((ntok, d), BF16),) + row, (jax.ShapeDtypeStruct((ntok, d), F32),) + row,
                         (jax.ShapeDtypeStruct((ntok // seq, 8, d), F32),) + per_b], epilogue=epilogue)


def _wo_dgrad_gate_bwd(dy1, wo, gab, ya, yb):
    ntok, d = ya.shape
    tm, tn = 1024, 512
    tile = ((tm, tn), lambda i, j: (i, j))
    tile_b = ((tm, tn), lambda i, j: (i, j + d // tn))

    def epilogue(dm_, i, j, ins, outs):
        ga_ref, gb_ref, ya_ref, yb_ref = ins
        dya_ref, dyb_ref, dga_ref, dgb_ref = outs
        sa, sb = _sigmoid(ga_ref[...].astype(F32)), _sigmoid(gb_ref[...].astype(F32))
        dya_ref[...] = (dm_ * sa).astype(BF16)
        dyb_ref[...] = (dm_ * sb).astype(BF16)
        dga_ref[...] = (dm_ * ya_ref[...].astype(F32) * sa * (1.0 - sa)).astype(BF16)
        dgb_ref[...] = (dm_ * yb_ref[...].astype(F32) * sb * (1.0 - sb)).astype(BF16)

    shp = jax.ShapeDtypeStruct((ntok, d), BF16)
    return _matmul(dy1, wo, mode="nt", tm=tm, tn=tn, tk=d, name="w_o_dgrad_gate_bwd",
                   ins=[(gab,) + tile, (gab,) + tile_b, (ya,) + tile, (yb,) + tile],
                   outs=[(shp,) + tile] * 4, epilogue=epilogue)


def _w_in_dgrad_grad_x(dproj, wint, dxa, x, mod, seq, token):
    ntok, d = x.shape
    row, per_b, _ = _ep_specs(seq, d)
    tiles = seq // EP_TILE

    def epilogue(du, i, j, ins, outs):
        dxa_ref, x_ref, mod_ref = ins
        gx_ref, acc_ref = outs
        gx_ref[...] = dxa_ref[...] + du * (1.0 + mod_ref[0, 1:2, :])
        _acc_rows(acc_ref, i % tiles == 0, [_colsum(du * x_ref[...]), _colsum(du)])

    return _matmul(dproj, wint, mode="nn", tm=EP_TILE, tn=d, tk=wint.shape[0] // 2, name="w_in_dgrad_grad_x", token=token,
                   ins=[(dxa,) + row, (x,) + row, (mod,) + per_b],
                   outs=[(jax.ShapeDtypeStruct((ntok, d), F32),) + row, (jax.ShapeDtypeStruct((ntok // seq, 8, d), F32),) + per_b],
                   epilogue=epilogue)


def _branch_b_gate_merge(ob, wbbt, gab, ya):
    ntok, d = ya.shape
    tm, tn = 1024, 512
    tile = ((tm, tn), lambda i, j: (i, j))
    tile_b = ((tm, tn), lambda i, j: (i, j + d // tn))

    def epilogue(yb, i, j, ins, outs):
        ga_ref, gb_ref, ya_ref = ins
        yb_ref, merged_ref = outs
        yb_ref[...] = yb.astype(BF16)
        merged_ref[...] = (_sigmoid(ga_ref[...].astype(F32)) * ya_ref[...].astype(F32)
                           + _sigmoid(gb_ref[...].astype(F32)) * yb).astype(BF16)

    shp = jax.ShapeDtypeStruct((ntok, d), BF16)
    return _matmul(ob, wbbt, mode="nt", tm=tm, tn=tn, tk=ob.shape[1], name="branch_b_gate_merge",
                   ins=[(gab,) + tile, (gab,) + tile_b, (ya,) + tile], outs=[(shp,) + tile] * 2, epilogue=epilogue)


def _segsum64(v):
    rows, width = v.shape
    ri = lax.broadcasted_iota(jnp.int32, (LANES, LANES), 0) // HEAD_DIM
    ci = lax.broadcasted_iota(jnp.int32, (LANES, LANES), 1) // HEAD_DIM
    ones = jnp.where(ri == ci, 1.0, 0.0).astype(BF16)
    out = []
    for c in range(width // LANES):
        part = v[:, c * LANES:(c + 1) * LANES]
        hi = part.astype(BF16)
        lo = (part - hi.astype(F32)).astype(BF16)
        out.append(jnp.dot(hi, ones, preferred_element_type=F32) + jnp.dot(lo, ones, preferred_element_type=F32))
    return jnp.concatenate(out, axis=1) if len(out) > 1 else out[0]


def _merge_b(os_, ls_):
    bsz, seq, w = os_[0].shape

    def body(o0, o1, o2, l0, l1, l2, ob_ref):
        ls = [l0[0], l1[0], l2[0]]
        mx = jnp.maximum(jnp.maximum(ls[0], ls[1]), ls[2])
        es = [jnp.exp(l - mx) for l in ls]
        den = es[0] + es[1] + es[2]
        ob_ref[0] = ((es[0] / den) * o0[0] + (es[1] / den) * o1[0] + (es[2] / den) * o2[0]).astype(BF16)

    return _row_call(body, "merge_b", bsz, seq, [_rows(w)] * 6, _rows(w),
                     jax.ShapeDtypeStruct((bsz, seq, w), BF16))(*os_, *ls_)


def _branch_b_dgrad_merge_bwd(dyb, wbbt, os_, ls_):
    ntok, w = os_[0].shape
    row = ((EP_TILE, w), lambda i, j: (i, 0))

    def epilogue(dob_, i, j, ins, outs):
        os_r, ls_r = ins[:3], ins[3:]
        do_r, dd_r = outs[:3], outs[3:]
        ls = [l[...] for l in ls_r]
        mx = jnp.maximum(jnp.maximum(ls[0], ls[1]), ls[2])
        es = [jnp.exp(l - mx) for l in ls]
        den = es[0] + es[1] + es[2]
        ws = [e / den for e in es]
        dws = [_segsum64(dob_ * o[...]) for o in os_r]
        mean = ws[0] * dws[0] + ws[1] * dws[1] + ws[2] * dws[2]
        for wg, do_ref, dd_ref in zip(ws, do_r, dd_r):
            do_ref[...] = wg * dob_
            dd_ref[...] = -wg * mean

    shp = jax.ShapeDtypeStruct((ntok, w), F32)
    return _matmul(dyb, wbbt, mode="nn", tm=EP_TILE, tn=w, tk=dyb.shape[1], name="branch_b_dgrad_merge_bwd",
                   ins=[(v,) + row for v in list(os_) + list(ls_)], outs=[(shp,) + row] * 6, epilogue=epilogue)


def _branch_a_dgrad_delta(dya, wba, oa, lse_a, sinks_exp, seq):
    ntok, w = oa.shape
    row, per_b, whole = _ep_specs(seq, w)
    tiles = seq // EP_TILE

    def epilogue(do_, i, j, ins, outs):
        o_ref, l_ref, s_ref = ins
        do_ref, dd_ref, acc_ref = outs
        dd = -_segsum64(do_ * o_ref[...])
        do_ref[...] = do_
        dd_ref[...] = dd
        _acc_rows(acc_ref, i % tiles == 0, [_colsum(dd * jnp.exp(s_ref[...] - l_ref[...]))])

    shp = jax.ShapeDtypeStruct((ntok, w), F32)
    return _matmul(dya, wba, mode="nt", tm=EP_TILE, tn=w, tk=dya.shape[1], name="branch_a_dgrad_delta",
                   ins=[(oa,) + row, (lse_a,) + row, (sinks_exp,) + whole],
                   outs=[(shp,) + row, (shp,) + row, (jax.ShapeDtypeStruct((ntok // seq, 8, w), F32),) + per_b],
                   epilogue=epilogue)


def _swap_halves(v):
    src = lax.broadcasted_iota(jnp.int32, (LANES, LANES), 0)
    dst = lax.broadcasted_iota(jnp.int32, (LANES, LANES), 1)
    partner = jnp.where((dst % HEAD_DIM) < HEAD_DIM // 2, dst + HEAD_DIM // 2, dst - HEAD_DIM // 2)
    perm = jnp.where(src == partner, 1.0, 0.0).astype(BF16)
    hi = v.astype(BF16)
    lo = (v - hi.astype(F32)).astype(BF16)
    return jnp.dot(hi, perm, preferred_element_type=F32) + jnp.dot(lo, perm, preferred_element_type=F32)


def _swap_halves_roll(v):
    lane = lax.broadcasted_iota(jnp.int32, v.shape, 1)
    return jnp.where((lane % HEAD_DIM) < HEAD_DIM // 2, pltpu.roll(v, LANES - HEAD_DIM // 2, 1),
                     pltpu.roll(v, HEAD_DIM // 2, 1))


def _rope(v, cos, sin, sign=1.0, mxu=True):
    swap = _swap_halves if mxu else _swap_halves_roll
    out = []
    for c in range(v.shape[1] // LANES):
        part = v[:, c * LANES:(c + 1) * LANES]
        out.append(part * cos + sign * (swap(part) * sin))
    return jnp.concatenate(out, axis=1) if len(out) > 1 else out[0]


def _half_mask(shape, half):
    lane = lax.broadcasted_iota(jnp.int32, shape, len(shape) - 1) % LANES
    return (lane < HEAD_DIM) if half == 0 else (lane >= HEAD_DIM)


def _dup_half(v, half):
    return jnp.where(_half_mask(v.shape, half), v, pltpu.roll(v, HEAD_DIM, 1))


def _fold_halves(v):
    return v + pltpu.roll(v, HEAD_DIM, 1)


def _pick_halves(lo_rows, hi_rows):
    return jnp.where(_half_mask(lo_rows.shape, 0), lo_rows, hi_rows)


def _stack_masked(v, pairs):
    parts = []
    for c in pairs:
        pair = v[:, c * LANES:(c + 1) * LANES]
        parts += [jnp.where(_half_mask(pair.shape, half), pair, 0.0) for half in (0, 1)]
    return jnp.concatenate(parts, axis=0)


def _stack_pair_cols(v, pairs):
    return jnp.concatenate([v[:, c * LANES + half * HEAD_DIM:c * LANES + half * HEAD_DIM + 1] for c in pairs for half in (0, 1)],
                           axis=0)


ATTN_UNITS = 16


def _class_rows(r):
    return [pl.ds(0, QBLOCK)] if r == 1 else [pl.ds(rho, QBLOCK, stride=r) for rho in range(r)]


def _band_mask(nrows, nk, blk, n_back, has_prev):
    qi = lax.broadcasted_iota(jnp.int32, (nrows, nk), 0) % QBLOCK
    ki = lax.broadcasted_iota(jnp.int32, (nrows, nk), 1)
    if has_prev:
        dist = qi + QBLOCK - ki
        return (dist >= 0) & (dist <= n_back) & ((ki >= QBLOCK) | (blk > 0))
    dist = qi - ki
    return (dist >= 0) & (dist <= n_back)


def _attn_fwd(q_arr, k_arr, v_arr, *, name, npair, gqa, q_col, k_col, v_col, nchunk, r, n_back, sinks=None):
    bsz, seq, _ = q_arr.shape
    rr = QBLOCK * r
    nblk = seq // rr
    qw = npair * LANES
    kw = LANES if gqa else qw
    has_prev = nblk > 1
    has_sink = sinks is not None
    scale = HEAD_DIM ** -0.5

    def body(*refs):
        refs = list(refs)
        q_ref, kc_ref, vc_ref = refs[:3]
        pos = 3
        if has_prev:
            kp_ref, vp_ref = refs[pos:pos + 2]
            pos += 2
        if has_sink:
            sink_ref = refs[pos]
            pos += 1
        o_ref, lse_ref = refs[pos:pos + 2]
        blk = pl.program_id(2)
        nk = (2 if has_prev else 1) * QBLOCK
        valid = _band_mask(QBLOCK, nk, blk, n_back, has_prev)
        per = npair // 2
        classes = _class_rows(r)
        step = max(1, ATTN_UNITS // (2 * npair))
        for first in range(0, len(classes), step):
            batch = classes[first:first + step]
            units = []
            for ci, rows in enumerate(batch):
                q = q_ref[0, rows, :] * scale
                k, v = kc_ref[0, rows, :], vc_ref[0, rows, :]
                if has_prev:
                    k = jnp.concatenate([kp_ref[0, rows, :], k], axis=0)
                    v = jnp.concatenate([vp_ref[0, rows, :], v], axis=0)
                if gqa:
                    kdup = [_dup_half(k, hk).astype(BF16) for hk in range(2)]
                    vdup = [_dup_half(v, hk) for hk in range(2)]
                for c in range(npair):
                    sl = slice(c * LANES, (c + 1) * LANES)
                    qc = q[:, sl]
                    kc, vc = (kdup[c // per], vdup[c // per]) if gqa else (k[:, sl].astype(BF16), v[:, sl])
                    for half in (0, 1):
                        qm = jnp.where(_half_mask(qc.shape, half), qc, 0.0).astype(BF16)
                        vm = jnp.where(_half_mask(vc.shape, half), vc, 0.0).astype(BF16)
                        s = lax.dot_general(qm, kc, _DIMS["nt"], preferred_element_type=F32)
                        units.append(dict(ci=ci, c=c, half=half, s=s, vm=vm, sk=sink_ref[2 * c + half] if has_sink else None))
            for u in units:
                s = jnp.where(valid, u["s"], NEG_INF)
                m = jnp.max(s, axis=1, keepdims=True)
                if has_sink:
                    m = jnp.maximum(m, u["sk"])
                p = jnp.exp(s - m)
                den = jnp.sum(p, axis=1, keepdims=True)
                if has_sink:
                    den = den + jnp.exp(u["sk"] - m)
                u.update(p=p.astype(BF16), den=den, lse=m + jnp.log(den))
            for u in units:
                u["o"] = jnp.dot(u["p"], u["vm"], preferred_element_type=F32) / u["den"]
            for ci, rows in enumerate(batch):
                outs, lses = [None] * npair, [None] * npair
                for u in units:
                    if u["ci"] != ci:
                        continue
                    c, o = u["c"], u["o"]
                    lse = jnp.broadcast_to(u["lse"], o.shape)
                    outs[c] = o if u["half"] == 0 else outs[c] + o
                    lses[c] = lse if u["half"] == 0 else _pick_halves(lses[c], lse)
                o_ref[0, rows, :] = jnp.concatenate(outs, axis=1) if npair > 1 else outs[0]
                lse_ref[0, rows, :] = jnp.concatenate(lses, axis=1) if npair > 1 else lses[0]

    def cur(width, col0):
        return pl.BlockSpec((1, rr, width), lambda b, c, i: (b, i, col0 + c))

    def prev(width, col0):
        return pl.BlockSpec((1, rr, width), lambda b, c, i: (b, jnp.maximum(i - 1, 0), col0 + c))

    in_specs = [cur(qw, q_col), cur(kw, k_col), cur(kw, v_col)]
    args = [q_arr, k_arr, v_arr]
    if has_prev:
        in_specs += [prev(kw, k_col), prev(kw, v_col)]
        args += [k_arr, v_arr]
    if has_sink:
        in_specs.append(pl.BlockSpec(memory_space=pltpu.SMEM))
        args.append(sinks)
    return pl.pallas_call(
        body,
        name=name,
        grid=(bsz, nchunk, nblk),
        in_specs=in_specs,
        out_specs=[pl.BlockSpec((1, rr, qw), lambda b, c, i: (b, i, c))] * 2,
        out_shape=[jax.ShapeDtypeStruct((bsz, seq, nchunk * qw), F32)] * 2,
        compiler_params=_params("parallel", "parallel", "parallel"),
    )(*args)


def _attn_bwd(q_arr, k_arr, v_arr, cos, sin, do, lse, dd, *, name, npair, gqa, q_col, k_col, v_col, nchunk, r, n_back,
              token=None):
    bsz, seq, _ = q_arr.shape
    rr = QBLOCK * r
    nblk = seq // rr
    qw = npair * LANES
    kw = LANES if gqa else qw
    has_next = nblk > 1
    has_token = token is not None
    scale = HEAD_DIM ** -0.5

    def body(*refs):
        refs = list(refs)
        k_ref, v_ref, c_ref, s_ref = refs[:4]
        tile_refs = [refs[4:8]]
        pos = 8
        if has_next:
            tile_refs.append(refs[pos:pos + 4])
            pos += 4
        if has_token:
            pos += 1
        dq_ref, dk_ref, dv_ref = refs[pos:pos + 3]
        carry_ref = refs[pos + 3]
        blk = pl.program_id(2)
        if has_next:
            @pl.when(blk == 0)
            def _():
                carry_ref[...] = jnp.zeros_like(carry_ref)

        nrows = (npair if gqa else 1) * QBLOCK
        qi = lax.broadcasted_iota(jnp.int32, (nrows, QBLOCK), 0) % QBLOCK
        ki = lax.broadcasted_iota(jnp.int32, (nrows, QBLOCK), 1)
        valids = [qi >= ki, (qi + QBLOCK - ki <= n_back) & (blk + 1 < nblk)]
        per = npair // 2
        ntile = len(tile_refs)
        cat = lambda parts: jnp.concatenate(parts, axis=1) if len(parts) > 1 else parts[0]
        classes = _class_rows(r)
        step = max(1, ATTN_UNITS // (ntile * (2 if gqa else 2 * npair)))
        for first in range(0, len(classes), step):
            batch = classes[first:first + step]
            units = []
            for ci, rows in enumerate(batch):
                tiles = [(q_ref[0, rows, :] * scale, do_ref[0, rows, :], l_ref[0, rows, :], d_ref[0, rows, :])
                         for q_ref, do_ref, l_ref, d_ref in tile_refs]
                k, v = k_ref[0, rows, :], v_ref[0, rows, :]
                if gqa:
                    for hk in range(2):
                        pairs = list(range(hk * per, (hk + 1) * per))
                        kd, vd = _dup_half(k, hk).astype(BF16), _dup_half(v, hk).astype(BF16)
                        for t, (q, do_, l_, d_) in enumerate(tiles):
                            units.append(dict(ci=ci, t=t, hk=hk, pairs=pairs, qs=_stack_masked(q, pairs).astype(BF16),
                                              dos=_stack_masked(do_, pairs).astype(BF16), lcol=_stack_pair_cols(l_, pairs),
                                              dcol=_stack_pair_cols(d_, pairs), kmat=kd, vmat=vd, kdq=kd))
                else:
                    for c in range(npair):
                        sl = slice(c * LANES, (c + 1) * LANES)
                        kc, vcb = k[:, sl], v[:, sl].astype(BF16)
                        kcb = kc.astype(BF16)
                        for t, (q, do_, l_, d_) in enumerate(tiles):
                            for half in (0, 1):
                                hm = _half_mask(kc.shape, half)
                                col = c * LANES + half * HEAD_DIM
                                units.append(dict(ci=ci, t=t, c=c, half=half, qs=jnp.where(hm, q[:, sl], 0.0).astype(BF16),
                                                  dos=jnp.where(hm, do_[:, sl], 0.0).astype(BF16), lcol=l_[:, col:col + 1],
                                                  dcol=d_[:, col:col + 1], kmat=kcb, vmat=vcb,
                                                  kdq=jnp.where(hm, kc, 0.0).astype(BF16)))
            for u in units:
                u["s"] = lax.dot_general(u["qs"], u["kmat"], _DIMS["nt"], preferred_element_type=F32)
                u["dp"] = lax.dot_general(u["dos"], u["vmat"], _DIMS["nt"], preferred_element_type=F32)
            for u in units:
                p = jnp.exp(jnp.where(valids[u["t"]], u["s"], NEG_INF) - u["lcol"])
                u["ds"] = (p * (u["dp"] + u["dcol"])).astype(BF16)
                u["p"] = p.astype(BF16)
            for u in units:
                u["dv"] = lax.dot_general(u["p"], u["dos"], _DIMS["tn"], preferred_element_type=F32)
                u["dk"] = lax.dot_general(u["ds"], u["qs"], _DIMS["tn"], preferred_element_type=F32)
                u["dq"] = jnp.dot(u["ds"], u["kdq"], preferred_element_type=F32) * scale
            for ci, rows in enumerate(batch):
                mine = [u for u in units if u["ci"] == ci]
                dq = [[None] * npair for _ in range(ntile)]
                if gqa:
                    dk_out = dv_out = None
                    for hk in range(2):
                        us = [u for u in mine if u["hk"] == hk]
                        for u in us:
                            for i, c in enumerate(u["pairs"]):
                                dq[u["t"]][c] = _pick_halves(u["dq"][2 * i * QBLOCK:(2 * i + 1) * QBLOCK],
                                                             u["dq"][(2 * i + 1) * QBLOCK:(2 * i + 2) * QBLOCK])
                        dk_h = _fold_halves(functools.reduce(jnp.add, [u["dk"] for u in us]))
                        dv_h = _fold_halves(functools.reduce(jnp.add, [u["dv"] for u in us]))
                        dk_out = dk_h if hk == 0 else _pick_halves(dk_out, dk_h)
                        dv_out = dv_h if hk == 0 else _pick_halves(dv_out, dv_h)
                else:
                    dks, dvs = [], []
                    for c in range(npair):
                        us = [u for u in mine if u["c"] == c]
                        dks.append(functools.reduce(jnp.add, [u["dk"] for u in us]))
                        dvs.append(functools.reduce(jnp.add, [u["dv"] for u in us]))
                        for t in range(ntile):
                            dq[t][c] = functools.reduce(jnp.add, [u["dq"] for u in us if u["t"] == t])
                    dk_out, dv_out = cat(dks), cat(dvs)
                ck, sk_ = c_ref[0, rows, :], s_ref[0, rows, :]
                dk_ref[0, rows, :] = _rope(dk_out, ck, sk_, sign=-1.0, mxu=gqa)
                dv_ref[0, rows, :] = dv_out
                dq_cur = cat(dq[0])
                if has_next:
                    dq_cur = dq_cur + carry_ref[rows, :]
                    carry_ref[rows, :] = cat(dq[1])
                dq_ref[0, rows, :] = _rope(dq_cur, ck, sk_, sign=-1.0, mxu=gqa)

    def at(width, col0, shift):
        return pl.BlockSpec((1, rr, width), lambda b, c, i: (b, jnp.minimum(i + shift, nblk - 1), col0 + c))

    in_specs = [at(kw, k_col, 0), at(kw, v_col, 0), pl.BlockSpec((1, rr, LANES), lambda b, c, i: (b, i, 0)),
                pl.BlockSpec((1, rr, LANES), lambda b, c, i: (b, i, 0))]
    args = [k_arr, v_arr, cos, sin]
    for shift in (0, 1) if has_next else (0,):
        in_specs += [at(qw, q_col, shift), at(qw, 0, shift), at(qw, 0, shift), at(qw, 0, shift)]
        args += [q_arr, do, lse, dd]
    if has_token:
        in_specs.append(pl.BlockSpec(token.shape, lambda b, c, i: (0, 0)))
        args.append(token)
    return pl.pallas_call(
        body,
        name=name,
        grid=(bsz, nchunk, nblk),
        in_specs=in_specs,
        out_specs=[pl.BlockSpec((1, rr, qw), lambda b, c, i: (b, i, c)),
                   pl.BlockSpec((1, rr, kw), lambda b, c, i: (b, i, c)),
                   pl.BlockSpec((1, rr, kw), lambda b, c, i: (b, i, c))],
        out_shape=[jax.ShapeDtypeStruct((bsz, seq, nchunk * qw), F32),
                   jax.ShapeDtypeStruct((bsz, seq, nchunk * kw), F32),
                   jax.ShapeDtypeStruct((bsz, seq, nchunk * kw), F32)],
        scratch_shapes=[pltpu.VMEM((rr, qw) if has_next else (8, LANES), F32)],
        compiler_params=_params("parallel", "parallel", "arbitrary"),
    )(*args)


B_CHUNKS = {1: (4, 1), 4: (1, 4), 16: (1, 4)}


def _rope_tables(positions):
    half = HEAD_DIM // 2
    inv = ROPE_THETA ** (-jnp.arange(half, dtype=F32) / half)
    ang = positions.astype(F32)[..., None] * inv
    cos, sin = jnp.cos(ang), jnp.sin(ang)
    return jnp.concatenate([cos] * 4, axis=-1), jnp.concatenate([-sin, sin, -sin, sin], axis=-1)


def _layer_step(x, mod, positions, sinks, ln1_g, ln1_b, ln2_g, ln2_b, target, get_w_in, get_rest, hook):
    bsz, seq, d = x.shape
    ntok = bsz * seq
    flat = lambda v: v.reshape(ntok, v.shape[-1])
    unflat = lambda v: v.reshape(bsz, seq, v.shape[-1])
    cos, sin = _rope_tables(positions)
    mm = functools.partial(_matmul, tm=1024, tk=1024)
    scalar = lambda tok: 0.0 if tok is None else tok[0, 0]

    u1 = _modulate_in(x, mod)
    u1f = flat(u1)
    wint = get_w_in(u1)
    cosf, sinf = flat(cos), flat(sin)
    proj = functools.partial(_proj_rope, u1f, wint, cosf, sinf, tm=2048)
    qa = unflat(proj(n=1024, b_off=OFF_QA, rope_cols=1024, tn=512, name="proj_qa"))
    kva = unflat(proj(n=256, b_off=OFF_KVA, rope_cols=128, tn=128, name="proj_kva"))
    qkvb = unflat(proj(n=4608, b_off=OFF_QKVB, rope_cols=3072, tn=256, name="proj_qkvb"))
    gab = unflat(proj(n=2048, b_off=OFF_GAB, rope_cols=0, tn=256, name="proj_gab", out_dtype=BF16))

    sink_vec = sinks.reshape(A_Q_HEADS) + scalar(hook("projected", gab))
    a_kw = dict(npair=A_Q_HEADS // 2, gqa=True, q_col=0, k_col=0, v_col=1, nchunk=1, r=1, n_back=A_WINDOW - 1)
    oa, lse_a = _attn_fwd(qa, kva, kva, name="attn_a_fwd", sinks=sink_vec, **a_kw)
    rest = get_rest(oa)
    wba, wbbt, wo, wgut, wd = (rest[n] for n in ("w_branch_a", "w_branch_b", "w_o", "w_gate_up", "w_down"))
    ya = unflat(mm(flat(oa), wba, mode="nn", out_dtype=BF16, tn=512, name="branch_a"))

    b_kws, os_, ls_ = [], [], []
    for g, (window, r) in enumerate(B_PATTERNS):
        npair, nch = B_CHUNKS[r]
        per = B_HEADS_PER_GROUP // (2 * npair)
        nsec = len(B_PATTERNS) * per
        kw_ = dict(npair=npair, gqa=False, q_col=g * per, k_col=nsec + g * per, v_col=2 * nsec + g * per, nchunk=nch, r=r,
                   n_back=window // r)
        b_kws.append(kw_)
        o_g, l_g = _attn_fwd(qkvb, qkvb, qkvb, name=f"attn_b{g}_fwd", **kw_)
        os_.append(o_g)
        ls_.append(l_g)
    ob = _merge_b(os_, ls_)
    ybf, mergedf = _branch_b_gate_merge(flat(ob), wbbt, flat(gab), flat(ya))
    xf = flat(x)
    y1f, h1f, u2f = _wo_ln1(mergedf, wo, xf, mod, ln1_g, ln1_b, seq)
    h = unflat(mm(u2f, wgut, mode="nt", out_dtype=BF16, tn=D_FF // 2, name="gate_up"))
    a = _silu_mul(h)

    dy2f, dh1af, acc2 = _down_ln2_loss_bwd(flat(a), wd, h1f, mod, ln2_g, ln2_b, flat(target), seq)
    da = unflat(mm(dy2f, wd, mode="nt", out_dtype=BF16, tn=D_FF // 2, name="down_dgrad"))
    g_wd = _matmul(flat(a), dy2f, mode="tn", out_dtype=BF16, tm=256, tn=1024, tk=ntok, name="down_wgrad")
    dh = _silu_mul_bwd(da, h)
    dhf = flat(dh)
    g_wgut = _matmul(dhf, u2f, mode="tn", out_dtype=BF16, tm=256, tn=1024, tk=ntok, name="gate_up_wgrad")
    dy1f, dxaf, acc1 = _gate_up_dgrad_ln1_bwd(dhf, wgut, dh1af, xf, y1f, mod, ln1_g, ln1_b, seq)
    g_wo = _matmul(mergedf, dy1f, mode="tn", out_dtype=BF16, tm=256, tn=1024, tk=ntok, name="w_o_wgrad")
    dyaf, dybf, dgaf, dgbf = _wo_dgrad_gate_bwd(dy1f, wo, flat(gab), flat(ya), ybf)
    g_wba = _matmul(flat(oa), dyaf, mode="tn", out_dtype=BF16, tm=256, tn=1024, tk=ntok, name="branch_a_wgrad")
    g_wbbt = _matmul(dybf, flat(ob), mode="tn", out_dtype=BF16, tm=256, tn=512, tk=ntok, name="branch_b_wgrad")
    tok = hook("grads_rest", dict(w_branch_a=g_wba, w_branch_b=g_wbbt, w_o=g_wo, w_gate_up=g_wgut, w_down=g_wd))

    sinks_exp = jnp.repeat(sinks.reshape(1, A_Q_HEADS), HEAD_DIM, axis=1) + scalar(tok)
    doa, dd_a, acc_s = _branch_a_dgrad_delta(dyaf, wba, flat(oa), flat(lse_a), sinks_exp, seq)
    doa, dd_a = unflat(doa), unflat(dd_a)
    tok = hook("delta_done", dd_a)
    dqa, dka, dva = _attn_bwd(qa, kva, kva, cos, sin, doa, lse_a, dd_a, name="attn_a_bwd", token=tok, **a_kw)
    merged_bwd = [unflat(t) for t in _branch_b_dgrad_merge_bwd(dybf, wbbt, [flat(t) for t in os_], [flat(t) for t in ls_])]
    dqs, dks, dvs = [], [], []
    for g in range(len(B_PATTERNS)):
        dq_g, dk_g, dv_g = _attn_bwd(qkvb, qkvb, qkvb, cos, sin, merged_bwd[g], ls_[g], merged_bwd[3 + g],
                                     name=f"attn_b{g}_bwd", **b_kws[g])
        dqs.append(dq_g)
        dks.append(dk_g)
        dvs.append(dv_g)
    dproj = jnp.concatenate([t.astype(BF16) for t in [dqa, dka, dva] + dqs + dks + dvs] + [unflat(dgaf), unflat(dgbf)], axis=-1)
    dprojf = flat(dproj)
    g_wint = _matmul(dprojf, u1f, mode="tn", out_dtype=BF16, tm=256, tn=1024, tk=ntok, name="w_in_wgrad")
    tok = hook("grads_w_in", dict(w_in=g_wint))
    grad_x, acc0 = _w_in_dgrad_grad_x(dprojf, wint, dxaf, xf, mod, seq, tok)
    grad_x = unflat(grad_x)
    tok = hook("dgrad_done", grad_x)

    loss_part = jnp.sum(acc2[:, 3, 0])
    dmod = jnp.stack([acc0[:, 1], acc0[:, 0], acc1[:, 2], acc1[:, 4], acc1[:, 3], acc2[:, 2]], axis=1)
    small = jnp.stack([acc1[:, 0].sum(0), acc1[:, 1].sum(0), acc2[:, 0].sum(0), acc2[:, 1].sum(0), acc_s[:, 0].sum(0)])
    small = small + scalar(tok)
    return loss_part, grad_x, dmod, small


CHIP_FLIPS = (2, 4, 6)


def _my_place():
    return lax.axis_index("x"), lax.axis_index("y"), lax.axis_index("c")


def _flip(place, k):
    px, py, pc = place
    return (1 - px if k & 4 else px, 1 - py if k & 2 else py, 1 - pc if k & 1 else pc)


def _index(place):
    return 4 * place[0] + 2 * place[1] + place[2]


def _gather_small(v, name):
    rows, cols = v.shape

    def body(v_ref, out_ref, send_sems, recv_sems):
        me = _my_place()
        out_ref[_index(me)] = v_ref[...]
        copies = []
        for k in range(1, N_DEV):
            copies.append(pltpu.make_async_remote_copy(
                src_ref=v_ref, dst_ref=out_ref.at[_index(me)], send_sem=send_sems.at[k - 1], recv_sem=recv_sems.at[k - 1],
                device_id=_flip(me, k), device_id_type=MESH))
        for cp in copies:
            cp.start()
        for k in range(1, N_DEV):
            pltpu.make_async_remote_copy(
                src_ref=v_ref, dst_ref=out_ref.at[_index(_flip(me, k))], send_sem=send_sems.at[k - 1],
                recv_sem=recv_sems.at[k - 1], device_id=_flip(me, k), device_id_type=MESH).wait_recv()
        for cp in copies:
            cp.wait_send()

    return pl.pallas_call(
        body,
        name=name,
        out_shape=jax.ShapeDtypeStruct((N_DEV, rows, cols), v.dtype),
        in_specs=[pl.BlockSpec(memory_space=pltpu.VMEM)],
        out_specs=pl.BlockSpec(memory_space=pltpu.VMEM),
        scratch_shapes=[pltpu.SemaphoreType.DMA((N_DEV - 1,)), pltpu.SemaphoreType.DMA((N_DEV - 1,))],
        compiler_params=pltpu.CompilerParams(vmem_limit_bytes=VMEM_LIMIT_BYTES),
    )(v)


_HBM = pl.BlockSpec(memory_space=pltpu.HBM)
_SEM = pl.BlockSpec(memory_space=pltpu.SEMAPHORE)
_EFFECT = pltpu.SideEffectType.DATAFLOW_SIDE_EFFECTING


def _remote(src, dst, send_sems, recv_sems, j, to):
    return pltpu.make_async_remote_copy(src_ref=src, dst_ref=dst, send_sem=send_sems.at[j], recv_sem=recv_sems.at[j],
                                        device_id=to, device_id_type=MESH)


def _copies_start(name, bufs, make_copies, nsem):
    nbuf = len(bufs)

    def body(*refs):
        for cp in make_copies(refs[:nbuf], refs[nbuf], refs[nbuf + 1]):
            cp.start()
        refs[-1][...] = jnp.zeros_like(refs[-1])

    sems = pltpu.SemaphoreType.DMA((nsem,))
    res = pl.pallas_call(
        body, name=name,
        out_shape=(sems, sems, *[pltpu.HBM(v.shape, v.dtype) for v in bufs], jax.ShapeDtypeStruct((8, LANES), F32)),
        in_specs=(_HBM,) * nbuf, out_specs=(_SEM, _SEM) + (_HBM,) * nbuf + (pl.BlockSpec(memory_space=pltpu.VMEM),),
        input_output_aliases={i: 2 + i for i in range(nbuf)},
        compiler_params=pltpu.CompilerParams(has_side_effects=_EFFECT),
    )(*[pltpu.with_memory_space_constraint(v, pltpu.HBM) for v in bufs])
    return res[0], res[1], list(res[2:2 + nbuf]), res[-1]


def _copies_wait(name, started, make_copies, after):
    send_sems, recv_sems, bufs, _ = started
    nbuf = len(bufs)

    def body(*refs):
        for cp in make_copies(refs[:nbuf], refs[nbuf], refs[nbuf + 1]):
            cp.wait_send()
            cp.wait_recv()

    return list(pl.pallas_call(
        body, name=name,
        out_shape=tuple(pltpu.HBM(v.shape, v.dtype) for v in bufs),
        in_specs=(_HBM,) * nbuf + (_SEM, _SEM, pl.BlockSpec(memory_space=pl.ANY)), out_specs=(_HBM,) * nbuf,
        input_output_aliases={i: i for i in range(nbuf)},
        compiler_params=pltpu.CompilerParams(has_side_effects=_EFFECT),
    )(*bufs, send_sems, recv_sems, after))


def _to_sibling_copies(refs, send_sems, recv_sems):
    src_ref, land_ref = refs
    me = _my_place()
    return [_remote(src_ref.at[q, 1 - me[2]], land_ref.at[q], send_sems, recv_sems, q, _flip(me, 1)) for q in range(4)]


def _to_chips_copies(refs, send_sems, recv_sems):
    src_ref, land_ref = refs
    me = _my_place()
    copies = []
    for j, k in enumerate(CHIP_FLIPS):
        to = _flip(me, k)
        copies.append(_remote(src_ref.at[2 * to[0] + to[1]], land_ref.at[j], send_sems, recv_sems, j, to))
    return copies


class _Gather:
    def __init__(self, name, blocks):
        self.name, self.n = name, len(blocks)
        lands = [lax.empty((N_DEV,) + v.shape, v.dtype) for v in blocks]
        self.first = _copies_start(name + "_start", list(blocks) + lands, self._first_copies, 5 * self.n)
        self.token = self.first[3]

    def _first_copies(self, refs, send_sems, recv_sems):
        me = _my_place()
        return [_remote(refs[w], refs[self.n + w].at[_index(me)], send_sems, recv_sems, 5 * w + j, _flip(me, k))
                for w in range(self.n) for j, k in enumerate((0, 1) + CHIP_FLIPS)]

    def _pass_copies(self, refs, send_sems, recv_sems):
        me = _my_place()
        copies = []
        for w, land in enumerate(refs):
            for j, k in enumerate(CHIP_FLIPS):
                slot = land.at[_index(_flip(me, k))]
                copies.append(_remote(slot, slot, send_sems, recv_sems, 3 * w + j, _flip(me, 1)))
        return copies

    def pass_on(self, after):
        lands = _copies_wait(self.name + "_wait", self.first, self._first_copies, after)[self.n:]
        self.second = _copies_start(self.name + "_pass_start", lands, self._pass_copies, 3 * self.n)
        return self.second[3]

    def finish(self, after):
        return _copies_wait(self.name + "_pass_wait", self.second, self._pass_copies, after)


SUM_SPLIT = 2


def _sum_pairs(parts, theirs):
    nchip, _, rows, cols = parts.shape
    tile = rows // SUM_SPLIT

    def body(c_ref, a_ref, b_ref, o_ref):
        o_ref[...] = (a_ref[0].astype(F32) + b_ref[...].astype(F32)).astype(BF16)

    spec = pl.BlockSpec((1, tile, cols), lambda q, t, c_ref: (q, t, 0))
    grid_spec = pltpu.PrefetchScalarGridSpec(
        num_scalar_prefetch=1, grid=(nchip, SUM_SPLIT),
        in_specs=[pl.BlockSpec((1, 1, tile, cols), lambda q, t, c_ref: (q, c_ref[0], t, 0)), spec], out_specs=spec)
    return pl.pallas_call(body, name="grad_sum_sibling", grid_spec=grid_spec,
                          out_shape=jax.ShapeDtypeStruct((nchip, rows, cols), BF16),
                          compiler_params=_params("parallel", "parallel"))(lax.axis_index("c").reshape(1), parts, theirs)


def _sum_final(chip_sum, got):
    _, rows, cols = chip_sum.shape
    tile = rows // SUM_SPLIT

    def body(q_ref, a_ref, g_ref, o_ref):
        o_ref[...] = ((a_ref[0].astype(F32) + g_ref[0].astype(F32)) + g_ref[1].astype(F32)) + g_ref[2].astype(F32)

    grid_spec = pltpu.PrefetchScalarGridSpec(
        num_scalar_prefetch=1, grid=(SUM_SPLIT,),
        in_specs=[pl.BlockSpec((1, tile, cols), lambda t, q_ref: (q_ref[0], t, 0)),
                  pl.BlockSpec((3, tile, cols), lambda t, q_ref: (0, t, 0))],
        out_specs=pl.BlockSpec((tile, cols), lambda t, q_ref: (t, 0)))
    my_chip = (2 * lax.axis_index("x") + lax.axis_index("y")).reshape(1)
    return pl.pallas_call(body, name="grad_sum_chips", grid_spec=grid_spec, out_shape=jax.ShapeDtypeStruct((rows, cols), F32),
                          compiler_params=_params("parallel"))(my_chip, chip_sum, got)


class _ReduceScatter:
    def __init__(self, name, slabs):
        self.name, self.rows = name, slabs.shape[1]
        parts = slabs.reshape(4, 2, self.rows, D_MODEL)
        self.first = _copies_start(name + "_sibling_start", [parts, lax.empty((4, self.rows, D_MODEL), slabs.dtype)],
                                   _to_sibling_copies, 4)
        self.token = self.first[3]

    def between_chips(self, after):
        parts, theirs = _copies_wait(self.name + "_sibling_wait", self.first, _to_sibling_copies, after)
        chip_sum = _sum_pairs(parts, theirs)
        self.second = _copies_start(self.name + "_chips_start", [chip_sum, lax.empty((3, self.rows, D_MODEL), chip_sum.dtype)],
                                    _to_chips_copies, 3)
        return self.second[3]

    def finish(self, after):
        chip_sum, got = _copies_wait(self.name + "_chips_wait", self.second, _to_chips_copies, after)
        return _sum_final(chip_sum, got)


def _ada_fwd(c_all, w, b):
    nb, _ = c_all.shape
    ncol = w.shape[1]

    def body(c_ref, w_ref, b_ref, o_ref):
        c = c_ref[...]
        act = (c * _sigmoid(c)).astype(BF16)
        o_ref[...] = jnp.dot(act, w_ref[...].astype(BF16), preferred_element_type=F32) + b_ref[...]

    return pl.pallas_call(body, name="ada_fwd", out_shape=jax.ShapeDtypeStruct((nb, ncol), F32),
                          compiler_params=pltpu.CompilerParams(vmem_limit_bytes=VMEM_LIMIT_BYTES))(c_all, w, b)


def _ada_wgrad(c_all_t, dmod_cols):
    d, nb = c_all_t.shape
    ncol = dmod_cols.shape[1]

    def body(ct_ref, dm_ref, o_ref):
        ct = ct_ref[...]
        act = (ct * _sigmoid(ct)).astype(BF16).astype(F32)
        dm = dm_ref[...].astype(BF16).astype(F32)
        acc = act[:, 0:1] * dm[0:1, :]
        for i in range(1, nb):
            acc = acc + act[:, i:i + 1] * dm[i:i + 1, :]
        o_ref[...] = acc

    return pl.pallas_call(body, name="ada_wgrad", out_shape=jax.ShapeDtypeStruct((d, ncol), F32),
                          compiler_params=pltpu.CompilerParams(vmem_limit_bytes=VMEM_LIMIT_BYTES))(c_all_t, dmod_cols)


SMALL_ROWS = 24


def _reduce_small(gathered):
    def body(g_ref, o_ref):
        acc = g_ref[0]
        for dev in range(1, N_DEV):
            acc = acc + g_ref[dev]
        o_ref[...] = acc

    return pl.pallas_call(body, name="reduce_small", out_shape=jax.ShapeDtypeStruct(gathered.shape[1:], F32))(gathered)


def _adamw(w, g, m, v, name):
    rows, cols = w.shape
    tile = rows
    for cand in (256, 128, 64, 32, 16, 8):
        if rows % cand == 0 and rows > cand:
            tile = cand
            break
    spec = pl.BlockSpec((tile, cols), lambda t: (t, 0))
    bc1 = 1.0 - ADAM_B1 ** ADAM_STEP
    bc2 = 1.0 - ADAM_B2 ** ADAM_STEP

    def body(w_ref, g_ref, m_ref, v_ref, d_ref, nm_ref, nv_ref):
        g_ = g_ref[...]
        nm = ADAM_B1 * m_ref[...] + (1.0 - ADAM_B1) * g_
        nv = ADAM_B2 * v_ref[...] + (1.0 - ADAM_B2) * (g_ * g_)
        d_ref[...] = -ADAM_LR * ((nm / bc1) / (jnp.sqrt(nv / bc2) + ADAM_EPS) + ADAM_WD * w_ref[...])
        nm_ref[...] = nm
        nv_ref[...] = nv

    shp = jax.ShapeDtypeStruct((rows, cols), F32)
    return pl.pallas_call(body, name=name, grid=(rows // tile,), in_specs=[spec] * 4, out_specs=[spec] * 3, out_shape=[shp] * 3,
                          compiler_params=_params("parallel"))(w, g, m, v)


_WEIGHTS = ("w_ada", "b_ada", "w_in", "sinks", "w_branch_a", "w_branch_b", "w_o", "ln1_g", "ln1_b", "w_gate_up", "w_down",
            "ln2_g", "ln2_b")
_TRANSPOSED = ("w_in", "w_branch_b", "w_gate_up")


def _pack_shard(name, w):
    w = w.astype(BF16)
    if name in _TRANSPOSED:
        w = w.T
    return w.reshape(-1, D_MODEL)


def _unpack_full(name, slab):
    if name == "w_branch_b":
        return slab.reshape(N_DEV * 128, 512)
    return slab.reshape(-1, D_MODEL)


def _unpack_group(group, gathered):
    return {n: _unpack_full(n, slab) for (n, _), slab in zip(group, gathered)}


def _unpack_grads(group, g_packed):
    g_w, off = {}, 0
    for n, r in group:
        part = g_packed[off:off + r]
        off += r
        g_w[n] = part.reshape(128, 512) if n == "w_branch_b" else part
    return g_w


def kernel(x, c, positions, w_ada, b_ada, w_in, sinks, w_branch_a, w_branch_b, w_o, ln1_g, ln1_b, w_gate_up, w_down, ln2_g, ln2_b, loss_target, m_w_ada, m_b_ada, m_w_in, m_sinks, m_w_branch_a, m_w_branch_b, m_w_o, m_ln1_g, m_ln1_b, m_w_gate_up, m_w_down, m_ln2_g, m_ln2_b, v_w_ada, v_b_ada, v_w_in, v_sinks, v_w_branch_a, v_w_branch_b, v_w_o, v_ln1_g, v_ln1_b, v_w_gate_up, v_w_down, v_ln2_g, v_ln2_b):
    weights = dict(w_ada=w_ada, b_ada=b_ada, w_in=w_in, sinks=sinks, w_branch_a=w_branch_a, w_branch_b=w_branch_b, w_o=w_o,
                   ln1_g=ln1_g, ln1_b=ln1_b, w_gate_up=w_gate_up, w_down=w_down, ln2_g=ln2_g, ln2_b=ln2_b)
    m_in = dict(w_ada=m_w_ada, b_ada=m_b_ada, w_in=m_w_in, sinks=m_sinks, w_branch_a=m_w_branch_a, w_branch_b=m_w_branch_b,
                w_o=m_w_o, ln1_g=m_ln1_g, ln1_b=m_ln1_b, w_gate_up=m_w_gate_up, w_down=m_w_down, ln2_g=m_ln2_g, ln2_b=m_ln2_b)
    v_in = dict(w_ada=v_w_ada, b_ada=v_b_ada, w_in=v_w_in, sinks=v_sinks, w_branch_a=v_w_branch_a, w_branch_b=v_w_branch_b,
                w_o=v_w_o, ln1_g=v_ln1_g, ln1_b=v_ln1_b, w_gate_up=v_w_gate_up, w_down=v_w_down, ln2_g=v_ln2_g, ln2_b=v_ln2_b)
    bsz = x.shape[0]
    me = _index(_my_place())
    ada_cols = w_ada.shape[2]
    outs = {}

    def adamw(n, g):
        w2, m2, v2 = (t[n][0] if t[n].ndim == 3 else t[n] for t in (weights, m_in, v_in))
        shape = weights[n].shape
        if n in _TRANSPOSED:
            dlt, nm, nv = _adamw(w2.T, g, m2.T, v2.T, "adamw_" + n)
            outs[n] = tuple(t.T.reshape(shape) for t in (g, dlt, nm, nv))
        else:
            dlt, nm, nv = _adamw(w2, g, m2, v2, "adamw_" + n)
            outs[n] = tuple(t.reshape(shape) for t in (g, dlt, nm, nv))
        return nv

    packed_in = [_pack_shard(n, weights[n][0]) for n, _ in GROUP_IN]
    packed_rest = [_pack_shard(n, weights[n][0]) for n, _ in GROUP_REST]
    c_all = _gather_small(jnp.pad(c, ((0, 8 - bsz), (0, 0))), "gather_c")[:, :bsz].reshape(N_DEV * bsz, D_MODEL)
    gather_in = _Gather("gather_w_in", lax.optimization_barrier((packed_in, c_all))[0])
    b_cols = lax.dynamic_slice_in_dim(b_ada, me * ada_cols, ada_cols, axis=1)
    mod_cols = _ada_fwd(c_all, w_ada[0], b_cols + gather_in.token[0, 0])
    mod_all = _gather_small(mod_cols, "gather_mod").transpose(1, 0, 2).reshape(N_DEV * bsz, 6, D_MODEL)
    gather_rest = _Gather("gather_rest", lax.optimization_barrier((packed_rest, mod_all))[0])
    mod = jnp.pad(lax.dynamic_slice_in_dim(mod_all, me * bsz, bsz, axis=0), ((0, 0), (0, 2), (0, 0)))
    mod = mod + gather_rest.token[0, 0]
    mod = mod + gather_in.pass_on(mod)[0, 0]

    scatters = {}

    def get_w_in(after):
        return _unpack_group(GROUP_IN, gather_in.finish(after))["w_in"]

    def get_rest(after):
        return _unpack_group(GROUP_REST, gather_rest.finish(after))

    def pack_grads(group, grads):
        return jnp.concatenate([grads[n].reshape(N_DEV, r, D_MODEL) for n, r in group], axis=1)

    def hook(point, value):
        if point == "projected":
            return gather_rest.pass_on(value)
        if point == "grads_rest":
            scatters["rest"] = _ReduceScatter("scatter_rest", pack_grads(GROUP_REST, value))
            return scatters["rest"].token
        if point == "delta_done":
            return scatters["rest"].between_chips(value)
        if point == "grads_w_in":
            scatters["in"] = _ReduceScatter("scatter_w_in", pack_grads(GROUP_IN, value))
            return scatters["in"].token
        if point == "dgrad_done":
            tok = scatters["in"].between_chips(value)
            for n, g in _unpack_grads(GROUP_REST, scatters["rest"].finish(tok)).items():
                adamw(n, g)
            return tok
        raise ValueError(point)

    loss_part, grad_x, dmod, small = _layer_step(x, mod, positions, sinks[0], ln1_g, ln1_b, ln2_g, ln2_b, loss_target,
                                                 get_w_in, get_rest, hook)

    rows = jnp.concatenate([dmod.reshape(bsz * 6, D_MODEL), small, jnp.full((1, D_MODEL), loss_part, F32),
                            jnp.zeros((SMALL_ROWS - bsz * 6 - 6, D_MODEL), F32)], axis=0)
    small_all = _gather_small(rows, "gather_small")
    sums = _reduce_small(small_all)
    loss = sums[bsz * 6 + 5, 0]
    dmod_all = small_all[:, :bsz * 6].reshape(N_DEV * bsz, 6 * D_MODEL)
    adamw("b_ada", functools.reduce(jnp.add, [sums[6 * i:6 * i + 6] for i in range(bsz)]).reshape(1, 6 * D_MODEL))
    for i, n in enumerate(("ln1_g", "ln1_b", "ln2_g", "ln2_b")):
        adamw(n, sums[12 + i][None])
    adamw("sinks", sums[16][::HEAD_DIM][None])
    dmod_cols = lax.dynamic_slice_in_dim(dmod_all, me * ada_cols, ada_cols, axis=1)
    last = adamw("w_ada", _ada_wgrad(c_all.T, dmod_cols))
    for n, g in _unpack_grads(GROUP_IN, scatters["in"].finish(last)).items():
        adamw(n, g)

    return (loss, grad_x, *[outs[n][0] for n in _WEIGHTS], *[outs[n][1] for n in _WEIGHTS], *[outs[n][2] for n in _WEIGHTS],
            *[outs[n][3] for n in _WEIGHTS])
```

```python
import functools

import jax
import jax.numpy as jnp
from jax import lax
from jax.experimental import pallas as pl
from jax.experimental.pallas import tpu as pltpu

F32 = jnp.float32
BF16 = jnp.bfloat16

D_MODEL = 1024
HEAD_DIM = 64
A_Q_HEADS = 16
A_WINDOW = 128
B_PATTERNS = ((128, 1), (512, 4), (2048, 16))
B_HEADS_PER_GROUP = 8
D_FF = 2816
QBLOCK = 128
ROPE_THETA = 10000.0
LN_EPS = 1e-5
DEEPNORM_ALPHA = 2.0 ** 0.25
NEG_INF = -1e30
ADAM_LR, ADAM_B1, ADAM_B2, ADAM_EPS, ADAM_WD, ADAM_STEP = 0.001, 0.9, 0.999, 1e-08, 0.01, 10

N_DEV = 8
MESH_AXES = ("x", "y", "c")
LANES = 128
VMEM_LIMIT_BYTES = 56 * 1024 * 1024
MESH = pl.DeviceIdType.MESH

OFF_QA, OFF_KVA, OFF_QKVB, OFF_GAB = 0, 1024, 1280, 5888
GROUP_IN = (("w_in", 992),)
GROUP_REST = (("w_branch_a", 128), ("w_branch_b", 64), ("w_o", 128), ("w_gate_up", 704), ("w_down", 352))


def _params(*sem):
    return pltpu.CompilerParams(dimension_semantics=sem, vmem_limit_bytes=VMEM_LIMIT_BYTES)


def _sigmoid(x):
    return 1.0 / (1.0 + jnp.exp(-x))


_DIMS = {"nn": (((1,), (0,)), ((), ())), "nt": (((1,), (1,)), ((), ())), "tn": (((0,), (0,)), ((), ()))}


def _matmul(a, b, *, mode, tm, tn, tk, name, out_dtype=None, n=None, b_off=0, token=None, ins=(), outs=None, epilogue=None):
    if mode == "nn":
        (m, k), nn_ = a.shape, b.shape[1]
    elif mode == "nt":
        (m, k), nn_ = a.shape, (b.shape[0] if n is None else n)
    else:
        (k, m), nn_ = a.shape, b.shape[1]
    assert m % tm == 0 and nn_ % tn == 0 and k % tk == 0 and b_off % tn == 0, (name, m, nn_, k)
    nk = k // tk
    joff = b_off // tn
    if mode == "nn":
        a_spec = pl.BlockSpec((tm, tk), lambda i, j, kk: (i, kk))
        b_spec = pl.BlockSpec((tk, tn), lambda i, j, kk: (kk, j))
    elif mode == "nt":
        a_spec = pl.BlockSpec((tm, tk), lambda i, j, kk: (i, kk))
        b_spec = pl.BlockSpec((tn, tk), lambda i, j, kk: (j + joff, kk))
    else:
        a_spec = pl.BlockSpec((tk, tm), lambda i, j, kk: (kk, i))
        b_spec = pl.BlockSpec((tk, tn), lambda i, j, kk: (kk, j))
    dims = _DIMS[mode]
    has_token = token is not None
    plain = epilogue is None
    if plain:
        outs = [(jax.ShapeDtypeStruct((m, nn_), out_dtype), (tm, tn), lambda i, j: (i, j))]

        def epilogue(acc, i, j, in_refs, out_refs):
            out_refs[0][...] = acc.astype(out_refs[0].dtype)

    nin = len(ins)

    def body(*refs):
        a_ref, b_ref = refs[:2]
        in_refs = refs[2:2 + nin]
        out_refs = refs[2 + nin + has_token:-1]
        acc_ref = refs[-1]
        kk = pl.program_id(2)
        part = lax.dot_general(a_ref[...].astype(BF16), b_ref[...].astype(BF16), dims, preferred_element_type=F32)

        def finish(acc):
            epilogue(acc, pl.program_id(0), pl.program_id(1), in_refs, out_refs)

        if nk == 1:
            finish(part)
        else:
            @pl.when(kk == 0)
            def _():
                acc_ref[...] = part

            @pl.when(kk > 0)
            def _():
                acc_ref[...] += part

            @pl.when(kk == nk - 1)
            def _():
                finish(acc_ref[...])

    def spec(block, index):
        return pl.BlockSpec(block, lambda i, j, kk: index(i, j))

    in_specs, args = [a_spec, b_spec], [a, b]
    for arr, block, index in ins:
        in_specs.append(spec(block, index))
        args.append(arr)
    if has_token:
        in_specs.append(pl.BlockSpec(token.shape, lambda i, j, kk: (0, 0)))
        args.append(token)
    res = pl.pallas_call(
        body,
        name=name,
        grid=(m // tm, nn_ // tn, nk),
        in_specs=in_specs,
        out_specs=[spec(block, index) for _, block, index in outs],
        out_shape=[shape for shape, _, _ in outs],
        scratch_shapes=[pltpu.VMEM((tm, tn) if nk > 1 else (8, LANES), F32)],
        compiler_params=_params("arbitrary", "arbitrary", "arbitrary"),
    )(*args)
    return res[0] if plain else res


def _proj_rope(a, bt, cos, sin, *, n, b_off, rope_cols, tm, tn, name, out_dtype=F32):
    m, k = a.shape
    assert m % tm == 0 and n % tn == 0 and b_off % tn == 0 and rope_cols % tn == 0, name
    joff = b_off // tn
    nrope = rope_cols // tn

    def body(a_ref, b_ref, c_ref, s_ref, o_ref):
        acc = lax.dot_general(a_ref[...], b_ref[...], _DIMS["nt"], preferred_element_type=F32)
        j = pl.program_id(1)

        @pl.when(j < nrope)
        def _():
            o_ref[...] = _rope(acc, c_ref[...], s_ref[...]).astype(o_ref.dtype)

        @pl.when(j >= nrope)
        def _():
            o_ref[...] = acc.astype(o_ref.dtype)

    table = pl.BlockSpec((tm, LANES), lambda i, j: (i, 0))
    return pl.pallas_call(
        body,
        name=name,
        grid=(m // tm, n // tn),
        in_specs=[pl.BlockSpec((tm, k), lambda i, j: (i, 0)), pl.BlockSpec((tn, k), lambda i, j: (j + joff, 0)), table, table],
        out_specs=pl.BlockSpec((tm, tn), lambda i, j: (i, j)),
        out_shape=jax.ShapeDtypeStruct((m, n), out_dtype),
        compiler_params=_params("parallel", "parallel"),
    )(a, bt, cos, sin)


ROW_TILE = 256


def _rows(width, col=0):
    return pl.BlockSpec((1, ROW_TILE, width), lambda b, t: (b, t, col))


def _per_batch(nrows, width):
    return pl.BlockSpec((1, nrows, width), lambda b, t: (b, 0, 0))


def _whole(shape):
    return pl.BlockSpec(shape, lambda b, t: (0,) * len(shape))


def _row_call(body, name, bsz, seq, in_specs, out_specs, out_shape, accumulates=False):
    return pl.pallas_call(
        body,
        name=name,
        grid=(bsz, seq // ROW_TILE),
        in_specs=in_specs,
        out_specs=out_specs,
        out_shape=out_shape,
        compiler_params=_params("parallel", "arbitrary" if accumulates else "parallel"),
    )


def _acc_rows(acc_ref, first, rows):
    @pl.when(first)
    def _():
        acc_ref[...] = jnp.zeros_like(acc_ref)

    for r, val in enumerate(rows):
        acc_ref[0, r:r + 1, :] += val


def _colsum(v):
    return jnp.sum(v, axis=0, keepdims=True)


def _ln_stats(z):
    mu = jnp.mean(z, axis=-1, keepdims=True)
    zc = z - mu
    var = jnp.mean(zc * zc, axis=-1, keepdims=True)
    rstd = lax.rsqrt(var + LN_EPS)
    return zc * rstd, rstd


def _ln_bwd(dxhat, xhat, rstd):
    m1 = jnp.mean(dxhat, axis=-1, keepdims=True)
    m2 = jnp.mean(dxhat * xhat, axis=-1, keepdims=True)
    return rstd * (dxhat - m1 - xhat * m2)


def _modulate_in(x, mod):
    bsz, seq, d = x.shape

    def body(x_ref, mod_ref, u_ref):
        u_ref[0] = (x_ref[0] * (1.0 + mod_ref[0, 1:2, :]) + mod_ref[0, 0:1, :]).astype(BF16)

    return _row_call(body, "modulate_in", bsz, seq, [_rows(d), _per_batch(8, d)], _rows(d),
                     jax.ShapeDtypeStruct((bsz, seq, d), BF16))(x, mod)


def _gate_merge(gab, ya, yb):
    bsz, seq, d = ya.shape

    def body(ga_ref, gb_ref, ya_ref, yb_ref, o_ref):
        ga, gb, ya_, yb_ = (r[0].astype(F32) for r in (ga_ref, gb_ref, ya_ref, yb_ref))
        o_ref[0] = (_sigmoid(ga) * ya_ + _sigmoid(gb) * yb_).astype(BF16)

    return _row_call(body, "gate_merge", bsz, seq, [_rows(d, 0), _rows(d, 1), _rows(d), _rows(d)], _rows(d),
                     jax.ShapeDtypeStruct((bsz, seq, d), BF16))(gab, gab, ya, yb)


EP_TILE = 512


def _ep_specs(seq, d):
    tiles = seq // EP_TILE
    return ((EP_TILE, d), lambda i, j: (i, 0)), ((1, 8, d), lambda i, j: (i // tiles, 0, 0)), ((1, d), lambda i, j: (0, 0))


def _wo_ln1(merged, wo, x, mod, g, b, seq):
    ntok, d = x.shape
    row, per_b, whole = _ep_specs(seq, d)

    def epilogue(y, i, j, ins, outs):
        x_ref, mod_ref, g_ref, b_ref = ins
        y_ref, h_ref, u_ref = outs
        z = DEEPNORM_ALPHA * x_ref[...] + (1.0 + mod_ref[0, 2:3, :]) * y
        xhat, _ = _ln_stats(z)
        h = xhat * g_ref[...] + b_ref[...]
        y_ref[...] = y
        h_ref[...] = h
        u_ref[...] = (h * (1.0 + mod_ref[0, 4:5, :]) + mod_ref[0, 3:4, :]).astype(BF16)

    f32, bf16 = jax.ShapeDtypeStruct((ntok, d), F32), jax.ShapeDtypeStruct((ntok, d), BF16)
    return _matmul(merged, wo, mode="nn", tm=EP_TILE, tn=d, tk=d, name="w_o_ln1",
                   ins=[(x,) + row, (mod,) + per_b, (g,) + whole, (b,) + whole],
                   outs=[(f32,) + row, (f32,) + row, (bf16,) + row], epilogue=epilogue)


FF_HALF = D_FF // 2


def _interleave_gate_up(w):
    return jnp.concatenate([w[:FF_HALF], w[2 * FF_HALF:3 * FF_HALF], w[FF_HALF:2 * FF_HALF], w[3 * FF_HALF:]], axis=0)


def _gate_up_silu(u2, wgut_i):
    ntok = u2.shape[0]

    def epilogue(h, i, j, ins, outs):
        h_ref, a_ref = outs
        hg, hu = h[:, :FF_HALF], h[:, FF_HALF:]
        h_ref[...] = h.astype(BF16)
        a_ref[...] = (hg * _sigmoid(hg) * hu).astype(BF16)

    return _matmul(u2, wgut_i, mode="nt", tm=EP_TILE, tn=2 * FF_HALF, tk=u2.shape[1], name="gate_up_silu",
                   outs=[(jax.ShapeDtypeStruct((ntok, 2 * D_FF), BF16), (EP_TILE, 2 * FF_HALF), lambda i, j: (i, j)),
                         (jax.ShapeDtypeStruct((ntok, D_FF), BF16), (EP_TILE, FF_HALF), lambda i, j: (i, j))],
                   epilogue=epilogue)


def _down_dgrad_silu_bwd(dy2, wd, h_i):
    ntok = dy2.shape[0]
    wide = ((EP_TILE, 2 * FF_HALF), lambda i, j: (i, j))

    def epilogue(da, i, j, ins, outs):
        h = ins[0][...].astype(F32)
        hg, hu = h[:, :FF_HALF], h[:, FF_HALF:]
        sg = _sigmoid(hg)
        outs[0][:, :FF_HALF] = (da * hu * (sg * (1.0 + hg * (1.0 - sg)))).astype(BF16)
        outs[0][:, FF_HALF:] = (da * (hg * sg)).astype(BF16)

    return _matmul(dy2, wd, mode="nt", tm=EP_TILE, tn=FF_HALF, tk=dy2.shape[1], name="down_dgrad_silu_bwd",
                   ins=[(h_i,) + wide], outs=[(jax.ShapeDtypeStruct((ntok, 2 * D_FF), BF16),) + wide], epilogue=epilogue)[0]


def _down_ln2_loss_bwd(a, wd, h1, mod, g, b, target, seq):
    ntok, d = h1.shape
    row, per_b, whole = _ep_specs(seq, d)
    tiles = seq // EP_TILE

    def epilogue(y, i, j, ins, outs):
        h_ref, mod_ref, g_ref, b_ref, t_ref = ins
        dy_ref, dh_ref, acc_ref = outs
        gate = 1.0 + mod_ref[0, 5:6, :]
        z = DEEPNORM_ALPHA * h_ref[...] + gate * y
        xhat, rstd = _ln_stats(z)
        diff = xhat * g_ref[...] + b_ref[...] - t_ref[...]
        loss = 0.5 * jnp.sum(jnp.sum(diff * diff, axis=-1, keepdims=True) / d, axis=0, keepdims=True)
        dout = diff / d
        dz = _ln_bwd(dout * g_ref[...], xhat, rstd)
        dy_ref[...] = (gate * dz).astype(BF16)
        dh_ref[...] = DEEPNORM_ALPHA * dz
        _acc_rows(acc_ref, i % tiles == 0,
                  [_colsum(dout * xhat), _colsum(dout), _colsum(dz * y), jnp.broadcast_to(loss, (1, d))])

    return _matmul(a, wd, mode="nn", tm=EP_TILE, tn=d, tk=a.shape[1], name="down_ln2_loss_bwd",
                   ins=[(h1,) + row, (mod,) + per_b, (g,) + whole, (b,) + whole, (target,) + row],
                   outs=[(jax.ShapeDtypeStruct((ntok, d), BF16),) + row, (jax.ShapeDtypeStruct((ntok, d), F32),) + row,
                         (jax.ShapeDtypeStruct((ntok // seq, 8, d), F32),) + per_b], epilogue=epilogue)


def _silu_mul_bwd(da, h):
    bsz, seq, _ = h.shape

    def body(da_ref, hg_ref, hu_ref, dh_ref):
        hg, da_ = hg_ref[0].astype(F32), da_ref[0].astype(F32)
        sg = _sigmoid(hg)
        dh_ref[0, :, :D_FF] = (da_ * hu_ref[0].astype(F32) * (sg * (1.0 + hg * (1.0 - sg)))).astype(BF16)
        dh_ref[0, :, D_FF:] = (da_ * (hg * sg)).astype(BF16)

    return _row_call(body, "silu_mul_bwd", bsz, seq, [_rows(D_FF), _rows(D_FF, 0), _rows(D_FF, 1)], _rows(2 * D_FF),
                     jax.ShapeDtypeStruct((bsz, seq, 2 * D_FF), BF16))(da, h, h)


def _gate_up_dgrad_ln1_bwd(dh, wgut, dh1a, x, y1, mod, g, b, seq):
    ntok, d = x.shape
    row, per_b, whole = _ep_specs(seq, d)
    tiles = seq // EP_TILE

    def epilogue(du, i, j, ins, outs):
        dh_ref, x_ref, y_ref, mod_ref, g_ref, b_ref = ins
        dy_ref, dx_ref, acc_ref = outs
        y = y_ref[...]
        gate = 1.0 + mod_ref[0, 2:3, :]
        z = DEEPNORM_ALPHA * x_ref[...] + gate * y
        xhat, rstd = _ln_stats(z)
        h1 = xhat * g_ref[...] + b_ref[...]
        dh1 = dh_ref[...] + du * (1.0 + mod_ref[0, 4:5, :])
        dz = _ln_bwd(dh1 * g_ref[...], xhat, rstd)
        dy_ref[...] = (gate * dz).astype(BF16)
        dx_ref[...] = DEEPNORM_ALPHA * dz
        _acc_rows(acc_ref, i % tiles == 0,
                  [_colsum(dh1 * xhat), _colsum(dh1), _colsum(dz * y), _colsum(du * h1), _colsum(du)])

    return _matmul(dh, wgut, mode="nn", tm=EP_TILE, tn=d, tk=D_FF, name="gate_up_dgrad_ln1_bwd",
                   ins=[(dh1a,) + row, (x,) + row, (y1,) + row, (mod,) + per_b, (g,) + whole, (b,) + whole],
                   outs=[(jax.ShapeDtypeStruct((ntok, d), BF16),) + row, (jax.ShapeDtypeStruct((ntok, d), F32),) + row,
                         (jax.ShapeDtypeStruct((ntok // seq, 8, d), F32),) + per_b], epilogue=epilogue)


def _wo_dgrad_gate_bwd(dy1, wo, gab, ya, yb):
    ntok, d = ya.shape
    tm, tn = 1024, 512
    tile = ((tm, tn), lambda i, j: (i, j))
    tile_b = ((tm, tn), lambda i, j: (i, j + d // tn))

    def epilogue(dm_, i, j, ins, outs):
        ga_ref, gb_ref, ya_ref, yb_ref = ins
        dya_ref, dyb_ref, dga_ref, dgb_ref = outs
        sa, sb = _sigmoid(ga_ref[...].astype(F32)), _sigmoid(gb_ref[...].astype(F32))
        dya_ref[...] = (dm_ * sa).astype(BF16)
        dyb_ref[...] = (dm_ * sb).astype(BF16)
        dga_ref[...] = (dm_ * ya_ref[...].astype(F32) * sa * (1.0 - sa)).astype(BF16)
        dgb_ref[...] = (dm_ * yb_ref[...].astype(F32) * sb * (1.0 - sb)).astype(BF16)

    shp = jax.ShapeDtypeStruct((ntok, d), BF16)
    return _matmul(dy1, wo, mode="nt", tm=tm, tn=tn, tk=d, name="w_o_dgrad_gate_bwd",
                   ins=[(gab,) + tile, (gab,) + tile_b, (ya,) + tile, (yb,) + tile],
                   outs=[(shp,) + tile] * 4, epilogue=epilogue)


def _w_in_dgrad_grad_x(dproj, wint, dxa, x, mod, seq, token):
    ntok, d = x.shape
    row, per_b, _ = _ep_specs(seq, d)
    tiles = seq // EP_TILE

    def epilogue(du, i, j, ins, outs):
        dxa_ref, x_ref, mod_ref = ins
        gx_ref, acc_ref = outs
        gx_ref[...] = dxa_ref[...] + du * (1.0 + mod_ref[0, 1:2, :])
        _acc_rows(acc_ref, i % tiles == 0, [_colsum(du * x_ref[...]), _colsum(du)])

    return _matmul(dproj, wint, mode="nn", tm=EP_TILE, tn=d, tk=wint.shape[0] // 2, name="w_in_dgrad_grad_x", token=token,
                   ins=[(dxa,) + row, (x,) + row, (mod,) + per_b],
                   outs=[(jax.ShapeDtypeStruct((ntok, d), F32),) + row, (jax.ShapeDtypeStruct((ntok // seq, 8, d), F32),) + per_b],
                   epilogue=epilogue)


def _branch_b_gate_merge(ob, wbbt, gab, ya):
    ntok, d = ya.shape
    tm, tn = 1024, 512
    tile = ((tm, tn), lambda i, j: (i, j))
    tile_b = ((tm, tn), lambda i, j: (i, j + d // tn))

    def epilogue(yb, i, j, ins, outs):
        ga_ref, gb_ref, ya_ref = ins
        yb_ref, merged_ref = outs
        yb_ref[...] = yb.astype(BF16)
        merged_ref[...] = (_sigmoid(ga_ref[...].astype(F32)) * ya_ref[...].astype(F32)
                           + _sigmoid(gb_ref[...].astype(F32)) * yb).astype(BF16)

    shp = jax.ShapeDtypeStruct((ntok, d), BF16)
    return _matmul(ob, wbbt, mode="nt", tm=tm, tn=tn, tk=ob.shape[1], name="branch_b_gate_merge",
                   ins=[(gab,) + tile, (gab,) + tile_b, (ya,) + tile], outs=[(shp,) + tile] * 2, epilogue=epilogue)


def _segsum64(v):
    rows, width = v.shape
    ri = lax.broadcasted_iota(jnp.int32, (LANES, LANES), 0) // HEAD_DIM
    ci = lax.broadcasted_iota(jnp.int32, (LANES, LANES), 1) // HEAD_DIM
    ones = jnp.where(ri == ci, 1.0, 0.0).astype(BF16)
    out = []
    for c in range(width // LANES):
        part = v[:, c * LANES:(c + 1) * LANES]
        hi = part.astype(BF16)
        lo = (part - hi.astype(F32)).astype(BF16)
        out.append(jnp.dot(hi, ones, preferred_element_type=F32) + jnp.dot(lo, ones, preferred_element_type=F32))
    return jnp.concatenate(out, axis=1) if len(out) > 1 else out[0]


def _merge_b(os_, ls_):
    bsz, seq, w = os_[0].shape

    def body(o0, o1, o2, l0, l1, l2, ob_ref):
        ls = [l0[0], l1[0], l2[0]]
        mx = jnp.maximum(jnp.maximum(ls[0], ls[1]), ls[2])
        es = [jnp.exp(l - mx) for l in ls]
        den = es[0] + es[1] + es[2]
        ob_ref[0] = ((es[0] / den) * o0[0] + (es[1] / den) * o1[0] + (es[2] / den) * o2[0]).astype(BF16)

    return _row_call(body, "merge_b", bsz, seq, [_rows(w)] * 6, _rows(w),
                     jax.ShapeDtypeStruct((bsz, seq, w), BF16))(*os_, *ls_)


def _branch_b_dgrad_merge_bwd(dyb, wbbt, os_, ls_):
    ntok, w = os_[0].shape
    row = ((EP_TILE, w), lambda i, j: (i, 0))

    def epilogue(dob_, i, j, ins, outs):
        os_r, ls_r = ins[:3], ins[3:]
        do_r, dd_r = outs[:3], outs[3:]
        ls = [l[...] for l in ls_r]
        mx = jnp.maximum(jnp.maximum(ls[0], ls[1]), ls[2])
        es = [jnp.exp(l - mx) for l in ls]
        den = es[0] + es[1] + es[2]
        ws = [e / den for e in es]
        dws = [_segsum64(dob_ * o[...]) for o in os_r]
        mean = ws[0] * dws[0] + ws[1] * dws[1] + ws[2] * dws[2]
        for wg, do_ref, dd_ref in zip(ws, do_r, dd_r):
            do_ref[...] = wg * dob_
            dd_ref[...] = -wg * mean

    shp = jax.ShapeDtypeStruct((ntok, w), F32)
    return _matmul(dyb, wbbt, mode="nn", tm=EP_TILE, tn=w, tk=dyb.shape[1], name="branch_b_dgrad_merge_bwd",
                   ins=[(v,) + row for v in list(os_) + list(ls_)], outs=[(shp,) + row] * 6, epilogue=epilogue)


def _branch_a_dgrad_delta(dya, wba, oa, lse_a, sinks_exp, seq):
    ntok, w = oa.shape
    row, per_b, whole = _ep_specs(seq, w)
    tiles = seq // EP_TILE

    def epilogue(do_, i, j, ins, outs):
        o_ref, l_ref, s_ref = ins
        do_ref, dd_ref, acc_ref = outs
        dd = -_segsum64(do_ * o_ref[...])
        do_ref[...] = do_
        dd_ref[...] = dd
        _acc_rows(acc_ref, i % tiles == 0, [_colsum(dd * jnp.exp(s_ref[...] - l_ref[...]))])

    shp = jax.ShapeDtypeStruct((ntok, w), F32)
    return _matmul(dya, wba, mode="nt", tm=EP_TILE, tn=w, tk=dya.shape[1], name="branch_a_dgrad_delta",
                   ins=[(oa,) + row, (lse_a,) + row, (sinks_exp,) + whole],
                   outs=[(shp,) + row, (shp,) + row, (jax.ShapeDtypeStruct((ntok // seq, 8, w), F32),) + per_b],
                   epilogue=epilogue)


def _swap_halves(v):
    src = lax.broadcasted_iota(jnp.int32, (LANES, LANES), 0)
    dst = lax.broadcasted_iota(jnp.int32, (LANES, LANES), 1)
    partner = jnp.where((dst % HEAD_DIM) < HEAD_DIM // 2, dst + HEAD_DIM // 2, dst - HEAD_DIM // 2)
    perm = jnp.where(src == partner, 1.0, 0.0).astype(BF16)
    hi = v.astype(BF16)
    lo = (v - hi.astype(F32)).astype(BF16)
    return jnp.dot(hi, perm, preferred_element_type=F32) + jnp.dot(lo, perm, preferred_element_type=F32)


def _swap_halves_roll(v):
    lane = lax.broadcasted_iota(jnp.int32, v.shape, 1)
    return jnp.where((lane % HEAD_DIM) < HEAD_DIM // 2, pltpu.roll(v, LANES - HEAD_DIM // 2, 1),
                     pltpu.roll(v, HEAD_DIM // 2, 1))


def _rope(v, cos, sin, sign=1.0, mxu=True):
    swap = _swap_halves if mxu else _swap_halves_roll
    out = []
    for c in range(v.shape[1] // LANES):
        part = v[:, c * LANES:(c + 1) * LANES]
        out.append(part * cos + sign * (swap(part) * sin))
    return jnp.concatenate(out, axis=1) if len(out) > 1 else out[0]


def _half_mask(shape, half):
    lane = lax.broadcasted_iota(jnp.int32, shape, len(shape) - 1) % LANES
    return (lane < HEAD_DIM) if half == 0 else (lane >= HEAD_DIM)


def _dup_half(v, half):
    return jnp.where(_half_mask(v.shape, half), v, pltpu.roll(v, HEAD_DIM, 1))


def _fold_halves(v):
    return v + pltpu.roll(v, HEAD_DIM, 1)


def _pick_halves(lo_rows, hi_rows):
    return jnp.where(_half_mask(lo_rows.shape, 0), lo_rows, hi_rows)


def _stack_masked(v, pairs):
    parts = []
    for c in pairs:
        pair = v[:, c * LANES:(c + 1) * LANES]
        parts += [jnp.where(_half_mask(pair.shape, half), pair, 0.0) for half in (0, 1)]
    return jnp.concatenate(parts, axis=0)


def _stack_pair_cols(v, pairs):
    return jnp.concatenate([v[:, c * LANES + half * HEAD_DIM:c * LANES + half * HEAD_DIM + 1] for c in pairs for half in (0, 1)],
                           axis=0)


ATTN_UNITS = 16


def _class_rows(r):
    return [pl.ds(0, QBLOCK)] if r == 1 else [pl.ds(rho, QBLOCK, stride=r) for rho in range(r)]


def _band_mask(nrows, nk, blk, n_back, has_prev):
    qi = lax.broadcasted_iota(jnp.int32, (nrows, nk), 0) % QBLOCK
    ki = lax.broadcasted_iota(jnp.int32, (nrows, nk), 1)
    if has_prev:
        dist = qi + QBLOCK - ki
        return (dist >= 0) & (dist <= n_back) & ((ki >= QBLOCK) | (blk > 0))
    dist = qi - ki
    return (dist >= 0) & (dist <= n_back)


def _attn_fwd(q_arr, k_arr, v_arr, *, name, npair, gqa, q_col, k_col, v_col, nchunk, r, n_back, sinks=None):
    bsz, seq, _ = q_arr.shape
    rr = QBLOCK * r
    nblk = seq // rr
    qw = npair * LANES
    kw = LANES if gqa else qw
    has_prev = nblk > 1
    has_sink = sinks is not None
    scale = HEAD_DIM ** -0.5

    def body(*refs):
        refs = list(refs)
        q_ref, kc_ref, vc_ref = refs[:3]
        pos = 3
        if has_prev:
            kp_ref, vp_ref = refs[pos:pos + 2]
            pos += 2
        if has_sink:
            sink_ref = refs[pos]
            pos += 1
        o_ref, lse_ref = refs[pos:pos + 2]
        blk = pl.program_id(2)
        nk = (2 if has_prev else 1) * QBLOCK
        valid = _band_mask(QBLOCK, nk, blk, n_back, has_prev)
        per = npair // 2
        classes = _class_rows(r)
        step = max(1, ATTN_UNITS // (2 * npair))
        for first in range(0, len(classes), step):
            batch = classes[first:first + step]
            units = []
            for ci, rows in enumerate(batch):
                q = q_ref[0, rows, :] * scale
                k, v = kc_ref[0, rows, :], vc_ref[0, rows, :]
                if has_prev:
                    k = jnp.concatenate([kp_ref[0, rows, :], k], axis=0)
                    v = jnp.concatenate([vp_ref[0, rows, :], v], axis=0)
                if gqa:
                    kdup = [_dup_half(k, hk).astype(BF16) for hk in range(2)]
                    vdup = [_dup_half(v, hk) for hk in range(2)]
                for c in range(npair):
                    sl = slice(c * LANES, (c + 1) * LANES)
                    qc = q[:, sl]
                    kc, vc = (kdup[c // per], vdup[c // per]) if gqa else (k[:, sl].astype(BF16), v[:, sl])
                    for half in (0, 1):
                        qm = jnp.where(_half_mask(qc.shape, half), qc, 0.0).astype(BF16)
                        vm = jnp.where(_half_mask(vc.shape, half), vc, 0.0).astype(BF16)
                        s = lax.dot_general(qm, kc, _DIMS["nt"], preferred_element_type=F32)
                        units.append(dict(ci=ci, c=c, half=half, s=s, vm=vm, sk=sink_ref[2 * c + half] if has_sink else None))
            for u in units:
                s = jnp.where(valid, u["s"], NEG_INF)
                m = jnp.max(s, axis=1, keepdims=True)
                if has_sink:
                    m = jnp.maximum(m, u["sk"])
                p = jnp.exp(s - m)
                den = jnp.sum(p, axis=1, keepdims=True)
                if has_sink:
                    den = den + jnp.exp(u["sk"] - m)
                u.update(p=p.astype(BF16), den=den, lse=m + jnp.log(den))
            for u in units:
                u["o"] = jnp.dot(u["p"], u["vm"], preferred_element_type=F32) / u["den"]
            for ci, rows in enumerate(batch):
                outs, lses = [None] * npair, [None] * npair
                for u in units:
                    if u["ci"] != ci:
                        continue
                    c, o = u["c"], u["o"]
                    lse = jnp.broadcast_to(u["lse"], o.shape)
                    outs[c] = o if u["half"] == 0 else outs[c] + o
                    lses[c] = lse if u["half"] == 0 else _pick_halves(lses[c], lse)
                o_ref[0, rows, :] = jnp.concatenate(outs, axis=1) if npair > 1 else outs[0]
                lse_ref[0, rows, :] = jnp.concatenate(lses, axis=1) if npair > 1 else lses[0]

    def cur(width, col0):
        return pl.BlockSpec((1, rr, width), lambda b, c, i: (b, i, col0 + c))

    def prev(width, col0):
        return pl.BlockSpec((1, rr, width), lambda b, c, i: (b, jnp.maximum(i - 1, 0), col0 + c))

    in_specs = [cur(qw, q_col), cur(kw, k_col), cur(kw, v_col)]
    args = [q_arr, k_arr, v_arr]
    if has_prev:
        in_specs += [prev(kw, k_col), prev(kw, v_col)]
        args += [k_arr, v_arr]
    if has_sink:
        in_specs.append(pl.BlockSpec(memory_space=pltpu.SMEM))
        args.append(sinks)
    return pl.pallas_call(
        body,
        name=name,
        grid=(bsz, nchunk, nblk),
        in_specs=in_specs,
        out_specs=[pl.BlockSpec((1, rr, qw), lambda b, c, i: (b, i, c))] * 2,
        out_shape=[jax.ShapeDtypeStruct((bsz, seq, nchunk * qw), F32)] * 2,
        compiler_params=_params("parallel", "parallel", "parallel"),
    )(*args)


def _attn_bwd(q_arr, k_arr, v_arr, cos, sin, do, lse, dd, *, name, npair, gqa, q_col, k_col, v_col, nchunk, r, n_back,
              token=None):
    bsz, seq, _ = q_arr.shape
    rr = QBLOCK * r
    nblk = seq // rr
    qw = npair * LANES
    kw = LANES if gqa else qw
    has_next = nblk > 1
    has_token = token is not None
    scale = HEAD_DIM ** -0.5

    def body(*refs):
        refs = list(refs)
        k_ref, v_ref, c_ref, s_ref = refs[:4]
        tile_refs = [refs[4:8]]
        pos = 8
        if has_next:
            tile_refs.append(refs[pos:pos + 4])
            pos += 4
        if has_token:
            pos += 1
        dq_ref, dk_ref, dv_ref = refs[pos:pos + 3]
        carry_ref = refs[pos + 3]
        blk = pl.program_id(2)
        if has_next:
            @pl.when(blk == 0)
            def _():
                carry_ref[...] = jnp.zeros_like(carry_ref)

        nrows = (npair if gqa else 1) * QBLOCK
        qi = lax.broadcasted_iota(jnp.int32, (nrows, QBLOCK), 0) % QBLOCK
        ki = lax.broadcasted_iota(jnp.int32, (nrows, QBLOCK), 1)
        valids = [qi >= ki, (qi + QBLOCK - ki <= n_back) & (blk + 1 < nblk)]
        per = npair // 2
        ntile = len(tile_refs)
        cat = lambda parts: jnp.concatenate(parts, axis=1) if len(parts) > 1 else parts[0]
        classes = _class_rows(r)
        step = max(1, ATTN_UNITS // (ntile * (2 if gqa else 2 * npair)))
        for first in range(0, len(classes), step):
            batch = classes[first:first + step]
            units = []
            for ci, rows in enumerate(batch):
                tiles = [(q_ref[0, rows, :] * scale, do_ref[0, rows, :], l_ref[0, rows, :], d_ref[0, rows, :])
                         for q_ref, do_ref, l_ref, d_ref in tile_refs]
                k, v = k_ref[0, rows, :], v_ref[0, rows, :]
                if gqa:
                    for hk in range(2):
                        pairs = list(range(hk * per, (hk + 1) * per))
                        kd, vd = _dup_half(k, hk).astype(BF16), _dup_half(v, hk).astype(BF16)
                        for t, (q, do_, l_, d_) in enumerate(tiles):
                            units.append(dict(ci=ci, t=t, hk=hk, pairs=pairs, qs=_stack_masked(q, pairs).astype(BF16),
                                              dos=_stack_masked(do_, pairs).astype(BF16), lcol=_stack_pair_cols(l_, pairs),
                                              dcol=_stack_pair_cols(d_, pairs), kmat=kd, vmat=vd, kdq=kd))
                else:
                    for c in range(npair):
                        sl = slice(c * LANES, (c + 1) * LANES)
                        kc, vcb = k[:, sl], v[:, sl].astype(BF16)
                        kcb = kc.astype(BF16)
                        for t, (q, do_, l_, d_) in enumerate(tiles):
                            for half in (0, 1):
                                hm = _half_mask(kc.shape, half)
                                col = c * LANES + half * HEAD_DIM
                                units.append(dict(ci=ci, t=t, c=c, half=half, qs=jnp.where(hm, q[:, sl], 0.0).astype(BF16),
                                                  dos=jnp.where(hm, do_[:, sl], 0.0).astype(BF16), lcol=l_[:, col:col + 1],
                                                  dcol=d_[:, col:col + 1], kmat=kcb, vmat=vcb,
                                                  kdq=jnp.where(hm, kc, 0.0).astype(BF16)))
            for u in units:
                u["s"] = lax.dot_general(u["qs"], u["kmat"], _DIMS["nt"], preferred_element_type=F32)
                u["dp"] = lax.dot_general(u["dos"], u["vmat"], _DIMS["nt"], preferred_element_type=F32)
            for u in units:
                p = jnp.exp(jnp.where(valids[u["t"]], u["s"], NEG_INF) - u["lcol"])
                u["ds"] = (p * (u["dp"] + u["dcol"])).astype(BF16)
                u["p"] = p.astype(BF16)
            for u in units:
                u["dv"] = lax.dot_general(u["p"], u["dos"], _DIMS["tn"], preferred_element_type=F32)
                u["dk"] = lax.dot_general(u["ds"], u["qs"], _DIMS["tn"], preferred_element_type=F32)
                u["dq"] = jnp.dot(u["ds"], u["kdq"], preferred_element_type=F32) * scale
            for ci, rows in enumerate(batch):
                mine = [u for u in units if u["ci"] == ci]
                dq = [[None] * npair for _ in range(ntile)]
                if gqa:
                    dk_out = dv_out = None
                    for hk in range(2):
                        us = [u for u in mine if u["hk"] == hk]
                        for u in us:
                            for i, c in enumerate(u["pairs"]):
                                dq[u["t"]][c] = _pick_halves(u["dq"][2 * i * QBLOCK:(2 * i + 1) * QBLOCK],
                                                             u["dq"][(2 * i + 1) * QBLOCK:(2 * i + 2) * QBLOCK])
                        dk_h = _fold_halves(functools.reduce(jnp.add, [u["dk"] for u in us]))
                        dv_h = _fold_halves(functools.reduce(jnp.add, [u["dv"] for u in us]))
                        dk_out = dk_h if hk == 0 else _pick_halves(dk_out, dk_h)
                        dv_out = dv_h if hk == 0 else _pick_halves(dv_out, dv_h)
                else:
                    dks, dvs = [], []
                    for c in range(npair):
                        us = [u for u in mine if u["c"] == c]
                        dks.append(functools.reduce(jnp.add, [u["dk"] for u in us]))
                        dvs.append(functools.reduce(jnp.add, [u["dv"] for u in us]))
                        for t in range(ntile):
                            dq[t][c] = functools.reduce(jnp.add, [u["dq"] for u in us if u["t"] == t])
                    dk_out, dv_out = cat(dks), cat(dvs)
                ck, sk_ = c_ref[0, rows, :], s_ref[0, rows, :]
                dk_ref[0, rows, :] = _rope(dk_out, ck, sk_, sign=-1.0, mxu=gqa)
                dv_ref[0, rows, :] = dv_out
                dq_cur = cat(dq[0])
                if has_next:
                    dq_cur = dq_cur + carry_ref[rows, :]
                    carry_ref[rows, :] = cat(dq[1])
                dq_ref[0, rows, :] = _rope(dq_cur, ck, sk_, sign=-1.0, mxu=gqa)

    def at(width, col0, shift):
        return pl.BlockSpec((1, rr, width), lambda b, c, i: (b, jnp.minimum(i + shift, nblk - 1), col0 + c))

    in_specs = [at(kw, k_col, 0), at(kw, v_col, 0), pl.BlockSpec((1, rr, LANES), lambda b, c, i: (b, i, 0)),
                pl.BlockSpec((1, rr, LANES), lambda b, c, i: (b, i, 0))]
    args = [k_arr, v_arr, cos, sin]
    for shift in (0, 1) if has_next else (0,):
        in_specs += [at(qw, q_col, shift), at(qw, 0, shift), at(qw, 0, shift), at(qw, 0, shift)]
        args += [q_arr, do, lse, dd]
    if has_token:
        in_specs.append(pl.BlockSpec(token.shape, lambda b, c, i: (0, 0)))
        args.append(token)
    return pl.pallas_call(
        body,
        name=name,
        grid=(bsz, nchunk, nblk),
        in_specs=in_specs,
        out_specs=[pl.BlockSpec((1, rr, qw), lambda b, c, i: (b, i, c)),
                   pl.BlockSpec((1, rr, kw), lambda b, c, i: (b, i, c)),
                   pl.BlockSpec((1, rr, kw), lambda b, c, i: (b, i, c))],
        out_shape=[jax.ShapeDtypeStruct((bsz, seq, nchunk * qw), F32),
                   jax.ShapeDtypeStruct((bsz, seq, nchunk * kw), F32),
                   jax.ShapeDtypeStruct((bsz, seq, nchunk * kw), F32)],
        scratch_shapes=[pltpu.VMEM((rr, qw) if has_next else (8, LANES), F32)],
        compiler_params=_params("parallel", "parallel", "arbitrary"),
    )(*args)


B_CHUNKS = {1: (4, 1), 4: (1, 4), 16: (1, 4)}


def _rope_tables(positions):
    half = HEAD_DIM // 2
    inv = ROPE_THETA ** (-jnp.arange(half, dtype=F32) / half)
    ang = positions.astype(F32)[..., None] * inv
    cos, sin = jnp.cos(ang), jnp.sin(ang)
    return jnp.concatenate([cos] * 4, axis=-1), jnp.concatenate([-sin, sin, -sin, sin], axis=-1)


def _layer_step(x, mod, positions, sinks, ln1_g, ln1_b, ln2_g, ln2_b, target, get_w_in, get_rest, hook):
    bsz, seq, d = x.shape
    ntok = bsz * seq
    flat = lambda v: v.reshape(ntok, v.shape[-1])
    unflat = lambda v: v.reshape(bsz, seq, v.shape[-1])
    cos, sin = _rope_tables(positions)
    mm = functools.partial(_matmul, tm=1024, tk=1024)
    scalar = lambda tok: 0.0 if tok is None else tok[0, 0]

    u1 = _modulate_in(x, mod)
    u1f = flat(u1)
    wint = get_w_in(u1)
    cosf, sinf = flat(cos), flat(sin)
    proj = functools.partial(_proj_rope, u1f, wint, cosf, sinf, tm=2048)
    qa = unflat(proj(n=1024, b_off=OFF_QA, rope_cols=1024, tn=512, name="proj_qa"))
    kva = unflat(proj(n=256, b_off=OFF_KVA, rope_cols=128, tn=128, name="proj_kva"))
    qkvb = unflat(proj(n=4608, b_off=OFF_QKVB, rope_cols=3072, tn=256, name="proj_qkvb"))
    gab = unflat(proj(n=2048, b_off=OFF_GAB, rope_cols=0, tn=256, name="proj_gab", out_dtype=BF16))

    sink_vec = sinks.reshape(A_Q_HEADS) + scalar(hook("projected", gab))
    a_kw = dict(npair=A_Q_HEADS // 2, gqa=True, q_col=0, k_col=0, v_col=1, nchunk=1, r=1, n_back=A_WINDOW - 1)
    oa, lse_a = _attn_fwd(qa, kva, kva, name="attn_a_fwd", sinks=sink_vec, **a_kw)
    rest = get_rest(oa)
    wba, wbbt, wo, wgut, wd = (rest[n] for n in ("w_branch_a", "w_branch_b", "w_o", "w_gate_up", "w_down"))
    ya = unflat(mm(flat(oa), wba, mode="nn", out_dtype=BF16, tn=512, name="branch_a"))

    b_kws, os_, ls_ = [], [], []
    for g, (window, r) in enumerate(B_PATTERNS):
        npair, nch = B_CHUNKS[r]
        per = B_HEADS_PER_GROUP // (2 * npair)
        nsec = len(B_PATTERNS) * per
        kw_ = dict(npair=npair, gqa=False, q_col=g * per, k_col=nsec + g * per, v_col=2 * nsec + g * per, nchunk=nch, r=r,
                   n_back=window // r)
        b_kws.append(kw_)
        o_g, l_g = _attn_fwd(qkvb, qkvb, qkvb, name=f"attn_b{g}_fwd", **kw_)
        os_.append(o_g)
        ls_.append(l_g)
    ob = _merge_b(os_, ls_)
    ybf, mergedf = _branch_b_gate_merge(flat(ob), wbbt, flat(gab), flat(ya))
    xf = flat(x)
    y1f, h1f, u2f = _wo_ln1(mergedf, wo, xf, mod, ln1_g, ln1_b, seq)
    wgut_i = _interleave_gate_up(wgut)
    hf, af = _gate_up_silu(u2f, wgut_i)

    dy2f, dh1af, acc2 = _down_ln2_loss_bwd(af, wd, h1f, mod, ln2_g, ln2_b, flat(target), seq)
    g_wd = _matmul(af, dy2f, mode="tn", out_dtype=BF16, tm=256, tn=1024, tk=ntok, name="down_wgrad")
    dhf = _down_dgrad_silu_bwd(dy2f, wd, hf)
    g_wgut = _interleave_gate_up(_matmul(dhf, u2f, mode="tn", out_dtype=BF16, tm=256, tn=1024, tk=ntok, name="gate_up_wgrad"))
    dy1f, dxaf, acc1 = _gate_up_dgrad_ln1_bwd(dhf, wgut_i, dh1af, xf, y1f, mod, ln1_g, ln1_b, seq)
    g_wo = _matmul(mergedf, dy1f, mode="tn", out_dtype=BF16, tm=256, tn=1024, tk=ntok, name="w_o_wgrad")
    dyaf, dybf, dgaf, dgbf = _wo_dgrad_gate_bwd(dy1f, wo, flat(gab), flat(ya), ybf)
    g_wba = _matmul(flat(oa), dyaf, mode="tn", out_dtype=BF16, tm=256, tn=1024, tk=ntok, name="branch_a_wgrad")
    g_wbbt = _matmul(dybf, flat(ob), mode="tn", out_dtype=BF16, tm=256, tn=512, tk=ntok, name="branch_b_wgrad")
    tok = hook("grads_rest", dict(w_branch_a=g_wba, w_branch_b=g_wbbt, w_o=g_wo, w_gate_up=g_wgut, w_down=g_wd))

    sinks_exp = jnp.repeat(sinks.reshape(1, A_Q_HEADS), HEAD_DIM, axis=1) + scalar(tok)
    doa, dd_a, acc_s = _branch_a_dgrad_delta(dyaf, wba, flat(oa), flat(lse_a), sinks_exp, seq)
    doa, dd_a = unflat(doa), unflat(dd_a)
    tok = hook("delta_done", dd_a)
    dqa, dka, dva = _attn_bwd(qa, kva, kva, cos, sin, doa, lse_a, dd_a, name="attn_a_bwd", token=tok, **a_kw)
    merged_bwd = [unflat(t) for t in _branch_b_dgrad_merge_bwd(dybf, wbbt, [flat(t) for t in os_], [flat(t) for t in ls_])]
    dqs, dks, dvs = [], [], []
    for g in range(len(B_PATTERNS)):
        dq_g, dk_g, dv_g = _attn_bwd(qkvb, qkvb, qkvb, cos, sin, merged_bwd[g], ls_[g], merged_bwd[3 + g],
                                     name=f"attn_b{g}_bwd", **b_kws[g])
        dqs.append(dq_g)
        dks.append(dk_g)
        dvs.append(dv_g)
    dproj = jnp.concatenate([t.astype(BF16) for t in [dqa, dka, dva] + dqs + dks + dvs] + [unflat(dgaf), unflat(dgbf)], axis=-1)
    dprojf = flat(dproj)
    g_wint = _matmul(dprojf, u1f, mode="tn", out_dtype=BF16, tm=256, tn=1024, tk=ntok, name="w_in_wgrad")
    tok = hook("grads_w_in", dict(w_in=g_wint))
    grad_x, acc0 = _w_in_dgrad_grad_x(dprojf, wint, dxaf, xf, mod, seq, tok)
    grad_x = unflat(grad_x)
    tok = hook("dgrad_done", grad_x)

    loss_part = jnp.sum(acc2[:, 3, 0])
    dmod = jnp.stack([acc0[:, 1], acc0[:, 0], acc1[:, 2], acc1[:, 4], acc1[:, 3], acc2[:, 2]], axis=1)
    small = jnp.stack([acc1[:, 0].sum(0), acc1[:, 1].sum(0), acc2[:, 0].sum(0), acc2[:, 1].sum(0), acc_s[:, 0].sum(0)])
    small = small + scalar(tok)
    return loss_part, grad_x, dmod, small


CHIP_FLIPS = (2, 4, 6)


def _my_place():
    return lax.axis_index("x"), lax.axis_index("y"), lax.axis_index("c")


def _flip(place, k):
    px, py, pc = place
    return (1 - px if k & 4 else px, 1 - py if k & 2 else py, 1 - pc if k & 1 else pc)


def _index(place):
    return 4 * place[0] + 2 * place[1] + place[2]


def _gather_small(v, name):
    rows, cols = v.shape

    def body(v_ref, out_ref, send_sems, recv_sems):
        me = _my_place()
        out_ref[_index(me)] = v_ref[...]
        copies = []
        for k in range(1, N_DEV):
            copies.append(pltpu.make_async_remote_copy(
                src_ref=v_ref, dst_ref=out_ref.at[_index(me)], send_sem=send_sems.at[k - 1], recv_sem=recv_sems.at[k - 1],
                device_id=_flip(me, k), device_id_type=MESH))
        for cp in copies:
            cp.start()
        for k in range(1, N_DEV):
            pltpu.make_async_remote_copy(
                src_ref=v_ref, dst_ref=out_ref.at[_index(_flip(me, k))], send_sem=send_sems.at[k - 1],
                recv_sem=recv_sems.at[k - 1], device_id=_flip(me, k), device_id_type=MESH).wait_recv()
        for cp in copies:
            cp.wait_send()

    return pl.pallas_call(
        body,
        name=name,
        out_shape=jax.ShapeDtypeStruct((N_DEV, rows, cols), v.dtype),
        in_specs=[pl.BlockSpec(memory_space=pltpu.VMEM)],
        out_specs=pl.BlockSpec(memory_space=pltpu.VMEM),
        scratch_shapes=[pltpu.SemaphoreType.DMA((N_DEV - 1,)), pltpu.SemaphoreType.DMA((N_DEV - 1,))],
        compiler_params=pltpu.CompilerParams(vmem_limit_bytes=VMEM_LIMIT_BYTES),
    )(v)


_HBM = pl.BlockSpec(memory_space=pltpu.HBM)
_SEM = pl.BlockSpec(memory_space=pltpu.SEMAPHORE)
_EFFECT = pltpu.SideEffectType.DATAFLOW_SIDE_EFFECTING


def _remote(src, dst, send_sems, recv_sems, j, to):
    return pltpu.make_async_remote_copy(src_ref=src, dst_ref=dst, send_sem=send_sems.at[j], recv_sem=recv_sems.at[j],
                                        device_id=to, device_id_type=MESH)


def _copies_start(name, bufs, make_copies, nsem):
    nbuf = len(bufs)

    def body(*refs):
        for cp in make_copies(refs[:nbuf], refs[nbuf], refs[nbuf + 1]):
            cp.start()
        refs[-1][...] = jnp.zeros_like(refs[-1])

    sems = pltpu.SemaphoreType.DMA((nsem,))
    res = pl.pallas_call(
        body, name=name,
        out_shape=(sems, sems, *[pltpu.HBM(v.shape, v.dtype) for v in bufs], jax.ShapeDtypeStruct((8, LANES), F32)),
        in_specs=(_HBM,) * nbuf, out_specs=(_SEM, _SEM) + (_HBM,) * nbuf + (pl.BlockSpec(memory_space=pltpu.VMEM),),
        input_output_aliases={i: 2 + i for i in range(nbuf)},
        compiler_params=pltpu.CompilerParams(has_side_effects=_EFFECT),
    )(*[pltpu.with_memory_space_constraint(v, pltpu.HBM) for v in bufs])
    return res[0], res[1], list(res[2:2 + nbuf]), res[-1]


def _copies_wait(name, started, make_copies, after):
    send_sems, recv_sems, bufs, _ = started
    nbuf = len(bufs)

    def body(*refs):
        for cp in make_copies(refs[:nbuf], refs[nbuf], refs[nbuf + 1]):
            cp.wait_send()
            cp.wait_recv()

    return list(pl.pallas_call(
        body, name=name,
        out_shape=tuple(pltpu.HBM(v.shape, v.dtype) for v in bufs),
        in_specs=(_HBM,) * nbuf + (_SEM, _SEM, pl.BlockSpec(memory_space=pl.ANY)), out_specs=(_HBM,) * nbuf,
        input_output_aliases={i: i for i in range(nbuf)},
        compiler_params=pltpu.CompilerParams(has_side_effects=_EFFECT),
    )(*bufs, send_sems, recv_sems, after))


def _to_sibling_copies(refs, send_sems, recv_sems):
    src_ref, land_ref = refs
    me = _my_place()
    return [_remote(src_ref.at[q, 1 - me[2]], land_ref.at[q], send_sems, recv_sems, q, _flip(me, 1)) for q in range(4)]


def _to_chips_copies(refs, send_sems, recv_sems):
    src_ref, land_ref = refs
    me = _my_place()
    copies = []
    for j, k in enumerate(CHIP_FLIPS):
        to = _flip(me, k)
        copies.append(_remote(src_ref.at[2 * to[0] + to[1]], land_ref.at[j], send_sems, recv_sems, j, to))
    return copies


class _Gather:
    def __init__(self, name, blocks):
        self.name, self.n = name, len(blocks)
        at_me = (_index(_my_place()), 0, 0)
        lands = [lax.dynamic_update_slice(lax.empty((N_DEV,) + v.shape, v.dtype), v[None], at_me) for v in blocks]
        self.first = _copies_start(name + "_start", list(blocks) + lands, self._first_copies, 4 * self.n)
        self.token = self.first[3]

    def _first_copies(self, refs, send_sems, recv_sems):
        me = _my_place()
        return [_remote(refs[w], refs[self.n + w].at[_index(me)], send_sems, recv_sems, 4 * w + j, _flip(me, k))
                for w in range(self.n) for j, k in enumerate((1,) + CHIP_FLIPS)]

    def _pass_copies(self, refs, send_sems, recv_sems):
        me = _my_place()
        copies = []
        for w, land in enumerate(refs):
            for j, k in enumerate(CHIP_FLIPS):
                slot = land.at[_index(_flip(me, k))]
                copies.append(_remote(slot, slot, send_sems, recv_sems, 3 * w + j, _flip(me, 1)))
        return copies

    def pass_on(self, after):
        lands = _copies_wait(self.name + "_wait", self.first, self._first_copies, after)[self.n:]
        self.second = _copies_start(self.name + "_pass_start", lands, self._pass_copies, 3 * self.n)
        return self.second[3]

    def finish(self, after):
        return _copies_wait(self.name + "_pass_wait", self.second, self._pass_copies, after)


SUM_SPLIT = 2


def _sum_pairs(parts, theirs):
    nchip, _, rows, cols = parts.shape
    tile = rows // SUM_SPLIT

    def body(c_ref, a_ref, b_ref, o_ref):
        o_ref[...] = (a_ref[0].astype(F32) + b_ref[...].astype(F32)).astype(BF16)

    spec = pl.BlockSpec((1, tile, cols), lambda q, t, c_ref: (q, t, 0))
    grid_spec = pltpu.PrefetchScalarGridSpec(
        num_scalar_prefetch=1, grid=(nchip, SUM_SPLIT),
        in_specs=[pl.BlockSpec((1, 1, tile, cols), lambda q, t, c_ref: (q, c_ref[0], t, 0)), spec], out_specs=spec)
    return pl.pallas_call(body, name="grad_sum_sibling", grid_spec=grid_spec,
                          out_shape=jax.ShapeDtypeStruct((nchip, rows, cols), BF16),
                          compiler_params=_params("parallel", "parallel"))(lax.axis_index("c").reshape(1), parts, theirs)


def _sum_final(chip_sum, got):
    _, rows, cols = chip_sum.shape
    tile = rows // SUM_SPLIT

    def body(q_ref, a_ref, g_ref, o_ref):
        o_ref[...] = ((a_ref[0].astype(F32) + g_ref[0].astype(F32)) + g_ref[1].astype(F32)) + g_ref[2].astype(F32)

    grid_spec = pltpu.PrefetchScalarGridSpec(
        num_scalar_prefetch=1, grid=(SUM_SPLIT,),
        in_specs=[pl.BlockSpec((1, tile, cols), lambda t, q_ref: (q_ref[0], t, 0)),
                  pl.BlockSpec((3, tile, cols), lambda t, q_ref: (0, t, 0))],
        out_specs=pl.BlockSpec((tile, cols), lambda t, q_ref: (t, 0)))
    my_chip = (2 * lax.axis_index("x") + lax.axis_index("y")).reshape(1)
    return pl.pallas_call(body, name="grad_sum_chips", grid_spec=grid_spec, out_shape=jax.ShapeDtypeStruct((rows, cols), F32),
                          compiler_params=_params("parallel"))(my_chip, chip_sum, got)


class _ReduceScatter:
    def __init__(self, name, slabs):
        self.name, self.rows = name, slabs.shape[1]
        parts = slabs.reshape(4, 2, self.rows, D_MODEL)
        self.first = _copies_start(name + "_sibling_start", [parts, lax.empty((4, self.rows, D_MODEL), slabs.dtype)],
                                   _to_sibling_copies, 4)
        self.token = self.first[3]

    def between_chips(self, after):
        parts, theirs = _copies_wait(self.name + "_sibling_wait", self.first, _to_sibling_copies, after)
        chip_sum = _sum_pairs(parts, theirs)
        self.second = _copies_start(self.name + "_chips_start", [chip_sum, lax.empty((3, self.rows, D_MODEL), chip_sum.dtype)],
                                    _to_chips_copies, 3)
        return self.second[3]

    def finish(self, after):
        chip_sum, got = _copies_wait(self.name + "_chips_wait", self.second, _to_chips_copies, after)
        return _sum_final(chip_sum, got)


def _ada_fwd(c_all, w, b):
    nb, _ = c_all.shape
    ncol = w.shape[1]

    def body(c_ref, w_ref, b_ref, o_ref):
        c = c_ref[...]
        act = (c * _sigmoid(c)).astype(BF16)
        o_ref[...] = jnp.dot(act, w_ref[...].astype(BF16), preferred_element_type=F32) + b_ref[...]

    return pl.pallas_call(body, name="ada_fwd", out_shape=jax.ShapeDtypeStruct((nb, ncol), F32),
                          compiler_params=pltpu.CompilerParams(vmem_limit_bytes=VMEM_LIMIT_BYTES))(c_all, w, b)


def _ada_wgrad(c_all_t, dmod_cols):
    d, nb = c_all_t.shape
    ncol = dmod_cols.shape[1]

    def body(ct_ref, dm_ref, o_ref):
        ct = ct_ref[...]
        act = (ct * _sigmoid(ct)).astype(BF16).astype(F32)
        dm = dm_ref[...].astype(BF16).astype(F32)
        acc = act[:, 0:1] * dm[0:1, :]
        for i in range(1, nb):
            acc = acc + act[:, i:i + 1] * dm[i:i + 1, :]
        o_ref[...] = acc

    return pl.pallas_call(body, name="ada_wgrad", out_shape=jax.ShapeDtypeStruct((d, ncol), F32),
                          compiler_params=pltpu.CompilerParams(vmem_limit_bytes=VMEM_LIMIT_BYTES))(c_all_t, dmod_cols)


SMALL_ROWS = 24


def _reduce_small(gathered):
    def body(g_ref, o_ref):
        acc = g_ref[0]
        for dev in range(1, N_DEV):
            acc = acc + g_ref[dev]
        o_ref[...] = acc

    return pl.pallas_call(body, name="reduce_small", out_shape=jax.ShapeDtypeStruct(gathered.shape[1:], F32))(gathered)


def _adamw(w, g, m, v, name):
    rows, cols = w.shape
    tile = rows
    for cand in (256, 128, 64, 32, 16, 8):
        if rows % cand == 0 and rows > cand:
            tile = cand
            break
    spec = pl.BlockSpec((tile, cols), lambda t: (t, 0))
    bc1 = 1.0 - ADAM_B1 ** ADAM_STEP
    bc2 = 1.0 - ADAM_B2 ** ADAM_STEP

    def body(w_ref, g_ref, m_ref, v_ref, d_ref, nm_ref, nv_ref):
        g_ = g_ref[...]
        nm = ADAM_B1 * m_ref[...] + (1.0 - ADAM_B1) * g_
        nv = ADAM_B2 * v_ref[...] + (1.0 - ADAM_B2) * (g_ * g_)
        d_ref[...] = -ADAM_LR * ((nm / bc1) / (jnp.sqrt(nv / bc2) + ADAM_EPS) + ADAM_WD * w_ref[...])
        nm_ref[...] = nm
        nv_ref[...] = nv

    shp = jax.ShapeDtypeStruct((rows, cols), F32)
    return pl.pallas_call(body, name=name, grid=(rows // tile,), in_specs=[spec] * 4, out_specs=[spec] * 3, out_shape=[shp] * 3,
                          compiler_params=_params("parallel"))(w, g, m, v)


_WEIGHTS = ("w_ada", "b_ada", "w_in", "sinks", "w_branch_a", "w_branch_b", "w_o", "ln1_g", "ln1_b", "w_gate_up", "w_down",
            "ln2_g", "ln2_b")
_TRANSPOSED = ("w_in", "w_branch_b", "w_gate_up")


def _pack_shard(name, w):
    w = w.astype(BF16)
    if name in _TRANSPOSED:
        w = w.T
    return w.reshape(-1, D_MODEL)


def _unpack_full(name, slab):
    if name == "w_branch_b":
        return slab.reshape(N_DEV * 128, 512)
    return slab.reshape(-1, D_MODEL)


def _unpack_group(group, gathered):
    return {n: _unpack_full(n, slab) for (n, _), slab in zip(group, gathered)}


def _unpack_grads(group, g_packed):
    g_w, off = {}, 0
    for n, r in group:
        part = g_packed[off:off + r]
        off += r
        g_w[n] = part.reshape(128, 512) if n == "w_branch_b" else part
    return g_w


def kernel(x, c, positions, w_ada, b_ada, w_in, sinks, w_branch_a, w_branch_b, w_o, ln1_g, ln1_b, w_gate_up, w_down, ln2_g, ln2_b, loss_target, m_w_ada, m_b_ada, m_w_in, m_sinks, m_w_branch_a, m_w_branch_b, m_w_o, m_ln1_g, m_ln1_b, m_w_gate_up, m_w_down, m_ln2_g, m_ln2_b, v_w_ada, v_b_ada, v_w_in, v_sinks, v_w_branch_a, v_w_branch_b, v_w_o, v_ln1_g, v_ln1_b, v_w_gate_up, v_w_down, v_ln2_g, v_ln2_b):
    weights = dict(w_ada=w_ada, b_ada=b_ada, w_in=w_in, sinks=sinks, w_branch_a=w_branch_a, w_branch_b=w_branch_b, w_o=w_o,
                   ln1_g=ln1_g, ln1_b=ln1_b, w_gate_up=w_gate_up, w_down=w_down, ln2_g=ln2_g, ln2_b=ln2_b)
    m_in = dict(w_ada=m_w_ada, b_ada=m_b_ada, w_in=m_w_in, sinks=m_sinks, w_branch_a=m_w_branch_a, w_branch_b=m_w_branch_b,
                w_o=m_w_o, ln1_g=m_ln1_g, ln1_b=m_ln1_b, w_gate_up=m_w_gate_up, w_down=m_w_down, ln2_g=m_ln2_g, ln2_b=m_ln2_b)
    v_in = dict(w_ada=v_w_ada, b_ada=v_b_ada, w_in=v_w_in, sinks=v_sinks, w_branch_a=v_w_branch_a, w_branch_b=v_w_branch_b,
                w_o=v_w_o, ln1_g=v_ln1_g, ln1_b=v_ln1_b, w_gate_up=v_w_gate_up, w_down=v_w_down, ln2_g=v_ln2_g, ln2_b=v_ln2_b)
    bsz = x.shape[0]
    me = _index(_my_place())
    ada_cols = w_ada.shape[2]
    outs = {}

    def adamw(n, g):
        w2, m2, v2 = (t[n][0] if t[n].ndim == 3 else t[n] for t in (weights, m_in, v_in))
        shape = weights[n].shape
        if n in _TRANSPOSED:
            dlt, nm, nv = _adamw(w2.T, g, m2.T, v2.T, "adamw_" + n)
            outs[n] = tuple(t.T.reshape(shape) for t in (g, dlt, nm, nv))
        else:
            dlt, nm, nv = _adamw(w2, g, m2, v2, "adamw_" + n)
            outs[n] = tuple(t.reshape(shape) for t in (g, dlt, nm, nv))
        return nv

    packed_in = [_pack_shard(n, weights[n][0]) for n, _ in GROUP_IN]
    packed_rest = [_pack_shard(n, weights[n][0]) for n, _ in GROUP_REST]
    c_all = _gather_small(jnp.pad(c, ((0, 8 - bsz), (0, 0))), "gather_c")[:, :bsz].reshape(N_DEV * bsz, D_MODEL)
    gather_in = _Gather("gather_w_in", lax.optimization_barrier((packed_in, c_all))[0])
    b_cols = lax.dynamic_slice_in_dim(b_ada, me * ada_cols, ada_cols, axis=1)
    mod_cols = _ada_fwd(c_all, w_ada[0], b_cols + gather_in.token[0, 0])
    mod_all = _gather_small(mod_cols, "gather_mod").transpose(1, 0, 2).reshape(N_DEV * bsz, 6, D_MODEL)
    gather_rest = _Gather("gather_rest", lax.optimization_barrier((packed_rest, mod_all))[0])
    mod = jnp.pad(lax.dynamic_slice_in_dim(mod_all, me * bsz, bsz, axis=0), ((0, 0), (0, 2), (0, 0)))
    mod = mod + gather_rest.token[0, 0]
    mod = mod + gather_in.pass_on(mod)[0, 0]

    scatters = {}

    def get_w_in(after):
        return _unpack_group(GROUP_IN, gather_in.finish(after))["w_in"]

    def get_rest(after):
        return _unpack_group(GROUP_REST, gather_rest.finish(after))

    def pack_grads(group, grads):
        return jnp.concatenate([grads[n].reshape(N_DEV, r, D_MODEL) for n, r in group], axis=1)

    def hook(point, value):
        if point == "projected":
            return gather_rest.pass_on(value)
        if point == "grads_rest":
            scatters["rest"] = _ReduceScatter("scatter_rest", pack_grads(GROUP_REST, value))
            return scatters["rest"].token
        if point == "delta_done":
            return scatters["rest"].between_chips(value)
        if point == "grads_w_in":
            scatters["in"] = _ReduceScatter("scatter_w_in", pack_grads(GROUP_IN, value))
            return scatters["in"].token
        if point == "dgrad_done":
            tok = scatters["in"].between_chips(value)
            for n, g in _unpack_grads(GROUP_REST, scatters["rest"].finish(tok)).items():
                adamw(n, g)
            return tok
        raise ValueError(point)

    loss_part, grad_x, dmod, small = _layer_step(x, mod, positions, sinks[0], ln1_g, ln1_b, ln2_g, ln2_b, loss_target,
                                                 get_w_in, get_rest, hook)

    rows = jnp.concatenate([dmod.reshape(bsz * 6, D_MODEL), small, jnp.full((1, D_MODEL), loss_part, F32),
                            jnp.zeros((SMALL_ROWS - bsz * 6 - 6, D_MODEL), F32)], axis=0)
    small_all = _gather_small(rows, "gather_small")
    sums = _reduce_small(small_all)
    loss = sums[bsz * 6 + 5, 0]
    dmod_all = small_all[:, :bsz * 6].reshape(N_DEV * bsz, 6 * D_MODEL)
    adamw("b_ada", functools.reduce(jnp.add, [sums[6 * i:6 * i + 6] for i in range(bsz)]).reshape(1, 6 * D_MODEL))
    for i, n in enumerate(("ln1_g", "ln1_b", "ln2_g", "ln2_b")):
        adamw(n, sums[12 + i][None])
    adamw("sinks", sums[16][::HEAD_DIM][None])
    dmod_cols = lax.dynamic_slice_in_dim(dmod_all, me * ada_cols, ada_cols, axis=1)
    last = adamw("w_ada", _ada_wgrad(c_all.T, dmod_cols))
    for n, g in _unpack_grads(GROUP_IN, scatters["in"].finish(last)).items():
        adamw(n, g)

    return (loss, grad_x, *[outs[n][0] for n in _WEIGHTS], *[outs[n][1] for n in _WEIGHTS], *[outs[n][2] for n in _WEIGHTS],
            *[outs[n][3] for n in _WEIGHTS])
```

```python
import functools

import jax
import jax.numpy as jnp
from jax import lax
from jax.experimental import pallas as pl
from jax.experimental.pallas import tpu as pltpu

F32 = jnp.float32
BF16 = jnp.bfloat16

D_MODEL = 1024
HEAD_DIM = 64
A_Q_HEADS = 16
A_WINDOW = 128
B_PATTERNS = ((128, 1), (512, 4), (2048, 16))
B_HEADS_PER_GROUP = 8
D_FF = 2816
QBLOCK = 128
ROPE_THETA = 10000.0
LN_EPS = 1e-5
DEEPNORM_ALPHA = 2.0 ** 0.25
NEG_INF = -1e30
ADAM_LR, ADAM_B1, ADAM_B2, ADAM_EPS, ADAM_WD, ADAM_STEP = 0.001, 0.9, 0.999, 1e-08, 0.01, 10

N_DEV = 8
MESH_AXES = ("x", "y", "c")
LANES = 128
VMEM_LIMIT_BYTES = 56 * 1024 * 1024
MESH = pl.DeviceIdType.MESH

OFF_QA, OFF_KVA, OFF_QKVB, OFF_GAB = 0, 1024, 1280, 5888
GROUP_IN = (("w_in", 992),)
GROUP_REST = (("w_branch_a", 128), ("w_branch_b", 64), ("w_o", 128), ("w_gate_up", 704), ("w_down", 352))


def _params(*sem):
    return pltpu.CompilerParams(dimension_semantics=sem, vmem_limit_bytes=VMEM_LIMIT_BYTES)


def _sigmoid(x):
    return 1.0 / (1.0 + jnp.exp(-x))


_DIMS = {"nn": (((1,), (0,)), ((), ())), "nt": (((1,), (1,)), ((), ())), "tn": (((0,), (0,)), ((), ()))}


def _matmul(a, b, *, mode, tm, tn, tk, name, out_dtype=None, n=None, b_off=0, token=None, ins=(), outs=None, epilogue=None):
    if mode == "nn":
        (m, k), nn_ = a.shape, b.shape[1]
    elif mode == "nt":
        (m, k), nn_ = a.shape, (b.shape[0] if n is None else n)
    else:
        (k, m), nn_ = a.shape, b.shape[1]
    assert m % tm == 0 and nn_ % tn == 0 and k % tk == 0 and b_off % tn == 0, (name, m, nn_, k)
    nk = k // tk
    joff = b_off // tn
    if mode == "nn":
        a_spec = pl.BlockSpec((tm, tk), lambda i, j, kk: (i, kk))
        b_spec = pl.BlockSpec((tk, tn), lambda i, j, kk: (kk, j))
    elif mode == "nt":
        a_spec = pl.BlockSpec((tm, tk), lambda i, j, kk: (i, kk))
        b_spec = pl.BlockSpec((tn, tk), lambda i, j, kk: (j + joff, kk))
    else:
        a_spec = pl.BlockSpec((tk, tm), lambda i, j, kk: (kk, i))
        b_spec = pl.BlockSpec((tk, tn), lambda i, j, kk: (kk, j))
    dims = _DIMS[mode]
    has_token = token is not None
    plain = epilogue is None
    if plain:
        outs = [(jax.ShapeDtypeStruct((m, nn_), out_dtype), (tm, tn), lambda i, j: (i, j))]

        def epilogue(acc, i, j, in_refs, out_refs):
            out_refs[0][...] = acc.astype(out_refs[0].dtype)

    nin = len(ins)

    def body(*refs):
        a_ref, b_ref = refs[:2]
        in_refs = refs[2:2 + nin]
        out_refs = refs[2 + nin + has_token:-1]
        acc_ref = refs[-1]
        kk = pl.program_id(2)
        part = lax.dot_general(a_ref[...].astype(BF16), b_ref[...].astype(BF16), dims, preferred_element_type=F32)

        def finish(acc):
            epilogue(acc, pl.program_id(0), pl.program_id(1), in_refs, out_refs)

        if nk == 1:
            finish(part)
        else:
            @pl.when(kk == 0)
            def _():
                acc_ref[...] = part

            @pl.when(kk > 0)
            def _():
                acc_ref[...] += part

            @pl.when(kk == nk - 1)
            def _():
                finish(acc_ref[...])

    def spec(block, index):
        return pl.BlockSpec(block, lambda i, j, kk: index(i, j))

    in_specs, args = [a_spec, b_spec], [a, b]
    for arr, block, index in ins:
        in_specs.append(spec(block, index))
        args.append(arr)
    if has_token:
        in_specs.append(pl.BlockSpec(token.shape, lambda i, j, kk: (0, 0)))
        args.append(token)
    res = pl.pallas_call(
        body,
        name=name,
        grid=(m // tm, nn_ // tn, nk),
        in_specs=in_specs,
        out_specs=[spec(block, index) for _, block, index in outs],
        out_shape=[shape for shape, _, _ in outs],
        scratch_shapes=[pltpu.VMEM((tm, tn) if nk > 1 else (8, LANES), F32)],
        compiler_params=_params("arbitrary", "arbitrary", "arbitrary"),
    )(*args)
    return res[0] if plain else res


def _proj_rope(a, bt, cos, sin, *, n, b_off, rope_cols, tm, tn, name, out_dtype=F32):
    m, k = a.shape
    assert m % tm == 0 and n % tn == 0 and b_off % tn == 0 and rope_cols % tn == 0, name
    joff = b_off // tn
    nrope = rope_cols // tn

    def body(a_ref, b_ref, c_ref, s_ref, o_ref):
        acc = lax.dot_general(a_ref[...], b_ref[...], _DIMS["nt"], preferred_element_type=F32)
        j = pl.program_id(1)

        @pl.when(j < nrope)
        def _():
            o_ref[...] = _rope(acc, c_ref[...], s_ref[...]).astype(o_ref.dtype)

        @pl.when(j >= nrope)
        def _():
            o_ref[...] = acc.astype(o_ref.dtype)

    table = pl.BlockSpec((tm, LANES), lambda i, j: (i, 0))
    return pl.pallas_call(
        body,
        name=name,
        grid=(m // tm, n // tn),
        in_specs=[pl.BlockSpec((tm, k), lambda i, j: (i, 0)), pl.BlockSpec((tn, k), lambda i, j: (j + joff, 0)), table, table],
        out_specs=pl.BlockSpec((tm, tn), lambda i, j: (i, j)),
        out_shape=jax.ShapeDtypeStruct((m, n), out_dtype),
        compiler_params=_params("parallel", "parallel"),
    )(a, bt, cos, sin)


ROW_TILE = 256


def _rows(width, col=0):
    return pl.BlockSpec((1, ROW_TILE, width), lambda b, t: (b, t, col))


def _per_batch(nrows, width):
    return pl.BlockSpec((1, nrows, width), lambda b, t: (b, 0, 0))


def _whole(shape):
    return pl.BlockSpec(shape, lambda b, t: (0,) * len(shape))


def _row_call(body, name, bsz, seq, in_specs, out_specs, out_shape, accumulates=False):
    return pl.pallas_call(
        body,
        name=name,
        grid=(bsz, seq // ROW_TILE),
        in_specs=in_specs,
        out_specs=out_specs,
        out_shape=out_shape,
        compiler_params=_params("parallel", "arbitrary" if accumulates else "parallel"),
    )


def _acc_rows(acc_ref, first, rows):
    @pl.when(first)
    def _():
        acc_ref[...] = jnp.zeros_like(acc_ref)

    for r, val in enumerate(rows):
        acc_ref[0, r:r + 1, :] += val


def _colsum(v):
    return jnp.sum(v, axis=0, keepdims=True)


def _ln_stats(z):
    mu = jnp.mean(z, axis=-1, keepdims=True)
    zc = z - mu
    var = jnp.mean(zc * zc, axis=-1, keepdims=True)
    rstd = lax.rsqrt(var + LN_EPS)
    return zc * rstd, rstd


def _ln_bwd(dxhat, xhat, rstd):
    m1 = jnp.mean(dxhat, axis=-1, keepdims=True)
    m2 = jnp.mean(dxhat * xhat, axis=-1, keepdims=True)
    return rstd * (dxhat - m1 - xhat * m2)


def _modulate_in(x, mod):
    bsz, seq, d = x.shape

    def body(x_ref, mod_ref, u_ref):
        u_ref[0] = (x_ref[0] * (1.0 + mod_ref[0, 1:2, :]) + mod_ref[0, 0:1, :]).astype(BF16)

    return _row_call(body, "modulate_in", bsz, seq, [_rows(d), _per_batch(8, d)], _rows(d),
                     jax.ShapeDtypeStruct((bsz, seq, d), BF16))(x, mod)


def _gate_merge(gab, ya, yb):
    bsz, seq, d = ya.shape

    def body(ga_ref, gb_ref, ya_ref, yb_ref, o_ref):
        ga, gb, ya_, yb_ = (r[0].astype(F32) for r in (ga_ref, gb_ref, ya_ref, yb_ref))
        o_ref[0] = (_sigmoid(ga) * ya_ + _sigmoid(gb) * yb_).astype(BF16)

    return _row_call(body, "gate_merge", bsz, seq, [_rows(d, 0), _rows(d, 1), _rows(d), _rows(d)], _rows(d),
                     jax.ShapeDtypeStruct((bsz, seq, d), BF16))(gab, gab, ya, yb)


EP_TILE = 512


def _ep_specs(seq, d):
    tiles = seq // EP_TILE
    return ((EP_TILE, d), lambda i, j: (i, 0)), ((1, 8, d), lambda i, j: (i // tiles, 0, 0)), ((1, d), lambda i, j: (0, 0))


def _wo_ln1(merged, wo, x, mod, g, b, seq):
    ntok, d = x.shape
    row, per_b, whole = _ep_specs(seq, d)

    def epilogue(y, i, j, ins, outs):
        x_ref, mod_ref, g_ref, b_ref = ins
        y_ref, h_ref, u_ref = outs
        z = DEEPNORM_ALPHA * x_ref[...] + (1.0 + mod_ref[0, 2:3, :]) * y
        xhat, _ = _ln_stats(z)
        h = xhat * g_ref[...] + b_ref[...]
        y_ref[...] = y
        h_ref[...] = h
        u_ref[...] = (h * (1.0 + mod_ref[0, 4:5, :]) + mod_ref[0, 3:4, :]).astype(BF16)

    f32, bf16 = jax.ShapeDtypeStruct((ntok, d), F32), jax.ShapeDtypeStruct((ntok, d), BF16)
    return _matmul(merged, wo, mode="nn", tm=EP_TILE, tn=d, tk=d, name="w_o_ln1",
                   ins=[(x,) + row, (mod,) + per_b, (g,) + whole, (b,) + whole],
                   outs=[(f32,) + row, (f32,) + row, (bf16,) + row], epilogue=epilogue)


FF_HALF = D_FF // 2


def _interleave_gate_up(w):
    return w.reshape(2, 2, FF_HALF, w.shape[1]).transpose(1, 0, 2, 3).reshape(w.shape)


def _gate_up_silu(u2, wgut_i):
    ntok = u2.shape[0]

    def epilogue(h, i, j, ins, outs):
        h_ref, a_ref = outs
        hg, hu = h[:, :FF_HALF], h[:, FF_HALF:]
        h_ref[...] = h.astype(BF16)
        a_ref[...] = (hg * _sigmoid(hg) * hu).astype(BF16)

    return _matmul(u2, wgut_i, mode="nt", tm=EP_TILE, tn=2 * FF_HALF, tk=u2.shape[1], name="gate_up_silu",
                   outs=[(jax.ShapeDtypeStruct((ntok, 2 * D_FF), BF16), (EP_TILE, 2 * FF_HALF), lambda i, j: (i, j)),
                         (jax.ShapeDtypeStruct((ntok, D_FF), BF16), (EP_TILE, FF_HALF), lambda i, j: (i, j))],
                   epilogue=epilogue)


def _down_dgrad_silu_bwd(dy2, wd, h_i):
    ntok = dy2.shape[0]
    wide = ((EP_TILE, 2 * FF_HALF), lambda i, j: (i, j))

    def epilogue(da, i, j, ins, outs):
        h = ins[0][...].astype(F32)
        hg, hu = h[:, :FF_HALF], h[:, FF_HALF:]
        sg = _sigmoid(hg)
        outs[0][:, :FF_HALF] = (da * hu * (sg * (1.0 + hg * (1.0 - sg)))).astype(BF16)
        outs[0][:, FF_HALF:] = (da * (hg * sg)).astype(BF16)

    return _matmul(dy2, wd, mode="nt", tm=EP_TILE, tn=FF_HALF, tk=dy2.shape[1], name="down_dgrad_silu_bwd",
                   ins=[(h_i,) + wide], outs=[(jax.ShapeDtypeStruct((ntok, 2 * D_FF), BF16),) + wide], epilogue=epilogue)[0]


def _down_ln2_loss_bwd(a, wd, h1, mod, g, b, target, seq):
    ntok, d = h1.shape
    row, per_b, whole = _ep_specs(seq, d)
    tiles = seq // EP_TILE

    def epilogue(y, i, j, ins, outs):
        h_ref, mod_ref, g_ref, b_ref, t_ref = ins
        dy_ref, dh_ref, acc_ref = outs
        gate = 1.0 + mod_ref[0, 5:6, :]
        z = DEEPNORM_ALPHA * h_ref[...] + gate * y
        xhat, rstd = _ln_stats(z)
        diff = xhat * g_ref[...] + b_ref[...] - t_ref[...]
        loss = 0.5 * jnp.sum(jnp.sum(diff * diff, axis=-1, keepdims=True) / d, axis=0, keepdims=True)
        dout = diff / d
        dz = _ln_bwd(dout * g_ref[...], xhat, rstd)
        dy_ref[...] = (gate * dz).astype(BF16)
        dh_ref[...] = DEEPNORM_ALPHA * dz
        _acc_rows(acc_ref, i % tiles == 0,
                  [_colsum(dout * xhat), _colsum(dout), _colsum(dz * y), jnp.broadcast_to(loss, (1, d))])

    return _matmul(a, wd, mode="nn", tm=EP_TILE, tn=d, tk=a.shape[1], name="down_ln2_loss_bwd",
                   ins=[(h1,) + row, (mod,) + per_b, (g,) + whole, (b,) + whole, (target,) + row],
                   outs=[(jax.ShapeDtypeStruct((ntok, d), BF16),) + row, (jax.ShapeDtypeStruct((ntok, d), F32),) + row,
                         (jax.ShapeDtypeStruct((ntok // seq, 8, d), F32),) + per_b], epilogue=epilogue)


def _silu_mul_bwd(da, h):
    bsz, seq, _ = h.shape

    def body(da_ref, hg_ref, hu_ref, dh_ref):
        hg, da_ = hg_ref[0].astype(F32), da_ref[0].astype(F32)
        sg = _sigmoid(hg)
        dh_ref[0, :, :D_FF] = (da_ * hu_ref[0].astype(F32) * (sg * (1.0 + hg * (1.0 - sg)))).astype(BF16)
        dh_ref[0, :, D_FF:] = (da_ * (hg * sg)).astype(BF16)

    return _row_call(body, "silu_mul_bwd", bsz, seq, [_rows(D_FF), _rows(D_FF, 0), _rows(D_FF, 1)], _rows(2 * D_FF),
                     jax.ShapeDtypeStruct((bsz, seq, 2 * D_FF), BF16))(da, h, h)


def _gate_up_dgrad_ln1_bwd(dh, wgut, dh1a, x, y1, mod, g, b, seq):
    ntok, d = x.shape
    row, per_b, whole = _ep_specs(seq, d)
    tiles = seq // EP_TILE

    def epilogue(du, i, j, ins, outs):
        dh_ref, x_ref, y_ref, mod_ref, g_ref, b_ref = ins
        dy_ref, dx_ref, acc_ref = outs
        y = y_ref[...]
        gate = 1.0 + mod_ref[0, 2:3, :]
        z = DEEPNORM_ALPHA * x_ref[...] + gate * y
        xhat, rstd = _ln_stats(z)
        h1 = xhat * g_ref[...] + b_ref[...]
        dh1 = dh_ref[...] + du * (1.0 + mod_ref[0, 4:5, :])
        dz = _ln_bwd(dh1 * g_ref[...], xhat, rstd)
        dy_ref[...] = (gate * dz).astype(BF16)
        dx_ref[...] = DEEPNORM_ALPHA * dz
        _acc_rows(acc_ref, i % tiles == 0,
                  [_colsum(dh1 * xhat), _colsum(dh1), _colsum(dz * y), _colsum(du * h1), _colsum(du)])

    return _matmul(dh, wgut, mode="nn", tm=EP_TILE, tn=d, tk=D_FF, name="gate_up_dgrad_ln1_bwd",
                   ins=[(dh1a,) + row, (x,) + row, (y1,) + row, (mod,) + per_b, (g,) + whole, (b,) + whole],
                   outs=[(jax.ShapeDtypeStruct((ntok, d), BF16),) + row, (jax.ShapeDtypeStruct((ntok, d), F32),) + row,
                         (jax.ShapeDtypeStruct((ntok // seq, 8, d), F32),) + per_b], epilogue=epilogue)


def _wo_dgrad_gate_bwd(dy1, wo, gab, ya, yb):
    ntok, d = ya.shape
    tm, tn = 1024, 512
    tile = ((tm, tn), lambda i, j: (i, j))
    tile_b = ((tm, tn), lambda i, j: (i, j + d // tn))

    def epilogue(dm_, i, j, ins, outs):
        ga_ref, gb_ref, ya_ref, yb_ref = ins
        dya_ref, dyb_ref, dga_ref, dgb_ref = outs
        sa, sb = _sigmoid(ga_ref[...].astype(F32)), _sigmoid(gb_ref[...].astype(F32))
        dya_ref[...] = (dm_ * sa).astype(BF16)
        dyb_ref[...] = (dm_ * sb).astype(BF16)
        dga_ref[...] = (dm_ * ya_ref[...].astype(F32) * sa * (1.0 - sa)).astype(BF16)
        dgb_ref[...] = (dm_ * yb_ref[...].astype(F32) * sb * (1.0 - sb)).astype(BF16)

    shp = jax.ShapeDtypeStruct((ntok, d), BF16)
    return _matmul(dy1, wo, mode="nt", tm=tm, tn=tn, tk=d, name="w_o_dgrad_gate_bwd",
                   ins=[(gab,) + tile, (gab,) + tile_b, (ya,) + tile, (yb,) + tile],
                   outs=[(shp,) + tile] * 4, epilogue=epilogue)


def _w_in_dgrad_grad_x(dproj, wint, dxa, x, mod, seq, token):
    ntok, d = x.shape
    row, per_b, _ = _ep_specs(seq, d)
    tiles = seq // EP_TILE

    def epilogue(du, i, j, ins, outs):
        dxa_ref, x_ref, mod_ref = ins
        gx_ref, acc_ref = outs
        gx_ref[...] = dxa_ref[...] + du * (1.0 + mod_ref[0, 1:2, :])
        _acc_rows(acc_ref, i % tiles == 0, [_colsum(du * x_ref[...]), _colsum(du)])

    return _matmul(dproj, wint, mode="nn", tm=EP_TILE, tn=d, tk=wint.shape[0] // 2, name="w_in_dgrad_grad_x", token=token,
                   ins=[(dxa,) + row, (x,) + row, (mod,) + per_b],
                   outs=[(jax.ShapeDtypeStruct((ntok, d), F32),) + row, (jax.ShapeDtypeStruct((ntok // seq, 8, d), F32),) + per_b],
                   epilogue=epilogue)


def _branch_b_gate_merge(ob, wbbt, gab, ya):
    ntok, d = ya.shape
    tm, tn = 1024, 512
    tile = ((tm, tn), lambda i, j: (i, j))
    tile_b = ((tm, tn), lambda i, j: (i, j + d // tn))

    def epilogue(yb, i, j, ins, outs):
        ga_ref, gb_ref, ya_ref = ins
        yb_ref, merged_ref = outs
        yb_ref[...] = yb.astype(BF16)
        merged_ref[...] = (_sigmoid(ga_ref[...].astype(F32)) * ya_ref[...].astype(F32)
                           + _sigmoid(gb_ref[...].astype(F32)) * yb).astype(BF16)

    shp = jax.ShapeDtypeStruct((ntok, d), BF16)
    return _matmul(ob, wbbt, mode="nt", tm=tm, tn=tn, tk=ob.shape[1], name="branch_b_gate_merge",
                   ins=[(gab,) + tile, (gab,) + tile_b, (ya,) + tile], outs=[(shp,) + tile] * 2, epilogue=epilogue)


def _segsum64(v):
    rows, width = v.shape
    ri = lax.broadcasted_iota(jnp.int32, (LANES, LANES), 0) // HEAD_DIM
    ci = lax.broadcasted_iota(jnp.int32, (LANES, LANES), 1) // HEAD_DIM
    ones = jnp.where(ri == ci, 1.0, 0.0).astype(BF16)
    out = []
    for c in range(width // LANES):
        part = v[:, c * LANES:(c + 1) * LANES]
        hi = part.astype(BF16)
        lo = (part - hi.astype(F32)).astype(BF16)
        out.append(jnp.dot(hi, ones, preferred_element_type=F32) + jnp.dot(lo, ones, preferred_element_type=F32))
    return jnp.concatenate(out, axis=1) if len(out) > 1 else out[0]


def _merge_b(os_, ls_):
    bsz, seq, w = os_[0].shape

    def body(o0, o1, o2, l0, l1, l2, ob_ref):
        ls = [l0[0], l1[0], l2[0]]
        mx = jnp.maximum(jnp.maximum(ls[0], ls[1]), ls[2])
        es = [jnp.exp(l - mx) for l in ls]
        den = es[0] + es[1] + es[2]
        ob_ref[0] = ((es[0] / den) * o0[0] + (es[1] / den) * o1[0] + (es[2] / den) * o2[0]).astype(BF16)

    return _row_call(body, "merge_b", bsz, seq, [_rows(w)] * 6, _rows(w),
                     jax.ShapeDtypeStruct((bsz, seq, w), BF16))(*os_, *ls_)


def _branch_b_dgrad_merge_bwd(dyb, wbbt, os_, ls_):
    ntok, w = os_[0].shape
    row = ((EP_TILE, w), lambda i, j: (i, 0))

    def epilogue(dob_, i, j, ins, outs):
        os_r, ls_r = ins[:3], ins[3:]
        do_r, dd_r = outs[:3], outs[3:]
        ls = [l[...] for l in ls_r]
        mx = jnp.maximum(jnp.maximum(ls[0], ls[1]), ls[2])
        es = [jnp.exp(l - mx) for l in ls]
        den = es[0] + es[1] + es[2]
        ws = [e / den for e in es]
        dws = [_segsum64(dob_ * o[...]) for o in os_r]
        mean = ws[0] * dws[0] + ws[1] * dws[1] + ws[2] * dws[2]
        for wg, do_ref, dd_ref in zip(ws, do_r, dd_r):
            do_ref[...] = wg * dob_
            dd_ref[...] = -wg * mean

    shp = jax.ShapeDtypeStruct((ntok, w), F32)
    return _matmul(dyb, wbbt, mode="nn", tm=EP_TILE, tn=w, tk=dyb.shape[1], name="branch_b_dgrad_merge_bwd",
                   ins=[(v,) + row for v in list(os_) + list(ls_)], outs=[(shp,) + row] * 6, epilogue=epilogue)


def _branch_a_dgrad_delta(dya, wba, oa, lse_a, sinks_exp, seq):
    ntok, w = oa.shape
    row, per_b, whole = _ep_specs(seq, w)
    tiles = seq // EP_TILE

    def epilogue(do_, i, j, ins, outs):
        o_ref, l_ref, s_ref = ins
        do_ref, dd_ref, acc_ref = outs
        dd = -_segsum64(do_ * o_ref[...])
        do_ref[...] = do_
        dd_ref[...] = dd
        _acc_rows(acc_ref, i % tiles == 0, [_colsum(dd * jnp.exp(s_ref[...] - l_ref[...]))])

    shp = jax.ShapeDtypeStruct((ntok, w), F32)
    return _matmul(dya, wba, mode="nt", tm=EP_TILE, tn=w, tk=dya.shape[1], name="branch_a_dgrad_delta",
                   ins=[(oa,) + row, (lse_a,) + row, (sinks_exp,) + whole],
                   outs=[(shp,) + row, (shp,) + row, (jax.ShapeDtypeStruct((ntok // seq, 8, w), F32),) + per_b],
                   epilogue=epilogue)


def _swap_halves(v):
    src = lax.broadcasted_iota(jnp.int32, (LANES, LANES), 0)
    dst = lax.broadcasted_iota(jnp.int32, (LANES, LANES), 1)
    partner = jnp.where((dst % HEAD_DIM) < HEAD_DIM // 2, dst + HEAD_DIM // 2, dst - HEAD_DIM // 2)
    perm = jnp.where(src == partner, 1.0, 0.0).astype(BF16)
    hi = v.astype(BF16)
    lo = (v - hi.astype(F32)).astype(BF16)
    return jnp.dot(hi, perm, preferred_element_type=F32) + jnp.dot(lo, perm, preferred_element_type=F32)


def _swap_halves_roll(v):
    lane = lax.broadcasted_iota(jnp.int32, v.shape, 1)
    return jnp.where((lane % HEAD_DIM) < HEAD_DIM // 2, pltpu.roll(v, LANES - HEAD_DIM // 2, 1),
                     pltpu.roll(v, HEAD_DIM // 2, 1))


def _rope(v, cos, sin, sign=1.0, mxu=True):
    swap = _swap_halves if mxu else _swap_halves_roll
    out = []
    for c in range(v.shape[1] // LANES):
        part = v[:, c * LANES:(c + 1) * LANES]
        out.append(part * cos + sign * (swap(part) * sin))
    return jnp.concatenate(out, axis=1) if len(out) > 1 else out[0]


def _half_mask(shape, half):
    lane = lax.broadcasted_iota(jnp.int32, shape, len(shape) - 1) % LANES
    return (lane < HEAD_DIM) if half == 0 else (lane >= HEAD_DIM)


def _dup_half(v, half):
    return jnp.where(_half_mask(v.shape, half), v, pltpu.roll(v, HEAD_DIM, 1))


def _fold_halves(v):
    return v + pltpu.roll(v, HEAD_DIM, 1)


def _pick_halves(lo_rows, hi_rows):
    return jnp.where(_half_mask(lo_rows.shape, 0), lo_rows, hi_rows)


def _stack_masked(v, pairs):
    parts = []
    for c in pairs:
        pair = v[:, c * LANES:(c + 1) * LANES]
        parts += [jnp.where(_half_mask(pair.shape, half), pair, 0.0) for half in (0, 1)]
    return jnp.concatenate(parts, axis=0)


def _stack_pair_cols(v, pairs):
    return jnp.concatenate([v[:, c * LANES + half * HEAD_DIM:c * LANES + half * HEAD_DIM + 1] for c in pairs for half in (0, 1)],
                           axis=0)


ATTN_UNITS = 16


def _class_rows(r):
    return [pl.ds(0, QBLOCK)] if r == 1 else [pl.ds(rho, QBLOCK, stride=r) for rho in range(r)]


def _band_mask(nrows, nk, blk, n_back, has_prev):
    qi = lax.broadcasted_iota(jnp.int32, (nrows, nk), 0) % QBLOCK
    ki = lax.broadcasted_iota(jnp.int32, (nrows, nk), 1)
    if has_prev:
        dist = qi + QBLOCK - ki
        return (dist >= 0) & (dist <= n_back) & ((ki >= QBLOCK) | (blk > 0))
    dist = qi - ki
    return (dist >= 0) & (dist <= n_back)


def _attn_fwd(q_arr, k_arr, v_arr, *, name, npair, gqa, q_col, k_col, v_col, nchunk, r, n_back, sinks=None):
    bsz, seq, _ = q_arr.shape
    rr = QBLOCK * r
    nblk = seq // rr
    qw = npair * LANES
    kw = LANES if gqa else qw
    has_prev = nblk > 1
    has_sink = sinks is not None
    scale = HEAD_DIM ** -0.5

    def body(*refs):
        refs = list(refs)
        q_ref, kc_ref, vc_ref = refs[:3]
        pos = 3
        if has_prev:
            kp_ref, vp_ref = refs[pos:pos + 2]
            pos += 2
        if has_sink:
            sink_ref = refs[pos]
            pos += 1
        o_ref, lse_ref = refs[pos:pos + 2]
        blk = pl.program_id(2)
        nk = (2 if has_prev else 1) * QBLOCK
        valid = _band_mask(QBLOCK, nk, blk, n_back, has_prev)
        per = npair // 2
        classes = _class_rows(r)
        step = max(1, ATTN_UNITS // (2 * npair))
        for first in range(0, len(classes), step):
            batch = classes[first:first + step]
            units = []
            for ci, rows in enumerate(batch):
                q = q_ref[0, rows, :] * scale
                k, v = kc_ref[0, rows, :], vc_ref[0, rows, :]
                if has_prev:
                    k = jnp.concatenate([kp_ref[0, rows, :], k], axis=0)
                    v = jnp.concatenate([vp_ref[0, rows, :], v], axis=0)
                if gqa:
                    kdup = [_dup_half(k, hk).astype(BF16) for hk in range(2)]
                    vdup = [_dup_half(v, hk) for hk in range(2)]
                for c in range(npair):
                    sl = slice(c * LANES, (c + 1) * LANES)
                    qc = q[:, sl]
                    kc, vc = (kdup[c // per], vdup[c // per]) if gqa else (k[:, sl].astype(BF16), v[:, sl])
                    for half in (0, 1):
                        qm = jnp.where(_half_mask(qc.shape, half), qc, 0.0).astype(BF16)
                        vm = jnp.where(_half_mask(vc.shape, half), vc, 0.0).astype(BF16)
                        s = lax.dot_general(qm, kc, _DIMS["nt"], preferred_element_type=F32)
                        units.append(dict(ci=ci, c=c, half=half, s=s, vm=vm, sk=sink_ref[2 * c + half] if has_sink else None))
            for u in units:
                s = jnp.where(valid, u["s"], NEG_INF)
                m = jnp.max(s, axis=1, keepdims=True)
                if has_sink:
                    m = jnp.maximum(m, u["sk"])
                p = jnp.exp(s - m)
                den = jnp.sum(p, axis=1, keepdims=True)
                if has_sink:
                    den = den + jnp.exp(u["sk"] - m)
                u.update(p=p.astype(BF16), den=den, lse=m + jnp.log(den))
            for u in units:
                u["o"] = jnp.dot(u["p"], u["vm"], preferred_element_type=F32) / u["den"]
            for ci, rows in enumerate(batch):
                outs, lses = [None] * npair, [None] * npair
                for u in units:
                    if u["ci"] != ci:
                        continue
                    c, o = u["c"], u["o"]
                    lse = jnp.broadcast_to(u["lse"], o.shape)
                    outs[c] = o if u["half"] == 0 else outs[c] + o
                    lses[c] = lse if u["half"] == 0 else _pick_halves(lses[c], lse)
                o_ref[0, rows, :] = jnp.concatenate(outs, axis=1) if npair > 1 else outs[0]
                lse_ref[0, rows, :] = jnp.concatenate(lses, axis=1) if npair > 1 else lses[0]

    def cur(width, col0):
        return pl.BlockSpec((1, rr, width), lambda b, c, i: (b, i, col0 + c))

    def prev(width, col0):
        return pl.BlockSpec((1, rr, width), lambda b, c, i: (b, jnp.maximum(i - 1, 0), col0 + c))

    in_specs = [cur(qw, q_col), cur(kw, k_col), cur(kw, v_col)]
    args = [q_arr, k_arr, v_arr]
    if has_prev:
        in_specs += [prev(kw, k_col), prev(kw, v_col)]
        args += [k_arr, v_arr]
    if has_sink:
        in_specs.append(pl.BlockSpec(memory_space=pltpu.SMEM))
        args.append(sinks)
    return pl.pallas_call(
        body,
        name=name,
        grid=(bsz, nchunk, nblk),
        in_specs=in_specs,
        out_specs=[pl.BlockSpec((1, rr, qw), lambda b, c, i: (b, i, c))] * 2,
        out_shape=[jax.ShapeDtypeStruct((bsz, seq, nchunk * qw), F32)] * 2,
        compiler_params=_params("parallel", "parallel", "parallel"),
    )(*args)


def _attn_bwd(q_arr, k_arr, v_arr, cos, sin, do, lse, dd, *, name, npair, gqa, q_col, k_col, v_col, nchunk, r, n_back,
              token=None):
    bsz, seq, _ = q_arr.shape
    rr = QBLOCK * r
    nblk = seq // rr
    qw = npair * LANES
    kw = LANES if gqa else qw
    has_next = nblk > 1
    has_token = token is not None
    scale = HEAD_DIM ** -0.5

    def body(*refs):
        refs = list(refs)
        k_ref, v_ref, c_ref, s_ref = refs[:4]
        tile_refs = [refs[4:8]]
        pos = 8
        if has_next:
            tile_refs.append(refs[pos:pos + 4])
            pos += 4
        if has_token:
            pos += 1
        dq_ref, dk_ref, dv_ref = refs[pos:pos + 3]
        carry_ref = refs[pos + 3]
        blk = pl.program_id(2)
        if has_next:
            @pl.when(blk == 0)
            def _():
                carry_ref[...] = jnp.zeros_like(carry_ref)

        nrows = (npair if gqa else 1) * QBLOCK
        qi = lax.broadcasted_iota(jnp.int32, (nrows, QBLOCK), 0) % QBLOCK
        ki = lax.broadcasted_iota(jnp.int32, (nrows, QBLOCK), 1)
        valids = [qi >= ki, (qi + QBLOCK - ki <= n_back) & (blk + 1 < nblk)]
        per = npair // 2
        ntile = len(tile_refs)
        cat = lambda parts: jnp.concatenate(parts, axis=1) if len(parts) > 1 else parts[0]
        classes = _class_rows(r)
        step = max(1, ATTN_UNITS // (ntile * (2 if gqa else 2 * npair)))
        for first in range(0, len(classes), step):
            batch = classes[first:first + step]
            units = []
            for ci, rows in enumerate(batch):
                tiles = [(q_ref[0, rows, :] * scale, do_ref[0, rows, :], l_ref[0, rows, :], d_ref[0, rows, :])
                         for q_ref, do_ref, l_ref, d_ref in tile_refs]
                k, v = k_ref[0, rows, :], v_ref[0, rows, :]
                if gqa:
                    for hk in range(2):
                        pairs = list(range(hk * per, (hk + 1) * per))
                        kd, vd = _dup_half(k, hk).astype(BF16), _dup_half(v, hk).astype(BF16)
                        for t, (q, do_, l_, d_) in enumerate(tiles):
                            units.append(dict(ci=ci, t=t, hk=hk, pairs=pairs, qs=_stack_masked(q, pairs).astype(BF16),
                                              dos=_stack_masked(do_, pairs).astype(BF16), lcol=_stack_pair_cols(l_, pairs),
                                              dcol=_stack_pair_cols(d_, pairs), kmat=kd, vmat=vd, kdq=kd))
                else:
                    for c in range(npair):
                        sl = slice(c * LANES, (c + 1) * LANES)
                        kc, vcb = k[:, sl], v[:, sl].astype(BF16)
                        kcb = kc.astype(BF16)
                        for t, (q, do_, l_, d_) in enumerate(tiles):
                            for half in (0, 1):
                                hm = _half_mask(kc.shape, half)
                                col = c * LANES + half * HEAD_DIM
                                units.append(dict(ci=ci, t=t, c=c, half=half, qs=jnp.where(hm, q[:, sl], 0.0).astype(BF16),
                                                  dos=jnp.where(hm, do_[:, sl], 0.0).astype(BF16), lcol=l_[:, col:col + 1],
                                                  dcol=d_[:, col:col + 1], kmat=kcb, vmat=vcb,
                                                  kdq=jnp.where(hm, kc, 0.0).astype(BF16)))
            for u in units:
                u["s"] = lax.dot_general(u["qs"], u["kmat"], _DIMS["nt"], preferred_element_type=F32)
                u["dp"] = lax.dot_general(u["dos"], u["vmat"], _DIMS["nt"], preferred_element_type=F32)
            for u in units:
                p = jnp.exp(jnp.where(valids[u["t"]], u["s"], NEG_INF) - u["lcol"])
                u["ds"] = (p * (u["dp"] + u["dcol"])).astype(BF16)
                u["p"] = p.astype(BF16)
            for u in units:
                u["dv"] = lax.dot_general(u["p"], u["dos"], _DIMS["tn"], preferred_element_type=F32)
                u["dk"] = lax.dot_general(u["ds"], u["qs"], _DIMS["tn"], preferred_element_type=F32)
                u["dq"] = jnp.dot(u["ds"], u["kdq"], preferred_element_type=F32) * scale
            for ci, rows in enumerate(batch):
                mine = [u for u in units if u["ci"] == ci]
                dq = [[None] * npair for _ in range(ntile)]
                if gqa:
                    dk_out = dv_out = None
                    for hk in range(2):
                        us = [u for u in mine if u["hk"] == hk]
                        for u in us:
                            for i, c in enumerate(u["pairs"]):
                                dq[u["t"]][c] = _pick_halves(u["dq"][2 * i * QBLOCK:(2 * i + 1) * QBLOCK],
                                                             u["dq"][(2 * i + 1) * QBLOCK:(2 * i + 2) * QBLOCK])
                        dk_h = _fold_halves(functools.reduce(jnp.add, [u["dk"] for u in us]))
                        dv_h = _fold_halves(functools.reduce(jnp.add, [u["dv"] for u in us]))
                        dk_out = dk_h if hk == 0 else _pick_halves(dk_out, dk_h)
                        dv_out = dv_h if hk == 0 else _pick_halves(dv_out, dv_h)
                else:
                    dks, dvs = [], []
                    for c in range(npair):
                        us = [u for u in mine if u["c"] == c]
                        dks.append(functools.reduce(jnp.add, [u["dk"] for u in us]))
                        dvs.append(functools.reduce(jnp.add, [u["dv"] for u in us]))
                        for t in range(ntile):
                            dq[t][c] = functools.reduce(jnp.add, [u["dq"] for u in us if u["t"] == t])
                    dk_out, dv_out = cat(dks), cat(dvs)
                ck, sk_ = c_ref[0, rows, :], s_ref[0, rows, :]
                dk_ref[0, rows, :] = _rope(dk_out, ck, sk_, sign=-1.0, mxu=gqa)
                dv_ref[0, rows, :] = dv_out
                dq_cur = cat(dq[0])
                if has_next:
                    dq_cur = dq_cur + carry_ref[rows, :]
                    carry_ref[rows, :] = cat(dq[1])
                dq_ref[0, rows, :] = _rope(dq_cur, ck, sk_, sign=-1.0, mxu=gqa)

    def at(width, col0, shift):
        return pl.BlockSpec((1, rr, width), lambda b, c, i: (b, jnp.minimum(i + shift, nblk - 1), col0 + c))

    in_specs = [at(kw, k_col, 0), at(kw, v_col, 0), pl.BlockSpec((1, rr, LANES), lambda b, c, i: (b, i, 0)),
                pl.BlockSpec((1, rr, LANES), lambda b, c, i: (b, i, 0))]
    args = [k_arr, v_arr, cos, sin]
    for shift in (0, 1) if has_next else (0,):
        in_specs += [at(qw, q_col, shift), at(qw, 0, shift), at(qw, 0, shift), at(qw, 0, shift)]
        args += [q_arr, do, lse, dd]
    if has_token:
        in_specs.append(pl.BlockSpec(token.shape, lambda b, c, i: (0, 0)))
        args.append(token)
    return pl.pallas_call(
        body,
        name=name,
        grid=(bsz, nchunk, nblk),
        in_specs=in_specs,
        out_specs=[pl.BlockSpec((1, rr, qw), lambda b, c, i: (b, i, c)),
                   pl.BlockSpec((1, rr, kw), lambda b, c, i: (b, i, c)),
                   pl.BlockSpec((1, rr, kw), lambda b, c, i: (b, i, c))],
        out_shape=[jax.ShapeDtypeStruct((bsz, seq, nchunk * qw), F32),
                   jax.ShapeDtypeStruct((bsz, seq, nchunk * kw), F32),
                   jax.ShapeDtypeStruct((bsz, seq, nchunk * kw), F32)],
        scratch_shapes=[pltpu.VMEM((rr, qw) if has_next else (8, LANES), F32)],
        compiler_params=_params("parallel", "parallel", "arbitrary"),
    )(*args)


B_CHUNKS = {1: (4, 1), 4: (1, 4), 16: (1, 4)}


def _rope_tables(positions):
    half = HEAD_DIM // 2
    inv = ROPE_THETA ** (-jnp.arange(half, dtype=F32) / half)
    ang = positions.astype(F32)[..., None] * inv
    cos, sin = jnp.cos(ang), jnp.sin(ang)
    return jnp.concatenate([cos] * 4, axis=-1), jnp.concatenate([-sin, sin, -sin, sin], axis=-1)


def _layer_step(x, mod, positions, sinks, ln1_g, ln1_b, ln2_g, ln2_b, target, get_w_in, get_rest, hook):
    bsz, seq, d = x.shape
    ntok = bsz * seq
    flat = lambda v: v.reshape(ntok, v.shape[-1])
    unflat = lambda v: v.reshape(bsz, seq, v.shape[-1])
    cos, sin = _rope_tables(positions)
    mm = functools.partial(_matmul, tm=1024, tk=1024)
    scalar = lambda tok: 0.0 if tok is None else tok[0, 0]

    u1 = _modulate_in(x, mod)
    u1f = flat(u1)
    wint = get_w_in(u1)
    cosf, sinf = flat(cos), flat(sin)
    proj = functools.partial(_proj_rope, u1f, wint, cosf, sinf, tm=2048)
    qkvb = unflat(proj(n=4608, b_off=OFF_QKVB, rope_cols=3072, tn=256, name="proj_qkvb"))
    b_kws, os_, ls_ = [], [], []
    for g, (window, r) in enumerate(B_PATTERNS):
        npair, nch = B_CHUNKS[r]
        per = B_HEADS_PER_GROUP // (2 * npair)
        nsec = len(B_PATTERNS) * per
        kw_ = dict(npair=npair, gqa=False, q_col=g * per, k_col=nsec + g * per, v_col=2 * nsec + g * per, nchunk=nch, r=r,
                   n_back=window // r)
        b_kws.append(kw_)
        o_g, l_g = _attn_fwd(qkvb, qkvb, qkvb, name=f"attn_b{g}_fwd", **kw_)
        os_.append(o_g)
        ls_.append(l_g)
    ob = _merge_b(os_, ls_)
    tok = hook("projected", ob)
    u1t = u1f if tok is None else lax.optimization_barrier((u1f, tok))[0]
    proj = functools.partial(_proj_rope, u1t, wint, cosf, sinf, tm=2048)
    gab = unflat(proj(n=2048, b_off=OFF_GAB, rope_cols=0, tn=256, name="proj_gab", out_dtype=BF16))
    qa = unflat(proj(n=1024, b_off=OFF_QA, rope_cols=1024, tn=512, name="proj_qa"))
    kva = unflat(proj(n=256, b_off=OFF_KVA, rope_cols=128, tn=128, name="proj_kva"))
    a_kw = dict(npair=A_Q_HEADS // 2, gqa=True, q_col=0, k_col=0, v_col=1, nchunk=1, r=1, n_back=A_WINDOW - 1)
    oa, lse_a = _attn_fwd(qa, kva, kva, name="attn_a_fwd", sinks=sinks.reshape(A_Q_HEADS), **a_kw)
    rest = get_rest(oa)
    wba, wbbt, wo, wgut, wd = (rest[n] for n in ("w_branch_a", "w_branch_b", "w_o", "w_gate_up", "w_down"))
    ya = unflat(mm(flat(oa), wba, mode="nn", out_dtype=BF16, tn=512, name="branch_a"))
    ybf, mergedf = _branch_b_gate_merge(flat(ob), wbbt, flat(gab), flat(ya))
    xf = flat(x)
    y1f, h1f, u2f = _wo_ln1(mergedf, wo, xf, mod, ln1_g, ln1_b, seq)
    wgut_i = _interleave_gate_up(wgut)
    hf, af = _gate_up_silu(u2f, wgut_i)

    dy2f, dh1af, acc2 = _down_ln2_loss_bwd(af, wd, h1f, mod, ln2_g, ln2_b, flat(target), seq)
    g_wd = _matmul(af, dy2f, mode="tn", out_dtype=BF16, tm=256, tn=1024, tk=ntok, name="down_wgrad")
    dhf = _down_dgrad_silu_bwd(dy2f, wd, hf)
    g_wgut = _interleave_gate_up(_matmul(dhf, u2f, mode="tn", out_dtype=BF16, tm=256, tn=1024, tk=ntok, name="gate_up_wgrad"))
    dy1f, dxaf, acc1 = _gate_up_dgrad_ln1_bwd(dhf, wgut_i, dh1af, xf, y1f, mod, ln1_g, ln1_b, seq)
    g_wo = _matmul(mergedf, dy1f, mode="tn", out_dtype=BF16, tm=256, tn=1024, tk=ntok, name="w_o_wgrad")
    dyaf, dybf, dgaf, dgbf = _wo_dgrad_gate_bwd(dy1f, wo, flat(gab), flat(ya), ybf)
    g_wba = _matmul(flat(oa), dyaf, mode="tn", out_dtype=BF16, tm=256, tn=1024, tk=ntok, name="branch_a_wgrad")
    g_wbbt = _matmul(dybf, flat(ob), mode="tn", out_dtype=BF16, tm=256, tn=512, tk=ntok, name="branch_b_wgrad")
    tok = hook("grads_rest", dict(w_branch_a=g_wba, w_branch_b=g_wbbt, w_o=g_wo, w_gate_up=g_wgut, w_down=g_wd))

    sinks_exp = jnp.repeat(sinks.reshape(1, A_Q_HEADS), HEAD_DIM, axis=1) + scalar(tok)
    doa, dd_a, acc_s = _branch_a_dgrad_delta(dyaf, wba, flat(oa), flat(lse_a), sinks_exp, seq)
    doa, dd_a = unflat(doa), unflat(dd_a)
    tok = hook("delta_done", dd_a)
    dqa, dka, dva = _attn_bwd(qa, kva, kva, cos, sin, doa, lse_a, dd_a, name="attn_a_bwd", token=tok, **a_kw)
    merged_bwd = [unflat(t) for t in _branch_b_dgrad_merge_bwd(dybf, wbbt, [flat(t) for t in os_], [flat(t) for t in ls_])]
    dqs, dks, dvs = [], [], []
    for g in range(len(B_PATTERNS)):
        dq_g, dk_g, dv_g = _attn_bwd(qkvb, qkvb, qkvb, cos, sin, merged_bwd[g], ls_[g], merged_bwd[3 + g],
                                     name=f"attn_b{g}_bwd", **b_kws[g])
        dqs.append(dq_g)
        dks.append(dk_g)
        dvs.append(dv_g)
    dproj = jnp.concatenate([t.astype(BF16) for t in [dqa, dka, dva] + dqs + dks + dvs] + [unflat(dgaf), unflat(dgbf)], axis=-1)
    dprojf = flat(dproj)
    g_wint = _matmul(dprojf, u1f, mode="tn", out_dtype=BF16, tm=256, tn=1024, tk=ntok, name="w_in_wgrad")
    tok = hook("grads_w_in", dict(w_in=g_wint))
    grad_x, acc0 = _w_in_dgrad_grad_x(dprojf, wint, dxaf, xf, mod, seq, tok)
    grad_x = unflat(grad_x)
    tok = hook("dgrad_done", grad_x)

    loss_part = jnp.sum(acc2[:, 3, 0])
    dmod = jnp.stack([acc0[:, 1], acc0[:, 0], acc1[:, 2], acc1[:, 4], acc1[:, 3], acc2[:, 2]], axis=1)
    small = jnp.stack([acc1[:, 0].sum(0), acc1[:, 1].sum(0), acc2[:, 0].sum(0), acc2[:, 1].sum(0), acc_s[:, 0].sum(0)])
    small = small + scalar(tok)
    return loss_part, grad_x, dmod, small


CHIP_FLIPS = (2, 4, 6)


def _my_place():
    return lax.axis_index("x"), lax.axis_index("y"), lax.axis_index("c")


def _flip(place, k):
    px, py, pc = place
    return (1 - px if k & 4 else px, 1 - py if k & 2 else py, 1 - pc if k & 1 else pc)


def _index(place):
    return 4 * place[0] + 2 * place[1] + place[2]


def _gather_small(v, name):
    rows, cols = v.shape

    def body(v_ref, out_ref, send_sems, recv_sems):
        me = _my_place()
        out_ref[_index(me)] = v_ref[...]
        copies = []
        for k in range(1, N_DEV):
            copies.append(pltpu.make_async_remote_copy(
                src_ref=v_ref, dst_ref=out_ref.at[_index(me)], send_sem=send_sems.at[k - 1], recv_sem=recv_sems.at[k - 1],
                device_id=_flip(me, k), device_id_type=MESH))
        for cp in copies:
            cp.start()
        for k in range(1, N_DEV):
            pltpu.make_async_remote_copy(
                src_ref=v_ref, dst_ref=out_ref.at[_index(_flip(me, k))], send_sem=send_sems.at[k - 1],
                recv_sem=recv_sems.at[k - 1], device_id=_flip(me, k), device_id_type=MESH).wait_recv()
        for cp in copies:
            cp.wait_send()

    return pl.pallas_call(
        body,
        name=name,
        out_shape=jax.ShapeDtypeStruct((N_DEV, rows, cols), v.dtype),
        in_specs=[pl.BlockSpec(memory_space=pltpu.VMEM)],
        out_specs=pl.BlockSpec(memory_space=pltpu.VMEM),
        scratch_shapes=[pltpu.SemaphoreType.DMA((N_DEV - 1,)), pltpu.SemaphoreType.DMA((N_DEV - 1,))],
        compiler_params=pltpu.CompilerParams(vmem_limit_bytes=VMEM_LIMIT_BYTES),
    )(v)


_HBM = pl.BlockSpec(memory_space=pltpu.HBM)
_SEM = pl.BlockSpec(memory_space=pltpu.SEMAPHORE)
_EFFECT = pltpu.SideEffectType.DATAFLOW_SIDE_EFFECTING


def _remote(src, dst, send_sems, recv_sems, j, to):
    return pltpu.make_async_remote_copy(src_ref=src, dst_ref=dst, send_sem=send_sems.at[j], recv_sem=recv_sems.at[j],
                                        device_id=to, device_id_type=MESH)


def _copies_start(name, bufs, make_copies, nsem):
    nbuf = len(bufs)

    def body(*refs):
        for cp in make_copies(refs[:nbuf], refs[nbuf], refs[nbuf + 1]):
            cp.start()
        refs[-1][...] = jnp.zeros_like(refs[-1])

    sems = pltpu.SemaphoreType.DMA((nsem,))
    res = pl.pallas_call(
        body, name=name,
        out_shape=(sems, sems, *[pltpu.HBM(v.shape, v.dtype) for v in bufs], jax.ShapeDtypeStruct((8, LANES), F32)),
        in_specs=(_HBM,) * nbuf, out_specs=(_SEM, _SEM) + (_HBM,) * nbuf + (pl.BlockSpec(memory_space=pltpu.VMEM),),
        input_output_aliases={i: 2 + i for i in range(nbuf)},
        compiler_params=pltpu.CompilerParams(has_side_effects=_EFFECT),
    )(*[pltpu.with_memory_space_constraint(v, pltpu.HBM) for v in bufs])
    return res[0], res[1], list(res[2:2 + nbuf]), res[-1]


def _copies_wait(name, started, make_copies, after):
    send_sems, recv_sems, bufs, _ = started
    nbuf = len(bufs)

    def body(*refs):
        for cp in make_copies(refs[:nbuf], refs[nbuf], refs[nbuf + 1]):
            cp.wait_send()
            cp.wait_recv()

    return list(pl.pallas_call(
        body, name=name,
        out_shape=tuple(pltpu.HBM(v.shape, v.dtype) for v in bufs),
        in_specs=(_HBM,) * nbuf + (_SEM, _SEM, pl.BlockSpec(memory_space=pl.ANY)), out_specs=(_HBM,) * nbuf,
        input_output_aliases={i: i for i in range(nbuf)},
        compiler_params=pltpu.CompilerParams(has_side_effects=_EFFECT),
    )(*bufs, send_sems, recv_sems, after))


def _to_sibling_copies(refs, send_sems, recv_sems):
    src_ref, land_ref = refs
    me = _my_place()
    return [_remote(src_ref.at[q, 1 - me[2]], land_ref.at[q], send_sems, recv_sems, q, _flip(me, 1)) for q in range(4)]


def _to_chips_copies(refs, send_sems, recv_sems):
    src_ref, land_ref = refs
    me = _my_place()
    copies = []
    for j, k in enumerate(CHIP_FLIPS):
        to = _flip(me, k)
        copies.append(_remote(src_ref.at[2 * to[0] + to[1]], land_ref.at[j], send_sems, recv_sems, j, to))
    return copies


class _Gather:
    def __init__(self, name, blocks):
        self.name, self.n = name, len(blocks)
        at_me = (_index(_my_place()), 0, 0)
        lands = [lax.dynamic_update_slice(lax.empty((N_DEV,) + v.shape, v.dtype), v[None], at_me) for v in blocks]
        self.first = _copies_start(name + "_start", list(blocks) + lands, self._first_copies, 4 * self.n)
        self.token = self.first[3]

    def _first_copies(self, refs, send_sems, recv_sems):
        me = _my_place()
        return [_remote(refs[w], refs[self.n + w].at[_index(me)], send_sems, recv_sems, 4 * w + j, _flip(me, k))
                for w in range(self.n) for j, k in enumerate((1,) + CHIP_FLIPS)]

    def _pass_copies(self, refs, send_sems, recv_sems):
        me = _my_place()
        copies = []
        for w, land in enumerate(refs):
            for j, k in enumerate(CHIP_FLIPS):
                slot = land.at[_index(_flip(me, k))]
                copies.append(_remote(slot, slot, send_sems, recv_sems, 3 * w + j, _flip(me, 1)))
        return copies

    def pass_on(self, after):
        lands = _copies_wait(self.name + "_wait", self.first, self._first_copies, after)[self.n:]
        self.second = _copies_start(self.name + "_pass_start", lands, self._pass_copies, 3 * self.n)
        return self.second[3]

    def finish(self, after):
        return _copies_wait(self.name + "_pass_wait", self.second, self._pass_copies, after)


SUM_SPLIT = 2


def _sum_pairs(parts, theirs):
    nchip, _, rows, cols = parts.shape
    tile = rows // SUM_SPLIT

    def body(c_ref, a_ref, b_ref, o_ref):
        o_ref[...] = (a_ref[0].astype(F32) + b_ref[...].astype(F32)).astype(BF16)

    spec = pl.BlockSpec((1, tile, cols), lambda q, t, c_ref: (q, t, 0))
    grid_spec = pltpu.PrefetchScalarGridSpec(
        num_scalar_prefetch=1, grid=(nchip, SUM_SPLIT),
        in_specs=[pl.BlockSpec((1, 1, tile, cols), lambda q, t, c_ref: (q, c_ref[0], t, 0)), spec], out_specs=spec)
    return pl.pallas_call(body, name="grad_sum_sibling", grid_spec=grid_spec,
                          out_shape=jax.ShapeDtypeStruct((nchip, rows, cols), BF16),
                          compiler_params=_params("parallel", "parallel"))(lax.axis_index("c").reshape(1), parts, theirs)


def _sum_final(chip_sum, got):
    _, rows, cols = chip_sum.shape
    tile = rows // SUM_SPLIT

    def body(q_ref, a_ref, g_ref, o_ref):
        o_ref[...] = ((a_ref[0].astype(F32) + g_ref[0].astype(F32)) + g_ref[1].astype(F32)) + g_ref[2].astype(F32)

    grid_spec = pltpu.PrefetchScalarGridSpec(
        num_scalar_prefetch=1, grid=(SUM_SPLIT,),
        in_specs=[pl.BlockSpec((1, tile, cols), lambda t, q_ref: (q_ref[0], t, 0)),
                  pl.BlockSpec((3, tile, cols), lambda t, q_ref: (0, t, 0))],
        out_specs=pl.BlockSpec((tile, cols), lambda t, q_ref: (t, 0)))
    my_chip = (2 * lax.axis_index("x") + lax.axis_index("y")).reshape(1)
    return pl.pallas_call(body, name="grad_sum_chips", grid_spec=grid_spec, out_shape=jax.ShapeDtypeStruct((rows, cols), F32),
                          compiler_params=_params("parallel"))(my_chip, chip_sum, got)


class _ReduceScatter:
    def __init__(self, name, slabs):
        self.name, self.rows = name, slabs.shape[1]
        parts = slabs.reshape(4, 2, self.rows, D_MODEL)
        self.first = _copies_start(name + "_sibling_start", [parts, lax.empty((4, self.rows, D_MODEL), slabs.dtype)],
                                   _to_sibling_copies, 4)
        self.token = self.first[3]

    def between_chips(self, after):
        parts, theirs = _copies_wait(self.name + "_sibling_wait", self.first, _to_sibling_copies, after)
        chip_sum = _sum_pairs(parts, theirs)
        self.second = _copies_start(self.name + "_chips_start", [chip_sum, lax.empty((3, self.rows, D_MODEL), chip_sum.dtype)],
                                    _to_chips_copies, 3)
        return self.second[3]

    def finish(self, after):
        chip_sum, got = _copies_wait(self.name + "_chips_wait", self.second, _to_chips_copies, after)
        return _sum_final(chip_sum, got)


def _ada_fwd(c_all, w, b):
    nb, _ = c_all.shape
    ncol = w.shape[1]

    def body(c_ref, w_ref, b_ref, o_ref):
        c = c_ref[...]
        act = (c * _sigmoid(c)).astype(BF16)
        o_ref[...] = jnp.dot(act, w_ref[...].astype(BF16), preferred_element_type=F32) + b_ref[...]

    return pl.pallas_call(body, name="ada_fwd", out_shape=jax.ShapeDtypeStruct((nb, ncol), F32),
                          compiler_params=pltpu.CompilerParams(vmem_limit_bytes=VMEM_LIMIT_BYTES))(c_all, w, b)


def _ada_wgrad(c_all_t, dmod_cols):
    d, nb = c_all_t.shape
    ncol = dmod_cols.shape[1]

    def body(ct_ref, dm_ref, o_ref):
        ct = ct_ref[...]
        act = (ct * _sigmoid(ct)).astype(BF16).astype(F32)
        dm = dm_ref[...].astype(BF16).astype(F32)
        acc = act[:, 0:1] * dm[0:1, :]
        for i in range(1, nb):
            acc = acc + act[:, i:i + 1] * dm[i:i + 1, :]
        o_ref[...] = acc

    return pl.pallas_call(body, name="ada_wgrad", out_shape=jax.ShapeDtypeStruct((d, ncol), F32),
                          compiler_params=pltpu.CompilerParams(vmem_limit_bytes=VMEM_LIMIT_BYTES))(c_all_t, dmod_cols)


SMALL_ROWS = 24


def _reduce_small(gathered):
    def body(g_ref, o_ref):
        acc = g_ref[0]
        for dev in range(1, N_DEV):
            acc = acc + g_ref[dev]
        o_ref[...] = acc

    return pl.pallas_call(body, name="reduce_small", out_shape=jax.ShapeDtypeStruct(gathered.shape[1:], F32))(gathered)


def _adamw(w, g, m, v, name):
    rows, cols = w.shape
    tile = rows
    for cand in (256, 128, 64, 32, 16, 8):
        if rows % cand == 0 and rows > cand:
            tile = cand
            break
    spec = pl.BlockSpec((tile, cols), lambda t: (t, 0))
    bc1 = 1.0 - ADAM_B1 ** ADAM_STEP
    bc2 = 1.0 - ADAM_B2 ** ADAM_STEP

    def body(w_ref, g_ref, m_ref, v_ref, d_ref, nm_ref, nv_ref):
        g_ = g_ref[...]
        nm = ADAM_B1 * m_ref[...] + (1.0 - ADAM_B1) * g_
        nv = ADAM_B2 * v_ref[...] + (1.0 - ADAM_B2) * (g_ * g_)
        d_ref[...] = -ADAM_LR * ((nm / bc1) / (jnp.sqrt(nv / bc2) + ADAM_EPS) + ADAM_WD * w_ref[...])
        nm_ref[...] = nm
        nv_ref[...] = nv

    shp = jax.ShapeDtypeStruct((rows, cols), F32)
    return pl.pallas_call(body, name=name, grid=(rows // tile,), in_specs=[spec] * 4, out_specs=[spec] * 3, out_shape=[shp] * 3,
                          compiler_params=_params("parallel"))(w, g, m, v)


_WEIGHTS = ("w_ada", "b_ada", "w_in", "sinks", "w_branch_a", "w_branch_b", "w_o", "ln1_g", "ln1_b", "w_gate_up", "w_down",
            "ln2_g", "ln2_b")
_TRANSPOSED = ("w_in", "w_branch_b", "w_gate_up")


def _pack_shard(name, w):
    w = w.astype(BF16)
    if name in _TRANSPOSED:
        w = w.T
    return w.reshape(-1, D_MODEL)


def _unpack_full(name, slab):
    if name == "w_branch_b":
        return slab.reshape(N_DEV * 128, 512)
    return slab.reshape(-1, D_MODEL)


def _unpack_group(group, gathered):
    return {n: _unpack_full(n, slab) for (n, _), slab in zip(group, gathered)}


def _unpack_grads(group, g_packed):
    g_w, off = {}, 0
    for n, r in group:
        part = g_packed[off:off + r]
        off += r
        g_w[n] = part.reshape(128, 512) if n == "w_branch_b" else part
    return g_w


def kernel(x, c, positions, w_ada, b_ada, w_in, sinks, w_branch_a, w_branch_b, w_o, ln1_g, ln1_b, w_gate_up, w_down, ln2_g, ln2_b, loss_target, m_w_ada, m_b_ada, m_w_in, m_sinks, m_w_branch_a, m_w_branch_b, m_w_o, m_ln1_g, m_ln1_b, m_w_gate_up, m_w_down, m_ln2_g, m_ln2_b, v_w_ada, v_b_ada, v_w_in, v_sinks, v_w_branch_a, v_w_branch_b, v_w_o, v_ln1_g, v_ln1_b, v_w_gate_up, v_w_down, v_ln2_g, v_ln2_b):
    weights = dict(w_ada=w_ada, b_ada=b_ada, w_in=w_in, sinks=sinks, w_branch_a=w_branch_a, w_branch_b=w_branch_b, w_o=w_o,
                   ln1_g=ln1_g, ln1_b=ln1_b, w_gate_up=w_gate_up, w_down=w_down, ln2_g=ln2_g, ln2_b=ln2_b)
    m_in = dict(w_ada=m_w_ada, b_ada=m_b_ada, w_in=m_w_in, sinks=m_sinks, w_branch_a=m_w_branch_a, w_branch_b=m_w_branch_b,
                w_o=m_w_o, ln1_g=m_ln1_g, ln1_b=m_ln1_b, w_gate_up=m_w_gate_up, w_down=m_w_down, ln2_g=m_ln2_g, ln2_b=m_ln2_b)
    v_in = dict(w_ada=v_w_ada, b_ada=v_b_ada, w_in=v_w_in, sinks=v_sinks, w_branch_a=v_w_branch_a, w_branch_b=v_w_branch_b,
                w_o=v_w_o, ln1_g=v_ln1_g, ln1_b=v_ln1_b, w_gate_up=v_w_gate_up, w_down=v_w_down, ln2_g=v_ln2_g, ln2_b=v_ln2_b)
    bsz = x.shape[0]
    me = _index(_my_place())
    ada_cols = w_ada.shape[2]
    outs = {}

    def adamw(n, g):
        w2, m2, v2 = (t[n][0] if t[n].ndim == 3 else t[n] for t in (weights, m_in, v_in))
        shape = weights[n].shape
        if n in _TRANSPOSED:
            dlt, nm, nv = _adamw(w2.T, g, m2.T, v2.T, "adamw_" + n)
            outs[n] = tuple(t.T.reshape(shape) for t in (g, dlt, nm, nv))
        else:
            dlt, nm, nv = _adamw(w2, g, m2, v2, "adamw_" + n)
            outs[n] = tuple(t.reshape(shape) for t in (g, dlt, nm, nv))
        return nv

    packed_in = [_pack_shard(n, weights[n][0]) for n, _ in GROUP_IN]
    packed_rest = [_pack_shard(n, weights[n][0]) for n, _ in GROUP_REST]
    c_all = _gather_small(jnp.pad(c, ((0, 8 - bsz), (0, 0))), "gather_c")[:, :bsz].reshape(N_DEV * bsz, D_MODEL)
    gather_in = _Gather("gather_w_in", lax.optimization_barrier((packed_in, c_all))[0])
    b_cols = lax.dynamic_slice_in_dim(b_ada, me * ada_cols, ada_cols, axis=1)
    mod_cols = _ada_fwd(c_all, w_ada[0], b_cols + gather_in.token[0, 0])
    mod_all = _gather_small(mod_cols, "gather_mod").transpose(1, 0, 2).reshape(N_DEV * bsz, 6, D_MODEL)
    gather_rest = _Gather("gather_rest", lax.optimization_barrier((packed_rest, mod_all))[0])
    mod = jnp.pad(lax.dynamic_slice_in_dim(mod_all, me * bsz, bsz, axis=0), ((0, 0), (0, 2), (0, 0)))
    mod = mod + gather_rest.token[0, 0]
    mod = mod + gather_in.pass_on(mod)[0, 0]

    scatters = {}

    def get_w_in(after):
        return _unpack_group(GROUP_IN, gather_in.finish(after))["w_in"]

    def get_rest(after):
        return _unpack_group(GROUP_REST, gather_rest.finish(after))

    def pack_grads(group, grads):
        return jnp.concatenate([grads[n].reshape(N_DEV, r, D_MODEL) for n, r in group], axis=1)

    def hook(point, value):
        if point == "projected":
            return gather_rest.pass_on(value)
        if point == "grads_rest":
            scatters["rest"] = _ReduceScatter("scatter_rest", pack_grads(GROUP_REST, value))
            return scatters["rest"].token
        if point == "delta_done":
            return scatters["rest"].between_chips(value)
        if point == "grads_w_in":
            scatters["in"] = _ReduceScatter("scatter_w_in", pack_grads(GROUP_IN, value))
            return scatters["in"].token
        if point == "dgrad_done":
            tok = scatters["in"].between_chips(value)
            for n, g in _unpack_grads(GROUP_REST, scatters["rest"].finish(tok)).items():
                adamw(n, g)
            return tok
        raise ValueError(point)

    loss_part, grad_x, dmod, small = _layer_step(x, mod, positions, sinks[0], ln1_g, ln1_b, ln2_g, ln2_b, loss_target,
                                                 get_w_in, get_rest, hook)

    rows = jnp.concatenate([dmod.reshape(bsz * 6, D_MODEL), small, jnp.full((1, D_MODEL), loss_part, F32),
                            jnp.zeros((SMALL_ROWS - bsz * 6 - 6, D_MODEL), F32)], axis=0)
    small_all = _gather_small(rows, "gather_small")
    sums = _reduce_small(small_all)
    loss = sums[bsz * 6 + 5, 0]
    dmod_all = small_all[:, :bsz * 6].reshape(N_DEV * bsz, 6 * D_MODEL)
    adamw("b_ada", functools.reduce(jnp.add, [sums[6 * i:6 * i + 6] for i in range(bsz)]).reshape(1, 6 * D_MODEL))
    for i, n in enumerate(("ln1_g", "ln1_b", "ln2_g", "ln2_b")):
        adamw(n, sums[12 + i][None])
    adamw("sinks", sums[16][::HEAD_DIM][None])
    dmod_cols = lax.dynamic_slice_in_dim(dmod_all, me * ada_cols, ada_cols, axis=1)
    last = adamw("w_ada", _ada_wgrad(c_all.T, dmod_cols))
    for n, g in _unpack_grads(GROUP_IN, scatters["in"].finish(last)).items():
        adamw(n, g)

    return (loss, grad_x, *[outs[n][0] for n in _WEIGHTS], *[outs[n][1] for n in _WEIGHTS], *[outs[n][2] for n in _WEIGHTS],
            *[outs[n][3] for n in _WEIGHTS])
```

```python
import functools

import jax
import jax.numpy as jnp
from jax import lax
from jax.experimental import pallas as pl
from jax.experimental.pallas import tpu as pltpu

F32 = jnp.float32
BF16 = jnp.bfloat16

D_MODEL = 1024
HEAD_DIM = 64
A_Q_HEADS = 16
A_WINDOW = 128
B_PATTERNS = ((128, 1), (512, 4), (2048, 16))
B_HEADS_PER_GROUP = 8
D_FF = 2816
QBLOCK = 128
ROPE_THETA = 10000.0
LN_EPS = 1e-5
DEEPNORM_ALPHA = 2.0 ** 0.25
NEG_INF = -1e30
ADAM_LR, ADAM_B1, ADAM_B2, ADAM_EPS, ADAM_WD, ADAM_STEP = 0.001, 0.9, 0.999, 1e-08, 0.01, 10

N_DEV = 8
MESH_AXES = ("x", "y", "c")
LANES = 128
VMEM_LIMIT_BYTES = 56 * 1024 * 1024
MESH = pl.DeviceIdType.MESH

OFF_QA, OFF_KVA, OFF_QKVB, OFF_GAB = 0, 1024, 1280, 5888
GROUP_IN = (("w_in", 992),)
GROUP_REST = (("w_branch_a", 128), ("w_branch_b", 64), ("w_o", 128), ("w_gate_up", 704), ("w_down", 352))


def _params(*sem):
    return pltpu.CompilerParams(dimension_semantics=sem, vmem_limit_bytes=VMEM_LIMIT_BYTES)


def _sigmoid(x):
    return 1.0 / (1.0 + jnp.exp(-x))


_DIMS = {"nn": (((1,), (0,)), ((), ())), "nt": (((1,), (1,)), ((), ())), "tn": (((0,), (0,)), ((), ()))}


def _matmul(a, b, *, mode, tm, tn, tk, name, out_dtype=None, n=None, b_off=0, token=None, ins=(), outs=None, epilogue=None):
    if mode == "nn":
        (m, k), nn_ = a.shape, b.shape[1]
    elif mode == "nt":
        (m, k), nn_ = a.shape, (b.shape[0] if n is None else n)
    else:
        (k, m), nn_ = a.shape, b.shape[1]
    assert m % tm == 0 and nn_ % tn == 0 and k % tk == 0 and b_off % tn == 0, (name, m, nn_, k)
    nk = k // tk
    joff = b_off // tn
    if mode == "nn":
        a_spec = pl.BlockSpec((tm, tk), lambda i, j, kk: (i, kk))
        b_spec = pl.BlockSpec((tk, tn), lambda i, j, kk: (kk, j))
    elif mode == "nt":
        a_spec = pl.BlockSpec((tm, tk), lambda i, j, kk: (i, kk))
        b_spec = pl.BlockSpec((tn, tk), lambda i, j, kk: (j + joff, kk))
    else:
        a_spec = pl.BlockSpec((tk, tm), lambda i, j, kk: (kk, i))
        b_spec = pl.BlockSpec((tk, tn), lambda i, j, kk: (kk, j))
    dims = _DIMS[mode]
    has_token = token is not None
    plain = epilogue is None
    if plain:
        outs = [(jax.ShapeDtypeStruct((m, nn_), out_dtype), (tm, tn), lambda i, j: (i, j))]

        def epilogue(acc, i, j, in_refs, out_refs):
            out_refs[0][...] = acc.astype(out_refs[0].dtype)

    nin = len(ins)

    def body(*refs):
        a_ref, b_ref = refs[:2]
        in_refs = refs[2:2 + nin]
        out_refs = refs[2 + nin + has_token:-1]
        acc_ref = refs[-1]
        kk = pl.program_id(2)
        part = lax.dot_general(a_ref[...].astype(BF16), b_ref[...].astype(BF16), dims, preferred_element_type=F32)

        def finish(acc):
            epilogue(acc, pl.program_id(0), pl.program_id(1), in_refs, out_refs)

        if nk == 1:
            finish(part)
        else:
            @pl.when(kk == 0)
            def _():
                acc_ref[...] = part

            @pl.when(kk > 0)
            def _():
                acc_ref[...] += part

            @pl.when(kk == nk - 1)
            def _():
                finish(acc_ref[...])

    def spec(block, index):
        return pl.BlockSpec(block, lambda i, j, kk: index(i, j))

    in_specs, args = [a_spec, b_spec], [a, b]
    for arr, block, index in ins:
        in_specs.append(spec(block, index))
        args.append(arr)
    if has_token:
        in_specs.append(pl.BlockSpec(token.shape, lambda i, j, kk: (0, 0)))
        args.append(token)
    res = pl.pallas_call(
        body,
        name=name,
        grid=(m // tm, nn_ // tn, nk),
        in_specs=in_specs,
        out_specs=[spec(block, index) for _, block, index in outs],
        out_shape=[shape for shape, _, _ in outs],
        scratch_shapes=[pltpu.VMEM((tm, tn) if nk > 1 else (8, LANES), F32)],
        compiler_params=_params("arbitrary", "arbitrary", "arbitrary"),
    )(*args)
    return res[0] if plain else res


def _proj_rope(a, bt, cos, sin, *, n, b_off, rope_cols, tm, tn, name, out_dtype=F32):
    m, k = a.shape
    assert m % tm == 0 and n % tn == 0 and b_off % tn == 0 and rope_cols % tn == 0, name
    joff = b_off // tn
    nrope = rope_cols // tn

    def body(a_ref, b_ref, c_ref, s_ref, o_ref):
        acc = lax.dot_general(a_ref[...], b_ref[...], _DIMS["nt"], preferred_element_type=F32)
        j = pl.program_id(1)

        @pl.when(j < nrope)
        def _():
            o_ref[...] = _rope(acc, c_ref[...], s_ref[...]).astype(o_ref.dtype)

        @pl.when(j >= nrope)
        def _():
            o_ref[...] = acc.astype(o_ref.dtype)

    table = pl.BlockSpec((tm, LANES), lambda i, j: (i, 0))
    return pl.pallas_call(
        body,
        name=name,
        grid=(m // tm, n // tn),
        in_specs=[pl.BlockSpec((tm, k), lambda i, j: (i, 0)), pl.BlockSpec((tn, k), lambda i, j: (j + joff, 0)), table, table],
        out_specs=pl.BlockSpec((tm, tn), lambda i, j: (i, j)),
        out_shape=jax.ShapeDtypeStruct((m, n), out_dtype),
        compiler_params=_params("parallel", "parallel"),
    )(a, bt, cos, sin)


ROW_TILE = 256


def _rows(width, col=0):
    return pl.BlockSpec((1, ROW_TILE, width), lambda b, t: (b, t, col))


def _per_batch(nrows, width):
    return pl.BlockSpec((1, nrows, width), lambda b, t: (b, 0, 0))


def _whole(shape):
    return pl.BlockSpec(shape, lambda b, t: (0,) * len(shape))


def _row_call(body, name, bsz, seq, in_specs, out_specs, out_shape, accumulates=False):
    return pl.pallas_call(
        body,
        name=name,
        grid=(bsz, seq // ROW_TILE),
        in_specs=in_specs,
        out_specs=out_specs,
        out_shape=out_shape,
        compiler_params=_params("parallel", "arbitrary" if accumulates else "parallel"),
    )


def _acc_rows(acc_ref, first, rows):
    @pl.when(first)
    def _():
        acc_ref[...] = jnp.zeros_like(acc_ref)

    for r, val in enumerate(rows):
        acc_ref[0, r:r + 1, :] += val


def _colsum(v):
    return jnp.sum(v, axis=0, keepdims=True)


def _ln_stats(z):
    mu = jnp.mean(z, axis=-1, keepdims=True)
    zc = z - mu
    var = jnp.mean(zc * zc, axis=-1, keepdims=True)
    rstd = lax.rsqrt(var + LN_EPS)
    return zc * rstd, rstd


def _ln_bwd(dxhat, xhat, rstd):
    m1 = jnp.mean(dxhat, axis=-1, keepdims=True)
    m2 = jnp.mean(dxhat * xhat, axis=-1, keepdims=True)
    return rstd * (dxhat - m1 - xhat * m2)


def _modulate_in(x, mod):
    bsz, seq, d = x.shape

    def body(x_ref, mod_ref, u_ref):
        u_ref[0] = (x_ref[0] * (1.0 + mod_ref[0, 1:2, :]) + mod_ref[0, 0:1, :]).astype(BF16)

    return _row_call(body, "modulate_in", bsz, seq, [_rows(d), _per_batch(8, d)], _rows(d),
                     jax.ShapeDtypeStruct((bsz, seq, d), BF16))(x, mod)


def _gate_merge(gab, ya, yb):
    bsz, seq, d = ya.shape

    def body(ga_ref, gb_ref, ya_ref, yb_ref, o_ref):
        ga, gb, ya_, yb_ = (r[0].astype(F32) for r in (ga_ref, gb_ref, ya_ref, yb_ref))
        o_ref[0] = (_sigmoid(ga) * ya_ + _sigmoid(gb) * yb_).astype(BF16)

    return _row_call(body, "gate_merge", bsz, seq, [_rows(d, 0), _rows(d, 1), _rows(d), _rows(d)], _rows(d),
                     jax.ShapeDtypeStruct((bsz, seq, d), BF16))(gab, gab, ya, yb)


EP_TILE = 512


def _ep_specs(seq, d):
    tiles = seq // EP_TILE
    return ((EP_TILE, d), lambda i, j: (i, 0)), ((1, 8, d), lambda i, j: (i // tiles, 0, 0)), ((1, d), lambda i, j: (0, 0))


def _wo_ln1(merged, wo, x, mod, g, b, seq):
    ntok, d = x.shape
    row, per_b, whole = _ep_specs(seq, d)

    def epilogue(y, i, j, ins, outs):
        x_ref, mod_ref, g_ref, b_ref = ins
        y_ref, h_ref, u_ref = outs
        z = DEEPNORM_ALPHA * x_ref[...] + (1.0 + mod_ref[0, 2:3, :]) * y
        xhat, _ = _ln_stats(z)
        h = xhat * g_ref[...] + b_ref[...]
        y_ref[...] = y
        h_ref[...] = h
        u_ref[...] = (h * (1.0 + mod_ref[0, 4:5, :]) + mod_ref[0, 3:4, :]).astype(BF16)

    f32, bf16 = jax.ShapeDtypeStruct((ntok, d), F32), jax.ShapeDtypeStruct((ntok, d), BF16)
    return _matmul(merged, wo, mode="nn", tm=EP_TILE, tn=d, tk=d, name="w_o_ln1",
                   ins=[(x,) + row, (mod,) + per_b, (g,) + whole, (b,) + whole],
                   outs=[(f32,) + row, (f32,) + row, (bf16,) + row], epilogue=epilogue)


FF_HALF = D_FF // 2


def _interleave_gate_up(w):
    return w.reshape(2, 2, FF_HALF, w.shape[1]).transpose(1, 0, 2, 3).reshape(w.shape)


def _gate_up_silu(u2, wgut_i):
    ntok = u2.shape[0]

    def epilogue(h, i, j, ins, outs):
        h_ref, a_ref = outs
        hg, hu = h[:, :FF_HALF], h[:, FF_HALF:]
        h_ref[...] = h.astype(BF16)
        a_ref[...] = (hg * _sigmoid(hg) * hu).astype(BF16)

    return _matmul(u2, wgut_i, mode="nt", tm=EP_TILE, tn=2 * FF_HALF, tk=u2.shape[1], name="gate_up_silu",
                   outs=[(jax.ShapeDtypeStruct((ntok, 2 * D_FF), BF16), (EP_TILE, 2 * FF_HALF), lambda i, j: (i, j)),
                         (jax.ShapeDtypeStruct((ntok, D_FF), BF16), (EP_TILE, FF_HALF), lambda i, j: (i, j))],
                   epilogue=epilogue)


def _down_dgrad_silu_bwd(dy2, wd, h_i):
    ntok = dy2.shape[0]
    wide = ((EP_TILE, 2 * FF_HALF), lambda i, j: (i, j))

    def epilogue(da, i, j, ins, outs):
        h = ins[0][...].astype(F32)
        hg, hu = h[:, :FF_HALF], h[:, FF_HALF:]
        sg = _sigmoid(hg)
        outs[0][:, :FF_HALF] = (da * hu * (sg * (1.0 + hg * (1.0 - sg)))).astype(BF16)
        outs[0][:, FF_HALF:] = (da * (hg * sg)).astype(BF16)

    return _matmul(dy2, wd, mode="nt", tm=EP_TILE, tn=FF_HALF, tk=dy2.shape[1], name="down_dgrad_silu_bwd",
                   ins=[(h_i,) + wide], outs=[(jax.ShapeDtypeStruct((ntok, 2 * D_FF), BF16),) + wide], epilogue=epilogue)[0]


def _down_ln2_loss_bwd(a, wd, h1, mod, g, b, target, seq):
    ntok, d = h1.shape
    row, per_b, whole = _ep_specs(seq, d)
    tiles = seq // EP_TILE

    def epilogue(y, i, j, ins, outs):
        h_ref, mod_ref, g_ref, b_ref, t_ref = ins
        dy_ref, dh_ref, acc_ref = outs
        gate = 1.0 + mod_ref[0, 5:6, :]
        z = DEEPNORM_ALPHA * h_ref[...] + gate * y
        xhat, rstd = _ln_stats(z)
        diff = xhat * g_ref[...] + b_ref[...] - t_ref[...]
        loss = 0.5 * jnp.sum(jnp.sum(diff * diff, axis=-1, keepdims=True) / d, axis=0, keepdims=True)
        dout = diff / d
        dz = _ln_bwd(dout * g_ref[...], xhat, rstd)
        dy_ref[...] = (gate * dz).astype(BF16)
        dh_ref[...] = DEEPNORM_ALPHA * dz
        _acc_rows(acc_ref, i % tiles == 0,
                  [_colsum(dout * xhat), _colsum(dout), _colsum(dz * y), jnp.broadcast_to(loss, (1, d))])

    return _matmul(a, wd, mode="nn", tm=EP_TILE, tn=d, tk=a.shape[1], name="down_ln2_loss_bwd",
                   ins=[(h1,) + row, (mod,) + per_b, (g,) + whole, (b,) + whole, (target,) + row],
                   outs=[(jax.ShapeDtypeStruct((ntok, d), BF16),) + row, (jax.ShapeDtypeStruct((ntok, d), F32),) + row,
                         (jax.ShapeDtypeStruct((ntok // seq, 8, d), F32),) + per_b], epilogue=epilogue)


def _silu_mul_bwd(da, h):
    bsz, seq, _ = h.shape

    def body(da_ref, hg_ref, hu_ref, dh_ref):
        hg, da_ = hg_ref[0].astype(F32), da_ref[0].astype(F32)
        sg = _sigmoid(hg)
        dh_ref[0, :, :D_FF] = (da_ * hu_ref[0].astype(F32) * (sg * (1.0 + hg * (1.0 - sg)))).astype(BF16)
        dh_ref[0, :, D_FF:] = (da_ * (hg * sg)).astype(BF16)

    return _row_call(body, "silu_mul_bwd", bsz, seq, [_rows(D_FF), _rows(D_FF, 0), _rows(D_FF, 1)], _rows(2 * D_FF),
                     jax.ShapeDtypeStruct((bsz, seq, 2 * D_FF), BF16))(da, h, h)


def _gate_up_dgrad_ln1_bwd(dh, wgut, dh1a, x, y1, mod, g, b, seq):
    ntok, d = x.shape
    row, per_b, whole = _ep_specs(seq, d)
    tiles = seq // EP_TILE

    def epilogue(du, i, j, ins, outs):
        dh_ref, x_ref, y_ref, mod_ref, g_ref, b_ref = ins
        dy_ref, dx_ref, acc_ref = outs
        y = y_ref[...]
        gate = 1.0 + mod_ref[0, 2:3, :]
        z = DEEPNORM_ALPHA * x_ref[...] + gate * y
        xhat, rstd = _ln_stats(z)
        h1 = xhat * g_ref[...] + b_ref[...]
        dh1 = dh_ref[...] + du * (1.0 + mod_ref[0, 4:5, :])
        dz = _ln_bwd(dh1 * g_ref[...], xhat, rstd)
        dy_ref[...] = (gate * dz).astype(BF16)
        dx_ref[...] = DEEPNORM_ALPHA * dz
        _acc_rows(acc_ref, i % tiles == 0,
                  [_colsum(dh1 * xhat), _colsum(dh1), _colsum(dz * y), _colsum(du * h1), _colsum(du)])

    return _matmul(dh, wgut, mode="nn", tm=EP_TILE, tn=d, tk=D_FF, name="gate_up_dgrad_ln1_bwd",
                   ins=[(dh1a,) + row, (x,) + row, (y1,) + row, (mod,) + per_b, (g,) + whole, (b,) + whole],
                   outs=[(jax.ShapeDtypeStruct((ntok, d), BF16),) + row, (jax.ShapeDtypeStruct((ntok, d), F32),) + row,
                         (jax.ShapeDtypeStruct((ntok // seq, 8, d), F32),) + per_b], epilogue=epilogue)


def _wo_dgrad_gate_bwd(dy1, wo, gab, ya, yb):
    ntok, d = ya.shape
    tm, tn = 1024, 512
    tile = ((tm, tn), lambda i, j: (i, j))
    tile_b = ((tm, tn), lambda i, j: (i, j + d // tn))

    def epilogue(dm_, i, j, ins, outs):
        ga_ref, gb_ref, ya_ref, yb_ref = ins
        dya_ref, dyb_ref, dga_ref, dgb_ref = outs
        sa, sb = _sigmoid(ga_ref[...].astype(F32)), _sigmoid(gb_ref[...].astype(F32))
        dya_ref[...] = (dm_ * sa).astype(BF16)
        dyb_ref[...] = (dm_ * sb).astype(BF16)
        dga_ref[...] = (dm_ * ya_ref[...].astype(F32) * sa * (1.0 - sa)).astype(BF16)
        dgb_ref[...] = (dm_ * yb_ref[...].astype(F32) * sb * (1.0 - sb)).astype(BF16)

    shp = jax.ShapeDtypeStruct((ntok, d), BF16)
    return _matmul(dy1, wo, mode="nt", tm=tm, tn=tn, tk=d, name="w_o_dgrad_gate_bwd",
                   ins=[(gab,) + tile, (gab,) + tile_b, (ya,) + tile, (yb,) + tile],
                   outs=[(shp,) + tile] * 4, epilogue=epilogue)


def _w_in_dgrad_grad_x(dproj, wint, dxa, x, mod, seq, token):
    ntok, d = x.shape
    row, per_b, _ = _ep_specs(seq, d)
    tiles = seq // EP_TILE

    def epilogue(du, i, j, ins, outs):
        dxa_ref, x_ref, mod_ref = ins
        gx_ref, acc_ref = outs
        gx_ref[...] = dxa_ref[...] + du * (1.0 + mod_ref[0, 1:2, :])
        _acc_rows(acc_ref, i % tiles == 0, [_colsum(du * x_ref[...]), _colsum(du)])

    return _matmul(dproj, wint, mode="nn", tm=EP_TILE, tn=d, tk=wint.shape[0] // 2, name="w_in_dgrad_grad_x", token=token,
                   ins=[(dxa,) + row, (x,) + row, (mod,) + per_b],
                   outs=[(jax.ShapeDtypeStruct((ntok, d), F32),) + row, (jax.ShapeDtypeStruct((ntok // seq, 8, d), F32),) + per_b],
                   epilogue=epilogue)


def _branch_b_gate_merge(ob, wbbt, gab, ya):
    ntok, d = ya.shape
    tm, tn = 1024, 512
    tile = ((tm, tn), lambda i, j: (i, j))
    tile_b = ((tm, tn), lambda i, j: (i, j + d // tn))

    def epilogue(yb, i, j, ins, outs):
        ga_ref, gb_ref, ya_ref = ins
        yb_ref, merged_ref = outs
        yb_ref[...] = yb.astype(BF16)
        merged_ref[...] = (_sigmoid(ga_ref[...].astype(F32)) * ya_ref[...].astype(F32)
                           + _sigmoid(gb_ref[...].astype(F32)) * yb).astype(BF16)

    shp = jax.ShapeDtypeStruct((ntok, d), BF16)
    return _matmul(ob, wbbt, mode="nt", tm=tm, tn=tn, tk=ob.shape[1], name="branch_b_gate_merge",
                   ins=[(gab,) + tile, (gab,) + tile_b, (ya,) + tile], outs=[(shp,) + tile] * 2, epilogue=epilogue)


def _segsum64(v):
    rows, width = v.shape
    ri = lax.broadcasted_iota(jnp.int32, (LANES, LANES), 0) // HEAD_DIM
    ci = lax.broadcasted_iota(jnp.int32, (LANES, LANES), 1) // HEAD_DIM
    ones = jnp.where(ri == ci, 1.0, 0.0).astype(BF16)
    out = []
    for c in range(width // LANES):
        part = v[:, c * LANES:(c + 1) * LANES]
        hi = part.astype(BF16)
        lo = (part - hi.astype(F32)).astype(BF16)
        out.append(jnp.dot(hi, ones, preferred_element_type=F32) + jnp.dot(lo, ones, preferred_element_type=F32))
    return jnp.concatenate(out, axis=1) if len(out) > 1 else out[0]


def _merge_b(os_, ls_):
    bsz, seq, w = os_[0].shape

    def body(o0, o1, o2, l0, l1, l2, ob_ref):
        ls = [l0[0], l1[0], l2[0]]
        mx = jnp.maximum(jnp.maximum(ls[0], ls[1]), ls[2])
        es = [jnp.exp(l - mx) for l in ls]
        den = es[0] + es[1] + es[2]
        ob_ref[0] = ((es[0] / den) * o0[0] + (es[1] / den) * o1[0] + (es[2] / den) * o2[0]).astype(BF16)

    return _row_call(body, "merge_b", bsz, seq, [_rows(w)] * 6, _rows(w),
                     jax.ShapeDtypeStruct((bsz, seq, w), BF16))(*os_, *ls_)


def _branch_b_dgrad_merge_bwd(dyb, wbbt, os_, ls_):
    ntok, w = os_[0].shape
    row = ((EP_TILE, w), lambda i, j: (i, 0))

    def epilogue(dob_, i, j, ins, outs):
        os_r, ls_r = ins[:3], ins[3:]
        do_r, dd_r = outs[:3], outs[3:]
        ls = [l[...] for l in ls_r]
        mx = jnp.maximum(jnp.maximum(ls[0], ls[1]), ls[2])
        es = [jnp.exp(l - mx) for l in ls]
        den = es[0] + es[1] + es[2]
        ws = [e / den for e in es]
        dws = [_segsum64(dob_ * o[...]) for o in os_r]
        mean = ws[0] * dws[0] + ws[1] * dws[1] + ws[2] * dws[2]
        for wg, do_ref, dd_ref in zip(ws, do_r, dd_r):
            do_ref[...] = wg * dob_
            dd_ref[...] = -wg * mean

    shp = jax.ShapeDtypeStruct((ntok, w), F32)
    return _matmul(dyb, wbbt, mode="nn", tm=EP_TILE, tn=w, tk=dyb.shape[1], name="branch_b_dgrad_merge_bwd",
                   ins=[(v,) + row for v in list(os_) + list(ls_)], outs=[(shp,) + row] * 6, epilogue=epilogue)


def _branch_a_dgrad_delta(dya, wba, oa, lse_a, sinks_exp, seq):
    ntok, w = oa.shape
    row, per_b, whole = _ep_specs(seq, w)
    tiles = seq // EP_TILE

    def epilogue(do_, i, j, ins, outs):
        o_ref, l_ref, s_ref = ins
        do_ref, dd_ref, acc_ref = outs
        dd = -_segsum64(do_ * o_ref[...])
        do_ref[...] = do_
        dd_ref[...] = dd
        _acc_rows(acc_ref, i % tiles == 0, [_colsum(dd * jnp.exp(s_ref[...] - l_ref[...]))])

    shp = jax.ShapeDtypeStruct((ntok, w), F32)
    return _matmul(dya, wba, mode="nt", tm=EP_TILE, tn=w, tk=dya.shape[1], name="branch_a_dgrad_delta",
                   ins=[(oa,) + row, (lse_a,) + row, (sinks_exp,) + whole],
                   outs=[(shp,) + row, (shp,) + row, (jax.ShapeDtypeStruct((ntok // seq, 8, w), F32),) + per_b],
                   epilogue=epilogue)


def _swap_halves(v):
    src = lax.broadcasted_iota(jnp.int32, (LANES, LANES), 0)
    dst = lax.broadcasted_iota(jnp.int32, (LANES, LANES), 1)
    partner = jnp.where((dst % HEAD_DIM) < HEAD_DIM // 2, dst + HEAD_DIM // 2, dst - HEAD_DIM // 2)
    perm = jnp.where(src == partner, 1.0, 0.0).astype(BF16)
    hi = v.astype(BF16)
    lo = (v - hi.astype(F32)).astype(BF16)
    return jnp.dot(hi, perm, preferred_element_type=F32) + jnp.dot(lo, perm, preferred_element_type=F32)


def _swap_halves_roll(v):
    lane = lax.broadcasted_iota(jnp.int32, v.shape, 1)
    return jnp.where((lane % HEAD_DIM) < HEAD_DIM // 2, pltpu.roll(v, LANES - HEAD_DIM // 2, 1),
                     pltpu.roll(v, HEAD_DIM // 2, 1))


def _rope(v, cos, sin, sign=1.0, mxu=True):
    swap = _swap_halves if mxu else _swap_halves_roll
    out = []
    for c in range(v.shape[1] // LANES):
        part = v[:, c * LANES:(c + 1) * LANES]
        out.append(part * cos + sign * (swap(part) * sin))
    return jnp.concatenate(out, axis=1) if len(out) > 1 else out[0]


def _half_mask(shape, half):
    lane = lax.broadcasted_iota(jnp.int32, shape, len(shape) - 1) % LANES
    return (lane < HEAD_DIM) if half == 0 else (lane >= HEAD_DIM)


def _dup_half(v, half):
    return jnp.where(_half_mask(v.shape, half), v, pltpu.roll(v, HEAD_DIM, 1))


def _fold_halves(v):
    return v + pltpu.roll(v, HEAD_DIM, 1)


def _pick_halves(lo_rows, hi_rows):
    return jnp.where(_half_mask(lo_rows.shape, 0), lo_rows, hi_rows)


def _stack_masked(v, pairs):
    parts = []
    for c in pairs:
        pair = v[:, c * LANES:(c + 1) * LANES]
        parts += [jnp.where(_half_mask(pair.shape, half), pair, 0.0) for half in (0, 1)]
    return jnp.concatenate(parts, axis=0)


def _stack_pair_cols(v, pairs):
    return jnp.concatenate([v[:, c * LANES + half * HEAD_DIM:c * LANES + half * HEAD_DIM + 1] for c in pairs for half in (0, 1)],
                           axis=0)


ATTN_UNITS = 16


def _class_rows(r):
    return [pl.ds(0, QBLOCK)] if r == 1 else [pl.ds(rho, QBLOCK, stride=r) for rho in range(r)]


def _band_mask(nrows, nk, blk, n_back, has_prev):
    qi = lax.broadcasted_iota(jnp.int32, (nrows, nk), 0) % QBLOCK
    ki = lax.broadcasted_iota(jnp.int32, (nrows, nk), 1)
    if has_prev:
        dist = qi + QBLOCK - ki
        return (dist >= 0) & (dist <= n_back) & ((ki >= QBLOCK) | (blk > 0))
    dist = qi - ki
    return (dist >= 0) & (dist <= n_back)


def _attn_fwd(q_arr, k_arr, v_arr, *, name, npair, gqa, q_col, k_col, v_col, nchunk, r, n_back, sinks=None):
    bsz, seq, _ = q_arr.shape
    rr = QBLOCK * r
    nblk = seq // rr
    qw = npair * LANES
    kw = LANES if gqa else qw
    has_prev = nblk > 1
    has_sink = sinks is not None
    scale = HEAD_DIM ** -0.5

    def body(*refs):
        refs = list(refs)
        q_ref, kc_ref, vc_ref = refs[:3]
        pos = 3
        if has_prev:
            kp_ref, vp_ref = refs[pos:pos + 2]
            pos += 2
        if has_sink:
            sink_ref = refs[pos]
            pos += 1
        o_ref, lse_ref = refs[pos:pos + 2]
        blk = pl.program_id(2)
        nk = (2 if has_prev else 1) * QBLOCK
        valid = _band_mask(QBLOCK, nk, blk, n_back, has_prev)
        per = npair // 2
        classes = _class_rows(r)
        step = max(1, ATTN_UNITS // (2 * npair))
        for first in range(0, len(classes), step):
            batch = classes[first:first + step]
            units = []
            for ci, rows in enumerate(batch):
                q = q_ref[0, rows, :] * scale
                k, v = kc_ref[0, rows, :], vc_ref[0, rows, :]
                if has_prev:
                    k = jnp.concatenate([kp_ref[0, rows, :], k], axis=0)
                    v = jnp.concatenate([vp_ref[0, rows, :], v], axis=0)
                if gqa:
                    kdup = [_dup_half(k, hk).astype(BF16) for hk in range(2)]
                    vdup = [_dup_half(v, hk) for hk in range(2)]
                for c in range(npair):
                    sl = slice(c * LANES, (c + 1) * LANES)
                    qc = q[:, sl]
                    kc, vc = (kdup[c // per], vdup[c // per]) if gqa else (k[:, sl].astype(BF16), v[:, sl])
                    for half in (0, 1):
                        qm = jnp.where(_half_mask(qc.shape, half), qc, 0.0).astype(BF16)
                        vm = jnp.where(_half_mask(vc.shape, half), vc, 0.0).astype(BF16)
                        s = lax.dot_general(qm, kc, _DIMS["nt"], preferred_element_type=F32)
                        units.append(dict(ci=ci, c=c, half=half, s=s, vm=vm, sk=sink_ref[2 * c + half] if has_sink else None))
            for u in units:
                s = jnp.where(valid, u["s"], NEG_INF)
                m = jnp.max(s, axis=1, keepdims=True)
                if has_sink:
                    m = jnp.maximum(m, u["sk"])
                p = jnp.exp(s - m)
                den = jnp.sum(p, axis=1, keepdims=True)
                if has_sink:
                    den = den + jnp.exp(u["sk"] - m)
                u.update(p=p.astype(BF16), den=den, lse=m + jnp.log(den))
            for u in units:
                u["o"] = jnp.dot(u["p"], u["vm"], preferred_element_type=F32) / u["den"]
            for ci, rows in enumerate(batch):
                outs, lses = [None] * npair, [None] * npair
                for u in units:
                    if u["ci"] != ci:
                        continue
                    c, o = u["c"], u["o"]
                    lse = jnp.broadcast_to(u["lse"], o.shape)
                    outs[c] = o if u["half"] == 0 else outs[c] + o
                    lses[c] = lse if u["half"] == 0 else _pick_halves(lses[c], lse)
                o_ref[0, rows, :] = jnp.concatenate(outs, axis=1) if npair > 1 else outs[0]
                lse_ref[0, rows, :] = jnp.concatenate(lses, axis=1) if npair > 1 else lses[0]

    def cur(width, col0):
        return pl.BlockSpec((1, rr, width), lambda b, c, i: (b, i, col0 + c))

    def prev(width, col0):
        return pl.BlockSpec((1, rr, width), lambda b, c, i: (b, jnp.maximum(i - 1, 0), col0 + c))

    in_specs = [cur(qw, q_col), cur(kw, k_col), cur(kw, v_col)]
    args = [q_arr, k_arr, v_arr]
    if has_prev:
        in_specs += [prev(kw, k_col), prev(kw, v_col)]
        args += [k_arr, v_arr]
    if has_sink:
        in_specs.append(pl.BlockSpec(memory_space=pltpu.SMEM))
        args.append(sinks)
    return pl.pallas_call(
        body,
        name=name,
        grid=(bsz, nchunk, nblk),
        in_specs=in_specs,
        out_specs=[pl.BlockSpec((1, rr, qw), lambda b, c, i: (b, i, c))] * 2,
        out_shape=[jax.ShapeDtypeStruct((bsz, seq, nchunk * qw), F32)] * 2,
        compiler_params=_params("parallel", "parallel", "parallel"),
    )(*args)


def _attn_bwd(q_arr, k_arr, v_arr, cos, sin, do, lse, dd, *, name, npair, gqa, q_col, k_col, v_col, nchunk, r, n_back,
              token=None):
    bsz, seq, _ = q_arr.shape
    rr = QBLOCK * r
    nblk = seq // rr
    qw = npair * LANES
    kw = LANES if gqa else qw
    has_next = nblk > 1
    has_token = token is not None
    scale = HEAD_DIM ** -0.5

    def body(*refs):
        refs = list(refs)
        k_ref, v_ref, c_ref, s_ref = refs[:4]
        tile_refs = [refs[4:8]]
        pos = 8
        if has_next:
            tile_refs.append(refs[pos:pos + 4])
            pos += 4
        if has_token:
            pos += 1
        dq_ref, dk_ref, dv_ref = refs[pos:pos + 3]
        carry_ref = refs[pos + 3]
        blk = pl.program_id(2)
        if has_next:
            @pl.when(blk == 0)
            def _():
                carry_ref[...] = jnp.zeros_like(carry_ref)

        nrows = (npair if gqa else 1) * QBLOCK
        qi = lax.broadcasted_iota(jnp.int32, (nrows, QBLOCK), 0) % QBLOCK
        ki = lax.broadcasted_iota(jnp.int32, (nrows, QBLOCK), 1)
        valids = [qi >= ki, (qi + QBLOCK - ki <= n_back) & (blk + 1 < nblk)]
        per = npair // 2
        ntile = len(tile_refs)
        cat = lambda parts: jnp.concatenate(parts, axis=1) if len(parts) > 1 else parts[0]
        classes = _class_rows(r)
        step = max(1, ATTN_UNITS // (ntile * (2 if gqa else 2 * npair)))
        for first in range(0, len(classes), step):
            batch = classes[first:first + step]
            units = []
            for ci, rows in enumerate(batch):
                tiles = [(q_ref[0, rows, :] * scale, do_ref[0, rows, :], l_ref[0, rows, :], d_ref[0, rows, :])
                         for q_ref, do_ref, l_ref, d_ref in tile_refs]
                k, v = k_ref[0, rows, :], v_ref[0, rows, :]
                if gqa:
                    for hk in range(2):
                        pairs = list(range(hk * per, (hk + 1) * per))
                        kd, vd = _dup_half(k, hk).astype(BF16), _dup_half(v, hk).astype(BF16)
                        for t, (q, do_, l_, d_) in enumerate(tiles):
                            units.append(dict(ci=ci, t=t, hk=hk, pairs=pairs, qs=_stack_masked(q, pairs).astype(BF16),
                                              dos=_stack_masked(do_, pairs).astype(BF16), lcol=_stack_pair_cols(l_, pairs),
                                              dcol=_stack_pair_cols(d_, pairs), kmat=kd, vmat=vd, kdq=kd))
                else:
                    for c in range(npair):
                        sl = slice(c * LANES, (c + 1) * LANES)
                        kc, vcb = k[:, sl], v[:, sl].astype(BF16)
                        kcb = kc.astype(BF16)
                        for t, (q, do_, l_, d_) in enumerate(tiles):
                            for half in (0, 1):
                                hm = _half_mask(kc.shape, half)
                                col = c * LANES + half * HEAD_DIM
                                units.append(dict(ci=ci, t=t, c=c, half=half, qs=jnp.where(hm, q[:, sl], 0.0).astype(BF16),
                                                  dos=jnp.where(hm, do_[:, sl], 0.0).astype(BF16), lcol=l_[:, col:col + 1],
                                                  dcol=d_[:, col:col + 1], kmat=kcb, vmat=vcb,
                                                  kdq=jnp.where(hm, kc, 0.0).astype(BF16)))
            for u in units:
                u["s"] = lax.dot_general(u["qs"], u["kmat"], _DIMS["nt"], preferred_element_type=F32)
                u["dp"] = lax.dot_general(u["dos"], u["vmat"], _DIMS["nt"], preferred_element_type=F32)
            for u in units:
                p = jnp.exp(jnp.where(valids[u["t"]], u["s"], NEG_INF) - u["lcol"])
                u["ds"] = (p * (u["dp"] + u["dcol"])).astype(BF16)
                u["p"] = p.astype(BF16)
            for u in units:
                u["dv"] = lax.dot_general(u["p"], u["dos"], _DIMS["tn"], preferred_element_type=F32)
                u["dk"] = lax.dot_general(u["ds"], u["qs"], _DIMS["tn"], preferred_element_type=F32)
                u["dq"] = jnp.dot(u["ds"], u["kdq"], preferred_element_type=F32) * scale
            for ci, rows in enumerate(batch):
                mine = [u for u in units if u["ci"] == ci]
                dq = [[None] * npair for _ in range(ntile)]
                if gqa:
                    dk_out = dv_out = None
                    for hk in range(2):
                        us = [u for u in mine if u["hk"] == hk]
                        for u in us:
                            for i, c in enumerate(u["pairs"]):
                                dq[u["t"]][c] = _pick_halves(u["dq"][2 * i * QBLOCK:(2 * i + 1) * QBLOCK],
                                                             u["dq"][(2 * i + 1) * QBLOCK:(2 * i + 2) * QBLOCK])
                        dk_h = _fold_halves(functools.reduce(jnp.add, [u["dk"] for u in us]))
                        dv_h = _fold_halves(functools.reduce(jnp.add, [u["dv"] for u in us]))
                        dk_out = dk_h if hk == 0 else _pick_halves(dk_out, dk_h)
                        dv_out = dv_h if hk == 0 else _pick_halves(dv_out, dv_h)
                else:
                    dks, dvs = [], []
                    for c in range(npair):
                        us = [u for u in mine if u["c"] == c]
                        dks.append(functools.reduce(jnp.add, [u["dk"] for u in us]))
                        dvs.append(functools.reduce(jnp.add, [u["dv"] for u in us]))
                        for t in range(ntile):
                            dq[t][c] = functools.reduce(jnp.add, [u["dq"] for u in us if u["t"] == t])
                    dk_out, dv_out = cat(dks), cat(dvs)
                ck, sk_ = c_ref[0, rows, :], s_ref[0, rows, :]
                dk_ref[0, rows, :] = _rope(dk_out, ck, sk_, sign=-1.0, mxu=gqa)
                dv_ref[0, rows, :] = dv_out
                dq_cur = cat(dq[0])
                if has_next:
                    dq_cur = dq_cur + carry_ref[rows, :]
                    carry_ref[rows, :] = cat(dq[1])
                dq_ref[0, rows, :] = _rope(dq_cur, ck, sk_, sign=-1.0, mxu=gqa)

    def at(width, col0, shift):
        return pl.BlockSpec((1, rr, width), lambda b, c, i: (b, jnp.minimum(i + shift, nblk - 1), col0 + c))

    in_specs = [at(kw, k_col, 0), at(kw, v_col, 0), pl.BlockSpec((1, rr, LANES), lambda b, c, i: (b, i, 0)),
                pl.BlockSpec((1, rr, LANES), lambda b, c, i: (b, i, 0))]
    args = [k_arr, v_arr, cos, sin]
    for shift in (0, 1) if has_next else (0,):
        in_specs += [at(qw, q_col, shift), at(qw, 0, shift), at(qw, 0, shift), at(qw, 0, shift)]
        args += [q_arr, do, lse, dd]
    if has_token:
        in_specs.append(pl.BlockSpec(token.shape, lambda b, c, i: (0, 0)))
        args.append(token)
    return pl.pallas_call(
        body,
        name=name,
        grid=(bsz, nchunk, nblk),
        in_specs=in_specs,
        out_specs=[pl.BlockSpec((1, rr, qw), lambda b, c, i: (b, i, c)),
                   pl.BlockSpec((1, rr, kw), lambda b, c, i: (b, i, c)),
                   pl.BlockSpec((1, rr, kw), lambda b, c, i: (b, i, c))],
        out_shape=[jax.ShapeDtypeStruct((bsz, seq, nchunk * qw), F32),
                   jax.ShapeDtypeStruct((bsz, seq, nchunk * kw), F32),
                   jax.ShapeDtypeStruct((bsz, seq, nchunk * kw), F32)],
        scratch_shapes=[pltpu.VMEM((rr, qw) if has_next else (8, LANES), F32)],
        compiler_params=_params("parallel", "parallel", "arbitrary"),
    )(*args)


B_CHUNKS = {1: (4, 1), 4: (1, 4), 16: (1, 4)}


def _rope_tables(positions):
    half = HEAD_DIM // 2
    inv = ROPE_THETA ** (-jnp.arange(half, dtype=F32) / half)
    ang = positions.astype(F32)[..., None] * inv
    cos, sin = jnp.cos(ang), jnp.sin(ang)
    return jnp.concatenate([cos] * 4, axis=-1), jnp.concatenate([-sin, sin, -sin, sin], axis=-1)


def _layer_step(x, mod, positions, sinks, ln1_g, ln1_b, ln2_g, ln2_b, target, get_w_in, get_rest, hook):
    bsz, seq, d = x.shape
    ntok = bsz * seq
    flat = lambda v: v.reshape(ntok, v.shape[-1])
    unflat = lambda v: v.reshape(bsz, seq, v.shape[-1])
    cos, sin = _rope_tables(positions)
    mm = functools.partial(_matmul, tm=1024, tk=1024)
    scalar = lambda tok: 0.0 if tok is None else tok[0, 0]

    u1 = _modulate_in(x, mod)
    u1f = flat(u1)
    wint = get_w_in(u1)
    cosf, sinf = flat(cos), flat(sin)
    proj = functools.partial(_proj_rope, u1f, wint, cosf, sinf, tm=2048)
    qkvb = unflat(proj(n=4608, b_off=OFF_QKVB, rope_cols=3072, tn=256, name="proj_qkvb"))
    b_kws, os_, ls_ = [], [], []
    for g, (window, r) in enumerate(B_PATTERNS):
        npair, nch = B_CHUNKS[r]
        per = B_HEADS_PER_GROUP // (2 * npair)
        nsec = len(B_PATTERNS) * per
        kw_ = dict(npair=npair, gqa=False, q_col=g * per, k_col=nsec + g * per, v_col=2 * nsec + g * per, nchunk=nch, r=r,
                   n_back=window // r)
        b_kws.append(kw_)
        o_g, l_g = _attn_fwd(qkvb, qkvb, qkvb, name=f"attn_b{g}_fwd", **kw_)
        os_.append(o_g)
        ls_.append(l_g)
    ob = _merge_b(os_, ls_)
    tok = hook("projected", ob)
    u1t = u1f if tok is None else lax.optimization_barrier((u1f, tok))[0]
    proj = functools.partial(_proj_rope, u1t, wint, cosf, sinf, tm=2048)
    gab = unflat(proj(n=2048, b_off=OFF_GAB, rope_cols=0, tn=256, name="proj_gab", out_dtype=BF16))
    qa = unflat(proj(n=1024, b_off=OFF_QA, rope_cols=1024, tn=512, name="proj_qa"))
    kva = unflat(proj(n=256, b_off=OFF_KVA, rope_cols=128, tn=128, name="proj_kva"))
    a_kw = dict(npair=A_Q_HEADS // 2, gqa=True, q_col=0, k_col=0, v_col=1, nchunk=1, r=1, n_back=A_WINDOW - 1)
    oa, lse_a = _attn_fwd(qa, kva, kva, name="attn_a_fwd", sinks=sinks.reshape(A_Q_HEADS), **a_kw)
    rest = get_rest(oa)
    wba, wbbt, wo, wgut, wd = (rest[n] for n in ("w_branch_a", "w_branch_b", "w_o", "w_gate_up", "w_down"))
    ya = unflat(mm(flat(oa), wba, mode="nn", out_dtype=BF16, tn=512, name="branch_a"))
    ybf, mergedf = _branch_b_gate_merge(flat(ob), wbbt, flat(gab), flat(ya))
    xf = flat(x)
    y1f, h1f, u2f = _wo_ln1(mergedf, wo, xf, mod, ln1_g, ln1_b, seq)
    wgut_i = _interleave_gate_up(wgut)
    hf, af = _gate_up_silu(u2f, wgut_i)

    dy2f, dh1af, acc2 = _down_ln2_loss_bwd(af, wd, h1f, mod, ln2_g, ln2_b, flat(target), seq)
    g_wd = _matmul(af, dy2f, mode="tn", out_dtype=BF16, tm=256, tn=1024, tk=ntok, name="down_wgrad")
    dhf = _down_dgrad_silu_bwd(dy2f, wd, hf)
    g_wgut = _interleave_gate_up(_matmul(dhf, u2f, mode="tn", out_dtype=BF16, tm=256, tn=1024, tk=ntok, name="gate_up_wgrad"))
    dy1f, dxaf, acc1 = _gate_up_dgrad_ln1_bwd(dhf, wgut_i, dh1af, xf, y1f, mod, ln1_g, ln1_b, seq)
    g_wo = _matmul(mergedf, dy1f, mode="tn", out_dtype=BF16, tm=256, tn=1024, tk=ntok, name="w_o_wgrad")
    dyaf, dybf, dgaf, dgbf = _wo_dgrad_gate_bwd(dy1f, wo, flat(gab), flat(ya), ybf)
    g_wba = _matmul(flat(oa), dyaf, mode="tn", out_dtype=BF16, tm=256, tn=1024, tk=ntok, name="branch_a_wgrad")
    g_wbbt = _matmul(dybf, flat(ob), mode="tn", out_dtype=BF16, tm=256, tn=512, tk=ntok, name="branch_b_wgrad")
    tok = hook("grads_rest", dict(w_branch_a=g_wba, w_branch_b=g_wbbt, w_o=g_wo, w_gate_up=g_wgut, w_down=g_wd))

    sinks_exp = jnp.repeat(sinks.reshape(1, A_Q_HEADS), HEAD_DIM, axis=1) + scalar(tok)
    doa, dd_a, acc_s = _branch_a_dgrad_delta(dyaf, wba, flat(oa), flat(lse_a), sinks_exp, seq)
    doa, dd_a = unflat(doa), unflat(dd_a)
    tok = hook("delta_done", dd_a)
    dqa, dka, dva = _attn_bwd(qa, kva, kva, cos, sin, doa, lse_a, dd_a, name="attn_a_bwd", token=tok, **a_kw)
    merged_bwd = [unflat(t) for t in _branch_b_dgrad_merge_bwd(dybf, wbbt, [flat(t) for t in os_], [flat(t) for t in ls_])]
    dqs, dks, dvs = [], [], []
    for g in range(len(B_PATTERNS)):
        dq_g, dk_g, dv_g = _attn_bwd(qkvb, qkvb, qkvb, cos, sin, merged_bwd[g], ls_[g], merged_bwd[3 + g],
                                     name=f"attn_b{g}_bwd", **b_kws[g])
        dqs.append(dq_g)
        dks.append(dk_g)
        dvs.append(dv_g)
    dproj = jnp.concatenate([t.astype(BF16) for t in [dqa, dka, dva] + dqs + dks + dvs] + [unflat(dgaf), unflat(dgbf)], axis=-1)
    dprojf = flat(dproj)
    g_wint = _matmul(dprojf, u1f, mode="tn", out_dtype=BF16, tm=256, tn=1024, tk=ntok, name="w_in_wgrad")
    tok = hook("grads_w_in", dict(w_in=g_wint))
    grad_x, acc0 = _w_in_dgrad_grad_x(dprojf, wint, dxaf, xf, mod, seq, tok)
    grad_x = unflat(grad_x)
    tok = hook("dgrad_done", grad_x)

    loss_part = jnp.sum(acc2[:, 3, 0])
    dmod = jnp.stack([acc0[:, 1], acc0[:, 0], acc1[:, 2], acc1[:, 4], acc1[:, 3], acc2[:, 2]], axis=1)
    small = jnp.stack([acc1[:, 0].sum(0), acc1[:, 1].sum(0), acc2[:, 0].sum(0), acc2[:, 1].sum(0), acc_s[:, 0].sum(0)])
    small = small + scalar(tok)
    return loss_part, grad_x, dmod, small


CHIP_FLIPS = (2, 4, 6)


def _my_place():
    return lax.axis_index("x"), lax.axis_index("y"), lax.axis_index("c")


def _flip(place, k):
    px, py, pc = place
    return (1 - px if k & 4 else px, 1 - py if k & 2 else py, 1 - pc if k & 1 else pc)


def _index(place):
    return 4 * place[0] + 2 * place[1] + place[2]


def _gather_small(v, name):
    rows, cols = v.shape

    def body(v_ref, out_ref, send_sems, recv_sems):
        me = _my_place()
        out_ref[_index(me)] = v_ref[...]
        copies = []
        for k in range(1, N_DEV):
            copies.append(pltpu.make_async_remote_copy(
                src_ref=v_ref, dst_ref=out_ref.at[_index(me)], send_sem=send_sems.at[k - 1], recv_sem=recv_sems.at[k - 1],
                device_id=_flip(me, k), device_id_type=MESH))
        for cp in copies:
            cp.start()
        for k in range(1, N_DEV):
            pltpu.make_async_remote_copy(
                src_ref=v_ref, dst_ref=out_ref.at[_index(_flip(me, k))], send_sem=send_sems.at[k - 1],
                recv_sem=recv_sems.at[k - 1], device_id=_flip(me, k), device_id_type=MESH).wait_recv()
        for cp in copies:
            cp.wait_send()

    return pl.pallas_call(
        body,
        name=name,
        out_shape=jax.ShapeDtypeStruct((N_DEV, rows, cols), v.dtype),
        in_specs=[pl.BlockSpec(memory_space=pltpu.VMEM)],
        out_specs=pl.BlockSpec(memory_space=pltpu.VMEM),
        scratch_shapes=[pltpu.SemaphoreType.DMA((N_DEV - 1,)), pltpu.SemaphoreType.DMA((N_DEV - 1,))],
        compiler_params=pltpu.CompilerParams(vmem_limit_bytes=VMEM_LIMIT_BYTES),
    )(v)


_HBM = pl.BlockSpec(memory_space=pltpu.HBM)
_SEM = pl.BlockSpec(memory_space=pltpu.SEMAPHORE)
_EFFECT = pltpu.SideEffectType.DATAFLOW_SIDE_EFFECTING


def _remote(src, dst, send_sems, recv_sems, j, to):
    return pltpu.make_async_remote_copy(src_ref=src, dst_ref=dst, send_sem=send_sems.at[j], recv_sem=recv_sems.at[j],
                                        device_id=to, device_id_type=MESH)


def _copies_start(name, bufs, make_copies, nsem):
    nbuf = len(bufs)

    def body(*refs):
        for cp in make_copies(refs[:nbuf], refs[nbuf], refs[nbuf + 1]):
            cp.start()
        refs[-1][...] = jnp.zeros_like(refs[-1])

    sems = pltpu.SemaphoreType.DMA((nsem,))
    res = pl.pallas_call(
        body, name=name,
        out_shape=(sems, sems, *[pltpu.HBM(v.shape, v.dtype) for v in bufs], jax.ShapeDtypeStruct((8, LANES), F32)),
        in_specs=(_HBM,) * nbuf, out_specs=(_SEM, _SEM) + (_HBM,) * nbuf + (pl.BlockSpec(memory_space=pltpu.VMEM),),
        input_output_aliases={i: 2 + i for i in range(nbuf)},
        compiler_params=pltpu.CompilerParams(has_side_effects=_EFFECT),
    )(*[pltpu.with_memory_space_constraint(v, pltpu.HBM) for v in bufs])
    return res[0], res[1], list(res[2:2 + nbuf]), res[-1]


def _copies_wait(name, started, make_copies, after):
    send_sems, recv_sems, bufs, _ = started
    nbuf = len(bufs)

    def body(*refs):
        for cp in make_copies(refs[:nbuf], refs[nbuf], refs[nbuf + 1]):
            cp.wait_send()
            cp.wait_recv()

    return list(pl.pallas_call(
        body, name=name,
        out_shape=tuple(pltpu.HBM(v.shape, v.dtype) for v in bufs),
        in_specs=(_HBM,) * nbuf + (_SEM, _SEM, pl.BlockSpec(memory_space=pl.ANY)), out_specs=(_HBM,) * nbuf,
        input_output_aliases={i: i for i in range(nbuf)},
        compiler_params=pltpu.CompilerParams(has_side_effects=_EFFECT),
    )(*bufs, send_sems, recv_sems, after))


def _to_sibling_copies(refs, send_sems, recv_sems):
    src_ref, land_ref = refs
    me = _my_place()
    return [_remote(src_ref.at[q, 1 - me[2]], land_ref.at[q], send_sems, recv_sems, q, _flip(me, 1)) for q in range(4)]


def _to_chips_copies(refs, send_sems, recv_sems):
    src_ref, land_ref = refs
    me = _my_place()
    copies = []
    for j, k in enumerate(CHIP_FLIPS):
        to = _flip(me, k)
        copies.append(_remote(src_ref.at[2 * to[0] + to[1]], land_ref.at[j], send_sems, recv_sems, j, to))
    return copies


class _Gather:
    def __init__(self, name, blocks):
        self.name, self.n = name, len(blocks)
        at_me = (_index(_my_place()), 0, 0)
        lands = [lax.dynamic_update_slice(lax.empty((N_DEV,) + v.shape, v.dtype), v[None], at_me) for v in blocks]
        self.first = _copies_start(name + "_start", list(blocks) + lands, self._first_copies, 4 * self.n)
        self.token = self.first[3]

    def _first_copies(self, refs, send_sems, recv_sems):
        me = _my_place()
        return [_remote(refs[w], refs[self.n + w].at[_index(me)], send_sems, recv_sems, 4 * w + j, _flip(me, k))
                for w in range(self.n) for j, k in enumerate((1,) + CHIP_FLIPS)]

    def _pass_copies(self, refs, send_sems, recv_sems):
        me = _my_place()
        copies = []
        for w, land in enumerate(refs):
            for j, k in enumerate(CHIP_FLIPS):
                slot = land.at[_index(_flip(me, k))]
                copies.append(_remote(slot, slot, send_sems, recv_sems, 3 * w + j, _flip(me, 1)))
        return copies

    def pass_on(self, after):
        lands = _copies_wait(self.name + "_wait", self.first, self._first_copies, after)[self.n:]
        self.second = _copies_start(self.name + "_pass_start", lands, self._pass_copies, 3 * self.n)
        return self.second[3]

    def finish(self, after):
        return _copies_wait(self.name + "_pass_wait", self.second, self._pass_copies, after)


SUM_SPLIT = 2


def _sum_pairs(parts, theirs):
    nchip, _, rows, cols = parts.shape
    tile = rows // SUM_SPLIT

    def body(c_ref, a_ref, b_ref, o_ref):
        o_ref[...] = (a_ref[0].astype(F32) + b_ref[...].astype(F32)).astype(BF16)

    spec = pl.BlockSpec((1, tile, cols), lambda q, t, c_ref: (q, t, 0))
    grid_spec = pltpu.PrefetchScalarGridSpec(
        num_scalar_prefetch=1, grid=(nchip, SUM_SPLIT),
        in_specs=[pl.BlockSpec((1, 1, tile, cols), lambda q, t, c_ref: (q, c_ref[0], t, 0)), spec], out_specs=spec)
    return pl.pallas_call(body, name="grad_sum_sibling", grid_spec=grid_spec,
                          out_shape=jax.ShapeDtypeStruct((nchip, rows, cols), BF16),
                          compiler_params=_params("parallel", "parallel"))(lax.axis_index("c").reshape(1), parts, theirs)


def _sum_final(chip_sum, got):
    _, rows, cols = chip_sum.shape
    tile = rows // SUM_SPLIT

    def body(q_ref, a_ref, g_ref, o_ref):
        o_ref[...] = ((a_ref[0].astype(F32) + g_ref[0].astype(F32)) + g_ref[1].astype(F32)) + g_ref[2].astype(F32)

    grid_spec = pltpu.PrefetchScalarGridSpec(
        num_scalar_prefetch=1, grid=(SUM_SPLIT,),
        in_specs=[pl.BlockSpec((1, tile, cols), lambda t, q_ref: (q_ref[0], t, 0)),
                  pl.BlockSpec((3, tile, cols), lambda t, q_ref: (0, t, 0))],
        out_specs=pl.BlockSpec((tile, cols), lambda t, q_ref: (t, 0)))
    my_chip = (2 * lax.axis_index("x") + lax.axis_index("y")).reshape(1)
    return pl.pallas_call(body, name="grad_sum_chips", grid_spec=grid_spec, out_shape=jax.ShapeDtypeStruct((rows, cols), F32),
                          compiler_params=_params("parallel"))(my_chip, chip_sum, got)


class _ReduceScatter:
    def __init__(self, name, slabs):
        self.name, self.rows = name, slabs.shape[1]
        parts = slabs.reshape(4, 2, self.rows, D_MODEL)
        self.first = _copies_start(name + "_sibling_start", [parts, lax.empty((4, self.rows, D_MODEL), slabs.dtype)],
                                   _to_sibling_copies, 4)
        self.token = self.first[3]

    def between_chips(self, after):
        parts, theirs = _copies_wait(self.name + "_sibling_wait", self.first, _to_sibling_copies, after)
        chip_sum = _sum_pairs(parts, theirs)
        self.second = _copies_start(self.name + "_chips_start", [chip_sum, lax.empty((3, self.rows, D_MODEL), chip_sum.dtype)],
                                    _to_chips_copies, 3)
        return self.second[3]

    def finish(self, after):
        chip_sum, got = _copies_wait(self.name + "_chips_wait", self.second, _to_chips_copies, after)
        return _sum_final(chip_sum, got)


def _ada_fwd(c_all, w, b):
    nb, _ = c_all.shape
    ncol = w.shape[1]

    def body(c_ref, w_ref, b_ref, o_ref):
        c = c_ref[...]
        act = (c * _sigmoid(c)).astype(BF16)
        o_ref[...] = jnp.dot(act, w_ref[...].astype(BF16), preferred_element_type=F32) + b_ref[...]

    return pl.pallas_call(body, name="ada_fwd", out_shape=jax.ShapeDtypeStruct((nb, ncol), F32),
                          compiler_params=pltpu.CompilerParams(vmem_limit_bytes=VMEM_LIMIT_BYTES))(c_all, w, b)


def _ada_wgrad(c_all_t, dmod_cols):
    d, nb = c_all_t.shape
    ncol = dmod_cols.shape[1]

    def body(ct_ref, dm_ref, o_ref):
        ct = ct_ref[...]
        act = (ct * _sigmoid(ct)).astype(BF16).astype(F32)
        dm = dm_ref[...].astype(BF16).astype(F32)
        acc = act[:, 0:1] * dm[0:1, :]
        for i in range(1, nb):
            acc = acc + act[:, i:i + 1] * dm[i:i + 1, :]
        o_ref[...] = acc

    return pl.pallas_call(body, name="ada_wgrad", out_shape=jax.ShapeDtypeStruct((d, ncol), F32),
                          compiler_params=pltpu.CompilerParams(vmem_limit_bytes=VMEM_LIMIT_BYTES))(c_all_t, dmod_cols)


SMALL_ROWS = 24


def _reduce_small(gathered):
    def body(g_ref, o_ref):
        acc = g_ref[0]
        for dev in range(1, N_DEV):
            acc = acc + g_ref[dev]
        o_ref[...] = acc

    return pl.pallas_call(body, name="reduce_small", out_shape=jax.ShapeDtypeStruct(gathered.shape[1:], F32))(gathered)


def _adamw(w, g, m, v, name):
    rows, cols = w.shape
    tile = rows
    for cand in (256, 128, 64, 32, 16, 8):
        if rows % cand == 0 and rows > cand:
            tile = cand
            break
    spec = pl.BlockSpec((tile, cols), lambda t: (t, 0))
    bc1 = 1.0 - ADAM_B1 ** ADAM_STEP
    bc2 = 1.0 - ADAM_B2 ** ADAM_STEP

    def body(w_ref, g_ref, m_ref, v_ref, d_ref, nm_ref, nv_ref):
        g_ = g_ref[...]
        nm = ADAM_B1 * m_ref[...] + (1.0 - ADAM_B1) * g_
        nv = ADAM_B2 * v_ref[...] + (1.0 - ADAM_B2) * (g_ * g_)
        d_ref[...] = -ADAM_LR * ((nm / bc1) / (jnp.sqrt(nv / bc2) + ADAM_EPS) + ADAM_WD * w_ref[...])
        nm_ref[...] = nm
        nv_ref[...] = nv

    shp = jax.ShapeDtypeStruct((rows, cols), F32)
    return pl.pallas_call(body, name=name, grid=(rows // tile,), in_specs=[spec] * 4, out_specs=[spec] * 3, out_shape=[shp] * 3,
                          compiler_params=_params("parallel"))(w, g, m, v)


_WEIGHTS = ("w_ada", "b_ada", "w_in", "sinks", "w_branch_a", "w_branch_b", "w_o", "ln1_g", "ln1_b", "w_gate_up", "w_down",
            "ln2_g", "ln2_b")
_TRANSPOSED = ("w_in", "w_branch_b", "w_gate_up")


def _pack_shard(name, w):
    w = w.astype(BF16)
    if name in _TRANSPOSED:
        w = w.T
    return w.reshape(-1, D_MODEL)


def _unpack_full(name, slab):
    if name == "w_branch_b":
        return slab.reshape(N_DEV * 128, 512)
    return slab.reshape(-1, D_MODEL)


def _unpack_group(group, gathered):
    return {n: _unpack_full(n, slab) for (n, _), slab in zip(group, gathered)}


def _unpack_grads(group, g_packed):
    g_w, off = {}, 0
    for n, r in group:
        part = g_packed[off:off + r]
        off += r
        g_w[n] = part.reshape(128, 512) if n == "w_branch_b" else part
    return g_w


def kernel(x, c, positions, w_ada, b_ada, w_in, sinks, w_branch_a, w_branch_b, w_o, ln1_g, ln1_b, w_gate_up, w_down, ln2_g, ln2_b, loss_target, m_w_ada, m_b_ada, m_w_in, m_sinks, m_w_branch_a, m_w_branch_b, m_w_o, m_ln1_g, m_ln1_b, m_w_gate_up, m_w_down, m_ln2_g, m_ln2_b, v_w_ada, v_b_ada, v_w_in, v_sinks, v_w_branch_a, v_w_branch_b, v_w_o, v_ln1_g, v_ln1_b, v_w_gate_up, v_w_down, v_ln2_g, v_ln2_b):
    weights = dict(w_ada=w_ada, b_ada=b_ada, w_in=w_in, sinks=sinks, w_branch_a=w_branch_a, w_branch_b=w_branch_b, w_o=w_o,
                   ln1_g=ln1_g, ln1_b=ln1_b, w_gate_up=w_gate_up, w_down=w_down, ln2_g=ln2_g, ln2_b=ln2_b)
    m_in = dict(w_ada=m_w_ada, b_ada=m_b_ada, w_in=m_w_in, sinks=m_sinks, w_branch_a=m_w_branch_a, w_branch_b=m_w_branch_b,
                w_o=m_w_o, ln1_g=m_ln1_g, ln1_b=m_ln1_b, w_gate_up=m_w_gate_up, w_down=m_w_down, ln2_g=m_ln2_g, ln2_b=m_ln2_b)
    v_in = dict(w_ada=v_w_ada, b_ada=v_b_ada, w_in=v_w_in, sinks=v_sinks, w_branch_a=v_w_branch_a, w_branch_b=v_w_branch_b,
                w_o=v_w_o, ln1_g=v_ln1_g, ln1_b=v_ln1_b, w_gate_up=v_w_gate_up, w_down=v_w_down, ln2_g=v_ln2_g, ln2_b=v_ln2_b)
    bsz = x.shape[0]
    me = _index(_my_place())
    ada_cols = w_ada.shape[2]
    outs = {}

    def adamw(n, g):
        w2, m2, v2 = (t[n][0] if t[n].ndim == 3 else t[n] for t in (weights, m_in, v_in))
        shape = weights[n].shape
        if n in _TRANSPOSED:
            dlt, nm, nv = _adamw(w2.T, g, m2.T, v2.T, "adamw_" + n)
            outs[n] = tuple(t.T.reshape(shape) for t in (g, dlt, nm, nv))
        else:
            dlt, nm, nv = _adamw(w2, g, m2, v2, "adamw_" + n)
            outs[n] = tuple(t.reshape(shape) for t in (g, dlt, nm, nv))
        return nv

    packed_in = [_pack_shard(n, weights[n][0]) for n, _ in GROUP_IN]
    packed_rest = [_pack_shard(n, weights[n][0]) for n, _ in GROUP_REST]
    c_all = _gather_small(jnp.pad(c, ((0, 8 - bsz), (0, 0))), "gather_c")[:, :bsz].reshape(N_DEV * bsz, D_MODEL)
    gather_in = _Gather("gather_w_in", lax.optimization_barrier((packed_in, c_all))[0])
    b_cols = lax.dynamic_slice_in_dim(b_ada, me * ada_cols, ada_cols, axis=1)
    mod_cols = _ada_fwd(c_all, w_ada[0], b_cols + gather_in.token[0, 0])
    mod_all = _gather_small(mod_cols, "gather_mod").transpose(1, 0, 2).reshape(N_DEV * bsz, 6, D_MODEL)
    gather_rest = _Gather("gather_rest", lax.optimization_barrier((packed_rest, mod_all))[0])
    mod = jnp.pad(lax.dynamic_slice_in_dim(mod_all, me * bsz, bsz, axis=0), ((0, 0), (0, 2), (0, 0)))
    mod = mod + gather_rest.token[0, 0]
    mod = mod + gather_in.pass_on(mod)[0, 0]

    scatters = {}

    def get_w_in(after):
        return _unpack_group(GROUP_IN, gather_in.finish(after))["w_in"]

    def get_rest(after):
        return _unpack_group(GROUP_REST, gather_rest.finish(after))

    def pack_grads(group, grads):
        return jnp.concatenate([grads[n].reshape(N_DEV, r, D_MODEL) for n, r in group], axis=1)

    def hook(point, value):
        if point == "projected":
            return gather_rest.pass_on(value)
        if point == "grads_rest":
            scatters["rest"] = _ReduceScatter("scatter_rest", pack_grads(GROUP_REST, value))
            return scatters["rest"].token
        if point == "delta_done":
            return scatters["rest"].between_chips(value)
        if point == "grads_w_in":
            scatters["in"] = _ReduceScatter("scatter_w_in", pack_grads(GROUP_IN, value))
            return scatters["in"].token
        if point == "dgrad_done":
            return None
        raise ValueError(point)

    loss_part, grad_x, dmod, small = _layer_step(x, mod, positions, sinks[0], ln1_g, ln1_b, ln2_g, ln2_b, loss_target,
                                                 get_w_in, get_rest, hook)

    rows = jnp.concatenate([dmod.reshape(bsz * 6, D_MODEL), small, jnp.full((1, D_MODEL), loss_part, F32),
                            jnp.zeros((SMALL_ROWS - bsz * 6 - 6, D_MODEL), F32)], axis=0)
    small_all = _gather_small(rows, "gather_small")
    small_all = small_all + scatters["in"].between_chips(small_all)[0, 0]
    sums = _reduce_small(small_all)
    loss = sums[bsz * 6 + 5, 0]
    dmod_all = small_all[:, :bsz * 6].reshape(N_DEV * bsz, 6 * D_MODEL)
    adamw("b_ada", functools.reduce(jnp.add, [sums[6 * i:6 * i + 6] for i in range(bsz)]).reshape(1, 6 * D_MODEL))
    for i, n in enumerate(("ln1_g", "ln1_b", "ln2_g", "ln2_b")):
        adamw(n, sums[12 + i][None])
    adamw("sinks", sums[16][::HEAD_DIM][None])
    dmod_cols = lax.dynamic_slice_in_dim(dmod_all, me * ada_cols, ada_cols, axis=1)
    last = adamw("w_ada", _ada_wgrad(c_all.T, dmod_cols))
    for n, g in _unpack_grads(GROUP_REST, scatters["rest"].finish(last)).items():
        last = adamw(n, g)
    for n, g in _unpack_grads(GROUP_IN, scatters["in"].finish(last)).items():
        adamw(n, g)

    return (loss, grad_x, *[outs[n][0] for n in _WEIGHTS], *[outs[n][1] for n in _WEIGHTS], *[outs[n][2] for n in _WEIGHTS],
            *[outs[n][3] for n in _WEIGHTS])
```

```python
import functools

import jax
import jax.numpy as jnp
from jax import lax
from jax.experimental import pallas as pl
from jax.experimental.pallas import tpu as pltpu

F32 = jnp.float32
BF16 = jnp.bfloat16

D_MODEL = 1024
HEAD_DIM = 64
A_Q_HEADS = 16
A_WINDOW = 128
B_PATTERNS = ((128, 1), (512, 4), (2048, 16))
B_HEADS_PER_GROUP = 8
D_FF = 2816
QBLOCK = 128
ROPE_THETA = 10000.0
LN_EPS = 1e-5
DEEPNORM_ALPHA = 2.0 ** 0.25
NEG_INF = -1e30
ADAM_LR, ADAM_B1, ADAM_B2, ADAM_EPS, ADAM_WD, ADAM_STEP = 0.001, 0.9, 0.999, 1e-08, 0.01, 10

N_DEV = 8
MESH_AXES = ("x", "y", "c")
LANES = 128
VMEM_LIMIT_BYTES = 56 * 1024 * 1024
MESH = pl.DeviceIdType.MESH

OFF_QA, OFF_KVA, OFF_QKVB, OFF_GAB = 0, 1024, 1280, 5888
GROUP_IN = (("w_in", 992),)
GROUP_REST = (("w_branch_a", 128), ("w_branch_b", 64), ("w_o", 128), ("w_gate_up", 704), ("w_down", 352))


def _params(*sem):
    return pltpu.CompilerParams(dimension_semantics=sem, vmem_limit_bytes=VMEM_LIMIT_BYTES)


def _sigmoid(x):
    return 1.0 / (1.0 + jnp.exp(-x))


_DIMS = {"nn": (((1,), (0,)), ((), ())), "nt": (((1,), (1,)), ((), ())), "tn": (((0,), (0,)), ((), ()))}


def _matmul(a, b, *, mode, tm, tn, tk, name, out_dtype=None, n=None, b_off=0, token=None, ins=(), outs=None, epilogue=None):
    if mode == "nn":
        (m, k), nn_ = a.shape, b.shape[1]
    elif mode == "nt":
        (m, k), nn_ = a.shape, (b.shape[0] if n is None else n)
    else:
        (k, m), nn_ = a.shape, b.shape[1]
    assert m % tm == 0 and nn_ % tn == 0 and k % tk == 0 and b_off % tn == 0, (name, m, nn_, k)
    nk = k // tk
    joff = b_off // tn
    if mode == "nn":
        a_spec = pl.BlockSpec((tm, tk), lambda i, j, kk: (i, kk))
        b_spec = pl.BlockSpec((tk, tn), lambda i, j, kk: (kk, j))
    elif mode == "nt":
        a_spec = pl.BlockSpec((tm, tk), lambda i, j, kk: (i, kk))
        b_spec = pl.BlockSpec((tn, tk), lambda i, j, kk: (j + joff, kk))
    else:
        a_spec = pl.BlockSpec((tk, tm), lambda i, j, kk: (kk, i))
        b_spec = pl.BlockSpec((tk, tn), lambda i, j, kk: (kk, j))
    dims = _DIMS[mode]
    has_token = token is not None
    plain = epilogue is None
    if plain:
        outs = [(jax.ShapeDtypeStruct((m, nn_), out_dtype), (tm, tn), lambda i, j: (i, j))]

        def epilogue(acc, i, j, in_refs, out_refs):
            out_refs[0][...] = acc.astype(out_refs[0].dtype)

    nin = len(ins)

    def body(*refs):
        a_ref, b_ref = refs[:2]
        in_refs = refs[2:2 + nin]
        out_refs = refs[2 + nin + has_token:-1]
        acc_ref = refs[-1]
        kk = pl.program_id(2)
        part = lax.dot_general(a_ref[...].astype(BF16), b_ref[...].astype(BF16), dims, preferred_element_type=F32)

        def finish(acc):
            epilogue(acc, pl.program_id(0), pl.program_id(1), in_refs, out_refs)

        if nk == 1:
            finish(part)
        else:
            @pl.when(kk == 0)
            def _():
                acc_ref[...] = part

            @pl.when(kk > 0)
            def _():
                acc_ref[...] += part

            @pl.when(kk == nk - 1)
            def _():
                finish(acc_ref[...])

    def spec(block, index):
        return pl.BlockSpec(block, lambda i, j, kk: index(i, j))

    in_specs, args = [a_spec, b_spec], [a, b]
    for arr, block, index in ins:
        in_specs.append(spec(block, index))
        args.append(arr)
    if has_token:
        in_specs.append(pl.BlockSpec(token.shape, lambda i, j, kk: (0, 0)))
        args.append(token)
    res = pl.pallas_call(
        body,
        name=name,
        grid=(m // tm, nn_ // tn, nk),
        in_specs=in_specs,
        out_specs=[spec(block, index) for _, block, index in outs],
        out_shape=[shape for shape, _, _ in outs],
        scratch_shapes=[pltpu.VMEM((tm, tn) if nk > 1 else (8, LANES), F32)],
        compiler_params=_params("arbitrary", "arbitrary", "arbitrary"),
    )(*args)
    return res[0] if plain else res


def _proj_rope(a, bt, cos, sin, *, n, b_off, rope_cols, tm, tn, name, out_dtype=F32):
    m, k = a.shape
    assert m % tm == 0 and n % tn == 0 and b_off % tn == 0 and rope_cols % tn == 0, name
    joff = b_off // tn
    nrope = rope_cols // tn

    def body(a_ref, b_ref, c_ref, s_ref, o_ref):
        acc = lax.dot_general(a_ref[...], b_ref[...], _DIMS["nt"], preferred_element_type=F32)
        j = pl.program_id(1)

        @pl.when(j < nrope)
        def _():
            o_ref[...] = _rope(acc, c_ref[...], s_ref[...]).astype(o_ref.dtype)

        @pl.when(j >= nrope)
        def _():
            o_ref[...] = acc.astype(o_ref.dtype)

    table = pl.BlockSpec((tm, LANES), lambda i, j: (i, 0))
    return pl.pallas_call(
        body,
        name=name,
        grid=(m // tm, n // tn),
        in_specs=[pl.BlockSpec((tm, k), lambda i, j: (i, 0)), pl.BlockSpec((tn, k), lambda i, j: (j + joff, 0)), table, table],
        out_specs=pl.BlockSpec((tm, tn), lambda i, j: (i, j)),
        out_shape=jax.ShapeDtypeStruct((m, n), out_dtype),
        compiler_params=_params("parallel", "parallel"),
    )(a, bt, cos, sin)


ROW_TILE = 256


def _rows(width, col=0):
    return pl.BlockSpec((1, ROW_TILE, width), lambda b, t: (b, t, col))


def _per_batch(nrows, width):
    return pl.BlockSpec((1, nrows, width), lambda b, t: (b, 0, 0))


def _whole(shape):
    return pl.BlockSpec(shape, lambda b, t: (0,) * len(shape))


def _row_call(body, name, bsz, seq, in_specs, out_specs, out_shape, accumulates=False):
    return pl.pallas_call(
        body,
        name=name,
        grid=(bsz, seq // ROW_TILE),
        in_specs=in_specs,
        out_specs=out_specs,
        out_shape=out_shape,
        compiler_params=_params("parallel", "arbitrary" if accumulates else "parallel"),
    )


def _acc_rows(acc_ref, first, rows):
    @pl.when(first)
    def _():
        acc_ref[...] = jnp.zeros_like(acc_ref)

    for r, val in enumerate(rows):
        acc_ref[0, r:r + 1, :] += val


def _colsum(v):
    return jnp.sum(v, axis=0, keepdims=True)


def _ln_stats(z):
    mu = jnp.mean(z, axis=-1, keepdims=True)
    zc = z - mu
    var = jnp.mean(zc * zc, axis=-1, keepdims=True)
    rstd = lax.rsqrt(var + LN_EPS)
    return zc * rstd, rstd


def _ln_bwd(dxhat, xhat, rstd):
    m1 = jnp.mean(dxhat, axis=-1, keepdims=True)
    m2 = jnp.mean(dxhat * xhat, axis=-1, keepdims=True)
    return rstd * (dxhat - m1 - xhat * m2)


def _modulate_in(x, mod):
    bsz, seq, d = x.shape

    def body(x_ref, mod_ref, u_ref):
        u_ref[0] = (x_ref[0] * (1.0 + mod_ref[0, 1:2, :]) + mod_ref[0, 0:1, :]).astype(BF16)

    return _row_call(body, "modulate_in", bsz, seq, [_rows(d), _per_batch(8, d)], _rows(d),
                     jax.ShapeDtypeStruct((bsz, seq, d), BF16))(x, mod)


def _gate_merge(gab, ya, yb):
    bsz, seq, d = ya.shape

    def body(ga_ref, gb_ref, ya_ref, yb_ref, o_ref):
        ga, gb, ya_, yb_ = (r[0].astype(F32) for r in (ga_ref, gb_ref, ya_ref, yb_ref))
        o_ref[0] = (_sigmoid(ga) * ya_ + _sigmoid(gb) * yb_).astype(BF16)

    return _row_call(body, "gate_merge", bsz, seq, [_rows(d, 0), _rows(d, 1), _rows(d), _rows(d)], _rows(d),
                     jax.ShapeDtypeStruct((bsz, seq, d), BF16))(gab, gab, ya, yb)


EP_TILE = 512


def _ep_specs(seq, d):
    tiles = seq // EP_TILE
    return ((EP_TILE, d), lambda i, j: (i, 0)), ((1, 8, d), lambda i, j: (i // tiles, 0, 0)), ((1, d), lambda i, j: (0, 0))


def _wo_ln1(merged, wo, x, mod, g, b, seq):
    ntok, d = x.shape
    row, per_b, whole = _ep_specs(seq, d)

    def epilogue(y, i, j, ins, outs):
        x_ref, mod_ref, g_ref, b_ref = ins
        y_ref, h_ref, u_ref = outs
        z = DEEPNORM_ALPHA * x_ref[...] + (1.0 + mod_ref[0, 2:3, :]) * y
        xhat, _ = _ln_stats(z)
        h = xhat * g_ref[...] + b_ref[...]
        y_ref[...] = y
        h_ref[...] = h
        u_ref[...] = (h * (1.0 + mod_ref[0, 4:5, :]) + mod_ref[0, 3:4, :]).astype(BF16)

    f32, bf16 = jax.ShapeDtypeStruct((ntok, d), F32), jax.ShapeDtypeStruct((ntok, d), BF16)
    return _matmul(merged, wo, mode="nn", tm=EP_TILE, tn=d, tk=d, name="w_o_ln1",
                   ins=[(x,) + row, (mod,) + per_b, (g,) + whole, (b,) + whole],
                   outs=[(f32,) + row, (f32,) + row, (bf16,) + row], epilogue=epilogue)


FF_HALF = D_FF // 2


def _interleave_gate_up(w):
    return w.reshape(2, 2, FF_HALF, w.shape[1]).transpose(1, 0, 2, 3).reshape(w.shape)


def _gate_up_silu(u2, wgut_i):
    ntok = u2.shape[0]

    def epilogue(h, i, j, ins, outs):
        h_ref, a_ref = outs
        hg, hu = h[:, :FF_HALF], h[:, FF_HALF:]
        h_ref[...] = h.astype(BF16)
        a_ref[...] = (hg * _sigmoid(hg) * hu).astype(BF16)

    return _matmul(u2, wgut_i, mode="nt", tm=EP_TILE, tn=2 * FF_HALF, tk=u2.shape[1], name="gate_up_silu",
                   outs=[(jax.ShapeDtypeStruct((ntok, 2 * D_FF), BF16), (EP_TILE, 2 * FF_HALF), lambda i, j: (i, j)),
                         (jax.ShapeDtypeStruct((ntok, D_FF), BF16), (EP_TILE, FF_HALF), lambda i, j: (i, j))],
                   epilogue=epilogue)


def _down_dgrad_silu_bwd(dy2, wd, h_i):
    ntok = dy2.shape[0]
    wide = ((EP_TILE, 2 * FF_HALF), lambda i, j: (i, j))

    def epilogue(da, i, j, ins, outs):
        h = ins[0][...].astype(F32)
        hg, hu = h[:, :FF_HALF], h[:, FF_HALF:]
        sg = _sigmoid(hg)
        outs[0][:, :FF_HALF] = (da * hu * (sg * (1.0 + hg * (1.0 - sg)))).astype(BF16)
        outs[0][:, FF_HALF:] = (da * (hg * sg)).astype(BF16)

    return _matmul(dy2, wd, mode="nt", tm=EP_TILE, tn=FF_HALF, tk=dy2.shape[1], name="down_dgrad_silu_bwd",
                   ins=[(h_i,) + wide], outs=[(jax.ShapeDtypeStruct((ntok, 2 * D_FF), BF16),) + wide], epilogue=epilogue)[0]


def _down_ln2_loss_bwd(a, wd, h1, mod, g, b, target, seq):
    ntok, d = h1.shape
    row, per_b, whole = _ep_specs(seq, d)
    tiles = seq // EP_TILE

    def epilogue(y, i, j, ins, outs):
        h_ref, mod_ref, g_ref, b_ref, t_ref = ins
        dy_ref, dh_ref, acc_ref = outs
        gate = 1.0 + mod_ref[0, 5:6, :]
        z = DEEPNORM_ALPHA * h_ref[...] + gate * y
        xhat, rstd = _ln_stats(z)
        diff = xhat * g_ref[...] + b_ref[...] - t_ref[...]
        loss = 0.5 * jnp.sum(jnp.sum(diff * diff, axis=-1, keepdims=True) / d, axis=0, keepdims=True)
        dout = diff / d
        dz = _ln_bwd(dout * g_ref[...], xhat, rstd)
        dy_ref[...] = (gate * dz).astype(BF16)
        dh_ref[...] = DEEPNORM_ALPHA * dz
        _acc_rows(acc_ref, i % tiles == 0,
                  [_colsum(dout * xhat), _colsum(dout), _colsum(dz * y), jnp.broadcast_to(loss, (1, d))])

    return _matmul(a, wd, mode="nn", tm=EP_TILE, tn=d, tk=a.shape[1], name="down_ln2_loss_bwd",
                   ins=[(h1,) + row, (mod,) + per_b, (g,) + whole, (b,) + whole, (target,) + row],
                   outs=[(jax.ShapeDtypeStruct((ntok, d), BF16),) + row, (jax.ShapeDtypeStruct((ntok, d), F32),) + row,
                         (jax.ShapeDtypeStruct((ntok // seq, 8, d), F32),) + per_b], epilogue=epilogue)


def _silu_mul_bwd(da, h):
    bsz, seq, _ = h.shape

    def body(da_ref, hg_ref, hu_ref, dh_ref):
        hg, da_ = hg_ref[0].astype(F32), da_ref[0].astype(F32)
        sg = _sigmoid(hg)
        dh_ref[0, :, :D_FF] = (da_ * hu_ref[0].astype(F32) * (sg * (1.0 + hg * (1.0 - sg)))).astype(BF16)
        dh_ref[0, :, D_FF:] = (da_ * (hg * sg)).astype(BF16)

    return _row_call(body, "silu_mul_bwd", bsz, seq, [_rows(D_FF), _rows(D_FF, 0), _rows(D_FF, 1)], _rows(2 * D_FF),
                     jax.ShapeDtypeStruct((bsz, seq, 2 * D_FF), BF16))(da, h, h)


def _gate_up_dgrad_ln1_bwd(dh, wgut, dh1a, x, y1, mod, g, b, seq):
    ntok, d = x.shape
    row, per_b, whole = _ep_specs(seq, d)
    tiles = seq // EP_TILE

    def epilogue(du, i, j, ins, outs):
        dh_ref, x_ref, y_ref, mod_ref, g_ref, b_ref = ins
        dy_ref, dx_ref, acc_ref = outs
        y = y_ref[...]
        gate = 1.0 + mod_ref[0, 2:3, :]
        z = DEEPNORM_ALPHA * x_ref[...] + gate * y
        xhat, rstd = _ln_stats(z)
        h1 = xhat * g_ref[...] + b_ref[...]
        dh1 = dh_ref[...] + du * (1.0 + mod_ref[0, 4:5, :])
        dz = _ln_bwd(dh1 * g_ref[...], xhat, rstd)
        dy_ref[...] = (gate * dz).astype(BF16)
        dx_ref[...] = DEEPNORM_ALPHA * dz
        _acc_rows(acc_ref, i % tiles == 0,
                  [_colsum(dh1 * xhat), _colsum(dh1), _colsum(dz * y), _colsum(du * h1), _colsum(du)])

    return _matmul(dh, wgut, mode="nn", tm=EP_TILE, tn=d, tk=D_FF, name="gate_up_dgrad_ln1_bwd",
                   ins=[(dh1a,) + row, (x,) + row, (y1,) + row, (mod,) + per_b, (g,) + whole, (b,) + whole],
                   outs=[(jax.ShapeDtypeStruct((ntok, d), BF16),) + row, (jax.ShapeDtypeStruct((ntok, d), F32),) + row,
                         (jax.ShapeDtypeStruct((ntok // seq, 8, d), F32),) + per_b], epilogue=epilogue)


def _wo_dgrad_gate_bwd(dy1, wo, gab, ya, yb):
    ntok, d = ya.shape
    tm, tn = 1024, 512
    tile = ((tm, tn), lambda i, j: (i, j))
    tile_b = ((tm, tn), lambda i, j: (i, j + d // tn))

    def epilogue(dm_, i, j, ins, outs):
        ga_ref, gb_ref, ya_ref, yb_ref = ins
        dya_ref, dyb_ref, dga_ref, dgb_ref = outs
        sa, sb = _sigmoid(ga_ref[...].astype(F32)), _sigmoid(gb_ref[...].astype(F32))
        dya_ref[...] = (dm_ * sa).astype(BF16)
        dyb_ref[...] = (dm_ * sb).astype(BF16)
        dga_ref[...] = (dm_ * ya_ref[...].astype(F32) * sa * (1.0 - sa)).astype(BF16)
        dgb_ref[...] = (dm_ * yb_ref[...].astype(F32) * sb * (1.0 - sb)).astype(BF16)

    shp = jax.ShapeDtypeStruct((ntok, d), BF16)
    return _matmul(dy1, wo, mode="nt", tm=tm, tn=tn, tk=d, name="w_o_dgrad_gate_bwd",
                   ins=[(gab,) + tile, (gab,) + tile_b, (ya,) + tile, (yb,) + tile],
                   outs=[(shp,) + tile] * 4, epilogue=epilogue)


def _w_in_dgrad_grad_x(dproj, wint, dxa, x, mod, seq, token):
    ntok, d = x.shape
    row, per_b, _ = _ep_specs(seq, d)
    tiles = seq // EP_TILE

    def epilogue(du, i, j, ins, outs):
        dxa_ref, x_ref, mod_ref = ins
        gx_ref, acc_ref = outs
        gx_ref[...] = dxa_ref[...] + du * (1.0 + mod_ref[0, 1:2, :])
        _acc_rows(acc_ref, i % tiles == 0, [_colsum(du * x_ref[...]), _colsum(du)])

    return _matmul(dproj, wint, mode="nn", tm=EP_TILE, tn=d, tk=wint.shape[0] // 2, name="w_in_dgrad_grad_x", token=token,
                   ins=[(dxa,) + row, (x,) + row, (mod,) + per_b],
                   outs=[(jax.ShapeDtypeStruct((ntok, d), F32),) + row, (jax.ShapeDtypeStruct((ntok // seq, 8, d), F32),) + per_b],
                   epilogue=epilogue)


def _branch_b_gate_merge(ob, wbbt, gab, ya):
    ntok, d = ya.shape
    tm, tn = 1024, 512
    tile = ((tm, tn), lambda i, j: (i, j))
    tile_b = ((tm, tn), lambda i, j: (i, j + d // tn))

    def epilogue(yb, i, j, ins, outs):
        ga_ref, gb_ref, ya_ref = ins
        yb_ref, merged_ref = outs
        yb_ref[...] = yb.astype(BF16)
        merged_ref[...] = (_sigmoid(ga_ref[...].astype(F32)) * ya_ref[...].astype(F32)
                           + _sigmoid(gb_ref[...].astype(F32)) * yb).astype(BF16)

    shp = jax.ShapeDtypeStruct((ntok, d), BF16)
    return _matmul(ob, wbbt, mode="nt", tm=tm, tn=tn, tk=ob.shape[1], name="branch_b_gate_merge",
                   ins=[(gab,) + tile, (gab,) + tile_b, (ya,) + tile], outs=[(shp,) + tile] * 2, epilogue=epilogue)


def _segsum64(v):
    rows, width = v.shape
    ri = lax.broadcasted_iota(jnp.int32, (LANES, LANES), 0) // HEAD_DIM
    ci = lax.broadcasted_iota(jnp.int32, (LANES, LANES), 1) // HEAD_DIM
    ones = jnp.where(ri == ci, 1.0, 0.0).astype(BF16)
    out = []
    for c in range(width // LANES):
        part = v[:, c * LANES:(c + 1) * LANES]
        hi = part.astype(BF16)
        lo = (part - hi.astype(F32)).astype(BF16)
        out.append(jnp.dot(hi, ones, preferred_element_type=F32) + jnp.dot(lo, ones, preferred_element_type=F32))
    return jnp.concatenate(out, axis=1) if len(out) > 1 else out[0]


def _merge_b(os_, ls_):
    bsz, seq, w = os_[0].shape

    def body(o0, o1, o2, l0, l1, l2, ob_ref):
        ls = [l0[0], l1[0], l2[0]]
        mx = jnp.maximum(jnp.maximum(ls[0], ls[1]), ls[2])
        es = [jnp.exp(l - mx) for l in ls]
        den = es[0] + es[1] + es[2]
        ob_ref[0] = ((es[0] / den) * o0[0] + (es[1] / den) * o1[0] + (es[2] / den) * o2[0]).astype(BF16)

    return _row_call(body, "merge_b", bsz, seq, [_rows(w)] * 6, _rows(w),
                     jax.ShapeDtypeStruct((bsz, seq, w), BF16))(*os_, *ls_)


def _branch_b_dgrad_merge_bwd(dyb, wbbt, os_, ls_):
    ntok, w = os_[0].shape
    row = ((EP_TILE, w), lambda i, j: (i, 0))

    def epilogue(dob_, i, j, ins, outs):
        os_r, ls_r = ins[:3], ins[3:]
        do_r, dd_r = outs[:3], outs[3:]
        ls = [l[...] for l in ls_r]
        mx = jnp.maximum(jnp.maximum(ls[0], ls[1]), ls[2])
        es = [jnp.exp(l - mx) for l in ls]
        den = es[0] + es[1] + es[2]
        ws = [e / den for e in es]
        dws = [_segsum64(dob_ * o[...]) for o in os_r]
        mean = ws[0] * dws[0] + ws[1] * dws[1] + ws[2] * dws[2]
        for wg, do_ref, dd_ref in zip(ws, do_r, dd_r):
            do_ref[...] = wg * dob_
            dd_ref[...] = -wg * mean

    shp = jax.ShapeDtypeStruct((ntok, w), F32)
    return _matmul(dyb, wbbt, mode="nn", tm=EP_TILE, tn=w, tk=dyb.shape[1], name="branch_b_dgrad_merge_bwd",
                   ins=[(v,) + row for v in list(os_) + list(ls_)], outs=[(shp,) + row] * 6, epilogue=epilogue)


def _branch_a_dgrad_delta(dya, wba, oa, lse_a, sinks_exp, seq):
    ntok, w = oa.shape
    row, per_b, whole = _ep_specs(seq, w)
    tiles = seq // EP_TILE

    def epilogue(do_, i, j, ins, outs):
        o_ref, l_ref, s_ref = ins
        do_ref, dd_ref, acc_ref = outs
        dd = -_segsum64(do_ * o_ref[...])
        do_ref[...] = do_
        dd_ref[...] = dd
        _acc_rows(acc_ref, i % tiles == 0, [_colsum(dd * jnp.exp(s_ref[...] - l_ref[...]))])

    shp = jax.ShapeDtypeStruct((ntok, w), F32)
    return _matmul(dya, wba, mode="nt", tm=EP_TILE, tn=w, tk=dya.shape[1], name="branch_a_dgrad_delta",
                   ins=[(oa,) + row, (lse_a,) + row, (sinks_exp,) + whole],
                   outs=[(shp,) + row, (shp,) + row, (jax.ShapeDtypeStruct((ntok // seq, 8, w), F32),) + per_b],
                   epilogue=epilogue)


def _swap_halves(v):
    src = lax.broadcasted_iota(jnp.int32, (LANES, LANES), 0)
    dst = lax.broadcasted_iota(jnp.int32, (LANES, LANES), 1)
    partner = jnp.where((dst % HEAD_DIM) < HEAD_DIM // 2, dst + HEAD_DIM // 2, dst - HEAD_DIM // 2)
    perm = jnp.where(src == partner, 1.0, 0.0).astype(BF16)
    hi = v.astype(BF16)
    lo = (v - hi.astype(F32)).astype(BF16)
    return jnp.dot(hi, perm, preferred_element_type=F32) + jnp.dot(lo, perm, preferred_element_type=F32)


def _swap_halves_roll(v):
    lane = lax.broadcasted_iota(jnp.int32, v.shape, 1)
    return jnp.where((lane % HEAD_DIM) < HEAD_DIM // 2, pltpu.roll(v, LANES - HEAD_DIM // 2, 1),
                     pltpu.roll(v, HEAD_DIM // 2, 1))


def _rope(v, cos, sin, sign=1.0, mxu=True):
    swap = _swap_halves if mxu else _swap_halves_roll
    out = []
    for c in range(v.shape[1] // LANES):
        part = v[:, c * LANES:(c + 1) * LANES]
        out.append(part * cos + sign * (swap(part) * sin))
    return jnp.concatenate(out, axis=1) if len(out) > 1 else out[0]


def _half_mask(shape, half):
    lane = lax.broadcasted_iota(jnp.int32, shape, len(shape) - 1) % LANES
    return (lane < HEAD_DIM) if half == 0 else (lane >= HEAD_DIM)


def _dup_half(v, half):
    return jnp.where(_half_mask(v.shape, half), v, pltpu.roll(v, HEAD_DIM, 1))


def _fold_halves(v):
    return v + pltpu.roll(v, HEAD_DIM, 1)


def _pick_halves(lo_rows, hi_rows):
    return jnp.where(_half_mask(lo_rows.shape, 0), lo_rows, hi_rows)


def _stack_masked(v, pairs):
    parts = []
    for c in pairs:
        pair = v[:, c * LANES:(c + 1) * LANES]
        parts += [jnp.where(_half_mask(pair.shape, half), pair, 0.0) for half in (0, 1)]
    return jnp.concatenate(parts, axis=0)


def _stack_pair_cols(v, pairs):
    return jnp.concatenate([v[:, c * LANES + half * HEAD_DIM:c * LANES + half * HEAD_DIM + 1] for c in pairs for half in (0, 1)],
                           axis=0)


ATTN_UNITS = 16


def _class_rows(r):
    return [pl.ds(0, QBLOCK)] if r == 1 else [pl.ds(rho, QBLOCK, stride=r) for rho in range(r)]


def _band_mask(nrows, nk, blk, n_back, has_prev):
    qi = lax.broadcasted_iota(jnp.int32, (nrows, nk), 0) % QBLOCK
    ki = lax.broadcasted_iota(jnp.int32, (nrows, nk), 1)
    if has_prev:
        dist = qi + QBLOCK - ki
        return (dist >= 0) & (dist <= n_back) & ((ki >= QBLOCK) | (blk > 0))
    dist = qi - ki
    return (dist >= 0) & (dist <= n_back)


def _attn_fwd(q_arr, k_arr, v_arr, *, name, npair, gqa, q_col, k_col, v_col, nchunk, r, n_back, sinks=None):
    bsz, seq, _ = q_arr.shape
    rr = QBLOCK * r
    nblk = seq // rr
    qw = npair * LANES
    kw = LANES if gqa else qw
    has_prev = nblk > 1
    has_sink = sinks is not None
    scale = HEAD_DIM ** -0.5

    def body(*refs):
        refs = list(refs)
        q_ref, kc_ref, vc_ref = refs[:3]
        pos = 3
        if has_prev:
            kp_ref, vp_ref = refs[pos:pos + 2]
            pos += 2
        if has_sink:
            sink_ref = refs[pos]
            pos += 1
        o_ref, lse_ref = refs[pos:pos + 2]
        blk = pl.program_id(2)
        nk = (2 if has_prev else 1) * QBLOCK
        valid = _band_mask(QBLOCK, nk, blk, n_back, has_prev)
        per = npair // 2
        classes = _class_rows(r)
        step = max(1, ATTN_UNITS // (2 * npair))
        for first in range(0, len(classes), step):
            batch = classes[first:first + step]
            units = []
            for ci, rows in enumerate(batch):
                q = q_ref[0, rows, :] * scale
                k, v = kc_ref[0, rows, :], vc_ref[0, rows, :]
                if has_prev:
                    k = jnp.concatenate([kp_ref[0, rows, :], k], axis=0)
                    v = jnp.concatenate([vp_ref[0, rows, :], v], axis=0)
                if gqa:
                    kdup = [_dup_half(k, hk).astype(BF16) for hk in range(2)]
                    vdup = [_dup_half(v, hk) for hk in range(2)]
                for c in range(npair):
                    sl = slice(c * LANES, (c + 1) * LANES)
                    qc = q[:, sl]
                    kc, vc = (kdup[c // per], vdup[c // per]) if gqa else (k[:, sl].astype(BF16), v[:, sl])
                    for half in (0, 1):
                        qm = jnp.where(_half_mask(qc.shape, half), qc, 0.0).astype(BF16)
                        vm = jnp.where(_half_mask(vc.shape, half), vc, 0.0).astype(BF16)
                        s = lax.dot_general(qm, kc, _DIMS["nt"], preferred_element_type=F32)
                        units.append(dict(ci=ci, c=c, half=half, s=s, vm=vm, sk=sink_ref[2 * c + half] if has_sink else None))
            for u in units:
                s = jnp.where(valid, u["s"], NEG_INF)
                m = jnp.max(s, axis=1, keepdims=True)
                if has_sink:
                    m = jnp.maximum(m, u["sk"])
                p = jnp.exp(s - m)
                den = jnp.sum(p, axis=1, keepdims=True)
                if has_sink:
                    den = den + jnp.exp(u["sk"] - m)
                u.update(p=p.astype(BF16), den=den, lse=m + jnp.log(den))
            for u in units:
                u["o"] = jnp.dot(u["p"], u["vm"], preferred_element_type=F32) / u["den"]
            for ci, rows in enumerate(batch):
                outs, lses = [None] * npair, [None] * npair
                for u in units:
                    if u["ci"] != ci:
                        continue
                    c, o = u["c"], u["o"]
                    lse = jnp.broadcast_to(u["lse"], o.shape)
                    outs[c] = o if u["half"] == 0 else outs[c] + o
                    lses[c] = lse if u["half"] == 0 else _pick_halves(lses[c], lse)
                o_ref[0, rows, :] = jnp.concatenate(outs, axis=1) if npair > 1 else outs[0]
                lse_ref[0, rows, :] = jnp.concatenate(lses, axis=1) if npair > 1 else lses[0]

    def cur(width, col0):
        return pl.BlockSpec((1, rr, width), lambda b, c, i: (b, i, col0 + c))

    def prev(width, col0):
        return pl.BlockSpec((1, rr, width), lambda b, c, i: (b, jnp.maximum(i - 1, 0), col0 + c))

    in_specs = [cur(qw, q_col), cur(kw, k_col), cur(kw, v_col)]
    args = [q_arr, k_arr, v_arr]
    if has_prev:
        in_specs += [prev(kw, k_col), prev(kw, v_col)]
        args += [k_arr, v_arr]
    if has_sink:
        in_specs.append(pl.BlockSpec(memory_space=pltpu.SMEM))
        args.append(sinks)
    return pl.pallas_call(
        body,
        name=name,
        grid=(bsz, nchunk, nblk),
        in_specs=in_specs,
        out_specs=[pl.BlockSpec((1, rr, qw), lambda b, c, i: (b, i, c))] * 2,
        out_shape=[jax.ShapeDtypeStruct((bsz, seq, nchunk * qw), F32)] * 2,
        compiler_params=_params("parallel", "parallel", "parallel"),
    )(*args)


def _attn_bwd(q_arr, k_arr, v_arr, cos, sin, do, lse, dd, *, name, npair, gqa, q_col, k_col, v_col, nchunk, r, n_back,
              token=None):
    bsz, seq, _ = q_arr.shape
    rr = QBLOCK * r
    nblk = seq // rr
    qw = npair * LANES
    kw = LANES if gqa else qw
    has_next = nblk > 1
    has_token = token is not None
    scale = HEAD_DIM ** -0.5

    def body(*refs):
        refs = list(refs)
        k_ref, v_ref, c_ref, s_ref = refs[:4]
        tile_refs = [refs[4:8]]
        pos = 8
        if has_next:
            tile_refs.append(refs[pos:pos + 4])
            pos += 4
        if has_token:
            pos += 1
        dq_ref, dk_ref, dv_ref = refs[pos:pos + 3]
        carry_ref = refs[pos + 3]
        blk = pl.program_id(2)
        if has_next:
            @pl.when(blk == 0)
            def _():
                carry_ref[...] = jnp.zeros_like(carry_ref)

        nrows = (npair if gqa else 1) * QBLOCK
        qi = lax.broadcasted_iota(jnp.int32, (nrows, QBLOCK), 0) % QBLOCK
        ki = lax.broadcasted_iota(jnp.int32, (nrows, QBLOCK), 1)
        valids = [qi >= ki, (qi + QBLOCK - ki <= n_back) & (blk + 1 < nblk)]
        per = npair // 2
        ntile = len(tile_refs)
        cat = lambda parts: jnp.concatenate(parts, axis=1) if len(parts) > 1 else parts[0]
        classes = _class_rows(r)
        step = max(1, ATTN_UNITS // (ntile * (2 if gqa else 2 * npair)))
        for first in range(0, len(classes), step):
            batch = classes[first:first + step]
            units = []
            for ci, rows in enumerate(batch):
                tiles = [(q_ref[0, rows, :] * scale, do_ref[0, rows, :], l_ref[0, rows, :], d_ref[0, rows, :])
                         for q_ref, do_ref, l_ref, d_ref in tile_refs]
                k, v = k_ref[0, rows, :], v_ref[0, rows, :]
                if gqa:
                    for hk in range(2):
                        pairs = list(range(hk * per, (hk + 1) * per))
                        kd, vd = _dup_half(k, hk).astype(BF16), _dup_half(v, hk).astype(BF16)
                        for t, (q, do_, l_, d_) in enumerate(tiles):
                            units.append(dict(ci=ci, t=t, hk=hk, pairs=pairs, qs=_stack_masked(q, pairs).astype(BF16),
                                              dos=_stack_masked(do_, pairs).astype(BF16), lcol=_stack_pair_cols(l_, pairs),
                                              dcol=_stack_pair_cols(d_, pairs), kmat=kd, vmat=vd, kdq=kd))
                else:
                    for c in range(npair):
                        sl = slice(c * LANES, (c + 1) * LANES)
                        kc, vcb = k[:, sl], v[:, sl].astype(BF16)
                        kcb = kc.astype(BF16)
                        for t, (q, do_, l_, d_) in enumerate(tiles):
                            for half in (0, 1):
                                hm = _half_mask(kc.shape, half)
                                col = c * LANES + half * HEAD_DIM
                                units.append(dict(ci=ci, t=t, c=c, half=half, qs=jnp.where(hm, q[:, sl], 0.0).astype(BF16),
                                                  dos=jnp.where(hm, do_[:, sl], 0.0).astype(BF16), lcol=l_[:, col:col + 1],
                                                  dcol=d_[:, col:col + 1], kmat=kcb, vmat=vcb,
                                                  kdq=jnp.where(hm, kc, 0.0).astype(BF16)))
            for u in units:
                u["s"] = lax.dot_general(u["qs"], u["kmat"], _DIMS["nt"], preferred_element_type=F32)
                u["dp"] = lax.dot_general(u["dos"], u["vmat"], _DIMS["nt"], preferred_element_type=F32)
            for u in units:
                p = jnp.exp(jnp.where(valids[u["t"]], u["s"], NEG_INF) - u["lcol"])
                u["ds"] = (p * (u["dp"] + u["dcol"])).astype(BF16)
                u["p"] = p.astype(BF16)
            for u in units:
                u["dv"] = lax.dot_general(u["p"], u["dos"], _DIMS["tn"], preferred_element_type=F32)
                u["dk"] = lax.dot_general(u["ds"], u["qs"], _DIMS["tn"], preferred_element_type=F32)
                u["dq"] = jnp.dot(u["ds"], u["kdq"], preferred_element_type=F32) * scale
            for ci, rows in enumerate(batch):
                mine = [u for u in units if u["ci"] == ci]
                dq = [[None] * npair for _ in range(ntile)]
                if gqa:
                    dk_out = dv_out = None
                    for hk in range(2):
                        us = [u for u in mine if u["hk"] == hk]
                        for u in us:
                            for i, c in enumerate(u["pairs"]):
                                dq[u["t"]][c] = _pick_halves(u["dq"][2 * i * QBLOCK:(2 * i + 1) * QBLOCK],
                                                             u["dq"][(2 * i + 1) * QBLOCK:(2 * i + 2) * QBLOCK])
                        dk_h = _fold_halves(functools.reduce(jnp.add, [u["dk"] for u in us]))
                        dv_h = _fold_halves(functools.reduce(jnp.add, [u["dv"] for u in us]))
                        dk_out = dk_h if hk == 0 else _pick_halves(dk_out, dk_h)
                        dv_out = dv_h if hk == 0 else _pick_halves(dv_out, dv_h)
                else:
                    dks, dvs = [], []
                    for c in range(npair):
                        us = [u for u in mine if u["c"] == c]
                        dks.append(functools.reduce(jnp.add, [u["dk"] for u in us]))
                        dvs.append(functools.reduce(jnp.add, [u["dv"] for u in us]))
                        for t in range(ntile):
                            dq[t][c] = functools.reduce(jnp.add, [u["dq"] for u in us if u["t"] == t])
                    dk_out, dv_out = cat(dks), cat(dvs)
                ck, sk_ = c_ref[0, rows, :], s_ref[0, rows, :]
                dk_ref[0, rows, :] = _rope(dk_out, ck, sk_, sign=-1.0, mxu=gqa).astype(dk_ref.dtype)
                dv_ref[0, rows, :] = dv_out.astype(dv_ref.dtype)
                dq_cur = cat(dq[0])
                if has_next:
                    dq_cur = dq_cur + carry_ref[rows, :]
                    carry_ref[rows, :] = cat(dq[1])
                dq_ref[0, rows, :] = _rope(dq_cur, ck, sk_, sign=-1.0, mxu=gqa).astype(dq_ref.dtype)

    def at(width, col0, shift):
        return pl.BlockSpec((1, rr, width), lambda b, c, i: (b, jnp.minimum(i + shift, nblk - 1), col0 + c))

    in_specs = [at(kw, k_col, 0), at(kw, v_col, 0), pl.BlockSpec((1, rr, LANES), lambda b, c, i: (b, i, 0)),
                pl.BlockSpec((1, rr, LANES), lambda b, c, i: (b, i, 0))]
    args = [k_arr, v_arr, cos, sin]
    for shift in (0, 1) if has_next else (0,):
        in_specs += [at(qw, q_col, shift), at(qw, 0, shift), at(qw, 0, shift), at(qw, 0, shift)]
        args += [q_arr, do, lse, dd]
    if has_token:
        in_specs.append(pl.BlockSpec(token.shape, lambda b, c, i: (0, 0)))
        args.append(token)
    return pl.pallas_call(
        body,
        name=name,
        grid=(bsz, nchunk, nblk),
        in_specs=in_specs,
        out_specs=[pl.BlockSpec((1, rr, qw), lambda b, c, i: (b, i, c)),
                   pl.BlockSpec((1, rr, kw), lambda b, c, i: (b, i, c)),
                   pl.BlockSpec((1, rr, kw), lambda b, c, i: (b, i, c))],
        out_shape=[jax.ShapeDtypeStruct((bsz, seq, nchunk * qw), BF16 if r == 1 else F32),
                   jax.ShapeDtypeStruct((bsz, seq, nchunk * kw), BF16 if r == 1 else F32),
                   jax.ShapeDtypeStruct((bsz, seq, nchunk * kw), BF16 if r == 1 else F32)],
        scratch_shapes=[pltpu.VMEM((rr, qw) if has_next else (8, LANES), F32)],
        compiler_params=_params("parallel", "parallel", "arbitrary"),
    )(*args)


B_CHUNKS = {1: (4, 1), 4: (1, 4), 16: (1, 4)}


def _rope_tables(positions):
    half = HEAD_DIM // 2
    inv = ROPE_THETA ** (-jnp.arange(half, dtype=F32) / half)
    ang = positions.astype(F32)[..., None] * inv
    cos, sin = jnp.cos(ang), jnp.sin(ang)
    return jnp.concatenate([cos] * 4, axis=-1), jnp.concatenate([-sin, sin, -sin, sin], axis=-1)


def _layer_step(x, mod, tables, sinks, ln1_g, ln1_b, ln2_g, ln2_b, target, get_w_in, get_rest, hook):
    bsz, seq, d = x.shape
    ntok = bsz * seq
    flat = lambda v: v.reshape(ntok, v.shape[-1])
    unflat = lambda v: v.reshape(bsz, seq, v.shape[-1])
    cos, sin = tables
    mm = functools.partial(_matmul, tm=1024, tk=1024)
    scalar = lambda tok: 0.0 if tok is None else tok[0, 0]

    u1 = _modulate_in(x, mod)
    u1f = flat(u1)
    wint = get_w_in(u1)
    cosf, sinf = flat(cos), flat(sin)
    proj = functools.partial(_proj_rope, u1f, wint, cosf, sinf, tm=2048)
    qkvb = unflat(proj(n=4608, b_off=OFF_QKVB, rope_cols=3072, tn=256, name="proj_qkvb"))
    b_kws, os_, ls_ = [], [], []
    for g, (window, r) in enumerate(B_PATTERNS):
        npair, nch = B_CHUNKS[r]
        per = B_HEADS_PER_GROUP // (2 * npair)
        nsec = len(B_PATTERNS) * per
        kw_ = dict(npair=npair, gqa=False, q_col=g * per, k_col=nsec + g * per, v_col=2 * nsec + g * per, nchunk=nch, r=r,
                   n_back=window // r)
        b_kws.append(kw_)
        o_g, l_g = _attn_fwd(qkvb, qkvb, qkvb, name=f"attn_b{g}_fwd", **kw_)
        os_.append(o_g)
        ls_.append(l_g)
    ob = _merge_b(os_, ls_)
    tok = hook("projected", ob)
    u1t = u1f if tok is None else lax.optimization_barrier((u1f, tok))[0]
    proj = functools.partial(_proj_rope, u1t, wint, cosf, sinf, tm=2048)
    gab = unflat(proj(n=2048, b_off=OFF_GAB, rope_cols=0, tn=256, name="proj_gab", out_dtype=BF16))
    qa = unflat(proj(n=1024, b_off=OFF_QA, rope_cols=1024, tn=512, name="proj_qa"))
    kva = unflat(proj(n=256, b_off=OFF_KVA, rope_cols=128, tn=128, name="proj_kva"))
    a_kw = dict(npair=A_Q_HEADS // 2, gqa=True, q_col=0, k_col=0, v_col=1, nchunk=1, r=1, n_back=A_WINDOW - 1)
    oa, lse_a = _attn_fwd(qa, kva, kva, name="attn_a_fwd", sinks=sinks.reshape(A_Q_HEADS), **a_kw)
    rest = get_rest(oa)
    wba, wbbt, wo, wgut, wd = (rest[n] for n in ("w_branch_a", "w_branch_b", "w_o", "w_gate_up", "w_down"))
    ya = unflat(mm(flat(oa), wba, mode="nn", out_dtype=BF16, tn=512, name="branch_a"))
    ybf, mergedf = _branch_b_gate_merge(flat(ob), wbbt, flat(gab), flat(ya))
    xf = flat(x)
    y1f, h1f, u2f = _wo_ln1(mergedf, wo, xf, mod, ln1_g, ln1_b, seq)
    wgut_i = _interleave_gate_up(wgut)
    hf, af = _gate_up_silu(u2f, wgut_i)

    dy2f, dh1af, acc2 = _down_ln2_loss_bwd(af, wd, h1f, mod, ln2_g, ln2_b, flat(target), seq)
    g_wd = _matmul(af, dy2f, mode="tn", out_dtype=BF16, tm=256, tn=1024, tk=ntok, name="down_wgrad")
    dhf = _down_dgrad_silu_bwd(dy2f, wd, hf)
    g_wgut = _interleave_gate_up(_matmul(dhf, u2f, mode="tn", out_dtype=BF16, tm=256, tn=1024, tk=ntok, name="gate_up_wgrad"))
    dy1f, dxaf, acc1 = _gate_up_dgrad_ln1_bwd(dhf, wgut_i, dh1af, xf, y1f, mod, ln1_g, ln1_b, seq)
    g_wo = _matmul(mergedf, dy1f, mode="tn", out_dtype=BF16, tm=256, tn=1024, tk=ntok, name="w_o_wgrad")
    dyaf, dybf, dgaf, dgbf = _wo_dgrad_gate_bwd(dy1f, wo, flat(gab), flat(ya), ybf)
    g_wba = _matmul(flat(oa), dyaf, mode="tn", out_dtype=BF16, tm=256, tn=1024, tk=ntok, name="branch_a_wgrad")
    g_wbbt = _matmul(dybf, flat(ob), mode="tn", out_dtype=BF16, tm=256, tn=512, tk=ntok, name="branch_b_wgrad")
    tok = hook("grads_rest", dict(w_branch_a=g_wba, w_branch_b=g_wbbt, w_o=g_wo, w_gate_up=g_wgut, w_down=g_wd))

    sinks_exp = jnp.repeat(sinks.reshape(1, A_Q_HEADS), HEAD_DIM, axis=1) + scalar(tok)
    doa, dd_a, acc_s = _branch_a_dgrad_delta(dyaf, wba, flat(oa), flat(lse_a), sinks_exp, seq)
    doa, dd_a = unflat(doa), unflat(dd_a)
    tok = hook("delta_done", dd_a)
    dqa, dka, dva = _attn_bwd(qa, kva, kva, cos, sin, doa, lse_a, dd_a, name="attn_a_bwd", token=tok, **a_kw)
    merged_bwd = [unflat(t) for t in _branch_b_dgrad_merge_bwd(dybf, wbbt, [flat(t) for t in os_], [flat(t) for t in ls_])]
    dqs, dks, dvs = [], [], []
    for g in range(len(B_PATTERNS)):
        dq_g, dk_g, dv_g = _attn_bwd(qkvb, qkvb, qkvb, cos, sin, merged_bwd[g], ls_[g], merged_bwd[3 + g],
                                     name=f"attn_b{g}_bwd", **b_kws[g])
        dqs.append(dq_g)
        dks.append(dk_g)
        dvs.append(dv_g)
    dproj = jnp.concatenate([t.astype(BF16) for t in [dqa, dka, dva] + dqs + dks + dvs] + [unflat(dgaf), unflat(dgbf)], axis=-1)
    dprojf = flat(dproj)
    g_wint = _matmul(dprojf, u1f, mode="tn", out_dtype=BF16, tm=256, tn=1024, tk=ntok, name="w_in_wgrad")
    tok = hook("grads_w_in", dict(w_in=g_wint))
    grad_x, acc0 = _w_in_dgrad_grad_x(dprojf, wint, dxaf, xf, mod, seq, tok)
    grad_x = unflat(grad_x)
    tok = hook("dgrad_done", grad_x)

    loss_part = jnp.sum(acc2[:, 3, 0])
    dmod = jnp.stack([acc0[:, 1], acc0[:, 0], acc1[:, 2], acc1[:, 4], acc1[:, 3], acc2[:, 2]], axis=1)
    small = jnp.stack([acc1[:, 0].sum(0), acc1[:, 1].sum(0), acc2[:, 0].sum(0), acc2[:, 1].sum(0), acc_s[:, 0].sum(0)])
    small = small + scalar(tok)
    return loss_part, grad_x, dmod, small


CHIP_FLIPS = (2, 4, 6)


def _my_place():
    return lax.axis_index("x"), lax.axis_index("y"), lax.axis_index("c")


def _flip(place, k):
    px, py, pc = place
    return (1 - px if k & 4 else px, 1 - py if k & 2 else py, 1 - pc if k & 1 else pc)


def _index(place):
    return 4 * place[0] + 2 * place[1] + place[2]


def _gather_small(v, name):
    rows, cols = v.shape

    def body(v_ref, out_ref, send_sems, recv_sems):
        me = _my_place()
        out_ref[_index(me)] = v_ref[...]
        copies = []
        for k in range(1, N_DEV):
            copies.append(pltpu.make_async_remote_copy(
                src_ref=v_ref, dst_ref=out_ref.at[_index(me)], send_sem=send_sems.at[k - 1], recv_sem=recv_sems.at[k - 1],
                device_id=_flip(me, k), device_id_type=MESH))
        for cp in copies:
            cp.start()
        for k in range(1, N_DEV):
            pltpu.make_async_remote_copy(
                src_ref=v_ref, dst_ref=out_ref.at[_index(_flip(me, k))], send_sem=send_sems.at[k - 1],
                recv_sem=recv_sems.at[k - 1], device_id=_flip(me, k), device_id_type=MESH).wait_recv()
        for cp in copies:
            cp.wait_send()

    return pl.pallas_call(
        body,
        name=name,
        out_shape=jax.ShapeDtypeStruct((N_DEV, rows, cols), v.dtype),
        in_specs=[pl.BlockSpec(memory_space=pltpu.VMEM)],
        out_specs=pl.BlockSpec(memory_space=pltpu.VMEM),
        scratch_shapes=[pltpu.SemaphoreType.DMA((N_DEV - 1,)), pltpu.SemaphoreType.DMA((N_DEV - 1,))],
        compiler_params=pltpu.CompilerParams(vmem_limit_bytes=VMEM_LIMIT_BYTES),
    )(v)


_HBM = pl.BlockSpec(memory_space=pltpu.HBM)
_SEM = pl.BlockSpec(memory_space=pltpu.SEMAPHORE)
_EFFECT = pltpu.SideEffectType.DATAFLOW_SIDE_EFFECTING


def _remote(src, dst, send_sems, recv_sems, j, to):
    return pltpu.make_async_remote_copy(src_ref=src, dst_ref=dst, send_sem=send_sems.at[j], recv_sem=recv_sems.at[j],
                                        device_id=to, device_id_type=MESH)


def _copies_start(name, bufs, make_copies, nsem):
    nbuf = len(bufs)

    def body(*refs):
        for cp in make_copies(refs[:nbuf], refs[nbuf], refs[nbuf + 1]):
            cp.start()
        refs[-1][...] = jnp.zeros_like(refs[-1])

    sems = pltpu.SemaphoreType.DMA((nsem,))
    res = pl.pallas_call(
        body, name=name,
        out_shape=(sems, sems, *[pltpu.HBM(v.shape, v.dtype) for v in bufs], jax.ShapeDtypeStruct((8, LANES), F32)),
        in_specs=(_HBM,) * nbuf, out_specs=(_SEM, _SEM) + (_HBM,) * nbuf + (pl.BlockSpec(memory_space=pltpu.VMEM),),
        input_output_aliases={i: 2 + i for i in range(nbuf)},
        compiler_params=pltpu.CompilerParams(has_side_effects=_EFFECT),
    )(*[pltpu.with_memory_space_constraint(v, pltpu.HBM) for v in bufs])
    return res[0], res[1], list(res[2:2 + nbuf]), res[-1]


def _copies_wait(name, started, make_copies, after):
    send_sems, recv_sems, bufs, _ = started
    nbuf = len(bufs)

    def body(*refs):
        for cp in make_copies(refs[:nbuf], refs[nbuf], refs[nbuf + 1]):
            cp.wait_send()
            cp.wait_recv()

    return list(pl.pallas_call(
        body, name=name,
        out_shape=tuple(pltpu.HBM(v.shape, v.dtype) for v in bufs),
        in_specs=(_HBM,) * nbuf + (_SEM, _SEM, pl.BlockSpec(memory_space=pl.ANY)), out_specs=(_HBM,) * nbuf,
        input_output_aliases={i: i for i in range(nbuf)},
        compiler_params=pltpu.CompilerParams(has_side_effects=_EFFECT),
    )(*bufs, send_sems, recv_sems, after))


def _to_sibling_copies(refs, send_sems, recv_sems):
    src_ref, land_ref = refs
    me = _my_place()
    return [_remote(src_ref.at[q, 1 - me[2]], land_ref.at[q], send_sems, recv_sems, q, _flip(me, 1)) for q in range(4)]


def _to_chips_copies(refs, send_sems, recv_sems):
    src_ref, land_ref = refs
    me = _my_place()
    copies = []
    for j, k in enumerate(CHIP_FLIPS):
        to = _flip(me, k)
        copies.append(_remote(src_ref.at[2 * to[0] + to[1]], land_ref.at[j], send_sems, recv_sems, j, to))
    return copies


class _Gather:
    def __init__(self, name, blocks):
        self.name, self.n = name, len(blocks)
        at_me = (_index(_my_place()), 0, 0)
        lands = [lax.dynamic_update_slice(lax.empty((N_DEV,) + v.shape, v.dtype), v[None], at_me) for v in blocks]
        self.first = _copies_start(name + "_start", list(blocks) + lands, self._first_copies, 4 * self.n)
        self.token = self.first[3]

    def _first_copies(self, refs, send_sems, recv_sems):
        me = _my_place()
        return [_remote(refs[w], refs[self.n + w].at[_index(me)], send_sems, recv_sems, 4 * w + j, _flip(me, k))
                for w in range(self.n) for j, k in enumerate((1,) + CHIP_FLIPS)]

    def _pass_copies(self, refs, send_sems, recv_sems):
        me = _my_place()
        copies = []
        for w, land in enumerate(refs):
            for j, k in enumerate(CHIP_FLIPS):
                slot = land.at[_index(_flip(me, k))]
                copies.append(_remote(slot, slot, send_sems, recv_sems, 3 * w + j, _flip(me, 1)))
        return copies

    def pass_on(self, after):
        lands = _copies_wait(self.name + "_wait", self.first, self._first_copies, after)[self.n:]
        self.second = _copies_start(self.name + "_pass_start", lands, self._pass_copies, 3 * self.n)
        return self.second[3]

    def finish(self, after):
        return _copies_wait(self.name + "_pass_wait", self.second, self._pass_copies, after)


SUM_SPLIT = 2


def _sum_pairs(parts, theirs):
    nchip, _, rows, cols = parts.shape
    tile = rows // SUM_SPLIT

    def body(c_ref, a_ref, b_ref, o_ref):
        o_ref[...] = (a_ref[0].astype(F32) + b_ref[...].astype(F32)).astype(BF16)

    spec = pl.BlockSpec((1, tile, cols), lambda q, t, c_ref: (q, t, 0))
    grid_spec = pltpu.PrefetchScalarGridSpec(
        num_scalar_prefetch=1, grid=(nchip, SUM_SPLIT),
        in_specs=[pl.BlockSpec((1, 1, tile, cols), lambda q, t, c_ref: (q, c_ref[0], t, 0)), spec], out_specs=spec)
    return pl.pallas_call(body, name="grad_sum_sibling", grid_spec=grid_spec,
                          out_shape=jax.ShapeDtypeStruct((nchip, rows, cols), BF16),
                          compiler_params=_params("parallel", "parallel"))(lax.axis_index("c").reshape(1), parts, theirs)


def _sum_final(chip_sum, got):
    _, rows, cols = chip_sum.shape
    tile = rows // SUM_SPLIT

    def body(q_ref, a_ref, g_ref, o_ref):
        o_ref[...] = ((a_ref[0].astype(F32) + g_ref[0].astype(F32)) + g_ref[1].astype(F32)) + g_ref[2].astype(F32)

    grid_spec = pltpu.PrefetchScalarGridSpec(
        num_scalar_prefetch=1, grid=(SUM_SPLIT,),
        in_specs=[pl.BlockSpec((1, tile, cols), lambda t, q_ref: (q_ref[0], t, 0)),
                  pl.BlockSpec((3, tile, cols), lambda t, q_ref: (0, t, 0))],
        out_specs=pl.BlockSpec((tile, cols), lambda t, q_ref: (t, 0)))
    my_chip = (2 * lax.axis_index("x") + lax.axis_index("y")).reshape(1)
    return pl.pallas_call(body, name="grad_sum_chips", grid_spec=grid_spec, out_shape=jax.ShapeDtypeStruct((rows, cols), F32),
                          compiler_params=_params("parallel"))(my_chip, chip_sum, got)


class _ReduceScatter:
    def __init__(self, name, slabs):
        self.name, self.rows = name, slabs.shape[1]
        parts = slabs.reshape(4, 2, self.rows, D_MODEL)
        self.first = _copies_start(name + "_sibling_start", [parts, lax.empty((4, self.rows, D_MODEL), slabs.dtype)],
                                   _to_sibling_copies, 4)
        self.token = self.first[3]

    def between_chips(self, after):
        parts, theirs = _copies_wait(self.name + "_sibling_wait", self.first, _to_sibling_copies, after)
        chip_sum = _sum_pairs(parts, theirs)
        self.second = _copies_start(self.name + "_chips_start", [chip_sum, lax.empty((3, self.rows, D_MODEL), chip_sum.dtype)],
                                    _to_chips_copies, 3)
        return self.second[3]

    def finish(self, after):
        chip_sum, got = _copies_wait(self.name + "_chips_wait", self.second, _to_chips_copies, after)
        return _sum_final(chip_sum, got)


def _ada_fwd(c_all, w, b):
    nb, _ = c_all.shape
    ncol = w.shape[1]

    def body(c_ref, w_ref, b_ref, o_ref):
        c = c_ref[...]
        act = (c * _sigmoid(c)).astype(BF16)
        o_ref[...] = jnp.dot(act, w_ref[...].astype(BF16), preferred_element_type=F32) + b_ref[...]

    return pl.pallas_call(body, name="ada_fwd", out_shape=jax.ShapeDtypeStruct((nb, ncol), F32),
                          compiler_params=pltpu.CompilerParams(vmem_limit_bytes=VMEM_LIMIT_BYTES))(c_all, w, b)


def _ada_wgrad(c_all_t, dmod_cols):
    d, nb = c_all_t.shape
    ncol = dmod_cols.shape[1]

    def body(ct_ref, dm_ref, o_ref):
        ct = ct_ref[...]
        act = (ct * _sigmoid(ct)).astype(BF16).astype(F32)
        dm = dm_ref[...].astype(BF16).astype(F32)
        acc = act[:, 0:1] * dm[0:1, :]
        for i in range(1, nb):
            acc = acc + act[:, i:i + 1] * dm[i:i + 1, :]
        o_ref[...] = acc

    return pl.pallas_call(body, name="ada_wgrad", out_shape=jax.ShapeDtypeStruct((d, ncol), F32),
                          compiler_params=pltpu.CompilerParams(vmem_limit_bytes=VMEM_LIMIT_BYTES))(c_all_t, dmod_cols)


SMALL_ROWS = 24


def _reduce_small(gathered):
    def body(g_ref, o_ref):
        acc = g_ref[0]
        for dev in range(1, N_DEV):
            acc = acc + g_ref[dev]
        o_ref[...] = acc

    return pl.pallas_call(body, name="reduce_small", out_shape=jax.ShapeDtypeStruct(gathered.shape[1:], F32))(gathered)


def _adamw(w, g, m, v, name):
    rows, cols = w.shape
    tile = rows
    for cand in (256, 128, 64, 32, 16, 8):
        if rows % cand == 0 and rows > cand:
            tile = cand
            break
    spec = pl.BlockSpec((tile, cols), lambda t: (t, 0))
    bc1 = 1.0 - ADAM_B1 ** ADAM_STEP
    bc2 = 1.0 - ADAM_B2 ** ADAM_STEP

    def body(w_ref, g_ref, m_ref, v_ref, d_ref, nm_ref, nv_ref):
        g_ = g_ref[...]
        nm = ADAM_B1 * m_ref[...] + (1.0 - ADAM_B1) * g_
        nv = ADAM_B2 * v_ref[...] + (1.0 - ADAM_B2) * (g_ * g_)
        d_ref[...] = -ADAM_LR * ((nm / bc1) / (jnp.sqrt(nv / bc2) + ADAM_EPS) + ADAM_WD * w_ref[...])
        nm_ref[...] = nm
        nv_ref[...] = nv

    shp = jax.ShapeDtypeStruct((rows, cols), F32)
    return pl.pallas_call(body, name=name, grid=(rows // tile,), in_specs=[spec] * 4, out_specs=[spec] * 3, out_shape=[shp] * 3,
                          compiler_params=_params("parallel"))(w, g, m, v)


_WEIGHTS = ("w_ada", "b_ada", "w_in", "sinks", "w_branch_a", "w_branch_b", "w_o", "ln1_g", "ln1_b", "w_gate_up", "w_down",
            "ln2_g", "ln2_b")
_TRANSPOSED = ("w_in", "w_branch_b", "w_gate_up")


def _pack_shard(name, w):
    w = w.astype(BF16)
    if name in _TRANSPOSED:
        w = w.T
    return w.reshape(-1, D_MODEL)


def _unpack_full(name, slab):
    if name == "w_branch_b":
        return slab.reshape(N_DEV * 128, 512)
    return slab.reshape(-1, D_MODEL)


def _unpack_group(group, gathered):
    return {n: _unpack_full(n, slab) for (n, _), slab in zip(group, gathered)}


def _unpack_grads(group, g_packed):
    g_w, off = {}, 0
    for n, r in group:
        part = g_packed[off:off + r]
        off += r
        g_w[n] = part.reshape(128, 512) if n == "w_branch_b" else part
    return g_w


def kernel(x, c, positions, w_ada, b_ada, w_in, sinks, w_branch_a, w_branch_b, w_o, ln1_g, ln1_b, w_gate_up, w_down, ln2_g, ln2_b, loss_target, m_w_ada, m_b_ada, m_w_in, m_sinks, m_w_branch_a, m_w_branch_b, m_w_o, m_ln1_g, m_ln1_b, m_w_gate_up, m_w_down, m_ln2_g, m_ln2_b, v_w_ada, v_b_ada, v_w_in, v_sinks, v_w_branch_a, v_w_branch_b, v_w_o, v_ln1_g, v_ln1_b, v_w_gate_up, v_w_down, v_ln2_g, v_ln2_b):
    weights = dict(w_ada=w_ada, b_ada=b_ada, w_in=w_in, sinks=sinks, w_branch_a=w_branch_a, w_branch_b=w_branch_b, w_o=w_o,
                   ln1_g=ln1_g, ln1_b=ln1_b, w_gate_up=w_gate_up, w_down=w_down, ln2_g=ln2_g, ln2_b=ln2_b)
    m_in = dict(w_ada=m_w_ada, b_ada=m_b_ada, w_in=m_w_in, sinks=m_sinks, w_branch_a=m_w_branch_a, w_branch_b=m_w_branch_b,
                w_o=m_w_o, ln1_g=m_ln1_g, ln1_b=m_ln1_b, w_gate_up=m_w_gate_up, w_down=m_w_down, ln2_g=m_ln2_g, ln2_b=m_ln2_b)
    v_in = dict(w_ada=v_w_ada, b_ada=v_b_ada, w_in=v_w_in, sinks=v_sinks, w_branch_a=v_w_branch_a, w_branch_b=v_w_branch_b,
                w_o=v_w_o, ln1_g=v_ln1_g, ln1_b=v_ln1_b, w_gate_up=v_w_gate_up, w_down=v_w_down, ln2_g=v_ln2_g, ln2_b=v_ln2_b)
    bsz = x.shape[0]
    me = _index(_my_place())
    ada_cols = w_ada.shape[2]
    outs = {}

    def adamw(n, g):
        w2, m2, v2 = (t[n][0] if t[n].ndim == 3 else t[n] for t in (weights, m_in, v_in))
        shape = weights[n].shape
        if n in _TRANSPOSED:
            dlt, nm, nv = _adamw(w2.T, g, m2.T, v2.T, "adamw_" + n)
            outs[n] = tuple(t.T.reshape(shape) for t in (g, dlt, nm, nv))
        else:
            dlt, nm, nv = _adamw(w2, g, m2, v2, "adamw_" + n)
            outs[n] = tuple(t.reshape(shape) for t in (g, dlt, nm, nv))
        return nv

    packed_in = [_pack_shard(n, weights[n][0]) for n, _ in GROUP_IN]
    packed_rest = [_pack_shard(n, weights[n][0]) for n, _ in GROUP_REST]
    c_all = _gather_small(jnp.pad(c, ((0, 8 - bsz), (0, 0))), "gather_c")[:, :bsz].reshape(N_DEV * bsz, D_MODEL)
    gather_in = _Gather("gather_w_in", lax.optimization_barrier((packed_in, c_all))[0])
    b_cols = lax.dynamic_slice_in_dim(b_ada, me * ada_cols, ada_cols, axis=1)
    mod_cols = _ada_fwd(c_all, w_ada[0], b_cols + gather_in.token[0, 0])
    tables = _rope_tables(positions)
    mod_cols, tables, packed_rest = lax.optimization_barrier((mod_cols, tables, packed_rest))
    mod_all = _gather_small(mod_cols, "gather_mod").transpose(1, 0, 2).reshape(N_DEV * bsz, 6, D_MODEL)
    gather_rest = _Gather("gather_rest", lax.optimization_barrier((packed_rest, mod_all))[0])
    mod = jnp.pad(lax.dynamic_slice_in_dim(mod_all, me * bsz, bsz, axis=0), ((0, 0), (0, 2), (0, 0)))
    mod = mod + gather_rest.token[0, 0]
    mod = mod + gather_in.pass_on(mod)[0, 0]

    scatters = {}

    def get_w_in(after):
        return _unpack_group(GROUP_IN, gather_in.finish(after))["w_in"]

    def get_rest(after):
        return _unpack_group(GROUP_REST, gather_rest.finish(after))

    def pack_grads(group, grads):
        return jnp.concatenate([grads[n].reshape(N_DEV, r, D_MODEL) for n, r in group], axis=1)

    def hook(point, value):
        if point == "projected":
            return gather_rest.pass_on(value)
        if point == "grads_rest":
            scatters["rest"] = _ReduceScatter("scatter_rest", pack_grads(GROUP_REST, value))
            return scatters["rest"].token
        if point == "delta_done":
            return scatters["rest"].between_chips(value)
        if point == "grads_w_in":
            scatters["in"] = _ReduceScatter("scatter_w_in", pack_grads(GROUP_IN, value))
            return scatters["in"].token
        if point == "dgrad_done":
            return None
        raise ValueError(point)

    loss_part, grad_x, dmod, small = _layer_step(x, mod, tables, sinks[0], ln1_g, ln1_b, ln2_g, ln2_b, loss_target,
                                                 get_w_in, get_rest, hook)

    rows = jnp.concatenate([dmod.reshape(bsz * 6, D_MODEL), small, jnp.full((1, D_MODEL), loss_part, F32),
                            jnp.zeros((SMALL_ROWS - bsz * 6 - 6, D_MODEL), F32)], axis=0)
    small_all = _gather_small(rows, "gather_small")
    small_all = small_all + scatters["in"].between_chips(small_all)[0, 0]
    sums = _reduce_small(small_all)
    loss = sums[bsz * 6 + 5, 0]
    dmod_all = small_all[:, :bsz * 6].reshape(N_DEV * bsz, 6 * D_MODEL)
    adamw("b_ada", functools.reduce(jnp.add, [sums[6 * i:6 * i + 6] for i in range(bsz)]).reshape(1, 6 * D_MODEL))
    for i, n in enumerate(("ln1_g", "ln1_b", "ln2_g", "ln2_b")):
        adamw(n, sums[12 + i][None])
    adamw("sinks", sums[16][::HEAD_DIM][None])
    dmod_cols = lax.dynamic_slice_in_dim(dmod_all, me * ada_cols, ada_cols, axis=1)
    last = adamw("w_ada", _ada_wgrad(c_all.T, dmod_cols))
    for n, g in _unpack_grads(GROUP_REST, scatters["rest"].finish(last)).items():
        last = adamw(n, g)
    for n, g in _unpack_grads(GROUP_IN, scatters["in"].finish(last)).items():
        adamw(n, g)

    return (loss, grad_x, *[outs[n][0] for n in _WEIGHTS], *[outs[n][1] for n in _WEIGHTS], *[outs[n][2] for n in _WEIGHTS],
            *[outs[n][3] for n in _WEIGHTS])
```

```python
import functools

import jax
import jax.numpy as jnp
from jax import lax
from jax.experimental import pallas as pl
from jax.experimental.pallas import tpu as pltpu

F32 = jnp.float32
BF16 = jnp.bfloat16

D_MODEL = 1024
HEAD_DIM = 64
A_Q_HEADS = 16
A_WINDOW = 128
B_PATTERNS = ((128, 1), (512, 4), (2048, 16))
B_HEADS_PER_GROUP = 8
D_FF = 2816
QBLOCK = 128
ROPE_THETA = 10000.0
LN_EPS = 1e-5
DEEPNORM_ALPHA = 2.0 ** 0.25
NEG_INF = -1e30
ADAM_LR, ADAM_B1, ADAM_B2, ADAM_EPS, ADAM_WD, ADAM_STEP = 0.001, 0.9, 0.999, 1e-08, 0.01, 10

N_DEV = 8
MESH_AXES = ("x", "y", "c")
LANES = 128
VMEM_LIMIT_BYTES = 56 * 1024 * 1024
MESH = pl.DeviceIdType.MESH

OFF_QA, OFF_KVA, OFF_QKVB, OFF_GAB = 0, 1024, 1280, 5888
GROUP_IN = (("w_in", 992),)
GROUP_REST = (("w_branch_a", 128), ("w_branch_b", 64), ("w_o", 128), ("w_gate_up", 704), ("w_down", 352))


def _params(*sem):
    return pltpu.CompilerParams(dimension_semantics=sem, vmem_limit_bytes=VMEM_LIMIT_BYTES)


def _sigmoid(x):
    return 1.0 / (1.0 + jnp.exp(-x))


_DIMS = {"nn": (((1,), (0,)), ((), ())), "nt": (((1,), (1,)), ((), ())), "tn": (((0,), (0,)), ((), ()))}


def _matmul(a, b, *, mode, tm, tn, tk, name, out_dtype=None, n=None, b_off=0, token=None, ins=(), outs=None, epilogue=None):
    if mode == "nn":
        (m, k), nn_ = a.shape, b.shape[1]
    elif mode == "nt":
        (m, k), nn_ = a.shape, (b.shape[0] if n is None else n)
    else:
        (k, m), nn_ = a.shape, b.shape[1]
    assert m % tm == 0 and nn_ % tn == 0 and k % tk == 0 and b_off % tn == 0, (name, m, nn_, k)
    nk = k // tk
    joff = b_off // tn
    if mode == "nn":
        a_spec = pl.BlockSpec((tm, tk), lambda i, j, kk: (i, kk))
        b_spec = pl.BlockSpec((tk, tn), lambda i, j, kk: (kk, j))
    elif mode == "nt":
        a_spec = pl.BlockSpec((tm, tk), lambda i, j, kk: (i, kk))
        b_spec = pl.BlockSpec((tn, tk), lambda i, j, kk: (j + joff, kk))
    else:
        a_spec = pl.BlockSpec((tk, tm), lambda i, j, kk: (kk, i))
        b_spec = pl.BlockSpec((tk, tn), lambda i, j, kk: (kk, j))
    dims = _DIMS[mode]
    has_token = token is not None
    plain = epilogue is None
    if plain:
        outs = [(jax.ShapeDtypeStruct((m, nn_), out_dtype), (tm, tn), lambda i, j: (i, j))]

        def epilogue(acc, i, j, in_refs, out_refs):
            out_refs[0][...] = acc.astype(out_refs[0].dtype)

    nin = len(ins)

    def body(*refs):
        a_ref, b_ref = refs[:2]
        in_refs = refs[2:2 + nin]
        out_refs = refs[2 + nin + has_token:-1]
        acc_ref = refs[-1]
        kk = pl.program_id(2)
        part = lax.dot_general(a_ref[...].astype(BF16), b_ref[...].astype(BF16), dims, preferred_element_type=F32)

        def finish(acc):
            epilogue(acc, pl.program_id(0), pl.program_id(1), in_refs, out_refs)

        if nk == 1:
            finish(part)
        else:
            @pl.when(kk == 0)
            def _():
                acc_ref[...] = part

            @pl.when(kk > 0)
            def _():
                acc_ref[...] += part

            @pl.when(kk == nk - 1)
            def _():
                finish(acc_ref[...])

    def spec(block, index):
        return pl.BlockSpec(block, lambda i, j, kk: index(i, j))

    in_specs, args = [a_spec, b_spec], [a, b]
    for arr, block, index in ins:
        in_specs.append(spec(block, index))
        args.append(arr)
    if has_token:
        in_specs.append(pl.BlockSpec(token.shape, lambda i, j, kk: (0, 0)))
        args.append(token)
    res = pl.pallas_call(
        body,
        name=name,
        grid=(m // tm, nn_ // tn, nk),
        in_specs=in_specs,
        out_specs=[spec(block, index) for _, block, index in outs],
        out_shape=[shape for shape, _, _ in outs],
        scratch_shapes=[pltpu.VMEM((tm, tn) if nk > 1 else (8, LANES), F32)],
        compiler_params=_params("arbitrary", "arbitrary", "arbitrary"),
    )(*args)
    return res[0] if plain else res


def _proj_rope(a, bt, cos, sin, *, n, b_off, rope_cols, tm, tn, name, out_dtype=F32):
    m, k = a.shape
    assert m % tm == 0 and n % tn == 0 and b_off % tn == 0 and rope_cols % tn == 0, name
    joff = b_off // tn
    nrope = rope_cols // tn

    def body(a_ref, b_ref, c_ref, s_ref, o_ref):
        acc = lax.dot_general(a_ref[...], b_ref[...], _DIMS["nt"], preferred_element_type=F32)
        j = pl.program_id(1)

        @pl.when(j < nrope)
        def _():
            o_ref[...] = _rope(acc, c_ref[...], s_ref[...], coarse=True).astype(o_ref.dtype)

        @pl.when(j >= nrope)
        def _():
            o_ref[...] = acc.astype(o_ref.dtype)

    table = pl.BlockSpec((tm, LANES), lambda i, j: (i, 0))
    return pl.pallas_call(
        body,
        name=name,
        grid=(m // tm, n // tn),
        in_specs=[pl.BlockSpec((tm, k), lambda i, j: (i, 0)), pl.BlockSpec((tn, k), lambda i, j: (j + joff, 0)), table, table],
        out_specs=pl.BlockSpec((tm, tn), lambda i, j: (i, j)),
        out_shape=jax.ShapeDtypeStruct((m, n), out_dtype),
        compiler_params=_params("parallel", "parallel"),
    )(a, bt, cos, sin)


ROW_TILE = 256


def _rows(width, col=0):
    return pl.BlockSpec((1, ROW_TILE, width), lambda b, t: (b, t, col))


def _per_batch(nrows, width):
    return pl.BlockSpec((1, nrows, width), lambda b, t: (b, 0, 0))


def _whole(shape):
    return pl.BlockSpec(shape, lambda b, t: (0,) * len(shape))


def _row_call(body, name, bsz, seq, in_specs, out_specs, out_shape, accumulates=False):
    return pl.pallas_call(
        body,
        name=name,
        grid=(bsz, seq // ROW_TILE),
        in_specs=in_specs,
        out_specs=out_specs,
        out_shape=out_shape,
        compiler_params=_params("parallel", "arbitrary" if accumulates else "parallel"),
    )


def _acc_rows(acc_ref, first, rows):
    @pl.when(first)
    def _():
        acc_ref[...] = jnp.zeros_like(acc_ref)

    for r, val in enumerate(rows):
        acc_ref[0, r:r + 1, :] += val


def _colsum(v):
    return jnp.sum(v, axis=0, keepdims=True)


def _ln_stats(z):
    mu = jnp.mean(z, axis=-1, keepdims=True)
    zc = z - mu
    var = jnp.mean(zc * zc, axis=-1, keepdims=True)
    rstd = lax.rsqrt(var + LN_EPS)
    return zc * rstd, rstd


def _ln_bwd(dxhat, xhat, rstd):
    m1 = jnp.mean(dxhat, axis=-1, keepdims=True)
    m2 = jnp.mean(dxhat * xhat, axis=-1, keepdims=True)
    return rstd * (dxhat - m1 - xhat * m2)


def _modulate_in(x, mod):
    bsz, seq, d = x.shape

    def body(x_ref, mod_ref, u_ref):
        u_ref[0] = (x_ref[0] * (1.0 + mod_ref[0, 1:2, :]) + mod_ref[0, 0:1, :]).astype(BF16)

    return _row_call(body, "modulate_in", bsz, seq, [_rows(d), _per_batch(8, d)], _rows(d),
                     jax.ShapeDtypeStruct((bsz, seq, d), BF16))(x, mod)


def _gate_merge(gab, ya, yb):
    bsz, seq, d = ya.shape

    def body(ga_ref, gb_ref, ya_ref, yb_ref, o_ref):
        ga, gb, ya_, yb_ = (r[0].astype(F32) for r in (ga_ref, gb_ref, ya_ref, yb_ref))
        o_ref[0] = (_sigmoid(ga) * ya_ + _sigmoid(gb) * yb_).astype(BF16)

    return _row_call(body, "gate_merge", bsz, seq, [_rows(d, 0), _rows(d, 1), _rows(d), _rows(d)], _rows(d),
                     jax.ShapeDtypeStruct((bsz, seq, d), BF16))(gab, gab, ya, yb)


EP_TILE = 512


def _ep_specs(seq, d):
    tiles = seq // EP_TILE
    return ((EP_TILE, d), lambda i, j: (i, 0)), ((1, 8, d), lambda i, j: (i // tiles, 0, 0)), ((1, d), lambda i, j: (0, 0))


def _wo_ln1(merged, wo, x, mod, g, b, seq):
    ntok, d = x.shape
    row, per_b, whole = _ep_specs(seq, d)

    def epilogue(y, i, j, ins, outs):
        x_ref, mod_ref, g_ref, b_ref = ins
        y_ref, h_ref, u_ref = outs
        z = DEEPNORM_ALPHA * x_ref[...] + (1.0 + mod_ref[0, 2:3, :]) * y
        xhat, _ = _ln_stats(z)
        h = xhat * g_ref[...] + b_ref[...]
        y_ref[...] = y
        h_ref[...] = h
        u_ref[...] = (h * (1.0 + mod_ref[0, 4:5, :]) + mod_ref[0, 3:4, :]).astype(BF16)

    f32, bf16 = jax.ShapeDtypeStruct((ntok, d), F32), jax.ShapeDtypeStruct((ntok, d), BF16)
    return _matmul(merged, wo, mode="nn", tm=EP_TILE, tn=d, tk=d, name="w_o_ln1",
                   ins=[(x,) + row, (mod,) + per_b, (g,) + whole, (b,) + whole],
                   outs=[(f32,) + row, (f32,) + row, (bf16,) + row], epilogue=epilogue)


FF_HALF = D_FF // 2


def _interleave_gate_up(w):
    return w.reshape(2, 2, FF_HALF, w.shape[1]).transpose(1, 0, 2, 3).reshape(w.shape)


def _gate_up_silu(u2, wgut_i):
    ntok = u2.shape[0]

    def epilogue(h, i, j, ins, outs):
        h_ref, a_ref = outs
        hg, hu = h[:, :FF_HALF], h[:, FF_HALF:]
        h_ref[...] = h.astype(BF16)
        a_ref[...] = (hg * _sigmoid(hg) * hu).astype(BF16)

    return _matmul(u2, wgut_i, mode="nt", tm=EP_TILE, tn=2 * FF_HALF, tk=u2.shape[1], name="gate_up_silu",
                   outs=[(jax.ShapeDtypeStruct((ntok, 2 * D_FF), BF16), (EP_TILE, 2 * FF_HALF), lambda i, j: (i, j)),
                         (jax.ShapeDtypeStruct((ntok, D_FF), BF16), (EP_TILE, FF_HALF), lambda i, j: (i, j))],
                   epilogue=epilogue)


def _down_dgrad_silu_bwd(dy2, wd, h_i):
    ntok = dy2.shape[0]
    wide = ((EP_TILE, 2 * FF_HALF), lambda i, j: (i, j))

    def epilogue(da, i, j, ins, outs):
        h = ins[0][...].astype(F32)
        hg, hu = h[:, :FF_HALF], h[:, FF_HALF:]
        sg = _sigmoid(hg)
        outs[0][:, :FF_HALF] = (da * hu * (sg * (1.0 + hg * (1.0 - sg)))).astype(BF16)
        outs[0][:, FF_HALF:] = (da * (hg * sg)).astype(BF16)

    return _matmul(dy2, wd, mode="nt", tm=EP_TILE, tn=FF_HALF, tk=dy2.shape[1], name="down_dgrad_silu_bwd",
                   ins=[(h_i,) + wide], outs=[(jax.ShapeDtypeStruct((ntok, 2 * D_FF), BF16),) + wide], epilogue=epilogue)[0]


def _down_ln2_loss_bwd(a, wd, h1, mod, g, b, target, seq):
    ntok, d = h1.shape
    row, per_b, whole = _ep_specs(seq, d)
    tiles = seq // EP_TILE

    def epilogue(y, i, j, ins, outs):
        h_ref, mod_ref, g_ref, b_ref, t_ref = ins
        dy_ref, dh_ref, acc_ref = outs
        gate = 1.0 + mod_ref[0, 5:6, :]
        z = DEEPNORM_ALPHA * h_ref[...] + gate * y
        xhat, rstd = _ln_stats(z)
        diff = xhat * g_ref[...] + b_ref[...] - t_ref[...]
        loss = 0.5 * jnp.sum(jnp.sum(diff * diff, axis=-1, keepdims=True) / d, axis=0, keepdims=True)
        dout = diff / d
        dz = _ln_bwd(dout * g_ref[...], xhat, rstd)
        dy_ref[...] = (gate * dz).astype(BF16)
        dh_ref[...] = DEEPNORM_ALPHA * dz
        _acc_rows(acc_ref, i % tiles == 0,
                  [_colsum(dout * xhat), _colsum(dout), _colsum(dz * y), jnp.broadcast_to(loss, (1, d))])

    return _matmul(a, wd, mode="nn", tm=EP_TILE, tn=d, tk=a.shape[1], name="down_ln2_loss_bwd",
                   ins=[(h1,) + row, (mod,) + per_b, (g,) + whole, (b,) + whole, (target,) + row],
                   outs=[(jax.ShapeDtypeStruct((ntok, d), BF16),) + row, (jax.ShapeDtypeStruct((ntok, d), F32),) + row,
                         (jax.ShapeDtypeStruct((ntok // seq, 8, d), F32),) + per_b], epilogue=epilogue)


def _silu_mul_bwd(da, h):
    bsz, seq, _ = h.shape

    def body(da_ref, hg_ref, hu_ref, dh_ref):
        hg, da_ = hg_ref[0].astype(F32), da_ref[0].astype(F32)
        sg = _sigmoid(hg)
        dh_ref[0, :, :D_FF] = (da_ * hu_ref[0].astype(F32) * (sg * (1.0 + hg * (1.0 - sg)))).astype(BF16)
        dh_ref[0, :, D_FF:] = (da_ * (hg * sg)).astype(BF16)

    return _row_call(body, "silu_mul_bwd", bsz, seq, [_rows(D_FF), _rows(D_FF, 0), _rows(D_FF, 1)], _rows(2 * D_FF),
                     jax.ShapeDtypeStruct((bsz, seq, 2 * D_FF), BF16))(da, h, h)


def _gate_up_dgrad_ln1_bwd(dh, wgut, dh1a, x, y1, mod, g, b, seq):
    ntok, d = x.shape
    row, per_b, whole = _ep_specs(seq, d)
    tiles = seq // EP_TILE

    def epilogue(du, i, j, ins, outs):
        dh_ref, x_ref, y_ref, mod_ref, g_ref, b_ref = ins
        dy_ref, dx_ref, acc_ref = outs
        y = y_ref[...]
        gate = 1.0 + mod_ref[0, 2:3, :]
        z = DEEPNORM_ALPHA * x_ref[...] + gate * y
        xhat, rstd = _ln_stats(z)
        h1 = xhat * g_ref[...] + b_ref[...]
        dh1 = dh_ref[...] + du * (1.0 + mod_ref[0, 4:5, :])
        dz = _ln_bwd(dh1 * g_ref[...], xhat, rstd)
        dy_ref[...] = (gate * dz).astype(BF16)
        dx_ref[...] = DEEPNORM_ALPHA * dz
        _acc_rows(acc_ref, i % tiles == 0,
                  [_colsum(dh1 * xhat), _colsum(dh1), _colsum(dz * y), _colsum(du * h1), _colsum(du)])

    return _matmul(dh, wgut, mode="nn", tm=EP_TILE, tn=d, tk=D_FF, name="gate_up_dgrad_ln1_bwd",
                   ins=[(dh1a,) + row, (x,) + row, (y1,) + row, (mod,) + per_b, (g,) + whole, (b,) + whole],
                   outs=[(jax.ShapeDtypeStruct((ntok, d), BF16),) + row, (jax.ShapeDtypeStruct((ntok, d), F32),) + row,
                         (jax.ShapeDtypeStruct((ntok // seq, 8, d), F32),) + per_b], epilogue=epilogue)


def _wo_dgrad_gate_bwd(dy1, wo, gab, ya, yb):
    ntok, d = ya.shape
    tm, tn = 1024, 512
    tile = ((tm, tn), lambda i, j: (i, j))
    tile_b = ((tm, tn), lambda i, j: (i, j + d // tn))

    def epilogue(dm_, i, j, ins, outs):
        ga_ref, gb_ref, ya_ref, yb_ref = ins
        dya_ref, dyb_ref, dga_ref, dgb_ref = outs
        sa, sb = _sigmoid(ga_ref[...].astype(F32)), _sigmoid(gb_ref[...].astype(F32))
        dya_ref[...] = (dm_ * sa).astype(BF16)
        dyb_ref[...] = (dm_ * sb).astype(BF16)
        dga_ref[...] = (dm_ * ya_ref[...].astype(F32) * sa * (1.0 - sa)).astype(BF16)
        dgb_ref[...] = (dm_ * yb_ref[...].astype(F32) * sb * (1.0 - sb)).astype(BF16)

    shp = jax.ShapeDtypeStruct((ntok, d), BF16)
    return _matmul(dy1, wo, mode="nt", tm=tm, tn=tn, tk=d, name="w_o_dgrad_gate_bwd",
                   ins=[(gab,) + tile, (gab,) + tile_b, (ya,) + tile, (yb,) + tile],
                   outs=[(shp,) + tile] * 4, epilogue=epilogue)


def _w_in_dgrad_grad_x(dproj, wint, dxa, x, mod, seq, token):
    ntok, d = x.shape
    row, per_b, _ = _ep_specs(seq, d)
    tiles = seq // EP_TILE

    def epilogue(du, i, j, ins, outs):
        dxa_ref, x_ref, mod_ref = ins
        gx_ref, acc_ref = outs
        gx_ref[...] = dxa_ref[...] + du * (1.0 + mod_ref[0, 1:2, :])
        _acc_rows(acc_ref, i % tiles == 0, [_colsum(du * x_ref[...]), _colsum(du)])

    return _matmul(dproj, wint, mode="nn", tm=EP_TILE, tn=d, tk=wint.shape[0] // 2, name="w_in_dgrad_grad_x", token=token,
                   ins=[(dxa,) + row, (x,) + row, (mod,) + per_b],
                   outs=[(jax.ShapeDtypeStruct((ntok, d), F32),) + row, (jax.ShapeDtypeStruct((ntok // seq, 8, d), F32),) + per_b],
                   epilogue=epilogue)


def _branch_b_gate_merge(ob, wbbt, gab, ya):
    ntok, d = ya.shape
    tm, tn = 1024, 512
    tile = ((tm, tn), lambda i, j: (i, j))
    tile_b = ((tm, tn), lambda i, j: (i, j + d // tn))

    def epilogue(yb, i, j, ins, outs):
        ga_ref, gb_ref, ya_ref = ins
        yb_ref, merged_ref = outs
        yb_ref[...] = yb.astype(BF16)
        merged_ref[...] = (_sigmoid(ga_ref[...].astype(F32)) * ya_ref[...].astype(F32)
                           + _sigmoid(gb_ref[...].astype(F32)) * yb).astype(BF16)

    shp = jax.ShapeDtypeStruct((ntok, d), BF16)
    return _matmul(ob, wbbt, mode="nt", tm=tm, tn=tn, tk=ob.shape[1], name="branch_b_gate_merge",
                   ins=[(gab,) + tile, (gab,) + tile_b, (ya,) + tile], outs=[(shp,) + tile] * 2, epilogue=epilogue)


def _segsum64(v):
    rows, width = v.shape
    ri = lax.broadcasted_iota(jnp.int32, (LANES, LANES), 0) // HEAD_DIM
    ci = lax.broadcasted_iota(jnp.int32, (LANES, LANES), 1) // HEAD_DIM
    ones = jnp.where(ri == ci, 1.0, 0.0).astype(BF16)
    out = []
    for c in range(width // LANES):
        part = v[:, c * LANES:(c + 1) * LANES]
        hi = part.astype(BF16)
        lo = (part - hi.astype(F32)).astype(BF16)
        out.append(jnp.dot(hi, ones, preferred_element_type=F32) + jnp.dot(lo, ones, preferred_element_type=F32))
    return jnp.concatenate(out, axis=1) if len(out) > 1 else out[0]


def _merge_b(os_, ls_):
    bsz, seq, w = os_[0].shape

    def body(o0, o1, o2, l0, l1, l2, ob_ref):
        ls = [l0[0], l1[0], l2[0]]
        mx = jnp.maximum(jnp.maximum(ls[0], ls[1]), ls[2])
        es = [jnp.exp(l - mx) for l in ls]
        den = es[0] + es[1] + es[2]
        ob_ref[0] = ((es[0] / den) * o0[0] + (es[1] / den) * o1[0] + (es[2] / den) * o2[0]).astype(BF16)

    return _row_call(body, "merge_b", bsz, seq, [_rows(w)] * 6, _rows(w),
                     jax.ShapeDtypeStruct((bsz, seq, w), BF16))(*os_, *ls_)


def _branch_b_dgrad_merge_bwd(dyb, wbbt, os_, ls_):
    ntok, w = os_[0].shape
    row = ((EP_TILE, w), lambda i, j: (i, 0))

    def epilogue(dob_, i, j, ins, outs):
        os_r, ls_r = ins[:3], ins[3:]
        do_r, dd_r = outs[:3], outs[3:]
        ls = [l[...] for l in ls_r]
        mx = jnp.maximum(jnp.maximum(ls[0], ls[1]), ls[2])
        es = [jnp.exp(l - mx) for l in ls]
        den = es[0] + es[1] + es[2]
        ws = [e / den for e in es]
        dws = [_segsum64(dob_ * o[...]) for o in os_r]
        mean = ws[0] * dws[0] + ws[1] * dws[1] + ws[2] * dws[2]
        for wg, do_ref, dd_ref in zip(ws, do_r, dd_r):
            do_ref[...] = wg * dob_
            dd_ref[...] = -wg * mean

    shp = jax.ShapeDtypeStruct((ntok, w), F32)
    return _matmul(dyb, wbbt, mode="nn", tm=EP_TILE, tn=w, tk=dyb.shape[1], name="branch_b_dgrad_merge_bwd",
                   ins=[(v,) + row for v in list(os_) + list(ls_)], outs=[(shp,) + row] * 6, epilogue=epilogue)


def _branch_a_dgrad_delta(dya, wba, oa, lse_a, sinks_exp, seq):
    ntok, w = oa.shape
    row, per_b, whole = _ep_specs(seq, w)
    tiles = seq // EP_TILE

    def epilogue(do_, i, j, ins, outs):
        o_ref, l_ref, s_ref = ins
        do_ref, dd_ref, acc_ref = outs
        dd = -_segsum64(do_ * o_ref[...])
        do_ref[...] = do_
        dd_ref[...] = dd
        _acc_rows(acc_ref, i % tiles == 0, [_colsum(dd * jnp.exp(s_ref[...] - l_ref[...]))])

    shp = jax.ShapeDtypeStruct((ntok, w), F32)
    return _matmul(dya, wba, mode="nt", tm=EP_TILE, tn=w, tk=dya.shape[1], name="branch_a_dgrad_delta",
                   ins=[(oa,) + row, (lse_a,) + row, (sinks_exp,) + whole],
                   outs=[(shp,) + row, (shp,) + row, (jax.ShapeDtypeStruct((ntok // seq, 8, w), F32),) + per_b],
                   epilogue=epilogue)


def _swap_halves(v):
    src = lax.broadcasted_iota(jnp.int32, (LANES, LANES), 0)
    dst = lax.broadcasted_iota(jnp.int32, (LANES, LANES), 1)
    partner = jnp.where((dst % HEAD_DIM) < HEAD_DIM // 2, dst + HEAD_DIM // 2, dst - HEAD_DIM // 2)
    perm = jnp.where(src == partner, 1.0, 0.0).astype(BF16)
    hi = v.astype(BF16)
    lo = (v - hi.astype(F32)).astype(BF16)
    return jnp.dot(hi, perm, preferred_element_type=F32) + jnp.dot(lo, perm, preferred_element_type=F32)


def _swap_halves_roll(v):
    lane = lax.broadcasted_iota(jnp.int32, v.shape, 1)
    return jnp.where((lane % HEAD_DIM) < HEAD_DIM // 2, pltpu.roll(v, LANES - HEAD_DIM // 2, 1),
                     pltpu.roll(v, HEAD_DIM // 2, 1))


def _swap_halves_coarse(v):
    src = lax.broadcasted_iota(jnp.int32, (LANES, LANES), 0)
    dst = lax.broadcasted_iota(jnp.int32, (LANES, LANES), 1)
    partner = jnp.where((dst % HEAD_DIM) < HEAD_DIM // 2, dst + HEAD_DIM // 2, dst - HEAD_DIM // 2)
    perm = jnp.where(src == partner, 1.0, 0.0).astype(BF16)
    return jnp.dot(v.astype(BF16), perm, preferred_element_type=F32)


def _rope(v, cos, sin, sign=1.0, mxu=True, coarse=False):
    swap = (_swap_halves_coarse if coarse else _swap_halves) if mxu else _swap_halves_roll
    out = []
    for c in range(v.shape[1] // LANES):
        part = v[:, c * LANES:(c + 1) * LANES]
        out.append(part * cos + sign * (swap(part) * sin))
    return jnp.concatenate(out, axis=1) if len(out) > 1 else out[0]


def _half_mask(shape, half):
    lane = lax.broadcasted_iota(jnp.int32, shape, len(shape) - 1) % LANES
    return (lane < HEAD_DIM) if half == 0 else (lane >= HEAD_DIM)


def _dup_half(v, half):
    return jnp.where(_half_mask(v.shape, half), v, pltpu.roll(v, HEAD_DIM, 1))


def _fold_halves(v):
    return v + pltpu.roll(v, HEAD_DIM, 1)


def _pick_halves(lo_rows, hi_rows):
    return jnp.where(_half_mask(lo_rows.shape, 0), lo_rows, hi_rows)


def _stack_masked(v, pairs):
    parts = []
    for c in pairs:
        pair = v[:, c * LANES:(c + 1) * LANES]
        parts += [jnp.where(_half_mask(pair.shape, half), pair, 0.0) for half in (0, 1)]
    return jnp.concatenate(parts, axis=0)


def _stack_pair_cols(v, pairs):
    return jnp.concatenate([v[:, c * LANES + half * HEAD_DIM:c * LANES + half * HEAD_DIM + 1] for c in pairs for half in (0, 1)],
                           axis=0)


ATTN_UNITS = 16


def _class_rows(r):
    return [pl.ds(0, QBLOCK)] if r == 1 else [pl.ds(rho, QBLOCK, stride=r) for rho in range(r)]


def _band_mask(nrows, nk, blk, n_back, has_prev):
    qi = lax.broadcasted_iota(jnp.int32, (nrows, nk), 0) % QBLOCK
    ki = lax.broadcasted_iota(jnp.int32, (nrows, nk), 1)
    if has_prev:
        dist = qi + QBLOCK - ki
        return (dist >= 0) & (dist <= n_back) & ((ki >= QBLOCK) | (blk > 0))
    dist = qi - ki
    return (dist >= 0) & (dist <= n_back)


def _attn_fwd(q_arr, k_arr, v_arr, *, name, npair, gqa, q_col, k_col, v_col, nchunk, r, n_back, sinks=None):
    bsz, seq, _ = q_arr.shape
    rr = QBLOCK * r
    nblk = seq // rr
    qw = npair * LANES
    kw = LANES if gqa else qw
    has_prev = nblk > 1
    has_sink = sinks is not None
    scale = HEAD_DIM ** -0.5

    def body(*refs):
        refs = list(refs)
        q_ref, kc_ref, vc_ref = refs[:3]
        pos = 3
        if has_prev:
            kp_ref, vp_ref = refs[pos:pos + 2]
            pos += 2
        if has_sink:
            sink_ref = refs[pos]
            pos += 1
        o_ref, lse_ref = refs[pos:pos + 2]
        blk = pl.program_id(2)
        nk = (2 if has_prev else 1) * QBLOCK
        valid = _band_mask(QBLOCK, nk, blk, n_back, has_prev)
        per = npair // 2
        classes = _class_rows(r)
        step = max(1, ATTN_UNITS // (2 * npair))
        for first in range(0, len(classes), step):
            batch = classes[first:first + step]
            units = []
            for ci, rows in enumerate(batch):
                q = q_ref[0, rows, :] * scale
                k, v = kc_ref[0, rows, :], vc_ref[0, rows, :]
                if has_prev:
                    k = jnp.concatenate([kp_ref[0, rows, :], k], axis=0)
                    v = jnp.concatenate([vp_ref[0, rows, :], v], axis=0)
                if gqa:
                    kdup = [_dup_half(k, hk).astype(BF16) for hk in range(2)]
                    vdup = [_dup_half(v, hk) for hk in range(2)]
                for c in range(npair):
                    sl = slice(c * LANES, (c + 1) * LANES)
                    qc = q[:, sl]
                    kc, vc = (kdup[c // per], vdup[c // per]) if gqa else (k[:, sl].astype(BF16), v[:, sl])
                    for half in (0, 1):
                        qm = jnp.where(_half_mask(qc.shape, half), qc, 0.0).astype(BF16)
                        vm = jnp.where(_half_mask(vc.shape, half), vc, 0.0).astype(BF16)
                        s = lax.dot_general(qm, kc, _DIMS["nt"], preferred_element_type=F32)
                        units.append(dict(ci=ci, c=c, half=half, s=s, vm=vm, sk=sink_ref[2 * c + half] if has_sink else None))
            for u in units:
                s = jnp.where(valid, u["s"], NEG_INF)
                m = jnp.max(s, axis=1, keepdims=True)
                if has_sink:
                    m = jnp.maximum(m, u["sk"])
                p = jnp.exp(s - m)
                den = jnp.sum(p, axis=1, keepdims=True)
                if has_sink:
                    den = den + jnp.exp(u["sk"] - m)
                u.update(p=p.astype(BF16), den=den, lse=m + jnp.log(den))
            for u in units:
                u["o"] = jnp.dot(u["p"], u["vm"], preferred_element_type=F32) / u["den"]
            for ci, rows in enumerate(batch):
                outs, lses = [None] * npair, [None] * npair
                for u in units:
                    if u["ci"] != ci:
                        continue
                    c, o = u["c"], u["o"]
                    lse = jnp.broadcast_to(u["lse"], o.shape)
                    outs[c] = o if u["half"] == 0 else outs[c] + o
                    lses[c] = lse if u["half"] == 0 else _pick_halves(lses[c], lse)
                o_ref[0, rows, :] = jnp.concatenate(outs, axis=1) if npair > 1 else outs[0]
                lse_ref[0, rows, :] = jnp.concatenate(lses, axis=1) if npair > 1 else lses[0]

    def cur(width, col0):
        return pl.BlockSpec((1, rr, width), lambda b, c, i: (b, i, col0 + c))

    def prev(width, col0):
        return pl.BlockSpec((1, rr, width), lambda b, c, i: (b, jnp.maximum(i - 1, 0), col0 + c))

    in_specs = [cur(qw, q_col), cur(kw, k_col), cur(kw, v_col)]
    args = [q_arr, k_arr, v_arr]
    if has_prev:
        in_specs += [prev(kw, k_col), prev(kw, v_col)]
        args += [k_arr, v_arr]
    if has_sink:
        in_specs.append(pl.BlockSpec(memory_space=pltpu.SMEM))
        args.append(sinks)
    return pl.pallas_call(
        body,
        name=name,
        grid=(bsz, nchunk, nblk),
        in_specs=in_specs,
        out_specs=[pl.BlockSpec((1, rr, qw), lambda b, c, i: (b, i, c))] * 2,
        out_shape=[jax.ShapeDtypeStruct((bsz, seq, nchunk * qw), F32)] * 2,
        compiler_params=_params("parallel", "parallel", "parallel"),
    )(*args)


def _attn_bwd(q_arr, k_arr, v_arr, cos, sin, do, lse, dd, *, name, npair, gqa, q_col, k_col, v_col, nchunk, r, n_back,
              token=None):
    bsz, seq, _ = q_arr.shape
    rr = QBLOCK * r
    nblk = seq // rr
    qw = npair * LANES
    kw = LANES if gqa else qw
    has_next = nblk > 1
    has_token = token is not None
    scale = HEAD_DIM ** -0.5

    def body(*refs):
        refs = list(refs)
        k_ref, v_ref, c_ref, s_ref = refs[:4]
        tile_refs = [refs[4:8]]
        pos = 8
        if has_next:
            tile_refs.append(refs[pos:pos + 4])
            pos += 4
        if has_token:
            pos += 1
        dq_ref, dk_ref, dv_ref = refs[pos:pos + 3]
        carry_ref = refs[pos + 3]
        blk = pl.program_id(2)
        if has_next:
            @pl.when(blk == 0)
            def _():
                carry_ref[...] = jnp.zeros_like(carry_ref)

        nrows = (npair if gqa else 1) * QBLOCK
        qi = lax.broadcasted_iota(jnp.int32, (nrows, QBLOCK), 0) % QBLOCK
        ki = lax.broadcasted_iota(jnp.int32, (nrows, QBLOCK), 1)
        valids = [qi >= ki, (qi + QBLOCK - ki <= n_back) & (blk + 1 < nblk)]
        per = npair // 2
        ntile = len(tile_refs)
        cat = lambda parts: jnp.concatenate(parts, axis=1) if len(parts) > 1 else parts[0]
        classes = _class_rows(r)
        step = max(1, ATTN_UNITS // (ntile * (2 if gqa else 2 * npair)))
        for first in range(0, len(classes), step):
            batch = classes[first:first + step]
            units = []
            for ci, rows in enumerate(batch):
                tiles = [(q_ref[0, rows, :] * scale, do_ref[0, rows, :], l_ref[0, rows, :], d_ref[0, rows, :])
                         for q_ref, do_ref, l_ref, d_ref in tile_refs]
                k, v = k_ref[0, rows, :], v_ref[0, rows, :]
                if gqa:
                    for hk in range(2):
                        pairs = list(range(hk * per, (hk + 1) * per))
                        kd, vd = _dup_half(k, hk).astype(BF16), _dup_half(v, hk).astype(BF16)
                        for t, (q, do_, l_, d_) in enumerate(tiles):
                            units.append(dict(ci=ci, t=t, hk=hk, pairs=pairs, qs=_stack_masked(q, pairs).astype(BF16),
                                              dos=_stack_masked(do_, pairs).astype(BF16), lcol=_stack_pair_cols(l_, pairs),
                                              dcol=_stack_pair_cols(d_, pairs), kmat=kd, vmat=vd, kdq=kd))
                else:
                    for c in range(npair):
                        sl = slice(c * LANES, (c + 1) * LANES)
                        kc, vcb = k[:, sl], v[:, sl].astype(BF16)
                        kcb = kc.astype(BF16)
                        for t, (q, do_, l_, d_) in enumerate(tiles):
                            for half in (0, 1):
                                hm = _half_mask(kc.shape, half)
                                col = c * LANES + half * HEAD_DIM
                                units.append(dict(ci=ci, t=t, c=c, half=half, qs=jnp.where(hm, q[:, sl], 0.0).astype(BF16),
                                                  dos=jnp.where(hm, do_[:, sl], 0.0).astype(BF16), lcol=l_[:, col:col + 1],
                                                  dcol=d_[:, col:col + 1], kmat=kcb, vmat=vcb,
                                                  kdq=jnp.where(hm, kc, 0.0).astype(BF16)))
            for u in units:
                u["s"] = lax.dot_general(u["qs"], u["kmat"], _DIMS["nt"], preferred_element_type=F32)
                u["dp"] = lax.dot_general(u["dos"], u["vmat"], _DIMS["nt"], preferred_element_type=F32)
            for u in units:
                p = jnp.exp(jnp.where(valids[u["t"]], u["s"], NEG_INF) - u["lcol"])
                u["ds"] = (p * (u["dp"] + u["dcol"])).astype(BF16)
                u["p"] = p.astype(BF16)
            for u in units:
                u["dv"] = lax.dot_general(u["p"], u["dos"], _DIMS["tn"], preferred_element_type=F32)
                u["dk"] = lax.dot_general(u["ds"], u["qs"], _DIMS["tn"], preferred_element_type=F32)
                u["dq"] = jnp.dot(u["ds"], u["kdq"], preferred_element_type=F32) * scale
            for ci, rows in enumerate(batch):
                mine = [u for u in units if u["ci"] == ci]
                dq = [[None] * npair for _ in range(ntile)]
                if gqa:
                    dk_out = dv_out = None
                    for hk in range(2):
                        us = [u for u in mine if u["hk"] == hk]
                        for u in us:
                            for i, c in enumerate(u["pairs"]):
                                dq[u["t"]][c] = _pick_halves(u["dq"][2 * i * QBLOCK:(2 * i + 1) * QBLOCK],
                                                             u["dq"][(2 * i + 1) * QBLOCK:(2 * i + 2) * QBLOCK])
                        dk_h = _fold_halves(functools.reduce(jnp.add, [u["dk"] for u in us]))
                        dv_h = _fold_halves(functools.reduce(jnp.add, [u["dv"] for u in us]))
                        dk_out = dk_h if hk == 0 else _pick_halves(dk_out, dk_h)
                        dv_out = dv_h if hk == 0 else _pick_halves(dv_out, dv_h)
                else:
                    dks, dvs = [], []
                    for c in range(npair):
                        us = [u for u in mine if u["c"] == c]
                        dks.append(functools.reduce(jnp.add, [u["dk"] for u in us]))
                        dvs.append(functools.reduce(jnp.add, [u["dv"] for u in us]))
                        for t in range(ntile):
                            dq[t][c] = functools.reduce(jnp.add, [u["dq"] for u in us if u["t"] == t])
                    dk_out, dv_out = cat(dks), cat(dvs)
                ck, sk_ = c_ref[0, rows, :], s_ref[0, rows, :]
                dk_ref[0, rows, :] = _rope(dk_out, ck, sk_, sign=-1.0, mxu=gqa, coarse=True).astype(dk_ref.dtype)
                dv_ref[0, rows, :] = dv_out.astype(dv_ref.dtype)
                dq_cur = cat(dq[0])
                if has_next:
                    dq_cur = dq_cur + carry_ref[rows, :]
                    carry_ref[rows, :] = cat(dq[1])
                dq_ref[0, rows, :] = _rope(dq_cur, ck, sk_, sign=-1.0, mxu=gqa, coarse=True).astype(dq_ref.dtype)

    def at(width, col0, shift):
        return pl.BlockSpec((1, rr, width), lambda b, c, i: (b, jnp.minimum(i + shift, nblk - 1), col0 + c))

    in_specs = [at(kw, k_col, 0), at(kw, v_col, 0), pl.BlockSpec((1, rr, LANES), lambda b, c, i: (b, i, 0)),
                pl.BlockSpec((1, rr, LANES), lambda b, c, i: (b, i, 0))]
    args = [k_arr, v_arr, cos, sin]
    for shift in (0, 1) if has_next else (0,):
        in_specs += [at(qw, q_col, shift), at(qw, 0, shift), at(qw, 0, shift), at(qw, 0, shift)]
        args += [q_arr, do, lse, dd]
    if has_token:
        in_specs.append(pl.BlockSpec(token.shape, lambda b, c, i: (0, 0)))
        args.append(token)
    return pl.pallas_call(
        body,
        name=name,
        grid=(bsz, nchunk, nblk),
        in_specs=in_specs,
        out_specs=[pl.BlockSpec((1, rr, qw), lambda b, c, i: (b, i, c)),
                   pl.BlockSpec((1, rr, kw), lambda b, c, i: (b, i, c)),
                   pl.BlockSpec((1, rr, kw), lambda b, c, i: (b, i, c))],
        out_shape=[jax.ShapeDtypeStruct((bsz, seq, nchunk * qw), BF16 if r == 1 else F32),
                   jax.ShapeDtypeStruct((bsz, seq, nchunk * kw), BF16 if r == 1 else F32),
                   jax.ShapeDtypeStruct((bsz, seq, nchunk * kw), BF16 if r == 1 else F32)],
        scratch_shapes=[pltpu.VMEM((rr, qw) if has_next else (8, LANES), F32)],
        compiler_params=_params("parallel", "parallel", "arbitrary"),
    )(*args)


B_CHUNKS = {1: (4, 1), 4: (1, 4), 16: (1, 4)}


def _rope_tables(positions):
    half = HEAD_DIM // 2
    inv = ROPE_THETA ** (-jnp.arange(half, dtype=F32) / half)
    ang = positions.astype(F32)[..., None] * inv
    cos, sin = jnp.cos(ang), jnp.sin(ang)
    return jnp.concatenate([cos] * 4, axis=-1), jnp.concatenate([-sin, sin, -sin, sin], axis=-1)


def _layer_step(x, mod, tables, sinks, ln1_g, ln1_b, ln2_g, ln2_b, target, get_w_in, get_rest, hook):
    bsz, seq, d = x.shape
    ntok = bsz * seq
    flat = lambda v: v.reshape(ntok, v.shape[-1])
    unflat = lambda v: v.reshape(bsz, seq, v.shape[-1])
    cos, sin = tables
    mm = functools.partial(_matmul, tm=1024, tk=1024)
    scalar = lambda tok: 0.0 if tok is None else tok[0, 0]

    u1 = _modulate_in(x, mod)
    u1f = flat(u1)
    wint = get_w_in(u1)
    cosf, sinf = flat(cos), flat(sin)
    proj = functools.partial(_proj_rope, u1f, wint, cosf, sinf, tm=2048)
    qkvb = unflat(proj(n=4608, b_off=OFF_QKVB, rope_cols=3072, tn=256, name="proj_qkvb"))
    b_kws, os_, ls_ = [], [], []
    for g, (window, r) in enumerate(B_PATTERNS):
        npair, nch = B_CHUNKS[r]
        per = B_HEADS_PER_GROUP // (2 * npair)
        nsec = len(B_PATTERNS) * per
        kw_ = dict(npair=npair, gqa=False, q_col=g * per, k_col=nsec + g * per, v_col=2 * nsec + g * per, nchunk=nch, r=r,
                   n_back=window // r)
        b_kws.append(kw_)
        o_g, l_g = _attn_fwd(qkvb, qkvb, qkvb, name=f"attn_b{g}_fwd", **kw_)
        os_.append(o_g)
        ls_.append(l_g)
    ob = _merge_b(os_, ls_)
    tok = hook("projected", ob)
    u1t = u1f if tok is None else lax.optimization_barrier((u1f, tok))[0]
    proj = functools.partial(_proj_rope, u1t, wint, cosf, sinf, tm=2048)
    gab = unflat(proj(n=2048, b_off=OFF_GAB, rope_cols=0, tn=256, name="proj_gab", out_dtype=BF16))
    qa = unflat(proj(n=1024, b_off=OFF_QA, rope_cols=1024, tn=512, name="proj_qa"))
    kva = unflat(proj(n=256, b_off=OFF_KVA, rope_cols=128, tn=128, name="proj_kva"))
    a_kw = dict(npair=A_Q_HEADS // 2, gqa=True, q_col=0, k_col=0, v_col=1, nchunk=1, r=1, n_back=A_WINDOW - 1)
    oa, lse_a = _attn_fwd(qa, kva, kva, name="attn_a_fwd", sinks=sinks.reshape(A_Q_HEADS), **a_kw)
    rest = get_rest(oa)
    wba, wbbt, wo, wgut, wd = (rest[n] for n in ("w_branch_a", "w_branch_b", "w_o", "w_gate_up", "w_down"))
    ya = unflat(mm(flat(oa), wba, mode="nn", out_dtype=BF16, tn=512, name="branch_a"))
    ybf, mergedf = _branch_b_gate_merge(flat(ob), wbbt, flat(gab), flat(ya))
    xf = flat(x)
    y1f, h1f, u2f = _wo_ln1(mergedf, wo, xf, mod, ln1_g, ln1_b, seq)
    wgut_i = _interleave_gate_up(wgut)
    hf, af = _gate_up_silu(u2f, wgut_i)

    dy2f, dh1af, acc2 = _down_ln2_loss_bwd(af, wd, h1f, mod, ln2_g, ln2_b, flat(target), seq)
    g_wd = _matmul(af, dy2f, mode="tn", out_dtype=BF16, tm=256, tn=1024, tk=ntok, name="down_wgrad")
    dhf = _down_dgrad_silu_bwd(dy2f, wd, hf)
    g_wgut = _interleave_gate_up(_matmul(dhf, u2f, mode="tn", out_dtype=BF16, tm=256, tn=1024, tk=ntok, name="gate_up_wgrad"))
    dy1f, dxaf, acc1 = _gate_up_dgrad_ln1_bwd(dhf, wgut_i, dh1af, xf, y1f, mod, ln1_g, ln1_b, seq)
    g_wo = _matmul(mergedf, dy1f, mode="tn", out_dtype=BF16, tm=256, tn=1024, tk=ntok, name="w_o_wgrad")
    dyaf, dybf, dgaf, dgbf = _wo_dgrad_gate_bwd(dy1f, wo, flat(gab), flat(ya), ybf)
    g_wba = _matmul(flat(oa), dyaf, mode="tn", out_dtype=BF16, tm=256, tn=1024, tk=ntok, name="branch_a_wgrad")
    g_wbbt = _matmul(dybf, flat(ob), mode="tn", out_dtype=BF16, tm=256, tn=512, tk=ntok, name="branch_b_wgrad")
    tok = hook("grads_rest", dict(w_branch_a=g_wba, w_branch_b=g_wbbt, w_o=g_wo, w_gate_up=g_wgut, w_down=g_wd))

    sinks_exp = jnp.repeat(sinks.reshape(1, A_Q_HEADS), HEAD_DIM, axis=1) + scalar(tok)
    doa, dd_a, acc_s = _branch_a_dgrad_delta(dyaf, wba, flat(oa), flat(lse_a), sinks_exp, seq)
    doa, dd_a = unflat(doa), unflat(dd_a)
    tok = hook("delta_done", dd_a)
    dqa, dka, dva = _attn_bwd(qa, kva, kva, cos, sin, doa, lse_a, dd_a, name="attn_a_bwd", token=tok, **a_kw)
    merged_bwd = [unflat(t) for t in _branch_b_dgrad_merge_bwd(dybf, wbbt, [flat(t) for t in os_], [flat(t) for t in ls_])]
    dqs, dks, dvs = [], [], []
    for g in range(len(B_PATTERNS)):
        dq_g, dk_g, dv_g = _attn_bwd(qkvb, qkvb, qkvb, cos, sin, merged_bwd[g], ls_[g], merged_bwd[3 + g],
                                     name=f"attn_b{g}_bwd", **b_kws[g])
        dqs.append(dq_g)
        dks.append(dk_g)
        dvs.append(dv_g)
    dproj = jnp.concatenate([t.astype(BF16) for t in [dqa, dka, dva] + dqs + dks + dvs] + [unflat(dgaf), unflat(dgbf)], axis=-1)
    dprojf = flat(dproj)
    g_wint = _matmul(dprojf, u1f, mode="tn", out_dtype=BF16, tm=256, tn=1024, tk=ntok, name="w_in_wgrad")
    tok = hook("grads_w_in", dict(w_in=g_wint))
    grad_x, acc0 = _w_in_dgrad_grad_x(dprojf, wint, dxaf, xf, mod, seq, tok)
    grad_x = unflat(grad_x)
    tok = hook("dgrad_done", grad_x)

    loss_part = jnp.sum(acc2[:, 3, 0])
    dmod = jnp.stack([acc0[:, 1], acc0[:, 0], acc1[:, 2], acc1[:, 4], acc1[:, 3], acc2[:, 2]], axis=1)
    small = jnp.stack([acc1[:, 0].sum(0), acc1[:, 1].sum(0), acc2[:, 0].sum(0), acc2[:, 1].sum(0), acc_s[:, 0].sum(0)])
    small = small + scalar(tok)
    return loss_part, grad_x, dmod, small


CHIP_FLIPS = (2, 4, 6)


def _my_place():
    return lax.axis_index("x"), lax.axis_index("y"), lax.axis_index("c")


def _flip(place, k):
    px, py, pc = place
    return (1 - px if k & 4 else px, 1 - py if k & 2 else py, 1 - pc if k & 1 else pc)


def _index(place):
    return 4 * place[0] + 2 * place[1] + place[2]


def _gather_small(v, name):
    rows, cols = v.shape

    def body(v_ref, out_ref, send_sems, recv_sems):
        me = _my_place()
        out_ref[_index(me)] = v_ref[...]
        copies = []
        for k in range(1, N_DEV):
            copies.append(pltpu.make_async_remote_copy(
                src_ref=v_ref, dst_ref=out_ref.at[_index(me)], send_sem=send_sems.at[k - 1], recv_sem=recv_sems.at[k - 1],
                device_id=_flip(me, k), device_id_type=MESH))
        for cp in copies:
            cp.start()
        for k in range(1, N_DEV):
            pltpu.make_async_remote_copy(
                src_ref=v_ref, dst_ref=out_ref.at[_index(_flip(me, k))], send_sem=send_sems.at[k - 1],
                recv_sem=recv_sems.at[k - 1], device_id=_flip(me, k), device_id_type=MESH).wait_recv()
        for cp in copies:
            cp.wait_send()

    return pl.pallas_call(
        body,
        name=name,
        out_shape=jax.ShapeDtypeStruct((N_DEV, rows, cols), v.dtype),
        in_specs=[pl.BlockSpec(memory_space=pltpu.VMEM)],
        out_specs=pl.BlockSpec(memory_space=pltpu.VMEM),
        scratch_shapes=[pltpu.SemaphoreType.DMA((N_DEV - 1,)), pltpu.SemaphoreType.DMA((N_DEV - 1,))],
        compiler_params=pltpu.CompilerParams(vmem_limit_bytes=VMEM_LIMIT_BYTES),
    )(v)


_HBM = pl.BlockSpec(memory_space=pltpu.HBM)
_SEM = pl.BlockSpec(memory_space=pltpu.SEMAPHORE)
_EFFECT = pltpu.SideEffectType.DATAFLOW_SIDE_EFFECTING


def _remote(src, dst, send_sems, recv_sems, j, to):
    return pltpu.make_async_remote_copy(src_ref=src, dst_ref=dst, send_sem=send_sems.at[j], recv_sem=recv_sems.at[j],
                                        device_id=to, device_id_type=MESH)


def _copies_start(name, bufs, make_copies, nsem):
    nbuf = len(bufs)

    def body(*refs):
        for cp in make_copies(refs[:nbuf], refs[nbuf], refs[nbuf + 1]):
            cp.start()
        refs[-1][...] = jnp.zeros_like(refs[-1])

    sems = pltpu.SemaphoreType.DMA((nsem,))
    res = pl.pallas_call(
        body, name=name,
        out_shape=(sems, sems, *[pltpu.HBM(v.shape, v.dtype) for v in bufs], jax.ShapeDtypeStruct((8, LANES), F32)),
        in_specs=(_HBM,) * nbuf, out_specs=(_SEM, _SEM) + (_HBM,) * nbuf + (pl.BlockSpec(memory_space=pltpu.VMEM),),
        input_output_aliases={i: 2 + i for i in range(nbuf)},
        compiler_params=pltpu.CompilerParams(has_side_effects=_EFFECT),
    )(*[pltpu.with_memory_space_constraint(v, pltpu.HBM) for v in bufs])
    return res[0], res[1], list(res[2:2 + nbuf]), res[-1]


def _copies_wait(name, started, make_copies, after):
    send_sems, recv_sems, bufs, _ = started
    nbuf = len(bufs)

    def body(*refs):
        for cp in make_copies(refs[:nbuf], refs[nbuf], refs[nbuf + 1]):
            cp.wait_send()
            cp.wait_recv()

    return list(pl.pallas_call(
        body, name=name,
        out_shape=tuple(pltpu.HBM(v.shape, v.dtype) for v in bufs),
        in_specs=(_HBM,) * nbuf + (_SEM, _SEM, pl.BlockSpec(memory_space=pl.ANY)), out_specs=(_HBM,) * nbuf,
        input_output_aliases={i: i for i in range(nbuf)},
        compiler_params=pltpu.CompilerParams(has_side_effects=_EFFECT),
    )(*bufs, send_sems, recv_sems, after))


def _to_sibling_copies(refs, send_sems, recv_sems):
    src_ref, land_ref = refs
    me = _my_place()
    return [_remote(src_ref.at[q, 1 - me[2]], land_ref.at[q], send_sems, recv_sems, q, _flip(me, 1)) for q in range(4)]


def _to_chips_copies(refs, send_sems, recv_sems):
    src_ref, land_ref = refs
    me = _my_place()
    copies = []
    for j, k in enumerate(CHIP_FLIPS):
        to = _flip(me, k)
        copies.append(_remote(src_ref.at[2 * to[0] + to[1]], land_ref.at[j], send_sems, recv_sems, j, to))
    return copies


class _Gather:
    def __init__(self, name, blocks):
        self.name, self.n = name, len(blocks)
        at_me = (_index(_my_place()), 0, 0)
        lands = [lax.dynamic_update_slice(lax.empty((N_DEV,) + v.shape, v.dtype), v[None], at_me) for v in blocks]
        self.first = _copies_start(name + "_start", list(blocks) + lands, self._first_copies, 4 * self.n)
        self.token = self.first[3]

    def _first_copies(self, refs, send_sems, recv_sems):
        me = _my_place()
        return [_remote(refs[w], refs[self.n + w].at[_index(me)], send_sems, recv_sems, 4 * w + j, _flip(me, k))
                for w in range(self.n) for j, k in enumerate((1,) + CHIP_FLIPS)]

    def _pass_copies(self, refs, send_sems, recv_sems):
        me = _my_place()
        copies = []
        for w, land in enumerate(refs):
            for j, k in enumerate(CHIP_FLIPS):
                slot = land.at[_index(_flip(me, k))]
                copies.append(_remote(slot, slot, send_sems, recv_sems, 3 * w + j, _flip(me, 1)))
        return copies

    def pass_on(self, after):
        lands = _copies_wait(self.name + "_wait", self.first, self._first_copies, after)[self.n:]
        self.second = _copies_start(self.name + "_pass_start", lands, self._pass_copies, 3 * self.n)
        return self.second[3]

    def finish(self, after):
        return _copies_wait(self.name + "_pass_wait", self.second, self._pass_copies, after)


SUM_SPLIT = 2


def _sum_pairs(parts, theirs):
    nchip, _, rows, cols = parts.shape
    tile = rows // SUM_SPLIT

    def body(c_ref, a_ref, b_ref, o_ref):
        o_ref[...] = (a_ref[0].astype(F32) + b_ref[...].astype(F32)).astype(BF16)

    spec = pl.BlockSpec((1, tile, cols), lambda q, t, c_ref: (q, t, 0))
    grid_spec = pltpu.PrefetchScalarGridSpec(
        num_scalar_prefetch=1, grid=(nchip, SUM_SPLIT),
        in_specs=[pl.BlockSpec((1, 1, tile, cols), lambda q, t, c_ref: (q, c_ref[0], t, 0)), spec], out_specs=spec)
    return pl.pallas_call(body, name="grad_sum_sibling", grid_spec=grid_spec,
                          out_shape=jax.ShapeDtypeStruct((nchip, rows, cols), BF16),
                          compiler_params=_params("parallel", "parallel"))(lax.axis_index("c").reshape(1), parts, theirs)


def _sum_final(chip_sum, got):
    _, rows, cols = chip_sum.shape
    tile = rows // SUM_SPLIT

    def body(q_ref, a_ref, g_ref, o_ref):
        o_ref[...] = ((a_ref[0].astype(F32) + g_ref[0].astype(F32)) + g_ref[1].astype(F32)) + g_ref[2].astype(F32)

    grid_spec = pltpu.PrefetchScalarGridSpec(
        num_scalar_prefetch=1, grid=(SUM_SPLIT,),
        in_specs=[pl.BlockSpec((1, tile, cols), lambda t, q_ref: (q_ref[0], t, 0)),
                  pl.BlockSpec((3, tile, cols), lambda t, q_ref: (0, t, 0))],
        out_specs=pl.BlockSpec((tile, cols), lambda t, q_ref: (t, 0)))
    my_chip = (2 * lax.axis_index("x") + lax.axis_index("y")).reshape(1)
    return pl.pallas_call(body, name="grad_sum_chips", grid_spec=grid_spec, out_shape=jax.ShapeDtypeStruct((rows, cols), F32),
                          compiler_params=_params("parallel"))(my_chip, chip_sum, got)


class _ReduceScatter:
    def __init__(self, name, slabs):
        self.name, self.rows = name, slabs.shape[1]
        parts = slabs.reshape(4, 2, self.rows, D_MODEL)
        self.first = _copies_start(name + "_sibling_start", [parts, lax.empty((4, self.rows, D_MODEL), slabs.dtype)],
                                   _to_sibling_copies, 4)
        self.token = self.first[3]

    def between_chips(self, after):
        parts, theirs = _copies_wait(self.name + "_sibling_wait", self.first, _to_sibling_copies, after)
        chip_sum = _sum_pairs(parts, theirs)
        self.second = _copies_start(self.name + "_chips_start", [chip_sum, lax.empty((3, self.rows, D_MODEL), chip_sum.dtype)],
                                    _to_chips_copies, 3)
        return self.second[3]

    def finish(self, after):
        chip_sum, got = _copies_wait(self.name + "_chips_wait", self.second, _to_chips_copies, after)
        return _sum_final(chip_sum, got)


def _ada_fwd(c_all, w, b):
    nb, _ = c_all.shape
    ncol = w.shape[1]

    def body(c_ref, w_ref, b_ref, o_ref):
        c = c_ref[...]
        act = (c * _sigmoid(c)).astype(BF16)
        o_ref[...] = jnp.dot(act, w_ref[...].astype(BF16), preferred_element_type=F32) + b_ref[...]

    return pl.pallas_call(body, name="ada_fwd", out_shape=jax.ShapeDtypeStruct((nb, ncol), F32),
                          compiler_params=pltpu.CompilerParams(vmem_limit_bytes=VMEM_LIMIT_BYTES))(c_all, w, b)


def _ada_wgrad(c_all_t, dmod_cols):
    d, nb = c_all_t.shape
    ncol = dmod_cols.shape[1]

    def body(ct_ref, dm_ref, o_ref):
        ct = ct_ref[...]
        act = (ct * _sigmoid(ct)).astype(BF16).astype(F32)
        dm = dm_ref[...].astype(BF16).astype(F32)
        acc = act[:, 0:1] * dm[0:1, :]
        for i in range(1, nb):
            acc = acc + act[:, i:i + 1] * dm[i:i + 1, :]
        o_ref[...] = acc

    return pl.pallas_call(body, name="ada_wgrad", out_shape=jax.ShapeDtypeStruct((d, ncol), F32),
                          compiler_params=pltpu.CompilerParams(vmem_limit_bytes=VMEM_LIMIT_BYTES))(c_all_t, dmod_cols)


SMALL_ROWS = 24


def _reduce_small(gathered):
    def body(g_ref, o_ref):
        acc = g_ref[0]
        for dev in range(1, N_DEV):
            acc = acc + g_ref[dev]
        o_ref[...] = acc

    return pl.pallas_call(body, name="reduce_small", out_shape=jax.ShapeDtypeStruct(gathered.shape[1:], F32))(gathered)


def _adamw(w, g, m, v, name):
    rows, cols = w.shape
    tile = rows
    for cand in range(min(rows // 2, 512) // 8 * 8, 7, -8):
        if rows % cand == 0:
            tile = cand
            break
    spec = pl.BlockSpec((tile, cols), lambda t: (t, 0))
    bc1 = 1.0 - ADAM_B1 ** ADAM_STEP
    bc2 = 1.0 - ADAM_B2 ** ADAM_STEP

    def body(w_ref, g_ref, m_ref, v_ref, d_ref, nm_ref, nv_ref):
        g_ = g_ref[...]
        nm = ADAM_B1 * m_ref[...] + (1.0 - ADAM_B1) * g_
        nv = ADAM_B2 * v_ref[...] + (1.0 - ADAM_B2) * (g_ * g_)
        d_ref[...] = -ADAM_LR * ((nm / bc1) / (jnp.sqrt(nv / bc2) + ADAM_EPS) + ADAM_WD * w_ref[...])
        nm_ref[...] = nm
        nv_ref[...] = nv

    shp = jax.ShapeDtypeStruct((rows, cols), F32)
    return pl.pallas_call(body, name=name, grid=(rows // tile,), in_specs=[spec] * 4, out_specs=[spec] * 3, out_shape=[shp] * 3,
                          compiler_params=_params("parallel"))(w, g, m, v)


_WEIGHTS = ("w_ada", "b_ada", "w_in", "sinks", "w_branch_a", "w_branch_b", "w_o", "ln1_g", "ln1_b", "w_gate_up", "w_down",
            "ln2_g", "ln2_b")
_TRANSPOSED = ("w_in", "w_branch_b", "w_gate_up")


def _pack_shard(name, w):
    w = w.astype(BF16)
    if name in _TRANSPOSED:
        w = w.T
    return w.reshape(-1, D_MODEL)


def _unpack_full(name, slab):
    if name == "w_branch_b":
        return slab.reshape(N_DEV * 128, 512)
    return slab.reshape(-1, D_MODEL)


def _unpack_group(group, gathered):
    return {n: _unpack_full(n, slab) for (n, _), slab in zip(group, gathered)}


def _unpack_grads(group, g_packed):
    g_w, off = {}, 0
    for n, r in group:
        part = g_packed[off:off + r]
        off += r
        g_w[n] = part.reshape(128, 512) if n == "w_branch_b" else part
    return g_w


def kernel(x, c, positions, w_ada, b_ada, w_in, sinks, w_branch_a, w_branch_b, w_o, ln1_g, ln1_b, w_gate_up, w_down, ln2_g, ln2_b, loss_target, m_w_ada, m_b_ada, m_w_in, m_sinks, m_w_branch_a, m_w_branch_b, m_w_o, m_ln1_g, m_ln1_b, m_w_gate_up, m_w_down, m_ln2_g, m_ln2_b, v_w_ada, v_b_ada, v_w_in, v_sinks, v_w_branch_a, v_w_branch_b, v_w_o, v_ln1_g, v_ln1_b, v_w_gate_up, v_w_down, v_ln2_g, v_ln2_b):
    weights = dict(w_ada=w_ada, b_ada=b_ada, w_in=w_in, sinks=sinks, w_branch_a=w_branch_a, w_branch_b=w_branch_b, w_o=w_o,
                   ln1_g=ln1_g, ln1_b=ln1_b, w_gate_up=w_gate_up, w_down=w_down, ln2_g=ln2_g, ln2_b=ln2_b)
    m_in = dict(w_ada=m_w_ada, b_ada=m_b_ada, w_in=m_w_in, sinks=m_sinks, w_branch_a=m_w_branch_a, w_branch_b=m_w_branch_b,
                w_o=m_w_o, ln1_g=m_ln1_g, ln1_b=m_ln1_b, w_gate_up=m_w_gate_up, w_down=m_w_down, ln2_g=m_ln2_g, ln2_b=m_ln2_b)
    v_in = dict(w_ada=v_w_ada, b_ada=v_b_ada, w_in=v_w_in, sinks=v_sinks, w_branch_a=v_w_branch_a, w_branch_b=v_w_branch_b,
                w_o=v_w_o, ln1_g=v_ln1_g, ln1_b=v_ln1_b, w_gate_up=v_w_gate_up, w_down=v_w_down, ln2_g=v_ln2_g, ln2_b=v_ln2_b)
    bsz = x.shape[0]
    me = _index(_my_place())
    ada_cols = w_ada.shape[2]
    outs = {}

    def adamw(n, g):
        w2, m2, v2 = (t[n][0] if t[n].ndim == 3 else t[n] for t in (weights, m_in, v_in))
        shape = weights[n].shape
        if n in _TRANSPOSED:
            dlt, nm, nv = _adamw(w2.T, g, m2.T, v2.T, "adamw_" + n)
            outs[n] = tuple(t.T.reshape(shape) for t in (g, dlt, nm, nv))
        else:
            dlt, nm, nv = _adamw(w2, g, m2, v2, "adamw_" + n)
            outs[n] = tuple(t.reshape(shape) for t in (g, dlt, nm, nv))
        return nv

    packed_in = [_pack_shard(n, weights[n][0]) for n, _ in GROUP_IN]
    packed_rest = [_pack_shard(n, weights[n][0]) for n, _ in GROUP_REST]
    c_all = _gather_small(jnp.pad(c, ((0, 8 - bsz), (0, 0))), "gather_c")[:, :bsz].reshape(N_DEV * bsz, D_MODEL)
    gather_in = _Gather("gather_w_in", lax.optimization_barrier((packed_in, c_all))[0])
    b_cols = lax.dynamic_slice_in_dim(b_ada, me * ada_cols, ada_cols, axis=1)
    mod_cols = _ada_fwd(c_all, w_ada[0], b_cols + gather_in.token[0, 0])
    tables = _rope_tables(positions)
    mod_cols, tables, packed_rest = lax.optimization_barrier((mod_cols, tables, packed_rest))
    mod_all = _gather_small(mod_cols, "gather_mod").transpose(1, 0, 2).reshape(N_DEV * bsz, 6, D_MODEL)
    gather_rest = _Gather("gather_rest", lax.optimization_barrier((packed_rest, mod_all))[0])
    mod = jnp.pad(lax.dynamic_slice_in_dim(mod_all, me * bsz, bsz, axis=0), ((0, 0), (0, 2), (0, 0)))
    mod = mod + gather_rest.token[0, 0]
    mod = mod + gather_in.pass_on(mod)[0, 0]

    scatters = {}

    def get_w_in(after):
        return _unpack_group(GROUP_IN, gather_in.finish(after))["w_in"]

    def get_rest(after):
        return _unpack_group(GROUP_REST, gather_rest.finish(after))

    def pack_grads(group, grads):
        return jnp.concatenate([grads[n].reshape(N_DEV, r, D_MODEL) for n, r in group], axis=1)

    def hook(point, value):
        if point == "projected":
            return gather_rest.pass_on(value)
        if point == "grads_rest":
            scatters["rest"] = _ReduceScatter("scatter_rest", pack_grads(GROUP_REST, value))
            return scatters["rest"].token
        if point == "delta_done":
            return scatters["rest"].between_chips(value)
        if point == "grads_w_in":
            scatters["in"] = _ReduceScatter("scatter_w_in", pack_grads(GROUP_IN, value))
            return scatters["in"].token
        if point == "dgrad_done":
            return None
        raise ValueError(point)

    loss_part, grad_x, dmod, small = _layer_step(x, mod, tables, sinks[0], ln1_g, ln1_b, ln2_g, ln2_b, loss_target,
                                                 get_w_in, get_rest, hook)

    rows = jnp.concatenate([dmod.reshape(bsz * 6, D_MODEL), small, jnp.full((1, D_MODEL), loss_part, F32),
                            jnp.zeros((SMALL_ROWS - bsz * 6 - 6, D_MODEL), F32)], axis=0)
    small_all = _gather_small(rows, "gather_small")
    small_all = small_all + scatters["in"].between_chips(small_all)[0, 0]
    sums = _reduce_small(small_all)
    loss = sums[bsz * 6 + 5, 0]
    dmod_all = small_all[:, :bsz * 6].reshape(N_DEV * bsz, 6 * D_MODEL)
    adamw("b_ada", functools.reduce(jnp.add, [sums[6 * i:6 * i + 6] for i in range(bsz)]).reshape(1, 6 * D_MODEL))
    for i, n in enumerate(("ln1_g", "ln1_b", "ln2_g", "ln2_b")):
        adamw(n, sums[12 + i][None])
    adamw("sinks", sums[16][::HEAD_DIM][None])
    dmod_cols = lax.dynamic_slice_in_dim(dmod_all, me * ada_cols, ada_cols, axis=1)
    last = adamw("w_ada", _ada_wgrad(c_all.T, dmod_cols))
    for n, g in _unpack_grads(GROUP_REST, scatters["rest"].finish(last)).items():
        adamw(n, g)
    done = lax.optimization_barrier(tuple(outs[n][3] for n in outs))
    for n, g in _unpack_grads(GROUP_IN, scatters["in"].finish(done[0])).items():
        adamw(n, g)

    return (loss, grad_x, *[outs[n][0] for n in _WEIGHTS], *[outs[n][1] for n in _WEIGHTS], *[outs[n][2] for n in _WEIGHTS],
            *[outs[n][3] for n in _WEIGHTS])
```

```python
import functools

import jax
import jax.numpy as jnp
from jax import lax
from jax.experimental import pallas as pl
from jax.experimental.pallas import tpu as pltpu

F32 = jnp.float32
BF16 = jnp.bfloat16

D_MODEL = 1024
HEAD_DIM = 64
A_Q_HEADS = 16
A_WINDOW = 128
B_PATTERNS = ((128, 1), (512, 4), (2048, 16))
B_HEADS_PER_GROUP = 8
D_FF = 2816
QBLOCK = 128
ROPE_THETA = 10000.0
LN_EPS = 1e-5
DEEPNORM_ALPHA = 2.0 ** 0.25
NEG_INF = -1e30
ADAM_LR, ADAM_B1, ADAM_B2, ADAM_EPS, ADAM_WD, ADAM_STEP = 0.001, 0.9, 0.999, 1e-08, 0.01, 10

N_DEV = 8
MESH_AXES = ("x", "y", "c")
LANES = 128
VMEM_LIMIT_BYTES = 56 * 1024 * 1024
MESH = pl.DeviceIdType.MESH

OFF_QA, OFF_KVA, OFF_QKVB, OFF_GAB = 0, 1024, 1280, 5888
GROUP_IN = (("w_in", 992),)
GROUP_REST = (("w_branch_a", 128), ("w_branch_b", 64), ("w_o", 128), ("w_gate_up", 704), ("w_down", 352))


def _params(*sem):
    return pltpu.CompilerParams(dimension_semantics=sem, vmem_limit_bytes=VMEM_LIMIT_BYTES)


def _sigmoid(x):
    return 1.0 / (1.0 + jnp.exp(-x))


_DIMS = {"nn": (((1,), (0,)), ((), ())), "nt": (((1,), (1,)), ((), ())), "tn": (((0,), (0,)), ((), ()))}


def _matmul(a, b, *, mode, tm, tn, tk, name, out_dtype=None, n=None, b_off=0, token=None, ins=(), outs=None, epilogue=None):
    if mode == "nn":
        (m, k), nn_ = a.shape, b.shape[1]
    elif mode == "nt":
        (m, k), nn_ = a.shape, (b.shape[0] if n is None else n)
    else:
        (k, m), nn_ = a.shape, b.shape[1]
    assert m % tm == 0 and nn_ % tn == 0 and k % tk == 0 and b_off % tn == 0, (name, m, nn_, k)
    nk = k // tk
    joff = b_off // tn
    if mode == "nn":
        a_spec = pl.BlockSpec((tm, tk), lambda i, j, kk: (i, kk))
        b_spec = pl.BlockSpec((tk, tn), lambda i, j, kk: (kk, j))
    elif mode == "nt":
        a_spec = pl.BlockSpec((tm, tk), lambda i, j, kk: (i, kk))
        b_spec = pl.BlockSpec((tn, tk), lambda i, j, kk: (j + joff, kk))
    else:
        a_spec = pl.BlockSpec((tk, tm), lambda i, j, kk: (kk, i))
        b_spec = pl.BlockSpec((tk, tn), lambda i, j, kk: (kk, j))
    dims = _DIMS[mode]
    has_token = token is not None
    plain = epilogue is None
    if plain:
        outs = [(jax.ShapeDtypeStruct((m, nn_), out_dtype), (tm, tn), lambda i, j: (i, j))]

        def epilogue(acc, i, j, in_refs, out_refs):
            out_refs[0][...] = acc.astype(out_refs[0].dtype)

    nin = len(ins)

    def body(*refs):
        a_ref, b_ref = refs[:2]
        in_refs = refs[2:2 + nin]
        out_refs = refs[2 + nin + has_token:-1]
        acc_ref = refs[-1]
        kk = pl.program_id(2)
        part = lax.dot_general(a_ref[...].astype(BF16), b_ref[...].astype(BF16), dims, preferred_element_type=F32)

        def finish(acc):
            epilogue(acc, pl.program_id(0), pl.program_id(1), in_refs, out_refs)

        if nk == 1:
            finish(part)
        else:
            @pl.when(kk == 0)
            def _():
                acc_ref[...] = part

            @pl.when(kk > 0)
            def _():
                acc_ref[...] += part

            @pl.when(kk == nk - 1)
            def _():
                finish(acc_ref[...])

    def spec(block, index):
        return pl.BlockSpec(block, lambda i, j, kk: index(i, j))

    in_specs, args = [a_spec, b_spec], [a, b]
    for arr, block, index in ins:
        in_specs.append(spec(block, index))
        args.append(arr)
    if has_token:
        in_specs.append(pl.BlockSpec(token.shape, lambda i, j, kk: (0, 0)))
        args.append(token)
    res = pl.pallas_call(
        body,
        name=name,
        grid=(m // tm, nn_ // tn, nk),
        in_specs=in_specs,
        out_specs=[spec(block, index) for _, block, index in outs],
        out_shape=[shape for shape, _, _ in outs],
        scratch_shapes=[pltpu.VMEM((tm, tn) if nk > 1 else (8, LANES), F32)],
        compiler_params=_params("arbitrary", "arbitrary", "arbitrary"),
    )(*args)
    return res[0] if plain else res


def _proj_rope(a, bt, cos, sin, *, n, b_off, rope_cols, tm, tn, name, out_dtype=F32):
    m, k = a.shape
    assert m % tm == 0 and n % tn == 0 and b_off % tn == 0 and rope_cols % tn == 0, name
    joff = b_off // tn
    nrope = rope_cols // tn

    def body(a_ref, b_ref, c_ref, s_ref, o_ref):
        acc = lax.dot_general(a_ref[...], b_ref[...], _DIMS["nt"], preferred_element_type=F32)
        j = pl.program_id(1)

        @pl.when(j < nrope)
        def _():
            o_ref[...] = _rope(acc, c_ref[...], s_ref[...], coarse=True).astype(o_ref.dtype)

        @pl.when(j >= nrope)
        def _():
            o_ref[...] = acc.astype(o_ref.dtype)

    table = pl.BlockSpec((tm, LANES), lambda i, j: (i, 0))
    return pl.pallas_call(
        body,
        name=name,
        grid=(m // tm, n // tn),
        in_specs=[pl.BlockSpec((tm, k), lambda i, j: (i, 0)), pl.BlockSpec((tn, k), lambda i, j: (j + joff, 0)), table, table],
        out_specs=pl.BlockSpec((tm, tn), lambda i, j: (i, j)),
        out_shape=jax.ShapeDtypeStruct((m, n), out_dtype),
        compiler_params=_params("parallel", "parallel"),
    )(a, bt, cos, sin)


ROW_TILE = 256


def _rows(width, col=0):
    return pl.BlockSpec((1, ROW_TILE, width), lambda b, t: (b, t, col))


def _per_batch(nrows, width):
    return pl.BlockSpec((1, nrows, width), lambda b, t: (b, 0, 0))


def _whole(shape):
    return pl.BlockSpec(shape, lambda b, t: (0,) * len(shape))


def _row_call(body, name, bsz, seq, in_specs, out_specs, out_shape, accumulates=False):
    return pl.pallas_call(
        body,
        name=name,
        grid=(bsz, seq // ROW_TILE),
        in_specs=in_specs,
        out_specs=out_specs,
        out_shape=out_shape,
        compiler_params=_params("parallel", "arbitrary" if accumulates else "parallel"),
    )


def _acc_rows(acc_ref, first, rows):
    @pl.when(first)
    def _():
        acc_ref[...] = jnp.zeros_like(acc_ref)

    for r, val in enumerate(rows):
        acc_ref[0, r:r + 1, :] += val


def _colsum(v):
    return jnp.sum(v, axis=0, keepdims=True)


def _ln_stats(z):
    mu = jnp.mean(z, axis=-1, keepdims=True)
    zc = z - mu
    var = jnp.mean(zc * zc, axis=-1, keepdims=True)
    rstd = lax.rsqrt(var + LN_EPS)
    return zc * rstd, rstd


def _ln_bwd(dxhat, xhat, rstd):
    m1 = jnp.mean(dxhat, axis=-1, keepdims=True)
    m2 = jnp.mean(dxhat * xhat, axis=-1, keepdims=True)
    return rstd * (dxhat - m1 - xhat * m2)


def _modulate_in(x, mod):
    bsz, seq, d = x.shape

    def body(x_ref, mod_ref, u_ref):
        u_ref[0] = (x_ref[0] * (1.0 + mod_ref[0, 1:2, :]) + mod_ref[0, 0:1, :]).astype(BF16)

    return _row_call(body, "modulate_in", bsz, seq, [_rows(d), _per_batch(8, d)], _rows(d),
                     jax.ShapeDtypeStruct((bsz, seq, d), BF16))(x, mod)


def _gate_merge(gab, ya, yb):
    bsz, seq, d = ya.shape

    def body(ga_ref, gb_ref, ya_ref, yb_ref, o_ref):
        ga, gb, ya_, yb_ = (r[0].astype(F32) for r in (ga_ref, gb_ref, ya_ref, yb_ref))
        o_ref[0] = (_sigmoid(ga) * ya_ + _sigmoid(gb) * yb_).astype(BF16)

    return _row_call(body, "gate_merge", bsz, seq, [_rows(d, 0), _rows(d, 1), _rows(d), _rows(d)], _rows(d),
                     jax.ShapeDtypeStruct((bsz, seq, d), BF16))(gab, gab, ya, yb)


EP_TILE = 512


def _ep_specs(seq, d):
    tiles = seq // EP_TILE
    return ((EP_TILE, d), lambda i, j: (i, 0)), ((1, 8, d), lambda i, j: (i // tiles, 0, 0)), ((1, d), lambda i, j: (0, 0))


def _wo_ln1(merged, wo, x, mod, g, b, seq):
    ntok, d = x.shape
    row, per_b, whole = _ep_specs(seq, d)

    def epilogue(y, i, j, ins, outs):
        x_ref, mod_ref, g_ref, b_ref = ins
        y_ref, h_ref, u_ref = outs
        z = DEEPNORM_ALPHA * x_ref[...] + (1.0 + mod_ref[0, 2:3, :]) * y
        xhat, _ = _ln_stats(z)
        h = xhat * g_ref[...] + b_ref[...]
        y_ref[...] = y
        h_ref[...] = h
        u_ref[...] = (h * (1.0 + mod_ref[0, 4:5, :]) + mod_ref[0, 3:4, :]).astype(BF16)

    f32, bf16 = jax.ShapeDtypeStruct((ntok, d), F32), jax.ShapeDtypeStruct((ntok, d), BF16)
    return _matmul(merged, wo, mode="nn", tm=EP_TILE, tn=d, tk=d, name="w_o_ln1",
                   ins=[(x,) + row, (mod,) + per_b, (g,) + whole, (b,) + whole],
                   outs=[(f32,) + row, (f32,) + row, (bf16,) + row], epilogue=epilogue)


FF_HALF = D_FF // 2


def _interleave_gate_up(w):
    return w.reshape(2, 2, FF_HALF, w.shape[1]).transpose(1, 0, 2, 3).reshape(w.shape)


def _gate_up_silu(u2, wgut_i):
    ntok = u2.shape[0]

    def epilogue(h, i, j, ins, outs):
        h_ref, a_ref = outs
        hg, hu = h[:, :FF_HALF], h[:, FF_HALF:]
        h_ref[...] = h.astype(BF16)
        a_ref[...] = (hg * _sigmoid(hg) * hu).astype(BF16)

    return _matmul(u2, wgut_i, mode="nt", tm=EP_TILE, tn=2 * FF_HALF, tk=u2.shape[1], name="gate_up_silu",
                   outs=[(jax.ShapeDtypeStruct((ntok, 2 * D_FF), BF16), (EP_TILE, 2 * FF_HALF), lambda i, j: (i, j)),
                         (jax.ShapeDtypeStruct((ntok, D_FF), BF16), (EP_TILE, FF_HALF), lambda i, j: (i, j))],
                   epilogue=epilogue)


def _down_dgrad_silu_bwd(dy2, wd, h_i):
    ntok = dy2.shape[0]
    wide = ((EP_TILE, 2 * FF_HALF), lambda i, j: (i, j))

    def epilogue(da, i, j, ins, outs):
        h = ins[0][...].astype(F32)
        hg, hu = h[:, :FF_HALF], h[:, FF_HALF:]
        sg = _sigmoid(hg)
        outs[0][:, :FF_HALF] = (da * hu * (sg * (1.0 + hg * (1.0 - sg)))).astype(BF16)
        outs[0][:, FF_HALF:] = (da * (hg * sg)).astype(BF16)

    return _matmul(dy2, wd, mode="nt", tm=EP_TILE, tn=FF_HALF, tk=dy2.shape[1], name="down_dgrad_silu_bwd",
                   ins=[(h_i,) + wide], outs=[(jax.ShapeDtypeStruct((ntok, 2 * D_FF), BF16),) + wide], epilogue=epilogue)[0]


def _down_ln2_loss_bwd(a, wd, h1, mod, g, b, target, seq):
    ntok, d = h1.shape
    row, per_b, whole = _ep_specs(seq, d)
    tiles = seq // EP_TILE

    def epilogue(y, i, j, ins, outs):
        h_ref, mod_ref, g_ref, b_ref, t_ref = ins
        dy_ref, dh_ref, acc_ref = outs
        gate = 1.0 + mod_ref[0, 5:6, :]
        z = DEEPNORM_ALPHA * h_ref[...] + gate * y
        xhat, rstd = _ln_stats(z)
        diff = xhat * g_ref[...] + b_ref[...] - t_ref[...]
        loss = 0.5 * jnp.sum(jnp.sum(diff * diff, axis=-1, keepdims=True) / d, axis=0, keepdims=True)
        dout = diff / d
        dz = _ln_bwd(dout * g_ref[...], xhat, rstd)
        dy_ref[...] = (gate * dz).astype(BF16)
        dh_ref[...] = DEEPNORM_ALPHA * dz
        _acc_rows(acc_ref, i % tiles == 0,
                  [_colsum(dout * xhat), _colsum(dout), _colsum(dz * y), jnp.broadcast_to(loss, (1, d))])

    return _matmul(a, wd, mode="nn", tm=EP_TILE, tn=d, tk=a.shape[1], name="down_ln2_loss_bwd",
                   ins=[(h1,) + row, (mod,) + per_b, (g,) + whole, (b,) + whole, (target,) + row],
                   outs=[(jax.ShapeDtypeStruct((ntok, d), BF16),) + row, (jax.ShapeDtypeStruct((ntok, d), F32),) + row,
                         (jax.ShapeDtypeStruct((ntok // seq, 8, d), F32),) + per_b], epilogue=epilogue)


def _silu_mul_bwd(da, h):
    bsz, seq, _ = h.shape

    def body(da_ref, hg_ref, hu_ref, dh_ref):
        hg, da_ = hg_ref[0].astype(F32), da_ref[0].astype(F32)
        sg = _sigmoid(hg)
        dh_ref[0, :, :D_FF] = (da_ * hu_ref[0].astype(F32) * (sg * (1.0 + hg * (1.0 - sg)))).astype(BF16)
        dh_ref[0, :, D_FF:] = (da_ * (hg * sg)).astype(BF16)

    return _row_call(body, "silu_mul_bwd", bsz, seq, [_rows(D_FF), _rows(D_FF, 0), _rows(D_FF, 1)], _rows(2 * D_FF),
                     jax.ShapeDtypeStruct((bsz, seq, 2 * D_FF), BF16))(da, h, h)


def _gate_up_dgrad_ln1_bwd(dh, wgut, dh1a, x, y1, mod, g, b, seq):
    ntok, d = x.shape
    row, per_b, whole = _ep_specs(seq, d)
    tiles = seq // EP_TILE

    def epilogue(du, i, j, ins, outs):
        dh_ref, x_ref, y_ref, mod_ref, g_ref, b_ref = ins
        dy_ref, dx_ref, acc_ref = outs
        y = y_ref[...]
        gate = 1.0 + mod_ref[0, 2:3, :]
        z = DEEPNORM_ALPHA * x_ref[...] + gate * y
        xhat, rstd = _ln_stats(z)
        h1 = xhat * g_ref[...] + b_ref[...]
        dh1 = dh_ref[...] + du * (1.0 + mod_ref[0, 4:5, :])
        dz = _ln_bwd(dh1 * g_ref[...], xhat, rstd)
        dy_ref[...] = (gate * dz).astype(BF16)
        dx_ref[...] = DEEPNORM_ALPHA * dz
        _acc_rows(acc_ref, i % tiles == 0,
                  [_colsum(dh1 * xhat), _colsum(dh1), _colsum(dz * y), _colsum(du * h1), _colsum(du)])

    return _matmul(dh, wgut, mode="nn", tm=EP_TILE, tn=d, tk=D_FF, name="gate_up_dgrad_ln1_bwd",
                   ins=[(dh1a,) + row, (x,) + row, (y1,) + row, (mod,) + per_b, (g,) + whole, (b,) + whole],
                   outs=[(jax.ShapeDtypeStruct((ntok, d), BF16),) + row, (jax.ShapeDtypeStruct((ntok, d), F32),) + row,
                         (jax.ShapeDtypeStruct((ntok // seq, 8, d), F32),) + per_b], epilogue=epilogue)


def _wo_dgrad_gate_bwd(dy1, wo, gab, ya, yb):
    ntok, d = ya.shape
    tm, tn = 1024, 512
    tile = ((tm, tn), lambda i, j: (i, j))
    tile_b = ((tm, tn), lambda i, j: (i, j + d // tn))

    def epilogue(dm_, i, j, ins, outs):
        ga_ref, gb_ref, ya_ref, yb_ref = ins
        dya_ref, dyb_ref, dga_ref, dgb_ref = outs
        sa, sb = _sigmoid(ga_ref[...].astype(F32)), _sigmoid(gb_ref[...].astype(F32))
        dya_ref[...] = (dm_ * sa).astype(BF16)
        dyb_ref[...] = (dm_ * sb).astype(BF16)
        dga_ref[...] = (dm_ * ya_ref[...].astype(F32) * sa * (1.0 - sa)).astype(BF16)
        dgb_ref[...] = (dm_ * yb_ref[...].astype(F32) * sb * (1.0 - sb)).astype(BF16)

    shp = jax.ShapeDtypeStruct((ntok, d), BF16)
    return _matmul(dy1, wo, mode="nt", tm=tm, tn=tn, tk=d, name="w_o_dgrad_gate_bwd",
                   ins=[(gab,) + tile, (gab,) + tile_b, (ya,) + tile, (yb,) + tile],
                   outs=[(shp,) + tile] * 4, epilogue=epilogue)


def _w_in_dgrad_grad_x(dproj, wint, dxa, x, mod, seq, token):
    ntok, d = x.shape
    row, per_b, _ = _ep_specs(seq, d)
    tiles = seq // EP_TILE

    def epilogue(du, i, j, ins, outs):
        dxa_ref, x_ref, mod_ref = ins
        gx_ref, acc_ref = outs
        gx_ref[...] = dxa_ref[...] + du * (1.0 + mod_ref[0, 1:2, :])
        _acc_rows(acc_ref, i % tiles == 0, [_colsum(du * x_ref[...]), _colsum(du)])

    return _matmul(dproj, wint, mode="nn", tm=EP_TILE, tn=d, tk=wint.shape[0] // 2, name="w_in_dgrad_grad_x", token=token,
                   ins=[(dxa,) + row, (x,) + row, (mod,) + per_b],
                   outs=[(jax.ShapeDtypeStruct((ntok, d), F32),) + row, (jax.ShapeDtypeStruct((ntok // seq, 8, d), F32),) + per_b],
                   epilogue=epilogue)


def _branch_b_gate_merge(ob, wbbt, gab, ya):
    ntok, d = ya.shape
    tm, tn = 1024, 512
    tile = ((tm, tn), lambda i, j: (i, j))
    tile_b = ((tm, tn), lambda i, j: (i, j + d // tn))

    def epilogue(yb, i, j, ins, outs):
        ga_ref, gb_ref, ya_ref = ins
        yb_ref, merged_ref = outs
        yb_ref[...] = yb.astype(BF16)
        merged_ref[...] = (_sigmoid(ga_ref[...].astype(F32)) * ya_ref[...].astype(F32)
                           + _sigmoid(gb_ref[...].astype(F32)) * yb).astype(BF16)

    shp = jax.ShapeDtypeStruct((ntok, d), BF16)
    return _matmul(ob, wbbt, mode="nt", tm=tm, tn=tn, tk=ob.shape[1], name="branch_b_gate_merge",
                   ins=[(gab,) + tile, (gab,) + tile_b, (ya,) + tile], outs=[(shp,) + tile] * 2, epilogue=epilogue)


def _segsum64(v):
    rows, width = v.shape
    ri = lax.broadcasted_iota(jnp.int32, (LANES, LANES), 0) // HEAD_DIM
    ci = lax.broadcasted_iota(jnp.int32, (LANES, LANES), 1) // HEAD_DIM
    ones = jnp.where(ri == ci, 1.0, 0.0).astype(BF16)
    out = []
    for c in range(width // LANES):
        part = v[:, c * LANES:(c + 1) * LANES]
        hi = part.astype(BF16)
        lo = (part - hi.astype(F32)).astype(BF16)
        out.append(jnp.dot(hi, ones, preferred_element_type=F32) + jnp.dot(lo, ones, preferred_element_type=F32))
    return jnp.concatenate(out, axis=1) if len(out) > 1 else out[0]


def _merge_b(os_, ls_):
    bsz, seq, w = os_[0].shape

    def body(o0, o1, o2, l0, l1, l2, ob_ref):
        ls = [l0[0], l1[0], l2[0]]
        mx = jnp.maximum(jnp.maximum(ls[0], ls[1]), ls[2])
        es = [jnp.exp(l - mx) for l in ls]
        den = es[0] + es[1] + es[2]
        ob_ref[0] = ((es[0] / den) * o0[0] + (es[1] / den) * o1[0] + (es[2] / den) * o2[0]).astype(BF16)

    return _row_call(body, "merge_b", bsz, seq, [_rows(w)] * 6, _rows(w),
                     jax.ShapeDtypeStruct((bsz, seq, w), BF16))(*os_, *ls_)


def _branch_b_dgrad_merge_bwd(dyb, wbbt, os_, ls_):
    ntok, w = os_[0].shape
    row = ((EP_TILE, w), lambda i, j: (i, 0))

    def epilogue(dob_, i, j, ins, outs):
        os_r, ls_r = ins[:3], ins[3:]
        do_r, dd_r = outs[:3], outs[3:]
        ls = [l[...] for l in ls_r]
        mx = jnp.maximum(jnp.maximum(ls[0], ls[1]), ls[2])
        es = [jnp.exp(l - mx) for l in ls]
        den = es[0] + es[1] + es[2]
        ws = [e / den for e in es]
        dws = [_segsum64(dob_ * o[...]) for o in os_r]
        mean = ws[0] * dws[0] + ws[1] * dws[1] + ws[2] * dws[2]
        for wg, do_ref, dd_ref in zip(ws, do_r, dd_r):
            do_ref[...] = wg * dob_
            dd_ref[...] = -wg * mean

    shp = jax.ShapeDtypeStruct((ntok, w), F32)
    return _matmul(dyb, wbbt, mode="nn", tm=EP_TILE, tn=w, tk=dyb.shape[1], name="branch_b_dgrad_merge_bwd",
                   ins=[(v,) + row for v in list(os_) + list(ls_)], outs=[(shp,) + row] * 6, epilogue=epilogue)


def _branch_a_dgrad_delta(dya, wba, oa, lse_a, sinks_exp, seq):
    ntok, w = oa.shape
    row, per_b, whole = _ep_specs(seq, w)
    tiles = seq // EP_TILE

    def epilogue(do_, i, j, ins, outs):
        o_ref, l_ref, s_ref = ins
        do_ref, dd_ref, acc_ref = outs
        dd = -_segsum64(do_ * o_ref[...])
        do_ref[...] = do_
        dd_ref[...] = dd
        _acc_rows(acc_ref, i % tiles == 0, [_colsum(dd * jnp.exp(s_ref[...] - l_ref[...]))])

    shp = jax.ShapeDtypeStruct((ntok, w), F32)
    return _matmul(dya, wba, mode="nt", tm=EP_TILE, tn=w, tk=dya.shape[1], name="branch_a_dgrad_delta",
                   ins=[(oa,) + row, (lse_a,) + row, (sinks_exp,) + whole],
                   outs=[(shp,) + row, (shp,) + row, (jax.ShapeDtypeStruct((ntok // seq, 8, w), F32),) + per_b],
                   epilogue=epilogue)


def _swap_halves(v):
    src = lax.broadcasted_iota(jnp.int32, (LANES, LANES), 0)
    dst = lax.broadcasted_iota(jnp.int32, (LANES, LANES), 1)
    partner = jnp.where((dst % HEAD_DIM) < HEAD_DIM // 2, dst + HEAD_DIM // 2, dst - HEAD_DIM // 2)
    perm = jnp.where(src == partner, 1.0, 0.0).astype(BF16)
    hi = v.astype(BF16)
    lo = (v - hi.astype(F32)).astype(BF16)
    return jnp.dot(hi, perm, preferred_element_type=F32) + jnp.dot(lo, perm, preferred_element_type=F32)


def _swap_halves_roll(v):
    lane = lax.broadcasted_iota(jnp.int32, v.shape, 1)
    return jnp.where((lane % HEAD_DIM) < HEAD_DIM // 2, pltpu.roll(v, LANES - HEAD_DIM // 2, 1),
                     pltpu.roll(v, HEAD_DIM // 2, 1))


def _swap_halves_coarse(v):
    src = lax.broadcasted_iota(jnp.int32, (LANES, LANES), 0)
    dst = lax.broadcasted_iota(jnp.int32, (LANES, LANES), 1)
    partner = jnp.where((dst % HEAD_DIM) < HEAD_DIM // 2, dst + HEAD_DIM // 2, dst - HEAD_DIM // 2)
    perm = jnp.where(src == partner, 1.0, 0.0).astype(BF16)
    return jnp.dot(v.astype(BF16), perm, preferred_element_type=F32)


def _rope(v, cos, sin, sign=1.0, mxu=True, coarse=False):
    swap = (_swap_halves_coarse if coarse else _swap_halves) if mxu else _swap_halves_roll
    out = []
    for c in range(v.shape[1] // LANES):
        part = v[:, c * LANES:(c + 1) * LANES]
        out.append(part * cos + sign * (swap(part) * sin))
    return jnp.concatenate(out, axis=1) if len(out) > 1 else out[0]


def _half_mask(shape, half):
    lane = lax.broadcasted_iota(jnp.int32, shape, len(shape) - 1) % LANES
    return (lane < HEAD_DIM) if half == 0 else (lane >= HEAD_DIM)


def _dup_half(v, half):
    return jnp.where(_half_mask(v.shape, half), v, pltpu.roll(v, HEAD_DIM, 1))


def _fold_halves(v):
    return v + pltpu.roll(v, HEAD_DIM, 1)


def _pick_halves(lo_rows, hi_rows):
    return jnp.where(_half_mask(lo_rows.shape, 0), lo_rows, hi_rows)


def _stack_masked(v, pairs):
    parts = []
    for c in pairs:
        pair = v[:, c * LANES:(c + 1) * LANES]
        parts += [jnp.where(_half_mask(pair.shape, half), pair, 0.0) for half in (0, 1)]
    return jnp.concatenate(parts, axis=0)


def _stack_pair_cols(v, pairs):
    return jnp.concatenate([v[:, c * LANES + half * HEAD_DIM:c * LANES + half * HEAD_DIM + 1] for c in pairs for half in (0, 1)],
                           axis=0)


ATTN_UNITS = 16


def _class_rows(r):
    return [pl.ds(0, QBLOCK)] if r == 1 else [pl.ds(rho, QBLOCK, stride=r) for rho in range(r)]


def _band_mask(nrows, nk, blk, n_back, has_prev):
    qi = lax.broadcasted_iota(jnp.int32, (nrows, nk), 0) % QBLOCK
    ki = lax.broadcasted_iota(jnp.int32, (nrows, nk), 1)
    if has_prev:
        dist = qi + QBLOCK - ki
        return (dist >= 0) & (dist <= n_back) & ((ki >= QBLOCK) | (blk > 0))
    dist = qi - ki
    return (dist >= 0) & (dist <= n_back)


def _attn_fwd(q_arr, k_arr, v_arr, *, name, npair, gqa, q_col, k_col, v_col, nchunk, r, n_back, sinks=None):
    bsz, seq, _ = q_arr.shape
    rr = QBLOCK * r
    nblk = seq // rr
    qw = npair * LANES
    kw = LANES if gqa else qw
    has_prev = nblk > 1
    has_sink = sinks is not None
    scale = HEAD_DIM ** -0.5

    def body(*refs):
        refs = list(refs)
        q_ref, kc_ref, vc_ref = refs[:3]
        pos = 3
        if has_prev:
            kp_ref, vp_ref = refs[pos:pos + 2]
            pos += 2
        if has_sink:
            sink_ref = refs[pos]
            pos += 1
        o_ref, lse_ref = refs[pos:pos + 2]
        blk = pl.program_id(2)
        nk = (2 if has_prev else 1) * QBLOCK
        valid = _band_mask(QBLOCK, nk, blk, n_back, has_prev)
        per = npair // 2
        classes = _class_rows(r)
        step = max(1, ATTN_UNITS // (2 * npair))
        for first in range(0, len(classes), step):
            batch = classes[first:first + step]
            units = []
            for ci, rows in enumerate(batch):
                q = q_ref[0, rows, :] * scale
                k, v = kc_ref[0, rows, :], vc_ref[0, rows, :]
                if has_prev:
                    k = jnp.concatenate([kp_ref[0, rows, :], k], axis=0)
                    v = jnp.concatenate([vp_ref[0, rows, :], v], axis=0)
                if gqa:
                    kdup = [_dup_half(k, hk).astype(BF16) for hk in range(2)]
                    vdup = [_dup_half(v, hk) for hk in range(2)]
                for c in range(npair):
                    sl = slice(c * LANES, (c + 1) * LANES)
                    qc = q[:, sl]
                    kc, vc = (kdup[c // per], vdup[c // per]) if gqa else (k[:, sl].astype(BF16), v[:, sl])
                    for half in (0, 1):
                        qm = jnp.where(_half_mask(qc.shape, half), qc, 0.0).astype(BF16)
                        vm = jnp.where(_half_mask(vc.shape, half), vc, 0.0).astype(BF16)
                        s = lax.dot_general(qm, kc, _DIMS["nt"], preferred_element_type=F32)
                        units.append(dict(ci=ci, c=c, half=half, s=s, vm=vm, sk=sink_ref[2 * c + half] if has_sink else None))
            for u in units:
                s = jnp.where(valid, u["s"], NEG_INF)
                m = jnp.max(s, axis=1, keepdims=True)
                if has_sink:
                    m = jnp.maximum(m, u["sk"])
                p = jnp.exp(s - m)
                den = jnp.sum(p, axis=1, keepdims=True)
                if has_sink:
                    den = den + jnp.exp(u["sk"] - m)
                u.update(p=p.astype(BF16), den=den, lse=m + jnp.log(den))
            for u in units:
                u["o"] = jnp.dot(u["p"], u["vm"], preferred_element_type=F32) / u["den"]
            for ci, rows in enumerate(batch):
                outs, lses = [None] * npair, [None] * npair
                for u in units:
                    if u["ci"] != ci:
                        continue
                    c, o = u["c"], u["o"]
                    lse = jnp.broadcast_to(u["lse"], o.shape)
                    outs[c] = o if u["half"] == 0 else outs[c] + o
                    lses[c] = lse if u["half"] == 0 else _pick_halves(lses[c], lse)
                o_ref[0, rows, :] = jnp.concatenate(outs, axis=1) if npair > 1 else outs[0]
                lse_ref[0, rows, :] = jnp.concatenate(lses, axis=1) if npair > 1 else lses[0]

    def cur(width, col0):
        return pl.BlockSpec((1, rr, width), lambda b, c, i: (b, i, col0 + c))

    def prev(width, col0):
        return pl.BlockSpec((1, rr, width), lambda b, c, i: (b, jnp.maximum(i - 1, 0), col0 + c))

    in_specs = [cur(qw, q_col), cur(kw, k_col), cur(kw, v_col)]
    args = [q_arr, k_arr, v_arr]
    if has_prev:
        in_specs += [prev(kw, k_col), prev(kw, v_col)]
        args += [k_arr, v_arr]
    if has_sink:
        in_specs.append(pl.BlockSpec(memory_space=pltpu.SMEM))
        args.append(sinks)
    return pl.pallas_call(
        body,
        name=name,
        grid=(bsz, nchunk, nblk),
        in_specs=in_specs,
        out_specs=[pl.BlockSpec((1, rr, qw), lambda b, c, i: (b, i, c))] * 2,
        out_shape=[jax.ShapeDtypeStruct((bsz, seq, nchunk * qw), F32)] * 2,
        compiler_params=_params("parallel", "parallel", "parallel"),
    )(*args)


def _attn_bwd(q_arr, k_arr, v_arr, cos, sin, do, lse, dd, *, name, npair, gqa, q_col, k_col, v_col, nchunk, r, n_back,
              token=None):
    bsz, seq, _ = q_arr.shape
    rr = QBLOCK * r
    nblk = seq // rr
    qw = npair * LANES
    kw = LANES if gqa else qw
    has_next = nblk > 1
    has_token = token is not None
    staged = r > 1
    scale = HEAD_DIM ** -0.5

    def body(*refs):
        refs = list(refs)
        k_ref, v_ref, c_ref, s_ref = refs[:4]
        tile_refs = [refs[4:8]]
        pos = 8
        if has_next:
            tile_refs.append(refs[pos:pos + 4])
            pos += 4
        if has_token:
            pos += 1
        dq_ref, dk_ref, dv_ref = refs[pos:pos + 3]
        carry_ref = refs[pos + 3]
        if staged:
            stage_q, stage_k, stage_v = refs[pos + 4:pos + 7]
        blk = pl.program_id(2)
        if has_next:
            @pl.when(blk == 0)
            def _():
                carry_ref[...] = jnp.zeros_like(carry_ref)

        nrows = (npair if gqa else 1) * QBLOCK
        qi = lax.broadcasted_iota(jnp.int32, (nrows, QBLOCK), 0) % QBLOCK
        ki = lax.broadcasted_iota(jnp.int32, (nrows, QBLOCK), 1)
        valids = [qi >= ki, (qi + QBLOCK - ki <= n_back) & (blk + 1 < nblk)]
        per = npair // 2
        ntile = len(tile_refs)
        cat = lambda parts: jnp.concatenate(parts, axis=1) if len(parts) > 1 else parts[0]
        classes = _class_rows(r)
        step = max(1, ATTN_UNITS // (ntile * (2 if gqa else 2 * npair)))
        for first in range(0, len(classes), step):
            batch = classes[first:first + step]
            units = []
            for ci, rows in enumerate(batch):
                tiles = [(q_ref[0, rows, :] * scale, do_ref[0, rows, :], l_ref[0, rows, :], d_ref[0, rows, :])
                         for q_ref, do_ref, l_ref, d_ref in tile_refs]
                k, v = k_ref[0, rows, :], v_ref[0, rows, :]
                if gqa:
                    for hk in range(2):
                        pairs = list(range(hk * per, (hk + 1) * per))
                        kd, vd = _dup_half(k, hk).astype(BF16), _dup_half(v, hk).astype(BF16)
                        for t, (q, do_, l_, d_) in enumerate(tiles):
                            units.append(dict(ci=ci, t=t, hk=hk, pairs=pairs, qs=_stack_masked(q, pairs).astype(BF16),
                                              dos=_stack_masked(do_, pairs).astype(BF16), lcol=_stack_pair_cols(l_, pairs),
                                              dcol=_stack_pair_cols(d_, pairs), kmat=kd, vmat=vd, kdq=kd))
                else:
                    for c in range(npair):
                        sl = slice(c * LANES, (c + 1) * LANES)
                        kc, vcb = k[:, sl], v[:, sl].astype(BF16)
                        kcb = kc.astype(BF16)
                        for t, (q, do_, l_, d_) in enumerate(tiles):
                            for half in (0, 1):
                                hm = _half_mask(kc.shape, half)
                                col = c * LANES + half * HEAD_DIM
                                units.append(dict(ci=ci, t=t, c=c, half=half, qs=jnp.where(hm, q[:, sl], 0.0).astype(BF16),
                                                  dos=jnp.where(hm, do_[:, sl], 0.0).astype(BF16), lcol=l_[:, col:col + 1],
                                                  dcol=d_[:, col:col + 1], kmat=kcb, vmat=vcb,
                                                  kdq=jnp.where(hm, kc, 0.0).astype(BF16)))
            for u in units:
                u["s"] = lax.dot_general(u["qs"], u["kmat"], _DIMS["nt"], preferred_element_type=F32)
                u["dp"] = lax.dot_general(u["dos"], u["vmat"], _DIMS["nt"], preferred_element_type=F32)
            for u in units:
                p = jnp.exp(jnp.where(valids[u["t"]], u["s"], NEG_INF) - u["lcol"])
                u["ds"] = (p * (u["dp"] + u["dcol"])).astype(BF16)
                u["p"] = p.astype(BF16)
            for u in units:
                u["dv"] = lax.dot_general(u["p"], u["dos"], _DIMS["tn"], preferred_element_type=F32)
                u["dk"] = lax.dot_general(u["ds"], u["qs"], _DIMS["tn"], preferred_element_type=F32)
                u["dq"] = jnp.dot(u["ds"], u["kdq"], preferred_element_type=F32) * scale
            for ci, rows in enumerate(batch):
                mine = [u for u in units if u["ci"] == ci]
                dq = [[None] * npair for _ in range(ntile)]
                if gqa:
                    dk_out = dv_out = None
                    for hk in range(2):
                        us = [u for u in mine if u["hk"] == hk]
                        for u in us:
                            for i, c in enumerate(u["pairs"]):
                                dq[u["t"]][c] = _pick_halves(u["dq"][2 * i * QBLOCK:(2 * i + 1) * QBLOCK],
                                                             u["dq"][(2 * i + 1) * QBLOCK:(2 * i + 2) * QBLOCK])
                        dk_h = _fold_halves(functools.reduce(jnp.add, [u["dk"] for u in us]))
                        dv_h = _fold_halves(functools.reduce(jnp.add, [u["dv"] for u in us]))
                        dk_out = dk_h if hk == 0 else _pick_halves(dk_out, dk_h)
                        dv_out = dv_h if hk == 0 else _pick_halves(dv_out, dv_h)
                else:
                    dks, dvs = [], []
                    for c in range(npair):
                        us = [u for u in mine if u["c"] == c]
                        dks.append(functools.reduce(jnp.add, [u["dk"] for u in us]))
                        dvs.append(functools.reduce(jnp.add, [u["dv"] for u in us]))
                        for t in range(ntile):
                            dq[t][c] = functools.reduce(jnp.add, [u["dq"] for u in us if u["t"] == t])
                    dk_out, dv_out = cat(dks), cat(dvs)
                ck, sk_ = c_ref[0, rows, :], s_ref[0, rows, :]
                dk_new = _rope(dk_out, ck, sk_, sign=-1.0, mxu=gqa, coarse=True)
                dq_cur = cat(dq[0])
                if has_next:
                    dq_cur = dq_cur + carry_ref[rows, :]
                    carry_ref[rows, :] = cat(dq[1])
                dq_new = _rope(dq_cur, ck, sk_, sign=-1.0, mxu=gqa, coarse=True)
                if staged:
                    stage_q[rows, :], stage_k[rows, :], stage_v[rows, :] = dq_new, dk_new, dv_out
                else:
                    dq_ref[0], dk_ref[0], dv_ref[0] = dq_new.astype(BF16), dk_new.astype(BF16), dv_out.astype(BF16)
        if staged:
            dq_ref[0], dk_ref[0], dv_ref[0] = stage_q[...].astype(BF16), stage_k[...].astype(BF16), stage_v[...].astype(BF16)

    def at(width, col0, shift):
        return pl.BlockSpec((1, rr, width), lambda b, c, i: (b, jnp.minimum(i + shift, nblk - 1), col0 + c))

    in_specs = [at(kw, k_col, 0), at(kw, v_col, 0), pl.BlockSpec((1, rr, LANES), lambda b, c, i: (b, i, 0)),
                pl.BlockSpec((1, rr, LANES), lambda b, c, i: (b, i, 0))]
    args = [k_arr, v_arr, cos, sin]
    for shift in (0, 1) if has_next else (0,):
        in_specs += [at(qw, q_col, shift), at(qw, 0, shift), at(qw, 0, shift), at(qw, 0, shift)]
        args += [q_arr, do, lse, dd]
    if has_token:
        in_specs.append(pl.BlockSpec(token.shape, lambda b, c, i: (0, 0)))
        args.append(token)
    return pl.pallas_call(
        body,
        name=name,
        grid=(bsz, nchunk, nblk),
        in_specs=in_specs,
        out_specs=[pl.BlockSpec((1, rr, qw), lambda b, c, i: (b, i, c)),
                   pl.BlockSpec((1, rr, kw), lambda b, c, i: (b, i, c)),
                   pl.BlockSpec((1, rr, kw), lambda b, c, i: (b, i, c))],
        out_shape=[jax.ShapeDtypeStruct((bsz, seq, nchunk * qw), BF16),
                   jax.ShapeDtypeStruct((bsz, seq, nchunk * kw), BF16),
                   jax.ShapeDtypeStruct((bsz, seq, nchunk * kw), BF16)],
        scratch_shapes=[pltpu.VMEM((rr, qw) if has_next else (8, LANES), F32)] +
                       ([pltpu.VMEM((rr, qw), F32), pltpu.VMEM((rr, kw), F32), pltpu.VMEM((rr, kw), F32)] if staged else []),
        compiler_params=_params("parallel", "parallel", "arbitrary"),
    )(*args)


B_CHUNKS = {1: (4, 1), 4: (1, 4), 16: (1, 4)}


def _rope_tables(positions):
    half = HEAD_DIM // 2
    inv = ROPE_THETA ** (-jnp.arange(half, dtype=F32) / half)
    ang = positions.astype(F32)[..., None] * inv
    cos, sin = jnp.cos(ang), jnp.sin(ang)
    return jnp.concatenate([cos] * 4, axis=-1), jnp.concatenate([-sin, sin, -sin, sin], axis=-1)


def _layer_step(x, mod, tables, sinks, ln1_g, ln1_b, ln2_g, ln2_b, target, get_w_in, get_rest, hook):
    bsz, seq, d = x.shape
    ntok = bsz * seq
    flat = lambda v: v.reshape(ntok, v.shape[-1])
    unflat = lambda v: v.reshape(bsz, seq, v.shape[-1])
    cos, sin = tables
    mm = functools.partial(_matmul, tm=1024, tk=1024)
    scalar = lambda tok: 0.0 if tok is None else tok[0, 0]

    u1 = _modulate_in(x, mod)
    u1f = flat(u1)
    wint = get_w_in(u1)
    cosf, sinf = flat(cos), flat(sin)
    proj = functools.partial(_proj_rope, u1f, wint, cosf, sinf, tm=2048)
    qkvb = unflat(proj(n=4608, b_off=OFF_QKVB, rope_cols=3072, tn=256, name="proj_qkvb"))
    b_kws, os_, ls_ = [], [], []
    for g, (window, r) in enumerate(B_PATTERNS):
        npair, nch = B_CHUNKS[r]
        per = B_HEADS_PER_GROUP // (2 * npair)
        nsec = len(B_PATTERNS) * per
        kw_ = dict(npair=npair, gqa=False, q_col=g * per, k_col=nsec + g * per, v_col=2 * nsec + g * per, nchunk=nch, r=r,
                   n_back=window // r)
        b_kws.append(kw_)
        o_g, l_g = _attn_fwd(qkvb, qkvb, qkvb, name=f"attn_b{g}_fwd", **kw_)
        os_.append(o_g)
        ls_.append(l_g)
    ob = _merge_b(os_, ls_)
    tok = hook("projected", ob)
    u1t = u1f if tok is None else lax.optimization_barrier((u1f, tok))[0]
    proj = functools.partial(_proj_rope, u1t, wint, cosf, sinf, tm=2048)
    gab = unflat(proj(n=2048, b_off=OFF_GAB, rope_cols=0, tn=256, name="proj_gab", out_dtype=BF16))
    qa = unflat(proj(n=1024, b_off=OFF_QA, rope_cols=1024, tn=512, name="proj_qa"))
    kva = unflat(proj(n=256, b_off=OFF_KVA, rope_cols=128, tn=128, name="proj_kva"))
    a_kw = dict(npair=A_Q_HEADS // 2, gqa=True, q_col=0, k_col=0, v_col=1, nchunk=1, r=1, n_back=A_WINDOW - 1)
    oa, lse_a = _attn_fwd(qa, kva, kva, name="attn_a_fwd", sinks=sinks.reshape(A_Q_HEADS), **a_kw)
    rest = get_rest(oa)
    wba, wbbt, wo, wgut, wd = (rest[n] for n in ("w_branch_a", "w_branch_b", "w_o", "w_gate_up", "w_down"))
    ya = unflat(mm(flat(oa), wba, mode="nn", out_dtype=BF16, tn=512, name="branch_a"))
    ybf, mergedf = _branch_b_gate_merge(flat(ob), wbbt, flat(gab), flat(ya))
    xf = flat(x)
    y1f, h1f, u2f = _wo_ln1(mergedf, wo, xf, mod, ln1_g, ln1_b, seq)
    wgut_i = _interleave_gate_up(wgut)
    hf, af = _gate_up_silu(u2f, wgut_i)

    dy2f, dh1af, acc2 = _down_ln2_loss_bwd(af, wd, h1f, mod, ln2_g, ln2_b, flat(target), seq)
    g_wd = _matmul(af, dy2f, mode="tn", out_dtype=BF16, tm=256, tn=1024, tk=ntok, name="down_wgrad")
    dhf = _down_dgrad_silu_bwd(dy2f, wd, hf)
    g_wgut = _interleave_gate_up(_matmul(dhf, u2f, mode="tn", out_dtype=BF16, tm=256, tn=1024, tk=ntok, name="gate_up_wgrad"))
    dy1f, dxaf, acc1 = _gate_up_dgrad_ln1_bwd(dhf, wgut_i, dh1af, xf, y1f, mod, ln1_g, ln1_b, seq)
    g_wo = _matmul(mergedf, dy1f, mode="tn", out_dtype=BF16, tm=256, tn=1024, tk=ntok, name="w_o_wgrad")
    dyaf, dybf, dgaf, dgbf = _wo_dgrad_gate_bwd(dy1f, wo, flat(gab), flat(ya), ybf)
    g_wba = _matmul(flat(oa), dyaf, mode="tn", out_dtype=BF16, tm=256, tn=1024, tk=ntok, name="branch_a_wgrad")
    g_wbbt = _matmul(dybf, flat(ob), mode="tn", out_dtype=BF16, tm=256, tn=512, tk=ntok, name="branch_b_wgrad")
    tok = hook("grads_rest", dict(w_branch_a=g_wba, w_branch_b=g_wbbt, w_o=g_wo, w_gate_up=g_wgut, w_down=g_wd))

    sinks_exp = jnp.repeat(sinks.reshape(1, A_Q_HEADS), HEAD_DIM, axis=1) + scalar(tok)
    doa, dd_a, acc_s = _branch_a_dgrad_delta(dyaf, wba, flat(oa), flat(lse_a), sinks_exp, seq)
    doa, dd_a = unflat(doa), unflat(dd_a)
    tok = hook("delta_done", dd_a)
    dqa, dka, dva = _attn_bwd(qa, kva, kva, cos, sin, doa, lse_a, dd_a, name="attn_a_bwd", token=tok, **a_kw)
    merged_bwd = [unflat(t) for t in _branch_b_dgrad_merge_bwd(dybf, wbbt, [flat(t) for t in os_], [flat(t) for t in ls_])]
    dqs, dks, dvs = [], [], []
    for g in range(len(B_PATTERNS)):
        dq_g, dk_g, dv_g = _attn_bwd(qkvb, qkvb, qkvb, cos, sin, merged_bwd[g], ls_[g], merged_bwd[3 + g],
                                     name=f"attn_b{g}_bwd", **b_kws[g])
        dqs.append(dq_g)
        dks.append(dk_g)
        dvs.append(dv_g)
    dproj = jnp.concatenate([t.astype(BF16) for t in [dqa, dka, dva] + dqs + dks + dvs] + [unflat(dgaf), unflat(dgbf)], axis=-1)
    dprojf = flat(dproj)
    g_wint = _matmul(dprojf, u1f, mode="tn", out_dtype=BF16, tm=256, tn=1024, tk=ntok, name="w_in_wgrad")
    tok = hook("grads_w_in", dict(w_in=g_wint))
    grad_x, acc0 = _w_in_dgrad_grad_x(dprojf, wint, dxaf, xf, mod, seq, tok)
    grad_x = unflat(grad_x)
    tok = hook("dgrad_done", grad_x)

    loss_part = jnp.sum(acc2[:, 3, 0])
    dmod = jnp.stack([acc0[:, 1], acc0[:, 0], acc1[:, 2], acc1[:, 4], acc1[:, 3], acc2[:, 2]], axis=1)
    small = jnp.stack([acc1[:, 0].sum(0), acc1[:, 1].sum(0), acc2[:, 0].sum(0), acc2[:, 1].sum(0), acc_s[:, 0].sum(0)])
    small = small + scalar(tok)
    return loss_part, grad_x, dmod, small


CHIP_FLIPS = (2, 4, 6)


def _my_place():
    return lax.axis_index("x"), lax.axis_index("y"), lax.axis_index("c")


def _flip(place, k):
    px, py, pc = place
    return (1 - px if k & 4 else px, 1 - py if k & 2 else py, 1 - pc if k & 1 else pc)


def _index(place):
    return 4 * place[0] + 2 * place[1] + place[2]


def _gather_small(v, name):
    rows, cols = v.shape

    def body(v_ref, out_ref, send_sems, recv_sems):
        me = _my_place()
        out_ref[_index(me)] = v_ref[...]
        copies = []
        for k in range(1, N_DEV):
            copies.append(pltpu.make_async_remote_copy(
                src_ref=v_ref, dst_ref=out_ref.at[_index(me)], send_sem=send_sems.at[k - 1], recv_sem=recv_sems.at[k - 1],
                device_id=_flip(me, k), device_id_type=MESH))
        for cp in copies:
            cp.start()
        for k in range(1, N_DEV):
            pltpu.make_async_remote_copy(
                src_ref=v_ref, dst_ref=out_ref.at[_index(_flip(me, k))], send_sem=send_sems.at[k - 1],
                recv_sem=recv_sems.at[k - 1], device_id=_flip(me, k), device_id_type=MESH).wait_recv()
        for cp in copies:
            cp.wait_send()

    return pl.pallas_call(
        body,
        name=name,
        out_shape=jax.ShapeDtypeStruct((N_DEV, rows, cols), v.dtype),
        in_specs=[pl.BlockSpec(memory_space=pltpu.VMEM)],
        out_specs=pl.BlockSpec(memory_space=pltpu.VMEM),
        scratch_shapes=[pltpu.SemaphoreType.DMA((N_DEV - 1,)), pltpu.SemaphoreType.DMA((N_DEV - 1,))],
        compiler_params=pltpu.CompilerParams(vmem_limit_bytes=VMEM_LIMIT_BYTES),
    )(v)


_HBM = pl.BlockSpec(memory_space=pltpu.HBM)
_SEM = pl.BlockSpec(memory_space=pltpu.SEMAPHORE)
_EFFECT = pltpu.SideEffectType.DATAFLOW_SIDE_EFFECTING


def _remote(src, dst, send_sems, recv_sems, j, to):
    return pltpu.make_async_remote_copy(src_ref=src, dst_ref=dst, send_sem=send_sems.at[j], recv_sem=recv_sems.at[j],
                                        device_id=to, device_id_type=MESH)


def _copies_start(name, bufs, make_copies, nsem):
    nbuf = len(bufs)

    def body(*refs):
        for cp in make_copies(refs[:nbuf], refs[nbuf], refs[nbuf + 1]):
            cp.start()
        refs[-1][...] = jnp.zeros_like(refs[-1])

    sems = pltpu.SemaphoreType.DMA((nsem,))
    res = pl.pallas_call(
        body, name=name,
        out_shape=(sems, sems, *[pltpu.HBM(v.shape, v.dtype) for v in bufs], jax.ShapeDtypeStruct((8, LANES), F32)),
        in_specs=(_HBM,) * nbuf, out_specs=(_SEM, _SEM) + (_HBM,) * nbuf + (pl.BlockSpec(memory_space=pltpu.VMEM),),
        input_output_aliases={i: 2 + i for i in range(nbuf)},
        compiler_params=pltpu.CompilerParams(has_side_effects=_EFFECT),
    )(*[pltpu.with_memory_space_constraint(v, pltpu.HBM) for v in bufs])
    return res[0], res[1], list(res[2:2 + nbuf]), res[-1]


def _copies_wait(name, started, make_copies, after):
    send_sems, recv_sems, bufs, _ = started
    nbuf = len(bufs)

    def body(*refs):
        for cp in make_copies(refs[:nbuf], refs[nbuf], refs[nbuf + 1]):
            cp.wait_send()
            cp.wait_recv()

    return list(pl.pallas_call(
        body, name=name,
        out_shape=tuple(pltpu.HBM(v.shape, v.dtype) for v in bufs),
        in_specs=(_HBM,) * nbuf + (_SEM, _SEM, pl.BlockSpec(memory_space=pl.ANY)), out_specs=(_HBM,) * nbuf,
        input_output_aliases={i: i for i in range(nbuf)},
        compiler_params=pltpu.CompilerParams(has_side_effects=_EFFECT),
    )(*bufs, send_sems, recv_sems, after))


def _to_sibling_copies(refs, send_sems, recv_sems):
    src_ref, land_ref = refs
    me = _my_place()
    return [_remote(src_ref.at[q, 1 - me[2]], land_ref.at[q], send_sems, recv_sems, q, _flip(me, 1)) for q in range(4)]


def _to_chips_copies(refs, send_sems, recv_sems):
    src_ref, land_ref = refs
    me = _my_place()
    copies = []
    for j, k in enumerate(CHIP_FLIPS):
        to = _flip(me, k)
        copies.append(_remote(src_ref.at[2 * to[0] + to[1]], land_ref.at[j], send_sems, recv_sems, j, to))
    return copies


class _Gather:
    def __init__(self, name, blocks):
        self.name, self.n = name, len(blocks)
        at_me = (_index(_my_place()), 0, 0)
        lands = [lax.dynamic_update_slice(lax.empty((N_DEV,) + v.shape, v.dtype), v[None], at_me) for v in blocks]
        self.first = _copies_start(name + "_start", list(blocks) + lands, self._first_copies, 4 * self.n)
        self.token = self.first[3]

    def _first_copies(self, refs, send_sems, recv_sems):
        me = _my_place()
        return [_remote(refs[w], refs[self.n + w].at[_index(me)], send_sems, recv_sems, 4 * w + j, _flip(me, k))
                for w in range(self.n) for j, k in enumerate((1,) + CHIP_FLIPS)]

    def _pass_copies(self, refs, send_sems, recv_sems):
        me = _my_place()
        copies = []
        for w, land in enumerate(refs):
            for j, k in enumerate(CHIP_FLIPS):
                slot = land.at[_index(_flip(me, k))]
                copies.append(_remote(slot, slot, send_sems, recv_sems, 3 * w + j, _flip(me, 1)))
        return copies

    def pass_on(self, after):
        lands = _copies_wait(self.name + "_wait", self.first, self._first_copies, after)[self.n:]
        self.second = _copies_start(self.name + "_pass_start", lands, self._pass_copies, 3 * self.n)
        return self.second[3]

    def finish(self, after):
        return _copies_wait(self.name + "_pass_wait", self.second, self._pass_copies, after)


SUM_SPLIT = 2


def _sum_pairs(parts, theirs):
    nchip, _, rows, cols = parts.shape
    tile = rows // SUM_SPLIT

    def body(c_ref, a_ref, b_ref, o_ref):
        o_ref[...] = (a_ref[0].astype(F32) + b_ref[...].astype(F32)).astype(BF16)

    spec = pl.BlockSpec((1, tile, cols), lambda q, t, c_ref: (q, t, 0))
    grid_spec = pltpu.PrefetchScalarGridSpec(
        num_scalar_prefetch=1, grid=(nchip, SUM_SPLIT),
        in_specs=[pl.BlockSpec((1, 1, tile, cols), lambda q, t, c_ref: (q, c_ref[0], t, 0)), spec], out_specs=spec)
    return pl.pallas_call(body, name="grad_sum_sibling", grid_spec=grid_spec,
                          out_shape=jax.ShapeDtypeStruct((nchip, rows, cols), BF16),
                          compiler_params=_params("parallel", "parallel"))(lax.axis_index("c").reshape(1), parts, theirs)


def _sum_final(chip_sum, got):
    _, rows, cols = chip_sum.shape
    tile = rows // SUM_SPLIT

    def body(q_ref, a_ref, g_ref, o_ref):
        o_ref[...] = ((a_ref[0].astype(F32) + g_ref[0].astype(F32)) + g_ref[1].astype(F32)) + g_ref[2].astype(F32)

    grid_spec = pltpu.PrefetchScalarGridSpec(
        num_scalar_prefetch=1, grid=(SUM_SPLIT,),
        in_specs=[pl.BlockSpec((1, tile, cols), lambda t, q_ref: (q_ref[0], t, 0)),
                  pl.BlockSpec((3, tile, cols), lambda t, q_ref: (0, t, 0))],
        out_specs=pl.BlockSpec((tile, cols), lambda t, q_ref: (t, 0)))
    my_chip = (2 * lax.axis_index("x") + lax.axis_index("y")).reshape(1)
    return pl.pallas_call(body, name="grad_sum_chips", grid_spec=grid_spec, out_shape=jax.ShapeDtypeStruct((rows, cols), F32),
                          compiler_params=_params("parallel"))(my_chip, chip_sum, got)


class _ReduceScatter:
    def __init__(self, name, slabs):
        self.name, self.rows = name, slabs.shape[1]
        parts = slabs.reshape(4, 2, self.rows, D_MODEL)
        self.first = _copies_start(name + "_sibling_start", [parts, lax.empty((4, self.rows, D_MODEL), slabs.dtype)],
                                   _to_sibling_copies, 4)
        self.token = self.first[3]

    def between_chips(self, after):
        parts, theirs = _copies_wait(self.name + "_sibling_wait", self.first, _to_sibling_copies, after)
        chip_sum = _sum_pairs(parts, theirs)
        self.second = _copies_start(self.name + "_chips_start", [chip_sum, lax.empty((3, self.rows, D_MODEL), chip_sum.dtype)],
                                    _to_chips_copies, 3)
        return self.second[3]

    def finish(self, after):
        chip_sum, got = _copies_wait(self.name + "_chips_wait", self.second, _to_chips_copies, after)
        return _sum_final(chip_sum, got)


def _ada_fwd(c_all, w, b):
    nb, _ = c_all.shape
    ncol = w.shape[1]

    def body(c_ref, w_ref, b_ref, o_ref):
        c = c_ref[...]
        act = (c * _sigmoid(c)).astype(BF16)
        o_ref[...] = jnp.dot(act, w_ref[...].astype(BF16), preferred_element_type=F32) + b_ref[...]

    return pl.pallas_call(body, name="ada_fwd", out_shape=jax.ShapeDtypeStruct((nb, ncol), F32),
                          compiler_params=pltpu.CompilerParams(vmem_limit_bytes=VMEM_LIMIT_BYTES))(c_all, w, b)


def _ada_wgrad(c_all_t, dmod_cols):
    d, nb = c_all_t.shape
    ncol = dmod_cols.shape[1]

    def body(ct_ref, dm_ref, o_ref):
        ct = ct_ref[...]
        act = (ct * _sigmoid(ct)).astype(BF16).astype(F32)
        dm = dm_ref[...].astype(BF16).astype(F32)
        acc = act[:, 0:1] * dm[0:1, :]
        for i in range(1, nb):
            acc = acc + act[:, i:i + 1] * dm[i:i + 1, :]
        o_ref[...] = acc

    return pl.pallas_call(body, name="ada_wgrad", out_shape=jax.ShapeDtypeStruct((d, ncol), F32),
                          compiler_params=pltpu.CompilerParams(vmem_limit_bytes=VMEM_LIMIT_BYTES))(c_all_t, dmod_cols)


SMALL_ROWS = 24


def _reduce_small(gathered):
    def body(g_ref, o_ref):
        acc = g_ref[0]
        for dev in range(1, N_DEV):
            acc = acc + g_ref[dev]
        o_ref[...] = acc

    return pl.pallas_call(body, name="reduce_small", out_shape=jax.ShapeDtypeStruct(gathered.shape[1:], F32))(gathered)


def _adamw(w, g, m, v, name):
    rows, cols = w.shape
    tile = rows
    for cand in range(min(rows // 2, 512) // 8 * 8, 7, -8):
        if rows % cand == 0:
            tile = cand
            break
    spec = pl.BlockSpec((tile, cols), lambda t: (t, 0))
    bc1 = 1.0 - ADAM_B1 ** ADAM_STEP
    bc2 = 1.0 - ADAM_B2 ** ADAM_STEP

    def body(w_ref, g_ref, m_ref, v_ref, d_ref, nm_ref, nv_ref):
        g_ = g_ref[...]
        nm = ADAM_B1 * m_ref[...] + (1.0 - ADAM_B1) * g_
        nv = ADAM_B2 * v_ref[...] + (1.0 - ADAM_B2) * (g_ * g_)
        d_ref[...] = -ADAM_LR * ((nm / bc1) / (jnp.sqrt(nv / bc2) + ADAM_EPS) + ADAM_WD * w_ref[...])
        nm_ref[...] = nm
        nv_ref[...] = nv

    shp = jax.ShapeDtypeStruct((rows, cols), F32)
    return pl.pallas_call(body, name=name, grid=(rows // tile,), in_specs=[spec] * 4, out_specs=[spec] * 3, out_shape=[shp] * 3,
                          compiler_params=_params("parallel"))(w, g, m, v)


_WEIGHTS = ("w_ada", "b_ada", "w_in", "sinks", "w_branch_a", "w_branch_b", "w_o", "ln1_g", "ln1_b", "w_gate_up", "w_down",
            "ln2_g", "ln2_b")
_TRANSPOSED = ("w_in", "w_branch_b", "w_gate_up")


def _pack_shard(name, w):
    w = w.astype(BF16)
    if name in _TRANSPOSED:
        w = w.T
    return w.reshape(-1, D_MODEL)


def _unpack_full(name, slab):
    if name == "w_branch_b":
        return slab.reshape(N_DEV * 128, 512)
    return slab.reshape(-1, D_MODEL)


def _unpack_group(group, gathered):
    return {n: _unpack_full(n, slab) for (n, _), slab in zip(group, gathered)}


def _unpack_grads(group, g_packed):
    g_w, off = {}, 0
    for n, r in group:
        part = g_packed[off:off + r]
        off += r
        g_w[n] = part.reshape(128, 512) if n == "w_branch_b" else part
    return g_w


def kernel(x, c, positions, w_ada, b_ada, w_in, sinks, w_branch_a, w_branch_b, w_o, ln1_g, ln1_b, w_gate_up, w_down, ln2_g, ln2_b, loss_target, m_w_ada, m_b_ada, m_w_in, m_sinks, m_w_branch_a, m_w_branch_b, m_w_o, m_ln1_g, m_ln1_b, m_w_gate_up, m_w_down, m_ln2_g, m_ln2_b, v_w_ada, v_b_ada, v_w_in, v_sinks, v_w_branch_a, v_w_branch_b, v_w_o, v_ln1_g, v_ln1_b, v_w_gate_up, v_w_down, v_ln2_g, v_ln2_b):
    weights = dict(w_ada=w_ada, b_ada=b_ada, w_in=w_in, sinks=sinks, w_branch_a=w_branch_a, w_branch_b=w_branch_b, w_o=w_o,
                   ln1_g=ln1_g, ln1_b=ln1_b, w_gate_up=w_gate_up, w_down=w_down, ln2_g=ln2_g, ln2_b=ln2_b)
    m_in = dict(w_ada=m_w_ada, b_ada=m_b_ada, w_in=m_w_in, sinks=m_sinks, w_branch_a=m_w_branch_a, w_branch_b=m_w_branch_b,
                w_o=m_w_o, ln1_g=m_ln1_g, ln1_b=m_ln1_b, w_gate_up=m_w_gate_up, w_down=m_w_down, ln2_g=m_ln2_g, ln2_b=m_ln2_b)
    v_in = dict(w_ada=v_w_ada, b_ada=v_b_ada, w_in=v_w_in, sinks=v_sinks, w_branch_a=v_w_branch_a, w_branch_b=v_w_branch_b,
                w_o=v_w_o, ln1_g=v_ln1_g, ln1_b=v_ln1_b, w_gate_up=v_w_gate_up, w_down=v_w_down, ln2_g=v_ln2_g, ln2_b=v_ln2_b)
    bsz = x.shape[0]
    me = _index(_my_place())
    ada_cols = w_ada.shape[2]
    outs = {}

    def adamw(n, g):
        w2, m2, v2 = (t[n][0] if t[n].ndim == 3 else t[n] for t in (weights, m_in, v_in))
        shape = weights[n].shape
        if n in _TRANSPOSED:
            dlt, nm, nv = _adamw(w2.T, g, m2.T, v2.T, "adamw_" + n)
            outs[n] = tuple(t.T.reshape(shape) for t in (g, dlt, nm, nv))
        else:
            dlt, nm, nv = _adamw(w2, g, m2, v2, "adamw_" + n)
            outs[n] = tuple(t.reshape(shape) for t in (g, dlt, nm, nv))
        return nv

    packed_in = [_pack_shard(n, weights[n][0]) for n, _ in GROUP_IN]
    packed_rest = [_pack_shard(n, weights[n][0]) for n, _ in GROUP_REST]
    c_all = _gather_small(jnp.pad(c, ((0, 8 - bsz), (0, 0))), "gather_c")[:, :bsz].reshape(N_DEV * bsz, D_MODEL)
    gather_in = _Gather("gather_w_in", lax.optimization_barrier((packed_in, c_all))[0])
    b_cols = lax.dynamic_slice_in_dim(b_ada, me * ada_cols, ada_cols, axis=1)
    mod_cols = _ada_fwd(c_all, w_ada[0], b_cols + gather_in.token[0, 0])
    tables = _rope_tables(positions)
    mod_cols, tables, packed_rest = lax.optimization_barrier((mod_cols, tables, packed_rest))
    mod_all = _gather_small(mod_cols, "gather_mod").transpose(1, 0, 2).reshape(N_DEV * bsz, 6, D_MODEL)
    gather_rest = _Gather("gather_rest", lax.optimization_barrier((packed_rest, mod_all))[0])
    mod = jnp.pad(lax.dynamic_slice_in_dim(mod_all, me * bsz, bsz, axis=0), ((0, 0), (0, 2), (0, 0)))
    mod = mod + gather_rest.token[0, 0]
    mod = mod + gather_in.pass_on(mod)[0, 0]

    scatters = {}

    def get_w_in(after):
        return _unpack_group(GROUP_IN, gather_in.finish(after))["w_in"]

    def get_rest(after):
        return _unpack_group(GROUP_REST, gather_rest.finish(after))

    def pack_grads(group, grads):
        return jnp.concatenate([grads[n].reshape(N_DEV, r, D_MODEL) for n, r in group], axis=1)

    def hook(point, value):
        if point == "projected":
            return gather_rest.pass_on(value)
        if point == "grads_rest":
            scatters["rest"] = _ReduceScatter("scatter_rest", pack_grads(GROUP_REST, value))
            return scatters["rest"].token
        if point == "delta_done":
            return scatters["rest"].between_chips(value)
        if point == "grads_w_in":
            scatters["in"] = _ReduceScatter("scatter_w_in", pack_grads(GROUP_IN, value))
            return scatters["in"].token
        if point == "dgrad_done":
            return None
        raise ValueError(point)

    loss_part, grad_x, dmod, small = _layer_step(x, mod, tables, sinks[0], ln1_g, ln1_b, ln2_g, ln2_b, loss_target,
                                                 get_w_in, get_rest, hook)

    rows = jnp.concatenate([dmod.reshape(bsz * 6, D_MODEL), small, jnp.full((1, D_MODEL), loss_part, F32),
                            jnp.zeros((SMALL_ROWS - bsz * 6 - 6, D_MODEL), F32)], axis=0)
    small_all = _gather_small(rows, "gather_small")
    small_all = small_all + scatters["in"].between_chips(small_all)[0, 0]
    sums = _reduce_small(small_all)
    loss = sums[bsz * 6 + 5, 0]
    dmod_all = small_all[:, :bsz * 6].reshape(N_DEV * bsz, 6 * D_MODEL)
    adamw("b_ada", functools.reduce(jnp.add, [sums[6 * i:6 * i + 6] for i in range(bsz)]).reshape(1, 6 * D_MODEL))
    for i, n in enumerate(("ln1_g", "ln1_b", "ln2_g", "ln2_b")):
        adamw(n, sums[12 + i][None])
    adamw("sinks", sums[16][::HEAD_DIM][None])
    dmod_cols = lax.dynamic_slice_in_dim(dmod_all, me * ada_cols, ada_cols, axis=1)
    last = adamw("w_ada", _ada_wgrad(c_all.T, dmod_cols))
    for n, g in _unpack_grads(GROUP_REST, scatters["rest"].finish(last)).items():
        adamw(n, g)
    done = lax.optimization_barrier(tuple(outs[n][3] for n in outs))
    for n, g in _unpack_grads(GROUP_IN, scatters["in"].finish(done[0])).items():
        adamw(n, g)

    return (loss, grad_x, *[outs[n][0] for n in _WEIGHTS], *[outs[n][1] for n in _WEIGHTS], *[outs[n][2] for n in _WEIGHTS],
            *[outs[n][3] for n in _WEIGHTS])
```

```python
import functools

import jax
import jax.numpy as jnp
from jax import lax
from jax.experimental import pallas as pl
from jax.experimental.pallas import tpu as pltpu

F32 = jnp.float32
BF16 = jnp.bfloat16

D_MODEL = 1024
HEAD_DIM = 64
A_Q_HEADS = 16
A_WINDOW = 128
B_PATTERNS = ((128, 1), (512, 4), (2048, 16))
B_HEADS_PER_GROUP = 8
D_FF = 2816
QBLOCK = 128
ROPE_THETA = 10000.0
LN_EPS = 1e-5
DEEPNORM_ALPHA = 2.0 ** 0.25
NEG_INF = -1e30
ADAM_LR, ADAM_B1, ADAM_B2, ADAM_EPS, ADAM_WD, ADAM_STEP = 0.001, 0.9, 0.999, 1e-08, 0.01, 10

N_DEV = 8
MESH_AXES = ("x", "y", "c")
LANES = 128
VMEM_LIMIT_BYTES = 56 * 1024 * 1024
MESH = pl.DeviceIdType.MESH

OFF_QA, OFF_KVA, OFF_QKVB, OFF_GAB = 0, 1024, 1280, 5888
GROUP_IN = (("w_in", 992),)
GROUP_REST = (("w_branch_a", 128), ("w_branch_b", 64), ("w_o", 128), ("w_gate_up", 704), ("w_down", 352))


def _params(*sem):
    return pltpu.CompilerParams(dimension_semantics=sem, vmem_limit_bytes=VMEM_LIMIT_BYTES)


def _sigmoid(x):
    return 1.0 / (1.0 + jnp.exp(-x))


_DIMS = {"nn": (((1,), (0,)), ((), ())), "nt": (((1,), (1,)), ((), ())), "tn": (((0,), (0,)), ((), ()))}


def _matmul(a, b, *, mode, tm, tn, tk, name, out_dtype=None, n=None, b_off=0, token=None, ins=(), outs=None, epilogue=None):
    if mode == "nn":
        (m, k), nn_ = a.shape, b.shape[1]
    elif mode == "nt":
        (m, k), nn_ = a.shape, (b.shape[0] if n is None else n)
    else:
        (k, m), nn_ = a.shape, b.shape[1]
    assert m % tm == 0 and nn_ % tn == 0 and k % tk == 0 and b_off % tn == 0, (name, m, nn_, k)
    nk = k // tk
    joff = b_off // tn
    if mode == "nn":
        a_spec = pl.BlockSpec((tm, tk), lambda i, j, kk: (i, kk))
        b_spec = pl.BlockSpec((tk, tn), lambda i, j, kk: (kk, j))
    elif mode == "nt":
        a_spec = pl.BlockSpec((tm, tk), lambda i, j, kk: (i, kk))
        b_spec = pl.BlockSpec((tn, tk), lambda i, j, kk: (j + joff, kk))
    else:
        a_spec = pl.BlockSpec((tk, tm), lambda i, j, kk: (kk, i))
        b_spec = pl.BlockSpec((tk, tn), lambda i, j, kk: (kk, j))
    dims = _DIMS[mode]
    has_token = token is not None
    plain = epilogue is None
    if plain:
        outs = [(jax.ShapeDtypeStruct((m, nn_), out_dtype), (tm, tn), lambda i, j: (i, j))]

        def epilogue(acc, i, j, in_refs, out_refs):
            out_refs[0][...] = acc.astype(out_refs[0].dtype)

    nin = len(ins)

    def body(*refs):
        a_ref, b_ref = refs[:2]
        in_refs = refs[2:2 + nin]
        out_refs = refs[2 + nin + has_token:-1]
        acc_ref = refs[-1]
        kk = pl.program_id(2)
        part = lax.dot_general(a_ref[...].astype(BF16), b_ref[...].astype(BF16), dims, preferred_element_type=F32)

        def finish(acc):
            epilogue(acc, pl.program_id(0), pl.program_id(1), in_refs, out_refs)

        if nk == 1:
            finish(part)
        else:
            @pl.when(kk == 0)
            def _():
                acc_ref[...] = part

            @pl.when(kk > 0)
            def _():
                acc_ref[...] += part

            @pl.when(kk == nk - 1)
            def _():
                finish(acc_ref[...])

    def spec(block, index):
        return pl.BlockSpec(block, lambda i, j, kk: index(i, j))

    in_specs, args = [a_spec, b_spec], [a, b]
    for arr, block, index in ins:
        in_specs.append(spec(block, index))
        args.append(arr)
    if has_token:
        in_specs.append(pl.BlockSpec(token.shape, lambda i, j, kk: (0, 0)))
        args.append(token)
    res = pl.pallas_call(
        body,
        name=name,
        grid=(m // tm, nn_ // tn, nk),
        in_specs=in_specs,
        out_specs=[spec(block, index) for _, block, index in outs],
        out_shape=[shape for shape, _, _ in outs],
        scratch_shapes=[pltpu.VMEM((tm, tn) if nk > 1 else (8, LANES), F32)],
        compiler_params=_params("arbitrary", "arbitrary", "arbitrary"),
    )(*args)
    return res[0] if plain else res


def _proj_rope(a, bt, cos, sin, *, n, b_off, rope_cols, tm, tn, name, out_dtype=F32):
    m, k = a.shape
    assert m % tm == 0 and n % tn == 0 and b_off % tn == 0 and rope_cols % tn == 0, name
    joff = b_off // tn
    nrope = rope_cols // tn

    def body(a_ref, b_ref, c_ref, s_ref, o_ref):
        acc = lax.dot_general(a_ref[...], b_ref[...], _DIMS["nt"], preferred_element_type=F32)
        j = pl.program_id(1)

        @pl.when(j < nrope)
        def _():
            o_ref[...] = _rope(acc, c_ref[...], s_ref[...], coarse=True).astype(o_ref.dtype)

        @pl.when(j >= nrope)
        def _():
            o_ref[...] = acc.astype(o_ref.dtype)

    table = pl.BlockSpec((tm, LANES), lambda i, j: (i, 0))
    return pl.pallas_call(
        body,
        name=name,
        grid=(m // tm, n // tn),
        in_specs=[pl.BlockSpec((tm, k), lambda i, j: (i, 0)), pl.BlockSpec((tn, k), lambda i, j: (j + joff, 0)), table, table],
        out_specs=pl.BlockSpec((tm, tn), lambda i, j: (i, j)),
        out_shape=jax.ShapeDtypeStruct((m, n), out_dtype),
        compiler_params=_params("parallel", "parallel"),
    )(a, bt, cos, sin)


ROW_TILE = 256


def _rows(width, col=0):
    return pl.BlockSpec((1, ROW_TILE, width), lambda b, t: (b, t, col))


def _per_batch(nrows, width):
    return pl.BlockSpec((1, nrows, width), lambda b, t: (b, 0, 0))


def _whole(shape):
    return pl.BlockSpec(shape, lambda b, t: (0,) * len(shape))


def _row_call(body, name, bsz, seq, in_specs, out_specs, out_shape, accumulates=False):
    return pl.pallas_call(
        body,
        name=name,
        grid=(bsz, seq // ROW_TILE),
        in_specs=in_specs,
        out_specs=out_specs,
        out_shape=out_shape,
        compiler_params=_params("parallel", "arbitrary" if accumulates else "parallel"),
    )


def _acc_rows(acc_ref, first, rows):
    @pl.when(first)
    def _():
        acc_ref[...] = jnp.zeros_like(acc_ref)

    for r, val in enumerate(rows):
        acc_ref[0, r:r + 1, :] += val


def _colsum(v):
    return jnp.sum(v, axis=0, keepdims=True)


def _ln_stats(z):
    mu = jnp.mean(z, axis=-1, keepdims=True)
    zc = z - mu
    var = jnp.mean(zc * zc, axis=-1, keepdims=True)
    rstd = lax.rsqrt(var + LN_EPS)
    return zc * rstd, rstd


def _ln_bwd(dxhat, xhat, rstd):
    m1 = jnp.mean(dxhat, axis=-1, keepdims=True)
    m2 = jnp.mean(dxhat * xhat, axis=-1, keepdims=True)
    return rstd * (dxhat - m1 - xhat * m2)


def _modulate_in(x, mod):
    bsz, seq, d = x.shape

    def body(x_ref, mod_ref, u_ref):
        u_ref[0] = (x_ref[0] * (1.0 + mod_ref[0, 1:2, :]) + mod_ref[0, 0:1, :]).astype(BF16)

    return _row_call(body, "modulate_in", bsz, seq, [_rows(d), _per_batch(8, d)], _rows(d),
                     jax.ShapeDtypeStruct((bsz, seq, d), BF16))(x, mod)


def _gate_merge(gab, ya, yb):
    bsz, seq, d = ya.shape

    def body(ga_ref, gb_ref, ya_ref, yb_ref, o_ref):
        ga, gb, ya_, yb_ = (r[0].astype(F32) for r in (ga_ref, gb_ref, ya_ref, yb_ref))
        o_ref[0] = (_sigmoid(ga) * ya_ + _sigmoid(gb) * yb_).astype(BF16)

    return _row_call(body, "gate_merge", bsz, seq, [_rows(d, 0), _rows(d, 1), _rows(d), _rows(d)], _rows(d),
                     jax.ShapeDtypeStruct((bsz, seq, d), BF16))(gab, gab, ya, yb)


EP_TILE = 512


def _ep_specs(seq, d):
    tiles = seq // EP_TILE
    return ((EP_TILE, d), lambda i, j: (i, 0)), ((1, 8, d), lambda i, j: (i // tiles, 0, 0)), ((1, d), lambda i, j: (0, 0))


def _wo_ln1(merged, wo, x, mod, g, b, seq):
    ntok, d = x.shape
    row, per_b, whole = _ep_specs(seq, d)

    def epilogue(y, i, j, ins, outs):
        x_ref, mod_ref, g_ref, b_ref = ins
        y_ref, h_ref, u_ref = outs
        z = DEEPNORM_ALPHA * x_ref[...] + (1.0 + mod_ref[0, 2:3, :]) * y
        xhat, _ = _ln_stats(z)
        h = xhat * g_ref[...] + b_ref[...]
        y_ref[...] = y
        h_ref[...] = h
        u_ref[...] = (h * (1.0 + mod_ref[0, 4:5, :]) + mod_ref[0, 3:4, :]).astype(BF16)

    f32, bf16 = jax.ShapeDtypeStruct((ntok, d), F32), jax.ShapeDtypeStruct((ntok, d), BF16)
    return _matmul(merged, wo, mode="nn", tm=EP_TILE, tn=d, tk=d, name="w_o_ln1",
                   ins=[(x,) + row, (mod,) + per_b, (g,) + whole, (b,) + whole],
                   outs=[(f32,) + row, (f32,) + row, (bf16,) + row], epilogue=epilogue)


FF_HALF = D_FF // 2


def _interleave_gate_up(w):
    return w.reshape(2, 2, FF_HALF, w.shape[1]).transpose(1, 0, 2, 3).reshape(w.shape)


def _gate_up_silu(u2, wgut_i):
    ntok = u2.shape[0]

    def epilogue(h, i, j, ins, outs):
        h_ref, a_ref = outs
        hg, hu = h[:, :FF_HALF], h[:, FF_HALF:]
        h_ref[...] = h.astype(BF16)
        a_ref[...] = (hg * _sigmoid(hg) * hu).astype(BF16)

    return _matmul(u2, wgut_i, mode="nt", tm=EP_TILE, tn=2 * FF_HALF, tk=u2.shape[1], name="gate_up_silu",
                   outs=[(jax.ShapeDtypeStruct((ntok, 2 * D_FF), BF16), (EP_TILE, 2 * FF_HALF), lambda i, j: (i, j)),
                         (jax.ShapeDtypeStruct((ntok, D_FF), BF16), (EP_TILE, FF_HALF), lambda i, j: (i, j))],
                   epilogue=epilogue)


def _down_dgrad_silu_bwd(dy2, wd, h_i):
    ntok = dy2.shape[0]
    wide = ((EP_TILE, 2 * FF_HALF), lambda i, j: (i, j))

    def epilogue(da, i, j, ins, outs):
        h = ins[0][...].astype(F32)
        hg, hu = h[:, :FF_HALF], h[:, FF_HALF:]
        sg = _sigmoid(hg)
        outs[0][:, :FF_HALF] = (da * hu * (sg * (1.0 + hg * (1.0 - sg)))).astype(BF16)
        outs[0][:, FF_HALF:] = (da * (hg * sg)).astype(BF16)

    return _matmul(dy2, wd, mode="nt", tm=EP_TILE, tn=FF_HALF, tk=dy2.shape[1], name="down_dgrad_silu_bwd",
                   ins=[(h_i,) + wide], outs=[(jax.ShapeDtypeStruct((ntok, 2 * D_FF), BF16),) + wide], epilogue=epilogue)[0]


def _down_ln2_loss_bwd(a, wd, h1, mod, g, b, target, seq):
    ntok, d = h1.shape
    row, per_b, whole = _ep_specs(seq, d)
    tiles = seq // EP_TILE

    def epilogue(y, i, j, ins, outs):
        h_ref, mod_ref, g_ref, b_ref, t_ref = ins
        dy_ref, dh_ref, acc_ref = outs
        gate = 1.0 + mod_ref[0, 5:6, :]
        z = DEEPNORM_ALPHA * h_ref[...] + gate * y
        xhat, rstd = _ln_stats(z)
        diff = xhat * g_ref[...] + b_ref[...] - t_ref[...]
        loss = 0.5 * jnp.sum(jnp.sum(diff * diff, axis=-1, keepdims=True) / d, axis=0, keepdims=True)
        dout = diff / d
        dz = _ln_bwd(dout * g_ref[...], xhat, rstd)
        dy_ref[...] = (gate * dz).astype(BF16)
        dh_ref[...] = DEEPNORM_ALPHA * dz
        _acc_rows(acc_ref, i % tiles == 0,
                  [_colsum(dout * xhat), _colsum(dout), _colsum(dz * y), jnp.broadcast_to(loss, (1, d))])

    return _matmul(a, wd, mode="nn", tm=EP_TILE, tn=d, tk=a.shape[1], name="down_ln2_loss_bwd",
                   ins=[(h1,) + row, (mod,) + per_b, (g,) + whole, (b,) + whole, (target,) + row],
                   outs=[(jax.ShapeDtypeStruct((ntok, d), BF16),) + row, (jax.ShapeDtypeStruct((ntok, d), F32),) + row,
                         (jax.ShapeDtypeStruct((ntok // seq, 8, d), F32),) + per_b], epilogue=epilogue)


def _silu_mul_bwd(da, h):
    bsz, seq, _ = h.shape

    def body(da_ref, hg_ref, hu_ref, dh_ref):
        hg, da_ = hg_ref[0].astype(F32), da_ref[0].astype(F32)
        sg = _sigmoid(hg)
        dh_ref[0, :, :D_FF] = (da_ * hu_ref[0].astype(F32) * (sg * (1.0 + hg * (1.0 - sg)))).astype(BF16)
        dh_ref[0, :, D_FF:] = (da_ * (hg * sg)).astype(BF16)

    return _row_call(body, "silu_mul_bwd", bsz, seq, [_rows(D_FF), _rows(D_FF, 0), _rows(D_FF, 1)], _rows(2 * D_FF),
                     jax.ShapeDtypeStruct((bsz, seq, 2 * D_FF), BF16))(da, h, h)


def _gate_up_dgrad_ln1_bwd(dh, wgut, dh1a, x, y1, mod, g, b, seq):
    ntok, d = x.shape
    row, per_b, whole = _ep_specs(seq, d)
    tiles = seq // EP_TILE

    def epilogue(du, i, j, ins, outs):
        dh_ref, x_ref, y_ref, mod_ref, g_ref, b_ref = ins
        dy_ref, dx_ref, acc_ref = outs
        y = y_ref[...]
        gate = 1.0 + mod_ref[0, 2:3, :]
        z = DEEPNORM_ALPHA * x_ref[...] + gate * y
        xhat, rstd = _ln_stats(z)
        h1 = xhat * g_ref[...] + b_ref[...]
        dh1 = dh_ref[...] + du * (1.0 + mod_ref[0, 4:5, :])
        dz = _ln_bwd(dh1 * g_ref[...], xhat, rstd)
        dy_ref[...] = (gate * dz).astype(BF16)
        dx_ref[...] = DEEPNORM_ALPHA * dz
        _acc_rows(acc_ref, i % tiles == 0,
                  [_colsum(dh1 * xhat), _colsum(dh1), _colsum(dz * y), _colsum(du * h1), _colsum(du)])

    return _matmul(dh, wgut, mode="nn", tm=EP_TILE, tn=d, tk=D_FF, name="gate_up_dgrad_ln1_bwd",
                   ins=[(dh1a,) + row, (x,) + row, (y1,) + row, (mod,) + per_b, (g,) + whole, (b,) + whole],
                   outs=[(jax.ShapeDtypeStruct((ntok, d), BF16),) + row, (jax.ShapeDtypeStruct((ntok, d), F32),) + row,
                         (jax.ShapeDtypeStruct((ntok // seq, 8, d), F32),) + per_b], epilogue=epilogue)


def _wo_dgrad_gate_bwd(dy1, wo, gab, ya, yb):
    ntok, d = ya.shape
    tm, tn = 1024, 512
    tile = ((tm, tn), lambda i, j: (i, j))
    tile_b = ((tm, tn), lambda i, j: (i, j + d // tn))

    def epilogue(dm_, i, j, ins, outs):
        ga_ref, gb_ref, ya_ref, yb_ref = ins
        dya_ref, dyb_ref, dga_ref, dgb_ref = outs
        sa, sb = _sigmoid(ga_ref[...].astype(F32)), _sigmoid(gb_ref[...].astype(F32))
        dya_ref[...] = (dm_ * sa).astype(BF16)
        dyb_ref[...] = (dm_ * sb).astype(BF16)
        dga_ref[...] = (dm_ * ya_ref[...].astype(F32) * sa * (1.0 - sa)).astype(BF16)
        dgb_ref[...] = (dm_ * yb_ref[...].astype(F32) * sb * (1.0 - sb)).astype(BF16)

    shp = jax.ShapeDtypeStruct((ntok, d), BF16)
    return _matmul(dy1, wo, mode="nt", tm=tm, tn=tn, tk=d, name="w_o_dgrad_gate_bwd",
                   ins=[(gab,) + tile, (gab,) + tile_b, (ya,) + tile, (yb,) + tile],
                   outs=[(shp,) + tile] * 4, epilogue=epilogue)


def _w_in_dgrad_grad_x(dproj, wint, dxa, x, mod, seq, token):
    ntok, d = x.shape
    row, per_b, _ = _ep_specs(seq, d)
    tiles = seq // EP_TILE

    def epilogue(du, i, j, ins, outs):
        dxa_ref, x_ref, mod_ref = ins
        gx_ref, acc_ref = outs
        gx_ref[...] = dxa_ref[...] + du * (1.0 + mod_ref[0, 1:2, :])
        _acc_rows(acc_ref, i % tiles == 0, [_colsum(du * x_ref[...]), _colsum(du)])

    return _matmul(dproj, wint, mode="nn", tm=EP_TILE, tn=d, tk=wint.shape[0] // 2, name="w_in_dgrad_grad_x", token=token,
                   ins=[(dxa,) + row, (x,) + row, (mod,) + per_b],
                   outs=[(jax.ShapeDtypeStruct((ntok, d), F32),) + row, (jax.ShapeDtypeStruct((ntok // seq, 8, d), F32),) + per_b],
                   epilogue=epilogue)


def _branch_b_gate_merge(ob, wbbt, gab, ya):
    ntok, d = ya.shape
    tm, tn = 1024, 512
    tile = ((tm, tn), lambda i, j: (i, j))
    tile_b = ((tm, tn), lambda i, j: (i, j + d // tn))

    def epilogue(yb, i, j, ins, outs):
        ga_ref, gb_ref, ya_ref = ins
        yb_ref, merged_ref = outs
        yb_ref[...] = yb.astype(BF16)
        merged_ref[...] = (_sigmoid(ga_ref[...].astype(F32)) * ya_ref[...].astype(F32)
                           + _sigmoid(gb_ref[...].astype(F32)) * yb).astype(BF16)

    shp = jax.ShapeDtypeStruct((ntok, d), BF16)
    return _matmul(ob, wbbt, mode="nt", tm=tm, tn=tn, tk=ob.shape[1], name="branch_b_gate_merge",
                   ins=[(gab,) + tile, (gab,) + tile_b, (ya,) + tile], outs=[(shp,) + tile] * 2, epilogue=epilogue)


def _segsum64(v):
    rows, width = v.shape
    ri = lax.broadcasted_iota(jnp.int32, (LANES, LANES), 0) // HEAD_DIM
    ci = lax.broadcasted_iota(jnp.int32, (LANES, LANES), 1) // HEAD_DIM
    ones = jnp.where(ri == ci, 1.0, 0.0).astype(BF16)
    out = []
    for c in range(width // LANES):
        part = v[:, c * LANES:(c + 1) * LANES]
        hi = part.astype(BF16)
        lo = (part - hi.astype(F32)).astype(BF16)
        out.append(jnp.dot(hi, ones, preferred_element_type=F32) + jnp.dot(lo, ones, preferred_element_type=F32))
    return jnp.concatenate(out, axis=1) if len(out) > 1 else out[0]


def _merge_b(os_, ls_):
    bsz, seq, w = os_[0].shape

    def body(o0, o1, o2, l0, l1, l2, ob_ref):
        ls = [l0[0], l1[0], l2[0]]
        mx = jnp.maximum(jnp.maximum(ls[0], ls[1]), ls[2])
        es = [jnp.exp(l - mx) for l in ls]
        den = es[0] + es[1] + es[2]
        ob_ref[0] = ((es[0] / den) * o0[0] + (es[1] / den) * o1[0] + (es[2] / den) * o2[0]).astype(BF16)

    return _row_call(body, "merge_b", bsz, seq, [_rows(w)] * 6, _rows(w),
                     jax.ShapeDtypeStruct((bsz, seq, w), BF16))(*os_, *ls_)


def _branch_b_dgrad_merge_bwd(dyb, wbbt, os_, ls_):
    ntok, w = os_[0].shape
    row = ((EP_TILE, w), lambda i, j: (i, 0))

    def epilogue(dob_, i, j, ins, outs):
        os_r, ls_r = ins[:3], ins[3:]
        do_r, dd_r = outs[:3], outs[3:]
        ls = [l[...] for l in ls_r]
        mx = jnp.maximum(jnp.maximum(ls[0], ls[1]), ls[2])
        es = [jnp.exp(l - mx) for l in ls]
        den = es[0] + es[1] + es[2]
        ws = [e / den for e in es]
        dws = [_segsum64(dob_ * o[...]) for o in os_r]
        mean = ws[0] * dws[0] + ws[1] * dws[1] + ws[2] * dws[2]
        for wg, do_ref, dd_ref in zip(ws, do_r, dd_r):
            do_ref[...] = wg * dob_
            dd_ref[...] = -wg * mean

    shp = jax.ShapeDtypeStruct((ntok, w), F32)
    return _matmul(dyb, wbbt, mode="nn", tm=EP_TILE, tn=w, tk=dyb.shape[1], name="branch_b_dgrad_merge_bwd",
                   ins=[(v,) + row for v in list(os_) + list(ls_)], outs=[(shp,) + row] * 6, epilogue=epilogue)


def _branch_a_dgrad_delta(dya, wba, oa, lse_a, sinks_exp, seq):
    ntok, w = oa.shape
    row, per_b, whole = _ep_specs(seq, w)
    tiles = seq // EP_TILE

    def epilogue(do_, i, j, ins, outs):
        o_ref, l_ref, s_ref = ins
        do_ref, dd_ref, acc_ref = outs
        dd = -_segsum64(do_ * o_ref[...])
        do_ref[...] = do_
        dd_ref[...] = dd
        _acc_rows(acc_ref, i % tiles == 0, [_colsum(dd * jnp.exp(s_ref[...] - l_ref[...]))])

    shp = jax.ShapeDtypeStruct((ntok, w), F32)
    return _matmul(dya, wba, mode="nt", tm=EP_TILE, tn=w, tk=dya.shape[1], name="branch_a_dgrad_delta",
                   ins=[(oa,) + row, (lse_a,) + row, (sinks_exp,) + whole],
                   outs=[(shp,) + row, (shp,) + row, (jax.ShapeDtypeStruct((ntok // seq, 8, w), F32),) + per_b],
                   epilogue=epilogue)


def _swap_halves(v):
    src = lax.broadcasted_iota(jnp.int32, (LANES, LANES), 0)
    dst = lax.broadcasted_iota(jnp.int32, (LANES, LANES), 1)
    partner = jnp.where((dst % HEAD_DIM) < HEAD_DIM // 2, dst + HEAD_DIM // 2, dst - HEAD_DIM // 2)
    perm = jnp.where(src == partner, 1.0, 0.0).astype(BF16)
    hi = v.astype(BF16)
    lo = (v - hi.astype(F32)).astype(BF16)
    return jnp.dot(hi, perm, preferred_element_type=F32) + jnp.dot(lo, perm, preferred_element_type=F32)


def _swap_halves_roll(v):
    lane = lax.broadcasted_iota(jnp.int32, v.shape, 1)
    return jnp.where((lane % HEAD_DIM) < HEAD_DIM // 2, pltpu.roll(v, LANES - HEAD_DIM // 2, 1),
                     pltpu.roll(v, HEAD_DIM // 2, 1))


def _swap_halves_coarse(v):
    src = lax.broadcasted_iota(jnp.int32, (LANES, LANES), 0)
    dst = lax.broadcasted_iota(jnp.int32, (LANES, LANES), 1)
    partner = jnp.where((dst % HEAD_DIM) < HEAD_DIM // 2, dst + HEAD_DIM // 2, dst - HEAD_DIM // 2)
    perm = jnp.where(src == partner, 1.0, 0.0).astype(BF16)
    return jnp.dot(v.astype(BF16), perm, preferred_element_type=F32)


def _rope(v, cos, sin, sign=1.0, mxu=True, coarse=False):
    swap = (_swap_halves_coarse if coarse else _swap_halves) if mxu else _swap_halves_roll
    out = []
    for c in range(v.shape[1] // LANES):
        part = v[:, c * LANES:(c + 1) * LANES]
        out.append(part * cos + sign * (swap(part) * sin))
    return jnp.concatenate(out, axis=1) if len(out) > 1 else out[0]


def _half_mask(shape, half):
    lane = lax.broadcasted_iota(jnp.int32, shape, len(shape) - 1) % LANES
    return (lane < HEAD_DIM) if half == 0 else (lane >= HEAD_DIM)


def _dup_half(v, half):
    return jnp.where(_half_mask(v.shape, half), v, pltpu.roll(v, HEAD_DIM, 1))


def _fold_halves(v):
    return v + pltpu.roll(v, HEAD_DIM, 1)


def _pick_halves(lo_rows, hi_rows):
    return jnp.where(_half_mask(lo_rows.shape, 0), lo_rows, hi_rows)


def _stack_masked(v, pairs):
    parts = []
    for c in pairs:
        pair = v[:, c * LANES:(c + 1) * LANES]
        parts += [jnp.where(_half_mask(pair.shape, half), pair, 0.0) for half in (0, 1)]
    return jnp.concatenate(parts, axis=0)


def _stack_pair_cols(v, pairs):
    return jnp.concatenate([v[:, c * LANES + half * HEAD_DIM:c * LANES + half * HEAD_DIM + 1] for c in pairs for half in (0, 1)],
                           axis=0)


ATTN_UNITS = 16


def _class_rows(r):
    return [pl.ds(0, QBLOCK)] if r == 1 else [pl.ds(rho, QBLOCK, stride=r) for rho in range(r)]


def _band_mask(nrows, nk, blk, n_back, has_prev):
    qi = lax.broadcasted_iota(jnp.int32, (nrows, nk), 0) % QBLOCK
    ki = lax.broadcasted_iota(jnp.int32, (nrows, nk), 1)
    if has_prev:
        dist = qi + QBLOCK - ki
        return (dist >= 0) & (dist <= n_back) & ((ki >= QBLOCK) | (blk > 0))
    dist = qi - ki
    return (dist >= 0) & (dist <= n_back)


def _attn_fwd(q_arr, k_arr, v_arr, *, name, npair, gqa, q_col, k_col, v_col, nchunk, r, n_back, sinks=None):
    bsz, seq, _ = q_arr.shape
    rr = QBLOCK * r
    nblk = seq // rr
    qw = npair * LANES
    kw = LANES if gqa else qw
    has_prev = nblk > 1
    has_sink = sinks is not None
    scale = HEAD_DIM ** -0.5

    def body(*refs):
        refs = list(refs)
        q_ref, kc_ref, vc_ref = refs[:3]
        pos = 3
        if has_prev:
            kp_ref, vp_ref = refs[pos:pos + 2]
            pos += 2
        if has_sink:
            sink_ref = refs[pos]
            pos += 1
        o_ref, lse_ref = refs[pos:pos + 2]
        blk = pl.program_id(2)
        nk = (2 if has_prev else 1) * QBLOCK
        valid = _band_mask(QBLOCK, nk, blk, n_back, has_prev)
        per = npair // 2
        classes = _class_rows(r)
        step = max(1, ATTN_UNITS // (2 * npair))
        for first in range(0, len(classes), step):
            batch = classes[first:first + step]
            units = []
            for ci, rows in enumerate(batch):
                q = q_ref[0, rows, :] * scale
                k, v = kc_ref[0, rows, :], vc_ref[0, rows, :]
                if has_prev:
                    k = jnp.concatenate([kp_ref[0, rows, :], k], axis=0)
                    v = jnp.concatenate([vp_ref[0, rows, :], v], axis=0)
                if gqa:
                    kdup = [_dup_half(k, hk).astype(BF16) for hk in range(2)]
                    vdup = [_dup_half(v, hk) for hk in range(2)]
                for c in range(npair):
                    sl = slice(c * LANES, (c + 1) * LANES)
                    qc = q[:, sl]
                    kc, vc = (kdup[c // per], vdup[c // per]) if gqa else (k[:, sl].astype(BF16), v[:, sl])
                    for half in (0, 1):
                        qm = jnp.where(_half_mask(qc.shape, half), qc, 0.0).astype(BF16)
                        vm = jnp.where(_half_mask(vc.shape, half), vc, 0.0).astype(BF16)
                        s = lax.dot_general(qm, kc, _DIMS["nt"], preferred_element_type=F32)
                        units.append(dict(ci=ci, c=c, half=half, s=s, vm=vm, sk=sink_ref[2 * c + half] if has_sink else None))
            for u in units:
                s = jnp.where(valid, u["s"], NEG_INF)
                m = jnp.max(s, axis=1, keepdims=True)
                if has_sink:
                    m = jnp.maximum(m, u["sk"])
                p = jnp.exp(s - m)
                den = jnp.sum(p, axis=1, keepdims=True)
                if has_sink:
                    den = den + jnp.exp(u["sk"] - m)
                u.update(p=p.astype(BF16), den=den, lse=m + jnp.log(den))
            for u in units:
                u["o"] = jnp.dot(u["p"], u["vm"], preferred_element_type=F32) / u["den"]
            for ci, rows in enumerate(batch):
                outs, lses = [None] * npair, [None] * npair
                for u in units:
                    if u["ci"] != ci:
                        continue
                    c, o = u["c"], u["o"]
                    lse = jnp.broadcast_to(u["lse"], o.shape)
                    outs[c] = o if u["half"] == 0 else outs[c] + o
                    lses[c] = lse if u["half"] == 0 else _pick_halves(lses[c], lse)
                o_ref[0, rows, :] = jnp.concatenate(outs, axis=1) if npair > 1 else outs[0]
                lse_ref[0, rows, :] = jnp.concatenate(lses, axis=1) if npair > 1 else lses[0]

    def cur(width, col0):
        return pl.BlockSpec((1, rr, width), lambda b, c, i: (b, i, col0 + c))

    def prev(width, col0):
        return pl.BlockSpec((1, rr, width), lambda b, c, i: (b, jnp.maximum(i - 1, 0), col0 + c))

    in_specs = [cur(qw, q_col), cur(kw, k_col), cur(kw, v_col)]
    args = [q_arr, k_arr, v_arr]
    if has_prev:
        in_specs += [prev(kw, k_col), prev(kw, v_col)]
        args += [k_arr, v_arr]
    if has_sink:
        in_specs.append(pl.BlockSpec(memory_space=pltpu.SMEM))
        args.append(sinks)
    return pl.pallas_call(
        body,
        name=name,
        grid=(bsz, nchunk, nblk),
        in_specs=in_specs,
        out_specs=[pl.BlockSpec((1, rr, qw), lambda b, c, i: (b, i, c))] * 2,
        out_shape=[jax.ShapeDtypeStruct((bsz, seq, nchunk * qw), F32)] * 2,
        compiler_params=_params("parallel", "parallel", "parallel"),
    )(*args)


def _attn_bwd(q_arr, k_arr, v_arr, cos, sin, do, lse, dd, *, name, npair, gqa, q_col, k_col, v_col, nchunk, r, n_back,
              token=None):
    bsz, seq, _ = q_arr.shape
    rr = QBLOCK * r
    nblk = seq // rr
    qw = npair * LANES
    kw = LANES if gqa else qw
    has_next = nblk > 1
    has_token = token is not None
    staged = r > 1
    scale = HEAD_DIM ** -0.5

    def body(*refs):
        refs = list(refs)
        k_ref, v_ref, c_ref, s_ref = refs[:4]
        tile_refs = [refs[4:8]]
        pos = 8
        if has_next:
            tile_refs.append(refs[pos:pos + 4])
            pos += 4
        if has_token:
            pos += 1
        dq_ref, dk_ref, dv_ref = refs[pos:pos + 3]
        carry_ref = refs[pos + 3]
        if staged:
            stage_q, stage_k, stage_v = refs[pos + 4:pos + 7]
        blk = pl.program_id(2)
        if has_next:
            @pl.when(blk == 0)
            def _():
                carry_ref[...] = jnp.zeros_like(carry_ref)

        nrows = (npair if gqa else 1) * QBLOCK
        qi = lax.broadcasted_iota(jnp.int32, (nrows, QBLOCK), 0) % QBLOCK
        ki = lax.broadcasted_iota(jnp.int32, (nrows, QBLOCK), 1)
        valids = [qi >= ki, (qi + QBLOCK - ki <= n_back) & (blk + 1 < nblk)]
        per = npair // 2
        ntile = len(tile_refs)
        cat = lambda parts: jnp.concatenate(parts, axis=1) if len(parts) > 1 else parts[0]
        classes = _class_rows(r)
        step = max(1, ATTN_UNITS // (ntile * (2 if gqa else 2 * npair)))
        for first in range(0, len(classes), step):
            batch = classes[first:first + step]
            units = []
            for ci, rows in enumerate(batch):
                tiles = [(q_ref[0, rows, :] * scale, do_ref[0, rows, :], l_ref[0, rows, :], d_ref[0, rows, :])
                         for q_ref, do_ref, l_ref, d_ref in tile_refs]
                k, v = k_ref[0, rows, :], v_ref[0, rows, :]
                if gqa:
                    for hk in range(2):
                        pairs = list(range(hk * per, (hk + 1) * per))
                        kd, vd = _dup_half(k, hk).astype(BF16), _dup_half(v, hk).astype(BF16)
                        for t, (q, do_, l_, d_) in enumerate(tiles):
                            units.append(dict(ci=ci, t=t, hk=hk, pairs=pairs, qs=_stack_masked(q, pairs).astype(BF16),
                                              dos=_stack_masked(do_, pairs).astype(BF16), lcol=_stack_pair_cols(l_, pairs),
                                              dcol=_stack_pair_cols(d_, pairs), kmat=kd, vmat=vd, kdq=kd))
                else:
                    for c in range(npair):
                        sl = slice(c * LANES, (c + 1) * LANES)
                        kc, vcb = k[:, sl], v[:, sl].astype(BF16)
                        kcb = kc.astype(BF16)
                        for t, (q, do_, l_, d_) in enumerate(tiles):
                            for half in (0, 1):
                                hm = _half_mask(kc.shape, half)
                                col = c * LANES + half * HEAD_DIM
                                units.append(dict(ci=ci, t=t, c=c, half=half, qs=jnp.where(hm, q[:, sl], 0.0).astype(BF16),
                                                  dos=jnp.where(hm, do_[:, sl], 0.0).astype(BF16), lcol=l_[:, col:col + 1],
                                                  dcol=d_[:, col:col + 1], kmat=kcb, vmat=vcb,
                                                  kdq=jnp.where(hm, kc, 0.0).astype(BF16)))
            for u in units:
                u["s"] = lax.dot_general(u["qs"], u["kmat"], _DIMS["nt"], preferred_element_type=F32)
                u["dp"] = lax.dot_general(u["dos"], u["vmat"], _DIMS["nt"], preferred_element_type=F32)
            for u in units:
                p = jnp.exp(jnp.where(valids[u["t"]], u["s"], NEG_INF) - u["lcol"])
                u["ds"] = (p * (u["dp"] + u["dcol"])).astype(BF16)
                u["p"] = p.astype(BF16)
            for u in units:
                u["dv"] = lax.dot_general(u["p"], u["dos"], _DIMS["tn"], preferred_element_type=F32)
                u["dk"] = lax.dot_general(u["ds"], u["qs"], _DIMS["tn"], preferred_element_type=F32)
                u["dq"] = jnp.dot(u["ds"], u["kdq"], preferred_element_type=F32) * scale
            for ci, rows in enumerate(batch):
                mine = [u for u in units if u["ci"] == ci]
                dq = [[None] * npair for _ in range(ntile)]
                if gqa:
                    dk_out = dv_out = None
                    for hk in range(2):
                        us = [u for u in mine if u["hk"] == hk]
                        for u in us:
                            for i, c in enumerate(u["pairs"]):
                                dq[u["t"]][c] = _pick_halves(u["dq"][2 * i * QBLOCK:(2 * i + 1) * QBLOCK],
                                                             u["dq"][(2 * i + 1) * QBLOCK:(2 * i + 2) * QBLOCK])
                        dk_h = _fold_halves(functools.reduce(jnp.add, [u["dk"] for u in us]))
                        dv_h = _fold_halves(functools.reduce(jnp.add, [u["dv"] for u in us]))
                        dk_out = dk_h if hk == 0 else _pick_halves(dk_out, dk_h)
                        dv_out = dv_h if hk == 0 else _pick_halves(dv_out, dv_h)
                else:
                    dks, dvs = [], []
                    for c in range(npair):
                        us = [u for u in mine if u["c"] == c]
                        dks.append(functools.reduce(jnp.add, [u["dk"] for u in us]))
                        dvs.append(functools.reduce(jnp.add, [u["dv"] for u in us]))
                        for t in range(ntile):
                            dq[t][c] = functools.reduce(jnp.add, [u["dq"] for u in us if u["t"] == t])
                    dk_out, dv_out = cat(dks), cat(dvs)
                ck, sk_ = c_ref[0, rows, :], s_ref[0, rows, :]
                dk_new = _rope(dk_out, ck, sk_, sign=-1.0, mxu=gqa, coarse=True)
                dq_cur = cat(dq[0])
                if has_next:
                    dq_cur = dq_cur + carry_ref[rows, :]
                    carry_ref[rows, :] = cat(dq[1])
                dq_new = _rope(dq_cur, ck, sk_, sign=-1.0, mxu=gqa, coarse=True)
                if staged:
                    stage_q[rows, :], stage_k[rows, :], stage_v[rows, :] = dq_new, dk_new, dv_out
                else:
                    dq_ref[0], dk_ref[0], dv_ref[0] = dq_new.astype(BF16), dk_new.astype(BF16), dv_out.astype(BF16)
        if staged:
            dq_ref[0], dk_ref[0], dv_ref[0] = stage_q[...].astype(BF16), stage_k[...].astype(BF16), stage_v[...].astype(BF16)

    def at(width, col0, shift):
        return pl.BlockSpec((1, rr, width), lambda b, c, i: (b, jnp.minimum(i + shift, nblk - 1), col0 + c))

    in_specs = [at(kw, k_col, 0), at(kw, v_col, 0), pl.BlockSpec((1, rr, LANES), lambda b, c, i: (b, i, 0)),
                pl.BlockSpec((1, rr, LANES), lambda b, c, i: (b, i, 0))]
    args = [k_arr, v_arr, cos, sin]
    for shift in (0, 1) if has_next else (0,):
        in_specs += [at(qw, q_col, shift), at(qw, 0, shift), at(qw, 0, shift), at(qw, 0, shift)]
        args += [q_arr, do, lse, dd]
    if has_token:
        in_specs.append(pl.BlockSpec(token.shape, lambda b, c, i: (0, 0)))
        args.append(token)
    return pl.pallas_call(
        body,
        name=name,
        grid=(bsz, nchunk, nblk),
        in_specs=in_specs,
        out_specs=[pl.BlockSpec((1, rr, qw), lambda b, c, i: (b, i, c)),
                   pl.BlockSpec((1, rr, kw), lambda b, c, i: (b, i, c)),
                   pl.BlockSpec((1, rr, kw), lambda b, c, i: (b, i, c))],
        out_shape=[jax.ShapeDtypeStruct((bsz, seq, nchunk * qw), BF16),
                   jax.ShapeDtypeStruct((bsz, seq, nchunk * kw), BF16),
                   jax.ShapeDtypeStruct((bsz, seq, nchunk * kw), BF16)],
        scratch_shapes=[pltpu.VMEM((rr, qw) if has_next else (8, LANES), F32)] +
                       ([pltpu.VMEM((rr, qw), F32), pltpu.VMEM((rr, kw), F32), pltpu.VMEM((rr, kw), F32)] if staged else []),
        compiler_params=_params("parallel", "parallel", "arbitrary"),
    )(*args)


B_CHUNKS = {1: (4, 1), 4: (1, 4), 16: (1, 4)}


def _rope_tables(positions):
    half = HEAD_DIM // 2
    inv = ROPE_THETA ** (-jnp.arange(half, dtype=F32) / half)
    ang = positions.astype(F32)[..., None] * inv
    cos, sin = jnp.cos(ang), jnp.sin(ang)
    return jnp.concatenate([cos] * 4, axis=-1), jnp.concatenate([-sin, sin, -sin, sin], axis=-1)


def _layer_step(x, mod, tables, sinks, ln1_g, ln1_b, ln2_g, ln2_b, target, get_w_in, get_rest, hook):
    bsz, seq, d = x.shape
    ntok = bsz * seq
    flat = lambda v: v.reshape(ntok, v.shape[-1])
    unflat = lambda v: v.reshape(bsz, seq, v.shape[-1])
    cos, sin = tables
    mm = functools.partial(_matmul, tm=1024, tk=1024)
    scalar = lambda tok: 0.0 if tok is None else tok[0, 0]

    u1 = _modulate_in(x, mod)
    u1f = flat(u1)
    wint = get_w_in(u1)
    cosf, sinf = flat(cos), flat(sin)
    proj = functools.partial(_proj_rope, u1f, wint, cosf, sinf, tm=2048)
    qkvb = unflat(proj(n=4608, b_off=OFF_QKVB, rope_cols=3072, tn=256, name="proj_qkvb"))
    b_kws, os_, ls_ = [], [], []
    for g, (window, r) in enumerate(B_PATTERNS):
        npair, nch = B_CHUNKS[r]
        per = B_HEADS_PER_GROUP // (2 * npair)
        nsec = len(B_PATTERNS) * per
        kw_ = dict(npair=npair, gqa=False, q_col=g * per, k_col=nsec + g * per, v_col=2 * nsec + g * per, nchunk=nch, r=r,
                   n_back=window // r)
        b_kws.append(kw_)
        o_g, l_g = _attn_fwd(qkvb, qkvb, qkvb, name=f"attn_b{g}_fwd", **kw_)
        os_.append(o_g)
        ls_.append(l_g)
    ob = _merge_b(os_, ls_)
    tok = hook("projected", ob)
    u1t = u1f if tok is None else lax.optimization_barrier((u1f, tok))[0]
    proj = functools.partial(_proj_rope, u1t, wint, cosf, sinf, tm=2048)
    gab = unflat(proj(n=2048, b_off=OFF_GAB, rope_cols=0, tn=256, name="proj_gab", out_dtype=BF16))
    qa = unflat(proj(n=1024, b_off=OFF_QA, rope_cols=1024, tn=512, name="proj_qa"))
    kva = unflat(proj(n=256, b_off=OFF_KVA, rope_cols=128, tn=128, name="proj_kva"))
    a_kw = dict(npair=A_Q_HEADS // 2, gqa=True, q_col=0, k_col=0, v_col=1, nchunk=1, r=1, n_back=A_WINDOW - 1)
    oa, lse_a = _attn_fwd(qa, kva, kva, name="attn_a_fwd", sinks=sinks.reshape(A_Q_HEADS), **a_kw)
    rest = get_rest(lax.optimization_barrier((gab, oa))[0])
    wba, wbbt, wo, wgut, wd = (rest[n] for n in ("w_branch_a", "w_branch_b", "w_o", "w_gate_up", "w_down"))
    ya = unflat(mm(flat(oa), wba, mode="nn", out_dtype=BF16, tn=512, name="branch_a"))
    ybf, mergedf = _branch_b_gate_merge(flat(ob), wbbt, flat(gab), flat(ya))
    xf = flat(x)
    y1f, h1f, u2f = _wo_ln1(mergedf, wo, xf, mod, ln1_g, ln1_b, seq)
    wgut_i = _interleave_gate_up(wgut)
    hf, af = _gate_up_silu(u2f, wgut_i)

    dy2f, dh1af, acc2 = _down_ln2_loss_bwd(af, wd, h1f, mod, ln2_g, ln2_b, flat(target), seq)
    g_wd = _matmul(af, dy2f, mode="tn", out_dtype=BF16, tm=256, tn=1024, tk=ntok, name="down_wgrad")
    dhf = _down_dgrad_silu_bwd(dy2f, wd, hf)
    g_wgut = _interleave_gate_up(_matmul(dhf, u2f, mode="tn", out_dtype=BF16, tm=256, tn=1024, tk=ntok, name="gate_up_wgrad"))
    dy1f, dxaf, acc1 = _gate_up_dgrad_ln1_bwd(dhf, wgut_i, dh1af, xf, y1f, mod, ln1_g, ln1_b, seq)
    g_wo = _matmul(mergedf, dy1f, mode="tn", out_dtype=BF16, tm=256, tn=1024, tk=ntok, name="w_o_wgrad")
    dyaf, dybf, dgaf, dgbf = _wo_dgrad_gate_bwd(dy1f, wo, flat(gab), flat(ya), ybf)
    g_wba = _matmul(flat(oa), dyaf, mode="tn", out_dtype=BF16, tm=256, tn=1024, tk=ntok, name="branch_a_wgrad")
    g_wbbt = _matmul(dybf, flat(ob), mode="tn", out_dtype=BF16, tm=256, tn=512, tk=ntok, name="branch_b_wgrad")
    tok = hook("grads_rest", dict(w_branch_a=g_wba, w_branch_b=g_wbbt, w_o=g_wo, w_gate_up=g_wgut, w_down=g_wd))

    sinks_exp = jnp.repeat(sinks.reshape(1, A_Q_HEADS), HEAD_DIM, axis=1) + scalar(tok)
    doa, dd_a, acc_s = _branch_a_dgrad_delta(dyaf, wba, flat(oa), flat(lse_a), sinks_exp, seq)
    doa, dd_a = unflat(doa), unflat(dd_a)
    tok = hook("delta_done", dd_a)
    dqa, dka, dva = _attn_bwd(qa, kva, kva, cos, sin, doa, lse_a, dd_a, name="attn_a_bwd", token=tok, **a_kw)
    merged_bwd = [unflat(t) for t in _branch_b_dgrad_merge_bwd(dybf, wbbt, [flat(t) for t in os_], [flat(t) for t in ls_])]
    dqs, dks, dvs = [], [], []
    for g in range(len(B_PATTERNS)):
        dq_g, dk_g, dv_g = _attn_bwd(qkvb, qkvb, qkvb, cos, sin, merged_bwd[g], ls_[g], merged_bwd[3 + g],
                                     name=f"attn_b{g}_bwd", **b_kws[g])
        dqs.append(dq_g)
        dks.append(dk_g)
        dvs.append(dv_g)
    dproj = jnp.concatenate([t.astype(BF16) for t in [dqa, dka, dva] + dqs + dks + dvs] + [unflat(dgaf), unflat(dgbf)], axis=-1)
    dprojf = flat(dproj)
    g_wint = _matmul(dprojf, u1f, mode="tn", out_dtype=BF16, tm=256, tn=1024, tk=ntok, name="w_in_wgrad")
    tok = hook("grads_w_in", dict(w_in=g_wint))
    grad_x, acc0 = _w_in_dgrad_grad_x(dprojf, wint, dxaf, xf, mod, seq, tok)
    grad_x = unflat(grad_x)
    tok = hook("dgrad_done", grad_x)

    loss_part = jnp.sum(acc2[:, 3, 0])
    dmod = jnp.stack([acc0[:, 1], acc0[:, 0], acc1[:, 2], acc1[:, 4], acc1[:, 3], acc2[:, 2]], axis=1)
    small = jnp.stack([acc1[:, 0].sum(0), acc1[:, 1].sum(0), acc2[:, 0].sum(0), acc2[:, 1].sum(0), acc_s[:, 0].sum(0)])
    small = small + scalar(tok)
    return loss_part, grad_x, dmod, small


CHIP_FLIPS = (2, 4, 6)


def _my_place():
    return lax.axis_index("x"), lax.axis_index("y"), lax.axis_index("c")


def _flip(place, k):
    px, py, pc = place
    return (1 - px if k & 4 else px, 1 - py if k & 2 else py, 1 - pc if k & 1 else pc)


def _index(place):
    return 4 * place[0] + 2 * place[1] + place[2]


def _gather_small(v, name):
    rows, cols = v.shape

    def body(v_ref, out_ref, send_sems, recv_sems):
        me = _my_place()
        out_ref[_index(me)] = v_ref[...]
        copies = []
        for k in range(1, N_DEV):
            copies.append(pltpu.make_async_remote_copy(
                src_ref=v_ref, dst_ref=out_ref.at[_index(me)], send_sem=send_sems.at[k - 1], recv_sem=recv_sems.at[k - 1],
                device_id=_flip(me, k), device_id_type=MESH))
        for cp in copies:
            cp.start()
        for k in range(1, N_DEV):
            pltpu.make_async_remote_copy(
                src_ref=v_ref, dst_ref=out_ref.at[_index(_flip(me, k))], send_sem=send_sems.at[k - 1],
                recv_sem=recv_sems.at[k - 1], device_id=_flip(me, k), device_id_type=MESH).wait_recv()
        for cp in copies:
            cp.wait_send()

    return pl.pallas_call(
        body,
        name=name,
        out_shape=jax.ShapeDtypeStruct((N_DEV, rows, cols), v.dtype),
        in_specs=[pl.BlockSpec(memory_space=pltpu.VMEM)],
        out_specs=pl.BlockSpec(memory_space=pltpu.VMEM),
        scratch_shapes=[pltpu.SemaphoreType.DMA((N_DEV - 1,)), pltpu.SemaphoreType.DMA((N_DEV - 1,))],
        compiler_params=pltpu.CompilerParams(vmem_limit_bytes=VMEM_LIMIT_BYTES),
    )(v)


_HBM = pl.BlockSpec(memory_space=pltpu.HBM)
_SEM = pl.BlockSpec(memory_space=pltpu.SEMAPHORE)
_EFFECT = pltpu.SideEffectType.DATAFLOW_SIDE_EFFECTING


def _remote(src, dst, send_sems, recv_sems, j, to):
    return pltpu.make_async_remote_copy(src_ref=src, dst_ref=dst, send_sem=send_sems.at[j], recv_sem=recv_sems.at[j],
                                        device_id=to, device_id_type=MESH)


def _copies_start(name, bufs, make_copies, nsem):
    nbuf = len(bufs)

    def body(*refs):
        for cp in make_copies(refs[:nbuf], refs[nbuf], refs[nbuf + 1]):
            cp.start()
        refs[-1][...] = jnp.zeros_like(refs[-1])

    sems = pltpu.SemaphoreType.DMA((nsem,))
    res = pl.pallas_call(
        body, name=name,
        out_shape=(sems, sems, *[pltpu.HBM(v.shape, v.dtype) for v in bufs], jax.ShapeDtypeStruct((8, LANES), F32)),
        in_specs=(_HBM,) * nbuf, out_specs=(_SEM, _SEM) + (_HBM,) * nbuf + (pl.BlockSpec(memory_space=pltpu.VMEM),),
        input_output_aliases={i: 2 + i for i in range(nbuf)},
        compiler_params=pltpu.CompilerParams(has_side_effects=_EFFECT),
    )(*[pltpu.with_memory_space_constraint(v, pltpu.HBM) for v in bufs])
    return res[0], res[1], list(res[2:2 + nbuf]), res[-1]


def _copies_wait(name, started, make_copies, after):
    send_sems, recv_sems, bufs, _ = started
    nbuf = len(bufs)

    def body(*refs):
        for cp in make_copies(refs[:nbuf], refs[nbuf], refs[nbuf + 1]):
            cp.wait_send()
            cp.wait_recv()

    return list(pl.pallas_call(
        body, name=name,
        out_shape=tuple(pltpu.HBM(v.shape, v.dtype) for v in bufs),
        in_specs=(_HBM,) * nbuf + (_SEM, _SEM, pl.BlockSpec(memory_space=pl.ANY)), out_specs=(_HBM,) * nbuf,
        input_output_aliases={i: i for i in range(nbuf)},
        compiler_params=pltpu.CompilerParams(has_side_effects=_EFFECT),
    )(*bufs, send_sems, recv_sems, after))


def _to_sibling_copies(refs, send_sems, recv_sems):
    src_ref, land_ref = refs
    me = _my_place()
    return [_remote(src_ref.at[q, 1 - me[2]], land_ref.at[q], send_sems, recv_sems, q, _flip(me, 1)) for q in range(4)]


def _to_chips_copies(refs, send_sems, recv_sems):
    src_ref, land_ref = refs
    me = _my_place()
    copies = []
    for j, k in enumerate(CHIP_FLIPS):
        to = _flip(me, k)
        copies.append(_remote(src_ref.at[2 * to[0] + to[1]], land_ref.at[j], send_sems, recv_sems, j, to))
    return copies


class _Gather:
    def __init__(self, name, blocks):
        self.name, self.n = name, len(blocks)
        at_me = (_index(_my_place()), 0, 0)
        lands = [lax.dynamic_update_slice(lax.empty((N_DEV,) + v.shape, v.dtype), v[None], at_me) for v in blocks]
        self.first = _copies_start(name + "_start", list(blocks) + lands, self._first_copies, 4 * self.n)
        self.token = self.first[3]

    def _first_copies(self, refs, send_sems, recv_sems):
        me = _my_place()
        return [_remote(refs[w], refs[self.n + w].at[_index(me)], send_sems, recv_sems, 4 * w + j, _flip(me, k))
                for w in range(self.n) for j, k in enumerate((1,) + CHIP_FLIPS)]

    def _pass_copies(self, refs, send_sems, recv_sems):
        me = _my_place()
        copies = []
        for w, land in enumerate(refs):
            for j, k in enumerate(CHIP_FLIPS):
                slot = land.at[_index(_flip(me, k))]
                copies.append(_remote(slot, slot, send_sems, recv_sems, 3 * w + j, _flip(me, 1)))
        return copies

    def pass_on(self, after):
        lands = _copies_wait(self.name + "_wait", self.first, self._first_copies, after)[self.n:]
        self.second = _copies_start(self.name + "_pass_start", lands, self._pass_copies, 3 * self.n)
        return self.second[3]

    def finish(self, after):
        return _copies_wait(self.name + "_pass_wait", self.second, self._pass_copies, after)


SUM_SPLIT = 2


def _sum_pairs(parts, theirs):
    nchip, _, rows, cols = parts.shape
    tile = rows // SUM_SPLIT

    def body(c_ref, a_ref, b_ref, o_ref):
        o_ref[...] = (a_ref[0].astype(F32) + b_ref[...].astype(F32)).astype(BF16)

    spec = pl.BlockSpec((1, tile, cols), lambda q, t, c_ref: (q, t, 0))
    grid_spec = pltpu.PrefetchScalarGridSpec(
        num_scalar_prefetch=1, grid=(nchip, SUM_SPLIT),
        in_specs=[pl.BlockSpec((1, 1, tile, cols), lambda q, t, c_ref: (q, c_ref[0], t, 0)), spec], out_specs=spec)
    return pl.pallas_call(body, name="grad_sum_sibling", grid_spec=grid_spec,
                          out_shape=jax.ShapeDtypeStruct((nchip, rows, cols), BF16),
                          compiler_params=_params("parallel", "parallel"))(lax.axis_index("c").reshape(1), parts, theirs)


def _sum_final(chip_sum, got):
    _, rows, cols = chip_sum.shape
    tile = rows // SUM_SPLIT

    def body(q_ref, a_ref, g_ref, o_ref):
        o_ref[...] = ((a_ref[0].astype(F32) + g_ref[0].astype(F32)) + g_ref[1].astype(F32)) + g_ref[2].astype(F32)

    grid_spec = pltpu.PrefetchScalarGridSpec(
        num_scalar_prefetch=1, grid=(SUM_SPLIT,),
        in_specs=[pl.BlockSpec((1, tile, cols), lambda t, q_ref: (q_ref[0], t, 0)),
                  pl.BlockSpec((3, tile, cols), lambda t, q_ref: (0, t, 0))],
        out_specs=pl.BlockSpec((tile, cols), lambda t, q_ref: (t, 0)))
    my_chip = (2 * lax.axis_index("x") + lax.axis_index("y")).reshape(1)
    return pl.pallas_call(body, name="grad_sum_chips", grid_spec=grid_spec, out_shape=jax.ShapeDtypeStruct((rows, cols), F32),
                          compiler_params=_params("parallel"))(my_chip, chip_sum, got)


class _ReduceScatter:
    def __init__(self, name, slabs):
        self.name, self.rows = name, slabs.shape[1]
        parts = slabs.reshape(4, 2, self.rows, D_MODEL)
        self.first = _copies_start(name + "_sibling_start", [parts, lax.empty((4, self.rows, D_MODEL), slabs.dtype)],
                                   _to_sibling_copies, 4)
        self.token = self.first[3]

    def between_chips(self, after):
        parts, theirs = _copies_wait(self.name + "_sibling_wait", self.first, _to_sibling_copies, after)
        chip_sum = _sum_pairs(parts, theirs)
        self.second = _copies_start(self.name + "_chips_start", [chip_sum, lax.empty((3, self.rows, D_MODEL), chip_sum.dtype)],
                                    _to_chips_copies, 3)
        return self.second[3]

    def finish(self, after):
        chip_sum, got = _copies_wait(self.name + "_chips_wait", self.second, _to_chips_copies, after)
        return _sum_final(chip_sum, got)


def _ada_fwd(c_all, w, b):
    nb, _ = c_all.shape
    ncol = w.shape[1]

    def body(c_ref, w_ref, b_ref, o_ref):
        c = c_ref[...]
        act = (c * _sigmoid(c)).astype(BF16)
        o_ref[...] = jnp.dot(act, w_ref[...].astype(BF16), preferred_element_type=F32) + b_ref[...]

    return pl.pallas_call(body, name="ada_fwd", out_shape=jax.ShapeDtypeStruct((nb, ncol), F32),
                          compiler_params=pltpu.CompilerParams(vmem_limit_bytes=VMEM_LIMIT_BYTES))(c_all, w, b)


def _ada_wgrad(c_all_t, dmod_cols):
    d, nb = c_all_t.shape
    ncol = dmod_cols.shape[1]

    def body(ct_ref, dm_ref, o_ref):
        ct = ct_ref[...]
        act = (ct * _sigmoid(ct)).astype(BF16).astype(F32)
        dm = dm_ref[...].astype(BF16).astype(F32)
        acc = act[:, 0:1] * dm[0:1, :]
        for i in range(1, nb):
            acc = acc + act[:, i:i + 1] * dm[i:i + 1, :]
        o_ref[...] = acc

    return pl.pallas_call(body, name="ada_wgrad", out_shape=jax.ShapeDtypeStruct((d, ncol), F32),
                          compiler_params=pltpu.CompilerParams(vmem_limit_bytes=VMEM_LIMIT_BYTES))(c_all_t, dmod_cols)


SMALL_ROWS = 24


def _reduce_small(gathered):
    def body(g_ref, o_ref):
        acc = g_ref[0]
        for dev in range(1, N_DEV):
            acc = acc + g_ref[dev]
        o_ref[...] = acc

    return pl.pallas_call(body, name="reduce_small", out_shape=jax.ShapeDtypeStruct(gathered.shape[1:], F32))(gathered)


def _adamw_math(w, g, m, v):
    nm = ADAM_B1 * m + (1.0 - ADAM_B1) * g
    nv = ADAM_B2 * v + (1.0 - ADAM_B2) * (g * g)
    bc1 = 1.0 - ADAM_B1 ** ADAM_STEP
    bc2 = 1.0 - ADAM_B2 ** ADAM_STEP
    return -ADAM_LR * ((nm / bc1) / (jnp.sqrt(nv / bc2) + ADAM_EPS) + ADAM_WD * w), nm, nv


def _adamw_small(ws, gs, ms, vs, name):
    n = len(ws)

    def body(*refs):
        for i in range(n):
            res = _adamw_math(*(refs[k * n + i][...] for k in range(4)))
            for k in range(3):
                refs[(4 + k) * n + i][...] = res[k]

    shapes = [jax.ShapeDtypeStruct(w.shape, F32) for w in ws]
    res = pl.pallas_call(body, name=name, out_shape=shapes * 3)(*ws, *gs, *ms, *vs)
    return [(res[i], res[n + i], res[2 * n + i]) for i in range(n)]


def _adamw(w, g, m, v, name):
    rows, cols = w.shape
    tile = rows
    for cand in range(min(rows // 2, 512) // 8 * 8, 7, -8):
        if rows % cand == 0:
            tile = cand
            break
    spec = pl.BlockSpec((tile, cols), lambda t: (t, 0))

    def body(w_ref, g_ref, m_ref, v_ref, d_ref, nm_ref, nv_ref):
        d_ref[...], nm_ref[...], nv_ref[...] = _adamw_math(w_ref[...], g_ref[...], m_ref[...], v_ref[...])

    shp = jax.ShapeDtypeStruct((rows, cols), F32)
    return pl.pallas_call(body, name=name, grid=(rows // tile,), in_specs=[spec] * 4, out_specs=[spec] * 3, out_shape=[shp] * 3,
                          compiler_params=_params("parallel"))(w, g, m, v)


_WEIGHTS = ("w_ada", "b_ada", "w_in", "sinks", "w_branch_a", "w_branch_b", "w_o", "ln1_g", "ln1_b", "w_gate_up", "w_down",
            "ln2_g", "ln2_b")
_TRANSPOSED = ("w_in", "w_branch_b", "w_gate_up")


def _pack_shard(name, w):
    w = w.astype(BF16)
    if name in _TRANSPOSED:
        w = w.T
    return w.reshape(-1, D_MODEL)


def _unpack_full(name, slab):
    if name == "w_branch_b":
        return slab.reshape(N_DEV * 128, 512)
    return slab.reshape(-1, D_MODEL)


def _unpack_group(group, gathered):
    return {n: _unpack_full(n, slab) for (n, _), slab in zip(group, gathered)}


def _unpack_grads(group, g_packed):
    g_w, off = {}, 0
    for n, r in group:
        part = g_packed[off:off + r]
        off += r
        g_w[n] = part.reshape(128, 512) if n == "w_branch_b" else part
    return g_w


def kernel(x, c, positions, w_ada, b_ada, w_in, sinks, w_branch_a, w_branch_b, w_o, ln1_g, ln1_b, w_gate_up, w_down, ln2_g, ln2_b, loss_target, m_w_ada, m_b_ada, m_w_in, m_sinks, m_w_branch_a, m_w_branch_b, m_w_o, m_ln1_g, m_ln1_b, m_w_gate_up, m_w_down, m_ln2_g, m_ln2_b, v_w_ada, v_b_ada, v_w_in, v_sinks, v_w_branch_a, v_w_branch_b, v_w_o, v_ln1_g, v_ln1_b, v_w_gate_up, v_w_down, v_ln2_g, v_ln2_b):
    weights = dict(w_ada=w_ada, b_ada=b_ada, w_in=w_in, sinks=sinks, w_branch_a=w_branch_a, w_branch_b=w_branch_b, w_o=w_o,
                   ln1_g=ln1_g, ln1_b=ln1_b, w_gate_up=w_gate_up, w_down=w_down, ln2_g=ln2_g, ln2_b=ln2_b)
    m_in = dict(w_ada=m_w_ada, b_ada=m_b_ada, w_in=m_w_in, sinks=m_sinks, w_branch_a=m_w_branch_a, w_branch_b=m_w_branch_b,
                w_o=m_w_o, ln1_g=m_ln1_g, ln1_b=m_ln1_b, w_gate_up=m_w_gate_up, w_down=m_w_down, ln2_g=m_ln2_g, ln2_b=m_ln2_b)
    v_in = dict(w_ada=v_w_ada, b_ada=v_b_ada, w_in=v_w_in, sinks=v_sinks, w_branch_a=v_w_branch_a, w_branch_b=v_w_branch_b,
                w_o=v_w_o, ln1_g=v_ln1_g, ln1_b=v_ln1_b, w_gate_up=v_w_gate_up, w_down=v_w_down, ln2_g=v_ln2_g, ln2_b=v_ln2_b)
    bsz = x.shape[0]
    me = _index(_my_place())
    ada_cols = w_ada.shape[2]
    outs = {}

    def adamw(n, g):
        w2, m2, v2 = (t[n][0] if t[n].ndim == 3 else t[n] for t in (weights, m_in, v_in))
        shape = weights[n].shape
        if n in _TRANSPOSED:
            dlt, nm, nv = _adamw(w2.T, g, m2.T, v2.T, "adamw_" + n)
            outs[n] = tuple(t.T.reshape(shape) for t in (g, dlt, nm, nv))
        else:
            dlt, nm, nv = _adamw(w2, g, m2, v2, "adamw_" + n)
            outs[n] = tuple(t.reshape(shape) for t in (g, dlt, nm, nv))
        return nv

    packed_in = [_pack_shard(n, weights[n][0]) for n, _ in GROUP_IN]
    packed_rest = [_pack_shard(n, weights[n][0]) for n, _ in GROUP_REST]
    c_all = _gather_small(jnp.pad(c, ((0, 8 - bsz), (0, 0))), "gather_c")[:, :bsz].reshape(N_DEV * bsz, D_MODEL)
    gather_in = _Gather("gather_w_in", lax.optimization_barrier((packed_in, c_all))[0])
    b_cols = lax.dynamic_slice_in_dim(b_ada, me * ada_cols, ada_cols, axis=1)
    mod_cols = _ada_fwd(c_all, w_ada[0], b_cols + gather_in.token[0, 0])
    tables = _rope_tables(positions)
    mod_cols, tables, packed_rest = lax.optimization_barrier((mod_cols, tables, packed_rest))
    mod_all = _gather_small(mod_cols, "gather_mod").transpose(1, 0, 2).reshape(N_DEV * bsz, 6, D_MODEL)
    gather_rest = _Gather("gather_rest", lax.optimization_barrier((packed_rest, mod_all))[0])
    mod = jnp.pad(lax.dynamic_slice_in_dim(mod_all, me * bsz, bsz, axis=0), ((0, 0), (0, 2), (0, 0)))
    mod = mod + gather_rest.token[0, 0]
    mod = mod + gather_in.pass_on(mod)[0, 0]

    scatters = {}

    def get_w_in(after):
        return _unpack_group(GROUP_IN, gather_in.finish(after))["w_in"]

    def get_rest(after):
        return _unpack_group(GROUP_REST, gather_rest.finish(after))

    def pack_grads(group, grads):
        return jnp.concatenate([grads[n].reshape(N_DEV, r, D_MODEL) for n, r in group], axis=1)

    def hook(point, value):
        if point == "projected":
            return gather_rest.pass_on(value)
        if point == "grads_rest":
            scatters["rest"] = _ReduceScatter("scatter_rest", pack_grads(GROUP_REST, value))
            return scatters["rest"].token
        if point == "delta_done":
            return scatters["rest"].between_chips(value)
        if point == "grads_w_in":
            scatters["in"] = _ReduceScatter("scatter_w_in", pack_grads(GROUP_IN, value))
            return scatters["in"].token
        if point == "dgrad_done":
            return None
        raise ValueError(point)

    loss_part, grad_x, dmod, small = _layer_step(x, mod, tables, sinks[0], ln1_g, ln1_b, ln2_g, ln2_b, loss_target,
                                                 get_w_in, get_rest, hook)

    rows = jnp.concatenate([dmod.reshape(bsz * 6, D_MODEL), small, jnp.full((1, D_MODEL), loss_part, F32),
                            jnp.zeros((SMALL_ROWS - bsz * 6 - 6, D_MODEL), F32)], axis=0)
    small_all = _gather_small(rows, "gather_small")
    small_all = small_all + scatters["in"].between_chips(small_all)[0, 0]
    sums = _reduce_small(small_all)
    loss = sums[bsz * 6 + 5, 0]
    dmod_all = small_all[:, :bsz * 6].reshape(N_DEV * bsz, 6 * D_MODEL)
    small_g = {"b_ada": functools.reduce(jnp.add, [sums[6 * i:6 * i + 6] for i in range(bsz)]).reshape(1, 6 * D_MODEL),
               "sinks": sums[bsz * 6 + 4][::HEAD_DIM][None]}
    small_g.update({n: sums[bsz * 6 + i][None] for i, n in enumerate(("ln1_g", "ln1_b", "ln2_g", "ln2_b"))})
    names = list(small_g)
    for n, (dlt, nm, nv) in zip(names, _adamw_small([weights[n] for n in names], [small_g[n] for n in names],
                                                     [m_in[n] for n in names], [v_in[n] for n in names], "adamw_small")):
        outs[n] = (small_g[n], dlt, nm, nv)
    dmod_cols = lax.dynamic_slice_in_dim(dmod_all, me * ada_cols, ada_cols, axis=1)
    last = adamw("w_ada", _ada_wgrad(c_all.T, dmod_cols))
    for n, g in _unpack_grads(GROUP_REST, scatters["rest"].finish(last)).items():
        adamw(n, g)
    done = lax.optimization_barrier(tuple(outs[n][3] for n in outs))
    for n, g in _unpack_grads(GROUP_IN, scatters["in"].finish(done[0])).items():
        adamw(n, g)

    return (loss, grad_x, *[outs[n][0] for n in _WEIGHTS], *[outs[n][1] for n in _WEIGHTS], *[outs[n][2] for n in _WEIGHTS],
            *[outs[n][3] for n in _WEIGHTS])
```

```python
import functools

import jax
import jax.numpy as jnp
from jax import lax
from jax.experimental import pallas as pl
from jax.experimental.pallas import tpu as pltpu

F32 = jnp.float32
BF16 = jnp.bfloat16

D_MODEL = 1024
HEAD_DIM = 64
A_Q_HEADS = 16
A_WINDOW = 128
B_PATTERNS = ((128, 1), (512, 4), (2048, 16))
B_HEADS_PER_GROUP = 8
D_FF = 2816
QBLOCK = 128
ROPE_THETA = 10000.0
LN_EPS = 1e-5
DEEPNORM_ALPHA = 2.0 ** 0.25
NEG_INF = -1e30
ADAM_LR, ADAM_B1, ADAM_B2, ADAM_EPS, ADAM_WD, ADAM_STEP = 0.001, 0.9, 0.999, 1e-08, 0.01, 10

N_DEV = 8
LANES = 128
VMEM_LIMIT_BYTES = 56 * 1024 * 1024
MESH = pl.DeviceIdType.MESH

OFF_QA, OFF_KVA, OFF_QKVB, OFF_GAB = 0, 1024, 1280, 5888
GROUP_IN = (("w_in", 992),)
GROUP_REST = (("w_branch_a", 128), ("w_branch_b", 64), ("w_o", 128), ("w_gate_up", 704), ("w_down", 352))


def _params(*sem):
    return pltpu.CompilerParams(dimension_semantics=sem, vmem_limit_bytes=VMEM_LIMIT_BYTES)


def _sigmoid(x):
    return 1.0 / (1.0 + jnp.exp(-x))


_DIMS = {"nn": (((1,), (0,)), ((), ())), "nt": (((1,), (1,)), ((), ())), "tn": (((0,), (0,)), ((), ()))}


def _matmul(a, b, *, mode, tm, tn, tk, name, out_dtype=None, n=None, b_off=0, token=None, ins=(), outs=None, epilogue=None):
    if mode == "nn":
        (m, k), nn_ = a.shape, b.shape[1]
    elif mode == "nt":
        (m, k), nn_ = a.shape, (b.shape[0] if n is None else n)
    else:
        (k, m), nn_ = a.shape, b.shape[1]
    assert m % tm == 0 and nn_ % tn == 0 and k % tk == 0 and b_off % tn == 0, (name, m, nn_, k)
    nk = k // tk
    joff = b_off // tn
    if mode == "nn":
        a_spec = pl.BlockSpec((tm, tk), lambda i, j, kk: (i, kk))
        b_spec = pl.BlockSpec((tk, tn), lambda i, j, kk: (kk, j))
    elif mode == "nt":
        a_spec = pl.BlockSpec((tm, tk), lambda i, j, kk: (i, kk))
        b_spec = pl.BlockSpec((tn, tk), lambda i, j, kk: (j + joff, kk))
    else:
        a_spec = pl.BlockSpec((tk, tm), lambda i, j, kk: (kk, i))
        b_spec = pl.BlockSpec((tk, tn), lambda i, j, kk: (kk, j))
    dims = _DIMS[mode]
    has_token = token is not None
    plain = epilogue is None
    if plain:
        outs = [(jax.ShapeDtypeStruct((m, nn_), out_dtype), (tm, tn), lambda i, j: (i, j))]

        def epilogue(acc, i, j, in_refs, out_refs):
            out_refs[0][...] = acc.astype(out_refs[0].dtype)

    nin = len(ins)

    def body(*refs):
        a_ref, b_ref = refs[:2]
        in_refs = refs[2:2 + nin]
        out_refs = refs[2 + nin + has_token:-1]
        acc_ref = refs[-1]
        kk = pl.program_id(2)
        part = lax.dot_general(a_ref[...].astype(BF16), b_ref[...].astype(BF16), dims, preferred_element_type=F32)

        def finish(acc):
            epilogue(acc, pl.program_id(0), pl.program_id(1), in_refs, out_refs)

        if nk == 1:
            finish(part)
        else:
            @pl.when(kk == 0)
            def _():
                acc_ref[...] = part

            @pl.when(kk > 0)
            def _():
                acc_ref[...] += part

            @pl.when(kk == nk - 1)
            def _():
                finish(acc_ref[...])

    def spec(block, index):
        return pl.BlockSpec(block, lambda i, j, kk: index(i, j))

    in_specs, args = [a_spec, b_spec], [a, b]
    for arr, block, index in ins:
        in_specs.append(spec(block, index))
        args.append(arr)
    if has_token:
        in_specs.append(pl.BlockSpec(token.shape, lambda i, j, kk: (0, 0)))
        args.append(token)
    res = pl.pallas_call(
        body,
        name=name,
        grid=(m // tm, nn_ // tn, nk),
        in_specs=in_specs,
        out_specs=[spec(block, index) for _, block, index in outs],
        out_shape=[shape for shape, _, _ in outs],
        scratch_shapes=[pltpu.VMEM((tm, tn) if nk > 1 else (8, LANES), F32)],
        compiler_params=_params("arbitrary", "arbitrary", "arbitrary"),
    )(*args)
    return res[0] if plain else res


def _proj_rope(a, bt, cos, sin, *, n, b_off, rope_cols, tm, tn, name, out_dtype=F32):
    m, k = a.shape
    assert m % tm == 0 and n % tn == 0 and b_off % tn == 0 and rope_cols % tn == 0, name
    joff = b_off // tn
    nrope = rope_cols // tn

    def body(a_ref, b_ref, c_ref, s_ref, o_ref):
        acc = lax.dot_general(a_ref[...], b_ref[...], _DIMS["nt"], preferred_element_type=F32)
        j = pl.program_id(1)

        @pl.when(j < nrope)
        def _():
            o_ref[...] = _rope(acc, c_ref[...], s_ref[...], coarse=True).astype(o_ref.dtype)

        @pl.when(j >= nrope)
        def _():
            o_ref[...] = acc.astype(o_ref.dtype)

    table = pl.BlockSpec((tm, LANES), lambda i, j: (i, 0))
    return pl.pallas_call(
        body,
        name=name,
        grid=(m // tm, n // tn),
        in_specs=[pl.BlockSpec((tm, k), lambda i, j: (i, 0)), pl.BlockSpec((tn, k), lambda i, j: (j + joff, 0)), table, table],
        out_specs=pl.BlockSpec((tm, tn), lambda i, j: (i, j)),
        out_shape=jax.ShapeDtypeStruct((m, n), out_dtype),
        compiler_params=_params("parallel", "parallel"),
    )(a, bt, cos, sin)


ROW_TILE = 256


def _rows(width, col=0):
    return pl.BlockSpec((1, ROW_TILE, width), lambda b, t: (b, t, col))


def _per_batch(nrows, width):
    return pl.BlockSpec((1, nrows, width), lambda b, t: (b, 0, 0))


def _row_call(body, name, bsz, seq, in_specs, out_specs, out_shape, accumulates=False):
    return pl.pallas_call(
        body,
        name=name,
        grid=(bsz, seq // ROW_TILE),
        in_specs=in_specs,
        out_specs=out_specs,
        out_shape=out_shape,
        compiler_params=_params("parallel", "arbitrary" if accumulates else "parallel"),
    )


def _acc_rows(acc_ref, first, rows):
    @pl.when(first)
    def _():
        acc_ref[...] = jnp.zeros_like(acc_ref)

    for r, val in enumerate(rows):
        acc_ref[0, r:r + 1, :] += val


def _colsum(v):
    return jnp.sum(v, axis=0, keepdims=True)


def _ln_stats(z):
    mu = jnp.mean(z, axis=-1, keepdims=True)
    zc = z - mu
    var = jnp.mean(zc * zc, axis=-1, keepdims=True)
    rstd = lax.rsqrt(var + LN_EPS)
    return zc * rstd, rstd


def _ln_bwd(dxhat, xhat, rstd):
    m1 = jnp.mean(dxhat, axis=-1, keepdims=True)
    m2 = jnp.mean(dxhat * xhat, axis=-1, keepdims=True)
    return rstd * (dxhat - m1 - xhat * m2)


def _modulate_in(x, mod):
    bsz, seq, d = x.shape

    def body(x_ref, mod_ref, u_ref):
        u_ref[0] = (x_ref[0] * (1.0 + mod_ref[0, 1:2, :]) + mod_ref[0, 0:1, :]).astype(BF16)

    return _row_call(body, "modulate_in", bsz, seq, [_rows(d), _per_batch(8, d)], _rows(d),
                     jax.ShapeDtypeStruct((bsz, seq, d), BF16))(x, mod)


EP_TILE = 512


def _ep_specs(seq, d):
    tiles = seq // EP_TILE
    return ((EP_TILE, d), lambda i, j: (i, 0)), ((1, 8, d), lambda i, j: (i // tiles, 0, 0)), ((1, d), lambda i, j: (0, 0))


def _wo_ln1(merged, wo, x, mod, g, b, seq):
    ntok, d = x.shape
    row, per_b, whole = _ep_specs(seq, d)

    def epilogue(y, i, j, ins, outs):
        x_ref, mod_ref, g_ref, b_ref = ins
        y_ref, h_ref, u_ref = outs
        z = DEEPNORM_ALPHA * x_ref[...] + (1.0 + mod_ref[0, 2:3, :]) * y
        xhat, _ = _ln_stats(z)
        h = xhat * g_ref[...] + b_ref[...]
        y_ref[...] = y
        h_ref[...] = h
        u_ref[...] = (h * (1.0 + mod_ref[0, 4:5, :]) + mod_ref[0, 3:4, :]).astype(BF16)

    f32, bf16 = jax.ShapeDtypeStruct((ntok, d), F32), jax.ShapeDtypeStruct((ntok, d), BF16)
    return _matmul(merged, wo, mode="nn", tm=EP_TILE, tn=d, tk=d, name="w_o_ln1",
                   ins=[(x,) + row, (mod,) + per_b, (g,) + whole, (b,) + whole],
                   outs=[(f32,) + row, (f32,) + row, (bf16,) + row], epilogue=epilogue)


FF_HALF = D_FF // 2


def _interleave_gate_up(w):
    return w.reshape(2, 2, FF_HALF, w.shape[1]).transpose(1, 0, 2, 3).reshape(w.shape)


def _gate_up_silu(u2, wgut_i):
    ntok = u2.shape[0]

    def epilogue(h, i, j, ins, outs):
        h_ref, a_ref = outs
        hg, hu = h[:, :FF_HALF], h[:, FF_HALF:]
        h_ref[...] = h.astype(BF16)
        a_ref[...] = (hg * _sigmoid(hg) * hu).astype(BF16)

    return _matmul(u2, wgut_i, mode="nt", tm=EP_TILE, tn=2 * FF_HALF, tk=u2.shape[1], name="gate_up_silu",
                   outs=[(jax.ShapeDtypeStruct((ntok, 2 * D_FF), BF16), (EP_TILE, 2 * FF_HALF), lambda i, j: (i, j)),
                         (jax.ShapeDtypeStruct((ntok, D_FF), BF16), (EP_TILE, FF_HALF), lambda i, j: (i, j))],
                   epilogue=epilogue)


def _down_dgrad_silu_bwd(dy2, wd, h_i):
    ntok = dy2.shape[0]
    wide = ((EP_TILE, 2 * FF_HALF), lambda i, j: (i, j))

    def epilogue(da, i, j, ins, outs):
        h = ins[0][...].astype(F32)
        hg, hu = h[:, :FF_HALF], h[:, FF_HALF:]
        sg = _sigmoid(hg)
        outs[0][:, :FF_HALF] = (da * hu * (sg * (1.0 + hg * (1.0 - sg)))).astype(BF16)
        outs[0][:, FF_HALF:] = (da * (hg * sg)).astype(BF16)

    return _matmul(dy2, wd, mode="nt", tm=EP_TILE, tn=FF_HALF, tk=dy2.shape[1], name="down_dgrad_silu_bwd",
                   ins=[(h_i,) + wide], outs=[(jax.ShapeDtypeStruct((ntok, 2 * D_FF), BF16),) + wide], epilogue=epilogue)[0]


def _down_ln2_loss_bwd(a, wd, h1, mod, g, b, target, seq):
    ntok, d = h1.shape
    row, per_b, whole = _ep_specs(seq, d)
    tiles = seq // EP_TILE

    def epilogue(y, i, j, ins, outs):
        h_ref, mod_ref, g_ref, b_ref, t_ref = ins
        dy_ref, dh_ref, acc_ref = outs
        gate = 1.0 + mod_ref[0, 5:6, :]
        z = DEEPNORM_ALPHA * h_ref[...] + gate * y
        xhat, rstd = _ln_stats(z)
        diff = xhat * g_ref[...] + b_ref[...] - t_ref[...]
        loss = 0.5 * jnp.sum(jnp.sum(diff * diff, axis=-1, keepdims=True) / d, axis=0, keepdims=True)
        dout = diff / d
        dz = _ln_bwd(dout * g_ref[...], xhat, rstd)
        dy_ref[...] = (gate * dz).astype(BF16)
        dh_ref[...] = DEEPNORM_ALPHA * dz
        _acc_rows(acc_ref, i % tiles == 0,
                  [_colsum(dout * xhat), _colsum(dout), _colsum(dz * y), jnp.broadcast_to(loss, (1, d))])

    return _matmul(a, wd, mode="nn", tm=EP_TILE, tn=d, tk=a.shape[1], name="down_ln2_loss_bwd",
                   ins=[(h1,) + row, (mod,) + per_b, (g,) + whole, (b,) + whole, (target,) + row],
                   outs=[(jax.ShapeDtypeStruct((ntok, d), BF16),) + row, (jax.ShapeDtypeStruct((ntok, d), F32),) + row,
                         (jax.ShapeDtypeStruct((ntok // seq, 8, d), F32),) + per_b], epilogue=epilogue)


def _gate_up_dgrad_ln1_bwd(dh, wgut, dh1a, x, y1, mod, g, b, seq):
    ntok, d = x.shape
    row, per_b, whole = _ep_specs(seq, d)
    tiles = seq // EP_TILE

    def epilogue(du, i, j, ins, outs):
        dh_ref, x_ref, y_ref, mod_ref, g_ref, b_ref = ins
        dy_ref, dx_ref, acc_ref = outs
        y = y_ref[...]
        gate = 1.0 + mod_ref[0, 2:3, :]
        z = DEEPNORM_ALPHA * x_ref[...] + gate * y
        xhat, rstd = _ln_stats(z)
        h1 = xhat * g_ref[...] + b_ref[...]
        dh1 = dh_ref[...] + du * (1.0 + mod_ref[0, 4:5, :])
        dz = _ln_bwd(dh1 * g_ref[...], xhat, rstd)
        dy_ref[...] = (gate * dz).astype(BF16)
        dx_ref[...] = DEEPNORM_ALPHA * dz
        _acc_rows(acc_ref, i % tiles == 0,
                  [_colsum(dh1 * xhat), _colsum(dh1), _colsum(dz * y), _colsum(du * h1), _colsum(du)])

    return _matmul(dh, wgut, mode="nn", tm=EP_TILE, tn=d, tk=D_FF, name="gate_up_dgrad_ln1_bwd",
                   ins=[(dh1a,) + row, (x,) + row, (y1,) + row, (mod,) + per_b, (g,) + whole, (b,) + whole],
                   outs=[(jax.ShapeDtypeStruct((ntok, d), BF16),) + row, (jax.ShapeDtypeStruct((ntok, d), F32),) + row,
                         (jax.ShapeDtypeStruct((ntok // seq, 8, d), F32),) + per_b], epilogue=epilogue)


def _wo_dgrad_gate_bwd(dy1, wo, gab, ya, yb):
    ntok, d = ya.shape
    tm, tn = 1024, 512
    tile = ((tm, tn), lambda i, j: (i, j))
    tile_b = ((tm, tn), lambda i, j: (i, j + d // tn))

    def epilogue(dm_, i, j, ins, outs):
        ga_ref, gb_ref, ya_ref, yb_ref = ins
        dya_ref, dyb_ref, dga_ref, dgb_ref = outs
        sa, sb = _sigmoid(ga_ref[...].astype(F32)), _sigmoid(gb_ref[...].astype(F32))
        dya_ref[...] = (dm_ * sa).astype(BF16)
        dyb_ref[...] = (dm_ * sb).astype(BF16)
        dga_ref[...] = (dm_ * ya_ref[...].astype(F32) * sa * (1.0 - sa)).astype(BF16)
        dgb_ref[...] = (dm_ * yb_ref[...].astype(F32) * sb * (1.0 - sb)).astype(BF16)

    shp = jax.ShapeDtypeStruct((ntok, d), BF16)
    return _matmul(dy1, wo, mode="nt", tm=tm, tn=tn, tk=d, name="w_o_dgrad_gate_bwd",
                   ins=[(gab,) + tile, (gab,) + tile_b, (ya,) + tile, (yb,) + tile],
                   outs=[(shp,) + tile] * 4, epilogue=epilogue)


def _w_in_dgrad_grad_x(dproj, wint, dxa, x, mod, seq, token):
    ntok, d = x.shape
    row, per_b, _ = _ep_specs(seq, d)
    tiles = seq // EP_TILE

    def epilogue(du, i, j, ins, outs):
        dxa_ref, x_ref, mod_ref = ins
        gx_ref, acc_ref = outs
        gx_ref[...] = dxa_ref[...] + du * (1.0 + mod_ref[0, 1:2, :])
        _acc_rows(acc_ref, i % tiles == 0, [_colsum(du * x_ref[...]), _colsum(du)])

    return _matmul(dproj, wint, mode="nn", tm=EP_TILE, tn=d, tk=wint.shape[0] // 2, name="w_in_dgrad_grad_x", token=token,
                   ins=[(dxa,) + row, (x,) + row, (mod,) + per_b],
                   outs=[(jax.ShapeDtypeStruct((ntok, d), F32),) + row, (jax.ShapeDtypeStruct((ntok // seq, 8, d), F32),) + per_b],
                   epilogue=epilogue)


def _branch_b_gate_merge(ob, wbbt, gab, ya):
    ntok, d = ya.shape
    tm, tn = 1024, 512
    tile = ((tm, tn), lambda i, j: (i, j))
    tile_b = ((tm, tn), lambda i, j: (i, j + d // tn))

    def epilogue(yb, i, j, ins, outs):
        ga_ref, gb_ref, ya_ref = ins
        yb_ref, merged_ref = outs
        yb_ref[...] = yb.astype(BF16)
        merged_ref[...] = (_sigmoid(ga_ref[...].astype(F32)) * ya_ref[...].astype(F32)
                           + _sigmoid(gb_ref[...].astype(F32)) * yb).astype(BF16)

    shp = jax.ShapeDtypeStruct((ntok, d), BF16)
    return _matmul(ob, wbbt, mode="nt", tm=tm, tn=tn, tk=ob.shape[1], name="branch_b_gate_merge",
                   ins=[(gab,) + tile, (gab,) + tile_b, (ya,) + tile], outs=[(shp,) + tile] * 2, epilogue=epilogue)


def _segsum64(v):
    rows, width = v.shape
    ri = lax.broadcasted_iota(jnp.int32, (LANES, LANES), 0) // HEAD_DIM
    ci = lax.broadcasted_iota(jnp.int32, (LANES, LANES), 1) // HEAD_DIM
    ones = jnp.where(ri == ci, 1.0, 0.0).astype(BF16)
    out = []
    for c in range(width // LANES):
        part = v[:, c * LANES:(c + 1) * LANES]
        hi = part.astype(BF16)
        lo = (part - hi.astype(F32)).astype(BF16)
        out.append(jnp.dot(hi, ones, preferred_element_type=F32) + jnp.dot(lo, ones, preferred_element_type=F32))
    return jnp.concatenate(out, axis=1) if len(out) > 1 else out[0]


def _merge_b(os_, ls_):
    bsz, seq, w = os_[0].shape

    def body(o0, o1, o2, l0, l1, l2, ob_ref):
        ls = [l0[0], l1[0], l2[0]]
        mx = jnp.maximum(jnp.maximum(ls[0], ls[1]), ls[2])
        es = [jnp.exp(l - mx) for l in ls]
        den = es[0] + es[1] + es[2]
        ob_ref[0] = ((es[0] / den) * o0[0].astype(F32) + (es[1] / den) * o1[0].astype(F32)
                     + (es[2] / den) * o2[0].astype(F32)).astype(BF16)

    return _row_call(body, "merge_b", bsz, seq, [_rows(w)] * 6, _rows(w),
                     jax.ShapeDtypeStruct((bsz, seq, w), BF16))(*os_, *ls_)


def _branch_b_dgrad_merge_bwd(dyb, wbbt, os_, ls_):
    ntok, w = os_[0].shape
    row = ((EP_TILE, w), lambda i, j: (i, 0))

    def epilogue(dob_, i, j, ins, outs):
        os_r, ls_r = ins[:3], ins[3:]
        do_r, dd_r = outs[:3], outs[3:]
        ls = [l[...] for l in ls_r]
        mx = jnp.maximum(jnp.maximum(ls[0], ls[1]), ls[2])
        es = [jnp.exp(l - mx) for l in ls]
        den = es[0] + es[1] + es[2]
        ws = [e / den for e in es]
        dws = [_segsum64(dob_ * o[...].astype(F32)) for o in os_r]
        mean = ws[0] * dws[0] + ws[1] * dws[1] + ws[2] * dws[2]
        for wg, do_ref, dd_ref in zip(ws, do_r, dd_r):
            do_ref[...] = wg * dob_
            dd_ref[...] = -wg * mean

    shp = jax.ShapeDtypeStruct((ntok, w), F32)
    return _matmul(dyb, wbbt, mode="nn", tm=EP_TILE, tn=w, tk=dyb.shape[1], name="branch_b_dgrad_merge_bwd",
                   ins=[(v,) + row for v in list(os_) + list(ls_)], outs=[(shp,) + row] * 6, epilogue=epilogue)


def _branch_a_dgrad_delta(dya, wba, oa, lse_a, sinks_exp, seq):
    ntok, w = oa.shape
    row, per_b, whole = _ep_specs(seq, w)
    tiles = seq // EP_TILE

    def epilogue(do_, i, j, ins, outs):
        o_ref, l_ref, s_ref = ins
        do_ref, dd_ref, acc_ref = outs
        dd = -_segsum64(do_ * o_ref[...].astype(F32))
        do_ref[...] = do_
        dd_ref[...] = dd
        _acc_rows(acc_ref, i % tiles == 0, [_colsum(dd * jnp.exp(s_ref[...] - l_ref[...]))])

    shp = jax.ShapeDtypeStruct((ntok, w), F32)
    return _matmul(dya, wba, mode="nt", tm=EP_TILE, tn=w, tk=dya.shape[1], name="branch_a_dgrad_delta",
                   ins=[(oa,) + row, (lse_a,) + row, (sinks_exp,) + whole],
                   outs=[(shp,) + row, (shp,) + row, (jax.ShapeDtypeStruct((ntok // seq, 8, w), F32),) + per_b],
                   epilogue=epilogue)


def _swap_halves(v):
    src = lax.broadcasted_iota(jnp.int32, (LANES, LANES), 0)
    dst = lax.broadcasted_iota(jnp.int32, (LANES, LANES), 1)
    partner = jnp.where((dst % HEAD_DIM) < HEAD_DIM // 2, dst + HEAD_DIM // 2, dst - HEAD_DIM // 2)
    perm = jnp.where(src == partner, 1.0, 0.0).astype(BF16)
    hi = v.astype(BF16)
    lo = (v - hi.astype(F32)).astype(BF16)
    return jnp.dot(hi, perm, preferred_element_type=F32) + jnp.dot(lo, perm, preferred_element_type=F32)


def _swap_halves_roll(v):
    lane = lax.broadcasted_iota(jnp.int32, v.shape, 1)
    return jnp.where((lane % HEAD_DIM) < HEAD_DIM // 2, pltpu.roll(v, LANES - HEAD_DIM // 2, 1),
                     pltpu.roll(v, HEAD_DIM // 2, 1))


def _swap_halves_coarse(v):
    src = lax.broadcasted_iota(jnp.int32, (LANES, LANES), 0)
    dst = lax.broadcasted_iota(jnp.int32, (LANES, LANES), 1)
    partner = jnp.where((dst % HEAD_DIM) < HEAD_DIM // 2, dst + HEAD_DIM // 2, dst - HEAD_DIM // 2)
    perm = jnp.where(src == partner, 1.0, 0.0).astype(BF16)
    return jnp.dot(v.astype(BF16), perm, preferred_element_type=F32)


def _rope(v, cos, sin, sign=1.0, mxu=True, coarse=False):
    swap = (_swap_halves_coarse if coarse else _swap_halves) if mxu else _swap_halves_roll
    out = []
    for c in range(v.shape[1] // LANES):
        part = v[:, c * LANES:(c + 1) * LANES]
        out.append(part * cos + sign * (swap(part) * sin))
    return jnp.concatenate(out, axis=1) if len(out) > 1 else out[0]


def _half_mask(shape, half):
    lane = lax.broadcasted_iota(jnp.int32, shape, len(shape) - 1) % LANES
    return (lane < HEAD_DIM) if half == 0 else (lane >= HEAD_DIM)


def _dup_half(v, half):
    return jnp.where(_half_mask(v.shape, half), v, pltpu.roll(v, HEAD_DIM, 1))


def _fold_halves(v):
    return v + pltpu.roll(v, HEAD_DIM, 1)


def _pick_halves(lo_rows, hi_rows):
    return jnp.where(_half_mask(lo_rows.shape, 0), lo_rows, hi_rows)


def _stack_masked(v, pairs):
    parts = []
    for c in pairs:
        pair = v[:, c * LANES:(c + 1) * LANES]
        parts += [jnp.where(_half_mask(pair.shape, half), pair, 0.0) for half in (0, 1)]
    return jnp.concatenate(parts, axis=0)


def _stack_pair_cols(v, pairs):
    return jnp.concatenate([v[:, c * LANES + half * HEAD_DIM:c * LANES + half * HEAD_DIM + 1] for c in pairs for half in (0, 1)],
                           axis=0)


ATTN_UNITS = 16


def _class_rows(r):
    return [pl.ds(0, QBLOCK)] if r == 1 else [pl.ds(rho, QBLOCK, stride=r) for rho in range(r)]


def _band_mask(nrows, nk, blk, n_back, has_prev):
    qi = lax.broadcasted_iota(jnp.int32, (nrows, nk), 0) % QBLOCK
    ki = lax.broadcasted_iota(jnp.int32, (nrows, nk), 1)
    if has_prev:
        dist = qi + QBLOCK - ki
        return (dist >= 0) & (dist <= n_back) & ((ki >= QBLOCK) | (blk > 0))
    dist = qi - ki
    return (dist >= 0) & (dist <= n_back)


def _attn_fwd(q_arr, k_arr, v_arr, *, name, npair, gqa, q_col, k_col, v_col, nchunk, r, n_back, sinks=None):
    bsz, seq, _ = q_arr.shape
    rr = QBLOCK * r
    nblk = seq // rr
    qw = npair * LANES
    kw = LANES if gqa else qw
    has_prev = nblk > 1
    has_sink = sinks is not None
    scale = HEAD_DIM ** -0.5

    def body(*refs):
        refs = list(refs)
        q_ref, kc_ref, vc_ref = refs[:3]
        pos = 3
        if has_prev:
            kp_ref, vp_ref = refs[pos:pos + 2]
            pos += 2
        if has_sink:
            sink_ref = refs[pos]
            pos += 1
        o_ref, lse_ref = refs[pos:pos + 2]
        if r > 1:
            stage_o = refs[pos + 2]
        blk = pl.program_id(2)
        nk = (2 if has_prev else 1) * QBLOCK
        valid = _band_mask(QBLOCK, nk, blk, n_back, has_prev)
        per = npair // 2
        classes = _class_rows(r)
        step = max(1, ATTN_UNITS // (2 * npair))
        for first in range(0, len(classes), step):
            batch = classes[first:first + step]
            units = []
            for ci, rows in enumerate(batch):
                q = q_ref[0, rows, :] * scale
                k, v = kc_ref[0, rows, :], vc_ref[0, rows, :]
                if has_prev:
                    k = jnp.concatenate([kp_ref[0, rows, :], k], axis=0)
                    v = jnp.concatenate([vp_ref[0, rows, :], v], axis=0)
                if gqa:
                    kdup = [_dup_half(k, hk).astype(BF16) for hk in range(2)]
                    vdup = [_dup_half(v, hk) for hk in range(2)]
                for c in range(npair):
                    sl = slice(c * LANES, (c + 1) * LANES)
                    qc = q[:, sl]
                    kc, vc = (kdup[c // per], vdup[c // per]) if gqa else (k[:, sl].astype(BF16), v[:, sl])
                    for half in (0, 1):
                        qm = jnp.where(_half_mask(qc.shape, half), qc, 0.0).astype(BF16)
                        vm = jnp.where(_half_mask(vc.shape, half), vc, 0.0).astype(BF16)
                        s = lax.dot_general(qm, kc, _DIMS["nt"], preferred_element_type=F32)
                        units.append(dict(ci=ci, c=c, half=half, s=s, vm=vm, sk=sink_ref[2 * c + half] if has_sink else None))
            for u in units:
                s = jnp.where(valid, u["s"], NEG_INF)
                m = jnp.max(s, axis=1, keepdims=True)
                if has_sink:
                    m = jnp.maximum(m, u["sk"])
                p = jnp.exp(s - m)
                den = jnp.sum(p, axis=1, keepdims=True)
                if has_sink:
                    den = den + jnp.exp(u["sk"] - m)
                u.update(p=p.astype(BF16), den=den, lse=m + jnp.log(den))
            for u in units:
                u["o"] = jnp.dot(u["p"], u["vm"], preferred_element_type=F32) / u["den"]
            for ci, rows in enumerate(batch):
                outs, lses = [None] * npair, [None] * npair
                for u in units:
                    if u["ci"] != ci:
                        continue
                    c, o = u["c"], u["o"]
                    lse = jnp.broadcast_to(u["lse"], o.shape)
                    outs[c] = o if u["half"] == 0 else outs[c] + o
                    lses[c] = lse if u["half"] == 0 else _pick_halves(lses[c], lse)
                o_new = jnp.concatenate(outs, axis=1) if npair > 1 else outs[0]
                if r > 1:
                    stage_o[rows, :] = o_new
                else:
                    o_ref[0] = o_new.astype(BF16)
                lse_ref[0, rows, :] = jnp.concatenate(lses, axis=1) if npair > 1 else lses[0]
        if r > 1:
            o_ref[0] = stage_o[...].astype(BF16)

    def cur(width, col0):
        return pl.BlockSpec((1, rr, width), lambda b, c, i: (b, i, col0 + c))

    def prev(width, col0):
        return pl.BlockSpec((1, rr, width), lambda b, c, i: (b, jnp.maximum(i - 1, 0), col0 + c))

    in_specs = [cur(qw, q_col), cur(kw, k_col), cur(kw, v_col)]
    args = [q_arr, k_arr, v_arr]
    if has_prev:
        in_specs += [prev(kw, k_col), prev(kw, v_col)]
        args += [k_arr, v_arr]
    if has_sink:
        in_specs.append(pl.BlockSpec(memory_space=pltpu.SMEM))
        args.append(sinks)
    return pl.pallas_call(
        body,
        name=name,
        grid=(bsz, nchunk, nblk),
        in_specs=in_specs,
        out_specs=[pl.BlockSpec((1, rr, qw), lambda b, c, i: (b, i, c))] * 2,
        out_shape=[jax.ShapeDtypeStruct((bsz, seq, nchunk * qw), BF16), jax.ShapeDtypeStruct((bsz, seq, nchunk * qw), F32)],
        scratch_shapes=[pltpu.VMEM((rr, qw), F32)] if r > 1 else [],
        compiler_params=_params("parallel", "parallel", "parallel"),
    )(*args)


def _attn_bwd(q_arr, k_arr, v_arr, cos, sin, do, lse, dd, *, name, npair, gqa, q_col, k_col, v_col, nchunk, r, n_back,
              token=None):
    bsz, seq, _ = q_arr.shape
    rr = QBLOCK * r
    nblk = seq // rr
    qw = npair * LANES
    kw = LANES if gqa else qw
    has_next = nblk > 1
    has_token = token is not None
    staged = r > 1
    scale = HEAD_DIM ** -0.5

    def body(*refs):
        refs = list(refs)
        k_ref, v_ref, c_ref, s_ref = refs[:4]
        tile_refs = [refs[4:8]]
        pos = 8
        if has_next:
            tile_refs.append(refs[pos:pos + 4])
            pos += 4
        if has_token:
            pos += 1
        dq_ref, dk_ref, dv_ref = refs[pos:pos + 3]
        carry_ref = refs[pos + 3]
        if staged:
            stage_q, stage_k, stage_v = refs[pos + 4:pos + 7]
        blk = pl.program_id(2)
        if has_next:
            @pl.when(blk == 0)
            def _():
                carry_ref[...] = jnp.zeros_like(carry_ref)

        nrows = (npair if gqa else 1) * QBLOCK
        qi = lax.broadcasted_iota(jnp.int32, (nrows, QBLOCK), 0) % QBLOCK
        ki = lax.broadcasted_iota(jnp.int32, (nrows, QBLOCK), 1)
        valids = [qi >= ki, (qi + QBLOCK - ki <= n_back) & (blk + 1 < nblk)]
        per = npair // 2
        ntile = len(tile_refs)
        cat = lambda parts: jnp.concatenate(parts, axis=1) if len(parts) > 1 else parts[0]
        classes = _class_rows(r)
        step = max(1, ATTN_UNITS // (ntile * (2 if gqa else 2 * npair)))
        for first in range(0, len(classes), step):
            batch = classes[first:first + step]
            units = []
            for ci, rows in enumerate(batch):
                tiles = [(q_ref[0, rows, :] * scale, do_ref[0, rows, :], l_ref[0, rows, :], d_ref[0, rows, :])
                         for q_ref, do_ref, l_ref, d_ref in tile_refs]
                k, v = k_ref[0, rows, :], v_ref[0, rows, :]
                if gqa:
                    for hk in range(2):
                        pairs = list(range(hk * per, (hk + 1) * per))
                        kd, vd = _dup_half(k, hk).astype(BF16), _dup_half(v, hk).astype(BF16)
                        for t, (q, do_, l_, d_) in enumerate(tiles):
                            units.append(dict(ci=ci, t=t, hk=hk, pairs=pairs, qs=_stack_masked(q, pairs).astype(BF16),
                                              dos=_stack_masked(do_, pairs).astype(BF16), lcol=_stack_pair_cols(l_, pairs),
                                              dcol=_stack_pair_cols(d_, pairs), kmat=kd, vmat=vd, kdq=kd))
                else:
                    for c in range(npair):
                        sl = slice(c * LANES, (c + 1) * LANES)
                        kc, vcb = k[:, sl], v[:, sl].astype(BF16)
                        kcb = kc.astype(BF16)
                        for t, (q, do_, l_, d_) in enumerate(tiles):
                            for half in (0, 1):
                                hm = _half_mask(kc.shape, half)
                                col = c * LANES + half * HEAD_DIM
                                units.append(dict(ci=ci, t=t, c=c, half=half, qs=jnp.where(hm, q[:, sl], 0.0).astype(BF16),
                                                  dos=jnp.where(hm, do_[:, sl], 0.0).astype(BF16), lcol=l_[:, col:col + 1],
                                                  dcol=d_[:, col:col + 1], kmat=kcb, vmat=vcb,
                                                  kdq=jnp.where(hm, kc, 0.0).astype(BF16)))
            for u in units:
                u["s"] = lax.dot_general(u["qs"], u["kmat"], _DIMS["nt"], preferred_element_type=F32)
                u["dp"] = lax.dot_general(u["dos"], u["vmat"], _DIMS["nt"], preferred_element_type=F32)
            for u in units:
                p = jnp.exp(jnp.where(valids[u["t"]], u["s"], NEG_INF) - u["lcol"])
                u["ds"] = (p * (u["dp"] + u["dcol"])).astype(BF16)
                u["p"] = p.astype(BF16)
            for u in units:
                u["dv"] = lax.dot_general(u["p"], u["dos"], _DIMS["tn"], preferred_element_type=F32)
                u["dk"] = lax.dot_general(u["ds"], u["qs"], _DIMS["tn"], preferred_element_type=F32)
                u["dq"] = jnp.dot(u["ds"], u["kdq"], preferred_element_type=F32) * scale
            for ci, rows in enumerate(batch):
                mine = [u for u in units if u["ci"] == ci]
                dq = [[None] * npair for _ in range(ntile)]
                if gqa:
                    dk_out = dv_out = None
                    for hk in range(2):
                        us = [u for u in mine if u["hk"] == hk]
                        for u in us:
                            for i, c in enumerate(u["pairs"]):
                                dq[u["t"]][c] = _pick_halves(u["dq"][2 * i * QBLOCK:(2 * i + 1) * QBLOCK],
                                                             u["dq"][(2 * i + 1) * QBLOCK:(2 * i + 2) * QBLOCK])
                        dk_h = _fold_halves(functools.reduce(jnp.add, [u["dk"] for u in us]))
                        dv_h = _fold_halves(functools.reduce(jnp.add, [u["dv"] for u in us]))
                        dk_out = dk_h if hk == 0 else _pick_halves(dk_out, dk_h)
                        dv_out = dv_h if hk == 0 else _pick_halves(dv_out, dv_h)
                else:
                    dks, dvs = [], []
                    for c in range(npair):
                        us = [u for u in mine if u["c"] == c]
                        dks.append(functools.reduce(jnp.add, [u["dk"] for u in us]))
                        dvs.append(functools.reduce(jnp.add, [u["dv"] for u in us]))
                        for t in range(ntile):
                            dq[t][c] = functools.reduce(jnp.add, [u["dq"] for u in us if u["t"] == t])
                    dk_out, dv_out = cat(dks), cat(dvs)
                ck, sk_ = c_ref[0, rows, :], s_ref[0, rows, :]
                dk_new = _rope(dk_out, ck, sk_, sign=-1.0, mxu=gqa, coarse=True)
                dq_cur = cat(dq[0])
                if has_next:
                    dq_cur = dq_cur + carry_ref[rows, :]
                    carry_ref[rows, :] = cat(dq[1])
                dq_new = _rope(dq_cur, ck, sk_, sign=-1.0, mxu=gqa, coarse=True)
                if staged:
                    stage_q[rows, :], stage_k[rows, :], stage_v[rows, :] = dq_new, dk_new, dv_out
                else:
                    dq_ref[0], dk_ref[0], dv_ref[0] = dq_new.astype(BF16), dk_new.astype(BF16), dv_out.astype(BF16)
        if staged:
            dq_ref[0], dk_ref[0], dv_ref[0] = stage_q[...].astype(BF16), stage_k[...].astype(BF16), stage_v[...].astype(BF16)

    def at(width, col0, shift):
        return pl.BlockSpec((1, rr, width), lambda b, c, i: (b, jnp.minimum(i + shift, nblk - 1), col0 + c))

    in_specs = [at(kw, k_col, 0), at(kw, v_col, 0), pl.BlockSpec((1, rr, LANES), lambda b, c, i: (b, i, 0)),
                pl.BlockSpec((1, rr, LANES), lambda b, c, i: (b, i, 0))]
    args = [k_arr, v_arr, cos, sin]
    for shift in (0, 1) if has_next else (0,):
        in_specs += [at(qw, q_col, shift), at(qw, 0, shift), at(qw, 0, shift), at(qw, 0, shift)]
        args += [q_arr, do, lse, dd]
    if has_token:
        in_specs.append(pl.BlockSpec(token.shape, lambda b, c, i: (0, 0)))
        args.append(token)
    return pl.pallas_call(
        body,
        name=name,
        grid=(bsz, nchunk, nblk),
        in_specs=in_specs,
        out_specs=[pl.BlockSpec((1, rr, qw), lambda b, c, i: (b, i, c)),
                   pl.BlockSpec((1, rr, kw), lambda b, c, i: (b, i, c)),
                   pl.BlockSpec((1, rr, kw), lambda b, c, i: (b, i, c))],
        out_shape=[jax.ShapeDtypeStruct((bsz, seq, nchunk * qw), BF16),
                   jax.ShapeDtypeStruct((bsz, seq, nchunk * kw), BF16),
                   jax.ShapeDtypeStruct((bsz, seq, nchunk * kw), BF16)],
        scratch_shapes=[pltpu.VMEM((rr, qw) if has_next else (8, LANES), F32)] +
                       ([pltpu.VMEM((rr, qw), F32), pltpu.VMEM((rr, kw), F32), pltpu.VMEM((rr, kw), F32)] if staged else []),
        compiler_params=_params("parallel", "parallel", "arbitrary"),
    )(*args)


B_CHUNKS = {1: (4, 1), 4: (1, 4), 16: (1, 4)}


def _rope_tables(positions):
    half = HEAD_DIM // 2
    inv = ROPE_THETA ** (-jnp.arange(half, dtype=F32) / half)
    ang = positions.astype(F32)[..., None] * inv
    cos, sin = jnp.cos(ang), jnp.sin(ang)
    return jnp.concatenate([cos] * 4, axis=-1), jnp.concatenate([-sin, sin, -sin, sin], axis=-1)


def _layer_step(x, mod, tables, sinks, ln1_g, ln1_b, ln2_g, ln2_b, target, get_w_in, get_rest, hook):
    bsz, seq, d = x.shape
    ntok = bsz * seq
    flat = lambda v: v.reshape(ntok, v.shape[-1])
    unflat = lambda v: v.reshape(bsz, seq, v.shape[-1])
    cos, sin = tables
    mm = functools.partial(_matmul, tm=1024, tk=1024)
    scalar = lambda tok: 0.0 if tok is None else tok[0, 0]

    u1 = _modulate_in(x, mod)
    u1f = flat(u1)
    wint = get_w_in(u1)
    cosf, sinf = flat(cos), flat(sin)
    proj = functools.partial(_proj_rope, u1f, wint, cosf, sinf, tm=2048)
    qkvb = unflat(proj(n=4608, b_off=OFF_QKVB, rope_cols=3072, tn=256, name="proj_qkvb"))
    b_kws, os_, ls_ = [], [], []
    for g, (window, r) in enumerate(B_PATTERNS):
        npair, nch = B_CHUNKS[r]
        per = B_HEADS_PER_GROUP // (2 * npair)
        nsec = len(B_PATTERNS) * per
        kw_ = dict(npair=npair, gqa=False, q_col=g * per, k_col=nsec + g * per, v_col=2 * nsec + g * per, nchunk=nch, r=r,
                   n_back=window // r)
        b_kws.append(kw_)
        o_g, l_g = _attn_fwd(qkvb, qkvb, qkvb, name=f"attn_b{g}_fwd", **kw_)
        os_.append(o_g)
        ls_.append(l_g)
    ob = _merge_b(os_, ls_)
    tok = hook("projected", ob)
    proj = functools.partial(_proj_rope, u1f, wint, cosf + scalar(tok), sinf, tm=2048)
    gab = unflat(proj(n=2048, b_off=OFF_GAB, rope_cols=0, tn=256, name="proj_gab", out_dtype=BF16))
    qa = unflat(proj(n=1024, b_off=OFF_QA, rope_cols=1024, tn=512, name="proj_qa"))
    kva = unflat(proj(n=256, b_off=OFF_KVA, rope_cols=128, tn=128, name="proj_kva"))
    a_kw = dict(npair=A_Q_HEADS // 2, gqa=True, q_col=0, k_col=0, v_col=1, nchunk=1, r=1, n_back=A_WINDOW - 1)
    after_gab = jnp.minimum(jnp.abs(gab[0, 0, 0].astype(F32)), 0.0)
    oa, lse_a = _attn_fwd(qa, kva, kva, name="attn_a_fwd", sinks=sinks.reshape(A_Q_HEADS) + after_gab, **a_kw)
    rest = get_rest(oa)
    wba, wbbt, wo, wgut, wd = (rest[n] for n in ("w_branch_a", "w_branch_b", "w_o", "w_gate_up", "w_down"))
    ya = unflat(mm(flat(oa), wba, mode="nn", out_dtype=BF16, tn=512, name="branch_a"))
    ybf, mergedf = _branch_b_gate_merge(flat(ob), wbbt, flat(gab), flat(ya))
    xf = flat(x)
    y1f, h1f, u2f = _wo_ln1(mergedf, wo, xf, mod, ln1_g, ln1_b, seq)
    wgut_i = _interleave_gate_up(wgut)
    hf, af = _gate_up_silu(u2f, wgut_i)

    dy2f, dh1af, acc2 = _down_ln2_loss_bwd(af, wd, h1f, mod, ln2_g, ln2_b, flat(target), seq)
    g_wd = _matmul(af, dy2f, mode="tn", out_dtype=BF16, tm=256, tn=1024, tk=ntok, name="down_wgrad")
    dhf = _down_dgrad_silu_bwd(dy2f, wd, hf)
    g_wgut = _interleave_gate_up(_matmul(dhf, u2f, mode="tn", out_dtype=BF16, tm=256, tn=1024, tk=ntok, name="gate_up_wgrad"))
    dy1f, dxaf, acc1 = _gate_up_dgrad_ln1_bwd(dhf, wgut_i, dh1af, xf, y1f, mod, ln1_g, ln1_b, seq)
    g_wo = _matmul(mergedf, dy1f, mode="tn", out_dtype=BF16, tm=256, tn=1024, tk=ntok, name="w_o_wgrad")
    dyaf, dybf, dgaf, dgbf = _wo_dgrad_gate_bwd(dy1f, wo, flat(gab), flat(ya), ybf)
    g_wba = _matmul(flat(oa), dyaf, mode="tn", out_dtype=BF16, tm=256, tn=1024, tk=ntok, name="branch_a_wgrad")
    g_wbbt = _matmul(dybf, flat(ob), mode="tn", out_dtype=BF16, tm=256, tn=512, tk=ntok, name="branch_b_wgrad")
    tok = hook("grads_rest", dict(w_branch_a=g_wba, w_branch_b=g_wbbt, w_o=g_wo, w_gate_up=g_wgut, w_down=g_wd))

    sinks_exp = jnp.repeat(sinks.reshape(1, A_Q_HEADS), HEAD_DIM, axis=1) + scalar(tok)
    doa, dd_a, acc_s = _branch_a_dgrad_delta(dyaf, wba, flat(oa), flat(lse_a), sinks_exp, seq)
    doa, dd_a = unflat(doa), unflat(dd_a)
    tok = hook("delta_done", dd_a)
    dqa, dka, dva = _attn_bwd(qa, kva, kva, cos, sin, doa, lse_a, dd_a, name="attn_a_bwd", token=tok, **a_kw)
    merged_bwd = [unflat(t) for t in _branch_b_dgrad_merge_bwd(dybf, wbbt, [flat(t) for t in os_], [flat(t) for t in ls_])]
    dqs, dks, dvs = [], [], []
    for g in range(len(B_PATTERNS)):
        dq_g, dk_g, dv_g = _attn_bwd(qkvb, qkvb, qkvb, cos, sin, merged_bwd[g], ls_[g], merged_bwd[3 + g],
                                     name=f"attn_b{g}_bwd", **b_kws[g])
        dqs.append(dq_g)
        dks.append(dk_g)
        dvs.append(dv_g)
    dproj = jnp.concatenate([t.astype(BF16) for t in [dqa, dka, dva] + dqs + dks + dvs] + [unflat(dgaf), unflat(dgbf)], axis=-1)
    dprojf = flat(dproj)
    g_wint = _matmul(dprojf, u1f, mode="tn", out_dtype=BF16, tm=256, tn=1024, tk=ntok, name="w_in_wgrad")
    tok = hook("grads_w_in", dict(w_in=g_wint))
    grad_x, acc0 = _w_in_dgrad_grad_x(dprojf, wint, dxaf, xf, mod, seq, tok)
    grad_x = unflat(grad_x)
    tok = hook("dgrad_done", grad_x)

    loss_part = jnp.sum(acc2[:, 3, 0])
    dmod = jnp.stack([acc0[:, 1], acc0[:, 0], acc1[:, 2], acc1[:, 4], acc1[:, 3], acc2[:, 2]], axis=1)
    small = jnp.stack([acc1[:, 0].sum(0), acc1[:, 1].sum(0), acc2[:, 0].sum(0), acc2[:, 1].sum(0), acc_s[:, 0].sum(0)])
    small = small + scalar(tok)
    return loss_part, grad_x, dmod, small


CHIP_FLIPS = (2, 4, 6)


def _my_place():
    return lax.axis_index("x"), lax.axis_index("y"), lax.axis_index("c")


def _flip(place, k):
    px, py, pc = place
    return (1 - px if k & 4 else px, 1 - py if k & 2 else py, 1 - pc if k & 1 else pc)


def _index(place):
    return 4 * place[0] + 2 * place[1] + place[2]


def _gather_small(v, name):
    rows, cols = v.shape

    def body(v_ref, out_ref, send_sems, recv_sems):
        me = _my_place()
        out_ref[_index(me)] = v_ref[...]
        copies = []
        for k in range(1, N_DEV):
            copies.append(pltpu.make_async_remote_copy(
                src_ref=v_ref, dst_ref=out_ref.at[_index(me)], send_sem=send_sems.at[k - 1], recv_sem=recv_sems.at[k - 1],
                device_id=_flip(me, k), device_id_type=MESH))
        for cp in copies:
            cp.start()
        for k in range(1, N_DEV):
            pltpu.make_async_remote_copy(
                src_ref=v_ref, dst_ref=out_ref.at[_index(_flip(me, k))], send_sem=send_sems.at[k - 1],
                recv_sem=recv_sems.at[k - 1], device_id=_flip(me, k), device_id_type=MESH).wait_recv()
        for cp in copies:
            cp.wait_send()

    return pl.pallas_call(
        body,
        name=name,
        out_shape=jax.ShapeDtypeStruct((N_DEV, rows, cols), v.dtype),
        in_specs=[pl.BlockSpec(memory_space=pltpu.VMEM)],
        out_specs=pl.BlockSpec(memory_space=pltpu.VMEM),
        scratch_shapes=[pltpu.SemaphoreType.DMA((N_DEV - 1,)), pltpu.SemaphoreType.DMA((N_DEV - 1,))],
        compiler_params=pltpu.CompilerParams(vmem_limit_bytes=VMEM_LIMIT_BYTES),
    )(v)


_HBM = pl.BlockSpec(memory_space=pltpu.HBM)
_SEM = pl.BlockSpec(memory_space=pltpu.SEMAPHORE)
_EFFECT = pltpu.SideEffectType.DATAFLOW_SIDE_EFFECTING


def _remote(src, dst, send_sems, recv_sems, j, to):
    return pltpu.make_async_remote_copy(src_ref=src, dst_ref=dst, send_sem=send_sems.at[j], recv_sem=recv_sems.at[j],
                                        device_id=to, device_id_type=MESH)


def _copies_start(name, bufs, make_copies, nsem):
    nbuf = len(bufs)

    def body(*refs):
        for cp in make_copies(refs[:nbuf], refs[nbuf], refs[nbuf + 1]):
            cp.start()
        refs[-1][...] = jnp.zeros_like(refs[-1])

    sems = pltpu.SemaphoreType.DMA((nsem,))
    res = pl.pallas_call(
        body, name=name,
        out_shape=(sems, sems, *[pltpu.HBM(v.shape, v.dtype) for v in bufs], jax.ShapeDtypeStruct((8, LANES), F32)),
        in_specs=(_HBM,) * nbuf, out_specs=(_SEM, _SEM) + (_HBM,) * nbuf + (pl.BlockSpec(memory_space=pltpu.VMEM),),
        input_output_aliases={i: 2 + i for i in range(nbuf)},
        compiler_params=pltpu.CompilerParams(has_side_effects=_EFFECT),
    )(*[pltpu.with_memory_space_constraint(v, pltpu.HBM) for v in bufs])
    return res[0], res[1], list(res[2:2 + nbuf]), res[-1]


def _copies_wait(name, started, make_copies, after):
    send_sems, recv_sems, bufs, _ = started
    nbuf = len(bufs)

    def body(*refs):
        for cp in make_copies(refs[:nbuf], refs[nbuf], refs[nbuf + 1]):
            cp.wait_send()
            cp.wait_recv()

    return list(pl.pallas_call(
        body, name=name,
        out_shape=tuple(pltpu.HBM(v.shape, v.dtype) for v in bufs),
        in_specs=(_HBM,) * nbuf + (_SEM, _SEM, pl.BlockSpec(memory_space=pl.ANY)), out_specs=(_HBM,) * nbuf,
        input_output_aliases={i: i for i in range(nbuf)},
        compiler_params=pltpu.CompilerParams(has_side_effects=_EFFECT),
    )(*bufs, send_sems, recv_sems, after))


def _to_sibling_copies(refs, send_sems, recv_sems):
    src_ref, land_ref = refs
    me = _my_place()
    return [_remote(src_ref.at[q, 1 - me[2]], land_ref.at[q], send_sems, recv_sems, q, _flip(me, 1)) for q in range(4)]


def _to_chips_copies(refs, send_sems, recv_sems):
    src_ref, land_ref = refs
    me = _my_place()
    copies = []
    for j, k in enumerate(CHIP_FLIPS):
        to = _flip(me, k)
        copies.append(_remote(src_ref.at[2 * to[0] + to[1]], land_ref.at[j], send_sems, recv_sems, j, to))
    return copies


class _Gather:
    def __init__(self, name, blocks):
        self.name, self.n = name, len(blocks)
        at_me = (_index(_my_place()), 0, 0)
        lands = [lax.dynamic_update_slice(lax.empty((N_DEV,) + v.shape, v.dtype), v[None], at_me) for v in blocks]
        self.first = _copies_start(name + "_start", list(blocks) + lands, self._first_copies, 4 * self.n)
        self.token = self.first[3]

    def _first_copies(self, refs, send_sems, recv_sems):
        me = _my_place()
        return [_remote(refs[w], refs[self.n + w].at[_index(me)], send_sems, recv_sems, 4 * w + j, _flip(me, k))
                for w in range(self.n) for j, k in enumerate((1,) + CHIP_FLIPS)]

    def _pass_copies(self, refs, send_sems, recv_sems):
        me = _my_place()
        copies = []
        for w, land in enumerate(refs):
            for j, k in enumerate(CHIP_FLIPS):
                slot = land.at[_index(_flip(me, k))]
                copies.append(_remote(slot, slot, send_sems, recv_sems, 3 * w + j, _flip(me, 1)))
        return copies

    def pass_on(self, after):
        lands = _copies_wait(self.name + "_wait", self.first, self._first_copies, after)[self.n:]
        self.second = _copies_start(self.name + "_pass_start", lands, self._pass_copies, 3 * self.n)
        return self.second[3]

    def finish(self, after):
        return _copies_wait(self.name + "_pass_wait", self.second, self._pass_copies, after)


SUM_SPLIT = 2


def _sum_pairs(parts, theirs):
    nchip, _, rows, cols = parts.shape
    tile = rows // SUM_SPLIT

    def body(c_ref, a_ref, b_ref, o_ref):
        o_ref[...] = (a_ref[0].astype(F32) + b_ref[...].astype(F32)).astype(BF16)

    spec = pl.BlockSpec((1, tile, cols), lambda q, t, c_ref: (q, t, 0))
    grid_spec = pltpu.PrefetchScalarGridSpec(
        num_scalar_prefetch=1, grid=(nchip, SUM_SPLIT),
        in_specs=[pl.BlockSpec((1, 1, tile, cols), lambda q, t, c_ref: (q, c_ref[0], t, 0)), spec], out_specs=spec)
    return pl.pallas_call(body, name="grad_sum_sibling", grid_spec=grid_spec,
                          out_shape=jax.ShapeDtypeStruct((nchip, rows, cols), BF16),
                          compiler_params=_params("parallel", "parallel"))(lax.axis_index("c").reshape(1), parts, theirs)


def _sum_final(chip_sum, got):
    _, rows, cols = chip_sum.shape
    tile = rows // SUM_SPLIT

    def body(q_ref, a_ref, g_ref, o_ref):
        o_ref[...] = ((a_ref[0].astype(F32) + g_ref[0].astype(F32)) + g_ref[1].astype(F32)) + g_ref[2].astype(F32)

    grid_spec = pltpu.PrefetchScalarGridSpec(
        num_scalar_prefetch=1, grid=(SUM_SPLIT,),
        in_specs=[pl.BlockSpec((1, tile, cols), lambda t, q_ref: (q_ref[0], t, 0)),
                  pl.BlockSpec((3, tile, cols), lambda t, q_ref: (0, t, 0))],
        out_specs=pl.BlockSpec((tile, cols), lambda t, q_ref: (t, 0)))
    my_chip = (2 * lax.axis_index("x") + lax.axis_index("y")).reshape(1)
    return pl.pallas_call(body, name="grad_sum_chips", grid_spec=grid_spec, out_shape=jax.ShapeDtypeStruct((rows, cols), F32),
                          compiler_params=_params("parallel"))(my_chip, chip_sum, got)


class _ReduceScatter:
    def __init__(self, name, slabs):
        self.name, self.rows = name, slabs.shape[1]
        parts = slabs.reshape(4, 2, self.rows, D_MODEL)
        self.first = _copies_start(name + "_sibling_start", [parts, lax.empty((4, self.rows, D_MODEL), slabs.dtype)],
                                   _to_sibling_copies, 4)
        self.token = self.first[3]

    def between_chips(self, after):
        parts, theirs = _copies_wait(self.name + "_sibling_wait", self.first, _to_sibling_copies, after)
        chip_sum = _sum_pairs(parts, theirs)
        self.second = _copies_start(self.name + "_chips_start", [chip_sum, lax.empty((3, self.rows, D_MODEL), chip_sum.dtype)],
                                    _to_chips_copies, 3)
        return self.second[3]

    def finish(self, after):
        chip_sum, got = _copies_wait(self.name + "_chips_wait", self.second, _to_chips_copies, after)
        return _sum_final(chip_sum, got)


def _ada_fwd(c_all, w, b):
    nb, _ = c_all.shape
    ncol = w.shape[1]

    def body(c_ref, w_ref, b_ref, o_ref):
        c = c_ref[...]
        act = (c * _sigmoid(c)).astype(BF16)
        o_ref[...] = jnp.dot(act, w_ref[...].astype(BF16), preferred_element_type=F32) + b_ref[...]

    return pl.pallas_call(body, name="ada_fwd", out_shape=jax.ShapeDtypeStruct((nb, ncol), F32),
                          compiler_params=pltpu.CompilerParams(vmem_limit_bytes=VMEM_LIMIT_BYTES))(c_all, w, b)


def _ada_wgrad(c_all_t, dmod_cols):
    d, nb = c_all_t.shape
    ncol = dmod_cols.shape[1]

    def body(ct_ref, dm_ref, o_ref):
        ct = ct_ref[...]
        act = (ct * _sigmoid(ct)).astype(BF16).astype(F32)
        dm = dm_ref[...].astype(BF16).astype(F32)
        acc = act[:, 0:1] * dm[0:1, :]
        for i in range(1, nb):
            acc = acc + act[:, i:i + 1] * dm[i:i + 1, :]
        o_ref[...] = acc

    return pl.pallas_call(body, name="ada_wgrad", out_shape=jax.ShapeDtypeStruct((d, ncol), F32),
                          compiler_params=pltpu.CompilerParams(vmem_limit_bytes=VMEM_LIMIT_BYTES))(c_all_t, dmod_cols)


SMALL_ROWS = 24


def _reduce_small(gathered):
    def body(g_ref, o_ref):
        acc = g_ref[0]
        for dev in range(1, N_DEV):
            acc = acc + g_ref[dev]
        o_ref[...] = acc

    return pl.pallas_call(body, name="reduce_small", out_shape=jax.ShapeDtypeStruct(gathered.shape[1:], F32))(gathered)


def _adamw_math(w, g, m, v):
    nm = ADAM_B1 * m + (1.0 - ADAM_B1) * g
    nv = ADAM_B2 * v + (1.0 - ADAM_B2) * (g * g)
    bc1 = 1.0 - ADAM_B1 ** ADAM_STEP
    bc2 = 1.0 - ADAM_B2 ** ADAM_STEP
    return -ADAM_LR * ((nm / bc1) / (jnp.sqrt(nv / bc2) + ADAM_EPS) + ADAM_WD * w), nm, nv


def _adamw_small(ws, gs, ms, vs, name):
    n = len(ws)

    def body(*refs):
        for i in range(n):
            res = _adamw_math(*(refs[k * n + i][...] for k in range(4)))
            for k in range(3):
                refs[(4 + k) * n + i][...] = res[k]

    shapes = [jax.ShapeDtypeStruct(w.shape, F32) for w in ws]
    res = pl.pallas_call(body, name=name, out_shape=shapes * 3)(*ws, *gs, *ms, *vs)
    return [(res[i], res[n + i], res[2 * n + i]) for i in range(n)]


def _adamw(w, g, m, v, name):
    rows, cols = w.shape
    tile = rows
    for cand in range(min(rows // 2, 512) // 8 * 8, 7, -8):
        if rows % cand == 0:
            tile = cand
            break
    spec = pl.BlockSpec((tile, cols), lambda t: (t, 0))

    def body(w_ref, g_ref, m_ref, v_ref, d_ref, nm_ref, nv_ref):
        d_ref[...], nm_ref[...], nv_ref[...] = _adamw_math(w_ref[...], g_ref[...], m_ref[...], v_ref[...])

    shp = jax.ShapeDtypeStruct((rows, cols), F32)
    return pl.pallas_call(body, name=name, grid=(rows // tile,), in_specs=[spec] * 4, out_specs=[spec] * 3, out_shape=[shp] * 3,
                          compiler_params=_params("parallel"))(w, g, m, v)


_WEIGHTS = ("w_ada", "b_ada", "w_in", "sinks", "w_branch_a", "w_branch_b", "w_o", "ln1_g", "ln1_b", "w_gate_up", "w_down",
            "ln2_g", "ln2_b")
_TRANSPOSED = ("w_in", "w_branch_b", "w_gate_up")


def _pack_shard(name, w):
    w = w.astype(BF16)
    if name in _TRANSPOSED:
        w = w.T
    return w.reshape(-1, D_MODEL)


def _unpack_full(name, slab):
    if name == "w_branch_b":
        return slab.reshape(N_DEV * 128, 512)
    return slab.reshape(-1, D_MODEL)


def _unpack_group(group, gathered):
    return {n: _unpack_full(n, slab) for (n, _), slab in zip(group, gathered)}


def _unpack_grads(group, g_packed):
    g_w, off = {}, 0
    for n, r in group:
        part = g_packed[off:off + r]
        off += r
        g_w[n] = part.reshape(128, 512) if n == "w_branch_b" else part
    return g_w


def kernel(x, c, positions, w_ada, b_ada, w_in, sinks, w_branch_a, w_branch_b, w_o, ln1_g, ln1_b, w_gate_up, w_down, ln2_g, ln2_b, loss_target, m_w_ada, m_b_ada, m_w_in, m_sinks, m_w_branch_a, m_w_branch_b, m_w_o, m_ln1_g, m_ln1_b, m_w_gate_up, m_w_down, m_ln2_g, m_ln2_b, v_w_ada, v_b_ada, v_w_in, v_sinks, v_w_branch_a, v_w_branch_b, v_w_o, v_ln1_g, v_ln1_b, v_w_gate_up, v_w_down, v_ln2_g, v_ln2_b):
    weights = dict(w_ada=w_ada, b_ada=b_ada, w_in=w_in, sinks=sinks, w_branch_a=w_branch_a, w_branch_b=w_branch_b, w_o=w_o,
                   ln1_g=ln1_g, ln1_b=ln1_b, w_gate_up=w_gate_up, w_down=w_down, ln2_g=ln2_g, ln2_b=ln2_b)
    m_in = dict(w_ada=m_w_ada, b_ada=m_b_ada, w_in=m_w_in, sinks=m_sinks, w_branch_a=m_w_branch_a, w_branch_b=m_w_branch_b,
                w_o=m_w_o, ln1_g=m_ln1_g, ln1_b=m_ln1_b, w_gate_up=m_w_gate_up, w_down=m_w_down, ln2_g=m_ln2_g, ln2_b=m_ln2_b)
    v_in = dict(w_ada=v_w_ada, b_ada=v_b_ada, w_in=v_w_in, sinks=v_sinks, w_branch_a=v_w_branch_a, w_branch_b=v_w_branch_b,
                w_o=v_w_o, ln1_g=v_ln1_g, ln1_b=v_ln1_b, w_gate_up=v_w_gate_up, w_down=v_w_down, ln2_g=v_ln2_g, ln2_b=v_ln2_b)
    bsz = x.shape[0]
    me = _index(_my_place())
    ada_cols = w_ada.shape[2]
    outs = {}

    def adamw(n, g):
        w2, m2, v2 = (t[n][0] if t[n].ndim == 3 else t[n] for t in (weights, m_in, v_in))
        shape = weights[n].shape
        if n in _TRANSPOSED:
            dlt, nm, nv = _adamw(w2.T, g, m2.T, v2.T, "adamw_" + n)
            outs[n] = tuple(t.T.reshape(shape) for t in (g, dlt, nm, nv))
        else:
            dlt, nm, nv = _adamw(w2, g, m2, v2, "adamw_" + n)
            outs[n] = tuple(t.reshape(shape) for t in (g, dlt, nm, nv))
        return nv

    packed_in = [_pack_shard(n, weights[n][0]) for n, _ in GROUP_IN]
    packed_rest = [_pack_shard(n, weights[n][0]) for n, _ in GROUP_REST]
    c_all = _gather_small(jnp.pad(c, ((0, 8 - bsz), (0, 0))), "gather_c")[:, :bsz].reshape(N_DEV * bsz, D_MODEL)
    gather_in = _Gather("gather_w_in", lax.optimization_barrier((packed_in, c_all))[0])
    b_cols = lax.dynamic_slice_in_dim(b_ada, me * ada_cols, ada_cols, axis=1)
    mod_cols = _ada_fwd(c_all, w_ada[0], b_cols + gather_in.token[0, 0])
    tables = _rope_tables(positions)
    mod_cols, tables, packed_rest = lax.optimization_barrier((mod_cols, tables, packed_rest))
    mod_all = _gather_small(mod_cols, "gather_mod").transpose(1, 0, 2).reshape(N_DEV * bsz, 6, D_MODEL)
    gather_rest = _Gather("gather_rest", lax.optimization_barrier((packed_rest, mod_all))[0])
    mod = jnp.pad(lax.dynamic_slice_in_dim(mod_all, me * bsz, bsz, axis=0), ((0, 0), (0, 2), (0, 0)))
    mod = mod + gather_rest.token[0, 0]
    mod = mod + gather_in.pass_on(mod)[0, 0]

    scatters = {}

    def get_w_in(after):
        return _unpack_group(GROUP_IN, gather_in.finish(after))["w_in"]

    def get_rest(after):
        return _unpack_group(GROUP_REST, gather_rest.finish(after))

    def pack_grads(group, grads):
        return jnp.concatenate([grads[n].reshape(N_DEV, r, D_MODEL) for n, r in group], axis=1)

    def hook(point, value):
        if point == "projected":
            return gather_rest.pass_on(value)
        if point == "grads_rest":
            scatters["rest"] = _ReduceScatter("scatter_rest", pack_grads(GROUP_REST, value))
            return scatters["rest"].token
        if point == "delta_done":
            return scatters["rest"].between_chips(value)
        if point == "grads_w_in":
            scatters["in"] = _ReduceScatter("scatter_w_in", pack_grads(GROUP_IN, value))
            return scatters["in"].token
        if point == "dgrad_done":
            return None
        raise ValueError(point)

    loss_part, grad_x, dmod, small = _layer_step(x, mod, tables, sinks[0], ln1_g, ln1_b, ln2_g, ln2_b, loss_target,
                                                 get_w_in, get_rest, hook)

    rows = jnp.concatenate([dmod.reshape(bsz * 6, D_MODEL), small, jnp.full((1, D_MODEL), loss_part, F32),
                            jnp.zeros((SMALL_ROWS - bsz * 6 - 6, D_MODEL), F32)], axis=0)
    small_all = _gather_small(rows, "gather_small")
    small_all = small_all + scatters["in"].between_chips(small_all)[0, 0]
    sums = _reduce_small(small_all)
    loss = sums[bsz * 6 + 5, 0]
    dmod_all = small_all[:, :bsz * 6].reshape(N_DEV * bsz, 6 * D_MODEL)
    small_g = {"b_ada": functools.reduce(jnp.add, [sums[6 * i:6 * i + 6] for i in range(bsz)]).reshape(1, 6 * D_MODEL),
               "sinks": sums[bsz * 6 + 4][::HEAD_DIM][None]}
    small_g.update({n: sums[bsz * 6 + i][None] for i, n in enumerate(("ln1_g", "ln1_b", "ln2_g", "ln2_b"))})
    names = list(small_g)
    for n, (dlt, nm, nv) in zip(names, _adamw_small([weights[n] for n in names], [small_g[n] for n in names],
                                                     [m_in[n] for n in names], [v_in[n] for n in names], "adamw_small")):
        outs[n] = (small_g[n], dlt, nm, nv)
    dmod_cols = lax.dynamic_slice_in_dim(dmod_all, me * ada_cols, ada_cols, axis=1)
    last = adamw("w_ada", _ada_wgrad(c_all.T, dmod_cols))
    for n, g in _unpack_grads(GROUP_REST, scatters["rest"].finish(last)).items():
        adamw(n, g)
    done = lax.optimization_barrier(tuple(outs[n][3] for n in outs))
    for n, g in _unpack_grads(GROUP_IN, scatters["in"].finish(done[0])).items():
        adamw(n, g)

    return (loss, grad_x, *[outs[n][0] for n in _WEIGHTS], *[outs[n][1] for n in _WEIGHTS], *[outs[n][2] for n in _WEIGHTS],
            *[outs[n][3] for n in _WEIGHTS])
```

```python
import functools

import jax
import jax.numpy as jnp
from jax import lax
from jax.experimental import pallas as pl
from jax.experimental.pallas import tpu as pltpu

F32 = jnp.float32
BF16 = jnp.bfloat16

D_MODEL = 1024
HEAD_DIM = 64
A_Q_HEADS = 16
A_WINDOW = 128
B_PATTERNS = ((128, 1), (512, 4), (2048, 16))
B_HEADS_PER_GROUP = 8
D_FF = 2816
QBLOCK = 128
ROPE_THETA = 10000.0
LN_EPS = 1e-5
DEEPNORM_ALPHA = 2.0 ** 0.25
NEG_INF = -1e30
ADAM_LR, ADAM_B1, ADAM_B2, ADAM_EPS, ADAM_WD, ADAM_STEP = 0.001, 0.9, 0.999, 1e-08, 0.01, 10

N_DEV = 8
LANES = 128
VMEM_LIMIT_BYTES = 56 * 1024 * 1024
MESH = pl.DeviceIdType.MESH

OFF_QA, OFF_KVA, OFF_QKVB, OFF_GAB = 0, 1024, 1280, 5888
GROUP_IN = (("w_in", 992),)
GROUP_REST = (("w_branch_a", 128), ("w_branch_b", 64), ("w_o", 128), ("w_gate_up", 704), ("w_down", 352))


def _params(*sem):
    return pltpu.CompilerParams(dimension_semantics=sem, vmem_limit_bytes=VMEM_LIMIT_BYTES)


def _sigmoid(x):
    return 1.0 / (1.0 + jnp.exp(-x))


_DIMS = {"nn": (((1,), (0,)), ((), ())), "nt": (((1,), (1,)), ((), ())), "tn": (((0,), (0,)), ((), ()))}


def _matmul(a, b, *, mode, tm, tn, tk, name, out_dtype=None, n=None, b_off=0, token=None, ins=(), outs=None, epilogue=None):
    if mode == "nn":
        (m, k), nn_ = a.shape, b.shape[1]
    elif mode == "nt":
        (m, k), nn_ = a.shape, (b.shape[0] if n is None else n)
    else:
        (k, m), nn_ = a.shape, b.shape[1]
    assert m % tm == 0 and nn_ % tn == 0 and k % tk == 0 and b_off % tn == 0, (name, m, nn_, k)
    nk = k // tk
    joff = b_off // tn
    if mode == "nn":
        a_spec = pl.BlockSpec((tm, tk), lambda i, j, kk: (i, kk))
        b_spec = pl.BlockSpec((tk, tn), lambda i, j, kk: (kk, j))
    elif mode == "nt":
        a_spec = pl.BlockSpec((tm, tk), lambda i, j, kk: (i, kk))
        b_spec = pl.BlockSpec((tn, tk), lambda i, j, kk: (j + joff, kk))
    else:
        a_spec = pl.BlockSpec((tk, tm), lambda i, j, kk: (kk, i))
        b_spec = pl.BlockSpec((tk, tn), lambda i, j, kk: (kk, j))
    dims = _DIMS[mode]
    has_token = token is not None
    plain = epilogue is None
    if plain:
        outs = [(jax.ShapeDtypeStruct((m, nn_), out_dtype), (tm, tn), lambda i, j: (i, j))]

        def epilogue(acc, i, j, in_refs, out_refs):
            out_refs[0][...] = acc.astype(out_refs[0].dtype)

    nin = len(ins)

    def body(*refs):
        a_ref, b_ref = refs[:2]
        in_refs = refs[2:2 + nin]
        out_refs = refs[2 + nin + has_token:-1]
        acc_ref = refs[-1]
        kk = pl.program_id(2)
        part = lax.dot_general(a_ref[...].astype(BF16), b_ref[...].astype(BF16), dims, preferred_element_type=F32)

        def finish(acc):
            epilogue(acc, pl.program_id(0), pl.program_id(1), in_refs, out_refs)

        if nk == 1:
            finish(part)
        else:
            @pl.when(kk == 0)
            def _():
                acc_ref[...] = part

            @pl.when(kk > 0)
            def _():
                acc_ref[...] += part

            @pl.when(kk == nk - 1)
            def _():
                finish(acc_ref[...])

    def spec(block, index):
        return pl.BlockSpec(block, lambda i, j, kk: index(i, j))

    in_specs, args = [a_spec, b_spec], [a, b]
    for arr, block, index in ins:
        in_specs.append(spec(block, index))
        args.append(arr)
    if has_token:
        in_specs.append(pl.BlockSpec(token.shape, lambda i, j, kk: (0, 0)))
        args.append(token)
    res = pl.pallas_call(
        body,
        name=name,
        grid=(m // tm, nn_ // tn, nk),
        in_specs=in_specs,
        out_specs=[spec(block, index) for _, block, index in outs],
        out_shape=[shape for shape, _, _ in outs],
        scratch_shapes=[pltpu.VMEM((tm, tn) if nk > 1 else (8, LANES), F32)],
        compiler_params=_params("arbitrary", "arbitrary", "arbitrary"),
    )(*args)
    return res[0] if plain else res


def _proj_rope(a, bt, cos, sin, *, n, b_off, rope_cols, tm, tn, name, out_dtype=F32):
    m, k = a.shape
    assert m % tm == 0 and n % tn == 0 and b_off % tn == 0 and rope_cols % tn == 0, name
    joff = b_off // tn
    nrope = rope_cols // tn

    def body(a_ref, b_ref, c_ref, s_ref, o_ref):
        acc = lax.dot_general(a_ref[...], b_ref[...], _DIMS["nt"], preferred_element_type=F32)
        j = pl.program_id(1)

        @pl.when(j < nrope)
        def _():
            o_ref[...] = _rope(acc, c_ref[...], s_ref[...], coarse=True).astype(o_ref.dtype)

        @pl.when(j >= nrope)
        def _():
            o_ref[...] = acc.astype(o_ref.dtype)

    table = pl.BlockSpec((tm, LANES), lambda i, j: (i, 0))
    return pl.pallas_call(
        body,
        name=name,
        grid=(m // tm, n // tn),
        in_specs=[pl.BlockSpec((tm, k), lambda i, j: (i, 0)), pl.BlockSpec((tn, k), lambda i, j: (j + joff, 0)), table, table],
        out_specs=pl.BlockSpec((tm, tn), lambda i, j: (i, j)),
        out_shape=jax.ShapeDtypeStruct((m, n), out_dtype),
        compiler_params=_params("parallel", "parallel"),
    )(a, bt, cos, sin)


ROW_TILE = 256


def _rows(width, col=0):
    return pl.BlockSpec((1, ROW_TILE, width), lambda b, t: (b, t, col))


def _per_batch(nrows, width):
    return pl.BlockSpec((1, nrows, width), lambda b, t: (b, 0, 0))


def _row_call(body, name, bsz, seq, in_specs, out_specs, out_shape, accumulates=False):
    return pl.pallas_call(
        body,
        name=name,
        grid=(bsz, seq // ROW_TILE),
        in_specs=in_specs,
        out_specs=out_specs,
        out_shape=out_shape,
        compiler_params=_params("parallel", "arbitrary" if accumulates else "parallel"),
    )


def _acc_rows(acc_ref, first, rows):
    @pl.when(first)
    def _():
        acc_ref[...] = jnp.zeros_like(acc_ref)

    for r, val in enumerate(rows):
        acc_ref[0, r:r + 1, :] += val


def _colsum(v):
    return jnp.sum(v, axis=0, keepdims=True)


def _ln_stats(z):
    mu = jnp.mean(z, axis=-1, keepdims=True)
    zc = z - mu
    var = jnp.mean(zc * zc, axis=-1, keepdims=True)
    rstd = lax.rsqrt(var + LN_EPS)
    return zc * rstd, rstd


def _ln_bwd(dxhat, xhat, rstd):
    m1 = jnp.mean(dxhat, axis=-1, keepdims=True)
    m2 = jnp.mean(dxhat * xhat, axis=-1, keepdims=True)
    return rstd * (dxhat - m1 - xhat * m2)


def _modulate_in(x, mod):
    bsz, seq, d = x.shape

    def body(x_ref, mod_ref, u_ref):
        u_ref[0] = (x_ref[0] * (1.0 + mod_ref[0, 1:2, :]) + mod_ref[0, 0:1, :]).astype(BF16)

    return _row_call(body, "modulate_in", bsz, seq, [_rows(d), _per_batch(8, d)], _rows(d),
                     jax.ShapeDtypeStruct((bsz, seq, d), BF16))(x, mod)


EP_TILE = 512


def _ep_specs(seq, d):
    tiles = seq // EP_TILE
    return ((EP_TILE, d), lambda i, j: (i, 0)), ((1, 8, d), lambda i, j: (i // tiles, 0, 0)), ((1, d), lambda i, j: (0, 0))


def _wo_ln1(merged, wo, x, mod, g, b, seq):
    ntok, d = x.shape
    row, per_b, whole = _ep_specs(seq, d)

    def epilogue(y, i, j, ins, outs):
        x_ref, mod_ref, g_ref, b_ref = ins
        y_ref, h_ref, u_ref = outs
        z = DEEPNORM_ALPHA * x_ref[...] + (1.0 + mod_ref[0, 2:3, :]) * y
        xhat, _ = _ln_stats(z)
        h = xhat * g_ref[...] + b_ref[...]
        y_ref[...] = y
        h_ref[...] = h
        u_ref[...] = (h * (1.0 + mod_ref[0, 4:5, :]) + mod_ref[0, 3:4, :]).astype(BF16)

    f32, bf16 = jax.ShapeDtypeStruct((ntok, d), F32), jax.ShapeDtypeStruct((ntok, d), BF16)
    return _matmul(merged, wo, mode="nn", tm=EP_TILE, tn=d, tk=d, name="w_o_ln1",
                   ins=[(x,) + row, (mod,) + per_b, (g,) + whole, (b,) + whole],
                   outs=[(f32,) + row, (f32,) + row, (bf16,) + row], epilogue=epilogue)


FF_HALF = D_FF // 2


def _interleave_gate_up(w):
    return w.reshape(2, 2, FF_HALF, w.shape[1]).transpose(1, 0, 2, 3).reshape(w.shape)


def _gate_up_silu(u2, wgut_i):
    ntok = u2.shape[0]

    def epilogue(h, i, j, ins, outs):
        h_ref, a_ref = outs
        hg, hu = h[:, :FF_HALF], h[:, FF_HALF:]
        h_ref[...] = h.astype(BF16)
        a_ref[...] = (hg * _sigmoid(hg) * hu).astype(BF16)

    return _matmul(u2, wgut_i, mode="nt", tm=EP_TILE, tn=2 * FF_HALF, tk=u2.shape[1], name="gate_up_silu",
                   outs=[(jax.ShapeDtypeStruct((ntok, 2 * D_FF), BF16), (EP_TILE, 2 * FF_HALF), lambda i, j: (i, j)),
                         (jax.ShapeDtypeStruct((ntok, D_FF), BF16), (EP_TILE, FF_HALF), lambda i, j: (i, j))],
                   epilogue=epilogue)


def _down_dgrad_silu_bwd(dy2, wd, h_i):
    ntok = dy2.shape[0]
    wide = ((EP_TILE, 2 * FF_HALF), lambda i, j: (i, j))

    def epilogue(da, i, j, ins, outs):
        h = ins[0][...].astype(F32)
        hg, hu = h[:, :FF_HALF], h[:, FF_HALF:]
        sg = _sigmoid(hg)
        outs[0][:, :FF_HALF] = (da * hu * (sg * (1.0 + hg * (1.0 - sg)))).astype(BF16)
        outs[0][:, FF_HALF:] = (da * (hg * sg)).astype(BF16)

    return _matmul(dy2, wd, mode="nt", tm=EP_TILE, tn=FF_HALF, tk=dy2.shape[1], name="down_dgrad_silu_bwd",
                   ins=[(h_i,) + wide], outs=[(jax.ShapeDtypeStruct((ntok, 2 * D_FF), BF16),) + wide], epilogue=epilogue)[0]


def _down_ln2_loss_bwd(a, wd, h1, mod, g, b, target, seq):
    ntok, d = h1.shape
    row, per_b, whole = _ep_specs(seq, d)
    tiles = seq // EP_TILE

    def epilogue(y, i, j, ins, outs):
        h_ref, mod_ref, g_ref, b_ref, t_ref = ins
        dy_ref, dh_ref, acc_ref = outs
        gate = 1.0 + mod_ref[0, 5:6, :]
        z = DEEPNORM_ALPHA * h_ref[...] + gate * y
        xhat, rstd = _ln_stats(z)
        diff = xhat * g_ref[...] + b_ref[...] - t_ref[...]
        loss = 0.5 * jnp.sum(jnp.sum(diff * diff, axis=-1, keepdims=True) / d, axis=0, keepdims=True)
        dout = diff / d
        dz = _ln_bwd(dout * g_ref[...], xhat, rstd)
        dy_ref[...] = (gate * dz).astype(BF16)
        dh_ref[...] = DEEPNORM_ALPHA * dz
        _acc_rows(acc_ref, i % tiles == 0,
                  [_colsum(dout * xhat), _colsum(dout), _colsum(dz * y), jnp.broadcast_to(loss, (1, d))])

    return _matmul(a, wd, mode="nn", tm=EP_TILE, tn=d, tk=a.shape[1], name="down_ln2_loss_bwd",
                   ins=[(h1,) + row, (mod,) + per_b, (g,) + whole, (b,) + whole, (target,) + row],
                   outs=[(jax.ShapeDtypeStruct((ntok, d), BF16),) + row, (jax.ShapeDtypeStruct((ntok, d), F32),) + row,
                         (jax.ShapeDtypeStruct((ntok // seq, 8, d), F32),) + per_b], epilogue=epilogue)


def _gate_up_dgrad_ln1_bwd(dh, wgut, dh1a, x, y1, mod, g, b, seq):
    ntok, d = x.shape
    row, per_b, whole = _ep_specs(seq, d)
    tiles = seq // EP_TILE

    def epilogue(du, i, j, ins, outs):
        dh_ref, x_ref, y_ref, mod_ref, g_ref, b_ref = ins
        dy_ref, dx_ref, acc_ref = outs
        y = y_ref[...]
        gate = 1.0 + mod_ref[0, 2:3, :]
        z = DEEPNORM_ALPHA * x_ref[...] + gate * y
        xhat, rstd = _ln_stats(z)
        h1 = xhat * g_ref[...] + b_ref[...]
        dh1 = dh_ref[...] + du * (1.0 + mod_ref[0, 4:5, :])
        dz = _ln_bwd(dh1 * g_ref[...], xhat, rstd)
        dy_ref[...] = (gate * dz).astype(BF16)
        dx_ref[...] = DEEPNORM_ALPHA * dz
        _acc_rows(acc_ref, i % tiles == 0,
                  [_colsum(dh1 * xhat), _colsum(dh1), _colsum(dz * y), _colsum(du * h1), _colsum(du)])

    return _matmul(dh, wgut, mode="nn", tm=EP_TILE, tn=d, tk=D_FF, name="gate_up_dgrad_ln1_bwd",
                   ins=[(dh1a,) + row, (x,) + row, (y1,) + row, (mod,) + per_b, (g,) + whole, (b,) + whole],
                   outs=[(jax.ShapeDtypeStruct((ntok, d), BF16),) + row, (jax.ShapeDtypeStruct((ntok, d), F32),) + row,
                         (jax.ShapeDtypeStruct((ntok // seq, 8, d), F32),) + per_b], epilogue=epilogue)


def _wo_dgrad_gate_bwd(dy1, wo, gab, ya, yb):
    ntok, d = ya.shape
    tm, tn = 1024, 512
    tile = ((tm, tn), lambda i, j: (i, j))
    tile_b = ((tm, tn), lambda i, j: (i, j + d // tn))

    def epilogue(dm_, i, j, ins, outs):
        ga_ref, gb_ref, ya_ref, yb_ref = ins
        dya_ref, dyb_ref, dga_ref, dgb_ref = outs
        sa, sb = _sigmoid(ga_ref[...].astype(F32)), _sigmoid(gb_ref[...].astype(F32))
        dya_ref[...] = (dm_ * sa).astype(BF16)
        dyb_ref[...] = (dm_ * sb).astype(BF16)
        dga_ref[...] = (dm_ * ya_ref[...].astype(F32) * sa * (1.0 - sa)).astype(BF16)
        dgb_ref[...] = (dm_ * yb_ref[...].astype(F32) * sb * (1.0 - sb)).astype(BF16)

    shp = jax.ShapeDtypeStruct((ntok, d), BF16)
    return _matmul(dy1, wo, mode="nt", tm=tm, tn=tn, tk=d, name="w_o_dgrad_gate_bwd",
                   ins=[(gab,) + tile, (gab,) + tile_b, (ya,) + tile, (yb,) + tile],
                   outs=[(shp,) + tile] * 4, epilogue=epilogue)


def _w_in_dgrad_grad_x(dproj, wint, dxa, x, mod, seq, token):
    ntok, d = x.shape
    row, per_b, _ = _ep_specs(seq, d)
    tiles = seq // EP_TILE

    def epilogue(du, i, j, ins, outs):
        dxa_ref, x_ref, mod_ref = ins
        gx_ref, acc_ref = outs
        gx_ref[...] = dxa_ref[...] + du * (1.0 + mod_ref[0, 1:2, :])
        _acc_rows(acc_ref, i % tiles == 0, [_colsum(du * x_ref[...]), _colsum(du)])

    return _matmul(dproj, wint, mode="nn", tm=EP_TILE, tn=d, tk=wint.shape[0] // 2, name="w_in_dgrad_grad_x", token=token,
                   ins=[(dxa,) + row, (x,) + row, (mod,) + per_b],
                   outs=[(jax.ShapeDtypeStruct((ntok, d), F32),) + row, (jax.ShapeDtypeStruct((ntok // seq, 8, d), F32),) + per_b],
                   epilogue=epilogue)


def _branch_b_gate_merge(ob, wbbt, gab, ya):
    ntok, d = ya.shape
    tm, tn = 1024, 512
    tile = ((tm, tn), lambda i, j: (i, j))
    tile_b = ((tm, tn), lambda i, j: (i, j + d // tn))

    def epilogue(yb, i, j, ins, outs):
        ga_ref, gb_ref, ya_ref = ins
        yb_ref, merged_ref = outs
        yb_ref[...] = yb.astype(BF16)
        merged_ref[...] = (_sigmoid(ga_ref[...].astype(F32)) * ya_ref[...].astype(F32)
                           + _sigmoid(gb_ref[...].astype(F32)) * yb).astype(BF16)

    shp = jax.ShapeDtypeStruct((ntok, d), BF16)
    return _matmul(ob, wbbt, mode="nt", tm=tm, tn=tn, tk=ob.shape[1], name="branch_b_gate_merge",
                   ins=[(gab,) + tile, (gab,) + tile_b, (ya,) + tile], outs=[(shp,) + tile] * 2, epilogue=epilogue)


def _segsum64(v):
    rows, width = v.shape
    ri = lax.broadcasted_iota(jnp.int32, (LANES, LANES), 0) // HEAD_DIM
    ci = lax.broadcasted_iota(jnp.int32, (LANES, LANES), 1) // HEAD_DIM
    ones = jnp.where(ri == ci, 1.0, 0.0).astype(BF16)
    out = []
    for c in range(width // LANES):
        part = v[:, c * LANES:(c + 1) * LANES]
        hi = part.astype(BF16)
        lo = (part - hi.astype(F32)).astype(BF16)
        out.append(jnp.dot(hi, ones, preferred_element_type=F32) + jnp.dot(lo, ones, preferred_element_type=F32))
    return jnp.concatenate(out, axis=1) if len(out) > 1 else out[0]


def _merge_b(os_, ls_):
    bsz, seq, w = os_[0].shape

    def body(o0, o1, o2, l0, l1, l2, ob_ref):
        ls = [l0[0], l1[0], l2[0]]
        mx = jnp.maximum(jnp.maximum(ls[0], ls[1]), ls[2])
        es = [jnp.exp(l - mx) for l in ls]
        den = es[0] + es[1] + es[2]
        ob_ref[0] = ((es[0] / den) * o0[0].astype(F32) + (es[1] / den) * o1[0].astype(F32)
                     + (es[2] / den) * o2[0].astype(F32)).astype(BF16)

    return _row_call(body, "merge_b", bsz, seq, [_rows(w)] * 6, _rows(w),
                     jax.ShapeDtypeStruct((bsz, seq, w), BF16))(*os_, *ls_)


def _branch_b_dgrad_merge_bwd(dyb, wbbt, os_, ls_):
    ntok, w = os_[0].shape
    row = ((EP_TILE, w), lambda i, j: (i, 0))

    def epilogue(dob_, i, j, ins, outs):
        os_r, ls_r = ins[:3], ins[3:]
        do_r, dd_r = outs[:3], outs[3:]
        ls = [l[...] for l in ls_r]
        mx = jnp.maximum(jnp.maximum(ls[0], ls[1]), ls[2])
        es = [jnp.exp(l - mx) for l in ls]
        den = es[0] + es[1] + es[2]
        ws = [e / den for e in es]
        dws = [_segsum64(dob_ * o[...].astype(F32)) for o in os_r]
        mean = ws[0] * dws[0] + ws[1] * dws[1] + ws[2] * dws[2]
        for wg, do_ref, dd_ref in zip(ws, do_r, dd_r):
            do_ref[...] = wg * dob_
            dd_ref[...] = -wg * mean

    shp = jax.ShapeDtypeStruct((ntok, w), F32)
    return _matmul(dyb, wbbt, mode="nn", tm=EP_TILE, tn=w, tk=dyb.shape[1], name="branch_b_dgrad_merge_bwd",
                   ins=[(v,) + row for v in list(os_) + list(ls_)], outs=[(shp,) + row] * 6, epilogue=epilogue)


def _branch_a_dgrad_delta(dya, wba, oa, lse_a, sinks_exp, seq):
    ntok, w = oa.shape
    row, per_b, whole = _ep_specs(seq, w)
    tiles = seq // EP_TILE

    def epilogue(do_, i, j, ins, outs):
        o_ref, l_ref, s_ref = ins
        do_ref, dd_ref, acc_ref = outs
        dd = -_segsum64(do_ * o_ref[...].astype(F32))
        do_ref[...] = do_
        dd_ref[...] = dd
        _acc_rows(acc_ref, i % tiles == 0, [_colsum(dd * jnp.exp(s_ref[...] - l_ref[...]))])

    shp = jax.ShapeDtypeStruct((ntok, w), F32)
    return _matmul(dya, wba, mode="nt", tm=EP_TILE, tn=w, tk=dya.shape[1], name="branch_a_dgrad_delta",
                   ins=[(oa,) + row, (lse_a,) + row, (sinks_exp,) + whole],
                   outs=[(shp,) + row, (shp,) + row, (jax.ShapeDtypeStruct((ntok // seq, 8, w), F32),) + per_b],
                   epilogue=epilogue)


def _swap_halves(v):
    src = lax.broadcasted_iota(jnp.int32, (LANES, LANES), 0)
    dst = lax.broadcasted_iota(jnp.int32, (LANES, LANES), 1)
    partner = jnp.where((dst % HEAD_DIM) < HEAD_DIM // 2, dst + HEAD_DIM // 2, dst - HEAD_DIM // 2)
    perm = jnp.where(src == partner, 1.0, 0.0).astype(BF16)
    hi = v.astype(BF16)
    lo = (v - hi.astype(F32)).astype(BF16)
    return jnp.dot(hi, perm, preferred_element_type=F32) + jnp.dot(lo, perm, preferred_element_type=F32)


def _swap_halves_roll(v):
    lane = lax.broadcasted_iota(jnp.int32, v.shape, 1)
    return jnp.where((lane % HEAD_DIM) < HEAD_DIM // 2, pltpu.roll(v, LANES - HEAD_DIM // 2, 1),
                     pltpu.roll(v, HEAD_DIM // 2, 1))


def _swap_halves_coarse(v):
    src = lax.broadcasted_iota(jnp.int32, (LANES, LANES), 0)
    dst = lax.broadcasted_iota(jnp.int32, (LANES, LANES), 1)
    partner = jnp.where((dst % HEAD_DIM) < HEAD_DIM // 2, dst + HEAD_DIM // 2, dst - HEAD_DIM // 2)
    perm = jnp.where(src == partner, 1.0, 0.0).astype(BF16)
    return jnp.dot(v.astype(BF16), perm, preferred_element_type=F32)


def _rope(v, cos, sin, sign=1.0, mxu=True, coarse=False):
    swap = (_swap_halves_coarse if coarse else _swap_halves) if mxu else _swap_halves_roll
    out = []
    for c in range(v.shape[1] // LANES):
        part = v[:, c * LANES:(c + 1) * LANES]
        out.append(part * cos + sign * (swap(part) * sin))
    return jnp.concatenate(out, axis=1) if len(out) > 1 else out[0]


def _half_mask(shape, half):
    lane = lax.broadcasted_iota(jnp.int32, shape, len(shape) - 1) % LANES
    return (lane < HEAD_DIM) if half == 0 else (lane >= HEAD_DIM)


def _dup_half(v, half):
    return jnp.where(_half_mask(v.shape, half), v, pltpu.roll(v, HEAD_DIM, 1))


def _fold_halves(v):
    return v + pltpu.roll(v, HEAD_DIM, 1)


def _pick_halves(lo_rows, hi_rows):
    return jnp.where(_half_mask(lo_rows.shape, 0), lo_rows, hi_rows)


def _stack_masked(v, pairs):
    parts = []
    for c in pairs:
        pair = v[:, c * LANES:(c + 1) * LANES]
        parts += [jnp.where(_half_mask(pair.shape, half), pair, 0.0) for half in (0, 1)]
    return jnp.concatenate(parts, axis=0)


def _stack_pair_cols(v, pairs):
    return jnp.concatenate([v[:, c * LANES + half * HEAD_DIM:c * LANES + half * HEAD_DIM + 1] for c in pairs for half in (0, 1)],
                           axis=0)


ATTN_UNITS = 16


def _class_rows(r):
    return [pl.ds(0, QBLOCK)] if r == 1 else [pl.ds(rho, QBLOCK, stride=r) for rho in range(r)]


def _band_mask(nrows, nk, blk, n_back, has_prev):
    qi = lax.broadcasted_iota(jnp.int32, (nrows, nk), 0) % QBLOCK
    ki = lax.broadcasted_iota(jnp.int32, (nrows, nk), 1)
    if has_prev:
        dist = qi + QBLOCK - ki
        return (dist >= 0) & (dist <= n_back) & ((ki >= QBLOCK) | (blk > 0))
    dist = qi - ki
    return (dist >= 0) & (dist <= n_back)


def _attn_fwd(q_arr, k_arr, v_arr, *, name, npair, gqa, q_col, k_col, v_col, nchunk, r, n_back, sinks=None):
    bsz, seq, _ = q_arr.shape
    rr = QBLOCK * r
    nblk = seq // rr
    qw = npair * LANES
    kw = LANES if gqa else qw
    has_prev = nblk > 1
    has_sink = sinks is not None
    scale = HEAD_DIM ** -0.5

    def body(*refs):
        refs = list(refs)
        q_ref, kc_ref, vc_ref = refs[:3]
        pos = 3
        if has_prev:
            kp_ref, vp_ref = refs[pos:pos + 2]
            pos += 2
        if has_sink:
            sink_ref = refs[pos]
            pos += 1
        o_ref, lse_ref = refs[pos:pos + 2]
        if r > 1:
            stage_o = refs[pos + 2]
        blk = pl.program_id(2)
        nk = (2 if has_prev else 1) * QBLOCK
        valid = _band_mask(QBLOCK, nk, blk, n_back, has_prev)
        per = npair // 2
        classes = _class_rows(r)
        step = max(1, ATTN_UNITS // (2 * npair))
        for first in range(0, len(classes), step):
            batch = classes[first:first + step]
            units = []
            for ci, rows in enumerate(batch):
                q = q_ref[0, rows, :] * scale
                k, v = kc_ref[0, rows, :], vc_ref[0, rows, :]
                if has_prev:
                    k = jnp.concatenate([kp_ref[0, rows, :], k], axis=0)
                    v = jnp.concatenate([vp_ref[0, rows, :], v], axis=0)
                if gqa:
                    kdup = [_dup_half(k, hk).astype(BF16) for hk in range(2)]
                    vdup = [_dup_half(v, hk) for hk in range(2)]
                for c in range(npair):
                    sl = slice(c * LANES, (c + 1) * LANES)
                    qc = q[:, sl]
                    kc, vc = (kdup[c // per], vdup[c // per]) if gqa else (k[:, sl].astype(BF16), v[:, sl])
                    for half in (0, 1):
                        qm = jnp.where(_half_mask(qc.shape, half), qc, 0.0).astype(BF16)
                        vm = jnp.where(_half_mask(vc.shape, half), vc, 0.0).astype(BF16)
                        s = lax.dot_general(qm, kc, _DIMS["nt"], preferred_element_type=F32)
                        units.append(dict(ci=ci, c=c, half=half, s=s, vm=vm, sk=sink_ref[2 * c + half] if has_sink else None))
            for u in units:
                s = jnp.where(valid, u["s"], NEG_INF)
                m = jnp.max(s, axis=1, keepdims=True)
                if has_sink:
                    m = jnp.maximum(m, u["sk"])
                p = jnp.exp(s - m)
                den = jnp.sum(p, axis=1, keepdims=True)
                if has_sink:
                    den = den + jnp.exp(u["sk"] - m)
                u.update(p=p.astype(BF16), den=den, lse=m + jnp.log(den))
            for u in units:
                u["o"] = jnp.dot(u["p"], u["vm"], preferred_element_type=F32) / u["den"]
            for ci, rows in enumerate(batch):
                outs, lses = [None] * npair, [None] * npair
                for u in units:
                    if u["ci"] != ci:
                        continue
                    c, o = u["c"], u["o"]
                    lse = jnp.broadcast_to(u["lse"], o.shape)
                    outs[c] = o if u["half"] == 0 else outs[c] + o
                    lses[c] = lse if u["half"] == 0 else _pick_halves(lses[c], lse)
                o_new = jnp.concatenate(outs, axis=1) if npair > 1 else outs[0]
                if r > 1:
                    stage_o[rows, :] = o_new
                else:
                    o_ref[0] = o_new.astype(BF16)
                lse_ref[0, rows, :] = jnp.concatenate(lses, axis=1) if npair > 1 else lses[0]
        if r > 1:
            o_ref[0] = stage_o[...].astype(BF16)

    def cur(width, col0):
        return pl.BlockSpec((1, rr, width), lambda b, c, i: (b, i, col0 + c))

    def prev(width, col0):
        return pl.BlockSpec((1, rr, width), lambda b, c, i: (b, jnp.maximum(i - 1, 0), col0 + c))

    in_specs = [cur(qw, q_col), cur(kw, k_col), cur(kw, v_col)]
    args = [q_arr, k_arr, v_arr]
    if has_prev:
        in_specs += [prev(kw, k_col), prev(kw, v_col)]
        args += [k_arr, v_arr]
    if has_sink:
        in_specs.append(pl.BlockSpec(memory_space=pltpu.SMEM))
        args.append(sinks)
    return pl.pallas_call(
        body,
        name=name,
        grid=(bsz, nchunk, nblk),
        in_specs=in_specs,
        out_specs=[pl.BlockSpec((1, rr, qw), lambda b, c, i: (b, i, c))] * 2,
        out_shape=[jax.ShapeDtypeStruct((bsz, seq, nchunk * qw), BF16), jax.ShapeDtypeStruct((bsz, seq, nchunk * qw), F32)],
        scratch_shapes=[pltpu.VMEM((rr, qw), F32)] if r > 1 else [],
        compiler_params=_params("parallel", "parallel", "parallel"),
    )(*args)


def _attn_bwd(q_arr, k_arr, v_arr, cos, sin, do, lse, dd, *, name, npair, gqa, q_col, k_col, v_col, nchunk, r, n_back,
              token=None):
    bsz, seq, _ = q_arr.shape
    rr = QBLOCK * r
    nblk = seq // rr
    qw = npair * LANES
    kw = LANES if gqa else qw
    has_next = nblk > 1
    has_token = token is not None
    staged = r > 1
    scale = HEAD_DIM ** -0.5

    def body(*refs):
        refs = list(refs)
        k_ref, v_ref, c_ref, s_ref = refs[:4]
        tile_refs = [refs[4:8]]
        pos = 8
        if has_next:
            tile_refs.append(refs[pos:pos + 4])
            pos += 4
        if has_token:
            pos += 1
        dq_ref, dk_ref, dv_ref = refs[pos:pos + 3]
        carry_ref, bcast_ref = refs[pos + 3:pos + 5]
        if staged:
            stage_q, stage_k, stage_v = refs[pos + 5:pos + 8]
        blk = pl.program_id(2)
        if has_next:
            @pl.when(blk == 0)
            def _():
                carry_ref[...] = jnp.zeros_like(carry_ref)

        nrows = (npair if gqa else 1) * QBLOCK
        qi = lax.broadcasted_iota(jnp.int32, (nrows, QBLOCK), 0) % QBLOCK
        ki = lax.broadcasted_iota(jnp.int32, (nrows, QBLOCK), 1)
        valids = [qi >= ki, (qi + QBLOCK - ki <= n_back) & (blk + 1 < nblk)]
        per = npair // 2
        ntile = len(tile_refs)
        cat = lambda parts: jnp.concatenate(parts, axis=1) if len(parts) > 1 else parts[0]
        classes = _class_rows(r)
        step = max(1, ATTN_UNITS // (ntile * (2 if gqa else 2 * npair)))
        def stat_cols(stat, slot):
            if gqa:
                return _stack_pair_cols(stat, list(range(slot * per, (slot + 1) * per)))
            col = slot * HEAD_DIM
            return stat[:, col:col + 1]

        nslot = 2 if gqa else 2 * npair
        if has_next:
            @pl.when(blk == 0)
            def _():
                for rows in classes:
                    for which, stat_ref in enumerate(tile_refs[0][2:4]):
                        stat = stat_ref[0, rows, :]
                        for slot in range(nslot):
                            bcast_ref[which, slot, rows if not gqa else slice(None), :] = jnp.broadcast_to(
                                stat_cols(stat, slot), (nrows, LANES))

        for first in range(0, len(classes), step):
            batch = classes[first:first + step]
            units = []
            for ci, rows in enumerate(batch):
                keep = slice(None) if gqa else rows
                tiles = [(q_ref[0, rows, :] * scale, do_ref[0, rows, :], l_ref[0, rows, :], d_ref[0, rows, :])
                         for q_ref, do_ref, l_ref, d_ref in tile_refs]

                def stats(t, slot, keep=keep, tiles=tiles):
                    if has_next and t == 0:
                        return bcast_ref[0, slot, keep, :], bcast_ref[1, slot, keep, :]
                    return tuple(jnp.broadcast_to(stat_cols(tiles[t][2 + w], slot), (nrows, LANES)) for w in range(2))

                k, v = k_ref[0, rows, :], v_ref[0, rows, :]
                if gqa:
                    for hk in range(2):
                        pairs = list(range(hk * per, (hk + 1) * per))
                        kd, vd = _dup_half(k, hk).astype(BF16), _dup_half(v, hk).astype(BF16)
                        for t, (q, do_, l_, d_) in enumerate(tiles):
                            lcol, dcol = stats(t, hk)
                            units.append(dict(ci=ci, t=t, hk=hk, slot=hk, keep=keep, pairs=pairs,
                                              qs=_stack_masked(q, pairs).astype(BF16),
                                              dos=_stack_masked(do_, pairs).astype(BF16), lcol=lcol, dcol=dcol,
                                              kmat=kd, vmat=vd, kdq=kd))
                else:
                    for c in range(npair):
                        sl = slice(c * LANES, (c + 1) * LANES)
                        kc, vcb = k[:, sl], v[:, sl].astype(BF16)
                        kcb = kc.astype(BF16)
                        for t, (q, do_, l_, d_) in enumerate(tiles):
                            for half in (0, 1):
                                hm = _half_mask(kc.shape, half)
                                lcol, dcol = stats(t, 2 * c + half)
                                units.append(dict(ci=ci, t=t, c=c, half=half, slot=2 * c + half, keep=keep,
                                                  qs=jnp.where(hm, q[:, sl], 0.0).astype(BF16),
                                                  dos=jnp.where(hm, do_[:, sl], 0.0).astype(BF16), lcol=lcol, dcol=dcol,
                                                  kmat=kcb, vmat=vcb, kdq=jnp.where(hm, kc, 0.0).astype(BF16)))
            for u in units:
                u["s"] = lax.dot_general(u["qs"], u["kmat"], _DIMS["nt"], preferred_element_type=F32)
                u["dp"] = lax.dot_general(u["dos"], u["vmat"], _DIMS["nt"], preferred_element_type=F32)
            for u in units:
                p = jnp.exp(jnp.where(valids[u["t"]], u["s"], NEG_INF) - u["lcol"])
                u["ds"] = (p * (u["dp"] + u["dcol"])).astype(BF16)
                u["p"] = p.astype(BF16)
            for u in units:
                u["dv"] = lax.dot_general(u["p"], u["dos"], _DIMS["tn"], preferred_element_type=F32)
                u["dk"] = lax.dot_general(u["ds"], u["qs"], _DIMS["tn"], preferred_element_type=F32)
                u["dq"] = jnp.dot(u["ds"], u["kdq"], preferred_element_type=F32) * scale
            for u in units:
                if u["t"] == 1:
                    bcast_ref[0, u["slot"], u["keep"], :] = u["lcol"]
                    bcast_ref[1, u["slot"], u["keep"], :] = u["dcol"]
            for ci, rows in enumerate(batch):
                mine = [u for u in units if u["ci"] == ci]
                dq = [[None] * npair for _ in range(ntile)]
                if gqa:
                    dk_out = dv_out = None
                    for hk in range(2):
                        us = [u for u in mine if u["hk"] == hk]
                        for u in us:
                            for i, c in enumerate(u["pairs"]):
                                dq[u["t"]][c] = _pick_halves(u["dq"][2 * i * QBLOCK:(2 * i + 1) * QBLOCK],
                                                             u["dq"][(2 * i + 1) * QBLOCK:(2 * i + 2) * QBLOCK])
                        dk_h = _fold_halves(functools.reduce(jnp.add, [u["dk"] for u in us]))
                        dv_h = _fold_halves(functools.reduce(jnp.add, [u["dv"] for u in us]))
                        dk_out = dk_h if hk == 0 else _pick_halves(dk_out, dk_h)
                        dv_out = dv_h if hk == 0 else _pick_halves(dv_out, dv_h)
                else:
                    dks, dvs = [], []
                    for c in range(npair):
                        us = [u for u in mine if u["c"] == c]
                        dks.append(functools.reduce(jnp.add, [u["dk"] for u in us]))
                        dvs.append(functools.reduce(jnp.add, [u["dv"] for u in us]))
                        for t in range(ntile):
                            dq[t][c] = functools.reduce(jnp.add, [u["dq"] for u in us if u["t"] == t])
                    dk_out, dv_out = cat(dks), cat(dvs)
                ck, sk_ = c_ref[0, rows, :], s_ref[0, rows, :]
                dk_new = _rope(dk_out, ck, sk_, sign=-1.0, mxu=gqa, coarse=True)
                dq_cur = cat(dq[0])
                if has_next:
                    dq_cur = dq_cur + carry_ref[rows, :]
                    carry_ref[rows, :] = cat(dq[1])
                dq_new = _rope(dq_cur, ck, sk_, sign=-1.0, mxu=gqa, coarse=True)
                if staged:
                    stage_q[rows, :], stage_k[rows, :], stage_v[rows, :] = dq_new, dk_new, dv_out
                else:
                    dq_ref[0], dk_ref[0], dv_ref[0] = dq_new.astype(BF16), dk_new.astype(BF16), dv_out.astype(BF16)
        if staged:
            dq_ref[0], dk_ref[0], dv_ref[0] = stage_q[...].astype(BF16), stage_k[...].astype(BF16), stage_v[...].astype(BF16)

    def at(width, col0, shift):
        return pl.BlockSpec((1, rr, width), lambda b, c, i: (b, jnp.minimum(i + shift, nblk - 1), col0 + c))

    in_specs = [at(kw, k_col, 0), at(kw, v_col, 0), pl.BlockSpec((1, rr, LANES), lambda b, c, i: (b, i, 0)),
                pl.BlockSpec((1, rr, LANES), lambda b, c, i: (b, i, 0))]
    args = [k_arr, v_arr, cos, sin]
    for shift in (0, 1) if has_next else (0,):
        in_specs += [at(qw, q_col, shift), at(qw, 0, shift), at(qw, 0, shift), at(qw, 0, shift)]
        args += [q_arr, do, lse, dd]
    if has_token:
        in_specs.append(pl.BlockSpec(token.shape, lambda b, c, i: (0, 0)))
        args.append(token)
    return pl.pallas_call(
        body,
        name=name,
        grid=(bsz, nchunk, nblk),
        in_specs=in_specs,
        out_specs=[pl.BlockSpec((1, rr, qw), lambda b, c, i: (b, i, c)),
                   pl.BlockSpec((1, rr, kw), lambda b, c, i: (b, i, c)),
                   pl.BlockSpec((1, rr, kw), lambda b, c, i: (b, i, c))],
        out_shape=[jax.ShapeDtypeStruct((bsz, seq, nchunk * qw), BF16),
                   jax.ShapeDtypeStruct((bsz, seq, nchunk * kw), BF16),
                   jax.ShapeDtypeStruct((bsz, seq, nchunk * kw), BF16)],
        scratch_shapes=[pltpu.VMEM((rr, qw) if has_next else (8, LANES), F32),
                        pltpu.VMEM((2, 2 if gqa else 2 * npair, npair * QBLOCK if gqa else rr, LANES) if has_next
                                   else (1, 1, 8, LANES), F32)] +
                       ([pltpu.VMEM((rr, qw), F32), pltpu.VMEM((rr, kw), F32), pltpu.VMEM((rr, kw), F32)] if staged else []),
        compiler_params=_params("parallel", "parallel", "arbitrary"),
    )(*args)


B_CHUNKS = {1: (4, 1), 4: (1, 4), 16: (1, 4)}


def _rope_tables(positions):
    half = HEAD_DIM // 2
    inv = ROPE_THETA ** (-jnp.arange(half, dtype=F32) / half)
    ang = positions.astype(F32)[..., None] * inv
    cos, sin = jnp.cos(ang), jnp.sin(ang)
    return jnp.concatenate([cos] * 4, axis=-1), jnp.concatenate([-sin, sin, -sin, sin], axis=-1)


def _layer_step(x, mod, tables, sinks, ln1_g, ln1_b, ln2_g, ln2_b, target, get_w_in, get_rest, hook):
    bsz, seq, d = x.shape
    ntok = bsz * seq
    flat = lambda v: v.reshape(ntok, v.shape[-1])
    unflat = lambda v: v.reshape(bsz, seq, v.shape[-1])
    cos, sin = tables
    mm = functools.partial(_matmul, tm=1024, tk=1024)
    scalar = lambda tok: 0.0 if tok is None else tok[0, 0]

    u1 = _modulate_in(x, mod)
    u1f = flat(u1)
    wint = get_w_in(u1)
    cosf, sinf = flat(cos), flat(sin)
    proj = functools.partial(_proj_rope, u1f, wint, cosf, sinf, tm=2048)
    qkvb = unflat(proj(n=4608, b_off=OFF_QKVB, rope_cols=3072, tn=256, name="proj_qkvb"))
    b_kws, os_, ls_ = [], [], []
    for g, (window, r) in enumerate(B_PATTERNS):
        npair, nch = B_CHUNKS[r]
        per = B_HEADS_PER_GROUP // (2 * npair)
        nsec = len(B_PATTERNS) * per
        kw_ = dict(npair=npair, gqa=False, q_col=g * per, k_col=nsec + g * per, v_col=2 * nsec + g * per, nchunk=nch, r=r,
                   n_back=window // r)
        b_kws.append(kw_)
        o_g, l_g = _attn_fwd(qkvb, qkvb, qkvb, name=f"attn_b{g}_fwd", **kw_)
        os_.append(o_g)
        ls_.append(l_g)
    ob = _merge_b(os_, ls_)
    tok = hook("projected", ob)
    proj = functools.partial(_proj_rope, u1f, wint, cosf + scalar(tok), sinf, tm=2048)
    gab = unflat(proj(n=2048, b_off=OFF_GAB, rope_cols=0, tn=256, name="proj_gab", out_dtype=BF16))
    qa = unflat(proj(n=1024, b_off=OFF_QA, rope_cols=1024, tn=512, name="proj_qa"))
    kva = unflat(proj(n=256, b_off=OFF_KVA, rope_cols=128, tn=128, name="proj_kva"))
    a_kw = dict(npair=A_Q_HEADS // 2, gqa=True, q_col=0, k_col=0, v_col=1, nchunk=1, r=1, n_back=A_WINDOW - 1)
    after_gab = jnp.minimum(jnp.abs(gab[0, 0, 0].astype(F32)), 0.0)
    oa, lse_a = _attn_fwd(qa, kva, kva, name="attn_a_fwd", sinks=sinks.reshape(A_Q_HEADS) + after_gab, **a_kw)
    rest = get_rest(oa)
    wba, wbbt, wo, wgut, wd = (rest[n] for n in ("w_branch_a", "w_branch_b", "w_o", "w_gate_up", "w_down"))
    ya = unflat(mm(flat(oa), wba, mode="nn", out_dtype=BF16, tn=512, name="branch_a"))
    ybf, mergedf = _branch_b_gate_merge(flat(ob), wbbt, flat(gab), flat(ya))
    xf = flat(x)
    y1f, h1f, u2f = _wo_ln1(mergedf, wo, xf, mod, ln1_g, ln1_b, seq)
    wgut_i = _interleave_gate_up(wgut)
    hf, af = _gate_up_silu(u2f, wgut_i)

    dy2f, dh1af, acc2 = _down_ln2_loss_bwd(af, wd, h1f, mod, ln2_g, ln2_b, flat(target), seq)
    g_wd = _matmul(af, dy2f, mode="tn", out_dtype=BF16, tm=256, tn=1024, tk=ntok, name="down_wgrad")
    dhf = _down_dgrad_silu_bwd(dy2f, wd, hf)
    g_wgut = _interleave_gate_up(_matmul(dhf, u2f, mode="tn", out_dtype=BF16, tm=256, tn=1024, tk=ntok, name="gate_up_wgrad"))
    dy1f, dxaf, acc1 = _gate_up_dgrad_ln1_bwd(dhf, wgut_i, dh1af, xf, y1f, mod, ln1_g, ln1_b, seq)
    g_wo = _matmul(mergedf, dy1f, mode="tn", out_dtype=BF16, tm=256, tn=1024, tk=ntok, name="w_o_wgrad")
    dyaf, dybf, dgaf, dgbf = _wo_dgrad_gate_bwd(dy1f, wo, flat(gab), flat(ya), ybf)
    g_wba = _matmul(flat(oa), dyaf, mode="tn", out_dtype=BF16, tm=256, tn=1024, tk=ntok, name="branch_a_wgrad")
    g_wbbt = _matmul(dybf, flat(ob), mode="tn", out_dtype=BF16, tm=256, tn=512, tk=ntok, name="branch_b_wgrad")
    tok = hook("grads_rest", dict(w_branch_a=g_wba, w_branch_b=g_wbbt, w_o=g_wo, w_gate_up=g_wgut, w_down=g_wd))

    sinks_exp = jnp.repeat(sinks.reshape(1, A_Q_HEADS), HEAD_DIM, axis=1) + scalar(tok)
    doa, dd_a, acc_s = _branch_a_dgrad_delta(dyaf, wba, flat(oa), flat(lse_a), sinks_exp, seq)
    doa, dd_a = unflat(doa), unflat(dd_a)
    tok = hook("delta_done", dd_a)
    dqa, dka, dva = _attn_bwd(qa, kva, kva, cos, sin, doa, lse_a, dd_a, name="attn_a_bwd", token=tok, **a_kw)
    merged_bwd = [unflat(t) for t in _branch_b_dgrad_merge_bwd(dybf, wbbt, [flat(t) for t in os_], [flat(t) for t in ls_])]
    dqs, dks, dvs = [], [], []
    for g in range(len(B_PATTERNS)):
        dq_g, dk_g, dv_g = _attn_bwd(qkvb, qkvb, qkvb, cos, sin, merged_bwd[g], ls_[g], merged_bwd[3 + g],
                                     name=f"attn_b{g}_bwd", **b_kws[g])
        dqs.append(dq_g)
        dks.append(dk_g)
        dvs.append(dv_g)
    dproj = jnp.concatenate([t.astype(BF16) for t in [dqa, dka, dva] + dqs + dks + dvs] + [unflat(dgaf), unflat(dgbf)], axis=-1)
    dprojf = flat(dproj)
    g_wint = _matmul(dprojf, u1f, mode="tn", out_dtype=BF16, tm=256, tn=1024, tk=ntok, name="w_in_wgrad")
    tok = hook("grads_w_in", dict(w_in=g_wint))
    grad_x, acc0 = _w_in_dgrad_grad_x(dprojf, wint, dxaf, xf, mod, seq, tok)
    grad_x = unflat(grad_x)
    tok = hook("dgrad_done", grad_x)

    loss_part = jnp.sum(acc2[:, 3, 0])
    dmod = jnp.stack([acc0[:, 1], acc0[:, 0], acc1[:, 2], acc1[:, 4], acc1[:, 3], acc2[:, 2]], axis=1)
    small = jnp.stack([acc1[:, 0].sum(0), acc1[:, 1].sum(0), acc2[:, 0].sum(0), acc2[:, 1].sum(0), acc_s[:, 0].sum(0)])
    small = small + scalar(tok)
    return loss_part, grad_x, dmod, small


CHIP_FLIPS = (2, 4, 6)


def _my_place():
    return lax.axis_index("x"), lax.axis_index("y"), lax.axis_index("c")


def _flip(place, k):
    px, py, pc = place
    return (1 - px if k & 4 else px, 1 - py if k & 2 else py, 1 - pc if k & 1 else pc)


def _index(place):
    return 4 * place[0] + 2 * place[1] + place[2]


def _gather_small(v, name):
    rows, cols = v.shape

    def body(v_ref, out_ref, send_sems, recv_sems):
        me = _my_place()
        out_ref[_index(me)] = v_ref[...]
        copies = []
        for k in range(1, N_DEV):
            copies.append(pltpu.make_async_remote_copy(
                src_ref=v_ref, dst_ref=out_ref.at[_index(me)], send_sem=send_sems.at[k - 1], recv_sem=recv_sems.at[k - 1],
                device_id=_flip(me, k), device_id_type=MESH))
        for cp in copies:
            cp.start()
        for k in range(1, N_DEV):
            pltpu.make_async_remote_copy(
                src_ref=v_ref, dst_ref=out_ref.at[_index(_flip(me, k))], send_sem=send_sems.at[k - 1],
                recv_sem=recv_sems.at[k - 1], device_id=_flip(me, k), device_id_type=MESH).wait_recv()
        for cp in copies:
            cp.wait_send()

    return pl.pallas_call(
        body,
        name=name,
        out_shape=jax.ShapeDtypeStruct((N_DEV, rows, cols), v.dtype),
        in_specs=[pl.BlockSpec(memory_space=pltpu.VMEM)],
        out_specs=pl.BlockSpec(memory_space=pltpu.VMEM),
        scratch_shapes=[pltpu.SemaphoreType.DMA((N_DEV - 1,)), pltpu.SemaphoreType.DMA((N_DEV - 1,))],
        compiler_params=pltpu.CompilerParams(vmem_limit_bytes=VMEM_LIMIT_BYTES),
    )(v)


_HBM = pl.BlockSpec(memory_space=pltpu.HBM)
_SEM = pl.BlockSpec(memory_space=pltpu.SEMAPHORE)
_EFFECT = pltpu.SideEffectType.DATAFLOW_SIDE_EFFECTING


def _remote(src, dst, send_sems, recv_sems, j, to):
    return pltpu.make_async_remote_copy(src_ref=src, dst_ref=dst, send_sem=send_sems.at[j], recv_sem=recv_sems.at[j],
                                        device_id=to, device_id_type=MESH)


def _copies_start(name, bufs, make_copies, nsem):
    nbuf = len(bufs)

    def body(*refs):
        for cp in make_copies(refs[:nbuf], refs[nbuf], refs[nbuf + 1]):
            cp.start()
        refs[-1][...] = jnp.zeros_like(refs[-1])

    sems = pltpu.SemaphoreType.DMA((nsem,))
    res = pl.pallas_call(
        body, name=name,
        out_shape=(sems, sems, *[pltpu.HBM(v.shape, v.dtype) for v in bufs], jax.ShapeDtypeStruct((8, LANES), F32)),
        in_specs=(_HBM,) * nbuf, out_specs=(_SEM, _SEM) + (_HBM,) * nbuf + (pl.BlockSpec(memory_space=pltpu.VMEM),),
        input_output_aliases={i: 2 + i for i in range(nbuf)},
        compiler_params=pltpu.CompilerParams(has_side_effects=_EFFECT),
    )(*[pltpu.with_memory_space_constraint(v, pltpu.HBM) for v in bufs])
    return res[0], res[1], list(res[2:2 + nbuf]), res[-1]


def _copies_wait(name, started, make_copies, after):
    send_sems, recv_sems, bufs, _ = started
    nbuf = len(bufs)

    def body(*refs):
        for cp in make_copies(refs[:nbuf], refs[nbuf], refs[nbuf + 1]):
            cp.wait_send()
            cp.wait_recv()

    return list(pl.pallas_call(
        body, name=name,
        out_shape=tuple(pltpu.HBM(v.shape, v.dtype) for v in bufs),
        in_specs=(_HBM,) * nbuf + (_SEM, _SEM, pl.BlockSpec(memory_space=pl.ANY)), out_specs=(_HBM,) * nbuf,
        input_output_aliases={i: i for i in range(nbuf)},
        compiler_params=pltpu.CompilerParams(has_side_effects=_EFFECT),
    )(*bufs, send_sems, recv_sems, after))


def _to_sibling_copies(refs, send_sems, recv_sems):
    src_ref, land_ref = refs
    me = _my_place()
    return [_remote(src_ref.at[q, 1 - me[2]], land_ref.at[q], send_sems, recv_sems, q, _flip(me, 1)) for q in range(4)]


def _to_chips_copies(refs, send_sems, recv_sems):
    src_ref, land_ref = refs
    me = _my_place()
    copies = []
    for j, k in enumerate(CHIP_FLIPS):
        to = _flip(me, k)
        copies.append(_remote(src_ref.at[2 * to[0] + to[1]], land_ref.at[j], send_sems, recv_sems, j, to))
    return copies


class _Gather:
    def __init__(self, name, blocks):
        self.name, self.n = name, len(blocks)
        at_me = (_index(_my_place()), 0, 0)
        lands = [lax.dynamic_update_slice(lax.empty((N_DEV,) + v.shape, v.dtype), v[None], at_me) for v in blocks]
        self.first = _copies_start(name + "_start", list(blocks) + lands, self._first_copies, 4 * self.n)
        self.token = self.first[3]

    def _first_copies(self, refs, send_sems, recv_sems):
        me = _my_place()
        return [_remote(refs[w], refs[self.n + w].at[_index(me)], send_sems, recv_sems, 4 * w + j, _flip(me, k))
                for w in range(self.n) for j, k in enumerate((1,) + CHIP_FLIPS)]

    def _pass_copies(self, refs, send_sems, recv_sems):
        me = _my_place()
        copies = []
        for w, land in enumerate(refs):
            for j, k in enumerate(CHIP_FLIPS):
                slot = land.at[_index(_flip(me, k))]
                copies.append(_remote(slot, slot, send_sems, recv_sems, 3 * w + j, _flip(me, 1)))
        return copies

    def pass_on(self, after):
        lands = _copies_wait(self.name + "_wait", self.first, self._first_copies, after)[self.n:]
        self.second = _copies_start(self.name + "_pass_start", lands, self._pass_copies, 3 * self.n)
        return self.second[3]

    def finish(self, after):
        return _copies_wait(self.name + "_pass_wait", self.second, self._pass_copies, after)


SUM_SPLIT = 2


def _sum_pairs(parts, theirs):
    nchip, _, rows, cols = parts.shape
    tile = rows // SUM_SPLIT

    def body(c_ref, a_ref, b_ref, o_ref):
        o_ref[...] = (a_ref[0].astype(F32) + b_ref[...].astype(F32)).astype(BF16)

    spec = pl.BlockSpec((1, tile, cols), lambda q, t, c_ref: (q, t, 0))
    grid_spec = pltpu.PrefetchScalarGridSpec(
        num_scalar_prefetch=1, grid=(nchip, SUM_SPLIT),
        in_specs=[pl.BlockSpec((1, 1, tile, cols), lambda q, t, c_ref: (q, c_ref[0], t, 0)), spec], out_specs=spec)
    return pl.pallas_call(body, name="grad_sum_sibling", grid_spec=grid_spec,
                          out_shape=jax.ShapeDtypeStruct((nchip, rows, cols), BF16),
                          compiler_params=_params("parallel", "parallel"))(lax.axis_index("c").reshape(1), parts, theirs)


def _sum_final(chip_sum, got):
    _, rows, cols = chip_sum.shape
    tile = rows // SUM_SPLIT

    def body(q_ref, a_ref, g_ref, o_ref):
        o_ref[...] = ((a_ref[0].astype(F32) + g_ref[0].astype(F32)) + g_ref[1].astype(F32)) + g_ref[2].astype(F32)

    grid_spec = pltpu.PrefetchScalarGridSpec(
        num_scalar_prefetch=1, grid=(SUM_SPLIT,),
        in_specs=[pl.BlockSpec((1, tile, cols), lambda t, q_ref: (q_ref[0], t, 0)),
                  pl.BlockSpec((3, tile, cols), lambda t, q_ref: (0, t, 0))],
        out_specs=pl.BlockSpec((tile, cols), lambda t, q_ref: (t, 0)))
    my_chip = (2 * lax.axis_index("x") + lax.axis_index("y")).reshape(1)
    return pl.pallas_call(body, name="grad_sum_chips", grid_spec=grid_spec, out_shape=jax.ShapeDtypeStruct((rows, cols), F32),
                          compiler_params=_params("parallel"))(my_chip, chip_sum, got)


class _ReduceScatter:
    def __init__(self, name, slabs):
        self.name, self.rows = name, slabs.shape[1]
        parts = slabs.reshape(4, 2, self.rows, D_MODEL)
        self.first = _copies_start(name + "_sibling_start", [parts, lax.empty((4, self.rows, D_MODEL), slabs.dtype)],
                                   _to_sibling_copies, 4)
        self.token = self.first[3]

    def between_chips(self, after):
        parts, theirs = _copies_wait(self.name + "_sibling_wait", self.first, _to_sibling_copies, after)
        chip_sum = _sum_pairs(parts, theirs)
        self.second = _copies_start(self.name + "_chips_start", [chip_sum, lax.empty((3, self.rows, D_MODEL), chip_sum.dtype)],
                                    _to_chips_copies, 3)
        return self.second[3]

    def finish(self, after):
        chip_sum, got = _copies_wait(self.name + "_chips_wait", self.second, _to_chips_copies, after)
        return _sum_final(chip_sum, got)


def _ada_fwd(c_all, w, b):
    nb, _ = c_all.shape
    ncol = w.shape[1]

    def body(c_ref, w_ref, b_ref, o_ref):
        c = c_ref[...]
        act = (c * _sigmoid(c)).astype(BF16)
        o_ref[...] = jnp.dot(act, w_ref[...].astype(BF16), preferred_element_type=F32) + b_ref[...]

    return pl.pallas_call(body, name="ada_fwd", out_shape=jax.ShapeDtypeStruct((nb, ncol), F32),
                          compiler_params=pltpu.CompilerParams(vmem_limit_bytes=VMEM_LIMIT_BYTES))(c_all, w, b)


def _ada_wgrad(c_all_t, dmod_cols):
    d, nb = c_all_t.shape
    ncol = dmod_cols.shape[1]

    def body(ct_ref, dm_ref, o_ref):
        ct = ct_ref[...]
        act = (ct * _sigmoid(ct)).astype(BF16).astype(F32)
        dm = dm_ref[...].astype(BF16).astype(F32)
        acc = act[:, 0:1] * dm[0:1, :]
        for i in range(1, nb):
            acc = acc + act[:, i:i + 1] * dm[i:i + 1, :]
        o_ref[...] = acc

    return pl.pallas_call(body, name="ada_wgrad", out_shape=jax.ShapeDtypeStruct((d, ncol), F32),
                          compiler_params=pltpu.CompilerParams(vmem_limit_bytes=VMEM_LIMIT_BYTES))(c_all_t, dmod_cols)


SMALL_ROWS = 24


def _reduce_small(gathered):
    def body(g_ref, o_ref):
        acc = g_ref[0]
        for dev in range(1, N_DEV):
            acc = acc + g_ref[dev]
        o_ref[...] = acc

    return pl.pallas_call(body, name="reduce_small", out_shape=jax.ShapeDtypeStruct(gathered.shape[1:], F32))(gathered)


def _adamw_math(w, g, m, v):
    nm = ADAM_B1 * m + (1.0 - ADAM_B1) * g
    nv = ADAM_B2 * v + (1.0 - ADAM_B2) * (g * g)
    bc1 = 1.0 - ADAM_B1 ** ADAM_STEP
    bc2 = 1.0 - ADAM_B2 ** ADAM_STEP
    return -ADAM_LR * ((nm / bc1) / (jnp.sqrt(nv / bc2) + ADAM_EPS) + ADAM_WD * w), nm, nv


def _adamw_small(ws, gs, ms, vs, name):
    n = len(ws)

    def body(*refs):
        for i in range(n):
            res = _adamw_math(*(refs[k * n + i][...] for k in range(4)))
            for k in range(3):
                refs[(4 + k) * n + i][...] = res[k]

    shapes = [jax.ShapeDtypeStruct(w.shape, F32) for w in ws]
    res = pl.pallas_call(body, name=name, out_shape=shapes * 3)(*ws, *gs, *ms, *vs)
    return [(res[i], res[n + i], res[2 * n + i]) for i in range(n)]


def _adamw(w, g, m, v, name):
    rows, cols = w.shape
    tile = rows
    for cand in range(min(rows // 2, 512) // 8 * 8, 7, -8):
        if rows % cand == 0:
            tile = cand
            break
    spec = pl.BlockSpec((tile, cols), lambda t: (t, 0))

    def body(w_ref, g_ref, m_ref, v_ref, d_ref, nm_ref, nv_ref):
        d_ref[...], nm_ref[...], nv_ref[...] = _adamw_math(w_ref[...], g_ref[...], m_ref[...], v_ref[...])

    shp = jax.ShapeDtypeStruct((rows, cols), F32)
    return pl.pallas_call(body, name=name, grid=(rows // tile,), in_specs=[spec] * 4, out_specs=[spec] * 3, out_shape=[shp] * 3,
                          compiler_params=_params("parallel"))(w, g, m, v)


_WEIGHTS = ("w_ada", "b_ada", "w_in", "sinks", "w_branch_a", "w_branch_b", "w_o", "ln1_g", "ln1_b", "w_gate_up", "w_down",
            "ln2_g", "ln2_b")
_TRANSPOSED = ("w_in", "w_branch_b", "w_gate_up")


def _pack_shard(name, w):
    w = w.astype(BF16)
    if name in _TRANSPOSED:
        w = w.T
    return w.reshape(-1, D_MODEL)


def _unpack_full(name, slab):
    if name == "w_branch_b":
        return slab.reshape(N_DEV * 128, 512)
    return slab.reshape(-1, D_MODEL)


def _unpack_group(group, gathered):
    return {n: _unpack_full(n, slab) for (n, _), slab in zip(group, gathered)}


def _unpack_grads(group, g_packed):
    g_w, off = {}, 0
    for n, r in group:
        part = g_packed[off:off + r]
        off += r
        g_w[n] = part.reshape(128, 512) if n == "w_branch_b" else part
    return g_w


def kernel(x, c, positions, w_ada, b_ada, w_in, sinks, w_branch_a, w_branch_b, w_o, ln1_g, ln1_b, w_gate_up, w_down, ln2_g, ln2_b, loss_target, m_w_ada, m_b_ada, m_w_in, m_sinks, m_w_branch_a, m_w_branch_b, m_w_o, m_ln1_g, m_ln1_b, m_w_gate_up, m_w_down, m_ln2_g, m_ln2_b, v_w_ada, v_b_ada, v_w_in, v_sinks, v_w_branch_a, v_w_branch_b, v_w_o, v_ln1_g, v_ln1_b, v_w_gate_up, v_w_down, v_ln2_g, v_ln2_b):
    weights = dict(w_ada=w_ada, b_ada=b_ada, w_in=w_in, sinks=sinks, w_branch_a=w_branch_a, w_branch_b=w_branch_b, w_o=w_o,
                   ln1_g=ln1_g, ln1_b=ln1_b, w_gate_up=w_gate_up, w_down=w_down, ln2_g=ln2_g, ln2_b=ln2_b)
    m_in = dict(w_ada=m_w_ada, b_ada=m_b_ada, w_in=m_w_in, sinks=m_sinks, w_branch_a=m_w_branch_a, w_branch_b=m_w_branch_b,
                w_o=m_w_o, ln1_g=m_ln1_g, ln1_b=m_ln1_b, w_gate_up=m_w_gate_up, w_down=m_w_down, ln2_g=m_ln2_g, ln2_b=m_ln2_b)
    v_in = dict(w_ada=v_w_ada, b_ada=v_b_ada, w_in=v_w_in, sinks=v_sinks, w_branch_a=v_w_branch_a, w_branch_b=v_w_branch_b,
                w_o=v_w_o, ln1_g=v_ln1_g, ln1_b=v_ln1_b, w_gate_up=v_w_gate_up, w_down=v_w_down, ln2_g=v_ln2_g, ln2_b=v_ln2_b)
    bsz = x.shape[0]
    me = _index(_my_place())
    ada_cols = w_ada.shape[2]
    outs = {}

    def adamw(n, g):
        w2, m2, v2 = (t[n][0] if t[n].ndim == 3 else t[n] for t in (weights, m_in, v_in))
        shape = weights[n].shape
        if n in _TRANSPOSED:
            dlt, nm, nv = _adamw(w2.T, g, m2.T, v2.T, "adamw_" + n)
            outs[n] = tuple(t.T.reshape(shape) for t in (g, dlt, nm, nv))
        else:
            dlt, nm, nv = _adamw(w2, g, m2, v2, "adamw_" + n)
            outs[n] = tuple(t.reshape(shape) for t in (g, dlt, nm, nv))
        return nv

    packed_in = [_pack_shard(n, weights[n][0]) for n, _ in GROUP_IN]
    packed_rest = [_pack_shard(n, weights[n][0]) for n, _ in GROUP_REST]
    c_all = _gather_small(jnp.pad(c, ((0, 8 - bsz), (0, 0))), "gather_c")[:, :bsz].reshape(N_DEV * bsz, D_MODEL)
    gather_in = _Gather("gather_w_in", lax.optimization_barrier((packed_in, c_all))[0])
    b_cols = lax.dynamic_slice_in_dim(b_ada, me * ada_cols, ada_cols, axis=1)
    mod_cols = _ada_fwd(c_all, w_ada[0], b_cols + gather_in.token[0, 0])
    tables = _rope_tables(positions)
    mod_cols, tables, packed_rest = lax.optimization_barrier((mod_cols, tables, packed_rest))
    mod_all = _gather_small(mod_cols, "gather_mod").transpose(1, 0, 2).reshape(N_DEV * bsz, 6, D_MODEL)
    gather_rest = _Gather("gather_rest", lax.optimization_barrier((packed_rest, mod_all))[0])
    mod = jnp.pad(lax.dynamic_slice_in_dim(mod_all, me * bsz, bsz, axis=0), ((0, 0), (0, 2), (0, 0)))
    mod = mod + gather_rest.token[0, 0]
    mod = mod + gather_in.pass_on(mod)[0, 0]

    scatters = {}

    def get_w_in(after):
        return _unpack_group(GROUP_IN, gather_in.finish(after))["w_in"]

    def get_rest(after):
        return _unpack_group(GROUP_REST, gather_rest.finish(after))

    def pack_grads(group, grads):
        return jnp.concatenate([grads[n].reshape(N_DEV, r, D_MODEL) for n, r in group], axis=1)

    def hook(point, value):
        if point == "projected":
            return gather_rest.pass_on(value)
        if point == "grads_rest":
            scatters["rest"] = _ReduceScatter("scatter_rest", pack_grads(GROUP_REST, value))
            return scatters["rest"].token
        if point == "delta_done":
            return scatters["rest"].between_chips(value)
        if point == "grads_w_in":
            scatters["in"] = _ReduceScatter("scatter_w_in", pack_grads(GROUP_IN, value))
            return scatters["in"].token
        if point == "dgrad_done":
            return None
        raise ValueError(point)

    loss_part, grad_x, dmod, small = _layer_step(x, mod, tables, sinks[0], ln1_g, ln1_b, ln2_g, ln2_b, loss_target,
                                                 get_w_in, get_rest, hook)

    rows = jnp.concatenate([dmod.reshape(bsz * 6, D_MODEL), small, jnp.full((1, D_MODEL), loss_part, F32),
                            jnp.zeros((SMALL_ROWS - bsz * 6 - 6, D_MODEL), F32)], axis=0)
    small_all = _gather_small(rows, "gather_small")
    small_all = small_all + scatters["in"].between_chips(small_all)[0, 0]
    sums = _reduce_small(small_all)
    loss = sums[bsz * 6 + 5, 0]
    dmod_all = small_all[:, :bsz * 6].reshape(N_DEV * bsz, 6 * D_MODEL)
    small_g = {"b_ada": functools.reduce(jnp.add, [sums[6 * i:6 * i + 6] for i in range(bsz)]).reshape(1, 6 * D_MODEL),
               "sinks": sums[bsz * 6 + 4][::HEAD_DIM][None]}
    small_g.update({n: sums[bsz * 6 + i][None] for i, n in enumerate(("ln1_g", "ln1_b", "ln2_g", "ln2_b"))})
    names = list(small_g)
    for n, (dlt, nm, nv) in zip(names, _adamw_small([weights[n] for n in names], [small_g[n] for n in names],
                                                     [m_in[n] for n in names], [v_in[n] for n in names], "adamw_small")):
        outs[n] = (small_g[n], dlt, nm, nv)
    dmod_cols = lax.dynamic_slice_in_dim(dmod_all, me * ada_cols, ada_cols, axis=1)
    last = adamw("w_ada", _ada_wgrad(c_all.T, dmod_cols))
    for n, g in _unpack_grads(GROUP_REST, scatters["rest"].finish(last)).items():
        adamw(n, g)
    done = lax.optimization_barrier(tuple(outs[n][3] for n in outs))
    for n, g in _unpack_grads(GROUP_IN, scatters["in"].finish(done[0])).items():
        adamw(n, g)

    return (loss, grad_x, *[outs[n][0] for n in _WEIGHTS], *[outs[n][1] for n in _WEIGHTS], *[outs[n][2] for n in _WEIGHTS],
            *[outs[n][3] for n in _WEIGHTS])
```

```python
import functools

import jax
import jax.numpy as jnp
from jax import lax
from jax.experimental import pallas as pl
from jax.experimental.pallas import tpu as pltpu

F32 = jnp.float32
BF16 = jnp.bfloat16

D_MODEL = 1024
HEAD_DIM = 64
A_Q_HEADS = 16
A_WINDOW = 128
B_PATTERNS = ((128, 1), (512, 4), (2048, 16))
B_HEADS_PER_GROUP = 8
D_FF = 2816
QBLOCK = 128
ROPE_THETA = 10000.0
LN_EPS = 1e-5
DEEPNORM_ALPHA = 2.0 ** 0.25
NEG_INF = -1e30
ADAM_LR, ADAM_B1, ADAM_B2, ADAM_EPS, ADAM_WD, ADAM_STEP = 0.001, 0.9, 0.999, 1e-08, 0.01, 10

N_DEV = 8
LANES = 128
VMEM_LIMIT_BYTES = 56 * 1024 * 1024
MESH = pl.DeviceIdType.MESH

OFF_QA, OFF_KVA, OFF_QKVB, OFF_GAB = 0, 1024, 1280, 5888
GROUP_IN = (("w_in", 992),)
GROUP_REST = (("w_branch_a", 128), ("w_branch_b", 64), ("w_o", 128), ("w_gate_up", 704), ("w_down", 352))


def _params(*sem):
    return pltpu.CompilerParams(dimension_semantics=sem, vmem_limit_bytes=VMEM_LIMIT_BYTES)


def _sigmoid(x):
    return 1.0 / (1.0 + jnp.exp(-x))


_DIMS = {"nn": (((1,), (0,)), ((), ())), "nt": (((1,), (1,)), ((), ())), "tn": (((0,), (0,)), ((), ()))}


def _matmul(a, b, *, mode, tm, tn, tk, name, out_dtype=None, n=None, b_off=0, token=None, ins=(), outs=None, epilogue=None,
            lhs_fn=None):
    if mode == "nn":
        (m, k), nn_ = a.shape, b.shape[1]
    elif mode == "nt":
        (m, k), nn_ = a.shape, (b.shape[0] if n is None else n)
    else:
        (k, m), nn_ = a.shape, b.shape[1]
    assert m % tm == 0 and nn_ % tn == 0 and k % tk == 0 and b_off % tn == 0, (name, m, nn_, k)
    nk = k // tk
    joff = b_off // tn
    if mode == "nn":
        a_spec = pl.BlockSpec((tm, tk), lambda i, j, kk: (i, kk))
        b_spec = pl.BlockSpec((tk, tn), lambda i, j, kk: (kk, j))
    elif mode == "nt":
        a_spec = pl.BlockSpec((tm, tk), lambda i, j, kk: (i, kk))
        b_spec = pl.BlockSpec((tn, tk), lambda i, j, kk: (j + joff, kk))
    else:
        a_spec = pl.BlockSpec((tk, tm), lambda i, j, kk: (kk, i))
        b_spec = pl.BlockSpec((tk, tn), lambda i, j, kk: (kk, j))
    dims = _DIMS[mode]
    has_token = token is not None
    plain = epilogue is None
    if plain:
        outs = [(jax.ShapeDtypeStruct((m, nn_), out_dtype), (tm, tn), lambda i, j: (i, j))]

        def epilogue(acc, i, j, in_refs, out_refs):
            out_refs[0][...] = acc.astype(out_refs[0].dtype)

    nin = len(ins)
    nscratch = 1 if lhs_fn is None else 2
    assert lhs_fn is None or nk == 1

    def body(*refs):
        a_ref, b_ref = refs[:2]
        in_refs = refs[2:2 + nin]
        out_refs = refs[2 + nin + has_token:-nscratch]
        acc_ref = refs[-nscratch]
        kk = pl.program_id(2)
        if lhs_fn is None:
            lhs = a_ref[...].astype(BF16)
        else:
            lhs_ref = refs[-1]

            @pl.when(pl.program_id(1) == 0)
            def _():
                lhs_ref[...] = lhs_fn(a_ref, in_refs, out_refs)

            lhs = lhs_ref[...]
        part = lax.dot_general(lhs, b_ref[...].astype(BF16), dims, preferred_element_type=F32)

        def finish(acc):
            epilogue(acc, pl.program_id(0), pl.program_id(1), in_refs, out_refs)

        if nk == 1:
            finish(part)
        else:
            @pl.when(kk == 0)
            def _():
                acc_ref[...] = part

            @pl.when(kk > 0)
            def _():
                acc_ref[...] += part

            @pl.when(kk == nk - 1)
            def _():
                finish(acc_ref[...])

    def spec(block, index):
        return pl.BlockSpec(block, lambda i, j, kk: index(i, j))

    in_specs, args = [a_spec, b_spec], [a, b]
    for arr, block, index in ins:
        in_specs.append(spec(block, index))
        args.append(arr)
    if has_token:
        in_specs.append(pl.BlockSpec(token.shape, lambda i, j, kk: (0, 0)))
        args.append(token)
    res = pl.pallas_call(
        body,
        name=name,
        grid=(m // tm, nn_ // tn, nk),
        in_specs=in_specs,
        out_specs=[spec(block, index) for _, block, index in outs],
        out_shape=[shape for shape, _, _ in outs],
        scratch_shapes=[pltpu.VMEM((tm, tn) if nk > 1 else (8, LANES), F32)] + ([] if lhs_fn is None else [pltpu.VMEM((tm, tk), BF16)]),
        compiler_params=_params("arbitrary", "arbitrary", "arbitrary"),
    )(*args)
    return res[0] if plain else res


def _proj_rope(a, bt, cos, sin, *, n, b_off, rope_cols, tm, tn, name, out_dtype=F32):
    m, k = a.shape
    assert m % tm == 0 and n % tn == 0 and b_off % tn == 0 and rope_cols % tn == 0, name
    joff = b_off // tn
    nrope = rope_cols // tn

    def body(a_ref, b_ref, c_ref, s_ref, o_ref):
        acc = lax.dot_general(a_ref[...], b_ref[...], _DIMS["nt"], preferred_element_type=F32)
        j = pl.program_id(1)

        @pl.when(j < nrope)
        def _():
            o_ref[...] = _rope(acc, c_ref[...], s_ref[...], coarse=True).astype(o_ref.dtype)

        @pl.when(j >= nrope)
        def _():
            o_ref[...] = acc.astype(o_ref.dtype)

    table = pl.BlockSpec((tm, LANES), lambda i, j: (i, 0))
    return pl.pallas_call(
        body,
        name=name,
        grid=(m // tm, n // tn),
        in_specs=[pl.BlockSpec((tm, k), lambda i, j: (i, 0)), pl.BlockSpec((tn, k), lambda i, j: (j + joff, 0)), table, table],
        out_specs=pl.BlockSpec((tm, tn), lambda i, j: (i, j)),
        out_shape=jax.ShapeDtypeStruct((m, n), out_dtype),
        compiler_params=_params("parallel", "parallel"),
    )(a, bt, cos, sin)


ROW_TILE = 256


def _rows(width, col=0):
    return pl.BlockSpec((1, ROW_TILE, width), lambda b, t: (b, t, col))


def _per_batch(nrows, width):
    return pl.BlockSpec((1, nrows, width), lambda b, t: (b, 0, 0))


def _row_call(body, name, bsz, seq, in_specs, out_specs, out_shape, accumulates=False):
    return pl.pallas_call(
        body,
        name=name,
        grid=(bsz, seq // ROW_TILE),
        in_specs=in_specs,
        out_specs=out_specs,
        out_shape=out_shape,
        compiler_params=_params("parallel", "arbitrary" if accumulates else "parallel"),
    )


def _acc_rows(acc_ref, first, rows):
    @pl.when(first)
    def _():
        acc_ref[...] = jnp.zeros_like(acc_ref)

    for r, val in enumerate(rows):
        acc_ref[0, r:r + 1, :] += val


def _colsum(v):
    return jnp.sum(v, axis=0, keepdims=True)


def _ln_stats(z):
    mu = jnp.mean(z, axis=-1, keepdims=True)
    zc = z - mu
    var = jnp.mean(zc * zc, axis=-1, keepdims=True)
    rstd = lax.rsqrt(var + LN_EPS)
    return zc * rstd, rstd


def _ln_bwd(dxhat, xhat, rstd):
    m1 = jnp.mean(dxhat, axis=-1, keepdims=True)
    m2 = jnp.mean(dxhat * xhat, axis=-1, keepdims=True)
    return rstd * (dxhat - m1 - xhat * m2)


def _modulate_in(x, mod):
    bsz, seq, d = x.shape

    def body(x_ref, mod_ref, u_ref):
        u_ref[0] = (x_ref[0] * (1.0 + mod_ref[0, 1:2, :]) + mod_ref[0, 0:1, :]).astype(BF16)

    return _row_call(body, "modulate_in", bsz, seq, [_rows(d), _per_batch(8, d)], _rows(d),
                     jax.ShapeDtypeStruct((bsz, seq, d), BF16))(x, mod)


EP_TILE = 512


def _ep_specs(seq, d):
    tiles = seq // EP_TILE
    return ((EP_TILE, d), lambda i, j: (i, 0)), ((1, 8, d), lambda i, j: (i // tiles, 0, 0)), ((1, d), lambda i, j: (0, 0))


def _wo_ln1(merged, wo, x, mod, g, b, seq):
    ntok, d = x.shape
    row, per_b, whole = _ep_specs(seq, d)

    def epilogue(y, i, j, ins, outs):
        x_ref, mod_ref, g_ref, b_ref = ins
        y_ref, h_ref, u_ref = outs
        z = DEEPNORM_ALPHA * x_ref[...] + (1.0 + mod_ref[0, 2:3, :]) * y
        xhat, _ = _ln_stats(z)
        h = xhat * g_ref[...] + b_ref[...]
        y_ref[...] = y
        h_ref[...] = h
        u_ref[...] = (h * (1.0 + mod_ref[0, 4:5, :]) + mod_ref[0, 3:4, :]).astype(BF16)

    f32, bf16 = jax.ShapeDtypeStruct((ntok, d), F32), jax.ShapeDtypeStruct((ntok, d), BF16)
    return _matmul(merged, wo, mode="nn", tm=EP_TILE, tn=d, tk=d, name="w_o_ln1",
                   ins=[(x,) + row, (mod,) + per_b, (g,) + whole, (b,) + whole],
                   outs=[(f32,) + row, (f32,) + row, (bf16,) + row], epilogue=epilogue)


FF_HALF = D_FF // 2


def _interleave_gate_up(w):
    return w.reshape(2, 2, FF_HALF, w.shape[1]).transpose(1, 0, 2, 3).reshape(w.shape)


def _gate_up_silu(u2, wgut_i):
    ntok = u2.shape[0]

    def epilogue(h, i, j, ins, outs):
        h_ref, a_ref = outs
        hg, hu = h[:, :FF_HALF], h[:, FF_HALF:]
        h_ref[...] = h.astype(BF16)
        a_ref[...] = (hg * _sigmoid(hg) * hu).astype(BF16)

    return _matmul(u2, wgut_i, mode="nt", tm=EP_TILE, tn=2 * FF_HALF, tk=u2.shape[1], name="gate_up_silu",
                   outs=[(jax.ShapeDtypeStruct((ntok, 2 * D_FF), BF16), (EP_TILE, 2 * FF_HALF), lambda i, j: (i, j)),
                         (jax.ShapeDtypeStruct((ntok, D_FF), BF16), (EP_TILE, FF_HALF), lambda i, j: (i, j))],
                   epilogue=epilogue)


def _down_dgrad_silu_bwd(dy2, wd, h_i):
    ntok = dy2.shape[0]
    wide = ((EP_TILE, 2 * FF_HALF), lambda i, j: (i, j))

    def epilogue(da, i, j, ins, outs):
        h = ins[0][...].astype(F32)
        hg, hu = h[:, :FF_HALF], h[:, FF_HALF:]
        sg = _sigmoid(hg)
        outs[0][:, :FF_HALF] = (da * hu * (sg * (1.0 + hg * (1.0 - sg)))).astype(BF16)
        outs[0][:, FF_HALF:] = (da * (hg * sg)).astype(BF16)

    return _matmul(dy2, wd, mode="nt", tm=EP_TILE, tn=FF_HALF, tk=dy2.shape[1], name="down_dgrad_silu_bwd",
                   ins=[(h_i,) + wide], outs=[(jax.ShapeDtypeStruct((ntok, 2 * D_FF), BF16),) + wide], epilogue=epilogue)[0]


def _down_ln2_loss_bwd(a, wd, h1, mod, g, b, target, seq):
    ntok, d = h1.shape
    row, per_b, whole = _ep_specs(seq, d)
    tiles = seq // EP_TILE

    def epilogue(y, i, j, ins, outs):
        h_ref, mod_ref, g_ref, b_ref, t_ref = ins
        dy_ref, dh_ref, acc_ref = outs
        gate = 1.0 + mod_ref[0, 5:6, :]
        z = DEEPNORM_ALPHA * h_ref[...] + gate * y
        xhat, rstd = _ln_stats(z)
        diff = xhat * g_ref[...] + b_ref[...] - t_ref[...]
        loss = 0.5 * jnp.sum(jnp.sum(diff * diff, axis=-1, keepdims=True) / d, axis=0, keepdims=True)
        dout = diff / d
        dz = _ln_bwd(dout * g_ref[...], xhat, rstd)
        dy_ref[...] = (gate * dz).astype(BF16)
        dh_ref[...] = DEEPNORM_ALPHA * dz
        _acc_rows(acc_ref, i % tiles == 0,
                  [_colsum(dout * xhat), _colsum(dout), _colsum(dz * y), jnp.broadcast_to(loss, (1, d))])

    return _matmul(a, wd, mode="nn", tm=EP_TILE, tn=d, tk=a.shape[1], name="down_ln2_loss_bwd",
                   ins=[(h1,) + row, (mod,) + per_b, (g,) + whole, (b,) + whole, (target,) + row],
                   outs=[(jax.ShapeDtypeStruct((ntok, d), BF16),) + row, (jax.ShapeDtypeStruct((ntok, d), F32),) + row,
                         (jax.ShapeDtypeStruct((ntok // seq, 8, d), F32),) + per_b], epilogue=epilogue)


def _gate_up_dgrad_ln1_bwd(dh, wgut, dh1a, x, y1, mod, g, b, seq):
    ntok, d = x.shape
    row, per_b, whole = _ep_specs(seq, d)
    tiles = seq // EP_TILE

    def epilogue(du, i, j, ins, outs):
        dh_ref, x_ref, y_ref, mod_ref, g_ref, b_ref = ins
        dy_ref, dx_ref, acc_ref = outs
        y = y_ref[...]
        gate = 1.0 + mod_ref[0, 2:3, :]
        z = DEEPNORM_ALPHA * x_ref[...] + gate * y
        xhat, rstd = _ln_stats(z)
        h1 = xhat * g_ref[...] + b_ref[...]
        dh1 = dh_ref[...] + du * (1.0 + mod_ref[0, 4:5, :])
        dz = _ln_bwd(dh1 * g_ref[...], xhat, rstd)
        dy_ref[...] = (gate * dz).astype(BF16)
        dx_ref[...] = DEEPNORM_ALPHA * dz
        _acc_rows(acc_ref, i % tiles == 0,
                  [_colsum(dh1 * xhat), _colsum(dh1), _colsum(dz * y), _colsum(du * h1), _colsum(du)])

    return _matmul(dh, wgut, mode="nn", tm=EP_TILE, tn=d, tk=D_FF, name="gate_up_dgrad_ln1_bwd",
                   ins=[(dh1a,) + row, (x,) + row, (y1,) + row, (mod,) + per_b, (g,) + whole, (b,) + whole],
                   outs=[(jax.ShapeDtypeStruct((ntok, d), BF16),) + row, (jax.ShapeDtypeStruct((ntok, d), F32),) + row,
                         (jax.ShapeDtypeStruct((ntok // seq, 8, d), F32),) + per_b], epilogue=epilogue)


def _wo_dgrad_gate_bwd(dy1, wo, gab, ya, yb):
    ntok, d = ya.shape
    tm, tn = 1024, 512
    tile = ((tm, tn), lambda i, j: (i, j))
    tile_b = ((tm, tn), lambda i, j: (i, j + d // tn))

    def epilogue(dm_, i, j, ins, outs):
        ga_ref, gb_ref, ya_ref, yb_ref = ins
        dya_ref, dyb_ref, dga_ref, dgb_ref = outs
        sa, sb = _sigmoid(ga_ref[...].astype(F32)), _sigmoid(gb_ref[...].astype(F32))
        dya_ref[...] = (dm_ * sa).astype(BF16)
        dyb_ref[...] = (dm_ * sb).astype(BF16)
        dga_ref[...] = (dm_ * ya_ref[...].astype(F32) * sa * (1.0 - sa)).astype(BF16)
        dgb_ref[...] = (dm_ * yb_ref[...].astype(F32) * sb * (1.0 - sb)).astype(BF16)

    shp = jax.ShapeDtypeStruct((ntok, d), BF16)
    return _matmul(dy1, wo, mode="nt", tm=tm, tn=tn, tk=d, name="w_o_dgrad_gate_bwd",
                   ins=[(gab,) + tile, (gab,) + tile_b, (ya,) + tile, (yb,) + tile],
                   outs=[(shp,) + tile] * 4, epilogue=epilogue)


def _w_in_dgrad_grad_x(dproj, wint, dxa, x, mod, seq, token):
    ntok, d = x.shape
    row, per_b, _ = _ep_specs(seq, d)
    tiles = seq // EP_TILE

    def epilogue(du, i, j, ins, outs):
        dxa_ref, x_ref, mod_ref = ins
        gx_ref, acc_ref = outs
        gx_ref[...] = dxa_ref[...] + du * (1.0 + mod_ref[0, 1:2, :])
        _acc_rows(acc_ref, i % tiles == 0, [_colsum(du * x_ref[...]), _colsum(du)])

    return _matmul(dproj, wint, mode="nn", tm=EP_TILE, tn=d, tk=wint.shape[0] // 2, name="w_in_dgrad_grad_x", token=token,
                   ins=[(dxa,) + row, (x,) + row, (mod,) + per_b],
                   outs=[(jax.ShapeDtypeStruct((ntok, d), F32),) + row, (jax.ShapeDtypeStruct((ntok // seq, 8, d), F32),) + per_b],
                   epilogue=epilogue)


def _merge_branch_b_gate(os_, ls_, wbbt, gab, ya):
    ntok, d = ya.shape
    w = os_[0].shape[1]
    tm, tn = 1024, 512
    tile = ((tm, tn), lambda i, j: (i, j))
    tile_b = ((tm, tn), lambda i, j: (i, j + d // tn))
    row = ((tm, w), lambda i, j: (i, 0))

    def lhs_fn(o0_ref, ins, outs):
        os_r, ls_r = (o0_ref,) + tuple(ins[3:5]), ins[5:8]
        ls = [l[...] for l in ls_r]
        mx = jnp.maximum(jnp.maximum(ls[0], ls[1]), ls[2])
        es = [jnp.exp(l - mx) for l in ls]
        den = es[0] + es[1] + es[2]
        ob = functools.reduce(jnp.add, [(e / den) * o[...].astype(F32) for e, o in zip(es, os_r)]).astype(BF16)
        outs[2][...] = ob
        return ob

    def epilogue(yb, i, j, ins, outs):
        ga_ref, gb_ref, ya_ref = ins[:3]
        yb_ref, merged_ref = outs[:2]
        yb_ref[...] = yb.astype(BF16)
        merged_ref[...] = (_sigmoid(ga_ref[...].astype(F32)) * ya_ref[...].astype(F32)
                           + _sigmoid(gb_ref[...].astype(F32)) * yb).astype(BF16)

    shp = jax.ShapeDtypeStruct((ntok, d), BF16)
    return _matmul(os_[0], wbbt, mode="nt", tm=tm, tn=tn, tk=w, name="merge_branch_b_gate", lhs_fn=lhs_fn,
                   ins=[(gab,) + tile, (gab,) + tile_b, (ya,) + tile] + [(v,) + row for v in list(os_[1:]) + list(ls_)],
                   outs=[(shp,) + tile] * 2 + [(jax.ShapeDtypeStruct((ntok, w), BF16),) + row], epilogue=epilogue)


def _segsum64(v):
    rows, width = v.shape
    ri = lax.broadcasted_iota(jnp.int32, (LANES, LANES), 0) // HEAD_DIM
    ci = lax.broadcasted_iota(jnp.int32, (LANES, LANES), 1) // HEAD_DIM
    ones = jnp.where(ri == ci, 1.0, 0.0).astype(BF16)
    out = []
    for c in range(width // LANES):
        part = v[:, c * LANES:(c + 1) * LANES]
        hi = part.astype(BF16)
        lo = (part - hi.astype(F32)).astype(BF16)
        out.append(jnp.dot(hi, ones, preferred_element_type=F32) + jnp.dot(lo, ones, preferred_element_type=F32))
    return jnp.concatenate(out, axis=1) if len(out) > 1 else out[0]


def _branch_b_dgrad_merge_bwd(dyb, wbbt, os_, ls_):
    ntok, w = os_[0].shape
    row = ((EP_TILE, w), lambda i, j: (i, 0))

    def epilogue(dob_, i, j, ins, outs):
        os_r, ls_r = ins[:3], ins[3:]
        do_r, dd_r = outs[:3], outs[3:]
        ls = [l[...] for l in ls_r]
        mx = jnp.maximum(jnp.maximum(ls[0], ls[1]), ls[2])
        es = [jnp.exp(l - mx) for l in ls]
        den = es[0] + es[1] + es[2]
        ws = [e / den for e in es]
        dws = [_segsum64(dob_ * o[...].astype(F32)) for o in os_r]
        mean = ws[0] * dws[0] + ws[1] * dws[1] + ws[2] * dws[2]
        for wg, do_ref, dd_ref in zip(ws, do_r, dd_r):
            do_ref[...] = wg * dob_
            dd_ref[...] = -wg * mean

    shp = jax.ShapeDtypeStruct((ntok, w), F32)
    return _matmul(dyb, wbbt, mode="nn", tm=EP_TILE, tn=w, tk=dyb.shape[1], name="branch_b_dgrad_merge_bwd",
                   ins=[(v,) + row for v in list(os_) + list(ls_)], outs=[(shp,) + row] * 6, epilogue=epilogue)


def _branch_a_dgrad_delta(dya, wba, oa, lse_a, sinks_exp, seq):
    ntok, w = oa.shape
    row, per_b, whole = _ep_specs(seq, w)
    tiles = seq // EP_TILE

    def epilogue(do_, i, j, ins, outs):
        o_ref, l_ref, s_ref = ins
        do_ref, dd_ref, acc_ref = outs
        dd = -_segsum64(do_ * o_ref[...].astype(F32))
        do_ref[...] = do_
        dd_ref[...] = dd
        _acc_rows(acc_ref, i % tiles == 0, [_colsum(dd * jnp.exp(s_ref[...] - l_ref[...]))])

    shp = jax.ShapeDtypeStruct((ntok, w), F32)
    return _matmul(dya, wba, mode="nt", tm=EP_TILE, tn=w, tk=dya.shape[1], name="branch_a_dgrad_delta",
                   ins=[(oa,) + row, (lse_a,) + row, (sinks_exp,) + whole],
                   outs=[(shp,) + row, (shp,) + row, (jax.ShapeDtypeStruct((ntok // seq, 8, w), F32),) + per_b],
                   epilogue=epilogue)


def _swap_halves(v):
    src = lax.broadcasted_iota(jnp.int32, (LANES, LANES), 0)
    dst = lax.broadcasted_iota(jnp.int32, (LANES, LANES), 1)
    partner = jnp.where((dst % HEAD_DIM) < HEAD_DIM // 2, dst + HEAD_DIM // 2, dst - HEAD_DIM // 2)
    perm = jnp.where(src == partner, 1.0, 0.0).astype(BF16)
    hi = v.astype(BF16)
    lo = (v - hi.astype(F32)).astype(BF16)
    return jnp.dot(hi, perm, preferred_element_type=F32) + jnp.dot(lo, perm, preferred_element_type=F32)


def _swap_halves_roll(v):
    lane = lax.broadcasted_iota(jnp.int32, v.shape, 1)
    return jnp.where((lane % HEAD_DIM) < HEAD_DIM // 2, pltpu.roll(v, LANES - HEAD_DIM // 2, 1),
                     pltpu.roll(v, HEAD_DIM // 2, 1))


def _swap_halves_coarse(v):
    src = lax.broadcasted_iota(jnp.int32, (LANES, LANES), 0)
    dst = lax.broadcasted_iota(jnp.int32, (LANES, LANES), 1)
    partner = jnp.where((dst % HEAD_DIM) < HEAD_DIM // 2, dst + HEAD_DIM // 2, dst - HEAD_DIM // 2)
    perm = jnp.where(src == partner, 1.0, 0.0).astype(BF16)
    return jnp.dot(v.astype(BF16), perm, preferred_element_type=F32)


def _rope(v, cos, sin, sign=1.0, mxu=True, coarse=False):
    swap = (_swap_halves_coarse if coarse else _swap_halves) if mxu else _swap_halves_roll
    out = []
    for c in range(v.shape[1] // LANES):
        part = v[:, c * LANES:(c + 1) * LANES]
        out.append(part * cos + sign * (swap(part) * sin))
    return jnp.concatenate(out, axis=1) if len(out) > 1 else out[0]


def _half_mask(shape, half):
    lane = lax.broadcasted_iota(jnp.int32, shape, len(shape) - 1) % LANES
    return (lane < HEAD_DIM) if half == 0 else (lane >= HEAD_DIM)


def _dup_half(v, half):
    return jnp.where(_half_mask(v.shape, half), v, pltpu.roll(v, HEAD_DIM, 1))


def _fold_halves(v):
    return v + pltpu.roll(v, HEAD_DIM, 1)


def _pick_halves(lo_rows, hi_rows):
    return jnp.where(_half_mask(lo_rows.shape, 0), lo_rows, hi_rows)


def _stack_masked(v, pairs):
    parts = []
    for c in pairs:
        pair = v[:, c * LANES:(c + 1) * LANES]
        parts += [jnp.where(_half_mask(pair.shape, half), pair, 0.0) for half in (0, 1)]
    return jnp.concatenate(parts, axis=0)


def _stack_pair_cols(v, pairs):
    return jnp.concatenate([v[:, c * LANES + half * HEAD_DIM:c * LANES + half * HEAD_DIM + 1] for c in pairs for half in (0, 1)],
                           axis=0)


ATTN_UNITS = 16


def _class_rows(r):
    return [pl.ds(0, QBLOCK)] if r == 1 else [pl.ds(rho, QBLOCK, stride=r) for rho in range(r)]


def _band_mask(nrows, nk, blk, n_back, has_prev):
    qi = lax.broadcasted_iota(jnp.int32, (nrows, nk), 0) % QBLOCK
    ki = lax.broadcasted_iota(jnp.int32, (nrows, nk), 1)
    if has_prev:
        dist = qi + QBLOCK - ki
        return (dist >= 0) & (dist <= n_back) & ((ki >= QBLOCK) | (blk > 0))
    dist = qi - ki
    return (dist >= 0) & (dist <= n_back)


def _attn_fwd(q_arr, k_arr, v_arr, *, name, npair, gqa, q_col, k_col, v_col, nchunk, r, n_back, sinks=None):
    bsz, seq, _ = q_arr.shape
    rr = QBLOCK * r
    nblk = seq // rr
    qw = npair * LANES
    kw = LANES if gqa else qw
    has_prev = nblk > 1
    has_sink = sinks is not None
    scale = HEAD_DIM ** -0.5

    def body(*refs):
        refs = list(refs)
        q_ref, kc_ref, vc_ref = refs[:3]
        pos = 3
        if has_prev:
            kp_ref, vp_ref = refs[pos:pos + 2]
            pos += 2
        if has_sink:
            sink_ref = refs[pos]
            pos += 1
        o_ref, lse_ref = refs[pos:pos + 2]
        if r > 1:
            stage_o = refs[pos + 2]
        blk = pl.program_id(2)
        nk = (2 if has_prev else 1) * QBLOCK
        valid = _band_mask(QBLOCK, nk, blk, n_back, has_prev)
        per = npair // 2
        classes = _class_rows(r)
        step = max(1, ATTN_UNITS // (2 * npair))
        for first in range(0, len(classes), step):
            batch = classes[first:first + step]
            units = []
            for ci, rows in enumerate(batch):
                q = q_ref[0, rows, :] * scale
                k, v = kc_ref[0, rows, :], vc_ref[0, rows, :]
                if has_prev:
                    k = jnp.concatenate([kp_ref[0, rows, :], k], axis=0)
                    v = jnp.concatenate([vp_ref[0, rows, :], v], axis=0)
                if gqa:
                    kdup = [_dup_half(k, hk).astype(BF16) for hk in range(2)]
                    vdup = [_dup_half(v, hk) for hk in range(2)]
                for c in range(npair):
                    sl = slice(c * LANES, (c + 1) * LANES)
                    qc = q[:, sl]
                    kc, vc = (kdup[c // per], vdup[c // per]) if gqa else (k[:, sl].astype(BF16), v[:, sl])
                    for half in (0, 1):
                        qm = jnp.where(_half_mask(qc.shape, half), qc, 0.0).astype(BF16)
                        vm = jnp.where(_half_mask(vc.shape, half), vc, 0.0).astype(BF16)
                        s = lax.dot_general(qm, kc, _DIMS["nt"], preferred_element_type=F32)
                        units.append(dict(ci=ci, c=c, half=half, s=s, vm=vm, sk=sink_ref[2 * c + half] if has_sink else None))
            for u in units:
                s = jnp.where(valid, u["s"], NEG_INF)
                m = jnp.max(s, axis=1, keepdims=True)
                if has_sink:
                    m = jnp.maximum(m, u["sk"])
                p = jnp.exp(s - m)
                den = jnp.sum(p, axis=1, keepdims=True)
                if has_sink:
                    den = den + jnp.exp(u["sk"] - m)
                u.update(p=p.astype(BF16), den=den, lse=m + jnp.log(den))
            for u in units:
                u["o"] = jnp.dot(u["p"], u["vm"], preferred_element_type=F32) / u["den"]
            for ci, rows in enumerate(batch):
                outs, lses = [None] * npair, [None] * npair
                for u in units:
                    if u["ci"] != ci:
                        continue
                    c, o = u["c"], u["o"]
                    lse = jnp.broadcast_to(u["lse"], o.shape)
                    outs[c] = o if u["half"] == 0 else outs[c] + o
                    lses[c] = lse if u["half"] == 0 else _pick_halves(lses[c], lse)
                o_new = jnp.concatenate(outs, axis=1) if npair > 1 else outs[0]
                if r > 1:
                    stage_o[rows, :] = o_new
                else:
                    o_ref[0] = o_new.astype(BF16)
                lse_ref[0, rows, :] = jnp.concatenate(lses, axis=1) if npair > 1 else lses[0]
        if r > 1:
            o_ref[0] = stage_o[...].astype(BF16)

    def cur(width, col0):
        return pl.BlockSpec((1, rr, width), lambda b, c, i: (b, i, col0 + c))

    def prev(width, col0):
        return pl.BlockSpec((1, rr, width), lambda b, c, i: (b, jnp.maximum(i - 1, 0), col0 + c))

    in_specs = [cur(qw, q_col), cur(kw, k_col), cur(kw, v_col)]
    args = [q_arr, k_arr, v_arr]
    if has_prev:
        in_specs += [prev(kw, k_col), prev(kw, v_col)]
        args += [k_arr, v_arr]
    if has_sink:
        in_specs.append(pl.BlockSpec(memory_space=pltpu.SMEM))
        args.append(sinks)
    return pl.pallas_call(
        body,
        name=name,
        grid=(bsz, nchunk, nblk),
        in_specs=in_specs,
        out_specs=[pl.BlockSpec((1, rr, qw), lambda b, c, i: (b, i, c))] * 2,
        out_shape=[jax.ShapeDtypeStruct((bsz, seq, nchunk * qw), BF16), jax.ShapeDtypeStruct((bsz, seq, nchunk * qw), F32)],
        scratch_shapes=[pltpu.VMEM((rr, qw), F32)] if r > 1 else [],
        compiler_params=_params("parallel", "parallel", "parallel"),
    )(*args)


def _attn_bwd(q_arr, k_arr, v_arr, cos, sin, do, lse, dd, *, name, npair, gqa, q_col, k_col, v_col, nchunk, r, n_back,
              token=None):
    bsz, seq, _ = q_arr.shape
    rr = QBLOCK * r
    nblk = seq // rr
    qw = npair * LANES
    kw = LANES if gqa else qw
    has_next = nblk > 1
    has_token = token is not None
    staged = r > 1
    scale = HEAD_DIM ** -0.5

    def body(*refs):
        refs = list(refs)
        k_ref, v_ref, c_ref, s_ref = refs[:4]
        tile_refs = [refs[4:8]]
        pos = 8
        if has_next:
            tile_refs.append(refs[pos:pos + 4])
            pos += 4
        if has_token:
            pos += 1
        dq_ref, dk_ref, dv_ref = refs[pos:pos + 3]
        carry_ref, bcast_ref = refs[pos + 3:pos + 5]
        if staged:
            stage_q, stage_k, stage_v = refs[pos + 5:pos + 8]
        blk = pl.program_id(2)
        if has_next:
            @pl.when(blk == 0)
            def _():
                carry_ref[...] = jnp.zeros_like(carry_ref)

        nrows = (npair if gqa else 1) * QBLOCK
        qi = lax.broadcasted_iota(jnp.int32, (nrows, QBLOCK), 0) % QBLOCK
        ki = lax.broadcasted_iota(jnp.int32, (nrows, QBLOCK), 1)
        valids = [qi >= ki, (qi + QBLOCK - ki <= n_back) & (blk + 1 < nblk)]
        per = npair // 2
        ntile = len(tile_refs)
        cat = lambda parts: jnp.concatenate(parts, axis=1) if len(parts) > 1 else parts[0]
        classes = _class_rows(r)
        step = max(1, ATTN_UNITS // (ntile * (2 if gqa else 2 * npair)))
        def stat_cols(stat, slot):
            if gqa:
                return _stack_pair_cols(stat, list(range(slot * per, (slot + 1) * per)))
            col = slot * HEAD_DIM
            return stat[:, col:col + 1]

        nslot = 2 if gqa else 2 * npair
        if has_next:
            @pl.when(blk == 0)
            def _():
                for rows in classes:
                    for which, stat_ref in enumerate(tile_refs[0][2:4]):
                        stat = stat_ref[0, rows, :]
                        for slot in range(nslot):
                            bcast_ref[which, slot, rows if not gqa else slice(None), :] = jnp.broadcast_to(
                                stat_cols(stat, slot), (nrows, LANES))

        for first in range(0, len(classes), step):
            batch = classes[first:first + step]
            units = []
            for ci, rows in enumerate(batch):
                keep = slice(None) if gqa else rows
                tiles = [(q_ref[0, rows, :] * scale, do_ref[0, rows, :], l_ref[0, rows, :], d_ref[0, rows, :])
                         for q_ref, do_ref, l_ref, d_ref in tile_refs]

                def stats(t, slot, keep=keep, tiles=tiles):
                    if has_next and t == 0:
                        return bcast_ref[0, slot, keep, :], bcast_ref[1, slot, keep, :]
                    return tuple(jnp.broadcast_to(stat_cols(tiles[t][2 + w], slot), (nrows, LANES)) for w in range(2))

                k, v = k_ref[0, rows, :], v_ref[0, rows, :]
                if gqa:
                    for hk in range(2):
                        pairs = list(range(hk * per, (hk + 1) * per))
                        kd, vd = _dup_half(k, hk).astype(BF16), _dup_half(v, hk).astype(BF16)
                        for t, (q, do_, l_, d_) in enumerate(tiles):
                            lcol, dcol = stats(t, hk)
                            units.append(dict(ci=ci, t=t, hk=hk, slot=hk, keep=keep, pairs=pairs,
                                              qs=_stack_masked(q, pairs).astype(BF16),
                                              dos=_stack_masked(do_, pairs).astype(BF16), lcol=lcol, dcol=dcol,
                                              kmat=kd, vmat=vd, kdq=kd))
                else:
                    for c in range(npair):
                        sl = slice(c * LANES, (c + 1) * LANES)
                        kc, vcb = k[:, sl], v[:, sl].astype(BF16)
                        kcb = kc.astype(BF16)
                        for t, (q, do_, l_, d_) in enumerate(tiles):
                            for half in (0, 1):
                                hm = _half_mask(kc.shape, half)
                                lcol, dcol = stats(t, 2 * c + half)
                                units.append(dict(ci=ci, t=t, c=c, half=half, slot=2 * c + half, keep=keep,
                                                  qs=jnp.where(hm, q[:, sl], 0.0).astype(BF16),
                                                  dos=jnp.where(hm, do_[:, sl], 0.0).astype(BF16), lcol=lcol, dcol=dcol,
                                                  kmat=kcb, vmat=vcb, kdq=jnp.where(hm, kc, 0.0).astype(BF16)))
            for u in units:
                u["s"] = lax.dot_general(u["qs"], u["kmat"], _DIMS["nt"], preferred_element_type=F32)
                u["dp"] = lax.dot_general(u["dos"], u["vmat"], _DIMS["nt"], preferred_element_type=F32)
            for u in units:
                p = jnp.exp(jnp.where(valids[u["t"]], u["s"], NEG_INF) - u["lcol"])
                u["ds"] = (p * (u["dp"] + u["dcol"])).astype(BF16)
                u["p"] = p.astype(BF16)
            for u in units:
                u["dv"] = lax.dot_general(u["p"], u["dos"], _DIMS["tn"], preferred_element_type=F32)
                u["dk"] = lax.dot_general(u["ds"], u["qs"], _DIMS["tn"], preferred_element_type=F32)
                u["dq"] = jnp.dot(u["ds"], u["kdq"], preferred_element_type=F32) * scale
            for u in units:
                if u["t"] == 1:
                    bcast_ref[0, u["slot"], u["keep"], :] = u["lcol"]
                    bcast_ref[1, u["slot"], u["keep"], :] = u["dcol"]
            for ci, rows in enumerate(batch):
                mine = [u for u in units if u["ci"] == ci]
                dq = [[None] * npair for _ in range(ntile)]
                if gqa:
                    dk_out = dv_out = None
                    for hk in range(2):
                        us = [u for u in mine if u["hk"] == hk]
                        for u in us:
                            for i, c in enumerate(u["pairs"]):
                                dq[u["t"]][c] = _pick_halves(u["dq"][2 * i * QBLOCK:(2 * i + 1) * QBLOCK],
                                                             u["dq"][(2 * i + 1) * QBLOCK:(2 * i + 2) * QBLOCK])
                        dk_h = _fold_halves(functools.reduce(jnp.add, [u["dk"] for u in us]))
                        dv_h = _fold_halves(functools.reduce(jnp.add, [u["dv"] for u in us]))
                        dk_out = dk_h if hk == 0 else _pick_halves(dk_out, dk_h)
                        dv_out = dv_h if hk == 0 else _pick_halves(dv_out, dv_h)
                else:
                    dks, dvs = [], []
                    for c in range(npair):
                        us = [u for u in mine if u["c"] == c]
                        dks.append(functools.reduce(jnp.add, [u["dk"] for u in us]))
                        dvs.append(functools.reduce(jnp.add, [u["dv"] for u in us]))
                        for t in range(ntile):
                            dq[t][c] = functools.reduce(jnp.add, [u["dq"] for u in us if u["t"] == t])
                    dk_out, dv_out = cat(dks), cat(dvs)
                ck, sk_ = c_ref[0, rows, :], s_ref[0, rows, :]
                dk_new = _rope(dk_out, ck, sk_, sign=-1.0, mxu=gqa, coarse=True)
                dq_cur = cat(dq[0])
                if has_next:
                    dq_cur = dq_cur + carry_ref[rows, :]
                    carry_ref[rows, :] = cat(dq[1])
                dq_new = _rope(dq_cur, ck, sk_, sign=-1.0, mxu=gqa, coarse=True)
                if staged:
                    stage_q[rows, :], stage_k[rows, :], stage_v[rows, :] = dq_new, dk_new, dv_out
                else:
                    dq_ref[0], dk_ref[0], dv_ref[0] = dq_new.astype(BF16), dk_new.astype(BF16), dv_out.astype(BF16)
        if staged:
            dq_ref[0], dk_ref[0], dv_ref[0] = stage_q[...].astype(BF16), stage_k[...].astype(BF16), stage_v[...].astype(BF16)

    def at(width, col0, shift):
        return pl.BlockSpec((1, rr, width), lambda b, c, i: (b, jnp.minimum(i + shift, nblk - 1), col0 + c))

    in_specs = [at(kw, k_col, 0), at(kw, v_col, 0), pl.BlockSpec((1, rr, LANES), lambda b, c, i: (b, i, 0)),
                pl.BlockSpec((1, rr, LANES), lambda b, c, i: (b, i, 0))]
    args = [k_arr, v_arr, cos, sin]
    for shift in (0, 1) if has_next else (0,):
        in_specs += [at(qw, q_col, shift), at(qw, 0, shift), at(qw, 0, shift), at(qw, 0, shift)]
        args += [q_arr, do, lse, dd]
    if has_token:
        in_specs.append(pl.BlockSpec(token.shape, lambda b, c, i: (0, 0)))
        args.append(token)
    return pl.pallas_call(
        body,
        name=name,
        grid=(bsz, nchunk, nblk),
        in_specs=in_specs,
        out_specs=[pl.BlockSpec((1, rr, qw), lambda b, c, i: (b, i, c)),
                   pl.BlockSpec((1, rr, kw), lambda b, c, i: (b, i, c)),
                   pl.BlockSpec((1, rr, kw), lambda b, c, i: (b, i, c))],
        out_shape=[jax.ShapeDtypeStruct((bsz, seq, nchunk * qw), BF16),
                   jax.ShapeDtypeStruct((bsz, seq, nchunk * kw), BF16),
                   jax.ShapeDtypeStruct((bsz, seq, nchunk * kw), BF16)],
        scratch_shapes=[pltpu.VMEM((rr, qw) if has_next else (8, LANES), F32),
                        pltpu.VMEM((2, 2 if gqa else 2 * npair, npair * QBLOCK if gqa else rr, LANES) if has_next
                                   else (1, 1, 8, LANES), F32)] +
                       ([pltpu.VMEM((rr, qw), F32), pltpu.VMEM((rr, kw), F32), pltpu.VMEM((rr, kw), F32)] if staged else []),
        compiler_params=_params("parallel", "parallel", "arbitrary"),
    )(*args)


B_CHUNKS = {1: (4, 1), 4: (1, 4), 16: (1, 4)}


def _rope_tables(positions):
    half = HEAD_DIM // 2
    inv = ROPE_THETA ** (-jnp.arange(half, dtype=F32) / half)
    ang = positions.astype(F32)[..., None] * inv
    cos, sin = jnp.cos(ang), jnp.sin(ang)
    return jnp.concatenate([cos] * 4, axis=-1), jnp.concatenate([-sin, sin, -sin, sin], axis=-1)


def _layer_step(x, mod, tables, sinks, ln1_g, ln1_b, ln2_g, ln2_b, target, get_w_in, get_rest, hook):
    bsz, seq, d = x.shape
    ntok = bsz * seq
    flat = lambda v: v.reshape(ntok, v.shape[-1])
    unflat = lambda v: v.reshape(bsz, seq, v.shape[-1])
    cos, sin = tables
    mm = functools.partial(_matmul, tm=1024, tk=1024)
    scalar = lambda tok: 0.0 if tok is None else tok[0, 0]

    u1 = _modulate_in(x, mod)
    u1f = flat(u1)
    wint = get_w_in(u1)
    cosf, sinf = flat(cos), flat(sin)
    proj = functools.partial(_proj_rope, u1f, wint, cosf, sinf, tm=2048)
    qkvb = unflat(proj(n=4608, b_off=OFF_QKVB, rope_cols=3072, tn=256, name="proj_qkvb"))
    b_kws, os_, ls_ = [], [], []
    for g, (window, r) in enumerate(B_PATTERNS):
        npair, nch = B_CHUNKS[r]
        per = B_HEADS_PER_GROUP // (2 * npair)
        nsec = len(B_PATTERNS) * per
        kw_ = dict(npair=npair, gqa=False, q_col=g * per, k_col=nsec + g * per, v_col=2 * nsec + g * per, nchunk=nch, r=r,
                   n_back=window // r)
        b_kws.append(kw_)
        o_g, l_g = _attn_fwd(qkvb, qkvb, qkvb, name=f"attn_b{g}_fwd", **kw_)
        os_.append(o_g)
        ls_.append(l_g)
    tok = hook("projected", os_[-1])
    proj = functools.partial(_proj_rope, u1f, wint, cosf + scalar(tok), sinf, tm=2048)
    gab = unflat(proj(n=2048, b_off=OFF_GAB, rope_cols=0, tn=256, name="proj_gab", out_dtype=BF16))
    qa = unflat(proj(n=1024, b_off=OFF_QA, rope_cols=1024, tn=512, name="proj_qa"))
    kva = unflat(proj(n=256, b_off=OFF_KVA, rope_cols=128, tn=128, name="proj_kva"))
    a_kw = dict(npair=A_Q_HEADS // 2, gqa=True, q_col=0, k_col=0, v_col=1, nchunk=1, r=1, n_back=A_WINDOW - 1)
    after_gab = jnp.minimum(jnp.abs(gab[0, 0, 0].astype(F32)), 0.0)
    oa, lse_a = _attn_fwd(qa, kva, kva, name="attn_a_fwd", sinks=sinks.reshape(A_Q_HEADS) + after_gab, **a_kw)
    rest = get_rest(oa)
    wba, wbbt, wo, wgut, wd = (rest[n] for n in ("w_branch_a", "w_branch_b", "w_o", "w_gate_up", "w_down"))
    ya = unflat(mm(flat(oa), wba, mode="nn", out_dtype=BF16, tn=512, name="branch_a"))
    ybf, mergedf, obf = _merge_branch_b_gate([flat(t) for t in os_], [flat(t) for t in ls_], wbbt, flat(gab), flat(ya))
    xf = flat(x)
    y1f, h1f, u2f = _wo_ln1(mergedf, wo, xf, mod, ln1_g, ln1_b, seq)
    wgut_i = _interleave_gate_up(wgut)
    hf, af = _gate_up_silu(u2f, wgut_i)

    dy2f, dh1af, acc2 = _down_ln2_loss_bwd(af, wd, h1f, mod, ln2_g, ln2_b, flat(target), seq)
    g_wd = _matmul(af, dy2f, mode="tn", out_dtype=BF16, tm=256, tn=1024, tk=ntok, name="down_wgrad")
    dhf = _down_dgrad_silu_bwd(dy2f, wd, hf)
    g_wgut = _interleave_gate_up(_matmul(dhf, u2f, mode="tn", out_dtype=BF16, tm=256, tn=1024, tk=ntok, name="gate_up_wgrad"))
    dy1f, dxaf, acc1 = _gate_up_dgrad_ln1_bwd(dhf, wgut_i, dh1af, xf, y1f, mod, ln1_g, ln1_b, seq)
    g_wo = _matmul(mergedf, dy1f, mode="tn", out_dtype=BF16, tm=256, tn=1024, tk=ntok, name="w_o_wgrad")
    dyaf, dybf, dgaf, dgbf = _wo_dgrad_gate_bwd(dy1f, wo, flat(gab), flat(ya), ybf)
    g_wba = _matmul(flat(oa), dyaf, mode="tn", out_dtype=BF16, tm=256, tn=1024, tk=ntok, name="branch_a_wgrad")
    g_wbbt = _matmul(dybf, obf, mode="tn", out_dtype=BF16, tm=256, tn=512, tk=ntok, name="branch_b_wgrad")
    tok = hook("grads_rest", dict(w_branch_a=g_wba, w_branch_b=g_wbbt, w_o=g_wo, w_gate_up=g_wgut, w_down=g_wd))

    sinks_exp = jnp.repeat(sinks.reshape(1, A_Q_HEADS), HEAD_DIM, axis=1) + scalar(tok)
    doa, dd_a, acc_s = _branch_a_dgrad_delta(dyaf, wba, flat(oa), flat(lse_a), sinks_exp, seq)
    doa, dd_a = unflat(doa), unflat(dd_a)
    tok = hook("delta_done", dd_a)
    dqa, dka, dva = _attn_bwd(qa, kva, kva, cos, sin, doa, lse_a, dd_a, name="attn_a_bwd", token=tok, **a_kw)
    merged_bwd = [unflat(t) for t in _branch_b_dgrad_merge_bwd(dybf, wbbt, [flat(t) for t in os_], [flat(t) for t in ls_])]
    dqs, dks, dvs = [], [], []
    for g in range(len(B_PATTERNS)):
        dq_g, dk_g, dv_g = _attn_bwd(qkvb, qkvb, qkvb, cos, sin, merged_bwd[g], ls_[g], merged_bwd[3 + g],
                                     name=f"attn_b{g}_bwd", **b_kws[g])
        dqs.append(dq_g)
        dks.append(dk_g)
        dvs.append(dv_g)
    dproj = jnp.concatenate([t.astype(BF16) for t in [dqa, dka, dva] + dqs + dks + dvs] + [unflat(dgaf), unflat(dgbf)], axis=-1)
    dprojf = flat(dproj)
    g_wint = _matmul(dprojf, u1f, mode="tn", out_dtype=BF16, tm=256, tn=1024, tk=ntok, name="w_in_wgrad")
    tok = hook("grads_w_in", dict(w_in=g_wint))
    grad_x, acc0 = _w_in_dgrad_grad_x(dprojf, wint, dxaf, xf, mod, seq, tok)
    grad_x = unflat(grad_x)
    tok = hook("dgrad_done", grad_x)

    loss_part = jnp.sum(acc2[:, 3, 0])
    dmod = jnp.stack([acc0[:, 1], acc0[:, 0], acc1[:, 2], acc1[:, 4], acc1[:, 3], acc2[:, 2]], axis=1)
    small = jnp.stack([acc1[:, 0].sum(0), acc1[:, 1].sum(0), acc2[:, 0].sum(0), acc2[:, 1].sum(0), acc_s[:, 0].sum(0)])
    small = small + scalar(tok)
    return loss_part, grad_x, dmod, small


CHIP_FLIPS = (2, 4, 6)


def _my_place():
    return lax.axis_index("x"), lax.axis_index("y"), lax.axis_index("c")


def _flip(place, k):
    px, py, pc = place
    return (1 - px if k & 4 else px, 1 - py if k & 2 else py, 1 - pc if k & 1 else pc)


def _index(place):
    return 4 * place[0] + 2 * place[1] + place[2]


def _gather_small(v, name):
    rows, cols = v.shape

    def body(v_ref, out_ref, send_sems, recv_sems):
        me = _my_place()
        out_ref[_index(me)] = v_ref[...]
        copies = []
        for k in range(1, N_DEV):
            copies.append(pltpu.make_async_remote_copy(
                src_ref=v_ref, dst_ref=out_ref.at[_index(me)], send_sem=send_sems.at[k - 1], recv_sem=recv_sems.at[k - 1],
                device_id=_flip(me, k), device_id_type=MESH))
        for cp in copies:
            cp.start()
        for k in range(1, N_DEV):
            pltpu.make_async_remote_copy(
                src_ref=v_ref, dst_ref=out_ref.at[_index(_flip(me, k))], send_sem=send_sems.at[k - 1],
                recv_sem=recv_sems.at[k - 1], device_id=_flip(me, k), device_id_type=MESH).wait_recv()
        for cp in copies:
            cp.wait_send()

    return pl.pallas_call(
        body,
        name=name,
        out_shape=jax.ShapeDtypeStruct((N_DEV, rows, cols), v.dtype),
        in_specs=[pl.BlockSpec(memory_space=pltpu.VMEM)],
        out_specs=pl.BlockSpec(memory_space=pltpu.VMEM),
        scratch_shapes=[pltpu.SemaphoreType.DMA((N_DEV - 1,)), pltpu.SemaphoreType.DMA((N_DEV - 1,))],
        compiler_params=pltpu.CompilerParams(vmem_limit_bytes=VMEM_LIMIT_BYTES),
    )(v)


_HBM = pl.BlockSpec(memory_space=pltpu.HBM)
_SEM = pl.BlockSpec(memory_space=pltpu.SEMAPHORE)
_EFFECT = pltpu.SideEffectType.DATAFLOW_SIDE_EFFECTING


def _remote(src, dst, send_sems, recv_sems, j, to):
    return pltpu.make_async_remote_copy(src_ref=src, dst_ref=dst, send_sem=send_sems.at[j], recv_sem=recv_sems.at[j],
                                        device_id=to, device_id_type=MESH)


def _copies_start(name, bufs, make_copies, nsem):
    nbuf = len(bufs)

    def body(*refs):
        for cp in make_copies(refs[:nbuf], refs[nbuf], refs[nbuf + 1]):
            cp.start()
        refs[-1][...] = jnp.zeros_like(refs[-1])

    sems = pltpu.SemaphoreType.DMA((nsem,))
    res = pl.pallas_call(
        body, name=name,
        out_shape=(sems, sems, *[pltpu.HBM(v.shape, v.dtype) for v in bufs], jax.ShapeDtypeStruct((8, LANES), F32)),
        in_specs=(_HBM,) * nbuf, out_specs=(_SEM, _SEM) + (_HBM,) * nbuf + (pl.BlockSpec(memory_space=pltpu.VMEM),),
        input_output_aliases={i: 2 + i for i in range(nbuf)},
        compiler_params=pltpu.CompilerParams(has_side_effects=_EFFECT),
    )(*[pltpu.with_memory_space_constraint(v, pltpu.HBM) for v in bufs])
    return res[0], res[1], list(res[2:2 + nbuf]), res[-1]


def _copies_wait(name, started, make_copies, after):
    send_sems, recv_sems, bufs, _ = started
    nbuf = len(bufs)

    def body(*refs):
        for cp in make_copies(refs[:nbuf], refs[nbuf], refs[nbuf + 1]):
            cp.wait_send()
            cp.wait_recv()

    return list(pl.pallas_call(
        body, name=name,
        out_shape=tuple(pltpu.HBM(v.shape, v.dtype) for v in bufs),
        in_specs=(_HBM,) * nbuf + (_SEM, _SEM, pl.BlockSpec(memory_space=pl.ANY)), out_specs=(_HBM,) * nbuf,
        input_output_aliases={i: i for i in range(nbuf)},
        compiler_params=pltpu.CompilerParams(has_side_effects=_EFFECT),
    )(*bufs, send_sems, recv_sems, after))


def _to_sibling_copies(refs, send_sems, recv_sems):
    src_ref, land_ref = refs
    me = _my_place()
    return [_remote(src_ref.at[q, 1 - me[2]], land_ref.at[q], send_sems, recv_sems, q, _flip(me, 1)) for q in range(4)]


def _to_chips_copies(refs, send_sems, recv_sems):
    src_ref, land_ref = refs
    me = _my_place()
    copies = []
    for j, k in enumerate(CHIP_FLIPS):
        to = _flip(me, k)
        copies.append(_remote(src_ref.at[2 * to[0] + to[1]], land_ref.at[j], send_sems, recv_sems, j, to))
    return copies


class _Gather:
    def __init__(self, name, blocks):
        self.name, self.n = name, len(blocks)
        at_me = (_index(_my_place()), 0, 0)
        lands = [lax.dynamic_update_slice(lax.empty((N_DEV,) + v.shape, v.dtype), v[None], at_me) for v in blocks]
        self.first = _copies_start(name + "_start", list(blocks) + lands, self._first_copies, 4 * self.n)
        self.token = self.first[3]

    def _first_copies(self, refs, send_sems, recv_sems):
        me = _my_place()
        return [_remote(refs[w], refs[self.n + w].at[_index(me)], send_sems, recv_sems, 4 * w + j, _flip(me, k))
                for w in range(self.n) for j, k in enumerate((1,) + CHIP_FLIPS)]

    def _pass_copies(self, refs, send_sems, recv_sems):
        me = _my_place()
        copies = []
        for w, land in enumerate(refs):
            for j, k in enumerate(CHIP_FLIPS):
                slot = land.at[_index(_flip(me, k))]
                copies.append(_remote(slot, slot, send_sems, recv_sems, 3 * w + j, _flip(me, 1)))
        return copies

    def pass_on(self, after):
        lands = _copies_wait(self.name + "_wait", self.first, self._first_copies, after)[self.n:]
        self.second = _copies_start(self.name + "_pass_start", lands, self._pass_copies, 3 * self.n)
        return self.second[3]

    def finish(self, after):
        return _copies_wait(self.name + "_pass_wait", self.second, self._pass_copies, after)


SUM_SPLIT = 2


def _sum_pairs(parts, theirs):
    nchip, _, rows, cols = parts.shape
    tile = rows // SUM_SPLIT

    def body(c_ref, a_ref, b_ref, o_ref):
        o_ref[...] = (a_ref[0].astype(F32) + b_ref[...].astype(F32)).astype(BF16)

    spec = pl.BlockSpec((1, tile, cols), lambda q, t, c_ref: (q, t, 0))
    grid_spec = pltpu.PrefetchScalarGridSpec(
        num_scalar_prefetch=1, grid=(nchip, SUM_SPLIT),
        in_specs=[pl.BlockSpec((1, 1, tile, cols), lambda q, t, c_ref: (q, c_ref[0], t, 0)), spec], out_specs=spec)
    return pl.pallas_call(body, name="grad_sum_sibling", grid_spec=grid_spec,
                          out_shape=jax.ShapeDtypeStruct((nchip, rows, cols), BF16),
                          compiler_params=_params("parallel", "parallel"))(lax.axis_index("c").reshape(1), parts, theirs)


def _sum_final(chip_sum, got):
    _, rows, cols = chip_sum.shape
    tile = rows // SUM_SPLIT

    def body(q_ref, a_ref, g_ref, o_ref):
        o_ref[...] = ((a_ref[0].astype(F32) + g_ref[0].astype(F32)) + g_ref[1].astype(F32)) + g_ref[2].astype(F32)

    grid_spec = pltpu.PrefetchScalarGridSpec(
        num_scalar_prefetch=1, grid=(SUM_SPLIT,),
        in_specs=[pl.BlockSpec((1, tile, cols), lambda t, q_ref: (q_ref[0], t, 0)),
                  pl.BlockSpec((3, tile, cols), lambda t, q_ref: (0, t, 0))],
        out_specs=pl.BlockSpec((tile, cols), lambda t, q_ref: (t, 0)))
    my_chip = (2 * lax.axis_index("x") + lax.axis_index("y")).reshape(1)
    return pl.pallas_call(body, name="grad_sum_chips", grid_spec=grid_spec, out_shape=jax.ShapeDtypeStruct((rows, cols), F32),
                          compiler_params=_params("parallel"))(my_chip, chip_sum, got)


class _ReduceScatter:
    def __init__(self, name, slabs):
        self.name, self.rows = name, slabs.shape[1]
        parts = slabs.reshape(4, 2, self.rows, D_MODEL)
        self.first = _copies_start(name + "_sibling_start", [parts, lax.empty((4, self.rows, D_MODEL), slabs.dtype)],
                                   _to_sibling_copies, 4)
        self.token = self.first[3]

    def between_chips(self, after):
        parts, theirs = _copies_wait(self.name + "_sibling_wait", self.first, _to_sibling_copies, after)
        chip_sum = _sum_pairs(parts, theirs)
        self.second = _copies_start(self.name + "_chips_start", [chip_sum, lax.empty((3, self.rows, D_MODEL), chip_sum.dtype)],
                                    _to_chips_copies, 3)
        return self.second[3]

    def finish(self, after):
        chip_sum, got = _copies_wait(self.name + "_chips_wait", self.second, _to_chips_copies, after)
        return _sum_final(chip_sum, got)


def _ada_fwd(c_all, w, b):
    nb, _ = c_all.shape
    ncol = w.shape[1]

    def body(c_ref, w_ref, b_ref, o_ref):
        c = c_ref[...]
        act = (c * _sigmoid(c)).astype(BF16)
        o_ref[...] = jnp.dot(act, w_ref[...].astype(BF16), preferred_element_type=F32) + b_ref[...]

    return pl.pallas_call(body, name="ada_fwd", out_shape=jax.ShapeDtypeStruct((nb, ncol), F32),
                          compiler_params=pltpu.CompilerParams(vmem_limit_bytes=VMEM_LIMIT_BYTES))(c_all, w, b)


def _ada_wgrad(c_all_t, dmod_cols):
    d, nb = c_all_t.shape
    ncol = dmod_cols.shape[1]

    def body(ct_ref, dm_ref, o_ref):
        ct = ct_ref[...]
        act = (ct * _sigmoid(ct)).astype(BF16).astype(F32)
        dm = dm_ref[...].astype(BF16).astype(F32)
        acc = act[:, 0:1] * dm[0:1, :]
        for i in range(1, nb):
            acc = acc + act[:, i:i + 1] * dm[i:i + 1, :]
        o_ref[...] = acc

    return pl.pallas_call(body, name="ada_wgrad", out_shape=jax.ShapeDtypeStruct((d, ncol), F32),
                          compiler_params=pltpu.CompilerParams(vmem_limit_bytes=VMEM_LIMIT_BYTES))(c_all_t, dmod_cols)


SMALL_ROWS = 24


def _reduce_small(gathered):
    def body(g_ref, o_ref):
        acc = g_ref[0]
        for dev in range(1, N_DEV):
            acc = acc + g_ref[dev]
        o_ref[...] = acc

    return pl.pallas_call(body, name="reduce_small", out_shape=jax.ShapeDtypeStruct(gathered.shape[1:], F32))(gathered)


def _adamw_math(w, g, m, v):
    nm = ADAM_B1 * m + (1.0 - ADAM_B1) * g
    nv = ADAM_B2 * v + (1.0 - ADAM_B2) * (g * g)
    bc1 = 1.0 - ADAM_B1 ** ADAM_STEP
    bc2 = 1.0 - ADAM_B2 ** ADAM_STEP
    return -ADAM_LR * ((nm / bc1) / (jnp.sqrt(nv / bc2) + ADAM_EPS) + ADAM_WD * w), nm, nv


def _adamw_small(ws, gs, ms, vs, name):
    n = len(ws)

    def body(*refs):
        for i in range(n):
            res = _adamw_math(*(refs[k * n + i][...] for k in range(4)))
            for k in range(3):
                refs[(4 + k) * n + i][...] = res[k]

    shapes = [jax.ShapeDtypeStruct(w.shape, F32) for w in ws]
    res = pl.pallas_call(body, name=name, out_shape=shapes * 3)(*ws, *gs, *ms, *vs)
    return [(res[i], res[n + i], res[2 * n + i]) for i in range(n)]


def _adamw(w, g, m, v, name):
    rows, cols = w.shape
    tile = rows
    for cand in range(min(rows // 2, 512) // 8 * 8, 7, -8):
        if rows % cand == 0:
            tile = cand
            break
    spec = pl.BlockSpec((tile, cols), lambda t: (t, 0))

    def body(w_ref, g_ref, m_ref, v_ref, d_ref, nm_ref, nv_ref):
        d_ref[...], nm_ref[...], nv_ref[...] = _adamw_math(w_ref[...], g_ref[...], m_ref[...], v_ref[...])

    shp = jax.ShapeDtypeStruct((rows, cols), F32)
    return pl.pallas_call(body, name=name, grid=(rows // tile,), in_specs=[spec] * 4, out_specs=[spec] * 3, out_shape=[shp] * 3,
                          compiler_params=_params("parallel"))(w, g, m, v)


_WEIGHTS = ("w_ada", "b_ada", "w_in", "sinks", "w_branch_a", "w_branch_b", "w_o", "ln1_g", "ln1_b", "w_gate_up", "w_down",
            "ln2_g", "ln2_b")
_TRANSPOSED = ("w_in", "w_branch_b", "w_gate_up")


def _pack_shard(name, w):
    w = w.astype(BF16)
    if name in _TRANSPOSED:
        w = w.T
    return w.reshape(-1, D_MODEL)


def _unpack_full(name, slab):
    if name == "w_branch_b":
        return slab.reshape(N_DEV * 128, 512)
    return slab.reshape(-1, D_MODEL)


def _unpack_group(group, gathered):
    return {n: _unpack_full(n, slab) for (n, _), slab in zip(group, gathered)}


def _unpack_grads(group, g_packed):
    g_w, off = {}, 0
    for n, r in group:
        part = g_packed[off:off + r]
        off += r
        g_w[n] = part.reshape(128, 512) if n == "w_branch_b" else part
    return g_w


def kernel(x, c, positions, w_ada, b_ada, w_in, sinks, w_branch_a, w_branch_b, w_o, ln1_g, ln1_b, w_gate_up, w_down, ln2_g, ln2_b, loss_target, m_w_ada, m_b_ada, m_w_in, m_sinks, m_w_branch_a, m_w_branch_b, m_w_o, m_ln1_g, m_ln1_b, m_w_gate_up, m_w_down, m_ln2_g, m_ln2_b, v_w_ada, v_b_ada, v_w_in, v_sinks, v_w_branch_a, v_w_branch_b, v_w_o, v_ln1_g, v_ln1_b, v_w_gate_up, v_w_down, v_ln2_g, v_ln2_b):
    weights = dict(w_ada=w_ada, b_ada=b_ada, w_in=w_in, sinks=sinks, w_branch_a=w_branch_a, w_branch_b=w_branch_b, w_o=w_o,
                   ln1_g=ln1_g, ln1_b=ln1_b, w_gate_up=w_gate_up, w_down=w_down, ln2_g=ln2_g, ln2_b=ln2_b)
    m_in = dict(w_ada=m_w_ada, b_ada=m_b_ada, w_in=m_w_in, sinks=m_sinks, w_branch_a=m_w_branch_a, w_branch_b=m_w_branch_b,
                w_o=m_w_o, ln1_g=m_ln1_g, ln1_b=m_ln1_b, w_gate_up=m_w_gate_up, w_down=m_w_down, ln2_g=m_ln2_g, ln2_b=m_ln2_b)
    v_in = dict(w_ada=v_w_ada, b_ada=v_b_ada, w_in=v_w_in, sinks=v_sinks, w_branch_a=v_w_branch_a, w_branch_b=v_w_branch_b,
                w_o=v_w_o, ln1_g=v_ln1_g, ln1_b=v_ln1_b, w_gate_up=v_w_gate_up, w_down=v_w_down, ln2_g=v_ln2_g, ln2_b=v_ln2_b)
    bsz = x.shape[0]
    me = _index(_my_place())
    ada_cols = w_ada.shape[2]
    outs = {}

    def adamw(n, g):
        w2, m2, v2 = (t[n][0] if t[n].ndim == 3 else t[n] for t in (weights, m_in, v_in))
        shape = weights[n].shape
        if n in _TRANSPOSED:
            dlt, nm, nv = _adamw(w2.T, g, m2.T, v2.T, "adamw_" + n)
            outs[n] = tuple(t.T.reshape(shape) for t in (g, dlt, nm, nv))
        else:
            dlt, nm, nv = _adamw(w2, g, m2, v2, "adamw_" + n)
            outs[n] = tuple(t.reshape(shape) for t in (g, dlt, nm, nv))
        return nv

    packed_in = [_pack_shard(n, weights[n][0]) for n, _ in GROUP_IN]
    packed_rest = [_pack_shard(n, weights[n][0]) for n, _ in GROUP_REST]
    c_all = _gather_small(jnp.pad(c, ((0, 8 - bsz), (0, 0))), "gather_c")[:, :bsz].reshape(N_DEV * bsz, D_MODEL)
    gather_in = _Gather("gather_w_in", lax.optimization_barrier((packed_in, c_all))[0])
    b_cols = lax.dynamic_slice_in_dim(b_ada, me * ada_cols, ada_cols, axis=1)
    mod_cols = _ada_fwd(c_all, w_ada[0], b_cols + gather_in.token[0, 0])
    tables = _rope_tables(positions)
    mod_cols, tables, packed_rest = lax.optimization_barrier((mod_cols, tables, packed_rest))
    mod_all = _gather_small(mod_cols, "gather_mod").transpose(1, 0, 2).reshape(N_DEV * bsz, 6, D_MODEL)
    gather_rest = _Gather("gather_rest", lax.optimization_barrier((packed_rest, mod_all))[0])
    mod = jnp.pad(lax.dynamic_slice_in_dim(mod_all, me * bsz, bsz, axis=0), ((0, 0), (0, 2), (0, 0)))
    mod = mod + gather_rest.token[0, 0]
    mod = mod + gather_in.pass_on(mod)[0, 0]

    scatters = {}

    def get_w_in(after):
        return _unpack_group(GROUP_IN, gather_in.finish(after))["w_in"]

    def get_rest(after):
        return _unpack_group(GROUP_REST, gather_rest.finish(after))

    def pack_grads(group, grads):
        return jnp.concatenate([grads[n].reshape(N_DEV, r, D_MODEL) for n, r in group], axis=1)

    def hook(point, value):
        if point == "projected":
            return gather_rest.pass_on(value)
        if point == "grads_rest":
            scatters["rest"] = _ReduceScatter("scatter_rest", pack_grads(GROUP_REST, value))
            return scatters["rest"].token
        if point == "delta_done":
            return scatters["rest"].between_chips(value)
        if point == "grads_w_in":
            scatters["in"] = _ReduceScatter("scatter_w_in", pack_grads(GROUP_IN, value))
            return scatters["in"].token
        if point == "dgrad_done":
            return None
        raise ValueError(point)

    loss_part, grad_x, dmod, small = _layer_step(x, mod, tables, sinks[0], ln1_g, ln1_b, ln2_g, ln2_b, loss_target,
                                                 get_w_in, get_rest, hook)

    rows = jnp.concatenate([dmod.reshape(bsz * 6, D_MODEL), small, jnp.full((1, D_MODEL), loss_part, F32),
                            jnp.zeros((SMALL_ROWS - bsz * 6 - 6, D_MODEL), F32)], axis=0)
    small_all = _gather_small(rows, "gather_small")
    small_all = small_all + scatters["in"].between_chips(small_all)[0, 0]
    sums = _reduce_small(small_all)
    loss = sums[bsz * 6 + 5, 0]
    dmod_all = small_all[:, :bsz * 6].reshape(N_DEV * bsz, 6 * D_MODEL)
    small_g = {"b_ada": functools.reduce(jnp.add, [sums[6 * i:6 * i + 6] for i in range(bsz)]).reshape(1, 6 * D_MODEL),
               "sinks": sums[bsz * 6 + 4][::HEAD_DIM][None]}
    small_g.update({n: sums[bsz * 6 + i][None] for i, n in enumerate(("ln1_g", "ln1_b", "ln2_g", "ln2_b"))})
    names = list(small_g)
    for n, (dlt, nm, nv) in zip(names, _adamw_small([weights[n] for n in names], [small_g[n] for n in names],
                                                     [m_in[n] for n in names], [v_in[n] for n in names], "adamw_small")):
        outs[n] = (small_g[n], dlt, nm, nv)
    dmod_cols = lax.dynamic_slice_in_dim(dmod_all, me * ada_cols, ada_cols, axis=1)
    last = adamw("w_ada", _ada_wgrad(c_all.T, dmod_cols))
    for n, g in _unpack_grads(GROUP_REST, scatters["rest"].finish(last)).items():
        adamw(n, g)
    done = lax.optimization_barrier(tuple(outs[n][3] for n in outs))
    for n, g in _unpack_grads(GROUP_IN, scatters["in"].finish(done[0])).items():
        adamw(n, g)

    return (loss, grad_x, *[outs[n][0] for n in _WEIGHTS], *[outs[n][1] for n in _WEIGHTS], *[outs[n][2] for n in _WEIGHTS],
            *[outs[n][3] for n in _WEIGHTS])
```

```python
import functools

import jax
import jax.numpy as jnp
from jax import lax
from jax.experimental import pallas as pl
from jax.experimental.pallas import tpu as pltpu

F32 = jnp.float32
BF16 = jnp.bfloat16

D_MODEL = 1024
HEAD_DIM = 64
A_Q_HEADS = 16
A_WINDOW = 128
B_PATTERNS = ((128, 1), (512, 4), (2048, 16))
B_HEADS_PER_GROUP = 8
D_FF = 2816
QBLOCK = 128
ROPE_THETA = 10000.0
LN_EPS = 1e-5
DEEPNORM_ALPHA = 2.0 ** 0.25
NEG_INF = -1e30
ADAM_LR, ADAM_B1, ADAM_B2, ADAM_EPS, ADAM_WD, ADAM_STEP = 0.001, 0.9, 0.999, 1e-08, 0.01, 10

N_DEV = 8
LANES = 128
VMEM_LIMIT_BYTES = 56 * 1024 * 1024
MESH = pl.DeviceIdType.MESH

OFF_QA, OFF_KVA, OFF_QKVB, OFF_GAB = 0, 1024, 1280, 5888
GROUP_IN = (("w_in", 992),)
GROUP_REST = (("w_branch_a", 128), ("w_branch_b", 64), ("w_o", 128), ("w_gate_up", 704), ("w_down", 352))


def _params(*sem):
    return pltpu.CompilerParams(dimension_semantics=sem, vmem_limit_bytes=VMEM_LIMIT_BYTES)


def _sigmoid(x):
    return 1.0 / (1.0 + jnp.exp(-x))


_DIMS = {"nn": (((1,), (0,)), ((), ())), "nt": (((1,), (1,)), ((), ())), "tn": (((0,), (0,)), ((), ()))}


def _matmul(a, b, *, mode, tm, tn, tk, name, out_dtype=None, n=None, b_off=0, token=None, ins=(), outs=None, epilogue=None,
            lhs_fn=None):
    if mode == "nn":
        (m, k), nn_ = a.shape, b.shape[1]
    elif mode == "nt":
        (m, k), nn_ = a.shape, (b.shape[0] if n is None else n)
    else:
        (k, m), nn_ = a.shape, b.shape[1]
    assert m % tm == 0 and nn_ % tn == 0 and k % tk == 0 and b_off % tn == 0, (name, m, nn_, k)
    nk = k // tk
    joff = b_off // tn
    if mode == "nn":
        a_spec = pl.BlockSpec((tm, tk), lambda i, j, kk: (i, kk))
        b_spec = pl.BlockSpec((tk, tn), lambda i, j, kk: (kk, j))
    elif mode == "nt":
        a_spec = pl.BlockSpec((tm, tk), lambda i, j, kk: (i, kk))
        b_spec = pl.BlockSpec((tn, tk), lambda i, j, kk: (j + joff, kk))
    else:
        a_spec = pl.BlockSpec((tk, tm), lambda i, j, kk: (kk, i))
        b_spec = pl.BlockSpec((tk, tn), lambda i, j, kk: (kk, j))
    dims = _DIMS[mode]
    has_token = token is not None
    plain = epilogue is None
    if plain:
        outs = [(jax.ShapeDtypeStruct((m, nn_), out_dtype), (tm, tn), lambda i, j: (i, j))]

        def epilogue(acc, i, j, in_refs, out_refs):
            out_refs[0][...] = acc.astype(out_refs[0].dtype)

    nin = len(ins)
    nscratch = 1 if lhs_fn is None else 2
    assert lhs_fn is None or nk == 1

    def body(*refs):
        a_ref, b_ref = refs[:2]
        in_refs = refs[2:2 + nin]
        out_refs = refs[2 + nin + has_token:-nscratch]
        acc_ref = refs[-nscratch]
        kk = pl.program_id(2)
        if lhs_fn is None:
            lhs = a_ref[...].astype(BF16)
        else:
            lhs_ref = refs[-1]

            @pl.when(pl.program_id(1) == 0)
            def _():
                lhs_ref[...] = lhs_fn(a_ref, in_refs, out_refs)

            lhs = lhs_ref[...]
        part = lax.dot_general(lhs, b_ref[...].astype(BF16), dims, preferred_element_type=F32)

        def finish(acc):
            epilogue(acc, pl.program_id(0), pl.program_id(1), in_refs, out_refs)

        if nk == 1:
            finish(part)
        else:
            @pl.when(kk == 0)
            def _():
                acc_ref[...] = part

            @pl.when(kk > 0)
            def _():
                acc_ref[...] += part

            @pl.when(kk == nk - 1)
            def _():
                finish(acc_ref[...])

    def spec(block, index):
        return pl.BlockSpec(block, lambda i, j, kk: index(i, j))

    in_specs, args = [a_spec, b_spec], [a, b]
    for arr, block, index in ins:
        in_specs.append(spec(block, index))
        args.append(arr)
    if has_token:
        in_specs.append(pl.BlockSpec(token.shape, lambda i, j, kk: (0, 0)))
        args.append(token)
    res = pl.pallas_call(
        body,
        name=name,
        grid=(m // tm, nn_ // tn, nk),
        in_specs=in_specs,
        out_specs=[spec(block, index) for _, block, index in outs],
        out_shape=[shape for shape, _, _ in outs],
        scratch_shapes=[pltpu.VMEM((tm, tn) if nk > 1 else (8, LANES), F32)] + ([] if lhs_fn is None else [pltpu.VMEM((tm, tk), BF16)]),
        compiler_params=_params("arbitrary", "arbitrary", "arbitrary"),
    )(*args)
    return res[0] if plain else res


def _proj_rope(a, bt, cos, sin, *, n, b_off, rope_cols, tm, tn, name, out_dtype=F32):
    m, k = a.shape
    assert m % tm == 0 and n % tn == 0 and b_off % tn == 0 and rope_cols % tn == 0, name
    joff = b_off // tn
    nrope = rope_cols // tn

    def body(a_ref, b_ref, c_ref, s_ref, o_ref):
        acc = lax.dot_general(a_ref[...], b_ref[...], _DIMS["nt"], preferred_element_type=F32)
        j = pl.program_id(1)

        @pl.when(j < nrope)
        def _():
            o_ref[...] = _rope(acc, c_ref[...], s_ref[...], coarse=True).astype(o_ref.dtype)

        @pl.when(j >= nrope)
        def _():
            o_ref[...] = acc.astype(o_ref.dtype)

    table = pl.BlockSpec((tm, LANES), lambda i, j: (i, 0))
    return pl.pallas_call(
        body,
        name=name,
        grid=(m // tm, n // tn),
        in_specs=[pl.BlockSpec((tm, k), lambda i, j: (i, 0)), pl.BlockSpec((tn, k), lambda i, j: (j + joff, 0)), table, table],
        out_specs=pl.BlockSpec((tm, tn), lambda i, j: (i, j)),
        out_shape=jax.ShapeDtypeStruct((m, n), out_dtype),
        compiler_params=_params("parallel", "parallel"),
    )(a, bt, cos, sin)


ROW_TILE = 256


def _rows(width, col=0):
    return pl.BlockSpec((1, ROW_TILE, width), lambda b, t: (b, t, col))


def _per_batch(nrows, width):
    return pl.BlockSpec((1, nrows, width), lambda b, t: (b, 0, 0))


def _row_call(body, name, bsz, seq, in_specs, out_specs, out_shape, accumulates=False):
    return pl.pallas_call(
        body,
        name=name,
        grid=(bsz, seq // ROW_TILE),
        in_specs=in_specs,
        out_specs=out_specs,
        out_shape=out_shape,
        compiler_params=_params("parallel", "arbitrary" if accumulates else "parallel"),
    )


def _acc_rows(acc_ref, first, rows):
    @pl.when(first)
    def _():
        acc_ref[...] = jnp.zeros_like(acc_ref)

    for r, val in enumerate(rows):
        acc_ref[0, r:r + 1, :] += val


def _colsum(v):
    return jnp.sum(v, axis=0, keepdims=True)


def _ln_stats(z):
    mu = jnp.mean(z, axis=-1, keepdims=True)
    zc = z - mu
    var = jnp.mean(zc * zc, axis=-1, keepdims=True)
    rstd = lax.rsqrt(var + LN_EPS)
    return zc * rstd, rstd


def _ln_bwd(dxhat, xhat, rstd):
    m1 = jnp.mean(dxhat, axis=-1, keepdims=True)
    m2 = jnp.mean(dxhat * xhat, axis=-1, keepdims=True)
    return rstd * (dxhat - m1 - xhat * m2)


def _modulate_in(x, mod):
    bsz, seq, d = x.shape

    def body(x_ref, mod_ref, u_ref):
        u_ref[0] = (x_ref[0] * (1.0 + mod_ref[0, 1:2, :]) + mod_ref[0, 0:1, :]).astype(BF16)

    return _row_call(body, "modulate_in", bsz, seq, [_rows(d), _per_batch(8, d)], _rows(d),
                     jax.ShapeDtypeStruct((bsz, seq, d), BF16))(x, mod)


EP_TILE = 512


def _ep_specs(seq, d):
    tiles = seq // EP_TILE
    return ((EP_TILE, d), lambda i, j: (i, 0)), ((1, 8, d), lambda i, j: (i // tiles, 0, 0)), ((1, d), lambda i, j: (0, 0))


def _wo_ln1(merged, wo, x, mod, g, b, seq):
    ntok, d = x.shape
    row, per_b, whole = _ep_specs(seq, d)

    def epilogue(y, i, j, ins, outs):
        x_ref, mod_ref, g_ref, b_ref = ins
        y_ref, h_ref, u_ref = outs
        z = DEEPNORM_ALPHA * x_ref[...] + (1.0 + mod_ref[0, 2:3, :]) * y
        xhat, _ = _ln_stats(z)
        h = xhat * g_ref[...] + b_ref[...]
        y_ref[...] = y
        h_ref[...] = h
        u_ref[...] = (h * (1.0 + mod_ref[0, 4:5, :]) + mod_ref[0, 3:4, :]).astype(BF16)

    f32, bf16 = jax.ShapeDtypeStruct((ntok, d), F32), jax.ShapeDtypeStruct((ntok, d), BF16)
    return _matmul(merged, wo, mode="nn", tm=EP_TILE, tn=d, tk=d, name="w_o_ln1",
                   ins=[(x,) + row, (mod,) + per_b, (g,) + whole, (b,) + whole],
                   outs=[(f32,) + row, (f32,) + row, (bf16,) + row], epilogue=epilogue)


FF_HALF = D_FF // 2


def _interleave_gate_up(w):
    return w.reshape(2, 2, FF_HALF, w.shape[1]).transpose(1, 0, 2, 3).reshape(w.shape)


def _gate_up_silu(u2, wgut_i):
    ntok = u2.shape[0]

    def epilogue(h, i, j, ins, outs):
        h_ref, a_ref = outs
        hg, hu = h[:, :FF_HALF], h[:, FF_HALF:]
        h_ref[...] = h.astype(BF16)
        a_ref[...] = (hg * _sigmoid(hg) * hu).astype(BF16)

    return _matmul(u2, wgut_i, mode="nt", tm=EP_TILE, tn=2 * FF_HALF, tk=u2.shape[1], name="gate_up_silu",
                   outs=[(jax.ShapeDtypeStruct((ntok, 2 * D_FF), BF16), (EP_TILE, 2 * FF_HALF), lambda i, j: (i, j)),
                         (jax.ShapeDtypeStruct((ntok, D_FF), BF16), (EP_TILE, FF_HALF), lambda i, j: (i, j))],
                   epilogue=epilogue)


def _down_dgrad_silu_bwd(dy2, wd, h_i):
    ntok = dy2.shape[0]
    wide = ((EP_TILE, 2 * FF_HALF), lambda i, j: (i, j))

    def epilogue(da, i, j, ins, outs):
        h = ins[0][...].astype(F32)
        hg, hu = h[:, :FF_HALF], h[:, FF_HALF:]
        sg = _sigmoid(hg)
        outs[0][:, :FF_HALF] = (da * hu * (sg * (1.0 + hg * (1.0 - sg)))).astype(BF16)
        outs[0][:, FF_HALF:] = (da * (hg * sg)).astype(BF16)

    return _matmul(dy2, wd, mode="nt", tm=EP_TILE, tn=FF_HALF, tk=dy2.shape[1], name="down_dgrad_silu_bwd",
                   ins=[(h_i,) + wide], outs=[(jax.ShapeDtypeStruct((ntok, 2 * D_FF), BF16),) + wide], epilogue=epilogue)[0]


def _down_ln2_loss_bwd(a, wd, h1, mod, g, b, target, seq):
    ntok, d = h1.shape
    row, per_b, whole = _ep_specs(seq, d)
    tiles = seq // EP_TILE

    def epilogue(y, i, j, ins, outs):
        h_ref, mod_ref, g_ref, b_ref, t_ref = ins
        dy_ref, dh_ref, acc_ref = outs
        gate = 1.0 + mod_ref[0, 5:6, :]
        z = DEEPNORM_ALPHA * h_ref[...] + gate * y
        xhat, rstd = _ln_stats(z)
        diff = xhat * g_ref[...] + b_ref[...] - t_ref[...]
        loss = 0.5 * jnp.sum(jnp.sum(diff * diff, axis=-1, keepdims=True) / d, axis=0, keepdims=True)
        dout = diff / d
        dz = _ln_bwd(dout * g_ref[...], xhat, rstd)
        dy_ref[...] = (gate * dz).astype(BF16)
        dh_ref[...] = DEEPNORM_ALPHA * dz
        _acc_rows(acc_ref, i % tiles == 0,
                  [_colsum(dout * xhat), _colsum(dout), _colsum(dz * y), jnp.broadcast_to(loss, (1, d))])

    return _matmul(a, wd, mode="nn", tm=EP_TILE, tn=d, tk=a.shape[1], name="down_ln2_loss_bwd",
                   ins=[(h1,) + row, (mod,) + per_b, (g,) + whole, (b,) + whole, (target,) + row],
                   outs=[(jax.ShapeDtypeStruct((ntok, d), BF16),) + row, (jax.ShapeDtypeStruct((ntok, d), F32),) + row,
                         (jax.ShapeDtypeStruct((ntok // seq, 8, d), F32),) + per_b], epilogue=epilogue)


def _gate_up_dgrad_ln1_bwd(dh, wgut, dh1a, x, y1, mod, g, b, seq):
    ntok, d = x.shape
    row, per_b, whole = _ep_specs(seq, d)
    tiles = seq // EP_TILE

    def epilogue(du, i, j, ins, outs):
        dh_ref, x_ref, y_ref, mod_ref, g_ref, b_ref = ins
        dy_ref, dx_ref, acc_ref = outs
        y = y_ref[...]
        gate = 1.0 + mod_ref[0, 2:3, :]
        z = DEEPNORM_ALPHA * x_ref[...] + gate * y
        xhat, rstd = _ln_stats(z)
        h1 = xhat * g_ref[...] + b_ref[...]
        dh1 = dh_ref[...] + du * (1.0 + mod_ref[0, 4:5, :])
        dz = _ln_bwd(dh1 * g_ref[...], xhat, rstd)
        dy_ref[...] = (gate * dz).astype(BF16)
        dx_ref[...] = DEEPNORM_ALPHA * dz
        _acc_rows(acc_ref, i % tiles == 0,
                  [_colsum(dh1 * xhat), _colsum(dh1), _colsum(dz * y), _colsum(du * h1), _colsum(du)])

    return _matmul(dh, wgut, mode="nn", tm=EP_TILE, tn=d, tk=D_FF, name="gate_up_dgrad_ln1_bwd",
                   ins=[(dh1a,) + row, (x,) + row, (y1,) + row, (mod,) + per_b, (g,) + whole, (b,) + whole],
                   outs=[(jax.ShapeDtypeStruct((ntok, d), BF16),) + row, (jax.ShapeDtypeStruct((ntok, d), F32),) + row,
                         (jax.ShapeDtypeStruct((ntok // seq, 8, d), F32),) + per_b], epilogue=epilogue)


def _wo_dgrad_gate_bwd(dy1, wo, gab, ya, yb):
    ntok, d = ya.shape
    tm, tn = EP_TILE, 512
    tile = ((tm, tn), lambda i, j: (i, j))
    tile_b = ((tm, tn), lambda i, j: (i, j + d // tn))

    def epilogue(dm_, i, j, ins, outs):
        ga_ref, gb_ref, ya_ref, yb_ref = ins
        dya_ref, dyb_ref, dga_ref, dgb_ref = outs
        sa, sb = _sigmoid(ga_ref[...].astype(F32)), _sigmoid(gb_ref[...].astype(F32))
        dya_ref[...] = (dm_ * sa).astype(BF16)
        dyb_ref[...] = (dm_ * sb).astype(BF16)
        dga_ref[...] = (dm_ * ya_ref[...].astype(F32) * sa * (1.0 - sa)).astype(BF16)
        dgb_ref[...] = (dm_ * yb_ref[...].astype(F32) * sb * (1.0 - sb)).astype(BF16)

    shp = jax.ShapeDtypeStruct((ntok, d), BF16)
    return _matmul(dy1, wo, mode="nt", tm=tm, tn=tn, tk=d, name="w_o_dgrad_gate_bwd",
                   ins=[(gab,) + tile, (gab,) + tile_b, (ya,) + tile, (yb,) + tile],
                   outs=[(shp,) + tile] * 4, epilogue=epilogue)


def _w_in_dgrad_grad_x(dproj, wint, dxa, x, mod, seq, token):
    ntok, d = x.shape
    row, per_b, _ = _ep_specs(seq, d)
    tiles = seq // EP_TILE

    def epilogue(du, i, j, ins, outs):
        dxa_ref, x_ref, mod_ref = ins
        gx_ref, acc_ref = outs
        gx_ref[...] = dxa_ref[...] + du * (1.0 + mod_ref[0, 1:2, :])
        _acc_rows(acc_ref, i % tiles == 0, [_colsum(du * x_ref[...]), _colsum(du)])

    return _matmul(dproj, wint, mode="nn", tm=EP_TILE, tn=d, tk=wint.shape[0] // 2, name="w_in_dgrad_grad_x", token=token,
                   ins=[(dxa,) + row, (x,) + row, (mod,) + per_b],
                   outs=[(jax.ShapeDtypeStruct((ntok, d), F32),) + row, (jax.ShapeDtypeStruct((ntok // seq, 8, d), F32),) + per_b],
                   epilogue=epilogue)


def _merge_branch_b_gate(os_, ls_, wbbt, gab, ya):
    ntok, d = ya.shape
    w = os_[0].shape[1]
    tm, tn = EP_TILE, 512
    tile = ((tm, tn), lambda i, j: (i, j))
    tile_b = ((tm, tn), lambda i, j: (i, j + d // tn))
    row = ((tm, w), lambda i, j: (i, 0))

    def lhs_fn(o0_ref, ins, outs):
        os_r, ls_r = (o0_ref,) + tuple(ins[3:5]), ins[5:8]
        ls = [l[...] for l in ls_r]
        mx = jnp.maximum(jnp.maximum(ls[0], ls[1]), ls[2])
        es = [jnp.exp(l - mx) for l in ls]
        den = es[0] + es[1] + es[2]
        ob = functools.reduce(jnp.add, [(e / den) * o[...].astype(F32) for e, o in zip(es, os_r)]).astype(BF16)
        outs[2][...] = ob
        return ob

    def epilogue(yb, i, j, ins, outs):
        ga_ref, gb_ref, ya_ref = ins[:3]
        yb_ref, merged_ref = outs[:2]
        yb_ref[...] = yb.astype(BF16)
        merged_ref[...] = (_sigmoid(ga_ref[...].astype(F32)) * ya_ref[...].astype(F32)
                           + _sigmoid(gb_ref[...].astype(F32)) * yb).astype(BF16)

    shp = jax.ShapeDtypeStruct((ntok, d), BF16)
    return _matmul(os_[0], wbbt, mode="nt", tm=tm, tn=tn, tk=w, name="merge_branch_b_gate", lhs_fn=lhs_fn,
                   ins=[(gab,) + tile, (gab,) + tile_b, (ya,) + tile] + [(v,) + row for v in list(os_[1:]) + list(ls_)],
                   outs=[(shp,) + tile] * 2 + [(jax.ShapeDtypeStruct((ntok, w), BF16),) + row], epilogue=epilogue)


def _segsum64(v):
    rows, width = v.shape
    ri = lax.broadcasted_iota(jnp.int32, (LANES, LANES), 0) // HEAD_DIM
    ci = lax.broadcasted_iota(jnp.int32, (LANES, LANES), 1) // HEAD_DIM
    ones = jnp.where(ri == ci, 1.0, 0.0).astype(BF16)
    out = []
    for c in range(width // LANES):
        part = v[:, c * LANES:(c + 1) * LANES]
        hi = part.astype(BF16)
        lo = (part - hi.astype(F32)).astype(BF16)
        out.append(jnp.dot(hi, ones, preferred_element_type=F32) + jnp.dot(lo, ones, preferred_element_type=F32))
    return jnp.concatenate(out, axis=1) if len(out) > 1 else out[0]


def _branch_b_dgrad_merge_bwd(dyb, wbbt, os_, ls_):
    ntok, w = os_[0].shape
    row = ((EP_TILE, w), lambda i, j: (i, 0))

    def epilogue(dob_, i, j, ins, outs):
        os_r, ls_r = ins[:3], ins[3:]
        do_r, dd_r = outs[:3], outs[3:]
        ls = [l[...] for l in ls_r]
        mx = jnp.maximum(jnp.maximum(ls[0], ls[1]), ls[2])
        es = [jnp.exp(l - mx) for l in ls]
        den = es[0] + es[1] + es[2]
        ws = [e / den for e in es]
        dws = [_segsum64(dob_ * o[...].astype(F32)) for o in os_r]
        mean = ws[0] * dws[0] + ws[1] * dws[1] + ws[2] * dws[2]
        for wg, do_ref, dd_ref in zip(ws, do_r, dd_r):
            do_ref[...] = wg * dob_
            dd_ref[...] = -wg * mean

    shp = jax.ShapeDtypeStruct((ntok, w), F32)
    return _matmul(dyb, wbbt, mode="nn", tm=EP_TILE, tn=w, tk=dyb.shape[1], name="branch_b_dgrad_merge_bwd",
                   ins=[(v,) + row for v in list(os_) + list(ls_)], outs=[(shp,) + row] * 6, epilogue=epilogue)


def _branch_a_dgrad_delta(dya, wba, oa, lse_a, sinks_exp, seq):
    ntok, w = oa.shape
    row, per_b, whole = _ep_specs(seq, w)
    tiles = seq // EP_TILE

    def epilogue(do_, i, j, ins, outs):
        o_ref, l_ref, s_ref = ins
        do_ref, dd_ref, acc_ref = outs
        dd = -_segsum64(do_ * o_ref[...].astype(F32))
        do_ref[...] = do_
        dd_ref[...] = dd
        _acc_rows(acc_ref, i % tiles == 0, [_colsum(dd * jnp.exp(s_ref[...] - l_ref[...]))])

    shp = jax.ShapeDtypeStruct((ntok, w), F32)
    return _matmul(dya, wba, mode="nt", tm=EP_TILE, tn=w, tk=dya.shape[1], name="branch_a_dgrad_delta",
                   ins=[(oa,) + row, (lse_a,) + row, (sinks_exp,) + whole],
                   outs=[(shp,) + row, (shp,) + row, (jax.ShapeDtypeStruct((ntok // seq, 8, w), F32),) + per_b],
                   epilogue=epilogue)


def _swap_halves(v):
    src = lax.broadcasted_iota(jnp.int32, (LANES, LANES), 0)
    dst = lax.broadcasted_iota(jnp.int32, (LANES, LANES), 1)
    partner = jnp.where((dst % HEAD_DIM) < HEAD_DIM // 2, dst + HEAD_DIM // 2, dst - HEAD_DIM // 2)
    perm = jnp.where(src == partner, 1.0, 0.0).astype(BF16)
    hi = v.astype(BF16)
    lo = (v - hi.astype(F32)).astype(BF16)
    return jnp.dot(hi, perm, preferred_element_type=F32) + jnp.dot(lo, perm, preferred_element_type=F32)


def _swap_halves_roll(v):
    lane = lax.broadcasted_iota(jnp.int32, v.shape, 1)
    return jnp.where((lane % HEAD_DIM) < HEAD_DIM // 2, pltpu.roll(v, LANES - HEAD_DIM // 2, 1),
                     pltpu.roll(v, HEAD_DIM // 2, 1))


def _swap_halves_coarse(v):
    src = lax.broadcasted_iota(jnp.int32, (LANES, LANES), 0)
    dst = lax.broadcasted_iota(jnp.int32, (LANES, LANES), 1)
    partner = jnp.where((dst % HEAD_DIM) < HEAD_DIM // 2, dst + HEAD_DIM // 2, dst - HEAD_DIM // 2)
    perm = jnp.where(src == partner, 1.0, 0.0).astype(BF16)
    return jnp.dot(v.astype(BF16), perm, preferred_element_type=F32)


def _rope(v, cos, sin, sign=1.0, mxu=True, coarse=False):
    swap = (_swap_halves_coarse if coarse else _swap_halves) if mxu else _swap_halves_roll
    out = []
    for c in range(v.shape[1] // LANES):
        part = v[:, c * LANES:(c + 1) * LANES]
        out.append(part * cos + sign * (swap(part) * sin))
    return jnp.concatenate(out, axis=1) if len(out) > 1 else out[0]


def _half_mask(shape, half):
    lane = lax.broadcasted_iota(jnp.int32, shape, len(shape) - 1) % LANES
    return (lane < HEAD_DIM) if half == 0 else (lane >= HEAD_DIM)


def _dup_half(v, half):
    return jnp.where(_half_mask(v.shape, half), v, pltpu.roll(v, HEAD_DIM, 1))


def _fold_halves(v):
    return v + pltpu.roll(v, HEAD_DIM, 1)


def _pick_halves(lo_rows, hi_rows):
    return jnp.where(_half_mask(lo_rows.shape, 0), lo_rows, hi_rows)


def _stack_masked(v, pairs):
    parts = []
    for c in pairs:
        pair = v[:, c * LANES:(c + 1) * LANES]
        parts += [jnp.where(_half_mask(pair.shape, half), pair, 0.0) for half in (0, 1)]
    return jnp.concatenate(parts, axis=0)


def _stack_pair_cols(v, pairs):
    return jnp.concatenate([v[:, c * LANES + half * HEAD_DIM:c * LANES + half * HEAD_DIM + 1] for c in pairs for half in (0, 1)],
                           axis=0)


ATTN_UNITS = 16


def _class_rows(r):
    return [pl.ds(0, QBLOCK)] if r == 1 else [pl.ds(rho, QBLOCK, stride=r) for rho in range(r)]


def _band_mask(nrows, nk, blk, n_back, has_prev):
    qi = lax.broadcasted_iota(jnp.int32, (nrows, nk), 0) % QBLOCK
    ki = lax.broadcasted_iota(jnp.int32, (nrows, nk), 1)
    if has_prev:
        dist = qi + QBLOCK - ki
        return (dist >= 0) & (dist <= n_back) & ((ki >= QBLOCK) | (blk > 0))
    dist = qi - ki
    return (dist >= 0) & (dist <= n_back)


def _attn_fwd(q_arr, k_arr, v_arr, *, name, npair, gqa, q_col, k_col, v_col, nchunk, r, n_back, sinks=None):
    bsz, seq, _ = q_arr.shape
    rr = QBLOCK * r
    nblk = seq // rr
    qw = npair * LANES
    kw = LANES if gqa else qw
    has_prev = nblk > 1
    has_sink = sinks is not None
    scale = HEAD_DIM ** -0.5

    def body(*refs):
        refs = list(refs)
        q_ref, kc_ref, vc_ref = refs[:3]
        pos = 3
        if has_prev:
            kp_ref, vp_ref = refs[pos:pos + 2]
            pos += 2
        if has_sink:
            sink_ref = refs[pos]
            pos += 1
        o_ref, lse_ref = refs[pos:pos + 2]
        if r > 1:
            stage_o = refs[pos + 2]
        blk = pl.program_id(2)
        nk = (2 if has_prev else 1) * QBLOCK
        valid = _band_mask(QBLOCK, nk, blk, n_back, has_prev)
        per = npair // 2
        classes = _class_rows(r)
        step = max(1, ATTN_UNITS // (2 * npair))
        for first in range(0, len(classes), step):
            batch = classes[first:first + step]
            units = []
            for ci, rows in enumerate(batch):
                q = q_ref[0, rows, :] * scale
                k, v = kc_ref[0, rows, :], vc_ref[0, rows, :]
                if has_prev:
                    k = jnp.concatenate([kp_ref[0, rows, :], k], axis=0)
                    v = jnp.concatenate([vp_ref[0, rows, :], v], axis=0)
                if gqa:
                    kdup = [_dup_half(k, hk).astype(BF16) for hk in range(2)]
                    vdup = [_dup_half(v, hk) for hk in range(2)]
                for c in range(npair):
                    sl = slice(c * LANES, (c + 1) * LANES)
                    qc = q[:, sl]
                    kc, vc = (kdup[c // per], vdup[c // per]) if gqa else (k[:, sl].astype(BF16), v[:, sl])
                    for half in (0, 1):
                        qm = jnp.where(_half_mask(qc.shape, half), qc, 0.0).astype(BF16)
                        vm = jnp.where(_half_mask(vc.shape, half), vc, 0.0).astype(BF16)
                        s = lax.dot_general(qm, kc, _DIMS["nt"], preferred_element_type=F32)
                        units.append(dict(ci=ci, c=c, half=half, s=s, vm=vm, sk=sink_ref[2 * c + half] if has_sink else None))
            for u in units:
                s = jnp.where(valid, u["s"], NEG_INF)
                m = jnp.max(s, axis=1, keepdims=True)
                if has_sink:
                    m = jnp.maximum(m, u["sk"])
                p = jnp.exp(s - m)
                den = jnp.sum(p, axis=1, keepdims=True)
                if has_sink:
                    den = den + jnp.exp(u["sk"] - m)
                u.update(p=p.astype(BF16), den=den, lse=m + jnp.log(den))
            for u in units:
                u["o"] = jnp.dot(u["p"], u["vm"], preferred_element_type=F32) / u["den"]
            for ci, rows in enumerate(batch):
                outs, lses = [None] * npair, [None] * npair
                for u in units:
                    if u["ci"] != ci:
                        continue
                    c, o = u["c"], u["o"]
                    lse = jnp.broadcast_to(u["lse"], o.shape)
                    outs[c] = o if u["half"] == 0 else outs[c] + o
                    lses[c] = lse if u["half"] == 0 else _pick_halves(lses[c], lse)
                o_new = jnp.concatenate(outs, axis=1) if npair > 1 else outs[0]
                if r > 1:
                    stage_o[rows, :] = o_new
                else:
                    o_ref[0] = o_new.astype(BF16)
                lse_ref[0, rows, :] = jnp.concatenate(lses, axis=1) if npair > 1 else lses[0]
        if r > 1:
            o_ref[0] = stage_o[...].astype(BF16)

    def cur(width, col0):
        return pl.BlockSpec((1, rr, width), lambda b, c, i: (b, i, col0 + c))

    def prev(width, col0):
        return pl.BlockSpec((1, rr, width), lambda b, c, i: (b, jnp.maximum(i - 1, 0), col0 + c))

    in_specs = [cur(qw, q_col), cur(kw, k_col), cur(kw, v_col)]
    args = [q_arr, k_arr, v_arr]
    if has_prev:
        in_specs += [prev(kw, k_col), prev(kw, v_col)]
        args += [k_arr, v_arr]
    if has_sink:
        in_specs.append(pl.BlockSpec(memory_space=pltpu.SMEM))
        args.append(sinks)
    return pl.pallas_call(
        body,
        name=name,
        grid=(bsz, nchunk, nblk),
        in_specs=in_specs,
        out_specs=[pl.BlockSpec((1, rr, qw), lambda b, c, i: (b, i, c))] * 2,
        out_shape=[jax.ShapeDtypeStruct((bsz, seq, nchunk * qw), BF16), jax.ShapeDtypeStruct((bsz, seq, nchunk * qw), F32)],
        scratch_shapes=[pltpu.VMEM((rr, qw), F32)] if r > 1 else [],
        compiler_params=_params("parallel", "parallel", "parallel"),
    )(*args)


def _attn_bwd(q_arr, k_arr, v_arr, cos, sin, do, lse, dd, *, name, npair, gqa, q_col, k_col, v_col, nchunk, r, n_back,
              token=None):
    bsz, seq, _ = q_arr.shape
    rr = QBLOCK * r
    nblk = seq // rr
    qw = npair * LANES
    kw = LANES if gqa else qw
    has_next = nblk > 1
    has_token = token is not None
    staged = r > 1
    scale = HEAD_DIM ** -0.5

    def body(*refs):
        refs = list(refs)
        k_ref, v_ref, c_ref, s_ref = refs[:4]
        tile_refs = [refs[4:8]]
        pos = 8
        if has_next:
            tile_refs.append(refs[pos:pos + 4])
            pos += 4
        if has_token:
            pos += 1
        dq_ref, dk_ref, dv_ref = refs[pos:pos + 3]
        carry_ref, bcast_ref = refs[pos + 3:pos + 5]
        if staged:
            stage_q, stage_k, stage_v = refs[pos + 5:pos + 8]
        blk = pl.program_id(2)
        if has_next:
            @pl.when(blk == 0)
            def _():
                carry_ref[...] = jnp.zeros_like(carry_ref)

        nrows = (npair if gqa else 1) * QBLOCK
        qi = lax.broadcasted_iota(jnp.int32, (nrows, QBLOCK), 0) % QBLOCK
        ki = lax.broadcasted_iota(jnp.int32, (nrows, QBLOCK), 1)
        valids = [qi >= ki, (qi + QBLOCK - ki <= n_back) & (blk + 1 < nblk)]
        per = npair // 2
        ntile = len(tile_refs)
        cat = lambda parts: jnp.concatenate(parts, axis=1) if len(parts) > 1 else parts[0]
        classes = _class_rows(r)
        step = max(1, ATTN_UNITS // (ntile * (2 if gqa else 2 * npair)))
        def stat_cols(stat, slot):
            if gqa:
                return _stack_pair_cols(stat, list(range(slot * per, (slot + 1) * per)))
            col = slot * HEAD_DIM
            return stat[:, col:col + 1]

        nslot = 2 if gqa else 2 * npair
        if has_next:
            @pl.when(blk == 0)
            def _():
                for rows in classes:
                    for which, stat_ref in enumerate(tile_refs[0][2:4]):
                        stat = stat_ref[0, rows, :]
                        for slot in range(nslot):
                            bcast_ref[which, slot, rows if not gqa else slice(None), :] = jnp.broadcast_to(
                                stat_cols(stat, slot), (nrows, LANES))

        for first in range(0, len(classes), step):
            batch = classes[first:first + step]
            units = []
            for ci, rows in enumerate(batch):
                keep = slice(None) if gqa else rows
                tiles = [(q_ref[0, rows, :] * scale, do_ref[0, rows, :], l_ref[0, rows, :], d_ref[0, rows, :])
                         for q_ref, do_ref, l_ref, d_ref in tile_refs]

                def stats(t, slot, keep=keep, tiles=tiles):
                    if has_next and t == 0:
                        return bcast_ref[0, slot, keep, :], bcast_ref[1, slot, keep, :]
                    return tuple(jnp.broadcast_to(stat_cols(tiles[t][2 + w], slot), (nrows, LANES)) for w in range(2))

                k, v = k_ref[0, rows, :], v_ref[0, rows, :]
                if gqa:
                    for hk in range(2):
                        pairs = list(range(hk * per, (hk + 1) * per))
                        kd, vd = _dup_half(k, hk).astype(BF16), _dup_half(v, hk).astype(BF16)
                        for t, (q, do_, l_, d_) in enumerate(tiles):
                            lcol, dcol = stats(t, hk)
                            units.append(dict(ci=ci, t=t, hk=hk, slot=hk, keep=keep, pairs=pairs,
                                              qs=_stack_masked(q, pairs).astype(BF16),
                                              dos=_stack_masked(do_, pairs).astype(BF16), lcol=lcol, dcol=dcol,
                                              kmat=kd, vmat=vd, kdq=kd))
                else:
                    for c in range(npair):
                        sl = slice(c * LANES, (c + 1) * LANES)
                        kc, vcb = k[:, sl], v[:, sl].astype(BF16)
                        kcb = kc.astype(BF16)
                        for t, (q, do_, l_, d_) in enumerate(tiles):
                            for half in (0, 1):
                                hm = _half_mask(kc.shape, half)
                                lcol, dcol = stats(t, 2 * c + half)
                                units.append(dict(ci=ci, t=t, c=c, half=half, slot=2 * c + half, keep=keep,
                                                  qs=jnp.where(hm, q[:, sl], 0.0).astype(BF16),
                                                  dos=jnp.where(hm, do_[:, sl], 0.0).astype(BF16), lcol=lcol, dcol=dcol,
                                                  kmat=kcb, vmat=vcb, kdq=jnp.where(hm, kc, 0.0).astype(BF16)))
            for u in units:
                u["s"] = lax.dot_general(u["qs"], u["kmat"], _DIMS["nt"], preferred_element_type=F32)
                u["dp"] = lax.dot_general(u["dos"], u["vmat"], _DIMS["nt"], preferred_element_type=F32)
            for u in units:
                p = jnp.exp(jnp.where(valids[u["t"]], u["s"], NEG_INF) - u["lcol"])
                u["ds"] = (p * (u["dp"] + u["dcol"])).astype(BF16)
                u["p"] = p.astype(BF16)
            for u in units:
                u["dv"] = lax.dot_general(u["p"], u["dos"], _DIMS["tn"], preferred_element_type=F32)
                u["dk"] = lax.dot_general(u["ds"], u["qs"], _DIMS["tn"], preferred_element_type=F32)
                u["dq"] = jnp.dot(u["ds"], u["kdq"], preferred_element_type=F32) * scale
            for u in units:
                if u["t"] == 1:
                    bcast_ref[0, u["slot"], u["keep"], :] = u["lcol"]
                    bcast_ref[1, u["slot"], u["keep"], :] = u["dcol"]
            for ci, rows in enumerate(batch):
                mine = [u for u in units if u["ci"] == ci]
                dq = [[None] * npair for _ in range(ntile)]
                if gqa:
                    dk_out = dv_out = None
                    for hk in range(2):
                        us = [u for u in mine if u["hk"] == hk]
                        for u in us:
                            for i, c in enumerate(u["pairs"]):
                                dq[u["t"]][c] = _pick_halves(u["dq"][2 * i * QBLOCK:(2 * i + 1) * QBLOCK],
                                                             u["dq"][(2 * i + 1) * QBLOCK:(2 * i + 2) * QBLOCK])
                        dk_h = _fold_halves(functools.reduce(jnp.add, [u["dk"] for u in us]))
                        dv_h = _fold_halves(functools.reduce(jnp.add, [u["dv"] for u in us]))
                        dk_out = dk_h if hk == 0 else _pick_halves(dk_out, dk_h)
                        dv_out = dv_h if hk == 0 else _pick_halves(dv_out, dv_h)
                else:
                    dks, dvs = [], []
                    for c in range(npair):
                        us = [u for u in mine if u["c"] == c]
                        dks.append(functools.reduce(jnp.add, [u["dk"] for u in us]))
                        dvs.append(functools.reduce(jnp.add, [u["dv"] for u in us]))
                        for t in range(ntile):
                            dq[t][c] = functools.reduce(jnp.add, [u["dq"] for u in us if u["t"] == t])
                    dk_out, dv_out = cat(dks), cat(dvs)
                ck, sk_ = c_ref[0, rows, :], s_ref[0, rows, :]
                dk_new = _rope(dk_out, ck, sk_, sign=-1.0, mxu=gqa, coarse=True)
                dq_cur = cat(dq[0])
                if has_next:
                    dq_cur = dq_cur + carry_ref[rows, :]
                    carry_ref[rows, :] = cat(dq[1])
                dq_new = _rope(dq_cur, ck, sk_, sign=-1.0, mxu=gqa, coarse=True)
                if staged:
                    stage_q[rows, :], stage_k[rows, :], stage_v[rows, :] = dq_new, dk_new, dv_out
                else:
                    dq_ref[0], dk_ref[0], dv_ref[0] = dq_new.astype(BF16), dk_new.astype(BF16), dv_out.astype(BF16)
        if staged:
            dq_ref[0], dk_ref[0], dv_ref[0] = stage_q[...].astype(BF16), stage_k[...].astype(BF16), stage_v[...].astype(BF16)

    def at(width, col0, shift):
        return pl.BlockSpec((1, rr, width), lambda b, c, i: (b, jnp.minimum(i + shift, nblk - 1), col0 + c))

    in_specs = [at(kw, k_col, 0), at(kw, v_col, 0), pl.BlockSpec((1, rr, LANES), lambda b, c, i: (b, i, 0)),
                pl.BlockSpec((1, rr, LANES), lambda b, c, i: (b, i, 0))]
    args = [k_arr, v_arr, cos, sin]
    for shift in (0, 1) if has_next else (0,):
        in_specs += [at(qw, q_col, shift), at(qw, 0, shift), at(qw, 0, shift), at(qw, 0, shift)]
        args += [q_arr, do, lse, dd]
    if has_token:
        in_specs.append(pl.BlockSpec(token.shape, lambda b, c, i: (0, 0)))
        args.append(token)
    return pl.pallas_call(
        body,
        name=name,
        grid=(bsz, nchunk, nblk),
        in_specs=in_specs,
        out_specs=[pl.BlockSpec((1, rr, qw), lambda b, c, i: (b, i, c)),
                   pl.BlockSpec((1, rr, kw), lambda b, c, i: (b, i, c)),
                   pl.BlockSpec((1, rr, kw), lambda b, c, i: (b, i, c))],
        out_shape=[jax.ShapeDtypeStruct((bsz, seq, nchunk * qw), BF16),
                   jax.ShapeDtypeStruct((bsz, seq, nchunk * kw), BF16),
                   jax.ShapeDtypeStruct((bsz, seq, nchunk * kw), BF16)],
        scratch_shapes=[pltpu.VMEM((rr, qw) if has_next else (8, LANES), F32),
                        pltpu.VMEM((2, 2 if gqa else 2 * npair, npair * QBLOCK if gqa else rr, LANES) if has_next
                                   else (1, 1, 8, LANES), F32)] +
                       ([pltpu.VMEM((rr, qw), F32), pltpu.VMEM((rr, kw), F32), pltpu.VMEM((rr, kw), F32)] if staged else []),
        compiler_params=_params("parallel", "parallel", "arbitrary"),
    )(*args)


B_CHUNKS = {1: (4, 1), 4: (1, 4), 16: (1, 4)}


def _rope_tables(positions):
    half = HEAD_DIM // 2
    inv = ROPE_THETA ** (-jnp.arange(half, dtype=F32) / half)
    ang = positions.astype(F32)[..., None] * inv
    cos, sin = jnp.cos(ang), jnp.sin(ang)
    return jnp.concatenate([cos] * 4, axis=-1), jnp.concatenate([-sin, sin, -sin, sin], axis=-1)


def _layer_step(x, mod, tables, sinks, ln1_g, ln1_b, ln2_g, ln2_b, target, get_w_in, get_rest, hook):
    bsz, seq, d = x.shape
    ntok = bsz * seq
    flat = lambda v: v.reshape(ntok, v.shape[-1])
    unflat = lambda v: v.reshape(bsz, seq, v.shape[-1])
    cos, sin = tables
    mm = functools.partial(_matmul, tm=1024, tk=1024)
    scalar = lambda tok: 0.0 if tok is None else tok[0, 0]

    u1 = _modulate_in(x, mod)
    u1f = flat(u1)
    wint = get_w_in(u1)
    cosf, sinf = flat(cos), flat(sin)
    proj = functools.partial(_proj_rope, u1f, wint, cosf, sinf, tm=2048)
    qkvb = unflat(proj(n=4608, b_off=OFF_QKVB, rope_cols=3072, tn=256, name="proj_qkvb"))
    b_kws, os_, ls_ = [], [], []
    for g, (window, r) in enumerate(B_PATTERNS):
        npair, nch = B_CHUNKS[r]
        per = B_HEADS_PER_GROUP // (2 * npair)
        nsec = len(B_PATTERNS) * per
        kw_ = dict(npair=npair, gqa=False, q_col=g * per, k_col=nsec + g * per, v_col=2 * nsec + g * per, nchunk=nch, r=r,
                   n_back=window // r)
        b_kws.append(kw_)
        o_g, l_g = _attn_fwd(qkvb, qkvb, qkvb, name=f"attn_b{g}_fwd", **kw_)
        os_.append(o_g)
        ls_.append(l_g)
    tok = hook("projected", os_[-1])
    proj = functools.partial(_proj_rope, u1f, wint, cosf + scalar(tok), sinf, tm=2048)
    gab = unflat(proj(n=2048, b_off=OFF_GAB, rope_cols=0, tn=256, name="proj_gab", out_dtype=BF16))
    qa = unflat(proj(n=1024, b_off=OFF_QA, rope_cols=1024, tn=512, name="proj_qa"))
    kva = unflat(proj(n=256, b_off=OFF_KVA, rope_cols=128, tn=128, name="proj_kva"))
    a_kw = dict(npair=A_Q_HEADS // 2, gqa=True, q_col=0, k_col=0, v_col=1, nchunk=1, r=1, n_back=A_WINDOW - 1)
    after_gab = jnp.minimum(jnp.abs(gab[0, 0, 0].astype(F32)), 0.0)
    oa, lse_a = _attn_fwd(qa, kva, kva, name="attn_a_fwd", sinks=sinks.reshape(A_Q_HEADS) + after_gab, **a_kw)
    rest = get_rest(oa)
    wba, wbbt, wo, wgut, wd = (rest[n] for n in ("w_branch_a", "w_branch_b", "w_o", "w_gate_up", "w_down"))
    ya = unflat(mm(flat(oa), wba, mode="nn", out_dtype=BF16, tn=512, name="branch_a"))
    ybf, mergedf, obf = _merge_branch_b_gate([flat(t) for t in os_], [flat(t) for t in ls_], wbbt, flat(gab), flat(ya))
    xf = flat(x)
    y1f, h1f, u2f = _wo_ln1(mergedf, wo, xf, mod, ln1_g, ln1_b, seq)
    wgut_i = _interleave_gate_up(wgut)
    hf, af = _gate_up_silu(u2f, wgut_i)

    dy2f, dh1af, acc2 = _down_ln2_loss_bwd(af, wd, h1f, mod, ln2_g, ln2_b, flat(target), seq)
    g_wd = _matmul(af, dy2f, mode="tn", out_dtype=BF16, tm=256, tn=1024, tk=ntok, name="down_wgrad")
    dhf = _down_dgrad_silu_bwd(dy2f, wd, hf)
    g_wgut = _interleave_gate_up(_matmul(dhf, u2f, mode="tn", out_dtype=BF16, tm=256, tn=1024, tk=ntok, name="gate_up_wgrad"))
    dy1f, dxaf, acc1 = _gate_up_dgrad_ln1_bwd(dhf, wgut_i, dh1af, xf, y1f, mod, ln1_g, ln1_b, seq)
    g_wo = _matmul(mergedf, dy1f, mode="tn", out_dtype=BF16, tm=256, tn=1024, tk=ntok, name="w_o_wgrad")
    dyaf, dybf, dgaf, dgbf = _wo_dgrad_gate_bwd(dy1f, wo, flat(gab), flat(ya), ybf)
    g_wba = _matmul(flat(oa), dyaf, mode="tn", out_dtype=BF16, tm=256, tn=1024, tk=ntok, name="branch_a_wgrad")
    g_wbbt = _matmul(dybf, obf, mode="tn", out_dtype=BF16, tm=256, tn=512, tk=ntok, name="branch_b_wgrad")
    tok = hook("grads_rest", dict(w_branch_a=g_wba, w_branch_b=g_wbbt, w_o=g_wo, w_gate_up=g_wgut, w_down=g_wd))

    sinks_exp = jnp.repeat(sinks.reshape(1, A_Q_HEADS), HEAD_DIM, axis=1) + scalar(tok)
    doa, dd_a, acc_s = _branch_a_dgrad_delta(dyaf, wba, flat(oa), flat(lse_a), sinks_exp, seq)
    doa, dd_a = unflat(doa), unflat(dd_a)
    tok = hook("delta_done", dd_a)
    dqa, dka, dva = _attn_bwd(qa, kva, kva, cos, sin, doa, lse_a, dd_a, name="attn_a_bwd", token=tok, **a_kw)
    merged_bwd = [unflat(t) for t in _branch_b_dgrad_merge_bwd(dybf, wbbt, [flat(t) for t in os_], [flat(t) for t in ls_])]
    dqs, dks, dvs = [], [], []
    for g in range(len(B_PATTERNS)):
        dq_g, dk_g, dv_g = _attn_bwd(qkvb, qkvb, qkvb, cos, sin, merged_bwd[g], ls_[g], merged_bwd[3 + g],
                                     name=f"attn_b{g}_bwd", **b_kws[g])
        dqs.append(dq_g)
        dks.append(dk_g)
        dvs.append(dv_g)
    dproj = jnp.concatenate([t.astype(BF16) for t in [dqa, dka, dva] + dqs + dks + dvs] + [unflat(dgaf), unflat(dgbf)], axis=-1)
    dprojf = flat(dproj)
    g_wint = _matmul(dprojf, u1f, mode="tn", out_dtype=BF16, tm=256, tn=1024, tk=ntok, name="w_in_wgrad")
    tok = hook("grads_w_in", dict(w_in=g_wint))
    grad_x, acc0 = _w_in_dgrad_grad_x(dprojf, wint, dxaf, xf, mod, seq, tok)
    grad_x = unflat(grad_x)
    tok = hook("dgrad_done", grad_x)

    loss_part = jnp.sum(acc2[:, 3, 0])
    dmod = jnp.stack([acc0[:, 1], acc0[:, 0], acc1[:, 2], acc1[:, 4], acc1[:, 3], acc2[:, 2]], axis=1)
    small = jnp.stack([acc1[:, 0].sum(0), acc1[:, 1].sum(0), acc2[:, 0].sum(0), acc2[:, 1].sum(0), acc_s[:, 0].sum(0)])
    small = small + scalar(tok)
    return loss_part, grad_x, dmod, small


CHIP_FLIPS = (2, 4, 6)


def _my_place():
    return lax.axis_index("x"), lax.axis_index("y"), lax.axis_index("c")


def _flip(place, k):
    px, py, pc = place
    return (1 - px if k & 4 else px, 1 - py if k & 2 else py, 1 - pc if k & 1 else pc)


def _index(place):
    return 4 * place[0] + 2 * place[1] + place[2]


def _gather_small(v, name):
    rows, cols = v.shape

    def body(v_ref, out_ref, send_sems, recv_sems):
        me = _my_place()
        out_ref[_index(me)] = v_ref[...]
        copies = []
        for k in range(1, N_DEV):
            copies.append(pltpu.make_async_remote_copy(
                src_ref=v_ref, dst_ref=out_ref.at[_index(me)], send_sem=send_sems.at[k - 1], recv_sem=recv_sems.at[k - 1],
                device_id=_flip(me, k), device_id_type=MESH))
        for cp in copies:
            cp.start()
        for k in range(1, N_DEV):
            pltpu.make_async_remote_copy(
                src_ref=v_ref, dst_ref=out_ref.at[_index(_flip(me, k))], send_sem=send_sems.at[k - 1],
                recv_sem=recv_sems.at[k - 1], device_id=_flip(me, k), device_id_type=MESH).wait_recv()
        for cp in copies:
            cp.wait_send()

    return pl.pallas_call(
        body,
        name=name,
        out_shape=jax.ShapeDtypeStruct((N_DEV, rows, cols), v.dtype),
        in_specs=[pl.BlockSpec(memory_space=pltpu.VMEM)],
        out_specs=pl.BlockSpec(memory_space=pltpu.VMEM),
        scratch_shapes=[pltpu.SemaphoreType.DMA((N_DEV - 1,)), pltpu.SemaphoreType.DMA((N_DEV - 1,))],
        compiler_params=pltpu.CompilerParams(vmem_limit_bytes=VMEM_LIMIT_BYTES),
    )(v)


_HBM = pl.BlockSpec(memory_space=pltpu.HBM)
_SEM = pl.BlockSpec(memory_space=pltpu.SEMAPHORE)
_EFFECT = pltpu.SideEffectType.DATAFLOW_SIDE_EFFECTING


def _remote(src, dst, send_sems, recv_sems, j, to):
    return pltpu.make_async_remote_copy(src_ref=src, dst_ref=dst, send_sem=send_sems.at[j], recv_sem=recv_sems.at[j],
                                        device_id=to, device_id_type=MESH)


def _copies_start(name, bufs, make_copies, nsem):
    nbuf = len(bufs)

    def body(*refs):
        for cp in make_copies(refs[:nbuf], refs[nbuf], refs[nbuf + 1]):
            cp.start()
        refs[-1][...] = jnp.zeros_like(refs[-1])

    sems = pltpu.SemaphoreType.DMA((nsem,))
    res = pl.pallas_call(
        body, name=name,
        out_shape=(sems, sems, *[pltpu.HBM(v.shape, v.dtype) for v in bufs], jax.ShapeDtypeStruct((8, LANES), F32)),
        in_specs=(_HBM,) * nbuf, out_specs=(_SEM, _SEM) + (_HBM,) * nbuf + (pl.BlockSpec(memory_space=pltpu.VMEM),),
        input_output_aliases={i: 2 + i for i in range(nbuf)},
        compiler_params=pltpu.CompilerParams(has_side_effects=_EFFECT),
    )(*[pltpu.with_memory_space_constraint(v, pltpu.HBM) for v in bufs])
    return res[0], res[1], list(res[2:2 + nbuf]), res[-1]


def _copies_wait(name, started, make_copies, after):
    send_sems, recv_sems, bufs, _ = started
    nbuf = len(bufs)

    def body(*refs):
        for cp in make_copies(refs[:nbuf], refs[nbuf], refs[nbuf + 1]):
            cp.wait_send()
            cp.wait_recv()

    return list(pl.pallas_call(
        body, name=name,
        out_shape=tuple(pltpu.HBM(v.shape, v.dtype) for v in bufs),
        in_specs=(_HBM,) * nbuf + (_SEM, _SEM, pl.BlockSpec(memory_space=pl.ANY)), out_specs=(_HBM,) * nbuf,
        input_output_aliases={i: i for i in range(nbuf)},
        compiler_params=pltpu.CompilerParams(has_side_effects=_EFFECT),
    )(*bufs, send_sems, recv_sems, after))


def _to_sibling_copies(refs, send_sems, recv_sems):
    src_ref, land_ref = refs
    me = _my_place()
    return [_remote(src_ref.at[q, 1 - me[2]], land_ref.at[q], send_sems, recv_sems, q, _flip(me, 1)) for q in range(4)]


def _to_chips_copies(refs, send_sems, recv_sems):
    src_ref, land_ref = refs
    me = _my_place()
    copies = []
    for j, k in enumerate(CHIP_FLIPS):
        to = _flip(me, k)
        copies.append(_remote(src_ref.at[2 * to[0] + to[1]], land_ref.at[j], send_sems, recv_sems, j, to))
    return copies


class _Gather:
    def __init__(self, name, blocks):
        self.name, self.n = name, len(blocks)
        at_me = (_index(_my_place()), 0, 0)
        lands = [lax.dynamic_update_slice(lax.empty((N_DEV,) + v.shape, v.dtype), v[None], at_me) for v in blocks]
        self.first = _copies_start(name + "_start", list(blocks) + lands, self._first_copies, 4 * self.n)
        self.token = self.first[3]

    def _first_copies(self, refs, send_sems, recv_sems):
        me = _my_place()
        return [_remote(refs[w], refs[self.n + w].at[_index(me)], send_sems, recv_sems, 4 * w + j, _flip(me, k))
                for w in range(self.n) for j, k in enumerate((1,) + CHIP_FLIPS)]

    def _pass_copies(self, refs, send_sems, recv_sems):
        me = _my_place()
        copies = []
        for w, land in enumerate(refs):
            for j, k in enumerate(CHIP_FLIPS):
                slot = land.at[_index(_flip(me, k))]
                copies.append(_remote(slot, slot, send_sems, recv_sems, 3 * w + j, _flip(me, 1)))
        return copies

    def pass_on(self, after):
        lands = _copies_wait(self.name + "_wait", self.first, self._first_copies, after)[self.n:]
        self.second = _copies_start(self.name + "_pass_start", lands, self._pass_copies, 3 * self.n)
        return self.second[3]

    def finish(self, after):
        return _copies_wait(self.name + "_pass_wait", self.second, self._pass_copies, after)


SUM_SPLIT = 2


def _sum_pairs(parts, theirs):
    nchip, _, rows, cols = parts.shape
    tile = rows // SUM_SPLIT

    def body(c_ref, a_ref, b_ref, o_ref):
        o_ref[...] = (a_ref[0].astype(F32) + b_ref[...].astype(F32)).astype(BF16)

    spec = pl.BlockSpec((1, tile, cols), lambda q, t, c_ref: (q, t, 0))
    grid_spec = pltpu.PrefetchScalarGridSpec(
        num_scalar_prefetch=1, grid=(nchip, SUM_SPLIT),
        in_specs=[pl.BlockSpec((1, 1, tile, cols), lambda q, t, c_ref: (q, c_ref[0], t, 0)), spec], out_specs=spec)
    return pl.pallas_call(body, name="grad_sum_sibling", grid_spec=grid_spec,
                          out_shape=jax.ShapeDtypeStruct((nchip, rows, cols), BF16),
                          compiler_params=_params("parallel", "parallel"))(lax.axis_index("c").reshape(1), parts, theirs)


def _sum_final(chip_sum, got):
    _, rows, cols = chip_sum.shape
    tile = rows // SUM_SPLIT

    def body(q_ref, a_ref, g_ref, o_ref):
        o_ref[...] = ((a_ref[0].astype(F32) + g_ref[0].astype(F32)) + g_ref[1].astype(F32)) + g_ref[2].astype(F32)

    grid_spec = pltpu.PrefetchScalarGridSpec(
        num_scalar_prefetch=1, grid=(SUM_SPLIT,),
        in_specs=[pl.BlockSpec((1, tile, cols), lambda t, q_ref: (q_ref[0], t, 0)),
                  pl.BlockSpec((3, tile, cols), lambda t, q_ref: (0, t, 0))],
        out_specs=pl.BlockSpec((tile, cols), lambda t, q_ref: (t, 0)))
    my_chip = (2 * lax.axis_index("x") + lax.axis_index("y")).reshape(1)
    return pl.pallas_call(body, name="grad_sum_chips", grid_spec=grid_spec, out_shape=jax.ShapeDtypeStruct((rows, cols), F32),
                          compiler_params=_params("parallel"))(my_chip, chip_sum, got)


class _ReduceScatter:
    def __init__(self, name, slabs):
        self.name, self.rows = name, slabs.shape[1]
        parts = slabs.reshape(4, 2, self.rows, D_MODEL)
        self.first = _copies_start(name + "_sibling_start", [parts, lax.empty((4, self.rows, D_MODEL), slabs.dtype)],
                                   _to_sibling_copies, 4)
        self.token = self.first[3]

    def between_chips(self, after):
        parts, theirs = _copies_wait(self.name + "_sibling_wait", self.first, _to_sibling_copies, after)
        chip_sum = _sum_pairs(parts, theirs)
        self.second = _copies_start(self.name + "_chips_start", [chip_sum, lax.empty((3, self.rows, D_MODEL), chip_sum.dtype)],
                                    _to_chips_copies, 3)
        return self.second[3]

    def finish(self, after):
        chip_sum, got = _copies_wait(self.name + "_chips_wait", self.second, _to_chips_copies, after)
        return _sum_final(chip_sum, got)


def _ada_fwd(c_all, w, b):
    nb, _ = c_all.shape
    ncol = w.shape[1]

    def body(c_ref, w_ref, b_ref, o_ref):
        c = c_ref[...]
        act = (c * _sigmoid(c)).astype(BF16)
        o_ref[...] = jnp.dot(act, w_ref[...].astype(BF16), preferred_element_type=F32) + b_ref[...]

    return pl.pallas_call(body, name="ada_fwd", out_shape=jax.ShapeDtypeStruct((nb, ncol), F32),
                          compiler_params=pltpu.CompilerParams(vmem_limit_bytes=VMEM_LIMIT_BYTES))(c_all, w, b)


def _ada_wgrad(c_all_t, dmod_cols):
    d, nb = c_all_t.shape
    ncol = dmod_cols.shape[1]

    def body(ct_ref, dm_ref, o_ref):
        ct = ct_ref[...]
        act = (ct * _sigmoid(ct)).astype(BF16).astype(F32)
        dm = dm_ref[...].astype(BF16).astype(F32)
        acc = act[:, 0:1] * dm[0:1, :]
        for i in range(1, nb):
            acc = acc + act[:, i:i + 1] * dm[i:i + 1, :]
        o_ref[...] = acc

    return pl.pallas_call(body, name="ada_wgrad", out_shape=jax.ShapeDtypeStruct((d, ncol), F32),
                          compiler_params=pltpu.CompilerParams(vmem_limit_bytes=VMEM_LIMIT_BYTES))(c_all_t, dmod_cols)


SMALL_ROWS = 24


def _reduce_small(gathered):
    def body(g_ref, o_ref):
        acc = g_ref[0]
        for dev in range(1, N_DEV):
            acc = acc + g_ref[dev]
        o_ref[...] = acc

    return pl.pallas_call(body, name="reduce_small", out_shape=jax.ShapeDtypeStruct(gathered.shape[1:], F32))(gathered)


def _adamw_math(w, g, m, v):
    nm = ADAM_B1 * m + (1.0 - ADAM_B1) * g
    nv = ADAM_B2 * v + (1.0 - ADAM_B2) * (g * g)
    bc1 = 1.0 - ADAM_B1 ** ADAM_STEP
    bc2 = 1.0 - ADAM_B2 ** ADAM_STEP
    return -ADAM_LR * ((nm / bc1) / (jnp.sqrt(nv / bc2) + ADAM_EPS) + ADAM_WD * w), nm, nv


def _adamw_small(ws, gs, ms, vs, name):
    n = len(ws)

    def body(*refs):
        for i in range(n):
            res = _adamw_math(*(refs[k * n + i][...] for k in range(4)))
            for k in range(3):
                refs[(4 + k) * n + i][...] = res[k]

    shapes = [jax.ShapeDtypeStruct(w.shape, F32) for w in ws]
    res = pl.pallas_call(body, name=name, out_shape=shapes * 3)(*ws, *gs, *ms, *vs)
    return [(res[i], res[n + i], res[2 * n + i]) for i in range(n)]


def _adamw(w, g, m, v, name):
    rows, cols = w.shape
    tile = rows
    for cand in range(min(rows // 2, 512) // 8 * 8, 7, -8):
        if rows % cand == 0:
            tile = cand
            break
    spec = pl.BlockSpec((tile, cols), lambda t: (t, 0))

    def body(w_ref, g_ref, m_ref, v_ref, d_ref, nm_ref, nv_ref):
        d_ref[...], nm_ref[...], nv_ref[...] = _adamw_math(w_ref[...], g_ref[...], m_ref[...], v_ref[...])

    shp = jax.ShapeDtypeStruct((rows, cols), F32)
    return pl.pallas_call(body, name=name, grid=(rows // tile,), in_specs=[spec] * 4, out_specs=[spec] * 3, out_shape=[shp] * 3,
                          compiler_params=_params("parallel"))(w, g, m, v)


_WEIGHTS = ("w_ada", "b_ada", "w_in", "sinks", "w_branch_a", "w_branch_b", "w_o", "ln1_g", "ln1_b", "w_gate_up", "w_down",
            "ln2_g", "ln2_b")
_TRANSPOSED = ("w_in", "w_branch_b", "w_gate_up")


def _pack_shard(name, w):
    w = w.astype(BF16)
    if name in _TRANSPOSED:
        w = w.T
    return w.reshape(-1, D_MODEL)


def _unpack_full(name, slab):
    if name == "w_branch_b":
        return slab.reshape(N_DEV * 128, 512)
    return slab.reshape(-1, D_MODEL)


def _unpack_group(group, gathered):
    return {n: _unpack_full(n, slab) for (n, _), slab in zip(group, gathered)}


def _unpack_grads(group, g_packed):
    g_w, off = {}, 0
    for n, r in group:
        part = g_packed[off:off + r]
        off += r
        g_w[n] = part.reshape(128, 512) if n == "w_branch_b" else part
    return g_w


def kernel(x, c, positions, w_ada, b_ada, w_in, sinks, w_branch_a, w_branch_b, w_o, ln1_g, ln1_b, w_gate_up, w_down, ln2_g, ln2_b, loss_target, m_w_ada, m_b_ada, m_w_in, m_sinks, m_w_branch_a, m_w_branch_b, m_w_o, m_ln1_g, m_ln1_b, m_w_gate_up, m_w_down, m_ln2_g, m_ln2_b, v_w_ada, v_b_ada, v_w_in, v_sinks, v_w_branch_a, v_w_branch_b, v_w_o, v_ln1_g, v_ln1_b, v_w_gate_up, v_w_down, v_ln2_g, v_ln2_b):
    weights = dict(w_ada=w_ada, b_ada=b_ada, w_in=w_in, sinks=sinks, w_branch_a=w_branch_a, w_branch_b=w_branch_b, w_o=w_o,
                   ln1_g=ln1_g, ln1_b=ln1_b, w_gate_up=w_gate_up, w_down=w_down, ln2_g=ln2_g, ln2_b=ln2_b)
    m_in = dict(w_ada=m_w_ada, b_ada=m_b_ada, w_in=m_w_in, sinks=m_sinks, w_branch_a=m_w_branch_a, w_branch_b=m_w_branch_b,
                w_o=m_w_o, ln1_g=m_ln1_g, ln1_b=m_ln1_b, w_gate_up=m_w_gate_up, w_down=m_w_down, ln2_g=m_ln2_g, ln2_b=m_ln2_b)
    v_in = dict(w_ada=v_w_ada, b_ada=v_b_ada, w_in=v_w_in, sinks=v_sinks, w_branch_a=v_w_branch_a, w_branch_b=v_w_branch_b,
                w_o=v_w_o, ln1_g=v_ln1_g, ln1_b=v_ln1_b, w_gate_up=v_w_gate_up, w_down=v_w_down, ln2_g=v_ln2_g, ln2_b=v_ln2_b)
    bsz = x.shape[0]
    me = _index(_my_place())
    ada_cols = w_ada.shape[2]
    outs = {}

    def adamw(n, g):
        w2, m2, v2 = (t[n][0] if t[n].ndim == 3 else t[n] for t in (weights, m_in, v_in))
        shape = weights[n].shape
        if n in _TRANSPOSED:
            dlt, nm, nv = _adamw(w2.T, g, m2.T, v2.T, "adamw_" + n)
            outs[n] = tuple(t.T.reshape(shape) for t in (g, dlt, nm, nv))
        else:
            dlt, nm, nv = _adamw(w2, g, m2, v2, "adamw_" + n)
            outs[n] = tuple(t.reshape(shape) for t in (g, dlt, nm, nv))
        return nv

    packed_in = [_pack_shard(n, weights[n][0]) for n, _ in GROUP_IN]
    packed_rest = [_pack_shard(n, weights[n][0]) for n, _ in GROUP_REST]
    c_all = _gather_small(jnp.pad(c, ((0, 8 - bsz), (0, 0))), "gather_c")[:, :bsz].reshape(N_DEV * bsz, D_MODEL)
    gather_in = _Gather("gather_w_in", lax.optimization_barrier((packed_in, c_all))[0])
    b_cols = lax.dynamic_slice_in_dim(b_ada, me * ada_cols, ada_cols, axis=1)
    mod_cols = _ada_fwd(c_all, w_ada[0], b_cols + gather_in.token[0, 0])
    tables = _rope_tables(positions)
    mod_cols, tables, packed_rest = lax.optimization_barrier((mod_cols, tables, packed_rest))
    mod_all = _gather_small(mod_cols, "gather_mod").transpose(1, 0, 2).reshape(N_DEV * bsz, 6, D_MODEL)
    gather_rest = _Gather("gather_rest", lax.optimization_barrier((packed_rest, mod_all))[0])
    mod = jnp.pad(lax.dynamic_slice_in_dim(mod_all, me * bsz, bsz, axis=0), ((0, 0), (0, 2), (0, 0)))
    mod = mod + gather_rest.token[0, 0]
    mod = mod + gather_in.pass_on(mod)[0, 0]

    scatters = {}

    def get_w_in(after):
        return _unpack_group(GROUP_IN, gather_in.finish(after))["w_in"]

    def get_rest(after):
        return _unpack_group(GROUP_REST, gather_rest.finish(after))

    def pack_grads(group, grads):
        return jnp.concatenate([grads[n].reshape(N_DEV, r, D_MODEL) for n, r in group], axis=1)

    def hook(point, value):
        if point == "projected":
            return gather_rest.pass_on(value)
        if point == "grads_rest":
            scatters["rest"] = _ReduceScatter("scatter_rest", pack_grads(GROUP_REST, value))
            return scatters["rest"].token
        if point == "delta_done":
            return scatters["rest"].between_chips(value)
        if point == "grads_w_in":
            scatters["in"] = _ReduceScatter("scatter_w_in", pack_grads(GROUP_IN, value))
            return scatters["in"].token
        if point == "dgrad_done":
            return None
        raise ValueError(point)

    loss_part, grad_x, dmod, small = _layer_step(x, mod, tables, sinks[0], ln1_g, ln1_b, ln2_g, ln2_b, loss_target,
                                                 get_w_in, get_rest, hook)

    rows = jnp.concatenate([dmod.reshape(bsz * 6, D_MODEL), small, jnp.full((1, D_MODEL), loss_part, F32),
                            jnp.zeros((SMALL_ROWS - bsz * 6 - 6, D_MODEL), F32)], axis=0)
    small_all = _gather_small(rows, "gather_small")
    small_all = small_all + scatters["in"].between_chips(small_all)[0, 0]
    sums = _reduce_small(small_all)
    loss = sums[bsz * 6 + 5, 0]
    dmod_all = small_all[:, :bsz * 6].reshape(N_DEV * bsz, 6 * D_MODEL)
    small_g = {"b_ada": functools.reduce(jnp.add, [sums[6 * i:6 * i + 6] for i in range(bsz)]).reshape(1, 6 * D_MODEL),
               "sinks": sums[bsz * 6 + 4][::HEAD_DIM][None]}
    small_g.update({n: sums[bsz * 6 + i][None] for i, n in enumerate(("ln1_g", "ln1_b", "ln2_g", "ln2_b"))})
    names = list(small_g)
    for n, (dlt, nm, nv) in zip(names, _adamw_small([weights[n] for n in names], [small_g[n] for n in names],
                                                     [m_in[n] for n in names], [v_in[n] for n in names], "adamw_small")):
        outs[n] = (small_g[n], dlt, nm, nv)
    dmod_cols = lax.dynamic_slice_in_dim(dmod_all, me * ada_cols, ada_cols, axis=1)
    last = adamw("w_ada", _ada_wgrad(c_all.T, dmod_cols))
    for n, g in _unpack_grads(GROUP_REST, scatters["rest"].finish(last)).items():
        adamw(n, g)
    done = lax.optimization_barrier(tuple(outs[n][3] for n in outs))
    for n, g in _unpack_grads(GROUP_IN, scatters["in"].finish(done[0])).items():
        adamw(n, g)

    return (loss, grad_x, *[outs[n][0] for n in _WEIGHTS], *[outs[n][1] for n in _WEIGHTS], *[outs[n][2] for n in _WEIGHTS],
            *[outs[n][3] for n in _WEIGHTS])
```

```python
import functools

import jax
import jax.numpy as jnp
from jax import lax
from jax.experimental import pallas as pl
from jax.experimental.pallas import tpu as pltpu

F32 = jnp.float32
BF16 = jnp.bfloat16

D_MODEL = 1024
HEAD_DIM = 64
A_Q_HEADS = 16
A_WINDOW = 128
B_PATTERNS = ((128, 1), (512, 4), (2048, 16))
B_HEADS_PER_GROUP = 8
D_FF = 2816
QBLOCK = 128
ROPE_THETA = 10000.0
LN_EPS = 1e-5
DEEPNORM_ALPHA = 2.0 ** 0.25
NEG_INF = -1e30
ADAM_LR, ADAM_B1, ADAM_B2, ADAM_EPS, ADAM_WD, ADAM_STEP = 0.001, 0.9, 0.999, 1e-08, 0.01, 10

N_DEV = 8
LANES = 128
VMEM_LIMIT_BYTES = 56 * 1024 * 1024
MESH = pl.DeviceIdType.MESH

OFF_QA, OFF_KVA, OFF_QKVB, OFF_GAB = 0, 1024, 1280, 5888
GROUP_IN = (("w_in", 992),)
GROUP_REST = (("w_branch_a", 128), ("w_branch_b", 64), ("w_o", 128), ("w_gate_up", 704), ("w_down", 352))


def _params(*sem):
    return pltpu.CompilerParams(dimension_semantics=sem, vmem_limit_bytes=VMEM_LIMIT_BYTES)


def _sigmoid(x):
    return 1.0 / (1.0 + jnp.exp(-x))


_DIMS = {"nn": (((1,), (0,)), ((), ())), "nt": (((1,), (1,)), ((), ())), "tn": (((0,), (0,)), ((), ()))}


def _matmul(a, b, *, mode, tm, tn, tk, name, out_dtype=None, n=None, b_off=0, token=None, ins=(), outs=None, epilogue=None,
            lhs_fn=None):
    if mode == "nn":
        (m, k), nn_ = a.shape, b.shape[1]
    elif mode == "nt":
        (m, k), nn_ = a.shape, (b.shape[0] if n is None else n)
    else:
        (k, m), nn_ = a.shape, b.shape[1]
    assert m % tm == 0 and nn_ % tn == 0 and k % tk == 0 and b_off % tn == 0, (name, m, nn_, k)
    nk = k // tk
    joff = b_off // tn
    if mode == "nn":
        a_spec = pl.BlockSpec((tm, tk), lambda i, j, kk: (i, kk))
        b_spec = pl.BlockSpec((tk, tn), lambda i, j, kk: (kk, j))
    elif mode == "nt":
        a_spec = pl.BlockSpec((tm, tk), lambda i, j, kk: (i, kk))
        b_spec = pl.BlockSpec((tn, tk), lambda i, j, kk: (j + joff, kk))
    else:
        a_spec = pl.BlockSpec((tk, tm), lambda i, j, kk: (kk, i))
        b_spec = pl.BlockSpec((tk, tn), lambda i, j, kk: (kk, j))
    dims = _DIMS[mode]
    has_token = token is not None
    plain = epilogue is None
    if plain:
        outs = [(jax.ShapeDtypeStruct((m, nn_), out_dtype), (tm, tn), lambda i, j: (i, j))]

        def epilogue(acc, i, j, in_refs, out_refs):
            out_refs[0][...] = acc.astype(out_refs[0].dtype)

    nin = len(ins)
    nscratch = 1 if lhs_fn is None else 2
    assert lhs_fn is None or nk == 1

    def body(*refs):
        a_ref, b_ref = refs[:2]
        in_refs = refs[2:2 + nin]
        out_refs = refs[2 + nin + has_token:-nscratch]
        acc_ref = refs[-nscratch]
        kk = pl.program_id(2)
        if lhs_fn is None:
            lhs = a_ref[...].astype(BF16)
        else:
            lhs_ref = refs[-1]

            @pl.when(pl.program_id(1) == 0)
            def _():
                lhs_ref[...] = lhs_fn(a_ref, in_refs, out_refs)

            lhs = lhs_ref[...]
        part = lax.dot_general(lhs, b_ref[...].astype(BF16), dims, preferred_element_type=F32)

        def finish(acc):
            epilogue(acc, pl.program_id(0), pl.program_id(1), in_refs, out_refs)

        if nk == 1:
            finish(part)
        else:
            @pl.when(kk == 0)
            def _():
                acc_ref[...] = part

            @pl.when(kk > 0)
            def _():
                acc_ref[...] += part

            @pl.when(kk == nk - 1)
            def _():
                finish(acc_ref[...])

    def spec(block, index):
        return pl.BlockSpec(block, lambda i, j, kk: index(i, j))

    in_specs, args = [a_spec, b_spec], [a, b]
    for arr, block, index in ins:
        in_specs.append(spec(block, index))
        args.append(arr)
    if has_token:
        in_specs.append(pl.BlockSpec(token.shape, lambda i, j, kk: (0, 0)))
        args.append(token)
    res = pl.pallas_call(
        body,
        name=name,
        grid=(m // tm, nn_ // tn, nk),
        in_specs=in_specs,
        out_specs=[spec(block, index) for _, block, index in outs],
        out_shape=[shape for shape, _, _ in outs],
        scratch_shapes=[pltpu.VMEM((tm, tn) if nk > 1 else (8, LANES), F32)] + ([] if lhs_fn is None else [pltpu.VMEM((tm, tk), BF16)]),
        compiler_params=_params("arbitrary", "arbitrary", "arbitrary"),
    )(*args)
    return res[0] if plain else res


def _proj_rope(a, bt, cos, sin, *, n, b_off, rope_cols, tm, tn, name, out_dtype=F32):
    m, k = a.shape
    assert m % tm == 0 and n % tn == 0 and b_off % tn == 0 and rope_cols % LANES == 0, name
    joff = b_off // tn
    nrope, part = divmod(rope_cols, tn)

    def body(a_ref, b_ref, c_ref, s_ref, o_ref):
        acc = lax.dot_general(a_ref[...], b_ref[...], _DIMS["nt"], preferred_element_type=F32)
        j = pl.program_id(1)

        @pl.when(j < nrope)
        def _():
            o_ref[...] = _rope(acc, c_ref[...], s_ref[...], coarse=True).astype(o_ref.dtype)

        if part:
            @pl.when(j == nrope)
            def _():
                o_ref[:, :part] = _rope(acc[:, :part], c_ref[...], s_ref[...], coarse=True).astype(o_ref.dtype)
                o_ref[:, part:] = acc[:, part:].astype(o_ref.dtype)

        @pl.when(j >= nrope + (1 if part else 0))
        def _():
            o_ref[...] = acc.astype(o_ref.dtype)

    table = pl.BlockSpec((tm, LANES), lambda i, j: (i, 0))
    return pl.pallas_call(
        body,
        name=name,
        grid=(m // tm, n // tn),
        in_specs=[pl.BlockSpec((tm, k), lambda i, j: (i, 0)), pl.BlockSpec((tn, k), lambda i, j: (j + joff, 0)), table, table],
        out_specs=pl.BlockSpec((tm, tn), lambda i, j: (i, j)),
        out_shape=jax.ShapeDtypeStruct((m, n), out_dtype),
        compiler_params=_params("parallel", "parallel"),
    )(a, bt, cos, sin)


ROW_TILE = 256


def _rows(width, col=0):
    return pl.BlockSpec((1, ROW_TILE, width), lambda b, t: (b, t, col))


def _per_batch(nrows, width):
    return pl.BlockSpec((1, nrows, width), lambda b, t: (b, 0, 0))


def _row_call(body, name, bsz, seq, in_specs, out_specs, out_shape, accumulates=False):
    return pl.pallas_call(
        body,
        name=name,
        grid=(bsz, seq // ROW_TILE),
        in_specs=in_specs,
        out_specs=out_specs,
        out_shape=out_shape,
        compiler_params=_params("parallel", "arbitrary" if accumulates else "parallel"),
    )


def _acc_rows(acc_ref, first, rows):
    @pl.when(first)
    def _():
        acc_ref[...] = jnp.zeros_like(acc_ref)

    for r, val in enumerate(rows):
        acc_ref[0, r:r + 1, :] += val


def _colsum(v):
    return jnp.sum(v, axis=0, keepdims=True)


def _ln_stats(z):
    mu = jnp.mean(z, axis=-1, keepdims=True)
    zc = z - mu
    var = jnp.mean(zc * zc, axis=-1, keepdims=True)
    rstd = lax.rsqrt(var + LN_EPS)
    return zc * rstd, rstd


def _ln_bwd(dxhat, xhat, rstd):
    m1 = jnp.mean(dxhat, axis=-1, keepdims=True)
    m2 = jnp.mean(dxhat * xhat, axis=-1, keepdims=True)
    return rstd * (dxhat - m1 - xhat * m2)


def _modulate_in(x, mod):
    bsz, seq, d = x.shape

    def body(x_ref, mod_ref, u_ref):
        u_ref[0] = (x_ref[0] * (1.0 + mod_ref[0, 1:2, :]) + mod_ref[0, 0:1, :]).astype(BF16)

    return _row_call(body, "modulate_in", bsz, seq, [_rows(d), _per_batch(8, d)], _rows(d),
                     jax.ShapeDtypeStruct((bsz, seq, d), BF16))(x, mod)


EP_TILE = 512


def _ep_specs(seq, d):
    tiles = seq // EP_TILE
    return ((EP_TILE, d), lambda i, j: (i, 0)), ((1, 8, d), lambda i, j: (i // tiles, 0, 0)), ((1, d), lambda i, j: (0, 0))


def _wo_ln1(merged, wo, x, mod, g, b, seq):
    ntok, d = x.shape
    row, per_b, whole = _ep_specs(seq, d)

    def epilogue(y, i, j, ins, outs):
        x_ref, mod_ref, g_ref, b_ref = ins
        y_ref, h_ref, u_ref = outs
        z = DEEPNORM_ALPHA * x_ref[...] + (1.0 + mod_ref[0, 2:3, :]) * y
        xhat, _ = _ln_stats(z)
        h = xhat * g_ref[...] + b_ref[...]
        y_ref[...] = y
        h_ref[...] = h
        u_ref[...] = (h * (1.0 + mod_ref[0, 4:5, :]) + mod_ref[0, 3:4, :]).astype(BF16)

    f32, bf16 = jax.ShapeDtypeStruct((ntok, d), F32), jax.ShapeDtypeStruct((ntok, d), BF16)
    return _matmul(merged, wo, mode="nn", tm=EP_TILE, tn=d, tk=d, name="w_o_ln1",
                   ins=[(x,) + row, (mod,) + per_b, (g,) + whole, (b,) + whole],
                   outs=[(f32,) + row, (f32,) + row, (bf16,) + row], epilogue=epilogue)


FF_HALF = D_FF // 2


def _interleave_gate_up(w):
    return w.reshape(2, 2, FF_HALF, w.shape[1]).transpose(1, 0, 2, 3).reshape(w.shape)


def _gate_up_silu(u2, wgut_i):
    ntok = u2.shape[0]

    def epilogue(h, i, j, ins, outs):
        h_ref, a_ref = outs
        hg, hu = h[:, :FF_HALF], h[:, FF_HALF:]
        h_ref[...] = h.astype(BF16)
        a_ref[...] = (hg * _sigmoid(hg) * hu).astype(BF16)

    return _matmul(u2, wgut_i, mode="nt", tm=EP_TILE, tn=2 * FF_HALF, tk=u2.shape[1], name="gate_up_silu",
                   outs=[(jax.ShapeDtypeStruct((ntok, 2 * D_FF), BF16), (EP_TILE, 2 * FF_HALF), lambda i, j: (i, j)),
                         (jax.ShapeDtypeStruct((ntok, D_FF), BF16), (EP_TILE, FF_HALF), lambda i, j: (i, j))],
                   epilogue=epilogue)


def _down_dgrad_silu_bwd(dy2, wd, h_i):
    ntok = dy2.shape[0]
    wide = ((EP_TILE, 2 * FF_HALF), lambda i, j: (i, j))

    def epilogue(da, i, j, ins, outs):
        h = ins[0][...].astype(F32)
        hg, hu = h[:, :FF_HALF], h[:, FF_HALF:]
        sg = _sigmoid(hg)
        outs[0][:, :FF_HALF] = (da * hu * (sg * (1.0 + hg * (1.0 - sg)))).astype(BF16)
        outs[0][:, FF_HALF:] = (da * (hg * sg)).astype(BF16)

    return _matmul(dy2, wd, mode="nt", tm=EP_TILE, tn=FF_HALF, tk=dy2.shape[1], name="down_dgrad_silu_bwd",
                   ins=[(h_i,) + wide], outs=[(jax.ShapeDtypeStruct((ntok, 2 * D_FF), BF16),) + wide], epilogue=epilogue)[0]


def _down_ln2_loss_bwd(a, wd, h1, mod, g, b, target, seq):
    ntok, d = h1.shape
    row, per_b, whole = _ep_specs(seq, d)
    tiles = seq // EP_TILE

    def epilogue(y, i, j, ins, outs):
        h_ref, mod_ref, g_ref, b_ref, t_ref = ins
        dy_ref, dh_ref, acc_ref = outs
        gate = 1.0 + mod_ref[0, 5:6, :]
        z = DEEPNORM_ALPHA * h_ref[...] + gate * y
        xhat, rstd = _ln_stats(z)
        diff = xhat * g_ref[...] + b_ref[...] - t_ref[...]
        loss = 0.5 * jnp.sum(jnp.sum(diff * diff, axis=-1, keepdims=True) / d, axis=0, keepdims=True)
        dout = diff / d
        dz = _ln_bwd(dout * g_ref[...], xhat, rstd)
        dy_ref[...] = (gate * dz).astype(BF16)
        dh_ref[...] = DEEPNORM_ALPHA * dz
        _acc_rows(acc_ref, i % tiles == 0,
                  [_colsum(dout * xhat), _colsum(dout), _colsum(dz * y), jnp.broadcast_to(loss, (1, d))])

    return _matmul(a, wd, mode="nn", tm=EP_TILE, tn=d, tk=a.shape[1], name="down_ln2_loss_bwd",
                   ins=[(h1,) + row, (mod,) + per_b, (g,) + whole, (b,) + whole, (target,) + row],
                   outs=[(jax.ShapeDtypeStruct((ntok, d), BF16),) + row, (jax.ShapeDtypeStruct((ntok, d), F32),) + row,
                         (jax.ShapeDtypeStruct((ntok // seq, 8, d), F32),) + per_b], epilogue=epilogue)


def _gate_up_dgrad_ln1_bwd(dh, wgut, dh1a, x, y1, mod, g, b, seq):
    ntok, d = x.shape
    row, per_b, whole = _ep_specs(seq, d)
    tiles = seq // EP_TILE

    def epilogue(du, i, j, ins, outs):
        dh_ref, x_ref, y_ref, mod_ref, g_ref, b_ref = ins
        dy_ref, dx_ref, acc_ref = outs
        y = y_ref[...]
        gate = 1.0 + mod_ref[0, 2:3, :]
        z = DEEPNORM_ALPHA * x_ref[...] + gate * y
        xhat, rstd = _ln_stats(z)
        h1 = xhat * g_ref[...] + b_ref[...]
        dh1 = dh_ref[...] + du * (1.0 + mod_ref[0, 4:5, :])
        dz = _ln_bwd(dh1 * g_ref[...], xhat, rstd)
        dy_ref[...] = (gate * dz).astype(BF16)
        dx_ref[...] = DEEPNORM_ALPHA * dz
        _acc_rows(acc_ref, i % tiles == 0,
                  [_colsum(dh1 * xhat), _colsum(dh1), _colsum(dz * y), _colsum(du * h1), _colsum(du)])

    return _matmul(dh, wgut, mode="nn", tm=EP_TILE, tn=d, tk=D_FF, name="gate_up_dgrad_ln1_bwd",
                   ins=[(dh1a,) + row, (x,) + row, (y1,) + row, (mod,) + per_b, (g,) + whole, (b,) + whole],
                   outs=[(jax.ShapeDtypeStruct((ntok, d), BF16),) + row, (jax.ShapeDtypeStruct((ntok, d), F32),) + row,
                         (jax.ShapeDtypeStruct((ntok // seq, 8, d), F32),) + per_b], epilogue=epilogue)


def _wo_dgrad_gate_bwd(dy1, wo, gab, ya, yb):
    ntok, d = ya.shape
    tm, tn = 1024, 512
    tile = ((tm, tn), lambda i, j: (i, j))
    tile_b = ((tm, tn), lambda i, j: (i, j + d // tn))

    def epilogue(dm_, i, j, ins, outs):
        ga_ref, gb_ref, ya_ref, yb_ref = ins
        dya_ref, dyb_ref, dga_ref, dgb_ref = outs
        sa, sb = _sigmoid(ga_ref[...].astype(F32)), _sigmoid(gb_ref[...].astype(F32))
        dya_ref[...] = (dm_ * sa).astype(BF16)
        dyb_ref[...] = (dm_ * sb).astype(BF16)
        dga_ref[...] = (dm_ * ya_ref[...].astype(F32) * sa * (1.0 - sa)).astype(BF16)
        dgb_ref[...] = (dm_ * yb_ref[...].astype(F32) * sb * (1.0 - sb)).astype(BF16)

    shp = jax.ShapeDtypeStruct((ntok, d), BF16)
    return _matmul(dy1, wo, mode="nt", tm=tm, tn=tn, tk=d, name="w_o_dgrad_gate_bwd",
                   ins=[(gab,) + tile, (gab,) + tile_b, (ya,) + tile, (yb,) + tile],
                   outs=[(shp,) + tile] * 4, epilogue=epilogue)


def _w_in_dgrad_grad_x(dproj, wint, dxa, x, mod, seq, token):
    ntok, d = x.shape
    row, per_b, _ = _ep_specs(seq, d)
    tiles = seq // EP_TILE

    def epilogue(du, i, j, ins, outs):
        dxa_ref, x_ref, mod_ref = ins
        gx_ref, acc_ref = outs
        gx_ref[...] = dxa_ref[...] + du * (1.0 + mod_ref[0, 1:2, :])
        _acc_rows(acc_ref, i % tiles == 0, [_colsum(du * x_ref[...]), _colsum(du)])

    return _matmul(dproj, wint, mode="nn", tm=EP_TILE, tn=d, tk=wint.shape[0] // 2, name="w_in_dgrad_grad_x", token=token,
                   ins=[(dxa,) + row, (x,) + row, (mod,) + per_b],
                   outs=[(jax.ShapeDtypeStruct((ntok, d), F32),) + row, (jax.ShapeDtypeStruct((ntok // seq, 8, d), F32),) + per_b],
                   epilogue=epilogue)


def _merge_branch_b_gate(os_, ls_, wbbt, gab, ya):
    ntok, d = ya.shape
    w = os_[0].shape[1]
    tm, tn = 1024, 512
    tile = ((tm, tn), lambda i, j: (i, j))
    tile_b = ((tm, tn), lambda i, j: (i, j + d // tn))
    row = ((tm, w), lambda i, j: (i, 0))

    def lhs_fn(o0_ref, ins, outs):
        os_r, ls_r = (o0_ref,) + tuple(ins[3:5]), ins[5:8]
        ls = [l[...] for l in ls_r]
        mx = jnp.maximum(jnp.maximum(ls[0], ls[1]), ls[2])
        es = [jnp.exp(l - mx) for l in ls]
        den = es[0] + es[1] + es[2]
        ob = functools.reduce(jnp.add, [(e / den) * o[...].astype(F32) for e, o in zip(es, os_r)]).astype(BF16)
        outs[2][...] = ob
        return ob

    def epilogue(yb, i, j, ins, outs):
        ga_ref, gb_ref, ya_ref = ins[:3]
        yb_ref, merged_ref = outs[:2]
        yb_ref[...] = yb.astype(BF16)
        merged_ref[...] = (_sigmoid(ga_ref[...].astype(F32)) * ya_ref[...].astype(F32)
                           + _sigmoid(gb_ref[...].astype(F32)) * yb).astype(BF16)

    shp = jax.ShapeDtypeStruct((ntok, d), BF16)
    return _matmul(os_[0], wbbt, mode="nt", tm=tm, tn=tn, tk=w, name="merge_branch_b_gate", lhs_fn=lhs_fn,
                   ins=[(gab,) + tile, (gab,) + tile_b, (ya,) + tile] + [(v,) + row for v in list(os_[1:]) + list(ls_)],
                   outs=[(shp,) + tile] * 2 + [(jax.ShapeDtypeStruct((ntok, w), BF16),) + row], epilogue=epilogue)


def _segsum64(v):
    rows, width = v.shape
    ri = lax.broadcasted_iota(jnp.int32, (LANES, LANES), 0) // HEAD_DIM
    ci = lax.broadcasted_iota(jnp.int32, (LANES, LANES), 1) // HEAD_DIM
    ones = jnp.where(ri == ci, 1.0, 0.0).astype(BF16)
    out = []
    for c in range(width // LANES):
        part = v[:, c * LANES:(c + 1) * LANES]
        hi = part.astype(BF16)
        lo = (part - hi.astype(F32)).astype(BF16)
        out.append(jnp.dot(hi, ones, preferred_element_type=F32) + jnp.dot(lo, ones, preferred_element_type=F32))
    return jnp.concatenate(out, axis=1) if len(out) > 1 else out[0]


def _branch_b_dgrad_merge_bwd(dyb, wbbt, os_, ls_):
    ntok, w = os_[0].shape
    row = ((EP_TILE, w), lambda i, j: (i, 0))

    def epilogue(dob_, i, j, ins, outs):
        os_r, ls_r = ins[:3], ins[3:]
        do_r, dd_r = outs[:3], outs[3:]
        ls = [l[...] for l in ls_r]
        mx = jnp.maximum(jnp.maximum(ls[0], ls[1]), ls[2])
        es = [jnp.exp(l - mx) for l in ls]
        den = es[0] + es[1] + es[2]
        ws = [e / den for e in es]
        dws = [_segsum64(dob_ * o[...].astype(F32)) for o in os_r]
        mean = ws[0] * dws[0] + ws[1] * dws[1] + ws[2] * dws[2]
        for wg, do_ref, dd_ref in zip(ws, do_r, dd_r):
            do_ref[...] = wg * dob_
            dd_ref[...] = -wg * mean

    shp = jax.ShapeDtypeStruct((ntok, w), F32)
    return _matmul(dyb, wbbt, mode="nn", tm=EP_TILE, tn=w, tk=dyb.shape[1], name="branch_b_dgrad_merge_bwd",
                   ins=[(v,) + row for v in list(os_) + list(ls_)], outs=[(shp,) + row] * 6, epilogue=epilogue)


def _branch_a_dgrad_delta(dya, wba, oa, lse_a, sinks_exp, seq):
    ntok, w = oa.shape
    row, per_b, whole = _ep_specs(seq, w)
    tiles = seq // EP_TILE

    def epilogue(do_, i, j, ins, outs):
        o_ref, l_ref, s_ref = ins
        do_ref, dd_ref, acc_ref = outs
        dd = -_segsum64(do_ * o_ref[...].astype(F32))
        do_ref[...] = do_
        dd_ref[...] = dd
        _acc_rows(acc_ref, i % tiles == 0, [_colsum(dd * jnp.exp(s_ref[...] - l_ref[...]))])

    shp = jax.ShapeDtypeStruct((ntok, w), F32)
    return _matmul(dya, wba, mode="nt", tm=EP_TILE, tn=w, tk=dya.shape[1], name="branch_a_dgrad_delta",
                   ins=[(oa,) + row, (lse_a,) + row, (sinks_exp,) + whole],
                   outs=[(shp,) + row, (shp,) + row, (jax.ShapeDtypeStruct((ntok // seq, 8, w), F32),) + per_b],
                   epilogue=epilogue)


def _swap_halves(v):
    src = lax.broadcasted_iota(jnp.int32, (LANES, LANES), 0)
    dst = lax.broadcasted_iota(jnp.int32, (LANES, LANES), 1)
    partner = jnp.where((dst % HEAD_DIM) < HEAD_DIM // 2, dst + HEAD_DIM // 2, dst - HEAD_DIM // 2)
    perm = jnp.where(src == partner, 1.0, 0.0).astype(BF16)
    hi = v.astype(BF16)
    lo = (v - hi.astype(F32)).astype(BF16)
    return jnp.dot(hi, perm, preferred_element_type=F32) + jnp.dot(lo, perm, preferred_element_type=F32)


def _swap_halves_roll(v):
    lane = lax.broadcasted_iota(jnp.int32, v.shape, 1)
    return jnp.where((lane % HEAD_DIM) < HEAD_DIM // 2, pltpu.roll(v, LANES - HEAD_DIM // 2, 1),
                     pltpu.roll(v, HEAD_DIM // 2, 1))


def _swap_halves_coarse(v):
    src = lax.broadcasted_iota(jnp.int32, (LANES, LANES), 0)
    dst = lax.broadcasted_iota(jnp.int32, (LANES, LANES), 1)
    partner = jnp.where((dst % HEAD_DIM) < HEAD_DIM // 2, dst + HEAD_DIM // 2, dst - HEAD_DIM // 2)
    perm = jnp.where(src == partner, 1.0, 0.0).astype(BF16)
    return jnp.dot(v.astype(BF16), perm, preferred_element_type=F32)


def _rope(v, cos, sin, sign=1.0, mxu=True, coarse=False):
    swap = (_swap_halves_coarse if coarse else _swap_halves) if mxu else _swap_halves_roll
    out = []
    for c in range(v.shape[1] // LANES):
        part = v[:, c * LANES:(c + 1) * LANES]
        out.append(part * cos + sign * (swap(part) * sin))
    return jnp.concatenate(out, axis=1) if len(out) > 1 else out[0]


def _half_mask(shape, half):
    lane = lax.broadcasted_iota(jnp.int32, shape, len(shape) - 1) % LANES
    return (lane < HEAD_DIM) if half == 0 else (lane >= HEAD_DIM)


def _dup_half(v, half):
    return jnp.where(_half_mask(v.shape, half), v, pltpu.roll(v, HEAD_DIM, 1))


def _fold_halves(v):
    return v + pltpu.roll(v, HEAD_DIM, 1)


def _pick_halves(lo_rows, hi_rows):
    return jnp.where(_half_mask(lo_rows.shape, 0), lo_rows, hi_rows)


def _stack_masked(v, pairs):
    parts = []
    for c in pairs:
        pair = v[:, c * LANES:(c + 1) * LANES]
        parts += [jnp.where(_half_mask(pair.shape, half), pair, 0.0) for half in (0, 1)]
    return jnp.concatenate(parts, axis=0)


def _stack_pair_cols(v, pairs):
    return jnp.concatenate([v[:, c * LANES + half * HEAD_DIM:c * LANES + half * HEAD_DIM + 1] for c in pairs for half in (0, 1)],
                           axis=0)


ATTN_UNITS = 16


def _class_rows(r):
    return [pl.ds(0, QBLOCK)] if r == 1 else [pl.ds(rho, QBLOCK, stride=r) for rho in range(r)]


def _band_mask(nrows, nk, blk, n_back, has_prev):
    qi = lax.broadcasted_iota(jnp.int32, (nrows, nk), 0) % QBLOCK
    ki = lax.broadcasted_iota(jnp.int32, (nrows, nk), 1)
    if has_prev:
        dist = qi + QBLOCK - ki
        return (dist >= 0) & (dist <= n_back) & ((ki >= QBLOCK) | (blk > 0))
    dist = qi - ki
    return (dist >= 0) & (dist <= n_back)


def _attn_fwd(q_arr, k_arr, v_arr, *, name, npair, gqa, q_col, k_col, v_col, nchunk, r, n_back, sinks=None):
    bsz, seq, _ = q_arr.shape
    rr = QBLOCK * r
    nblk = seq // rr
    qw = npair * LANES
    kw = LANES if gqa else qw
    has_prev = nblk > 1
    has_sink = sinks is not None
    scale = HEAD_DIM ** -0.5

    def body(*refs):
        refs = list(refs)
        q_ref, kc_ref, vc_ref = refs[:3]
        pos = 3
        if has_prev:
            kp_ref, vp_ref = refs[pos:pos + 2]
            pos += 2
        if has_sink:
            sink_ref = refs[pos]
            pos += 1
        o_ref, lse_ref = refs[pos:pos + 2]
        if r > 1:
            stage_o = refs[pos + 2]
        blk = pl.program_id(2)
        nk = (2 if has_prev else 1) * QBLOCK
        valid = _band_mask(QBLOCK, nk, blk, n_back, has_prev)
        per = npair // 2
        classes = _class_rows(r)
        step = max(1, ATTN_UNITS // (2 * npair))
        for first in range(0, len(classes), step):
            batch = classes[first:first + step]
            units = []
            for ci, rows in enumerate(batch):
                q = q_ref[0, rows, :] * scale
                k, v = kc_ref[0, rows, :], vc_ref[0, rows, :]
                if has_prev:
                    k = jnp.concatenate([kp_ref[0, rows, :], k], axis=0)
                    v = jnp.concatenate([vp_ref[0, rows, :], v], axis=0)
                if gqa:
                    kdup = [_dup_half(k, hk).astype(BF16) for hk in range(2)]
                    vdup = [_dup_half(v, hk) for hk in range(2)]
                for c in range(npair):
                    sl = slice(c * LANES, (c + 1) * LANES)
                    qc = q[:, sl]
                    kc, vc = (kdup[c // per], vdup[c // per]) if gqa else (k[:, sl].astype(BF16), v[:, sl])
                    for half in (0, 1):
                        qm = jnp.where(_half_mask(qc.shape, half), qc, 0.0).astype(BF16)
                        vm = jnp.where(_half_mask(vc.shape, half), vc, 0.0).astype(BF16)
                        s = lax.dot_general(qm, kc, _DIMS["nt"], preferred_element_type=F32)
                        units.append(dict(ci=ci, c=c, half=half, s=s, vm=vm, sk=sink_ref[2 * c + half] if has_sink else None))
            for u in units:
                s = jnp.where(valid, u["s"], NEG_INF)
                m = jnp.max(s, axis=1, keepdims=True)
                if has_sink:
                    m = jnp.maximum(m, u["sk"])
                p = jnp.exp(s - m)
                den = jnp.sum(p, axis=1, keepdims=True)
                if has_sink:
                    den = den + jnp.exp(u["sk"] - m)
                u.update(p=p.astype(BF16), den=den, lse=m + jnp.log(den))
            for u in units:
                u["o"] = jnp.dot(u["p"], u["vm"], preferred_element_type=F32) / u["den"]
            for ci, rows in enumerate(batch):
                outs, lses = [None] * npair, [None] * npair
                for u in units:
                    if u["ci"] != ci:
                        continue
                    c, o = u["c"], u["o"]
                    lse = jnp.broadcast_to(u["lse"], o.shape)
                    outs[c] = o if u["half"] == 0 else outs[c] + o
                    lses[c] = lse if u["half"] == 0 else _pick_halves(lses[c], lse)
                o_new = jnp.concatenate(outs, axis=1) if npair > 1 else outs[0]
                if r > 1:
                    stage_o[rows, :] = o_new
                else:
                    o_ref[0] = o_new.astype(BF16)
                lse_ref[0, rows, :] = jnp.concatenate(lses, axis=1) if npair > 1 else lses[0]
        if r > 1:
            o_ref[0] = stage_o[...].astype(BF16)

    def cur(width, col0):
        return pl.BlockSpec((1, rr, width), lambda b, c, i: (b, i, col0 + c))

    def prev(width, col0):
        return pl.BlockSpec((1, rr, width), lambda b, c, i: (b, jnp.maximum(i - 1, 0), col0 + c))

    in_specs = [cur(qw, q_col), cur(kw, k_col), cur(kw, v_col)]
    args = [q_arr, k_arr, v_arr]
    if has_prev:
        in_specs += [prev(kw, k_col), prev(kw, v_col)]
        args += [k_arr, v_arr]
    if has_sink:
        in_specs.append(pl.BlockSpec(memory_space=pltpu.SMEM))
        args.append(sinks)
    return pl.pallas_call(
        body,
        name=name,
        grid=(bsz, nchunk, nblk),
        in_specs=in_specs,
        out_specs=[pl.BlockSpec((1, rr, qw), lambda b, c, i: (b, i, c))] * 2,
        out_shape=[jax.ShapeDtypeStruct((bsz, seq, nchunk * qw), BF16), jax.ShapeDtypeStruct((bsz, seq, nchunk * qw), F32)],
        scratch_shapes=[pltpu.VMEM((rr, qw), F32)] if r > 1 else [],
        compiler_params=_params("parallel", "parallel", "parallel"),
    )(*args)


def _attn_bwd(q_arr, k_arr, v_arr, cos, sin, do, lse, dd, *, name, npair, gqa, q_col, k_col, v_col, nchunk, r, n_back,
              token=None):
    bsz, seq, _ = q_arr.shape
    rr = QBLOCK * r
    nblk = seq // rr
    qw = npair * LANES
    kw = LANES if gqa else qw
    has_next = nblk > 1
    has_token = token is not None
    staged = r > 1
    scale = HEAD_DIM ** -0.5

    def body(*refs):
        refs = list(refs)
        k_ref, v_ref, c_ref, s_ref = refs[:4]
        tile_refs = [refs[4:8]]
        pos = 8
        if has_next:
            tile_refs.append(refs[pos:pos + 4])
            pos += 4
        if has_token:
            pos += 1
        dq_ref, dk_ref, dv_ref = refs[pos:pos + 3]
        carry_ref, bcast_ref = refs[pos + 3:pos + 5]
        if staged:
            stage_q, stage_k, stage_v = refs[pos + 5:pos + 8]
        blk = pl.program_id(2)
        if has_next:
            @pl.when(blk == 0)
            def _():
                carry_ref[...] = jnp.zeros_like(carry_ref)

        nrows = (npair if gqa else 1) * QBLOCK
        qi = lax.broadcasted_iota(jnp.int32, (nrows, QBLOCK), 0) % QBLOCK
        ki = lax.broadcasted_iota(jnp.int32, (nrows, QBLOCK), 1)
        valids = [qi >= ki, (qi + QBLOCK - ki <= n_back) & (blk + 1 < nblk)]
        per = npair // 2
        ntile = len(tile_refs)
        cat = lambda parts: jnp.concatenate(parts, axis=1) if len(parts) > 1 else parts[0]
        classes = _class_rows(r)
        step = max(1, ATTN_UNITS // (ntile * (2 if gqa else 2 * npair)))
        def stat_cols(stat, slot):
            if gqa:
                return _stack_pair_cols(stat, list(range(slot * per, (slot + 1) * per)))
            col = slot * HEAD_DIM
            return stat[:, col:col + 1]

        nslot = 2 if gqa else 2 * npair
        if has_next:
            @pl.when(blk == 0)
            def _():
                for rows in classes:
                    for which, stat_ref in enumerate(tile_refs[0][2:4]):
                        stat = stat_ref[0, rows, :]
                        for slot in range(nslot):
                            bcast_ref[which, slot, rows if not gqa else slice(None), :] = jnp.broadcast_to(
                                stat_cols(stat, slot), (nrows, LANES))

        for first in range(0, len(classes), step):
            batch = classes[first:first + step]
            units = []
            for ci, rows in enumerate(batch):
                keep = slice(None) if gqa else rows
                tiles = [(q_ref[0, rows, :] * scale, do_ref[0, rows, :], l_ref[0, rows, :], d_ref[0, rows, :])
                         for q_ref, do_ref, l_ref, d_ref in tile_refs]

                def stats(t, slot, keep=keep, tiles=tiles):
                    if has_next and t == 0:
                        return bcast_ref[0, slot, keep, :], bcast_ref[1, slot, keep, :]
                    return tuple(jnp.broadcast_to(stat_cols(tiles[t][2 + w], slot), (nrows, LANES)) for w in range(2))

                k, v = k_ref[0, rows, :], v_ref[0, rows, :]
                if gqa:
                    for hk in range(2):
                        pairs = list(range(hk * per, (hk + 1) * per))
                        kd, vd = _dup_half(k, hk).astype(BF16), _dup_half(v, hk).astype(BF16)
                        for t, (q, do_, l_, d_) in enumerate(tiles):
                            lcol, dcol = stats(t, hk)
                            units.append(dict(ci=ci, t=t, hk=hk, slot=hk, keep=keep, pairs=pairs,
                                              qs=_stack_masked(q, pairs).astype(BF16),
                                              dos=_stack_masked(do_, pairs).astype(BF16), lcol=lcol, dcol=dcol,
                                              kmat=kd, vmat=vd, kdq=kd))
                else:
                    for c in range(npair):
                        sl = slice(c * LANES, (c + 1) * LANES)
                        kc, vcb = k[:, sl], v[:, sl].astype(BF16)
                        kcb = kc.astype(BF16)
                        for t, (q, do_, l_, d_) in enumerate(tiles):
                            for half in (0, 1):
                                hm = _half_mask(kc.shape, half)
                                lcol, dcol = stats(t, 2 * c + half)
                                units.append(dict(ci=ci, t=t, c=c, half=half, slot=2 * c + half, keep=keep,
                                                  qs=jnp.where(hm, q[:, sl], 0.0).astype(BF16),
                                                  dos=jnp.where(hm, do_[:, sl], 0.0).astype(BF16), lcol=lcol, dcol=dcol,
                                                  kmat=kcb, vmat=vcb, kdq=jnp.where(hm, kc, 0.0).astype(BF16)))
            for u in units:
                u["s"] = lax.dot_general(u["qs"], u["kmat"], _DIMS["nt"], preferred_element_type=F32)
                u["dp"] = lax.dot_general(u["dos"], u["vmat"], _DIMS["nt"], preferred_element_type=F32)
            for u in units:
                p = jnp.exp(jnp.where(valids[u["t"]], u["s"], NEG_INF) - u["lcol"])
                u["ds"] = (p * (u["dp"] + u["dcol"])).astype(BF16)
                u["p"] = p.astype(BF16)
            for u in units:
                u["dv"] = lax.dot_general(u["p"], u["dos"], _DIMS["tn"], preferred_element_type=F32)
                u["dk"] = lax.dot_general(u["ds"], u["qs"], _DIMS["tn"], preferred_element_type=F32)
                u["dq"] = jnp.dot(u["ds"], u["kdq"], preferred_element_type=F32) * scale
            for u in units:
                if u["t"] == 1:
                    bcast_ref[0, u["slot"], u["keep"], :] = u["lcol"]
                    bcast_ref[1, u["slot"], u["keep"], :] = u["dcol"]
            for ci, rows in enumerate(batch):
                mine = [u for u in units if u["ci"] == ci]
                dq = [[None] * npair for _ in range(ntile)]
                if gqa:
                    dk_out = dv_out = None
                    for hk in range(2):
                        us = [u for u in mine if u["hk"] == hk]
                        for u in us:
                            for i, c in enumerate(u["pairs"]):
                                dq[u["t"]][c] = _pick_halves(u["dq"][2 * i * QBLOCK:(2 * i + 1) * QBLOCK],
                                                             u["dq"][(2 * i + 1) * QBLOCK:(2 * i + 2) * QBLOCK])
                        dk_h = _fold_halves(functools.reduce(jnp.add, [u["dk"] for u in us]))
                        dv_h = _fold_halves(functools.reduce(jnp.add, [u["dv"] for u in us]))
                        dk_out = dk_h if hk == 0 else _pick_halves(dk_out, dk_h)
                        dv_out = dv_h if hk == 0 else _pick_halves(dv_out, dv_h)
                else:
                    dks, dvs = [], []
                    for c in range(npair):
                        us = [u for u in mine if u["c"] == c]
                        dks.append(functools.reduce(jnp.add, [u["dk"] for u in us]))
                        dvs.append(functools.reduce(jnp.add, [u["dv"] for u in us]))
                        for t in range(ntile):
                            dq[t][c] = functools.reduce(jnp.add, [u["dq"] for u in us if u["t"] == t])
                    dk_out, dv_out = cat(dks), cat(dvs)
                ck, sk_ = c_ref[0, rows, :], s_ref[0, rows, :]
                dk_new = _rope(dk_out, ck, sk_, sign=-1.0, mxu=gqa, coarse=True)
                dq_cur = cat(dq[0])
                if has_next:
                    dq_cur = dq_cur + carry_ref[rows, :]
                    carry_ref[rows, :] = cat(dq[1])
                dq_new = _rope(dq_cur, ck, sk_, sign=-1.0, mxu=gqa, coarse=True)
                if staged:
                    stage_q[rows, :], stage_k[rows, :], stage_v[rows, :] = dq_new, dk_new, dv_out
                else:
                    dq_ref[0], dk_ref[0], dv_ref[0] = dq_new.astype(BF16), dk_new.astype(BF16), dv_out.astype(BF16)
        if staged:
            dq_ref[0], dk_ref[0], dv_ref[0] = stage_q[...].astype(BF16), stage_k[...].astype(BF16), stage_v[...].astype(BF16)

    def at(width, col0, shift):
        return pl.BlockSpec((1, rr, width), lambda b, c, i: (b, jnp.minimum(i + shift, nblk - 1), col0 + c))

    in_specs = [at(kw, k_col, 0), at(kw, v_col, 0), pl.BlockSpec((1, rr, LANES), lambda b, c, i: (b, i, 0)),
                pl.BlockSpec((1, rr, LANES), lambda b, c, i: (b, i, 0))]
    args = [k_arr, v_arr, cos, sin]
    for shift in (0, 1) if has_next else (0,):
        in_specs += [at(qw, q_col, shift), at(qw, 0, shift), at(qw, 0, shift), at(qw, 0, shift)]
        args += [q_arr, do, lse, dd]
    if has_token:
        in_specs.append(pl.BlockSpec(token.shape, lambda b, c, i: (0, 0)))
        args.append(token)
    return pl.pallas_call(
        body,
        name=name,
        grid=(bsz, nchunk, nblk),
        in_specs=in_specs,
        out_specs=[pl.BlockSpec((1, rr, qw), lambda b, c, i: (b, i, c)),
                   pl.BlockSpec((1, rr, kw), lambda b, c, i: (b, i, c)),
                   pl.BlockSpec((1, rr, kw), lambda b, c, i: (b, i, c))],
        out_shape=[jax.ShapeDtypeStruct((bsz, seq, nchunk * qw), BF16),
                   jax.ShapeDtypeStruct((bsz, seq, nchunk * kw), BF16),
                   jax.ShapeDtypeStruct((bsz, seq, nchunk * kw), BF16)],
        scratch_shapes=[pltpu.VMEM((rr, qw) if has_next else (8, LANES), F32),
                        pltpu.VMEM((2, 2 if gqa else 2 * npair, npair * QBLOCK if gqa else rr, LANES) if has_next
                                   else (1, 1, 8, LANES), F32)] +
                       ([pltpu.VMEM((rr, qw), F32), pltpu.VMEM((rr, kw), F32), pltpu.VMEM((rr, kw), F32)] if staged else []),
        compiler_params=_params("parallel", "parallel", "arbitrary"),
    )(*args)


B_CHUNKS = {1: (4, 1), 4: (1, 4), 16: (1, 4)}


def _rope_tables(positions):
    half = HEAD_DIM // 2
    inv = ROPE_THETA ** (-jnp.arange(half, dtype=F32) / half)
    ang = positions.astype(F32)[..., None] * inv
    cos, sin = jnp.cos(ang), jnp.sin(ang)
    return jnp.concatenate([cos] * 4, axis=-1), jnp.concatenate([-sin, sin, -sin, sin], axis=-1)


def _layer_step(x, mod, tables, sinks, ln1_g, ln1_b, ln2_g, ln2_b, target, get_w_in, get_rest, hook):
    bsz, seq, d = x.shape
    ntok = bsz * seq
    flat = lambda v: v.reshape(ntok, v.shape[-1])
    unflat = lambda v: v.reshape(bsz, seq, v.shape[-1])
    cos, sin = tables
    mm = functools.partial(_matmul, tm=1024, tk=1024)
    scalar = lambda tok: 0.0 if tok is None else tok[0, 0]

    u1 = _modulate_in(x, mod)
    u1f = flat(u1)
    wint = get_w_in(u1)
    cosf, sinf = flat(cos), flat(sin)
    proj = functools.partial(_proj_rope, u1f, wint, cosf, sinf, tm=2048)
    qkvb = unflat(proj(n=4608, b_off=OFF_QKVB, rope_cols=3072, tn=256, name="proj_qkvb"))
    b_kws, os_, ls_ = [], [], []
    for g, (window, r) in enumerate(B_PATTERNS):
        npair, nch = B_CHUNKS[r]
        per = B_HEADS_PER_GROUP // (2 * npair)
        nsec = len(B_PATTERNS) * per
        kw_ = dict(npair=npair, gqa=False, q_col=g * per, k_col=nsec + g * per, v_col=2 * nsec + g * per, nchunk=nch, r=r,
                   n_back=window // r)
        b_kws.append(kw_)
        o_g, l_g = _attn_fwd(qkvb, qkvb, qkvb, name=f"attn_b{g}_fwd", **kw_)
        os_.append(o_g)
        ls_.append(l_g)
    tok = hook("projected", os_[-1])
    proj = functools.partial(_proj_rope, u1f, wint, cosf + scalar(tok), sinf, tm=2048)
    gab = unflat(proj(n=2048, b_off=OFF_GAB, rope_cols=0, tn=256, name="proj_gab", out_dtype=BF16))
    qa = kva = unflat(proj(n=OFF_QKVB, b_off=OFF_QA, rope_cols=OFF_KVA + LANES, tn=256, name="proj_qkva"))
    a_kw = dict(npair=A_Q_HEADS // 2, gqa=True, q_col=0, k_col=OFF_KVA // LANES, v_col=OFF_KVA // LANES + 1, nchunk=1, r=1,
                n_back=A_WINDOW - 1)
    after_gab = jnp.minimum(jnp.abs(gab[0, 0, 0].astype(F32)), 0.0)
    oa, lse_a = _attn_fwd(qa, kva, kva, name="attn_a_fwd", sinks=sinks.reshape(A_Q_HEADS) + after_gab, **a_kw)
    rest = get_rest(oa)
    wba, wbbt, wo, wgut, wd = (rest[n] for n in ("w_branch_a", "w_branch_b", "w_o", "w_gate_up", "w_down"))
    ya = unflat(mm(flat(oa), wba, mode="nn", out_dtype=BF16, tn=512, name="branch_a"))
    ybf, mergedf, obf = _merge_branch_b_gate([flat(t) for t in os_], [flat(t) for t in ls_], wbbt, flat(gab), flat(ya))
    xf = flat(x)
    y1f, h1f, u2f = _wo_ln1(mergedf, wo, xf, mod, ln1_g, ln1_b, seq)
    wgut_i = _interleave_gate_up(wgut)
    hf, af = _gate_up_silu(u2f, wgut_i)

    dy2f, dh1af, acc2 = _down_ln2_loss_bwd(af, wd, h1f, mod, ln2_g, ln2_b, flat(target), seq)
    g_wd = _matmul(af, dy2f, mode="tn", out_dtype=BF16, tm=256, tn=1024, tk=ntok, name="down_wgrad")
    dhf = _down_dgrad_silu_bwd(dy2f, wd, hf)
    g_wgut = _interleave_gate_up(_matmul(dhf, u2f, mode="tn", out_dtype=BF16, tm=256, tn=1024, tk=ntok, name="gate_up_wgrad"))
    dy1f, dxaf, acc1 = _gate_up_dgrad_ln1_bwd(dhf, wgut_i, dh1af, xf, y1f, mod, ln1_g, ln1_b, seq)
    g_wo = _matmul(mergedf, dy1f, mode="tn", out_dtype=BF16, tm=256, tn=1024, tk=ntok, name="w_o_wgrad")
    dyaf, dybf, dgaf, dgbf = _wo_dgrad_gate_bwd(dy1f, wo, flat(gab), flat(ya), ybf)
    g_wba = _matmul(flat(oa), dyaf, mode="tn", out_dtype=BF16, tm=256, tn=1024, tk=ntok, name="branch_a_wgrad")
    g_wbbt = _matmul(dybf, obf, mode="tn", out_dtype=BF16, tm=256, tn=512, tk=ntok, name="branch_b_wgrad")
    tok = hook("grads_rest", dict(w_branch_a=g_wba, w_branch_b=g_wbbt, w_o=g_wo, w_gate_up=g_wgut, w_down=g_wd))

    sinks_exp = jnp.repeat(sinks.reshape(1, A_Q_HEADS), HEAD_DIM, axis=1) + scalar(tok)
    doa, dd_a, acc_s = _branch_a_dgrad_delta(dyaf, wba, flat(oa), flat(lse_a), sinks_exp, seq)
    doa, dd_a = unflat(doa), unflat(dd_a)
    tok = hook("delta_done", dd_a)
    dqa, dka, dva = _attn_bwd(qa, kva, kva, cos, sin, doa, lse_a, dd_a, name="attn_a_bwd", token=tok, **a_kw)
    merged_bwd = [unflat(t) for t in _branch_b_dgrad_merge_bwd(dybf, wbbt, [flat(t) for t in os_], [flat(t) for t in ls_])]
    dqs, dks, dvs = [], [], []
    for g in range(len(B_PATTERNS)):
        dq_g, dk_g, dv_g = _attn_bwd(qkvb, qkvb, qkvb, cos, sin, merged_bwd[g], ls_[g], merged_bwd[3 + g],
                                     name=f"attn_b{g}_bwd", **b_kws[g])
        dqs.append(dq_g)
        dks.append(dk_g)
        dvs.append(dv_g)
    dproj = jnp.concatenate([t.astype(BF16) for t in [dqa, dka, dva] + dqs + dks + dvs] + [unflat(dgaf), unflat(dgbf)], axis=-1)
    dprojf = flat(dproj)
    g_wint = _matmul(dprojf, u1f, mode="tn", out_dtype=BF16, tm=256, tn=1024, tk=ntok, name="w_in_wgrad")
    tok = hook("grads_w_in", dict(w_in=g_wint))
    grad_x, acc0 = _w_in_dgrad_grad_x(dprojf, wint, dxaf, xf, mod, seq, tok)
    grad_x = unflat(grad_x)
    tok = hook("dgrad_done", grad_x)

    loss_part = jnp.sum(acc2[:, 3, 0])
    dmod = jnp.stack([acc0[:, 1], acc0[:, 0], acc1[:, 2], acc1[:, 4], acc1[:, 3], acc2[:, 2]], axis=1)
    small = jnp.stack([acc1[:, 0].sum(0), acc1[:, 1].sum(0), acc2[:, 0].sum(0), acc2[:, 1].sum(0), acc_s[:, 0].sum(0)])
    small = small + scalar(tok)
    return loss_part, grad_x, dmod, small


CHIP_FLIPS = (2, 4, 6)


def _my_place():
    return lax.axis_index("x"), lax.axis_index("y"), lax.axis_index("c")


def _flip(place, k):
    px, py, pc = place
    return (1 - px if k & 4 else px, 1 - py if k & 2 else py, 1 - pc if k & 1 else pc)


def _index(place):
    return 4 * place[0] + 2 * place[1] + place[2]


def _gather_small(v, name):
    rows, cols = v.shape

    def body(v_ref, out_ref, send_sems, recv_sems):
        me = _my_place()
        out_ref[_index(me)] = v_ref[...]
        copies = []
        for k in range(1, N_DEV):
            copies.append(pltpu.make_async_remote_copy(
                src_ref=v_ref, dst_ref=out_ref.at[_index(me)], send_sem=send_sems.at[k - 1], recv_sem=recv_sems.at[k - 1],
                device_id=_flip(me, k), device_id_type=MESH))
        for cp in copies:
            cp.start()
        for k in range(1, N_DEV):
            pltpu.make_async_remote_copy(
                src_ref=v_ref, dst_ref=out_ref.at[_index(_flip(me, k))], send_sem=send_sems.at[k - 1],
                recv_sem=recv_sems.at[k - 1], device_id=_flip(me, k), device_id_type=MESH).wait_recv()
        for cp in copies:
            cp.wait_send()

    return pl.pallas_call(
        body,
        name=name,
        out_shape=jax.ShapeDtypeStruct((N_DEV, rows, cols), v.dtype),
        in_specs=[pl.BlockSpec(memory_space=pltpu.VMEM)],
        out_specs=pl.BlockSpec(memory_space=pltpu.VMEM),
        scratch_shapes=[pltpu.SemaphoreType.DMA((N_DEV - 1,)), pltpu.SemaphoreType.DMA((N_DEV - 1,))],
        compiler_params=pltpu.CompilerParams(vmem_limit_bytes=VMEM_LIMIT_BYTES),
    )(v)


_HBM = pl.BlockSpec(memory_space=pltpu.HBM)
_SEM = pl.BlockSpec(memory_space=pltpu.SEMAPHORE)
_EFFECT = pltpu.SideEffectType.DATAFLOW_SIDE_EFFECTING


def _remote(src, dst, send_sems, recv_sems, j, to):
    return pltpu.make_async_remote_copy(src_ref=src, dst_ref=dst, send_sem=send_sems.at[j], recv_sem=recv_sems.at[j],
                                        device_id=to, device_id_type=MESH)


def _copies_start(name, bufs, make_copies, nsem):
    nbuf = len(bufs)

    def body(*refs):
        for cp in make_copies(refs[:nbuf], refs[nbuf], refs[nbuf + 1]):
            cp.start()
        refs[-1][...] = jnp.zeros_like(refs[-1])

    sems = pltpu.SemaphoreType.DMA((nsem,))
    res = pl.pallas_call(
        body, name=name,
        out_shape=(sems, sems, *[pltpu.HBM(v.shape, v.dtype) for v in bufs], jax.ShapeDtypeStruct((8, LANES), F32)),
        in_specs=(_HBM,) * nbuf, out_specs=(_SEM, _SEM) + (_HBM,) * nbuf + (pl.BlockSpec(memory_space=pltpu.VMEM),),
        input_output_aliases={i: 2 + i for i in range(nbuf)},
        compiler_params=pltpu.CompilerParams(has_side_effects=_EFFECT),
    )(*[pltpu.with_memory_space_constraint(v, pltpu.HBM) for v in bufs])
    return res[0], res[1], list(res[2:2 + nbuf]), res[-1]


def _copies_wait(name, started, make_copies, after):
    send_sems, recv_sems, bufs, _ = started
    nbuf = len(bufs)

    def body(*refs):
        for cp in make_copies(refs[:nbuf], refs[nbuf], refs[nbuf + 1]):
            cp.wait_send()
            cp.wait_recv()

    return list(pl.pallas_call(
        body, name=name,
        out_shape=tuple(pltpu.HBM(v.shape, v.dtype) for v in bufs),
        in_specs=(_HBM,) * nbuf + (_SEM, _SEM, pl.BlockSpec(memory_space=pl.ANY)), out_specs=(_HBM,) * nbuf,
        input_output_aliases={i: i for i in range(nbuf)},
        compiler_params=pltpu.CompilerParams(has_side_effects=_EFFECT),
    )(*bufs, send_sems, recv_sems, after))


def _to_sibling_copies(refs, send_sems, recv_sems):
    src_ref, land_ref = refs
    me = _my_place()
    return [_remote(src_ref.at[q, 1 - me[2]], land_ref.at[q], send_sems, recv_sems, q, _flip(me, 1)) for q in range(4)]


def _to_chips_copies(refs, send_sems, recv_sems):
    src_ref, land_ref = refs
    me = _my_place()
    copies = []
    for j, k in enumerate(CHIP_FLIPS):
        to = _flip(me, k)
        copies.append(_remote(src_ref.at[2 * to[0] + to[1]], land_ref.at[j], send_sems, recv_sems, j, to))
    return copies


class _Gather:
    def __init__(self, name, blocks):
        self.name, self.n = name, len(blocks)
        at_me = (_index(_my_place()), 0, 0)
        lands = [lax.dynamic_update_slice(lax.empty((N_DEV,) + v.shape, v.dtype), v[None], at_me) for v in blocks]
        self.first = _copies_start(name + "_start", list(blocks) + lands, self._first_copies, 4 * self.n)
        self.token = self.first[3]

    def _first_copies(self, refs, send_sems, recv_sems):
        me = _my_place()
        return [_remote(refs[w], refs[self.n + w].at[_index(me)], send_sems, recv_sems, 4 * w + j, _flip(me, k))
                for w in range(self.n) for j, k in enumerate((1,) + CHIP_FLIPS)]

    def _pass_copies(self, refs, send_sems, recv_sems):
        me = _my_place()
        copies = []
        for w, land in enumerate(refs):
            for j, k in enumerate(CHIP_FLIPS):
                slot = land.at[_index(_flip(me, k))]
                copies.append(_remote(slot, slot, send_sems, recv_sems, 3 * w + j, _flip(me, 1)))
        return copies

    def pass_on(self, after):
        lands = _copies_wait(self.name + "_wait", self.first, self._first_copies, after)[self.n:]
        self.second = _copies_start(self.name + "_pass_start", lands, self._pass_copies, 3 * self.n)
        return self.second[3]

    def finish(self, after):
        return _copies_wait(self.name + "_pass_wait", self.second, self._pass_copies, after)


SUM_SPLIT = 2


def _sum_pairs(parts, theirs):
    nchip, _, rows, cols = parts.shape
    tile = rows // SUM_SPLIT

    def body(c_ref, a_ref, b_ref, o_ref):
        o_ref[...] = (a_ref[0].astype(F32) + b_ref[...].astype(F32)).astype(BF16)

    spec = pl.BlockSpec((1, tile, cols), lambda q, t, c_ref: (q, t, 0))
    grid_spec = pltpu.PrefetchScalarGridSpec(
        num_scalar_prefetch=1, grid=(nchip, SUM_SPLIT),
        in_specs=[pl.BlockSpec((1, 1, tile, cols), lambda q, t, c_ref: (q, c_ref[0], t, 0)), spec], out_specs=spec)
    return pl.pallas_call(body, name="grad_sum_sibling", grid_spec=grid_spec,
                          out_shape=jax.ShapeDtypeStruct((nchip, rows, cols), BF16),
                          compiler_params=_params("parallel", "parallel"))(lax.axis_index("c").reshape(1), parts, theirs)


def _sum_final(chip_sum, got):
    _, rows, cols = chip_sum.shape
    tile = rows // SUM_SPLIT

    def body(q_ref, a_ref, g_ref, o_ref):
        o_ref[...] = ((a_ref[0].astype(F32) + g_ref[0].astype(F32)) + g_ref[1].astype(F32)) + g_ref[2].astype(F32)

    grid_spec = pltpu.PrefetchScalarGridSpec(
        num_scalar_prefetch=1, grid=(SUM_SPLIT,),
        in_specs=[pl.BlockSpec((1, tile, cols), lambda t, q_ref: (q_ref[0], t, 0)),
                  pl.BlockSpec((3, tile, cols), lambda t, q_ref: (0, t, 0))],
        out_specs=pl.BlockSpec((tile, cols), lambda t, q_ref: (t, 0)))
    my_chip = (2 * lax.axis_index("x") + lax.axis_index("y")).reshape(1)
    return pl.pallas_call(body, name="grad_sum_chips", grid_spec=grid_spec, out_shape=jax.ShapeDtypeStruct((rows, cols), F32),
                          compiler_params=_params("parallel"))(my_chip, chip_sum, got)


class _ReduceScatter:
    def __init__(self, name, slabs):
        self.name, self.rows = name, slabs.shape[1]
        parts = slabs.reshape(4, 2, self.rows, D_MODEL)
        self.first = _copies_start(name + "_sibling_start", [parts, lax.empty((4, self.rows, D_MODEL), slabs.dtype)],
                                   _to_sibling_copies, 4)
        self.token = self.first[3]

    def between_chips(self, after):
        parts, theirs = _copies_wait(self.name + "_sibling_wait", self.first, _to_sibling_copies, after)
        chip_sum = _sum_pairs(parts, theirs)
        self.second = _copies_start(self.name + "_chips_start", [chip_sum, lax.empty((3, self.rows, D_MODEL), chip_sum.dtype)],
                                    _to_chips_copies, 3)
        return self.second[3]

    def finish(self, after):
        chip_sum, got = _copies_wait(self.name + "_chips_wait", self.second, _to_chips_copies, after)
        return _sum_final(chip_sum, got)


def _ada_fwd(c_all, w, b):
    nb, _ = c_all.shape
    ncol = w.shape[1]

    def body(c_ref, w_ref, b_ref, o_ref):
        c = c_ref[...]
        act = (c * _sigmoid(c)).astype(BF16)
        o_ref[...] = jnp.dot(act, w_ref[...].astype(BF16), preferred_element_type=F32) + b_ref[...]

    return pl.pallas_call(body, name="ada_fwd", out_shape=jax.ShapeDtypeStruct((nb, ncol), F32),
                          compiler_params=pltpu.CompilerParams(vmem_limit_bytes=VMEM_LIMIT_BYTES))(c_all, w, b)


def _ada_wgrad(c_all_t, dmod_cols):
    d, nb = c_all_t.shape
    ncol = dmod_cols.shape[1]

    def body(ct_ref, dm_ref, o_ref):
        ct = ct_ref[...]
        act = (ct * _sigmoid(ct)).astype(BF16).astype(F32)
        dm = dm_ref[...].astype(BF16).astype(F32)
        acc = act[:, 0:1] * dm[0:1, :]
        for i in range(1, nb):
            acc = acc + act[:, i:i + 1] * dm[i:i + 1, :]
        o_ref[...] = acc

    return pl.pallas_call(body, name="ada_wgrad", out_shape=jax.ShapeDtypeStruct((d, ncol), F32),
                          compiler_params=pltpu.CompilerParams(vmem_limit_bytes=VMEM_LIMIT_BYTES))(c_all_t, dmod_cols)


SMALL_ROWS = 24


def _reduce_small(gathered):
    def body(g_ref, o_ref):
        acc = g_ref[0]
        for dev in range(1, N_DEV):
            acc = acc + g_ref[dev]
        o_ref[...] = acc

    return pl.pallas_call(body, name="reduce_small", out_shape=jax.ShapeDtypeStruct(gathered.shape[1:], F32))(gathered)


def _adamw_math(w, g, m, v):
    nm = ADAM_B1 * m + (1.0 - ADAM_B1) * g
    nv = ADAM_B2 * v + (1.0 - ADAM_B2) * (g * g)
    bc1 = 1.0 - ADAM_B1 ** ADAM_STEP
    bc2 = 1.0 - ADAM_B2 ** ADAM_STEP
    return -ADAM_LR * ((nm / bc1) / (jnp.sqrt(nv / bc2) + ADAM_EPS) + ADAM_WD * w), nm, nv


def _adamw_small(ws, gs, ms, vs, name):
    n = len(ws)

    def body(*refs):
        for i in range(n):
            res = _adamw_math(*(refs[k * n + i][...] for k in range(4)))
            for k in range(3):
                refs[(4 + k) * n + i][...] = res[k]

    shapes = [jax.ShapeDtypeStruct(w.shape, F32) for w in ws]
    res = pl.pallas_call(body, name=name, out_shape=shapes * 3)(*ws, *gs, *ms, *vs)
    return [(res[i], res[n + i], res[2 * n + i]) for i in range(n)]


def _adamw(w, g, m, v, name):
    rows, cols = w.shape
    tile = rows
    for cand in range(min(rows // 2, 512) // 8 * 8, 7, -8):
        if rows % cand == 0:
            tile = cand
            break
    spec = pl.BlockSpec((tile, cols), lambda t: (t, 0))

    def body(w_ref, g_ref, m_ref, v_ref, d_ref, nm_ref, nv_ref):
        d_ref[...], nm_ref[...], nv_ref[...] = _adamw_math(w_ref[...], g_ref[...], m_ref[...], v_ref[...])

    shp = jax.ShapeDtypeStruct((rows, cols), F32)
    return pl.pallas_call(body, name=name, grid=(rows // tile,), in_specs=[spec] * 4, out_specs=[spec] * 3, out_shape=[shp] * 3,
                          compiler_params=_params("parallel"))(w, g, m, v)


_WEIGHTS = ("w_ada", "b_ada", "w_in", "sinks", "w_branch_a", "w_branch_b", "w_o", "ln1_g", "ln1_b", "w_gate_up", "w_down",
            "ln2_g", "ln2_b")
_TRANSPOSED = ("w_in", "w_branch_b", "w_gate_up")


def _pack_shard(name, w):
    w = w.astype(BF16)
    if name in _TRANSPOSED:
        w = w.T
    return w.reshape(-1, D_MODEL)


def _unpack_full(name, slab):
    if name == "w_branch_b":
        return slab.reshape(N_DEV * 128, 512)
    return slab.reshape(-1, D_MODEL)


def _unpack_group(group, gathered):
    return {n: _unpack_full(n, slab) for (n, _), slab in zip(group, gathered)}


def _unpack_grads(group, g_packed):
    g_w, off = {}, 0
    for n, r in group:
        part = g_packed[off:off + r]
        off += r
        g_w[n] = part.reshape(128, 512) if n == "w_branch_b" else part
    return g_w


def kernel(x, c, positions, w_ada, b_ada, w_in, sinks, w_branch_a, w_branch_b, w_o, ln1_g, ln1_b, w_gate_up, w_down, ln2_g, ln2_b, loss_target, m_w_ada, m_b_ada, m_w_in, m_sinks, m_w_branch_a, m_w_branch_b, m_w_o, m_ln1_g, m_ln1_b, m_w_gate_up, m_w_down, m_ln2_g, m_ln2_b, v_w_ada, v_b_ada, v_w_in, v_sinks, v_w_branch_a, v_w_branch_b, v_w_o, v_ln1_g, v_ln1_b, v_w_gate_up, v_w_down, v_ln2_g, v_ln2_b):
    weights = dict(w_ada=w_ada, b_ada=b_ada, w_in=w_in, sinks=sinks, w_branch_a=w_branch_a, w_branch_b=w_branch_b, w_o=w_o,
                   ln1_g=ln1_g, ln1_b=ln1_b, w_gate_up=w_gate_up, w_down=w_down, ln2_g=ln2_g, ln2_b=ln2_b)
    m_in = dict(w_ada=m_w_ada, b_ada=m_b_ada, w_in=m_w_in, sinks=m_sinks, w_branch_a=m_w_branch_a, w_branch_b=m_w_branch_b,
                w_o=m_w_o, ln1_g=m_ln1_g, ln1_b=m_ln1_b, w_gate_up=m_w_gate_up, w_down=m_w_down, ln2_g=m_ln2_g, ln2_b=m_ln2_b)
    v_in = dict(w_ada=v_w_ada, b_ada=v_b_ada, w_in=v_w_in, sinks=v_sinks, w_branch_a=v_w_branch_a, w_branch_b=v_w_branch_b,
                w_o=v_w_o, ln1_g=v_ln1_g, ln1_b=v_ln1_b, w_gate_up=v_w_gate_up, w_down=v_w_down, ln2_g=v_ln2_g, ln2_b=v_ln2_b)
    bsz = x.shape[0]
    me = _index(_my_place())
    ada_cols = w_ada.shape[2]
    outs = {}

    def adamw(n, g):
        w2, m2, v2 = (t[n][0] if t[n].ndim == 3 else t[n] for t in (weights, m_in, v_in))
        shape = weights[n].shape
        if n in _TRANSPOSED:
            dlt, nm, nv = _adamw(w2.T, g, m2.T, v2.T, "adamw_" + n)
            outs[n] = tuple(t.T.reshape(shape) for t in (g, dlt, nm, nv))
        else:
            dlt, nm, nv = _adamw(w2, g, m2, v2, "adamw_" + n)
            outs[n] = tuple(t.reshape(shape) for t in (g, dlt, nm, nv))
        return nv

    packed_in = [_pack_shard(n, weights[n][0]) for n, _ in GROUP_IN]
    packed_rest = [_pack_shard(n, weights[n][0]) for n, _ in GROUP_REST]
    c_all = _gather_small(jnp.pad(c, ((0, 8 - bsz), (0, 0))), "gather_c")[:, :bsz].reshape(N_DEV * bsz, D_MODEL)
    gather_in = _Gather("gather_w_in", lax.optimization_barrier((packed_in, c_all))[0])
    b_cols = lax.dynamic_slice_in_dim(b_ada, me * ada_cols, ada_cols, axis=1)
    mod_cols = _ada_fwd(c_all, w_ada[0], b_cols + gather_in.token[0, 0])
    tables = _rope_tables(positions)
    mod_cols, tables, packed_rest = lax.optimization_barrier((mod_cols, tables, packed_rest))
    mod_all = _gather_small(mod_cols, "gather_mod").transpose(1, 0, 2).reshape(N_DEV * bsz, 6, D_MODEL)
    gather_rest = _Gather("gather_rest", lax.optimization_barrier((packed_rest, mod_all))[0])
    mod = jnp.pad(lax.dynamic_slice_in_dim(mod_all, me * bsz, bsz, axis=0), ((0, 0), (0, 2), (0, 0)))
    mod = mod + gather_rest.token[0, 0]
    mod = mod + gather_in.pass_on(mod)[0, 0]

    scatters = {}

    def get_w_in(after):
        return _unpack_group(GROUP_IN, gather_in.finish(after))["w_in"]

    def get_rest(after):
        return _unpack_group(GROUP_REST, gather_rest.finish(after))

    def pack_grads(group, grads):
        return jnp.concatenate([grads[n].reshape(N_DEV, r, D_MODEL) for n, r in group], axis=1)

    def hook(point, value):
        if point == "projected":
            return gather_rest.pass_on(value)
        if point == "grads_rest":
            scatters["rest"] = _ReduceScatter("scatter_rest", pack_grads(GROUP_REST, value))
            return scatters["rest"].token
        if point == "delta_done":
            return scatters["rest"].between_chips(value)
        if point == "grads_w_in":
            scatters["in"] = _ReduceScatter("scatter_w_in", pack_grads(GROUP_IN, value))
            return scatters["in"].token
        if point == "dgrad_done":
            return None
        raise ValueError(point)

    loss_part, grad_x, dmod, small = _layer_step(x, mod, tables, sinks[0], ln1_g, ln1_b, ln2_g, ln2_b, loss_target,
                                                 get_w_in, get_rest, hook)

    rows = jnp.concatenate([dmod.reshape(bsz * 6, D_MODEL), small, jnp.full((1, D_MODEL), loss_part, F32),
                            jnp.zeros((SMALL_ROWS - bsz * 6 - 6, D_MODEL), F32)], axis=0)
    small_all = _gather_small(rows, "gather_small")
    small_all = small_all + scatters["in"].between_chips(small_all)[0, 0]
    sums = _reduce_small(small_all)
    loss = sums[bsz * 6 + 5, 0]
    dmod_all = small_all[:, :bsz * 6].reshape(N_DEV * bsz, 6 * D_MODEL)
    small_g = {"b_ada": functools.reduce(jnp.add, [sums[6 * i:6 * i + 6] for i in range(bsz)]).reshape(1, 6 * D_MODEL),
               "sinks": sums[bsz * 6 + 4][::HEAD_DIM][None]}
    small_g.update({n: sums[bsz * 6 + i][None] for i, n in enumerate(("ln1_g", "ln1_b", "ln2_g", "ln2_b"))})
    names = list(small_g)
    for n, (dlt, nm, nv) in zip(names, _adamw_small([weights[n] for n in names], [small_g[n] for n in names],
                                                     [m_in[n] for n in names], [v_in[n] for n in names], "adamw_small")):
        outs[n] = (small_g[n], dlt, nm, nv)
    dmod_cols = lax.dynamic_slice_in_dim(dmod_all, me * ada_cols, ada_cols, axis=1)
    last = adamw("w_ada", _ada_wgrad(c_all.T, dmod_cols))
    for n, g in _unpack_grads(GROUP_REST, scatters["rest"].finish(last)).items():
        adamw(n, g)
    done = lax.optimization_barrier(tuple(outs[n][3] for n in outs))
    for n, g in _unpack_grads(GROUP_IN, scatters["in"].finish(done[0])).items():
        adamw(n, g)

    return (loss, grad_x, *[outs[n][0] for n in _WEIGHTS], *[outs[n][1] for n in _WEIGHTS], *[outs[n][2] for n in _WEIGHTS],
            *[outs[n][3] for n in _WEIGHTS])
```

```python
import functools

import jax
import jax.numpy as jnp
from jax import lax
from jax.experimental import pallas as pl
from jax.experimental.pallas import tpu as pltpu

F32 = jnp.float32
BF16 = jnp.bfloat16

D_MODEL = 1024
HEAD_DIM = 64
A_Q_HEADS = 16
A_WINDOW = 128
B_PATTERNS = ((128, 1), (512, 4), (2048, 16))
B_HEADS_PER_GROUP = 8
D_FF = 2816
QBLOCK = 128
ROPE_THETA = 10000.0
LN_EPS = 1e-5
DEEPNORM_ALPHA = 2.0 ** 0.25
NEG_INF = -1e30
ADAM_LR, ADAM_B1, ADAM_B2, ADAM_EPS, ADAM_WD, ADAM_STEP = 0.001, 0.9, 0.999, 1e-08, 0.01, 10

N_DEV = 8
LANES = 128
VMEM_LIMIT_BYTES = 56 * 1024 * 1024
MESH = pl.DeviceIdType.MESH

OFF_QA, OFF_KVA, OFF_QKVB, OFF_GAB = 0, 1024, 1280, 5888
GROUP_IN = (("w_in", 992),)
GROUP_REST = (("w_branch_a", 128), ("w_branch_b", 64), ("w_o", 128), ("w_gate_up", 704), ("w_down", 352))


def _params(*sem):
    return pltpu.CompilerParams(dimension_semantics=sem, vmem_limit_bytes=VMEM_LIMIT_BYTES)


def _sigmoid(x):
    return 1.0 / (1.0 + jnp.exp(-x))


_DIMS = {"nn": (((1,), (0,)), ((), ())), "nt": (((1,), (1,)), ((), ())), "tn": (((0,), (0,)), ((), ()))}


def _matmul(a, b, *, mode, tm, tn, tk, name, out_dtype=None, n=None, b_off=0, token=None, ins=(), outs=None, epilogue=None,
            lhs_fn=None):
    if mode == "nn":
        (m, k), nn_ = a.shape, b.shape[1]
    elif mode == "nt":
        (m, k), nn_ = a.shape, (b.shape[0] if n is None else n)
    else:
        (k, m), nn_ = a.shape, b.shape[1]
    assert m % tm == 0 and nn_ % tn == 0 and k % tk == 0 and b_off % tn == 0, (name, m, nn_, k)
    nk = k // tk
    joff = b_off // tn
    if mode == "nn":
        a_spec = pl.BlockSpec((tm, tk), lambda i, j, kk: (i, kk))
        b_spec = pl.BlockSpec((tk, tn), lambda i, j, kk: (kk, j))
    elif mode == "nt":
        a_spec = pl.BlockSpec((tm, tk), lambda i, j, kk: (i, kk))
        b_spec = pl.BlockSpec((tn, tk), lambda i, j, kk: (j + joff, kk))
    else:
        a_spec = pl.BlockSpec((tk, tm), lambda i, j, kk: (kk, i))
        b_spec = pl.BlockSpec((tk, tn), lambda i, j, kk: (kk, j))
    dims = _DIMS[mode]
    has_token = token is not None
    plain = epilogue is None
    if plain:
        outs = [(jax.ShapeDtypeStruct((m, nn_), out_dtype), (tm, tn), lambda i, j: (i, j))]

        def epilogue(acc, i, j, in_refs, out_refs):
            out_refs[0][...] = acc.astype(out_refs[0].dtype)

    nin = len(ins)
    nscratch = 1 if lhs_fn is None else 2
    assert lhs_fn is None or nk == 1

    def body(*refs):
        a_ref, b_ref = refs[:2]
        in_refs = refs[2:2 + nin]
        out_refs = refs[2 + nin + has_token:-nscratch]
        acc_ref = refs[-nscratch]
        kk = pl.program_id(2)
        if lhs_fn is None:
            lhs = a_ref[...].astype(BF16)
        else:
            lhs_ref = refs[-1]

            @pl.when(pl.program_id(1) == 0)
            def _():
                lhs_ref[...] = lhs_fn(a_ref, in_refs, out_refs)

            lhs = lhs_ref[...]
        part = lax.dot_general(lhs, b_ref[...].astype(BF16), dims, preferred_element_type=F32)

        def finish(acc):
            epilogue(acc, pl.program_id(0), pl.program_id(1), in_refs, out_refs)

        if nk == 1:
            finish(part)
        else:
            @pl.when(kk == 0)
            def _():
                acc_ref[...] = part

            @pl.when(kk > 0)
            def _():
                acc_ref[...] += part

            @pl.when(kk == nk - 1)
            def _():
                finish(acc_ref[...])

    def spec(block, index):
        return pl.BlockSpec(block, lambda i, j, kk: index(i, j))

    in_specs, args = [a_spec, b_spec], [a, b]
    for arr, block, index in ins:
        in_specs.append(spec(block, index))
        args.append(arr)
    if has_token:
        in_specs.append(pl.BlockSpec(token.shape, lambda i, j, kk: (0, 0)))
        args.append(token)
    res = pl.pallas_call(
        body,
        name=name,
        grid=(m // tm, nn_ // tn, nk),
        in_specs=in_specs,
        out_specs=[spec(block, index) for _, block, index in outs],
        out_shape=[shape for shape, _, _ in outs],
        scratch_shapes=[pltpu.VMEM((tm, tn) if nk > 1 else (8, LANES), F32)] + ([] if lhs_fn is None else [pltpu.VMEM((tm, tk), BF16)]),
        compiler_params=_params("arbitrary", "arbitrary", "arbitrary"),
    )(*args)
    return res[0] if plain else res


def _proj_rope(a, bt, cos, sin, *, n, b_off, rope_cols, tm, tn, name, out_dtype=F32):
    m, k = a.shape
    assert m % tm == 0 and n % tn == 0 and b_off % tn == 0 and rope_cols % LANES == 0, name
    joff = b_off // tn
    nrope, part = divmod(rope_cols, tn)

    def body(a_ref, b_ref, c_ref, s_ref, o_ref):
        acc = lax.dot_general(a_ref[...], b_ref[...], _DIMS["nt"], preferred_element_type=F32)
        j = pl.program_id(1)

        @pl.when(j < nrope)
        def _():
            o_ref[...] = _rope(acc, c_ref[...], s_ref[...], coarse=True).astype(o_ref.dtype)

        if part:
            @pl.when(j == nrope)
            def _():
                o_ref[:, :part] = _rope(acc[:, :part], c_ref[...], s_ref[...], coarse=True).astype(o_ref.dtype)
                o_ref[:, part:] = acc[:, part:].astype(o_ref.dtype)

        @pl.when(j >= nrope + (1 if part else 0))
        def _():
            o_ref[...] = acc.astype(o_ref.dtype)

    table = pl.BlockSpec((tm, LANES), lambda i, j: (i, 0))
    return pl.pallas_call(
        body,
        name=name,
        grid=(m // tm, n // tn),
        in_specs=[pl.BlockSpec((tm, k), lambda i, j: (i, 0)), pl.BlockSpec((tn, k), lambda i, j: (j + joff, 0)), table, table],
        out_specs=pl.BlockSpec((tm, tn), lambda i, j: (i, j)),
        out_shape=jax.ShapeDtypeStruct((m, n), out_dtype),
        compiler_params=_params("parallel", "parallel"),
    )(a, bt, cos, sin)


ROW_TILE = 256


def _rows(width, col=0):
    return pl.BlockSpec((1, ROW_TILE, width), lambda b, t: (b, t, col))


def _per_batch(nrows, width):
    return pl.BlockSpec((1, nrows, width), lambda b, t: (b, 0, 0))


def _row_call(body, name, bsz, seq, in_specs, out_specs, out_shape, accumulates=False):
    return pl.pallas_call(
        body,
        name=name,
        grid=(bsz, seq // ROW_TILE),
        in_specs=in_specs,
        out_specs=out_specs,
        out_shape=out_shape,
        compiler_params=_params("parallel", "arbitrary" if accumulates else "parallel"),
    )


def _acc_rows(acc_ref, first, rows):
    @pl.when(first)
    def _():
        acc_ref[...] = jnp.zeros_like(acc_ref)

    for r, val in enumerate(rows):
        acc_ref[0, r:r + 1, :] += val


def _colsum(v):
    return jnp.sum(v, axis=0, keepdims=True)


def _ln_stats(z):
    mu = jnp.mean(z, axis=-1, keepdims=True)
    zc = z - mu
    var = jnp.mean(zc * zc, axis=-1, keepdims=True)
    rstd = lax.rsqrt(var + LN_EPS)
    return zc * rstd, rstd


def _ln_bwd(dxhat, xhat, rstd):
    m1 = jnp.mean(dxhat, axis=-1, keepdims=True)
    m2 = jnp.mean(dxhat * xhat, axis=-1, keepdims=True)
    return rstd * (dxhat - m1 - xhat * m2)


def _modulate_in(x, mod):
    bsz, seq, d = x.shape

    def body(x_ref, mod_ref, u_ref):
        u_ref[0] = (x_ref[0] * (1.0 + mod_ref[0, 1:2, :]) + mod_ref[0, 0:1, :]).astype(BF16)

    return _row_call(body, "modulate_in", bsz, seq, [_rows(d), _per_batch(8, d)], _rows(d),
                     jax.ShapeDtypeStruct((bsz, seq, d), BF16))(x, mod)


EP_TILE = 512


def _ep_specs(seq, d):
    tiles = seq // EP_TILE
    return ((EP_TILE, d), lambda i, j: (i, 0)), ((1, 8, d), lambda i, j: (i // tiles, 0, 0)), ((1, d), lambda i, j: (0, 0))


def _wo_ln1(merged, wo, x, mod, g, b, seq):
    ntok, d = x.shape
    row, per_b, whole = _ep_specs(seq, d)

    def epilogue(y, i, j, ins, outs):
        x_ref, mod_ref, g_ref, b_ref = ins
        y_ref, h_ref, u_ref = outs
        z = DEEPNORM_ALPHA * x_ref[...] + (1.0 + mod_ref[0, 2:3, :]) * y
        xhat, _ = _ln_stats(z)
        h = xhat * g_ref[...] + b_ref[...]
        y_ref[...] = y
        h_ref[...] = h
        u_ref[...] = (h * (1.0 + mod_ref[0, 4:5, :]) + mod_ref[0, 3:4, :]).astype(BF16)

    f32, bf16 = jax.ShapeDtypeStruct((ntok, d), F32), jax.ShapeDtypeStruct((ntok, d), BF16)
    return _matmul(merged, wo, mode="nn", tm=EP_TILE, tn=d, tk=d, name="w_o_ln1",
                   ins=[(x,) + row, (mod,) + per_b, (g,) + whole, (b,) + whole],
                   outs=[(f32,) + row, (f32,) + row, (bf16,) + row], epilogue=epilogue)


FF_HALF = D_FF // 2


def _interleave_gate_up(w):
    return w.reshape(2, 2, FF_HALF, w.shape[1]).transpose(1, 0, 2, 3).reshape(w.shape)


def _gate_up_silu(u2, wgut_i):
    ntok = u2.shape[0]

    def epilogue(h, i, j, ins, outs):
        h_ref, a_ref = outs
        hg, hu = h[:, :FF_HALF], h[:, FF_HALF:]
        h_ref[...] = h.astype(BF16)
        a_ref[...] = (hg * _sigmoid(hg) * hu).astype(BF16)

    return _matmul(u2, wgut_i, mode="nt", tm=EP_TILE, tn=2 * FF_HALF, tk=u2.shape[1], name="gate_up_silu",
                   outs=[(jax.ShapeDtypeStruct((ntok, 2 * D_FF), BF16), (EP_TILE, 2 * FF_HALF), lambda i, j: (i, j)),
                         (jax.ShapeDtypeStruct((ntok, D_FF), BF16), (EP_TILE, FF_HALF), lambda i, j: (i, j))],
                   epilogue=epilogue)


def _down_dgrad_silu_bwd(dy2, wd, h_i):
    ntok = dy2.shape[0]
    wide = ((EP_TILE, 2 * FF_HALF), lambda i, j: (i, j))

    def epilogue(da, i, j, ins, outs):
        h = ins[0][...].astype(F32)
        hg, hu = h[:, :FF_HALF], h[:, FF_HALF:]
        sg = _sigmoid(hg)
        outs[0][:, :FF_HALF] = (da * hu * (sg * (1.0 + hg * (1.0 - sg)))).astype(BF16)
        outs[0][:, FF_HALF:] = (da * (hg * sg)).astype(BF16)

    return _matmul(dy2, wd, mode="nt", tm=EP_TILE, tn=FF_HALF, tk=dy2.shape[1], name="down_dgrad_silu_bwd",
                   ins=[(h_i,) + wide], outs=[(jax.ShapeDtypeStruct((ntok, 2 * D_FF), BF16),) + wide], epilogue=epilogue)[0]


def _down_ln2_loss_bwd(a, wd, h1, mod, g, b, target, seq):
    ntok, d = h1.shape
    row, per_b, whole = _ep_specs(seq, d)
    tiles = seq // EP_TILE

    def epilogue(y, i, j, ins, outs):
        h_ref, mod_ref, g_ref, b_ref, t_ref = ins
        dy_ref, dh_ref, acc_ref = outs
        gate = 1.0 + mod_ref[0, 5:6, :]
        z = DEEPNORM_ALPHA * h_ref[...] + gate * y
        xhat, rstd = _ln_stats(z)
        diff = xhat * g_ref[...] + b_ref[...] - t_ref[...]
        loss = 0.5 * jnp.sum(jnp.sum(diff * diff, axis=-1, keepdims=True) / d, axis=0, keepdims=True)
        dout = diff / d
        dz = _ln_bwd(dout * g_ref[...], xhat, rstd)
        dy_ref[...] = (gate * dz).astype(BF16)
        dh_ref[...] = DEEPNORM_ALPHA * dz
        _acc_rows(acc_ref, i % tiles == 0,
                  [_colsum(dout * xhat), _colsum(dout), _colsum(dz * y), jnp.broadcast_to(loss, (1, d))])

    return _matmul(a, wd, mode="nn", tm=EP_TILE, tn=d, tk=a.shape[1], name="down_ln2_loss_bwd",
                   ins=[(h1,) + row, (mod,) + per_b, (g,) + whole, (b,) + whole, (target,) + row],
                   outs=[(jax.ShapeDtypeStruct((ntok, d), BF16),) + row, (jax.ShapeDtypeStruct((ntok, d), F32),) + row,
                         (jax.ShapeDtypeStruct((ntok // seq, 8, d), F32),) + per_b], epilogue=epilogue)


def _gate_up_dgrad_ln1_bwd(dh, wgut, dh1a, x, y1, mod, g, b, seq):
    ntok, d = x.shape
    row, per_b, whole = _ep_specs(seq, d)
    tiles = seq // EP_TILE

    def epilogue(du, i, j, ins, outs):
        dh_ref, x_ref, y_ref, mod_ref, g_ref, b_ref = ins
        dy_ref, dx_ref, acc_ref = outs
        y = y_ref[...]
        gate = 1.0 + mod_ref[0, 2:3, :]
        z = DEEPNORM_ALPHA * x_ref[...] + gate * y
        xhat, rstd = _ln_stats(z)
        h1 = xhat * g_ref[...] + b_ref[...]
        dh1 = dh_ref[...] + du * (1.0 + mod_ref[0, 4:5, :])
        dz = _ln_bwd(dh1 * g_ref[...], xhat, rstd)
        dy_ref[...] = (gate * dz).astype(BF16)
        dx_ref[...] = DEEPNORM_ALPHA * dz
        _acc_rows(acc_ref, i % tiles == 0,
                  [_colsum(dh1 * xhat), _colsum(dh1), _colsum(dz * y), _colsum(du * h1), _colsum(du)])

    return _matmul(dh, wgut, mode="nn", tm=EP_TILE, tn=d, tk=D_FF, name="gate_up_dgrad_ln1_bwd",
                   ins=[(dh1a,) + row, (x,) + row, (y1,) + row, (mod,) + per_b, (g,) + whole, (b,) + whole],
                   outs=[(jax.ShapeDtypeStruct((ntok, d), BF16),) + row, (jax.ShapeDtypeStruct((ntok, d), F32),) + row,
                         (jax.ShapeDtypeStruct((ntok // seq, 8, d), F32),) + per_b], epilogue=epilogue)


def _wo_dgrad_gate_bwd(dy1, wo, gab, ya, yb):
    ntok, d = ya.shape
    tm, tn = 1024, 512
    tile = ((tm, tn), lambda i, j: (i, j))
    tile_b = ((tm, tn), lambda i, j: (i, j + d // tn))

    def epilogue(dm_, i, j, ins, outs):
        ga_ref, gb_ref, ya_ref, yb_ref = ins
        dya_ref, dyb_ref, dga_ref, dgb_ref = outs
        sa, sb = _sigmoid(ga_ref[...].astype(F32)), _sigmoid(gb_ref[...].astype(F32))
        dya_ref[...] = (dm_ * sa).astype(BF16)
        dyb_ref[...] = (dm_ * sb).astype(BF16)
        dga_ref[...] = (dm_ * ya_ref[...].astype(F32) * sa * (1.0 - sa)).astype(BF16)
        dgb_ref[...] = (dm_ * yb_ref[...].astype(F32) * sb * (1.0 - sb)).astype(BF16)

    shp = jax.ShapeDtypeStruct((ntok, d), BF16)
    return _matmul(dy1, wo, mode="nt", tm=tm, tn=tn, tk=d, name="w_o_dgrad_gate_bwd",
                   ins=[(gab,) + tile, (gab,) + tile_b, (ya,) + tile, (yb,) + tile],
                   outs=[(shp,) + tile] * 4, epilogue=epilogue)


def _w_in_dgrad_grad_x(dproj, wint, dxa, x, mod, seq, token):
    ntok, d = x.shape
    row, per_b, _ = _ep_specs(seq, d)
    tiles = seq // EP_TILE

    def epilogue(du, i, j, ins, outs):
        dxa_ref, x_ref, mod_ref = ins
        gx_ref, acc_ref = outs
        gx_ref[...] = dxa_ref[...] + du * (1.0 + mod_ref[0, 1:2, :])
        _acc_rows(acc_ref, i % tiles == 0, [_colsum(du * x_ref[...]), _colsum(du)])

    return _matmul(dproj, wint, mode="nn", tm=EP_TILE, tn=d, tk=wint.shape[0] // 2, name="w_in_dgrad_grad_x", token=token,
                   ins=[(dxa,) + row, (x,) + row, (mod,) + per_b],
                   outs=[(jax.ShapeDtypeStruct((ntok, d), F32),) + row, (jax.ShapeDtypeStruct((ntok // seq, 8, d), F32),) + per_b],
                   epilogue=epilogue)


def _merge_branch_b_gate(os_, ls_, wbbt, gab, ya):
    ntok, d = ya.shape
    w = os_[0].shape[1]
    tm, tn = 1024, 512
    tile = ((tm, tn), lambda i, j: (i, j))
    tile_b = ((tm, tn), lambda i, j: (i, j + d // tn))
    row = ((tm, w), lambda i, j: (i, 0))

    def lhs_fn(o0_ref, ins, outs):
        os_r, ls_r = (o0_ref,) + tuple(ins[3:5]), ins[5:8]
        ls = [l[...] for l in ls_r]
        mx = jnp.maximum(jnp.maximum(ls[0], ls[1]), ls[2])
        es = [jnp.exp(l - mx) for l in ls]
        den = es[0] + es[1] + es[2]
        ob = functools.reduce(jnp.add, [(e / den) * o[...].astype(F32) for e, o in zip(es, os_r)]).astype(BF16)
        outs[2][...] = ob
        return ob

    def epilogue(yb, i, j, ins, outs):
        ga_ref, gb_ref, ya_ref = ins[:3]
        yb_ref, merged_ref = outs[:2]
        yb_ref[...] = yb.astype(BF16)
        merged_ref[...] = (_sigmoid(ga_ref[...].astype(F32)) * ya_ref[...].astype(F32)
                           + _sigmoid(gb_ref[...].astype(F32)) * yb).astype(BF16)

    shp = jax.ShapeDtypeStruct((ntok, d), BF16)
    return _matmul(os_[0], wbbt, mode="nt", tm=tm, tn=tn, tk=w, name="merge_branch_b_gate", lhs_fn=lhs_fn,
                   ins=[(gab,) + tile, (gab,) + tile_b, (ya,) + tile] + [(v,) + row for v in list(os_[1:]) + list(ls_)],
                   outs=[(shp,) + tile] * 2 + [(jax.ShapeDtypeStruct((ntok, w), BF16),) + row], epilogue=epilogue)


def _segsum64(v):
    rows, width = v.shape
    ri = lax.broadcasted_iota(jnp.int32, (LANES, LANES), 0) // HEAD_DIM
    ci = lax.broadcasted_iota(jnp.int32, (LANES, LANES), 1) // HEAD_DIM
    ones = jnp.where(ri == ci, 1.0, 0.0).astype(BF16)
    out = []
    for c in range(width // LANES):
        part = v[:, c * LANES:(c + 1) * LANES]
        hi = part.astype(BF16)
        lo = (part - hi.astype(F32)).astype(BF16)
        out.append(jnp.dot(hi, ones, preferred_element_type=F32) + jnp.dot(lo, ones, preferred_element_type=F32))
    return jnp.concatenate(out, axis=1) if len(out) > 1 else out[0]


def _branch_b_dgrad_merge_bwd(dyb, wbbt, os_, ls_):
    ntok, w = os_[0].shape
    row = ((EP_TILE, w), lambda i, j: (i, 0))

    def epilogue(dob_, i, j, ins, outs):
        os_r, ls_r = ins[:3], ins[3:]
        do_r, dd_r = outs[:3], outs[3:]
        ls = [l[...] for l in ls_r]
        mx = jnp.maximum(jnp.maximum(ls[0], ls[1]), ls[2])
        es = [jnp.exp(l - mx) for l in ls]
        den = es[0] + es[1] + es[2]
        ws = [e / den for e in es]
        dws = [_segsum64(dob_ * o[...].astype(F32)) for o in os_r]
        mean = ws[0] * dws[0] + ws[1] * dws[1] + ws[2] * dws[2]
        for wg, do_ref, dd_ref in zip(ws, do_r, dd_r):
            do_ref[...] = wg * dob_
            dd_ref[...] = -wg * mean

    shp = jax.ShapeDtypeStruct((ntok, w), F32)
    return _matmul(dyb, wbbt, mode="nn", tm=EP_TILE, tn=w, tk=dyb.shape[1], name="branch_b_dgrad_merge_bwd",
                   ins=[(v,) + row for v in list(os_) + list(ls_)], outs=[(shp,) + row] * 6, epilogue=epilogue)


def _branch_a_dgrad_delta(dya, wba, oa, lse_a, sinks_exp, seq):
    ntok, w = oa.shape
    row, per_b, whole = _ep_specs(seq, w)
    tiles = seq // EP_TILE

    def epilogue(do_, i, j, ins, outs):
        o_ref, l_ref, s_ref = ins
        do_ref, dd_ref, acc_ref = outs
        dd = -_segsum64(do_ * o_ref[...].astype(F32))
        do_ref[...] = do_.astype(BF16)
        dd_ref[...] = dd
        _acc_rows(acc_ref, i % tiles == 0, [_colsum(dd * jnp.exp(s_ref[...] - l_ref[...]))])

    shp = jax.ShapeDtypeStruct((ntok, w), F32)
    return _matmul(dya, wba, mode="nt", tm=EP_TILE, tn=w, tk=dya.shape[1], name="branch_a_dgrad_delta",
                   ins=[(oa,) + row, (lse_a,) + row, (sinks_exp,) + whole],
                   outs=[(jax.ShapeDtypeStruct((ntok, w), BF16),) + row, (shp,) + row,
                         (jax.ShapeDtypeStruct((ntok // seq, 8, w), F32),) + per_b],
                   epilogue=epilogue)


def _swap_halves(v):
    src = lax.broadcasted_iota(jnp.int32, (LANES, LANES), 0)
    dst = lax.broadcasted_iota(jnp.int32, (LANES, LANES), 1)
    partner = jnp.where((dst % HEAD_DIM) < HEAD_DIM // 2, dst + HEAD_DIM // 2, dst - HEAD_DIM // 2)
    perm = jnp.where(src == partner, 1.0, 0.0).astype(BF16)
    hi = v.astype(BF16)
    lo = (v - hi.astype(F32)).astype(BF16)
    return jnp.dot(hi, perm, preferred_element_type=F32) + jnp.dot(lo, perm, preferred_element_type=F32)


def _swap_halves_roll(v):
    lane = lax.broadcasted_iota(jnp.int32, v.shape, 1)
    return jnp.where((lane % HEAD_DIM) < HEAD_DIM // 2, pltpu.roll(v, LANES - HEAD_DIM // 2, 1),
                     pltpu.roll(v, HEAD_DIM // 2, 1))


def _swap_halves_coarse(v):
    src = lax.broadcasted_iota(jnp.int32, (LANES, LANES), 0)
    dst = lax.broadcasted_iota(jnp.int32, (LANES, LANES), 1)
    partner = jnp.where((dst % HEAD_DIM) < HEAD_DIM // 2, dst + HEAD_DIM // 2, dst - HEAD_DIM // 2)
    perm = jnp.where(src == partner, 1.0, 0.0).astype(BF16)
    return jnp.dot(v.astype(BF16), perm, preferred_element_type=F32)


def _rope(v, cos, sin, sign=1.0, mxu=True, coarse=False):
    swap = (_swap_halves_coarse if coarse else _swap_halves) if mxu else _swap_halves_roll
    out = []
    for c in range(v.shape[1] // LANES):
        part = v[:, c * LANES:(c + 1) * LANES]
        out.append(part * cos + sign * (swap(part) * sin))
    return jnp.concatenate(out, axis=1) if len(out) > 1 else out[0]


def _half_mask(shape, half):
    lane = lax.broadcasted_iota(jnp.int32, shape, len(shape) - 1) % LANES
    return (lane < HEAD_DIM) if half == 0 else (lane >= HEAD_DIM)


def _dup_half(v, half):
    return jnp.where(_half_mask(v.shape, half), v, pltpu.roll(v, HEAD_DIM, 1))


def _fold_halves(v):
    return v + pltpu.roll(v, HEAD_DIM, 1)


def _pick_halves(lo_rows, hi_rows):
    return jnp.where(_half_mask(lo_rows.shape, 0), lo_rows, hi_rows)


def _stack_masked(v, pairs):
    parts = []
    for c in pairs:
        pair = v[:, c * LANES:(c + 1) * LANES]
        parts += [jnp.where(_half_mask(pair.shape, half), pair, 0.0) for half in (0, 1)]
    return jnp.concatenate(parts, axis=0)


def _stack_pair_cols(v, pairs):
    return jnp.concatenate([v[:, c * LANES + half * HEAD_DIM:c * LANES + half * HEAD_DIM + 1] for c in pairs for half in (0, 1)],
                           axis=0)


ATTN_UNITS = 16


def _class_rows(r):
    return [pl.ds(0, QBLOCK)] if r == 1 else [pl.ds(rho, QBLOCK, stride=r) for rho in range(r)]


def _band_mask(nrows, nk, blk, n_back, has_prev):
    qi = lax.broadcasted_iota(jnp.int32, (nrows, nk), 0) % QBLOCK
    ki = lax.broadcasted_iota(jnp.int32, (nrows, nk), 1)
    if has_prev:
        dist = qi + QBLOCK - ki
        return (dist >= 0) & (dist <= n_back) & ((ki >= QBLOCK) | (blk > 0))
    dist = qi - ki
    return (dist >= 0) & (dist <= n_back)


def _attn_fwd(q_arr, k_arr, v_arr, *, name, npair, gqa, q_col, k_col, v_col, nchunk, r, n_back, sinks=None):
    bsz, seq, _ = q_arr.shape
    rr = QBLOCK * r
    nblk = seq // rr
    qw = npair * LANES
    kw = LANES if gqa else qw
    has_prev = nblk > 1
    has_sink = sinks is not None
    scale = HEAD_DIM ** -0.5

    def body(*refs):
        refs = list(refs)
        q_ref, kc_ref, vc_ref = refs[:3]
        pos = 3
        if has_prev:
            kp_ref, vp_ref = refs[pos:pos + 2]
            pos += 2
        if has_sink:
            sink_ref = refs[pos]
            pos += 1
        o_ref, lse_ref = refs[pos:pos + 2]
        if r > 1:
            stage_o = refs[pos + 2]
        blk = pl.program_id(2)
        nk = (2 if has_prev else 1) * QBLOCK
        valid = _band_mask(QBLOCK, nk, blk, n_back, has_prev)
        per = npair // 2
        classes = _class_rows(r)
        step = max(1, ATTN_UNITS // (2 * npair))
        for first in range(0, len(classes), step):
            batch = classes[first:first + step]
            units = []
            for ci, rows in enumerate(batch):
                q = q_ref[0, rows, :] * scale
                k, v = kc_ref[0, rows, :], vc_ref[0, rows, :]
                if has_prev:
                    k = jnp.concatenate([kp_ref[0, rows, :], k], axis=0)
                    v = jnp.concatenate([vp_ref[0, rows, :], v], axis=0)
                if gqa:
                    kdup = [_dup_half(k, hk).astype(BF16) for hk in range(2)]
                    vdup = [_dup_half(v, hk) for hk in range(2)]
                for c in range(npair):
                    sl = slice(c * LANES, (c + 1) * LANES)
                    qc = q[:, sl]
                    kc, vc = (kdup[c // per], vdup[c // per]) if gqa else (k[:, sl].astype(BF16), v[:, sl])
                    for half in (0, 1):
                        qm = jnp.where(_half_mask(qc.shape, half), qc, 0.0).astype(BF16)
                        vm = jnp.where(_half_mask(vc.shape, half), vc, 0.0).astype(BF16)
                        s = lax.dot_general(qm, kc, _DIMS["nt"], preferred_element_type=F32)
                        units.append(dict(ci=ci, c=c, half=half, s=s, vm=vm, sk=sink_ref[2 * c + half] if has_sink else None))
            for u in units:
                s = jnp.where(valid, u["s"], NEG_INF)
                m = jnp.max(s, axis=1, keepdims=True)
                if has_sink:
                    m = jnp.maximum(m, u["sk"])
                p = jnp.exp(s - m)
                den = jnp.sum(p, axis=1, keepdims=True)
                if has_sink:
                    den = den + jnp.exp(u["sk"] - m)
                u.update(p=p.astype(BF16), den=den, lse=m + jnp.log(den))
            for u in units:
                u["o"] = jnp.dot(u["p"], u["vm"], preferred_element_type=F32) / u["den"]
            for ci, rows in enumerate(batch):
                outs, lses = [None] * npair, [None] * npair
                for u in units:
                    if u["ci"] != ci:
                        continue
                    c, o = u["c"], u["o"]
                    lse = jnp.broadcast_to(u["lse"], o.shape)
                    outs[c] = o if u["half"] == 0 else outs[c] + o
                    lses[c] = lse if u["half"] == 0 else _pick_halves(lses[c], lse)
                o_new = jnp.concatenate(outs, axis=1) if npair > 1 else outs[0]
                if r > 1:
                    stage_o[rows, :] = o_new
                else:
                    o_ref[0] = o_new.astype(BF16)
                lse_ref[0, rows, :] = jnp.concatenate(lses, axis=1) if npair > 1 else lses[0]
        if r > 1:
            o_ref[0] = stage_o[...].astype(BF16)

    def cur(width, col0):
        return pl.BlockSpec((1, rr, width), lambda b, c, i: (b, i, col0 + c))

    def prev(width, col0):
        return pl.BlockSpec((1, rr, width), lambda b, c, i: (b, jnp.maximum(i - 1, 0), col0 + c))

    in_specs = [cur(qw, q_col), cur(kw, k_col), cur(kw, v_col)]
    args = [q_arr, k_arr, v_arr]
    if has_prev:
        in_specs += [prev(kw, k_col), prev(kw, v_col)]
        args += [k_arr, v_arr]
    if has_sink:
        in_specs.append(pl.BlockSpec(memory_space=pltpu.SMEM))
        args.append(sinks)
    return pl.pallas_call(
        body,
        name=name,
        grid=(bsz, nchunk, nblk),
        in_specs=in_specs,
        out_specs=[pl.BlockSpec((1, rr, qw), lambda b, c, i: (b, i, c))] * 2,
        out_shape=[jax.ShapeDtypeStruct((bsz, seq, nchunk * qw), BF16), jax.ShapeDtypeStruct((bsz, seq, nchunk * qw), F32)],
        scratch_shapes=[pltpu.VMEM((rr, qw), F32)] if r > 1 else [],
        compiler_params=_params("parallel", "parallel", "parallel"),
    )(*args)


def _attn_bwd(q_arr, k_arr, v_arr, cos, sin, do, lse, dd, *, name, npair, gqa, q_col, k_col, v_col, nchunk, r, n_back,
              token=None):
    bsz, seq, _ = q_arr.shape
    rr = QBLOCK * r
    nblk = seq // rr
    qw = npair * LANES
    kw = LANES if gqa else qw
    has_next = nblk > 1
    has_token = token is not None
    staged = r > 1
    scale = HEAD_DIM ** -0.5

    def body(*refs):
        refs = list(refs)
        k_ref, v_ref, c_ref, s_ref = refs[:4]
        tile_refs = [refs[4:8]]
        pos = 8
        if has_next:
            tile_refs.append(refs[pos:pos + 4])
            pos += 4
        if has_token:
            pos += 1
        dq_ref, dk_ref, dv_ref = refs[pos:pos + 3]
        carry_ref, bcast_ref = refs[pos + 3:pos + 5]
        if staged:
            stage_q, stage_k, stage_v = refs[pos + 5:pos + 8]
        blk = pl.program_id(2)
        if has_next:
            @pl.when(blk == 0)
            def _():
                carry_ref[...] = jnp.zeros_like(carry_ref)

        nrows = (npair if gqa else 1) * QBLOCK
        qi = lax.broadcasted_iota(jnp.int32, (nrows, QBLOCK), 0) % QBLOCK
        ki = lax.broadcasted_iota(jnp.int32, (nrows, QBLOCK), 1)
        valids = [qi >= ki, (qi + QBLOCK - ki <= n_back) & (blk + 1 < nblk)]
        per = npair // 2
        ntile = len(tile_refs)
        cat = lambda parts: jnp.concatenate(parts, axis=1) if len(parts) > 1 else parts[0]
        classes = _class_rows(r)
        step = max(1, ATTN_UNITS // (ntile * (2 if gqa else 2 * npair)))
        def stat_cols(stat, slot):
            if gqa:
                return _stack_pair_cols(stat, list(range(slot * per, (slot + 1) * per)))
            col = slot * HEAD_DIM
            return stat[:, col:col + 1]

        nslot = 2 if gqa else 2 * npair
        if has_next:
            @pl.when(blk == 0)
            def _():
                for rows in classes:
                    for which, stat_ref in enumerate(tile_refs[0][2:4]):
                        stat = stat_ref[0, rows, :]
                        for slot in range(nslot):
                            bcast_ref[which, slot, rows if not gqa else slice(None), :] = jnp.broadcast_to(
                                stat_cols(stat, slot), (nrows, LANES))

        for first in range(0, len(classes), step):
            batch = classes[first:first + step]
            units = []
            for ci, rows in enumerate(batch):
                keep = slice(None) if gqa else rows
                tiles = [(q_ref[0, rows, :] * scale, do_ref[0, rows, :], l_ref[0, rows, :], d_ref[0, rows, :])
                         for q_ref, do_ref, l_ref, d_ref in tile_refs]

                def stats(t, slot, keep=keep, tiles=tiles):
                    if has_next and t == 0:
                        return bcast_ref[0, slot, keep, :], bcast_ref[1, slot, keep, :]
                    return tuple(jnp.broadcast_to(stat_cols(tiles[t][2 + w], slot), (nrows, LANES)) for w in range(2))

                k, v = k_ref[0, rows, :], v_ref[0, rows, :]
                if gqa:
                    for hk in range(2):
                        pairs = list(range(hk * per, (hk + 1) * per))
                        kd, vd = _dup_half(k, hk).astype(BF16), _dup_half(v, hk).astype(BF16)
                        for t, (q, do_, l_, d_) in enumerate(tiles):
                            lcol, dcol = stats(t, hk)
                            units.append(dict(ci=ci, t=t, hk=hk, slot=hk, keep=keep, pairs=pairs,
                                              qs=_stack_masked(q, pairs).astype(BF16),
                                              dos=_stack_masked(do_, pairs).astype(BF16), lcol=lcol, dcol=dcol,
                                              kmat=kd, vmat=vd, kdq=kd))
                else:
                    for c in range(npair):
                        sl = slice(c * LANES, (c + 1) * LANES)
                        kc, vcb = k[:, sl], v[:, sl].astype(BF16)
                        kcb = kc.astype(BF16)
                        for t, (q, do_, l_, d_) in enumerate(tiles):
                            for half in (0, 1):
                                hm = _half_mask(kc.shape, half)
                                lcol, dcol = stats(t, 2 * c + half)
                                units.append(dict(ci=ci, t=t, c=c, half=half, slot=2 * c + half, keep=keep,
                                                  qs=jnp.where(hm, q[:, sl], 0.0).astype(BF16),
                                                  dos=jnp.where(hm, do_[:, sl], 0.0).astype(BF16), lcol=lcol, dcol=dcol,
                                                  kmat=kcb, vmat=vcb, kdq=jnp.where(hm, kc, 0.0).astype(BF16)))
            for u in units:
                u["s"] = lax.dot_general(u["qs"], u["kmat"], _DIMS["nt"], preferred_element_type=F32)
                u["dp"] = lax.dot_general(u["dos"], u["vmat"], _DIMS["nt"], preferred_element_type=F32)
            for u in units:
                p = jnp.exp(jnp.where(valids[u["t"]], u["s"], NEG_INF) - u["lcol"])
                u["ds"] = (p * (u["dp"] + u["dcol"])).astype(BF16)
                u["p"] = p.astype(BF16)
            for u in units:
                u["dv"] = lax.dot_general(u["p"], u["dos"], _DIMS["tn"], preferred_element_type=F32)
                u["dk"] = lax.dot_general(u["ds"], u["qs"], _DIMS["tn"], preferred_element_type=F32)
                u["dq"] = jnp.dot(u["ds"], u["kdq"], preferred_element_type=F32) * scale
            for u in units:
                if u["t"] == 1:
                    bcast_ref[0, u["slot"], u["keep"], :] = u["lcol"]
                    bcast_ref[1, u["slot"], u["keep"], :] = u["dcol"]
            for ci, rows in enumerate(batch):
                mine = [u for u in units if u["ci"] == ci]
                dq = [[None] * npair for _ in range(ntile)]
                if gqa:
                    dk_out = dv_out = None
                    for hk in range(2):
                        us = [u for u in mine if u["hk"] == hk]
                        for u in us:
                            for i, c in enumerate(u["pairs"]):
                                dq[u["t"]][c] = _pick_halves(u["dq"][2 * i * QBLOCK:(2 * i + 1) * QBLOCK],
                                                             u["dq"][(2 * i + 1) * QBLOCK:(2 * i + 2) * QBLOCK])
                        dk_h = _fold_halves(functools.reduce(jnp.add, [u["dk"] for u in us]))
                        dv_h = _fold_halves(functools.reduce(jnp.add, [u["dv"] for u in us]))
                        dk_out = dk_h if hk == 0 else _pick_halves(dk_out, dk_h)
                        dv_out = dv_h if hk == 0 else _pick_halves(dv_out, dv_h)
                else:
                    dks, dvs = [], []
                    for c in range(npair):
                        us = [u for u in mine if u["c"] == c]
                        dks.append(functools.reduce(jnp.add, [u["dk"] for u in us]))
                        dvs.append(functools.reduce(jnp.add, [u["dv"] for u in us]))
                        for t in range(ntile):
                            dq[t][c] = functools.reduce(jnp.add, [u["dq"] for u in us if u["t"] == t])
                    dk_out, dv_out = cat(dks), cat(dvs)
                ck, sk_ = c_ref[0, rows, :], s_ref[0, rows, :]
                dk_new = _rope(dk_out, ck, sk_, sign=-1.0, mxu=gqa, coarse=True)
                dq_cur = cat(dq[0])
                if has_next:
                    dq_cur = dq_cur + carry_ref[rows, :]
                    carry_ref[rows, :] = cat(dq[1])
                dq_new = _rope(dq_cur, ck, sk_, sign=-1.0, mxu=gqa, coarse=True)
                if staged:
                    stage_q[rows, :], stage_k[rows, :], stage_v[rows, :] = dq_new, dk_new, dv_out
                else:
                    dq_ref[0], dk_ref[0], dv_ref[0] = dq_new.astype(BF16), dk_new.astype(BF16), dv_out.astype(BF16)
        if staged:
            dq_ref[0], dk_ref[0], dv_ref[0] = stage_q[...].astype(BF16), stage_k[...].astype(BF16), stage_v[...].astype(BF16)

    def at(width, col0, shift):
        return pl.BlockSpec((1, rr, width), lambda b, c, i: (b, jnp.minimum(i + shift, nblk - 1), col0 + c))

    in_specs = [at(kw, k_col, 0), at(kw, v_col, 0), pl.BlockSpec((1, rr, LANES), lambda b, c, i: (b, i, 0)),
                pl.BlockSpec((1, rr, LANES), lambda b, c, i: (b, i, 0))]
    args = [k_arr, v_arr, cos, sin]
    for shift in (0, 1) if has_next else (0,):
        in_specs += [at(qw, q_col, shift), at(qw, 0, shift), at(qw, 0, shift), at(qw, 0, shift)]
        args += [q_arr, do, lse, dd]
    if has_token:
        in_specs.append(pl.BlockSpec(token.shape, lambda b, c, i: (0, 0)))
        args.append(token)
    return pl.pallas_call(
        body,
        name=name,
        grid=(bsz, nchunk, nblk),
        in_specs=in_specs,
        out_specs=[pl.BlockSpec((1, rr, qw), lambda b, c, i: (b, i, c)),
                   pl.BlockSpec((1, rr, kw), lambda b, c, i: (b, i, c)),
                   pl.BlockSpec((1, rr, kw), lambda b, c, i: (b, i, c))],
        out_shape=[jax.ShapeDtypeStruct((bsz, seq, nchunk * qw), BF16),
                   jax.ShapeDtypeStruct((bsz, seq, nchunk * kw), BF16),
                   jax.ShapeDtypeStruct((bsz, seq, nchunk * kw), BF16)],
        scratch_shapes=[pltpu.VMEM((rr, qw) if has_next else (8, LANES), F32),
                        pltpu.VMEM((2, 2 if gqa else 2 * npair, npair * QBLOCK if gqa else rr, LANES) if has_next
                                   else (1, 1, 8, LANES), F32)] +
                       ([pltpu.VMEM((rr, qw), F32), pltpu.VMEM((rr, kw), F32), pltpu.VMEM((rr, kw), F32)] if staged else []),
        compiler_params=_params("parallel", "parallel", "arbitrary"),
    )(*args)


B_CHUNKS = {1: (4, 1), 4: (1, 4), 16: (1, 4)}


def _rope_tables(positions):
    half = HEAD_DIM // 2
    inv = ROPE_THETA ** (-jnp.arange(half, dtype=F32) / half)
    ang = positions.astype(F32)[..., None] * inv
    cos, sin = jnp.cos(ang), jnp.sin(ang)
    return jnp.concatenate([cos] * 4, axis=-1), jnp.concatenate([-sin, sin, -sin, sin], axis=-1)


def _layer_step(x, mod, tables, sinks, ln1_g, ln1_b, ln2_g, ln2_b, target, get_w_in, get_rest, hook):
    bsz, seq, d = x.shape
    ntok = bsz * seq
    flat = lambda v: v.reshape(ntok, v.shape[-1])
    unflat = lambda v: v.reshape(bsz, seq, v.shape[-1])
    cos, sin = tables
    mm = functools.partial(_matmul, tm=1024, tk=1024)
    scalar = lambda tok: 0.0 if tok is None else tok[0, 0]

    u1 = _modulate_in(x, mod)
    u1f = flat(u1)
    wint = get_w_in(u1)
    cosf, sinf = flat(cos), flat(sin)
    proj = functools.partial(_proj_rope, u1f, wint, cosf, sinf, tm=2048)
    qkvb = unflat(proj(n=4608, b_off=OFF_QKVB, rope_cols=3072, tn=256, name="proj_qkvb"))
    b_kws, os_, ls_ = [], [], []
    for g, (window, r) in enumerate(B_PATTERNS):
        npair, nch = B_CHUNKS[r]
        per = B_HEADS_PER_GROUP // (2 * npair)
        nsec = len(B_PATTERNS) * per
        kw_ = dict(npair=npair, gqa=False, q_col=g * per, k_col=nsec + g * per, v_col=2 * nsec + g * per, nchunk=nch, r=r,
                   n_back=window // r)
        b_kws.append(kw_)
        o_g, l_g = _attn_fwd(qkvb, qkvb, qkvb, name=f"attn_b{g}_fwd", **kw_)
        os_.append(o_g)
        ls_.append(l_g)
    tok = hook("projected", os_[-1])
    proj = functools.partial(_proj_rope, u1f, wint, cosf + scalar(tok), sinf, tm=2048)
    gab = unflat(proj(n=2048, b_off=OFF_GAB, rope_cols=0, tn=256, name="proj_gab", out_dtype=BF16))
    qa = kva = unflat(proj(n=OFF_QKVB, b_off=OFF_QA, rope_cols=OFF_KVA + LANES, tn=256, name="proj_qkva", out_dtype=BF16))
    a_kw = dict(npair=A_Q_HEADS // 2, gqa=True, q_col=0, k_col=OFF_KVA // LANES, v_col=OFF_KVA // LANES + 1, nchunk=1, r=1,
                n_back=A_WINDOW - 1)
    after_gab = jnp.minimum(jnp.abs(gab[0, 0, 0].astype(F32)), 0.0)
    oa, lse_a = _attn_fwd(qa, kva, kva, name="attn_a_fwd", sinks=sinks.reshape(A_Q_HEADS) + after_gab, **a_kw)
    rest = get_rest(oa)
    wba, wbbt, wo, wgut, wd = (rest[n] for n in ("w_branch_a", "w_branch_b", "w_o", "w_gate_up", "w_down"))
    ya = unflat(mm(flat(oa), wba, mode="nn", out_dtype=BF16, tn=512, name="branch_a"))
    ybf, mergedf, obf = _merge_branch_b_gate([flat(t) for t in os_], [flat(t) for t in ls_], wbbt, flat(gab), flat(ya))
    xf = flat(x)
    y1f, h1f, u2f = _wo_ln1(mergedf, wo, xf, mod, ln1_g, ln1_b, seq)
    wgut_i = _interleave_gate_up(wgut)
    hf, af = _gate_up_silu(u2f, wgut_i)

    dy2f, dh1af, acc2 = _down_ln2_loss_bwd(af, wd, h1f, mod, ln2_g, ln2_b, flat(target), seq)
    g_wd = _matmul(af, dy2f, mode="tn", out_dtype=BF16, tm=256, tn=1024, tk=ntok, name="down_wgrad")
    dhf = _down_dgrad_silu_bwd(dy2f, wd, hf)
    g_wgut = _interleave_gate_up(_matmul(dhf, u2f, mode="tn", out_dtype=BF16, tm=256, tn=1024, tk=ntok, name="gate_up_wgrad"))
    dy1f, dxaf, acc1 = _gate_up_dgrad_ln1_bwd(dhf, wgut_i, dh1af, xf, y1f, mod, ln1_g, ln1_b, seq)
    g_wo = _matmul(mergedf, dy1f, mode="tn", out_dtype=BF16, tm=256, tn=1024, tk=ntok, name="w_o_wgrad")
    dyaf, dybf, dgaf, dgbf = _wo_dgrad_gate_bwd(dy1f, wo, flat(gab), flat(ya), ybf)
    g_wba = _matmul(flat(oa), dyaf, mode="tn", out_dtype=BF16, tm=256, tn=1024, tk=ntok, name="branch_a_wgrad")
    g_wbbt = _matmul(dybf, obf, mode="tn", out_dtype=BF16, tm=256, tn=512, tk=ntok, name="branch_b_wgrad")
    tok = hook("grads_rest", dict(w_branch_a=g_wba, w_branch_b=g_wbbt, w_o=g_wo, w_gate_up=g_wgut, w_down=g_wd))

    sinks_exp = jnp.repeat(sinks.reshape(1, A_Q_HEADS), HEAD_DIM, axis=1) + scalar(tok)
    doa, dd_a, acc_s = _branch_a_dgrad_delta(dyaf, wba, flat(oa), flat(lse_a), sinks_exp, seq)
    doa, dd_a = unflat(doa), unflat(dd_a)
    tok = hook("delta_done", dd_a)
    dqa, dka, dva = _attn_bwd(qa, kva, kva, cos, sin, doa, lse_a, dd_a, name="attn_a_bwd", token=tok, **a_kw)
    merged_bwd = [unflat(t) for t in _branch_b_dgrad_merge_bwd(dybf, wbbt, [flat(t) for t in os_], [flat(t) for t in ls_])]
    dqs, dks, dvs = [], [], []
    for g in range(len(B_PATTERNS)):
        dq_g, dk_g, dv_g = _attn_bwd(qkvb, qkvb, qkvb, cos, sin, merged_bwd[g], ls_[g], merged_bwd[3 + g],
                                     name=f"attn_b{g}_bwd", **b_kws[g])
        dqs.append(dq_g)
        dks.append(dk_g)
        dvs.append(dv_g)
    dproj = jnp.concatenate([t.astype(BF16) for t in [dqa, dka, dva] + dqs + dks + dvs] + [unflat(dgaf), unflat(dgbf)], axis=-1)
    dprojf = flat(dproj)
    g_wint = _matmul(dprojf, u1f, mode="tn", out_dtype=BF16, tm=256, tn=1024, tk=ntok, name="w_in_wgrad")
    tok = hook("grads_w_in", dict(w_in=g_wint))
    grad_x, acc0 = _w_in_dgrad_grad_x(dprojf, wint, dxaf, xf, mod, seq, tok)
    grad_x = unflat(grad_x)
    tok = hook("dgrad_done", grad_x)

    loss_part = jnp.sum(acc2[:, 3, 0])
    dmod = jnp.stack([acc0[:, 1], acc0[:, 0], acc1[:, 2], acc1[:, 4], acc1[:, 3], acc2[:, 2]], axis=1)
    small = jnp.stack([acc1[:, 0].sum(0), acc1[:, 1].sum(0), acc2[:, 0].sum(0), acc2[:, 1].sum(0), acc_s[:, 0].sum(0)])
    small = small + scalar(tok)
    return loss_part, grad_x, dmod, small


CHIP_FLIPS = (2, 4, 6)


def _my_place():
    return lax.axis_index("x"), lax.axis_index("y"), lax.axis_index("c")


def _flip(place, k):
    px, py, pc = place
    return (1 - px if k & 4 else px, 1 - py if k & 2 else py, 1 - pc if k & 1 else pc)


def _index(place):
    return 4 * place[0] + 2 * place[1] + place[2]


def _gather_small(v, name):
    rows, cols = v.shape

    def body(v_ref, out_ref, send_sems, recv_sems):
        me = _my_place()
        out_ref[_index(me)] = v_ref[...]
        copies = []
        for k in range(1, N_DEV):
            copies.append(pltpu.make_async_remote_copy(
                src_ref=v_ref, dst_ref=out_ref.at[_index(me)], send_sem=send_sems.at[k - 1], recv_sem=recv_sems.at[k - 1],
                device_id=_flip(me, k), device_id_type=MESH))
        for cp in copies:
            cp.start()
        for k in range(1, N_DEV):
            pltpu.make_async_remote_copy(
                src_ref=v_ref, dst_ref=out_ref.at[_index(_flip(me, k))], send_sem=send_sems.at[k - 1],
                recv_sem=recv_sems.at[k - 1], device_id=_flip(me, k), device_id_type=MESH).wait_recv()
        for cp in copies:
            cp.wait_send()

    return pl.pallas_call(
        body,
        name=name,
        out_shape=jax.ShapeDtypeStruct((N_DEV, rows, cols), v.dtype),
        in_specs=[pl.BlockSpec(memory_space=pltpu.VMEM)],
        out_specs=pl.BlockSpec(memory_space=pltpu.VMEM),
        scratch_shapes=[pltpu.SemaphoreType.DMA((N_DEV - 1,)), pltpu.SemaphoreType.DMA((N_DEV - 1,))],
        compiler_params=pltpu.CompilerParams(vmem_limit_bytes=VMEM_LIMIT_BYTES),
    )(v)


_HBM = pl.BlockSpec(memory_space=pltpu.HBM)
_SEM = pl.BlockSpec(memory_space=pltpu.SEMAPHORE)
_EFFECT = pltpu.SideEffectType.DATAFLOW_SIDE_EFFECTING


def _remote(src, dst, send_sems, recv_sems, j, to):
    return pltpu.make_async_remote_copy(src_ref=src, dst_ref=dst, send_sem=send_sems.at[j], recv_sem=recv_sems.at[j],
                                        device_id=to, device_id_type=MESH)


def _copies_start(name, bufs, make_copies, nsem):
    nbuf = len(bufs)

    def body(*refs):
        for cp in make_copies(refs[:nbuf], refs[nbuf], refs[nbuf + 1]):
            cp.start()
        refs[-1][...] = jnp.zeros_like(refs[-1])

    sems = pltpu.SemaphoreType.DMA((nsem,))
    res = pl.pallas_call(
        body, name=name,
        out_shape=(sems, sems, *[pltpu.HBM(v.shape, v.dtype) for v in bufs], jax.ShapeDtypeStruct((8, LANES), F32)),
        in_specs=(_HBM,) * nbuf, out_specs=(_SEM, _SEM) + (_HBM,) * nbuf + (pl.BlockSpec(memory_space=pltpu.VMEM),),
        input_output_aliases={i: 2 + i for i in range(nbuf)},
        compiler_params=pltpu.CompilerParams(has_side_effects=_EFFECT),
    )(*[pltpu.with_memory_space_constraint(v, pltpu.HBM) for v in bufs])
    return res[0], res[1], list(res[2:2 + nbuf]), res[-1]


def _copies_wait(name, started, make_copies, after):
    send_sems, recv_sems, bufs, _ = started
    nbuf = len(bufs)

    def body(*refs):
        for cp in make_copies(refs[:nbuf], refs[nbuf], refs[nbuf + 1]):
            cp.wait_send()
            cp.wait_recv()

    return list(pl.pallas_call(
        body, name=name,
        out_shape=tuple(pltpu.HBM(v.shape, v.dtype) for v in bufs),
        in_specs=(_HBM,) * nbuf + (_SEM, _SEM, pl.BlockSpec(memory_space=pl.ANY)), out_specs=(_HBM,) * nbuf,
        input_output_aliases={i: i for i in range(nbuf)},
        compiler_params=pltpu.CompilerParams(has_side_effects=_EFFECT),
    )(*bufs, send_sems, recv_sems, after))


def _to_sibling_copies(refs, send_sems, recv_sems):
    src_ref, land_ref = refs
    me = _my_place()
    return [_remote(src_ref.at[q, 1 - me[2]], land_ref.at[q], send_sems, recv_sems, q, _flip(me, 1)) for q in range(4)]


def _to_chips_copies(refs, send_sems, recv_sems):
    src_ref, land_ref = refs
    me = _my_place()
    copies = []
    for j, k in enumerate(CHIP_FLIPS):
        to = _flip(me, k)
        copies.append(_remote(src_ref.at[2 * to[0] + to[1]], land_ref.at[j], send_sems, recv_sems, j, to))
    return copies


class _Gather:
    def __init__(self, name, blocks):
        self.name, self.n = name, len(blocks)
        at_me = (_index(_my_place()), 0, 0)
        lands = [lax.dynamic_update_slice(lax.empty((N_DEV,) + v.shape, v.dtype), v[None], at_me) for v in blocks]
        self.first = _copies_start(name + "_start", list(blocks) + lands, self._first_copies, 4 * self.n)
        self.token = self.first[3]

    def _first_copies(self, refs, send_sems, recv_sems):
        me = _my_place()
        return [_remote(refs[w], refs[self.n + w].at[_index(me)], send_sems, recv_sems, 4 * w + j, _flip(me, k))
                for w in range(self.n) for j, k in enumerate((1,) + CHIP_FLIPS)]

    def _pass_copies(self, refs, send_sems, recv_sems):
        me = _my_place()
        copies = []
        for w, land in enumerate(refs):
            for j, k in enumerate(CHIP_FLIPS):
                slot = land.at[_index(_flip(me, k))]
                copies.append(_remote(slot, slot, send_sems, recv_sems, 3 * w + j, _flip(me, 1)))
        return copies

    def pass_on(self, after):
        lands = _copies_wait(self.name + "_wait", self.first, self._first_copies, after)[self.n:]
        self.second = _copies_start(self.name + "_pass_start", lands, self._pass_copies, 3 * self.n)
        return self.second[3]

    def finish(self, after):
        return _copies_wait(self.name + "_pass_wait", self.second, self._pass_copies, after)


SUM_SPLIT = 2


def _sum_pairs(parts, theirs):
    nchip, _, rows, cols = parts.shape
    tile = rows // SUM_SPLIT

    def body(c_ref, a_ref, b_ref, o_ref):
        o_ref[...] = (a_ref[0].astype(F32) + b_ref[...].astype(F32)).astype(BF16)

    spec = pl.BlockSpec((1, tile, cols), lambda q, t, c_ref: (q, t, 0))
    grid_spec = pltpu.PrefetchScalarGridSpec(
        num_scalar_prefetch=1, grid=(nchip, SUM_SPLIT),
        in_specs=[pl.BlockSpec((1, 1, tile, cols), lambda q, t, c_ref: (q, c_ref[0], t, 0)), spec], out_specs=spec)
    return pl.pallas_call(body, name="grad_sum_sibling", grid_spec=grid_spec,
                          out_shape=jax.ShapeDtypeStruct((nchip, rows, cols), BF16),
                          compiler_params=_params("parallel", "parallel"))(lax.axis_index("c").reshape(1), parts, theirs)


def _sum_final(chip_sum, got):
    _, rows, cols = chip_sum.shape
    tile = rows // SUM_SPLIT

    def body(q_ref, a_ref, g_ref, o_ref):
        o_ref[...] = ((a_ref[0].astype(F32) + g_ref[0].astype(F32)) + g_ref[1].astype(F32)) + g_ref[2].astype(F32)

    grid_spec = pltpu.PrefetchScalarGridSpec(
        num_scalar_prefetch=1, grid=(SUM_SPLIT,),
        in_specs=[pl.BlockSpec((1, tile, cols), lambda t, q_ref: (q_ref[0], t, 0)),
                  pl.BlockSpec((3, tile, cols), lambda t, q_ref: (0, t, 0))],
        out_specs=pl.BlockSpec((tile, cols), lambda t, q_ref: (t, 0)))
    my_chip = (2 * lax.axis_index("x") + lax.axis_index("y")).reshape(1)
    return pl.pallas_call(body, name="grad_sum_chips", grid_spec=grid_spec, out_shape=jax.ShapeDtypeStruct((rows, cols), F32),
                          compiler_params=_params("parallel"))(my_chip, chip_sum, got)


class _ReduceScatter:
    def __init__(self, name, slabs):
        self.name, self.rows = name, slabs.shape[1]
        parts = slabs.reshape(4, 2, self.rows, D_MODEL)
        self.first = _copies_start(name + "_sibling_start", [parts, lax.empty((4, self.rows, D_MODEL), slabs.dtype)],
                                   _to_sibling_copies, 4)
        self.token = self.first[3]

    def between_chips(self, after):
        parts, theirs = _copies_wait(self.name + "_sibling_wait", self.first, _to_sibling_copies, after)
        chip_sum = _sum_pairs(parts, theirs)
        self.second = _copies_start(self.name + "_chips_start", [chip_sum, lax.empty((3, self.rows, D_MODEL), chip_sum.dtype)],
                                    _to_chips_copies, 3)
        return self.second[3]

    def finish(self, after):
        chip_sum, got = _copies_wait(self.name + "_chips_wait", self.second, _to_chips_copies, after)
        return _sum_final(chip_sum, got)


def _ada_fwd(c_all, w, b):
    nb, _ = c_all.shape
    ncol = w.shape[1]

    def body(c_ref, w_ref, b_ref, o_ref):
        c = c_ref[...]
        act = (c * _sigmoid(c)).astype(BF16)
        o_ref[...] = jnp.dot(act, w_ref[...].astype(BF16), preferred_element_type=F32) + b_ref[...]

    return pl.pallas_call(body, name="ada_fwd", out_shape=jax.ShapeDtypeStruct((nb, ncol), F32),
                          compiler_params=pltpu.CompilerParams(vmem_limit_bytes=VMEM_LIMIT_BYTES))(c_all, w, b)


def _ada_wgrad(c_all_t, dmod_cols):
    d, nb = c_all_t.shape
    ncol = dmod_cols.shape[1]

    def body(ct_ref, dm_ref, o_ref):
        ct = ct_ref[...]
        act = (ct * _sigmoid(ct)).astype(BF16).astype(F32)
        dm = dm_ref[...].astype(BF16).astype(F32)
        acc = act[:, 0:1] * dm[0:1, :]
        for i in range(1, nb):
            acc = acc + act[:, i:i + 1] * dm[i:i + 1, :]
        o_ref[...] = acc

    return pl.pallas_call(body, name="ada_wgrad", out_shape=jax.ShapeDtypeStruct((d, ncol), F32),
                          compiler_params=pltpu.CompilerParams(vmem_limit_bytes=VMEM_LIMIT_BYTES))(c_all_t, dmod_cols)


SMALL_ROWS = 24


def _reduce_small(gathered):
    def body(g_ref, o_ref):
        acc = g_ref[0]
        for dev in range(1, N_DEV):
            acc = acc + g_ref[dev]
        o_ref[...] = acc

    return pl.pallas_call(body, name="reduce_small", out_shape=jax.ShapeDtypeStruct(gathered.shape[1:], F32))(gathered)


def _adamw_math(w, g, m, v):
    nm = ADAM_B1 * m + (1.0 - ADAM_B1) * g
    nv = ADAM_B2 * v + (1.0 - ADAM_B2) * (g * g)
    bc1 = 1.0 - ADAM_B1 ** ADAM_STEP
    bc2 = 1.0 - ADAM_B2 ** ADAM_STEP
    return -ADAM_LR * ((nm / bc1) / (jnp.sqrt(nv / bc2) + ADAM_EPS) + ADAM_WD * w), nm, nv


def _adamw_small(ws, gs, ms, vs, name):
    n = len(ws)

    def body(*refs):
        for i in range(n):
            res = _adamw_math(*(refs[k * n + i][...] for k in range(4)))
            for k in range(3):
                refs[(4 + k) * n + i][...] = res[k]

    shapes = [jax.ShapeDtypeStruct(w.shape, F32) for w in ws]
    res = pl.pallas_call(body, name=name, out_shape=shapes * 3)(*ws, *gs, *ms, *vs)
    return [(res[i], res[n + i], res[2 * n + i]) for i in range(n)]


def _adamw(w, g, m, v, name):
    rows, cols = w.shape
    tile = rows
    for cand in range(min(rows // 2, 512) // 8 * 8, 7, -8):
        if rows % cand == 0:
            tile = cand
            break
    spec = pl.BlockSpec((tile, cols), lambda t: (t, 0))

    def body(w_ref, g_ref, m_ref, v_ref, d_ref, nm_ref, nv_ref):
        d_ref[...], nm_ref[...], nv_ref[...] = _adamw_math(w_ref[...], g_ref[...], m_ref[...], v_ref[...])

    shp = jax.ShapeDtypeStruct((rows, cols), F32)
    return pl.pallas_call(body, name=name, grid=(rows // tile,), in_specs=[spec] * 4, out_specs=[spec] * 3, out_shape=[shp] * 3,
                          compiler_params=_params("parallel"))(w, g, m, v)


_WEIGHTS = ("w_ada", "b_ada", "w_in", "sinks", "w_branch_a", "w_branch_b", "w_o", "ln1_g", "ln1_b", "w_gate_up", "w_down",
            "ln2_g", "ln2_b")
_TRANSPOSED = ("w_in", "w_branch_b", "w_gate_up")


def _pack_shard(name, w):
    w = w.astype(BF16)
    if name in _TRANSPOSED:
        w = w.T
    return w.reshape(-1, D_MODEL)


def _unpack_full(name, slab):
    if name == "w_branch_b":
        return slab.reshape(N_DEV * 128, 512)
    return slab.reshape(-1, D_MODEL)


def _unpack_group(group, gathered):
    return {n: _unpack_full(n, slab) for (n, _), slab in zip(group, gathered)}


def _unpack_grads(group, g_packed):
    g_w, off = {}, 0
    for n, r in group:
        part = g_packed[off:off + r]
        off += r
        g_w[n] = part.reshape(128, 512) if n == "w_branch_b" else part
    return g_w


def kernel(x, c, positions, w_ada, b_ada, w_in, sinks, w_branch_a, w_branch_b, w_o, ln1_g, ln1_b, w_gate_up, w_down, ln2_g, ln2_b, loss_target, m_w_ada, m_b_ada, m_w_in, m_sinks, m_w_branch_a, m_w_branch_b, m_w_o, m_ln1_g, m_ln1_b, m_w_gate_up, m_w_down, m_ln2_g, m_ln2_b, v_w_ada, v_b_ada, v_w_in, v_sinks, v_w_branch_a, v_w_branch_b, v_w_o, v_ln1_g, v_ln1_b, v_w_gate_up, v_w_down, v_ln2_g, v_ln2_b):
    weights = dict(w_ada=w_ada, b_ada=b_ada, w_in=w_in, sinks=sinks, w_branch_a=w_branch_a, w_branch_b=w_branch_b, w_o=w_o,
                   ln1_g=ln1_g, ln1_b=ln1_b, w_gate_up=w_gate_up, w_down=w_down, ln2_g=ln2_g, ln2_b=ln2_b)
    m_in = dict(w_ada=m_w_ada, b_ada=m_b_ada, w_in=m_w_in, sinks=m_sinks, w_branch_a=m_w_branch_a, w_branch_b=m_w_branch_b,
                w_o=m_w_o, ln1_g=m_ln1_g, ln1_b=m_ln1_b, w_gate_up=m_w_gate_up, w_down=m_w_down, ln2_g=m_ln2_g, ln2_b=m_ln2_b)
    v_in = dict(w_ada=v_w_ada, b_ada=v_b_ada, w_in=v_w_in, sinks=v_sinks, w_branch_a=v_w_branch_a, w_branch_b=v_w_branch_b,
                w_o=v_w_o, ln1_g=v_ln1_g, ln1_b=v_ln1_b, w_gate_up=v_w_gate_up, w_down=v_w_down, ln2_g=v_ln2_g, ln2_b=v_ln2_b)
    bsz = x.shape[0]
    me = _index(_my_place())
    ada_cols = w_ada.shape[2]
    outs = {}

    def adamw(n, g):
        w2, m2, v2 = (t[n][0] if t[n].ndim == 3 else t[n] for t in (weights, m_in, v_in))
        shape = weights[n].shape
        if n in _TRANSPOSED:
            dlt, nm, nv = _adamw(w2.T, g, m2.T, v2.T, "adamw_" + n)
            outs[n] = tuple(t.T.reshape(shape) for t in (g, dlt, nm, nv))
        else:
            dlt, nm, nv = _adamw(w2, g, m2, v2, "adamw_" + n)
            outs[n] = tuple(t.reshape(shape) for t in (g, dlt, nm, nv))
        return nv

    packed_in = [_pack_shard(n, weights[n][0]) for n, _ in GROUP_IN]
    packed_rest = [_pack_shard(n, weights[n][0]) for n, _ in GROUP_REST]
    c_all = _gather_small(jnp.pad(c, ((0, 8 - bsz), (0, 0))), "gather_c")[:, :bsz].reshape(N_DEV * bsz, D_MODEL)
    gather_in = _Gather("gather_w_in", lax.optimization_barrier((packed_in, c_all))[0])
    b_cols = lax.dynamic_slice_in_dim(b_ada, me * ada_cols, ada_cols, axis=1)
    mod_cols = _ada_fwd(c_all, w_ada[0], b_cols + gather_in.token[0, 0])
    tables = _rope_tables(positions)
    mod_cols, tables, packed_rest = lax.optimization_barrier((mod_cols, tables, packed_rest))
    mod_all = _gather_small(mod_cols, "gather_mod").transpose(1, 0, 2).reshape(N_DEV * bsz, 6, D_MODEL)
    gather_rest = _Gather("gather_rest", lax.optimization_barrier((packed_rest, mod_all))[0])
    mod = jnp.pad(lax.dynamic_slice_in_dim(mod_all, me * bsz, bsz, axis=0), ((0, 0), (0, 2), (0, 0)))
    mod = mod + gather_rest.token[0, 0]
    mod = mod + gather_in.pass_on(mod)[0, 0]

    scatters = {}

    def get_w_in(after):
        return _unpack_group(GROUP_IN, gather_in.finish(after))["w_in"]

    def get_rest(after):
        return _unpack_group(GROUP_REST, gather_rest.finish(after))

    def pack_grads(group, grads):
        return jnp.concatenate([grads[n].reshape(N_DEV, r, D_MODEL) for n, r in group], axis=1)

    def hook(point, value):
        if point == "projected":
            return gather_rest.pass_on(value)
        if point == "grads_rest":
            scatters["rest"] = _ReduceScatter("scatter_rest", pack_grads(GROUP_REST, value))
            return scatters["rest"].token
        if point == "delta_done":
            return scatters["rest"].between_chips(value)
        if point == "grads_w_in":
            scatters["in"] = _ReduceScatter("scatter_w_in", pack_grads(GROUP_IN, value))
            return scatters["in"].token
        if point == "dgrad_done":
            return None
        raise ValueError(point)

    loss_part, grad_x, dmod, small = _layer_step(x, mod, tables, sinks[0], ln1_g, ln1_b, ln2_g, ln2_b, loss_target,
                                                 get_w_in, get_rest, hook)

    rows = jnp.concatenate([dmod.reshape(bsz * 6, D_MODEL), small, jnp.full((1, D_MODEL), loss_part, F32),
                            jnp.zeros((SMALL_ROWS - bsz * 6 - 6, D_MODEL), F32)], axis=0)
    small_all = _gather_small(rows, "gather_small")
    small_all = small_all + scatters["in"].between_chips(small_all)[0, 0]
    sums = _reduce_small(small_all)
    loss = sums[bsz * 6 + 5, 0]
    dmod_all = small_all[:, :bsz * 6].reshape(N_DEV * bsz, 6 * D_MODEL)
    small_g = {"b_ada": functools.reduce(jnp.add, [sums[6 * i:6 * i + 6] for i in range(bsz)]).reshape(1, 6 * D_MODEL),
               "sinks": sums[bsz * 6 + 4][::HEAD_DIM][None]}
    small_g.update({n: sums[bsz * 6 + i][None] for i, n in enumerate(("ln1_g", "ln1_b", "ln2_g", "ln2_b"))})
    names = list(small_g)
    for n, (dlt, nm, nv) in zip(names, _adamw_small([weights[n] for n in names], [small_g[n] for n in names],
                                                     [m_in[n] for n in names], [v_in[n] for n in names], "adamw_small")):
        outs[n] = (small_g[n], dlt, nm, nv)
    dmod_cols = lax.dynamic_slice_in_dim(dmod_all, me * ada_cols, ada_cols, axis=1)
    last = adamw("w_ada", _ada_wgrad(c_all.T, dmod_cols))
    for n, g in _unpack_grads(GROUP_REST, scatters["rest"].finish(last)).items():
        adamw(n, g)
    done = lax.optimization_barrier(tuple(outs[n][3] for n in outs))
    for n, g in _unpack_grads(GROUP_IN, scatters["in"].finish(done[0])).items():
        adamw(n, g)

    return (loss, grad_x, *[outs[n][0] for n in _WEIGHTS], *[outs[n][1] for n in _WEIGHTS], *[outs[n][2] for n in _WEIGHTS],
            *[outs[n][3] for n in _WEIGHTS])
```

```python
import functools

import jax
import jax.numpy as jnp
from jax import lax
from jax.experimental import pallas as pl
from jax.experimental.pallas import tpu as pltpu

F32 = jnp.float32
BF16 = jnp.bfloat16

D_MODEL = 1024
HEAD_DIM = 64
A_Q_HEADS = 16
A_WINDOW = 128
B_PATTERNS = ((128, 1), (512, 4), (2048, 16))
B_HEADS_PER_GROUP = 8
D_FF = 2816
QBLOCK = 128
ROPE_THETA = 10000.0
LN_EPS = 1e-5
DEEPNORM_ALPHA = 2.0 ** 0.25
NEG_INF = -1e30
ADAM_LR, ADAM_B1, ADAM_B2, ADAM_EPS, ADAM_WD, ADAM_STEP = 0.001, 0.9, 0.999, 1e-08, 0.01, 10

N_DEV = 8
LANES = 128
VMEM_LIMIT_BYTES = 56 * 1024 * 1024
MESH = pl.DeviceIdType.MESH

OFF_QA, OFF_KVA, OFF_QKVB, OFF_GAB = 0, 1024, 1280, 5888
GROUP_IN = (("w_in", 992),)
GROUP_REST = (("w_branch_a", 128), ("w_branch_b", 64), ("w_o", 128), ("w_gate_up", 704), ("w_down", 352))


def _params(*sem):
    return pltpu.CompilerParams(dimension_semantics=sem, vmem_limit_bytes=VMEM_LIMIT_BYTES)


def _sigmoid(x):
    return 1.0 / (1.0 + jnp.exp(-x))


_DIMS = {"nn": (((1,), (0,)), ((), ())), "nt": (((1,), (1,)), ((), ())), "tn": (((0,), (0,)), ((), ()))}


def _matmul(a, b, *, mode, tm, tn, tk, name, out_dtype=None, n=None, b_off=0, token=None, ins=(), outs=None, epilogue=None,
            lhs_fn=None):
    if mode == "nn":
        (m, k), nn_ = a.shape, b.shape[1]
    elif mode == "nt":
        (m, k), nn_ = a.shape, (b.shape[0] if n is None else n)
    else:
        (k, m), nn_ = a.shape, b.shape[1]
    assert m % tm == 0 and nn_ % tn == 0 and k % tk == 0 and b_off % tn == 0, (name, m, nn_, k)
    nk = k // tk
    joff = b_off // tn
    if mode == "nn":
        a_spec = pl.BlockSpec((tm, tk), lambda i, j, kk: (i, kk))
        b_spec = pl.BlockSpec((tk, tn), lambda i, j, kk: (kk, j))
    elif mode == "nt":
        a_spec = pl.BlockSpec((tm, tk), lambda i, j, kk: (i, kk))
        b_spec = pl.BlockSpec((tn, tk), lambda i, j, kk: (j + joff, kk))
    else:
        a_spec = pl.BlockSpec((tk, tm), lambda i, j, kk: (kk, i))
        b_spec = pl.BlockSpec((tk, tn), lambda i, j, kk: (kk, j))
    dims = _DIMS[mode]
    has_token = token is not None
    plain = epilogue is None
    if plain:
        outs = [(jax.ShapeDtypeStruct((m, nn_), out_dtype), (tm, tn), lambda i, j: (i, j))]

        def epilogue(acc, i, j, in_refs, out_refs):
            out_refs[0][...] = acc.astype(out_refs[0].dtype)

    nin = len(ins)
    nscratch = 1 if lhs_fn is None else 2
    assert lhs_fn is None or nk == 1

    def body(*refs):
        a_ref, b_ref = refs[:2]
        in_refs = refs[2:2 + nin]
        out_refs = refs[2 + nin + has_token:-nscratch]
        acc_ref = refs[-nscratch]
        kk = pl.program_id(2)
        if lhs_fn is None:
            lhs = a_ref[...].astype(BF16)
        else:
            lhs_ref = refs[-1]

            @pl.when(pl.program_id(1) == 0)
            def _():
                lhs_ref[...] = lhs_fn(a_ref, in_refs, out_refs)

            lhs = lhs_ref[...]
        part = lax.dot_general(lhs, b_ref[...].astype(BF16), dims, preferred_element_type=F32)

        def finish(acc):
            epilogue(acc, pl.program_id(0), pl.program_id(1), in_refs, out_refs)

        if nk == 1:
            finish(part)
        else:
            @pl.when(kk == 0)
            def _():
                acc_ref[...] = part

            @pl.when(kk > 0)
            def _():
                acc_ref[...] += part

            @pl.when(kk == nk - 1)
            def _():
                finish(acc_ref[...])

    def spec(block, index):
        return pl.BlockSpec(block, lambda i, j, kk: index(i, j))

    in_specs, args = [a_spec, b_spec], [a, b]
    for arr, block, index in ins:
        in_specs.append(spec(block, index))
        args.append(arr)
    if has_token:
        in_specs.append(pl.BlockSpec(token.shape, lambda i, j, kk: (0, 0)))
        args.append(token)
    res = pl.pallas_call(
        body,
        name=name,
        grid=(m // tm, nn_ // tn, nk),
        in_specs=in_specs,
        out_specs=[spec(block, index) for _, block, index in outs],
        out_shape=[shape for shape, _, _ in outs],
        scratch_shapes=[pltpu.VMEM((tm, tn) if nk > 1 else (8, LANES), F32)] + ([] if lhs_fn is None else [pltpu.VMEM((tm, tk), BF16)]),
        compiler_params=_params("arbitrary", "arbitrary", "arbitrary"),
    )(*args)
    return res[0] if plain else res


def _proj_rope(a, bt, cos, sin, *, n, b_off, rope_cols, tm, tn, name, out_dtype=F32):
    m, k = a.shape
    assert m % tm == 0 and n % tn == 0 and b_off % tn == 0 and rope_cols % LANES == 0, name
    joff = b_off // tn
    nrope, part = divmod(rope_cols, tn)

    def body(a_ref, b_ref, c_ref, s_ref, o_ref):
        acc = lax.dot_general(a_ref[...], b_ref[...], _DIMS["nt"], preferred_element_type=F32)
        j = pl.program_id(1)

        @pl.when(j < nrope)
        def _():
            o_ref[...] = _rope(acc, c_ref[...], s_ref[...], coarse=True).astype(o_ref.dtype)

        if part:
            @pl.when(j == nrope)
            def _():
                o_ref[:, :part] = _rope(acc[:, :part], c_ref[...], s_ref[...], coarse=True).astype(o_ref.dtype)
                o_ref[:, part:] = acc[:, part:].astype(o_ref.dtype)

        @pl.when(j >= nrope + (1 if part else 0))
        def _():
            o_ref[...] = acc.astype(o_ref.dtype)

    table = pl.BlockSpec((tm, LANES), lambda i, j: (i, 0))
    return pl.pallas_call(
        body,
        name=name,
        grid=(m // tm, n // tn),
        in_specs=[pl.BlockSpec((tm, k), lambda i, j: (i, 0)), pl.BlockSpec((tn, k), lambda i, j: (j + joff, 0)), table, table],
        out_specs=pl.BlockSpec((tm, tn), lambda i, j: (i, j)),
        out_shape=jax.ShapeDtypeStruct((m, n), out_dtype),
        compiler_params=_params("parallel", "parallel"),
    )(a, bt, cos, sin)


ROW_TILE = 256


def _rows(width, col=0):
    return pl.BlockSpec((1, ROW_TILE, width), lambda b, t: (b, t, col))


def _per_batch(nrows, width):
    return pl.BlockSpec((1, nrows, width), lambda b, t: (b, 0, 0))


def _row_call(body, name, bsz, seq, in_specs, out_specs, out_shape, accumulates=False):
    return pl.pallas_call(
        body,
        name=name,
        grid=(bsz, seq // ROW_TILE),
        in_specs=in_specs,
        out_specs=out_specs,
        out_shape=out_shape,
        compiler_params=_params("parallel", "arbitrary" if accumulates else "parallel"),
    )


def _acc_rows(acc_ref, first, rows):
    @pl.when(first)
    def _():
        acc_ref[...] = jnp.zeros_like(acc_ref)

    for r, val in enumerate(rows):
        acc_ref[0, r:r + 1, :] += val


def _colsum(v):
    return jnp.sum(v, axis=0, keepdims=True)


def _ln_stats(z):
    mu = jnp.mean(z, axis=-1, keepdims=True)
    zc = z - mu
    var = jnp.mean(zc * zc, axis=-1, keepdims=True)
    rstd = lax.rsqrt(var + LN_EPS)
    return zc * rstd, rstd


def _ln_bwd(dxhat, xhat, rstd):
    m1 = jnp.mean(dxhat, axis=-1, keepdims=True)
    m2 = jnp.mean(dxhat * xhat, axis=-1, keepdims=True)
    return rstd * (dxhat - m1 - xhat * m2)


def _modulate_in(x, mod):
    bsz, seq, d = x.shape

    def body(x_ref, mod_ref, u_ref):
        u_ref[0] = (x_ref[0] * (1.0 + mod_ref[0, 1:2, :]) + mod_ref[0, 0:1, :]).astype(BF16)

    return _row_call(body, "modulate_in", bsz, seq, [_rows(d), _per_batch(8, d)], _rows(d),
                     jax.ShapeDtypeStruct((bsz, seq, d), BF16))(x, mod)


EP_TILE = 512


def _ep_specs(seq, d):
    tiles = seq // EP_TILE
    return ((EP_TILE, d), lambda i, j: (i, 0)), ((1, 8, d), lambda i, j: (i // tiles, 0, 0)), ((1, d), lambda i, j: (0, 0))


def _wo_ln1(merged, wo, x, mod, g, b, seq):
    ntok, d = x.shape
    row, per_b, whole = _ep_specs(seq, d)

    def epilogue(y, i, j, ins, outs):
        x_ref, mod_ref, g_ref, b_ref = ins
        y_ref, h_ref, u_ref = outs
        z = DEEPNORM_ALPHA * x_ref[...] + (1.0 + mod_ref[0, 2:3, :]) * y
        xhat, _ = _ln_stats(z)
        h = xhat * g_ref[...] + b_ref[...]
        y_ref[...] = y
        h_ref[...] = h
        u_ref[...] = (h * (1.0 + mod_ref[0, 4:5, :]) + mod_ref[0, 3:4, :]).astype(BF16)

    f32, bf16 = jax.ShapeDtypeStruct((ntok, d), F32), jax.ShapeDtypeStruct((ntok, d), BF16)
    return _matmul(merged, wo, mode="nn", tm=EP_TILE, tn=d, tk=d, name="w_o_ln1",
                   ins=[(x,) + row, (mod,) + per_b, (g,) + whole, (b,) + whole],
                   outs=[(f32,) + row, (f32,) + row, (bf16,) + row], epilogue=epilogue)


FF_HALF = D_FF // 2


def _interleave_gate_up(w):
    return w.reshape(2, 2, FF_HALF, w.shape[1]).transpose(1, 0, 2, 3).reshape(w.shape)


def _gate_up_silu(u2, wgut_i):
    ntok = u2.shape[0]

    def epilogue(h, i, j, ins, outs):
        h_ref, a_ref = outs
        hg, hu = h[:, :FF_HALF], h[:, FF_HALF:]
        h_ref[...] = h.astype(BF16)
        a_ref[...] = (hg * _sigmoid(hg) * hu).astype(BF16)

    return _matmul(u2, wgut_i, mode="nt", tm=EP_TILE, tn=2 * FF_HALF, tk=u2.shape[1], name="gate_up_silu",
                   outs=[(jax.ShapeDtypeStruct((ntok, 2 * D_FF), BF16), (EP_TILE, 2 * FF_HALF), lambda i, j: (i, j)),
                         (jax.ShapeDtypeStruct((ntok, D_FF), BF16), (EP_TILE, FF_HALF), lambda i, j: (i, j))],
                   epilogue=epilogue)


def _down_dgrad_silu_bwd(dy2, wd, h_i):
    ntok = dy2.shape[0]
    wide = ((EP_TILE, 2 * FF_HALF), lambda i, j: (i, j))

    def epilogue(da, i, j, ins, outs):
        h = ins[0][...].astype(F32)
        hg, hu = h[:, :FF_HALF], h[:, FF_HALF:]
        sg = _sigmoid(hg)
        outs[0][:, :FF_HALF] = (da * hu * (sg * (1.0 + hg * (1.0 - sg)))).astype(BF16)
        outs[0][:, FF_HALF:] = (da * (hg * sg)).astype(BF16)

    return _matmul(dy2, wd, mode="nt", tm=EP_TILE, tn=FF_HALF, tk=dy2.shape[1], name="down_dgrad_silu_bwd",
                   ins=[(h_i,) + wide], outs=[(jax.ShapeDtypeStruct((ntok, 2 * D_FF), BF16),) + wide], epilogue=epilogue)[0]


def _down_ln2_loss_bwd(a, wd, h1, mod, g, b, target, seq):
    ntok, d = h1.shape
    row, per_b, whole = _ep_specs(seq, d)
    tiles = seq // EP_TILE

    def epilogue(y, i, j, ins, outs):
        h_ref, mod_ref, g_ref, b_ref, t_ref = ins
        dy_ref, dh_ref, acc_ref = outs
        gate = 1.0 + mod_ref[0, 5:6, :]
        z = DEEPNORM_ALPHA * h_ref[...] + gate * y
        xhat, rstd = _ln_stats(z)
        diff = xhat * g_ref[...] + b_ref[...] - t_ref[...]
        loss = 0.5 * jnp.sum(jnp.sum(diff * diff, axis=-1, keepdims=True) / d, axis=0, keepdims=True)
        dout = diff / d
        dz = _ln_bwd(dout * g_ref[...], xhat, rstd)
        dy_ref[...] = (gate * dz).astype(BF16)
        dh_ref[...] = DEEPNORM_ALPHA * dz
        _acc_rows(acc_ref, i % tiles == 0,
                  [_colsum(dout * xhat), _colsum(dout), _colsum(dz * y), jnp.broadcast_to(loss, (1, d))])

    return _matmul(a, wd, mode="nn", tm=EP_TILE, tn=d, tk=a.shape[1], name="down_ln2_loss_bwd",
                   ins=[(h1,) + row, (mod,) + per_b, (g,) + whole, (b,) + whole, (target,) + row],
                   outs=[(jax.ShapeDtypeStruct((ntok, d), BF16),) + row, (jax.ShapeDtypeStruct((ntok, d), F32),) + row,
                         (jax.ShapeDtypeStruct((ntok // seq, 8, d), F32),) + per_b], epilogue=epilogue)


def _gate_up_dgrad_ln1_bwd(dh, wgut, dh1a, x, y1, mod, g, b, seq):
    ntok, d = x.shape
    row, per_b, whole = _ep_specs(seq, d)
    tiles = seq // EP_TILE

    def epilogue(du, i, j, ins, outs):
        dh_ref, x_ref, y_ref, mod_ref, g_ref, b_ref = ins
        dy_ref, dx_ref, acc_ref = outs
        y = y_ref[...]
        gate = 1.0 + mod_ref[0, 2:3, :]
        z = DEEPNORM_ALPHA * x_ref[...] + gate * y
        xhat, rstd = _ln_stats(z)
        h1 = xhat * g_ref[...] + b_ref[...]
        dh1 = dh_ref[...] + du * (1.0 + mod_ref[0, 4:5, :])
        dz = _ln_bwd(dh1 * g_ref[...], xhat, rstd)
        dy_ref[...] = (gate * dz).astype(BF16)
        dx_ref[...] = DEEPNORM_ALPHA * dz
        _acc_rows(acc_ref, i % tiles == 0,
                  [_colsum(dh1 * xhat), _colsum(dh1), _colsum(dz * y), _colsum(du * h1), _colsum(du)])

    return _matmul(dh, wgut, mode="nn", tm=EP_TILE, tn=d, tk=D_FF, name="gate_up_dgrad_ln1_bwd",
                   ins=[(dh1a,) + row, (x,) + row, (y1,) + row, (mod,) + per_b, (g,) + whole, (b,) + whole],
                   outs=[(jax.ShapeDtypeStruct((ntok, d), BF16),) + row, (jax.ShapeDtypeStruct((ntok, d), F32),) + row,
                         (jax.ShapeDtypeStruct((ntok // seq, 8, d), F32),) + per_b], epilogue=epilogue)


def _wo_dgrad_gate_bwd(dy1, wo, gab, ya, yb):
    ntok, d = ya.shape
    tm, tn = 1024, 512
    tile = ((tm, tn), lambda i, j: (i, j))
    tile_b = ((tm, tn), lambda i, j: (i, j + d // tn))

    def epilogue(dm_, i, j, ins, outs):
        ga_ref, gb_ref, ya_ref, yb_ref = ins
        dya_ref, dyb_ref, dga_ref, dgb_ref = outs
        sa, sb = _sigmoid(ga_ref[...].astype(F32)), _sigmoid(gb_ref[...].astype(F32))
        dya_ref[...] = (dm_ * sa).astype(BF16)
        dyb_ref[...] = (dm_ * sb).astype(BF16)
        dga_ref[...] = (dm_ * ya_ref[...].astype(F32) * sa * (1.0 - sa)).astype(BF16)
        dgb_ref[...] = (dm_ * yb_ref[...].astype(F32) * sb * (1.0 - sb)).astype(BF16)

    shp = jax.ShapeDtypeStruct((ntok, d), BF16)
    return _matmul(dy1, wo, mode="nt", tm=tm, tn=tn, tk=d, name="w_o_dgrad_gate_bwd",
                   ins=[(gab,) + tile, (gab,) + tile_b, (ya,) + tile, (yb,) + tile],
                   outs=[(shp,) + tile] * 4, epilogue=epilogue)


def _w_in_dgrad_grad_x(dproj, wint, dxa, x, mod, seq, token):
    ntok, d = x.shape
    row, per_b, _ = _ep_specs(seq, d)
    tiles = seq // EP_TILE

    def epilogue(du, i, j, ins, outs):
        dxa_ref, x_ref, mod_ref = ins
        gx_ref, acc_ref = outs
        gx_ref[...] = dxa_ref[...] + du * (1.0 + mod_ref[0, 1:2, :])
        _acc_rows(acc_ref, i % tiles == 0, [_colsum(du * x_ref[...]), _colsum(du)])

    return _matmul(dproj, wint, mode="nn", tm=EP_TILE, tn=d, tk=wint.shape[0] // 2, name="w_in_dgrad_grad_x", token=token,
                   ins=[(dxa,) + row, (x,) + row, (mod,) + per_b],
                   outs=[(jax.ShapeDtypeStruct((ntok, d), F32),) + row, (jax.ShapeDtypeStruct((ntok // seq, 8, d), F32),) + per_b],
                   epilogue=epilogue)


def _merge_branch_b_gate(os_, ls_, wbbt, gab, ya):
    ntok, d = ya.shape
    w = os_[0].shape[1]
    tm, tn = 1024, 512
    tile = ((tm, tn), lambda i, j: (i, j))
    tile_b = ((tm, tn), lambda i, j: (i, j + d // tn))
    row = ((tm, w), lambda i, j: (i, 0))

    def lhs_fn(o0_ref, ins, outs):
        os_r, ls_r = (o0_ref,) + tuple(ins[3:5]), ins[5:8]
        ls = [l[...] for l in ls_r]
        mx = jnp.maximum(jnp.maximum(ls[0], ls[1]), ls[2])
        es = [jnp.exp(l - mx) for l in ls]
        den = es[0] + es[1] + es[2]
        ob = functools.reduce(jnp.add, [(e / den) * o[...].astype(F32) for e, o in zip(es, os_r)]).astype(BF16)
        outs[2][...] = ob
        return ob

    def epilogue(yb, i, j, ins, outs):
        ga_ref, gb_ref, ya_ref = ins[:3]
        yb_ref, merged_ref = outs[:2]
        yb_ref[...] = yb.astype(BF16)
        merged_ref[...] = (_sigmoid(ga_ref[...].astype(F32)) * ya_ref[...].astype(F32)
                           + _sigmoid(gb_ref[...].astype(F32)) * yb).astype(BF16)

    shp = jax.ShapeDtypeStruct((ntok, d), BF16)
    return _matmul(os_[0], wbbt, mode="nt", tm=tm, tn=tn, tk=w, name="merge_branch_b_gate", lhs_fn=lhs_fn,
                   ins=[(gab,) + tile, (gab,) + tile_b, (ya,) + tile] + [(v,) + row for v in list(os_[1:]) + list(ls_)],
                   outs=[(shp,) + tile] * 2 + [(jax.ShapeDtypeStruct((ntok, w), BF16),) + row], epilogue=epilogue)


def _segsum64(v):
    rows, width = v.shape
    ri = lax.broadcasted_iota(jnp.int32, (LANES, LANES), 0) // HEAD_DIM
    ci = lax.broadcasted_iota(jnp.int32, (LANES, LANES), 1) // HEAD_DIM
    ones = jnp.where(ri == ci, 1.0, 0.0).astype(BF16)
    out = []
    for c in range(width // LANES):
        part = v[:, c * LANES:(c + 1) * LANES]
        hi = part.astype(BF16)
        lo = (part - hi.astype(F32)).astype(BF16)
        out.append(jnp.dot(hi, ones, preferred_element_type=F32) + jnp.dot(lo, ones, preferred_element_type=F32))
    return jnp.concatenate(out, axis=1) if len(out) > 1 else out[0]


def _branch_b_dgrad_merge_bwd(dyb, wbbt, os_, ls_):
    ntok, w = os_[0].shape
    row = ((EP_TILE, w), lambda i, j: (i, 0))

    def epilogue(dob_, i, j, ins, outs):
        os_r, ls_r = ins[:3], ins[3:]
        do_r, dd_r = outs[:3], outs[3:]
        ls = [l[...] for l in ls_r]
        mx = jnp.maximum(jnp.maximum(ls[0], ls[1]), ls[2])
        es = [jnp.exp(l - mx) for l in ls]
        den = es[0] + es[1] + es[2]
        ws = [e / den for e in es]
        dws = [_segsum64(dob_ * o[...].astype(F32)) for o in os_r]
        mean = ws[0] * dws[0] + ws[1] * dws[1] + ws[2] * dws[2]
        for wg, do_ref, dd_ref in zip(ws, do_r, dd_r):
            do_ref[...] = (wg * dob_).astype(do_ref.dtype)
            dd_ref[...] = -wg * mean

    shp = jax.ShapeDtypeStruct((ntok, w), F32)
    return _matmul(dyb, wbbt, mode="nn", tm=EP_TILE, tn=w, tk=dyb.shape[1], name="branch_b_dgrad_merge_bwd",
                   ins=[(v,) + row for v in list(os_) + list(ls_)],
                   outs=[(jax.ShapeDtypeStruct((ntok, w), BF16),) + row] + [(shp,) + row] * 5, epilogue=epilogue)


def _branch_a_dgrad_delta(dya, wba, oa, lse_a, sinks_exp, seq):
    ntok, w = oa.shape
    row, per_b, whole = _ep_specs(seq, w)
    tiles = seq // EP_TILE

    def epilogue(do_, i, j, ins, outs):
        o_ref, l_ref, s_ref = ins
        do_ref, dd_ref, acc_ref = outs
        dd = -_segsum64(do_ * o_ref[...].astype(F32))
        do_ref[...] = do_.astype(BF16)
        dd_ref[...] = dd
        _acc_rows(acc_ref, i % tiles == 0, [_colsum(dd * jnp.exp(s_ref[...] - l_ref[...]))])

    shp = jax.ShapeDtypeStruct((ntok, w), F32)
    return _matmul(dya, wba, mode="nt", tm=EP_TILE, tn=w, tk=dya.shape[1], name="branch_a_dgrad_delta",
                   ins=[(oa,) + row, (lse_a,) + row, (sinks_exp,) + whole],
                   outs=[(jax.ShapeDtypeStruct((ntok, w), BF16),) + row, (shp,) + row,
                         (jax.ShapeDtypeStruct((ntok // seq, 8, w), F32),) + per_b],
                   epilogue=epilogue)


def _swap_halves(v):
    src = lax.broadcasted_iota(jnp.int32, (LANES, LANES), 0)
    dst = lax.broadcasted_iota(jnp.int32, (LANES, LANES), 1)
    partner = jnp.where((dst % HEAD_DIM) < HEAD_DIM // 2, dst + HEAD_DIM // 2, dst - HEAD_DIM // 2)
    perm = jnp.where(src == partner, 1.0, 0.0).astype(BF16)
    hi = v.astype(BF16)
    lo = (v - hi.astype(F32)).astype(BF16)
    return jnp.dot(hi, perm, preferred_element_type=F32) + jnp.dot(lo, perm, preferred_element_type=F32)


def _swap_halves_roll(v):
    lane = lax.broadcasted_iota(jnp.int32, v.shape, 1)
    return jnp.where((lane % HEAD_DIM) < HEAD_DIM // 2, pltpu.roll(v, LANES - HEAD_DIM // 2, 1),
                     pltpu.roll(v, HEAD_DIM // 2, 1))


def _swap_halves_coarse(v):
    src = lax.broadcasted_iota(jnp.int32, (LANES, LANES), 0)
    dst = lax.broadcasted_iota(jnp.int32, (LANES, LANES), 1)
    partner = jnp.where((dst % HEAD_DIM) < HEAD_DIM // 2, dst + HEAD_DIM // 2, dst - HEAD_DIM // 2)
    perm = jnp.where(src == partner, 1.0, 0.0).astype(BF16)
    return jnp.dot(v.astype(BF16), perm, preferred_element_type=F32)


def _rope(v, cos, sin, sign=1.0, mxu=True, coarse=False):
    swap = (_swap_halves_coarse if coarse else _swap_halves) if mxu else _swap_halves_roll
    out = []
    for c in range(v.shape[1] // LANES):
        part = v[:, c * LANES:(c + 1) * LANES]
        out.append(part * cos + sign * (swap(part) * sin))
    return jnp.concatenate(out, axis=1) if len(out) > 1 else out[0]


def _half_mask(shape, half):
    lane = lax.broadcasted_iota(jnp.int32, shape, len(shape) - 1) % LANES
    return (lane < HEAD_DIM) if half == 0 else (lane >= HEAD_DIM)


def _dup_half(v, half):
    return jnp.where(_half_mask(v.shape, half), v, pltpu.roll(v, HEAD_DIM, 1))


def _fold_halves(v):
    return v + pltpu.roll(v, HEAD_DIM, 1)


def _pick_halves(lo_rows, hi_rows):
    return jnp.where(_half_mask(lo_rows.shape, 0), lo_rows, hi_rows)


def _stack_masked(v, pairs):
    parts = []
    for c in pairs:
        pair = v[:, c * LANES:(c + 1) * LANES]
        parts += [jnp.where(_half_mask(pair.shape, half), pair, 0.0) for half in (0, 1)]
    return jnp.concatenate(parts, axis=0)


def _stack_pair_cols(v, pairs):
    return jnp.concatenate([v[:, c * LANES + half * HEAD_DIM:c * LANES + half * HEAD_DIM + 1] for c in pairs for half in (0, 1)],
                           axis=0)


ATTN_UNITS = 16


def _class_rows(r):
    return [pl.ds(0, QBLOCK)] if r == 1 else [pl.ds(rho, QBLOCK, stride=r) for rho in range(r)]


def _band_mask(nrows, nk, blk, n_back, has_prev):
    qi = lax.broadcasted_iota(jnp.int32, (nrows, nk), 0) % QBLOCK
    ki = lax.broadcasted_iota(jnp.int32, (nrows, nk), 1)
    if has_prev:
        dist = qi + QBLOCK - ki
        return (dist >= 0) & (dist <= n_back) & ((ki >= QBLOCK) | (blk > 0))
    dist = qi - ki
    return (dist >= 0) & (dist <= n_back)


def _attn_fwd(q_arr, k_arr, v_arr, *, name, npair, gqa, q_col, k_col, v_col, nchunk, r, n_back, sinks=None):
    bsz, seq, _ = q_arr.shape
    rr = QBLOCK * r
    nblk = seq // rr
    qw = npair * LANES
    kw = LANES if gqa else qw
    has_prev = nblk > 1
    has_sink = sinks is not None
    scale = HEAD_DIM ** -0.5

    def body(*refs):
        refs = list(refs)
        q_ref, kc_ref, vc_ref = refs[:3]
        pos = 3
        if has_prev:
            kp_ref, vp_ref = refs[pos:pos + 2]
            pos += 2
        if has_sink:
            sink_ref = refs[pos]
            pos += 1
        o_ref, lse_ref = refs[pos:pos + 2]
        if r > 1:
            stage_o = refs[pos + 2]
        blk = pl.program_id(2)
        nk = (2 if has_prev else 1) * QBLOCK
        valid = _band_mask(QBLOCK, nk, blk, n_back, has_prev)
        per = npair // 2
        classes = _class_rows(r)
        step = max(1, ATTN_UNITS // (2 * npair))
        for first in range(0, len(classes), step):
            batch = classes[first:first + step]
            units = []
            for ci, rows in enumerate(batch):
                q = q_ref[0, rows, :] * scale
                k, v = kc_ref[0, rows, :], vc_ref[0, rows, :]
                if has_prev:
                    k = jnp.concatenate([kp_ref[0, rows, :], k], axis=0)
                    v = jnp.concatenate([vp_ref[0, rows, :], v], axis=0)
                if gqa:
                    kdup = [_dup_half(k, hk).astype(BF16) for hk in range(2)]
                    vdup = [_dup_half(v, hk) for hk in range(2)]
                for c in range(npair):
                    sl = slice(c * LANES, (c + 1) * LANES)
                    qc = q[:, sl]
                    kc, vc = (kdup[c // per], vdup[c // per]) if gqa else (k[:, sl].astype(BF16), v[:, sl])
                    for half in (0, 1):
                        qm = jnp.where(_half_mask(qc.shape, half), qc, 0.0).astype(BF16)
                        vm = jnp.where(_half_mask(vc.shape, half), vc, 0.0).astype(BF16)
                        s = lax.dot_general(qm, kc, _DIMS["nt"], preferred_element_type=F32)
                        units.append(dict(ci=ci, c=c, half=half, s=s, vm=vm, sk=sink_ref[2 * c + half] if has_sink else None))
            for u in units:
                s = jnp.where(valid, u["s"], NEG_INF)
                m = jnp.max(s, axis=1, keepdims=True)
                if has_sink:
                    m = jnp.maximum(m, u["sk"])
                p = jnp.exp(s - m)
                den = jnp.sum(p, axis=1, keepdims=True)
                if has_sink:
                    den = den + jnp.exp(u["sk"] - m)
                u.update(p=p.astype(BF16), den=den, lse=m + jnp.log(den))
            for u in units:
                u["o"] = jnp.dot(u["p"], u["vm"], preferred_element_type=F32) / u["den"]
            for ci, rows in enumerate(batch):
                outs, lses = [None] * npair, [None] * npair
                for u in units:
                    if u["ci"] != ci:
                        continue
                    c, o = u["c"], u["o"]
                    lse = jnp.broadcast_to(u["lse"], o.shape)
                    outs[c] = o if u["half"] == 0 else outs[c] + o
                    lses[c] = lse if u["half"] == 0 else _pick_halves(lses[c], lse)
                o_new = jnp.concatenate(outs, axis=1) if npair > 1 else outs[0]
                if r > 1:
                    stage_o[rows, :] = o_new
                else:
                    o_ref[0] = o_new.astype(BF16)
                lse_ref[0, rows, :] = jnp.concatenate(lses, axis=1) if npair > 1 else lses[0]
        if r > 1:
            o_ref[0] = stage_o[...].astype(BF16)

    def cur(width, col0):
        return pl.BlockSpec((1, rr, width), lambda b, c, i: (b, i, col0 + c))

    def prev(width, col0):
        return pl.BlockSpec((1, rr, width), lambda b, c, i: (b, jnp.maximum(i - 1, 0), col0 + c))

    in_specs = [cur(qw, q_col), cur(kw, k_col), cur(kw, v_col)]
    args = [q_arr, k_arr, v_arr]
    if has_prev:
        in_specs += [prev(kw, k_col), prev(kw, v_col)]
        args += [k_arr, v_arr]
    if has_sink:
        in_specs.append(pl.BlockSpec(memory_space=pltpu.SMEM))
        args.append(sinks)
    return pl.pallas_call(
        body,
        name=name,
        grid=(bsz, nchunk, nblk),
        in_specs=in_specs,
        out_specs=[pl.BlockSpec((1, rr, qw), lambda b, c, i: (b, i, c))] * 2,
        out_shape=[jax.ShapeDtypeStruct((bsz, seq, nchunk * qw), BF16), jax.ShapeDtypeStruct((bsz, seq, nchunk * qw), F32)],
        scratch_shapes=[pltpu.VMEM((rr, qw), F32)] if r > 1 else [],
        compiler_params=_params("parallel", "parallel", "parallel"),
    )(*args)


def _attn_bwd(q_arr, k_arr, v_arr, cos, sin, do, lse, dd, *, name, npair, gqa, q_col, k_col, v_col, nchunk, r, n_back,
              token=None):
    bsz, seq, _ = q_arr.shape
    rr = QBLOCK * r
    nblk = seq // rr
    qw = npair * LANES
    kw = LANES if gqa else qw
    has_next = nblk > 1
    has_token = token is not None
    staged = r > 1
    scale = HEAD_DIM ** -0.5

    def body(*refs):
        refs = list(refs)
        k_ref, v_ref, c_ref, s_ref = refs[:4]
        tile_refs = [refs[4:8]]
        pos = 8
        if has_next:
            tile_refs.append(refs[pos:pos + 4])
            pos += 4
        if has_token:
            pos += 1
        dq_ref, dk_ref, dv_ref = refs[pos:pos + 3]
        carry_ref, bcast_ref = refs[pos + 3:pos + 5]
        if staged:
            stage_q, stage_k, stage_v = refs[pos + 5:pos + 8]
        blk = pl.program_id(2)
        if has_next:
            @pl.when(blk == 0)
            def _():
                carry_ref[...] = jnp.zeros_like(carry_ref)

        nrows = (npair if gqa else 1) * QBLOCK
        qi = lax.broadcasted_iota(jnp.int32, (nrows, QBLOCK), 0) % QBLOCK
        ki = lax.broadcasted_iota(jnp.int32, (nrows, QBLOCK), 1)
        valids = [qi >= ki, (qi + QBLOCK - ki <= n_back) & (blk + 1 < nblk)]
        per = npair // 2
        ntile = len(tile_refs)
        cat = lambda parts: jnp.concatenate(parts, axis=1) if len(parts) > 1 else parts[0]
        classes = _class_rows(r)
        step = max(1, ATTN_UNITS // (ntile * (2 if gqa else 2 * npair)))
        def stat_cols(stat, slot):
            if gqa:
                return _stack_pair_cols(stat, list(range(slot * per, (slot + 1) * per)))
            col = slot * HEAD_DIM
            return stat[:, col:col + 1]

        nslot = 2 if gqa else 2 * npair
        if has_next:
            @pl.when(blk == 0)
            def _():
                for rows in classes:
                    for which, stat_ref in enumerate(tile_refs[0][2:4]):
                        stat = stat_ref[0, rows, :]
                        for slot in range(nslot):
                            bcast_ref[which, slot, rows if not gqa else slice(None), :] = jnp.broadcast_to(
                                stat_cols(stat, slot), (nrows, LANES))

        for first in range(0, len(classes), step):
            batch = classes[first:first + step]
            units = []
            for ci, rows in enumerate(batch):
                keep = slice(None) if gqa else rows
                tiles = [(q_ref[0, rows, :] * scale, do_ref[0, rows, :], l_ref[0, rows, :], d_ref[0, rows, :])
                         for q_ref, do_ref, l_ref, d_ref in tile_refs]

                def stats(t, slot, keep=keep, tiles=tiles):
                    if has_next and t == 0:
                        return bcast_ref[0, slot, keep, :], bcast_ref[1, slot, keep, :]
                    return tuple(jnp.broadcast_to(stat_cols(tiles[t][2 + w], slot), (nrows, LANES)) for w in range(2))

                k, v = k_ref[0, rows, :], v_ref[0, rows, :]
                if gqa:
                    for hk in range(2):
                        pairs = list(range(hk * per, (hk + 1) * per))
                        kd, vd = _dup_half(k, hk).astype(BF16), _dup_half(v, hk).astype(BF16)
                        for t, (q, do_, l_, d_) in enumerate(tiles):
                            lcol, dcol = stats(t, hk)
                            units.append(dict(ci=ci, t=t, hk=hk, slot=hk, keep=keep, pairs=pairs,
                                              qs=_stack_masked(q, pairs).astype(BF16),
                                              dos=_stack_masked(do_, pairs).astype(BF16), lcol=lcol, dcol=dcol,
                                              kmat=kd, vmat=vd, kdq=kd))
                else:
                    for c in range(npair):
                        sl = slice(c * LANES, (c + 1) * LANES)
                        kc, vcb = k[:, sl], v[:, sl].astype(BF16)
                        kcb = kc.astype(BF16)
                        for t, (q, do_, l_, d_) in enumerate(tiles):
                            for half in (0, 1):
                                hm = _half_mask(kc.shape, half)
                                lcol, dcol = stats(t, 2 * c + half)
                                units.append(dict(ci=ci, t=t, c=c, half=half, slot=2 * c + half, keep=keep,
                                                  qs=jnp.where(hm, q[:, sl], 0.0).astype(BF16),
                                                  dos=jnp.where(hm, do_[:, sl], 0.0).astype(BF16), lcol=lcol, dcol=dcol,
                                                  kmat=kcb, vmat=vcb, kdq=jnp.where(hm, kc, 0.0).astype(BF16)))
            for u in units:
                u["s"] = lax.dot_general(u["qs"], u["kmat"], _DIMS["nt"], preferred_element_type=F32)
                u["dp"] = lax.dot_general(u["dos"], u["vmat"], _DIMS["nt"], preferred_element_type=F32)
            for u in units:
                p = jnp.exp(jnp.where(valids[u["t"]], u["s"], NEG_INF) - u["lcol"])
                u["ds"] = (p * (u["dp"] + u["dcol"])).astype(BF16)
                u["p"] = p.astype(BF16)
            for u in units:
                u["dv"] = lax.dot_general(u["p"], u["dos"], _DIMS["tn"], preferred_element_type=F32)
                u["dk"] = lax.dot_general(u["ds"], u["qs"], _DIMS["tn"], preferred_element_type=F32)
                u["dq"] = jnp.dot(u["ds"], u["kdq"], preferred_element_type=F32) * scale
            for u in units:
                if u["t"] == 1:
                    bcast_ref[0, u["slot"], u["keep"], :] = u["lcol"]
                    bcast_ref[1, u["slot"], u["keep"], :] = u["dcol"]
            for ci, rows in enumerate(batch):
                mine = [u for u in units if u["ci"] == ci]
                dq = [[None] * npair for _ in range(ntile)]
                if gqa:
                    dk_out = dv_out = None
                    for hk in range(2):
                        us = [u for u in mine if u["hk"] == hk]
                        for u in us:
                            for i, c in enumerate(u["pairs"]):
                                dq[u["t"]][c] = _pick_halves(u["dq"][2 * i * QBLOCK:(2 * i + 1) * QBLOCK],
                                                             u["dq"][(2 * i + 1) * QBLOCK:(2 * i + 2) * QBLOCK])
                        dk_h = _fold_halves(functools.reduce(jnp.add, [u["dk"] for u in us]))
                        dv_h = _fold_halves(functools.reduce(jnp.add, [u["dv"] for u in us]))
                        dk_out = dk_h if hk == 0 else _pick_halves(dk_out, dk_h)
                        dv_out = dv_h if hk == 0 else _pick_halves(dv_out, dv_h)
                else:
                    dks, dvs = [], []
                    for c in range(npair):
                        us = [u for u in mine if u["c"] == c]
                        dks.append(functools.reduce(jnp.add, [u["dk"] for u in us]))
                        dvs.append(functools.reduce(jnp.add, [u["dv"] for u in us]))
                        for t in range(ntile):
                            dq[t][c] = functools.reduce(jnp.add, [u["dq"] for u in us if u["t"] == t])
                    dk_out, dv_out = cat(dks), cat(dvs)
                ck, sk_ = c_ref[0, rows, :], s_ref[0, rows, :]
                dk_new = _rope(dk_out, ck, sk_, sign=-1.0, mxu=gqa, coarse=True)
                dq_cur = cat(dq[0])
                if has_next:
                    dq_cur = dq_cur + carry_ref[rows, :]
                    carry_ref[rows, :] = cat(dq[1])
                dq_new = _rope(dq_cur, ck, sk_, sign=-1.0, mxu=gqa, coarse=True)
                if staged:
                    stage_q[rows, :], stage_k[rows, :], stage_v[rows, :] = dq_new, dk_new, dv_out
                else:
                    dq_ref[0], dk_ref[0], dv_ref[0] = dq_new.astype(BF16), dk_new.astype(BF16), dv_out.astype(BF16)
        if staged:
            dq_ref[0], dk_ref[0], dv_ref[0] = stage_q[...].astype(BF16), stage_k[...].astype(BF16), stage_v[...].astype(BF16)

    def at(width, col0, shift):
        return pl.BlockSpec((1, rr, width), lambda b, c, i: (b, jnp.minimum(i + shift, nblk - 1), col0 + c))

    in_specs = [at(kw, k_col, 0), at(kw, v_col, 0), pl.BlockSpec((1, rr, LANES), lambda b, c, i: (b, i, 0)),
                pl.BlockSpec((1, rr, LANES), lambda b, c, i: (b, i, 0))]
    args = [k_arr, v_arr, cos, sin]
    for shift in (0, 1) if has_next else (0,):
        in_specs += [at(qw, q_col, shift), at(qw, 0, shift), at(qw, 0, shift), at(qw, 0, shift)]
        args += [q_arr, do, lse, dd]
    if has_token:
        in_specs.append(pl.BlockSpec(token.shape, lambda b, c, i: (0, 0)))
        args.append(token)
    return pl.pallas_call(
        body,
        name=name,
        grid=(bsz, nchunk, nblk),
        in_specs=in_specs,
        out_specs=[pl.BlockSpec((1, rr, qw), lambda b, c, i: (b, i, c)),
                   pl.BlockSpec((1, rr, kw), lambda b, c, i: (b, i, c)),
                   pl.BlockSpec((1, rr, kw), lambda b, c, i: (b, i, c))],
        out_shape=[jax.ShapeDtypeStruct((bsz, seq, nchunk * qw), BF16),
                   jax.ShapeDtypeStruct((bsz, seq, nchunk * kw), BF16),
                   jax.ShapeDtypeStruct((bsz, seq, nchunk * kw), BF16)],
        scratch_shapes=[pltpu.VMEM((rr, qw) if has_next else (8, LANES), F32),
                        pltpu.VMEM((2, 2 if gqa else 2 * npair, npair * QBLOCK if gqa else rr, LANES) if has_next
                                   else (1, 1, 8, LANES), F32)] +
                       ([pltpu.VMEM((rr, qw), F32), pltpu.VMEM((rr, kw), F32), pltpu.VMEM((rr, kw), F32)] if staged else []),
        compiler_params=_params("parallel", "parallel", "arbitrary"),
    )(*args)


B_CHUNKS = {1: (4, 1), 4: (1, 4), 16: (1, 4)}
assert B_PATTERNS[0][1] == 1


def _rope_tables(positions):
    half = HEAD_DIM // 2
    inv = ROPE_THETA ** (-jnp.arange(half, dtype=F32) / half)
    ang = positions.astype(F32)[..., None] * inv
    cos, sin = jnp.cos(ang), jnp.sin(ang)
    return jnp.concatenate([cos] * 4, axis=-1), jnp.concatenate([-sin, sin, -sin, sin], axis=-1)


def _layer_step(x, mod, tables, sinks, ln1_g, ln1_b, ln2_g, ln2_b, target, get_w_in, get_rest, hook):
    bsz, seq, d = x.shape
    ntok = bsz * seq
    flat = lambda v: v.reshape(ntok, v.shape[-1])
    unflat = lambda v: v.reshape(bsz, seq, v.shape[-1])
    cos, sin = tables
    mm = functools.partial(_matmul, tm=1024, tk=1024)
    scalar = lambda tok: 0.0 if tok is None else tok[0, 0]

    u1 = _modulate_in(x, mod)
    u1f = flat(u1)
    wint = get_w_in(u1)
    cosf, sinf = flat(cos), flat(sin)
    proj = functools.partial(_proj_rope, u1f, wint, cosf, sinf, tm=2048)
    qkvb = unflat(proj(n=4608, b_off=OFF_QKVB, rope_cols=3072, tn=256, name="proj_qkvb"))
    b_kws, os_, ls_ = [], [], []
    for g, (window, r) in enumerate(B_PATTERNS):
        npair, nch = B_CHUNKS[r]
        per = B_HEADS_PER_GROUP // (2 * npair)
        nsec = len(B_PATTERNS) * per
        kw_ = dict(npair=npair, gqa=False, q_col=g * per, k_col=nsec + g * per, v_col=2 * nsec + g * per, nchunk=nch, r=r,
                   n_back=window // r)
        b_kws.append(kw_)
        o_g, l_g = _attn_fwd(qkvb, qkvb, qkvb, name=f"attn_b{g}_fwd", **kw_)
        os_.append(o_g)
        ls_.append(l_g)
    tok = hook("projected", os_[-1])
    proj = functools.partial(_proj_rope, u1f, wint, cosf + scalar(tok), sinf, tm=2048)
    gab = unflat(proj(n=2048, b_off=OFF_GAB, rope_cols=0, tn=256, name="proj_gab", out_dtype=BF16))
    qa = kva = unflat(proj(n=OFF_QKVB, b_off=OFF_QA, rope_cols=OFF_KVA + LANES, tn=256, name="proj_qkva", out_dtype=BF16))
    a_kw = dict(npair=A_Q_HEADS // 2, gqa=True, q_col=0, k_col=OFF_KVA // LANES, v_col=OFF_KVA // LANES + 1, nchunk=1, r=1,
                n_back=A_WINDOW - 1)
    after_gab = jnp.minimum(jnp.abs(gab[0, 0, 0].astype(F32)), 0.0)
    oa, lse_a = _attn_fwd(qa, kva, kva, name="attn_a_fwd", sinks=sinks.reshape(A_Q_HEADS) + after_gab, **a_kw)
    rest = get_rest(oa)
    wba, wbbt, wo, wgut, wd = (rest[n] for n in ("w_branch_a", "w_branch_b", "w_o", "w_gate_up", "w_down"))
    ya = unflat(mm(flat(oa), wba, mode="nn", out_dtype=BF16, tn=512, name="branch_a"))
    ybf, mergedf, obf = _merge_branch_b_gate([flat(t) for t in os_], [flat(t) for t in ls_], wbbt, flat(gab), flat(ya))
    xf = flat(x)
    y1f, h1f, u2f = _wo_ln1(mergedf, wo, xf, mod, ln1_g, ln1_b, seq)
    wgut_i = _interleave_gate_up(wgut)
    hf, af = _gate_up_silu(u2f, wgut_i)

    dy2f, dh1af, acc2 = _down_ln2_loss_bwd(af, wd, h1f, mod, ln2_g, ln2_b, flat(target), seq)
    g_wd = _matmul(af, dy2f, mode="tn", out_dtype=BF16, tm=256, tn=1024, tk=ntok, name="down_wgrad")
    dhf = _down_dgrad_silu_bwd(dy2f, wd, hf)
    g_wgut = _interleave_gate_up(_matmul(dhf, u2f, mode="tn", out_dtype=BF16, tm=256, tn=1024, tk=ntok, name="gate_up_wgrad"))
    dy1f, dxaf, acc1 = _gate_up_dgrad_ln1_bwd(dhf, wgut_i, dh1af, xf, y1f, mod, ln1_g, ln1_b, seq)
    g_wo = _matmul(mergedf, dy1f, mode="tn", out_dtype=BF16, tm=256, tn=1024, tk=ntok, name="w_o_wgrad")
    dyaf, dybf, dgaf, dgbf = _wo_dgrad_gate_bwd(dy1f, wo, flat(gab), flat(ya), ybf)
    g_wba = _matmul(flat(oa), dyaf, mode="tn", out_dtype=BF16, tm=256, tn=1024, tk=ntok, name="branch_a_wgrad")
    g_wbbt = _matmul(dybf, obf, mode="tn", out_dtype=BF16, tm=256, tn=512, tk=ntok, name="branch_b_wgrad")
    tok = hook("grads_rest", dict(w_branch_a=g_wba, w_branch_b=g_wbbt, w_o=g_wo, w_gate_up=g_wgut, w_down=g_wd))

    sinks_exp = jnp.repeat(sinks.reshape(1, A_Q_HEADS), HEAD_DIM, axis=1) + scalar(tok)
    doa, dd_a, acc_s = _branch_a_dgrad_delta(dyaf, wba, flat(oa), flat(lse_a), sinks_exp, seq)
    doa, dd_a = unflat(doa), unflat(dd_a)
    tok = hook("delta_done", dd_a)
    dqa, dka, dva = _attn_bwd(qa, kva, kva, cos, sin, doa, lse_a, dd_a, name="attn_a_bwd", token=tok, **a_kw)
    merged_bwd = [unflat(t) for t in _branch_b_dgrad_merge_bwd(dybf, wbbt, [flat(t) for t in os_], [flat(t) for t in ls_])]
    dqs, dks, dvs = [], [], []
    for g in range(len(B_PATTERNS)):
        dq_g, dk_g, dv_g = _attn_bwd(qkvb, qkvb, qkvb, cos, sin, merged_bwd[g], ls_[g], merged_bwd[3 + g],
                                     name=f"attn_b{g}_bwd", **b_kws[g])
        dqs.append(dq_g)
        dks.append(dk_g)
        dvs.append(dv_g)
    dproj = jnp.concatenate([t.astype(BF16) for t in [dqa, dka, dva] + dqs + dks + dvs] + [unflat(dgaf), unflat(dgbf)], axis=-1)
    dprojf = flat(dproj)
    g_wint = _matmul(dprojf, u1f, mode="tn", out_dtype=BF16, tm=256, tn=1024, tk=ntok, name="w_in_wgrad")
    tok = hook("grads_w_in", dict(w_in=g_wint))
    grad_x, acc0 = _w_in_dgrad_grad_x(dprojf, wint, dxaf, xf, mod, seq, tok)
    grad_x = unflat(grad_x)
    tok = hook("dgrad_done", grad_x)

    loss_part = jnp.sum(acc2[:, 3, 0])
    dmod = jnp.stack([acc0[:, 1], acc0[:, 0], acc1[:, 2], acc1[:, 4], acc1[:, 3], acc2[:, 2]], axis=1)
    small = jnp.stack([acc1[:, 0].sum(0), acc1[:, 1].sum(0), acc2[:, 0].sum(0), acc2[:, 1].sum(0), acc_s[:, 0].sum(0)])
    small = small + scalar(tok)
    return loss_part, grad_x, dmod, small


CHIP_FLIPS = (2, 4, 6)


def _my_place():
    return lax.axis_index("x"), lax.axis_index("y"), lax.axis_index("c")


def _flip(place, k):
    px, py, pc = place
    return (1 - px if k & 4 else px, 1 - py if k & 2 else py, 1 - pc if k & 1 else pc)


def _index(place):
    return 4 * place[0] + 2 * place[1] + place[2]


def _gather_small(v, name):
    rows, cols = v.shape

    def body(v_ref, out_ref, send_sems, recv_sems):
        me = _my_place()
        out_ref[_index(me)] = v_ref[...]
        copies = []
        for k in range(1, N_DEV):
            copies.append(pltpu.make_async_remote_copy(
                src_ref=v_ref, dst_ref=out_ref.at[_index(me)], send_sem=send_sems.at[k - 1], recv_sem=recv_sems.at[k - 1],
                device_id=_flip(me, k), device_id_type=MESH))
        for cp in copies:
            cp.start()
        for k in range(1, N_DEV):
            pltpu.make_async_remote_copy(
                src_ref=v_ref, dst_ref=out_ref.at[_index(_flip(me, k))], send_sem=send_sems.at[k - 1],
                recv_sem=recv_sems.at[k - 1], device_id=_flip(me, k), device_id_type=MESH).wait_recv()
        for cp in copies:
            cp.wait_send()

    return pl.pallas_call(
        body,
        name=name,
        out_shape=jax.ShapeDtypeStruct((N_DEV, rows, cols), v.dtype),
        in_specs=[pl.BlockSpec(memory_space=pltpu.VMEM)],
        out_specs=pl.BlockSpec(memory_space=pltpu.VMEM),
        scratch_shapes=[pltpu.SemaphoreType.DMA((N_DEV - 1,)), pltpu.SemaphoreType.DMA((N_DEV - 1,))],
        compiler_params=pltpu.CompilerParams(vmem_limit_bytes=VMEM_LIMIT_BYTES),
    )(v)


_HBM = pl.BlockSpec(memory_space=pltpu.HBM)
_SEM = pl.BlockSpec(memory_space=pltpu.SEMAPHORE)
_EFFECT = pltpu.SideEffectType.DATAFLOW_SIDE_EFFECTING


def _remote(src, dst, send_sems, recv_sems, j, to):
    return pltpu.make_async_remote_copy(src_ref=src, dst_ref=dst, send_sem=send_sems.at[j], recv_sem=recv_sems.at[j],
                                        device_id=to, device_id_type=MESH)


def _copies_start(name, bufs, make_copies, nsem):
    nbuf = len(bufs)

    def body(*refs):
        for cp in make_copies(refs[:nbuf], refs[nbuf], refs[nbuf + 1]):
            cp.start()
        refs[-1][...] = jnp.zeros_like(refs[-1])

    sems = pltpu.SemaphoreType.DMA((nsem,))
    res = pl.pallas_call(
        body, name=name,
        out_shape=(sems, sems, *[pltpu.HBM(v.shape, v.dtype) for v in bufs], jax.ShapeDtypeStruct((8, LANES), F32)),
        in_specs=(_HBM,) * nbuf, out_specs=(_SEM, _SEM) + (_HBM,) * nbuf + (pl.BlockSpec(memory_space=pltpu.VMEM),),
        input_output_aliases={i: 2 + i for i in range(nbuf)},
        compiler_params=pltpu.CompilerParams(has_side_effects=_EFFECT),
    )(*[pltpu.with_memory_space_constraint(v, pltpu.HBM) for v in bufs])
    return res[0], res[1], list(res[2:2 + nbuf]), res[-1]


def _copies_wait(name, started, make_copies, after):
    send_sems, recv_sems, bufs, _ = started
    nbuf = len(bufs)

    def body(*refs):
        for cp in make_copies(refs[:nbuf], refs[nbuf], refs[nbuf + 1]):
            cp.wait_send()
            cp.wait_recv()

    return list(pl.pallas_call(
        body, name=name,
        out_shape=tuple(pltpu.HBM(v.shape, v.dtype) for v in bufs),
        in_specs=(_HBM,) * nbuf + (_SEM, _SEM, pl.BlockSpec(memory_space=pl.ANY)), out_specs=(_HBM,) * nbuf,
        input_output_aliases={i: i for i in range(nbuf)},
        compiler_params=pltpu.CompilerParams(has_side_effects=_EFFECT),
    )(*bufs, send_sems, recv_sems, after))


def _to_sibling_copies(refs, send_sems, recv_sems):
    src_ref, land_ref = refs
    me = _my_place()
    return [_remote(src_ref.at[q, 1 - me[2]], land_ref.at[q], send_sems, recv_sems, q, _flip(me, 1)) for q in range(4)]


def _to_chips_copies(refs, send_sems, recv_sems):
    src_ref, land_ref = refs
    me = _my_place()
    copies = []
    for j, k in enumerate(CHIP_FLIPS):
        to = _flip(me, k)
        copies.append(_remote(src_ref.at[2 * to[0] + to[1]], land_ref.at[j], send_sems, recv_sems, j, to))
    return copies


class _Gather:
    def __init__(self, name, blocks):
        self.name, self.n = name, len(blocks)
        at_me = (_index(_my_place()), 0, 0)
        lands = [lax.dynamic_update_slice(lax.empty((N_DEV,) + v.shape, v.dtype), v[None], at_me) for v in blocks]
        self.first = _copies_start(name + "_start", list(blocks) + lands, self._first_copies, 4 * self.n)
        self.token = self.first[3]

    def _first_copies(self, refs, send_sems, recv_sems):
        me = _my_place()
        return [_remote(refs[w], refs[self.n + w].at[_index(me)], send_sems, recv_sems, 4 * w + j, _flip(me, k))
                for w in range(self.n) for j, k in enumerate((1,) + CHIP_FLIPS)]

    def _pass_copies(self, refs, send_sems, recv_sems):
        me = _my_place()
        copies = []
        for w, land in enumerate(refs):
            for j, k in enumerate(CHIP_FLIPS):
                slot = land.at[_index(_flip(me, k))]
                copies.append(_remote(slot, slot, send_sems, recv_sems, 3 * w + j, _flip(me, 1)))
        return copies

    def pass_on(self, after):
        lands = _copies_wait(self.name + "_wait", self.first, self._first_copies, after)[self.n:]
        self.second = _copies_start(self.name + "_pass_start", lands, self._pass_copies, 3 * self.n)
        return self.second[3]

    def finish(self, after):
        return _copies_wait(self.name + "_pass_wait", self.second, self._pass_copies, after)


SUM_SPLIT = 2


def _sum_pairs(parts, theirs):
    nchip, _, rows, cols = parts.shape
    tile = rows // SUM_SPLIT

    def body(c_ref, a_ref, b_ref, o_ref):
        o_ref[...] = (a_ref[0].astype(F32) + b_ref[...].astype(F32)).astype(BF16)

    spec = pl.BlockSpec((1, tile, cols), lambda q, t, c_ref: (q, t, 0))
    grid_spec = pltpu.PrefetchScalarGridSpec(
        num_scalar_prefetch=1, grid=(nchip, SUM_SPLIT),
        in_specs=[pl.BlockSpec((1, 1, tile, cols), lambda q, t, c_ref: (q, c_ref[0], t, 0)), spec], out_specs=spec)
    return pl.pallas_call(body, name="grad_sum_sibling", grid_spec=grid_spec,
                          out_shape=jax.ShapeDtypeStruct((nchip, rows, cols), BF16),
                          compiler_params=_params("parallel", "parallel"))(lax.axis_index("c").reshape(1), parts, theirs)


def _sum_final(chip_sum, got):
    _, rows, cols = chip_sum.shape
    tile = rows // SUM_SPLIT

    def body(q_ref, a_ref, g_ref, o_ref):
        o_ref[...] = ((a_ref[0].astype(F32) + g_ref[0].astype(F32)) + g_ref[1].astype(F32)) + g_ref[2].astype(F32)

    grid_spec = pltpu.PrefetchScalarGridSpec(
        num_scalar_prefetch=1, grid=(SUM_SPLIT,),
        in_specs=[pl.BlockSpec((1, tile, cols), lambda t, q_ref: (q_ref[0], t, 0)),
                  pl.BlockSpec((3, tile, cols), lambda t, q_ref: (0, t, 0))],
        out_specs=pl.BlockSpec((tile, cols), lambda t, q_ref: (t, 0)))
    my_chip = (2 * lax.axis_index("x") + lax.axis_index("y")).reshape(1)
    return pl.pallas_call(body, name="grad_sum_chips", grid_spec=grid_spec, out_shape=jax.ShapeDtypeStruct((rows, cols), F32),
                          compiler_params=_params("parallel"))(my_chip, chip_sum, got)


class _ReduceScatter:
    def __init__(self, name, slabs):
        self.name, self.rows = name, slabs.shape[1]
        parts = slabs.reshape(4, 2, self.rows, D_MODEL)
        self.first = _copies_start(name + "_sibling_start", [parts, lax.empty((4, self.rows, D_MODEL), slabs.dtype)],
                                   _to_sibling_copies, 4)
        self.token = self.first[3]

    def between_chips(self, after):
        parts, theirs = _copies_wait(self.name + "_sibling_wait", self.first, _to_sibling_copies, after)
        chip_sum = _sum_pairs(parts, theirs)
        self.second = _copies_start(self.name + "_chips_start", [chip_sum, lax.empty((3, self.rows, D_MODEL), chip_sum.dtype)],
                                    _to_chips_copies, 3)
        return self.second[3]

    def finish(self, after):
        chip_sum, got = _copies_wait(self.name + "_chips_wait", self.second, _to_chips_copies, after)
        return _sum_final(chip_sum, got)


def _ada_fwd(c_all, w, b):
    nb, _ = c_all.shape
    ncol = w.shape[1]

    def body(c_ref, w_ref, b_ref, o_ref):
        c = c_ref[...]
        act = (c * _sigmoid(c)).astype(BF16)
        o_ref[...] = jnp.dot(act, w_ref[...].astype(BF16), preferred_element_type=F32) + b_ref[...]

    return pl.pallas_call(body, name="ada_fwd", out_shape=jax.ShapeDtypeStruct((nb, ncol), F32),
                          compiler_params=pltpu.CompilerParams(vmem_limit_bytes=VMEM_LIMIT_BYTES))(c_all, w, b)


def _ada_wgrad(c_all_t, dmod_cols):
    d, nb = c_all_t.shape
    ncol = dmod_cols.shape[1]

    def body(ct_ref, dm_ref, o_ref):
        ct = ct_ref[...]
        act = (ct * _sigmoid(ct)).astype(BF16).astype(F32)
        dm = dm_ref[...].astype(BF16).astype(F32)
        acc = act[:, 0:1] * dm[0:1, :]
        for i in range(1, nb):
            acc = acc + act[:, i:i + 1] * dm[i:i + 1, :]
        o_ref[...] = acc

    return pl.pallas_call(body, name="ada_wgrad", out_shape=jax.ShapeDtypeStruct((d, ncol), F32),
                          compiler_params=pltpu.CompilerParams(vmem_limit_bytes=VMEM_LIMIT_BYTES))(c_all_t, dmod_cols)


SMALL_ROWS = 24


def _reduce_small(gathered):
    def body(g_ref, o_ref):
        acc = g_ref[0]
        for dev in range(1, N_DEV):
            acc = acc + g_ref[dev]
        o_ref[...] = acc

    return pl.pallas_call(body, name="reduce_small", out_shape=jax.ShapeDtypeStruct(gathered.shape[1:], F32))(gathered)


def _adamw_math(w, g, m, v):
    nm = ADAM_B1 * m + (1.0 - ADAM_B1) * g
    nv = ADAM_B2 * v + (1.0 - ADAM_B2) * (g * g)
    bc1 = 1.0 - ADAM_B1 ** ADAM_STEP
    bc2 = 1.0 - ADAM_B2 ** ADAM_STEP
    return -ADAM_LR * ((nm / bc1) / (jnp.sqrt(nv / bc2) + ADAM_EPS) + ADAM_WD * w), nm, nv


def _adamw_small(ws, gs, ms, vs, name):
    n = len(ws)

    def body(*refs):
        for i in range(n):
            res = _adamw_math(*(refs[k * n + i][...] for k in range(4)))
            for k in range(3):
                refs[(4 + k) * n + i][...] = res[k]

    shapes = [jax.ShapeDtypeStruct(w.shape, F32) for w in ws]
    res = pl.pallas_call(body, name=name, out_shape=shapes * 3)(*ws, *gs, *ms, *vs)
    return [(res[i], res[n + i], res[2 * n + i]) for i in range(n)]


def _adamw(w, g, m, v, name):
    rows, cols = w.shape
    tile = rows
    for cand in range(min(rows // 2, 512) // 8 * 8, 7, -8):
        if rows % cand == 0:
            tile = cand
            break
    spec = pl.BlockSpec((tile, cols), lambda t: (t, 0))

    def body(w_ref, g_ref, m_ref, v_ref, d_ref, nm_ref, nv_ref):
        d_ref[...], nm_ref[...], nv_ref[...] = _adamw_math(w_ref[...], g_ref[...], m_ref[...], v_ref[...])

    shp = jax.ShapeDtypeStruct((rows, cols), F32)
    return pl.pallas_call(body, name=name, grid=(rows // tile,), in_specs=[spec] * 4, out_specs=[spec] * 3, out_shape=[shp] * 3,
                          compiler_params=_params("parallel"))(w, g, m, v)


_WEIGHTS = ("w_ada", "b_ada", "w_in", "sinks", "w_branch_a", "w_branch_b", "w_o", "ln1_g", "ln1_b", "w_gate_up", "w_down",
            "ln2_g", "ln2_b")
_TRANSPOSED = ("w_in", "w_branch_b", "w_gate_up")


def _pack_shard(name, w):
    w = w.astype(BF16)
    if name in _TRANSPOSED:
        w = w.T
    return w.reshape(-1, D_MODEL)


def _unpack_full(name, slab):
    if name == "w_branch_b":
        return slab.reshape(N_DEV * 128, 512)
    return slab.reshape(-1, D_MODEL)


def _unpack_group(group, gathered):
    return {n: _unpack_full(n, slab) for (n, _), slab in zip(group, gathered)}


def _unpack_grads(group, g_packed):
    g_w, off = {}, 0
    for n, r in group:
        part = g_packed[off:off + r]
        off += r
        g_w[n] = part.reshape(128, 512) if n == "w_branch_b" else part
    return g_w


def kernel(x, c, positions, w_ada, b_ada, w_in, sinks, w_branch_a, w_branch_b, w_o, ln1_g, ln1_b, w_gate_up, w_down, ln2_g, ln2_b, loss_target, m_w_ada, m_b_ada, m_w_in, m_sinks, m_w_branch_a, m_w_branch_b, m_w_o, m_ln1_g, m_ln1_b, m_w_gate_up, m_w_down, m_ln2_g, m_ln2_b, v_w_ada, v_b_ada, v_w_in, v_sinks, v_w_branch_a, v_w_branch_b, v_w_o, v_ln1_g, v_ln1_b, v_w_gate_up, v_w_down, v_ln2_g, v_ln2_b):
    weights = dict(w_ada=w_ada, b_ada=b_ada, w_in=w_in, sinks=sinks, w_branch_a=w_branch_a, w_branch_b=w_branch_b, w_o=w_o,
                   ln1_g=ln1_g, ln1_b=ln1_b, w_gate_up=w_gate_up, w_down=w_down, ln2_g=ln2_g, ln2_b=ln2_b)
    m_in = dict(w_ada=m_w_ada, b_ada=m_b_ada, w_in=m_w_in, sinks=m_sinks, w_branch_a=m_w_branch_a, w_branch_b=m_w_branch_b,
                w_o=m_w_o, ln1_g=m_ln1_g, ln1_b=m_ln1_b, w_gate_up=m_w_gate_up, w_down=m_w_down, ln2_g=m_ln2_g, ln2_b=m_ln2_b)
    v_in = dict(w_ada=v_w_ada, b_ada=v_b_ada, w_in=v_w_in, sinks=v_sinks, w_branch_a=v_w_branch_a, w_branch_b=v_w_branch_b,
                w_o=v_w_o, ln1_g=v_ln1_g, ln1_b=v_ln1_b, w_gate_up=v_w_gate_up, w_down=v_w_down, ln2_g=v_ln2_g, ln2_b=v_ln2_b)
    bsz = x.shape[0]
    me = _index(_my_place())
    ada_cols = w_ada.shape[2]
    outs = {}

    def adamw(n, g):
        w2, m2, v2 = (t[n][0] if t[n].ndim == 3 else t[n] for t in (weights, m_in, v_in))
        shape = weights[n].shape
        if n in _TRANSPOSED:
            dlt, nm, nv = _adamw(w2.T, g, m2.T, v2.T, "adamw_" + n)
            outs[n] = tuple(t.T.reshape(shape) for t in (g, dlt, nm, nv))
        else:
            dlt, nm, nv = _adamw(w2, g, m2, v2, "adamw_" + n)
            outs[n] = tuple(t.reshape(shape) for t in (g, dlt, nm, nv))
        return nv

    packed_in = [_pack_shard(n, weights[n][0]) for n, _ in GROUP_IN]
    packed_rest = [_pack_shard(n, weights[n][0]) for n, _ in GROUP_REST]
    c_all = _gather_small(jnp.pad(c, ((0, 8 - bsz), (0, 0))), "gather_c")[:, :bsz].reshape(N_DEV * bsz, D_MODEL)
    gather_in = _Gather("gather_w_in", lax.optimization_barrier((packed_in, c_all))[0])
    b_cols = lax.dynamic_slice_in_dim(b_ada, me * ada_cols, ada_cols, axis=1)
    mod_cols = _ada_fwd(c_all, w_ada[0], b_cols + gather_in.token[0, 0])
    tables = _rope_tables(positions)
    mod_cols, tables, packed_rest = lax.optimization_barrier((mod_cols, tables, packed_rest))
    mod_all = _gather_small(mod_cols, "gather_mod").transpose(1, 0, 2).reshape(N_DEV * bsz, 6, D_MODEL)
    gather_rest = _Gather("gather_rest", lax.optimization_barrier((packed_rest, mod_all))[0])
    mod = jnp.pad(lax.dynamic_slice_in_dim(mod_all, me * bsz, bsz, axis=0), ((0, 0), (0, 2), (0, 0)))
    mod = mod + gather_rest.token[0, 0]
    mod = mod + gather_in.pass_on(mod)[0, 0]

    scatters = {}

    def get_w_in(after):
        return _unpack_group(GROUP_IN, gather_in.finish(after))["w_in"]

    def get_rest(after):
        return _unpack_group(GROUP_REST, gather_rest.finish(after))

    def pack_grads(group, grads):
        return jnp.concatenate([grads[n].reshape(N_DEV, r, D_MODEL) for n, r in group], axis=1)

    def hook(point, value):
        if point == "projected":
            return gather_rest.pass_on(value)
        if point == "grads_rest":
            scatters["rest"] = _ReduceScatter("scatter_rest", pack_grads(GROUP_REST, value))
            return scatters["rest"].token
        if point == "delta_done":
            return scatters["rest"].between_chips(value)
        if point == "grads_w_in":
            scatters["in"] = _ReduceScatter("scatter_w_in", pack_grads(GROUP_IN, value))
            return scatters["in"].token
        if point == "dgrad_done":
            return None
        raise ValueError(point)

    loss_part, grad_x, dmod, small = _layer_step(x, mod, tables, sinks[0], ln1_g, ln1_b, ln2_g, ln2_b, loss_target,
                                                 get_w_in, get_rest, hook)

    rows = jnp.concatenate([dmod.reshape(bsz * 6, D_MODEL), small, jnp.full((1, D_MODEL), loss_part, F32),
                            jnp.zeros((SMALL_ROWS - bsz * 6 - 6, D_MODEL), F32)], axis=0)
    small_all = _gather_small(rows, "gather_small")
    small_all = small_all + scatters["in"].between_chips(small_all)[0, 0]
    sums = _reduce_small(small_all)
    loss = sums[bsz * 6 + 5, 0]
    dmod_all = small_all[:, :bsz * 6].reshape(N_DEV * bsz, 6 * D_MODEL)
    small_g = {"b_ada": functools.reduce(jnp.add, [sums[6 * i:6 * i + 6] for i in range(bsz)]).reshape(1, 6 * D_MODEL),
               "sinks": sums[bsz * 6 + 4][::HEAD_DIM][None]}
    small_g.update({n: sums[bsz * 6 + i][None] for i, n in enumerate(("ln1_g", "ln1_b", "ln2_g", "ln2_b"))})
    names = list(small_g)
    for n, (dlt, nm, nv) in zip(names, _adamw_small([weights[n] for n in names], [small_g[n] for n in names],
                                                     [m_in[n] for n in names], [v_in[n] for n in names], "adamw_small")):
        outs[n] = (small_g[n], dlt, nm, nv)
    dmod_cols = lax.dynamic_slice_in_dim(dmod_all, me * ada_cols, ada_cols, axis=1)
    last = adamw("w_ada", _ada_wgrad(c_all.T, dmod_cols))
    for n, g in _unpack_grads(GROUP_REST, scatters["rest"].finish(last)).items():
        adamw(n, g)
    done = lax.optimization_barrier(tuple(outs[n][3] for n in outs))
    for n, g in _unpack_grads(GROUP_IN, scatters["in"].finish(done[0])).items():
        adamw(n, g)

    return (loss, grad_x, *[outs[n][0] for n in _WEIGHTS], *[outs[n][1] for n in _WEIGHTS], *[outs[n][2] for n in _WEIGHTS],
            *[outs[n][3] for n in _WEIGHTS])
```

```python
import functools

import jax
import jax.numpy as jnp
from jax import lax
from jax.experimental import pallas as pl
from jax.experimental.pallas import tpu as pltpu

F32 = jnp.float32
BF16 = jnp.bfloat16

D_MODEL = 1024
HEAD_DIM = 64
A_Q_HEADS = 16
A_WINDOW = 128
B_PATTERNS = ((128, 1), (512, 4), (2048, 16))
B_HEADS_PER_GROUP = 8
D_FF = 2816
QBLOCK = 128
ROPE_THETA = 10000.0
LN_EPS = 1e-5
DEEPNORM_ALPHA = 2.0 ** 0.25
NEG_INF = -1e30
ADAM_LR, ADAM_B1, ADAM_B2, ADAM_EPS, ADAM_WD, ADAM_STEP = 0.001, 0.9, 0.999, 1e-08, 0.01, 10

N_DEV = 8
LANES = 128
VMEM_LIMIT_BYTES = 56 * 1024 * 1024
MESH = pl.DeviceIdType.MESH

OFF_QA, OFF_KVA, OFF_QKVB, OFF_GAB = 0, 1024, 1280, 5888
GROUP_IN = (("w_in", 992),)
GROUP_REST = (("w_branch_a", 128), ("w_branch_b", 64), ("w_o", 128), ("w_gate_up", 704), ("w_down", 352))


def _params(*sem):
    return pltpu.CompilerParams(dimension_semantics=sem, vmem_limit_bytes=VMEM_LIMIT_BYTES)


def _sigmoid(x):
    return 1.0 / (1.0 + jnp.exp(-x))


_DIMS = {"nn": (((1,), (0,)), ((), ())), "nt": (((1,), (1,)), ((), ())), "tn": (((0,), (0,)), ((), ()))}


def _matmul(a, b, *, mode, tm, tn, tk, name, out_dtype=None, n=None, b_off=0, token=None, ins=(), outs=None, epilogue=None,
            lhs_fn=None):
    if mode == "nn":
        (m, k), nn_ = a.shape, b.shape[1]
    elif mode == "nt":
        (m, k), nn_ = a.shape, (b.shape[0] if n is None else n)
    else:
        (k, m), nn_ = a.shape, b.shape[1]
    assert m % tm == 0 and nn_ % tn == 0 and k % tk == 0 and b_off % tn == 0, (name, m, nn_, k)
    nk = k // tk
    joff = b_off // tn
    if mode == "nn":
        a_spec = pl.BlockSpec((tm, tk), lambda i, j, kk: (i, kk))
        b_spec = pl.BlockSpec((tk, tn), lambda i, j, kk: (kk, j))
    elif mode == "nt":
        a_spec = pl.BlockSpec((tm, tk), lambda i, j, kk: (i, kk))
        b_spec = pl.BlockSpec((tn, tk), lambda i, j, kk: (j + joff, kk))
    else:
        a_spec = pl.BlockSpec((tk, tm), lambda i, j, kk: (kk, i))
        b_spec = pl.BlockSpec((tk, tn), lambda i, j, kk: (kk, j))
    dims = _DIMS[mode]
    has_token = token is not None
    plain = epilogue is None
    if plain:
        outs = [(jax.ShapeDtypeStruct((m, nn_), out_dtype), (tm, tn), lambda i, j: (i, j))]

        def epilogue(acc, i, j, in_refs, out_refs):
            out_refs[0][...] = acc.astype(out_refs[0].dtype)

    nin = len(ins)
    nscratch = 1 if lhs_fn is None else 2
    assert lhs_fn is None or nk == 1

    def body(*refs):
        a_ref, b_ref = refs[:2]
        in_refs = refs[2:2 + nin]
        out_refs = refs[2 + nin + has_token:-nscratch]
        acc_ref = refs[-nscratch]
        kk = pl.program_id(2)
        if lhs_fn is None:
            lhs = a_ref[...].astype(BF16)
        else:
            lhs_ref = refs[-1]

            @pl.when(pl.program_id(1) == 0)
            def _():
                lhs_ref[...] = lhs_fn(a_ref, in_refs, out_refs)

            lhs = lhs_ref[...]
        part = lax.dot_general(lhs, b_ref[...].astype(BF16), dims, preferred_element_type=F32)

        def finish(acc):
            epilogue(acc, pl.program_id(0), pl.program_id(1), in_refs, out_refs)

        if nk == 1:
            finish(part)
        else:
            @pl.when(kk == 0)
            def _():
                acc_ref[...] = part

            @pl.when(kk > 0)
            def _():
                acc_ref[...] += part

            @pl.when(kk == nk - 1)
            def _():
                finish(acc_ref[...])

    def spec(block, index):
        return pl.BlockSpec(block, lambda i, j, kk: index(i, j))

    in_specs, args = [a_spec, b_spec], [a, b]
    for arr, block, index in ins:
        in_specs.append(spec(block, index))
        args.append(arr)
    if has_token:
        in_specs.append(pl.BlockSpec(token.shape, lambda i, j, kk: (0, 0)))
        args.append(token)
    res = pl.pallas_call(
        body,
        name=name,
        grid=(m // tm, nn_ // tn, nk),
        in_specs=in_specs,
        out_specs=[spec(block, index) for _, block, index in outs],
        out_shape=[shape for shape, _, _ in outs],
        scratch_shapes=[pltpu.VMEM((tm, tn) if nk > 1 else (8, LANES), F32)] + ([] if lhs_fn is None else [pltpu.VMEM((tm, tk), BF16)]),
        compiler_params=_params("arbitrary", "arbitrary", "arbitrary"),
    )(*args)
    return res[0] if plain else res


def _proj_rope(a, bt, cos, sin, *, n, b_off, rope_cols, tm, tn, name, out_dtype=F32):
    m, k = a.shape
    assert m % tm == 0 and n % tn == 0 and b_off % tn == 0 and rope_cols % LANES == 0, name
    joff = b_off // tn
    nrope, part = divmod(rope_cols, tn)

    def body(a_ref, b_ref, c_ref, s_ref, o_ref):
        acc = lax.dot_general(a_ref[...], b_ref[...], _DIMS["nt"], preferred_element_type=F32)
        j = pl.program_id(1)

        @pl.when(j < nrope)
        def _():
            o_ref[...] = _rope(acc, c_ref[...], s_ref[...], coarse=True).astype(o_ref.dtype)

        if part:
            @pl.when(j == nrope)
            def _():
                o_ref[:, :part] = _rope(acc[:, :part], c_ref[...], s_ref[...], coarse=True).astype(o_ref.dtype)
                o_ref[:, part:] = acc[:, part:].astype(o_ref.dtype)

        @pl.when(j >= nrope + (1 if part else 0))
        def _():
            o_ref[...] = acc.astype(o_ref.dtype)

    table = pl.BlockSpec((tm, LANES), lambda i, j: (i, 0))
    return pl.pallas_call(
        body,
        name=name,
        grid=(m // tm, n // tn),
        in_specs=[pl.BlockSpec((tm, k), lambda i, j: (i, 0)), pl.BlockSpec((tn, k), lambda i, j: (j + joff, 0)), table, table],
        out_specs=pl.BlockSpec((tm, tn), lambda i, j: (i, j)),
        out_shape=jax.ShapeDtypeStruct((m, n), out_dtype),
        compiler_params=_params("parallel", "parallel"),
    )(a, bt, cos, sin)


ROW_TILE = 256


def _rows(width, col=0):
    return pl.BlockSpec((1, ROW_TILE, width), lambda b, t: (b, t, col))


def _per_batch(nrows, width):
    return pl.BlockSpec((1, nrows, width), lambda b, t: (b, 0, 0))


def _row_call(body, name, bsz, seq, in_specs, out_specs, out_shape, accumulates=False):
    return pl.pallas_call(
        body,
        name=name,
        grid=(bsz, seq // ROW_TILE),
        in_specs=in_specs,
        out_specs=out_specs,
        out_shape=out_shape,
        compiler_params=_params("parallel", "arbitrary" if accumulates else "parallel"),
    )


def _acc_rows(acc_ref, first, rows):
    @pl.when(first)
    def _():
        acc_ref[...] = jnp.zeros_like(acc_ref)

    for r, val in enumerate(rows):
        acc_ref[0, r:r + 1, :] += val


def _colsum(v):
    return jnp.sum(v, axis=0, keepdims=True)


def _ln_stats(z):
    mu = jnp.mean(z, axis=-1, keepdims=True)
    zc = z - mu
    var = jnp.mean(zc * zc, axis=-1, keepdims=True)
    rstd = lax.rsqrt(var + LN_EPS)
    return zc * rstd, rstd


def _ln_bwd(dxhat, xhat, rstd):
    m1 = jnp.mean(dxhat, axis=-1, keepdims=True)
    m2 = jnp.mean(dxhat * xhat, axis=-1, keepdims=True)
    return rstd * (dxhat - m1 - xhat * m2)


def _modulate_in(x, mod):
    bsz, seq, d = x.shape

    def body(x_ref, mod_ref, u_ref):
        u_ref[0] = (x_ref[0] * (1.0 + mod_ref[0, 1:2, :]) + mod_ref[0, 0:1, :]).astype(BF16)

    return _row_call(body, "modulate_in", bsz, seq, [_rows(d), _per_batch(8, d)], _rows(d),
                     jax.ShapeDtypeStruct((bsz, seq, d), BF16))(x, mod)


EP_TILE = 512


def _ep_specs(seq, d):
    tiles = seq // EP_TILE
    return ((EP_TILE, d), lambda i, j: (i, 0)), ((1, 8, d), lambda i, j: (i // tiles, 0, 0)), ((1, d), lambda i, j: (0, 0))


def _wo_ln1(merged, wo, x, mod, g, b, seq):
    ntok, d = x.shape
    row, per_b, whole = _ep_specs(seq, d)

    def epilogue(y, i, j, ins, outs):
        x_ref, mod_ref, g_ref, b_ref = ins
        y_ref, h_ref, u_ref = outs
        z = DEEPNORM_ALPHA * x_ref[...] + (1.0 + mod_ref[0, 2:3, :]) * y
        xhat, _ = _ln_stats(z)
        h = xhat * g_ref[...] + b_ref[...]
        y_ref[...] = y
        h_ref[...] = h
        u_ref[...] = (h * (1.0 + mod_ref[0, 4:5, :]) + mod_ref[0, 3:4, :]).astype(BF16)

    f32, bf16 = jax.ShapeDtypeStruct((ntok, d), F32), jax.ShapeDtypeStruct((ntok, d), BF16)
    return _matmul(merged, wo, mode="nn", tm=EP_TILE, tn=d, tk=d, name="w_o_ln1",
                   ins=[(x,) + row, (mod,) + per_b, (g,) + whole, (b,) + whole],
                   outs=[(f32,) + row, (f32,) + row, (bf16,) + row], epilogue=epilogue)


FF_HALF = D_FF // 2


def _interleave_gate_up(w):
    return w.reshape(2, 2, FF_HALF, w.shape[1]).transpose(1, 0, 2, 3).reshape(w.shape)


def _gate_up_silu(u2, wgut_i):
    ntok = u2.shape[0]

    def epilogue(h, i, j, ins, outs):
        h_ref, a_ref = outs
        hg, hu = h[:, :FF_HALF], h[:, FF_HALF:]
        h_ref[...] = h.astype(BF16)
        a_ref[...] = (hg * _sigmoid(hg) * hu).astype(BF16)

    return _matmul(u2, wgut_i, mode="nt", tm=EP_TILE, tn=2 * FF_HALF, tk=u2.shape[1], name="gate_up_silu",
                   outs=[(jax.ShapeDtypeStruct((ntok, 2 * D_FF), BF16), (EP_TILE, 2 * FF_HALF), lambda i, j: (i, j)),
                         (jax.ShapeDtypeStruct((ntok, D_FF), BF16), (EP_TILE, FF_HALF), lambda i, j: (i, j))],
                   epilogue=epilogue)


def _down_dgrad_silu_bwd(dy2, wd, h_i):
    ntok = dy2.shape[0]
    wide = ((EP_TILE, 2 * FF_HALF), lambda i, j: (i, j))

    def epilogue(da, i, j, ins, outs):
        h = ins[0][...].astype(F32)
        hg, hu = h[:, :FF_HALF], h[:, FF_HALF:]
        sg = _sigmoid(hg)
        outs[0][:, :FF_HALF] = (da * hu * (sg * (1.0 + hg * (1.0 - sg)))).astype(BF16)
        outs[0][:, FF_HALF:] = (da * (hg * sg)).astype(BF16)

    return _matmul(dy2, wd, mode="nt", tm=EP_TILE, tn=FF_HALF, tk=dy2.shape[1], name="down_dgrad_silu_bwd",
                   ins=[(h_i,) + wide], outs=[(jax.ShapeDtypeStruct((ntok, 2 * D_FF), BF16),) + wide], epilogue=epilogue)[0]


def _down_ln2_loss_bwd(a, wd, h1, mod, g, b, target, seq):
    ntok, d = h1.shape
    row, per_b, whole = _ep_specs(seq, d)
    tiles = seq // EP_TILE

    def epilogue(y, i, j, ins, outs):
        h_ref, mod_ref, g_ref, b_ref, t_ref = ins
        dy_ref, dh_ref, acc_ref = outs
        gate = 1.0 + mod_ref[0, 5:6, :]
        z = DEEPNORM_ALPHA * h_ref[...] + gate * y
        xhat, rstd = _ln_stats(z)
        diff = xhat * g_ref[...] + b_ref[...] - t_ref[...]
        loss = 0.5 * jnp.sum(jnp.sum(diff * diff, axis=-1, keepdims=True) / d, axis=0, keepdims=True)
        dout = diff / d
        dz = _ln_bwd(dout * g_ref[...], xhat, rstd)
        dy_ref[...] = (gate * dz).astype(BF16)
        dh_ref[...] = DEEPNORM_ALPHA * dz
        _acc_rows(acc_ref, i % tiles == 0,
                  [_colsum(dout * xhat), _colsum(dout), _colsum(dz * y), jnp.broadcast_to(loss, (1, d))])

    return _matmul(a, wd, mode="nn", tm=EP_TILE, tn=d, tk=a.shape[1], name="down_ln2_loss_bwd",
                   ins=[(h1,) + row, (mod,) + per_b, (g,) + whole, (b,) + whole, (target,) + row],
                   outs=[(jax.ShapeDtypeStruct((ntok, d), BF16),) + row, (jax.ShapeDtypeStruct((ntok, d), F32),) + row,
                         (jax.ShapeDtypeStruct((ntok // seq, 8, d), F32),) + per_b], epilogue=epilogue)


def _gate_up_dgrad_ln1_bwd(dh, wgut, dh1a, x, y1, mod, g, b, seq):
    ntok, d = x.shape
    row, per_b, whole = _ep_specs(seq, d)
    tiles = seq // EP_TILE

    def epilogue(du, i, j, ins, outs):
        dh_ref, x_ref, y_ref, mod_ref, g_ref, b_ref = ins
        dy_ref, dx_ref, acc_ref = outs
        y = y_ref[...]
        gate = 1.0 + mod_ref[0, 2:3, :]
        z = DEEPNORM_ALPHA * x_ref[...] + gate * y
        xhat, rstd = _ln_stats(z)
        h1 = xhat * g_ref[...] + b_ref[...]
        dh1 = dh_ref[...] + du * (1.0 + mod_ref[0, 4:5, :])
        dz = _ln_bwd(dh1 * g_ref[...], xhat, rstd)
        dy_ref[...] = (gate * dz).astype(BF16)
        dx_ref[...] = DEEPNORM_ALPHA * dz
        _acc_rows(acc_ref, i % tiles == 0,
                  [_colsum(dh1 * xhat), _colsum(dh1), _colsum(dz * y), _colsum(du * h1), _colsum(du)])

    return _matmul(dh, wgut, mode="nn", tm=EP_TILE, tn=d, tk=D_FF, name="gate_up_dgrad_ln1_bwd",
                   ins=[(dh1a,) + row, (x,) + row, (y1,) + row, (mod,) + per_b, (g,) + whole, (b,) + whole],
                   outs=[(jax.ShapeDtypeStruct((ntok, d), BF16),) + row, (jax.ShapeDtypeStruct((ntok, d), F32),) + row,
                         (jax.ShapeDtypeStruct((ntok // seq, 8, d), F32),) + per_b], epilogue=epilogue)


def _wo_dgrad_gate_bwd(dy1, wo, gab, ya, yb):
    ntok, d = ya.shape
    tm, tn = 1024, 512
    tile = ((tm, tn), lambda i, j: (i, j))
    tile_b = ((tm, tn), lambda i, j: (i, j + d // tn))

    def epilogue(dm_, i, j, ins, outs):
        ga_ref, gb_ref, ya_ref, yb_ref = ins
        dya_ref, dyb_ref, dga_ref, dgb_ref = outs
        sa, sb = _sigmoid(ga_ref[...].astype(F32)), _sigmoid(gb_ref[...].astype(F32))
        dya_ref[...] = (dm_ * sa).astype(BF16)
        dyb_ref[...] = (dm_ * sb).astype(BF16)
        dga_ref[...] = (dm_ * ya_ref[...].astype(F32) * sa * (1.0 - sa)).astype(BF16)
        dgb_ref[...] = (dm_ * yb_ref[...].astype(F32) * sb * (1.0 - sb)).astype(BF16)

    shp = jax.ShapeDtypeStruct((ntok, d), BF16)
    return _matmul(dy1, wo, mode="nt", tm=tm, tn=tn, tk=d, name="w_o_dgrad_gate_bwd",
                   ins=[(gab,) + tile, (gab,) + tile_b, (ya,) + tile, (yb,) + tile],
                   outs=[(shp,) + tile] * 4, epilogue=epilogue)


def _w_in_dgrad_grad_x(dproj, wint, dxa, x, mod, seq, token):
    ntok, d = x.shape
    row, per_b, _ = _ep_specs(seq, d)
    tiles = seq // EP_TILE

    def epilogue(du, i, j, ins, outs):
        dxa_ref, x_ref, mod_ref = ins
        gx_ref, acc_ref = outs
        gx_ref[...] = dxa_ref[...] + du * (1.0 + mod_ref[0, 1:2, :])
        _acc_rows(acc_ref, i % tiles == 0, [_colsum(du * x_ref[...]), _colsum(du)])

    return _matmul(dproj, wint, mode="nn", tm=EP_TILE, tn=d, tk=wint.shape[0] // 2, name="w_in_dgrad_grad_x", token=token,
                   ins=[(dxa,) + row, (x,) + row, (mod,) + per_b],
                   outs=[(jax.ShapeDtypeStruct((ntok, d), F32),) + row, (jax.ShapeDtypeStruct((ntok // seq, 8, d), F32),) + per_b],
                   epilogue=epilogue)


def _merge_branch_b_gate(os_, ls_, wbbt, gab, ya):
    ntok, d = ya.shape
    w = os_[0].shape[1]
    tm, tn = 1024, 512
    tile = ((tm, tn), lambda i, j: (i, j))
    tile_b = ((tm, tn), lambda i, j: (i, j + d // tn))
    row = ((tm, w), lambda i, j: (i, 0))

    def lhs_fn(o0_ref, ins, outs):
        os_r, ls_r = (o0_ref,) + tuple(ins[3:5]), ins[5:8]
        ls = [l[...] for l in ls_r]
        mx = jnp.maximum(jnp.maximum(ls[0], ls[1]), ls[2])
        es = [jnp.exp(l - mx) for l in ls]
        den = es[0] + es[1] + es[2]
        ob = functools.reduce(jnp.add, [(e / den) * o[...].astype(F32) for e, o in zip(es, os_r)]).astype(BF16)
        outs[2][...] = ob
        return ob

    def epilogue(yb, i, j, ins, outs):
        ga_ref, gb_ref, ya_ref = ins[:3]
        yb_ref, merged_ref = outs[:2]
        yb_ref[...] = yb.astype(BF16)
        merged_ref[...] = (_sigmoid(ga_ref[...].astype(F32)) * ya_ref[...].astype(F32)
                           + _sigmoid(gb_ref[...].astype(F32)) * yb).astype(BF16)

    shp = jax.ShapeDtypeStruct((ntok, d), BF16)
    return _matmul(os_[0], wbbt, mode="nt", tm=tm, tn=tn, tk=w, name="merge_branch_b_gate", lhs_fn=lhs_fn,
                   ins=[(gab,) + tile, (gab,) + tile_b, (ya,) + tile] + [(v,) + row for v in list(os_[1:]) + list(ls_)],
                   outs=[(shp,) + tile] * 2 + [(jax.ShapeDtypeStruct((ntok, w), BF16),) + row], epilogue=epilogue)


def _segsum64(v):
    rows, width = v.shape
    ri = lax.broadcasted_iota(jnp.int32, (LANES, LANES), 0) // HEAD_DIM
    ci = lax.broadcasted_iota(jnp.int32, (LANES, LANES), 1) // HEAD_DIM
    ones = jnp.where(ri == ci, 1.0, 0.0).astype(BF16)
    out = []
    for c in range(width // LANES):
        part = v[:, c * LANES:(c + 1) * LANES]
        hi = part.astype(BF16)
        lo = (part - hi.astype(F32)).astype(BF16)
        out.append(jnp.dot(hi, ones, preferred_element_type=F32) + jnp.dot(lo, ones, preferred_element_type=F32))
    return jnp.concatenate(out, axis=1) if len(out) > 1 else out[0]


def _branch_b_dgrad_merge_bwd(dyb, wbbt, os_, ls_):
    ntok, w = os_[0].shape
    row = ((EP_TILE, w), lambda i, j: (i, 0))

    def epilogue(dob_, i, j, ins, outs):
        os_r, ls_r = ins[:3], ins[3:]
        do_r, dd_r = outs[:3], outs[3:]
        ls = [l[...] for l in ls_r]
        mx = jnp.maximum(jnp.maximum(ls[0], ls[1]), ls[2])
        es = [jnp.exp(l - mx) for l in ls]
        den = es[0] + es[1] + es[2]
        ws = [e / den for e in es]
        dws = [_segsum64(dob_ * o[...].astype(F32)) for o in os_r]
        mean = ws[0] * dws[0] + ws[1] * dws[1] + ws[2] * dws[2]
        for wg, do_ref, dd_ref in zip(ws, do_r, dd_r):
            do_ref[...] = wg * dob_
            dd_ref[...] = -wg * mean

    shp = jax.ShapeDtypeStruct((ntok, w), F32)
    return _matmul(dyb, wbbt, mode="nn", tm=EP_TILE, tn=w, tk=dyb.shape[1], name="branch_b_dgrad_merge_bwd",
                   ins=[(v,) + row for v in list(os_) + list(ls_)], outs=[(shp,) + row] * 6, epilogue=epilogue)


def _branch_a_dgrad_delta(dya, wba, oa, lse_a, sinks_exp, seq):
    ntok, w = oa.shape
    row, per_b, whole = _ep_specs(seq, w)
    tiles = seq // EP_TILE

    def epilogue(do_, i, j, ins, outs):
        o_ref, l_ref, s_ref = ins
        do_ref, dd_ref, acc_ref = outs
        dd = -_segsum64(do_ * o_ref[...].astype(F32))
        do_ref[...] = do_.astype(BF16)
        dd_ref[...] = dd
        _acc_rows(acc_ref, i % tiles == 0, [_colsum(dd * jnp.exp(s_ref[...] - l_ref[...]))])

    shp = jax.ShapeDtypeStruct((ntok, w), F32)
    return _matmul(dya, wba, mode="nt", tm=EP_TILE, tn=w, tk=dya.shape[1], name="branch_a_dgrad_delta",
                   ins=[(oa,) + row, (lse_a,) + row, (sinks_exp,) + whole],
                   outs=[(jax.ShapeDtypeStruct((ntok, w), BF16),) + row, (shp,) + row,
                         (jax.ShapeDtypeStruct((ntok // seq, 8, w), F32),) + per_b],
                   epilogue=epilogue)


def _swap_halves(v):
    src = lax.broadcasted_iota(jnp.int32, (LANES, LANES), 0)
    dst = lax.broadcasted_iota(jnp.int32, (LANES, LANES), 1)
    partner = jnp.where((dst % HEAD_DIM) < HEAD_DIM // 2, dst + HEAD_DIM // 2, dst - HEAD_DIM // 2)
    perm = jnp.where(src == partner, 1.0, 0.0).astype(BF16)
    hi = v.astype(BF16)
    lo = (v - hi.astype(F32)).astype(BF16)
    return jnp.dot(hi, perm, preferred_element_type=F32) + jnp.dot(lo, perm, preferred_element_type=F32)


def _swap_halves_roll(v):
    lane = lax.broadcasted_iota(jnp.int32, v.shape, 1)
    return jnp.where((lane % HEAD_DIM) < HEAD_DIM // 2, pltpu.roll(v, LANES - HEAD_DIM // 2, 1),
                     pltpu.roll(v, HEAD_DIM // 2, 1))


def _swap_halves_coarse(v):
    src = lax.broadcasted_iota(jnp.int32, (LANES, LANES), 0)
    dst = lax.broadcasted_iota(jnp.int32, (LANES, LANES), 1)
    partner = jnp.where((dst % HEAD_DIM) < HEAD_DIM // 2, dst + HEAD_DIM // 2, dst - HEAD_DIM // 2)
    perm = jnp.where(src == partner, 1.0, 0.0).astype(BF16)
    return jnp.dot(v.astype(BF16), perm, preferred_element_type=F32)


def _rope(v, cos, sin, sign=1.0, mxu=True, coarse=False):
    swap = (_swap_halves_coarse if coarse else _swap_halves) if mxu else _swap_halves_roll
    out = []
    for c in range(v.shape[1] // LANES):
        part = v[:, c * LANES:(c + 1) * LANES]
        out.append(part * cos + sign * (swap(part) * sin))
    return jnp.concatenate(out, axis=1) if len(out) > 1 else out[0]


def _half_mask(shape, half):
    lane = lax.broadcasted_iota(jnp.int32, shape, len(shape) - 1) % LANES
    return (lane < HEAD_DIM) if half == 0 else (lane >= HEAD_DIM)


def _dup_half(v, half):
    return jnp.where(_half_mask(v.shape, half), v, pltpu.roll(v, HEAD_DIM, 1))


def _fold_halves(v):
    return v + pltpu.roll(v, HEAD_DIM, 1)


def _pick_halves(lo_rows, hi_rows):
    return jnp.where(_half_mask(lo_rows.shape, 0), lo_rows, hi_rows)


def _stack_masked(v, pairs):
    parts = []
    for c in pairs:
        pair = v[:, c * LANES:(c + 1) * LANES]
        parts += [jnp.where(_half_mask(pair.shape, half), pair, 0.0) for half in (0, 1)]
    return jnp.concatenate(parts, axis=0)


def _stack_pair_cols(v, pairs):
    return jnp.concatenate([v[:, c * LANES + half * HEAD_DIM:c * LANES + half * HEAD_DIM + 1] for c in pairs for half in (0, 1)],
                           axis=0)


ATTN_UNITS = 16


def _class_rows(r):
    return [pl.ds(0, QBLOCK)] if r == 1 else [pl.ds(rho, QBLOCK, stride=r) for rho in range(r)]


def _band_mask(nrows, nk, blk, n_back, has_prev):
    qi = lax.broadcasted_iota(jnp.int32, (nrows, nk), 0) % QBLOCK
    ki = lax.broadcasted_iota(jnp.int32, (nrows, nk), 1)
    if has_prev:
        dist = qi + QBLOCK - ki
        return (dist >= 0) & (dist <= n_back) & ((ki >= QBLOCK) | (blk > 0))
    dist = qi - ki
    return (dist >= 0) & (dist <= n_back)


def _attn_fwd(q_arr, k_arr, v_arr, *, name, npair, gqa, q_col, k_col, v_col, nchunk, r, n_back, sinks=None):
    bsz, seq, _ = q_arr.shape
    rr = QBLOCK * r
    nblk = seq // rr
    qw = npair * LANES
    kw = LANES if gqa else qw
    has_prev = nblk > 1
    has_sink = sinks is not None
    scale = HEAD_DIM ** -0.5

    def body(*refs):
        refs = list(refs)
        q_ref, kc_ref, vc_ref = refs[:3]
        pos = 3
        if has_prev:
            kp_ref, vp_ref = refs[pos:pos + 2]
            pos += 2
        if has_sink:
            sink_ref = refs[pos]
            pos += 1
        o_ref, lse_ref = refs[pos:pos + 2]
        if r > 1:
            stage_o = refs[pos + 2]
        blk = pl.program_id(2)
        nk = (2 if has_prev else 1) * QBLOCK
        valid = _band_mask(QBLOCK, nk, blk, n_back, has_prev)
        per = npair // 2
        classes = _class_rows(r)
        step = max(1, ATTN_UNITS // (2 * npair))
        for first in range(0, len(classes), step):
            batch = classes[first:first + step]
            units = []
            for ci, rows in enumerate(batch):
                q = q_ref[0, rows, :] * scale
                k, v = kc_ref[0, rows, :], vc_ref[0, rows, :]
                if has_prev:
                    k = jnp.concatenate([kp_ref[0, rows, :], k], axis=0)
                    v = jnp.concatenate([vp_ref[0, rows, :], v], axis=0)
                if gqa:
                    kdup = [_dup_half(k, hk).astype(BF16) for hk in range(2)]
                    vdup = [_dup_half(v, hk) for hk in range(2)]
                for c in range(npair):
                    sl = slice(c * LANES, (c + 1) * LANES)
                    qc = q[:, sl]
                    kc, vc = (kdup[c // per], vdup[c // per]) if gqa else (k[:, sl].astype(BF16), v[:, sl])
                    for half in (0, 1):
                        qm = jnp.where(_half_mask(qc.shape, half), qc, 0.0).astype(BF16)
                        vm = jnp.where(_half_mask(vc.shape, half), vc, 0.0).astype(BF16)
                        s = lax.dot_general(qm, kc, _DIMS["nt"], preferred_element_type=F32)
                        units.append(dict(ci=ci, c=c, half=half, s=s, vm=vm, sk=sink_ref[2 * c + half] if has_sink else None))
            for u in units:
                s = jnp.where(valid, u["s"], NEG_INF)
                m = jnp.max(s, axis=1, keepdims=True)
                if has_sink:
                    m = jnp.maximum(m, u["sk"])
                p = jnp.exp(s - m)
                den = jnp.sum(p, axis=1, keepdims=True)
                if has_sink:
                    den = den + jnp.exp(u["sk"] - m)
                u.update(p=p.astype(BF16), den=den, lse=m + jnp.log(den))
            for u in units:
                u["o"] = jnp.dot(u["p"], u["vm"], preferred_element_type=F32) / u["den"]
            for ci, rows in enumerate(batch):
                outs, lses = [None] * npair, [None] * npair
                for u in units:
                    if u["ci"] != ci:
                        continue
                    c, o = u["c"], u["o"]
                    lse = jnp.broadcast_to(u["lse"], o.shape)
                    outs[c] = o if u["half"] == 0 else outs[c] + o
                    lses[c] = lse if u["half"] == 0 else _pick_halves(lses[c], lse)
                o_new = jnp.concatenate(outs, axis=1) if npair > 1 else outs[0]
                if r > 1:
                    stage_o[rows, :] = o_new
                else:
                    o_ref[0] = o_new.astype(BF16)
                lse_ref[0, rows, :] = jnp.concatenate(lses, axis=1) if npair > 1 else lses[0]
        if r > 1:
            o_ref[0] = stage_o[...].astype(BF16)

    def cur(width, col0):
        return pl.BlockSpec((1, rr, width), lambda b, c, i: (b, i, col0 + c))

    def prev(width, col0):
        return pl.BlockSpec((1, rr, width), lambda b, c, i: (b, jnp.maximum(i - 1, 0), col0 + c))

    in_specs = [cur(qw, q_col), cur(kw, k_col), cur(kw, v_col)]
    args = [q_arr, k_arr, v_arr]
    if has_prev:
        in_specs += [prev(kw, k_col), prev(kw, v_col)]
        args += [k_arr, v_arr]
    if has_sink:
        in_specs.append(pl.BlockSpec(memory_space=pltpu.SMEM))
        args.append(sinks)
    return pl.pallas_call(
        body,
        name=name,
        grid=(bsz, nchunk, nblk),
        in_specs=in_specs,
        out_specs=[pl.BlockSpec((1, rr, qw), lambda b, c, i: (b, i, c))] * 2,
        out_shape=[jax.ShapeDtypeStruct((bsz, seq, nchunk * qw), BF16), jax.ShapeDtypeStruct((bsz, seq, nchunk * qw), F32)],
        scratch_shapes=[pltpu.VMEM((rr, qw), F32)] if r > 1 else [],
        compiler_params=_params("parallel", "parallel", "parallel"),
    )(*args)


def _attn_bwd(q_arr, k_arr, v_arr, cos, sin, do, lse, dd, *, name, npair, gqa, q_col, k_col, v_col, nchunk, r, n_back,
              token=None):
    bsz, seq, _ = q_arr.shape
    rr = QBLOCK * r
    nblk = seq // rr
    qw = npair * LANES
    kw = LANES if gqa else qw
    has_next = nblk > 1
    has_token = token is not None
    staged = r > 1
    scale = HEAD_DIM ** -0.5

    def body(*refs):
        refs = list(refs)
        k_ref, v_ref, c_ref, s_ref = refs[:4]
        tile_refs = [refs[4:8]]
        pos = 8
        if has_next:
            tile_refs.append(refs[pos:pos + 4])
            pos += 4
        if has_token:
            pos += 1
        dq_ref, dk_ref, dv_ref = refs[pos:pos + 3]
        carry_ref, bcast_ref = refs[pos + 3:pos + 5]
        if staged:
            stage_q, stage_k, stage_v = refs[pos + 5:pos + 8]
        blk = pl.program_id(2)
        if has_next:
            @pl.when(blk == 0)
            def _():
                carry_ref[...] = jnp.zeros_like(carry_ref)

        nrows = (npair if gqa else 1) * QBLOCK
        qi = lax.broadcasted_iota(jnp.int32, (nrows, QBLOCK), 0) % QBLOCK
        ki = lax.broadcasted_iota(jnp.int32, (nrows, QBLOCK), 1)
        valids = [qi >= ki, (qi + QBLOCK - ki <= n_back) & (blk + 1 < nblk)]
        per = npair // 2
        ntile = len(tile_refs)
        cat = lambda parts: jnp.concatenate(parts, axis=1) if len(parts) > 1 else parts[0]
        classes = _class_rows(r)
        step = max(1, ATTN_UNITS // (ntile * (2 if gqa else 2 * npair)))
        def stat_cols(stat, slot):
            if gqa:
                return _stack_pair_cols(stat, list(range(slot * per, (slot + 1) * per)))
            col = slot * HEAD_DIM
            return stat[:, col:col + 1]

        nslot = 2 if gqa else 2 * npair
        if has_next:
            @pl.when(blk == 0)
            def _():
                for rows in classes:
                    for which, stat_ref in enumerate(tile_refs[0][2:4]):
                        stat = stat_ref[0, rows, :]
                        for slot in range(nslot):
                            bcast_ref[which, slot, rows if not gqa else slice(None), :] = jnp.broadcast_to(
                                stat_cols(stat, slot), (nrows, LANES))

        for first in range(0, len(classes), step):
            batch = classes[first:first + step]
            units = []
            for ci, rows in enumerate(batch):
                keep = slice(None) if gqa else rows
                tiles = [(q_ref[0, rows, :] * scale, do_ref[0, rows, :], l_ref[0, rows, :], d_ref[0, rows, :])
                         for q_ref, do_ref, l_ref, d_ref in tile_refs]

                def stats(t, slot, keep=keep, tiles=tiles):
                    if has_next and t == 0:
                        return bcast_ref[0, slot, keep, :], bcast_ref[1, slot, keep, :]
                    return tuple(jnp.broadcast_to(stat_cols(tiles[t][2 + w], slot), (nrows, LANES)) for w in range(2))

                k, v = k_ref[0, rows, :], v_ref[0, rows, :]
                if gqa:
                    for hk in range(2):
                        pairs = list(range(hk * per, (hk + 1) * per))
                        kd, vd = _dup_half(k, hk).astype(BF16), _dup_half(v, hk).astype(BF16)
                        for t, (q, do_, l_, d_) in enumerate(tiles):
                            lcol, dcol = stats(t, hk)
                            units.append(dict(ci=ci, t=t, hk=hk, slot=hk, keep=keep, pairs=pairs,
                                              qs=_stack_masked(q, pairs).astype(BF16),
                                              dos=_stack_masked(do_, pairs).astype(BF16), lcol=lcol, dcol=dcol,
                                              kmat=kd, vmat=vd, kdq=kd))
                else:
                    for c in range(npair):
                        sl = slice(c * LANES, (c + 1) * LANES)
                        kc, vcb = k[:, sl], v[:, sl].astype(BF16)
                        kcb = kc.astype(BF16)
                        for t, (q, do_, l_, d_) in enumerate(tiles):
                            for half in (0, 1):
                                hm = _half_mask(kc.shape, half)
                                lcol, dcol = stats(t, 2 * c + half)
                                units.append(dict(ci=ci, t=t, c=c, half=half, slot=2 * c + half, keep=keep,
                                                  qs=jnp.where(hm, q[:, sl], 0.0).astype(BF16),
                                                  dos=jnp.where(hm, do_[:, sl], 0.0).astype(BF16), lcol=lcol, dcol=dcol,
                                                  kmat=kcb, vmat=vcb, kdq=jnp.where(hm, kc, 0.0).astype(BF16)))
            for u in units:
                u["s"] = lax.dot_general(u["qs"], u["kmat"], _DIMS["nt"], preferred_element_type=F32)
                u["dp"] = lax.dot_general(u["dos"], u["vmat"], _DIMS["nt"], preferred_element_type=F32)
            for u in units:
                p = jnp.exp(jnp.where(valids[u["t"]], u["s"], NEG_INF) - u["lcol"])
                u["ds"] = (p * (u["dp"] + u["dcol"])).astype(BF16)
                u["p"] = p.astype(BF16)
            for u in units:
                u["dv"] = lax.dot_general(u["p"], u["dos"], _DIMS["tn"], preferred_element_type=F32)
                u["dk"] = lax.dot_general(u["ds"], u["qs"], _DIMS["tn"], preferred_element_type=F32)
                u["dq"] = jnp.dot(u["ds"], u["kdq"], preferred_element_type=F32) * scale
            for u in units:
                if u["t"] == 1:
                    bcast_ref[0, u["slot"], u["keep"], :] = u["lcol"]
                    bcast_ref[1, u["slot"], u["keep"], :] = u["dcol"]
            for ci, rows in enumerate(batch):
                mine = [u for u in units if u["ci"] == ci]
                dq = [[None] * npair for _ in range(ntile)]
                if gqa:
                    dk_out = dv_out = None
                    for hk in range(2):
                        us = [u for u in mine if u["hk"] == hk]
                        for u in us:
                            for i, c in enumerate(u["pairs"]):
                                dq[u["t"]][c] = _pick_halves(u["dq"][2 * i * QBLOCK:(2 * i + 1) * QBLOCK],
                                                             u["dq"][(2 * i + 1) * QBLOCK:(2 * i + 2) * QBLOCK])
                        dk_h = _fold_halves(functools.reduce(jnp.add, [u["dk"] for u in us]))
                        dv_h = _fold_halves(functools.reduce(jnp.add, [u["dv"] for u in us]))
                        dk_out = dk_h if hk == 0 else _pick_halves(dk_out, dk_h)
                        dv_out = dv_h if hk == 0 else _pick_halves(dv_out, dv_h)
                else:
                    dks, dvs = [], []
                    for c in range(npair):
                        us = [u for u in mine if u["c"] == c]
                        dks.append(functools.reduce(jnp.add, [u["dk"] for u in us]))
                        dvs.append(functools.reduce(jnp.add, [u["dv"] for u in us]))
                        for t in range(ntile):
                            dq[t][c] = functools.reduce(jnp.add, [u["dq"] for u in us if u["t"] == t])
                    dk_out, dv_out = cat(dks), cat(dvs)
                ck, sk_ = c_ref[0, rows, :], s_ref[0, rows, :]
                dk_new = _rope(dk_out, ck, sk_, sign=-1.0, mxu=gqa, coarse=True)
                dq_cur = cat(dq[0])
                if has_next:
                    dq_cur = dq_cur + carry_ref[rows, :]
                    carry_ref[rows, :] = cat(dq[1])
                dq_new = _rope(dq_cur, ck, sk_, sign=-1.0, mxu=gqa, coarse=True)
                if staged:
                    stage_q[rows, :], stage_k[rows, :], stage_v[rows, :] = dq_new, dk_new, dv_out
                else:
                    dq_ref[0], dk_ref[0], dv_ref[0] = dq_new.astype(BF16), dk_new.astype(BF16), dv_out.astype(BF16)
        if staged:
            dq_ref[0], dk_ref[0], dv_ref[0] = stage_q[...].astype(BF16), stage_k[...].astype(BF16), stage_v[...].astype(BF16)

    def at(width, col0, shift):
        return pl.BlockSpec((1, rr, width), lambda b, c, i: (b, jnp.minimum(i + shift, nblk - 1), col0 + c))

    in_specs = [at(kw, k_col, 0), at(kw, v_col, 0), pl.BlockSpec((1, rr, LANES), lambda b, c, i: (b, i, 0)),
                pl.BlockSpec((1, rr, LANES), lambda b, c, i: (b, i, 0))]
    args = [k_arr, v_arr, cos, sin]
    for shift in (0, 1) if has_next else (0,):
        in_specs += [at(qw, q_col, shift), at(qw, 0, shift), at(qw, 0, shift), at(qw, 0, shift)]
        args += [q_arr, do, lse, dd]
    if has_token:
        in_specs.append(pl.BlockSpec(token.shape, lambda b, c, i: (0, 0)))
        args.append(token)
    return pl.pallas_call(
        body,
        name=name,
        grid=(bsz, nchunk, nblk),
        in_specs=in_specs,
        out_specs=[pl.BlockSpec((1, rr, qw), lambda b, c, i: (b, i, c)),
                   pl.BlockSpec((1, rr, kw), lambda b, c, i: (b, i, c)),
                   pl.BlockSpec((1, rr, kw), lambda b, c, i: (b, i, c))],
        out_shape=[jax.ShapeDtypeStruct((bsz, seq, nchunk * qw), BF16),
                   jax.ShapeDtypeStruct((bsz, seq, nchunk * kw), BF16),
                   jax.ShapeDtypeStruct((bsz, seq, nchunk * kw), BF16)],
        scratch_shapes=[pltpu.VMEM((rr, qw) if has_next else (8, LANES), F32),
                        pltpu.VMEM((2, 2 if gqa else 2 * npair, npair * QBLOCK if gqa else rr, LANES) if has_next
                                   else (1, 1, 8, LANES), F32)] +
                       ([pltpu.VMEM((rr, qw), F32), pltpu.VMEM((rr, kw), F32), pltpu.VMEM((rr, kw), F32)] if staged else []),
        compiler_params=_params("parallel", "parallel", "arbitrary"),
    )(*args)


B_CHUNKS = {1: (4, 1), 4: (1, 4), 16: (1, 4)}


def _rope_tables(positions):
    half = HEAD_DIM // 2
    inv = ROPE_THETA ** (-jnp.arange(half, dtype=F32) / half)
    ang = positions.astype(F32)[..., None] * inv
    cos, sin = jnp.cos(ang), jnp.sin(ang)
    return jnp.concatenate([cos] * 4, axis=-1), jnp.concatenate([-sin, sin, -sin, sin], axis=-1)


def _layer_step(x, mod, tables, sinks, ln1_g, ln1_b, ln2_g, ln2_b, target, get_w_in, get_rest, hook):
    bsz, seq, d = x.shape
    ntok = bsz * seq
    flat = lambda v: v.reshape(ntok, v.shape[-1])
    unflat = lambda v: v.reshape(bsz, seq, v.shape[-1])
    cos, sin = tables
    mm = functools.partial(_matmul, tm=1024, tk=1024)
    scalar = lambda tok: 0.0 if tok is None else tok[0, 0]

    u1 = _modulate_in(x, mod)
    u1f = flat(u1)
    wint = get_w_in(u1)
    cosf, sinf = flat(cos), flat(sin)
    proj = functools.partial(_proj_rope, u1f, wint, cosf, sinf, tm=2048)
    qkvb = unflat(proj(n=4608, b_off=OFF_QKVB, rope_cols=3072, tn=256, name="proj_qkvb"))
    b_kws, os_, ls_ = [], [], []
    for g, (window, r) in enumerate(B_PATTERNS):
        npair, nch = B_CHUNKS[r]
        per = B_HEADS_PER_GROUP // (2 * npair)
        nsec = len(B_PATTERNS) * per
        kw_ = dict(npair=npair, gqa=False, q_col=g * per, k_col=nsec + g * per, v_col=2 * nsec + g * per, nchunk=nch, r=r,
                   n_back=window // r)
        b_kws.append(kw_)
        o_g, l_g = _attn_fwd(qkvb, qkvb, qkvb, name=f"attn_b{g}_fwd", **kw_)
        os_.append(o_g)
        ls_.append(l_g)
    tok = hook("projected", os_[-1])
    proj = functools.partial(_proj_rope, u1f, wint, cosf + scalar(tok), sinf, tm=2048)
    gab = unflat(proj(n=2048, b_off=OFF_GAB, rope_cols=0, tn=256, name="proj_gab", out_dtype=BF16))
    qa = kva = unflat(proj(n=OFF_QKVB, b_off=OFF_QA, rope_cols=OFF_KVA + LANES, tn=256, name="proj_qkva", out_dtype=BF16))
    a_kw = dict(npair=A_Q_HEADS // 2, gqa=True, q_col=0, k_col=OFF_KVA // LANES, v_col=OFF_KVA // LANES + 1, nchunk=1, r=1,
                n_back=A_WINDOW - 1)
    after_gab = jnp.minimum(jnp.abs(gab[0, 0, 0].astype(F32)), 0.0)
    oa, lse_a = _attn_fwd(qa, kva, kva, name="attn_a_fwd", sinks=sinks.reshape(A_Q_HEADS) + after_gab, **a_kw)
    rest = get_rest(oa)
    wba, wbbt, wo, wgut, wd = (rest[n] for n in ("w_branch_a", "w_branch_b", "w_o", "w_gate_up", "w_down"))
    ya = unflat(mm(flat(oa), wba, mode="nn", out_dtype=BF16, tn=512, name="branch_a"))
    ybf, mergedf, obf = _merge_branch_b_gate([flat(t) for t in os_], [flat(t) for t in ls_], wbbt, flat(gab), flat(ya))
    xf = flat(x)
    y1f, h1f, u2f = _wo_ln1(mergedf, wo, xf, mod, ln1_g, ln1_b, seq)
    wgut_i = _interleave_gate_up(wgut)
    hf, af = _gate_up_silu(u2f, wgut_i)

    dy2f, dh1af, acc2 = _down_ln2_loss_bwd(af, wd, h1f, mod, ln2_g, ln2_b, flat(target), seq)
    g_wd = _matmul(af, dy2f, mode="tn", out_dtype=BF16, tm=256, tn=1024, tk=ntok, name="down_wgrad")
    dhf = _down_dgrad_silu_bwd(dy2f, wd, hf)
    g_wgut = _interleave_gate_up(_matmul(dhf, u2f, mode="tn", out_dtype=BF16, tm=256, tn=1024, tk=ntok, name="gate_up_wgrad"))
    dy1f, dxaf, acc1 = _gate_up_dgrad_ln1_bwd(dhf, wgut_i, dh1af, xf, y1f, mod, ln1_g, ln1_b, seq)
    g_wo = _matmul(mergedf, dy1f, mode="tn", out_dtype=BF16, tm=256, tn=1024, tk=ntok, name="w_o_wgrad")
    dyaf, dybf, dgaf, dgbf = _wo_dgrad_gate_bwd(dy1f, wo, flat(gab), flat(ya), ybf)
    g_wba = _matmul(flat(oa), dyaf, mode="tn", out_dtype=BF16, tm=256, tn=1024, tk=ntok, name="branch_a_wgrad")
    g_wbbt = _matmul(dybf, obf, mode="tn", out_dtype=BF16, tm=256, tn=512, tk=ntok, name="branch_b_wgrad")
    tok = hook("grads_rest", dict(w_branch_a=g_wba, w_branch_b=g_wbbt, w_o=g_wo, w_gate_up=g_wgut, w_down=g_wd))

    sinks_exp = jnp.repeat(sinks.reshape(1, A_Q_HEADS), HEAD_DIM, axis=1) + scalar(tok)
    doa, dd_a, acc_s = _branch_a_dgrad_delta(dyaf, wba, flat(oa), flat(lse_a), sinks_exp, seq)
    doa, dd_a = unflat(doa), unflat(dd_a)
    tok = hook("delta_done", dd_a)
    dqa, dka, dva = _attn_bwd(qa, kva, kva, cos, sin, doa, lse_a, dd_a, name="attn_a_bwd", token=tok, **a_kw)
    merged_bwd = [unflat(t) for t in _branch_b_dgrad_merge_bwd(dybf, wbbt, [flat(t) for t in os_], [flat(t) for t in ls_])]
    dqs, dks, dvs = [], [], []
    for g in range(len(B_PATTERNS)):
        dq_g, dk_g, dv_g = _attn_bwd(qkvb, qkvb, qkvb, cos, sin, merged_bwd[g], ls_[g], merged_bwd[3 + g],
                                     name=f"attn_b{g}_bwd", **b_kws[g])
        dqs.append(dq_g)
        dks.append(dk_g)
        dvs.append(dv_g)
    dproj = jnp.concatenate([t.astype(BF16) for t in [dqa, dka, dva] + dqs + dks + dvs] + [unflat(dgaf), unflat(dgbf)], axis=-1)
    dprojf = flat(dproj)
    g_wint = _matmul(dprojf, u1f, mode="tn", out_dtype=BF16, tm=256, tn=1024, tk=ntok, name="w_in_wgrad")
    tok = hook("grads_w_in", dict(w_in=g_wint))
    grad_x, acc0 = _w_in_dgrad_grad_x(dprojf, wint, dxaf, xf, mod, seq, tok)
    grad_x = unflat(grad_x)
    tok = hook("dgrad_done", grad_x)

    loss_part = jnp.sum(acc2[:, 3, 0])
    dmod = jnp.stack([acc0[:, 1], acc0[:, 0], acc1[:, 2], acc1[:, 4], acc1[:, 3], acc2[:, 2]], axis=1)
    small = jnp.stack([acc1[:, 0].sum(0), acc1[:, 1].sum(0), acc2[:, 0].sum(0), acc2[:, 1].sum(0), acc_s[:, 0].sum(0)])
    small = small + scalar(tok)
    return loss_part, grad_x, dmod, small


CHIP_FLIPS = (2, 4, 6)


def _my_place():
    return lax.axis_index("x"), lax.axis_index("y"), lax.axis_index("c")


def _flip(place, k):
    px, py, pc = place
    return (1 - px if k & 4 else px, 1 - py if k & 2 else py, 1 - pc if k & 1 else pc)


def _index(place):
    return 4 * place[0] + 2 * place[1] + place[2]


def _gather_small(v, name):
    rows, cols = v.shape

    def body(v_ref, out_ref, send_sems, recv_sems):
        me = _my_place()
        out_ref[_index(me)] = v_ref[...]
        copies = []
        for k in range(1, N_DEV):
            copies.append(pltpu.make_async_remote_copy(
                src_ref=v_ref, dst_ref=out_ref.at[_index(me)], send_sem=send_sems.at[k - 1], recv_sem=recv_sems.at[k - 1],
                device_id=_flip(me, k), device_id_type=MESH))
        for cp in copies:
            cp.start()
        for k in range(1, N_DEV):
            pltpu.make_async_remote_copy(
                src_ref=v_ref, dst_ref=out_ref.at[_index(_flip(me, k))], send_sem=send_sems.at[k - 1],
                recv_sem=recv_sems.at[k - 1], device_id=_flip(me, k), device_id_type=MESH).wait_recv()
        for cp in copies:
            cp.wait_send()

    return pl.pallas_call(
        body,
        name=name,
        out_shape=jax.ShapeDtypeStruct((N_DEV, rows, cols), v.dtype),
        in_specs=[pl.BlockSpec(memory_space=pltpu.VMEM)],
        out_specs=pl.BlockSpec(memory_space=pltpu.VMEM),
        scratch_shapes=[pltpu.SemaphoreType.DMA((N_DEV - 1,)), pltpu.SemaphoreType.DMA((N_DEV - 1,))],
        compiler_params=pltpu.CompilerParams(vmem_limit_bytes=VMEM_LIMIT_BYTES),
    )(v)


_HBM = pl.BlockSpec(memory_space=pltpu.HBM)
_SEM = pl.BlockSpec(memory_space=pltpu.SEMAPHORE)
_EFFECT = pltpu.SideEffectType.DATAFLOW_SIDE_EFFECTING


def _remote(src, dst, send_sems, recv_sems, j, to):
    return pltpu.make_async_remote_copy(src_ref=src, dst_ref=dst, send_sem=send_sems.at[j], recv_sem=recv_sems.at[j],
                                        device_id=to, device_id_type=MESH)


def _copies_start(name, bufs, make_copies, nsem):
    nbuf = len(bufs)

    def body(*refs):
        for cp in make_copies(refs[:nbuf], refs[nbuf], refs[nbuf + 1]):
            cp.start()
        refs[-1][...] = jnp.zeros_like(refs[-1])

    sems = pltpu.SemaphoreType.DMA((nsem,))
    res = pl.pallas_call(
        body, name=name,
        out_shape=(sems, sems, *[pltpu.HBM(v.shape, v.dtype) for v in bufs], jax.ShapeDtypeStruct((8, LANES), F32)),
        in_specs=(_HBM,) * nbuf, out_specs=(_SEM, _SEM) + (_HBM,) * nbuf + (pl.BlockSpec(memory_space=pltpu.VMEM),),
        input_output_aliases={i: 2 + i for i in range(nbuf)},
        compiler_params=pltpu.CompilerParams(has_side_effects=_EFFECT),
    )(*[pltpu.with_memory_space_constraint(v, pltpu.HBM) for v in bufs])
    return res[0], res[1], list(res[2:2 + nbuf]), res[-1]


def _copies_wait(name, started, make_copies, after):
    send_sems, recv_sems, bufs, _ = started
    nbuf = len(bufs)

    def body(*refs):
        for cp in make_copies(refs[:nbuf], refs[nbuf], refs[nbuf + 1]):
            cp.wait_send()
            cp.wait_recv()

    return list(pl.pallas_call(
        body, name=name,
        out_shape=tuple(pltpu.HBM(v.shape, v.dtype) for v in bufs),
        in_specs=(_HBM,) * nbuf + (_SEM, _SEM, pl.BlockSpec(memory_space=pl.ANY)), out_specs=(_HBM,) * nbuf,
        input_output_aliases={i: i for i in range(nbuf)},
        compiler_params=pltpu.CompilerParams(has_side_effects=_EFFECT),
    )(*bufs, send_sems, recv_sems, after))


def _to_sibling_copies(refs, send_sems, recv_sems):
    src_ref, land_ref = refs
    me = _my_place()
    return [_remote(src_ref.at[q, 1 - me[2]], land_ref.at[q], send_sems, recv_sems, q, _flip(me, 1)) for q in range(4)]


def _to_chips_copies(refs, send_sems, recv_sems):
    src_ref, land_ref = refs
    me = _my_place()
    copies = []
    for j, k in enumerate(CHIP_FLIPS):
        to = _flip(me, k)
        copies.append(_remote(src_ref.at[2 * to[0] + to[1]], land_ref.at[j], send_sems, recv_sems, j, to))
    return copies


class _Gather:
    def __init__(self, name, blocks):
        self.name, self.n = name, len(blocks)
        at_me = (_index(_my_place()), 0, 0)
        lands = [lax.dynamic_update_slice(lax.empty((N_DEV,) + v.shape, v.dtype), v[None], at_me) for v in blocks]
        self.first = _copies_start(name + "_start", list(blocks) + lands, self._first_copies, 4 * self.n)
        self.token = self.first[3]

    def _first_copies(self, refs, send_sems, recv_sems):
        me = _my_place()
        return [_remote(refs[w], refs[self.n + w].at[_index(me)], send_sems, recv_sems, 4 * w + j, _flip(me, k))
                for w in range(self.n) for j, k in enumerate((1,) + CHIP_FLIPS)]

    def _pass_copies(self, refs, send_sems, recv_sems):
        me = _my_place()
        copies = []
        for w, land in enumerate(refs):
            for j, k in enumerate(CHIP_FLIPS):
                slot = land.at[_index(_flip(me, k))]
                copies.append(_remote(slot, slot, send_sems, recv_sems, 3 * w + j, _flip(me, 1)))
        return copies

    def pass_on(self, after):
        lands = _copies_wait(self.name + "_wait", self.first, self._first_copies, after)[self.n:]
        self.second = _copies_start(self.name + "_pass_start", lands, self._pass_copies, 3 * self.n)
        return self.second[3]

    def finish(self, after):
        return _copies_wait(self.name + "_pass_wait", self.second, self._pass_copies, after)


SUM_SPLIT = 2


def _sum_pairs(parts, theirs):
    nchip, _, rows, cols = parts.shape
    tile = rows // SUM_SPLIT

    def body(c_ref, a_ref, b_ref, o_ref):
        o_ref[...] = (a_ref[0].astype(F32) + b_ref[...].astype(F32)).astype(BF16)

    spec = pl.BlockSpec((1, tile, cols), lambda q, t, c_ref: (q, t, 0))
    grid_spec = pltpu.PrefetchScalarGridSpec(
        num_scalar_prefetch=1, grid=(nchip, SUM_SPLIT),
        in_specs=[pl.BlockSpec((1, 1, tile, cols), lambda q, t, c_ref: (q, c_ref[0], t, 0)), spec], out_specs=spec)
    return pl.pallas_call(body, name="grad_sum_sibling", grid_spec=grid_spec,
                          out_shape=jax.ShapeDtypeStruct((nchip, rows, cols), BF16),
                          compiler_params=_params("parallel", "parallel"))(lax.axis_index("c").reshape(1), parts, theirs)


def _sum_final(chip_sum, got):
    _, rows, cols = chip_sum.shape
    tile = rows // SUM_SPLIT

    def body(q_ref, a_ref, g_ref, o_ref):
        o_ref[...] = ((a_ref[0].astype(F32) + g_ref[0].astype(F32)) + g_ref[1].astype(F32)) + g_ref[2].astype(F32)

    grid_spec = pltpu.PrefetchScalarGridSpec(
        num_scalar_prefetch=1, grid=(SUM_SPLIT,),
        in_specs=[pl.BlockSpec((1, tile, cols), lambda t, q_ref: (q_ref[0], t, 0)),
                  pl.BlockSpec((3, tile, cols), lambda t, q_ref: (0, t, 0))],
        out_specs=pl.BlockSpec((tile, cols), lambda t, q_ref: (t, 0)))
    my_chip = (2 * lax.axis_index("x") + lax.axis_index("y")).reshape(1)
    return pl.pallas_call(body, name="grad_sum_chips", grid_spec=grid_spec, out_shape=jax.ShapeDtypeStruct((rows, cols), F32),
                          compiler_params=_params("parallel"))(my_chip, chip_sum, got)


class _ReduceScatter:
    def __init__(self, name, slabs):
        self.name, self.rows = name, slabs.shape[1]
        parts = slabs.reshape(4, 2, self.rows, D_MODEL)
        self.first = _copies_start(name + "_sibling_start", [parts, lax.empty((4, self.rows, D_MODEL), slabs.dtype)],
                                   _to_sibling_copies, 4)
        self.token = self.first[3]

    def between_chips(self, after):
        parts, theirs = _copies_wait(self.name + "_sibling_wait", self.first, _to_sibling_copies, after)
        chip_sum = _sum_pairs(parts, theirs)
        self.second = _copies_start(self.name + "_chips_start", [chip_sum, lax.empty((3, self.rows, D_MODEL), chip_sum.dtype)],
                                    _to_chips_copies, 3)
        return self.second[3]

    def finish(self, after):
        chip_sum, got = _copies_wait(self.name + "_chips_wait", self.second, _to_chips_copies, after)
        return _sum_final(chip_sum, got)


def _ada_fwd(c_all, w, b):
    nb, _ = c_all.shape
    ncol = w.shape[1]

    def body(c_ref, w_ref, b_ref, o_ref):
        c = c_ref[...]
        act = (c * _sigmoid(c)).astype(BF16)
        o_ref[...] = jnp.dot(act, w_ref[...].astype(BF16), preferred_element_type=F32) + b_ref[...]

    return pl.pallas_call(body, name="ada_fwd", out_shape=jax.ShapeDtypeStruct((nb, ncol), F32),
                          compiler_params=pltpu.CompilerParams(vmem_limit_bytes=VMEM_LIMIT_BYTES))(c_all, w, b)


def _ada_wgrad(c_all_t, dmod_cols):
    d, nb = c_all_t.shape
    ncol = dmod_cols.shape[1]

    def body(ct_ref, dm_ref, o_ref):
        ct = ct_ref[...]
        act = (ct * _sigmoid(ct)).astype(BF16).astype(F32)
        dm = dm_ref[...].astype(BF16).astype(F32)
        acc = act[:, 0:1] * dm[0:1, :]
        for i in range(1, nb):
            acc = acc + act[:, i:i + 1] * dm[i:i + 1, :]
        o_ref[...] = acc

    return pl.pallas_call(body, name="ada_wgrad", out_shape=jax.ShapeDtypeStruct((d, ncol), F32),
                          compiler_params=pltpu.CompilerParams(vmem_limit_bytes=VMEM_LIMIT_BYTES))(c_all_t, dmod_cols)


SMALL_ROWS = 24


def _reduce_small(gathered):
    def body(g_ref, o_ref):
        acc = g_ref[0]
        for dev in range(1, N_DEV):
            acc = acc + g_ref[dev]
        o_ref[...] = acc

    return pl.pallas_call(body, name="reduce_small", out_shape=jax.ShapeDtypeStruct(gathered.shape[1:], F32))(gathered)


def _adamw_math(w, g, m, v):
    nm = ADAM_B1 * m + (1.0 - ADAM_B1) * g
    nv = ADAM_B2 * v + (1.0 - ADAM_B2) * (g * g)
    bc1 = 1.0 - ADAM_B1 ** ADAM_STEP
    bc2 = 1.0 - ADAM_B2 ** ADAM_STEP
    return -ADAM_LR * ((nm / bc1) / (jnp.sqrt(nv / bc2) + ADAM_EPS) + ADAM_WD * w), nm, nv


def _adamw_small(ws, gs, ms, vs, name):
    n = len(ws)

    def body(*refs):
        for i in range(n):
            res = _adamw_math(*(refs[k * n + i][...] for k in range(4)))
            for k in range(3):
                refs[(4 + k) * n + i][...] = res[k]

    shapes = [jax.ShapeDtypeStruct(w.shape, F32) for w in ws]
    res = pl.pallas_call(body, name=name, out_shape=shapes * 3)(*ws, *gs, *ms, *vs)
    return [(res[i], res[n + i], res[2 * n + i]) for i in range(n)]


def _adamw(w, g, m, v, name):
    rows, cols = w.shape
    tile = rows
    for cand in range(min(rows // 2, 512) // 8 * 8, 7, -8):
        if rows % cand == 0:
            tile = cand
            break
    spec = pl.BlockSpec((tile, cols), lambda t: (t, 0))

    def body(w_ref, g_ref, m_ref, v_ref, d_ref, nm_ref, nv_ref):
        d_ref[...], nm_ref[...], nv_ref[...] = _adamw_math(w_ref[...], g_ref[...], m_ref[...], v_ref[...])

    shp = jax.ShapeDtypeStruct((rows, cols), F32)
    return pl.pallas_call(body, name=name, grid=(rows // tile,), in_specs=[spec] * 4, out_specs=[spec] * 3, out_shape=[shp] * 3,
                          compiler_params=_params("parallel"))(w, g, m, v)


_WEIGHTS = ("w_ada", "b_ada", "w_in", "sinks", "w_branch_a", "w_branch_b", "w_o", "ln1_g", "ln1_b", "w_gate_up", "w_down",
            "ln2_g", "ln2_b")
_TRANSPOSED = ("w_in", "w_branch_b", "w_gate_up")


def _pack_shard(name, w):
    w = w.astype(BF16)
    if name in _TRANSPOSED:
        w = w.T
    return w.reshape(-1, D_MODEL)


def _unpack_full(name, slab):
    if name == "w_branch_b":
        return slab.reshape(N_DEV * 128, 512)
    return slab.reshape(-1, D_MODEL)


def _unpack_group(group, gathered):
    return {n: _unpack_full(n, slab) for (n, _), slab in zip(group, gathered)}


def _unpack_grads(group, g_packed):
    g_w, off = {}, 0
    for n, r in group:
        part = g_packed[off:off + r]
        off += r
        g_w[n] = part.reshape(128, 512) if n == "w_branch_b" else part
    return g_w


def kernel(x, c, positions, w_ada, b_ada, w_in, sinks, w_branch_a, w_branch_b, w_o, ln1_g, ln1_b, w_gate_up, w_down, ln2_g, ln2_b, loss_target, m_w_ada, m_b_ada, m_w_in, m_sinks, m_w_branch_a, m_w_branch_b, m_w_o, m_ln1_g, m_ln1_b, m_w_gate_up, m_w_down, m_ln2_g, m_ln2_b, v_w_ada, v_b_ada, v_w_in, v_sinks, v_w_branch_a, v_w_branch_b, v_w_o, v_ln1_g, v_ln1_b, v_w_gate_up, v_w_down, v_ln2_g, v_ln2_b):
    weights = dict(w_ada=w_ada, b_ada=b_ada, w_in=w_in, sinks=sinks, w_branch_a=w_branch_a, w_branch_b=w_branch_b, w_o=w_o,
                   ln1_g=ln1_g, ln1_b=ln1_b, w_gate_up=w_gate_up, w_down=w_down, ln2_g=ln2_g, ln2_b=ln2_b)
    m_in = dict(w_ada=m_w_ada, b_ada=m_b_ada, w_in=m_w_in, sinks=m_sinks, w_branch_a=m_w_branch_a, w_branch_b=m_w_branch_b,
                w_o=m_w_o, ln1_g=m_ln1_g, ln1_b=m_ln1_b, w_gate_up=m_w_gate_up, w_down=m_w_down, ln2_g=m_ln2_g, ln2_b=m_ln2_b)
    v_in = dict(w_ada=v_w_ada, b_ada=v_b_ada, w_in=v_w_in, sinks=v_sinks, w_branch_a=v_w_branch_a, w_branch_b=v_w_branch_b,
                w_o=v_w_o, ln1_g=v_ln1_g, ln1_b=v_ln1_b, w_gate_up=v_w_gate_up, w_down=v_w_down, ln2_g=v_ln2_g, ln2_b=v_ln2_b)
    bsz = x.shape[0]
    me = _index(_my_place())
    ada_cols = w_ada.shape[2]
    outs = {}

    def adamw(n, g):
        w2, m2, v2 = (t[n][0] if t[n].ndim == 3 else t[n] for t in (weights, m_in, v_in))
        shape = weights[n].shape
        if n in _TRANSPOSED:
            dlt, nm, nv = _adamw(w2.T, g, m2.T, v2.T, "adamw_" + n)
            outs[n] = tuple(t.T.reshape(shape) for t in (g, dlt, nm, nv))
        else:
            dlt, nm, nv = _adamw(w2, g, m2, v2, "adamw_" + n)
            outs[n] = tuple(t.reshape(shape) for t in (g, dlt, nm, nv))
        return nv

    packed_in = [_pack_shard(n, weights[n][0]) for n, _ in GROUP_IN]
    packed_rest = [_pack_shard(n, weights[n][0]) for n, _ in GROUP_REST]
    c_all = _gather_small(jnp.pad(c, ((0, 8 - bsz), (0, 0))), "gather_c")[:, :bsz].reshape(N_DEV * bsz, D_MODEL)
    gather_in = _Gather("gather_w_in", lax.optimization_barrier((packed_in, c_all))[0])
    b_cols = lax.dynamic_slice_in_dim(b_ada, me * ada_cols, ada_cols, axis=1)
    mod_cols = _ada_fwd(c_all, w_ada[0], b_cols + gather_in.token[0, 0])
    tables = _rope_tables(positions)
    mod_cols, tables, packed_rest = lax.optimization_barrier((mod_cols, tables, packed_rest))
    mod_all = _gather_small(mod_cols, "gather_mod").transpose(1, 0, 2).reshape(N_DEV * bsz, 6, D_MODEL)
    gather_rest = _Gather("gather_rest", lax.optimization_barrier((packed_rest, mod_all))[0])
    mod = jnp.pad(lax.dynamic_slice_in_dim(mod_all, me * bsz, bsz, axis=0), ((0, 0), (0, 2), (0, 0)))
    mod = mod + gather_rest.token[0, 0]
    mod = mod + gather_in.pass_on(mod)[0, 0]

    scatters = {}

    def get_w_in(after):
        return _unpack_group(GROUP_IN, gather_in.finish(after))["w_in"]

    def get_rest(after):
        return _unpack_group(GROUP_REST, gather_rest.finish(after))

    def pack_grads(group, grads):
        return jnp.concatenate([grads[n].reshape(N_DEV, r, D_MODEL) for n, r in group], axis=1)

    def hook(point, value):
        if point == "projected":
            return gather_rest.pass_on(value)
        if point == "grads_rest":
            scatters["rest"] = _ReduceScatter("scatter_rest", pack_grads(GROUP_REST, value))
            return scatters["rest"].token
        if point == "delta_done":
            return scatters["rest"].between_chips(value)
        if point == "grads_w_in":
            scatters["in"] = _ReduceScatter("scatter_w_in", pack_grads(GROUP_IN, value))
            last = scatters["in"].token
            for n, g in _unpack_grads(GROUP_REST, scatters["rest"].finish(last)).items():
                last = adamw(n, g)
            return scatters["in"].between_chips(last)
        if point == "dgrad_done":
            return None
        raise ValueError(point)

    loss_part, grad_x, dmod, small = _layer_step(x, mod, tables, sinks[0], ln1_g, ln1_b, ln2_g, ln2_b, loss_target,
                                                 get_w_in, get_rest, hook)

    rows = jnp.concatenate([dmod.reshape(bsz * 6, D_MODEL), small, jnp.full((1, D_MODEL), loss_part, F32),
                            jnp.zeros((SMALL_ROWS - bsz * 6 - 6, D_MODEL), F32)], axis=0)
    small_all = _gather_small(rows, "gather_small")
    sums = _reduce_small(small_all)
    loss = sums[bsz * 6 + 5, 0]
    dmod_all = small_all[:, :bsz * 6].reshape(N_DEV * bsz, 6 * D_MODEL)
    small_g = {"b_ada": functools.reduce(jnp.add, [sums[6 * i:6 * i + 6] for i in range(bsz)]).reshape(1, 6 * D_MODEL),
               "sinks": sums[bsz * 6 + 4][::HEAD_DIM][None]}
    small_g.update({n: sums[bsz * 6 + i][None] for i, n in enumerate(("ln1_g", "ln1_b", "ln2_g", "ln2_b"))})
    names = list(small_g)
    for n, (dlt, nm, nv) in zip(names, _adamw_small([weights[n] for n in names], [small_g[n] for n in names],
                                                     [m_in[n] for n in names], [v_in[n] for n in names], "adamw_small")):
        outs[n] = (small_g[n], dlt, nm, nv)
    dmod_cols = lax.dynamic_slice_in_dim(dmod_all, me * ada_cols, ada_cols, axis=1)
    adamw("w_ada", _ada_wgrad(c_all.T, dmod_cols))
    done = lax.optimization_barrier(tuple(outs[n][3] for n in outs))
    for n, g in _unpack_grads(GROUP_IN, scatters["in"].finish(done[0])).items():
        adamw(n, g)

    return (loss, grad_x, *[outs[n][0] for n in _WEIGHTS], *[outs[n][1] for n in _WEIGHTS], *[outs[n][2] for n in _WEIGHTS],
            *[outs[n][3] for n in _WEIGHTS])
```

```python
import functools

import jax
import jax.numpy as jnp
from jax import lax
from jax.experimental import pallas as pl
from jax.experimental.pallas import tpu as pltpu

F32 = jnp.float32
BF16 = jnp.bfloat16

D_MODEL = 1024
HEAD_DIM = 64
A_Q_HEADS = 16
A_WINDOW = 128
B_PATTERNS = ((128, 1), (512, 4), (2048, 16))
B_HEADS_PER_GROUP = 8
D_FF = 2816
QBLOCK = 128
ROPE_THETA = 10000.0
LN_EPS = 1e-5
DEEPNORM_ALPHA = 2.0 ** 0.25
NEG_INF = -1e30
ADAM_LR, ADAM_B1, ADAM_B2, ADAM_EPS, ADAM_WD, ADAM_STEP = 0.001, 0.9, 0.999, 1e-08, 0.01, 10

N_DEV = 8
LANES = 128
VMEM_LIMIT_BYTES = 56 * 1024 * 1024
MESH = pl.DeviceIdType.MESH

OFF_QA, OFF_KVA, OFF_QKVB, OFF_GAB = 0, 1024, 1280, 5888
GROUP_IN = (("w_in", 992),)
GROUP_REST = (("w_branch_a", 128), ("w_branch_b", 64), ("w_o", 128), ("w_gate_up", 704), ("w_down", 352))


def _params(*sem):
    return pltpu.CompilerParams(dimension_semantics=sem, vmem_limit_bytes=VMEM_LIMIT_BYTES)


def _sigmoid(x):
    return 1.0 / (1.0 + jnp.exp(-x))


_DIMS = {"nn": (((1,), (0,)), ((), ())), "nt": (((1,), (1,)), ((), ())), "tn": (((0,), (0,)), ((), ()))}


def _matmul(a, b, *, mode, tm, tn, tk, name, out_dtype=None, n=None, b_off=0, token=None, ins=(), outs=None, epilogue=None,
            lhs_fn=None):
    if mode == "nn":
        (m, k), nn_ = a.shape, b.shape[1]
    elif mode == "nt":
        (m, k), nn_ = a.shape, (b.shape[0] if n is None else n)
    else:
        (k, m), nn_ = a.shape, b.shape[1]
    assert m % tm == 0 and nn_ % tn == 0 and k % tk == 0 and b_off % tn == 0, (name, m, nn_, k)
    nk = k // tk
    joff = b_off // tn
    if mode == "nn":
        a_spec = pl.BlockSpec((tm, tk), lambda i, j, kk: (i, kk))
        b_spec = pl.BlockSpec((tk, tn), lambda i, j, kk: (kk, j))
    elif mode == "nt":
        a_spec = pl.BlockSpec((tm, tk), lambda i, j, kk: (i, kk))
        b_spec = pl.BlockSpec((tn, tk), lambda i, j, kk: (j + joff, kk))
    else:
        a_spec = pl.BlockSpec((tk, tm), lambda i, j, kk: (kk, i))
        b_spec = pl.BlockSpec((tk, tn), lambda i, j, kk: (kk, j))
    dims = _DIMS[mode]
    has_token = token is not None
    plain = epilogue is None
    if plain:
        outs = [(jax.ShapeDtypeStruct((m, nn_), out_dtype), (tm, tn), lambda i, j: (i, j))]

        def epilogue(acc, i, j, in_refs, out_refs):
            out_refs[0][...] = acc.astype(out_refs[0].dtype)

    nin = len(ins)
    nscratch = 1 if lhs_fn is None else 2
    assert lhs_fn is None or nk == 1

    def body(*refs):
        a_ref, b_ref = refs[:2]
        in_refs = refs[2:2 + nin]
        out_refs = refs[2 + nin + has_token:-nscratch]
        acc_ref = refs[-nscratch]
        kk = pl.program_id(2)
        if lhs_fn is None:
            lhs = a_ref[...].astype(BF16)
        else:
            lhs_ref = refs[-1]

            @pl.when(pl.program_id(1) == 0)
            def _():
                lhs_ref[...] = lhs_fn(a_ref, in_refs, out_refs)

            lhs = lhs_ref[...]
        part = lax.dot_general(lhs, b_ref[...].astype(BF16), dims, preferred_element_type=F32)

        def finish(acc):
            epilogue(acc, pl.program_id(0), pl.program_id(1), in_refs, out_refs)

        if nk == 1:
            finish(part)
        else:
            @pl.when(kk == 0)
            def _():
                acc_ref[...] = part

            @pl.when(kk > 0)
            def _():
                acc_ref[...] += part

            @pl.when(kk == nk - 1)
            def _():
                finish(acc_ref[...])

    def spec(block, index):
        return pl.BlockSpec(block, lambda i, j, kk: index(i, j))

    in_specs, args = [a_spec, b_spec], [a, b]
    for arr, block, index in ins:
        in_specs.append(spec(block, index))
        args.append(arr)
    if has_token:
        in_specs.append(pl.BlockSpec(token.shape, lambda i, j, kk: (0, 0)))
        args.append(token)
    res = pl.pallas_call(
        body,
        name=name,
        grid=(m // tm, nn_ // tn, nk),
        in_specs=in_specs,
        out_specs=[spec(block, index) for _, block, index in outs],
        out_shape=[shape for shape, _, _ in outs],
        scratch_shapes=[pltpu.VMEM((tm, tn) if nk > 1 else (8, LANES), F32)] + ([] if lhs_fn is None else [pltpu.VMEM((tm, tk), BF16)]),
        compiler_params=_params("arbitrary", "arbitrary", "arbitrary"),
    )(*args)
    return res[0] if plain else res


def _proj_rope(a, bt, cos, sin, *, n, b_off, rope_cols, tm, tn, name, out_dtype=F32):
    m, k = a.shape
    assert m % tm == 0 and n % tn == 0 and b_off % tn == 0 and rope_cols % LANES == 0, name
    joff = b_off // tn
    nrope, part = divmod(rope_cols, tn)

    def body(a_ref, b_ref, c_ref, s_ref, o_ref):
        acc = lax.dot_general(a_ref[...], b_ref[...], _DIMS["nt"], preferred_element_type=F32)
        j = pl.program_id(1)

        @pl.when(j < nrope)
        def _():
            o_ref[...] = _rope(acc, c_ref[...], s_ref[...], coarse=True).astype(o_ref.dtype)

        if part:
            @pl.when(j == nrope)
            def _():
                o_ref[:, :part] = _rope(acc[:, :part], c_ref[...], s_ref[...], coarse=True).astype(o_ref.dtype)
                o_ref[:, part:] = acc[:, part:].astype(o_ref.dtype)

        @pl.when(j >= nrope + (1 if part else 0))
        def _():
            o_ref[...] = acc.astype(o_ref.dtype)

    table = pl.BlockSpec((tm, LANES), lambda i, j: (i, 0))
    return pl.pallas_call(
        body,
        name=name,
        grid=(m // tm, n // tn),
        in_specs=[pl.BlockSpec((tm, k), lambda i, j: (i, 0)), pl.BlockSpec((tn, k), lambda i, j: (j + joff, 0)), table, table],
        out_specs=pl.BlockSpec((tm, tn), lambda i, j: (i, j)),
        out_shape=jax.ShapeDtypeStruct((m, n), out_dtype),
        compiler_params=_params("parallel", "parallel"),
    )(a, bt, cos, sin)


ROW_TILE = 256


def _rows(width, col=0):
    return pl.BlockSpec((1, ROW_TILE, width), lambda b, t: (b, t, col))


def _per_batch(nrows, width):
    return pl.BlockSpec((1, nrows, width), lambda b, t: (b, 0, 0))


def _row_call(body, name, bsz, seq, in_specs, out_specs, out_shape, accumulates=False):
    return pl.pallas_call(
        body,
        name=name,
        grid=(bsz, seq // ROW_TILE),
        in_specs=in_specs,
        out_specs=out_specs,
        out_shape=out_shape,
        compiler_params=_params("parallel", "arbitrary" if accumulates else "parallel"),
    )


def _acc_rows(acc_ref, first, rows):
    @pl.when(first)
    def _():
        acc_ref[...] = jnp.zeros_like(acc_ref)

    for r, val in enumerate(rows):
        acc_ref[0, r:r + 1, :] += val


def _colsum(v):
    return jnp.sum(v, axis=0, keepdims=True)


def _ln_stats(z):
    mu = jnp.mean(z, axis=-1, keepdims=True)
    zc = z - mu
    var = jnp.mean(zc * zc, axis=-1, keepdims=True)
    rstd = lax.rsqrt(var + LN_EPS)
    return zc * rstd, rstd


def _ln_bwd(dxhat, xhat, rstd):
    m1 = jnp.mean(dxhat, axis=-1, keepdims=True)
    m2 = jnp.mean(dxhat * xhat, axis=-1, keepdims=True)
    return rstd * (dxhat - m1 - xhat * m2)


def _modulate_in(x, mod):
    bsz, seq, d = x.shape

    def body(x_ref, mod_ref, u_ref):
        u_ref[0] = (x_ref[0] * (1.0 + mod_ref[0, 1:2, :]) + mod_ref[0, 0:1, :]).astype(BF16)

    return _row_call(body, "modulate_in", bsz, seq, [_rows(d), _per_batch(8, d)], _rows(d),
                     jax.ShapeDtypeStruct((bsz, seq, d), BF16))(x, mod)


EP_TILE = 512


def _ep_specs(seq, d):
    tiles = seq // EP_TILE
    return ((EP_TILE, d), lambda i, j: (i, 0)), ((1, 8, d), lambda i, j: (i // tiles, 0, 0)), ((1, d), lambda i, j: (0, 0))


def _wo_ln1(merged, wo, x, mod, g, b, seq):
    ntok, d = x.shape
    row, per_b, whole = _ep_specs(seq, d)

    def epilogue(y, i, j, ins, outs):
        x_ref, mod_ref, g_ref, b_ref = ins
        y_ref, h_ref, u_ref = outs
        z = DEEPNORM_ALPHA * x_ref[...] + (1.0 + mod_ref[0, 2:3, :]) * y
        xhat, _ = _ln_stats(z)
        h = xhat * g_ref[...] + b_ref[...]
        y_ref[...] = y
        h_ref[...] = h
        u_ref[...] = (h * (1.0 + mod_ref[0, 4:5, :]) + mod_ref[0, 3:4, :]).astype(BF16)

    f32, bf16 = jax.ShapeDtypeStruct((ntok, d), F32), jax.ShapeDtypeStruct((ntok, d), BF16)
    return _matmul(merged, wo, mode="nn", tm=EP_TILE, tn=d, tk=d, name="w_o_ln1",
                   ins=[(x,) + row, (mod,) + per_b, (g,) + whole, (b,) + whole],
                   outs=[(f32,) + row, (f32,) + row, (bf16,) + row], epilogue=epilogue)


FF_HALF = D_FF // 2


def _interleave_gate_up(w):
    return w.reshape(2, 2, FF_HALF, w.shape[1]).transpose(1, 0, 2, 3).reshape(w.shape)


def _gate_up_silu(u2, wgut_i):
    ntok = u2.shape[0]

    def epilogue(h, i, j, ins, outs):
        h_ref, a_ref = outs
        hg, hu = h[:, :FF_HALF], h[:, FF_HALF:]
        h_ref[...] = h.astype(BF16)
        a_ref[...] = (hg * _sigmoid(hg) * hu).astype(BF16)

    return _matmul(u2, wgut_i, mode="nt", tm=EP_TILE, tn=2 * FF_HALF, tk=u2.shape[1], name="gate_up_silu",
                   outs=[(jax.ShapeDtypeStruct((ntok, 2 * D_FF), BF16), (EP_TILE, 2 * FF_HALF), lambda i, j: (i, j)),
                         (jax.ShapeDtypeStruct((ntok, D_FF), BF16), (EP_TILE, FF_HALF), lambda i, j: (i, j))],
                   epilogue=epilogue)


def _down_dgrad_silu_bwd(dy2, wd, h_i):
    ntok = dy2.shape[0]
    wide = ((EP_TILE, 2 * FF_HALF), lambda i, j: (i, j))

    def epilogue(da, i, j, ins, outs):
        h = ins[0][...].astype(F32)
        hg, hu = h[:, :FF_HALF], h[:, FF_HALF:]
        sg = _sigmoid(hg)
        outs[0][:, :FF_HALF] = (da * hu * (sg * (1.0 + hg * (1.0 - sg)))).astype(BF16)
        outs[0][:, FF_HALF:] = (da * (hg * sg)).astype(BF16)

    return _matmul(dy2, wd, mode="nt", tm=EP_TILE, tn=FF_HALF, tk=dy2.shape[1], name="down_dgrad_silu_bwd",
                   ins=[(h_i,) + wide], outs=[(jax.ShapeDtypeStruct((ntok, 2 * D_FF), BF16),) + wide], epilogue=epilogue)[0]


def _down_ln2_loss_bwd(a, wd, h1, mod, g, b, target, seq):
    ntok, d = h1.shape
    row, per_b, whole = _ep_specs(seq, d)
    tiles = seq // EP_TILE

    def epilogue(y, i, j, ins, outs):
        h_ref, mod_ref, g_ref, b_ref, t_ref = ins
        dy_ref, dh_ref, acc_ref = outs
        gate = 1.0 + mod_ref[0, 5:6, :]
        z = DEEPNORM_ALPHA * h_ref[...] + gate * y
        xhat, rstd = _ln_stats(z)
        diff = xhat * g_ref[...] + b_ref[...] - t_ref[...]
        loss = 0.5 * jnp.sum(jnp.sum(diff * diff, axis=-1, keepdims=True) / d, axis=0, keepdims=True)
        dout = diff / d
        dz = _ln_bwd(dout * g_ref[...], xhat, rstd)
        dy_ref[...] = (gate * dz).astype(BF16)
        dh_ref[...] = DEEPNORM_ALPHA * dz
        _acc_rows(acc_ref, i % tiles == 0,
                  [_colsum(dout * xhat), _colsum(dout), _colsum(dz * y), jnp.broadcast_to(loss, (1, d))])

    return _matmul(a, wd, mode="nn", tm=EP_TILE, tn=d, tk=a.shape[1], name="down_ln2_loss_bwd",
                   ins=[(h1,) + row, (mod,) + per_b, (g,) + whole, (b,) + whole, (target,) + row],
                   outs=[(jax.ShapeDtypeStruct((ntok, d), BF16),) + row, (jax.ShapeDtypeStruct((ntok, d), F32),) + row,
                         (jax.ShapeDtypeStruct((ntok // seq, 8, d), F32),) + per_b], epilogue=epilogue)


def _gate_up_dgrad_ln1_bwd(dh, wgut, dh1a, x, y1, mod, g, b, seq):
    ntok, d = x.shape
    row, per_b, whole = _ep_specs(seq, d)
    tiles = seq // EP_TILE

    def epilogue(du, i, j, ins, outs):
        dh_ref, x_ref, y_ref, mod_ref, g_ref, b_ref = ins
        dy_ref, dx_ref, acc_ref = outs
        y = y_ref[...]
        gate = 1.0 + mod_ref[0, 2:3, :]
        z = DEEPNORM_ALPHA * x_ref[...] + gate * y
        xhat, rstd = _ln_stats(z)
        h1 = xhat * g_ref[...] + b_ref[...]
        dh1 = dh_ref[...] + du * (1.0 + mod_ref[0, 4:5, :])
        dz = _ln_bwd(dh1 * g_ref[...], xhat, rstd)
        dy_ref[...] = (gate * dz).astype(BF16)
        dx_ref[...] = DEEPNORM_ALPHA * dz
        _acc_rows(acc_ref, i % tiles == 0,
                  [_colsum(dh1 * xhat), _colsum(dh1), _colsum(dz * y), _colsum(du * h1), _colsum(du)])

    return _matmul(dh, wgut, mode="nn", tm=EP_TILE, tn=d, tk=D_FF, name="gate_up_dgrad_ln1_bwd",
                   ins=[(dh1a,) + row, (x,) + row, (y1,) + row, (mod,) + per_b, (g,) + whole, (b,) + whole],
                   outs=[(jax.ShapeDtypeStruct((ntok, d), BF16),) + row, (jax.ShapeDtypeStruct((ntok, d), F32),) + row,
                         (jax.ShapeDtypeStruct((ntok // seq, 8, d), F32),) + per_b], epilogue=epilogue)


def _wo_dgrad_gate_bwd(dy1, wo, gab, ya, yb):
    ntok, d = ya.shape
    tm, tn = 1024, 512
    tile = ((tm, tn), lambda i, j: (i, j))
    tile_b = ((tm, tn), lambda i, j: (i, j + d // tn))

    def epilogue(dm_, i, j, ins, outs):
        ga_ref, gb_ref, ya_ref, yb_ref = ins
        dya_ref, dyb_ref, dga_ref, dgb_ref = outs
        sa, sb = _sigmoid(ga_ref[...].astype(F32)), _sigmoid(gb_ref[...].astype(F32))
        dya_ref[...] = (dm_ * sa).astype(BF16)
        dyb_ref[...] = (dm_ * sb).astype(BF16)
        dga_ref[...] = (dm_ * ya_ref[...].astype(F32) * sa * (1.0 - sa)).astype(BF16)
        dgb_ref[...] = (dm_ * yb_ref[...].astype(F32) * sb * (1.0 - sb)).astype(BF16)

    shp = jax.ShapeDtypeStruct((ntok, d), BF16)
    return _matmul(dy1, wo, mode="nt", tm=tm, tn=tn, tk=d, name="w_o_dgrad_gate_bwd",
                   ins=[(gab,) + tile, (gab,) + tile_b, (ya,) + tile, (yb,) + tile],
                   outs=[(shp,) + tile] * 4, epilogue=epilogue)


def _w_in_dgrad_grad_x(dproj, wint, dxa, x, mod, seq, token):
    ntok, d = x.shape
    row, per_b, _ = _ep_specs(seq, d)
    tiles = seq // EP_TILE

    def epilogue(du, i, j, ins, outs):
        dxa_ref, x_ref, mod_ref = ins
        gx_ref, acc_ref = outs
        gx_ref[...] = dxa_ref[...] + du * (1.0 + mod_ref[0, 1:2, :])
        _acc_rows(acc_ref, i % tiles == 0, [_colsum(du * x_ref[...]), _colsum(du)])

    return _matmul(dproj, wint, mode="nn", tm=EP_TILE, tn=d, tk=wint.shape[0] // 2, name="w_in_dgrad_grad_x", token=token,
                   ins=[(dxa,) + row, (x,) + row, (mod,) + per_b],
                   outs=[(jax.ShapeDtypeStruct((ntok, d), F32),) + row, (jax.ShapeDtypeStruct((ntok // seq, 8, d), F32),) + per_b],
                   epilogue=epilogue)


def _merge_branch_b_gate(os_, ls_, wbbt, gab, ya):
    ntok, d = ya.shape
    w = os_[0].shape[1]
    tm, tn = 1024, 512
    tile = ((tm, tn), lambda i, j: (i, j))
    tile_b = ((tm, tn), lambda i, j: (i, j + d // tn))
    row = ((tm, w), lambda i, j: (i, 0))

    def lhs_fn(o0_ref, ins, outs):
        os_r, ls_r = (o0_ref,) + tuple(ins[3:5]), ins[5:8]
        ls = [l[...] for l in ls_r]
        mx = jnp.maximum(jnp.maximum(ls[0], ls[1]), ls[2])
        es = [jnp.exp(l - mx) for l in ls]
        den = es[0] + es[1] + es[2]
        ob = functools.reduce(jnp.add, [(e / den) * o[...].astype(F32) for e, o in zip(es, os_r)]).astype(BF16)
        outs[2][...] = ob
        return ob

    def epilogue(yb, i, j, ins, outs):
        ga_ref, gb_ref, ya_ref = ins[:3]
        yb_ref, merged_ref = outs[:2]
        yb_ref[...] = yb.astype(BF16)
        merged_ref[...] = (_sigmoid(ga_ref[...].astype(F32)) * ya_ref[...].astype(F32)
                           + _sigmoid(gb_ref[...].astype(F32)) * yb).astype(BF16)

    shp = jax.ShapeDtypeStruct((ntok, d), BF16)
    return _matmul(os_[0], wbbt, mode="nt", tm=tm, tn=tn, tk=w, name="merge_branch_b_gate", lhs_fn=lhs_fn,
                   ins=[(gab,) + tile, (gab,) + tile_b, (ya,) + tile] + [(v,) + row for v in list(os_[1:]) + list(ls_)],
                   outs=[(shp,) + tile] * 2 + [(jax.ShapeDtypeStruct((ntok, w), BF16),) + row], epilogue=epilogue)


def _segsum64(v):
    rows, width = v.shape
    ri = lax.broadcasted_iota(jnp.int32, (LANES, LANES), 0) // HEAD_DIM
    ci = lax.broadcasted_iota(jnp.int32, (LANES, LANES), 1) // HEAD_DIM
    ones = jnp.where(ri == ci, 1.0, 0.0).astype(BF16)
    out = []
    for c in range(width // LANES):
        part = v[:, c * LANES:(c + 1) * LANES]
        hi = part.astype(BF16)
        lo = (part - hi.astype(F32)).astype(BF16)
        out.append(jnp.dot(hi, ones, preferred_element_type=F32) + jnp.dot(lo, ones, preferred_element_type=F32))
    return jnp.concatenate(out, axis=1) if len(out) > 1 else out[0]


def _branch_b_dgrad_merge_bwd(dyb, wbbt, os_, ls_):
    ntok, w = os_[0].shape
    row = ((EP_TILE, w), lambda i, j: (i, 0))

    def epilogue(dob_, i, j, ins, outs):
        os_r, ls_r = ins[:3], ins[3:]
        do_r, dd_r = outs[:3], outs[3:]
        ls = [l[...] for l in ls_r]
        mx = jnp.maximum(jnp.maximum(ls[0], ls[1]), ls[2])
        es = [jnp.exp(l - mx) for l in ls]
        den = es[0] + es[1] + es[2]
        ws = [e / den for e in es]
        dws = [_segsum64(dob_ * o[...].astype(F32)) for o in os_r]
        mean = ws[0] * dws[0] + ws[1] * dws[1] + ws[2] * dws[2]
        for wg, do_ref, dd_ref in zip(ws, do_r, dd_r):
            do_ref[...] = wg * dob_
            dd_ref[...] = -wg * mean

    shp = jax.ShapeDtypeStruct((ntok, w), F32)
    return _matmul(dyb, wbbt, mode="nn", tm=EP_TILE, tn=w, tk=dyb.shape[1], name="branch_b_dgrad_merge_bwd",
                   ins=[(v,) + row for v in list(os_) + list(ls_)], outs=[(shp,) + row] * 6, epilogue=epilogue)


def _branch_a_dgrad_delta(dya, wba, oa, lse_a, sinks_exp, seq):
    ntok, w = oa.shape
    row, per_b, whole = _ep_specs(seq, w)
    tiles = seq // EP_TILE

    def epilogue(do_, i, j, ins, outs):
        o_ref, l_ref, s_ref = ins
        do_ref, dd_ref, acc_ref = outs
        dd = -_segsum64(do_ * o_ref[...].astype(F32))
        do_ref[...] = do_.astype(BF16)
        dd_ref[...] = dd
        _acc_rows(acc_ref, i % tiles == 0, [_colsum(dd * jnp.exp(s_ref[...] - l_ref[...]))])

    shp = jax.ShapeDtypeStruct((ntok, w), F32)
    return _matmul(dya, wba, mode="nt", tm=EP_TILE, tn=w, tk=dya.shape[1], name="branch_a_dgrad_delta",
                   ins=[(oa,) + row, (lse_a,) + row, (sinks_exp,) + whole],
                   outs=[(jax.ShapeDtypeStruct((ntok, w), BF16),) + row, (shp,) + row,
                         (jax.ShapeDtypeStruct((ntok // seq, 8, w), F32),) + per_b],
                   epilogue=epilogue)


def _swap_halves(v):
    src = lax.broadcasted_iota(jnp.int32, (LANES, LANES), 0)
    dst = lax.broadcasted_iota(jnp.int32, (LANES, LANES), 1)
    partner = jnp.where((dst % HEAD_DIM) < HEAD_DIM // 2, dst + HEAD_DIM // 2, dst - HEAD_DIM // 2)
    perm = jnp.where(src == partner, 1.0, 0.0).astype(BF16)
    hi = v.astype(BF16)
    lo = (v - hi.astype(F32)).astype(BF16)
    return jnp.dot(hi, perm, preferred_element_type=F32) + jnp.dot(lo, perm, preferred_element_type=F32)


def _swap_halves_roll(v):
    lane = lax.broadcasted_iota(jnp.int32, v.shape, 1)
    return jnp.where((lane % HEAD_DIM) < HEAD_DIM // 2, pltpu.roll(v, LANES - HEAD_DIM // 2, 1),
                     pltpu.roll(v, HEAD_DIM // 2, 1))


def _swap_halves_coarse(v):
    src = lax.broadcasted_iota(jnp.int32, (LANES, LANES), 0)
    dst = lax.broadcasted_iota(jnp.int32, (LANES, LANES), 1)
    partner = jnp.where((dst % HEAD_DIM) < HEAD_DIM // 2, dst + HEAD_DIM // 2, dst - HEAD_DIM // 2)
    perm = jnp.where(src == partner, 1.0, 0.0).astype(BF16)
    return jnp.dot(v.astype(BF16), perm, preferred_element_type=F32)


def _rope(v, cos, sin, sign=1.0, mxu=True, coarse=False):
    swap = (_swap_halves_coarse if coarse else _swap_halves) if mxu else _swap_halves_roll
    out = []
    for c in range(v.shape[1] // LANES):
        part = v[:, c * LANES:(c + 1) * LANES]
        out.append(part * cos + sign * (swap(part) * sin))
    return jnp.concatenate(out, axis=1) if len(out) > 1 else out[0]


def _half_mask(shape, half):
    lane = lax.broadcasted_iota(jnp.int32, shape, len(shape) - 1) % LANES
    return (lane < HEAD_DIM) if half == 0 else (lane >= HEAD_DIM)


def _dup_half(v, half):
    return jnp.where(_half_mask(v.shape, half), v, pltpu.roll(v, HEAD_DIM, 1))


def _fold_halves(v):
    return v + pltpu.roll(v, HEAD_DIM, 1)


def _pick_halves(lo_rows, hi_rows):
    return jnp.where(_half_mask(lo_rows.shape, 0), lo_rows, hi_rows)


def _stack_masked(v, pairs):
    parts = []
    for c in pairs:
        pair = v[:, c * LANES:(c + 1) * LANES]
        parts += [jnp.where(_half_mask(pair.shape, half), pair, 0.0) for half in (0, 1)]
    return jnp.concatenate(parts, axis=0)


def _stack_pair_cols(v, pairs):
    return jnp.concatenate([v[:, c * LANES + half * HEAD_DIM:c * LANES + half * HEAD_DIM + 1] for c in pairs for half in (0, 1)],
                           axis=0)


ATTN_UNITS = 16


def _class_rows(r):
    return [pl.ds(0, QBLOCK)] if r == 1 else [pl.ds(rho, QBLOCK, stride=r) for rho in range(r)]


def _band_mask(nrows, nk, blk, n_back, has_prev):
    qi = lax.broadcasted_iota(jnp.int32, (nrows, nk), 0) % QBLOCK
    ki = lax.broadcasted_iota(jnp.int32, (nrows, nk), 1)
    if has_prev:
        dist = qi + QBLOCK - ki
        return (dist >= 0) & (dist <= n_back) & ((ki >= QBLOCK) | (blk > 0))
    dist = qi - ki
    return (dist >= 0) & (dist <= n_back)


def _attn_fwd(q_arr, k_arr, v_arr, *, name, npair, gqa, q_col, k_col, v_col, nchunk, r, n_back, sinks=None):
    bsz, seq, _ = q_arr.shape
    rr = QBLOCK * r
    nblk = seq // rr
    qw = npair * LANES
    kw = LANES if gqa else qw
    has_prev = nblk > 1
    has_sink = sinks is not None
    scale = HEAD_DIM ** -0.5

    def body(*refs):
        refs = list(refs)
        q_ref, kc_ref, vc_ref = refs[:3]
        pos = 3
        if has_prev:
            kp_ref, vp_ref = refs[pos:pos + 2]
            pos += 2
        if has_sink:
            sink_ref = refs[pos]
            pos += 1
        o_ref, lse_ref = refs[pos:pos + 2]
        if r > 1:
            stage_o = refs[pos + 2]
        blk = pl.program_id(2)
        nk = (2 if has_prev else 1) * QBLOCK
        valid = _band_mask(QBLOCK, nk, blk, n_back, has_prev)
        per = npair // 2
        classes = _class_rows(r)
        step = max(1, ATTN_UNITS // (2 * npair))
        for first in range(0, len(classes), step):
            batch = classes[first:first + step]
            units = []
            for ci, rows in enumerate(batch):
                q = q_ref[0, rows, :] * scale
                k, v = kc_ref[0, rows, :], vc_ref[0, rows, :]
                if has_prev:
                    k = jnp.concatenate([kp_ref[0, rows, :], k], axis=0)
                    v = jnp.concatenate([vp_ref[0, rows, :], v], axis=0)
                if gqa:
                    kdup = [_dup_half(k, hk).astype(BF16) for hk in range(2)]
                    vdup = [_dup_half(v, hk) for hk in range(2)]
                for c in range(npair):
                    sl = slice(c * LANES, (c + 1) * LANES)
                    qc = q[:, sl]
                    kc, vc = (kdup[c // per], vdup[c // per]) if gqa else (k[:, sl].astype(BF16), v[:, sl])
                    for half in (0, 1):
                        qm = jnp.where(_half_mask(qc.shape, half), qc, 0.0).astype(BF16)
                        vm = jnp.where(_half_mask(vc.shape, half), vc, 0.0).astype(BF16)
                        s = lax.dot_general(qm, kc, _DIMS["nt"], preferred_element_type=F32)
                        units.append(dict(ci=ci, c=c, half=half, s=s, vm=vm, sk=sink_ref[2 * c + half] if has_sink else None))
            for u in units:
                s = jnp.where(valid, u["s"], NEG_INF)
                m = jnp.max(s, axis=1, keepdims=True)
                if has_sink:
                    m = jnp.maximum(m, u["sk"])
                p = jnp.exp(s - m)
                den = jnp.sum(p, axis=1, keepdims=True)
                if has_sink:
                    den = den + jnp.exp(u["sk"] - m)
                u.update(p=p.astype(BF16), den=den, lse=m + jnp.log(den))
            for u in units:
                u["o"] = jnp.dot(u["p"], u["vm"], preferred_element_type=F32) / u["den"]
            for ci, rows in enumerate(batch):
                outs, lses = [None] * npair, [None] * npair
                for u in units:
                    if u["ci"] != ci:
                        continue
                    c, o = u["c"], u["o"]
                    lse = jnp.broadcast_to(u["lse"], o.shape)
                    outs[c] = o if u["half"] == 0 else outs[c] + o
                    lses[c] = lse if u["half"] == 0 else _pick_halves(lses[c], lse)
                o_new = jnp.concatenate(outs, axis=1) if npair > 1 else outs[0]
                if r > 1:
                    stage_o[rows, :] = o_new
                else:
                    o_ref[0] = o_new.astype(BF16)
                lse_ref[0, rows, :] = jnp.concatenate(lses, axis=1) if npair > 1 else lses[0]
        if r > 1:
            o_ref[0] = stage_o[...].astype(BF16)

    def cur(width, col0):
        return pl.BlockSpec((1, rr, width), lambda b, c, i: (b, i, col0 + c))

    def prev(width, col0):
        return pl.BlockSpec((1, rr, width), lambda b, c, i: (b, jnp.maximum(i - 1, 0), col0 + c))

    in_specs = [cur(qw, q_col), cur(kw, k_col), cur(kw, v_col)]
    args = [q_arr, k_arr, v_arr]
    if has_prev:
        in_specs += [prev(kw, k_col), prev(kw, v_col)]
        args += [k_arr, v_arr]
    if has_sink:
        in_specs.append(pl.BlockSpec(memory_space=pltpu.SMEM))
        args.append(sinks)
    return pl.pallas_call(
        body,
        name=name,
        grid=(bsz, nchunk, nblk),
        in_specs=in_specs,
        out_specs=[pl.BlockSpec((1, rr, qw), lambda b, c, i: (b, i, c))] * 2,
        out_shape=[jax.ShapeDtypeStruct((bsz, seq, nchunk * qw), BF16), jax.ShapeDtypeStruct((bsz, seq, nchunk * qw), F32)],
        scratch_shapes=[pltpu.VMEM((rr, qw), F32)] if r > 1 else [],
        compiler_params=_params("parallel", "parallel", "parallel"),
    )(*args)


def _attn_bwd(q_arr, k_arr, v_arr, cos, sin, do, lse, dd, *, name, npair, gqa, q_col, k_col, v_col, nchunk, r, n_back,
              token=None):
    bsz, seq, _ = q_arr.shape
    rr = QBLOCK * r
    nblk = seq // rr
    qw = npair * LANES
    kw = LANES if gqa else qw
    has_next = nblk > 1
    has_token = token is not None
    staged = r > 1
    scale = HEAD_DIM ** -0.5

    def body(*refs):
        refs = list(refs)
        k_ref, v_ref, c_ref, s_ref = refs[:4]
        tile_refs = [refs[4:8]]
        pos = 8
        if has_next:
            tile_refs.append(refs[pos:pos + 4])
            pos += 4
        if has_token:
            pos += 1
        dq_ref, dk_ref, dv_ref = refs[pos:pos + 3]
        carry_ref, bcast_ref = refs[pos + 3:pos + 5]
        if staged:
            stage_q, stage_k, stage_v = refs[pos + 5:pos + 8]
        blk = pl.program_id(2)
        if has_next:
            @pl.when(blk == 0)
            def _():
                carry_ref[...] = jnp.zeros_like(carry_ref)

        nrows = (npair if gqa else 1) * QBLOCK
        qi = lax.broadcasted_iota(jnp.int32, (nrows, QBLOCK), 0) % QBLOCK
        ki = lax.broadcasted_iota(jnp.int32, (nrows, QBLOCK), 1)
        valids = [qi >= ki, (qi + QBLOCK - ki <= n_back) & (blk + 1 < nblk)]
        per = npair // 2
        ntile = len(tile_refs)
        cat = lambda parts: jnp.concatenate(parts, axis=1) if len(parts) > 1 else parts[0]
        classes = _class_rows(r)
        step = max(1, ATTN_UNITS // (ntile * (2 if gqa else 2 * npair)))
        def stat_cols(stat, slot):
            if gqa:
                return _stack_pair_cols(stat, list(range(slot * per, (slot + 1) * per)))
            col = slot * HEAD_DIM
            return stat[:, col:col + 1]

        nslot = 2 if gqa else 2 * npair
        if has_next:
            @pl.when(blk == 0)
            def _():
                for rows in classes:
                    for which, stat_ref in enumerate(tile_refs[0][2:4]):
                        stat = stat_ref[0, rows, :]
                        for slot in range(nslot):
                            bcast_ref[which, slot, rows if not gqa else slice(None), :] = jnp.broadcast_to(
                                stat_cols(stat, slot), (nrows, LANES))

        for first in range(0, len(classes), step):
            batch = classes[first:first + step]
            units = []
            for ci, rows in enumerate(batch):
                keep = slice(None) if gqa else rows
                tiles = [(q_ref[0, rows, :] * scale, do_ref[0, rows, :], l_ref[0, rows, :], d_ref[0, rows, :])
                         for q_ref, do_ref, l_ref, d_ref in tile_refs]

                def stats(t, slot, keep=keep, tiles=tiles):
                    if has_next and t == 0:
                        return bcast_ref[0, slot, keep, :], bcast_ref[1, slot, keep, :]
                    return tuple(jnp.broadcast_to(stat_cols(tiles[t][2 + w], slot), (nrows, LANES)) for w in range(2))

                k, v = k_ref[0, rows, :], v_ref[0, rows, :]
                if gqa:
                    for hk in range(2):
                        pairs = list(range(hk * per, (hk + 1) * per))
                        kd, vd = _dup_half(k, hk).astype(BF16), _dup_half(v, hk).astype(BF16)
                        for t, (q, do_, l_, d_) in enumerate(tiles):
                            lcol, dcol = stats(t, hk)
                            units.append(dict(ci=ci, t=t, hk=hk, slot=hk, keep=keep, pairs=pairs,
                                              qs=_stack_masked(q, pairs).astype(BF16),
                                              dos=_stack_masked(do_, pairs).astype(BF16), lcol=lcol, dcol=dcol,
                                              kmat=kd, vmat=vd, kdq=kd))
                else:
                    for c in range(npair):
                        sl = slice(c * LANES, (c + 1) * LANES)
                        kc, vcb = k[:, sl], v[:, sl].astype(BF16)
                        kcb = kc.astype(BF16)
                        for t, (q, do_, l_, d_) in enumerate(tiles):
                            for half in (0, 1):
                                hm = _half_mask(kc.shape, half)
                                lcol, dcol = stats(t, 2 * c + half)
                                units.append(dict(ci=ci, t=t, c=c, half=half, slot=2 * c + half, keep=keep,
                                                  qs=jnp.where(hm, q[:, sl], 0.0).astype(BF16),
                                                  dos=jnp.where(hm, do_[:, sl], 0.0).astype(BF16), lcol=lcol, dcol=dcol,
                                                  kmat=kcb, vmat=vcb, kdq=jnp.where(hm, kc, 0.0).astype(BF16)))
            for u in units:
                u["s"] = lax.dot_general(u["qs"], u["kmat"], _DIMS["nt"], preferred_element_type=F32)
                u["dp"] = lax.dot_general(u["dos"], u["vmat"], _DIMS["nt"], preferred_element_type=F32)
            for u in units:
                p = jnp.exp(jnp.where(valids[u["t"]], u["s"], NEG_INF) - u["lcol"])
                u["ds"] = (p * (u["dp"] + u["dcol"])).astype(BF16)
                u["p"] = p.astype(BF16)
            for u in units:
                u["dv"] = lax.dot_general(u["p"], u["dos"], _DIMS["tn"], preferred_element_type=F32)
                u["dk"] = lax.dot_general(u["ds"], u["qs"], _DIMS["tn"], preferred_element_type=F32)
                u["dq"] = jnp.dot(u["ds"], u["kdq"], preferred_element_type=F32) * scale
            for u in units:
                if u["t"] == 1:
                    bcast_ref[0, u["slot"], u["keep"], :] = u["lcol"]
                    bcast_ref[1, u["slot"], u["keep"], :] = u["dcol"]
            for ci, rows in enumerate(batch):
                mine = [u for u in units if u["ci"] == ci]
                dq = [[None] * npair for _ in range(ntile)]
                if gqa:
                    dk_out = dv_out = None
                    for hk in range(2):
                        us = [u for u in mine if u["hk"] == hk]
                        for u in us:
                            for i, c in enumerate(u["pairs"]):
                                dq[u["t"]][c] = _pick_halves(u["dq"][2 * i * QBLOCK:(2 * i + 1) * QBLOCK],
                                                             u["dq"][(2 * i + 1) * QBLOCK:(2 * i + 2) * QBLOCK])
                        dk_h = _fold_halves(functools.reduce(jnp.add, [u["dk"] for u in us]))
                        dv_h = _fold_halves(functools.reduce(jnp.add, [u["dv"] for u in us]))
                        dk_out = dk_h if hk == 0 else _pick_halves(dk_out, dk_h)
                        dv_out = dv_h if hk == 0 else _pick_halves(dv_out, dv_h)
                else:
                    dks, dvs = [], []
                    for c in range(npair):
                        us = [u for u in mine if u["c"] == c]
                        dks.append(functools.reduce(jnp.add, [u["dk"] for u in us]))
                        dvs.append(functools.reduce(jnp.add, [u["dv"] for u in us]))
                        for t in range(ntile):
                            dq[t][c] = functools.reduce(jnp.add, [u["dq"] for u in us if u["t"] == t])
                    dk_out, dv_out = cat(dks), cat(dvs)
                ck, sk_ = c_ref[0, rows, :], s_ref[0, rows, :]
                dk_new = _rope(dk_out, ck, sk_, sign=-1.0, mxu=gqa, coarse=True)
                dq_cur = cat(dq[0])
                if has_next:
                    dq_cur = dq_cur + carry_ref[rows, :]
                    carry_ref[rows, :] = cat(dq[1])
                dq_new = _rope(dq_cur, ck, sk_, sign=-1.0, mxu=gqa, coarse=True)
                if staged:
                    stage_q[rows, :], stage_k[rows, :], stage_v[rows, :] = dq_new, dk_new, dv_out
                else:
                    dq_ref[0], dk_ref[0], dv_ref[0] = dq_new.astype(BF16), dk_new.astype(BF16), dv_out.astype(BF16)
        if staged:
            dq_ref[0], dk_ref[0], dv_ref[0] = stage_q[...].astype(BF16), stage_k[...].astype(BF16), stage_v[...].astype(BF16)

    def at(width, col0, shift):
        return pl.BlockSpec((1, rr, width), lambda b, c, i: (b, jnp.minimum(i + shift, nblk - 1), col0 + c))

    in_specs = [at(kw, k_col, 0), at(kw, v_col, 0), pl.BlockSpec((1, rr, LANES), lambda b, c, i: (b, i, 0)),
                pl.BlockSpec((1, rr, LANES), lambda b, c, i: (b, i, 0))]
    args = [k_arr, v_arr, cos, sin]
    for shift in (0, 1) if has_next else (0,):
        in_specs += [at(qw, q_col, shift), at(qw, 0, shift), at(qw, 0, shift), at(qw, 0, shift)]
        args += [q_arr, do, lse, dd]
    if has_token:
        in_specs.append(pl.BlockSpec(token.shape, lambda b, c, i: (0, 0)))
        args.append(token)
    return pl.pallas_call(
        body,
        name=name,
        grid=(bsz, nchunk, nblk),
        in_specs=in_specs,
        out_specs=[pl.BlockSpec((1, rr, qw), lambda b, c, i: (b, i, c)),
                   pl.BlockSpec((1, rr, kw), lambda b, c, i: (b, i, c)),
                   pl.BlockSpec((1, rr, kw), lambda b, c, i: (b, i, c))],
        out_shape=[jax.ShapeDtypeStruct((bsz, seq, nchunk * qw), BF16),
                   jax.ShapeDtypeStruct((bsz, seq, nchunk * kw), BF16),
                   jax.ShapeDtypeStruct((bsz, seq, nchunk * kw), BF16)],
        scratch_shapes=[pltpu.VMEM((rr, qw) if has_next else (8, LANES), F32),
                        pltpu.VMEM((2, 2 if gqa else 2 * npair, npair * QBLOCK if gqa else rr, LANES) if has_next
                                   else (1, 1, 8, LANES), F32)] +
                       ([pltpu.VMEM((rr, qw), F32), pltpu.VMEM((rr, kw), F32), pltpu.VMEM((rr, kw), F32)] if staged else []),
        compiler_params=_params("parallel", "parallel", "arbitrary"),
    )(*args)


B_CHUNKS = {1: (4, 1), 4: (1, 4), 16: (1, 4)}


def _rope_tables(positions):
    half = HEAD_DIM // 2
    inv = ROPE_THETA ** (-jnp.arange(half, dtype=F32) / half)
    ang = positions.astype(F32)[..., None] * inv
    cos, sin = jnp.cos(ang), jnp.sin(ang)
    return jnp.concatenate([cos] * 4, axis=-1), jnp.concatenate([-sin, sin, -sin, sin], axis=-1)


def _layer_step(x, mod, tables, sinks, ln1_g, ln1_b, ln2_g, ln2_b, target, get_w_in, get_rest, hook):
    bsz, seq, d = x.shape
    ntok = bsz * seq
    flat = lambda v: v.reshape(ntok, v.shape[-1])
    unflat = lambda v: v.reshape(bsz, seq, v.shape[-1])
    cos, sin = tables
    mm = functools.partial(_matmul, tm=1024, tk=1024)
    scalar = lambda tok: 0.0 if tok is None else tok[0, 0]

    u1 = _modulate_in(x, mod)
    u1f = flat(u1)
    wint = get_w_in(u1)
    cosf, sinf = flat(cos), flat(sin)
    proj = functools.partial(_proj_rope, u1f, wint, cosf, sinf, tm=2048)
    qkvb = unflat(proj(n=4608, b_off=OFF_QKVB, rope_cols=3072, tn=256, name="proj_qkvb"))
    b_kws, os_, ls_ = [], [], []
    for g, (window, r) in enumerate(B_PATTERNS):
        npair, nch = B_CHUNKS[r]
        per = B_HEADS_PER_GROUP // (2 * npair)
        nsec = len(B_PATTERNS) * per
        kw_ = dict(npair=npair, gqa=False, q_col=g * per, k_col=nsec + g * per, v_col=2 * nsec + g * per, nchunk=nch, r=r,
                   n_back=window // r)
        b_kws.append(kw_)
        o_g, l_g = _attn_fwd(qkvb, qkvb, qkvb, name=f"attn_b{g}_fwd", **kw_)
        os_.append(o_g)
        ls_.append(l_g)
    tok = hook("projected", os_[-1])
    proj = functools.partial(_proj_rope, u1f, wint, cosf + scalar(tok), sinf, tm=2048)
    gab = unflat(proj(n=2048, b_off=OFF_GAB, rope_cols=0, tn=256, name="proj_gab", out_dtype=BF16))
    qa = kva = unflat(proj(n=OFF_QKVB, b_off=OFF_QA, rope_cols=OFF_KVA + LANES, tn=256, name="proj_qkva", out_dtype=BF16))
    a_kw = dict(npair=A_Q_HEADS // 2, gqa=True, q_col=0, k_col=OFF_KVA // LANES, v_col=OFF_KVA // LANES + 1, nchunk=1, r=1,
                n_back=A_WINDOW - 1)
    after_gab = jnp.minimum(jnp.abs(gab[0, 0, 0].astype(F32)), 0.0)
    oa, lse_a = _attn_fwd(qa, kva, kva, name="attn_a_fwd", sinks=sinks.reshape(A_Q_HEADS) + after_gab, **a_kw)
    rest = get_rest(oa)
    wba, wbbt, wo, wgut, wd = (rest[n] for n in ("w_branch_a", "w_branch_b", "w_o", "w_gate_up", "w_down"))
    ya = unflat(mm(flat(oa), wba, mode="nn", out_dtype=BF16, tn=512, name="branch_a"))
    ybf, mergedf, obf = _merge_branch_b_gate([flat(t) for t in os_], [flat(t) for t in ls_], wbbt, flat(gab), flat(ya))
    xf = flat(x)
    y1f, h1f, u2f = _wo_ln1(mergedf, wo, xf, mod, ln1_g, ln1_b, seq)
    wgut_i = _interleave_gate_up(wgut)
    hf, af = _gate_up_silu(u2f, wgut_i)

    dy2f, dh1af, acc2 = _down_ln2_loss_bwd(af, wd, h1f, mod, ln2_g, ln2_b, flat(target), seq)
    g_wd = _matmul(af, dy2f, mode="tn", out_dtype=BF16, tm=256, tn=1024, tk=ntok, name="down_wgrad")
    dhf = _down_dgrad_silu_bwd(dy2f, wd, hf)
    g_wgut = _interleave_gate_up(_matmul(dhf, u2f, mode="tn", out_dtype=BF16, tm=256, tn=1024, tk=ntok, name="gate_up_wgrad"))
    dy1f, dxaf, acc1 = _gate_up_dgrad_ln1_bwd(dhf, wgut_i, dh1af, xf, y1f, mod, ln1_g, ln1_b, seq)
    g_wo = _matmul(mergedf, dy1f, mode="tn", out_dtype=BF16, tm=256, tn=1024, tk=ntok, name="w_o_wgrad")
    dyaf, dybf, dgaf, dgbf = _wo_dgrad_gate_bwd(dy1f, wo, flat(gab), flat(ya), ybf)
    g_wba = _matmul(flat(oa), dyaf, mode="tn", out_dtype=BF16, tm=256, tn=1024, tk=ntok, name="branch_a_wgrad")
    g_wbbt = _matmul(dybf, obf, mode="tn", out_dtype=BF16, tm=256, tn=512, tk=ntok, name="branch_b_wgrad")
    tok = hook("grads_rest", dict(w_branch_a=g_wba, w_branch_b=g_wbbt, w_o=g_wo, w_gate_up=g_wgut, w_down=g_wd))

    sinks_exp = jnp.repeat(sinks.reshape(1, A_Q_HEADS), HEAD_DIM, axis=1) + scalar(tok)
    doa, dd_a, acc_s = _branch_a_dgrad_delta(dyaf, wba, flat(oa), flat(lse_a), sinks_exp, seq)
    doa, dd_a = unflat(doa), unflat(dd_a)
    tok = hook("delta_done", dd_a)
    dqa, dka, dva = _attn_bwd(qa, kva, kva, cos, sin, doa, lse_a, dd_a, name="attn_a_bwd", token=tok, **a_kw)
    merged_bwd = [unflat(t) for t in _branch_b_dgrad_merge_bwd(dybf, wbbt, [flat(t) for t in os_], [flat(t) for t in ls_])]
    dqs, dks, dvs = [], [], []
    for g in range(len(B_PATTERNS)):
        dq_g, dk_g, dv_g = _attn_bwd(qkvb, qkvb, qkvb, cos, sin, merged_bwd[g], ls_[g], merged_bwd[3 + g],
                                     name=f"attn_b{g}_bwd", **b_kws[g])
        dqs.append(dq_g)
        dks.append(dk_g)
        dvs.append(dv_g)
    dproj = jnp.concatenate([t.astype(BF16) for t in [dqa, dka, dva] + dqs + dks + dvs] + [unflat(dgaf), unflat(dgbf)], axis=-1)
    dprojf = flat(dproj)
    g_wint = _matmul(dprojf, u1f, mode="tn", out_dtype=BF16, tm=256, tn=1024, tk=ntok, name="w_in_wgrad")
    tok = hook("grads_w_in", dict(w_in=g_wint))
    grad_x, acc0 = _w_in_dgrad_grad_x(dprojf, wint, dxaf, xf, mod, seq, tok)
    grad_x = unflat(grad_x)
    tok = hook("dgrad_done", grad_x)

    loss_part = jnp.sum(acc2[:, 3, 0])
    dmod = jnp.stack([acc0[:, 1], acc0[:, 0], acc1[:, 2], acc1[:, 4], acc1[:, 3], acc2[:, 2]], axis=1)
    small = jnp.stack([acc1[:, 0].sum(0), acc1[:, 1].sum(0), acc2[:, 0].sum(0), acc2[:, 1].sum(0), acc_s[:, 0].sum(0)])
    small = small + scalar(tok)
    return loss_part, grad_x, dmod, small


CHIP_FLIPS = (2, 4, 6)


def _my_place():
    return lax.axis_index("x"), lax.axis_index("y"), lax.axis_index("c")


def _flip(place, k):
    px, py, pc = place
    return (1 - px if k & 4 else px, 1 - py if k & 2 else py, 1 - pc if k & 1 else pc)


def _index(place):
    return 4 * place[0] + 2 * place[1] + place[2]


def _gather_small(v, name):
    rows, cols = v.shape

    def body(v_ref, out_ref, send_sems, recv_sems):
        me = _my_place()
        out_ref[_index(me)] = v_ref[...]
        copies = []
        for k in range(1, N_DEV):
            copies.append(pltpu.make_async_remote_copy(
                src_ref=v_ref, dst_ref=out_ref.at[_index(me)], send_sem=send_sems.at[k - 1], recv_sem=recv_sems.at[k - 1],
                device_id=_flip(me, k), device_id_type=MESH))
        for cp in copies:
            cp.start()
        for k in range(1, N_DEV):
            pltpu.make_async_remote_copy(
                src_ref=v_ref, dst_ref=out_ref.at[_index(_flip(me, k))], send_sem=send_sems.at[k - 1],
                recv_sem=recv_sems.at[k - 1], device_id=_flip(me, k), device_id_type=MESH).wait_recv()
        for cp in copies:
            cp.wait_send()

    return pl.pallas_call(
        body,
        name=name,
        out_shape=jax.ShapeDtypeStruct((N_DEV, rows, cols), v.dtype),
        in_specs=[pl.BlockSpec(memory_space=pltpu.VMEM)],
        out_specs=pl.BlockSpec(memory_space=pltpu.VMEM),
        scratch_shapes=[pltpu.SemaphoreType.DMA((N_DEV - 1,)), pltpu.SemaphoreType.DMA((N_DEV - 1,))],
        compiler_params=pltpu.CompilerParams(vmem_limit_bytes=VMEM_LIMIT_BYTES),
    )(v)


_HBM = pl.BlockSpec(memory_space=pltpu.HBM)
_SEM = pl.BlockSpec(memory_space=pltpu.SEMAPHORE)
_EFFECT = pltpu.SideEffectType.DATAFLOW_SIDE_EFFECTING


def _remote(src, dst, send_sems, recv_sems, j, to):
    return pltpu.make_async_remote_copy(src_ref=src, dst_ref=dst, send_sem=send_sems.at[j], recv_sem=recv_sems.at[j],
                                        device_id=to, device_id_type=MESH)


def _copies_start(name, bufs, make_copies, nsem):
    nbuf = len(bufs)

    def body(*refs):
        for cp in make_copies(refs[:nbuf], refs[nbuf], refs[nbuf + 1]):
            cp.start()
        refs[-1][...] = jnp.zeros_like(refs[-1])

    sems = pltpu.SemaphoreType.DMA((nsem,))
    res = pl.pallas_call(
        body, name=name,
        out_shape=(sems, sems, *[pltpu.HBM(v.shape, v.dtype) for v in bufs], jax.ShapeDtypeStruct((8, LANES), F32)),
        in_specs=(_HBM,) * nbuf, out_specs=(_SEM, _SEM) + (_HBM,) * nbuf + (pl.BlockSpec(memory_space=pltpu.VMEM),),
        input_output_aliases={i: 2 + i for i in range(nbuf)},
        compiler_params=pltpu.CompilerParams(has_side_effects=_EFFECT),
    )(*[pltpu.with_memory_space_constraint(v, pltpu.HBM) for v in bufs])
    return res[0], res[1], list(res[2:2 + nbuf]), res[-1]


def _copies_wait(name, started, make_copies, after):
    send_sems, recv_sems, bufs, _ = started
    nbuf = len(bufs)

    def body(*refs):
        for cp in make_copies(refs[:nbuf], refs[nbuf], refs[nbuf + 1]):
            cp.wait_send()
            cp.wait_recv()

    return list(pl.pallas_call(
        body, name=name,
        out_shape=tuple(pltpu.HBM(v.shape, v.dtype) for v in bufs),
        in_specs=(_HBM,) * nbuf + (_SEM, _SEM, pl.BlockSpec(memory_space=pl.ANY)), out_specs=(_HBM,) * nbuf,
        input_output_aliases={i: i for i in range(nbuf)},
        compiler_params=pltpu.CompilerParams(has_side_effects=_EFFECT),
    )(*bufs, send_sems, recv_sems, after))


def _to_sibling_copies(refs, send_sems, recv_sems):
    src_ref, land_ref = refs
    me = _my_place()
    return [_remote(src_ref.at[q, 1 - me[2]], land_ref.at[q], send_sems, recv_sems, q, _flip(me, 1)) for q in range(4)]


def _to_chips_copies(refs, send_sems, recv_sems):
    src_ref, land_ref = refs
    me = _my_place()
    copies = []
    for j, k in enumerate(CHIP_FLIPS):
        to = _flip(me, k)
        copies.append(_remote(src_ref.at[2 * to[0] + to[1]], land_ref.at[j], send_sems, recv_sems, j, to))
    return copies


class _Gather:
    def __init__(self, name, blocks):
        self.name, self.n = name, len(blocks)
        at_me = (_index(_my_place()), 0, 0)
        lands = [lax.dynamic_update_slice(lax.empty((N_DEV,) + v.shape, v.dtype), v[None], at_me) for v in blocks]
        self.first = _copies_start(name + "_start", list(blocks) + lands, self._first_copies, 4 * self.n)
        self.token = self.first[3]

    def _first_copies(self, refs, send_sems, recv_sems):
        me = _my_place()
        return [_remote(refs[w], refs[self.n + w].at[_index(me)], send_sems, recv_sems, 4 * w + j, _flip(me, k))
                for w in range(self.n) for j, k in enumerate((1,) + CHIP_FLIPS)]

    def _pass_copies(self, refs, send_sems, recv_sems):
        me = _my_place()
        copies = []
        for w, land in enumerate(refs):
            for j, k in enumerate(CHIP_FLIPS):
                slot = land.at[_index(_flip(me, k))]
                copies.append(_remote(slot, slot, send_sems, recv_sems, 3 * w + j, _flip(me, 1)))
        return copies

    def pass_on(self, after):
        lands = _copies_wait(self.name + "_wait", self.first, self._first_copies, after)[self.n:]
        self.second = _copies_start(self.name + "_pass_start", lands, self._pass_copies, 3 * self.n)
        return self.second[3]

    def finish(self, after):
        return _copies_wait(self.name + "_pass_wait", self.second, self._pass_copies, after)


def _to_all_copies(refs, send_sems, recv_sems):
    src_ref, land_ref = refs
    me = _my_place()
    return [_remote(src_ref, land_ref.at[_index(me)], send_sems, recv_sems, k - 1, _flip(me, k)) for k in range(1, N_DEV)]


SUM_SPLIT = 2


def _sum_pairs(parts, theirs):
    nchip, _, rows, cols = parts.shape
    tile = rows // SUM_SPLIT

    def body(c_ref, a_ref, b_ref, o_ref):
        o_ref[...] = (a_ref[0].astype(F32) + b_ref[...].astype(F32)).astype(BF16)

    spec = pl.BlockSpec((1, tile, cols), lambda q, t, c_ref: (q, t, 0))
    grid_spec = pltpu.PrefetchScalarGridSpec(
        num_scalar_prefetch=1, grid=(nchip, SUM_SPLIT),
        in_specs=[pl.BlockSpec((1, 1, tile, cols), lambda q, t, c_ref: (q, c_ref[0], t, 0)), spec], out_specs=spec)
    return pl.pallas_call(body, name="grad_sum_sibling", grid_spec=grid_spec,
                          out_shape=jax.ShapeDtypeStruct((nchip, rows, cols), BF16),
                          compiler_params=_params("parallel", "parallel"))(lax.axis_index("c").reshape(1), parts, theirs)


def _sum_final(chip_sum, got):
    _, rows, cols = chip_sum.shape
    tile = rows // SUM_SPLIT

    def body(q_ref, a_ref, g_ref, o_ref):
        o_ref[...] = ((a_ref[0].astype(F32) + g_ref[0].astype(F32)) + g_ref[1].astype(F32)) + g_ref[2].astype(F32)

    grid_spec = pltpu.PrefetchScalarGridSpec(
        num_scalar_prefetch=1, grid=(SUM_SPLIT,),
        in_specs=[pl.BlockSpec((1, tile, cols), lambda t, q_ref: (q_ref[0], t, 0)),
                  pl.BlockSpec((3, tile, cols), lambda t, q_ref: (0, t, 0))],
        out_specs=pl.BlockSpec((tile, cols), lambda t, q_ref: (t, 0)))
    my_chip = (2 * lax.axis_index("x") + lax.axis_index("y")).reshape(1)
    return pl.pallas_call(body, name="grad_sum_chips", grid_spec=grid_spec, out_shape=jax.ShapeDtypeStruct((rows, cols), F32),
                          compiler_params=_params("parallel"))(my_chip, chip_sum, got)


class _ReduceScatter:
    def __init__(self, name, slabs):
        self.name, self.rows = name, slabs.shape[1]
        parts = slabs.reshape(4, 2, self.rows, D_MODEL)
        self.first = _copies_start(name + "_sibling_start", [parts, lax.empty((4, self.rows, D_MODEL), slabs.dtype)],
                                   _to_sibling_copies, 4)
        self.token = self.first[3]

    def between_chips(self, after):
        parts, theirs = _copies_wait(self.name + "_sibling_wait", self.first, _to_sibling_copies, after)
        chip_sum = _sum_pairs(parts, theirs)
        self.second = _copies_start(self.name + "_chips_start", [chip_sum, lax.empty((3, self.rows, D_MODEL), chip_sum.dtype)],
                                    _to_chips_copies, 3)
        return self.second[3]

    def finish(self, after):
        chip_sum, got = _copies_wait(self.name + "_chips_wait", self.second, _to_chips_copies, after)
        return _sum_final(chip_sum, got)


def _ada_fwd(c_all, w, b):
    nb, _ = c_all.shape
    ncol = w.shape[1]

    def body(c_ref, w_ref, b_ref, o_ref):
        c = c_ref[...]
        act = (c * _sigmoid(c)).astype(BF16)
        o_ref[...] = jnp.dot(act, w_ref[...].astype(BF16), preferred_element_type=F32) + b_ref[...]

    return pl.pallas_call(body, name="ada_fwd", out_shape=jax.ShapeDtypeStruct((nb, ncol), F32),
                          compiler_params=pltpu.CompilerParams(vmem_limit_bytes=VMEM_LIMIT_BYTES))(c_all, w, b)


def _ada_wgrad(c_all_t, dmod_cols):
    d, nb = c_all_t.shape
    ncol = dmod_cols.shape[1]

    def body(ct_ref, dm_ref, o_ref):
        ct = ct_ref[...]
        act = (ct * _sigmoid(ct)).astype(BF16).astype(F32)
        dm = dm_ref[...].astype(BF16).astype(F32)
        acc = act[:, 0:1] * dm[0:1, :]
        for i in range(1, nb):
            acc = acc + act[:, i:i + 1] * dm[i:i + 1, :]
        o_ref[...] = acc

    return pl.pallas_call(body, name="ada_wgrad", out_shape=jax.ShapeDtypeStruct((d, ncol), F32),
                          compiler_params=pltpu.CompilerParams(vmem_limit_bytes=VMEM_LIMIT_BYTES))(c_all_t, dmod_cols)


SMALL_ROWS = 24


def _reduce_small(gathered):
    def body(g_ref, o_ref):
        acc = g_ref[0]
        for dev in range(1, N_DEV):
            acc = acc + g_ref[dev]
        o_ref[...] = acc

    return pl.pallas_call(body, name="reduce_small", out_shape=jax.ShapeDtypeStruct(gathered.shape[1:], F32))(gathered)


def _adamw_math(w, g, m, v):
    nm = ADAM_B1 * m + (1.0 - ADAM_B1) * g
    nv = ADAM_B2 * v + (1.0 - ADAM_B2) * (g * g)
    bc1 = 1.0 - ADAM_B1 ** ADAM_STEP
    bc2 = 1.0 - ADAM_B2 ** ADAM_STEP
    return -ADAM_LR * ((nm / bc1) / (jnp.sqrt(nv / bc2) + ADAM_EPS) + ADAM_WD * w), nm, nv


def _adamw_small(ws, gs, ms, vs, name):
    n = len(ws)

    def body(*refs):
        for i in range(n):
            res = _adamw_math(*(refs[k * n + i][...] for k in range(4)))
            for k in range(3):
                refs[(4 + k) * n + i][...] = res[k]

    shapes = [jax.ShapeDtypeStruct(w.shape, F32) for w in ws]
    res = pl.pallas_call(body, name=name, out_shape=shapes * 3)(*ws, *gs, *ms, *vs)
    return [(res[i], res[n + i], res[2 * n + i]) for i in range(n)]


def _adamw(w, g, m, v, name):
    rows, cols = w.shape
    tile = rows
    for cand in range(min(rows // 2, 512) // 8 * 8, 7, -8):
        if rows % cand == 0:
            tile = cand
            break
    spec = pl.BlockSpec((tile, cols), lambda t: (t, 0))

    def body(w_ref, g_ref, m_ref, v_ref, d_ref, nm_ref, nv_ref):
        d_ref[...], nm_ref[...], nv_ref[...] = _adamw_math(w_ref[...], g_ref[...], m_ref[...], v_ref[...])

    shp = jax.ShapeDtypeStruct((rows, cols), F32)
    return pl.pallas_call(body, name=name, grid=(rows // tile,), in_specs=[spec] * 4, out_specs=[spec] * 3, out_shape=[shp] * 3,
                          compiler_params=_params("parallel"))(w, g, m, v)


_WEIGHTS = ("w_ada", "b_ada", "w_in", "sinks", "w_branch_a", "w_branch_b", "w_o", "ln1_g", "ln1_b", "w_gate_up", "w_down",
            "ln2_g", "ln2_b")
_TRANSPOSED = ("w_in", "w_branch_b", "w_gate_up")


def _pack_shard(name, w):
    w = w.astype(BF16)
    if name in _TRANSPOSED:
        w = w.T
    return w.reshape(-1, D_MODEL)


def _unpack_full(name, slab):
    if name == "w_branch_b":
        return slab.reshape(N_DEV * 128, 512)
    return slab.reshape(-1, D_MODEL)


def _unpack_group(group, gathered):
    return {n: _unpack_full(n, slab) for (n, _), slab in zip(group, gathered)}


def _unpack_grads(group, g_packed):
    g_w, off = {}, 0
    for n, r in group:
        part = g_packed[off:off + r]
        off += r
        g_w[n] = part.reshape(128, 512) if n == "w_branch_b" else part
    return g_w


def kernel(x, c, positions, w_ada, b_ada, w_in, sinks, w_branch_a, w_branch_b, w_o, ln1_g, ln1_b, w_gate_up, w_down, ln2_g, ln2_b, loss_target, m_w_ada, m_b_ada, m_w_in, m_sinks, m_w_branch_a, m_w_branch_b, m_w_o, m_ln1_g, m_ln1_b, m_w_gate_up, m_w_down, m_ln2_g, m_ln2_b, v_w_ada, v_b_ada, v_w_in, v_sinks, v_w_branch_a, v_w_branch_b, v_w_o, v_ln1_g, v_ln1_b, v_w_gate_up, v_w_down, v_ln2_g, v_ln2_b):
    weights = dict(w_ada=w_ada, b_ada=b_ada, w_in=w_in, sinks=sinks, w_branch_a=w_branch_a, w_branch_b=w_branch_b, w_o=w_o,
                   ln1_g=ln1_g, ln1_b=ln1_b, w_gate_up=w_gate_up, w_down=w_down, ln2_g=ln2_g, ln2_b=ln2_b)
    m_in = dict(w_ada=m_w_ada, b_ada=m_b_ada, w_in=m_w_in, sinks=m_sinks, w_branch_a=m_w_branch_a, w_branch_b=m_w_branch_b,
                w_o=m_w_o, ln1_g=m_ln1_g, ln1_b=m_ln1_b, w_gate_up=m_w_gate_up, w_down=m_w_down, ln2_g=m_ln2_g, ln2_b=m_ln2_b)
    v_in = dict(w_ada=v_w_ada, b_ada=v_b_ada, w_in=v_w_in, sinks=v_sinks, w_branch_a=v_w_branch_a, w_branch_b=v_w_branch_b,
                w_o=v_w_o, ln1_g=v_ln1_g, ln1_b=v_ln1_b, w_gate_up=v_w_gate_up, w_down=v_w_down, ln2_g=v_ln2_g, ln2_b=v_ln2_b)
    bsz = x.shape[0]
    me = _index(_my_place())
    ada_cols = w_ada.shape[2]
    outs = {}

    def adamw(n, g):
        w2, m2, v2 = (t[n][0] if t[n].ndim == 3 else t[n] for t in (weights, m_in, v_in))
        shape = weights[n].shape
        if n in _TRANSPOSED:
            dlt, nm, nv = _adamw(w2.T, g, m2.T, v2.T, "adamw_" + n)
            outs[n] = tuple(t.T.reshape(shape) for t in (g, dlt, nm, nv))
        else:
            dlt, nm, nv = _adamw(w2, g, m2, v2, "adamw_" + n)
            outs[n] = tuple(t.reshape(shape) for t in (g, dlt, nm, nv))
        return nv

    packed_in = [_pack_shard(n, weights[n][0]) for n, _ in GROUP_IN]
    packed_rest = [_pack_shard(n, weights[n][0]) for n, _ in GROUP_REST]
    c_all = _gather_small(jnp.pad(c, ((0, 8 - bsz), (0, 0))), "gather_c")[:, :bsz].reshape(N_DEV * bsz, D_MODEL)
    gather_in = _Gather("gather_w_in", lax.optimization_barrier((packed_in, c_all))[0])
    b_cols = lax.dynamic_slice_in_dim(b_ada, me * ada_cols, ada_cols, axis=1)
    mod_cols = _ada_fwd(c_all, w_ada[0], b_cols + gather_in.token[0, 0])
    tables = _rope_tables(positions)
    mod_cols, tables, packed_rest = lax.optimization_barrier((mod_cols, tables, packed_rest))
    mod_all = _gather_small(mod_cols, "gather_mod").transpose(1, 0, 2).reshape(N_DEV * bsz, 6, D_MODEL)
    gather_rest = _Gather("gather_rest", lax.optimization_barrier((packed_rest, mod_all))[0])
    mod = jnp.pad(lax.dynamic_slice_in_dim(mod_all, me * bsz, bsz, axis=0), ((0, 0), (0, 2), (0, 0)))
    mod = mod + gather_rest.token[0, 0]
    mod = mod + gather_in.pass_on(mod)[0, 0]

    scatters, rest_grads = {}, {}

    def get_w_in(after):
        return _unpack_group(GROUP_IN, gather_in.finish(after))["w_in"]

    def get_rest(after):
        return _unpack_group(GROUP_REST, gather_rest.finish(after))

    def pack_grads(group, grads):
        return jnp.concatenate([grads[n].reshape(N_DEV, r, D_MODEL) for n, r in group], axis=1)

    def hook(point, value):
        if point == "projected":
            return gather_rest.pass_on(value)
        if point == "grads_rest":
            scatters["rest"] = _ReduceScatter("scatter_rest", pack_grads(GROUP_REST, value))
            return scatters["rest"].token
        if point == "delta_done":
            return scatters["rest"].between_chips(value)
        if point == "grads_w_in":
            scatters["in"] = _ReduceScatter("scatter_w_in", pack_grads(GROUP_IN, value))
            rest_grads.update(_unpack_grads(GROUP_REST, scatters["rest"].finish(scatters["in"].token)))
            return scatters["in"].between_chips(lax.optimization_barrier(tuple(rest_grads.values()))[0])
        if point == "dgrad_done":
            return None
        raise ValueError(point)

    loss_part, grad_x, dmod, small = _layer_step(x, mod, tables, sinks[0], ln1_g, ln1_b, ln2_g, ln2_b, loss_target,
                                                 get_w_in, get_rest, hook)

    rows = jnp.concatenate([dmod.reshape(bsz * 6, D_MODEL), small, jnp.full((1, D_MODEL), loss_part, F32),
                            jnp.zeros((SMALL_ROWS - bsz * 6 - 6, D_MODEL), F32)], axis=0)
    land = lax.dynamic_update_slice(lax.empty((N_DEV,) + rows.shape, rows.dtype), rows[None], (me, 0, 0))
    gather_small = _copies_start("gather_small_start", [rows, land], _to_all_copies, N_DEV - 1)
    grads, _ = lax.optimization_barrier((rest_grads, gather_small[3]))
    updated = lax.optimization_barrier(tuple(adamw(n, g) for n, g in grads.items()))
    small_all = _copies_wait("gather_small_wait", gather_small, _to_all_copies, updated[0])[1]
    sums = _reduce_small(small_all)
    loss = sums[bsz * 6 + 5, 0]
    dmod_all = small_all[:, :bsz * 6].reshape(N_DEV * bsz, 6 * D_MODEL)
    small_g = {"b_ada": functools.reduce(jnp.add, [sums[6 * i:6 * i + 6] for i in range(bsz)]).reshape(1, 6 * D_MODEL),
               "sinks": sums[bsz * 6 + 4][::HEAD_DIM][None]}
    small_g.update({n: sums[bsz * 6 + i][None] for i, n in enumerate(("ln1_g", "ln1_b", "ln2_g", "ln2_b"))})
    names = list(small_g)
    for n, (dlt, nm, nv) in zip(names, _adamw_small([weights[n] for n in names], [small_g[n] for n in names],
                                                     [m_in[n] for n in names], [v_in[n] for n in names], "adamw_small")):
        outs[n] = (small_g[n], dlt, nm, nv)
    dmod_cols = lax.dynamic_slice_in_dim(dmod_all, me * ada_cols, ada_cols, axis=1)
    adamw("w_ada", _ada_wgrad(c_all.T, dmod_cols))
    done = lax.optimization_barrier(tuple(outs[n][3] for n in outs))
    for n, g in _unpack_grads(GROUP_IN, scatters["in"].finish(done[0])).items():
        adamw(n, g)

    return (loss, grad_x, *[outs[n][0] for n in _WEIGHTS], *[outs[n][1] for n in _WEIGHTS], *[outs[n][2] for n in _WEIGHTS],
            *[outs[n][3] for n in _WEIGHTS])
```

```python
import functools

import jax
import jax.numpy as jnp
from jax import lax
from jax.experimental import pallas as pl
from jax.experimental.pallas import tpu as pltpu

F32 = jnp.float32
BF16 = jnp.bfloat16

D_MODEL = 1024
HEAD_DIM = 64
A_Q_HEADS = 16
A_WINDOW = 128
B_PATTERNS = ((128, 1), (512, 4), (2048, 16))
B_HEADS_PER_GROUP = 8
D_FF = 2816
QBLOCK = 128
ROPE_THETA = 10000.0
LN_EPS = 1e-5
DEEPNORM_ALPHA = 2.0 ** 0.25
NEG_INF = -1e30
ADAM_LR, ADAM_B1, ADAM_B2, ADAM_EPS, ADAM_WD, ADAM_STEP = 0.001, 0.9, 0.999, 1e-08, 0.01, 10

N_DEV = 8
LANES = 128
VMEM_LIMIT_BYTES = 56 * 1024 * 1024
MESH = pl.DeviceIdType.MESH

OFF_QA, OFF_KVA, OFF_QKVB, OFF_GAB = 0, 1024, 1280, 5888
GROUP_IN = (("w_in", 992),)
GROUP_REST = (("w_branch_a", 128), ("w_branch_b", 64), ("w_o", 128), ("w_gate_up", 704), ("w_down", 352))


def _params(*sem):
    return pltpu.CompilerParams(dimension_semantics=sem, vmem_limit_bytes=VMEM_LIMIT_BYTES)


def _sigmoid(x):
    return 1.0 / (1.0 + jnp.exp(-x))


_DIMS = {"nn": (((1,), (0,)), ((), ())), "nt": (((1,), (1,)), ((), ())), "tn": (((0,), (0,)), ((), ()))}


def _matmul(a, b, *, mode, tm, tn, tk, name, out_dtype=None, n=None, b_off=0, token=None, ins=(), outs=None, epilogue=None,
            lhs_fn=None):
    if mode == "nn":
        (m, k), nn_ = a.shape, b.shape[1]
    elif mode == "nt":
        (m, k), nn_ = a.shape, (b.shape[0] if n is None else n)
    else:
        (k, m), nn_ = a.shape, b.shape[1]
    assert m % tm == 0 and nn_ % tn == 0 and k % tk == 0 and b_off % tn == 0, (name, m, nn_, k)
    nk = k // tk
    joff = b_off // tn
    if mode == "nn":
        a_spec = pl.BlockSpec((tm, tk), lambda i, j, kk: (i, kk))
        b_spec = pl.BlockSpec((tk, tn), lambda i, j, kk: (kk, j))
    elif mode == "nt":
        a_spec = pl.BlockSpec((tm, tk), lambda i, j, kk: (i, kk))
        b_spec = pl.BlockSpec((tn, tk), lambda i, j, kk: (j + joff, kk))
    else:
        a_spec = pl.BlockSpec((tk, tm), lambda i, j, kk: (kk, i))
        b_spec = pl.BlockSpec((tk, tn), lambda i, j, kk: (kk, j))
    dims = _DIMS[mode]
    has_token = token is not None
    plain = epilogue is None
    if plain:
        outs = [(jax.ShapeDtypeStruct((m, nn_), out_dtype), (tm, tn), lambda i, j: (i, j))]

        def epilogue(acc, i, j, in_refs, out_refs):
            out_refs[0][...] = acc.astype(out_refs[0].dtype)

    nin = len(ins)
    nscratch = 1 if lhs_fn is None else 2
    assert lhs_fn is None or nk == 1

    def body(*refs):
        a_ref, b_ref = refs[:2]
        in_refs = refs[2:2 + nin]
        out_refs = refs[2 + nin + has_token:-nscratch]
        acc_ref = refs[-nscratch]
        kk = pl.program_id(2)
        if lhs_fn is None:
            lhs = a_ref[...].astype(BF16)
        else:
            lhs_ref = refs[-1]

            @pl.when(pl.program_id(1) == 0)
            def _():
                lhs_ref[...] = lhs_fn(a_ref, in_refs, out_refs)

            lhs = lhs_ref[...]
        part = lax.dot_general(lhs, b_ref[...].astype(BF16), dims, preferred_element_type=F32)

        def finish(acc):
            epilogue(acc, pl.program_id(0), pl.program_id(1), in_refs, out_refs)

        if nk == 1:
            finish(part)
        else:
            @pl.when(kk == 0)
            def _():
                acc_ref[...] = part

            @pl.when(kk > 0)
            def _():
                acc_ref[...] += part

            @pl.when(kk == nk - 1)
            def _():
                finish(acc_ref[...])

    def spec(block, index):
        return pl.BlockSpec(block, lambda i, j, kk: index(i, j))

    in_specs, args = [a_spec, b_spec], [a, b]
    for arr, block, index in ins:
        in_specs.append(spec(block, index))
        args.append(arr)
    if has_token:
        in_specs.append(pl.BlockSpec(token.shape, lambda i, j, kk: (0, 0)))
        args.append(token)
    res = pl.pallas_call(
        body,
        name=name,
        grid=(m // tm, nn_ // tn, nk),
        in_specs=in_specs,
        out_specs=[spec(block, index) for _, block, index in outs],
        out_shape=[shape for shape, _, _ in outs],
        scratch_shapes=[pltpu.VMEM((tm, tn) if nk > 1 else (8, LANES), F32)] + ([] if lhs_fn is None else [pltpu.VMEM((tm, tk), BF16)]),
        compiler_params=_params("arbitrary", "arbitrary", "arbitrary"),
    )(*args)
    return res[0] if plain else res


def _proj_rope(a, bt, cos, sin, *, n, b_off, rope_cols, tm, tn, name, out_dtype=F32):
    m, k = a.shape
    assert m % tm == 0 and n % tn == 0 and b_off % tn == 0 and rope_cols % LANES == 0, name
    joff = b_off // tn
    nrope, part = divmod(rope_cols, tn)

    def body(a_ref, b_ref, c_ref, s_ref, o_ref):
        acc = lax.dot_general(a_ref[...], b_ref[...], _DIMS["nt"], preferred_element_type=F32)
        j = pl.program_id(1)

        @pl.when(j < nrope)
        def _():
            o_ref[...] = _rope(acc, c_ref[...], s_ref[...], coarse=True).astype(o_ref.dtype)

        if part:
            @pl.when(j == nrope)
            def _():
                o_ref[:, :part] = _rope(acc[:, :part], c_ref[...], s_ref[...], coarse=True).astype(o_ref.dtype)
                o_ref[:, part:] = acc[:, part:].astype(o_ref.dtype)

        @pl.when(j >= nrope + (1 if part else 0))
        def _():
            o_ref[...] = acc.astype(o_ref.dtype)

    table = pl.BlockSpec((tm, LANES), lambda i, j: (i, 0))
    return pl.pallas_call(
        body,
        name=name,
        grid=(m // tm, n // tn),
        in_specs=[pl.BlockSpec((tm, k), lambda i, j: (i, 0)), pl.BlockSpec((tn, k), lambda i, j: (j + joff, 0)), table, table],
        out_specs=pl.BlockSpec((tm, tn), lambda i, j: (i, j)),
        out_shape=jax.ShapeDtypeStruct((m, n), out_dtype),
        compiler_params=_params("parallel", "parallel"),
    )(a, bt, cos, sin)


ROW_TILE = 256


def _rows(width, col=0):
    return pl.BlockSpec((1, ROW_TILE, width), lambda b, t: (b, t, col))


def _per_batch(nrows, width):
    return pl.BlockSpec((1, nrows, width), lambda b, t: (b, 0, 0))


def _row_call(body, name, bsz, seq, in_specs, out_specs, out_shape, accumulates=False):
    return pl.pallas_call(
        body,
        name=name,
        grid=(bsz, seq // ROW_TILE),
        in_specs=in_specs,
        out_specs=out_specs,
        out_shape=out_shape,
        compiler_params=_params("parallel", "arbitrary" if accumulates else "parallel"),
    )


def _acc_rows(acc_ref, first, rows):
    @pl.when(first)
    def _():
        acc_ref[...] = jnp.zeros_like(acc_ref)

    for r, val in enumerate(rows):
        acc_ref[0, r:r + 1, :] += val


def _colsum(v):
    return jnp.sum(v, axis=0, keepdims=True)


def _ln_stats(z):
    mu = jnp.mean(z, axis=-1, keepdims=True)
    zc = z - mu
    var = jnp.mean(zc * zc, axis=-1, keepdims=True)
    rstd = lax.rsqrt(var + LN_EPS)
    return zc * rstd, rstd


def _ln_bwd(dxhat, xhat, rstd):
    m1 = jnp.mean(dxhat, axis=-1, keepdims=True)
    m2 = jnp.mean(dxhat * xhat, axis=-1, keepdims=True)
    return rstd * (dxhat - m1 - xhat * m2)


def _modulate_in(x, mod):
    bsz, seq, d = x.shape

    def body(x_ref, mod_ref, u_ref):
        u_ref[0] = (x_ref[0] * (1.0 + mod_ref[0, 1:2, :]) + mod_ref[0, 0:1, :]).astype(BF16)

    return _row_call(body, "modulate_in", bsz, seq, [_rows(d), _per_batch(8, d)], _rows(d),
                     jax.ShapeDtypeStruct((bsz, seq, d), BF16))(x, mod)


EP_TILE = 512


def _ep_specs(seq, d):
    tiles = seq // EP_TILE
    return ((EP_TILE, d), lambda i, j: (i, 0)), ((1, 8, d), lambda i, j: (i // tiles, 0, 0)), ((1, d), lambda i, j: (0, 0))


def _wo_ln1(merged, wo, x, mod, g, b, seq):
    ntok, d = x.shape
    row, per_b, whole = _ep_specs(seq, d)

    def epilogue(y, i, j, ins, outs):
        x_ref, mod_ref, g_ref, b_ref = ins
        y_ref, h_ref, u_ref = outs
        z = DEEPNORM_ALPHA * x_ref[...] + (1.0 + mod_ref[0, 2:3, :]) * y
        xhat, _ = _ln_stats(z)
        h = xhat * g_ref[...] + b_ref[...]
        y_ref[...] = y
        h_ref[...] = h
        u_ref[...] = (h * (1.0 + mod_ref[0, 4:5, :]) + mod_ref[0, 3:4, :]).astype(BF16)

    f32, bf16 = jax.ShapeDtypeStruct((ntok, d), F32), jax.ShapeDtypeStruct((ntok, d), BF16)
    return _matmul(merged, wo, mode="nn", tm=EP_TILE, tn=d, tk=d, name="w_o_ln1",
                   ins=[(x,) + row, (mod,) + per_b, (g,) + whole, (b,) + whole],
                   outs=[(f32,) + row, (f32,) + row, (bf16,) + row], epilogue=epilogue)


FF_HALF = D_FF // 2


def _interleave_gate_up(w):
    return w.reshape(2, 2, FF_HALF, w.shape[1]).transpose(1, 0, 2, 3).reshape(w.shape)


def _gate_up_silu(u2, wgut_i):
    ntok = u2.shape[0]

    def epilogue(h, i, j, ins, outs):
        h_ref, a_ref = outs
        hg, hu = h[:, :FF_HALF], h[:, FF_HALF:]
        h_ref[...] = h.astype(BF16)
        a_ref[...] = (hg * _sigmoid(hg) * hu).astype(BF16)

    return _matmul(u2, wgut_i, mode="nt", tm=EP_TILE, tn=2 * FF_HALF, tk=u2.shape[1], name="gate_up_silu",
                   outs=[(jax.ShapeDtypeStruct((ntok, 2 * D_FF), BF16), (EP_TILE, 2 * FF_HALF), lambda i, j: (i, j)),
                         (jax.ShapeDtypeStruct((ntok, D_FF), BF16), (EP_TILE, FF_HALF), lambda i, j: (i, j))],
                   epilogue=epilogue)


def _down_dgrad_silu_bwd(dy2, wd, h_i):
    ntok = dy2.shape[0]
    wide = ((EP_TILE, 2 * FF_HALF), lambda i, j: (i, j))

    def epilogue(da, i, j, ins, outs):
        h = ins[0][...].astype(F32)
        hg, hu = h[:, :FF_HALF], h[:, FF_HALF:]
        sg = _sigmoid(hg)
        outs[0][:, :FF_HALF] = (da * hu * (sg * (1.0 + hg * (1.0 - sg)))).astype(BF16)
        outs[0][:, FF_HALF:] = (da * (hg * sg)).astype(BF16)

    return _matmul(dy2, wd, mode="nt", tm=EP_TILE, tn=FF_HALF, tk=dy2.shape[1], name="down_dgrad_silu_bwd",
                   ins=[(h_i,) + wide], outs=[(jax.ShapeDtypeStruct((ntok, 2 * D_FF), BF16),) + wide], epilogue=epilogue)[0]


def _down_ln2_loss_bwd(a, wd, h1, mod, g, b, target, seq):
    ntok, d = h1.shape
    row, per_b, whole = _ep_specs(seq, d)
    tiles = seq // EP_TILE

    def epilogue(y, i, j, ins, outs):
        h_ref, mod_ref, g_ref, b_ref, t_ref = ins
        dy_ref, dh_ref, acc_ref = outs
        gate = 1.0 + mod_ref[0, 5:6, :]
        z = DEEPNORM_ALPHA * h_ref[...] + gate * y
        xhat, rstd = _ln_stats(z)
        diff = xhat * g_ref[...] + b_ref[...] - t_ref[...]
        loss = 0.5 * jnp.sum(jnp.sum(diff * diff, axis=-1, keepdims=True) / d, axis=0, keepdims=True)
        dout = diff / d
        dz = _ln_bwd(dout * g_ref[...], xhat, rstd)
        dy_ref[...] = (gate * dz).astype(BF16)
        dh_ref[...] = DEEPNORM_ALPHA * dz
        _acc_rows(acc_ref, i % tiles == 0,
                  [_colsum(dout * xhat), _colsum(dout), _colsum(dz * y), jnp.broadcast_to(loss, (1, d))])

    return _matmul(a, wd, mode="nn", tm=EP_TILE, tn=d, tk=a.shape[1], name="down_ln2_loss_bwd",
                   ins=[(h1,) + row, (mod,) + per_b, (g,) + whole, (b,) + whole, (target,) + row],
                   outs=[(jax.ShapeDtypeStruct((ntok, d), BF16),) + row, (jax.ShapeDtypeStruct((ntok, d), F32),) + row,
                         (jax.ShapeDtypeStruct((ntok // seq, 8, d), F32),) + per_b], epilogue=epilogue)


def _gate_up_dgrad_ln1_bwd(dh, wgut, dh1a, x, y1, mod, g, b, seq):
    ntok, d = x.shape
    row, per_b, whole = _ep_specs(seq, d)
    tiles = seq // EP_TILE

    def epilogue(du, i, j, ins, outs):
        dh_ref, x_ref, y_ref, mod_ref, g_ref, b_ref = ins
        dy_ref, dx_ref, acc_ref = outs
        y = y_ref[...]
        gate = 1.0 + mod_ref[0, 2:3, :]
        z = DEEPNORM_ALPHA * x_ref[...] + gate * y
        xhat, rstd = _ln_stats(z)
        h1 = xhat * g_ref[...] + b_ref[...]
        dh1 = dh_ref[...] + du * (1.0 + mod_ref[0, 4:5, :])
        dz = _ln_bwd(dh1 * g_ref[...], xhat, rstd)
        dy_ref[...] = (gate * dz).astype(BF16)
        dx_ref[...] = DEEPNORM_ALPHA * dz
        _acc_rows(acc_ref, i % tiles == 0,
                  [_colsum(dh1 * xhat), _colsum(dh1), _colsum(dz * y), _colsum(du * h1), _colsum(du)])

    return _matmul(dh, wgut, mode="nn", tm=EP_TILE, tn=d, tk=D_FF, name="gate_up_dgrad_ln1_bwd",
                   ins=[(dh1a,) + row, (x,) + row, (y1,) + row, (mod,) + per_b, (g,) + whole, (b,) + whole],
                   outs=[(jax.ShapeDtypeStruct((ntok, d), BF16),) + row, (jax.ShapeDtypeStruct((ntok, d), F32),) + row,
                         (jax.ShapeDtypeStruct((ntok // seq, 8, d), F32),) + per_b], epilogue=epilogue)


def _wo_dgrad_gate_bwd(dy1, wo, gab, ya, yb):
    ntok, d = ya.shape
    tm, tn = 1024, 512
    tile = ((tm, tn), lambda i, j: (i, j))
    tile_b = ((tm, tn), lambda i, j: (i, j + d // tn))

    def epilogue(dm_, i, j, ins, outs):
        ga_ref, gb_ref, ya_ref, yb_ref = ins
        dya_ref, dyb_ref, dga_ref, dgb_ref = outs
        sa, sb = _sigmoid(ga_ref[...].astype(F32)), _sigmoid(gb_ref[...].astype(F32))
        dya_ref[...] = (dm_ * sa).astype(BF16)
        dyb_ref[...] = (dm_ * sb).astype(BF16)
        dga_ref[...] = (dm_ * ya_ref[...].astype(F32) * sa * (1.0 - sa)).astype(BF16)
        dgb_ref[...] = (dm_ * yb_ref[...].astype(F32) * sb * (1.0 - sb)).astype(BF16)

    shp = jax.ShapeDtypeStruct((ntok, d), BF16)
    return _matmul(dy1, wo, mode="nt", tm=tm, tn=tn, tk=d, name="w_o_dgrad_gate_bwd",
                   ins=[(gab,) + tile, (gab,) + tile_b, (ya,) + tile, (yb,) + tile],
                   outs=[(shp,) + tile] * 4, epilogue=epilogue)


def _w_in_dgrad_grad_x(dproj, wint, dxa, x, mod, seq, token):
    ntok, d = x.shape
    row, per_b, _ = _ep_specs(seq, d)
    tiles = seq // EP_TILE

    def epilogue(du, i, j, ins, outs):
        dxa_ref, x_ref, mod_ref = ins
        gx_ref, acc_ref = outs
        gx_ref[...] = dxa_ref[...] + du * (1.0 + mod_ref[0, 1:2, :])
        _acc_rows(acc_ref, i % tiles == 0, [_colsum(du * x_ref[...]), _colsum(du)])

    return _matmul(dproj, wint, mode="nn", tm=EP_TILE, tn=d, tk=wint.shape[0] // 2, name="w_in_dgrad_grad_x", token=token,
                   ins=[(dxa,) + row, (x,) + row, (mod,) + per_b],
                   outs=[(jax.ShapeDtypeStruct((ntok, d), F32),) + row, (jax.ShapeDtypeStruct((ntok // seq, 8, d), F32),) + per_b],
                   epilogue=epilogue)


def _merge_branch_b_gate(os_, ls_, wbbt, gab, ya):
    ntok, d = ya.shape
    w = os_[0].shape[1]
    tm, tn = 1024, 512
    tile = ((tm, tn), lambda i, j: (i, j))
    tile_b = ((tm, tn), lambda i, j: (i, j + d // tn))
    row = ((tm, w), lambda i, j: (i, 0))

    def lhs_fn(o0_ref, ins, outs):
        os_r, ls_r = (o0_ref,) + tuple(ins[3:5]), ins[5:8]
        ls = [l[...] for l in ls_r]
        mx = jnp.maximum(jnp.maximum(ls[0], ls[1]), ls[2])
        es = [jnp.exp(l - mx) for l in ls]
        den = es[0] + es[1] + es[2]
        ob = functools.reduce(jnp.add, [(e / den) * o[...].astype(F32) for e, o in zip(es, os_r)]).astype(BF16)
        outs[2][...] = ob
        return ob

    def epilogue(yb, i, j, ins, outs):
        ga_ref, gb_ref, ya_ref = ins[:3]
        yb_ref, merged_ref = outs[:2]
        yb_ref[...] = yb.astype(BF16)
        merged_ref[...] = (_sigmoid(ga_ref[...].astype(F32)) * ya_ref[...].astype(F32)
                           + _sigmoid(gb_ref[...].astype(F32)) * yb).astype(BF16)

    shp = jax.ShapeDtypeStruct((ntok, d), BF16)
    return _matmul(os_[0], wbbt, mode="nt", tm=tm, tn=tn, tk=w, name="merge_branch_b_gate", lhs_fn=lhs_fn,
                   ins=[(gab,) + tile, (gab,) + tile_b, (ya,) + tile] + [(v,) + row for v in list(os_[1:]) + list(ls_)],
                   outs=[(shp,) + tile] * 2 + [(jax.ShapeDtypeStruct((ntok, w), BF16),) + row], epilogue=epilogue)


def _segsum64(v):
    rows, width = v.shape
    ri = lax.broadcasted_iota(jnp.int32, (LANES, LANES), 0) // HEAD_DIM
    ci = lax.broadcasted_iota(jnp.int32, (LANES, LANES), 1) // HEAD_DIM
    ones = jnp.where(ri == ci, 1.0, 0.0).astype(BF16)
    out = []
    for c in range(width // LANES):
        part = v[:, c * LANES:(c + 1) * LANES]
        hi = part.astype(BF16)
        lo = (part - hi.astype(F32)).astype(BF16)
        out.append(jnp.dot(hi, ones, preferred_element_type=F32) + jnp.dot(lo, ones, preferred_element_type=F32))
    return jnp.concatenate(out, axis=1) if len(out) > 1 else out[0]


def _branch_b_dgrad_merge_bwd(dyb, wbbt, os_, ls_):
    ntok, w = os_[0].shape
    row = ((EP_TILE, w), lambda i, j: (i, 0))

    def epilogue(dob_, i, j, ins, outs):
        os_r, ls_r = ins[:3], ins[3:]
        do_r, dd_r = outs[:3], outs[3:]
        ls = [l[...] for l in ls_r]
        mx = jnp.maximum(jnp.maximum(ls[0], ls[1]), ls[2])
        es = [jnp.exp(l - mx) for l in ls]
        den = es[0] + es[1] + es[2]
        ws = [e / den for e in es]
        dws = [_segsum64(dob_ * o[...].astype(F32)) for o in os_r]
        mean = ws[0] * dws[0] + ws[1] * dws[1] + ws[2] * dws[2]
        for wg, do_ref, dd_ref in zip(ws, do_r, dd_r):
            do_ref[...] = wg * dob_
            dd_ref[...] = -wg * mean

    shp = jax.ShapeDtypeStruct((ntok, w), F32)
    return _matmul(dyb, wbbt, mode="nn", tm=EP_TILE, tn=w, tk=dyb.shape[1], name="branch_b_dgrad_merge_bwd",
                   ins=[(v,) + row for v in list(os_) + list(ls_)], outs=[(shp,) + row] * 6, epilogue=epilogue)


def _branch_a_dgrad_delta(dya, wba, oa, lse_a, sinks_exp, seq):
    ntok, w = oa.shape
    row, per_b, whole = _ep_specs(seq, w)
    tiles = seq // EP_TILE

    def epilogue(do_, i, j, ins, outs):
        o_ref, l_ref, s_ref = ins
        do_ref, dd_ref, acc_ref = outs
        dd = -_segsum64(do_ * o_ref[...].astype(F32))
        do_ref[...] = do_.astype(BF16)
        dd_ref[...] = dd
        _acc_rows(acc_ref, i % tiles == 0, [_colsum(dd * jnp.exp(s_ref[...] - l_ref[...]))])

    shp = jax.ShapeDtypeStruct((ntok, w), F32)
    return _matmul(dya, wba, mode="nt", tm=EP_TILE, tn=w, tk=dya.shape[1], name="branch_a_dgrad_delta",
                   ins=[(oa,) + row, (lse_a,) + row, (sinks_exp,) + whole],
                   outs=[(jax.ShapeDtypeStruct((ntok, w), BF16),) + row, (shp,) + row,
                         (jax.ShapeDtypeStruct((ntok // seq, 8, w), F32),) + per_b],
                   epilogue=epilogue)


def _swap_halves(v):
    src = lax.broadcasted_iota(jnp.int32, (LANES, LANES), 0)
    dst = lax.broadcasted_iota(jnp.int32, (LANES, LANES), 1)
    partner = jnp.where((dst % HEAD_DIM) < HEAD_DIM // 2, dst + HEAD_DIM // 2, dst - HEAD_DIM // 2)
    perm = jnp.where(src == partner, 1.0, 0.0).astype(BF16)
    hi = v.astype(BF16)
    lo = (v - hi.astype(F32)).astype(BF16)
    return jnp.dot(hi, perm, preferred_element_type=F32) + jnp.dot(lo, perm, preferred_element_type=F32)


def _swap_halves_roll(v):
    lane = lax.broadcasted_iota(jnp.int32, v.shape, 1)
    return jnp.where((lane % HEAD_DIM) < HEAD_DIM // 2, pltpu.roll(v, LANES - HEAD_DIM // 2, 1),
                     pltpu.roll(v, HEAD_DIM // 2, 1))


def _swap_halves_coarse(v):
    src = lax.broadcasted_iota(jnp.int32, (LANES, LANES), 0)
    dst = lax.broadcasted_iota(jnp.int32, (LANES, LANES), 1)
    partner = jnp.where((dst % HEAD_DIM) < HEAD_DIM // 2, dst + HEAD_DIM // 2, dst - HEAD_DIM // 2)
    perm = jnp.where(src == partner, 1.0, 0.0).astype(BF16)
    return jnp.dot(v.astype(BF16), perm, preferred_element_type=F32)


def _rope(v, cos, sin, sign=1.0, mxu=True, coarse=False):
    swap = (_swap_halves_coarse if coarse else _swap_halves) if mxu else _swap_halves_roll
    out = []
    for c in range(v.shape[1] // LANES):
        part = v[:, c * LANES:(c + 1) * LANES]
        out.append(part * cos + sign * (swap(part) * sin))
    return jnp.concatenate(out, axis=1) if len(out) > 1 else out[0]


def _half_mask(shape, half):
    lane = lax.broadcasted_iota(jnp.int32, shape, len(shape) - 1) % LANES
    return (lane < HEAD_DIM) if half == 0 else (lane >= HEAD_DIM)


def _dup_half(v, half):
    return jnp.where(_half_mask(v.shape, half), v, pltpu.roll(v, HEAD_DIM, 1))


def _fold_halves(v):
    return v + pltpu.roll(v, HEAD_DIM, 1)


def _pick_halves(lo_rows, hi_rows):
    return jnp.where(_half_mask(lo_rows.shape, 0), lo_rows, hi_rows)


def _stack_masked(v, pairs):
    parts = []
    for c in pairs:
        pair = v[:, c * LANES:(c + 1) * LANES]
        parts += [jnp.where(_half_mask(pair.shape, half), pair, 0.0) for half in (0, 1)]
    return jnp.concatenate(parts, axis=0)


def _stack_pair_cols(v, pairs):
    return jnp.concatenate([v[:, c * LANES + half * HEAD_DIM:c * LANES + half * HEAD_DIM + 1] for c in pairs for half in (0, 1)],
                           axis=0)


ATTN_UNITS = 16


def _class_rows(r):
    return [pl.ds(0, QBLOCK)] if r == 1 else [pl.ds(rho, QBLOCK, stride=r) for rho in range(r)]


def _band_mask(nrows, nk, blk, n_back, has_prev):
    qi = lax.broadcasted_iota(jnp.int32, (nrows, nk), 0) % QBLOCK
    ki = lax.broadcasted_iota(jnp.int32, (nrows, nk), 1)
    if has_prev:
        dist = qi + QBLOCK - ki
        return (dist >= 0) & (dist <= n_back) & ((ki >= QBLOCK) | (blk > 0))
    dist = qi - ki
    return (dist >= 0) & (dist <= n_back)


def _attn_fwd(q_arr, k_arr, v_arr, *, name, npair, gqa, q_col, k_col, v_col, nchunk, r, n_back, sinks=None):
    bsz, seq, _ = q_arr.shape
    rr = QBLOCK * r
    nblk = seq // rr
    qw = npair * LANES
    kw = LANES if gqa else qw
    has_prev = nblk > 1
    has_sink = sinks is not None
    scale = HEAD_DIM ** -0.5

    def body(*refs):
        refs = list(refs)
        q_ref, kc_ref, vc_ref = refs[:3]
        pos = 3
        if has_prev:
            kp_ref, vp_ref = refs[pos:pos + 2]
            pos += 2
        if has_sink:
            sink_ref = refs[pos]
            pos += 1
        o_ref, lse_ref = refs[pos:pos + 2]
        if r > 1:
            stage_o = refs[pos + 2]
        blk = pl.program_id(2)
        nk = (2 if has_prev else 1) * QBLOCK
        valid = _band_mask(QBLOCK, nk, blk, n_back, has_prev)
        per = npair // 2
        classes = _class_rows(r)
        step = max(1, ATTN_UNITS // (2 * npair))
        for first in range(0, len(classes), step):
            batch = classes[first:first + step]
            units = []
            for ci, rows in enumerate(batch):
                q = q_ref[0, rows, :] * scale
                k, v = kc_ref[0, rows, :], vc_ref[0, rows, :]
                if has_prev:
                    k = jnp.concatenate([kp_ref[0, rows, :], k], axis=0)
                    v = jnp.concatenate([vp_ref[0, rows, :], v], axis=0)
                if gqa:
                    kdup = [_dup_half(k, hk).astype(BF16) for hk in range(2)]
                    vdup = [_dup_half(v, hk) for hk in range(2)]
                for c in range(npair):
                    sl = slice(c * LANES, (c + 1) * LANES)
                    qc = q[:, sl]
                    kc, vc = (kdup[c // per], vdup[c // per]) if gqa else (k[:, sl].astype(BF16), v[:, sl])
                    for half in (0, 1):
                        qm = jnp.where(_half_mask(qc.shape, half), qc, 0.0).astype(BF16)
                        vm = jnp.where(_half_mask(vc.shape, half), vc, 0.0).astype(BF16)
                        s = lax.dot_general(qm, kc, _DIMS["nt"], preferred_element_type=F32)
                        units.append(dict(ci=ci, c=c, half=half, s=s, vm=vm, sk=sink_ref[2 * c + half] if has_sink else None))
            for u in units:
                s = jnp.where(valid, u["s"], NEG_INF)
                m = jnp.max(s, axis=1, keepdims=True)
                if has_sink:
                    m = jnp.maximum(m, u["sk"])
                p = jnp.exp(s - m)
                den = jnp.sum(p, axis=1, keepdims=True)
                if has_sink:
                    den = den + jnp.exp(u["sk"] - m)
                u.update(p=p.astype(BF16), den=den, lse=m + jnp.log(den))
            for u in units:
                u["o"] = jnp.dot(u["p"], u["vm"], preferred_element_type=F32) / u["den"]
            for ci, rows in enumerate(batch):
                outs, lses = [None] * npair, [None] * npair
                for u in units:
                    if u["ci"] != ci:
                        continue
                    c, o = u["c"], u["o"]
                    lse = jnp.broadcast_to(u["lse"], o.shape)
                    outs[c] = o if u["half"] == 0 else outs[c] + o
                    lses[c] = lse if u["half"] == 0 else _pick_halves(lses[c], lse)
                o_new = jnp.concatenate(outs, axis=1) if npair > 1 else outs[0]
                if r > 1:
                    stage_o[rows, :] = o_new
                else:
                    o_ref[0] = o_new.astype(BF16)
                lse_ref[0, rows, :] = jnp.concatenate(lses, axis=1) if npair > 1 else lses[0]
        if r > 1:
            o_ref[0] = stage_o[...].astype(BF16)

    def cur(width, col0):
        return pl.BlockSpec((1, rr, width), lambda b, c, i: (b, i, col0 + c))

    def prev(width, col0):
        return pl.BlockSpec((1, rr, width), lambda b, c, i: (b, jnp.maximum(i - 1, 0), col0 + c))

    in_specs = [cur(qw, q_col), cur(kw, k_col), cur(kw, v_col)]
    args = [q_arr, k_arr, v_arr]
    if has_prev:
        in_specs += [prev(kw, k_col), prev(kw, v_col)]
        args += [k_arr, v_arr]
    if has_sink:
        in_specs.append(pl.BlockSpec(memory_space=pltpu.SMEM))
        args.append(sinks)
    return pl.pallas_call(
        body,
        name=name,
        grid=(bsz, nchunk, nblk),
        in_specs=in_specs,
        out_specs=[pl.BlockSpec((1, rr, qw), lambda b, c, i: (b, i, c))] * 2,
        out_shape=[jax.ShapeDtypeStruct((bsz, seq, nchunk * qw), BF16), jax.ShapeDtypeStruct((bsz, seq, nchunk * qw), F32)],
        scratch_shapes=[pltpu.VMEM((rr, qw), F32)] if r > 1 else [],
        compiler_params=_params("parallel", "parallel", "parallel"),
    )(*args)


def _attn_bwd(q_arr, k_arr, v_arr, cos, sin, do, lse, dd, *, name, npair, gqa, q_col, k_col, v_col, nchunk, r, n_back,
              token=None):
    bsz, seq, _ = q_arr.shape
    rr = QBLOCK * r
    nblk = seq // rr
    qw = npair * LANES
    kw = LANES if gqa else qw
    has_next = nblk > 1
    has_token = token is not None
    staged = r > 1
    scale = HEAD_DIM ** -0.5

    def body(*refs):
        refs = list(refs)
        k_ref, v_ref, c_ref, s_ref = refs[:4]
        tile_refs = [refs[4:8]]
        pos = 8
        if has_next:
            tile_refs.append(refs[pos:pos + 4])
            pos += 4
        if has_token:
            pos += 1
        dq_ref, dk_ref, dv_ref = refs[pos:pos + 3]
        carry_ref, bcast_ref = refs[pos + 3:pos + 5]
        if staged:
            stage_q, stage_k, stage_v = refs[pos + 5:pos + 8]
        blk = pl.program_id(2)
        if has_next:
            @pl.when(blk == 0)
            def _():
                carry_ref[...] = jnp.zeros_like(carry_ref)

        nrows = (npair if gqa else 1) * QBLOCK
        qi = lax.broadcasted_iota(jnp.int32, (nrows, QBLOCK), 0) % QBLOCK
        ki = lax.broadcasted_iota(jnp.int32, (nrows, QBLOCK), 1)
        valids = [qi >= ki, (qi + QBLOCK - ki <= n_back) & (blk + 1 < nblk)]
        per = npair // 2
        ntile = len(tile_refs)
        cat = lambda parts: jnp.concatenate(parts, axis=1) if len(parts) > 1 else parts[0]
        classes = _class_rows(r)
        step = max(1, ATTN_UNITS // (ntile * (2 if gqa else 2 * npair)))
        def stat_cols(stat, slot):
            if gqa:
                return _stack_pair_cols(stat, list(range(slot * per, (slot + 1) * per)))
            col = slot * HEAD_DIM
            return stat[:, col:col + 1]

        nslot = 2 if gqa else 2 * npair
        if has_next:
            @pl.when(blk == 0)
            def _():
                for rows in classes:
                    for which, stat_ref in enumerate(tile_refs[0][2:4]):
                        stat = stat_ref[0, rows, :]
                        for slot in range(nslot):
                            bcast_ref[which, slot, rows if not gqa else slice(None), :] = jnp.broadcast_to(
                                stat_cols(stat, slot), (nrows, LANES))

        for first in range(0, len(classes), step):
            batch = classes[first:first + step]
            units = []
            for ci, rows in enumerate(batch):
                keep = slice(None) if gqa else rows
                tiles = [(q_ref[0, rows, :] * scale, do_ref[0, rows, :], l_ref[0, rows, :], d_ref[0, rows, :])
                         for q_ref, do_ref, l_ref, d_ref in tile_refs]

                def stats(t, slot, keep=keep, tiles=tiles):
                    if has_next and t == 0:
                        return bcast_ref[0, slot, keep, :], bcast_ref[1, slot, keep, :]
                    return tuple(jnp.broadcast_to(stat_cols(tiles[t][2 + w], slot), (nrows, LANES)) for w in range(2))

                k, v = k_ref[0, rows, :], v_ref[0, rows, :]
                if gqa:
                    for hk in range(2):
                        pairs = list(range(hk * per, (hk + 1) * per))
                        kd, vd = _dup_half(k, hk).astype(BF16), _dup_half(v, hk).astype(BF16)
                        for t, (q, do_, l_, d_) in enumerate(tiles):
                            lcol, dcol = stats(t, hk)
                            units.append(dict(ci=ci, t=t, hk=hk, slot=hk, keep=keep, pairs=pairs,
                                              qs=_stack_masked(q, pairs).astype(BF16),
                                              dos=_stack_masked(do_, pairs).astype(BF16), lcol=lcol, dcol=dcol,
                                              kmat=kd, vmat=vd, kdq=kd))
                else:
                    for c in range(npair):
                        sl = slice(c * LANES, (c + 1) * LANES)
                        kc, vcb = k[:, sl], v[:, sl].astype(BF16)
                        kcb = kc.astype(BF16)
                        for t, (q, do_, l_, d_) in enumerate(tiles):
                            for half in (0, 1):
                                hm = _half_mask(kc.shape, half)
                                lcol, dcol = stats(t, 2 * c + half)
                                units.append(dict(ci=ci, t=t, c=c, half=half, slot=2 * c + half, keep=keep,
                                                  qs=jnp.where(hm, q[:, sl], 0.0).astype(BF16),
                                                  dos=jnp.where(hm, do_[:, sl], 0.0).astype(BF16), lcol=lcol, dcol=dcol,
                                                  kmat=kcb, vmat=vcb, kdq=jnp.where(hm, kc, 0.0).astype(BF16)))
            for u in units:
                u["s"] = lax.dot_general(u["qs"], u["kmat"], _DIMS["nt"], preferred_element_type=F32)
                u["dp"] = lax.dot_general(u["dos"], u["vmat"], _DIMS["nt"], preferred_element_type=F32)
            for u in units:
                p = jnp.exp(jnp.where(valids[u["t"]], u["s"], NEG_INF) - u["lcol"])
                u["ds"] = (p * (u["dp"] + u["dcol"])).astype(BF16)
                u["p"] = p.astype(BF16)
            for u in units:
                u["dv"] = lax.dot_general(u["p"], u["dos"], _DIMS["tn"], preferred_element_type=F32)
                u["dk"] = lax.dot_general(u["ds"], u["qs"], _DIMS["tn"], preferred_element_type=F32)
                u["dq"] = jnp.dot(u["ds"], u["kdq"], preferred_element_type=F32) * scale
            for u in units:
                if u["t"] == 1:
                    bcast_ref[0, u["slot"], u["keep"], :] = u["lcol"]
                    bcast_ref[1, u["slot"], u["keep"], :] = u["dcol"]
            for ci, rows in enumerate(batch):
                mine = [u for u in units if u["ci"] == ci]
                dq = [[None] * npair for _ in range(ntile)]
                if gqa:
                    dk_out = dv_out = None
                    for hk in range(2):
                        us = [u for u in mine if u["hk"] == hk]
                        for u in us:
                            for i, c in enumerate(u["pairs"]):
                                dq[u["t"]][c] = _pick_halves(u["dq"][2 * i * QBLOCK:(2 * i + 1) * QBLOCK],
                                                             u["dq"][(2 * i + 1) * QBLOCK:(2 * i + 2) * QBLOCK])
                        dk_h = _fold_halves(functools.reduce(jnp.add, [u["dk"] for u in us]))
                        dv_h = _fold_halves(functools.reduce(jnp.add, [u["dv"] for u in us]))
                        dk_out = dk_h if hk == 0 else _pick_halves(dk_out, dk_h)
                        dv_out = dv_h if hk == 0 else _pick_halves(dv_out, dv_h)
                else:
                    dks, dvs = [], []
                    for c in range(npair):
                        us = [u for u in mine if u["c"] == c]
                        dks.append(functools.reduce(jnp.add, [u["dk"] for u in us]))
                        dvs.append(functools.reduce(jnp.add, [u["dv"] for u in us]))
                        for t in range(ntile):
                            dq[t][c] = functools.reduce(jnp.add, [u["dq"] for u in us if u["t"] == t])
                    dk_out, dv_out = cat(dks), cat(dvs)
                ck, sk_ = c_ref[0, rows, :], s_ref[0, rows, :]
                dk_new = _rope(dk_out, ck, sk_, sign=-1.0, mxu=gqa, coarse=True)
                dq_cur = cat(dq[0])
                if has_next:
                    dq_cur = dq_cur + carry_ref[rows, :]
                    carry_ref[rows, :] = cat(dq[1])
                dq_new = _rope(dq_cur, ck, sk_, sign=-1.0, mxu=gqa, coarse=True)
                if staged:
                    stage_q[rows, :], stage_k[rows, :], stage_v[rows, :] = dq_new, dk_new, dv_out
                else:
                    dq_ref[0], dk_ref[0], dv_ref[0] = dq_new.astype(BF16), dk_new.astype(BF16), dv_out.astype(BF16)
        if staged:
            dq_ref[0], dk_ref[0], dv_ref[0] = stage_q[...].astype(BF16), stage_k[...].astype(BF16), stage_v[...].astype(BF16)

    def at(width, col0, shift):
        return pl.BlockSpec((1, rr, width), lambda b, c, i: (b, jnp.minimum(i + shift, nblk - 1), col0 + c))

    in_specs = [at(kw, k_col, 0), at(kw, v_col, 0), pl.BlockSpec((1, rr, LANES), lambda b, c, i: (b, i, 0)),
                pl.BlockSpec((1, rr, LANES), lambda b, c, i: (b, i, 0))]
    args = [k_arr, v_arr, cos, sin]
    for shift in (0, 1) if has_next else (0,):
        in_specs += [at(qw, q_col, shift), at(qw, 0, shift), at(qw, 0, shift), at(qw, 0, shift)]
        args += [q_arr, do, lse, dd]
    if has_token:
        in_specs.append(pl.BlockSpec(token.shape, lambda b, c, i: (0, 0)))
        args.append(token)
    return pl.pallas_call(
        body,
        name=name,
        grid=(bsz, nchunk, nblk),
        in_specs=in_specs,
        out_specs=[pl.BlockSpec((1, rr, qw), lambda b, c, i: (b, i, c)),
                   pl.BlockSpec((1, rr, kw), lambda b, c, i: (b, i, c)),
                   pl.BlockSpec((1, rr, kw), lambda b, c, i: (b, i, c))],
        out_shape=[jax.ShapeDtypeStruct((bsz, seq, nchunk * qw), BF16),
                   jax.ShapeDtypeStruct((bsz, seq, nchunk * kw), BF16),
                   jax.ShapeDtypeStruct((bsz, seq, nchunk * kw), BF16)],
        scratch_shapes=[pltpu.VMEM((rr, qw) if has_next else (8, LANES), F32),
                        pltpu.VMEM((2, 2 if gqa else 2 * npair, npair * QBLOCK if gqa else rr, LANES) if has_next
                                   else (1, 1, 8, LANES), F32)] +
                       ([pltpu.VMEM((rr, qw), F32), pltpu.VMEM((rr, kw), F32), pltpu.VMEM((rr, kw), F32)] if staged else []),
        compiler_params=_params("parallel", "parallel", "arbitrary"),
    )(*args)


B_CHUNKS = {1: (4, 1), 4: (1, 4), 16: (1, 4)}


def _rope_tables(positions):
    half = HEAD_DIM // 2
    inv = ROPE_THETA ** (-jnp.arange(half, dtype=F32) / half)
    ang = positions.astype(F32)[..., None] * inv
    cos, sin = jnp.cos(ang), jnp.sin(ang)
    return jnp.concatenate([cos] * 4, axis=-1), jnp.concatenate([-sin, sin, -sin, sin], axis=-1)


def _layer_step(x, mod, tables, sinks, ln1_g, ln1_b, ln2_g, ln2_b, target, get_w_in, get_rest, hook):
    bsz, seq, d = x.shape
    ntok = bsz * seq
    flat = lambda v: v.reshape(ntok, v.shape[-1])
    unflat = lambda v: v.reshape(bsz, seq, v.shape[-1])
    cos, sin = tables
    mm = functools.partial(_matmul, tm=1024, tk=1024)
    scalar = lambda tok: 0.0 if tok is None else tok[0, 0]

    u1 = _modulate_in(x, mod)
    u1f = flat(u1)
    wint = get_w_in(u1)
    cosf, sinf = flat(cos), flat(sin)
    proj = functools.partial(_proj_rope, u1f, wint, cosf, sinf, tm=2048)
    qkvb = unflat(proj(n=4608, b_off=OFF_QKVB, rope_cols=3072, tn=256, name="proj_qkvb"))
    b_kws, os_, ls_ = [], [], []
    for g, (window, r) in enumerate(B_PATTERNS):
        npair, nch = B_CHUNKS[r]
        per = B_HEADS_PER_GROUP // (2 * npair)
        nsec = len(B_PATTERNS) * per
        kw_ = dict(npair=npair, gqa=False, q_col=g * per, k_col=nsec + g * per, v_col=2 * nsec + g * per, nchunk=nch, r=r,
                   n_back=window // r)
        b_kws.append(kw_)
        o_g, l_g = _attn_fwd(qkvb, qkvb, qkvb, name=f"attn_b{g}_fwd", **kw_)
        os_.append(o_g)
        ls_.append(l_g)
    tok = hook("projected", os_[-1])
    proj = functools.partial(_proj_rope, u1f, wint, cosf + scalar(tok), sinf, tm=2048)
    gab = unflat(proj(n=2048, b_off=OFF_GAB, rope_cols=0, tn=256, name="proj_gab", out_dtype=BF16))
    qa = kva = unflat(proj(n=OFF_QKVB, b_off=OFF_QA, rope_cols=OFF_KVA + LANES, tn=256, name="proj_qkva", out_dtype=BF16))
    a_kw = dict(npair=A_Q_HEADS // 2, gqa=True, q_col=0, k_col=OFF_KVA // LANES, v_col=OFF_KVA // LANES + 1, nchunk=1, r=1,
                n_back=A_WINDOW - 1)
    after_gab = jnp.minimum(jnp.abs(gab[0, 0, 0].astype(F32)), 0.0)
    oa, lse_a = _attn_fwd(qa, kva, kva, name="attn_a_fwd", sinks=sinks.reshape(A_Q_HEADS) + after_gab, **a_kw)
    rest = get_rest(oa)
    wba, wbbt, wo, wgut, wd = (rest[n] for n in ("w_branch_a", "w_branch_b", "w_o", "w_gate_up", "w_down"))
    ya = unflat(mm(flat(oa), wba, mode="nn", out_dtype=BF16, tn=512, name="branch_a"))
    ybf, mergedf, obf = _merge_branch_b_gate([flat(t) for t in os_], [flat(t) for t in ls_], wbbt, flat(gab), flat(ya))
    xf = flat(x)
    y1f, h1f, u2f = _wo_ln1(mergedf, wo, xf, mod, ln1_g, ln1_b, seq)
    wgut_i = _interleave_gate_up(wgut)
    hf, af = _gate_up_silu(u2f, wgut_i)

    dy2f, dh1af, acc2 = _down_ln2_loss_bwd(af, wd, h1f, mod, ln2_g, ln2_b, flat(target), seq)
    g_wd = _matmul(af, dy2f, mode="tn", out_dtype=BF16, tm=256, tn=1024, tk=ntok, name="down_wgrad")
    dhf = _down_dgrad_silu_bwd(dy2f, wd, hf)
    g_wgut = _interleave_gate_up(_matmul(dhf, u2f, mode="tn", out_dtype=BF16, tm=256, tn=1024, tk=ntok, name="gate_up_wgrad"))
    dy1f, dxaf, acc1 = _gate_up_dgrad_ln1_bwd(dhf, wgut_i, dh1af, xf, y1f, mod, ln1_g, ln1_b, seq)
    g_wo = _matmul(mergedf, dy1f, mode="tn", out_dtype=BF16, tm=256, tn=1024, tk=ntok, name="w_o_wgrad")
    dyaf, dybf, dgaf, dgbf = _wo_dgrad_gate_bwd(dy1f, wo, flat(gab), flat(ya), ybf)
    g_wba = _matmul(flat(oa), dyaf, mode="tn", out_dtype=BF16, tm=256, tn=1024, tk=ntok, name="branch_a_wgrad")
    g_wbbt = _matmul(dybf, obf, mode="tn", out_dtype=BF16, tm=256, tn=512, tk=ntok, name="branch_b_wgrad")
    tok = hook("grads_rest", dict(w_branch_a=g_wba, w_branch_b=g_wbbt, w_o=g_wo, w_gate_up=g_wgut, w_down=g_wd))

    sinks_exp = jnp.repeat(sinks.reshape(1, A_Q_HEADS), HEAD_DIM, axis=1) + scalar(tok)
    doa, dd_a, acc_s = _branch_a_dgrad_delta(dyaf, wba, flat(oa), flat(lse_a), sinks_exp, seq)
    doa, dd_a = unflat(doa), unflat(dd_a)
    tok = hook("delta_done", dd_a)
    dqa, dka, dva = _attn_bwd(qa, kva, kva, cos, sin, doa, lse_a, dd_a, name="attn_a_bwd", token=tok, **a_kw)
    merged_bwd = [unflat(t) for t in _branch_b_dgrad_merge_bwd(dybf, wbbt, [flat(t) for t in os_], [flat(t) for t in ls_])]
    dqs, dks, dvs = [], [], []
    for g in range(len(B_PATTERNS)):
        dq_g, dk_g, dv_g = _attn_bwd(qkvb, qkvb, qkvb, cos, sin, merged_bwd[g], ls_[g], merged_bwd[3 + g],
                                     name=f"attn_b{g}_bwd", **b_kws[g])
        dqs.append(dq_g)
        dks.append(dk_g)
        dvs.append(dv_g)
    dproj = jnp.concatenate([t.astype(BF16) for t in [dqa, dka, dva] + dqs + dks + dvs] + [unflat(dgaf), unflat(dgbf)], axis=-1)
    dprojf = flat(dproj)
    g_wint = _matmul(dprojf, u1f, mode="tn", out_dtype=BF16, tm=256, tn=1024, tk=ntok, name="w_in_wgrad")
    tok = hook("grads_w_in", dict(w_in=g_wint))
    grad_x, acc0 = _w_in_dgrad_grad_x(dprojf, wint, dxaf, xf, mod, seq, tok)
    grad_x = unflat(grad_x)
    tok = hook("dgrad_done", grad_x)

    loss_part = jnp.sum(acc2[:, 3, 0])
    dmod = jnp.stack([acc0[:, 1], acc0[:, 0], acc1[:, 2], acc1[:, 4], acc1[:, 3], acc2[:, 2]], axis=1)
    small = jnp.stack([acc1[:, 0].sum(0), acc1[:, 1].sum(0), acc2[:, 0].sum(0), acc2[:, 1].sum(0), acc_s[:, 0].sum(0)])
    small = small + scalar(tok)
    return loss_part, grad_x, dmod, small


CHIP_FLIPS = (2, 4, 6)


def _my_place():
    return lax.axis_index("x"), lax.axis_index("y"), lax.axis_index("c")


def _flip(place, k):
    px, py, pc = place
    return (1 - px if k & 4 else px, 1 - py if k & 2 else py, 1 - pc if k & 1 else pc)


def _index(place):
    return 4 * place[0] + 2 * place[1] + place[2]


def _gather_small(v, name):
    rows, cols = v.shape

    def body(v_ref, out_ref, send_sems, recv_sems):
        me = _my_place()
        out_ref[_index(me)] = v_ref[...]
        copies = []
        for k in range(1, N_DEV):
            copies.append(pltpu.make_async_remote_copy(
                src_ref=v_ref, dst_ref=out_ref.at[_index(me)], send_sem=send_sems.at[k - 1], recv_sem=recv_sems.at[k - 1],
                device_id=_flip(me, k), device_id_type=MESH))
        for cp in copies:
            cp.start()
        for k in range(1, N_DEV):
            pltpu.make_async_remote_copy(
                src_ref=v_ref, dst_ref=out_ref.at[_index(_flip(me, k))], send_sem=send_sems.at[k - 1],
                recv_sem=recv_sems.at[k - 1], device_id=_flip(me, k), device_id_type=MESH).wait_recv()
        for cp in copies:
            cp.wait_send()

    return pl.pallas_call(
        body,
        name=name,
        out_shape=jax.ShapeDtypeStruct((N_DEV, rows, cols), v.dtype),
        in_specs=[pl.BlockSpec(memory_space=pltpu.VMEM)],
        out_specs=pl.BlockSpec(memory_space=pltpu.VMEM),
        scratch_shapes=[pltpu.SemaphoreType.DMA((N_DEV - 1,)), pltpu.SemaphoreType.DMA((N_DEV - 1,))],
        compiler_params=pltpu.CompilerParams(vmem_limit_bytes=VMEM_LIMIT_BYTES),
    )(v)


_HBM = pl.BlockSpec(memory_space=pltpu.HBM)
_SEM = pl.BlockSpec(memory_space=pltpu.SEMAPHORE)
_EFFECT = pltpu.SideEffectType.DATAFLOW_SIDE_EFFECTING


def _remote(src, dst, send_sems, recv_sems, j, to):
    return pltpu.make_async_remote_copy(src_ref=src, dst_ref=dst, send_sem=send_sems.at[j], recv_sem=recv_sems.at[j],
                                        device_id=to, device_id_type=MESH)


def _copies_start(name, bufs, make_copies, nsem):
    nbuf = len(bufs)

    def body(*refs):
        for cp in make_copies(refs[:nbuf], refs[nbuf], refs[nbuf + 1]):
            cp.start()
        refs[-1][...] = jnp.zeros_like(refs[-1])

    sems = pltpu.SemaphoreType.DMA((nsem,))
    res = pl.pallas_call(
        body, name=name,
        out_shape=(sems, sems, *[pltpu.HBM(v.shape, v.dtype) for v in bufs], jax.ShapeDtypeStruct((8, LANES), F32)),
        in_specs=(_HBM,) * nbuf, out_specs=(_SEM, _SEM) + (_HBM,) * nbuf + (pl.BlockSpec(memory_space=pltpu.VMEM),),
        input_output_aliases={i: 2 + i for i in range(nbuf)},
        compiler_params=pltpu.CompilerParams(has_side_effects=_EFFECT),
    )(*[pltpu.with_memory_space_constraint(v, pltpu.HBM) for v in bufs])
    return res[0], res[1], list(res[2:2 + nbuf]), res[-1]


def _copies_wait(name, started, make_copies, after):
    send_sems, recv_sems, bufs, _ = started
    nbuf = len(bufs)

    def body(*refs):
        for cp in make_copies(refs[:nbuf], refs[nbuf], refs[nbuf + 1]):
            cp.wait_send()
            cp.wait_recv()

    return list(pl.pallas_call(
        body, name=name,
        out_shape=tuple(pltpu.HBM(v.shape, v.dtype) for v in bufs),
        in_specs=(_HBM,) * nbuf + (_SEM, _SEM, pl.BlockSpec(memory_space=pl.ANY)), out_specs=(_HBM,) * nbuf,
        input_output_aliases={i: i for i in range(nbuf)},
        compiler_params=pltpu.CompilerParams(has_side_effects=_EFFECT),
    )(*bufs, send_sems, recv_sems, after))


def _to_sibling_copies(refs, send_sems, recv_sems):
    src_ref, land_ref = refs
    me = _my_place()
    return [_remote(src_ref.at[q, 1 - me[2]], land_ref.at[q], send_sems, recv_sems, q, _flip(me, 1)) for q in range(4)]


def _to_chips_copies(refs, send_sems, recv_sems):
    src_ref, land_ref = refs
    me = _my_place()
    copies = []
    for j, k in enumerate(CHIP_FLIPS):
        to = _flip(me, k)
        copies.append(_remote(src_ref.at[2 * to[0] + to[1]], land_ref.at[j], send_sems, recv_sems, j, to))
    return copies


class _Gather:
    def __init__(self, name, blocks):
        self.name, self.n = name, len(blocks)
        at_me = (_index(_my_place()), 0, 0)
        lands = [lax.dynamic_update_slice(lax.empty((N_DEV,) + v.shape, v.dtype), v[None], at_me) for v in blocks]
        self.first = _copies_start(name + "_start", list(blocks) + lands, self._first_copies, 4 * self.n)
        self.token = self.first[3]

    def _first_copies(self, refs, send_sems, recv_sems):
        me = _my_place()
        return [_remote(refs[w], refs[self.n + w].at[_index(me)], send_sems, recv_sems, 4 * w + j, _flip(me, k))
                for w in range(self.n) for j, k in enumerate((1,) + CHIP_FLIPS)]

    def _pass_copies(self, refs, send_sems, recv_sems):
        me = _my_place()
        copies = []
        for w, land in enumerate(refs):
            for j, k in enumerate(CHIP_FLIPS):
                slot = land.at[_index(_flip(me, k))]
                copies.append(_remote(slot, slot, send_sems, recv_sems, 3 * w + j, _flip(me, 1)))
        return copies

    def pass_on(self, after):
        lands = _copies_wait(self.name + "_wait", self.first, self._first_copies, after)[self.n:]
        self.second = _copies_start(self.name + "_pass_start", lands, self._pass_copies, 3 * self.n)
        return self.second[3]

    def finish(self, after):
        return _copies_wait(self.name + "_pass_wait", self.second, self._pass_copies, after)


def _to_all_copies(refs, send_sems, recv_sems):
    src_ref, land_ref = refs
    me = _my_place()
    return [_remote(src_ref, land_ref.at[_index(me)], send_sems, recv_sems, k - 1, _flip(me, k)) for k in range(1, N_DEV)]


SUM_SPLIT = 2


def _sum_pairs(parts, theirs):
    nchip, _, rows, cols = parts.shape
    tile = rows // SUM_SPLIT

    def body(c_ref, a_ref, b_ref, o_ref):
        o_ref[...] = (a_ref[0].astype(F32) + b_ref[...].astype(F32)).astype(BF16)

    spec = pl.BlockSpec((1, tile, cols), lambda q, t, c_ref: (q, t, 0))
    grid_spec = pltpu.PrefetchScalarGridSpec(
        num_scalar_prefetch=1, grid=(nchip, SUM_SPLIT),
        in_specs=[pl.BlockSpec((1, 1, tile, cols), lambda q, t, c_ref: (q, c_ref[0], t, 0)), spec], out_specs=spec)
    return pl.pallas_call(body, name="grad_sum_sibling", grid_spec=grid_spec,
                          out_shape=jax.ShapeDtypeStruct((nchip, rows, cols), BF16),
                          compiler_params=_params("parallel", "parallel"))(lax.axis_index("c").reshape(1), parts, theirs)


def _sum_final(chip_sum, got):
    _, rows, cols = chip_sum.shape
    tile = rows // SUM_SPLIT

    def body(q_ref, a_ref, g_ref, o_ref):
        o_ref[...] = ((a_ref[0].astype(F32) + g_ref[0].astype(F32)) + g_ref[1].astype(F32)) + g_ref[2].astype(F32)

    grid_spec = pltpu.PrefetchScalarGridSpec(
        num_scalar_prefetch=1, grid=(SUM_SPLIT,),
        in_specs=[pl.BlockSpec((1, tile, cols), lambda t, q_ref: (q_ref[0], t, 0)),
                  pl.BlockSpec((3, tile, cols), lambda t, q_ref: (0, t, 0))],
        out_specs=pl.BlockSpec((tile, cols), lambda t, q_ref: (t, 0)))
    my_chip = (2 * lax.axis_index("x") + lax.axis_index("y")).reshape(1)
    return pl.pallas_call(body, name="grad_sum_chips", grid_spec=grid_spec, out_shape=jax.ShapeDtypeStruct((rows, cols), F32),
                          compiler_params=_params("parallel"))(my_chip, chip_sum, got)


class _ReduceScatter:
    def __init__(self, name, slabs):
        self.name, self.rows = name, slabs.shape[1]
        parts = slabs.reshape(4, 2, self.rows, D_MODEL)
        self.first = _copies_start(name + "_sibling_start", [parts, lax.empty((4, self.rows, D_MODEL), slabs.dtype)],
                                   _to_sibling_copies, 4)
        self.token = self.first[3]

    def between_chips(self, after):
        parts, theirs = _copies_wait(self.name + "_sibling_wait", self.first, _to_sibling_copies, after)
        chip_sum = _sum_pairs(parts, theirs)
        self.second = _copies_start(self.name + "_chips_start", [chip_sum, lax.empty((3, self.rows, D_MODEL), chip_sum.dtype)],
                                    _to_chips_copies, 3)
        return self.second[3]

    def finish(self, after):
        chip_sum, got = _copies_wait(self.name + "_chips_wait", self.second, _to_chips_copies, after)
        return _sum_final(chip_sum, got)


def _ada_fwd(c_all, w, b):
    nb, _ = c_all.shape
    ncol = w.shape[1]

    def body(c_ref, w_ref, b_ref, o_ref):
        c = c_ref[...]
        act = (c * _sigmoid(c)).astype(BF16)
        o_ref[...] = jnp.dot(act, w_ref[...].astype(BF16), preferred_element_type=F32) + b_ref[...]

    return pl.pallas_call(body, name="ada_fwd", out_shape=jax.ShapeDtypeStruct((nb, ncol), F32),
                          compiler_params=pltpu.CompilerParams(vmem_limit_bytes=VMEM_LIMIT_BYTES))(c_all, w, b)


def _ada_wgrad(c_all_t, dmod_cols):
    d, nb = c_all_t.shape
    ncol = dmod_cols.shape[1]

    def body(ct_ref, dm_ref, o_ref):
        ct = ct_ref[...]
        act = (ct * _sigmoid(ct)).astype(BF16).astype(F32)
        dm = dm_ref[...].astype(BF16).astype(F32)
        acc = act[:, 0:1] * dm[0:1, :]
        for i in range(1, nb):
            acc = acc + act[:, i:i + 1] * dm[i:i + 1, :]
        o_ref[...] = acc

    return pl.pallas_call(body, name="ada_wgrad", out_shape=jax.ShapeDtypeStruct((d, ncol), F32),
                          compiler_params=pltpu.CompilerParams(vmem_limit_bytes=VMEM_LIMIT_BYTES))(c_all_t, dmod_cols)


SMALL_ROWS = 24


def _reduce_small(gathered):
    def body(g_ref, o_ref):
        acc = g_ref[0]
        for dev in range(1, N_DEV):
            acc = acc + g_ref[dev]
        o_ref[...] = acc

    return pl.pallas_call(body, name="reduce_small", out_shape=jax.ShapeDtypeStruct(gathered.shape[1:], F32))(gathered)


def _adamw_math(w, g, m, v):
    nm = ADAM_B1 * m + (1.0 - ADAM_B1) * g
    nv = ADAM_B2 * v + (1.0 - ADAM_B2) * (g * g)
    bc1 = 1.0 - ADAM_B1 ** ADAM_STEP
    bc2 = 1.0 - ADAM_B2 ** ADAM_STEP
    return -ADAM_LR * ((nm / bc1) / (jnp.sqrt(nv / bc2) + ADAM_EPS) + ADAM_WD * w), nm, nv


def _adamw_small(ws, gs, ms, vs, name):
    n = len(ws)

    def body(*refs):
        for i in range(n):
            res = _adamw_math(*(refs[k * n + i][...] for k in range(4)))
            for k in range(3):
                refs[(4 + k) * n + i][...] = res[k]

    shapes = [jax.ShapeDtypeStruct(w.shape, F32) for w in ws]
    res = pl.pallas_call(body, name=name, out_shape=shapes * 3)(*ws, *gs, *ms, *vs)
    return [(res[i], res[n + i], res[2 * n + i]) for i in range(n)]


def _adamw(w, g, m, v, name, token=None):
    rows, cols = w.shape
    tile = rows
    for cand in range(min(rows // 2, 512) // 8 * 8, 7, -8):
        if rows % cand == 0:
            tile = cand
            break
    spec = pl.BlockSpec((tile, cols), lambda t: (t, 0))

    def body(w_ref, g_ref, m_ref, v_ref, *refs):
        d_ref, nm_ref, nv_ref = refs[-3:]
        d_ref[...], nm_ref[...], nv_ref[...] = _adamw_math(w_ref[...], g_ref[...], m_ref[...], v_ref[...])

    shp = jax.ShapeDtypeStruct((rows, cols), F32)
    follows = [] if token is None else [token]
    return pl.pallas_call(body, name=name, grid=(rows // tile,),
                          in_specs=[spec] * 4 + [pl.BlockSpec(t.shape, lambda t_: (0, 0)) for t in follows],
                          out_specs=[spec] * 3, out_shape=[shp] * 3, compiler_params=_params("parallel"))(w, g, m, v, *follows)


_WEIGHTS = ("w_ada", "b_ada", "w_in", "sinks", "w_branch_a", "w_branch_b", "w_o", "ln1_g", "ln1_b", "w_gate_up", "w_down",
            "ln2_g", "ln2_b")
_TRANSPOSED = ("w_in", "w_branch_b", "w_gate_up")


def _pack_shard(name, w):
    w = w.astype(BF16)
    if name in _TRANSPOSED:
        w = w.T
    return w.reshape(-1, D_MODEL)


def _unpack_full(name, slab):
    if name == "w_branch_b":
        return slab.reshape(N_DEV * 128, 512)
    return slab.reshape(-1, D_MODEL)


def _unpack_group(group, gathered):
    return {n: _unpack_full(n, slab) for (n, _), slab in zip(group, gathered)}


def _unpack_grads(group, g_packed):
    g_w, off = {}, 0
    for n, r in group:
        part = g_packed[off:off + r]
        off += r
        g_w[n] = part.reshape(128, 512) if n == "w_branch_b" else part
    return g_w


def kernel(x, c, positions, w_ada, b_ada, w_in, sinks, w_branch_a, w_branch_b, w_o, ln1_g, ln1_b, w_gate_up, w_down, ln2_g, ln2_b, loss_target, m_w_ada, m_b_ada, m_w_in, m_sinks, m_w_branch_a, m_w_branch_b, m_w_o, m_ln1_g, m_ln1_b, m_w_gate_up, m_w_down, m_ln2_g, m_ln2_b, v_w_ada, v_b_ada, v_w_in, v_sinks, v_w_branch_a, v_w_branch_b, v_w_o, v_ln1_g, v_ln1_b, v_w_gate_up, v_w_down, v_ln2_g, v_ln2_b):
    weights = dict(w_ada=w_ada, b_ada=b_ada, w_in=w_in, sinks=sinks, w_branch_a=w_branch_a, w_branch_b=w_branch_b, w_o=w_o,
                   ln1_g=ln1_g, ln1_b=ln1_b, w_gate_up=w_gate_up, w_down=w_down, ln2_g=ln2_g, ln2_b=ln2_b)
    m_in = dict(w_ada=m_w_ada, b_ada=m_b_ada, w_in=m_w_in, sinks=m_sinks, w_branch_a=m_w_branch_a, w_branch_b=m_w_branch_b,
                w_o=m_w_o, ln1_g=m_ln1_g, ln1_b=m_ln1_b, w_gate_up=m_w_gate_up, w_down=m_w_down, ln2_g=m_ln2_g, ln2_b=m_ln2_b)
    v_in = dict(w_ada=v_w_ada, b_ada=v_b_ada, w_in=v_w_in, sinks=v_sinks, w_branch_a=v_w_branch_a, w_branch_b=v_w_branch_b,
                w_o=v_w_o, ln1_g=v_ln1_g, ln1_b=v_ln1_b, w_gate_up=v_w_gate_up, w_down=v_w_down, ln2_g=v_ln2_g, ln2_b=v_ln2_b)
    bsz = x.shape[0]
    me = _index(_my_place())
    ada_cols = w_ada.shape[2]
    outs = {}

    def adamw(n, g, token=None):
        w2, m2, v2 = (t[n][0] if t[n].ndim == 3 else t[n] for t in (weights, m_in, v_in))
        shape = weights[n].shape
        if n in _TRANSPOSED:
            dlt, nm, nv = _adamw(w2.T, g, m2.T, v2.T, "adamw_" + n, token)
            outs[n] = tuple(t.T.reshape(shape) for t in (g, dlt, nm, nv))
        else:
            dlt, nm, nv = _adamw(w2, g, m2, v2, "adamw_" + n, token)
            outs[n] = tuple(t.reshape(shape) for t in (g, dlt, nm, nv))
        return nv

    packed_in = [_pack_shard(n, weights[n][0]) for n, _ in GROUP_IN]
    packed_rest = [_pack_shard(n, weights[n][0]) for n, _ in GROUP_REST]
    c_all = _gather_small(jnp.pad(c, ((0, 8 - bsz), (0, 0))), "gather_c")[:, :bsz].reshape(N_DEV * bsz, D_MODEL)
    gather_in = _Gather("gather_w_in", lax.optimization_barrier((packed_in, c_all))[0])
    b_cols = lax.dynamic_slice_in_dim(b_ada, me * ada_cols, ada_cols, axis=1)
    mod_cols = _ada_fwd(c_all, w_ada[0], b_cols + gather_in.token[0, 0])
    tables = _rope_tables(positions)
    mod_cols, tables, packed_rest = lax.optimization_barrier((mod_cols, tables, packed_rest))
    mod_all = _gather_small(mod_cols, "gather_mod").transpose(1, 0, 2).reshape(N_DEV * bsz, 6, D_MODEL)
    gather_rest = _Gather("gather_rest", lax.optimization_barrier((packed_rest, mod_all))[0])
    mod = jnp.pad(lax.dynamic_slice_in_dim(mod_all, me * bsz, bsz, axis=0), ((0, 0), (0, 2), (0, 0)))
    mod = mod + gather_rest.token[0, 0]
    mod = mod + gather_in.pass_on(mod)[0, 0]

    scatters, rest_grads = {}, {}

    def get_w_in(after):
        return _unpack_group(GROUP_IN, gather_in.finish(after))["w_in"]

    def get_rest(after):
        return _unpack_group(GROUP_REST, gather_rest.finish(after))

    def pack_grads(group, grads):
        return jnp.concatenate([grads[n].reshape(N_DEV, r, D_MODEL) for n, r in group], axis=1)

    def hook(point, value):
        if point == "projected":
            return gather_rest.pass_on(value)
        if point == "grads_rest":
            scatters["rest"] = _ReduceScatter("scatter_rest", pack_grads(GROUP_REST, value))
            return scatters["rest"].token
        if point == "delta_done":
            return scatters["rest"].between_chips(value)
        if point == "grads_w_in":
            scatters["in"] = _ReduceScatter("scatter_w_in", pack_grads(GROUP_IN, value))
            rest_grads.update(_unpack_grads(GROUP_REST, scatters["rest"].finish(scatters["in"].token)))
            return scatters["in"].between_chips(lax.optimization_barrier(tuple(rest_grads.values()))[0])
        if point == "dgrad_done":
            return None
        raise ValueError(point)

    loss_part, grad_x, dmod, small = _layer_step(x, mod, tables, sinks[0], ln1_g, ln1_b, ln2_g, ln2_b, loss_target,
                                                 get_w_in, get_rest, hook)

    rows = jnp.concatenate([dmod.reshape(bsz * 6, D_MODEL), small, jnp.full((1, D_MODEL), loss_part, F32),
                            jnp.zeros((SMALL_ROWS - bsz * 6 - 6, D_MODEL), F32)], axis=0)
    land = lax.dynamic_update_slice(lax.empty((N_DEV,) + rows.shape, rows.dtype), rows[None], (me, 0, 0))
    gather_small = _copies_start("gather_small_start", [rows, land], _to_all_copies, N_DEV - 1)
    follow = gather_small[3]
    for n, g in rest_grads.items():
        follow = adamw(n, g, follow)[:8, :LANES]
    small_all = _copies_wait("gather_small_wait", gather_small, _to_all_copies, follow)[1]
    sums = _reduce_small(small_all)
    loss = sums[bsz * 6 + 5, 0]
    dmod_all = small_all[:, :bsz * 6].reshape(N_DEV * bsz, 6 * D_MODEL)
    small_g = {"b_ada": functools.reduce(jnp.add, [sums[6 * i:6 * i + 6] for i in range(bsz)]).reshape(1, 6 * D_MODEL),
               "sinks": sums[bsz * 6 + 4][::HEAD_DIM][None]}
    small_g.update({n: sums[bsz * 6 + i][None] for i, n in enumerate(("ln1_g", "ln1_b", "ln2_g", "ln2_b"))})
    names = list(small_g)
    for n, (dlt, nm, nv) in zip(names, _adamw_small([weights[n] for n in names], [small_g[n] for n in names],
                                                     [m_in[n] for n in names], [v_in[n] for n in names], "adamw_small")):
        outs[n] = (small_g[n], dlt, nm, nv)
    dmod_cols = lax.dynamic_slice_in_dim(dmod_all, me * ada_cols, ada_cols, axis=1)
    adamw("w_ada", _ada_wgrad(c_all.T, dmod_cols))
    done = lax.optimization_barrier(tuple(outs[n][3] for n in outs))
    for n, g in _unpack_grads(GROUP_IN, scatters["in"].finish(done[0])).items():
        adamw(n, g)

    return (loss, grad_x, *[outs[n][0] for n in _WEIGHTS], *[outs[n][1] for n in _WEIGHTS], *[outs[n][2] for n in _WEIGHTS],
            *[outs[n][3] for n in _WEIGHTS])
```

```python
import functools

import jax
import jax.numpy as jnp
from jax import lax
from jax.experimental import pallas as pl
from jax.experimental.pallas import tpu as pltpu

F32 = jnp.float32
BF16 = jnp.bfloat16

D_MODEL = 1024
HEAD_DIM = 64
A_Q_HEADS = 16
A_WINDOW = 128
B_PATTERNS = ((128, 1), (512, 4), (2048, 16))
B_HEADS_PER_GROUP = 8
D_FF = 2816
QBLOCK = 128
ROPE_THETA = 10000.0
LN_EPS = 1e-5
DEEPNORM_ALPHA = 2.0 ** 0.25
NEG_INF = -1e30
ADAM_LR, ADAM_B1, ADAM_B2, ADAM_EPS, ADAM_WD, ADAM_STEP = 0.001, 0.9, 0.999, 1e-08, 0.01, 10

N_DEV = 8
LANES = 128
VMEM_LIMIT_BYTES = 56 * 1024 * 1024
MESH = pl.DeviceIdType.MESH

OFF_QA, OFF_KVA, OFF_QKVB, OFF_GAB = 0, 1024, 1280, 5888
GROUP_IN = (("w_in", 992),)
GROUP_REST = (("w_branch_a", 128), ("w_branch_b", 64), ("w_o", 128), ("w_gate_up", 704), ("w_down", 352))


def _params(*sem):
    return pltpu.CompilerParams(dimension_semantics=sem, vmem_limit_bytes=VMEM_LIMIT_BYTES)


def _sigmoid(x):
    return 1.0 / (1.0 + jnp.exp(-x))


_DIMS = {"nn": (((1,), (0,)), ((), ())), "nt": (((1,), (1,)), ((), ())), "tn": (((0,), (0,)), ((), ()))}


def _matmul(a, b, *, mode, tm, tn, tk, name, out_dtype=None, n=None, b_off=0, token=None, ins=(), outs=None, epilogue=None,
            lhs_fn=None):
    if mode == "nn":
        (m, k), nn_ = a.shape, b.shape[1]
    elif mode == "nt":
        (m, k), nn_ = a.shape, (b.shape[0] if n is None else n)
    else:
        (k, m), nn_ = a.shape, b.shape[1]
    assert m % tm == 0 and nn_ % tn == 0 and k % tk == 0 and b_off % tn == 0, (name, m, nn_, k)
    nk = k // tk
    joff = b_off // tn
    if mode == "nn":
        a_spec = pl.BlockSpec((tm, tk), lambda i, j, kk: (i, kk))
        b_spec = pl.BlockSpec((tk, tn), lambda i, j, kk: (kk, j))
    elif mode == "nt":
        a_spec = pl.BlockSpec((tm, tk), lambda i, j, kk: (i, kk))
        b_spec = pl.BlockSpec((tn, tk), lambda i, j, kk: (j + joff, kk))
    else:
        a_spec = pl.BlockSpec((tk, tm), lambda i, j, kk: (kk, i))
        b_spec = pl.BlockSpec((tk, tn), lambda i, j, kk: (kk, j))
    dims = _DIMS[mode]
    has_token = token is not None
    plain = epilogue is None
    if plain:
        outs = [(jax.ShapeDtypeStruct((m, nn_), out_dtype), (tm, tn), lambda i, j: (i, j))]

        def epilogue(acc, i, j, in_refs, out_refs):
            out_refs[0][...] = acc.astype(out_refs[0].dtype)

    nin = len(ins)
    nscratch = 1 if lhs_fn is None else 2
    assert lhs_fn is None or nk == 1

    def body(*refs):
        a_ref, b_ref = refs[:2]
        in_refs = refs[2:2 + nin]
        out_refs = refs[2 + nin + has_token:-nscratch]
        acc_ref = refs[-nscratch]
        kk = pl.program_id(2)
        if lhs_fn is None:
            lhs = a_ref[...].astype(BF16)
        else:
            lhs_ref = refs[-1]

            @pl.when(pl.program_id(1) == 0)
            def _():
                lhs_ref[...] = lhs_fn(a_ref, in_refs, out_refs)

            lhs = lhs_ref[...]
        part = lax.dot_general(lhs, b_ref[...].astype(BF16), dims, preferred_element_type=F32)

        def finish(acc):
            epilogue(acc, pl.program_id(0), pl.program_id(1), in_refs, out_refs)

        if nk == 1:
            finish(part)
        else:
            @pl.when(kk == 0)
            def _():
                acc_ref[...] = part

            @pl.when(kk > 0)
            def _():
                acc_ref[...] += part

            @pl.when(kk == nk - 1)
            def _():
                finish(acc_ref[...])

    def spec(block, index):
        return pl.BlockSpec(block, lambda i, j, kk: index(i, j))

    in_specs, args = [a_spec, b_spec], [a, b]
    for arr, block, index in ins:
        in_specs.append(spec(block, index))
        args.append(arr)
    if has_token:
        in_specs.append(pl.BlockSpec(token.shape, lambda i, j, kk: (0, 0)))
        args.append(token)
    res = pl.pallas_call(
        body,
        name=name,
        grid=(m // tm, nn_ // tn, nk),
        in_specs=in_specs,
        out_specs=[spec(block, index) for _, block, index in outs],
        out_shape=[shape for shape, _, _ in outs],
        scratch_shapes=[pltpu.VMEM((tm, tn) if nk > 1 else (8, LANES), F32)] + ([] if lhs_fn is None else [pltpu.VMEM((tm, tk), BF16)]),
        compiler_params=_params("arbitrary", "arbitrary", "arbitrary"),
    )(*args)
    return res[0] if plain else res


def _proj_rope(a, bt, cos, sin, *, n, b_off, rope_cols, tm, tn, name, out_dtype=F32):
    m, k = a.shape
    assert m % tm == 0 and n % tn == 0 and b_off % tn == 0 and rope_cols % LANES == 0, name
    joff = b_off // tn
    nrope, part = divmod(rope_cols, tn)

    def body(a_ref, b_ref, c_ref, s_ref, o_ref):
        acc = lax.dot_general(a_ref[...], b_ref[...], _DIMS["nt"], preferred_element_type=F32)
        j = pl.program_id(1)

        @pl.when(j < nrope)
        def _():
            o_ref[...] = _rope(acc, c_ref[...], s_ref[...], coarse=True).astype(o_ref.dtype)

        if part:
            @pl.when(j == nrope)
            def _():
                o_ref[:, :part] = _rope(acc[:, :part], c_ref[...], s_ref[...], coarse=True).astype(o_ref.dtype)
                o_ref[:, part:] = acc[:, part:].astype(o_ref.dtype)

        @pl.when(j >= nrope + (1 if part else 0))
        def _():
            o_ref[...] = acc.astype(o_ref.dtype)

    table = pl.BlockSpec((tm, LANES), lambda i, j: (i, 0))
    return pl.pallas_call(
        body,
        name=name,
        grid=(m // tm, n // tn),
        in_specs=[pl.BlockSpec((tm, k), lambda i, j: (i, 0)), pl.BlockSpec((tn, k), lambda i, j: (j + joff, 0)), table, table],
        out_specs=pl.BlockSpec((tm, tn), lambda i, j: (i, j)),
        out_shape=jax.ShapeDtypeStruct((m, n), out_dtype),
        compiler_params=_params("parallel", "parallel"),
    )(a, bt, cos, sin)


ROW_TILE = 256


def _rows(width, col=0):
    return pl.BlockSpec((1, ROW_TILE, width), lambda b, t: (b, t, col))


def _per_batch(nrows, width):
    return pl.BlockSpec((1, nrows, width), lambda b, t: (b, 0, 0))


def _row_call(body, name, bsz, seq, in_specs, out_specs, out_shape, accumulates=False):
    return pl.pallas_call(
        body,
        name=name,
        grid=(bsz, seq // ROW_TILE),
        in_specs=in_specs,
        out_specs=out_specs,
        out_shape=out_shape,
        compiler_params=_params("parallel", "arbitrary" if accumulates else "parallel"),
    )


def _acc_rows(acc_ref, first, rows):
    @pl.when(first)
    def _():
        acc_ref[...] = jnp.zeros_like(acc_ref)

    for r, val in enumerate(rows):
        acc_ref[0, r:r + 1, :] += val


def _colsum(v):
    return jnp.sum(v, axis=0, keepdims=True)


def _ln_stats(z):
    mu = jnp.mean(z, axis=-1, keepdims=True)
    zc = z - mu
    var = jnp.mean(zc * zc, axis=-1, keepdims=True)
    rstd = lax.rsqrt(var + LN_EPS)
    return zc * rstd, rstd


def _ln_bwd(dxhat, xhat, rstd):
    m1 = jnp.mean(dxhat, axis=-1, keepdims=True)
    m2 = jnp.mean(dxhat * xhat, axis=-1, keepdims=True)
    return rstd * (dxhat - m1 - xhat * m2)


def _modulate_in(x, mod):
    bsz, seq, d = x.shape

    def body(x_ref, mod_ref, u_ref):
        u_ref[0] = (x_ref[0] * (1.0 + mod_ref[0, 1:2, :]) + mod_ref[0, 0:1, :]).astype(BF16)

    return _row_call(body, "modulate_in", bsz, seq, [_rows(d), _per_batch(8, d)], _rows(d),
                     jax.ShapeDtypeStruct((bsz, seq, d), BF16))(x, mod)


EP_TILE = 512


def _ep_specs(seq, d):
    tiles = seq // EP_TILE
    return ((EP_TILE, d), lambda i, j: (i, 0)), ((1, 8, d), lambda i, j: (i // tiles, 0, 0)), ((1, d), lambda i, j: (0, 0))


def _wo_ln1(merged, wo, x, mod, g, b, seq):
    ntok, d = x.shape
    row, per_b, whole = _ep_specs(seq, d)

    def epilogue(y, i, j, ins, outs):
        x_ref, mod_ref, g_ref, b_ref = ins
        y_ref, h_ref, u_ref = outs
        z = DEEPNORM_ALPHA * x_ref[...] + (1.0 + mod_ref[0, 2:3, :]) * y
        xhat, _ = _ln_stats(z)
        h = xhat * g_ref[...] + b_ref[...]
        y_ref[...] = y
        h_ref[...] = h
        u_ref[...] = (h * (1.0 + mod_ref[0, 4:5, :]) + mod_ref[0, 3:4, :]).astype(BF16)

    f32, bf16 = jax.ShapeDtypeStruct((ntok, d), F32), jax.ShapeDtypeStruct((ntok, d), BF16)
    return _matmul(merged, wo, mode="nn", tm=EP_TILE, tn=d, tk=d, name="w_o_ln1",
                   ins=[(x,) + row, (mod,) + per_b, (g,) + whole, (b,) + whole],
                   outs=[(f32,) + row, (f32,) + row, (bf16,) + row], epilogue=epilogue)


FF_HALF = D_FF // 2


def _interleave_gate_up(w):
    return w.reshape(2, 2, FF_HALF, w.shape[1]).transpose(1, 0, 2, 3).reshape(w.shape)


def _gate_up_silu(u2, wgut_i):
    ntok = u2.shape[0]

    def epilogue(h, i, j, ins, outs):
        h_ref, a_ref = outs
        hg, hu = h[:, :FF_HALF], h[:, FF_HALF:]
        h_ref[...] = h.astype(BF16)
        a_ref[...] = (hg * _sigmoid(hg) * hu).astype(BF16)

    return _matmul(u2, wgut_i, mode="nt", tm=EP_TILE, tn=2 * FF_HALF, tk=u2.shape[1], name="gate_up_silu",
                   outs=[(jax.ShapeDtypeStruct((ntok, 2 * D_FF), BF16), (EP_TILE, 2 * FF_HALF), lambda i, j: (i, j)),
                         (jax.ShapeDtypeStruct((ntok, D_FF), BF16), (EP_TILE, FF_HALF), lambda i, j: (i, j))],
                   epilogue=epilogue)


def _down_dgrad_silu_bwd(dy2, wd, h_i):
    ntok = dy2.shape[0]
    wide = ((EP_TILE, 2 * FF_HALF), lambda i, j: (i, j))

    def epilogue(da, i, j, ins, outs):
        h = ins[0][...].astype(F32)
        hg, hu = h[:, :FF_HALF], h[:, FF_HALF:]
        sg = _sigmoid(hg)
        outs[0][:, :FF_HALF] = (da * hu * (sg * (1.0 + hg * (1.0 - sg)))).astype(BF16)
        outs[0][:, FF_HALF:] = (da * (hg * sg)).astype(BF16)

    return _matmul(dy2, wd, mode="nt", tm=EP_TILE, tn=FF_HALF, tk=dy2.shape[1], name="down_dgrad_silu_bwd",
                   ins=[(h_i,) + wide], outs=[(jax.ShapeDtypeStruct((ntok, 2 * D_FF), BF16),) + wide], epilogue=epilogue)[0]


def _down_ln2_loss_bwd(a, wd, h1, mod, g, b, target, seq):
    ntok, d = h1.shape
    row, per_b, whole = _ep_specs(seq, d)
    tiles = seq // EP_TILE

    def epilogue(y, i, j, ins, outs):
        h_ref, mod_ref, g_ref, b_ref, t_ref = ins
        dy_ref, dh_ref, acc_ref = outs
        gate = 1.0 + mod_ref[0, 5:6, :]
        z = DEEPNORM_ALPHA * h_ref[...] + gate * y
        xhat, rstd = _ln_stats(z)
        diff = xhat * g_ref[...] + b_ref[...] - t_ref[...]
        loss = 0.5 * jnp.sum(jnp.sum(diff * diff, axis=-1, keepdims=True) / d, axis=0, keepdims=True)
        dout = diff / d
        dz = _ln_bwd(dout * g_ref[...], xhat, rstd)
        dy_ref[...] = (gate * dz).astype(BF16)
        dh_ref[...] = DEEPNORM_ALPHA * dz
        _acc_rows(acc_ref, i % tiles == 0,
                  [_colsum(dout * xhat), _colsum(dout), _colsum(dz * y), jnp.broadcast_to(loss, (1, d))])

    return _matmul(a, wd, mode="nn", tm=EP_TILE, tn=d, tk=a.shape[1], name="down_ln2_loss_bwd",
                   ins=[(h1,) + row, (mod,) + per_b, (g,) + whole, (b,) + whole, (target,) + row],
                   outs=[(jax.ShapeDtypeStruct((ntok, d), BF16),) + row, (jax.ShapeDtypeStruct((ntok, d), F32),) + row,
                         (jax.ShapeDtypeStruct((ntok // seq, 8, d), F32),) + per_b], epilogue=epilogue)


def _gate_up_dgrad_ln1_bwd(dh, wgut, dh1a, x, y1, mod, g, b, seq):
    ntok, d = x.shape
    row, per_b, whole = _ep_specs(seq, d)
    tiles = seq // EP_TILE

    def epilogue(du, i, j, ins, outs):
        dh_ref, x_ref, y_ref, mod_ref, g_ref, b_ref = ins
        dy_ref, dx_ref, acc_ref = outs
        y = y_ref[...]
        gate = 1.0 + mod_ref[0, 2:3, :]
        z = DEEPNORM_ALPHA * x_ref[...] + gate * y
        xhat, rstd = _ln_stats(z)
        h1 = xhat * g_ref[...] + b_ref[...]
        dh1 = dh_ref[...] + du * (1.0 + mod_ref[0, 4:5, :])
        dz = _ln_bwd(dh1 * g_ref[...], xhat, rstd)
        dy_ref[...] = (gate * dz).astype(BF16)
        dx_ref[...] = DEEPNORM_ALPHA * dz
        _acc_rows(acc_ref, i % tiles == 0,
                  [_colsum(dh1 * xhat), _colsum(dh1), _colsum(dz * y), _colsum(du * h1), _colsum(du)])

    return _matmul(dh, wgut, mode="nn", tm=EP_TILE, tn=d, tk=D_FF, name="gate_up_dgrad_ln1_bwd",
                   ins=[(dh1a,) + row, (x,) + row, (y1,) + row, (mod,) + per_b, (g,) + whole, (b,) + whole],
                   outs=[(jax.ShapeDtypeStruct((ntok, d), BF16),) + row, (jax.ShapeDtypeStruct((ntok, d), F32),) + row,
                         (jax.ShapeDtypeStruct((ntok // seq, 8, d), F32),) + per_b], epilogue=epilogue)


def _wo_dgrad_gate_bwd(dy1, wo, gab, ya, yb):
    ntok, d = ya.shape
    tm, tn = 1024, 512
    tile = ((tm, tn), lambda i, j: (i, j))
    tile_b = ((tm, tn), lambda i, j: (i, j + d // tn))

    def epilogue(dm_, i, j, ins, outs):
        ga_ref, gb_ref, ya_ref, yb_ref = ins
        dya_ref, dyb_ref, dga_ref, dgb_ref = outs
        sa, sb = _sigmoid(ga_ref[...].astype(F32)), _sigmoid(gb_ref[...].astype(F32))
        dya_ref[...] = (dm_ * sa).astype(BF16)
        dyb_ref[...] = (dm_ * sb).astype(BF16)
        dga_ref[...] = (dm_ * ya_ref[...].astype(F32) * sa * (1.0 - sa)).astype(BF16)
        dgb_ref[...] = (dm_ * yb_ref[...].astype(F32) * sb * (1.0 - sb)).astype(BF16)

    shp = jax.ShapeDtypeStruct((ntok, d), BF16)
    return _matmul(dy1, wo, mode="nt", tm=tm, tn=tn, tk=d, name="w_o_dgrad_gate_bwd",
                   ins=[(gab,) + tile, (gab,) + tile_b, (ya,) + tile, (yb,) + tile],
                   outs=[(shp,) + tile] * 4, epilogue=epilogue)


def _w_in_dgrad_grad_x(dproj, wint, dxa, x, mod, seq, token):
    ntok, d = x.shape
    row, per_b, _ = _ep_specs(seq, d)
    tiles = seq // EP_TILE

    def epilogue(du, i, j, ins, outs):
        dxa_ref, x_ref, mod_ref = ins
        gx_ref, acc_ref = outs
        gx_ref[...] = dxa_ref[...] + du * (1.0 + mod_ref[0, 1:2, :])
        _acc_rows(acc_ref, i % tiles == 0, [_colsum(du * x_ref[...]), _colsum(du)])

    return _matmul(dproj, wint, mode="nn", tm=EP_TILE, tn=d, tk=wint.shape[0] // 2, name="w_in_dgrad_grad_x", token=token,
                   ins=[(dxa,) + row, (x,) + row, (mod,) + per_b],
                   outs=[(jax.ShapeDtypeStruct((ntok, d), F32),) + row, (jax.ShapeDtypeStruct((ntok // seq, 8, d), F32),) + per_b],
                   epilogue=epilogue)


def _merge_branch_b_gate(os_, ls_, wbbt, gab, ya):
    ntok, d = ya.shape
    w = os_[0].shape[1]
    tm, tn = 1024, 512
    tile = ((tm, tn), lambda i, j: (i, j))
    tile_b = ((tm, tn), lambda i, j: (i, j + d // tn))
    row = ((tm, w), lambda i, j: (i, 0))

    def lhs_fn(o0_ref, ins, outs):
        os_r, ls_r = (o0_ref,) + tuple(ins[3:5]), ins[5:8]
        ls = [l[...] for l in ls_r]
        mx = jnp.maximum(jnp.maximum(ls[0], ls[1]), ls[2])
        es = [jnp.exp(l - mx) for l in ls]
        den = es[0] + es[1] + es[2]
        ob = functools.reduce(jnp.add, [(e / den) * o[...].astype(F32) for e, o in zip(es, os_r)]).astype(BF16)
        outs[2][...] = ob
        return ob

    def epilogue(yb, i, j, ins, outs):
        ga_ref, gb_ref, ya_ref = ins[:3]
        yb_ref, merged_ref = outs[:2]
        yb_ref[...] = yb.astype(BF16)
        merged_ref[...] = (_sigmoid(ga_ref[...].astype(F32)) * ya_ref[...].astype(F32)
                           + _sigmoid(gb_ref[...].astype(F32)) * yb).astype(BF16)

    shp = jax.ShapeDtypeStruct((ntok, d), BF16)
    return _matmul(os_[0], wbbt, mode="nt", tm=tm, tn=tn, tk=w, name="merge_branch_b_gate", lhs_fn=lhs_fn,
                   ins=[(gab,) + tile, (gab,) + tile_b, (ya,) + tile] + [(v,) + row for v in list(os_[1:]) + list(ls_)],
                   outs=[(shp,) + tile] * 2 + [(jax.ShapeDtypeStruct((ntok, w), BF16),) + row], epilogue=epilogue)


def _segsum64(v):
    rows, width = v.shape
    ri = lax.broadcasted_iota(jnp.int32, (LANES, LANES), 0) // HEAD_DIM
    ci = lax.broadcasted_iota(jnp.int32, (LANES, LANES), 1) // HEAD_DIM
    ones = jnp.where(ri == ci, 1.0, 0.0).astype(BF16)
    out = []
    for c in range(width // LANES):
        part = v[:, c * LANES:(c + 1) * LANES]
        hi = part.astype(BF16)
        lo = (part - hi.astype(F32)).astype(BF16)
        out.append(jnp.dot(hi, ones, preferred_element_type=F32) + jnp.dot(lo, ones, preferred_element_type=F32))
    return jnp.concatenate(out, axis=1) if len(out) > 1 else out[0]


def _branch_b_dgrad_merge_bwd(dyb, wbbt, os_, ls_):
    ntok, w = os_[0].shape
    row = ((EP_TILE, w), lambda i, j: (i, 0))

    def epilogue(dob_, i, j, ins, outs):
        os_r, ls_r = ins[:3], ins[3:]
        do_r, dd_r = outs[:3], outs[3:]
        ls = [l[...] for l in ls_r]
        mx = jnp.maximum(jnp.maximum(ls[0], ls[1]), ls[2])
        es = [jnp.exp(l - mx) for l in ls]
        den = es[0] + es[1] + es[2]
        ws = [e / den for e in es]
        dws = [_segsum64(dob_ * o[...].astype(F32)) for o in os_r]
        mean = ws[0] * dws[0] + ws[1] * dws[1] + ws[2] * dws[2]
        for wg, do_ref, dd_ref in zip(ws, do_r, dd_r):
            do_ref[...] = wg * dob_
            dd_ref[...] = -wg * mean

    shp = jax.ShapeDtypeStruct((ntok, w), F32)
    return _matmul(dyb, wbbt, mode="nn", tm=EP_TILE, tn=w, tk=dyb.shape[1], name="branch_b_dgrad_merge_bwd",
                   ins=[(v,) + row for v in list(os_) + list(ls_)], outs=[(shp,) + row] * 6, epilogue=epilogue)


def _branch_a_dgrad_delta(dya, wba, oa, lse_a, sinks_exp, seq):
    ntok, w = oa.shape
    row, per_b, whole = _ep_specs(seq, w)
    tiles = seq // EP_TILE

    def epilogue(do_, i, j, ins, outs):
        o_ref, l_ref, s_ref = ins
        do_ref, dd_ref, acc_ref = outs
        dd = -_segsum64(do_ * o_ref[...].astype(F32))
        do_ref[...] = do_.astype(BF16)
        dd_ref[...] = dd
        _acc_rows(acc_ref, i % tiles == 0, [_colsum(dd * jnp.exp(s_ref[...] - l_ref[...]))])

    shp = jax.ShapeDtypeStruct((ntok, w), F32)
    return _matmul(dya, wba, mode="nt", tm=EP_TILE, tn=w, tk=dya.shape[1], name="branch_a_dgrad_delta",
                   ins=[(oa,) + row, (lse_a,) + row, (sinks_exp,) + whole],
                   outs=[(jax.ShapeDtypeStruct((ntok, w), BF16),) + row, (shp,) + row,
                         (jax.ShapeDtypeStruct((ntok // seq, 8, w), F32),) + per_b],
                   epilogue=epilogue)


def _swap_halves(v):
    src = lax.broadcasted_iota(jnp.int32, (LANES, LANES), 0)
    dst = lax.broadcasted_iota(jnp.int32, (LANES, LANES), 1)
    partner = jnp.where((dst % HEAD_DIM) < HEAD_DIM // 2, dst + HEAD_DIM // 2, dst - HEAD_DIM // 2)
    perm = jnp.where(src == partner, 1.0, 0.0).astype(BF16)
    hi = v.astype(BF16)
    lo = (v - hi.astype(F32)).astype(BF16)
    return jnp.dot(hi, perm, preferred_element_type=F32) + jnp.dot(lo, perm, preferred_element_type=F32)


def _swap_halves_roll(v):
    lane = lax.broadcasted_iota(jnp.int32, v.shape, 1)
    return jnp.where((lane % HEAD_DIM) < HEAD_DIM // 2, pltpu.roll(v, LANES - HEAD_DIM // 2, 1),
                     pltpu.roll(v, HEAD_DIM // 2, 1))


def _swap_halves_coarse(v):
    src = lax.broadcasted_iota(jnp.int32, (LANES, LANES), 0)
    dst = lax.broadcasted_iota(jnp.int32, (LANES, LANES), 1)
    partner = jnp.where((dst % HEAD_DIM) < HEAD_DIM // 2, dst + HEAD_DIM // 2, dst - HEAD_DIM // 2)
    perm = jnp.where(src == partner, 1.0, 0.0).astype(BF16)
    return jnp.dot(v.astype(BF16), perm, preferred_element_type=F32)


def _rope(v, cos, sin, sign=1.0, mxu=True, coarse=False):
    swap = (_swap_halves_coarse if coarse else _swap_halves) if mxu else _swap_halves_roll
    out = []
    for c in range(v.shape[1] // LANES):
        part = v[:, c * LANES:(c + 1) * LANES]
        out.append(part * cos + sign * (swap(part) * sin))
    return jnp.concatenate(out, axis=1) if len(out) > 1 else out[0]


def _half_mask(shape, half):
    lane = lax.broadcasted_iota(jnp.int32, shape, len(shape) - 1) % LANES
    return (lane < HEAD_DIM) if half == 0 else (lane >= HEAD_DIM)


def _dup_half(v, half):
    return jnp.where(_half_mask(v.shape, half), v, pltpu.roll(v, HEAD_DIM, 1))


def _fold_halves(v):
    return v + pltpu.roll(v, HEAD_DIM, 1)


def _pick_halves(lo_rows, hi_rows):
    return jnp.where(_half_mask(lo_rows.shape, 0), lo_rows, hi_rows)


def _stack_masked(v, pairs):
    parts = []
    for c in pairs:
        pair = v[:, c * LANES:(c + 1) * LANES]
        parts += [jnp.where(_half_mask(pair.shape, half), pair, 0.0) for half in (0, 1)]
    return jnp.concatenate(parts, axis=0)


def _stack_pair_cols(v, pairs):
    return jnp.concatenate([v[:, c * LANES + half * HEAD_DIM:c * LANES + half * HEAD_DIM + 1] for c in pairs for half in (0, 1)],
                           axis=0)


ATTN_UNITS = 16


def _class_rows(r):
    return [pl.ds(0, QBLOCK)] if r == 1 else [pl.ds(rho, QBLOCK, stride=r) for rho in range(r)]


def _band_mask(nrows, nk, blk, n_back, has_prev):
    qi = lax.broadcasted_iota(jnp.int32, (nrows, nk), 0) % QBLOCK
    ki = lax.broadcasted_iota(jnp.int32, (nrows, nk), 1)
    if has_prev:
        dist = qi + QBLOCK - ki
        return (dist >= 0) & (dist <= n_back) & ((ki >= QBLOCK) | (blk > 0))
    dist = qi - ki
    return (dist >= 0) & (dist <= n_back)


def _attn_fwd(q_arr, k_arr, v_arr, *, name, npair, gqa, q_col, k_col, v_col, nchunk, r, n_back, sinks=None):
    bsz, seq, _ = q_arr.shape
    rr = QBLOCK * r
    nblk = seq // rr
    qw = npair * LANES
    kw = LANES if gqa else qw
    has_prev = nblk > 1
    has_sink = sinks is not None
    scale = HEAD_DIM ** -0.5

    def body(*refs):
        refs = list(refs)
        q_ref, kc_ref, vc_ref = refs[:3]
        pos = 3
        if has_prev:
            kp_ref, vp_ref = refs[pos:pos + 2]
            pos += 2
        if has_sink:
            sink_ref = refs[pos]
            pos += 1
        o_ref, lse_ref = refs[pos:pos + 2]
        if r > 1:
            stage_o = refs[pos + 2]
        blk = pl.program_id(2)
        nk = (2 if has_prev else 1) * QBLOCK
        valid = _band_mask(QBLOCK, nk, blk, n_back, has_prev)
        per = npair // 2
        classes = _class_rows(r)
        step = max(1, ATTN_UNITS // (2 * npair))
        for first in range(0, len(classes), step):
            batch = classes[first:first + step]
            units = []
            for ci, rows in enumerate(batch):
                q = q_ref[0, rows, :] * scale
                k, v = kc_ref[0, rows, :], vc_ref[0, rows, :]
                if has_prev:
                    k = jnp.concatenate([kp_ref[0, rows, :], k], axis=0)
                    v = jnp.concatenate([vp_ref[0, rows, :], v], axis=0)
                if gqa:
                    kdup = [_dup_half(k, hk).astype(BF16) for hk in range(2)]
                    vdup = [_dup_half(v, hk) for hk in range(2)]
                for c in range(npair):
                    sl = slice(c * LANES, (c + 1) * LANES)
                    qc = q[:, sl]
                    kc, vc = (kdup[c // per], vdup[c // per]) if gqa else (k[:, sl].astype(BF16), v[:, sl])
                    for half in (0, 1):
                        qm = jnp.where(_half_mask(qc.shape, half), qc, 0.0).astype(BF16)
                        vm = jnp.where(_half_mask(vc.shape, half), vc, 0.0).astype(BF16)
                        s = lax.dot_general(qm, kc, _DIMS["nt"], preferred_element_type=F32)
                        units.append(dict(ci=ci, c=c, half=half, s=s, vm=vm, sk=sink_ref[2 * c + half] if has_sink else None))
            for u in units:
                s = jnp.where(valid, u["s"], NEG_INF)
                m = jnp.max(s, axis=1, keepdims=True)
                if has_sink:
                    m = jnp.maximum(m, u["sk"])
                p = jnp.exp(s - m)
                den = jnp.sum(p, axis=1, keepdims=True)
                if has_sink:
                    den = den + jnp.exp(u["sk"] - m)
                u.update(p=p.astype(BF16), den=den, lse=m + jnp.log(den))
            for u in units:
                u["o"] = jnp.dot(u["p"], u["vm"], preferred_element_type=F32) / u["den"]
            for ci, rows in enumerate(batch):
                outs, lses = [None] * npair, [None] * npair
                for u in units:
                    if u["ci"] != ci:
                        continue
                    c, o = u["c"], u["o"]
                    lse = jnp.broadcast_to(u["lse"], o.shape)
                    outs[c] = o if u["half"] == 0 else outs[c] + o
                    lses[c] = lse if u["half"] == 0 else _pick_halves(lses[c], lse)
                o_new = jnp.concatenate(outs, axis=1) if npair > 1 else outs[0]
                if r > 1:
                    stage_o[rows, :] = o_new
                else:
                    o_ref[0] = o_new.astype(BF16)
                lse_ref[0, rows, :] = jnp.concatenate(lses, axis=1) if npair > 1 else lses[0]
        if r > 1:
            o_ref[0] = stage_o[...].astype(BF16)

    def cur(width, col0):
        return pl.BlockSpec((1, rr, width), lambda b, c, i: (b, i, col0 + c))

    def prev(width, col0):
        return pl.BlockSpec((1, rr, width), lambda b, c, i: (b, jnp.maximum(i - 1, 0), col0 + c))

    in_specs = [cur(qw, q_col), cur(kw, k_col), cur(kw, v_col)]
    args = [q_arr, k_arr, v_arr]
    if has_prev:
        in_specs += [prev(kw, k_col), prev(kw, v_col)]
        args += [k_arr, v_arr]
    if has_sink:
        in_specs.append(pl.BlockSpec(memory_space=pltpu.SMEM))
        args.append(sinks)
    return pl.pallas_call(
        body,
        name=name,
        grid=(bsz, nchunk, nblk),
        in_specs=in_specs,
        out_specs=[pl.BlockSpec((1, rr, qw), lambda b, c, i: (b, i, c))] * 2,
        out_shape=[jax.ShapeDtypeStruct((bsz, seq, nchunk * qw), BF16), jax.ShapeDtypeStruct((bsz, seq, nchunk * qw), F32)],
        scratch_shapes=[pltpu.VMEM((rr, qw), F32)] if r > 1 else [],
        compiler_params=_params("parallel", "parallel", "parallel"),
    )(*args)


def _attn_bwd(q_arr, k_arr, v_arr, cos, sin, do, lse, dd, *, name, npair, gqa, q_col, k_col, v_col, nchunk, r, n_back,
              token=None):
    bsz, seq, _ = q_arr.shape
    rr = QBLOCK * r
    nblk = seq // rr
    qw = npair * LANES
    kw = LANES if gqa else qw
    has_next = nblk > 1
    has_token = token is not None
    staged = r > 1
    scale = HEAD_DIM ** -0.5

    def body(*refs):
        refs = list(refs)
        k_ref, v_ref, c_ref, s_ref = refs[:4]
        tile_refs = [refs[4:8]]
        pos = 8
        if has_next:
            tile_refs.append(refs[pos:pos + 4])
            pos += 4
        if has_token:
            pos += 1
        dq_ref, dk_ref, dv_ref = refs[pos:pos + 3]
        carry_ref, bcast_ref = refs[pos + 3:pos + 5]
        if staged:
            stage_q, stage_k, stage_v = refs[pos + 5:pos + 8]
        blk = pl.program_id(2)
        if has_next:
            @pl.when(blk == 0)
            def _():
                carry_ref[...] = jnp.zeros_like(carry_ref)

        nrows = (npair if gqa else 1) * QBLOCK
        qi = lax.broadcasted_iota(jnp.int32, (nrows, QBLOCK), 0) % QBLOCK
        ki = lax.broadcasted_iota(jnp.int32, (nrows, QBLOCK), 1)
        valids = [qi >= ki, (qi + QBLOCK - ki <= n_back) & (blk + 1 < nblk)]
        per = npair // 2
        ntile = len(tile_refs)
        cat = lambda parts: jnp.concatenate(parts, axis=1) if len(parts) > 1 else parts[0]
        classes = _class_rows(r)
        step = max(1, ATTN_UNITS // (ntile * (2 if gqa else 2 * npair)))
        def stat_cols(stat, slot):
            if gqa:
                return _stack_pair_cols(stat, list(range(slot * per, (slot + 1) * per)))
            col = slot * HEAD_DIM
            return stat[:, col:col + 1]

        nslot = 2 if gqa else 2 * npair
        if has_next:
            @pl.when(blk == 0)
            def _():
                for rows in classes:
                    for which, stat_ref in enumerate(tile_refs[0][2:4]):
                        stat = stat_ref[0, rows, :]
                        for slot in range(nslot):
                            bcast_ref[which, slot, rows if not gqa else slice(None), :] = jnp.broadcast_to(
                                stat_cols(stat, slot), (nrows, LANES))

        for first in range(0, len(classes), step):
            batch = classes[first:first + step]
            units = []
            for ci, rows in enumerate(batch):
                keep = slice(None) if gqa else rows
                tiles = [(q_ref[0, rows, :] * scale, do_ref[0, rows, :], l_ref[0, rows, :], d_ref[0, rows, :])
                         for q_ref, do_ref, l_ref, d_ref in tile_refs]

                def stats(t, slot, keep=keep, tiles=tiles):
                    if has_next and t == 0:
                        return bcast_ref[0, slot, keep, :], bcast_ref[1, slot, keep, :]
                    return tuple(jnp.broadcast_to(stat_cols(tiles[t][2 + w], slot), (nrows, LANES)) for w in range(2))

                k, v = k_ref[0, rows, :], v_ref[0, rows, :]
                if gqa:
                    for hk in range(2):
                        pairs = list(range(hk * per, (hk + 1) * per))
                        kd, vd = _dup_half(k, hk).astype(BF16), _dup_half(v, hk).astype(BF16)
                        for t, (q, do_, l_, d_) in enumerate(tiles):
                            lcol, dcol = stats(t, hk)
                            units.append(dict(ci=ci, t=t, hk=hk, slot=hk, keep=keep, pairs=pairs,
                                              qs=_stack_masked(q, pairs).astype(BF16),
                                              dos=_stack_masked(do_, pairs).astype(BF16), lcol=lcol, dcol=dcol,
                                              kmat=kd, vmat=vd, kdq=kd))
                else:
                    for c in range(npair):
                        sl = slice(c * LANES, (c + 1) * LANES)
                        kc, vcb = k[:, sl], v[:, sl].astype(BF16)
                        kcb = kc.astype(BF16)
                        for t, (q, do_, l_, d_) in enumerate(tiles):
                            for half in (0, 1):
                                hm = _half_mask(kc.shape, half)
                                lcol, dcol = stats(t, 2 * c + half)
                                units.append(dict(ci=ci, t=t, c=c, half=half, slot=2 * c + half, keep=keep,
                                                  qs=jnp.where(hm, q[:, sl], 0.0).astype(BF16),
                                                  dos=jnp.where(hm, do_[:, sl], 0.0).astype(BF16), lcol=lcol, dcol=dcol,
                                                  kmat=kcb, vmat=vcb, kdq=jnp.where(hm, kc, 0.0).astype(BF16)))
            for u in units:
                u["s"] = lax.dot_general(u["qs"], u["kmat"], _DIMS["nt"], preferred_element_type=F32)
                u["dp"] = lax.dot_general(u["dos"], u["vmat"], _DIMS["nt"], preferred_element_type=F32)
            for u in units:
                p = jnp.exp(jnp.where(valids[u["t"]], u["s"], NEG_INF) - u["lcol"])
                u["ds"] = (p * (u["dp"] + u["dcol"])).astype(BF16)
                u["p"] = p.astype(BF16)
            for u in units:
                u["dv"] = lax.dot_general(u["p"], u["dos"], _DIMS["tn"], preferred_element_type=F32)
                u["dk"] = lax.dot_general(u["ds"], u["qs"], _DIMS["tn"], preferred_element_type=F32)
                u["dq"] = jnp.dot(u["ds"], u["kdq"], preferred_element_type=F32) * scale
            for u in units:
                if u["t"] == 1:
                    bcast_ref[0, u["slot"], u["keep"], :] = u["lcol"]
                    bcast_ref[1, u["slot"], u["keep"], :] = u["dcol"]
            for ci, rows in enumerate(batch):
                mine = [u for u in units if u["ci"] == ci]
                dq = [[None] * npair for _ in range(ntile)]
                if gqa:
                    dk_out = dv_out = None
                    for hk in range(2):
                        us = [u for u in mine if u["hk"] == hk]
                        for u in us:
                            for i, c in enumerate(u["pairs"]):
                                dq[u["t"]][c] = _pick_halves(u["dq"][2 * i * QBLOCK:(2 * i + 1) * QBLOCK],
                                                             u["dq"][(2 * i + 1) * QBLOCK:(2 * i + 2) * QBLOCK])
                        dk_h = _fold_halves(functools.reduce(jnp.add, [u["dk"] for u in us]))
                        dv_h = _fold_halves(functools.reduce(jnp.add, [u["dv"] for u in us]))
                        dk_out = dk_h if hk == 0 else _pick_halves(dk_out, dk_h)
                        dv_out = dv_h if hk == 0 else _pick_halves(dv_out, dv_h)
                else:
                    dks, dvs = [], []
                    for c in range(npair):
                        us = [u for u in mine if u["c"] == c]
                        dks.append(functools.reduce(jnp.add, [u["dk"] for u in us]))
                        dvs.append(functools.reduce(jnp.add, [u["dv"] for u in us]))
                        for t in range(ntile):
                            dq[t][c] = functools.reduce(jnp.add, [u["dq"] for u in us if u["t"] == t])
                    dk_out, dv_out = cat(dks), cat(dvs)
                ck, sk_ = c_ref[0, rows, :], s_ref[0, rows, :]
                dk_new = _rope(dk_out, ck, sk_, sign=-1.0, mxu=gqa, coarse=True)
                dq_cur = cat(dq[0])
                if has_next:
                    dq_cur = dq_cur + carry_ref[rows, :]
                    carry_ref[rows, :] = cat(dq[1])
                dq_new = _rope(dq_cur, ck, sk_, sign=-1.0, mxu=gqa, coarse=True)
                if staged:
                    stage_q[rows, :], stage_k[rows, :], stage_v[rows, :] = dq_new, dk_new, dv_out
                else:
                    dq_ref[0], dk_ref[0], dv_ref[0] = dq_new.astype(BF16), dk_new.astype(BF16), dv_out.astype(BF16)
        if staged:
            dq_ref[0], dk_ref[0], dv_ref[0] = stage_q[...].astype(BF16), stage_k[...].astype(BF16), stage_v[...].astype(BF16)

    def at(width, col0, shift):
        return pl.BlockSpec((1, rr, width), lambda b, c, i: (b, jnp.minimum(i + shift, nblk - 1), col0 + c))

    in_specs = [at(kw, k_col, 0), at(kw, v_col, 0), pl.BlockSpec((1, rr, LANES), lambda b, c, i: (b, i, 0)),
                pl.BlockSpec((1, rr, LANES), lambda b, c, i: (b, i, 0))]
    args = [k_arr, v_arr, cos, sin]
    for shift in (0, 1) if has_next else (0,):
        in_specs += [at(qw, q_col, shift), at(qw, 0, shift), at(qw, 0, shift), at(qw, 0, shift)]
        args += [q_arr, do, lse, dd]
    if has_token:
        in_specs.append(pl.BlockSpec(token.shape, lambda b, c, i: (0, 0)))
        args.append(token)
    return pl.pallas_call(
        body,
        name=name,
        grid=(bsz, nchunk, nblk),
        in_specs=in_specs,
        out_specs=[pl.BlockSpec((1, rr, qw), lambda b, c, i: (b, i, c)),
                   pl.BlockSpec((1, rr, kw), lambda b, c, i: (b, i, c)),
                   pl.BlockSpec((1, rr, kw), lambda b, c, i: (b, i, c))],
        out_shape=[jax.ShapeDtypeStruct((bsz, seq, nchunk * qw), BF16),
                   jax.ShapeDtypeStruct((bsz, seq, nchunk * kw), BF16),
                   jax.ShapeDtypeStruct((bsz, seq, nchunk * kw), BF16)],
        scratch_shapes=[pltpu.VMEM((rr, qw) if has_next else (8, LANES), F32),
                        pltpu.VMEM((2, 2 if gqa else 2 * npair, npair * QBLOCK if gqa else rr, LANES) if has_next
                                   else (1, 1, 8, LANES), F32)] +
                       ([pltpu.VMEM((rr, qw), F32), pltpu.VMEM((rr, kw), F32), pltpu.VMEM((rr, kw), F32)] if staged else []),
        compiler_params=_params("parallel", "parallel", "arbitrary"),
    )(*args)


B_CHUNKS = {1: (4, 1), 4: (1, 4), 16: (1, 4)}


def _rope_tables(positions):
    half = HEAD_DIM // 2
    inv = ROPE_THETA ** (-jnp.arange(half, dtype=F32) / half)
    ang = positions.astype(F32)[..., None] * inv
    cos, sin = jnp.cos(ang), jnp.sin(ang)
    return jnp.concatenate([cos] * 4, axis=-1), jnp.concatenate([-sin, sin, -sin, sin], axis=-1)


def _layer_step(x, mod, tables, sinks, ln1_g, ln1_b, ln2_g, ln2_b, target, get_w_in, get_rest, hook):
    bsz, seq, d = x.shape
    ntok = bsz * seq
    flat = lambda v: v.reshape(ntok, v.shape[-1])
    unflat = lambda v: v.reshape(bsz, seq, v.shape[-1])
    cos, sin = tables
    mm = functools.partial(_matmul, tm=1024, tk=1024)
    scalar = lambda tok: 0.0 if tok is None else tok[0, 0]

    u1 = _modulate_in(x, mod)
    u1f = flat(u1)
    wint = get_w_in(u1)
    cosf, sinf = flat(cos), flat(sin)
    proj = functools.partial(_proj_rope, u1f, wint, cosf, sinf, tm=2048)
    qkvb = unflat(proj(n=4608, b_off=OFF_QKVB, rope_cols=3072, tn=256, name="proj_qkvb"))
    b_kws, os_, ls_ = [], [], []
    for g, (window, r) in enumerate(B_PATTERNS):
        npair, nch = B_CHUNKS[r]
        per = B_HEADS_PER_GROUP // (2 * npair)
        nsec = len(B_PATTERNS) * per
        kw_ = dict(npair=npair, gqa=False, q_col=g * per, k_col=nsec + g * per, v_col=2 * nsec + g * per, nchunk=nch, r=r,
                   n_back=window // r)
        b_kws.append(kw_)
        o_g, l_g = _attn_fwd(qkvb, qkvb, qkvb, name=f"attn_b{g}_fwd", **kw_)
        os_.append(o_g)
        ls_.append(l_g)
    tok = hook("projected", os_[-1])
    proj = functools.partial(_proj_rope, u1f, wint, cosf + scalar(tok), sinf, tm=2048)
    gab = unflat(proj(n=2048, b_off=OFF_GAB, rope_cols=0, tn=256, name="proj_gab", out_dtype=BF16))
    qa = kva = unflat(proj(n=OFF_QKVB, b_off=OFF_QA, rope_cols=OFF_KVA + LANES, tn=256, name="proj_qkva", out_dtype=BF16))
    a_kw = dict(npair=A_Q_HEADS // 2, gqa=True, q_col=0, k_col=OFF_KVA // LANES, v_col=OFF_KVA // LANES + 1, nchunk=1, r=1,
                n_back=A_WINDOW - 1)
    after_gab = jnp.minimum(jnp.abs(gab[0, 0, 0].astype(F32)), 0.0)
    oa, lse_a = _attn_fwd(qa, kva, kva, name="attn_a_fwd", sinks=sinks.reshape(A_Q_HEADS) + after_gab, **a_kw)
    rest = get_rest(oa)
    wba, wbbt, wo, wgut, wd = (rest[n] for n in ("w_branch_a", "w_branch_b", "w_o", "w_gate_up", "w_down"))
    ya = unflat(mm(flat(oa), wba, mode="nn", out_dtype=BF16, tn=512, name="branch_a"))
    ybf, mergedf, obf = _merge_branch_b_gate([flat(t) for t in os_], [flat(t) for t in ls_], wbbt, flat(gab), flat(ya))
    xf = flat(x)
    y1f, h1f, u2f = _wo_ln1(mergedf, wo, xf, mod, ln1_g, ln1_b, seq)
    wgut_i = _interleave_gate_up(wgut)
    hf, af = _gate_up_silu(u2f, wgut_i)

    dy2f, dh1af, acc2 = _down_ln2_loss_bwd(af, wd, h1f, mod, ln2_g, ln2_b, flat(target), seq)
    g_wd = _matmul(af, dy2f, mode="tn", out_dtype=BF16, tm=256, tn=1024, tk=ntok, name="down_wgrad")
    dhf = _down_dgrad_silu_bwd(dy2f, wd, hf)
    g_wgut = _interleave_gate_up(_matmul(dhf, u2f, mode="tn", out_dtype=BF16, tm=256, tn=1024, tk=ntok, name="gate_up_wgrad"))
    dy1f, dxaf, acc1 = _gate_up_dgrad_ln1_bwd(dhf, wgut_i, dh1af, xf, y1f, mod, ln1_g, ln1_b, seq)
    g_wo = _matmul(mergedf, dy1f, mode="tn", out_dtype=BF16, tm=256, tn=1024, tk=ntok, name="w_o_wgrad")
    dyaf, dybf, dgaf, dgbf = _wo_dgrad_gate_bwd(dy1f, wo, flat(gab), flat(ya), ybf)
    g_wba = _matmul(flat(oa), dyaf, mode="tn", out_dtype=BF16, tm=256, tn=1024, tk=ntok, name="branch_a_wgrad")
    g_wbbt = _matmul(dybf, obf, mode="tn", out_dtype=BF16, tm=256, tn=512, tk=ntok, name="branch_b_wgrad")
    tok = hook("grads_rest", dict(w_branch_a=g_wba, w_branch_b=g_wbbt, w_o=g_wo, w_gate_up=g_wgut, w_down=g_wd))

    sinks_exp = jnp.repeat(sinks.reshape(1, A_Q_HEADS), HEAD_DIM, axis=1) + scalar(tok)
    doa, dd_a, acc_s = _branch_a_dgrad_delta(dyaf, wba, flat(oa), flat(lse_a), sinks_exp, seq)
    doa, dd_a = unflat(doa), unflat(dd_a)
    tok = hook("delta_done", dd_a)
    dqa, dka, dva = _attn_bwd(qa, kva, kva, cos, sin, doa, lse_a, dd_a, name="attn_a_bwd", token=tok, **a_kw)
    merged_bwd = [unflat(t) for t in _branch_b_dgrad_merge_bwd(dybf, wbbt, [flat(t) for t in os_], [flat(t) for t in ls_])]
    dqs, dks, dvs = [], [], []
    for g in range(len(B_PATTERNS)):
        dq_g, dk_g, dv_g = _attn_bwd(qkvb, qkvb, qkvb, cos, sin, merged_bwd[g], ls_[g], merged_bwd[3 + g],
                                     name=f"attn_b{g}_bwd", **b_kws[g])
        dqs.append(dq_g)
        dks.append(dk_g)
        dvs.append(dv_g)
    dproj = jnp.concatenate([t.astype(BF16) for t in [dqa, dka, dva] + dqs + dks + dvs] + [unflat(dgaf), unflat(dgbf)], axis=-1)
    dprojf = flat(dproj)
    g_wint = _matmul(dprojf, u1f, mode="tn", out_dtype=BF16, tm=256, tn=1024, tk=ntok, name="w_in_wgrad")
    tok = hook("grads_w_in", dict(w_in=g_wint))
    grad_x, acc0 = _w_in_dgrad_grad_x(dprojf, wint, dxaf, xf, mod, seq, tok)
    grad_x = unflat(grad_x)
    tok = hook("dgrad_done", grad_x)

    loss_part = jnp.sum(acc2[:, 3, 0])
    dmod = jnp.stack([acc0[:, 1], acc0[:, 0], acc1[:, 2], acc1[:, 4], acc1[:, 3], acc2[:, 2]], axis=1)
    small = jnp.stack([acc1[:, 0].sum(0), acc1[:, 1].sum(0), acc2[:, 0].sum(0), acc2[:, 1].sum(0), acc_s[:, 0].sum(0)])
    small = small + scalar(tok)
    return loss_part, grad_x, dmod, small


CHIP_FLIPS = (2, 4, 6)


def _my_place():
    return lax.axis_index("x"), lax.axis_index("y"), lax.axis_index("c")


def _flip(place, k):
    px, py, pc = place
    return (1 - px if k & 4 else px, 1 - py if k & 2 else py, 1 - pc if k & 1 else pc)


def _index(place):
    return 4 * place[0] + 2 * place[1] + place[2]


def _gather_small(v, name):
    rows, cols = v.shape

    def body(v_ref, out_ref, send_sems, recv_sems):
        me = _my_place()
        out_ref[_index(me)] = v_ref[...]
        copies = []
        for k in range(1, N_DEV):
            copies.append(pltpu.make_async_remote_copy(
                src_ref=v_ref, dst_ref=out_ref.at[_index(me)], send_sem=send_sems.at[k - 1], recv_sem=recv_sems.at[k - 1],
                device_id=_flip(me, k), device_id_type=MESH))
        for cp in copies:
            cp.start()
        for k in range(1, N_DEV):
            pltpu.make_async_remote_copy(
                src_ref=v_ref, dst_ref=out_ref.at[_index(_flip(me, k))], send_sem=send_sems.at[k - 1],
                recv_sem=recv_sems.at[k - 1], device_id=_flip(me, k), device_id_type=MESH).wait_recv()
        for cp in copies:
            cp.wait_send()

    return pl.pallas_call(
        body,
        name=name,
        out_shape=jax.ShapeDtypeStruct((N_DEV, rows, cols), v.dtype),
        in_specs=[pl.BlockSpec(memory_space=pltpu.VMEM)],
        out_specs=pl.BlockSpec(memory_space=pltpu.VMEM),
        scratch_shapes=[pltpu.SemaphoreType.DMA((N_DEV - 1,)), pltpu.SemaphoreType.DMA((N_DEV - 1,))],
        compiler_params=pltpu.CompilerParams(vmem_limit_bytes=VMEM_LIMIT_BYTES),
    )(v)


_HBM = pl.BlockSpec(memory_space=pltpu.HBM)
_SEM = pl.BlockSpec(memory_space=pltpu.SEMAPHORE)
_EFFECT = pltpu.SideEffectType.DATAFLOW_SIDE_EFFECTING


def _remote(src, dst, send_sems, recv_sems, j, to):
    return pltpu.make_async_remote_copy(src_ref=src, dst_ref=dst, send_sem=send_sems.at[j], recv_sem=recv_sems.at[j],
                                        device_id=to, device_id_type=MESH)


def _copies_start(name, bufs, make_copies, nsem):
    nbuf = len(bufs)

    def body(*refs):
        for cp in make_copies(refs[:nbuf], refs[nbuf], refs[nbuf + 1]):
            cp.start()
        refs[-1][...] = jnp.zeros_like(refs[-1])

    sems = pltpu.SemaphoreType.DMA((nsem,))
    res = pl.pallas_call(
        body, name=name,
        out_shape=(sems, sems, *[pltpu.HBM(v.shape, v.dtype) for v in bufs], jax.ShapeDtypeStruct((8, LANES), F32)),
        in_specs=(_HBM,) * nbuf, out_specs=(_SEM, _SEM) + (_HBM,) * nbuf + (pl.BlockSpec(memory_space=pltpu.VMEM),),
        input_output_aliases={i: 2 + i for i in range(nbuf)},
        compiler_params=pltpu.CompilerParams(has_side_effects=_EFFECT),
    )(*[pltpu.with_memory_space_constraint(v, pltpu.HBM) for v in bufs])
    return res[0], res[1], list(res[2:2 + nbuf]), res[-1]


def _copies_wait(name, started, make_copies, after):
    send_sems, recv_sems, bufs, _ = started
    nbuf = len(bufs)

    def body(*refs):
        for cp in make_copies(refs[:nbuf], refs[nbuf], refs[nbuf + 1]):
            cp.wait_send()
            cp.wait_recv()

    return list(pl.pallas_call(
        body, name=name,
        out_shape=tuple(pltpu.HBM(v.shape, v.dtype) for v in bufs),
        in_specs=(_HBM,) * nbuf + (_SEM, _SEM, pl.BlockSpec(memory_space=pl.ANY)), out_specs=(_HBM,) * nbuf,
        input_output_aliases={i: i for i in range(nbuf)},
        compiler_params=pltpu.CompilerParams(has_side_effects=_EFFECT),
    )(*bufs, send_sems, recv_sems, after))


def _to_sibling_copies(refs, send_sems, recv_sems):
    src_ref, land_ref = refs
    me = _my_place()
    return [_remote(src_ref.at[q, 1 - me[2]], land_ref.at[q], send_sems, recv_sems, q, _flip(me, 1)) for q in range(4)]


def _to_chips_copies(refs, send_sems, recv_sems):
    src_ref, land_ref = refs
    me = _my_place()
    copies = []
    for j, k in enumerate(CHIP_FLIPS):
        to = _flip(me, k)
        copies.append(_remote(src_ref.at[2 * to[0] + to[1]], land_ref.at[j], send_sems, recv_sems, j, to))
    return copies


class _Gather:
    def __init__(self, name, blocks):
        self.name, self.n = name, len(blocks)
        at_me = (_index(_my_place()), 0, 0)
        lands = [lax.dynamic_update_slice(lax.empty((N_DEV,) + v.shape, v.dtype), v[None], at_me) for v in blocks]
        self.first = _copies_start(name + "_start", list(blocks) + lands, self._first_copies, 4 * self.n)
        self.token = self.first[3]

    def _first_copies(self, refs, send_sems, recv_sems):
        me = _my_place()
        return [_remote(refs[w], refs[self.n + w].at[_index(me)], send_sems, recv_sems, 4 * w + j, _flip(me, k))
                for w in range(self.n) for j, k in enumerate((1,) + CHIP_FLIPS)]

    def _pass_copies(self, refs, send_sems, recv_sems):
        me = _my_place()
        copies = []
        for w, land in enumerate(refs):
            for j, k in enumerate(CHIP_FLIPS):
                slot = land.at[_index(_flip(me, k))]
                copies.append(_remote(slot, slot, send_sems, recv_sems, 3 * w + j, _flip(me, 1)))
        return copies

    def pass_on(self, after):
        lands = _copies_wait(self.name + "_wait", self.first, self._first_copies, after)[self.n:]
        self.second = _copies_start(self.name + "_pass_start", lands, self._pass_copies, 3 * self.n)
        return self.second[3]

    def finish(self, after):
        return _copies_wait(self.name + "_pass_wait", self.second, self._pass_copies, after)


def _to_all_copies(refs, send_sems, recv_sems):
    src_ref, land_ref = refs
    me = _my_place()
    return [_remote(src_ref, land_ref.at[_index(me)], send_sems, recv_sems, k - 1, _flip(me, k)) for k in range(1, N_DEV)]


SUM_SPLIT = 2


def _sum_pairs(parts, theirs):
    nchip, _, rows, cols = parts.shape
    tile = rows // SUM_SPLIT

    def body(c_ref, a_ref, b_ref, o_ref):
        o_ref[...] = (a_ref[0].astype(F32) + b_ref[...].astype(F32)).astype(BF16)

    spec = pl.BlockSpec((1, tile, cols), lambda q, t, c_ref: (q, t, 0))
    grid_spec = pltpu.PrefetchScalarGridSpec(
        num_scalar_prefetch=1, grid=(nchip, SUM_SPLIT),
        in_specs=[pl.BlockSpec((1, 1, tile, cols), lambda q, t, c_ref: (q, c_ref[0], t, 0)), spec], out_specs=spec)
    return pl.pallas_call(body, name="grad_sum_sibling", grid_spec=grid_spec,
                          out_shape=jax.ShapeDtypeStruct((nchip, rows, cols), BF16),
                          compiler_params=_params("parallel", "parallel"))(lax.axis_index("c").reshape(1), parts, theirs)


def _sum_final(chip_sum, got):
    _, rows, cols = chip_sum.shape
    tile = rows // SUM_SPLIT

    def body(q_ref, a_ref, g_ref, o_ref):
        o_ref[...] = ((a_ref[0].astype(F32) + g_ref[0].astype(F32)) + g_ref[1].astype(F32)) + g_ref[2].astype(F32)

    grid_spec = pltpu.PrefetchScalarGridSpec(
        num_scalar_prefetch=1, grid=(SUM_SPLIT,),
        in_specs=[pl.BlockSpec((1, tile, cols), lambda t, q_ref: (q_ref[0], t, 0)),
                  pl.BlockSpec((3, tile, cols), lambda t, q_ref: (0, t, 0))],
        out_specs=pl.BlockSpec((tile, cols), lambda t, q_ref: (t, 0)))
    my_chip = (2 * lax.axis_index("x") + lax.axis_index("y")).reshape(1)
    return pl.pallas_call(body, name="grad_sum_chips", grid_spec=grid_spec, out_shape=jax.ShapeDtypeStruct((rows, cols), F32),
                          compiler_params=_params("parallel"))(my_chip, chip_sum, got)


class _ReduceScatter:
    def __init__(self, name, slabs):
        self.name, self.rows = name, slabs.shape[1]
        parts = slabs.reshape(4, 2, self.rows, D_MODEL)
        self.first = _copies_start(name + "_sibling_start", [parts, lax.empty((4, self.rows, D_MODEL), slabs.dtype)],
                                   _to_sibling_copies, 4)
        self.token = self.first[3]

    def between_chips(self, after):
        parts, theirs = _copies_wait(self.name + "_sibling_wait", self.first, _to_sibling_copies, after)
        chip_sum = _sum_pairs(parts, theirs)
        self.second = _copies_start(self.name + "_chips_start", [chip_sum, lax.empty((3, self.rows, D_MODEL), chip_sum.dtype)],
                                    _to_chips_copies, 3)
        return self.second[3]

    def finish(self, after):
        chip_sum, got = _copies_wait(self.name + "_chips_wait", self.second, _to_chips_copies, after)
        return _sum_final(chip_sum, got)


def _ada_fwd(c_all, w, b):
    nb, _ = c_all.shape
    ncol = w.shape[1]

    def body(c_ref, w_ref, b_ref, o_ref):
        c = c_ref[...]
        act = (c * _sigmoid(c)).astype(BF16)
        o_ref[...] = jnp.dot(act, w_ref[...].astype(BF16), preferred_element_type=F32) + b_ref[...]

    return pl.pallas_call(body, name="ada_fwd", out_shape=jax.ShapeDtypeStruct((nb, ncol), F32),
                          compiler_params=pltpu.CompilerParams(vmem_limit_bytes=VMEM_LIMIT_BYTES))(c_all, w, b)


def _ada_wgrad(c_all_t, dmod_cols):
    d, nb = c_all_t.shape
    ncol = dmod_cols.shape[1]

    def body(ct_ref, dm_ref, o_ref):
        ct = ct_ref[...]
        act = (ct * _sigmoid(ct)).astype(BF16).astype(F32)
        dm = dm_ref[...].astype(BF16).astype(F32)
        acc = act[:, 0:1] * dm[0:1, :]
        for i in range(1, nb):
            acc = acc + act[:, i:i + 1] * dm[i:i + 1, :]
        o_ref[...] = acc

    return pl.pallas_call(body, name="ada_wgrad", out_shape=jax.ShapeDtypeStruct((d, ncol), F32),
                          compiler_params=pltpu.CompilerParams(vmem_limit_bytes=VMEM_LIMIT_BYTES))(c_all_t, dmod_cols)


SMALL_ROWS = 24


def _reduce_small(gathered):
    def body(g_ref, o_ref):
        acc = g_ref[0]
        for dev in range(1, N_DEV):
            acc = acc + g_ref[dev]
        o_ref[...] = acc

    return pl.pallas_call(body, name="reduce_small", out_shape=jax.ShapeDtypeStruct(gathered.shape[1:], F32))(gathered)


def _adamw_math(w, g, m, v):
    nm = ADAM_B1 * m + (1.0 - ADAM_B1) * g
    nv = ADAM_B2 * v + (1.0 - ADAM_B2) * (g * g)
    bc1 = 1.0 - ADAM_B1 ** ADAM_STEP
    bc2 = 1.0 - ADAM_B2 ** ADAM_STEP
    return -ADAM_LR * ((nm / bc1) / (jnp.sqrt(nv / bc2) + ADAM_EPS) + ADAM_WD * w), nm, nv


def _adamw_small(ws, gs, ms, vs, name):
    n = len(ws)

    def body(*refs):
        for i in range(n):
            res = _adamw_math(*(refs[k * n + i][...] for k in range(4)))
            for k in range(3):
                refs[(4 + k) * n + i][...] = res[k]

    shapes = [jax.ShapeDtypeStruct(w.shape, F32) for w in ws]
    res = pl.pallas_call(body, name=name, out_shape=shapes * 3)(*ws, *gs, *ms, *vs)
    return [(res[i], res[n + i], res[2 * n + i]) for i in range(n)]


def _adamw(w, g, m, v, name, token=None):
    rows, cols = w.shape
    tile = rows
    for cand in range(min(rows // 2, 512) // 8 * 8, 7, -8):
        if rows % cand == 0:
            tile = cand
            break
    spec = pl.BlockSpec((tile, cols), lambda t: (t, 0))

    def body(w_ref, g_ref, m_ref, v_ref, *refs):
        d_ref, nm_ref, nv_ref = refs[-3:]
        d_ref[...], nm_ref[...], nv_ref[...] = _adamw_math(w_ref[...], g_ref[...], m_ref[...], v_ref[...])

    shp = jax.ShapeDtypeStruct((rows, cols), F32)
    follows = [] if token is None else [token]
    return pl.pallas_call(body, name=name, grid=(rows // tile,),
                          in_specs=[spec] * 4 + [pl.BlockSpec(t.shape, lambda t_: (0, 0)) for t in follows],
                          out_specs=[spec] * 3, out_shape=[shp] * 3, compiler_params=_params("parallel"))(w, g, m, v, *follows)


_WEIGHTS = ("w_ada", "b_ada", "w_in", "sinks", "w_branch_a", "w_branch_b", "w_o", "ln1_g", "ln1_b", "w_gate_up", "w_down",
            "ln2_g", "ln2_b")
_TRANSPOSED = ("w_in", "w_branch_b", "w_gate_up")


def _pack_shard(name, w):
    w = w.astype(BF16)
    if name in _TRANSPOSED:
        w = w.T
    return w.reshape(-1, D_MODEL)


def _unpack_full(name, slab):
    if name == "w_branch_b":
        return slab.reshape(N_DEV * 128, 512)
    return slab.reshape(-1, D_MODEL)


def _unpack_group(group, gathered):
    return {n: _unpack_full(n, slab) for (n, _), slab in zip(group, gathered)}


def _unpack_grads(group, g_packed):
    g_w, off = {}, 0
    for n, r in group:
        part = g_packed[off:off + r]
        off += r
        g_w[n] = part.reshape(128, 512) if n == "w_branch_b" else part
    return g_w


def kernel(x, c, positions, w_ada, b_ada, w_in, sinks, w_branch_a, w_branch_b, w_o, ln1_g, ln1_b, w_gate_up, w_down, ln2_g, ln2_b, loss_target, m_w_ada, m_b_ada, m_w_in, m_sinks, m_w_branch_a, m_w_branch_b, m_w_o, m_ln1_g, m_ln1_b, m_w_gate_up, m_w_down, m_ln2_g, m_ln2_b, v_w_ada, v_b_ada, v_w_in, v_sinks, v_w_branch_a, v_w_branch_b, v_w_o, v_ln1_g, v_ln1_b, v_w_gate_up, v_w_down, v_ln2_g, v_ln2_b):
    weights = dict(w_ada=w_ada, b_ada=b_ada, w_in=w_in, sinks=sinks, w_branch_a=w_branch_a, w_branch_b=w_branch_b, w_o=w_o,
                   ln1_g=ln1_g, ln1_b=ln1_b, w_gate_up=w_gate_up, w_down=w_down, ln2_g=ln2_g, ln2_b=ln2_b)
    m_in = dict(w_ada=m_w_ada, b_ada=m_b_ada, w_in=m_w_in, sinks=m_sinks, w_branch_a=m_w_branch_a, w_branch_b=m_w_branch_b,
                w_o=m_w_o, ln1_g=m_ln1_g, ln1_b=m_ln1_b, w_gate_up=m_w_gate_up, w_down=m_w_down, ln2_g=m_ln2_g, ln2_b=m_ln2_b)
    v_in = dict(w_ada=v_w_ada, b_ada=v_b_ada, w_in=v_w_in, sinks=v_sinks, w_branch_a=v_w_branch_a, w_branch_b=v_w_branch_b,
                w_o=v_w_o, ln1_g=v_ln1_g, ln1_b=v_ln1_b, w_gate_up=v_w_gate_up, w_down=v_w_down, ln2_g=v_ln2_g, ln2_b=v_ln2_b)
    bsz = x.shape[0]
    me = _index(_my_place())
    ada_cols = w_ada.shape[2]
    outs = {}

    def adamw(n, g, token=None):
        w2, m2, v2 = (t[n][0] if t[n].ndim == 3 else t[n] for t in (weights, m_in, v_in))
        shape = weights[n].shape
        if n in _TRANSPOSED:
            dlt, nm, nv = _adamw(w2.T, g, m2.T, v2.T, "adamw_" + n, token)
            outs[n] = tuple(t.T.reshape(shape) for t in (g, dlt, nm, nv))
        else:
            dlt, nm, nv = _adamw(w2, g, m2, v2, "adamw_" + n, token)
            outs[n] = tuple(t.reshape(shape) for t in (g, dlt, nm, nv))
        return nv

    packed_in = [_pack_shard(n, weights[n][0]) for n, _ in GROUP_IN]
    packed_rest = [_pack_shard(n, weights[n][0]) for n, _ in GROUP_REST]
    c_all = _gather_small(jnp.pad(c, ((0, 8 - bsz), (0, 0))), "gather_c")[:, :bsz].reshape(N_DEV * bsz, D_MODEL)
    b_cols = lax.dynamic_slice_in_dim(b_ada, me * ada_cols, ada_cols, axis=1)
    mod_cols = _ada_fwd(c_all, w_ada[0], b_cols)
    mod_all = _gather_small(mod_cols, "gather_mod").transpose(1, 0, 2).reshape(N_DEV * bsz, 6, D_MODEL)
    gather_in = _Gather("gather_w_in", lax.optimization_barrier((packed_in, mod_all))[0])
    tables = _rope_tables(positions)
    tables, packed_rest, _ = lax.optimization_barrier((tables, packed_rest, gather_in.token))
    gather_rest = _Gather("gather_rest", packed_rest)
    mod = jnp.pad(lax.dynamic_slice_in_dim(mod_all, me * bsz, bsz, axis=0), ((0, 0), (0, 2), (0, 0)))
    mod = mod + gather_rest.token[0, 0]

    scatters, rest_grads = {}, {}

    def get_w_in(after):
        return _unpack_group(GROUP_IN, gather_in.finish(gather_in.pass_on(after)))["w_in"]

    def get_rest(after):
        return _unpack_group(GROUP_REST, gather_rest.finish(after))

    def pack_grads(group, grads):
        return jnp.concatenate([grads[n].reshape(N_DEV, r, D_MODEL) for n, r in group], axis=1)

    def hook(point, value):
        if point == "projected":
            return gather_rest.pass_on(value)
        if point == "grads_rest":
            scatters["rest"] = _ReduceScatter("scatter_rest", pack_grads(GROUP_REST, value))
            return scatters["rest"].token
        if point == "delta_done":
            return scatters["rest"].between_chips(value)
        if point == "grads_w_in":
            scatters["in"] = _ReduceScatter("scatter_w_in", pack_grads(GROUP_IN, value))
            rest_grads.update(_unpack_grads(GROUP_REST, scatters["rest"].finish(scatters["in"].token)))
            return scatters["in"].between_chips(lax.optimization_barrier(tuple(rest_grads.values()))[0])
        if point == "dgrad_done":
            return None
        raise ValueError(point)

    loss_part, grad_x, dmod, small = _layer_step(x, mod, tables, sinks[0], ln1_g, ln1_b, ln2_g, ln2_b, loss_target,
                                                 get_w_in, get_rest, hook)

    rows = jnp.concatenate([dmod.reshape(bsz * 6, D_MODEL), small, jnp.full((1, D_MODEL), loss_part, F32),
                            jnp.zeros((SMALL_ROWS - bsz * 6 - 6, D_MODEL), F32)], axis=0)
    land = lax.dynamic_update_slice(lax.empty((N_DEV,) + rows.shape, rows.dtype), rows[None], (me, 0, 0))
    gather_small = _copies_start("gather_small_start", [rows, land], _to_all_copies, N_DEV - 1)
    follow = gather_small[3]
    for n, g in rest_grads.items():
        follow = adamw(n, g, follow)[:8, :LANES]
    small_all = _copies_wait("gather_small_wait", gather_small, _to_all_copies, follow)[1]
    sums = _reduce_small(small_all)
    loss = sums[bsz * 6 + 5, 0]
    dmod_all = small_all[:, :bsz * 6].reshape(N_DEV * bsz, 6 * D_MODEL)
    small_g = {"b_ada": functools.reduce(jnp.add, [sums[6 * i:6 * i + 6] for i in range(bsz)]).reshape(1, 6 * D_MODEL),
               "sinks": sums[bsz * 6 + 4][::HEAD_DIM][None]}
    small_g.update({n: sums[bsz * 6 + i][None] for i, n in enumerate(("ln1_g", "ln1_b", "ln2_g", "ln2_b"))})
    names = list(small_g)
    for n, (dlt, nm, nv) in zip(names, _adamw_small([weights[n] for n in names], [small_g[n] for n in names],
                                                     [m_in[n] for n in names], [v_in[n] for n in names], "adamw_small")):
        outs[n] = (small_g[n], dlt, nm, nv)
    dmod_cols = lax.dynamic_slice_in_dim(dmod_all, me * ada_cols, ada_cols, axis=1)
    adamw("w_ada", _ada_wgrad(c_all.T, dmod_cols))
    done = lax.optimization_barrier(tuple(outs[n][3] for n in outs))
    for n, g in _unpack_grads(GROUP_IN, scatters["in"].finish(done[0])).items():
        adamw(n, g)

    return (loss, grad_x, *[outs[n][0] for n in _WEIGHTS], *[outs[n][1] for n in _WEIGHTS], *[outs[n][2] for n in _WEIGHTS],
            *[outs[n][3] for n in _WEIGHTS])
```

```python
import functools

import jax
import jax.numpy as jnp
from jax import lax
from jax.experimental import pallas as pl
from jax.experimental.pallas import tpu as pltpu

F32 = jnp.float32
BF16 = jnp.bfloat16

D_MODEL = 1024
HEAD_DIM = 64
A_Q_HEADS = 16
A_WINDOW = 128
B_PATTERNS = ((128, 1), (512, 4), (2048, 16))
B_HEADS_PER_GROUP = 8
D_FF = 2816
QBLOCK = 128
ROPE_THETA = 10000.0
LN_EPS = 1e-5
DEEPNORM_ALPHA = 2.0 ** 0.25
NEG_INF = -1e30
ADAM_LR, ADAM_B1, ADAM_B2, ADAM_EPS, ADAM_WD, ADAM_STEP = 0.001, 0.9, 0.999, 1e-08, 0.01, 10

N_DEV = 8
LANES = 128
VMEM_LIMIT_BYTES = 56 * 1024 * 1024
MESH = pl.DeviceIdType.MESH

OFF_QA, OFF_KVA, OFF_QKVB, OFF_GAB = 0, 1024, 1280, 5888
GROUP_IN = (("w_in", 992),)
GROUP_REST = (("w_branch_a", 128), ("w_branch_b", 64), ("w_o", 128), ("w_gate_up", 704), ("w_down", 352))


def _params(*sem):
    return pltpu.CompilerParams(dimension_semantics=sem, vmem_limit_bytes=VMEM_LIMIT_BYTES)


def _sigmoid(x):
    return 1.0 / (1.0 + jnp.exp(-x))


_DIMS = {"nn": (((1,), (0,)), ((), ())), "nt": (((1,), (1,)), ((), ())), "tn": (((0,), (0,)), ((), ()))}


def _matmul(a, b, *, mode, tm, tn, tk, name, out_dtype=None, n=None, b_off=0, token=None, ins=(), outs=None, epilogue=None,
            lhs_fn=None):
    if mode == "nn":
        (m, k), nn_ = a.shape, b.shape[1]
    elif mode == "nt":
        (m, k), nn_ = a.shape, (b.shape[0] if n is None else n)
    else:
        (k, m), nn_ = a.shape, b.shape[1]
    assert m % tm == 0 and nn_ % tn == 0 and k % tk == 0 and b_off % tn == 0, (name, m, nn_, k)
    nk = k // tk
    joff = b_off // tn
    if mode == "nn":
        a_spec = pl.BlockSpec((tm, tk), lambda i, j, kk: (i, kk))
        whole = dict(pipeline_mode=pl.Buffered(1)) if (tk, tn) == b.shape else {}
        b_spec = pl.BlockSpec((tk, tn), lambda i, j, kk: (kk, j), **whole)
    elif mode == "nt":
        a_spec = pl.BlockSpec((tm, tk), lambda i, j, kk: (i, kk))
        b_spec = pl.BlockSpec((tn, tk), lambda i, j, kk: (j + joff, kk))
    else:
        a_spec = pl.BlockSpec((tk, tm), lambda i, j, kk: (kk, i))
        b_spec = pl.BlockSpec((tk, tn), lambda i, j, kk: (kk, j))
    dims = _DIMS[mode]
    has_token = token is not None
    plain = epilogue is None
    if plain:
        outs = [(jax.ShapeDtypeStruct((m, nn_), out_dtype), (tm, tn), lambda i, j: (i, j))]

        def epilogue(acc, i, j, in_refs, out_refs):
            out_refs[0][...] = acc.astype(out_refs[0].dtype)

    nin = len(ins)
    nscratch = 1 if lhs_fn is None else 2
    assert lhs_fn is None or nk == 1

    def body(*refs):
        a_ref, b_ref = refs[:2]
        in_refs = refs[2:2 + nin]
        out_refs = refs[2 + nin + has_token:-nscratch]
        acc_ref = refs[-nscratch]
        kk = pl.program_id(2)
        if lhs_fn is None:
            lhs = a_ref[...].astype(BF16)
        else:
            lhs_ref = refs[-1]

            @pl.when(pl.program_id(1) == 0)
            def _():
                lhs_ref[...] = lhs_fn(a_ref, in_refs, out_refs)

            lhs = lhs_ref[...]
        part = lax.dot_general(lhs, b_ref[...].astype(BF16), dims, preferred_element_type=F32)

        def finish(acc):
            epilogue(acc, pl.program_id(0), pl.program_id(1), in_refs, out_refs)

        if nk == 1:
            finish(part)
        else:
            @pl.when(kk == 0)
            def _():
                acc_ref[...] = part

            @pl.when(kk > 0)
            def _():
                acc_ref[...] += part

            @pl.when(kk == nk - 1)
            def _():
                finish(acc_ref[...])

    def spec(block, index):
        return pl.BlockSpec(block, lambda i, j, kk: index(i, j))

    in_specs, args = [a_spec, b_spec], [a, b]
    for arr, block, index in ins:
        in_specs.append(spec(block, index))
        args.append(arr)
    if has_token:
        in_specs.append(pl.BlockSpec(token.shape, lambda i, j, kk: (0, 0)))
        args.append(token)
    res = pl.pallas_call(
        body,
        name=name,
        grid=(m // tm, nn_ // tn, nk),
        in_specs=in_specs,
        out_specs=[spec(block, index) for _, block, index in outs],
        out_shape=[shape for shape, _, _ in outs],
        scratch_shapes=[pltpu.VMEM((tm, tn) if nk > 1 else (8, LANES), F32)] + ([] if lhs_fn is None else [pltpu.VMEM((tm, tk), BF16)]),
        compiler_params=_params("arbitrary", "arbitrary", "arbitrary"),
    )(*args)
    return res[0] if plain else res


def _proj_rope(a, bt, cos, sin, *, n, b_off, rope_cols, tm, tn, name, out_dtype=F32):
    m, k = a.shape
    assert m % tm == 0 and n % tn == 0 and b_off % tn == 0 and rope_cols % LANES == 0, name
    joff = b_off // tn
    nrope, part = divmod(rope_cols, tn)

    def body(a_ref, b_ref, c_ref, s_ref, o_ref):
        acc = lax.dot_general(a_ref[...], b_ref[...], _DIMS["nt"], preferred_element_type=F32)
        j = pl.program_id(1)

        @pl.when(j < nrope)
        def _():
            o_ref[...] = _rope(acc, c_ref[...], s_ref[...], coarse=True).astype(o_ref.dtype)

        if part:
            @pl.when(j == nrope)
            def _():
                o_ref[:, :part] = _rope(acc[:, :part], c_ref[...], s_ref[...], coarse=True).astype(o_ref.dtype)
                o_ref[:, part:] = acc[:, part:].astype(o_ref.dtype)

        @pl.when(j >= nrope + (1 if part else 0))
        def _():
            o_ref[...] = acc.astype(o_ref.dtype)

    table = pl.BlockSpec((tm, LANES), lambda i, j: (i, 0))
    return pl.pallas_call(
        body,
        name=name,
        grid=(m // tm, n // tn),
        in_specs=[pl.BlockSpec((tm, k), lambda i, j: (i, 0)), pl.BlockSpec((tn, k), lambda i, j: (j + joff, 0)), table, table],
        out_specs=pl.BlockSpec((tm, tn), lambda i, j: (i, j)),
        out_shape=jax.ShapeDtypeStruct((m, n), out_dtype),
        compiler_params=_params("parallel", "parallel"),
    )(a, bt, cos, sin)


ROW_TILE = 256


def _rows(width, col=0):
    return pl.BlockSpec((1, ROW_TILE, width), lambda b, t: (b, t, col))


def _per_batch(nrows, width):
    return pl.BlockSpec((1, nrows, width), lambda b, t: (b, 0, 0))


def _row_call(body, name, bsz, seq, in_specs, out_specs, out_shape, accumulates=False):
    return pl.pallas_call(
        body,
        name=name,
        grid=(bsz, seq // ROW_TILE),
        in_specs=in_specs,
        out_specs=out_specs,
        out_shape=out_shape,
        compiler_params=_params("parallel", "arbitrary" if accumulates else "parallel"),
    )


def _acc_rows(acc_ref, first, rows):
    @pl.when(first)
    def _():
        acc_ref[...] = jnp.zeros_like(acc_ref)

    for r, val in enumerate(rows):
        acc_ref[0, r:r + 1, :] += val


def _colsum(v):
    return jnp.sum(v, axis=0, keepdims=True)


def _ln_stats(z):
    mu = jnp.mean(z, axis=-1, keepdims=True)
    zc = z - mu
    var = jnp.mean(zc * zc, axis=-1, keepdims=True)
    rstd = lax.rsqrt(var + LN_EPS)
    return zc * rstd, rstd


def _ln_bwd(dxhat, xhat, rstd):
    m1 = jnp.mean(dxhat, axis=-1, keepdims=True)
    m2 = jnp.mean(dxhat * xhat, axis=-1, keepdims=True)
    return rstd * (dxhat - m1 - xhat * m2)


def _modulate_in(x, mod):
    bsz, seq, d = x.shape

    def body(x_ref, mod_ref, u_ref):
        u_ref[0] = (x_ref[0] * (1.0 + mod_ref[0, 1:2, :]) + mod_ref[0, 0:1, :]).astype(BF16)

    return _row_call(body, "modulate_in", bsz, seq, [_rows(d), _per_batch(8, d)], _rows(d),
                     jax.ShapeDtypeStruct((bsz, seq, d), BF16))(x, mod)


EP_TILE = 512


def _ep_specs(seq, d):
    tiles = seq // EP_TILE
    return ((EP_TILE, d), lambda i, j: (i, 0)), ((1, 8, d), lambda i, j: (i // tiles, 0, 0)), ((1, d), lambda i, j: (0, 0))


def _wo_ln1(merged, wo, x, mod, g, b, seq):
    ntok, d = x.shape
    row, per_b, whole = _ep_specs(seq, d)

    def epilogue(y, i, j, ins, outs):
        x_ref, mod_ref, g_ref, b_ref = ins
        y_ref, h_ref, u_ref = outs
        z = DEEPNORM_ALPHA * x_ref[...] + (1.0 + mod_ref[0, 2:3, :]) * y
        xhat, _ = _ln_stats(z)
        h = xhat * g_ref[...] + b_ref[...]
        y_ref[...] = y
        h_ref[...] = h
        u_ref[...] = (h * (1.0 + mod_ref[0, 4:5, :]) + mod_ref[0, 3:4, :]).astype(BF16)

    f32, bf16 = jax.ShapeDtypeStruct((ntok, d), F32), jax.ShapeDtypeStruct((ntok, d), BF16)
    return _matmul(merged, wo, mode="nn", tm=EP_TILE, tn=d, tk=d, name="w_o_ln1",
                   ins=[(x,) + row, (mod,) + per_b, (g,) + whole, (b,) + whole],
                   outs=[(f32,) + row, (f32,) + row, (bf16,) + row], epilogue=epilogue)


FF_HALF = D_FF // 2


def _interleave_gate_up(w):
    return w.reshape(2, 2, FF_HALF, w.shape[1]).transpose(1, 0, 2, 3).reshape(w.shape)


def _gate_up_silu(u2, wgut_i):
    ntok = u2.shape[0]

    def epilogue(h, i, j, ins, outs):
        h_ref, a_ref = outs
        hg, hu = h[:, :FF_HALF], h[:, FF_HALF:]
        h_ref[...] = h.astype(BF16)
        a_ref[...] = (hg * _sigmoid(hg) * hu).astype(BF16)

    return _matmul(u2, wgut_i, mode="nt", tm=EP_TILE, tn=2 * FF_HALF, tk=u2.shape[1], name="gate_up_silu",
                   outs=[(jax.ShapeDtypeStruct((ntok, 2 * D_FF), BF16), (EP_TILE, 2 * FF_HALF), lambda i, j: (i, j)),
                         (jax.ShapeDtypeStruct((ntok, D_FF), BF16), (EP_TILE, FF_HALF), lambda i, j: (i, j))],
                   epilogue=epilogue)


def _down_dgrad_silu_bwd(dy2, wd, h_i):
    ntok = dy2.shape[0]
    wide = ((EP_TILE, 2 * FF_HALF), lambda i, j: (i, j))

    def epilogue(da, i, j, ins, outs):
        h = ins[0][...].astype(F32)
        hg, hu = h[:, :FF_HALF], h[:, FF_HALF:]
        sg = _sigmoid(hg)
        outs[0][:, :FF_HALF] = (da * hu * (sg * (1.0 + hg * (1.0 - sg)))).astype(BF16)
        outs[0][:, FF_HALF:] = (da * (hg * sg)).astype(BF16)

    return _matmul(dy2, wd, mode="nt", tm=EP_TILE, tn=FF_HALF, tk=dy2.shape[1], name="down_dgrad_silu_bwd",
                   ins=[(h_i,) + wide], outs=[(jax.ShapeDtypeStruct((ntok, 2 * D_FF), BF16),) + wide], epilogue=epilogue)[0]


def _down_ln2_loss_bwd(a, wd, h1, mod, g, b, target, seq):
    ntok, d = h1.shape
    row, per_b, whole = _ep_specs(seq, d)
    tiles = seq // EP_TILE

    def epilogue(y, i, j, ins, outs):
        h_ref, mod_ref, g_ref, b_ref, t_ref = ins
        dy_ref, dh_ref, acc_ref = outs
        gate = 1.0 + mod_ref[0, 5:6, :]
        z = DEEPNORM_ALPHA * h_ref[...] + gate * y
        xhat, rstd = _ln_stats(z)
        diff = xhat * g_ref[...] + b_ref[...] - t_ref[...]
        loss = 0.5 * jnp.sum(jnp.sum(diff * diff, axis=-1, keepdims=True) / d, axis=0, keepdims=True)
        dout = diff / d
        dz = _ln_bwd(dout * g_ref[...], xhat, rstd)
        dy_ref[...] = (gate * dz).astype(BF16)
        dh_ref[...] = DEEPNORM_ALPHA * dz
        _acc_rows(acc_ref, i % tiles == 0,
                  [_colsum(dout * xhat), _colsum(dout), _colsum(dz * y), jnp.broadcast_to(loss, (1, d))])

    return _matmul(a, wd, mode="nn", tm=EP_TILE, tn=d, tk=a.shape[1], name="down_ln2_loss_bwd",
                   ins=[(h1,) + row, (mod,) + per_b, (g,) + whole, (b,) + whole, (target,) + row],
                   outs=[(jax.ShapeDtypeStruct((ntok, d), BF16),) + row, (jax.ShapeDtypeStruct((ntok, d), F32),) + row,
                         (jax.ShapeDtypeStruct((ntok // seq, 8, d), F32),) + per_b], epilogue=epilogue)


def _gate_up_dgrad_ln1_bwd(dh, wgut, dh1a, x, y1, mod, g, b, seq):
    ntok, d = x.shape
    row, per_b, whole = _ep_specs(seq, d)
    tiles = seq // EP_TILE

    def epilogue(du, i, j, ins, outs):
        dh_ref, x_ref, y_ref, mod_ref, g_ref, b_ref = ins
        dy_ref, dx_ref, acc_ref = outs
        y = y_ref[...]
        gate = 1.0 + mod_ref[0, 2:3, :]
        z = DEEPNORM_ALPHA * x_ref[...] + gate * y
        xhat, rstd = _ln_stats(z)
        h1 = xhat * g_ref[...] + b_ref[...]
        dh1 = dh_ref[...] + du * (1.0 + mod_ref[0, 4:5, :])
        dz = _ln_bwd(dh1 * g_ref[...], xhat, rstd)
        dy_ref[...] = (gate * dz).astype(BF16)
        dx_ref[...] = DEEPNORM_ALPHA * dz
        _acc_rows(acc_ref, i % tiles == 0,
                  [_colsum(dh1 * xhat), _colsum(dh1), _colsum(dz * y), _colsum(du * h1), _colsum(du)])

    return _matmul(dh, wgut, mode="nn", tm=EP_TILE, tn=d, tk=D_FF, name="gate_up_dgrad_ln1_bwd",
                   ins=[(dh1a,) + row, (x,) + row, (y1,) + row, (mod,) + per_b, (g,) + whole, (b,) + whole],
                   outs=[(jax.ShapeDtypeStruct((ntok, d), BF16),) + row, (jax.ShapeDtypeStruct((ntok, d), F32),) + row,
                         (jax.ShapeDtypeStruct((ntok // seq, 8, d), F32),) + per_b], epilogue=epilogue)


def _wo_dgrad_gate_bwd(dy1, wo, gab, ya, yb):
    ntok, d = ya.shape
    tm, tn = 1024, 512
    tile = ((tm, tn), lambda i, j: (i, j))
    tile_b = ((tm, tn), lambda i, j: (i, j + d // tn))

    def epilogue(dm_, i, j, ins, outs):
        ga_ref, gb_ref, ya_ref, yb_ref = ins
        dya_ref, dyb_ref, dga_ref, dgb_ref = outs
        sa, sb = _sigmoid(ga_ref[...].astype(F32)), _sigmoid(gb_ref[...].astype(F32))
        dya_ref[...] = (dm_ * sa).astype(BF16)
        dyb_ref[...] = (dm_ * sb).astype(BF16)
        dga_ref[...] = (dm_ * ya_ref[...].astype(F32) * sa * (1.0 - sa)).astype(BF16)
        dgb_ref[...] = (dm_ * yb_ref[...].astype(F32) * sb * (1.0 - sb)).astype(BF16)

    shp = jax.ShapeDtypeStruct((ntok, d), BF16)
    return _matmul(dy1, wo, mode="nt", tm=tm, tn=tn, tk=d, name="w_o_dgrad_gate_bwd",
                   ins=[(gab,) + tile, (gab,) + tile_b, (ya,) + tile, (yb,) + tile],
                   outs=[(shp,) + tile] * 4, epilogue=epilogue)


def _w_in_dgrad_grad_x(dproj, wint, dxa, x, mod, seq, token):
    ntok, d = x.shape
    row, per_b, _ = _ep_specs(seq, d)
    tiles = seq // EP_TILE

    def epilogue(du, i, j, ins, outs):
        dxa_ref, x_ref, mod_ref = ins
        gx_ref, acc_ref = outs
        gx_ref[...] = dxa_ref[...] + du * (1.0 + mod_ref[0, 1:2, :])
        _acc_rows(acc_ref, i % tiles == 0, [_colsum(du * x_ref[...]), _colsum(du)])

    return _matmul(dproj, wint, mode="nn", tm=EP_TILE, tn=d, tk=wint.shape[0], name="w_in_dgrad_grad_x", token=token,
                   ins=[(dxa,) + row, (x,) + row, (mod,) + per_b],
                   outs=[(jax.ShapeDtypeStruct((ntok, d), F32),) + row, (jax.ShapeDtypeStruct((ntok // seq, 8, d), F32),) + per_b],
                   epilogue=epilogue)


def _merge_branch_b_gate(os_, ls_, wbbt, gab, ya):
    ntok, d = ya.shape
    w = os_[0].shape[1]
    tm, tn = 1024, 512
    tile = ((tm, tn), lambda i, j: (i, j))
    tile_b = ((tm, tn), lambda i, j: (i, j + d // tn))
    row = ((tm, w), lambda i, j: (i, 0))

    def lhs_fn(o0_ref, ins, outs):
        os_r, ls_r = (o0_ref,) + tuple(ins[3:5]), ins[5:8]
        ls = [l[...] for l in ls_r]
        mx = jnp.maximum(jnp.maximum(ls[0], ls[1]), ls[2])
        es = [jnp.exp(l - mx) for l in ls]
        den = es[0] + es[1] + es[2]
        ob = functools.reduce(jnp.add, [(e / den) * o[...].astype(F32) for e, o in zip(es, os_r)]).astype(BF16)
        outs[2][...] = ob
        return ob

    def epilogue(yb, i, j, ins, outs):
        ga_ref, gb_ref, ya_ref = ins[:3]
        yb_ref, merged_ref = outs[:2]
        yb_ref[...] = yb.astype(BF16)
        merged_ref[...] = (_sigmoid(ga_ref[...].astype(F32)) * ya_ref[...].astype(F32)
                           + _sigmoid(gb_ref[...].astype(F32)) * yb).astype(BF16)

    shp = jax.ShapeDtypeStruct((ntok, d), BF16)
    return _matmul(os_[0], wbbt, mode="nt", tm=tm, tn=tn, tk=w, name="merge_branch_b_gate", lhs_fn=lhs_fn,
                   ins=[(gab,) + tile, (gab,) + tile_b, (ya,) + tile] + [(v,) + row for v in list(os_[1:]) + list(ls_)],
                   outs=[(shp,) + tile] * 2 + [(jax.ShapeDtypeStruct((ntok, w), BF16),) + row], epilogue=epilogue)


def _segsum64(v):
    rows, width = v.shape
    ri = lax.broadcasted_iota(jnp.int32, (LANES, LANES), 0) // HEAD_DIM
    ci = lax.broadcasted_iota(jnp.int32, (LANES, LANES), 1) // HEAD_DIM
    ones = jnp.where(ri == ci, 1.0, 0.0).astype(BF16)
    out = []
    for c in range(width // LANES):
        part = v[:, c * LANES:(c + 1) * LANES]
        hi = part.astype(BF16)
        lo = (part - hi.astype(F32)).astype(BF16)
        out.append(jnp.dot(hi, ones, preferred_element_type=F32) + jnp.dot(lo, ones, preferred_element_type=F32))
    return jnp.concatenate(out, axis=1) if len(out) > 1 else out[0]


def _branch_b_dgrad_merge_bwd(dyb, wbbt, os_, ls_):
    ntok, w = os_[0].shape
    row = ((EP_TILE, w), lambda i, j: (i, 0))

    def epilogue(dob_, i, j, ins, outs):
        os_r, ls_r = ins[:3], ins[3:]
        do_r, dd_r = outs[:3], outs[3:]
        ls = [l[...] for l in ls_r]
        mx = jnp.maximum(jnp.maximum(ls[0], ls[1]), ls[2])
        es = [jnp.exp(l - mx) for l in ls]
        den = es[0] + es[1] + es[2]
        ws = [e / den for e in es]
        dws = [_segsum64(dob_ * o[...].astype(F32)) for o in os_r]
        mean = ws[0] * dws[0] + ws[1] * dws[1] + ws[2] * dws[2]
        for wg, do_ref, dd_ref in zip(ws, do_r, dd_r):
            do_ref[...] = wg * dob_
            dd_ref[...] = -wg * mean

    shp = jax.ShapeDtypeStruct((ntok, w), F32)
    return _matmul(dyb, wbbt, mode="nn", tm=EP_TILE, tn=w, tk=dyb.shape[1], name="branch_b_dgrad_merge_bwd",
                   ins=[(v,) + row for v in list(os_) + list(ls_)], outs=[(shp,) + row] * 6, epilogue=epilogue)


def _branch_a_dgrad_delta(dya, wba, oa, lse_a, sinks_exp, seq):
    ntok, w = oa.shape
    row, per_b, whole = _ep_specs(seq, w)
    tiles = seq // EP_TILE

    def epilogue(do_, i, j, ins, outs):
        o_ref, l_ref, s_ref = ins
        do_ref, dd_ref, acc_ref = outs
        dd = -_segsum64(do_ * o_ref[...].astype(F32))
        do_ref[...] = do_.astype(BF16)
        dd_ref[...] = dd
        _acc_rows(acc_ref, i % tiles == 0, [_colsum(dd * jnp.exp(s_ref[...] - l_ref[...]))])

    shp = jax.ShapeDtypeStruct((ntok, w), F32)
    return _matmul(dya, wba, mode="nt", tm=EP_TILE, tn=w, tk=dya.shape[1], name="branch_a_dgrad_delta",
                   ins=[(oa,) + row, (lse_a,) + row, (sinks_exp,) + whole],
                   outs=[(jax.ShapeDtypeStruct((ntok, w), BF16),) + row, (shp,) + row,
                         (jax.ShapeDtypeStruct((ntok // seq, 8, w), F32),) + per_b],
                   epilogue=epilogue)


def _swap_halves(v):
    src = lax.broadcasted_iota(jnp.int32, (LANES, LANES), 0)
    dst = lax.broadcasted_iota(jnp.int32, (LANES, LANES), 1)
    partner = jnp.where((dst % HEAD_DIM) < HEAD_DIM // 2, dst + HEAD_DIM // 2, dst - HEAD_DIM // 2)
    perm = jnp.where(src == partner, 1.0, 0.0).astype(BF16)
    hi = v.astype(BF16)
    lo = (v - hi.astype(F32)).astype(BF16)
    return jnp.dot(hi, perm, preferred_element_type=F32) + jnp.dot(lo, perm, preferred_element_type=F32)


def _swap_halves_roll(v):
    lane = lax.broadcasted_iota(jnp.int32, v.shape, 1)
    return jnp.where((lane % HEAD_DIM) < HEAD_DIM // 2, pltpu.roll(v, LANES - HEAD_DIM // 2, 1),
                     pltpu.roll(v, HEAD_DIM // 2, 1))


def _swap_halves_coarse(v):
    src = lax.broadcasted_iota(jnp.int32, (LANES, LANES), 0)
    dst = lax.broadcasted_iota(jnp.int32, (LANES, LANES), 1)
    partner = jnp.where((dst % HEAD_DIM) < HEAD_DIM // 2, dst + HEAD_DIM // 2, dst - HEAD_DIM // 2)
    perm = jnp.where(src == partner, 1.0, 0.0).astype(BF16)
    return jnp.dot(v.astype(BF16), perm, preferred_element_type=F32)


def _rope(v, cos, sin, sign=1.0, mxu=True, coarse=False):
    swap = (_swap_halves_coarse if coarse else _swap_halves) if mxu else _swap_halves_roll
    out = []
    for c in range(v.shape[1] // LANES):
        part = v[:, c * LANES:(c + 1) * LANES]
        out.append(part * cos + sign * (swap(part) * sin))
    return jnp.concatenate(out, axis=1) if len(out) > 1 else out[0]


def _half_mask(shape, half):
    lane = lax.broadcasted_iota(jnp.int32, shape, len(shape) - 1) % LANES
    return (lane < HEAD_DIM) if half == 0 else (lane >= HEAD_DIM)


def _dup_half(v, half):
    return jnp.where(_half_mask(v.shape, half), v, pltpu.roll(v, HEAD_DIM, 1))


def _fold_halves(v):
    return v + pltpu.roll(v, HEAD_DIM, 1)


def _pick_halves(lo_rows, hi_rows):
    return jnp.where(_half_mask(lo_rows.shape, 0), lo_rows, hi_rows)


def _stack_masked(v, pairs):
    parts = []
    for c in pairs:
        pair = v[:, c * LANES:(c + 1) * LANES]
        parts += [jnp.where(_half_mask(pair.shape, half), pair, 0.0) for half in (0, 1)]
    return jnp.concatenate(parts, axis=0)


def _stack_pair_cols(v, pairs):
    return jnp.concatenate([v[:, c * LANES + half * HEAD_DIM:c * LANES + half * HEAD_DIM + 1] for c in pairs for half in (0, 1)],
                           axis=0)


ATTN_UNITS = 16


def _class_rows(r):
    return [pl.ds(0, QBLOCK)] if r == 1 else [pl.ds(rho, QBLOCK, stride=r) for rho in range(r)]


def _band_mask(nrows, nk, blk, n_back, has_prev):
    qi = lax.broadcasted_iota(jnp.int32, (nrows, nk), 0) % QBLOCK
    ki = lax.broadcasted_iota(jnp.int32, (nrows, nk), 1)
    if has_prev:
        dist = qi + QBLOCK - ki
        return (dist >= 0) & (dist <= n_back) & ((ki >= QBLOCK) | (blk > 0))
    dist = qi - ki
    return (dist >= 0) & (dist <= n_back)


def _attn_fwd(q_arr, k_arr, v_arr, *, name, npair, gqa, q_col, k_col, v_col, nchunk, r, n_back, sinks=None):
    bsz, seq, _ = q_arr.shape
    rr = QBLOCK * r
    nblk = seq // rr
    qw = npair * LANES
    kw = LANES if gqa else qw
    has_prev = nblk > 1
    has_sink = sinks is not None
    scale = HEAD_DIM ** -0.5

    def body(*refs):
        refs = list(refs)
        q_ref, kc_ref, vc_ref = refs[:3]
        pos = 3
        if has_prev:
            kp_ref, vp_ref = refs[pos:pos + 2]
            pos += 2
        if has_sink:
            sink_ref = refs[pos]
            pos += 1
        o_ref, lse_ref = refs[pos:pos + 2]
        if r > 1:
            stage_o = refs[pos + 2]
        blk = pl.program_id(2)
        nk = (2 if has_prev else 1) * QBLOCK
        valid = _band_mask(QBLOCK, nk, blk, n_back, has_prev)
        per = npair // 2
        classes = _class_rows(r)
        step = max(1, ATTN_UNITS // (2 * npair))
        for first in range(0, len(classes), step):
            batch = classes[first:first + step]
            units = []
            for ci, rows in enumerate(batch):
                q = q_ref[0, rows, :] * scale
                k, v = kc_ref[0, rows, :], vc_ref[0, rows, :]
                if has_prev:
                    k = jnp.concatenate([kp_ref[0, rows, :], k], axis=0)
                    v = jnp.concatenate([vp_ref[0, rows, :], v], axis=0)
                if gqa:
                    kdup = [_dup_half(k, hk).astype(BF16) for hk in range(2)]
                    vdup = [_dup_half(v, hk) for hk in range(2)]
                for c in range(npair):
                    sl = slice(c * LANES, (c + 1) * LANES)
                    qc = q[:, sl]
                    kc, vc = (kdup[c // per], vdup[c // per]) if gqa else (k[:, sl].astype(BF16), v[:, sl])
                    for half in (0, 1):
                        qm = jnp.where(_half_mask(qc.shape, half), qc, 0.0).astype(BF16)
                        vm = jnp.where(_half_mask(vc.shape, half), vc, 0.0).astype(BF16)
                        s = lax.dot_general(qm, kc, _DIMS["nt"], preferred_element_type=F32)
                        units.append(dict(ci=ci, c=c, half=half, s=s, vm=vm, sk=sink_ref[2 * c + half] if has_sink else None))
            for u in units:
                s = jnp.where(valid, u["s"], NEG_INF)
                m = jnp.max(s, axis=1, keepdims=True)
                if has_sink:
                    m = jnp.maximum(m, u["sk"])
                p = jnp.exp(s - m)
                den = jnp.sum(p, axis=1, keepdims=True)
                if has_sink:
                    den = den + jnp.exp(u["sk"] - m)
                u.update(p=p.astype(BF16), den=den, lse=m + jnp.log(den))
            for u in units:
                u["o"] = jnp.dot(u["p"], u["vm"], preferred_element_type=F32) / u["den"]
            for ci, rows in enumerate(batch):
                outs, lses = [None] * npair, [None] * npair
                for u in units:
                    if u["ci"] != ci:
                        continue
                    c, o = u["c"], u["o"]
                    lse = jnp.broadcast_to(u["lse"], o.shape)
                    outs[c] = o if u["half"] == 0 else outs[c] + o
                    lses[c] = lse if u["half"] == 0 else _pick_halves(lses[c], lse)
                o_new = jnp.concatenate(outs, axis=1) if npair > 1 else outs[0]
                if r > 1:
                    stage_o[rows, :] = o_new
                else:
                    o_ref[0] = o_new.astype(BF16)
                lse_ref[0, rows, :] = jnp.concatenate(lses, axis=1) if npair > 1 else lses[0]
        if r > 1:
            o_ref[0] = stage_o[...].astype(BF16)

    def cur(width, col0):
        return pl.BlockSpec((1, rr, width), lambda b, c, i: (b, i, col0 + c))

    def prev(width, col0):
        return pl.BlockSpec((1, rr, width), lambda b, c, i: (b, jnp.maximum(i - 1, 0), col0 + c))

    in_specs = [cur(qw, q_col), cur(kw, k_col), cur(kw, v_col)]
    args = [q_arr, k_arr, v_arr]
    if has_prev:
        in_specs += [prev(kw, k_col), prev(kw, v_col)]
        args += [k_arr, v_arr]
    if has_sink:
        in_specs.append(pl.BlockSpec(memory_space=pltpu.SMEM))
        args.append(sinks)
    return pl.pallas_call(
        body,
        name=name,
        grid=(bsz, nchunk, nblk),
        in_specs=in_specs,
        out_specs=[pl.BlockSpec((1, rr, qw), lambda b, c, i: (b, i, c))] * 2,
        out_shape=[jax.ShapeDtypeStruct((bsz, seq, nchunk * qw), BF16), jax.ShapeDtypeStruct((bsz, seq, nchunk * qw), F32)],
        scratch_shapes=[pltpu.VMEM((rr, qw), F32)] if r > 1 else [],
        compiler_params=_params("parallel", "parallel", "parallel"),
    )(*args)


def _attn_bwd(q_arr, k_arr, v_arr, cos, sin, do, lse, dd, *, name, npair, gqa, q_col, k_col, v_col, nchunk, r, n_back,
              token=None):
    bsz, seq, _ = q_arr.shape
    rr = QBLOCK * r
    nblk = seq // rr
    qw = npair * LANES
    kw = LANES if gqa else qw
    has_next = nblk > 1
    has_token = token is not None
    staged = r > 1
    scale = HEAD_DIM ** -0.5

    def body(*refs):
        refs = list(refs)
        k_ref, v_ref, c_ref, s_ref = refs[:4]
        tile_refs = [refs[4:8]]
        pos = 8
        if has_next:
            tile_refs.append(refs[pos:pos + 4])
            pos += 4
        if has_token:
            pos += 1
        dq_ref, dk_ref, dv_ref = refs[pos:pos + 3]
        carry_ref, bcast_ref = refs[pos + 3:pos + 5]
        if staged:
            stage_q, stage_k, stage_v = refs[pos + 5:pos + 8]
        blk = pl.program_id(2)
        if has_next:
            @pl.when(blk == 0)
            def _():
                carry_ref[...] = jnp.zeros_like(carry_ref)

        nrows = (npair if gqa else 1) * QBLOCK
        qi = lax.broadcasted_iota(jnp.int32, (nrows, QBLOCK), 0) % QBLOCK
        ki = lax.broadcasted_iota(jnp.int32, (nrows, QBLOCK), 1)
        valids = [qi >= ki, (qi + QBLOCK - ki <= n_back) & (blk + 1 < nblk)]
        per = npair // 2
        ntile = len(tile_refs)
        cat = lambda parts: jnp.concatenate(parts, axis=1) if len(parts) > 1 else parts[0]
        classes = _class_rows(r)
        step = max(1, ATTN_UNITS // (ntile * (2 if gqa else 2 * npair)))
        def stat_cols(stat, slot):
            if gqa:
                return _stack_pair_cols(stat, list(range(slot * per, (slot + 1) * per)))
            col = slot * HEAD_DIM
            return stat[:, col:col + 1]

        nslot = 2 if gqa else 2 * npair
        if has_next:
            @pl.when(blk == 0)
            def _():
                for rows in classes:
                    for which, stat_ref in enumerate(tile_refs[0][2:4]):
                        stat = stat_ref[0, rows, :]
                        for slot in range(nslot):
                            bcast_ref[which, slot, rows if not gqa else slice(None), :] = jnp.broadcast_to(
                                stat_cols(stat, slot), (nrows, LANES))

        for first in range(0, len(classes), step):
            batch = classes[first:first + step]
            units = []
            for ci, rows in enumerate(batch):
                keep = slice(None) if gqa else rows
                tiles = [(q_ref[0, rows, :] * scale, do_ref[0, rows, :], l_ref[0, rows, :], d_ref[0, rows, :])
                         for q_ref, do_ref, l_ref, d_ref in tile_refs]

                def stats(t, slot, keep=keep, tiles=tiles):
                    if has_next and t == 0:
                        return bcast_ref[0, slot, keep, :], bcast_ref[1, slot, keep, :]
                    return tuple(jnp.broadcast_to(stat_cols(tiles[t][2 + w], slot), (nrows, LANES)) for w in range(2))

                k, v = k_ref[0, rows, :], v_ref[0, rows, :]
                if gqa:
                    for hk in range(2):
                        pairs = list(range(hk * per, (hk + 1) * per))
                        kd, vd = _dup_half(k, hk).astype(BF16), _dup_half(v, hk).astype(BF16)
                        for t, (q, do_, l_, d_) in enumerate(tiles):
                            lcol, dcol = stats(t, hk)
                            units.append(dict(ci=ci, t=t, hk=hk, slot=hk, keep=keep, pairs=pairs,
                                              qs=_stack_masked(q, pairs).astype(BF16),
                                              dos=_stack_masked(do_, pairs).astype(BF16), lcol=lcol, dcol=dcol,
                                              kmat=kd, vmat=vd, kdq=kd))
                else:
                    for c in range(npair):
                        sl = slice(c * LANES, (c + 1) * LANES)
                        kc, vcb = k[:, sl], v[:, sl].astype(BF16)
                        kcb = kc.astype(BF16)
                        for t, (q, do_, l_, d_) in enumerate(tiles):
                            for half in (0, 1):
                                hm = _half_mask(kc.shape, half)
                                lcol, dcol = stats(t, 2 * c + half)
                                units.append(dict(ci=ci, t=t, c=c, half=half, slot=2 * c + half, keep=keep,
                                                  qs=jnp.where(hm, q[:, sl], 0.0).astype(BF16),
                                                  dos=jnp.where(hm, do_[:, sl], 0.0).astype(BF16), lcol=lcol, dcol=dcol,
                                                  kmat=kcb, vmat=vcb, kdq=jnp.where(hm, kc, 0.0).astype(BF16)))
            for u in units:
                u["s"] = lax.dot_general(u["qs"], u["kmat"], _DIMS["nt"], preferred_element_type=F32)
                u["dp"] = lax.dot_general(u["dos"], u["vmat"], _DIMS["nt"], preferred_element_type=F32)
            for u in units:
                p = jnp.exp(jnp.where(valids[u["t"]], u["s"], NEG_INF) - u["lcol"])
                u["ds"] = (p * (u["dp"] + u["dcol"])).astype(BF16)
                u["p"] = p.astype(BF16)
            for u in units:
                u["dv"] = lax.dot_general(u["p"], u["dos"], _DIMS["tn"], preferred_element_type=F32)
                u["dk"] = lax.dot_general(u["ds"], u["qs"], _DIMS["tn"], preferred_element_type=F32)
                u["dq"] = jnp.dot(u["ds"], u["kdq"], preferred_element_type=F32) * scale
            for u in units:
                if u["t"] == 1:
                    bcast_ref[0, u["slot"], u["keep"], :] = u["lcol"]
                    bcast_ref[1, u["slot"], u["keep"], :] = u["dcol"]
            for ci, rows in enumerate(batch):
                mine = [u for u in units if u["ci"] == ci]
                dq = [[None] * npair for _ in range(ntile)]
                if gqa:
                    dk_out = dv_out = None
                    for hk in range(2):
                        us = [u for u in mine if u["hk"] == hk]
                        for u in us:
                            for i, c in enumerate(u["pairs"]):
                                dq[u["t"]][c] = _pick_halves(u["dq"][2 * i * QBLOCK:(2 * i + 1) * QBLOCK],
                                                             u["dq"][(2 * i + 1) * QBLOCK:(2 * i + 2) * QBLOCK])
                        dk_h = _fold_halves(functools.reduce(jnp.add, [u["dk"] for u in us]))
                        dv_h = _fold_halves(functools.reduce(jnp.add, [u["dv"] for u in us]))
                        dk_out = dk_h if hk == 0 else _pick_halves(dk_out, dk_h)
                        dv_out = dv_h if hk == 0 else _pick_halves(dv_out, dv_h)
                else:
                    dks, dvs = [], []
                    for c in range(npair):
                        us = [u for u in mine if u["c"] == c]
                        dks.append(functools.reduce(jnp.add, [u["dk"] for u in us]))
                        dvs.append(functools.reduce(jnp.add, [u["dv"] for u in us]))
                        for t in range(ntile):
                            dq[t][c] = functools.reduce(jnp.add, [u["dq"] for u in us if u["t"] == t])
                    dk_out, dv_out = cat(dks), cat(dvs)
                ck, sk_ = c_ref[0, rows, :], s_ref[0, rows, :]
                dk_new = _rope(dk_out, ck, sk_, sign=-1.0, mxu=gqa, coarse=True)
                dq_cur = cat(dq[0])
                if has_next:
                    dq_cur = dq_cur + carry_ref[rows, :]
                    carry_ref[rows, :] = cat(dq[1])
                dq_new = _rope(dq_cur, ck, sk_, sign=-1.0, mxu=gqa, coarse=True)
                if staged:
                    stage_q[rows, :], stage_k[rows, :], stage_v[rows, :] = dq_new, dk_new, dv_out
                else:
                    dq_ref[0], dk_ref[0], dv_ref[0] = dq_new.astype(BF16), dk_new.astype(BF16), dv_out.astype(BF16)
        if staged:
            dq_ref[0], dk_ref[0], dv_ref[0] = stage_q[...].astype(BF16), stage_k[...].astype(BF16), stage_v[...].astype(BF16)

    def at(width, col0, shift):
        return pl.BlockSpec((1, rr, width), lambda b, c, i: (b, jnp.minimum(i + shift, nblk - 1), col0 + c))

    in_specs = [at(kw, k_col, 0), at(kw, v_col, 0), pl.BlockSpec((1, rr, LANES), lambda b, c, i: (b, i, 0)),
                pl.BlockSpec((1, rr, LANES), lambda b, c, i: (b, i, 0))]
    args = [k_arr, v_arr, cos, sin]
    for shift in (0, 1) if has_next else (0,):
        in_specs += [at(qw, q_col, shift), at(qw, 0, shift), at(qw, 0, shift), at(qw, 0, shift)]
        args += [q_arr, do, lse, dd]
    if has_token:
        in_specs.append(pl.BlockSpec(token.shape, lambda b, c, i: (0, 0)))
        args.append(token)
    return pl.pallas_call(
        body,
        name=name,
        grid=(bsz, nchunk, nblk),
        in_specs=in_specs,
        out_specs=[pl.BlockSpec((1, rr, qw), lambda b, c, i: (b, i, c)),
                   pl.BlockSpec((1, rr, kw), lambda b, c, i: (b, i, c)),
                   pl.BlockSpec((1, rr, kw), lambda b, c, i: (b, i, c))],
        out_shape=[jax.ShapeDtypeStruct((bsz, seq, nchunk * qw), BF16),
                   jax.ShapeDtypeStruct((bsz, seq, nchunk * kw), BF16),
                   jax.ShapeDtypeStruct((bsz, seq, nchunk * kw), BF16)],
        scratch_shapes=[pltpu.VMEM((rr, qw) if has_next else (8, LANES), F32),
                        pltpu.VMEM((2, 2 if gqa else 2 * npair, npair * QBLOCK if gqa else rr, LANES) if has_next
                                   else (1, 1, 8, LANES), F32)] +
                       ([pltpu.VMEM((rr, qw), F32), pltpu.VMEM((rr, kw), F32), pltpu.VMEM((rr, kw), F32)] if staged else []),
        compiler_params=_params("parallel", "parallel", "arbitrary"),
    )(*args)


B_CHUNKS = {1: (4, 1), 4: (1, 4), 16: (1, 4)}


def _rope_tables(positions):
    half = HEAD_DIM // 2
    inv = ROPE_THETA ** (-jnp.arange(half, dtype=F32) / half)
    ang = positions.astype(F32)[..., None] * inv
    cos, sin = jnp.cos(ang), jnp.sin(ang)
    return jnp.concatenate([cos] * 4, axis=-1), jnp.concatenate([-sin, sin, -sin, sin], axis=-1)


def _layer_step(x, mod, tables, sinks, ln1_g, ln1_b, ln2_g, ln2_b, target, get_w_in, get_rest, hook):
    bsz, seq, d = x.shape
    ntok = bsz * seq
    flat = lambda v: v.reshape(ntok, v.shape[-1])
    unflat = lambda v: v.reshape(bsz, seq, v.shape[-1])
    cos, sin = tables
    mm = functools.partial(_matmul, tm=1024, tk=1024)
    scalar = lambda tok: 0.0 if tok is None else tok[0, 0]

    u1 = _modulate_in(x, mod)
    u1f = flat(u1)
    wint = get_w_in(u1)
    cosf, sinf = flat(cos), flat(sin)
    proj = functools.partial(_proj_rope, u1f, wint, cosf, sinf, tm=2048)
    qkvb = unflat(proj(n=4608, b_off=OFF_QKVB, rope_cols=3072, tn=256, name="proj_qkvb"))
    b_kws, os_, ls_ = [], [], []
    for g, (window, r) in enumerate(B_PATTERNS):
        npair, nch = B_CHUNKS[r]
        per = B_HEADS_PER_GROUP // (2 * npair)
        nsec = len(B_PATTERNS) * per
        kw_ = dict(npair=npair, gqa=False, q_col=g * per, k_col=nsec + g * per, v_col=2 * nsec + g * per, nchunk=nch, r=r,
                   n_back=window // r)
        b_kws.append(kw_)
        o_g, l_g = _attn_fwd(qkvb, qkvb, qkvb, name=f"attn_b{g}_fwd", **kw_)
        os_.append(o_g)
        ls_.append(l_g)
    tok = hook("projected", os_[-1])
    proj = functools.partial(_proj_rope, u1f, wint, cosf + scalar(tok), sinf, tm=2048)
    gab = unflat(proj(n=2048, b_off=OFF_GAB, rope_cols=0, tn=256, name="proj_gab", out_dtype=BF16))
    qa = kva = unflat(proj(n=OFF_QKVB, b_off=OFF_QA, rope_cols=OFF_KVA + LANES, tn=256, name="proj_qkva", out_dtype=BF16))
    a_kw = dict(npair=A_Q_HEADS // 2, gqa=True, q_col=0, k_col=OFF_KVA // LANES, v_col=OFF_KVA // LANES + 1, nchunk=1, r=1,
                n_back=A_WINDOW - 1)
    after_gab = jnp.minimum(jnp.abs(gab[0, 0, 0].astype(F32)), 0.0)
    oa, lse_a = _attn_fwd(qa, kva, kva, name="attn_a_fwd", sinks=sinks.reshape(A_Q_HEADS) + after_gab, **a_kw)
    rest = get_rest(oa)
    wba, wbbt, wo, wgut, wd = (rest[n] for n in ("w_branch_a", "w_branch_b", "w_o", "w_gate_up", "w_down"))
    ya = unflat(mm(flat(oa), wba, mode="nn", out_dtype=BF16, tn=512, name="branch_a"))
    ybf, mergedf, obf = _merge_branch_b_gate([flat(t) for t in os_], [flat(t) for t in ls_], wbbt, flat(gab), flat(ya))
    xf = flat(x)
    y1f, h1f, u2f = _wo_ln1(mergedf, wo, xf, mod, ln1_g, ln1_b, seq)
    wgut_i = _interleave_gate_up(wgut)
    hf, af = _gate_up_silu(u2f, wgut_i)

    dy2f, dh1af, acc2 = _down_ln2_loss_bwd(af, wd, h1f, mod, ln2_g, ln2_b, flat(target), seq)
    g_wd = _matmul(af, dy2f, mode="tn", out_dtype=BF16, tm=256, tn=1024, tk=ntok, name="down_wgrad")
    dhf = _down_dgrad_silu_bwd(dy2f, wd, hf)
    g_wgut = _interleave_gate_up(_matmul(dhf, u2f, mode="tn", out_dtype=BF16, tm=256, tn=1024, tk=ntok, name="gate_up_wgrad"))
    dy1f, dxaf, acc1 = _gate_up_dgrad_ln1_bwd(dhf, wgut_i, dh1af, xf, y1f, mod, ln1_g, ln1_b, seq)
    g_wo = _matmul(mergedf, dy1f, mode="tn", out_dtype=BF16, tm=256, tn=1024, tk=ntok, name="w_o_wgrad")
    dyaf, dybf, dgaf, dgbf = _wo_dgrad_gate_bwd(dy1f, wo, flat(gab), flat(ya), ybf)
    g_wba = _matmul(flat(oa), dyaf, mode="tn", out_dtype=BF16, tm=256, tn=1024, tk=ntok, name="branch_a_wgrad")
    g_wbbt = _matmul(dybf, obf, mode="tn", out_dtype=BF16, tm=256, tn=512, tk=ntok, name="branch_b_wgrad")
    tok = hook("grads_rest", dict(w_branch_a=g_wba, w_branch_b=g_wbbt, w_o=g_wo, w_gate_up=g_wgut, w_down=g_wd))

    sinks_exp = jnp.repeat(sinks.reshape(1, A_Q_HEADS), HEAD_DIM, axis=1) + scalar(tok)
    doa, dd_a, acc_s = _branch_a_dgrad_delta(dyaf, wba, flat(oa), flat(lse_a), sinks_exp, seq)
    doa, dd_a = unflat(doa), unflat(dd_a)
    tok = hook("delta_done", dd_a)
    dqa, dka, dva = _attn_bwd(qa, kva, kva, cos, sin, doa, lse_a, dd_a, name="attn_a_bwd", token=tok, **a_kw)
    merged_bwd = [unflat(t) for t in _branch_b_dgrad_merge_bwd(dybf, wbbt, [flat(t) for t in os_], [flat(t) for t in ls_])]
    dqs, dks, dvs = [], [], []
    for g in range(len(B_PATTERNS)):
        dq_g, dk_g, dv_g = _attn_bwd(qkvb, qkvb, qkvb, cos, sin, merged_bwd[g], ls_[g], merged_bwd[3 + g],
                                     name=f"attn_b{g}_bwd", **b_kws[g])
        dqs.append(dq_g)
        dks.append(dk_g)
        dvs.append(dv_g)
    dproj = jnp.concatenate([t.astype(BF16) for t in [dqa, dka, dva] + dqs + dks + dvs] + [unflat(dgaf), unflat(dgbf)], axis=-1)
    dprojf = flat(dproj)
    g_wint = _matmul(dprojf, u1f, mode="tn", out_dtype=BF16, tm=256, tn=1024, tk=ntok, name="w_in_wgrad")
    tok = hook("grads_w_in", dict(w_in=g_wint))
    grad_x, acc0 = _w_in_dgrad_grad_x(dprojf, wint, dxaf, xf, mod, seq, tok)
    grad_x = unflat(grad_x)
    tok = hook("dgrad_done", grad_x)

    loss_part = jnp.sum(acc2[:, 3, 0])
    dmod = jnp.stack([acc0[:, 1], acc0[:, 0], acc1[:, 2], acc1[:, 4], acc1[:, 3], acc2[:, 2]], axis=1)
    small = jnp.stack([acc1[:, 0].sum(0), acc1[:, 1].sum(0), acc2[:, 0].sum(0), acc2[:, 1].sum(0), acc_s[:, 0].sum(0)])
    small = small + scalar(tok)
    return loss_part, grad_x, dmod, small


CHIP_FLIPS = (2, 4, 6)


def _my_place():
    return lax.axis_index("x"), lax.axis_index("y"), lax.axis_index("c")


def _flip(place, k):
    px, py, pc = place
    return (1 - px if k & 4 else px, 1 - py if k & 2 else py, 1 - pc if k & 1 else pc)


def _index(place):
    return 4 * place[0] + 2 * place[1] + place[2]


def _gather_small(v, name):
    rows, cols = v.shape

    def body(v_ref, out_ref, send_sems, recv_sems):
        me = _my_place()
        out_ref[_index(me)] = v_ref[...]
        copies = []
        for k in range(1, N_DEV):
            copies.append(pltpu.make_async_remote_copy(
                src_ref=v_ref, dst_ref=out_ref.at[_index(me)], send_sem=send_sems.at[k - 1], recv_sem=recv_sems.at[k - 1],
                device_id=_flip(me, k), device_id_type=MESH))
        for cp in copies:
            cp.start()
        for k in range(1, N_DEV):
            pltpu.make_async_remote_copy(
                src_ref=v_ref, dst_ref=out_ref.at[_index(_flip(me, k))], send_sem=send_sems.at[k - 1],
                recv_sem=recv_sems.at[k - 1], device_id=_flip(me, k), device_id_type=MESH).wait_recv()
        for cp in copies:
            cp.wait_send()

    return pl.pallas_call(
        body,
        name=name,
        out_shape=jax.ShapeDtypeStruct((N_DEV, rows, cols), v.dtype),
        in_specs=[pl.BlockSpec(memory_space=pltpu.VMEM)],
        out_specs=pl.BlockSpec(memory_space=pltpu.VMEM),
        scratch_shapes=[pltpu.SemaphoreType.DMA((N_DEV - 1,)), pltpu.SemaphoreType.DMA((N_DEV - 1,))],
        compiler_params=pltpu.CompilerParams(vmem_limit_bytes=VMEM_LIMIT_BYTES),
    )(v)


_HBM = pl.BlockSpec(memory_space=pltpu.HBM)
_SEM = pl.BlockSpec(memory_space=pltpu.SEMAPHORE)
_EFFECT = pltpu.SideEffectType.DATAFLOW_SIDE_EFFECTING


def _remote(src, dst, send_sems, recv_sems, j, to):
    return pltpu.make_async_remote_copy(src_ref=src, dst_ref=dst, send_sem=send_sems.at[j], recv_sem=recv_sems.at[j],
                                        device_id=to, device_id_type=MESH)


def _copies_start(name, bufs, make_copies, nsem):
    nbuf = len(bufs)

    def body(*refs):
        for cp in make_copies(refs[:nbuf], refs[nbuf], refs[nbuf + 1]):
            cp.start()
        refs[-1][...] = jnp.zeros_like(refs[-1])

    sems = pltpu.SemaphoreType.DMA((nsem,))
    res = pl.pallas_call(
        body, name=name,
        out_shape=(sems, sems, *[pltpu.HBM(v.shape, v.dtype) for v in bufs], jax.ShapeDtypeStruct((8, LANES), F32)),
        in_specs=(_HBM,) * nbuf, out_specs=(_SEM, _SEM) + (_HBM,) * nbuf + (pl.BlockSpec(memory_space=pltpu.VMEM),),
        input_output_aliases={i: 2 + i for i in range(nbuf)},
        compiler_params=pltpu.CompilerParams(has_side_effects=_EFFECT),
    )(*[pltpu.with_memory_space_constraint(v, pltpu.HBM) for v in bufs])
    return res[0], res[1], list(res[2:2 + nbuf]), res[-1]


def _copies_wait(name, started, make_copies, after):
    send_sems, recv_sems, bufs, _ = started
    nbuf = len(bufs)

    def body(*refs):
        for cp in make_copies(refs[:nbuf], refs[nbuf], refs[nbuf + 1]):
            cp.wait_send()
            cp.wait_recv()

    return list(pl.pallas_call(
        body, name=name,
        out_shape=tuple(pltpu.HBM(v.shape, v.dtype) for v in bufs),
        in_specs=(_HBM,) * nbuf + (_SEM, _SEM, pl.BlockSpec(memory_space=pl.ANY)), out_specs=(_HBM,) * nbuf,
        input_output_aliases={i: i for i in range(nbuf)},
        compiler_params=pltpu.CompilerParams(has_side_effects=_EFFECT),
    )(*bufs, send_sems, recv_sems, after))


def _to_sibling_copies(refs, send_sems, recv_sems):
    src_ref, land_ref = refs
    me = _my_place()
    return [_remote(src_ref.at[q, 1 - me[2]], land_ref.at[q], send_sems, recv_sems, q, _flip(me, 1)) for q in range(4)]


def _to_chips_copies(refs, send_sems, recv_sems):
    src_ref, land_ref = refs
    me = _my_place()
    copies = []
    for j, k in enumerate(CHIP_FLIPS):
        to = _flip(me, k)
        copies.append(_remote(src_ref.at[2 * to[0] + to[1]], land_ref.at[j], send_sems, recv_sems, j, to))
    return copies


class _Gather:
    def __init__(self, name, blocks):
        self.name, self.n = name, len(blocks)
        at_me = (_index(_my_place()), 0, 0)
        lands = [lax.dynamic_update_slice(lax.empty((N_DEV,) + v.shape, v.dtype), v[None], at_me) for v in blocks]
        self.first = _copies_start(name + "_start", list(blocks) + lands, self._first_copies, 4 * self.n)
        self.token = self.first[3]

    def _first_copies(self, refs, send_sems, recv_sems):
        me = _my_place()
        return [_remote(refs[w], refs[self.n + w].at[_index(me)], send_sems, recv_sems, 4 * w + j, _flip(me, k))
                for w in range(self.n) for j, k in enumerate((1,) + CHIP_FLIPS)]

    def _pass_copies(self, refs, send_sems, recv_sems):
        me = _my_place()
        copies = []
        for w, land in enumerate(refs):
            for j, k in enumerate(CHIP_FLIPS):
                slot = land.at[_index(_flip(me, k))]
                copies.append(_remote(slot, slot, send_sems, recv_sems, 3 * w + j, _flip(me, 1)))
        return copies

    def pass_on(self, after):
        lands = _copies_wait(self.name + "_wait", self.first, self._first_copies, after)[self.n:]
        self.second = _copies_start(self.name + "_pass_start", lands, self._pass_copies, 3 * self.n)
        return self.second[3]

    def finish(self, after):
        return _copies_wait(self.name + "_pass_wait", self.second, self._pass_copies, after)


def _to_all_copies(refs, send_sems, recv_sems):
    src_ref, land_ref = refs
    me = _my_place()
    return [_remote(src_ref, land_ref.at[_index(me)], send_sems, recv_sems, k - 1, _flip(me, k)) for k in range(1, N_DEV)]


SUM_SPLIT = 2


def _sum_pairs(parts, theirs):
    nchip, _, rows, cols = parts.shape
    tile = rows // SUM_SPLIT

    def body(c_ref, a_ref, b_ref, o_ref):
        o_ref[...] = (a_ref[0].astype(F32) + b_ref[...].astype(F32)).astype(BF16)

    spec = pl.BlockSpec((1, tile, cols), lambda q, t, c_ref: (q, t, 0))
    grid_spec = pltpu.PrefetchScalarGridSpec(
        num_scalar_prefetch=1, grid=(nchip, SUM_SPLIT),
        in_specs=[pl.BlockSpec((1, 1, tile, cols), lambda q, t, c_ref: (q, c_ref[0], t, 0)), spec], out_specs=spec)
    return pl.pallas_call(body, name="grad_sum_sibling", grid_spec=grid_spec,
                          out_shape=jax.ShapeDtypeStruct((nchip, rows, cols), BF16),
                          compiler_params=_params("parallel", "parallel"))(lax.axis_index("c").reshape(1), parts, theirs)


def _sum_final(chip_sum, got):
    _, rows, cols = chip_sum.shape
    tile = rows // SUM_SPLIT

    def body(q_ref, a_ref, g_ref, o_ref):
        o_ref[...] = ((a_ref[0].astype(F32) + g_ref[0].astype(F32)) + g_ref[1].astype(F32)) + g_ref[2].astype(F32)

    grid_spec = pltpu.PrefetchScalarGridSpec(
        num_scalar_prefetch=1, grid=(SUM_SPLIT,),
        in_specs=[pl.BlockSpec((1, tile, cols), lambda t, q_ref: (q_ref[0], t, 0)),
                  pl.BlockSpec((3, tile, cols), lambda t, q_ref: (0, t, 0))],
        out_specs=pl.BlockSpec((tile, cols), lambda t, q_ref: (t, 0)))
    my_chip = (2 * lax.axis_index("x") + lax.axis_index("y")).reshape(1)
    return pl.pallas_call(body, name="grad_sum_chips", grid_spec=grid_spec, out_shape=jax.ShapeDtypeStruct((rows, cols), F32),
                          compiler_params=_params("parallel"))(my_chip, chip_sum, got)


class _ReduceScatter:
    def __init__(self, name, slabs):
        self.name, self.rows = name, slabs.shape[1]
        parts = slabs.reshape(4, 2, self.rows, D_MODEL)
        self.first = _copies_start(name + "_sibling_start", [parts, lax.empty((4, self.rows, D_MODEL), slabs.dtype)],
                                   _to_sibling_copies, 4)
        self.token = self.first[3]

    def between_chips(self, after):
        parts, theirs = _copies_wait(self.name + "_sibling_wait", self.first, _to_sibling_copies, after)
        chip_sum = _sum_pairs(parts, theirs)
        self.second = _copies_start(self.name + "_chips_start", [chip_sum, lax.empty((3, self.rows, D_MODEL), chip_sum.dtype)],
                                    _to_chips_copies, 3)
        return self.second[3]

    def finish(self, after):
        chip_sum, got = _copies_wait(self.name + "_chips_wait", self.second, _to_chips_copies, after)
        return _sum_final(chip_sum, got)


def _ada_fwd(c_all, w, b):
    nb, _ = c_all.shape
    ncol = w.shape[1]

    def body(c_ref, w_ref, b_ref, o_ref):
        c = c_ref[...]
        act = (c * _sigmoid(c)).astype(BF16)
        o_ref[...] = jnp.dot(act, w_ref[...].astype(BF16), preferred_element_type=F32) + b_ref[...]

    return pl.pallas_call(body, name="ada_fwd", out_shape=jax.ShapeDtypeStruct((nb, ncol), F32),
                          compiler_params=pltpu.CompilerParams(vmem_limit_bytes=VMEM_LIMIT_BYTES))(c_all, w, b)


def _ada_wgrad(c_all_t, dmod_cols):
    d, nb = c_all_t.shape
    ncol = dmod_cols.shape[1]

    def body(ct_ref, dm_ref, o_ref):
        ct = ct_ref[...]
        act = (ct * _sigmoid(ct)).astype(BF16).astype(F32)
        dm = dm_ref[...].astype(BF16).astype(F32)
        acc = act[:, 0:1] * dm[0:1, :]
        for i in range(1, nb):
            acc = acc + act[:, i:i + 1] * dm[i:i + 1, :]
        o_ref[...] = acc

    return pl.pallas_call(body, name="ada_wgrad", out_shape=jax.ShapeDtypeStruct((d, ncol), F32),
                          compiler_params=pltpu.CompilerParams(vmem_limit_bytes=VMEM_LIMIT_BYTES))(c_all_t, dmod_cols)


SMALL_ROWS = 24


def _reduce_small(gathered):
    def body(g_ref, o_ref):
        acc = g_ref[0]
        for dev in range(1, N_DEV):
            acc = acc + g_ref[dev]
        o_ref[...] = acc

    return pl.pallas_call(body, name="reduce_small", out_shape=jax.ShapeDtypeStruct(gathered.shape[1:], F32))(gathered)


def _adamw_math(w, g, m, v):
    nm = ADAM_B1 * m + (1.0 - ADAM_B1) * g
    nv = ADAM_B2 * v + (1.0 - ADAM_B2) * (g * g)
    bc1 = 1.0 - ADAM_B1 ** ADAM_STEP
    bc2 = 1.0 - ADAM_B2 ** ADAM_STEP
    return -ADAM_LR * ((nm / bc1) / (jnp.sqrt(nv / bc2) + ADAM_EPS) + ADAM_WD * w), nm, nv


def _adamw_small(ws, gs, ms, vs, name):
    n = len(ws)

    def body(*refs):
        for i in range(n):
            res = _adamw_math(*(refs[k * n + i][...] for k in range(4)))
            for k in range(3):
                refs[(4 + k) * n + i][...] = res[k]

    shapes = [jax.ShapeDtypeStruct(w.shape, F32) for w in ws]
    res = pl.pallas_call(body, name=name, out_shape=shapes * 3)(*ws, *gs, *ms, *vs)
    return [(res[i], res[n + i], res[2 * n + i]) for i in range(n)]


def _adamw(w, g, m, v, name, token=None):
    rows, cols = w.shape
    tile = rows
    for cand in range(min(rows // 2, 512) // 8 * 8, 7, -8):
        if rows % cand == 0:
            tile = cand
            break
    spec = pl.BlockSpec((tile, cols), lambda t: (t, 0))

    def body(w_ref, g_ref, m_ref, v_ref, *refs):
        d_ref, nm_ref, nv_ref = refs[-3:]
        d_ref[...], nm_ref[...], nv_ref[...] = _adamw_math(w_ref[...], g_ref[...], m_ref[...], v_ref[...])

    shp = jax.ShapeDtypeStruct((rows, cols), F32)
    follows = [] if token is None else [token]
    return pl.pallas_call(body, name=name, grid=(rows // tile,),
                          in_specs=[spec] * 4 + [pl.BlockSpec(t.shape, lambda t_: (0, 0)) for t in follows],
                          out_specs=[spec] * 3, out_shape=[shp] * 3, compiler_params=_params("parallel"))(w, g, m, v, *follows)


_WEIGHTS = ("w_ada", "b_ada", "w_in", "sinks", "w_branch_a", "w_branch_b", "w_o", "ln1_g", "ln1_b", "w_gate_up", "w_down",
            "ln2_g", "ln2_b")
_TRANSPOSED = ("w_in", "w_branch_b", "w_gate_up")


def _pack_shard(name, w):
    w = w.astype(BF16)
    if name in _TRANSPOSED:
        w = w.T
    return w.reshape(-1, D_MODEL)


def _unpack_full(name, slab):
    if name == "w_branch_b":
        return slab.reshape(N_DEV * 128, 512)
    return slab.reshape(-1, D_MODEL)


def _unpack_group(group, gathered):
    return {n: _unpack_full(n, slab) for (n, _), slab in zip(group, gathered)}


def _unpack_grads(group, g_packed):
    g_w, off = {}, 0
    for n, r in group:
        part = g_packed[off:off + r]
        off += r
        g_w[n] = part.reshape(128, 512) if n == "w_branch_b" else part
    return g_w


def kernel(x, c, positions, w_ada, b_ada, w_in, sinks, w_branch_a, w_branch_b, w_o, ln1_g, ln1_b, w_gate_up, w_down, ln2_g, ln2_b, loss_target, m_w_ada, m_b_ada, m_w_in, m_sinks, m_w_branch_a, m_w_branch_b, m_w_o, m_ln1_g, m_ln1_b, m_w_gate_up, m_w_down, m_ln2_g, m_ln2_b, v_w_ada, v_b_ada, v_w_in, v_sinks, v_w_branch_a, v_w_branch_b, v_w_o, v_ln1_g, v_ln1_b, v_w_gate_up, v_w_down, v_ln2_g, v_ln2_b):
    weights = dict(w_ada=w_ada, b_ada=b_ada, w_in=w_in, sinks=sinks, w_branch_a=w_branch_a, w_branch_b=w_branch_b, w_o=w_o,
                   ln1_g=ln1_g, ln1_b=ln1_b, w_gate_up=w_gate_up, w_down=w_down, ln2_g=ln2_g, ln2_b=ln2_b)
    m_in = dict(w_ada=m_w_ada, b_ada=m_b_ada, w_in=m_w_in, sinks=m_sinks, w_branch_a=m_w_branch_a, w_branch_b=m_w_branch_b,
                w_o=m_w_o, ln1_g=m_ln1_g, ln1_b=m_ln1_b, w_gate_up=m_w_gate_up, w_down=m_w_down, ln2_g=m_ln2_g, ln2_b=m_ln2_b)
    v_in = dict(w_ada=v_w_ada, b_ada=v_b_ada, w_in=v_w_in, sinks=v_sinks, w_branch_a=v_w_branch_a, w_branch_b=v_w_branch_b,
                w_o=v_w_o, ln1_g=v_ln1_g, ln1_b=v_ln1_b, w_gate_up=v_w_gate_up, w_down=v_w_down, ln2_g=v_ln2_g, ln2_b=v_ln2_b)
    bsz = x.shape[0]
    me = _index(_my_place())
    ada_cols = w_ada.shape[2]
    outs = {}

    def adamw(n, g, token=None):
        w2, m2, v2 = (t[n][0] if t[n].ndim == 3 else t[n] for t in (weights, m_in, v_in))
        shape = weights[n].shape
        if n in _TRANSPOSED:
            dlt, nm, nv = _adamw(w2.T, g, m2.T, v2.T, "adamw_" + n, token)
            outs[n] = tuple(t.T.reshape(shape) for t in (g, dlt, nm, nv))
        else:
            dlt, nm, nv = _adamw(w2, g, m2, v2, "adamw_" + n, token)
            outs[n] = tuple(t.reshape(shape) for t in (g, dlt, nm, nv))
        return nv

    packed_in = [_pack_shard(n, weights[n][0]) for n, _ in GROUP_IN]
    packed_rest = [_pack_shard(n, weights[n][0]) for n, _ in GROUP_REST]
    c_all = _gather_small(jnp.pad(c, ((0, 8 - bsz), (0, 0))), "gather_c")[:, :bsz].reshape(N_DEV * bsz, D_MODEL)
    gather_in = _Gather("gather_w_in", lax.optimization_barrier((packed_in, c_all))[0])
    b_cols = lax.dynamic_slice_in_dim(b_ada, me * ada_cols, ada_cols, axis=1)
    mod_cols = _ada_fwd(c_all, w_ada[0], b_cols + gather_in.token[0, 0])
    tables = _rope_tables(positions)
    mod_cols, tables, packed_rest = lax.optimization_barrier((mod_cols, tables, packed_rest))
    mod_all = _gather_small(mod_cols, "gather_mod").transpose(1, 0, 2).reshape(N_DEV * bsz, 6, D_MODEL)
    gather_rest = _Gather("gather_rest", lax.optimization_barrier((packed_rest, mod_all))[0])
    mod = jnp.pad(lax.dynamic_slice_in_dim(mod_all, me * bsz, bsz, axis=0), ((0, 0), (0, 2), (0, 0)))
    mod = mod + gather_rest.token[0, 0]
    mod = mod + gather_in.pass_on(mod)[0, 0]

    scatters, rest_grads = {}, {}

    def get_w_in(after):
        return _unpack_group(GROUP_IN, gather_in.finish(after))["w_in"]

    def get_rest(after):
        return _unpack_group(GROUP_REST, gather_rest.finish(after))

    def pack_grads(group, grads):
        return jnp.concatenate([grads[n].reshape(N_DEV, r, D_MODEL) for n, r in group], axis=1)

    def hook(point, value):
        if point == "projected":
            return gather_rest.pass_on(value)
        if point == "grads_rest":
            scatters["rest"] = _ReduceScatter("scatter_rest", pack_grads(GROUP_REST, value))
            return scatters["rest"].token
        if point == "delta_done":
            return scatters["rest"].between_chips(value)
        if point == "grads_w_in":
            scatters["in"] = _ReduceScatter("scatter_w_in", pack_grads(GROUP_IN, value))
            rest_grads.update(_unpack_grads(GROUP_REST, scatters["rest"].finish(scatters["in"].token)))
            return scatters["in"].between_chips(lax.optimization_barrier(tuple(rest_grads.values()))[0])
        if point == "dgrad_done":
            return None
        raise ValueError(point)

    loss_part, grad_x, dmod, small = _layer_step(x, mod, tables, sinks[0], ln1_g, ln1_b, ln2_g, ln2_b, loss_target,
                                                 get_w_in, get_rest, hook)

    rows = jnp.concatenate([dmod.reshape(bsz * 6, D_MODEL), small, jnp.full((1, D_MODEL), loss_part, F32),
                            jnp.zeros((SMALL_ROWS - bsz * 6 - 6, D_MODEL), F32)], axis=0)
    land = lax.dynamic_update_slice(lax.empty((N_DEV,) + rows.shape, rows.dtype), rows[None], (me, 0, 0))
    gather_small = _copies_start("gather_small_start", [rows, land], _to_all_copies, N_DEV - 1)
    follow = gather_small[3]
    for n, g in rest_grads.items():
        follow = adamw(n, g, follow)[:8, :LANES]
    small_all = _copies_wait("gather_small_wait", gather_small, _to_all_copies, follow)[1]
    sums = _reduce_small(small_all)
    loss = sums[bsz * 6 + 5, 0]
    dmod_all = small_all[:, :bsz * 6].reshape(N_DEV * bsz, 6 * D_MODEL)
    small_g = {"b_ada": functools.reduce(jnp.add, [sums[6 * i:6 * i + 6] for i in range(bsz)]).reshape(1, 6 * D_MODEL),
               "sinks": sums[bsz * 6 + 4][::HEAD_DIM][None]}
    small_g.update({n: sums[bsz * 6 + i][None] for i, n in enumerate(("ln1_g", "ln1_b", "ln2_g", "ln2_b"))})
    names = list(small_g)
    for n, (dlt, nm, nv) in zip(names, _adamw_small([weights[n] for n in names], [small_g[n] for n in names],
                                                     [m_in[n] for n in names], [v_in[n] for n in names], "adamw_small")):
        outs[n] = (small_g[n], dlt, nm, nv)
    dmod_cols = lax.dynamic_slice_in_dim(dmod_all, me * ada_cols, ada_cols, axis=1)
    adamw("w_ada", _ada_wgrad(c_all.T, dmod_cols))
    done = lax.optimization_barrier(tuple(outs[n][3] for n in outs))
    for n, g in _unpack_grads(GROUP_IN, scatters["in"].finish(done[0])).items():
        adamw(n, g)

    return (loss, grad_x, *[outs[n][0] for n in _WEIGHTS], *[outs[n][1] for n in _WEIGHTS], *[outs[n][2] for n in _WEIGHTS],
            *[outs[n][3] for n in _WEIGHTS])
```

```python
import functools

import jax
import jax.numpy as jnp
from jax import lax
from jax.experimental import pallas as pl
from jax.experimental.pallas import tpu as pltpu

F32 = jnp.float32
BF16 = jnp.bfloat16

D_MODEL = 1024
HEAD_DIM = 64
A_Q_HEADS = 16
A_WINDOW = 128
B_PATTERNS = ((128, 1), (512, 4), (2048, 16))
B_HEADS_PER_GROUP = 8
D_FF = 2816
QBLOCK = 128
ROPE_THETA = 10000.0
LN_EPS = 1e-5
DEEPNORM_ALPHA = 2.0 ** 0.25
NEG_INF = -1e30
ADAM_LR, ADAM_B1, ADAM_B2, ADAM_EPS, ADAM_WD, ADAM_STEP = 0.001, 0.9, 0.999, 1e-08, 0.01, 10

N_DEV = 8
LANES = 128
VMEM_LIMIT_BYTES = 56 * 1024 * 1024
MESH = pl.DeviceIdType.MESH

OFF_QA, OFF_KVA, OFF_QKVB, OFF_GAB = 0, 1024, 1280, 5888
GROUP_IN = (("w_in", 992),)
GROUP_REST = (("w_branch_a", 128), ("w_branch_b", 64), ("w_o", 128), ("w_gate_up", 704), ("w_down", 352))


def _params(*sem):
    return pltpu.CompilerParams(dimension_semantics=sem, vmem_limit_bytes=VMEM_LIMIT_BYTES)


def _sigmoid(x):
    return 1.0 / (1.0 + jnp.exp(-x))


_DIMS = {"nn": (((1,), (0,)), ((), ())), "nt": (((1,), (1,)), ((), ())), "tn": (((0,), (0,)), ((), ()))}


def _matmul(a, b, *, mode, tm, tn, tk, name, out_dtype=None, n=None, b_off=0, token=None, ins=(), outs=None, epilogue=None,
            lhs_fn=None):
    if mode == "nn":
        (m, k), nn_ = a.shape, b.shape[1]
    elif mode == "nt":
        (m, k), nn_ = a.shape, (b.shape[0] if n is None else n)
    else:
        (k, m), nn_ = a.shape, b.shape[1]
    assert m % tm == 0 and nn_ % tn == 0 and k % tk == 0 and b_off % tn == 0, (name, m, nn_, k)
    nk = k // tk
    joff = b_off // tn
    if mode == "nn":
        a_spec = pl.BlockSpec((tm, tk), lambda i, j, kk: (i, kk))
        whole = dict(pipeline_mode=pl.Buffered(1)) if (tk, tn) == b.shape else {}
        b_spec = pl.BlockSpec((tk, tn), lambda i, j, kk: (kk, j), **whole)
    elif mode == "nt":
        a_spec = pl.BlockSpec((tm, tk), lambda i, j, kk: (i, kk))
        b_spec = pl.BlockSpec((tn, tk), lambda i, j, kk: (j + joff, kk))
    else:
        a_spec = pl.BlockSpec((tk, tm), lambda i, j, kk: (kk, i))
        b_spec = pl.BlockSpec((tk, tn), lambda i, j, kk: (kk, j))
    dims = _DIMS[mode]
    has_token = token is not None
    plain = epilogue is None
    if plain:
        outs = [(jax.ShapeDtypeStruct((m, nn_), out_dtype), (tm, tn), lambda i, j: (i, j))]

        def epilogue(acc, i, j, in_refs, out_refs):
            out_refs[0][...] = acc.astype(out_refs[0].dtype)

    nin = len(ins)
    nscratch = 1 if lhs_fn is None else 2
    assert lhs_fn is None or nk == 1

    def body(*refs):
        a_ref, b_ref = refs[:2]
        in_refs = refs[2:2 + nin]
        out_refs = refs[2 + nin + has_token:-nscratch]
        acc_ref = refs[-nscratch]
        kk = pl.program_id(2)
        if lhs_fn is None:
            lhs = a_ref[...].astype(BF16)
        else:
            lhs_ref = refs[-1]

            @pl.when(pl.program_id(1) == 0)
            def _():
                lhs_ref[...] = lhs_fn(a_ref, in_refs, out_refs)

            lhs = lhs_ref[...]
        part = lax.dot_general(lhs, b_ref[...].astype(BF16), dims, preferred_element_type=F32)

        def finish(acc):
            epilogue(acc, pl.program_id(0), pl.program_id(1), in_refs, out_refs)

        if nk == 1:
            finish(part)
        else:
            @pl.when(kk == 0)
            def _():
                acc_ref[...] = part

            @pl.when(kk > 0)
            def _():
                acc_ref[...] += part

            @pl.when(kk == nk - 1)
            def _():
                finish(acc_ref[...])

    def spec(block, index):
        return pl.BlockSpec(block, lambda i, j, kk: index(i, j))

    in_specs, args = [a_spec, b_spec], [a, b]
    for arr, block, index in ins:
        in_specs.append(spec(block, index))
        args.append(arr)
    if has_token:
        in_specs.append(pl.BlockSpec(token.shape, lambda i, j, kk: (0, 0)))
        args.append(token)
    res = pl.pallas_call(
        body,
        name=name,
        grid=(m // tm, nn_ // tn, nk),
        in_specs=in_specs,
        out_specs=[spec(block, index) for _, block, index in outs],
        out_shape=[shape for shape, _, _ in outs],
        scratch_shapes=[pltpu.VMEM((tm, tn) if nk > 1 else (8, LANES), F32)] + ([] if lhs_fn is None else [pltpu.VMEM((tm, tk), BF16)]),
        compiler_params=_params("arbitrary", "arbitrary", "arbitrary"),
    )(*args)
    return res[0] if plain else res


def _proj_rope(a, bt, cos, sin, *, n, b_off, rope_cols, tm, tn, name, out_dtype=F32):
    m, k = a.shape
    assert m % tm == 0 and n % tn == 0 and b_off % tn == 0 and rope_cols % LANES == 0, name
    joff = b_off // tn
    nrope, part = divmod(rope_cols, tn)

    def body(a_ref, b_ref, c_ref, s_ref, o_ref):
        acc = lax.dot_general(a_ref[...], b_ref[...], _DIMS["nt"], preferred_element_type=F32)
        j = pl.program_id(1)

        @pl.when(j < nrope)
        def _():
            o_ref[...] = _rope(acc, c_ref[...], s_ref[...], coarse=True).astype(o_ref.dtype)

        if part:
            @pl.when(j == nrope)
            def _():
                o_ref[:, :part] = _rope(acc[:, :part], c_ref[...], s_ref[...], coarse=True).astype(o_ref.dtype)
                o_ref[:, part:] = acc[:, part:].astype(o_ref.dtype)

        @pl.when(j >= nrope + (1 if part else 0))
        def _():
            o_ref[...] = acc.astype(o_ref.dtype)

    table = pl.BlockSpec((tm, LANES), lambda i, j: (i, 0))
    return pl.pallas_call(
        body,
        name=name,
        grid=(m // tm, n // tn),
        in_specs=[pl.BlockSpec((tm, k), lambda i, j: (i, 0)), pl.BlockSpec((tn, k), lambda i, j: (j + joff, 0)), table, table],
        out_specs=pl.BlockSpec((tm, tn), lambda i, j: (i, j)),
        out_shape=jax.ShapeDtypeStruct((m, n), out_dtype),
        compiler_params=_params("parallel", "parallel"),
    )(a, bt, cos, sin)


ROW_TILE = 256


def _rows(width, col=0):
    return pl.BlockSpec((1, ROW_TILE, width), lambda b, t: (b, t, col))


def _per_batch(nrows, width):
    return pl.BlockSpec((1, nrows, width), lambda b, t: (b, 0, 0))


def _row_call(body, name, bsz, seq, in_specs, out_specs, out_shape, accumulates=False):
    return pl.pallas_call(
        body,
        name=name,
        grid=(bsz, seq // ROW_TILE),
        in_specs=in_specs,
        out_specs=out_specs,
        out_shape=out_shape,
        compiler_params=_params("parallel", "arbitrary" if accumulates else "parallel"),
    )


def _acc_rows(acc_ref, first, rows):
    @pl.when(first)
    def _():
        acc_ref[...] = jnp.zeros_like(acc_ref)

    for r, val in enumerate(rows):
        acc_ref[0, r:r + 1, :] += val


def _colsum(v):
    return jnp.sum(v, axis=0, keepdims=True)


def _ln_stats(z):
    mu = jnp.mean(z, axis=-1, keepdims=True)
    zc = z - mu
    var = jnp.mean(zc * zc, axis=-1, keepdims=True)
    rstd = lax.rsqrt(var + LN_EPS)
    return zc * rstd, rstd


def _ln_bwd(dxhat, xhat, rstd):
    m1 = jnp.mean(dxhat, axis=-1, keepdims=True)
    m2 = jnp.mean(dxhat * xhat, axis=-1, keepdims=True)
    return rstd * (dxhat - m1 - xhat * m2)


def _modulate_in(x, mod):
    bsz, seq, d = x.shape

    def body(x_ref, mod_ref, u_ref):
        u_ref[0] = (x_ref[0] * (1.0 + mod_ref[0, 1:2, :]) + mod_ref[0, 0:1, :]).astype(BF16)

    return _row_call(body, "modulate_in", bsz, seq, [_rows(d), _per_batch(8, d)], _rows(d),
                     jax.ShapeDtypeStruct((bsz, seq, d), BF16))(x, mod)


EP_TILE = 512


def _ep_specs(seq, d):
    tiles = seq // EP_TILE
    return ((EP_TILE, d), lambda i, j: (i, 0)), ((1, 8, d), lambda i, j: (i // tiles, 0, 0)), ((1, d), lambda i, j: (0, 0))


def _wo_ln1(merged, wo, x, mod, g, b, seq):
    ntok, d = x.shape
    row, per_b, whole = _ep_specs(seq, d)

    def epilogue(y, i, j, ins, outs):
        x_ref, mod_ref, g_ref, b_ref = ins
        y_ref, h_ref, u_ref = outs
        z = DEEPNORM_ALPHA * x_ref[...] + (1.0 + mod_ref[0, 2:3, :]) * y
        xhat, _ = _ln_stats(z)
        h = xhat * g_ref[...] + b_ref[...]
        y_ref[...] = y
        h_ref[...] = h
        u_ref[...] = (h * (1.0 + mod_ref[0, 4:5, :]) + mod_ref[0, 3:4, :]).astype(BF16)

    f32, bf16 = jax.ShapeDtypeStruct((ntok, d), F32), jax.ShapeDtypeStruct((ntok, d), BF16)
    return _matmul(merged, wo, mode="nn", tm=EP_TILE, tn=d, tk=d, name="w_o_ln1",
                   ins=[(x,) + row, (mod,) + per_b, (g,) + whole, (b,) + whole],
                   outs=[(f32,) + row, (f32,) + row, (bf16,) + row], epilogue=epilogue)


FF_HALF = D_FF // 2


def _interleave_gate_up(w):
    return w.reshape(2, 2, FF_HALF, w.shape[1]).transpose(1, 0, 2, 3).reshape(w.shape)


def _gate_up_silu(u2, wgut_i):
    ntok = u2.shape[0]

    def epilogue(h, i, j, ins, outs):
        h_ref, a_ref = outs
        hg, hu = h[:, :FF_HALF], h[:, FF_HALF:]
        h_ref[...] = h.astype(BF16)
        a_ref[...] = (hg * _sigmoid(hg) * hu).astype(BF16)

    return _matmul(u2, wgut_i, mode="nt", tm=EP_TILE, tn=2 * FF_HALF, tk=u2.shape[1], name="gate_up_silu",
                   outs=[(jax.ShapeDtypeStruct((ntok, 2 * D_FF), BF16), (EP_TILE, 2 * FF_HALF), lambda i, j: (i, j)),
                         (jax.ShapeDtypeStruct((ntok, D_FF), BF16), (EP_TILE, FF_HALF), lambda i, j: (i, j))],
                   epilogue=epilogue)


def _down_dgrad_silu_bwd(dy2, wd, h_i):
    ntok = dy2.shape[0]
    wide = ((EP_TILE, 2 * FF_HALF), lambda i, j: (i, j))

    def epilogue(da, i, j, ins, outs):
        h = ins[0][...].astype(F32)
        hg, hu = h[:, :FF_HALF], h[:, FF_HALF:]
        sg = _sigmoid(hg)
        outs[0][:, :FF_HALF] = (da * hu * (sg * (1.0 + hg * (1.0 - sg)))).astype(BF16)
        outs[0][:, FF_HALF:] = (da * (hg * sg)).astype(BF16)

    return _matmul(dy2, wd, mode="nt", tm=EP_TILE, tn=FF_HALF, tk=dy2.shape[1], name="down_dgrad_silu_bwd",
                   ins=[(h_i,) + wide], outs=[(jax.ShapeDtypeStruct((ntok, 2 * D_FF), BF16),) + wide], epilogue=epilogue)[0]


def _down_ln2_loss_bwd(a, wd, h1, mod, g, b, target, seq):
    ntok, d = h1.shape
    row, per_b, whole = _ep_specs(seq, d)
    tiles = seq // EP_TILE

    def epilogue(y, i, j, ins, outs):
        h_ref, mod_ref, g_ref, b_ref, t_ref = ins
        dy_ref, dh_ref, acc_ref = outs
        gate = 1.0 + mod_ref[0, 5:6, :]
        z = DEEPNORM_ALPHA * h_ref[...] + gate * y
        xhat, rstd = _ln_stats(z)
        diff = xhat * g_ref[...] + b_ref[...] - t_ref[...]
        loss = 0.5 * jnp.sum(jnp.sum(diff * diff, axis=-1, keepdims=True) / d, axis=0, keepdims=True)
        dout = diff / d
        dz = _ln_bwd(dout * g_ref[...], xhat, rstd)
        dy_ref[...] = (gate * dz).astype(BF16)
        dh_ref[...] = DEEPNORM_ALPHA * dz
        _acc_rows(acc_ref, i % tiles == 0,
                  [_colsum(dout * xhat), _colsum(dout), _colsum(dz * y), jnp.broadcast_to(loss, (1, d))])

    return _matmul(a, wd, mode="nn", tm=EP_TILE, tn=d, tk=a.shape[1], name="down_ln2_loss_bwd",
                   ins=[(h1,) + row, (mod,) + per_b, (g,) + whole, (b,) + whole, (target,) + row],
                   outs=[(jax.ShapeDtypeStruct((ntok, d), BF16),) + row, (jax.ShapeDtypeStruct((ntok, d), F32),) + row,
                         (jax.ShapeDtypeStruct((ntok // seq, 8, d), F32),) + per_b], epilogue=epilogue)


def _gate_up_dgrad_ln1_bwd(dh, wgut, dh1a, x, y1, mod, g, b, seq):
    ntok, d = x.shape
    row, per_b, whole = _ep_specs(seq, d)
    tiles = seq // EP_TILE

    def epilogue(du, i, j, ins, outs):
        dh_ref, x_ref, y_ref, mod_ref, g_ref, b_ref = ins
        dy_ref, dx_ref, acc_ref = outs
        y = y_ref[...]
        gate = 1.0 + mod_ref[0, 2:3, :]
        z = DEEPNORM_ALPHA * x_ref[...] + gate * y
        xhat, rstd = _ln_stats(z)
        h1 = xhat * g_ref[...] + b_ref[...]
        dh1 = dh_ref[...] + du * (1.0 + mod_ref[0, 4:5, :])
        dz = _ln_bwd(dh1 * g_ref[...], xhat, rstd)
        dy_ref[...] = (gate * dz).astype(BF16)
        dx_ref[...] = DEEPNORM_ALPHA * dz
        _acc_rows(acc_ref, i % tiles == 0,
                  [_colsum(dh1 * xhat), _colsum(dh1), _colsum(dz * y), _colsum(du * h1), _colsum(du)])

    return _matmul(dh, wgut, mode="nn", tm=EP_TILE, tn=d, tk=dh.shape[1], name="gate_up_dgrad_ln1_bwd",
                   ins=[(dh1a,) + row, (x,) + row, (y1,) + row, (mod,) + per_b, (g,) + whole, (b,) + whole],
                   outs=[(jax.ShapeDtypeStruct((ntok, d), BF16),) + row, (jax.ShapeDtypeStruct((ntok, d), F32),) + row,
                         (jax.ShapeDtypeStruct((ntok // seq, 8, d), F32),) + per_b], epilogue=epilogue)


def _wo_dgrad_gate_bwd(dy1, wo, gab, ya, yb):
    ntok, d = ya.shape
    tm, tn = 1024, 512
    tile = ((tm, tn), lambda i, j: (i, j))
    tile_b = ((tm, tn), lambda i, j: (i, j + d // tn))

    def epilogue(dm_, i, j, ins, outs):
        ga_ref, gb_ref, ya_ref, yb_ref = ins
        dya_ref, dyb_ref, dga_ref, dgb_ref = outs
        sa, sb = _sigmoid(ga_ref[...].astype(F32)), _sigmoid(gb_ref[...].astype(F32))
        dya_ref[...] = (dm_ * sa).astype(BF16)
        dyb_ref[...] = (dm_ * sb).astype(BF16)
        dga_ref[...] = (dm_ * ya_ref[...].astype(F32) * sa * (1.0 - sa)).astype(BF16)
        dgb_ref[...] = (dm_ * yb_ref[...].astype(F32) * sb * (1.0 - sb)).astype(BF16)

    shp = jax.ShapeDtypeStruct((ntok, d), BF16)
    return _matmul(dy1, wo, mode="nt", tm=tm, tn=tn, tk=d, name="w_o_dgrad_gate_bwd",
                   ins=[(gab,) + tile, (gab,) + tile_b, (ya,) + tile, (yb,) + tile],
                   outs=[(shp,) + tile] * 4, epilogue=epilogue)


def _w_in_dgrad_grad_x(dproj, wint, dxa, x, mod, seq, token):
    ntok, d = x.shape
    row, per_b, _ = _ep_specs(seq, d)
    tiles = seq // EP_TILE

    def epilogue(du, i, j, ins, outs):
        dxa_ref, x_ref, mod_ref = ins
        gx_ref, acc_ref = outs
        gx_ref[...] = dxa_ref[...] + du * (1.0 + mod_ref[0, 1:2, :])
        _acc_rows(acc_ref, i % tiles == 0, [_colsum(du * x_ref[...]), _colsum(du)])

    return _matmul(dproj, wint, mode="nn", tm=EP_TILE, tn=d, tk=wint.shape[0], name="w_in_dgrad_grad_x", token=token,
                   ins=[(dxa,) + row, (x,) + row, (mod,) + per_b],
                   outs=[(jax.ShapeDtypeStruct((ntok, d), F32),) + row, (jax.ShapeDtypeStruct((ntok // seq, 8, d), F32),) + per_b],
                   epilogue=epilogue)


def _merge_branch_b_gate(os_, ls_, wbbt, gab, ya):
    ntok, d = ya.shape
    w = os_[0].shape[1]
    tm, tn = 1024, 512
    tile = ((tm, tn), lambda i, j: (i, j))
    tile_b = ((tm, tn), lambda i, j: (i, j + d // tn))
    row = ((tm, w), lambda i, j: (i, 0))

    def lhs_fn(o0_ref, ins, outs):
        os_r, ls_r = (o0_ref,) + tuple(ins[3:5]), ins[5:8]
        ls = [l[...] for l in ls_r]
        mx = jnp.maximum(jnp.maximum(ls[0], ls[1]), ls[2])
        es = [jnp.exp(l - mx) for l in ls]
        den = es[0] + es[1] + es[2]
        ob = functools.reduce(jnp.add, [(e / den) * o[...].astype(F32) for e, o in zip(es, os_r)]).astype(BF16)
        outs[2][...] = ob
        return ob

    def epilogue(yb, i, j, ins, outs):
        ga_ref, gb_ref, ya_ref = ins[:3]
        yb_ref, merged_ref = outs[:2]
        yb_ref[...] = yb.astype(BF16)
        merged_ref[...] = (_sigmoid(ga_ref[...].astype(F32)) * ya_ref[...].astype(F32)
                           + _sigmoid(gb_ref[...].astype(F32)) * yb).astype(BF16)

    shp = jax.ShapeDtypeStruct((ntok, d), BF16)
    return _matmul(os_[0], wbbt, mode="nt", tm=tm, tn=tn, tk=w, name="merge_branch_b_gate", lhs_fn=lhs_fn,
                   ins=[(gab,) + tile, (gab,) + tile_b, (ya,) + tile] + [(v,) + row for v in list(os_[1:]) + list(ls_)],
                   outs=[(shp,) + tile] * 2 + [(jax.ShapeDtypeStruct((ntok, w), BF16),) + row], epilogue=epilogue)


def _segsum64(v):
    rows, width = v.shape
    ri = lax.broadcasted_iota(jnp.int32, (LANES, LANES), 0) // HEAD_DIM
    ci = lax.broadcasted_iota(jnp.int32, (LANES, LANES), 1) // HEAD_DIM
    ones = jnp.where(ri == ci, 1.0, 0.0).astype(BF16)
    out = []
    for c in range(width // LANES):
        part = v[:, c * LANES:(c + 1) * LANES]
        hi = part.astype(BF16)
        lo = (part - hi.astype(F32)).astype(BF16)
        out.append(jnp.dot(hi, ones, preferred_element_type=F32) + jnp.dot(lo, ones, preferred_element_type=F32))
    return jnp.concatenate(out, axis=1) if len(out) > 1 else out[0]


def _branch_b_dgrad_merge_bwd(dyb, wbbt, os_, ls_):
    ntok, w = os_[0].shape
    row = ((EP_TILE, w), lambda i, j: (i, 0))

    def epilogue(dob_, i, j, ins, outs):
        os_r, ls_r = ins[:3], ins[3:]
        do_r, dd_r = outs[:3], outs[3:]
        ls = [l[...] for l in ls_r]
        mx = jnp.maximum(jnp.maximum(ls[0], ls[1]), ls[2])
        es = [jnp.exp(l - mx) for l in ls]
        den = es[0] + es[1] + es[2]
        ws = [e / den for e in es]
        dws = [_segsum64(dob_ * o[...].astype(F32)) for o in os_r]
        mean = ws[0] * dws[0] + ws[1] * dws[1] + ws[2] * dws[2]
        for wg, do_ref, dd_ref in zip(ws, do_r, dd_r):
            do_ref[...] = wg * dob_
            dd_ref[...] = -wg * mean

    shp = jax.ShapeDtypeStruct((ntok, w), F32)
    return _matmul(dyb, wbbt, mode="nn", tm=EP_TILE, tn=w, tk=dyb.shape[1], name="branch_b_dgrad_merge_bwd",
                   ins=[(v,) + row for v in list(os_) + list(ls_)], outs=[(shp,) + row] * 6, epilogue=epilogue)


def _branch_a_dgrad_delta(dya, wba, oa, lse_a, sinks_exp, seq):
    ntok, w = oa.shape
    row, per_b, whole = _ep_specs(seq, w)
    tiles = seq // EP_TILE

    def epilogue(do_, i, j, ins, outs):
        o_ref, l_ref, s_ref = ins
        do_ref, dd_ref, acc_ref = outs
        dd = -_segsum64(do_ * o_ref[...].astype(F32))
        do_ref[...] = do_.astype(BF16)
        dd_ref[...] = dd
        _acc_rows(acc_ref, i % tiles == 0, [_colsum(dd * jnp.exp(s_ref[...] - l_ref[...]))])

    shp = jax.ShapeDtypeStruct((ntok, w), F32)
    return _matmul(dya, wba, mode="nt", tm=EP_TILE, tn=w, tk=dya.shape[1], name="branch_a_dgrad_delta",
                   ins=[(oa,) + row, (lse_a,) + row, (sinks_exp,) + whole],
                   outs=[(jax.ShapeDtypeStruct((ntok, w), BF16),) + row, (shp,) + row,
                         (jax.ShapeDtypeStruct((ntok // seq, 8, w), F32),) + per_b],
                   epilogue=epilogue)


def _swap_halves(v):
    src = lax.broadcasted_iota(jnp.int32, (LANES, LANES), 0)
    dst = lax.broadcasted_iota(jnp.int32, (LANES, LANES), 1)
    partner = jnp.where((dst % HEAD_DIM) < HEAD_DIM // 2, dst + HEAD_DIM // 2, dst - HEAD_DIM // 2)
    perm = jnp.where(src == partner, 1.0, 0.0).astype(BF16)
    hi = v.astype(BF16)
    lo = (v - hi.astype(F32)).astype(BF16)
    return jnp.dot(hi, perm, preferred_element_type=F32) + jnp.dot(lo, perm, preferred_element_type=F32)


def _swap_halves_roll(v):
    lane = lax.broadcasted_iota(jnp.int32, v.shape, 1)
    return jnp.where((lane % HEAD_DIM) < HEAD_DIM // 2, pltpu.roll(v, LANES - HEAD_DIM // 2, 1),
                     pltpu.roll(v, HEAD_DIM // 2, 1))


def _swap_halves_coarse(v):
    src = lax.broadcasted_iota(jnp.int32, (LANES, LANES), 0)
    dst = lax.broadcasted_iota(jnp.int32, (LANES, LANES), 1)
    partner = jnp.where((dst % HEAD_DIM) < HEAD_DIM // 2, dst + HEAD_DIM // 2, dst - HEAD_DIM // 2)
    perm = jnp.where(src == partner, 1.0, 0.0).astype(BF16)
    return jnp.dot(v.astype(BF16), perm, preferred_element_type=F32)


def _rope(v, cos, sin, sign=1.0, mxu=True, coarse=False):
    swap = (_swap_halves_coarse if coarse else _swap_halves) if mxu else _swap_halves_roll
    out = []
    for c in range(v.shape[1] // LANES):
        part = v[:, c * LANES:(c + 1) * LANES]
        out.append(part * cos + sign * (swap(part) * sin))
    return jnp.concatenate(out, axis=1) if len(out) > 1 else out[0]


def _half_mask(shape, half):
    lane = lax.broadcasted_iota(jnp.int32, shape, len(shape) - 1) % LANES
    return (lane < HEAD_DIM) if half == 0 else (lane >= HEAD_DIM)


def _dup_half(v, half):
    return jnp.where(_half_mask(v.shape, half), v, pltpu.roll(v, HEAD_DIM, 1))


def _fold_halves(v):
    return v + pltpu.roll(v, HEAD_DIM, 1)


def _pick_halves(lo_rows, hi_rows):
    return jnp.where(_half_mask(lo_rows.shape, 0), lo_rows, hi_rows)


def _stack_masked(v, pairs):
    parts = []
    for c in pairs:
        pair = v[:, c * LANES:(c + 1) * LANES]
        parts += [jnp.where(_half_mask(pair.shape, half), pair, 0.0) for half in (0, 1)]
    return jnp.concatenate(parts, axis=0)


def _stack_pair_cols(v, pairs):
    return jnp.concatenate([v[:, c * LANES + half * HEAD_DIM:c * LANES + half * HEAD_DIM + 1] for c in pairs for half in (0, 1)],
                           axis=0)


ATTN_UNITS = 16


def _class_rows(r):
    return [pl.ds(0, QBLOCK)] if r == 1 else [pl.ds(rho, QBLOCK, stride=r) for rho in range(r)]


def _band_mask(nrows, nk, blk, n_back, has_prev):
    qi = lax.broadcasted_iota(jnp.int32, (nrows, nk), 0) % QBLOCK
    ki = lax.broadcasted_iota(jnp.int32, (nrows, nk), 1)
    if has_prev:
        dist = qi + QBLOCK - ki
        return (dist >= 0) & (dist <= n_back) & ((ki >= QBLOCK) | (blk > 0))
    dist = qi - ki
    return (dist >= 0) & (dist <= n_back)


def _attn_fwd(q_arr, k_arr, v_arr, *, name, npair, gqa, q_col, k_col, v_col, nchunk, r, n_back, sinks=None):
    bsz, seq, _ = q_arr.shape
    rr = QBLOCK * r
    nblk = seq // rr
    qw = npair * LANES
    kw = LANES if gqa else qw
    has_prev = nblk > 1
    has_sink = sinks is not None
    scale = HEAD_DIM ** -0.5

    def body(*refs):
        refs = list(refs)
        q_ref, kc_ref, vc_ref = refs[:3]
        pos = 3
        if has_prev:
            kp_ref, vp_ref = refs[pos:pos + 2]
            pos += 2
        if has_sink:
            sink_ref = refs[pos]
            pos += 1
        o_ref, lse_ref = refs[pos:pos + 2]
        if r > 1:
            stage_o = refs[pos + 2]
        blk = pl.program_id(2)
        nk = (2 if has_prev else 1) * QBLOCK
        valid = _band_mask(QBLOCK, nk, blk, n_back, has_prev)
        per = npair // 2
        classes = _class_rows(r)
        step = max(1, ATTN_UNITS // (2 * npair))
        for first in range(0, len(classes), step):
            batch = classes[first:first + step]
            units = []
            for ci, rows in enumerate(batch):
                q = q_ref[0, rows, :] * scale
                k, v = kc_ref[0, rows, :], vc_ref[0, rows, :]
                if has_prev:
                    k = jnp.concatenate([kp_ref[0, rows, :], k], axis=0)
                    v = jnp.concatenate([vp_ref[0, rows, :], v], axis=0)
                if gqa:
                    kdup = [_dup_half(k, hk).astype(BF16) for hk in range(2)]
                    vdup = [_dup_half(v, hk) for hk in range(2)]
                for c in range(npair):
                    sl = slice(c * LANES, (c + 1) * LANES)
                    qc = q[:, sl]
                    kc, vc = (kdup[c // per], vdup[c // per]) if gqa else (k[:, sl].astype(BF16), v[:, sl])
                    for half in (0, 1):
                        qm = jnp.where(_half_mask(qc.shape, half), qc, 0.0).astype(BF16)
                        vm = jnp.where(_half_mask(vc.shape, half), vc, 0.0).astype(BF16)
                        s = lax.dot_general(qm, kc, _DIMS["nt"], preferred_element_type=F32)
                        units.append(dict(ci=ci, c=c, half=half, s=s, vm=vm, sk=sink_ref[2 * c + half] if has_sink else None))
            for u in units:
                s = jnp.where(valid, u["s"], NEG_INF)
                m = jnp.max(s, axis=1, keepdims=True)
                if has_sink:
                    m = jnp.maximum(m, u["sk"])
                p = jnp.exp(s - m)
                den = jnp.sum(p, axis=1, keepdims=True)
                if has_sink:
                    den = den + jnp.exp(u["sk"] - m)
                u.update(p=p.astype(BF16), den=den, lse=m + jnp.log(den))
            for u in units:
                u["o"] = jnp.dot(u["p"], u["vm"], preferred_element_type=F32) / u["den"]
            for ci, rows in enumerate(batch):
                outs, lses = [None] * npair, [None] * npair
                for u in units:
                    if u["ci"] != ci:
                        continue
                    c, o = u["c"], u["o"]
                    lse = jnp.broadcast_to(u["lse"], o.shape)
                    outs[c] = o if u["half"] == 0 else outs[c] + o
                    lses[c] = lse if u["half"] == 0 else _pick_halves(lses[c], lse)
                o_new = jnp.concatenate(outs, axis=1) if npair > 1 else outs[0]
                if r > 1:
                    stage_o[rows, :] = o_new
                else:
                    o_ref[0] = o_new.astype(BF16)
                lse_ref[0, rows, :] = jnp.concatenate(lses, axis=1) if npair > 1 else lses[0]
        if r > 1:
            o_ref[0] = stage_o[...].astype(BF16)

    def cur(width, col0):
        return pl.BlockSpec((1, rr, width), lambda b, c, i: (b, i, col0 + c))

    def prev(width, col0):
        return pl.BlockSpec((1, rr, width), lambda b, c, i: (b, jnp.maximum(i - 1, 0), col0 + c))

    in_specs = [cur(qw, q_col), cur(kw, k_col), cur(kw, v_col)]
    args = [q_arr, k_arr, v_arr]
    if has_prev:
        in_specs += [prev(kw, k_col), prev(kw, v_col)]
        args += [k_arr, v_arr]
    if has_sink:
        in_specs.append(pl.BlockSpec(memory_space=pltpu.SMEM))
        args.append(sinks)
    return pl.pallas_call(
        body,
        name=name,
        grid=(bsz, nchunk, nblk),
        in_specs=in_specs,
        out_specs=[pl.BlockSpec((1, rr, qw), lambda b, c, i: (b, i, c))] * 2,
        out_shape=[jax.ShapeDtypeStruct((bsz, seq, nchunk * qw), BF16), jax.ShapeDtypeStruct((bsz, seq, nchunk * qw), F32)],
        scratch_shapes=[pltpu.VMEM((rr, qw), F32)] if r > 1 else [],
        compiler_params=_params("parallel", "parallel", "parallel"),
    )(*args)


def _attn_bwd(q_arr, k_arr, v_arr, cos, sin, do, lse, dd, *, name, npair, gqa, q_col, k_col, v_col, nchunk, r, n_back,
              token=None):
    bsz, seq, _ = q_arr.shape
    rr = QBLOCK * r
    nblk = seq // rr
    qw = npair * LANES
    kw = LANES if gqa else qw
    has_next = nblk > 1
    has_token = token is not None
    staged = r > 1
    scale = HEAD_DIM ** -0.5

    def body(*refs):
        refs = list(refs)
        k_ref, v_ref, c_ref, s_ref = refs[:4]
        tile_refs = [refs[4:8]]
        pos = 8
        if has_next:
            tile_refs.append(refs[pos:pos + 4])
            pos += 4
        if has_token:
            pos += 1
        dq_ref, dk_ref, dv_ref = refs[pos:pos + 3]
        carry_ref, bcast_ref = refs[pos + 3:pos + 5]
        if staged:
            stage_q, stage_k, stage_v = refs[pos + 5:pos + 8]
        blk = pl.program_id(2)
        if has_next:
            @pl.when(blk == 0)
            def _():
                carry_ref[...] = jnp.zeros_like(carry_ref)

        nrows = (npair if gqa else 1) * QBLOCK
        qi = lax.broadcasted_iota(jnp.int32, (nrows, QBLOCK), 0) % QBLOCK
        ki = lax.broadcasted_iota(jnp.int32, (nrows, QBLOCK), 1)
        valids = [qi >= ki, (qi + QBLOCK - ki <= n_back) & (blk + 1 < nblk)]
        per = npair // 2
        ntile = len(tile_refs)
        cat = lambda parts: jnp.concatenate(parts, axis=1) if len(parts) > 1 else parts[0]
        classes = _class_rows(r)
        step = max(1, ATTN_UNITS // (ntile * (2 if gqa else 2 * npair)))
        def stat_cols(stat, slot):
            if gqa:
                return _stack_pair_cols(stat, list(range(slot * per, (slot + 1) * per)))
            col = slot * HEAD_DIM
            return stat[:, col:col + 1]

        nslot = 2 if gqa else 2 * npair
        if has_next:
            @pl.when(blk == 0)
            def _():
                for rows in classes:
                    for which, stat_ref in enumerate(tile_refs[0][2:4]):
                        stat = stat_ref[0, rows, :]
                        for slot in range(nslot):
                            bcast_ref[which, slot, rows if not gqa else slice(None), :] = jnp.broadcast_to(
                                stat_cols(stat, slot), (nrows, LANES))

        for first in range(0, len(classes), step):
            batch = classes[first:first + step]
            units = []
            for ci, rows in enumerate(batch):
                keep = slice(None) if gqa else rows
                tiles = [(q_ref[0, rows, :] * scale, do_ref[0, rows, :], l_ref[0, rows, :], d_ref[0, rows, :])
                         for q_ref, do_ref, l_ref, d_ref in tile_refs]

                def stats(t, slot, keep=keep, tiles=tiles):
                    if has_next and t == 0:
                        return bcast_ref[0, slot, keep, :], bcast_ref[1, slot, keep, :]
                    return tuple(jnp.broadcast_to(stat_cols(tiles[t][2 + w], slot), (nrows, LANES)) for w in range(2))

                k, v = k_ref[0, rows, :], v_ref[0, rows, :]
                if gqa:
                    for hk in range(2):
                        pairs = list(range(hk * per, (hk + 1) * per))
                        kd, vd = _dup_half(k, hk).astype(BF16), _dup_half(v, hk).astype(BF16)
                        for t, (q, do_, l_, d_) in enumerate(tiles):
                            lcol, dcol = stats(t, hk)
                            units.append(dict(ci=ci, t=t, hk=hk, slot=hk, keep=keep, pairs=pairs,
                                              qs=_stack_masked(q, pairs).astype(BF16),
                                              dos=_stack_masked(do_, pairs).astype(BF16), lcol=lcol, dcol=dcol,
                                              kmat=kd, vmat=vd, kdq=kd))
                else:
                    for c in range(npair):
                        sl = slice(c * LANES, (c + 1) * LANES)
                        kc, vcb = k[:, sl], v[:, sl].astype(BF16)
                        kcb = kc.astype(BF16)
                        for t, (q, do_, l_, d_) in enumerate(tiles):
                            for half in (0, 1):
                                hm = _half_mask(kc.shape, half)
                                lcol, dcol = stats(t, 2 * c + half)
                                units.append(dict(ci=ci, t=t, c=c, half=half, slot=2 * c + half, keep=keep,
                                                  qs=jnp.where(hm, q[:, sl], 0.0).astype(BF16),
                                                  dos=jnp.where(hm, do_[:, sl], 0.0).astype(BF16), lcol=lcol, dcol=dcol,
                                                  kmat=kcb, vmat=vcb, kdq=jnp.where(hm, kc, 0.0).astype(BF16)))
            for u in units:
                u["s"] = lax.dot_general(u["qs"], u["kmat"], _DIMS["nt"], preferred_element_type=F32)
                u["dp"] = lax.dot_general(u["dos"], u["vmat"], _DIMS["nt"], preferred_element_type=F32)
            for u in units:
                p = jnp.exp(jnp.where(valids[u["t"]], u["s"], NEG_INF) - u["lcol"])
                u["ds"] = (p * (u["dp"] + u["dcol"])).astype(BF16)
                u["p"] = p.astype(BF16)
            for u in units:
                u["dv"] = lax.dot_general(u["p"], u["dos"], _DIMS["tn"], preferred_element_type=F32)
                u["dk"] = lax.dot_general(u["ds"], u["qs"], _DIMS["tn"], preferred_element_type=F32)
                u["dq"] = jnp.dot(u["ds"], u["kdq"], preferred_element_type=F32) * scale
            for u in units:
                if u["t"] == 1:
                    bcast_ref[0, u["slot"], u["keep"], :] = u["lcol"]
                    bcast_ref[1, u["slot"], u["keep"], :] = u["dcol"]
            for ci, rows in enumerate(batch):
                mine = [u for u in units if u["ci"] == ci]
                dq = [[None] * npair for _ in range(ntile)]
                if gqa:
                    dk_out = dv_out = None
                    for hk in range(2):
                        us = [u for u in mine if u["hk"] == hk]
                        for u in us:
                            for i, c in enumerate(u["pairs"]):
                                dq[u["t"]][c] = _pick_halves(u["dq"][2 * i * QBLOCK:(2 * i + 1) * QBLOCK],
                                                             u["dq"][(2 * i + 1) * QBLOCK:(2 * i + 2) * QBLOCK])
                        dk_h = _fold_halves(functools.reduce(jnp.add, [u["dk"] for u in us]))
                        dv_h = _fold_halves(functools.reduce(jnp.add, [u["dv"] for u in us]))
                        dk_out = dk_h if hk == 0 else _pick_halves(dk_out, dk_h)
                        dv_out = dv_h if hk == 0 else _pick_halves(dv_out, dv_h)
                else:
                    dks, dvs = [], []
                    for c in range(npair):
                        us = [u for u in mine if u["c"] == c]
                        dks.append(functools.reduce(jnp.add, [u["dk"] for u in us]))
                        dvs.append(functools.reduce(jnp.add, [u["dv"] for u in us]))
                        for t in range(ntile):
                            dq[t][c] = functools.reduce(jnp.add, [u["dq"] for u in us if u["t"] == t])
                    dk_out, dv_out = cat(dks), cat(dvs)
                ck, sk_ = c_ref[0, rows, :], s_ref[0, rows, :]
                dk_new = _rope(dk_out, ck, sk_, sign=-1.0, mxu=gqa, coarse=True)
                dq_cur = cat(dq[0])
                if has_next:
                    dq_cur = dq_cur + carry_ref[rows, :]
                    carry_ref[rows, :] = cat(dq[1])
                dq_new = _rope(dq_cur, ck, sk_, sign=-1.0, mxu=gqa, coarse=True)
                if staged:
                    stage_q[rows, :], stage_k[rows, :], stage_v[rows, :] = dq_new, dk_new, dv_out
                else:
                    dq_ref[0], dk_ref[0], dv_ref[0] = dq_new.astype(BF16), dk_new.astype(BF16), dv_out.astype(BF16)
        if staged:
            dq_ref[0], dk_ref[0], dv_ref[0] = stage_q[...].astype(BF16), stage_k[...].astype(BF16), stage_v[...].astype(BF16)

    def at(width, col0, shift):
        return pl.BlockSpec((1, rr, width), lambda b, c, i: (b, jnp.minimum(i + shift, nblk - 1), col0 + c))

    in_specs = [at(kw, k_col, 0), at(kw, v_col, 0), pl.BlockSpec((1, rr, LANES), lambda b, c, i: (b, i, 0)),
                pl.BlockSpec((1, rr, LANES), lambda b, c, i: (b, i, 0))]
    args = [k_arr, v_arr, cos, sin]
    for shift in (0, 1) if has_next else (0,):
        in_specs += [at(qw, q_col, shift), at(qw, 0, shift), at(qw, 0, shift), at(qw, 0, shift)]
        args += [q_arr, do, lse, dd]
    if has_token:
        in_specs.append(pl.BlockSpec(token.shape, lambda b, c, i: (0, 0)))
        args.append(token)
    return pl.pallas_call(
        body,
        name=name,
        grid=(bsz, nchunk, nblk),
        in_specs=in_specs,
        out_specs=[pl.BlockSpec((1, rr, qw), lambda b, c, i: (b, i, c)),
                   pl.BlockSpec((1, rr, kw), lambda b, c, i: (b, i, c)),
                   pl.BlockSpec((1, rr, kw), lambda b, c, i: (b, i, c))],
        out_shape=[jax.ShapeDtypeStruct((bsz, seq, nchunk * qw), BF16),
                   jax.ShapeDtypeStruct((bsz, seq, nchunk * kw), BF16),
                   jax.ShapeDtypeStruct((bsz, seq, nchunk * kw), BF16)],
        scratch_shapes=[pltpu.VMEM((rr, qw) if has_next else (8, LANES), F32),
                        pltpu.VMEM((2, 2 if gqa else 2 * npair, npair * QBLOCK if gqa else rr, LANES) if has_next
                                   else (1, 1, 8, LANES), F32)] +
                       ([pltpu.VMEM((rr, qw), F32), pltpu.VMEM((rr, kw), F32), pltpu.VMEM((rr, kw), F32)] if staged else []),
        compiler_params=_params("parallel", "parallel", "arbitrary"),
    )(*args)


B_CHUNKS = {1: (4, 1), 4: (1, 4), 16: (1, 4)}


def _rope_tables(positions):
    half = HEAD_DIM // 2
    inv = ROPE_THETA ** (-jnp.arange(half, dtype=F32) / half)
    ang = positions.astype(F32)[..., None] * inv
    cos, sin = jnp.cos(ang), jnp.sin(ang)
    return jnp.concatenate([cos] * 4, axis=-1), jnp.concatenate([-sin, sin, -sin, sin], axis=-1)


def _layer_step(x, mod, tables, sinks, ln1_g, ln1_b, ln2_g, ln2_b, target, get_w_in, get_rest, hook):
    bsz, seq, d = x.shape
    ntok = bsz * seq
    flat = lambda v: v.reshape(ntok, v.shape[-1])
    unflat = lambda v: v.reshape(bsz, seq, v.shape[-1])
    cos, sin = tables
    mm = functools.partial(_matmul, tm=1024, tk=1024)
    scalar = lambda tok: 0.0 if tok is None else tok[0, 0]

    u1 = _modulate_in(x, mod)
    u1f = flat(u1)
    wint = get_w_in(u1)
    cosf, sinf = flat(cos), flat(sin)
    proj = functools.partial(_proj_rope, u1f, wint, cosf, sinf, tm=2048)
    qkvb = unflat(proj(n=4608, b_off=OFF_QKVB, rope_cols=3072, tn=256, name="proj_qkvb"))
    b_kws, os_, ls_ = [], [], []
    for g, (window, r) in enumerate(B_PATTERNS):
        npair, nch = B_CHUNKS[r]
        per = B_HEADS_PER_GROUP // (2 * npair)
        nsec = len(B_PATTERNS) * per
        kw_ = dict(npair=npair, gqa=False, q_col=g * per, k_col=nsec + g * per, v_col=2 * nsec + g * per, nchunk=nch, r=r,
                   n_back=window // r)
        b_kws.append(kw_)
        o_g, l_g = _attn_fwd(qkvb, qkvb, qkvb, name=f"attn_b{g}_fwd", **kw_)
        os_.append(o_g)
        ls_.append(l_g)
    tok = hook("projected", os_[-1])
    proj = functools.partial(_proj_rope, u1f, wint, cosf + scalar(tok), sinf, tm=2048)
    gab = unflat(proj(n=2048, b_off=OFF_GAB, rope_cols=0, tn=256, name="proj_gab", out_dtype=BF16))
    qa = kva = unflat(proj(n=OFF_QKVB, b_off=OFF_QA, rope_cols=OFF_KVA + LANES, tn=256, name="proj_qkva", out_dtype=BF16))
    a_kw = dict(npair=A_Q_HEADS // 2, gqa=True, q_col=0, k_col=OFF_KVA // LANES, v_col=OFF_KVA // LANES + 1, nchunk=1, r=1,
                n_back=A_WINDOW - 1)
    after_gab = jnp.minimum(jnp.abs(gab[0, 0, 0].astype(F32)), 0.0)
    oa, lse_a = _attn_fwd(qa, kva, kva, name="attn_a_fwd", sinks=sinks.reshape(A_Q_HEADS) + after_gab, **a_kw)
    rest = get_rest(oa)
    wba, wbbt, wo, wgut, wd = (rest[n] for n in ("w_branch_a", "w_branch_b", "w_o", "w_gate_up", "w_down"))
    ya = unflat(mm(flat(oa), wba, mode="nn", out_dtype=BF16, tn=512, name="branch_a"))
    ybf, mergedf, obf = _merge_branch_b_gate([flat(t) for t in os_], [flat(t) for t in ls_], wbbt, flat(gab), flat(ya))
    xf = flat(x)
    y1f, h1f, u2f = _wo_ln1(mergedf, wo, xf, mod, ln1_g, ln1_b, seq)
    wgut_i = _interleave_gate_up(wgut)
    hf, af = _gate_up_silu(u2f, wgut_i)

    dy2f, dh1af, acc2 = _down_ln2_loss_bwd(af, wd, h1f, mod, ln2_g, ln2_b, flat(target), seq)
    g_wd = _matmul(af, dy2f, mode="tn", out_dtype=BF16, tm=256, tn=1024, tk=ntok, name="down_wgrad")
    dhf = _down_dgrad_silu_bwd(dy2f, wd, hf)
    g_wgut = _interleave_gate_up(_matmul(dhf, u2f, mode="tn", out_dtype=BF16, tm=256, tn=1024, tk=ntok, name="gate_up_wgrad"))
    dy1f, dxaf, acc1 = _gate_up_dgrad_ln1_bwd(dhf, wgut_i, dh1af, xf, y1f, mod, ln1_g, ln1_b, seq)
    g_wo = _matmul(mergedf, dy1f, mode="tn", out_dtype=BF16, tm=256, tn=1024, tk=ntok, name="w_o_wgrad")
    dyaf, dybf, dgaf, dgbf = _wo_dgrad_gate_bwd(dy1f, wo, flat(gab), flat(ya), ybf)
    g_wba = _matmul(flat(oa), dyaf, mode="tn", out_dtype=BF16, tm=256, tn=1024, tk=ntok, name="branch_a_wgrad")
    g_wbbt = _matmul(dybf, obf, mode="tn", out_dtype=BF16, tm=256, tn=512, tk=ntok, name="branch_b_wgrad")
    tok = hook("grads_rest", dict(w_branch_a=g_wba, w_branch_b=g_wbbt, w_o=g_wo, w_gate_up=g_wgut, w_down=g_wd))

    sinks_exp = jnp.repeat(sinks.reshape(1, A_Q_HEADS), HEAD_DIM, axis=1) + scalar(tok)
    doa, dd_a, acc_s = _branch_a_dgrad_delta(dyaf, wba, flat(oa), flat(lse_a), sinks_exp, seq)
    doa, dd_a = unflat(doa), unflat(dd_a)
    tok = hook("delta_done", dd_a)
    dqa, dka, dva = _attn_bwd(qa, kva, kva, cos, sin, doa, lse_a, dd_a, name="attn_a_bwd", token=tok, **a_kw)
    merged_bwd = [unflat(t) for t in _branch_b_dgrad_merge_bwd(dybf, wbbt, [flat(t) for t in os_], [flat(t) for t in ls_])]
    dqs, dks, dvs = [], [], []
    for g in range(len(B_PATTERNS)):
        dq_g, dk_g, dv_g = _attn_bwd(qkvb, qkvb, qkvb, cos, sin, merged_bwd[g], ls_[g], merged_bwd[3 + g],
                                     name=f"attn_b{g}_bwd", **b_kws[g])
        dqs.append(dq_g)
        dks.append(dk_g)
        dvs.append(dv_g)
    dproj = jnp.concatenate([t.astype(BF16) for t in [dqa, dka, dva] + dqs + dks + dvs] + [unflat(dgaf), unflat(dgbf)], axis=-1)
    dprojf = flat(dproj)
    g_wint = _matmul(dprojf, u1f, mode="tn", out_dtype=BF16, tm=256, tn=1024, tk=ntok, name="w_in_wgrad")
    tok = hook("grads_w_in", dict(w_in=g_wint))
    grad_x, acc0 = _w_in_dgrad_grad_x(dprojf, wint, dxaf, xf, mod, seq, tok)
    grad_x = unflat(grad_x)
    tok = hook("dgrad_done", grad_x)

    loss_part = jnp.sum(acc2[:, 3, 0])
    dmod = jnp.stack([acc0[:, 1], acc0[:, 0], acc1[:, 2], acc1[:, 4], acc1[:, 3], acc2[:, 2]], axis=1)
    small = jnp.stack([acc1[:, 0].sum(0), acc1[:, 1].sum(0), acc2[:, 0].sum(0), acc2[:, 1].sum(0), acc_s[:, 0].sum(0)])
    small = small + scalar(tok)
    return loss_part, grad_x, dmod, small


CHIP_FLIPS = (2, 4, 6)


def _my_place():
    return lax.axis_index("x"), lax.axis_index("y"), lax.axis_index("c")


def _flip(place, k):
    px, py, pc = place
    return (1 - px if k & 4 else px, 1 - py if k & 2 else py, 1 - pc if k & 1 else pc)


def _index(place):
    return 4 * place[0] + 2 * place[1] + place[2]


def _gather_small(v, name):
    rows, cols = v.shape

    def body(v_ref, out_ref, send_sems, recv_sems):
        me = _my_place()
        out_ref[_index(me)] = v_ref[...]
        copies = []
        for k in range(1, N_DEV):
            copies.append(pltpu.make_async_remote_copy(
                src_ref=v_ref, dst_ref=out_ref.at[_index(me)], send_sem=send_sems.at[k - 1], recv_sem=recv_sems.at[k - 1],
                device_id=_flip(me, k), device_id_type=MESH))
        for cp in copies:
            cp.start()
        for k in range(1, N_DEV):
            pltpu.make_async_remote_copy(
                src_ref=v_ref, dst_ref=out_ref.at[_index(_flip(me, k))], send_sem=send_sems.at[k - 1],
                recv_sem=recv_sems.at[k - 1], device_id=_flip(me, k), device_id_type=MESH).wait_recv()
        for cp in copies:
            cp.wait_send()

    return pl.pallas_call(
        body,
        name=name,
        out_shape=jax.ShapeDtypeStruct((N_DEV, rows, cols), v.dtype),
        in_specs=[pl.BlockSpec(memory_space=pltpu.VMEM)],
        out_specs=pl.BlockSpec(memory_space=pltpu.VMEM),
        scratch_shapes=[pltpu.SemaphoreType.DMA((N_DEV - 1,)), pltpu.SemaphoreType.DMA((N_DEV - 1,))],
        compiler_params=pltpu.CompilerParams(vmem_limit_bytes=VMEM_LIMIT_BYTES),
    )(v)


_HBM = pl.BlockSpec(memory_space=pltpu.HBM)
_SEM = pl.BlockSpec(memory_space=pltpu.SEMAPHORE)
_EFFECT = pltpu.SideEffectType.DATAFLOW_SIDE_EFFECTING


def _remote(src, dst, send_sems, recv_sems, j, to):
    return pltpu.make_async_remote_copy(src_ref=src, dst_ref=dst, send_sem=send_sems.at[j], recv_sem=recv_sems.at[j],
                                        device_id=to, device_id_type=MESH)


def _copies_start(name, bufs, make_copies, nsem):
    nbuf = len(bufs)

    def body(*refs):
        for cp in make_copies(refs[:nbuf], refs[nbuf], refs[nbuf + 1]):
            cp.start()
        refs[-1][...] = jnp.zeros_like(refs[-1])

    sems = pltpu.SemaphoreType.DMA((nsem,))
    res = pl.pallas_call(
        body, name=name,
        out_shape=(sems, sems, *[pltpu.HBM(v.shape, v.dtype) for v in bufs], jax.ShapeDtypeStruct((8, LANES), F32)),
        in_specs=(_HBM,) * nbuf, out_specs=(_SEM, _SEM) + (_HBM,) * nbuf + (pl.BlockSpec(memory_space=pltpu.VMEM),),
        input_output_aliases={i: 2 + i for i in range(nbuf)},
        compiler_params=pltpu.CompilerParams(has_side_effects=_EFFECT),
    )(*[pltpu.with_memory_space_constraint(v, pltpu.HBM) for v in bufs])
    return res[0], res[1], list(res[2:2 + nbuf]), res[-1]


def _copies_wait(name, started, make_copies, after):
    send_sems, recv_sems, bufs, _ = started
    nbuf = len(bufs)

    def body(*refs):
        for cp in make_copies(refs[:nbuf], refs[nbuf], refs[nbuf + 1]):
            cp.wait_send()
            cp.wait_recv()

    return list(pl.pallas_call(
        body, name=name,
        out_shape=tuple(pltpu.HBM(v.shape, v.dtype) for v in bufs),
        in_specs=(_HBM,) * nbuf + (_SEM, _SEM, pl.BlockSpec(memory_space=pl.ANY)), out_specs=(_HBM,) * nbuf,
        input_output_aliases={i: i for i in range(nbuf)},
        compiler_params=pltpu.CompilerParams(has_side_effects=_EFFECT),
    )(*bufs, send_sems, recv_sems, after))


def _to_sibling_copies(refs, send_sems, recv_sems):
    src_ref, land_ref = refs
    me = _my_place()
    return [_remote(src_ref.at[q, 1 - me[2]], land_ref.at[q], send_sems, recv_sems, q, _flip(me, 1)) for q in range(4)]


def _to_chips_copies(refs, send_sems, recv_sems):
    src_ref, land_ref = refs
    me = _my_place()
    copies = []
    for j, k in enumerate(CHIP_FLIPS):
        to = _flip(me, k)
        copies.append(_remote(src_ref.at[2 * to[0] + to[1]], land_ref.at[j], send_sems, recv_sems, j, to))
    return copies


class _Gather:
    def __init__(self, name, blocks):
        self.name, self.n = name, len(blocks)
        at_me = (_index(_my_place()), 0, 0)
        lands = [lax.dynamic_update_slice(lax.empty((N_DEV,) + v.shape, v.dtype), v[None], at_me) for v in blocks]
        self.first = _copies_start(name + "_start", list(blocks) + lands, self._first_copies, 4 * self.n)
        self.token = self.first[3]

    def _first_copies(self, refs, send_sems, recv_sems):
        me = _my_place()
        return [_remote(refs[w], refs[self.n + w].at[_index(me)], send_sems, recv_sems, 4 * w + j, _flip(me, k))
                for w in range(self.n) for j, k in enumerate((1,) + CHIP_FLIPS)]

    def _pass_copies(self, refs, send_sems, recv_sems):
        me = _my_place()
        copies = []
        for w, land in enumerate(refs):
            for j, k in enumerate(CHIP_FLIPS):
                slot = land.at[_index(_flip(me, k))]
                copies.append(_remote(slot, slot, send_sems, recv_sems, 3 * w + j, _flip(me, 1)))
        return copies

    def pass_on(self, after):
        lands = _copies_wait(self.name + "_wait", self.first, self._first_copies, after)[self.n:]
        self.second = _copies_start(self.name + "_pass_start", lands, self._pass_copies, 3 * self.n)
        return self.second[3]

    def finish(self, after):
        return _copies_wait(self.name + "_pass_wait", self.second, self._pass_copies, after)


def _to_all_copies(refs, send_sems, recv_sems):
    src_ref, land_ref = refs
    me = _my_place()
    return [_remote(src_ref, land_ref.at[_index(me)], send_sems, recv_sems, k - 1, _flip(me, k)) for k in range(1, N_DEV)]


SUM_SPLIT = 2


def _sum_pairs(parts, theirs):
    nchip, _, rows, cols = parts.shape
    tile = rows // SUM_SPLIT

    def body(c_ref, a_ref, b_ref, o_ref):
        o_ref[...] = (a_ref[0].astype(F32) + b_ref[...].astype(F32)).astype(BF16)

    spec = pl.BlockSpec((1, tile, cols), lambda q, t, c_ref: (q, t, 0))
    grid_spec = pltpu.PrefetchScalarGridSpec(
        num_scalar_prefetch=1, grid=(nchip, SUM_SPLIT),
        in_specs=[pl.BlockSpec((1, 1, tile, cols), lambda q, t, c_ref: (q, c_ref[0], t, 0)), spec], out_specs=spec)
    return pl.pallas_call(body, name="grad_sum_sibling", grid_spec=grid_spec,
                          out_shape=jax.ShapeDtypeStruct((nchip, rows, cols), BF16),
                          compiler_params=_params("parallel", "parallel"))(lax.axis_index("c").reshape(1), parts, theirs)


def _sum_final(chip_sum, got):
    _, rows, cols = chip_sum.shape
    tile = rows // SUM_SPLIT

    def body(q_ref, a_ref, g_ref, o_ref):
        o_ref[...] = ((a_ref[0].astype(F32) + g_ref[0].astype(F32)) + g_ref[1].astype(F32)) + g_ref[2].astype(F32)

    grid_spec = pltpu.PrefetchScalarGridSpec(
        num_scalar_prefetch=1, grid=(SUM_SPLIT,),
        in_specs=[pl.BlockSpec((1, tile, cols), lambda t, q_ref: (q_ref[0], t, 0)),
                  pl.BlockSpec((3, tile, cols), lambda t, q_ref: (0, t, 0))],
        out_specs=pl.BlockSpec((tile, cols), lambda t, q_ref: (t, 0)))
    my_chip = (2 * lax.axis_index("x") + lax.axis_index("y")).reshape(1)
    return pl.pallas_call(body, name="grad_sum_chips", grid_spec=grid_spec, out_shape=jax.ShapeDtypeStruct((rows, cols), F32),
                          compiler_params=_params("parallel"))(my_chip, chip_sum, got)


class _ReduceScatter:
    def __init__(self, name, slabs):
        self.name, self.rows = name, slabs.shape[1]
        parts = slabs.reshape(4, 2, self.rows, D_MODEL)
        self.first = _copies_start(name + "_sibling_start", [parts, lax.empty((4, self.rows, D_MODEL), slabs.dtype)],
                                   _to_sibling_copies, 4)
        self.token = self.first[3]

    def between_chips(self, after):
        parts, theirs = _copies_wait(self.name + "_sibling_wait", self.first, _to_sibling_copies, after)
        chip_sum = _sum_pairs(parts, theirs)
        self.second = _copies_start(self.name + "_chips_start", [chip_sum, lax.empty((3, self.rows, D_MODEL), chip_sum.dtype)],
                                    _to_chips_copies, 3)
        return self.second[3]

    def finish(self, after):
        chip_sum, got = _copies_wait(self.name + "_chips_wait", self.second, _to_chips_copies, after)
        return _sum_final(chip_sum, got)


def _ada_fwd(c_all, w, b):
    nb, _ = c_all.shape
    ncol = w.shape[1]

    def body(c_ref, w_ref, b_ref, o_ref):
        c = c_ref[...]
        act = (c * _sigmoid(c)).astype(BF16)
        o_ref[...] = jnp.dot(act, w_ref[...].astype(BF16), preferred_element_type=F32) + b_ref[...]

    return pl.pallas_call(body, name="ada_fwd", out_shape=jax.ShapeDtypeStruct((nb, ncol), F32),
                          compiler_params=pltpu.CompilerParams(vmem_limit_bytes=VMEM_LIMIT_BYTES))(c_all, w, b)


def _ada_wgrad(c_all_t, dmod_cols):
    d, nb = c_all_t.shape
    ncol = dmod_cols.shape[1]

    def body(ct_ref, dm_ref, o_ref):
        ct = ct_ref[...]
        act = (ct * _sigmoid(ct)).astype(BF16).astype(F32)
        dm = dm_ref[...].astype(BF16).astype(F32)
        acc = act[:, 0:1] * dm[0:1, :]
        for i in range(1, nb):
            acc = acc + act[:, i:i + 1] * dm[i:i + 1, :]
        o_ref[...] = acc

    return pl.pallas_call(body, name="ada_wgrad", out_shape=jax.ShapeDtypeStruct((d, ncol), F32),
                          compiler_params=pltpu.CompilerParams(vmem_limit_bytes=VMEM_LIMIT_BYTES))(c_all_t, dmod_cols)


SMALL_ROWS = 24


def _reduce_small(gathered):
    def body(g_ref, o_ref):
        acc = g_ref[0]
        for dev in range(1, N_DEV):
            acc = acc + g_ref[dev]
        o_ref[...] = acc

    return pl.pallas_call(body, name="reduce_small", out_shape=jax.ShapeDtypeStruct(gathered.shape[1:], F32))(gathered)


def _adamw_math(w, g, m, v):
    nm = ADAM_B1 * m + (1.0 - ADAM_B1) * g
    nv = ADAM_B2 * v + (1.0 - ADAM_B2) * (g * g)
    bc1 = 1.0 - ADAM_B1 ** ADAM_STEP
    bc2 = 1.0 - ADAM_B2 ** ADAM_STEP
    return -ADAM_LR * ((nm / bc1) / (jnp.sqrt(nv / bc2) + ADAM_EPS) + ADAM_WD * w), nm, nv


def _adamw_small(ws, gs, ms, vs, name):
    n = len(ws)

    def body(*refs):
        for i in range(n):
            res = _adamw_math(*(refs[k * n + i][...] for k in range(4)))
            for k in range(3):
                refs[(4 + k) * n + i][...] = res[k]

    shapes = [jax.ShapeDtypeStruct(w.shape, F32) for w in ws]
    res = pl.pallas_call(body, name=name, out_shape=shapes * 3)(*ws, *gs, *ms, *vs)
    return [(res[i], res[n + i], res[2 * n + i]) for i in range(n)]


def _adamw(w, g, m, v, name, token=None):
    rows, cols = w.shape
    tile = rows
    for cand in range(min(rows // 2, 512) // 8 * 8, 7, -8):
        if rows % cand == 0:
            tile = cand
            break
    spec = pl.BlockSpec((tile, cols), lambda t: (t, 0))

    def body(w_ref, g_ref, m_ref, v_ref, *refs):
        d_ref, nm_ref, nv_ref = refs[-3:]
        d_ref[...], nm_ref[...], nv_ref[...] = _adamw_math(w_ref[...], g_ref[...], m_ref[...], v_ref[...])

    shp = jax.ShapeDtypeStruct((rows, cols), F32)
    follows = [] if token is None else [token]
    return pl.pallas_call(body, name=name, grid=(rows // tile,),
                          in_specs=[spec] * 4 + [pl.BlockSpec(t.shape, lambda t_: (0, 0)) for t in follows],
                          out_specs=[spec] * 3, out_shape=[shp] * 3, compiler_params=_params("parallel"))(w, g, m, v, *follows)


_WEIGHTS = ("w_ada", "b_ada", "w_in", "sinks", "w_branch_a", "w_branch_b", "w_o", "ln1_g", "ln1_b", "w_gate_up", "w_down",
            "ln2_g", "ln2_b")
_TRANSPOSED = ("w_in", "w_branch_b", "w_gate_up")


def _pack_shard(name, w):
    w = w.astype(BF16)
    if name in _TRANSPOSED:
        w = w.T
    return w.reshape(-1, D_MODEL)


def _unpack_full(name, slab):
    if name == "w_branch_b":
        return slab.reshape(N_DEV * 128, 512)
    return slab.reshape(-1, D_MODEL)


def _unpack_group(group, gathered):
    return {n: _unpack_full(n, slab) for (n, _), slab in zip(group, gathered)}


def _unpack_grads(group, g_packed):
    g_w, off = {}, 0
    for n, r in group:
        part = g_packed[off:off + r]
        off += r
        g_w[n] = part.reshape(128, 512) if n == "w_branch_b" else part
    return g_w


def kernel(x, c, positions, w_ada, b_ada, w_in, sinks, w_branch_a, w_branch_b, w_o, ln1_g, ln1_b, w_gate_up, w_down, ln2_g, ln2_b, loss_target, m_w_ada, m_b_ada, m_w_in, m_sinks, m_w_branch_a, m_w_branch_b, m_w_o, m_ln1_g, m_ln1_b, m_w_gate_up, m_w_down, m_ln2_g, m_ln2_b, v_w_ada, v_b_ada, v_w_in, v_sinks, v_w_branch_a, v_w_branch_b, v_w_o, v_ln1_g, v_ln1_b, v_w_gate_up, v_w_down, v_ln2_g, v_ln2_b):
    weights = dict(w_ada=w_ada, b_ada=b_ada, w_in=w_in, sinks=sinks, w_branch_a=w_branch_a, w_branch_b=w_branch_b, w_o=w_o,
                   ln1_g=ln1_g, ln1_b=ln1_b, w_gate_up=w_gate_up, w_down=w_down, ln2_g=ln2_g, ln2_b=ln2_b)
    m_in = dict(w_ada=m_w_ada, b_ada=m_b_ada, w_in=m_w_in, sinks=m_sinks, w_branch_a=m_w_branch_a, w_branch_b=m_w_branch_b,
                w_o=m_w_o, ln1_g=m_ln1_g, ln1_b=m_ln1_b, w_gate_up=m_w_gate_up, w_down=m_w_down, ln2_g=m_ln2_g, ln2_b=m_ln2_b)
    v_in = dict(w_ada=v_w_ada, b_ada=v_b_ada, w_in=v_w_in, sinks=v_sinks, w_branch_a=v_w_branch_a, w_branch_b=v_w_branch_b,
                w_o=v_w_o, ln1_g=v_ln1_g, ln1_b=v_ln1_b, w_gate_up=v_w_gate_up, w_down=v_w_down, ln2_g=v_ln2_g, ln2_b=v_ln2_b)
    bsz = x.shape[0]
    me = _index(_my_place())
    ada_cols = w_ada.shape[2]
    outs = {}

    def adamw(n, g, token=None):
        w2, m2, v2 = (t[n][0] if t[n].ndim == 3 else t[n] for t in (weights, m_in, v_in))
        shape = weights[n].shape
        if n in _TRANSPOSED:
            dlt, nm, nv = _adamw(w2.T, g, m2.T, v2.T, "adamw_" + n, token)
            outs[n] = tuple(t.T.reshape(shape) for t in (g, dlt, nm, nv))
        else:
            dlt, nm, nv = _adamw(w2, g, m2, v2, "adamw_" + n, token)
            outs[n] = tuple(t.reshape(shape) for t in (g, dlt, nm, nv))
        return nv

    packed_in = [_pack_shard(n, weights[n][0]) for n, _ in GROUP_IN]
    packed_rest = [_pack_shard(n, weights[n][0]) for n, _ in GROUP_REST]
    c_all = _gather_small(jnp.pad(c, ((0, 8 - bsz), (0, 0))), "gather_c")[:, :bsz].reshape(N_DEV * bsz, D_MODEL)
    gather_in = _Gather("gather_w_in", lax.optimization_barrier((packed_in, c_all))[0])
    b_cols = lax.dynamic_slice_in_dim(b_ada, me * ada_cols, ada_cols, axis=1)
    mod_cols = _ada_fwd(c_all, w_ada[0], b_cols + gather_in.token[0, 0])
    tables = _rope_tables(positions)
    mod_cols, tables, packed_rest = lax.optimization_barrier((mod_cols, tables, packed_rest))
    mod_all = _gather_small(mod_cols, "gather_mod").transpose(1, 0, 2).reshape(N_DEV * bsz, 6, D_MODEL)
    gather_rest = _Gather("gather_rest", lax.optimization_barrier((packed_rest, mod_all))[0])
    mod = jnp.pad(lax.dynamic_slice_in_dim(mod_all, me * bsz, bsz, axis=0), ((0, 0), (0, 2), (0, 0)))
    mod = mod + gather_rest.token[0, 0]
    mod = mod + gather_in.pass_on(mod)[0, 0]

    scatters, rest_grads = {}, {}

    def get_w_in(after):
        return _unpack_group(GROUP_IN, gather_in.finish(after))["w_in"]

    def get_rest(after):
        return _unpack_group(GROUP_REST, gather_rest.finish(after))

    def pack_grads(group, grads):
        return jnp.concatenate([grads[n].reshape(N_DEV, r, D_MODEL) for n, r in group], axis=1)

    def hook(point, value):
        if point == "projected":
            return gather_rest.pass_on(value)
        if point == "grads_rest":
            scatters["rest"] = _ReduceScatter("scatter_rest", pack_grads(GROUP_REST, value))
            return scatters["rest"].token
        if point == "delta_done":
            return scatters["rest"].between_chips(value)
        if point == "grads_w_in":
            scatters["in"] = _ReduceScatter("scatter_w_in", pack_grads(GROUP_IN, value))
            rest_grads.update(_unpack_grads(GROUP_REST, scatters["rest"].finish(scatters["in"].token)))
            return scatters["in"].between_chips(lax.optimization_barrier(tuple(rest_grads.values()))[0])
        if point == "dgrad_done":
            return None
        raise ValueError(point)

    loss_part, grad_x, dmod, small = _layer_step(x, mod, tables, sinks[0], ln1_g, ln1_b, ln2_g, ln2_b, loss_target,
                                                 get_w_in, get_rest, hook)

    rows = jnp.concatenate([dmod.reshape(bsz * 6, D_MODEL), small, jnp.full((1, D_MODEL), loss_part, F32),
                            jnp.zeros((SMALL_ROWS - bsz * 6 - 6, D_MODEL), F32)], axis=0)
    land = lax.dynamic_update_slice(lax.empty((N_DEV,) + rows.shape, rows.dtype), rows[None], (me, 0, 0))
    gather_small = _copies_start("gather_small_start", [rows, land], _to_all_copies, N_DEV - 1)
    follow = gather_small[3]
    for n, g in rest_grads.items():
        follow = adamw(n, g, follow)[:8, :LANES]
    small_all = _copies_wait("gather_small_wait", gather_small, _to_all_copies, follow)[1]
    sums = _reduce_small(small_all)
    loss = sums[bsz * 6 + 5, 0]
    dmod_all = small_all[:, :bsz * 6].reshape(N_DEV * bsz, 6 * D_MODEL)
    small_g = {"b_ada": functools.reduce(jnp.add, [sums[6 * i:6 * i + 6] for i in range(bsz)]).reshape(1, 6 * D_MODEL),
               "sinks": sums[bsz * 6 + 4][::HEAD_DIM][None]}
    small_g.update({n: sums[bsz * 6 + i][None] for i, n in enumerate(("ln1_g", "ln1_b", "ln2_g", "ln2_b"))})
    names = list(small_g)
    for n, (dlt, nm, nv) in zip(names, _adamw_small([weights[n] for n in names], [small_g[n] for n in names],
                                                     [m_in[n] for n in names], [v_in[n] for n in names], "adamw_small")):
        outs[n] = (small_g[n], dlt, nm, nv)
    dmod_cols = lax.dynamic_slice_in_dim(dmod_all, me * ada_cols, ada_cols, axis=1)
    adamw("w_ada", _ada_wgrad(c_all.T, dmod_cols))
    done = lax.optimization_barrier(tuple(outs[n][3] for n in outs))
    for n, g in _unpack_grads(GROUP_IN, scatters["in"].finish(done[0])).items():
        adamw(n, g)

    return (loss, grad_x, *[outs[n][0] for n in _WEIGHTS], *[outs[n][1] for n in _WEIGHTS], *[outs[n][2] for n in _WEIGHTS],
            *[outs[n][3] for n in _WEIGHTS])
```
